```python
import jax, jax.numpy as jnp
from jax import lax
import numpy as np

D_MODEL = 1024
BATCH = 8
SEQ = 4096
DEPTH = 2

N_META = 16
RET_HEADS = 4
RET_HEAD_DIM = 128
RET_WIDTH = RET_HEADS * RET_HEAD_DIM
POOL_WINDOWS = (2, 4, 8, 16)
POOL_GROUPS = len(POOL_WINDOWS)
POOL_GROUP_DIM = 128
POOL_WIDTH = POOL_GROUPS * POOL_GROUP_DIM
CHUNK = 128
D_FF = 2816
ROPE_BASE = 10000.0
EPS = 1e-6
N_IN = 4 * RET_WIDTH + POOL_WIDTH + 2 * D_MODEL

kernel_name = 'hybrid_retention_pool_macaron'


def _rmsnorm(x, g):
    xf = x.astype(jnp.float32)
    y = xf * lax.rsqrt(jnp.mean(xf * xf, axis=-1, keepdims=True) + EPS)
    return (y * g.astype(jnp.float32)).astype(x.dtype)


def _swiglu(h, w_gate, w_up, w_down):
    return (jax.nn.silu(h @ w_gate) * (h @ w_up)) @ w_down


def _rotary(t, pos):
    half = t.shape[-1] // 2
    inv_freq = ROPE_BASE ** (-jnp.arange(half, dtype=jnp.float32) / half)
    ang = pos[:, None] * inv_freq[None, :]
    cos = jnp.cos(ang).astype(t.dtype)
    sin = jnp.sin(ang).astype(t.dtype)
    t1, t2 = t[..., :half], t[..., half:]
    return jnp.concatenate([t1 * cos - t2 * sin, t1 * sin + t2 * cos], axis=-1)


def _retention(q, k, v):
    b, L, _ = q.shape
    pad = (-L) % CHUNK
    Lp = L + pad
    n = Lp // CHUNK

    def heads(t):
        t = t.reshape(b, L, RET_HEADS, RET_HEAD_DIM).transpose(0, 2, 1, 3)
        return jnp.pad(t, ((0, 0), (0, 0), (pad, 0), (0, 0)))

    q, k, v = heads(q), heads(k), heads(v)
    pos = jnp.arange(Lp, dtype=jnp.float32) - pad
    q = _rotary(q, pos) * (RET_HEAD_DIM ** -0.5)
    k = _rotary(k, pos)
    shp = (b, RET_HEADS, n, CHUNK, RET_HEAD_DIM)
    q, k, v = q.reshape(shp), k.reshape(shp), v.reshape(shp)

    dt = q.dtype
    log_gamma = jnp.log1p(-(2.0 ** (-5.0 - jnp.arange(RET_HEADS, dtype=jnp.float32))))
    idx = jnp.arange(CHUNK, dtype=jnp.float32)
    diff = idx[:, None] - idx[None, :]
    intra = jnp.where(diff[None] >= 0, jnp.exp(diff[None] * log_gamma[:, None, None]), 0.0)
    k_decay = jnp.exp((CHUNK - 1.0 - idx)[None, :] * log_gamma[:, None])
    q_decay = jnp.exp((idx + 1.0)[None, :] * log_gamma[:, None])
    chunk_decay = jnp.exp(CHUNK * log_gamma)

    scores = jnp.einsum('bhncd,bhnmd->bhncm', q, k) * intra[None, :, None].astype(dt)
    inner = jnp.einsum('bhncm,bhnmd->bhncd', scores, v)
    kv = jnp.einsum('bhncd,bhnce->nbhde', k * k_decay[None, :, None, :, None].astype(dt), v)
    cd = chunk_decay[None, :, None, None].astype(dt)

    def step(state, kv_n):
        return state * cd + kv_n, state

    s0 = jnp.zeros((b, RET_HEADS, RET_HEAD_DIM, RET_HEAD_DIM), dt)
    _, s_prev = lax.scan(step, s0, kv)
    cross = jnp.einsum('bhncd,nbhde->bhnce', q * q_decay[None, :, None, :, None].astype(dt), s_prev)
    out = (inner + cross).reshape(b, RET_HEADS, Lp, RET_HEAD_DIM)[:, :, pad:]

    of = out.astype(jnp.float32)
    mu = jnp.mean(of, axis=-1, keepdims=True)
    var = jnp.mean(jnp.square(of - mu), axis=-1, keepdims=True)
    of = (of - mu) * lax.rsqrt(var + EPS)
    return of.astype(dt).transpose(0, 2, 1, 3).reshape(b, L, RET_WIDTH)


def _pool_mixer(u, maps, scale):
    b, L, _ = u.shape
    uf = u.astype(jnp.float32)
    t = jnp.arange(L, dtype=jnp.float32)[None, :, None]
    outs = []
    for g, w in enumerate(POOL_WINDOWS):
        ug = uf[..., g * POOL_GROUP_DIM:(g + 1) * POOL_GROUP_DIM]
        cs = jnp.cumsum(ug, axis=1)
        lag = jnp.pad(cs, ((0, 0), (w, 0), (0, 0)))[:, :L]
        mean = (cs - lag) / jnp.minimum(t + 1.0, float(w))
        pooled = (mean - ug).astype(u.dtype)
        outs.append(pooled @ maps[g])
    return jnp.concatenate(outs, axis=-1) * scale


def _fwd_setup_inputs(seed: int = 0) -> dict:
    key = jax.random.key(seed)
    ks = jax.random.split(key, 20)
    f32 = jnp.float32

    def w(k, shape, fan_in):
        return jax.random.normal(k, shape, f32) * (fan_in ** -0.5)

    def gain(k, shape):
        return 1.0 + 0.02 * jax.random.normal(k, shape, f32)

    return {
        'x': jax.random.normal(ks[0], (BATCH, SEQ, D_MODEL), f32),
        'meta': jax.random.normal(ks[1], (N_META, D_MODEL), f32),
        'ffn1_norm': gain(ks[2], (DEPTH, D_MODEL)),
        'ffn1_gate': w(ks[3], (DEPTH, D_MODEL, D_FF), D_MODEL),
        'ffn1_up': w(ks[4], (DEPTH, D_MODEL, D_FF), D_MODEL),
        'ffn1_down': w(ks[5], (DEPTH, D_FF, D_MODEL), D_FF),
        'mix_norm': gain(ks[6], (DEPTH, D_MODEL)),
        'w_in': w(ks[7], (DEPTH, D_MODEL, N_IN), D_MODEL),
        'pool_maps': w(ks[8], (DEPTH, POOL_GROUPS, POOL_GROUP_DIM, POOL_GROUP_DIM), POOL_GROUP_DIM),
        'pool_scale': gain(ks[9], (DEPTH, POOL_WIDTH)),
        'w_ret_up': w(ks[10], (DEPTH, RET_WIDTH, D_MODEL), RET_WIDTH),
        'w_pool_up': w(ks[11], (DEPTH, POOL_WIDTH, D_MODEL), POOL_WIDTH),
        'w_out': w(ks[12], (DEPTH, D_MODEL, D_MODEL), D_MODEL),
        'ffn2_norm': gain(ks[13], (DEPTH, D_MODEL)),
        'ffn2_gate': w(ks[14], (DEPTH, D_MODEL, D_FF), D_MODEL),
        'ffn2_up': w(ks[15], (DEPTH, D_MODEL, D_FF), D_MODEL),
        'ffn2_down': w(ks[16], (DEPTH, D_FF, D_MODEL), D_FF),
        'final_norm': gain(ks[17], (D_MODEL,)),
    }


def _fwd_reference(x, meta, ffn1_norm, ffn1_gate, ffn1_up, ffn1_down, mix_norm, w_in,
              pool_maps, pool_scale, w_ret_up, w_pool_up, w_out,
              ffn2_norm, ffn2_gate, ffn2_up, ffn2_down, final_norm):
    b = x.shape[0]
    meta_b = jnp.broadcast_to(meta[None].astype(x.dtype), (b, N_META, D_MODEL))
    h = jnp.concatenate([meta_b, x], axis=1)
    splits = np.cumsum([RET_WIDTH, RET_WIDTH, RET_WIDTH, RET_WIDTH, POOL_WIDTH, D_MODEL])
    for i in range(DEPTH):
        h = h + 0.5 * _swiglu(_rmsnorm(h, ffn1_norm[i]), ffn1_gate[i], ffn1_up[i], ffn1_down[i])
        z = _rmsnorm(h, mix_norm[i]) @ w_in[i]
        q, k, v, g_ret, u_pool, gate_a, gate_b = jnp.split(z, splits, axis=-1)
        ret = (_retention(q, k, v) * jax.nn.silu(g_ret)) @ w_ret_up[i]
        pool = _pool_mixer(u_pool, pool_maps[i], pool_scale[i]) @ w_pool_up[i]
        mixed = jax.nn.sigmoid(gate_a) * ret + jax.nn.sigmoid(gate_b) * pool
        h = h + mixed @ w_out[i]
        h = h + 0.5 * _swiglu(_rmsnorm(h, ffn2_norm[i]), ffn2_gate[i], ffn2_up[i], ffn2_down[i])
    h = _rmsnorm(h, final_norm)
    return h[:, N_META:]


import jax as _jax
import jax.numpy as _jnp

TWIN_FORMAT = 'train_step'
FWD_PARAMS = ['x', 'meta', 'ffn1_norm', 'ffn1_gate', 'ffn1_up', 'ffn1_down', 'mix_norm', 'w_in', 'pool_maps', 'pool_scale', 'w_ret_up', 'w_pool_up', 'w_out', 'ffn2_norm', 'ffn2_gate', 'ffn2_up', 'ffn2_down', 'final_norm']
TWIN_WEIGHTS = ['meta', 'ffn1_norm', 'ffn1_gate', 'ffn1_up', 'ffn1_down', 'mix_norm', 'w_in', 'pool_maps', 'pool_scale', 'w_ret_up', 'w_pool_up', 'w_out', 'ffn2_norm', 'ffn2_gate', 'ffn2_up', 'ffn2_down', 'final_norm']
TWIN_DIFF_INPUT = 'x'
TWIN_INPUTS = ['x', 'meta', 'ffn1_norm', 'ffn1_gate', 'ffn1_up', 'ffn1_down', 'mix_norm', 'w_in', 'pool_maps', 'pool_scale', 'w_ret_up', 'w_pool_up', 'w_out', 'ffn2_norm', 'ffn2_gate', 'ffn2_up', 'ffn2_down', 'final_norm', 'loss_target', 'm_meta', 'm_ffn1_norm', 'm_ffn1_gate', 'm_ffn1_up', 'm_ffn1_down', 'm_mix_norm', 'm_w_in', 'm_pool_maps', 'm_pool_scale', 'm_w_ret_up', 'm_w_pool_up', 'm_w_out', 'm_ffn2_norm', 'm_ffn2_gate', 'm_ffn2_up', 'm_ffn2_down', 'm_final_norm', 'v_meta', 'v_ffn1_norm', 'v_ffn1_gate', 'v_ffn1_up', 'v_ffn1_down', 'v_mix_norm', 'v_w_in', 'v_pool_maps', 'v_pool_scale', 'v_w_ret_up', 'v_w_pool_up', 'v_w_out', 'v_ffn2_norm', 'v_ffn2_gate', 'v_ffn2_up', 'v_ffn2_down', 'v_final_norm']
TWIN_OUTPUTS = ['loss', 'grad_x', 'grad_meta', 'grad_ffn1_norm', 'grad_ffn1_gate', 'grad_ffn1_up', 'grad_ffn1_down', 'grad_mix_norm', 'grad_w_in', 'grad_pool_maps', 'grad_pool_scale', 'grad_w_ret_up', 'grad_w_pool_up', 'grad_w_out', 'grad_ffn2_norm', 'grad_ffn2_gate', 'grad_ffn2_up', 'grad_ffn2_down', 'grad_final_norm', 'delta_meta', 'delta_ffn1_norm', 'delta_ffn1_gate', 'delta_ffn1_up', 'delta_ffn1_down', 'delta_mix_norm', 'delta_w_in', 'delta_pool_maps', 'delta_pool_scale', 'delta_w_ret_up', 'delta_w_pool_up', 'delta_w_out', 'delta_ffn2_norm', 'delta_ffn2_gate', 'delta_ffn2_up', 'delta_ffn2_down', 'delta_final_norm', 'new_m_meta', 'new_m_ffn1_norm', 'new_m_ffn1_gate', 'new_m_ffn1_up', 'new_m_ffn1_down', 'new_m_mix_norm', 'new_m_w_in', 'new_m_pool_maps', 'new_m_pool_scale', 'new_m_w_ret_up', 'new_m_w_pool_up', 'new_m_w_out', 'new_m_ffn2_norm', 'new_m_ffn2_gate', 'new_m_ffn2_up', 'new_m_ffn2_down', 'new_m_final_norm', 'new_v_meta', 'new_v_ffn1_norm', 'new_v_ffn1_gate', 'new_v_ffn1_up', 'new_v_ffn1_down', 'new_v_mix_norm', 'new_v_w_in', 'new_v_pool_maps', 'new_v_pool_scale', 'new_v_w_ret_up', 'new_v_w_pool_up', 'new_v_w_out', 'new_v_ffn2_norm', 'new_v_ffn2_gate', 'new_v_ffn2_up', 'new_v_ffn2_down', 'new_v_final_norm']
TWIN_LEAF_KINDS = {'loss': 'loss', 'grad_x': 'grad_x', 'grad_meta': 'grad_w', 'grad_ffn1_norm': 'grad_w', 'grad_ffn1_gate': 'grad_w', 'grad_ffn1_up': 'grad_w', 'grad_ffn1_down': 'grad_w', 'grad_mix_norm': 'grad_w', 'grad_w_in': 'grad_w', 'grad_pool_maps': 'grad_w', 'grad_pool_scale': 'grad_w', 'grad_w_ret_up': 'grad_w', 'grad_w_pool_up': 'grad_w', 'grad_w_out': 'grad_w', 'grad_ffn2_norm': 'grad_w', 'grad_ffn2_gate': 'grad_w', 'grad_ffn2_up': 'grad_w', 'grad_ffn2_down': 'grad_w', 'grad_final_norm': 'grad_w', 'delta_meta': 'delta_w', 'delta_ffn1_norm': 'delta_w', 'delta_ffn1_gate': 'delta_w', 'delta_ffn1_up': 'delta_w', 'delta_ffn1_down': 'delta_w', 'delta_mix_norm': 'delta_w', 'delta_w_in': 'delta_w', 'delta_pool_maps': 'delta_w', 'delta_pool_scale': 'delta_w', 'delta_w_ret_up': 'delta_w', 'delta_w_pool_up': 'delta_w', 'delta_w_out': 'delta_w', 'delta_ffn2_norm': 'delta_w', 'delta_ffn2_gate': 'delta_w', 'delta_ffn2_up': 'delta_w', 'delta_ffn2_down': 'delta_w', 'delta_final_norm': 'delta_w', 'new_m_meta': 'new_m', 'new_m_ffn1_norm': 'new_m', 'new_m_ffn1_gate': 'new_m', 'new_m_ffn1_up': 'new_m', 'new_m_ffn1_down': 'new_m', 'new_m_mix_norm': 'new_m', 'new_m_w_in': 'new_m', 'new_m_pool_maps': 'new_m', 'new_m_pool_scale': 'new_m', 'new_m_w_ret_up': 'new_m', 'new_m_w_pool_up': 'new_m', 'new_m_w_out': 'new_m', 'new_m_ffn2_norm': 'new_m', 'new_m_ffn2_gate': 'new_m', 'new_m_ffn2_up': 'new_m', 'new_m_ffn2_down': 'new_m', 'new_m_final_norm': 'new_m', 'new_v_meta': 'new_v', 'new_v_ffn1_norm': 'new_v', 'new_v_ffn1_gate': 'new_v', 'new_v_ffn1_up': 'new_v', 'new_v_ffn1_down': 'new_v', 'new_v_mix_norm': 'new_v', 'new_v_w_in': 'new_v', 'new_v_pool_maps': 'new_v', 'new_v_pool_scale': 'new_v', 'new_v_w_ret_up': 'new_v', 'new_v_w_pool_up': 'new_v', 'new_v_w_out': 'new_v', 'new_v_ffn2_norm': 'new_v', 'new_v_ffn2_gate': 'new_v', 'new_v_ffn2_up': 'new_v', 'new_v_ffn2_down': 'new_v', 'new_v_final_norm': 'new_v'}


def _forward(args):
    return _fwd_reference(*[args[k] for k in FWD_PARAMS])


def _output_shape():
    out = _jax.eval_shape(lambda: _forward(_fwd_setup_inputs(0)))
    return out.shape, out.dtype

N_MICROBATCH = 1
ADAM_LR = 0.001
ADAM_B1 = 0.9
ADAM_B2 = 0.999
ADAM_EPS = 1e-08
ADAM_WD = 0.01
ADAM_STEP = 10
PER_EXAMPLE_BATCH_AXIS = {'x': 0, 'loss_target': 0}
SHARED_INPUTS = []
_WEIGHT_DTYPES = {'meta': _jnp.float32, 'ffn1_norm': _jnp.float32, 'ffn1_gate': _jnp.float32, 'ffn1_up': _jnp.float32, 'ffn1_down': _jnp.float32, 'mix_norm': _jnp.float32, 'w_in': _jnp.float32, 'pool_maps': _jnp.float32, 'pool_scale': _jnp.float32, 'w_ret_up': _jnp.float32, 'w_pool_up': _jnp.float32, 'w_out': _jnp.float32, 'ffn2_norm': _jnp.float32, 'ffn2_gate': _jnp.float32, 'ffn2_up': _jnp.float32, 'ffn2_down': _jnp.float32, 'final_norm': _jnp.float32}
MOMENT_SCALE = {'meta': 8.490036e-03, 'ffn1_norm': 8.487066e-02, 'ffn1_gate': 3.606626e-02, 'ffn1_up': 3.497880e-02, 'ffn1_down': 5.785205e-02, 'mix_norm': 1.337781e-01, 'w_in': 6.157328e-02, 'pool_maps': 1.052655e-01, 'pool_scale': 1.072240e-01, 'w_ret_up': 5.014172e-02, 'w_pool_up': 7.395301e-02, 'w_out': 8.941294e-02, 'ffn2_norm': 6.393319e-02, 'ffn2_gate': 2.734570e-02, 'ffn2_up': 2.645241e-02, 'ffn2_down': 4.396475e-02, 'final_norm': 3.203034e+01}


def _to_microbatches(a, axis):
    t = _jnp.moveaxis(a, axis, 0)
    t = t.reshape((N_MICROBATCH, t.shape[0] // N_MICROBATCH) + t.shape[1:])
    return _jnp.moveaxis(t, 1, axis + 1)


def setup_inputs(seed: int = 0) -> dict:
    inp = _fwd_setup_inputs(seed)
    key = _jax.random.fold_in(_jax.random.key(seed), 7919)
    shape, _ = _output_shape()
    out = dict(inp)
    out["loss_target"] = _jax.random.normal(_jax.random.fold_in(key, 0), shape, _jnp.float32)
    for i, name in enumerate(TWIN_WEIGHTS):
        w = inp[name].astype(_jnp.float32)
        if MOMENT_SCALE is None:
            s = _jnp.sqrt(_jnp.mean(_jnp.square(w)) + 1e-30)
        else:
            s = MOMENT_SCALE[name]
        km, kv = _jax.random.split(_jax.random.fold_in(key, i + 1))
        out[name] = w
        out["m_" + name] = s * _jax.random.normal(km, w.shape, _jnp.float32)
        out["v_" + name] = (s * s) * _jax.random.uniform(kv, w.shape, _jnp.float32, 0.5, 1.5)
    if N_MICROBATCH > 1:
        for name, axis in PER_EXAMPLE_BATCH_AXIS.items():
            out[name] = _to_microbatches(out[name], axis)
    return {'x': out['x'], 'meta': out['meta'], 'ffn1_norm': out['ffn1_norm'], 'ffn1_gate': out['ffn1_gate'], 'ffn1_up': out['ffn1_up'], 'ffn1_down': out['ffn1_down'], 'mix_norm': out['mix_norm'], 'w_in': out['w_in'], 'pool_maps': out['pool_maps'], 'pool_scale': out['pool_scale'], 'w_ret_up': out['w_ret_up'], 'w_pool_up': out['w_pool_up'], 'w_out': out['w_out'], 'ffn2_norm': out['ffn2_norm'], 'ffn2_gate': out['ffn2_gate'], 'ffn2_up': out['ffn2_up'], 'ffn2_down': out['ffn2_down'], 'final_norm': out['final_norm'], 'loss_target': out['loss_target'], 'm_meta': out['m_meta'], 'm_ffn1_norm': out['m_ffn1_norm'], 'm_ffn1_gate': out['m_ffn1_gate'], 'm_ffn1_up': out['m_ffn1_up'], 'm_ffn1_down': out['m_ffn1_down'], 'm_mix_norm': out['m_mix_norm'], 'm_w_in': out['m_w_in'], 'm_pool_maps': out['m_pool_maps'], 'm_pool_scale': out['m_pool_scale'], 'm_w_ret_up': out['m_w_ret_up'], 'm_w_pool_up': out['m_w_pool_up'], 'm_w_out': out['m_w_out'], 'm_ffn2_norm': out['m_ffn2_norm'], 'm_ffn2_gate': out['m_ffn2_gate'], 'm_ffn2_up': out['m_ffn2_up'], 'm_ffn2_down': out['m_ffn2_down'], 'm_final_norm': out['m_final_norm'], 'v_meta': out['v_meta'], 'v_ffn1_norm': out['v_ffn1_norm'], 'v_ffn1_gate': out['v_ffn1_gate'], 'v_ffn1_up': out['v_ffn1_up'], 'v_ffn1_down': out['v_ffn1_down'], 'v_mix_norm': out['v_mix_norm'], 'v_w_in': out['v_w_in'], 'v_pool_maps': out['v_pool_maps'], 'v_pool_scale': out['v_pool_scale'], 'v_w_ret_up': out['v_w_ret_up'], 'v_w_pool_up': out['v_w_pool_up'], 'v_w_out': out['v_w_out'], 'v_ffn2_norm': out['v_ffn2_norm'], 'v_ffn2_gate': out['v_ffn2_gate'], 'v_ffn2_up': out['v_ffn2_up'], 'v_ffn2_down': out['v_ffn2_down'], 'v_final_norm': out['v_final_norm']}


def _loss(weights, diff, rest, loss_target):
    with _jax.named_scope("forward"):
        args = {**rest, TWIN_DIFF_INPUT: diff, **{k: w.astype(_WEIGHT_DTYPES[k]) for k, w in weights.items()}}
        y = _forward(args)
    with _jax.named_scope("loss_head"):
        err = _jnp.square(y.astype(_jnp.float32) - loss_target)
        return 0.5 * _jnp.sum(_jnp.mean(err, axis=-1)) if err.ndim else 0.5 * err


def _adamw(w, g, m, v):
    m = ADAM_B1 * m + (1.0 - ADAM_B1) * g
    v = ADAM_B2 * v + (1.0 - ADAM_B2) * _jnp.square(g)
    m_hat = m / (1.0 - ADAM_B1 ** ADAM_STEP)
    v_hat = v / (1.0 - ADAM_B2 ** ADAM_STEP)
    delta = -ADAM_LR * (m_hat / (_jnp.sqrt(v_hat) + ADAM_EPS) + ADAM_WD * w)
    return delta, m, v


def reference(x, meta, ffn1_norm, ffn1_gate, ffn1_up, ffn1_down, mix_norm, w_in, pool_maps, pool_scale, w_ret_up, w_pool_up, w_out, ffn2_norm, ffn2_gate, ffn2_up, ffn2_down, final_norm, loss_target, m_meta, m_ffn1_norm, m_ffn1_gate, m_ffn1_up, m_ffn1_down, m_mix_norm, m_w_in, m_pool_maps, m_pool_scale, m_w_ret_up, m_w_pool_up, m_w_out, m_ffn2_norm, m_ffn2_gate, m_ffn2_up, m_ffn2_down, m_final_norm, v_meta, v_ffn1_norm, v_ffn1_gate, v_ffn1_up, v_ffn1_down, v_mix_norm, v_w_in, v_pool_maps, v_pool_scale, v_w_ret_up, v_w_pool_up, v_w_out, v_ffn2_norm, v_ffn2_gate, v_ffn2_up, v_ffn2_down, v_final_norm):
    given = dict(x=x, meta=meta, ffn1_norm=ffn1_norm, ffn1_gate=ffn1_gate, ffn1_up=ffn1_up, ffn1_down=ffn1_down, mix_norm=mix_norm, w_in=w_in, pool_maps=pool_maps, pool_scale=pool_scale, w_ret_up=w_ret_up, w_pool_up=w_pool_up, w_out=w_out, ffn2_norm=ffn2_norm, ffn2_gate=ffn2_gate, ffn2_up=ffn2_up, ffn2_down=ffn2_down, final_norm=final_norm, loss_target=loss_target, m_meta=m_meta, m_ffn1_norm=m_ffn1_norm, m_ffn1_gate=m_ffn1_gate, m_ffn1_up=m_ffn1_up, m_ffn1_down=m_ffn1_down, m_mix_norm=m_mix_norm, m_w_in=m_w_in, m_pool_maps=m_pool_maps, m_pool_scale=m_pool_scale, m_w_ret_up=m_w_ret_up, m_w_pool_up=m_w_pool_up, m_w_out=m_w_out, m_ffn2_norm=m_ffn2_norm, m_ffn2_gate=m_ffn2_gate, m_ffn2_up=m_ffn2_up, m_ffn2_down=m_ffn2_down, m_final_norm=m_final_norm, v_meta=v_meta, v_ffn1_norm=v_ffn1_norm, v_ffn1_gate=v_ffn1_gate, v_ffn1_up=v_ffn1_up, v_ffn1_down=v_ffn1_down, v_mix_norm=v_mix_norm, v_w_in=v_w_in, v_pool_maps=v_pool_maps, v_pool_scale=v_pool_scale, v_w_ret_up=v_w_ret_up, v_w_pool_up=v_w_pool_up, v_w_out=v_w_out, v_ffn2_norm=v_ffn2_norm, v_ffn2_gate=v_ffn2_gate, v_ffn2_up=v_ffn2_up, v_ffn2_down=v_ffn2_down, v_final_norm=v_final_norm)
    weights = {n: given[n] for n in TWIN_WEIGHTS}
    shared = {n: given[n] for n in SHARED_INPUTS}
    per_example = {n: given[n] for n in ['x']}
    grad_fn = _jax.value_and_grad(_loss, argnums=(0, 1))

    def one_microbatch(ex, loss_target):
        ex = dict(ex)
        diff = ex.pop(TWIN_DIFF_INPUT)
        return grad_fn(weights, diff, {**shared, **ex}, loss_target)

    if N_MICROBATCH == 1:
        loss, (grad_w, grad_x) = one_microbatch(per_example, given["loss_target"])
    else:
        def body(carry, xs):
            loss_sum, grad_sum = carry
            l_k, (gw_k, gx_k) = one_microbatch(xs[0], xs[1])
            with _jax.named_scope("update"):
                return (loss_sum + l_k, _jax.tree.map(_jnp.add, grad_sum, gw_k)), gx_k

        init = (_jnp.zeros((), _jnp.float32), _jax.tree.map(_jnp.zeros_like, weights))
        (loss, grad_w), grad_x = _jax.lax.scan(body, init, (per_example, given["loss_target"]))
    with _jax.named_scope("update"):
        delta_w, new_m, new_v = {}, {}, {}
        for n in TWIN_WEIGHTS:
            delta_w[n], new_m[n], new_v[n] = _adamw(weights[n], grad_w[n], given["m_" + n], given["v_" + n])
    return (loss, grad_x, *[grad_w[n] for n in TWIN_WEIGHTS], *[delta_w[n] for n in TWIN_WEIGHTS],
            *[new_m[n] for n in TWIN_WEIGHTS], *[new_v[n] for n in TWIN_WEIGHTS])
```

```python
import functools
import math

import jax
import jax.numpy as jnp
from jax import lax
from jax.experimental import pallas as pl
from jax.experimental.pallas import tpu as pltpu

F32 = jnp.float32
BF16 = jnp.bfloat16

N_META = 16
RET_HEADS = 4
HEAD_DIM = 128
RET_WIDTH = RET_HEADS * HEAD_DIM
POOL_WINDOWS = (2, 4, 8, 16)
POOL_GROUPS = len(POOL_WINDOWS)
POOL_WIDTH = POOL_GROUPS * HEAD_DIM
CHUNK = 128
ROPE_BASE = 10000.0
EPS = 1e-6
ADAM_LR = 0.001
ADAM_B1 = 0.9
ADAM_B2 = 0.999
ADAM_EPS = 1e-08
ADAM_WD = 0.01
ADAM_STEP = 10

N_CHIPS = 4
LANES = 128
BF16_ROWS = 16
V7X_VMEM_LIMIT = 52 * 1024 * 1024
MESH = pl.DeviceIdType.MESH
ANY = pl.BlockSpec(memory_space=pl.ANY)


def _round_up(n, m):
    return -(-n // m) * m


def _pick_tile(n, target, mult):
    best = None
    for d in range(mult, min(n, target) + 1, mult):
        if n % d == 0:
            best = d
    assert best is not None, (n, target, mult)
    return best


def _params(sem=None):
    return pltpu.CompilerParams(dimension_semantics=sem, vmem_limit_bytes=V7X_VMEM_LIMIT)


def _dot(a, b):
    return jnp.dot(a, b, preferred_element_type=F32)


def _dot_nt(a, b):
    return lax.dot_general(a, b, (((1,), (1,)), ((), ())), preferred_element_type=F32)


def _dot_tn(a, b):
    return lax.dot_general(a, b, (((0,), (0,)), ((), ())), preferred_element_type=F32)


def _ein(spec, a, b):
    return jnp.einsum(spec, a, b, preferred_element_type=F32)


def _sigmoid(x):
    return jax.nn.sigmoid(x)


def _rms_fwd(x, gain):
    r = lax.rsqrt(jnp.mean(x * x, axis=-1, keepdims=True) + EPS)
    return x * r * gain


def _rms_bwd(x, gain, da):
    r = lax.rsqrt(jnp.mean(x * x, axis=-1, keepdims=True) + EPS)
    xh = x * r
    dgain = jnp.sum(da * xh, axis=0, keepdims=True)
    dxh = da * gain
    dx = r * (dxh - xh * jnp.mean(dxh * xh, axis=-1, keepdims=True))
    return dx, dgain


def _row_mask(t, tm, pad, shape):
    rows = t * tm + lax.broadcasted_iota(jnp.int32, shape, 0)
    return rows >= pad


def _ffn_fwd(h, gain, wg, wu, wd, layer, tm, name):
    T, D = h.shape
    Fs = wg.shape[-1]
    F = N_CHIPS * Fs

    def body(h_ref, g_ref, wg_ref, wu_ref, wd_ref, ho_ref, a_ref, go_ref, uo_ref, act_ref, acc_ref):
        s = pl.program_id(1)

        @pl.when(s == 0)
        def _():
            a_ref[...] = _rms_fwd(h_ref[...], g_ref[...]).astype(BF16)
            acc_ref[...] = jnp.zeros_like(acc_ref)

        a = a_ref[...]
        g = _dot(a, wg_ref[...])
        u = _dot(a, wu_ref[...])
        act = (g * _sigmoid(g) * u).astype(BF16)
        go_ref[...] = g.astype(BF16)
        uo_ref[...] = u.astype(BF16)
        act_ref[...] = act
        acc_ref[...] += _dot(act, wd_ref[...])

        @pl.when(s == N_CHIPS - 1)
        def _():
            ho_ref[...] = h_ref[...] + 0.5 * acc_ref[...]

    row = pl.BlockSpec((tm, D), lambda t, s: (t, 0))
    col = pl.BlockSpec((tm, Fs), lambda t, s: (t, s))
    return pl.pallas_call(
        body,
        name=name,
        grid=(T // tm, N_CHIPS),
        in_specs=[
            row,
            pl.BlockSpec((None, 1, D), lambda t, s: (layer, 0, 0)),
            pl.BlockSpec((None, None, D, Fs), lambda t, s: (layer, s, 0, 0)),
            pl.BlockSpec((None, None, D, Fs), lambda t, s: (layer, s, 0, 0)),
            pl.BlockSpec((None, None, Fs, D), lambda t, s: (layer, s, 0, 0)),
        ],
        out_specs=[row, row, col, col, col],
        out_shape=[
            jax.ShapeDtypeStruct((T, D), F32),
            jax.ShapeDtypeStruct((T, D), BF16),
            jax.ShapeDtypeStruct((T, F), BF16),
            jax.ShapeDtypeStruct((T, F), BF16),
            jax.ShapeDtypeStruct((T, F), BF16),
        ],
        scratch_shapes=[pltpu.VMEM((tm, D), F32)],
        compiler_params=_params(("parallel", "arbitrary")),
    )(h, gain, wg, wu, wd)


def _inproj_fwd(h, gain, win, layer, tm, name):
    T, D = h.shape
    Ns = win.shape[-1]

    def body(h_ref, g_ref, w_ref, z_ref, b_ref):
        @pl.when(pl.program_id(1) == 0)
        def _():
            b_ref[...] = _rms_fwd(h_ref[...], g_ref[...]).astype(BF16)

        z_ref[...] = _dot(b_ref[...], w_ref[...])

    return pl.pallas_call(
        body,
        name=name,
        grid=(T // tm, N_CHIPS),
        in_specs=[
            pl.BlockSpec((tm, D), lambda t, s: (t, 0)),
            pl.BlockSpec((None, 1, D), lambda t, s: (layer, 0, 0)),
            pl.BlockSpec((None, None, D, Ns), lambda t, s: (layer, s, 0, 0)),
        ],
        out_specs=[
            pl.BlockSpec((tm, Ns), lambda t, s: (t, s)),
            pl.BlockSpec((tm, D), lambda t, s: (t, 0)),
        ],
        out_shape=[
            jax.ShapeDtypeStruct((T, N_CHIPS * Ns), F32),
            jax.ShapeDtypeStruct((T, D), BF16),
        ],
        compiler_params=_params(("parallel", "arbitrary")),
    )(h, gain, win)


def _ret_consts(T, pad):
    half = HEAD_DIM // 2
    inv_freq = ROPE_BASE ** (-jnp.arange(half, dtype=F32) / half)
    pos = jnp.arange(T, dtype=F32) - pad
    ang = pos[:, None] * inv_freq[None, :]
    cos = jnp.cos(ang)
    sin = jnp.sin(ang)
    cosf = jnp.concatenate([cos, cos], axis=1)
    sinf = jnp.concatenate([-sin, sin], axis=1)
    log_gamma = jnp.log1p(-(2.0 ** (-5.0 - jnp.arange(RET_HEADS, dtype=F32))))
    idx = jnp.arange(CHUNK, dtype=F32)
    diff = idx[:, None] - idx[None, :]
    intra = jnp.where(diff[None] >= 0, jnp.exp(diff[None] * log_gamma[:, None, None]), 0.0)
    k_decay = jnp.exp((CHUNK - 1.0 - idx)[None, :] * log_gamma[:, None])
    q_decay = jnp.exp((idx + 1.0)[None, :] * log_gamma[:, None])
    chunk_decay = jnp.exp(CHUNK * log_gamma)
    kdec = jnp.broadcast_to(k_decay[:, :, None], (RET_HEADS, CHUNK, HEAD_DIM))
    qdec = jnp.broadcast_to(q_decay[:, :, None], (RET_HEADS, CHUNK, HEAD_DIM))
    cdb = jnp.broadcast_to(chunk_decay[:, None, None], (RET_HEADS, 8, HEAD_DIM))
    return cosf, sinf, intra, kdec, qdec, cdb


def _rot(t, cosv, sinv):
    return t * cosv + pltpu.roll(t, HEAD_DIM // 2, 1) * sinv


def _rot_t(g, cosv, sinv):
    return g * cosv + pltpu.roll(g * sinv, HEAD_DIM // 2, 1)


def _head_specs(tg, section, order):
    return pl.BlockSpec((tg, HEAD_DIM), lambda h, g: (order(g), section * RET_HEADS + h))


def _ret_fwd(z, consts, cg, name):
    T = z.shape[0]
    N = T // CHUNK
    ng = N // cg
    tg = cg * CHUNK
    cosf, sinf, intra, kdec, qdec, cdb = consts
    fwd = lambda g: g

    def body(zq, zk, zv, zg, cos_ref, sin_ref, m_ref, kd_ref, qd_ref, cd_ref, r_ref, o_ref, s_ref, st_ref):
        @pl.when(pl.program_id(1) == 0)
        def _():
            st_ref[...] = jnp.zeros_like(st_ref)

        cosv = cos_ref[...]
        sinv = sin_ref[...]
        q3 = (_rot(zq[...], cosv, sinv) * (HEAD_DIM ** -0.5)).reshape(cg, CHUNK, HEAD_DIM)
        k3 = _rot(zk[...], cosv, sinv).reshape(cg, CHUNK, HEAD_DIM)
        vb = zv[...].reshape(cg, CHUNK, HEAD_DIM).astype(BF16)
        scores = _ein("ncd,nmd->ncm", q3.astype(BF16), k3.astype(BF16)) * m_ref[...][None]
        inner = _ein("ncm,nmd->ncd", scores.astype(BF16), vb)
        kv = _ein("ncd,nce->nde", (k3 * kd_ref[...][None]).astype(BF16), vb)
        cd = cd_ref[0:1, :]
        state = st_ref[...]
        for n in range(cg):
            s_ref[n] = state
            state = state * cd + kv[n]
        st_ref[...] = state
        qdb = (q3 * qd_ref[...][None]).astype(BF16)
        cross = _ein("ncd,nde->nce", qdb, s_ref[...].astype(BF16))
        out = (inner + cross).reshape(tg, HEAD_DIM)
        o_ref[...] = out
        xc = out - jnp.mean(out, axis=-1, keepdims=True)
        rn = xc * lax.rsqrt(jnp.mean(xc * xc, axis=-1, keepdims=True) + EPS)
        g = zg[...]
        r_ref[...] = (rn * (g * _sigmoid(g))).astype(BF16)

    tab = pl.BlockSpec((tg, HEAD_DIM), lambda h, g: (g, 0))
    per_head = lambda rows: pl.BlockSpec((None, rows, HEAD_DIM), lambda h, g: (h, 0, 0))
    head_out = pl.BlockSpec((tg, HEAD_DIM), lambda h, g: (g, h))
    return pl.pallas_call(
        body,
        name=name,
        grid=(RET_HEADS, ng),
        in_specs=[_head_specs(tg, i, fwd) for i in range(4)]
        + [tab, tab, per_head(CHUNK), per_head(CHUNK), per_head(CHUNK), per_head(8)],
        out_specs=[
            head_out,
            head_out,
            pl.BlockSpec((None, cg, HEAD_DIM, HEAD_DIM), lambda h, g: (h, g, 0, 0)),
        ],
        out_shape=[
            jax.ShapeDtypeStruct((T, RET_WIDTH), BF16),
            jax.ShapeDtypeStruct((T, RET_WIDTH), F32),
            jax.ShapeDtypeStruct((RET_HEADS, N, HEAD_DIM, HEAD_DIM), F32),
        ],
        scratch_shapes=[pltpu.VMEM((HEAD_DIM, HEAD_DIM), F32)],
        compiler_params=_params(("parallel", "arbitrary")),
    )(z, z, z, z, cosf, sinf, intra, kdec, qdec, cdb)


def _window_sums(u, shift_of):
    sums = []
    s = u
    k = 1
    while k < POOL_WINDOWS[-1]:
        s = s + pltpu.roll(s, shift_of(k), 0)
        sums.append(s)
        k *= 2
    return sums


def _select_group(vals, g):
    out = vals[-1]
    for i in range(len(vals) - 2, -1, -1):
        out = jnp.where(g == i, vals[i], out)
    return out


def _pool_parts(u, g, T, pad):
    rows = lax.broadcasted_iota(jnp.int32, (T, HEAD_DIM), 0)
    valid = rows >= pad
    win = _select_group([float(w) for w in POOL_WINDOWS], g)
    div = jnp.clip((rows - pad + 1).astype(F32), 1.0, win)
    s = _select_group(_window_sums(u, lambda k: k), g)
    pooled = jnp.where(valid, s / div - u, 0.0)
    return pooled, div, valid


def _pool_specs(T, layer):
    first = 4 * RET_WIDTH // HEAD_DIM
    return [
        pl.BlockSpec((T, HEAD_DIM), lambda g: (0, first + g)),
        pl.BlockSpec((None, None, HEAD_DIM, HEAD_DIM), lambda g: (layer, g, 0, 0)),
        pl.BlockSpec((None, 1, HEAD_DIM), lambda g: (layer, 0, g)),
    ]


def _pool_fwd(z, maps, scale, layer, pad, name):
    T = z.shape[0]
    assert pad >= POOL_WINDOWS[-1], "window rolls wrap into the zero rows in front"

    def body(zu, maps_ref, sc_ref, pm_ref):
        g = pl.program_id(0)
        pooled, _, _ = _pool_parts(zu[...], g, T, pad)
        y = _dot(pooled.astype(BF16), maps_ref[...].astype(BF16))
        pm_ref[...] = (y * sc_ref[...]).astype(BF16)

    return pl.pallas_call(
        body,
        name=name,
        grid=(POOL_GROUPS,),
        in_specs=_pool_specs(T, layer),
        out_specs=pl.BlockSpec((T, HEAD_DIM), lambda g: (0, g)),
        out_shape=jax.ShapeDtypeStruct((T, POOL_WIDTH), BF16),
        compiler_params=_params(("parallel",)),
    )(z, maps, scale)


def _gate_specs(tm, D):
    nb = D // RET_WIDTH
    first = (4 * RET_WIDTH + POOL_WIDTH) // RET_WIDTH
    return [pl.BlockSpec((tm, RET_WIDTH), functools.partial(lambda t, j: (t, j), j=first + j)) for j in range(2 * nb)]


def _load_gates(refs, nb):
    ga = jnp.concatenate([r[...] for r in refs[:nb]], axis=1) if nb > 1 else refs[0][...]
    gb = jnp.concatenate([r[...] for r in refs[nb:]], axis=1) if nb > 1 else refs[nb][...]
    return ga, gb


def _mix_fwd(h, r, pm, z, wru, wpu, wout, layer, tm, name):
    T, D = h.shape
    Dq = D // N_CHIPS
    nb = D // RET_WIDTH

    def body(*refs):
        h_ref, r_ref, pm_ref = refs[:3]
        gate_refs = refs[3:3 + 2 * nb]
        wru_ref, wpu_ref, wout_ref, ho_ref, mx_ref, ret_ref, pool_ref = refs[3 + 2 * nb:]
        rv = r_ref[...]
        pv = pm_ref[...]
        ret = jnp.concatenate([_dot(rv, wru_ref[s]) for s in range(N_CHIPS)], axis=1)
        pool = jnp.concatenate([_dot(pv, wpu_ref[s]) for s in range(N_CHIPS)], axis=1)
        ga, gb = _load_gates(gate_refs, nb)
        mixed = (_sigmoid(ga) * ret + _sigmoid(gb) * pool).astype(BF16)
        mx_ref[...] = mixed
        ret_ref[...] = ret.astype(BF16)
        pool_ref[...] = pool.astype(BF16)
        ho_ref[...] = h_ref[...] + _dot(mixed, wout_ref[...].reshape(D, D))

    row = pl.BlockSpec((tm, D), lambda t: (t, 0))
    half = pl.BlockSpec((tm, RET_WIDTH), lambda t: (t, 0))
    up = pl.BlockSpec((None, N_CHIPS, RET_WIDTH, Dq), lambda t: (layer, 0, 0, 0))
    return pl.pallas_call(
        body,
        name=name,
        grid=(T // tm,),
        in_specs=[row, half, half] + _gate_specs(tm, D)
        + [up, up, pl.BlockSpec((None, N_CHIPS, Dq, D), lambda t: (layer, 0, 0, 0))],
        out_specs=[row, row, row, row],
        out_shape=[jax.ShapeDtypeStruct((T, D), F32)] + [jax.ShapeDtypeStruct((T, D), BF16)] * 3,
        compiler_params=_params(("parallel",)),
    )(h, r, pm, *([z] * (2 * nb)), wru, wpu, wout)


def _final_loss(h, gain, tgt, name):
    T, D = h.shape
    first = (T - tgt.shape[0]) // CHUNK

    def body(h_ref, g_ref, t_ref, dh_ref, loss_ref, dg_ref):
        i = pl.program_id(0)

        @pl.when(i == 0)
        def _():
            loss_ref[...] = jnp.zeros_like(loss_ref)
            dg_ref[...] = jnp.zeros_like(dg_ref)

        x = h_ref[...]
        gain_v = g_ref[...]
        err = jnp.where(i >= first, _rms_fwd(x, gain_v) - t_ref[...], 0.0)
        loss_ref[...] += 0.5 * jnp.sum(jnp.mean(err * err, axis=-1))
        dx, dgain = _rms_bwd(x, gain_v, err * (1.0 / D))
        dg_ref[...] += dgain
        dh_ref[...] = dx

    return pl.pallas_call(
        body,
        name=name,
        grid=(T // CHUNK,),
        in_specs=[
            pl.BlockSpec((CHUNK, D), lambda i: (i, 0)),
            pl.BlockSpec((1, D), lambda i: (0, 0)),
            pl.BlockSpec((CHUNK, D), lambda i: (jnp.maximum(i - first, 0), 0)),
        ],
        out_specs=[
            pl.BlockSpec((CHUNK, D), lambda i: (i, 0)),
            pl.BlockSpec((1, LANES), lambda i: (0, 0)),
            pl.BlockSpec((1, D), lambda i: (0, 0)),
        ],
        out_shape=[
            jax.ShapeDtypeStruct((T, D), F32),
            jax.ShapeDtypeStruct((1, LANES), F32),
            jax.ShapeDtypeStruct((1, D), F32),
        ],
        compiler_params=_params(("arbitrary",)),
    )(h, gain, tgt)


def _ffn_bwd_dx(dy, h, gain, g, u, wg, wu, wd, layer, tm, pad, name):
    T, D = h.shape
    Fs = wg.shape[-1]
    F = N_CHIPS * Fs

    def body(dy_ref, h_ref, g_ref, go_ref, uo_ref, wg_ref, wu_ref, wd_ref,
             dh_ref, dg_ref, du_ref, dgain_ref, dyh_ref, da_ref):
        t = pl.program_id(0)
        s = pl.program_id(1)

        @pl.when((t == 0) & (s == 0))
        def _():
            dgain_ref[...] = jnp.zeros_like(dgain_ref)

        @pl.when(s == 0)
        def _():
            dyh_ref[...] = (0.5 * dy_ref[...]).astype(BF16)
            da_ref[...] = jnp.zeros_like(da_ref)

        dact = _dot_nt(dyh_ref[...], wd_ref[...])
        gf = go_ref[...].astype(F32)
        uf = uo_ref[...].astype(F32)
        sg = _sigmoid(gf)
        du = (dact * (gf * sg)).astype(BF16)
        dg = (dact * uf * (sg * (1.0 + gf * (1.0 - sg)))).astype(BF16)
        dg_ref[...] = dg
        du_ref[...] = du
        da_ref[...] += _dot_nt(dg, wg_ref[...]) + _dot_nt(du, wu_ref[...])

        @pl.when(s == N_CHIPS - 1)
        def _():
            dx, dgain = _rms_bwd(h_ref[...], g_ref[...], da_ref[...])
            dgain_ref[...] += dgain
            dh_ref[...] = jnp.where(_row_mask(t, tm, pad, (tm, D)), dy_ref[...] + dx, 0.0)

    row = pl.BlockSpec((tm, D), lambda t, s: (t, 0))
    col = pl.BlockSpec((tm, Fs), lambda t, s: (t, s))
    wcol = pl.BlockSpec((None, None, D, Fs), lambda t, s: (layer, s, 0, 0))
    return pl.pallas_call(
        body,
        name=name,
        grid=(T // tm, N_CHIPS),
        in_specs=[row, row, pl.BlockSpec((None, 1, D), lambda t, s: (layer, 0, 0)), col, col, wcol, wcol,
                  pl.BlockSpec((None, None, Fs, D), lambda t, s: (layer, s, 0, 0))],
        out_specs=[row, col, col, pl.BlockSpec((1, D), lambda t, s: (0, 0))],
        out_shape=[
            jax.ShapeDtypeStruct((T, D), F32),
            jax.ShapeDtypeStruct((T, F), BF16),
            jax.ShapeDtypeStruct((T, F), BF16),
            jax.ShapeDtypeStruct((1, D), F32),
        ],
        scratch_shapes=[pltpu.VMEM((tm, D), BF16), pltpu.VMEM((tm, D), F32)],
        compiler_params=_params(("arbitrary", "arbitrary")),
    )(dy, h, gain, g, u, wg, wu, wd)


def _grad_tn(a, b, prev, layer, n_layers, mode, scale, tm, name):
    T = a.shape[0]
    if mode == "col":
        R, C = a.shape[1], b.shape[1] // N_CHIPS
        a_spec = pl.BlockSpec((tm, R), lambda s, t: (t, 0))
        b_spec = pl.BlockSpec((tm, C), lambda s, t: (t, s))
    else:
        R, C = a.shape[1] // N_CHIPS, b.shape[1]
        a_spec = pl.BlockSpec((tm, R), lambda s, t: (t, s))
        b_spec = pl.BlockSpec((tm, C), lambda s, t: (t, 0))
    nt = T // tm

    def body(*refs):
        a_ref, b_ref = refs[:2]
        o_ref, acc_ref = refs[-2:]
        t = pl.program_id(1)

        @pl.when(t == 0)
        def _():
            acc_ref[...] = jnp.zeros_like(acc_ref)

        acc_ref[...] += _dot_tn(a_ref[...].astype(BF16), b_ref[...].astype(BF16))

        @pl.when(t == nt - 1)
        def _():
            o_ref[...] = (scale * acc_ref[...]).astype(BF16)

    operands = [a, b] + ([] if prev is None else [prev])
    return pl.pallas_call(
        body,
        name=name,
        grid=(N_CHIPS, nt),
        in_specs=[a_spec, b_spec] + ([] if prev is None else [ANY]),
        out_specs=pl.BlockSpec((None, None, R, C), lambda s, t: (layer, s, 0, 0)),
        out_shape=jax.ShapeDtypeStruct((n_layers, N_CHIPS, R, C), BF16),
        scratch_shapes=[pltpu.VMEM((R, C), F32)],
        input_output_aliases={} if prev is None else {2: 0},
        compiler_params=_params(("parallel", "arbitrary")),
    )(*operands)


def _mix_bwd_dx(dh, z, ret, pool, wout, wru, wpu, layer, tm, name):
    T, D = dh.shape
    Dq = D // N_CHIPS
    nb = D // RET_WIDTH

    def body(*refs):
        dh_ref = refs[0]
        gate_refs = refs[1:1 + 2 * nb]
        ret_ref, pool_ref, wout_ref, wru_ref, wpu_ref, dgab_ref, dret_ref, dpool_ref, dr_ref, dpm_ref = refs[1 + 2 * nb:]
        dmixed = _dot_nt(dh_ref[...].astype(BF16), wout_ref[...].reshape(D, D))
        ga, gb = _load_gates(gate_refs, nb)
        sa = _sigmoid(ga)
        sb = _sigmoid(gb)
        dgab_ref[:, :D] = (dmixed * ret_ref[...].astype(F32) * (sa * (1.0 - sa))).astype(BF16)
        dgab_ref[:, D:] = (dmixed * pool_ref[...].astype(F32) * (sb * (1.0 - sb))).astype(BF16)
        dret = (dmixed * sa).astype(BF16)
        dpool = (dmixed * sb).astype(BF16)
        dret_ref[...] = dret
        dpool_ref[...] = dpool
        dr = _dot_nt(dret[:, :Dq], wru_ref[0])
        dpm = _dot_nt(dpool[:, :Dq], wpu_ref[0])
        for s in range(1, N_CHIPS):
            dr += _dot_nt(dret[:, s * Dq:(s + 1) * Dq], wru_ref[s])
            dpm += _dot_nt(dpool[:, s * Dq:(s + 1) * Dq], wpu_ref[s])
        dr_ref[...] = dr
        dpm_ref[...] = dpm

    row = pl.BlockSpec((tm, D), lambda t: (t, 0))
    half = pl.BlockSpec((tm, RET_WIDTH), lambda t: (t, 0))
    up = pl.BlockSpec((None, N_CHIPS, RET_WIDTH, Dq), lambda t: (layer, 0, 0, 0))
    return pl.pallas_call(
        body,
        name=name,
        grid=(T // tm,),
        in_specs=[row] + _gate_specs(tm, D)
        + [row, row, pl.BlockSpec((None, N_CHIPS, Dq, D), lambda t: (layer, 0, 0, 0)), up, up],
        out_specs=[pl.BlockSpec((tm, 2 * D), lambda t: (t, 0)), row, row, half, half],
        out_shape=[
            jax.ShapeDtypeStruct((T, 2 * D), BF16),
            jax.ShapeDtypeStruct((T, D), BF16),
            jax.ShapeDtypeStruct((T, D), BF16),
            jax.ShapeDtypeStruct((T, RET_WIDTH), F32),
            jax.ShapeDtypeStruct((T, POOL_WIDTH), F32),
        ],
        compiler_params=_params(("parallel",)),
    )(dh, *([z] * (2 * nb)), ret, pool, wout, wru, wpu)


def _pool_bwd(z, dpm, maps, scale, layer, pad, name):
    T = z.shape[0]

    def body(zu, maps_ref, sc_ref, dpm_ref, du_ref, dmaps_ref, dsc_ref):
        g = pl.program_id(0)
        u = zu[...]
        pooled, div, valid = _pool_parts(u, g, T, pad)
        pb = pooled.astype(BF16)
        mb = maps_ref[...].astype(BF16)
        dp = dpm_ref[...]
        dsc_ref[...] = jnp.sum(dp * _dot(pb, mb), axis=0, keepdims=True)
        dyb = (dp * sc_ref[...]).astype(BF16)
        dmaps_ref[...] = _dot_tn(pb, dyb)
        dpooled = jnp.where(valid, _dot_nt(dyb, mb), 0.0)
        ahead = _select_group(_window_sums(dpooled / div, lambda k: T - k), g)
        du_ref[...] = jnp.where(valid, ahead - dpooled, 0.0).astype(BF16)

    blk = pl.BlockSpec((T, HEAD_DIM), lambda g: (0, g))
    return pl.pallas_call(
        body,
        name=name,
        grid=(POOL_GROUPS,),
        in_specs=_pool_specs(T, layer) + [blk],
        out_specs=[
            blk,
            pl.BlockSpec((None, HEAD_DIM, HEAD_DIM), lambda g: (g, 0, 0)),
            pl.BlockSpec((1, HEAD_DIM), lambda g: (0, g)),
        ],
        out_shape=[
            jax.ShapeDtypeStruct((T, POOL_WIDTH), BF16),
            jax.ShapeDtypeStruct((POOL_GROUPS, HEAD_DIM, HEAD_DIM), F32),
            jax.ShapeDtypeStruct((1, POOL_WIDTH), F32),
        ],
        compiler_params=_params(("parallel",)),
    )(z, maps, scale, dpm)


def _ret_bwd_local(z, o_pre, s_all, dr, consts, cg, name):
    T = z.shape[0]
    N = T // CHUNK
    ng = N // cg
    tg = cg * CHUNK
    cosf, sinf, intra, _, qdec, _ = consts
    fwd = lambda g: g

    def body(zq, zk, zv, zg, o_ref, s_ref, dr_ref, cos_ref, sin_ref, m_ref, qd_ref,
             dq_ref, dg_ref, dk_ref, dv_ref, ds_ref):
        cosv = cos_ref[...]
        sinv = sin_ref[...]
        scale = HEAD_DIM ** -0.5
        q3 = (_rot(zq[...], cosv, sinv) * scale).reshape(cg, CHUNK, HEAD_DIM)
        k3 = _rot(zk[...], cosv, sinv).reshape(cg, CHUNK, HEAD_DIM)
        qb = q3.astype(BF16)
        kb = k3.astype(BF16)
        vb = zv[...].reshape(cg, CHUNK, HEAD_DIM).astype(BF16)
        mask = m_ref[...][None]
        sb = (_ein("ncd,nmd->ncm", qb, kb) * mask).astype(BF16)
        qdv = qd_ref[...][None]
        qdb = (q3 * qdv).astype(BF16)

        out = o_ref[...]
        xc = out - jnp.mean(out, axis=-1, keepdims=True)
        rstd = lax.rsqrt(jnp.mean(xc * xc, axis=-1, keepdims=True) + EPS)
        rn = xc * rstd
        g = zg[...]
        sg = _sigmoid(g)
        drv = dr_ref[...]
        dg_ref[...] = (drv * rn * (sg * (1.0 + g * (1.0 - sg)))).astype(BF16)
        drn = drv * (g * sg)
        dout = rstd * (drn - jnp.mean(drn, axis=-1, keepdims=True)
                       - rn * jnp.mean(drn * rn, axis=-1, keepdims=True))
        dob = dout.reshape(cg, CHUNK, HEAD_DIM).astype(BF16)

        dsb = (_ein("ncd,nmd->ncm", dob, vb) * mask).astype(BF16)
        dv_ref[...] = _ein("ncm,ncd->nmd", sb, dob).reshape(tg, HEAD_DIM)
        dk_ref[...] = _ein("ncm,ncd->nmd", dsb, qb).reshape(tg, HEAD_DIM)
        dq3 = _ein("ncm,nmd->ncd", dsb, kb) + _ein("nce,nde->ncd", dob, s_ref[...].astype(BF16)) * qdv
        dq_ref[...] = _rot_t(dq3.reshape(tg, HEAD_DIM) * scale, cosv, sinv).astype(BF16)
        ds_ref[...] = _ein("ncd,nce->nde", qdb, dob)

    tab = pl.BlockSpec((tg, HEAD_DIM), lambda h, g: (g, 0))
    per_head = pl.BlockSpec((None, CHUNK, HEAD_DIM), lambda h, g: (h, 0, 0))
    head_blk = pl.BlockSpec((tg, HEAD_DIM), lambda h, g: (g, h))
    state_blk = pl.BlockSpec((None, cg, HEAD_DIM, HEAD_DIM), lambda h, g: (h, g, 0, 0))
    return pl.pallas_call(
        body,
        name=name,
        grid=(RET_HEADS, ng),
        in_specs=[_head_specs(tg, i, fwd) for i in range(4)]
        + [head_blk, state_blk, head_blk, tab, tab, per_head, per_head],
        out_specs=[head_blk, head_blk, head_blk, head_blk, state_blk],
        out_shape=[
            jax.ShapeDtypeStruct((T, RET_WIDTH), BF16),
            jax.ShapeDtypeStruct((T, RET_WIDTH), BF16),
            jax.ShapeDtypeStruct((T, RET_WIDTH), F32),
            jax.ShapeDtypeStruct((T, RET_WIDTH), F32),
            jax.ShapeDtypeStruct((RET_HEADS, N, HEAD_DIM, HEAD_DIM), F32),
        ],
        compiler_params=_params(("parallel", "parallel")),
    )(z, z, z, z, o_pre, s_all, dr, cosf, sinf, intra, qdec)


def _ret_bwd_state(z, dkp, dvp, ds, consts, cg, name):
    T = z.shape[0]
    N = T // CHUNK
    ng = N // cg
    tg = cg * CHUNK
    cosf, sinf, _, kdec, _, cdb = consts
    rev = lambda g: ng - 1 - g

    def body(zk, zv, dkp_ref, dvp_ref, ds_ref, cos_ref, sin_ref, kd_ref, cd_ref, dk_ref, dv_ref, gs_ref, dkv_ref):
        @pl.when(pl.program_id(1) == 0)
        def _():
            gs_ref[...] = jnp.zeros_like(gs_ref)

        cosv = cos_ref[...]
        sinv = sin_ref[...]
        cd = cd_ref[0:1, :]
        grad = gs_ref[...]
        for n in reversed(range(cg)):
            dkv_ref[n] = grad
            grad = ds_ref[n] + cd * grad
        gs_ref[...] = grad
        dkvb = dkv_ref[...].astype(BF16)
        kdv = kd_ref[...][None]
        k3 = _rot(zk[...], cosv, sinv).reshape(cg, CHUNK, HEAD_DIM)
        vb = zv[...].reshape(cg, CHUNK, HEAD_DIM).astype(BF16)
        dk3 = _ein("nce,nde->ncd", vb, dkvb) * kdv
        dv3 = _ein("ncd,nde->nce", (k3 * kdv).astype(BF16), dkvb)
        dk_ref[...] = _rot_t(dkp_ref[...] + dk3.reshape(tg, HEAD_DIM), cosv, sinv).astype(BF16)
        dv_ref[...] = (dvp_ref[...] + dv3.reshape(tg, HEAD_DIM)).astype(BF16)

    tab = pl.BlockSpec((tg, HEAD_DIM), lambda h, g: (rev(g), 0))
    head_blk = pl.BlockSpec((tg, HEAD_DIM), lambda h, g: (rev(g), h))
    return pl.pallas_call(
        body,
        name=name,
        grid=(RET_HEADS, ng),
        in_specs=[_head_specs(tg, 1, rev), _head_specs(tg, 2, rev), head_blk, head_blk,
                  pl.BlockSpec((None, cg, HEAD_DIM, HEAD_DIM), lambda h, g: (h, rev(g), 0, 0)),
                  tab, tab,
                  pl.BlockSpec((None, CHUNK, HEAD_DIM), lambda h, g: (h, 0, 0)),
                  pl.BlockSpec((None, 8, HEAD_DIM), lambda h, g: (h, 0, 0))],
        out_specs=[head_blk, head_blk],
        out_shape=[jax.ShapeDtypeStruct((T, RET_WIDTH), BF16)] * 2,
        scratch_shapes=[pltpu.VMEM((HEAD_DIM, HEAD_DIM), F32), pltpu.VMEM((cg, HEAD_DIM, HEAD_DIM), F32)],
        compiler_params=_params(("parallel", "arbitrary")),
    )(z, z, dkp, dvp, ds, cosf, sinf, kdec, cdb)


def _inproj_bwd_dx(dz, win, h, gain, dh_in, layer, tm, pad, name):
    T, D = h.shape
    Ns = win.shape[-1]

    def body(dz_ref, w_ref, h_ref, g_ref, dhi_ref, dh_ref, dgain_ref, db_ref):
        t = pl.program_id(0)
        s = pl.program_id(1)

        @pl.when((t == 0) & (s == 0))
        def _():
            dgain_ref[...] = jnp.zeros_like(dgain_ref)

        @pl.when(s == 0)
        def _():
            db_ref[...] = jnp.zeros_like(db_ref)

        db_ref[...] += _dot_nt(dz_ref[...], w_ref[...])

        @pl.when(s == N_CHIPS - 1)
        def _():
            dx, dgain = _rms_bwd(h_ref[...], g_ref[...], db_ref[...])
            dgain_ref[...] += dgain
            dh_ref[...] = jnp.where(_row_mask(t, tm, pad, (tm, D)), dhi_ref[...] + dx, 0.0)

    row = pl.BlockSpec((tm, D), lambda t, s: (t, 0))
    return pl.pallas_call(
        body,
        name=name,
        grid=(T // tm, N_CHIPS),
        in_specs=[
            pl.BlockSpec((tm, Ns), lambda t, s: (t, s)),
            pl.BlockSpec((None, None, D, Ns), lambda t, s: (layer, s, 0, 0)),
            row,
            pl.BlockSpec((None, 1, D), lambda t, s: (layer, 0, 0)),
            row,
        ],
        out_specs=[row, pl.BlockSpec((1, D), lambda t, s: (0, 0))],
        out_shape=[jax.ShapeDtypeStruct((T, D), F32), jax.ShapeDtypeStruct((1, D), F32)],
        scratch_shapes=[pltpu.VMEM((tm, D), F32)],
        compiler_params=_params(("arbitrary", "arbitrary")),
    )(dz, win, h, gain, dh_in)


def _mesh_pos():
    x, y, c = lax.axis_index("x"), lax.axis_index("y"), lax.axis_index("c")
    others = [(1 - x, y), (x, 1 - y), (1 - x, 1 - y)]
    return x, y, c, 2 * x + y, others


def _half_rows(c, rh):
    return pl.ds(pl.multiple_of(c * rh, rh), rh)


def _remote(src, dst, ssem, rsem, dev):
    return pltpu.make_async_remote_copy(src_ref=src, dst_ref=dst, send_sem=ssem, recv_sem=rsem,
                                        device_id=dev, device_id_type=MESH)


def _all_gather(shards):
    n = len(shards)

    def body(*refs):
        ins, outs = refs[:n], refs[n:2 * n]
        ssem, rsem, lsem = refs[2 * n:]
        x, y, c, chip, others = _mesh_pos()
        sends, local = [], []
        for i in range(n):
            rh = ins[i].shape[1] // 2
            mine = _half_rows(c, rh)
            loc = pltpu.make_async_copy(ins[i], outs[i].at[:, chip], lsem.at[i])
            loc.start()
            local.append(loc)
            for j, (ox, oy) in enumerate(others):
                cp = _remote(ins[i].at[:, mine, :], outs[i].at[:, chip, mine, :],
                             ssem.at[6 * i + j], rsem.at[6 * i + j], (ox, oy, c))
                cp.start()
                sends.append(cp)
        for i in range(n):
            rh = ins[i].shape[1] // 2
            mine = _half_rows(c, rh)
            for j, (ox, oy) in enumerate(others):
                rows = outs[i].at[:, 2 * ox + oy, mine, :]
                _remote(rows, rows, ssem.at[6 * i + j], rsem.at[6 * i + j], (ox, oy, c)).wait_recv()
                fwd = _remote(rows, rows, ssem.at[6 * i + 3 + j], rsem.at[6 * i + 3 + j], (x, y, 1 - c))
                fwd.start()
                sends.append(fwd)
        for i in range(n):
            rh = ins[i].shape[1] // 2
            theirs = _half_rows(1 - c, rh)
            for j, (ox, oy) in enumerate(others):
                rows = outs[i].at[:, 2 * ox + oy, theirs, :]
                _remote(rows, rows, ssem.at[6 * i + 3 + j], rsem.at[6 * i + 3 + j], (x, y, 1 - c)).wait_recv()
        for cp in sends:
            cp.wait_send()
        for cp in local:
            cp.wait()

    return pl.pallas_call(
        body,
        name="weights_all_gather",
        in_specs=[ANY] * n,
        out_specs=[ANY] * n,
        out_shape=[jax.ShapeDtypeStruct((s.shape[0], N_CHIPS) + s.shape[1:], s.dtype) for s in shards],
        scratch_shapes=[pltpu.SemaphoreType.DMA((6 * n,)), pltpu.SemaphoreType.DMA((6 * n,)),
                        pltpu.SemaphoreType.DMA((n,))],
    )(*shards)


def _pair_exchange(gs):
    n = len(gs)

    def body(*refs):
        ins, outs = refs[:n], refs[n:2 * n]
        ssem, rsem = refs[2 * n:]
        x, y, c, _, _ = _mesh_pos()
        copies = []
        for i in range(n):
            rh = ins[i].shape[2] // 2
            cp = _remote(ins[i].at[:, :, _half_rows(1 - c, rh), :], outs[i], ssem.at[i], rsem.at[i], (x, y, 1 - c))
            cp.start()
            copies.append(cp)
        for cp in copies:
            cp.wait()

    return pl.pallas_call(
        body,
        name="grads_pair_exchange",
        in_specs=[ANY] * n,
        out_specs=[ANY] * n,
        out_shape=[jax.ShapeDtypeStruct(g.shape[:2] + (g.shape[2] // 2, g.shape[3]), g.dtype) for g in gs],
        scratch_shapes=[pltpu.SemaphoreType.DMA((n,)), pltpu.SemaphoreType.DMA((n,))],
    )(*gs)


def _chip_exchange(ps):
    n = len(ps)

    def body(*refs):
        ins, outs = refs[:n], refs[n:2 * n]
        ssem, rsem, lsem = refs[2 * n:]
        x, y, c, chip, others = _mesh_pos()
        sends, local = [], []
        for i in range(n):
            loc = pltpu.make_async_copy(ins[i].at[:, chip], outs[i].at[:, chip], lsem.at[i])
            loc.start()
            local.append(loc)
            for j, (ox, oy) in enumerate(others):
                cp = _remote(ins[i].at[:, 2 * ox + oy], outs[i].at[:, chip],
                             ssem.at[3 * i + j], rsem.at[3 * i + j], (ox, oy, c))
                cp.start()
                sends.append(cp)
        for i in range(n):
            for j, (ox, oy) in enumerate(others):
                slot = outs[i].at[:, 2 * ox + oy]
                _remote(slot, slot, ssem.at[3 * i + j], rsem.at[3 * i + j], (ox, oy, c)).wait_recv()
        for cp in sends:
            cp.wait_send()
        for cp in local:
            cp.wait()

    return pl.pallas_call(
        body,
        name="grads_chip_exchange",
        in_specs=[ANY] * n,
        out_specs=[ANY] * n,
        out_shape=[jax.ShapeDtypeStruct(p.shape, p.dtype) for p in ps],
        scratch_shapes=[pltpu.SemaphoreType.DMA((3 * n,)), pltpu.SemaphoreType.DMA((3 * n,)),
                        pltpu.SemaphoreType.DMA((n,))],
    )(*ps)


def _pair_gather(fs):
    n = len(fs)

    def body(*refs):
        ins, outs = refs[:n], refs[n:2 * n]
        ssem, rsem, lsem = refs[2 * n:]
        x, y, c, _, _ = _mesh_pos()
        sends, local = [], []
        for i in range(n):
            rh = ins[i].shape[1]
            mine = outs[i].at[:, _half_rows(c, rh), :]
            loc = pltpu.make_async_copy(ins[i], mine, lsem.at[i])
            loc.start()
            local.append(loc)
            cp = _remote(ins[i], mine, ssem.at[i], rsem.at[i], (x, y, 1 - c))
            cp.start()
            sends.append(cp)
        for i in range(n):
            rh = ins[i].shape[1]
            theirs = outs[i].at[:, _half_rows(1 - c, rh), :]
            _remote(theirs, theirs, ssem.at[i], rsem.at[i], (x, y, 1 - c)).wait_recv()
        for cp in sends:
            cp.wait_send()
        for cp in local:
            cp.wait()

    return pl.pallas_call(
        body,
        name="grads_pair_gather",
        in_specs=[ANY] * n,
        out_specs=[ANY] * n,
        out_shape=[jax.ShapeDtypeStruct((f.shape[0], 2 * f.shape[1], f.shape[2]), f.dtype) for f in fs],
        scratch_shapes=[pltpu.SemaphoreType.DMA((n,)), pltpu.SemaphoreType.DMA((n,)),
                        pltpu.SemaphoreType.DMA((n,))],
    )(*fs)


def _sum_pair(g, r, c_idx, name):
    L, _, R, C = g.shape
    rh = R // 2

    def body(c_ref, g_ref, r_ref, o_ref):
        o_ref[...] = (g_ref[...].astype(F32) + r_ref[...].astype(F32)).astype(BF16)

    blk = pl.BlockSpec((None, None, rh, C), lambda l, s, c_ref: (l, s, 0, 0))
    return pl.pallas_call(
        body,
        name=name,
        grid_spec=pltpu.PrefetchScalarGridSpec(
            num_scalar_prefetch=1,
            grid=(L, N_CHIPS),
            in_specs=[pl.BlockSpec((None, None, rh, C), lambda l, s, c_ref: (l, s, c_ref[0], 0)), blk],
            out_specs=blk,
        ),
        out_shape=jax.ShapeDtypeStruct((L, N_CHIPS, rh, C), BF16),
        compiler_params=_params(("parallel", "parallel")),
    )(c_idx, g, r)


def _sum_chips(r, name):
    L, _, rh, C = r.shape

    def body(r_ref, o_ref):
        o_ref[...] = ((r_ref[0].astype(F32) + r_ref[1].astype(F32)) + r_ref[2].astype(F32)) + r_ref[3].astype(F32)

    return pl.pallas_call(
        body,
        name=name,
        grid=(L,),
        in_specs=[pl.BlockSpec((None, N_CHIPS, rh, C), lambda l: (l, 0, 0, 0))],
        out_specs=pl.BlockSpec((None, rh, C), lambda l: (l, 0, 0)),
        out_shape=jax.ShapeDtypeStruct((L, rh, C), F32),
        compiler_params=_params(("parallel",)),
    )(r)


def _small_all_reduce(p):
    rows, width = p.shape

    def body(p_ref, o_ref, sib_ref, slot_ref, ssem, rsem):
        x, y, c, chip, others = _mesh_pos()
        pair = _remote(p_ref, sib_ref, ssem.at[0], rsem.at[0], (x, y, 1 - c))
        pair.start()
        pair.wait()
        slot_ref[chip] = p_ref[...] + sib_ref[...]
        sends = []
        for j, (ox, oy) in enumerate(others):
            cp = _remote(slot_ref.at[chip], slot_ref.at[chip], ssem.at[1 + j], rsem.at[1 + j], (ox, oy, c))
            cp.start()
            sends.append(cp)
        for j, (ox, oy) in enumerate(others):
            slot = slot_ref.at[2 * ox + oy]
            _remote(slot, slot, ssem.at[1 + j], rsem.at[1 + j], (ox, oy, c)).wait_recv()
        for cp in sends:
            cp.wait_send()
        o_ref[...] = ((slot_ref[0] + slot_ref[1]) + slot_ref[2]) + slot_ref[3]

    vmem = pl.BlockSpec(memory_space=pltpu.VMEM)
    return pl.pallas_call(
        body,
        name="small_grads_all_reduce",
        in_specs=[vmem],
        out_specs=vmem,
        out_shape=jax.ShapeDtypeStruct(p.shape, F32),
        scratch_shapes=[pltpu.VMEM((rows, width), F32), pltpu.VMEM((N_CHIPS, rows, width), F32),
                        pltpu.SemaphoreType.DMA((4,)), pltpu.SemaphoreType.DMA((4,))],
    )(p)


def _adamw(g, w, m, v, name):
    L, R, C = w.shape
    Ct = g.shape[2]
    tr = _pick_tile(R, 256, 8)

    def body(g_ref, w_ref, m_ref, v_ref, go_ref, d_ref, mo_ref, vo_ref):
        grad = g_ref[...]
        if Ct != C:
            grad = grad[:, :C]
        m_new = ADAM_B1 * m_ref[...] + (1.0 - ADAM_B1) * grad
        v_new = ADAM_B2 * v_ref[...] + (1.0 - ADAM_B2) * jnp.square(grad)
        m_hat = m_new / (1.0 - ADAM_B1 ** ADAM_STEP)
        v_hat = v_new / (1.0 - ADAM_B2 ** ADAM_STEP)
        go_ref[...] = grad
        d_ref[...] = -ADAM_LR * (m_hat / (jnp.sqrt(v_hat) + ADAM_EPS) + ADAM_WD * w_ref[...])
        mo_ref[...] = m_new
        vo_ref[...] = v_new

    blk = pl.BlockSpec((None, tr, C), lambda l, r: (l, r, 0))
    return pl.pallas_call(
        body,
        name=name,
        grid=(L, R // tr),
        in_specs=[pl.BlockSpec((None, tr, Ct), lambda l, r: (l, r, 0)), blk, blk, blk],
        out_specs=[blk] * 4,
        out_shape=[jax.ShapeDtypeStruct((L, R, C), F32)] * 4,
        compiler_params=_params(("parallel", "parallel")),
    )(g, w, m, v)


_BIG = ("ffn1_gate", "ffn1_up", "ffn1_down", "w_in", "w_ret_up", "w_pool_up", "w_out",
        "ffn2_gate", "ffn2_up", "ffn2_down")
_SMALL = ("ffn1_norm", "mix_norm", "ffn2_norm", "final_norm", "pool_scale", "pool_maps")
_ORDER = ("meta", "ffn1_norm", "ffn1_gate", "ffn1_up", "ffn1_down", "mix_norm", "w_in", "pool_maps",
          "pool_scale", "w_ret_up", "w_pool_up", "w_out", "ffn2_norm", "ffn2_gate", "ffn2_up", "ffn2_down",
          "final_norm")


def _transport(a):
    r, c = a.shape[1], a.shape[2]
    return jnp.pad(a.astype(BF16), ((0, 0), (0, _round_up(r, LANES) - r), (0, _round_up(c, LANES) - c)))


def _pack_rows(parts, width):
    rows = [p.reshape(-1, width) for p in parts]
    total = sum(r.shape[0] for r in rows)
    fill = _round_up(total, 8) - total
    if fill:
        rows.append(jnp.zeros((fill, width), F32))
    return jnp.concatenate(rows, axis=0)


def _unpack_rows(packed, shapes, width):
    out, at = [], 0
    for shp in shapes:
        size = 1
        for d in shp:
            size *= d
        n = size // width
        out.append(packed[at:at + n].reshape(shp))
        at += n
    return out


def kernel(x, meta, ffn1_norm, ffn1_gate, ffn1_up, ffn1_down, mix_norm, w_in, pool_maps, pool_scale, w_ret_up, w_pool_up, w_out, ffn2_norm, ffn2_gate, ffn2_up, ffn2_down, final_norm, loss_target, m_meta, m_ffn1_norm, m_ffn1_gate, m_ffn1_up, m_ffn1_down, m_mix_norm, m_w_in, m_pool_maps, m_pool_scale, m_w_ret_up, m_w_pool_up, m_w_out, m_ffn2_norm, m_ffn2_gate, m_ffn2_up, m_ffn2_down, m_final_norm, v_meta, v_ffn1_norm, v_ffn1_gate, v_ffn1_up, v_ffn1_down, v_mix_norm, v_w_in, v_pool_maps, v_pool_scale, v_w_ret_up, v_w_pool_up, v_w_out, v_ffn2_norm, v_ffn2_gate, v_ffn2_up, v_ffn2_down, v_final_norm):
    args = dict(locals())
    w = {n: args[n] for n in _ORDER}
    mom = {n: args["m_" + n] for n in _ORDER}
    var = {n: args["v_" + n] for n in _ORDER}

    assert x.shape[0] == 1, "one batch element per device"
    seq, D = x.shape[1], x.shape[2]
    L = ffn1_gate.shape[0]
    assert seq % CHUNK == 0 and D % RET_WIDTH == 0 and (2 * POOL_WIDTH) % D == 0
    pad = (-(seq + N_META)) % CHUNK
    T = seq + N_META + pad
    Dq = D // N_CHIPS
    tm = _pick_tile(T, 528, BF16_ROWS)
    cg = _pick_tile(T // CHUNK, 11, 1)

    gathered = _all_gather([_transport(w[n]) for n in _BIG] + [meta[None]])
    wt = dict(zip(_BIG, gathered[:-1]))
    meta_full = jnp.transpose(gathered[-1][0], (1, 0, 2)).reshape(N_META, D)

    loss_acc, dh, gw, small, d_final = _local_step(x[0], meta_full, loss_target[0], w, wt, pad, tm, cg)
    loss = lax.psum(loss_acc[0, 0], ("x", "y", "c"))
    grad_x = dh[pad + N_META:][None]
    return _reduce_and_update(loss, grad_x, dh[pad:pad + N_META], gw, small, d_final, w, mom, var)


def _local_step(x, meta_full, tgt, w, wt, pad, tm, cg):
    D = x.shape[1]
    T = pad + N_META + x.shape[0]
    L = w["ffn1_norm"].shape[0]
    pool_maps = w["pool_maps"]
    gains = {n: w[n].reshape(L, 1, D) for n in ("ffn1_norm", "mix_norm", "ffn2_norm")}
    scale3 = w["pool_scale"].reshape(L, 1, POOL_WIDTH)
    consts = _ret_consts(T, pad)

    h = jnp.concatenate([jnp.zeros((pad, D), F32), meta_full, x], axis=0)
    saved = []
    for i in range(L):
        s = {"h0": h}
        h, s["a1"], s["g1"], s["u1"], s["act1"] = _ffn_fwd(
            h, gains["ffn1_norm"], wt["ffn1_gate"], wt["ffn1_up"], wt["ffn1_down"], i, tm, f"ffn1_fwd_{i}")
        s["h1"] = h
        s["z"], s["b"] = _inproj_fwd(h, gains["mix_norm"], wt["w_in"], i, tm, f"inproj_fwd_{i}")
        s["r"], s["o_pre"], s["s_all"] = _ret_fwd(s["z"], consts, cg, f"retention_fwd_{i}")
        s["pm"] = _pool_fwd(s["z"], pool_maps, scale3, i, pad, f"pool_fwd_{i}")
        h, s["mixed"], s["ret"], s["pool"] = _mix_fwd(
            h, s["r"], s["pm"], s["z"], wt["w_ret_up"], wt["w_pool_up"], wt["w_out"], i, tm, f"mix_fwd_{i}")
        s["h2"] = h
        h, s["a2"], s["g2"], s["u2"], s["act2"] = _ffn_fwd(
            h, gains["ffn2_norm"], wt["ffn2_gate"], wt["ffn2_up"], wt["ffn2_down"], i, tm, f"ffn2_fwd_{i}")
        saved.append(s)

    dh, loss_acc, d_final = _final_loss(h, w["final_norm"].reshape(1, D), tgt, "final_norm_loss")

    gw = {n: None for n in _BIG}
    small = {n: [None] * L for n in ("ffn1_norm", "mix_norm", "ffn2_norm", "pool_scale", "pool_maps")}

    def grad(n, a, b, i, mode, scale=1.0):
        gw[n] = _grad_tn(a, b, gw[n], i, L, mode, scale, tm, f"grad_{n}_{i}")

    for i in reversed(range(L)):
        s = saved[i]
        dh3 = dh
        dh, dg, du, small["ffn2_norm"][i] = _ffn_bwd_dx(
            dh3, s["h2"], gains["ffn2_norm"], s["g2"], s["u2"], wt["ffn2_gate"], wt["ffn2_up"], wt["ffn2_down"],
            i, tm, pad, f"ffn2_bwd_{i}")
        grad("ffn2_gate", s["a2"], dg, i, "col")
        grad("ffn2_up", s["a2"], du, i, "col")
        grad("ffn2_down", s["act2"], dh3, i, "row", 0.5)
        dgab, dret, dpool, dr, dpm = _mix_bwd_dx(
            dh, s["z"], s["ret"], s["pool"], wt["w_out"], wt["w_ret_up"], wt["w_pool_up"], i, tm, f"mix_bwd_{i}")
        grad("w_out", s["mixed"], dh, i, "row")
        grad("w_ret_up", s["r"], dret, i, "col")
        grad("w_pool_up", s["pm"], dpool, i, "col")
        du_pool, small["pool_maps"][i], small["pool_scale"][i] = _pool_bwd(
            s["z"], dpm, pool_maps, scale3, i, pad, f"pool_bwd_{i}")
        dq, dgr, dkp, dvp, ds = _ret_bwd_local(s["z"], s["o_pre"], s["s_all"], dr, consts, cg, f"retention_bwd_{i}")
        dk, dv = _ret_bwd_state(s["z"], dkp, dvp, ds, consts, cg, f"retention_bwd_state_{i}")
        dz = jnp.concatenate([dq, dk, dv, dgr, du_pool, dgab], axis=1)
        dh2 = dh
        dh, small["mix_norm"][i] = _inproj_bwd_dx(
            dz, wt["w_in"], s["h1"], gains["mix_norm"], dh2, i, tm, pad, f"inproj_bwd_{i}")
        grad("w_in", s["b"], dz, i, "col")
        dh1 = dh
        dh, dg, du, small["ffn1_norm"][i] = _ffn_bwd_dx(
            dh1, s["h0"], gains["ffn1_norm"], s["g1"], s["u1"], wt["ffn1_gate"], wt["ffn1_up"], wt["ffn1_down"],
            i, tm, pad, f"ffn1_bwd_{i}")
        grad("ffn1_gate", s["a1"], dg, i, "col")
        grad("ffn1_up", s["a1"], du, i, "col")
        grad("ffn1_down", s["act1"], dh1, i, "row", 0.5)

    return loss_acc, dh, gw, small, d_final


def _reduce_and_update(loss, grad_x, d_meta_rows, gw, small, d_final, w, mom, var):
    meta = w["meta"]
    D = w["final_norm"].shape[0]
    Dq = D // N_CHIPS

    c_idx = lax.axis_index("c").astype(jnp.int32).reshape(1)
    partials = [gw[n] for n in _BIG]
    from_sibling = _pair_exchange(partials)
    chip_partials = [_sum_pair(g, r, c_idx, f"sum_pair_{n}") for n, g, r in zip(_BIG, partials, from_sibling)]
    from_chips = _chip_exchange(chip_partials)
    halves = [_sum_chips(r, f"sum_chips_{n}") for n, r in zip(_BIG, from_chips)]
    shard_grads = dict(zip(_BIG, _pair_gather(halves)))

    small_parts = [jnp.concatenate(small[n], axis=0) for n in ("ffn1_norm", "mix_norm", "ffn2_norm")]
    small_parts += [d_final, jnp.concatenate(small["pool_scale"], axis=0), jnp.concatenate(small["pool_maps"], axis=0)]
    reduced = _small_all_reduce(_pack_rows(small_parts + [d_meta_rows], D))
    small_shapes = [w[n].shape for n in _SMALL]
    small_rows = sum(math.prod(shp) for shp in small_shapes) // D
    chip = 2 * lax.axis_index("x") + lax.axis_index("y")
    d_meta = lax.dynamic_slice_in_dim(reduced[small_rows:small_rows + N_META], chip * Dq, Dq, axis=1)

    out = {}
    for n in _BIG:
        out[n] = _adamw(shard_grads[n], w[n], mom[n], var[n], f"adamw_{n}")
    names = _SMALL + ("meta",)
    packed_g = _pack_rows([reduced[:small_rows], d_meta], D)
    packed = [_pack_rows([t[n] for n in names], D) for t in (w, mom, var)]
    res = _adamw(packed_g[None], packed[0][None], packed[1][None], packed[2][None], "adamw_small")
    shapes = small_shapes + [meta.shape]
    unpacked = [_unpack_rows(r[0], shapes, D) for r in res]
    for k, n in enumerate(names):
        out[n] = tuple(u[k] for u in unpacked)

    return (loss, grad_x) + tuple(out[n][j] for j in range(4) for n in _ORDER)
```

```python
import functools
import math

import jax
import jax.numpy as jnp
from jax import lax
from jax.experimental import pallas as pl
from jax.experimental.pallas import tpu as pltpu

F32 = jnp.float32
BF16 = jnp.bfloat16

N_META = 16
RET_HEADS = 4
HEAD_DIM = 128
RET_WIDTH = RET_HEADS * HEAD_DIM
POOL_WINDOWS = (2, 4, 8, 16)
POOL_GROUPS = len(POOL_WINDOWS)
POOL_WIDTH = POOL_GROUPS * HEAD_DIM
CHUNK = 128
ROPE_BASE = 10000.0
EPS = 1e-6
ADAM_LR = 0.001
ADAM_B1 = 0.9
ADAM_B2 = 0.999
ADAM_EPS = 1e-08
ADAM_WD = 0.01
ADAM_STEP = 10

N_CHIPS = 4
LANES = 128
BF16_ROWS = 16
V7X_VMEM_LIMIT = 52 * 1024 * 1024
MESH = pl.DeviceIdType.MESH
ANY = pl.BlockSpec(memory_space=pl.ANY)


def _round_up(n, m):
    return -(-n // m) * m


def _pick_tile(n, target, mult):
    best = None
    for d in range(mult, min(n, target) + 1, mult):
        if n % d == 0:
            best = d
    assert best is not None, (n, target, mult)
    return best


def _params(sem=None):
    return pltpu.CompilerParams(dimension_semantics=sem, vmem_limit_bytes=V7X_VMEM_LIMIT)


def _dot(a, b):
    return jnp.dot(a, b, preferred_element_type=F32)


def _dot_nt(a, b):
    return lax.dot_general(a, b, (((1,), (1,)), ((), ())), preferred_element_type=F32)


def _dot_tn(a, b):
    return lax.dot_general(a, b, (((0,), (0,)), ((), ())), preferred_element_type=F32)


def _ein(spec, a, b):
    return jnp.einsum(spec, a, b, preferred_element_type=F32)


def _sigmoid(x):
    return jax.nn.sigmoid(x)


def _rms_fwd(x, gain):
    r = lax.rsqrt(jnp.mean(x * x, axis=-1, keepdims=True) + EPS)
    return x * r * gain


def _rms_bwd(x, gain, da):
    r = lax.rsqrt(jnp.mean(x * x, axis=-1, keepdims=True) + EPS)
    xh = x * r
    dgain = jnp.sum(da * xh, axis=0, keepdims=True)
    dxh = da * gain
    dx = r * (dxh - xh * jnp.mean(dxh * xh, axis=-1, keepdims=True))
    return dx, dgain


def _row_mask(t, tm, pad, shape):
    rows = t * tm + lax.broadcasted_iota(jnp.int32, shape, 0)
    return rows >= pad


def _ffn_fwd(h, gain, wg, wu, wd, layer, tm, name):
    T, D = h.shape
    Fs = wg.shape[-1]
    F = N_CHIPS * Fs

    def body(h_ref, g_ref, wg_ref, wu_ref, wd_ref, ho_ref, a_ref, go_ref, uo_ref, act_ref, acc_ref):
        s = pl.program_id(1)

        @pl.when(s == 0)
        def _():
            a_ref[...] = _rms_fwd(h_ref[...], g_ref[...]).astype(BF16)
            acc_ref[...] = jnp.zeros_like(acc_ref)

        a = a_ref[...]
        g = _dot(a, wg_ref[...])
        u = _dot(a, wu_ref[...])
        act = (g * _sigmoid(g) * u).astype(BF16)
        go_ref[...] = g.astype(BF16)
        uo_ref[...] = u.astype(BF16)
        act_ref[...] = act
        acc_ref[...] += _dot(act, wd_ref[...])

        @pl.when(s == N_CHIPS - 1)
        def _():
            ho_ref[...] = h_ref[...] + 0.5 * acc_ref[...]

    row = pl.BlockSpec((tm, D), lambda t, s: (t, 0))
    col = pl.BlockSpec((tm, Fs), lambda t, s: (t, s))
    return pl.pallas_call(
        body,
        name=name,
        grid=(T // tm, N_CHIPS),
        in_specs=[
            row,
            pl.BlockSpec((None, 1, D), lambda t, s: (layer, 0, 0)),
            pl.BlockSpec((None, None, D, Fs), lambda t, s: (layer, s, 0, 0)),
            pl.BlockSpec((None, None, D, Fs), lambda t, s: (layer, s, 0, 0)),
            pl.BlockSpec((None, None, Fs, D), lambda t, s: (layer, s, 0, 0)),
        ],
        out_specs=[row, row, col, col, col],
        out_shape=[
            jax.ShapeDtypeStruct((T, D), F32),
            jax.ShapeDtypeStruct((T, D), BF16),
            jax.ShapeDtypeStruct((T, F), BF16),
            jax.ShapeDtypeStruct((T, F), BF16),
            jax.ShapeDtypeStruct((T, F), BF16),
        ],
        scratch_shapes=[pltpu.VMEM((tm, D), F32)],
        compiler_params=_params(("parallel", "arbitrary")),
    )(h, gain, wg, wu, wd)


def _inproj_fwd(h, gain, win, layer, tm, name):
    T, D = h.shape
    Ns = win.shape[-1]

    def body(h_ref, g_ref, w_ref, z_ref, b_ref):
        @pl.when(pl.program_id(1) == 0)
        def _():
            b_ref[...] = _rms_fwd(h_ref[...], g_ref[...]).astype(BF16)

        z_ref[...] = _dot(b_ref[...], w_ref[...])

    return pl.pallas_call(
        body,
        name=name,
        grid=(T // tm, N_CHIPS),
        in_specs=[
            pl.BlockSpec((tm, D), lambda t, s: (t, 0)),
            pl.BlockSpec((None, 1, D), lambda t, s: (layer, 0, 0)),
            pl.BlockSpec((None, None, D, Ns), lambda t, s: (layer, s, 0, 0)),
        ],
        out_specs=[
            pl.BlockSpec((tm, Ns), lambda t, s: (t, s)),
            pl.BlockSpec((tm, D), lambda t, s: (t, 0)),
        ],
        out_shape=[
            jax.ShapeDtypeStruct((T, N_CHIPS * Ns), F32),
            jax.ShapeDtypeStruct((T, D), BF16),
        ],
        compiler_params=_params(("parallel", "arbitrary")),
    )(h, gain, win)


def _ret_consts(T, pad):
    half = HEAD_DIM // 2
    inv_freq = ROPE_BASE ** (-jnp.arange(half, dtype=F32) / half)
    pos = jnp.arange(T, dtype=F32) - pad
    ang = pos[:, None] * inv_freq[None, :]
    cos = jnp.cos(ang)
    sin = jnp.sin(ang)
    cosf = jnp.concatenate([cos, cos], axis=1)
    sinf = jnp.concatenate([-sin, sin], axis=1)
    log_gamma = jnp.log1p(-(2.0 ** (-5.0 - jnp.arange(RET_HEADS, dtype=F32))))
    idx = jnp.arange(CHUNK, dtype=F32)
    diff = idx[:, None] - idx[None, :]
    intra = jnp.where(diff[None] >= 0, jnp.exp(diff[None] * log_gamma[:, None, None]), 0.0)
    k_decay = jnp.exp((CHUNK - 1.0 - idx)[None, :] * log_gamma[:, None])
    q_decay = jnp.exp((idx + 1.0)[None, :] * log_gamma[:, None])
    chunk_decay = jnp.exp(CHUNK * log_gamma)
    kdec = jnp.broadcast_to(k_decay[:, :, None], (RET_HEADS, CHUNK, HEAD_DIM))
    qdec = jnp.broadcast_to(q_decay[:, :, None], (RET_HEADS, CHUNK, HEAD_DIM))
    cdb = jnp.broadcast_to(chunk_decay[:, None, None], (RET_HEADS, 8, HEAD_DIM))
    return cosf, sinf, intra, kdec, qdec, cdb


def _rot(t, cosv, sinv):
    return t * cosv + pltpu.roll(t, HEAD_DIM // 2, 1) * sinv


def _rot_t(g, cosv, sinv):
    return g * cosv + pltpu.roll(g * sinv, HEAD_DIM // 2, 1)


def _head_specs(tg, section, order):
    return pl.BlockSpec((tg, HEAD_DIM), lambda h, g: (order(g), section * RET_HEADS + h))


def _ret_fwd(z, consts, cg, name):
    T = z.shape[0]
    N = T // CHUNK
    ng = N // cg
    tg = cg * CHUNK
    cosf, sinf, intra, kdec, qdec, cdb = consts
    fwd = lambda g: g

    def body(zq, zk, zv, zg, cos_ref, sin_ref, m_ref, kd_ref, qd_ref, cd_ref, r_ref, o_ref, s_ref, st_ref):
        @pl.when(pl.program_id(1) == 0)
        def _():
            st_ref[...] = jnp.zeros_like(st_ref)

        cosv = cos_ref[...]
        sinv = sin_ref[...]
        q3 = (_rot(zq[...], cosv, sinv) * (HEAD_DIM ** -0.5)).reshape(cg, CHUNK, HEAD_DIM)
        k3 = _rot(zk[...], cosv, sinv).reshape(cg, CHUNK, HEAD_DIM)
        vb = zv[...].reshape(cg, CHUNK, HEAD_DIM).astype(BF16)
        scores = _ein("ncd,nmd->ncm", q3.astype(BF16), k3.astype(BF16)) * m_ref[...][None]
        inner = _ein("ncm,nmd->ncd", scores.astype(BF16), vb)
        kv = _ein("ncd,nce->nde", (k3 * kd_ref[...][None]).astype(BF16), vb)
        cd = cd_ref[0:1, :]
        state = st_ref[...]
        for n in range(cg):
            s_ref[n] = state
            state = state * cd + kv[n]
        st_ref[...] = state
        qdb = (q3 * qd_ref[...][None]).astype(BF16)
        cross = _ein("ncd,nde->nce", qdb, s_ref[...].astype(BF16))
        out = (inner + cross).reshape(tg, HEAD_DIM)
        o_ref[...] = out
        xc = out - jnp.mean(out, axis=-1, keepdims=True)
        rn = xc * lax.rsqrt(jnp.mean(xc * xc, axis=-1, keepdims=True) + EPS)
        g = zg[...]
        r_ref[...] = (rn * (g * _sigmoid(g))).astype(BF16)

    tab = pl.BlockSpec((tg, HEAD_DIM), lambda h, g: (g, 0))
    per_head = lambda rows: pl.BlockSpec((None, rows, HEAD_DIM), lambda h, g: (h, 0, 0))
    head_out = pl.BlockSpec((tg, HEAD_DIM), lambda h, g: (g, h))
    return pl.pallas_call(
        body,
        name=name,
        grid=(RET_HEADS, ng),
        in_specs=[_head_specs(tg, i, fwd) for i in range(4)]
        + [tab, tab, per_head(CHUNK), per_head(CHUNK), per_head(CHUNK), per_head(8)],
        out_specs=[
            head_out,
            head_out,
            pl.BlockSpec((None, cg, HEAD_DIM, HEAD_DIM), lambda h, g: (h, g, 0, 0)),
        ],
        out_shape=[
            jax.ShapeDtypeStruct((T, RET_WIDTH), BF16),
            jax.ShapeDtypeStruct((T, RET_WIDTH), F32),
            jax.ShapeDtypeStruct((RET_HEADS, N, HEAD_DIM, HEAD_DIM), F32),
        ],
        scratch_shapes=[pltpu.VMEM((HEAD_DIM, HEAD_DIM), F32)],
        compiler_params=_params(("parallel", "arbitrary")),
    )(z, z, z, z, cosf, sinf, intra, kdec, qdec, cdb)


def _window_sums(u, shift_of):
    sums = []
    s = u
    k = 1
    while k < POOL_WINDOWS[-1]:
        s = s + pltpu.roll(s, shift_of(k), 0)
        sums.append(s)
        k *= 2
    return sums


def _select_group(vals, g):
    out = vals[-1]
    for i in range(len(vals) - 2, -1, -1):
        out = jnp.where(g == i, vals[i], out)
    return out


def _pool_parts(u, g, T, pad):
    rows = lax.broadcasted_iota(jnp.int32, (T, HEAD_DIM), 0)
    valid = rows >= pad
    win = _select_group([float(w) for w in POOL_WINDOWS], g)
    div = jnp.clip((rows - pad + 1).astype(F32), 1.0, win)
    s = _select_group(_window_sums(u, lambda k: k), g)
    pooled = jnp.where(valid, s / div - u, 0.0)
    return pooled, div, valid


def _pool_specs(T, layer):
    first = 4 * RET_WIDTH // HEAD_DIM
    return [
        pl.BlockSpec((T, HEAD_DIM), lambda g: (0, first + g)),
        pl.BlockSpec((None, None, HEAD_DIM, HEAD_DIM), lambda g: (layer, g, 0, 0)),
        pl.BlockSpec((None, 1, HEAD_DIM), lambda g: (layer, 0, g)),
    ]


def _pool_fwd(z, maps, scale, layer, pad, name):
    T = z.shape[0]
    assert pad >= POOL_WINDOWS[-1], "window rolls wrap into the zero rows in front"

    def body(zu, maps_ref, sc_ref, pm_ref):
        g = pl.program_id(0)
        pooled, _, _ = _pool_parts(zu[...], g, T, pad)
        y = _dot(pooled.astype(BF16), maps_ref[...].astype(BF16))
        pm_ref[...] = (y * sc_ref[...]).astype(BF16)

    return pl.pallas_call(
        body,
        name=name,
        grid=(POOL_GROUPS,),
        in_specs=_pool_specs(T, layer),
        out_specs=pl.BlockSpec((T, HEAD_DIM), lambda g: (0, g)),
        out_shape=jax.ShapeDtypeStruct((T, POOL_WIDTH), BF16),
        compiler_params=_params(("parallel",)),
    )(z, maps, scale)


def _gate_specs(tm, D):
    nb = D // RET_WIDTH
    first = (4 * RET_WIDTH + POOL_WIDTH) // RET_WIDTH
    return [pl.BlockSpec((tm, RET_WIDTH), functools.partial(lambda t, j: (t, j), j=first + j)) for j in range(2 * nb)]


def _load_gates(refs, nb):
    ga = jnp.concatenate([r[...] for r in refs[:nb]], axis=1) if nb > 1 else refs[0][...]
    gb = jnp.concatenate([r[...] for r in refs[nb:]], axis=1) if nb > 1 else refs[nb][...]
    return ga, gb


def _mix_fwd(h, r, pm, z, wru, wpu, wout, layer, tm, name):
    T, D = h.shape
    Dq = D // N_CHIPS
    nb = D // RET_WIDTH

    def body(*refs):
        h_ref, r_ref, pm_ref = refs[:3]
        gate_refs = refs[3:3 + 2 * nb]
        wru_ref, wpu_ref, wout_ref, ho_ref, mx_ref, ret_ref, pool_ref = refs[3 + 2 * nb:]
        rv = r_ref[...]
        pv = pm_ref[...]
        ret = jnp.concatenate([_dot(rv, wru_ref[s]) for s in range(N_CHIPS)], axis=1)
        pool = jnp.concatenate([_dot(pv, wpu_ref[s]) for s in range(N_CHIPS)], axis=1)
        ga, gb = _load_gates(gate_refs, nb)
        mixed = (_sigmoid(ga) * ret + _sigmoid(gb) * pool).astype(BF16)
        mx_ref[...] = mixed
        ret_ref[...] = ret.astype(BF16)
        pool_ref[...] = pool.astype(BF16)
        ho_ref[...] = h_ref[...] + _dot(mixed, wout_ref[...].reshape(D, D))

    row = pl.BlockSpec((tm, D), lambda t: (t, 0))
    half = pl.BlockSpec((tm, RET_WIDTH), lambda t: (t, 0))
    up = pl.BlockSpec((None, N_CHIPS, RET_WIDTH, Dq), lambda t: (layer, 0, 0, 0))
    return pl.pallas_call(
        body,
        name=name,
        grid=(T // tm,),
        in_specs=[row, half, half] + _gate_specs(tm, D)
        + [up, up, pl.BlockSpec((None, N_CHIPS, Dq, D), lambda t: (layer, 0, 0, 0))],
        out_specs=[row, row, row, row],
        out_shape=[jax.ShapeDtypeStruct((T, D), F32)] + [jax.ShapeDtypeStruct((T, D), BF16)] * 3,
        compiler_params=_params(("parallel",)),
    )(h, r, pm, *([z] * (2 * nb)), wru, wpu, wout)


def _final_loss(h, gain, tgt, name):
    T, D = h.shape
    first = (T - tgt.shape[0]) // CHUNK

    def body(h_ref, g_ref, t_ref, dh_ref, loss_ref, dg_ref):
        i = pl.program_id(0)

        @pl.when(i == 0)
        def _():
            loss_ref[...] = jnp.zeros_like(loss_ref)
            dg_ref[...] = jnp.zeros_like(dg_ref)

        x = h_ref[...]
        gain_v = g_ref[...]
        err = jnp.where(i >= first, _rms_fwd(x, gain_v) - t_ref[...], 0.0)
        loss_ref[...] += 0.5 * jnp.sum(jnp.mean(err * err, axis=-1))
        dx, dgain = _rms_bwd(x, gain_v, err * (1.0 / D))
        dg_ref[...] += dgain
        dh_ref[...] = dx

    return pl.pallas_call(
        body,
        name=name,
        grid=(T // CHUNK,),
        in_specs=[
            pl.BlockSpec((CHUNK, D), lambda i: (i, 0)),
            pl.BlockSpec((1, D), lambda i: (0, 0)),
            pl.BlockSpec((CHUNK, D), lambda i: (jnp.maximum(i - first, 0), 0)),
        ],
        out_specs=[
            pl.BlockSpec((CHUNK, D), lambda i: (i, 0)),
            pl.BlockSpec((1, LANES), lambda i: (0, 0)),
            pl.BlockSpec((1, D), lambda i: (0, 0)),
        ],
        out_shape=[
            jax.ShapeDtypeStruct((T, D), F32),
            jax.ShapeDtypeStruct((1, LANES), F32),
            jax.ShapeDtypeStruct((1, D), F32),
        ],
        compiler_params=_params(("arbitrary",)),
    )(h, gain, tgt)


def _ffn_bwd_dx(dy, h, gain, g, u, wg, wu, wd, layer, tm, pad, name):
    T, D = h.shape
    Fs = wg.shape[-1]
    F = N_CHIPS * Fs

    def body(dy_ref, h_ref, g_ref, go_ref, uo_ref, wg_ref, wu_ref, wd_ref,
             dh_ref, dg_ref, du_ref, dgain_ref, dyh_ref, da_ref):
        t = pl.program_id(0)
        s = pl.program_id(1)

        @pl.when((t == 0) & (s == 0))
        def _():
            dgain_ref[...] = jnp.zeros_like(dgain_ref)

        @pl.when(s == 0)
        def _():
            dyh_ref[...] = (0.5 * dy_ref[...]).astype(BF16)
            da_ref[...] = jnp.zeros_like(da_ref)

        dact = _dot_nt(dyh_ref[...], wd_ref[...])
        gf = go_ref[...].astype(F32)
        uf = uo_ref[...].astype(F32)
        sg = _sigmoid(gf)
        du = (dact * (gf * sg)).astype(BF16)
        dg = (dact * uf * (sg * (1.0 + gf * (1.0 - sg)))).astype(BF16)
        dg_ref[...] = dg
        du_ref[...] = du
        da_ref[...] += _dot_nt(dg, wg_ref[...]) + _dot_nt(du, wu_ref[...])

        @pl.when(s == N_CHIPS - 1)
        def _():
            dx, dgain = _rms_bwd(h_ref[...], g_ref[...], da_ref[...])
            dgain_ref[...] += dgain
            dh_ref[...] = jnp.where(_row_mask(t, tm, pad, (tm, D)), dy_ref[...] + dx, 0.0)

    row = pl.BlockSpec((tm, D), lambda t, s: (t, 0))
    col = pl.BlockSpec((tm, Fs), lambda t, s: (t, s))
    wcol = pl.BlockSpec((None, None, D, Fs), lambda t, s: (layer, s, 0, 0))
    return pl.pallas_call(
        body,
        name=name,
        grid=(T // tm, N_CHIPS),
        in_specs=[row, row, pl.BlockSpec((None, 1, D), lambda t, s: (layer, 0, 0)), col, col, wcol, wcol,
                  pl.BlockSpec((None, None, Fs, D), lambda t, s: (layer, s, 0, 0))],
        out_specs=[row, col, col, pl.BlockSpec((1, D), lambda t, s: (0, 0))],
        out_shape=[
            jax.ShapeDtypeStruct((T, D), F32),
            jax.ShapeDtypeStruct((T, F), BF16),
            jax.ShapeDtypeStruct((T, F), BF16),
            jax.ShapeDtypeStruct((1, D), F32),
        ],
        scratch_shapes=[pltpu.VMEM((tm, D), BF16), pltpu.VMEM((tm, D), F32)],
        compiler_params=_params(("arbitrary", "arbitrary")),
    )(dy, h, gain, g, u, wg, wu, wd)


def _grad_tn(a, b, prev, layer, n_layers, mode, scale, tm, name):
    T = a.shape[0]
    if mode == "col":
        R, C = a.shape[1], b.shape[1] // N_CHIPS
        a_spec = pl.BlockSpec((tm, R), lambda s, t: (t, 0))
        b_spec = pl.BlockSpec((tm, C), lambda s, t: (t, s))
    else:
        R, C = a.shape[1] // N_CHIPS, b.shape[1]
        a_spec = pl.BlockSpec((tm, R), lambda s, t: (t, s))
        b_spec = pl.BlockSpec((tm, C), lambda s, t: (t, 0))
    nt = T // tm

    def body(*refs):
        a_ref, b_ref = refs[:2]
        o_ref, acc_ref = refs[-2:]
        t = pl.program_id(1)

        @pl.when(t == 0)
        def _():
            acc_ref[...] = jnp.zeros_like(acc_ref)

        acc_ref[...] += _dot_tn(a_ref[...].astype(BF16), b_ref[...].astype(BF16))

        @pl.when(t == nt - 1)
        def _():
            o_ref[...] = (scale * acc_ref[...]).astype(BF16)

    operands = [a, b] + ([] if prev is None else [prev])
    return pl.pallas_call(
        body,
        name=name,
        grid=(N_CHIPS, nt),
        in_specs=[a_spec, b_spec] + ([] if prev is None else [ANY]),
        out_specs=pl.BlockSpec((None, None, R, C), lambda s, t: (layer, s, 0, 0)),
        out_shape=jax.ShapeDtypeStruct((n_layers, N_CHIPS, R, C), BF16),
        scratch_shapes=[pltpu.VMEM((R, C), F32)],
        input_output_aliases={} if prev is None else {2: 0},
        compiler_params=_params(("parallel", "arbitrary")),
    )(*operands)


def _mix_bwd_dx(dh, z, ret, pool, wout, wru, wpu, layer, tm, name):
    T, D = dh.shape
    Dq = D // N_CHIPS
    nb = D // RET_WIDTH

    def body(*refs):
        dh_ref = refs[0]
        gate_refs = refs[1:1 + 2 * nb]
        ret_ref, pool_ref, wout_ref, wru_ref, wpu_ref, dgab_ref, dret_ref, dpool_ref, dr_ref, dpm_ref = refs[1 + 2 * nb:]
        dmixed = _dot_nt(dh_ref[...].astype(BF16), wout_ref[...].reshape(D, D))
        ga, gb = _load_gates(gate_refs, nb)
        sa = _sigmoid(ga)
        sb = _sigmoid(gb)
        dgab_ref[:, :D] = (dmixed * ret_ref[...].astype(F32) * (sa * (1.0 - sa))).astype(BF16)
        dgab_ref[:, D:] = (dmixed * pool_ref[...].astype(F32) * (sb * (1.0 - sb))).astype(BF16)
        dret = (dmixed * sa).astype(BF16)
        dpool = (dmixed * sb).astype(BF16)
        dret_ref[...] = dret
        dpool_ref[...] = dpool
        dr = _dot_nt(dret[:, :Dq], wru_ref[0])
        dpm = _dot_nt(dpool[:, :Dq], wpu_ref[0])
        for s in range(1, N_CHIPS):
            dr += _dot_nt(dret[:, s * Dq:(s + 1) * Dq], wru_ref[s])
            dpm += _dot_nt(dpool[:, s * Dq:(s + 1) * Dq], wpu_ref[s])
        dr_ref[...] = dr
        dpm_ref[...] = dpm

    row = pl.BlockSpec((tm, D), lambda t: (t, 0))
    half = pl.BlockSpec((tm, RET_WIDTH), lambda t: (t, 0))
    up = pl.BlockSpec((None, N_CHIPS, RET_WIDTH, Dq), lambda t: (layer, 0, 0, 0))
    return pl.pallas_call(
        body,
        name=name,
        grid=(T // tm,),
        in_specs=[row] + _gate_specs(tm, D)
        + [row, row, pl.BlockSpec((None, N_CHIPS, Dq, D), lambda t: (layer, 0, 0, 0)), up, up],
        out_specs=[pl.BlockSpec((tm, 2 * D), lambda t: (t, 0)), row, row, half, half],
        out_shape=[
            jax.ShapeDtypeStruct((T, 2 * D), BF16),
            jax.ShapeDtypeStruct((T, D), BF16),
            jax.ShapeDtypeStruct((T, D), BF16),
            jax.ShapeDtypeStruct((T, RET_WIDTH), F32),
            jax.ShapeDtypeStruct((T, POOL_WIDTH), F32),
        ],
        compiler_params=_params(("parallel",)),
    )(dh, *([z] * (2 * nb)), ret, pool, wout, wru, wpu)


def _pool_bwd(z, dpm, maps, scale, layer, pad, name):
    T = z.shape[0]

    def body(zu, maps_ref, sc_ref, dpm_ref, du_ref, dmaps_ref, dsc_ref):
        g = pl.program_id(0)
        u = zu[...]
        pooled, div, valid = _pool_parts(u, g, T, pad)
        pb = pooled.astype(BF16)
        mb = maps_ref[...].astype(BF16)
        dp = dpm_ref[...]
        dsc_ref[...] = jnp.sum(dp * _dot(pb, mb), axis=0, keepdims=True)
        dyb = (dp * sc_ref[...]).astype(BF16)
        dmaps_ref[...] = _dot_tn(pb, dyb)
        dpooled = jnp.where(valid, _dot_nt(dyb, mb), 0.0)
        ahead = _select_group(_window_sums(dpooled / div, lambda k: T - k), g)
        du_ref[...] = jnp.where(valid, ahead - dpooled, 0.0).astype(BF16)

    blk = pl.BlockSpec((T, HEAD_DIM), lambda g: (0, g))
    return pl.pallas_call(
        body,
        name=name,
        grid=(POOL_GROUPS,),
        in_specs=_pool_specs(T, layer) + [blk],
        out_specs=[
            blk,
            pl.BlockSpec((None, HEAD_DIM, HEAD_DIM), lambda g: (g, 0, 0)),
            pl.BlockSpec((1, HEAD_DIM), lambda g: (0, g)),
        ],
        out_shape=[
            jax.ShapeDtypeStruct((T, POOL_WIDTH), BF16),
            jax.ShapeDtypeStruct((POOL_GROUPS, HEAD_DIM, HEAD_DIM), F32),
            jax.ShapeDtypeStruct((1, POOL_WIDTH), F32),
        ],
        compiler_params=_params(("parallel",)),
    )(z, maps, scale, dpm)


def _ret_bwd_local(z, o_pre, s_all, dr, consts, cg, name):
    T = z.shape[0]
    N = T // CHUNK
    ng = N // cg
    tg = cg * CHUNK
    cosf, sinf, intra, _, qdec, _ = consts
    fwd = lambda g: g

    def body(zq, zk, zv, zg, o_ref, s_ref, dr_ref, cos_ref, sin_ref, m_ref, qd_ref,
             dq_ref, dg_ref, dk_ref, dv_ref, ds_ref):
        cosv = cos_ref[...]
        sinv = sin_ref[...]
        scale = HEAD_DIM ** -0.5
        q3 = (_rot(zq[...], cosv, sinv) * scale).reshape(cg, CHUNK, HEAD_DIM)
        k3 = _rot(zk[...], cosv, sinv).reshape(cg, CHUNK, HEAD_DIM)
        qb = q3.astype(BF16)
        kb = k3.astype(BF16)
        vb = zv[...].reshape(cg, CHUNK, HEAD_DIM).astype(BF16)
        mask = m_ref[...][None]
        sb = (_ein("ncd,nmd->ncm", qb, kb) * mask).astype(BF16)
        qdv = qd_ref[...][None]
        qdb = (q3 * qdv).astype(BF16)

        out = o_ref[...]
        xc = out - jnp.mean(out, axis=-1, keepdims=True)
        rstd = lax.rsqrt(jnp.mean(xc * xc, axis=-1, keepdims=True) + EPS)
        rn = xc * rstd
        g = zg[...]
        sg = _sigmoid(g)
        drv = dr_ref[...]
        dg_ref[...] = (drv * rn * (sg * (1.0 + g * (1.0 - sg)))).astype(BF16)
        drn = drv * (g * sg)
        dout = rstd * (drn - jnp.mean(drn, axis=-1, keepdims=True)
                       - rn * jnp.mean(drn * rn, axis=-1, keepdims=True))
        dob = dout.reshape(cg, CHUNK, HEAD_DIM).astype(BF16)

        dsb = (_ein("ncd,nmd->ncm", dob, vb) * mask).astype(BF16)
        dv_ref[...] = _ein("ncm,ncd->nmd", sb, dob).reshape(tg, HEAD_DIM)
        dk_ref[...] = _ein("ncm,ncd->nmd", dsb, qb).reshape(tg, HEAD_DIM)
        dq3 = _ein("ncm,nmd->ncd", dsb, kb) + _ein("nce,nde->ncd", dob, s_ref[...].astype(BF16)) * qdv
        dq_ref[...] = _rot_t(dq3.reshape(tg, HEAD_DIM) * scale, cosv, sinv).astype(BF16)
        ds_ref[...] = _ein("ncd,nce->nde", qdb, dob)

    tab = pl.BlockSpec((tg, HEAD_DIM), lambda h, g: (g, 0))
    per_head = pl.BlockSpec((None, CHUNK, HEAD_DIM), lambda h, g: (h, 0, 0))
    head_blk = pl.BlockSpec((tg, HEAD_DIM), lambda h, g: (g, h))
    state_blk = pl.BlockSpec((None, cg, HEAD_DIM, HEAD_DIM), lambda h, g: (h, g, 0, 0))
    return pl.pallas_call(
        body,
        name=name,
        grid=(RET_HEADS, ng),
        in_specs=[_head_specs(tg, i, fwd) for i in range(4)]
        + [head_blk, state_blk, head_blk, tab, tab, per_head, per_head],
        out_specs=[head_blk, head_blk, head_blk, head_blk, state_blk],
        out_shape=[
            jax.ShapeDtypeStruct((T, RET_WIDTH), BF16),
            jax.ShapeDtypeStruct((T, RET_WIDTH), BF16),
            jax.ShapeDtypeStruct((T, RET_WIDTH), F32),
            jax.ShapeDtypeStruct((T, RET_WIDTH), F32),
            jax.ShapeDtypeStruct((RET_HEADS, N, HEAD_DIM, HEAD_DIM), F32),
        ],
        compiler_params=_params(("parallel", "parallel")),
    )(z, z, z, z, o_pre, s_all, dr, cosf, sinf, intra, qdec)


def _ret_bwd_state(z, dkp, dvp, ds, consts, cg, name):
    T = z.shape[0]
    N = T // CHUNK
    ng = N // cg
    tg = cg * CHUNK
    cosf, sinf, _, kdec, _, cdb = consts
    rev = lambda g: ng - 1 - g

    def body(zk, zv, dkp_ref, dvp_ref, ds_ref, cos_ref, sin_ref, kd_ref, cd_ref, dk_ref, dv_ref, gs_ref, dkv_ref):
        @pl.when(pl.program_id(1) == 0)
        def _():
            gs_ref[...] = jnp.zeros_like(gs_ref)

        cosv = cos_ref[...]
        sinv = sin_ref[...]
        cd = cd_ref[0:1, :]
        grad = gs_ref[...]
        for n in reversed(range(cg)):
            dkv_ref[n] = grad
            grad = ds_ref[n] + cd * grad
        gs_ref[...] = grad
        dkvb = dkv_ref[...].astype(BF16)
        kdv = kd_ref[...][None]
        k3 = _rot(zk[...], cosv, sinv).reshape(cg, CHUNK, HEAD_DIM)
        vb = zv[...].reshape(cg, CHUNK, HEAD_DIM).astype(BF16)
        dk3 = _ein("nce,nde->ncd", vb, dkvb) * kdv
        dv3 = _ein("ncd,nde->nce", (k3 * kdv).astype(BF16), dkvb)
        dk_ref[...] = _rot_t(dkp_ref[...] + dk3.reshape(tg, HEAD_DIM), cosv, sinv).astype(BF16)
        dv_ref[...] = (dvp_ref[...] + dv3.reshape(tg, HEAD_DIM)).astype(BF16)

    tab = pl.BlockSpec((tg, HEAD_DIM), lambda h, g: (rev(g), 0))
    head_blk = pl.BlockSpec((tg, HEAD_DIM), lambda h, g: (rev(g), h))
    return pl.pallas_call(
        body,
        name=name,
        grid=(RET_HEADS, ng),
        in_specs=[_head_specs(tg, 1, rev), _head_specs(tg, 2, rev), head_blk, head_blk,
                  pl.BlockSpec((None, cg, HEAD_DIM, HEAD_DIM), lambda h, g: (h, rev(g), 0, 0)),
                  tab, tab,
                  pl.BlockSpec((None, CHUNK, HEAD_DIM), lambda h, g: (h, 0, 0)),
                  pl.BlockSpec((None, 8, HEAD_DIM), lambda h, g: (h, 0, 0))],
        out_specs=[head_blk, head_blk],
        out_shape=[jax.ShapeDtypeStruct((T, RET_WIDTH), BF16)] * 2,
        scratch_shapes=[pltpu.VMEM((HEAD_DIM, HEAD_DIM), F32), pltpu.VMEM((cg, HEAD_DIM, HEAD_DIM), F32)],
        compiler_params=_params(("parallel", "arbitrary")),
    )(z, z, dkp, dvp, ds, cosf, sinf, kdec, cdb)


def _inproj_bwd_dx(dz, win, h, gain, dh_in, layer, tm, pad, name):
    T, D = h.shape
    Ns = win.shape[-1]

    def body(dz_ref, w_ref, h_ref, g_ref, dhi_ref, dh_ref, dgain_ref, db_ref):
        t = pl.program_id(0)
        s = pl.program_id(1)

        @pl.when((t == 0) & (s == 0))
        def _():
            dgain_ref[...] = jnp.zeros_like(dgain_ref)

        @pl.when(s == 0)
        def _():
            db_ref[...] = jnp.zeros_like(db_ref)

        db_ref[...] += _dot_nt(dz_ref[...], w_ref[...])

        @pl.when(s == N_CHIPS - 1)
        def _():
            dx, dgain = _rms_bwd(h_ref[...], g_ref[...], db_ref[...])
            dgain_ref[...] += dgain
            dh_ref[...] = jnp.where(_row_mask(t, tm, pad, (tm, D)), dhi_ref[...] + dx, 0.0)

    row = pl.BlockSpec((tm, D), lambda t, s: (t, 0))
    return pl.pallas_call(
        body,
        name=name,
        grid=(T // tm, N_CHIPS),
        in_specs=[
            pl.BlockSpec((tm, Ns), lambda t, s: (t, s)),
            pl.BlockSpec((None, None, D, Ns), lambda t, s: (layer, s, 0, 0)),
            row,
            pl.BlockSpec((None, 1, D), lambda t, s: (layer, 0, 0)),
            row,
        ],
        out_specs=[row, pl.BlockSpec((1, D), lambda t, s: (0, 0))],
        out_shape=[jax.ShapeDtypeStruct((T, D), F32), jax.ShapeDtypeStruct((1, D), F32)],
        scratch_shapes=[pltpu.VMEM((tm, D), F32)],
        compiler_params=_params(("arbitrary", "arbitrary")),
    )(dz, win, h, gain, dh_in)


def _mesh_pos():
    x, y, c = lax.axis_index("x"), lax.axis_index("y"), lax.axis_index("c")
    others = [(1 - x, y), (x, 1 - y), (1 - x, 1 - y)]
    return x, y, c, 2 * x + y, others


def _half_rows(c, rh):
    return pl.ds(pl.multiple_of(c * rh, rh), rh)


def _remote(src, dst, ssem, rsem, dev):
    return pltpu.make_async_remote_copy(src_ref=src, dst_ref=dst, send_sem=ssem, recv_sem=rsem,
                                        device_id=dev, device_id_type=MESH)


def _all_gather(shards):
    n = len(shards)
    per = 7

    def body(*refs):
        ins, outs = refs[:n], refs[n:2 * n]
        ssem, rsem = refs[2 * n:]
        x, y, c, chip, others = _mesh_pos()
        sibling = (x, y, 1 - c)
        sends = []
        for i in range(n):
            rh = ins[i].shape[1] // 2
            mine = _half_rows(c, rh)
            for j, (ox, oy) in enumerate(others):
                cp = _remote(ins[i].at[:, mine, :], outs[i].at[:, chip, mine, :],
                             ssem.at[per * i + j], rsem.at[per * i + j], (ox, oy, c))
                cp.start()
                sends.append(cp)
            own = _remote(ins[i], outs[i].at[:, chip], ssem.at[per * i + 6], rsem.at[per * i + 6], sibling)
            own.start()
            sends.append(own)
        for i in range(n):
            rh = ins[i].shape[1] // 2
            mine = _half_rows(c, rh)
            for j, (ox, oy) in enumerate(others):
                rows = outs[i].at[:, 2 * ox + oy, mine, :]
                _remote(rows, rows, ssem.at[per * i + j], rsem.at[per * i + j], (ox, oy, c)).wait_recv()
                fwd = _remote(rows, rows, ssem.at[per * i + 3 + j], rsem.at[per * i + 3 + j], sibling)
                fwd.start()
                sends.append(fwd)
        for i in range(n):
            rh = ins[i].shape[1] // 2
            theirs = _half_rows(1 - c, rh)
            for j, (ox, oy) in enumerate(others):
                rows = outs[i].at[:, 2 * ox + oy, theirs, :]
                _remote(rows, rows, ssem.at[per * i + 3 + j], rsem.at[per * i + 3 + j], sibling).wait_recv()
            own = outs[i].at[:, chip]
            _remote(own, own, ssem.at[per * i + 6], rsem.at[per * i + 6], sibling).wait_recv()
        for cp in sends:
            cp.wait_send()

    return pl.pallas_call(
        body,
        name="weights_all_gather",
        in_specs=[ANY] * n,
        out_specs=[ANY] * n,
        out_shape=[jax.ShapeDtypeStruct((s.shape[0], N_CHIPS) + s.shape[1:], s.dtype) for s in shards],
        scratch_shapes=[pltpu.SemaphoreType.DMA((per * n,)), pltpu.SemaphoreType.DMA((per * n,))],
    )(*shards)


def _pair_exchange(gs):
    n = len(gs)

    def body(*refs):
        ins, outs = refs[:n], refs[n:2 * n]
        ssem, rsem = refs[2 * n:]
        x, y, c, _, _ = _mesh_pos()
        copies = []
        for i in range(n):
            rh = ins[i].shape[2] // 2
            cp = _remote(ins[i].at[:, :, _half_rows(1 - c, rh), :], outs[i], ssem.at[i], rsem.at[i], (x, y, 1 - c))
            cp.start()
            copies.append(cp)
        for cp in copies:
            cp.wait()

    return pl.pallas_call(
        body,
        name="grads_pair_exchange",
        in_specs=[ANY] * n,
        out_specs=[ANY] * n,
        out_shape=[jax.ShapeDtypeStruct(g.shape[:2] + (g.shape[2] // 2, g.shape[3]), g.dtype) for g in gs],
        scratch_shapes=[pltpu.SemaphoreType.DMA((n,)), pltpu.SemaphoreType.DMA((n,))],
    )(*gs)


def _chip_exchange(ps):
    n = len(ps)

    def body(*refs):
        ins, outs = refs[:n], refs[n:2 * n]
        ssem, rsem = refs[2 * n:]
        x, y, c, chip, others = _mesh_pos()
        sends = []
        for i in range(n):
            for j, (ox, oy) in enumerate(others):
                cp = _remote(ins[i].at[:, 2 * ox + oy], outs[i].at[:, chip],
                             ssem.at[3 * i + j], rsem.at[3 * i + j], (ox, oy, c))
                cp.start()
                sends.append(cp)
        for i in range(n):
            for j, (ox, oy) in enumerate(others):
                slot = outs[i].at[:, 2 * ox + oy]
                _remote(slot, slot, ssem.at[3 * i + j], rsem.at[3 * i + j], (ox, oy, c)).wait_recv()
        for cp in sends:
            cp.wait_send()

    return pl.pallas_call(
        body,
        name="grads_chip_exchange",
        in_specs=[ANY] * n,
        out_specs=[ANY] * n,
        out_shape=[jax.ShapeDtypeStruct(p.shape, p.dtype) for p in ps],
        scratch_shapes=[pltpu.SemaphoreType.DMA((3 * n,)), pltpu.SemaphoreType.DMA((3 * n,))],
    )(*ps)


def _pair_gather(fs):
    n = len(fs)

    def body(*refs):
        bufs = refs[n:2 * n]
        ssem, rsem = refs[2 * n:]
        x, y, c, _, _ = _mesh_pos()
        sends = []
        for i in range(n):
            rh = bufs[i].shape[1] // 2
            mine = bufs[i].at[:, _half_rows(c, rh), :]
            cp = _remote(mine, mine, ssem.at[i], rsem.at[i], (x, y, 1 - c))
            cp.start()
            sends.append(cp)
        for i in range(n):
            rh = bufs[i].shape[1] // 2
            theirs = bufs[i].at[:, _half_rows(1 - c, rh), :]
            _remote(theirs, theirs, ssem.at[i], rsem.at[i], (x, y, 1 - c)).wait_recv()
        for cp in sends:
            cp.wait_send()

    return pl.pallas_call(
        body,
        name="grads_pair_gather",
        in_specs=[ANY] * n,
        out_specs=[ANY] * n,
        out_shape=[jax.ShapeDtypeStruct(f.shape, f.dtype) for f in fs],
        input_output_aliases={i: i for i in range(n)},
        scratch_shapes=[pltpu.SemaphoreType.DMA((n,)), pltpu.SemaphoreType.DMA((n,))],
    )(*fs)


def _sum_pair(g, r, c_idx, name):
    L, _, R, C = g.shape
    rh = R // 2

    def body(c_ref, g_ref, r_ref, o_ref):
        o_ref[...] = (g_ref[...].astype(F32) + r_ref[...].astype(F32)).astype(BF16)

    blk = pl.BlockSpec((None, None, rh, C), lambda l, s, c_ref: (l, s, 0, 0))
    return pl.pallas_call(
        body,
        name=name,
        grid_spec=pltpu.PrefetchScalarGridSpec(
            num_scalar_prefetch=1,
            grid=(L, N_CHIPS),
            in_specs=[pl.BlockSpec((None, None, rh, C), lambda l, s, c_ref: (l, s, c_ref[0], 0)), blk],
            out_specs=blk,
        ),
        out_shape=jax.ShapeDtypeStruct((L, N_CHIPS, rh, C), BF16),
        compiler_params=_params(("parallel", "parallel")),
    )(c_idx, g, r)


def _sum_chips(p, r, pos, name):
    L, _, rh, C = r.shape

    def body(pos_ref, p_ref, r_ref, o_ref):
        chip = pos_ref[0]
        own = p_ref[...].astype(F32)
        terms = [jnp.where(chip == k, own, r_ref[k].astype(F32)) for k in range(N_CHIPS)]
        o_ref[...] = ((terms[0] + terms[1]) + terms[2]) + terms[3]

    return pl.pallas_call(
        body,
        name=name,
        grid_spec=pltpu.PrefetchScalarGridSpec(
            num_scalar_prefetch=1,
            grid=(L,),
            in_specs=[pl.BlockSpec((None, None, rh, C), lambda l, pos_ref: (l, pos_ref[0], 0, 0)),
                      pl.BlockSpec((None, N_CHIPS, rh, C), lambda l, pos_ref: (l, 0, 0, 0))],
            out_specs=pl.BlockSpec((None, rh, C), lambda l, pos_ref: (l, pos_ref[1], 0)),
        ),
        out_shape=jax.ShapeDtypeStruct((L, 2 * rh, C), F32),
        compiler_params=_params(("parallel",)),
    )(pos, p, r)


def _small_all_reduce(p):
    rows, width = p.shape

    def body(p_ref, o_ref, sib_ref, slot_ref, ssem, rsem):
        x, y, c, chip, others = _mesh_pos()
        pair = _remote(p_ref, sib_ref, ssem.at[0], rsem.at[0], (x, y, 1 - c))
        pair.start()
        pair.wait()
        slot_ref[chip] = p_ref[...] + sib_ref[...]
        sends = []
        for j, (ox, oy) in enumerate(others):
            cp = _remote(slot_ref.at[chip], slot_ref.at[chip], ssem.at[1 + j], rsem.at[1 + j], (ox, oy, c))
            cp.start()
            sends.append(cp)
        for j, (ox, oy) in enumerate(others):
            slot = slot_ref.at[2 * ox + oy]
            _remote(slot, slot, ssem.at[1 + j], rsem.at[1 + j], (ox, oy, c)).wait_recv()
        for cp in sends:
            cp.wait_send()
        o_ref[...] = ((slot_ref[0] + slot_ref[1]) + slot_ref[2]) + slot_ref[3]

    vmem = pl.BlockSpec(memory_space=pltpu.VMEM)
    return pl.pallas_call(
        body,
        name="small_grads_all_reduce",
        in_specs=[vmem],
        out_specs=vmem,
        out_shape=jax.ShapeDtypeStruct(p.shape, F32),
        scratch_shapes=[pltpu.VMEM((rows, width), F32), pltpu.VMEM((N_CHIPS, rows, width), F32),
                        pltpu.SemaphoreType.DMA((4,)), pltpu.SemaphoreType.DMA((4,))],
    )(p)


def _adamw(g, w, m, v, name):
    L, R, C = w.shape
    Ct = g.shape[2]
    tr = _pick_tile(R, 256, 8)

    def body(g_ref, w_ref, m_ref, v_ref, go_ref, d_ref, mo_ref, vo_ref):
        grad = g_ref[...]
        if Ct != C:
            grad = grad[:, :C]
        m_new = ADAM_B1 * m_ref[...] + (1.0 - ADAM_B1) * grad
        v_new = ADAM_B2 * v_ref[...] + (1.0 - ADAM_B2) * jnp.square(grad)
        m_hat = m_new / (1.0 - ADAM_B1 ** ADAM_STEP)
        v_hat = v_new / (1.0 - ADAM_B2 ** ADAM_STEP)
        go_ref[...] = grad
        d_ref[...] = -ADAM_LR * (m_hat / (jnp.sqrt(v_hat) + ADAM_EPS) + ADAM_WD * w_ref[...])
        mo_ref[...] = m_new
        vo_ref[...] = v_new

    blk = pl.BlockSpec((None, tr, C), lambda l, r: (l, r, 0))
    return pl.pallas_call(
        body,
        name=name,
        grid=(L, R // tr),
        in_specs=[pl.BlockSpec((None, tr, Ct), lambda l, r: (l, r, 0)), blk, blk, blk],
        out_specs=[blk] * 4,
        out_shape=[jax.ShapeDtypeStruct((L, R, C), F32)] * 4,
        compiler_params=_params(("parallel", "parallel")),
    )(g, w, m, v)


_BIG = ("ffn1_gate", "ffn1_up", "ffn1_down", "w_in", "w_ret_up", "w_pool_up", "w_out",
        "ffn2_gate", "ffn2_up", "ffn2_down")
_SMALL = ("ffn1_norm", "mix_norm", "ffn2_norm", "final_norm", "pool_scale", "pool_maps")
_ORDER = ("meta", "ffn1_norm", "ffn1_gate", "ffn1_up", "ffn1_down", "mix_norm", "w_in", "pool_maps",
          "pool_scale", "w_ret_up", "w_pool_up", "w_out", "ffn2_norm", "ffn2_gate", "ffn2_up", "ffn2_down",
          "final_norm")


def _transport(a):
    r, c = a.shape[1], a.shape[2]
    return jnp.pad(a.astype(BF16), ((0, 0), (0, _round_up(r, LANES) - r), (0, _round_up(c, LANES) - c)))


def _pack_rows(parts, width):
    rows = [p.reshape(-1, width) for p in parts]
    total = sum(r.shape[0] for r in rows)
    fill = _round_up(total, 8) - total
    if fill:
        rows.append(jnp.zeros((fill, width), F32))
    return jnp.concatenate(rows, axis=0)


def _unpack_rows(packed, shapes, width):
    out, at = [], 0
    for shp in shapes:
        size = 1
        for d in shp:
            size *= d
        n = size // width
        out.append(packed[at:at + n].reshape(shp))
        at += n
    return out


def kernel(x, meta, ffn1_norm, ffn1_gate, ffn1_up, ffn1_down, mix_norm, w_in, pool_maps, pool_scale, w_ret_up, w_pool_up, w_out, ffn2_norm, ffn2_gate, ffn2_up, ffn2_down, final_norm, loss_target, m_meta, m_ffn1_norm, m_ffn1_gate, m_ffn1_up, m_ffn1_down, m_mix_norm, m_w_in, m_pool_maps, m_pool_scale, m_w_ret_up, m_w_pool_up, m_w_out, m_ffn2_norm, m_ffn2_gate, m_ffn2_up, m_ffn2_down, m_final_norm, v_meta, v_ffn1_norm, v_ffn1_gate, v_ffn1_up, v_ffn1_down, v_mix_norm, v_w_in, v_pool_maps, v_pool_scale, v_w_ret_up, v_w_pool_up, v_w_out, v_ffn2_norm, v_ffn2_gate, v_ffn2_up, v_ffn2_down, v_final_norm):
    args = dict(locals())
    w = {n: args[n] for n in _ORDER}
    mom = {n: args["m_" + n] for n in _ORDER}
    var = {n: args["v_" + n] for n in _ORDER}

    assert x.shape[0] == 1, "one batch element per device"
    seq, D = x.shape[1], x.shape[2]
    L = ffn1_gate.shape[0]
    assert seq % CHUNK == 0 and D % RET_WIDTH == 0 and (2 * POOL_WIDTH) % D == 0
    pad = (-(seq + N_META)) % CHUNK
    T = seq + N_META + pad
    Dq = D // N_CHIPS
    tm = _pick_tile(T, 528, BF16_ROWS)
    cg = _pick_tile(T // CHUNK, 11, 1)

    gathered = _all_gather([_transport(w[n]) for n in _BIG] + [meta[None]])
    wt = dict(zip(_BIG, gathered[:-1]))
    meta_full = jnp.transpose(gathered[-1][0], (1, 0, 2)).reshape(N_META, D)

    loss_acc, dh, gw, small, d_final = _local_step(x[0], meta_full, loss_target[0], w, wt, pad, tm, cg)
    loss = lax.psum(loss_acc[0, 0], ("x", "y", "c"))
    grad_x = dh[pad + N_META:][None]
    return _reduce_and_update(loss, grad_x, dh[pad:pad + N_META], gw, small, d_final, w, mom, var)


def _local_step(x, meta_full, tgt, w, wt, pad, tm, cg):
    D = x.shape[1]
    T = pad + N_META + x.shape[0]
    L = w["ffn1_norm"].shape[0]
    pool_maps = w["pool_maps"]
    gains = {n: w[n].reshape(L, 1, D) for n in ("ffn1_norm", "mix_norm", "ffn2_norm")}
    scale3 = w["pool_scale"].reshape(L, 1, POOL_WIDTH)
    consts = _ret_consts(T, pad)

    h = jnp.concatenate([jnp.zeros((pad, D), F32), meta_full, x], axis=0)
    saved = []
    for i in range(L):
        s = {"h0": h}
        h, s["a1"], s["g1"], s["u1"], s["act1"] = _ffn_fwd(
            h, gains["ffn1_norm"], wt["ffn1_gate"], wt["ffn1_up"], wt["ffn1_down"], i, tm, f"ffn1_fwd_{i}")
        s["h1"] = h
        s["z"], s["b"] = _inproj_fwd(h, gains["mix_norm"], wt["w_in"], i, tm, f"inproj_fwd_{i}")
        s["r"], s["o_pre"], s["s_all"] = _ret_fwd(s["z"], consts, cg, f"retention_fwd_{i}")
        s["pm"] = _pool_fwd(s["z"], pool_maps, scale3, i, pad, f"pool_fwd_{i}")
        h, s["mixed"], s["ret"], s["pool"] = _mix_fwd(
            h, s["r"], s["pm"], s["z"], wt["w_ret_up"], wt["w_pool_up"], wt["w_out"], i, tm, f"mix_fwd_{i}")
        s["h2"] = h
        h, s["a2"], s["g2"], s["u2"], s["act2"] = _ffn_fwd(
            h, gains["ffn2_norm"], wt["ffn2_gate"], wt["ffn2_up"], wt["ffn2_down"], i, tm, f"ffn2_fwd_{i}")
        saved.append(s)

    dh, loss_acc, d_final = _final_loss(h, w["final_norm"].reshape(1, D), tgt, "final_norm_loss")

    gw = {n: None for n in _BIG}
    small = {n: [None] * L for n in ("ffn1_norm", "mix_norm", "ffn2_norm", "pool_scale", "pool_maps")}

    def grad(n, a, b, i, mode, scale=1.0):
        gw[n] = _grad_tn(a, b, gw[n], i, L, mode, scale, tm, f"grad_{n}_{i}")

    for i in reversed(range(L)):
        s = saved[i]
        dh3 = dh
        dh, dg, du, small["ffn2_norm"][i] = _ffn_bwd_dx(
            dh3, s["h2"], gains["ffn2_norm"], s["g2"], s["u2"], wt["ffn2_gate"], wt["ffn2_up"], wt["ffn2_down"],
            i, tm, pad, f"ffn2_bwd_{i}")
        grad("ffn2_gate", s["a2"], dg, i, "col")
        grad("ffn2_up", s["a2"], du, i, "col")
        grad("ffn2_down", s["act2"], dh3, i, "row", 0.5)
        dgab, dret, dpool, dr, dpm = _mix_bwd_dx(
            dh, s["z"], s["ret"], s["pool"], wt["w_out"], wt["w_ret_up"], wt["w_pool_up"], i, tm, f"mix_bwd_{i}")
        grad("w_out", s["mixed"], dh, i, "row")
        grad("w_ret_up", s["r"], dret, i, "col")
        grad("w_pool_up", s["pm"], dpool, i, "col")
        du_pool, small["pool_maps"][i], small["pool_scale"][i] = _pool_bwd(
            s["z"], dpm, pool_maps, scale3, i, pad, f"pool_bwd_{i}")
        dq, dgr, dkp, dvp, ds = _ret_bwd_local(s["z"], s["o_pre"], s["s_all"], dr, consts, cg, f"retention_bwd_{i}")
        dk, dv = _ret_bwd_state(s["z"], dkp, dvp, ds, consts, cg, f"retention_bwd_state_{i}")
        dz = jnp.concatenate([dq, dk, dv, dgr, du_pool, dgab], axis=1)
        dh2 = dh
        dh, small["mix_norm"][i] = _inproj_bwd_dx(
            dz, wt["w_in"], s["h1"], gains["mix_norm"], dh2, i, tm, pad, f"inproj_bwd_{i}")
        grad("w_in", s["b"], dz, i, "col")
        dh1 = dh
        dh, dg, du, small["ffn1_norm"][i] = _ffn_bwd_dx(
            dh1, s["h0"], gains["ffn1_norm"], s["g1"], s["u1"], wt["ffn1_gate"], wt["ffn1_up"], wt["ffn1_down"],
            i, tm, pad, f"ffn1_bwd_{i}")
        grad("ffn1_gate", s["a1"], dg, i, "col")
        grad("ffn1_up", s["a1"], du, i, "col")
        grad("ffn1_down", s["act1"], dh1, i, "row", 0.5)

    return loss_acc, dh, gw, small, d_final


def _reduce_and_update(loss, grad_x, d_meta_rows, gw, small, d_final, w, mom, var):
    meta = w["meta"]
    D = w["final_norm"].shape[0]
    Dq = D // N_CHIPS

    c_idx = lax.axis_index("c").astype(jnp.int32).reshape(1)
    partials = [gw[n] for n in _BIG]
    from_sibling = _pair_exchange(partials)
    chip_partials = [_sum_pair(g, r, c_idx, f"sum_pair_{n}") for n, g, r in zip(_BIG, partials, from_sibling)]
    from_chips = _chip_exchange(chip_partials)
    chip = 2 * lax.axis_index("x") + lax.axis_index("y")
    pos = jnp.stack([chip, lax.axis_index("c")]).astype(jnp.int32)
    halves = [_sum_chips(p, r, pos, f"sum_chips_{n}") for n, p, r in zip(_BIG, chip_partials, from_chips)]
    shard_grads = dict(zip(_BIG, _pair_gather(halves)))

    small_parts = [jnp.concatenate(small[n], axis=0) for n in ("ffn1_norm", "mix_norm", "ffn2_norm")]
    small_parts += [d_final, jnp.concatenate(small["pool_scale"], axis=0), jnp.concatenate(small["pool_maps"], axis=0)]
    reduced = _small_all_reduce(_pack_rows(small_parts + [d_meta_rows], D))
    small_shapes = [w[n].shape for n in _SMALL]
    small_rows = sum(math.prod(shp) for shp in small_shapes) // D
    d_meta = lax.dynamic_slice_in_dim(reduced[small_rows:small_rows + N_META], chip * Dq, Dq, axis=1)

    out = {}
    for n in _BIG:
        out[n] = _adamw(shard_grads[n], w[n], mom[n], var[n], f"adamw_{n}")
    names = _SMALL + ("meta",)
    packed_g = _pack_rows([reduced[:small_rows], d_meta], D)
    packed = [_pack_rows([t[n] for n in names], D) for t in (w, mom, var)]
    res = _adamw(packed_g[None], packed[0][None], packed[1][None], packed[2][None], "adamw_small")
    shapes = small_shapes + [meta.shape]
    unpacked = [_unpack_rows(r[0], shapes, D) for r in res]
    for k, n in enumerate(names):
        out[n] = tuple(u[k] for u in unpacked)

    return (loss, grad_x) + tuple(out[n][j] for j in range(4) for n in _ORDER)
```

```python
import functools
import math

import jax
import jax.numpy as jnp
from jax import lax
from jax.experimental import pallas as pl
from jax.experimental.pallas import tpu as pltpu

F32 = jnp.float32
BF16 = jnp.bfloat16

N_META = 16
RET_HEADS = 4
HEAD_DIM = 128
RET_WIDTH = RET_HEADS * HEAD_DIM
POOL_WINDOWS = (2, 4, 8, 16)
POOL_GROUPS = len(POOL_WINDOWS)
POOL_WIDTH = POOL_GROUPS * HEAD_DIM
CHUNK = 128
ROPE_BASE = 10000.0
EPS = 1e-6
ADAM_LR = 0.001
ADAM_B1 = 0.9
ADAM_B2 = 0.999
ADAM_EPS = 1e-08
ADAM_WD = 0.01
ADAM_STEP = 10

N_CHIPS = 4
LANES = 128
BF16_ROWS = 16
V7X_VMEM_LIMIT = 52 * 1024 * 1024
MESH = pl.DeviceIdType.MESH
ANY = pl.BlockSpec(memory_space=pl.ANY)


def _round_up(n, m):
    return -(-n // m) * m


def _pick_tile(n, target, mult):
    best = None
    for d in range(mult, min(n, target) + 1, mult):
        if n % d == 0:
            best = d
    assert best is not None, (n, target, mult)
    return best


def _params(sem=None):
    return pltpu.CompilerParams(dimension_semantics=sem, vmem_limit_bytes=V7X_VMEM_LIMIT)


def _dot(a, b):
    return jnp.dot(a, b, preferred_element_type=F32)


def _dot_nt(a, b):
    return lax.dot_general(a, b, (((1,), (1,)), ((), ())), preferred_element_type=F32)


def _dot_tn(a, b):
    return lax.dot_general(a, b, (((0,), (0,)), ((), ())), preferred_element_type=F32)


def _ein(spec, a, b):
    return jnp.einsum(spec, a, b, preferred_element_type=F32)


def _sigmoid(x):
    return jax.nn.sigmoid(x)


def _rms_fwd(x, gain):
    r = lax.rsqrt(jnp.mean(x * x, axis=-1, keepdims=True) + EPS)
    return x * r * gain


def _rms_bwd(x, gain, da):
    r = lax.rsqrt(jnp.mean(x * x, axis=-1, keepdims=True) + EPS)
    xh = x * r
    dgain = jnp.sum(da * xh, axis=0, keepdims=True)
    dxh = da * gain
    dx = r * (dxh - xh * jnp.mean(dxh * xh, axis=-1, keepdims=True))
    return dx, dgain


def _row_mask(t, tm, pad, shape):
    rows = t * tm + lax.broadcasted_iota(jnp.int32, shape, 0)
    return rows >= pad


def _mesh_pos():
    x, y, c = lax.axis_index("x"), lax.axis_index("y"), lax.axis_index("c")
    others = [(1 - x, y), (x, 1 - y), (1 - x, 1 - y)]
    return x, y, c, 2 * x + y, others


def _half_rows(c, rh):
    return pl.ds(pl.multiple_of(c * rh, rh), rh)


def _remote(src, dst, ssem, rsem, dev):
    return pltpu.make_async_remote_copy(src_ref=src, dst_ref=dst, send_sem=ssem, recv_sem=rsem,
                                        device_id=dev, device_id_type=MESH)


class _Rider:
    def __init__(self, ins, out_shapes, n_sem, start, finish):
        self.ins, self.out_shapes, self.n_sem, self.start, self.finish = ins, out_shapes, n_sem, start, finish
        self.results = None


def _gather_rider(pieces):
    per = 7
    layers = [layer for _, layer in pieces]

    def first_copies(ins, outs, ssem, rsem):
        x, y, c, chip, others = _mesh_pos()
        copies = []
        for i, layer in enumerate(layers):
            mine = _half_rows(c, ins[i].shape[1] // 2)
            for j, (ox, oy) in enumerate(others):
                copies.append(_remote(ins[i].at[layer, mine, :], outs[i].at[chip, mine, :],
                                      ssem.at[per * i + j], rsem.at[per * i + j], (ox, oy, c)))
            copies.append(_remote(ins[i].at[layer], outs[i].at[chip],
                                  ssem.at[per * i + 6], rsem.at[per * i + 6], (x, y, 1 - c)))
        return copies

    def start(ins, outs, ssem, rsem):
        for cp in first_copies(ins, outs, ssem, rsem):
            cp.start()

    def finish(ins, outs, ssem, rsem):
        x, y, c, chip, others = _mesh_pos()
        sibling = (x, y, 1 - c)
        forwards = []
        for i in range(len(layers)):
            mine = _half_rows(c, ins[i].shape[1] // 2)
            for j, (ox, oy) in enumerate(others):
                rows = outs[i].at[2 * ox + oy, mine, :]
                _remote(rows, rows, ssem.at[per * i + j], rsem.at[per * i + j], (ox, oy, c)).wait_recv()
                fwd = _remote(rows, rows, ssem.at[per * i + 3 + j], rsem.at[per * i + 3 + j], sibling)
                fwd.start()
                forwards.append(fwd)
        for i in range(len(layers)):
            theirs = _half_rows(1 - c, ins[i].shape[1] // 2)
            for j, (ox, oy) in enumerate(others):
                rows = outs[i].at[2 * ox + oy, theirs, :]
                _remote(rows, rows, ssem.at[per * i + 3 + j], rsem.at[per * i + 3 + j], sibling).wait_recv()
            own = outs[i].at[chip]
            _remote(own, own, ssem.at[per * i + 6], rsem.at[per * i + 6], sibling).wait_recv()
        for cp in first_copies(ins, outs, ssem, rsem) + forwards:
            cp.wait_send()

    shapes = [jax.ShapeDtypeStruct((N_CHIPS,) + s.shape[1:], s.dtype) for s, _ in pieces]
    return _Rider([s for s, _ in pieces], shapes, per * len(pieces), start, finish)


def _chip_exchange_rider(ps):
    def copies(ins, outs, ssem, rsem):
        x, y, c, chip, others = _mesh_pos()
        return [_remote(ins[i].at[2 * ox + oy], outs[i].at[chip], ssem.at[3 * i + j], rsem.at[3 * i + j], (ox, oy, c))
                for i in range(len(ps)) for j, (ox, oy) in enumerate(others)]

    def start(ins, outs, ssem, rsem):
        for cp in copies(ins, outs, ssem, rsem):
            cp.start()

    def finish(ins, outs, ssem, rsem):
        x, y, c, chip, others = _mesh_pos()
        for i in range(len(ps)):
            for j, (ox, oy) in enumerate(others):
                slot = outs[i].at[2 * ox + oy]
                _remote(slot, slot, ssem.at[3 * i + j], rsem.at[3 * i + j], (ox, oy, c)).wait_recv()
        for cp in copies(ins, outs, ssem, rsem):
            cp.wait_send()

    return _Rider(list(ps), [jax.ShapeDtypeStruct(p.shape, p.dtype) for p in ps], 3 * len(ps), start, finish)


def _pair_exchange_rider(gs):
    def copies(ins, outs, ssem, rsem):
        x, y, c, _, _ = _mesh_pos()
        return [_remote(ins[i].at[:, _half_rows(1 - c, ins[i].shape[1] // 2), :], outs[i],
                        ssem.at[i], rsem.at[i], (x, y, 1 - c)) for i in range(len(gs))]

    def start(ins, outs, ssem, rsem):
        for cp in copies(ins, outs, ssem, rsem):
            cp.start()

    def finish(ins, outs, ssem, rsem):
        for cp in copies(ins, outs, ssem, rsem):
            cp.wait()

    shapes = [jax.ShapeDtypeStruct((g.shape[0], g.shape[1] // 2, g.shape[2]), g.dtype) for g in gs]
    return _Rider(list(gs), shapes, len(gs), start, finish)


def _run_rider(rider, name):
    def body(*refs):
        n = len(rider.ins)
        ins, outs = refs[:n], refs[n:2 * n]
        ssem, rsem = refs[2 * n:]
        rider.start(ins, outs, ssem, rsem)
        rider.finish(ins, outs, ssem, rsem)

    rider.results = pl.pallas_call(
        body,
        name=name,
        in_specs=[ANY] * len(rider.ins),
        out_specs=[ANY] * len(rider.ins),
        out_shape=rider.out_shapes,
        scratch_shapes=[pltpu.SemaphoreType.DMA((rider.n_sem,)), pltpu.SemaphoreType.DMA((rider.n_sem,))],
    )(*rider.ins)
    return rider.results


def _pair_gather(fs):
    n = len(fs)

    def body(*refs):
        bufs = refs[n:2 * n]
        ssem, rsem = refs[2 * n:]
        x, y, c, _, _ = _mesh_pos()
        sends = []
        for i in range(n):
            rh = bufs[i].shape[0] // 2
            mine = bufs[i].at[_half_rows(c, rh), :]
            cp = _remote(mine, mine, ssem.at[i], rsem.at[i], (x, y, 1 - c))
            cp.start()
            sends.append(cp)
        for i in range(n):
            rh = bufs[i].shape[0] // 2
            theirs = bufs[i].at[_half_rows(1 - c, rh), :]
            _remote(theirs, theirs, ssem.at[i], rsem.at[i], (x, y, 1 - c)).wait_recv()
        for cp in sends:
            cp.wait_send()

    return pl.pallas_call(
        body,
        name="grads_pair_gather",
        in_specs=[ANY] * n,
        out_specs=[ANY] * n,
        out_shape=[jax.ShapeDtypeStruct(f.shape, f.dtype) for f in fs],
        input_output_aliases={i: i for i in range(n)},
        scratch_shapes=[pltpu.SemaphoreType.DMA((n,)), pltpu.SemaphoreType.DMA((n,))],
    )(*fs)


def _call(body, *, name, grid, in_specs, out_specs, out_shape, operands, scratch=(), sem=None, rider=None):
    if rider is None:
        return pl.pallas_call(
            body, name=name, grid=grid, in_specs=in_specs, out_specs=out_specs, out_shape=out_shape,
            scratch_shapes=list(scratch), compiler_params=_params(sem))(*operands)
    n_in, n_out, n_sc, r = len(in_specs), len(out_specs), len(scratch), len(rider.ins)

    def carrying(*refs):
        a, b = n_in, n_in + r
        c, d = b + n_out, b + n_out + r
        e = d + n_sc
        ids = [pl.program_id(k) for k in range(len(grid))]
        first = functools.reduce(jnp.logical_and, [i == 0 for i in ids])
        last = functools.reduce(jnp.logical_and, [i == g - 1 for i, g in zip(ids, grid)])

        @pl.when(first)
        def _():
            rider.start(refs[a:b], refs[c:d], refs[e], refs[e + 1])

        body(*refs[:a], *refs[b:c], *refs[d:e])

        @pl.when(last)
        def _():
            rider.finish(refs[a:b], refs[c:d], refs[e], refs[e + 1])

    outs = pl.pallas_call(
        carrying, name=name, grid=grid,
        in_specs=list(in_specs) + [ANY] * r,
        out_specs=list(out_specs) + [ANY] * r,
        out_shape=list(out_shape) + list(rider.out_shapes),
        scratch_shapes=list(scratch) + [pltpu.SemaphoreType.DMA((rider.n_sem,)), pltpu.SemaphoreType.DMA((rider.n_sem,))],
        compiler_params=_params(("arbitrary",) * len(grid)),
    )(*operands, *rider.ins)
    rider.results = outs[n_out:]
    return outs[:n_out]


def _ffn_fwd(h, gain, wg, wu, wd, layer, tm, name, rider=None):
    T, D = h.shape
    Fs = wg.shape[-1]
    F = N_CHIPS * Fs

    def body(h_ref, g_ref, wg_ref, wu_ref, wd_ref, ho_ref, a_ref, go_ref, uo_ref, act_ref, acc_ref):
        s = pl.program_id(1)

        @pl.when(s == 0)
        def _():
            a_ref[...] = _rms_fwd(h_ref[...], g_ref[...]).astype(BF16)
            acc_ref[...] = jnp.zeros_like(acc_ref)

        a = a_ref[...]
        g = _dot(a, wg_ref[...])
        u = _dot(a, wu_ref[...])
        act = (g * _sigmoid(g) * u).astype(BF16)
        go_ref[...] = g.astype(BF16)
        uo_ref[...] = u.astype(BF16)
        act_ref[...] = act
        acc_ref[...] += _dot(act, wd_ref[...])

        @pl.when(s == N_CHIPS - 1)
        def _():
            ho_ref[...] = h_ref[...] + 0.5 * acc_ref[...]

    row = pl.BlockSpec((tm, D), lambda t, s: (t, 0))
    col = pl.BlockSpec((tm, Fs), lambda t, s: (t, s))
    wcol = pl.BlockSpec((None, D, Fs), lambda t, s: (s, 0, 0))
    return _call(
        body, name=name, grid=(T // tm, N_CHIPS),
        in_specs=[row, pl.BlockSpec((None, 1, D), lambda t, s: (layer, 0, 0)), wcol, wcol,
                  pl.BlockSpec((None, Fs, D), lambda t, s: (s, 0, 0))],
        out_specs=[row, row, col, col, col],
        out_shape=[jax.ShapeDtypeStruct((T, D), F32), jax.ShapeDtypeStruct((T, D), BF16)]
        + [jax.ShapeDtypeStruct((T, F), BF16)] * 3,
        scratch=[pltpu.VMEM((tm, D), F32)],
        sem=("parallel", "arbitrary"), operands=(h, gain, wg, wu, wd), rider=rider)


def _inproj_fwd(h, gain, win, layer, tm, name, rider=None):
    T, D = h.shape
    Ns = win.shape[-1]

    def body(h_ref, g_ref, w_ref, z_ref, b_ref):
        @pl.when(pl.program_id(1) == 0)
        def _():
            b_ref[...] = _rms_fwd(h_ref[...], g_ref[...]).astype(BF16)

        z_ref[...] = _dot(b_ref[...], w_ref[...])

    return _call(
        body, name=name, grid=(T // tm, N_CHIPS),
        in_specs=[pl.BlockSpec((tm, D), lambda t, s: (t, 0)),
                  pl.BlockSpec((None, 1, D), lambda t, s: (layer, 0, 0)),
                  pl.BlockSpec((None, D, Ns), lambda t, s: (s, 0, 0))],
        out_specs=[pl.BlockSpec((tm, Ns), lambda t, s: (t, s)), pl.BlockSpec((tm, D), lambda t, s: (t, 0))],
        out_shape=[jax.ShapeDtypeStruct((T, N_CHIPS * Ns), F32), jax.ShapeDtypeStruct((T, D), BF16)],
        sem=("parallel", "arbitrary"), operands=(h, gain, win), rider=rider)


def _ret_consts(T, pad):
    half = HEAD_DIM // 2
    inv_freq = ROPE_BASE ** (-jnp.arange(half, dtype=F32) / half)
    pos = jnp.arange(T, dtype=F32) - pad
    ang = pos[:, None] * inv_freq[None, :]
    cos = jnp.cos(ang)
    sin = jnp.sin(ang)
    cosf = jnp.concatenate([cos, cos], axis=1)
    sinf = jnp.concatenate([-sin, sin], axis=1)
    log_gamma = jnp.log1p(-(2.0 ** (-5.0 - jnp.arange(RET_HEADS, dtype=F32))))
    idx = jnp.arange(CHUNK, dtype=F32)
    diff = idx[:, None] - idx[None, :]
    intra = jnp.where(diff[None] >= 0, jnp.exp(diff[None] * log_gamma[:, None, None]), 0.0)
    k_decay = jnp.exp((CHUNK - 1.0 - idx)[None, :] * log_gamma[:, None])
    q_decay = jnp.exp((idx + 1.0)[None, :] * log_gamma[:, None])
    chunk_decay = jnp.exp(CHUNK * log_gamma)
    kdec = jnp.broadcast_to(k_decay[:, :, None], (RET_HEADS, CHUNK, HEAD_DIM))
    qdec = jnp.broadcast_to(q_decay[:, :, None], (RET_HEADS, CHUNK, HEAD_DIM))
    cdb = jnp.broadcast_to(chunk_decay[:, None, None], (RET_HEADS, 8, HEAD_DIM))
    return cosf, sinf, intra, kdec, qdec, cdb


def _rot(t, cosv, sinv):
    return t * cosv + pltpu.roll(t, HEAD_DIM // 2, 1) * sinv


def _rot_t(g, cosv, sinv):
    return g * cosv + pltpu.roll(g * sinv, HEAD_DIM // 2, 1)


def _head_specs(tg, section, order):
    return pl.BlockSpec((tg, HEAD_DIM), lambda h, g: (order(g), section * RET_HEADS + h))


def _ret_fwd(z, consts, cg, name):
    T = z.shape[0]
    N = T // CHUNK
    ng = N // cg
    tg = cg * CHUNK
    cosf, sinf, intra, kdec, qdec, cdb = consts
    fwd = lambda g: g

    def body(zq, zk, zv, zg, cos_ref, sin_ref, m_ref, kd_ref, qd_ref, cd_ref, r_ref, o_ref, s_ref, st_ref):
        @pl.when(pl.program_id(1) == 0)
        def _():
            st_ref[...] = jnp.zeros_like(st_ref)

        cosv = cos_ref[...]
        sinv = sin_ref[...]
        q3 = (_rot(zq[...], cosv, sinv) * (HEAD_DIM ** -0.5)).reshape(cg, CHUNK, HEAD_DIM)
        k3 = _rot(zk[...], cosv, sinv).reshape(cg, CHUNK, HEAD_DIM)
        vb = zv[...].reshape(cg, CHUNK, HEAD_DIM).astype(BF16)
        scores = _ein("ncd,nmd->ncm", q3.astype(BF16), k3.astype(BF16)) * m_ref[...][None]
        inner = _ein("ncm,nmd->ncd", scores.astype(BF16), vb)
        kv = _ein("ncd,nce->nde", (k3 * kd_ref[...][None]).astype(BF16), vb)
        cd = cd_ref[0:1, :]
        state = st_ref[...]
        for n in range(cg):
            s_ref[n] = state
            state = state * cd + kv[n]
        st_ref[...] = state
        qdb = (q3 * qd_ref[...][None]).astype(BF16)
        cross = _ein("ncd,nde->nce", qdb, s_ref[...].astype(BF16))
        out = (inner + cross).reshape(tg, HEAD_DIM)
        o_ref[...] = out
        xc = out - jnp.mean(out, axis=-1, keepdims=True)
        rn = xc * lax.rsqrt(jnp.mean(xc * xc, axis=-1, keepdims=True) + EPS)
        g = zg[...]
        r_ref[...] = (rn * (g * _sigmoid(g))).astype(BF16)

    tab = pl.BlockSpec((tg, HEAD_DIM), lambda h, g: (g, 0))
    per_head = lambda rows: pl.BlockSpec((None, rows, HEAD_DIM), lambda h, g: (h, 0, 0))
    head_out = pl.BlockSpec((tg, HEAD_DIM), lambda h, g: (g, h))
    return _call(
        body, name=name, grid=(RET_HEADS, ng),
        in_specs=[_head_specs(tg, i, fwd) for i in range(4)]
        + [tab, tab, per_head(CHUNK), per_head(CHUNK), per_head(CHUNK), per_head(8)],
        out_specs=[head_out, head_out, pl.BlockSpec((None, cg, HEAD_DIM, HEAD_DIM), lambda h, g: (h, g, 0, 0))],
        out_shape=[jax.ShapeDtypeStruct((T, RET_WIDTH), BF16), jax.ShapeDtypeStruct((T, RET_WIDTH), F32),
                   jax.ShapeDtypeStruct((RET_HEADS, N, HEAD_DIM, HEAD_DIM), F32)],
        scratch=[pltpu.VMEM((HEAD_DIM, HEAD_DIM), F32)],
        sem=("parallel", "arbitrary"), operands=(z, z, z, z, cosf, sinf, intra, kdec, qdec, cdb))


def _window_sums(u, shift_of):
    sums = []
    s = u
    k = 1
    while k < POOL_WINDOWS[-1]:
        s = s + pltpu.roll(s, shift_of(k), 0)
        sums.append(s)
        k *= 2
    return sums


def _select_group(vals, g):
    out = vals[-1]
    for i in range(len(vals) - 2, -1, -1):
        out = jnp.where(g == i, vals[i], out)
    return out


def _pool_parts(u, g, T, pad):
    rows = lax.broadcasted_iota(jnp.int32, (T, HEAD_DIM), 0)
    valid = rows >= pad
    win = _select_group([float(w) for w in POOL_WINDOWS], g)
    div = jnp.clip((rows - pad + 1).astype(F32), 1.0, win)
    s = _select_group(_window_sums(u, lambda k: k), g)
    pooled = jnp.where(valid, s / div - u, 0.0)
    return pooled, div, valid


def _pool_specs(T, layer):
    first = 4 * RET_WIDTH // HEAD_DIM
    return [
        pl.BlockSpec((T, HEAD_DIM), lambda g: (0, first + g)),
        pl.BlockSpec((None, None, HEAD_DIM, HEAD_DIM), lambda g: (layer, g, 0, 0)),
        pl.BlockSpec((None, 1, HEAD_DIM), lambda g: (layer, 0, g)),
    ]


def _pool_fwd(z, maps, scale, layer, pad, name):
    T = z.shape[0]
    assert pad >= POOL_WINDOWS[-1], "window rolls wrap into the zero rows in front"

    def body(zu, maps_ref, sc_ref, pm_ref):
        g = pl.program_id(0)
        pooled, _, _ = _pool_parts(zu[...], g, T, pad)
        y = _dot(pooled.astype(BF16), maps_ref[...].astype(BF16))
        pm_ref[...] = (y * sc_ref[...]).astype(BF16)

    return _call(
        body, name=name, grid=(POOL_GROUPS,),
        in_specs=_pool_specs(T, layer),
        out_specs=[pl.BlockSpec((T, HEAD_DIM), lambda g: (0, g))],
        out_shape=[jax.ShapeDtypeStruct((T, POOL_WIDTH), BF16)],
        sem=("parallel",), operands=(z, maps, scale))[0]


def _gate_specs(tm, D):
    nb = D // RET_WIDTH
    first = (4 * RET_WIDTH + POOL_WIDTH) // RET_WIDTH
    return [pl.BlockSpec((tm, RET_WIDTH), functools.partial(lambda t, j: (t, j), j=first + j)) for j in range(2 * nb)]


def _load_gates(refs, nb):
    ga = jnp.concatenate([r[...] for r in refs[:nb]], axis=1) if nb > 1 else refs[0][...]
    gb = jnp.concatenate([r[...] for r in refs[nb:]], axis=1) if nb > 1 else refs[nb][...]
    return ga, gb


def _mix_fwd(h, r, pm, z, wru, wpu, wout, tm, name, rider=None):
    T, D = h.shape
    Dq = D // N_CHIPS
    nb = D // RET_WIDTH

    def body(*refs):
        h_ref, r_ref, pm_ref = refs[:3]
        gate_refs = refs[3:3 + 2 * nb]
        wru_ref, wpu_ref, wout_ref, ho_ref, mx_ref, ret_ref, pool_ref = refs[3 + 2 * nb:]
        rv = r_ref[...]
        pv = pm_ref[...]
        ret = jnp.concatenate([_dot(rv, wru_ref[s]) for s in range(N_CHIPS)], axis=1)
        pool = jnp.concatenate([_dot(pv, wpu_ref[s]) for s in range(N_CHIPS)], axis=1)
        ga, gb = _load_gates(gate_refs, nb)
        mixed = (_sigmoid(ga) * ret + _sigmoid(gb) * pool).astype(BF16)
        mx_ref[...] = mixed
        ret_ref[...] = ret.astype(BF16)
        pool_ref[...] = pool.astype(BF16)
        ho_ref[...] = h_ref[...] + _dot(mixed, wout_ref[...].reshape(D, D))

    row = pl.BlockSpec((tm, D), lambda t: (t, 0))
    half = pl.BlockSpec((tm, RET_WIDTH), lambda t: (t, 0))
    up = pl.BlockSpec((N_CHIPS, RET_WIDTH, Dq), lambda t: (0, 0, 0))
    return _call(
        body, name=name, grid=(T // tm,),
        in_specs=[row, half, half] + _gate_specs(tm, D) + [up, up, pl.BlockSpec((N_CHIPS, Dq, D), lambda t: (0, 0, 0))],
        out_specs=[row, row, row, row],
        out_shape=[jax.ShapeDtypeStruct((T, D), F32)] + [jax.ShapeDtypeStruct((T, D), BF16)] * 3,
        sem=("parallel",), operands=(h, r, pm, *([z] * (2 * nb)), wru, wpu, wout), rider=rider)


def _final_loss(h, gain, tgt, name):
    T, D = h.shape
    first = (T - tgt.shape[0]) // CHUNK

    def body(h_ref, g_ref, t_ref, dh_ref, loss_ref, dg_ref):
        i = pl.program_id(0)

        @pl.when(i == 0)
        def _():
            loss_ref[...] = jnp.zeros_like(loss_ref)
            dg_ref[...] = jnp.zeros_like(dg_ref)

        x = h_ref[...]
        gain_v = g_ref[...]
        err = jnp.where(i >= first, _rms_fwd(x, gain_v) - t_ref[...], 0.0)
        loss_ref[...] += 0.5 * jnp.sum(jnp.mean(err * err, axis=-1))
        dx, dgain = _rms_bwd(x, gain_v, err * (1.0 / D))
        dg_ref[...] += dgain
        dh_ref[...] = dx

    return _call(
        body, name=name, grid=(T // CHUNK,),
        in_specs=[pl.BlockSpec((CHUNK, D), lambda i: (i, 0)),
                  pl.BlockSpec((1, D), lambda i: (0, 0)),
                  pl.BlockSpec((CHUNK, D), lambda i: (jnp.maximum(i - first, 0), 0))],
        out_specs=[pl.BlockSpec((CHUNK, D), lambda i: (i, 0)),
                   pl.BlockSpec((1, LANES), lambda i: (0, 0)),
                   pl.BlockSpec((1, D), lambda i: (0, 0))],
        out_shape=[jax.ShapeDtypeStruct((T, D), F32), jax.ShapeDtypeStruct((1, LANES), F32),
                   jax.ShapeDtypeStruct((1, D), F32)],
        sem=("arbitrary",), operands=(h, gain, tgt))


def _ffn_bwd_dx(dy, h, gain, g, u, wg, wu, wd, layer, tm, pad, name, rider=None):
    T, D = h.shape
    Fs = wg.shape[-1]
    F = N_CHIPS * Fs

    def body(dy_ref, h_ref, g_ref, go_ref, uo_ref, wg_ref, wu_ref, wd_ref,
             dh_ref, dg_ref, du_ref, dgain_ref, dyh_ref, da_ref):
        t = pl.program_id(0)
        s = pl.program_id(1)

        @pl.when((t == 0) & (s == 0))
        def _():
            dgain_ref[...] = jnp.zeros_like(dgain_ref)

        @pl.when(s == 0)
        def _():
            dyh_ref[...] = (0.5 * dy_ref[...]).astype(BF16)
            da_ref[...] = jnp.zeros_like(da_ref)

        dact = _dot_nt(dyh_ref[...], wd_ref[...])
        gf = go_ref[...].astype(F32)
        uf = uo_ref[...].astype(F32)
        sg = _sigmoid(gf)
        du = (dact * (gf * sg)).astype(BF16)
        dg = (dact * uf * (sg * (1.0 + gf * (1.0 - sg)))).astype(BF16)
        dg_ref[...] = dg
        du_ref[...] = du
        da_ref[...] += _dot_nt(dg, wg_ref[...]) + _dot_nt(du, wu_ref[...])

        @pl.when(s == N_CHIPS - 1)
        def _():
            dx, dgain = _rms_bwd(h_ref[...], g_ref[...], da_ref[...])
            dgain_ref[...] += dgain
            dh_ref[...] = jnp.where(_row_mask(t, tm, pad, (tm, D)), dy_ref[...] + dx, 0.0)

    row = pl.BlockSpec((tm, D), lambda t, s: (t, 0))
    col = pl.BlockSpec((tm, Fs), lambda t, s: (t, s))
    wcol = pl.BlockSpec((None, D, Fs), lambda t, s: (s, 0, 0))
    return _call(
        body, name=name, grid=(T // tm, N_CHIPS),
        in_specs=[row, row, pl.BlockSpec((None, 1, D), lambda t, s: (layer, 0, 0)), col, col, wcol, wcol,
                  pl.BlockSpec((None, Fs, D), lambda t, s: (s, 0, 0))],
        out_specs=[row, col, col, pl.BlockSpec((1, D), lambda t, s: (0, 0))],
        out_shape=[jax.ShapeDtypeStruct((T, D), F32), jax.ShapeDtypeStruct((T, F), BF16),
                   jax.ShapeDtypeStruct((T, F), BF16), jax.ShapeDtypeStruct((1, D), F32)],
        scratch=[pltpu.VMEM((tm, D), BF16), pltpu.VMEM((tm, D), F32)],
        sem=("arbitrary", "arbitrary"), operands=(dy, h, gain, g, u, wg, wu, wd), rider=rider)


def _grad_tn(a, b, mode, scale, tm, name, rider=None):
    T = a.shape[0]
    if mode == "col":
        R, C = a.shape[1], b.shape[1] // N_CHIPS
        a_spec = pl.BlockSpec((tm, R), lambda s, t: (t, 0))
        b_spec = pl.BlockSpec((tm, C), lambda s, t: (t, s))
    else:
        R, C = a.shape[1] // N_CHIPS, b.shape[1]
        a_spec = pl.BlockSpec((tm, R), lambda s, t: (t, s))
        b_spec = pl.BlockSpec((tm, C), lambda s, t: (t, 0))
    nt = T // tm

    def body(a_ref, b_ref, o_ref, acc_ref):
        t = pl.program_id(1)

        @pl.when(t == 0)
        def _():
            acc_ref[...] = jnp.zeros_like(acc_ref)

        acc_ref[...] += _dot_tn(a_ref[...].astype(BF16), b_ref[...].astype(BF16))

        @pl.when(t == nt - 1)
        def _():
            o_ref[...] = (scale * acc_ref[...]).astype(BF16)

    return _call(
        body, name=name, grid=(N_CHIPS, nt),
        in_specs=[a_spec, b_spec],
        out_specs=[pl.BlockSpec((None, R, C), lambda s, t: (s, 0, 0))],
        out_shape=[jax.ShapeDtypeStruct((N_CHIPS, R, C), BF16)],
        scratch=[pltpu.VMEM((R, C), F32)],
        sem=("parallel", "arbitrary"), operands=(a, b), rider=rider)[0]


def _mix_bwd_dx(dh, z, ret, pool, wout, wru, wpu, tm, name, rider=None):
    T, D = dh.shape
    Dq = D // N_CHIPS
    nb = D // RET_WIDTH

    def body(*refs):
        dh_ref = refs[0]
        gate_refs = refs[1:1 + 2 * nb]
        ret_ref, pool_ref, wout_ref, wru_ref, wpu_ref, dgab_ref, dret_ref, dpool_ref, dr_ref, dpm_ref = refs[1 + 2 * nb:]
        dmixed = _dot_nt(dh_ref[...].astype(BF16), wout_ref[...].reshape(D, D))
        ga, gb = _load_gates(gate_refs, nb)
        sa = _sigmoid(ga)
        sb = _sigmoid(gb)
        dgab_ref[:, :D] = (dmixed * ret_ref[...].astype(F32) * (sa * (1.0 - sa))).astype(BF16)
        dgab_ref[:, D:] = (dmixed * pool_ref[...].astype(F32) * (sb * (1.0 - sb))).astype(BF16)
        dret = (dmixed * sa).astype(BF16)
        dpool = (dmixed * sb).astype(BF16)
        dret_ref[...] = dret
        dpool_ref[...] = dpool
        dr = _dot_nt(dret[:, :Dq], wru_ref[0])
        dpm = _dot_nt(dpool[:, :Dq], wpu_ref[0])
        for s in range(1, N_CHIPS):
            dr += _dot_nt(dret[:, s * Dq:(s + 1) * Dq], wru_ref[s])
            dpm += _dot_nt(dpool[:, s * Dq:(s + 1) * Dq], wpu_ref[s])
        dr_ref[...] = dr
        dpm_ref[...] = dpm

    row = pl.BlockSpec((tm, D), lambda t: (t, 0))
    half = pl.BlockSpec((tm, RET_WIDTH), lambda t: (t, 0))
    up = pl.BlockSpec((N_CHIPS, RET_WIDTH, Dq), lambda t: (0, 0, 0))
    return _call(
        body, name=name, grid=(T // tm,),
        in_specs=[row] + _gate_specs(tm, D) + [row, row, pl.BlockSpec((N_CHIPS, Dq, D), lambda t: (0, 0, 0)), up, up],
        out_specs=[pl.BlockSpec((tm, 2 * D), lambda t: (t, 0)), row, row, half, half],
        out_shape=[jax.ShapeDtypeStruct((T, 2 * D), BF16), jax.ShapeDtypeStruct((T, D), BF16),
                   jax.ShapeDtypeStruct((T, D), BF16), jax.ShapeDtypeStruct((T, RET_WIDTH), F32),
                   jax.ShapeDtypeStruct((T, POOL_WIDTH), F32)],
        sem=("parallel",), operands=(dh, *([z] * (2 * nb)), ret, pool, wout, wru, wpu), rider=rider)


def _pool_bwd(z, dpm, maps, scale, layer, pad, name):
    T = z.shape[0]

    def body(zu, maps_ref, sc_ref, dpm_ref, du_ref, dmaps_ref, dsc_ref):
        g = pl.program_id(0)
        u = zu[...]
        pooled, div, valid = _pool_parts(u, g, T, pad)
        pb = pooled.astype(BF16)
        mb = maps_ref[...].astype(BF16)
        dp = dpm_ref[...]
        dsc_ref[...] = jnp.sum(dp * _dot(pb, mb), axis=0, keepdims=True)
        dyb = (dp * sc_ref[...]).astype(BF16)
        dmaps_ref[...] = _dot_tn(pb, dyb)
        dpooled = jnp.where(valid, _dot_nt(dyb, mb), 0.0)
        ahead = _select_group(_window_sums(dpooled / div, lambda k: T - k), g)
        du_ref[...] = jnp.where(valid, ahead - dpooled, 0.0).astype(BF16)

    blk = pl.BlockSpec((T, HEAD_DIM), lambda g: (0, g))
    return _call(
        body, name=name, grid=(POOL_GROUPS,),
        in_specs=_pool_specs(T, layer) + [blk],
        out_specs=[blk, pl.BlockSpec((None, HEAD_DIM, HEAD_DIM), lambda g: (g, 0, 0)),
                   pl.BlockSpec((1, HEAD_DIM), lambda g: (0, g))],
        out_shape=[jax.ShapeDtypeStruct((T, POOL_WIDTH), BF16),
                   jax.ShapeDtypeStruct((POOL_GROUPS, HEAD_DIM, HEAD_DIM), F32),
                   jax.ShapeDtypeStruct((1, POOL_WIDTH), F32)],
        sem=("parallel",), operands=(z, maps, scale, dpm))


def _ret_bwd_local(z, o_pre, s_all, dr, consts, cg, name):
    T = z.shape[0]
    N = T // CHUNK
    ng = N // cg
    tg = cg * CHUNK
    cosf, sinf, intra, _, qdec, _ = consts
    fwd = lambda g: g

    def body(zq, zk, zv, zg, o_ref, s_ref, dr_ref, cos_ref, sin_ref, m_ref, qd_ref,
             dq_ref, dg_ref, dk_ref, dv_ref, ds_ref):
        cosv = cos_ref[...]
        sinv = sin_ref[...]
        scale = HEAD_DIM ** -0.5
        q3 = (_rot(zq[...], cosv, sinv) * scale).reshape(cg, CHUNK, HEAD_DIM)
        k3 = _rot(zk[...], cosv, sinv).reshape(cg, CHUNK, HEAD_DIM)
        qb = q3.astype(BF16)
        kb = k3.astype(BF16)
        vb = zv[...].reshape(cg, CHUNK, HEAD_DIM).astype(BF16)
        mask = m_ref[...][None]
        sb = (_ein("ncd,nmd->ncm", qb, kb) * mask).astype(BF16)
        qdv = qd_ref[...][None]
        qdb = (q3 * qdv).astype(BF16)

        out = o_ref[...]
        xc = out - jnp.mean(out, axis=-1, keepdims=True)
        rstd = lax.rsqrt(jnp.mean(xc * xc, axis=-1, keepdims=True) + EPS)
        rn = xc * rstd
        g = zg[...]
        sg = _sigmoid(g)
        drv = dr_ref[...]
        dg_ref[...] = (drv * rn * (sg * (1.0 + g * (1.0 - sg)))).astype(BF16)
        drn = drv * (g * sg)
        dout = rstd * (drn - jnp.mean(drn, axis=-1, keepdims=True)
                       - rn * jnp.mean(drn * rn, axis=-1, keepdims=True))
        dob = dout.reshape(cg, CHUNK, HEAD_DIM).astype(BF16)

        dsb = (_ein("ncd,nmd->ncm", dob, vb) * mask).astype(BF16)
        dv_ref[...] = _ein("ncm,ncd->nmd", sb, dob).reshape(tg, HEAD_DIM)
        dk_ref[...] = _ein("ncm,ncd->nmd", dsb, qb).reshape(tg, HEAD_DIM)
        dq3 = _ein("ncm,nmd->ncd", dsb, kb) + _ein("nce,nde->ncd", dob, s_ref[...].astype(BF16)) * qdv
        dq_ref[...] = _rot_t(dq3.reshape(tg, HEAD_DIM) * scale, cosv, sinv).astype(BF16)
        ds_ref[...] = _ein("ncd,nce->nde", qdb, dob)

    tab = pl.BlockSpec((tg, HEAD_DIM), lambda h, g: (g, 0))
    per_head = pl.BlockSpec((None, CHUNK, HEAD_DIM), lambda h, g: (h, 0, 0))
    head_blk = pl.BlockSpec((tg, HEAD_DIM), lambda h, g: (g, h))
    state_blk = pl.BlockSpec((None, cg, HEAD_DIM, HEAD_DIM), lambda h, g: (h, g, 0, 0))
    return _call(
        body, name=name, grid=(RET_HEADS, ng),
        in_specs=[_head_specs(tg, i, fwd) for i in range(4)]
        + [head_blk, state_blk, head_blk, tab, tab, per_head, per_head],
        out_specs=[head_blk, head_blk, head_blk, head_blk, state_blk],
        out_shape=[jax.ShapeDtypeStruct((T, RET_WIDTH), BF16), jax.ShapeDtypeStruct((T, RET_WIDTH), BF16),
                   jax.ShapeDtypeStruct((T, RET_WIDTH), F32), jax.ShapeDtypeStruct((T, RET_WIDTH), F32),
                   jax.ShapeDtypeStruct((RET_HEADS, N, HEAD_DIM, HEAD_DIM), F32)],
        sem=("parallel", "parallel"), operands=(z, z, z, z, o_pre, s_all, dr, cosf, sinf, intra, qdec))


def _ret_bwd_state(z, dkp, dvp, ds, consts, cg, name):
    T = z.shape[0]
    N = T // CHUNK
    ng = N // cg
    tg = cg * CHUNK
    cosf, sinf, _, kdec, _, cdb = consts
    rev = lambda g: ng - 1 - g

    def body(zk, zv, dkp_ref, dvp_ref, ds_ref, cos_ref, sin_ref, kd_ref, cd_ref, dk_ref, dv_ref, gs_ref, dkv_ref):
        @pl.when(pl.program_id(1) == 0)
        def _():
            gs_ref[...] = jnp.zeros_like(gs_ref)

        cosv = cos_ref[...]
        sinv = sin_ref[...]
        cd = cd_ref[0:1, :]
        grad = gs_ref[...]
        for n in reversed(range(cg)):
            dkv_ref[n] = grad
            grad = ds_ref[n] + cd * grad
        gs_ref[...] = grad
        dkvb = dkv_ref[...].astype(BF16)
        kdv = kd_ref[...][None]
        k3 = _rot(zk[...], cosv, sinv).reshape(cg, CHUNK, HEAD_DIM)
        vb = zv[...].reshape(cg, CHUNK, HEAD_DIM).astype(BF16)
        dk3 = _ein("nce,nde->ncd", vb, dkvb) * kdv
        dv3 = _ein("ncd,nde->nce", (k3 * kdv).astype(BF16), dkvb)
        dk_ref[...] = _rot_t(dkp_ref[...] + dk3.reshape(tg, HEAD_DIM), cosv, sinv).astype(BF16)
        dv_ref[...] = (dvp_ref[...] + dv3.reshape(tg, HEAD_DIM)).astype(BF16)

    tab = pl.BlockSpec((tg, HEAD_DIM), lambda h, g: (rev(g), 0))
    head_blk = pl.BlockSpec((tg, HEAD_DIM), lambda h, g: (rev(g), h))
    return _call(
        body, name=name, grid=(RET_HEADS, ng),
        in_specs=[_head_specs(tg, 1, rev), _head_specs(tg, 2, rev), head_blk, head_blk,
                  pl.BlockSpec((None, cg, HEAD_DIM, HEAD_DIM), lambda h, g: (h, rev(g), 0, 0)),
                  tab, tab,
                  pl.BlockSpec((None, CHUNK, HEAD_DIM), lambda h, g: (h, 0, 0)),
                  pl.BlockSpec((None, 8, HEAD_DIM), lambda h, g: (h, 0, 0))],
        out_specs=[head_blk, head_blk],
        out_shape=[jax.ShapeDtypeStruct((T, RET_WIDTH), BF16)] * 2,
        scratch=[pltpu.VMEM((HEAD_DIM, HEAD_DIM), F32), pltpu.VMEM((cg, HEAD_DIM, HEAD_DIM), F32)],
        sem=("parallel", "arbitrary"), operands=(z, z, dkp, dvp, ds, cosf, sinf, kdec, cdb))


def _inproj_bwd_dx(dz, win, h, gain, dh_in, layer, tm, pad, name, rider=None):
    T, D = h.shape
    Ns = win.shape[-1]

    def body(dz_ref, w_ref, h_ref, g_ref, dhi_ref, dh_ref, dgain_ref, db_ref):
        t = pl.program_id(0)
        s = pl.program_id(1)

        @pl.when((t == 0) & (s == 0))
        def _():
            dgain_ref[...] = jnp.zeros_like(dgain_ref)

        @pl.when(s == 0)
        def _():
            db_ref[...] = jnp.zeros_like(db_ref)

        db_ref[...] += _dot_nt(dz_ref[...], w_ref[...])

        @pl.when(s == N_CHIPS - 1)
        def _():
            dx, dgain = _rms_bwd(h_ref[...], g_ref[...], db_ref[...])
            dgain_ref[...] += dgain
            dh_ref[...] = jnp.where(_row_mask(t, tm, pad, (tm, D)), dhi_ref[...] + dx, 0.0)

    row = pl.BlockSpec((tm, D), lambda t, s: (t, 0))
    return _call(
        body, name=name, grid=(T // tm, N_CHIPS),
        in_specs=[pl.BlockSpec((tm, Ns), lambda t, s: (t, s)),
                  pl.BlockSpec((None, D, Ns), lambda t, s: (s, 0, 0)),
                  row, pl.BlockSpec((None, 1, D), lambda t, s: (layer, 0, 0)), row],
        out_specs=[row, pl.BlockSpec((1, D), lambda t, s: (0, 0))],
        out_shape=[jax.ShapeDtypeStruct((T, D), F32), jax.ShapeDtypeStruct((1, D), F32)],
        scratch=[pltpu.VMEM((tm, D), F32)],
        sem=("arbitrary", "arbitrary"), operands=(dz, win, h, gain, dh_in), rider=rider)


def _sum_pair(g, r, c_idx, name):
    _, R, C = g.shape
    rh = R // 2

    def body(c_ref, g_ref, r_ref, o_ref):
        o_ref[...] = (g_ref[...].astype(F32) + r_ref[...].astype(F32)).astype(BF16)

    blk = pl.BlockSpec((None, rh, C), lambda s, c_ref: (s, 0, 0))
    return pl.pallas_call(
        body,
        name=name,
        grid_spec=pltpu.PrefetchScalarGridSpec(
            num_scalar_prefetch=1,
            grid=(N_CHIPS,),
            in_specs=[pl.BlockSpec((None, rh, C), lambda s, c_ref: (s, c_ref[0], 0)), blk],
            out_specs=blk,
        ),
        out_shape=jax.ShapeDtypeStruct((N_CHIPS, rh, C), BF16),
        compiler_params=_params(("parallel",)),
    )(c_idx, g, r)


def _sum_chips(p, r, pos, name):
    _, rh, C = r.shape

    def body(pos_ref, p_ref, r_ref, o_ref):
        chip = pos_ref[0]
        own = p_ref[...].astype(F32)
        terms = [jnp.where(chip == k, own, r_ref[k].astype(F32)) for k in range(N_CHIPS)]
        o_ref[...] = ((terms[0] + terms[1]) + terms[2]) + terms[3]

    return pl.pallas_call(
        body,
        name=name,
        grid_spec=pltpu.PrefetchScalarGridSpec(
            num_scalar_prefetch=1,
            grid=(1,),
            in_specs=[pl.BlockSpec((None, rh, C), lambda i, pos_ref: (pos_ref[0], 0, 0)),
                      pl.BlockSpec((N_CHIPS, rh, C), lambda i, pos_ref: (0, 0, 0))],
            out_specs=pl.BlockSpec((rh, C), lambda i, pos_ref: (pos_ref[1], 0)),
        ),
        out_shape=jax.ShapeDtypeStruct((2 * rh, C), F32),
        compiler_params=_params(("arbitrary",)),
    )(pos, p, r)


def _small_all_reduce(p):
    rows, width = p.shape

    def body(p_ref, o_ref, sib_ref, slot_ref, ssem, rsem):
        x, y, c, chip, others = _mesh_pos()
        pair = _remote(p_ref, sib_ref, ssem.at[0], rsem.at[0], (x, y, 1 - c))
        pair.start()
        pair.wait()
        slot_ref[chip] = p_ref[...] + sib_ref[...]
        sends = []
        for j, (ox, oy) in enumerate(others):
            cp = _remote(slot_ref.at[chip], slot_ref.at[chip], ssem.at[1 + j], rsem.at[1 + j], (ox, oy, c))
            cp.start()
            sends.append(cp)
        for j, (ox, oy) in enumerate(others):
            slot = slot_ref.at[2 * ox + oy]
            _remote(slot, slot, ssem.at[1 + j], rsem.at[1 + j], (ox, oy, c)).wait_recv()
        for cp in sends:
            cp.wait_send()
        o_ref[...] = ((slot_ref[0] + slot_ref[1]) + slot_ref[2]) + slot_ref[3]

    vmem = pl.BlockSpec(memory_space=pltpu.VMEM)
    return pl.pallas_call(
        body,
        name="small_grads_all_reduce",
        in_specs=[vmem],
        out_specs=vmem,
        out_shape=jax.ShapeDtypeStruct(p.shape, F32),
        scratch_shapes=[pltpu.VMEM((rows, width), F32), pltpu.VMEM((N_CHIPS, rows, width), F32),
                        pltpu.SemaphoreType.DMA((4,)), pltpu.SemaphoreType.DMA((4,))],
    )(p)


def _adamw(gs, w, m, v, name):
    L, R, C = w.shape
    Ct = gs[0].shape[1]
    tr = _pick_tile(R, 256, 8)

    def body(*refs):
        g_refs = refs[:L]
        w_ref, m_ref, v_ref, go_ref, d_ref, mo_ref, vo_ref = refs[L:]
        layer = pl.program_id(0)
        grad = g_refs[L - 1][...]
        for i in range(L - 2, -1, -1):
            grad = jnp.where(layer == i, g_refs[i][...], grad)
        if Ct != C:
            grad = grad[:, :C]
        m_new = ADAM_B1 * m_ref[...] + (1.0 - ADAM_B1) * grad
        v_new = ADAM_B2 * v_ref[...] + (1.0 - ADAM_B2) * jnp.square(grad)
        m_hat = m_new / (1.0 - ADAM_B1 ** ADAM_STEP)
        v_hat = v_new / (1.0 - ADAM_B2 ** ADAM_STEP)
        go_ref[...] = grad
        d_ref[...] = -ADAM_LR * (m_hat / (jnp.sqrt(v_hat) + ADAM_EPS) + ADAM_WD * w_ref[...])
        mo_ref[...] = m_new
        vo_ref[...] = v_new

    g_specs = [pl.BlockSpec((tr, Ct), functools.partial(lambda l, r, i: (jnp.where(l == i, r, 0), 0), i=i))
               for i in range(L)]
    blk = pl.BlockSpec((None, tr, C), lambda l, r: (l, r, 0))
    return pl.pallas_call(
        body,
        name=name,
        grid=(L, R // tr),
        in_specs=g_specs + [blk, blk, blk],
        out_specs=[blk] * 4,
        out_shape=[jax.ShapeDtypeStruct((L, R, C), F32)] * 4,
        compiler_params=_params(("arbitrary", "arbitrary")),
    )(*gs, w, m, v)


_FFN1 = ("ffn1_gate", "ffn1_up", "ffn1_down")
_FFN2 = ("ffn2_gate", "ffn2_up", "ffn2_down")
_MIXW = ("w_ret_up", "w_pool_up", "w_out")
_BIG = _FFN1 + ("w_in",) + _MIXW + _FFN2
_SMALL = ("ffn1_norm", "mix_norm", "ffn2_norm", "final_norm", "pool_scale", "pool_maps")
_ORDER = ("meta", "ffn1_norm", "ffn1_gate", "ffn1_up", "ffn1_down", "mix_norm", "w_in", "pool_maps",
          "pool_scale", "w_ret_up", "w_pool_up", "w_out", "ffn2_norm", "ffn2_gate", "ffn2_up", "ffn2_down",
          "final_norm")


def _transport(a):
    r, c = a.shape[1], a.shape[2]
    return jnp.pad(a.astype(BF16), ((0, 0), (0, _round_up(r, LANES) - r), (0, _round_up(c, LANES) - c)))


def _pack_rows(parts, width):
    rows = [p.reshape(-1, width) for p in parts]
    total = sum(r.shape[0] for r in rows)
    fill = _round_up(total, 8) - total
    if fill:
        rows.append(jnp.zeros((fill, width), F32))
    return jnp.concatenate(rows, axis=0)


def _unpack_rows(packed, shapes, width):
    out, at = [], 0
    for shp in shapes:
        n = math.prod(shp) // width
        out.append(packed[at:at + n].reshape(shp))
        at += n
    return out


class _Weights:
    def __init__(self, shards):
        self.shards = shards
        self.full = {}

    def rider(self, keys):
        r = _gather_rider([(self.shards[n], i) for n, i in keys])
        r.keys = keys
        return r

    def take(self, rider):
        for key, arr in zip(rider.keys, rider.results):
            self.full[key] = arr

    def __call__(self, name, layer):
        return self.full[(name, layer)]


def _local_step(x, meta_full, tgt, w, wts, pad, tm, cg, reducer):
    D = x.shape[1]
    T = pad + N_META + x.shape[0]
    L = w["ffn1_norm"].shape[0]
    pool_maps = w["pool_maps"]
    gains = {n: w[n].reshape(L, 1, D) for n in ("ffn1_norm", "mix_norm", "ffn2_norm")}
    scale3 = w["pool_scale"].reshape(L, 1, POOL_WIDTH)
    consts = _ret_consts(T, pad)

    def gather(keys):
        return wts.rider(keys) if keys and keys[0] not in wts.full else None

    def done(rider):
        if rider is not None:
            wts.take(rider)

    h = jnp.concatenate([jnp.zeros((pad, D), F32), meta_full, x], axis=0)
    saved = []
    for i in range(L):
        s = {"h0": h}
        rd = gather([("w_in", i)] + [(n, i) for n in _MIXW])
        h, s["a1"], s["g1"], s["u1"], s["act1"] = _ffn_fwd(
            h, gains["ffn1_norm"], wts("ffn1_gate", i), wts("ffn1_up", i), wts("ffn1_down", i), i, tm,
            f"ffn1_fwd_{i}", rd)
        done(rd)
        s["h1"] = h
        rd = gather([("ffn2_gate", i), ("ffn2_up", i)])
        s["z"], s["b"] = _inproj_fwd(h, gains["mix_norm"], wts("w_in", i), i, tm, f"inproj_fwd_{i}", rd)
        done(rd)
        s["r"], s["o_pre"], s["s_all"] = _ret_fwd(s["z"], consts, cg, f"retention_fwd_{i}")
        s["pm"] = _pool_fwd(s["z"], pool_maps, scale3, i, pad, f"pool_fwd_{i}")
        rd = gather([("ffn2_down", i)])
        h, s["mixed"], s["ret"], s["pool"] = _mix_fwd(
            h, s["r"], s["pm"], s["z"], wts("w_ret_up", i), wts("w_pool_up", i), wts("w_out", i), tm,
            f"mix_fwd_{i}", rd)
        done(rd)
        s["h2"] = h
        rd = gather([(n, i + 1) for n in _FFN1]) if i + 1 < L else None
        h, s["a2"], s["g2"], s["u2"], s["act2"] = _ffn_fwd(
            h, gains["ffn2_norm"], wts("ffn2_gate", i), wts("ffn2_up", i), wts("ffn2_down", i), i, tm,
            f"ffn2_fwd_{i}", rd)
        done(rd)
        saved.append(s)

    dh, loss_acc, d_final = _final_loss(h, w["final_norm"].reshape(1, D), tgt, "final_norm_loss")

    small = {n: [None] * L for n in ("ffn1_norm", "mix_norm", "ffn2_norm", "pool_scale", "pool_maps")}

    carry = {"ffn_bwd": 4.0, "mix_bwd": 1.0, "inproj_bwd": 2.0, "w_in": 2.0, "w_out": 0.5, "w_ret_up": 0.5,
             "w_pool_up": 0.5}

    def grad(n, a, b, i, mode, scale=1.0):
        rd = reducer.rider(carry.get(n, 1.5))
        reducer.add(n, i, _grad_tn(a, b, mode, scale, tm, f"grad_{n}_{i}", rd))
        reducer.done(rd)

    for i in reversed(range(L)):
        s = saved[i]
        dh3 = dh
        rd = reducer.rider(carry["ffn_bwd"])
        dh, dg, du, small["ffn2_norm"][i] = _ffn_bwd_dx(
            dh3, s["h2"], gains["ffn2_norm"], s["g2"], s["u2"], wts("ffn2_gate", i), wts("ffn2_up", i),
            wts("ffn2_down", i), i, tm, pad, f"ffn2_bwd_{i}", rd)
        reducer.done(rd)
        grad("ffn2_gate", s["a2"], dg, i, "col")
        grad("ffn2_up", s["a2"], du, i, "col")
        grad("ffn2_down", s["act2"], dh3, i, "row", 0.5)
        rd = reducer.rider(carry["mix_bwd"])
        dgab, dret, dpool, dr, dpm = _mix_bwd_dx(
            dh, s["z"], s["ret"], s["pool"], wts("w_out", i), wts("w_ret_up", i), wts("w_pool_up", i), tm,
            f"mix_bwd_{i}", rd)
        reducer.done(rd)
        grad("w_out", s["mixed"], dh, i, "row")
        grad("w_ret_up", s["r"], dret, i, "col")
        grad("w_pool_up", s["pm"], dpool, i, "col")
        du_pool, small["pool_maps"][i], small["pool_scale"][i] = _pool_bwd(
            s["z"], dpm, pool_maps, scale3, i, pad, f"pool_bwd_{i}")
        dq, dgr, dkp, dvp, ds = _ret_bwd_local(s["z"], s["o_pre"], s["s_all"], dr, consts, cg, f"retention_bwd_{i}")
        dk, dv = _ret_bwd_state(s["z"], dkp, dvp, ds, consts, cg, f"retention_bwd_state_{i}")
        dz = jnp.concatenate([dq, dk, dv, dgr, du_pool, dgab], axis=1)
        dh2 = dh
        rd = reducer.rider(carry["inproj_bwd"])
        dh, small["mix_norm"][i] = _inproj_bwd_dx(
            dz, wts("w_in", i), s["h1"], gains["mix_norm"], dh2, i, tm, pad, f"inproj_bwd_{i}", rd)
        reducer.done(rd)
        grad("w_in", s["b"], dz, i, "col")
        reducer.stage(f"mid{i}")
        dh1 = dh
        rd = reducer.rider(carry["ffn_bwd"])
        dh, dg, du, small["ffn1_norm"][i] = _ffn_bwd_dx(
            dh1, s["h0"], gains["ffn1_norm"], s["g1"], s["u1"], wts("ffn1_gate", i), wts("ffn1_up", i),
            wts("ffn1_down", i), i, tm, pad, f"ffn1_bwd_{i}", rd)
        reducer.done(rd)
        grad("ffn1_gate", s["a1"], dg, i, "col")
        grad("ffn1_up", s["a1"], du, i, "col")
        grad("ffn1_down", s["act1"], dh1, i, "row", 0.5)
        reducer.stage(f"end{i}")

    return loss_acc, dh, small, d_final


class _Reducer:
    def __init__(self, unit):
        self.c_idx = lax.axis_index("c").astype(jnp.int32).reshape(1)
        chip = 2 * lax.axis_index("x") + lax.axis_index("y")
        self.pos = jnp.stack([chip, lax.axis_index("c")]).astype(jnp.int32)
        self.pending, self.queue, self.halves = [], [], {}
        self.unit = unit

    def add(self, name, layer, g):
        self.pending.append(((name, layer), g))

    def stage(self, tag):
        if not self.pending:
            return
        keys = [k for k, _ in self.pending]
        gs = [g for _, g in self.pending]
        self.pending = []
        from_sibling = _run_rider(_pair_exchange_rider(gs), f"grads_pair_exchange_{tag}")
        for key, g, r in zip(keys, gs, from_sibling):
            self.queue.append((key, _sum_pair(g, r, self.c_idx, f"sum_pair_{key[0]}_{key[1]}")))

    def rider(self, units):
        take, size = [], 0
        while self.queue and size + self.queue[0][1].size <= units * self.unit:
            take.append(self.queue.pop(0))
            size += take[-1][1].size
        if not take:
            return None
        rd = _chip_exchange_rider([p for _, p in take])
        rd.keys = [k for k, _ in take]
        return rd

    def done(self, rd):
        if rd is None:
            return
        for key, p, r in zip(rd.keys, rd.ins, rd.results):
            self.halves[key] = _sum_chips(p, r, self.pos, f"sum_chips_{key[0]}_{key[1]}")

    def finish(self):
        assert not self.pending
        if self.queue:
            rd = _chip_exchange_rider([p for _, p in self.queue])
            rd.keys = [k for k, _ in self.queue]
            self.queue = []
            _run_rider(rd, "grads_chip_exchange_tail")
            self.done(rd)
        keys = list(self.halves)
        return dict(zip(keys, _pair_gather([self.halves[k] for k in keys])))


def _update(loss, grad_x, d_meta_rows, shard_grads, small, d_final, w, mom, var):
    meta = w["meta"]
    D = w["final_norm"].shape[0]
    L = w["ffn1_norm"].shape[0]
    Dq = D // N_CHIPS

    small_parts = [jnp.concatenate(small[n], axis=0) for n in ("ffn1_norm", "mix_norm", "ffn2_norm")]
    small_parts += [d_final, jnp.concatenate(small["pool_scale"], axis=0), jnp.concatenate(small["pool_maps"], axis=0)]
    reduced = _small_all_reduce(_pack_rows(small_parts + [d_meta_rows], D))
    small_shapes = [w[n].shape for n in _SMALL]
    small_rows = sum(math.prod(shp) for shp in small_shapes) // D
    chip = 2 * lax.axis_index("x") + lax.axis_index("y")
    d_meta = lax.dynamic_slice_in_dim(reduced[small_rows:small_rows + N_META], chip * Dq, Dq, axis=1)

    out = {}
    for n in _BIG:
        out[n] = _adamw([shard_grads[(n, i)] for i in range(L)], w[n], mom[n], var[n], f"adamw_{n}")
    names = _SMALL + ("meta",)
    packed_g = _pack_rows([reduced[:small_rows], d_meta], D)
    packed = [_pack_rows([t[n] for n in names], D) for t in (w, mom, var)]
    res = _adamw([packed_g], packed[0][None], packed[1][None], packed[2][None], "adamw_small")
    shapes = small_shapes + [meta.shape]
    unpacked = [_unpack_rows(r[0], shapes, D) for r in res]
    for k, n in enumerate(names):
        out[n] = tuple(u[k] for u in unpacked)

    return (loss, grad_x) + tuple(out[n][j] for j in range(4) for n in _ORDER)


def kernel(x, meta, ffn1_norm, ffn1_gate, ffn1_up, ffn1_down, mix_norm, w_in, pool_maps, pool_scale, w_ret_up, w_pool_up, w_out, ffn2_norm, ffn2_gate, ffn2_up, ffn2_down, final_norm, loss_target, m_meta, m_ffn1_norm, m_ffn1_gate, m_ffn1_up, m_ffn1_down, m_mix_norm, m_w_in, m_pool_maps, m_pool_scale, m_w_ret_up, m_w_pool_up, m_w_out, m_ffn2_norm, m_ffn2_gate, m_ffn2_up, m_ffn2_down, m_final_norm, v_meta, v_ffn1_norm, v_ffn1_gate, v_ffn1_up, v_ffn1_down, v_mix_norm, v_w_in, v_pool_maps, v_pool_scale, v_w_ret_up, v_w_pool_up, v_w_out, v_ffn2_norm, v_ffn2_gate, v_ffn2_up, v_ffn2_down, v_final_norm):
    args = dict(locals())
    w = {n: args[n] for n in _ORDER}
    mom = {n: args["m_" + n] for n in _ORDER}
    var = {n: args["v_" + n] for n in _ORDER}

    assert x.shape[0] == 1, "one batch element per device"
    seq, D = x.shape[1], x.shape[2]
    assert seq % CHUNK == 0 and D % RET_WIDTH == 0 and (2 * POOL_WIDTH) % D == 0
    pad = (-(seq + N_META)) % CHUNK
    T = seq + N_META + pad
    tm = _pick_tile(T, 528, BF16_ROWS)
    cg = _pick_tile(T // CHUNK, 11, 1)

    shards = {n: _transport(w[n]) for n in _BIG}
    shards["meta"] = meta[None]
    wts = _Weights(shards)
    head = wts.rider([(n, 0) for n in _FFN1] + [("meta", 0)])
    _run_rider(head, "weights_gather_head")
    wts.take(head)
    meta_full = jnp.transpose(wts("meta", 0), (1, 0, 2)).reshape(N_META, D)

    reducer = _Reducer(unit=2 * shards["ffn1_gate"][0].size)
    loss_acc, dh, small, d_final = _local_step(x[0], meta_full, loss_target[0], w, wts, pad, tm, cg, reducer)
    loss = lax.psum(loss_acc[0, 0], ("x", "y", "c"))
    grad_x = dh[pad + N_META:][None]
    return _update(loss, grad_x, dh[pad:pad + N_META], reducer.finish(), small, d_final, w, mom, var)
```

```python
import functools
import math

import jax
import jax.numpy as jnp
from jax import lax
from jax.experimental import pallas as pl
from jax.experimental.pallas import tpu as pltpu

F32 = jnp.float32
BF16 = jnp.bfloat16

N_META = 16
RET_HEADS = 4
HEAD_DIM = 128
RET_WIDTH = RET_HEADS * HEAD_DIM
POOL_WINDOWS = (2, 4, 8, 16)
POOL_GROUPS = len(POOL_WINDOWS)
POOL_WIDTH = POOL_GROUPS * HEAD_DIM
CHUNK = 128
ROPE_BASE = 10000.0
EPS = 1e-6
ADAM_LR = 0.001
ADAM_B1 = 0.9
ADAM_B2 = 0.999
ADAM_EPS = 1e-08
ADAM_WD = 0.01
ADAM_STEP = 10

N_CHIPS = 4
LANES = 128
BF16_ROWS = 16
V7X_VMEM_LIMIT = 52 * 1024 * 1024
MESH = pl.DeviceIdType.MESH
ANY = pl.BlockSpec(memory_space=pl.ANY)


def _round_up(n, m):
    return -(-n // m) * m


def _pick_tile(n, target, mult):
    best = None
    for d in range(mult, min(n, target) + 1, mult):
        if n % d == 0:
            best = d
    assert best is not None, (n, target, mult)
    return best


def _params(sem=None):
    return pltpu.CompilerParams(dimension_semantics=sem, vmem_limit_bytes=V7X_VMEM_LIMIT)


def _dot(a, b):
    return jnp.dot(a, b, preferred_element_type=F32)


def _dot_nt(a, b):
    return lax.dot_general(a, b, (((1,), (1,)), ((), ())), preferred_element_type=F32)


def _dot_tn(a, b):
    return lax.dot_general(a, b, (((0,), (0,)), ((), ())), preferred_element_type=F32)


def _ein(spec, a, b):
    return jnp.einsum(spec, a, b, preferred_element_type=F32)


def _sigmoid(x):
    return jax.nn.sigmoid(x)


def _rms_fwd(x, gain):
    r = lax.rsqrt(jnp.mean(x * x, axis=-1, keepdims=True) + EPS)
    return x * r * gain


def _rms_bwd(x, gain, da):
    r = lax.rsqrt(jnp.mean(x * x, axis=-1, keepdims=True) + EPS)
    xh = x * r
    dgain = jnp.sum(da * xh, axis=0, keepdims=True)
    dxh = da * gain
    dx = r * (dxh - xh * jnp.mean(dxh * xh, axis=-1, keepdims=True))
    return dx, dgain


def _row_mask(t, tm, pad, shape):
    rows = t * tm + lax.broadcasted_iota(jnp.int32, shape, 0)
    return rows >= pad


def _mesh_pos():
    x, y, c = lax.axis_index("x"), lax.axis_index("y"), lax.axis_index("c")
    others = [(1 - x, y), (x, 1 - y), (1 - x, 1 - y)]
    return x, y, c, 2 * x + y, others


def _half_rows(c, rh):
    return pl.ds(pl.multiple_of(c * rh, rh), rh)


def _remote(src, dst, ssem, rsem, dev):
    return pltpu.make_async_remote_copy(src_ref=src, dst_ref=dst, send_sem=ssem, recv_sem=rsem,
                                        device_id=dev, device_id_type=MESH)


class _Rider:
    def __init__(self, ins, out_shapes, n_sem, start, finish):
        self.ins, self.out_shapes, self.n_sem, self.start, self.finish = ins, out_shapes, n_sem, start, finish
        self.results = None


def _gather_rider(pieces):
    per = 7
    layers = [layer for _, layer in pieces]

    def first_copies(ins, outs, ssem, rsem):
        x, y, c, chip, others = _mesh_pos()
        copies = []
        for i, layer in enumerate(layers):
            mine = _half_rows(c, ins[i].shape[1] // 2)
            for j, (ox, oy) in enumerate(others):
                copies.append(_remote(ins[i].at[layer, mine, :], outs[i].at[chip, mine, :],
                                      ssem.at[per * i + j], rsem.at[per * i + j], (ox, oy, c)))
            copies.append(_remote(ins[i].at[layer], outs[i].at[chip],
                                  ssem.at[per * i + 6], rsem.at[per * i + 6], (x, y, 1 - c)))
        return copies

    def start(ins, outs, ssem, rsem):
        for cp in first_copies(ins, outs, ssem, rsem):
            cp.start()

    def finish(ins, outs, ssem, rsem):
        x, y, c, chip, others = _mesh_pos()
        sibling = (x, y, 1 - c)
        forwards = []
        for i in range(len(layers)):
            mine = _half_rows(c, ins[i].shape[1] // 2)
            for j, (ox, oy) in enumerate(others):
                rows = outs[i].at[2 * ox + oy, mine, :]
                _remote(rows, rows, ssem.at[per * i + j], rsem.at[per * i + j], (ox, oy, c)).wait_recv()
                fwd = _remote(rows, rows, ssem.at[per * i + 3 + j], rsem.at[per * i + 3 + j], sibling)
                fwd.start()
                forwards.append(fwd)
        for i in range(len(layers)):
            theirs = _half_rows(1 - c, ins[i].shape[1] // 2)
            for j, (ox, oy) in enumerate(others):
                rows = outs[i].at[2 * ox + oy, theirs, :]
                _remote(rows, rows, ssem.at[per * i + 3 + j], rsem.at[per * i + 3 + j], sibling).wait_recv()
            own = outs[i].at[chip]
            _remote(own, own, ssem.at[per * i + 6], rsem.at[per * i + 6], sibling).wait_recv()
        for cp in first_copies(ins, outs, ssem, rsem) + forwards:
            cp.wait_send()

    shapes = [jax.ShapeDtypeStruct((N_CHIPS,) + s.shape[1:], s.dtype) for s, _ in pieces]
    return _Rider([s for s, _ in pieces], shapes, per * len(pieces), start, finish)


def _chip_exchange_rider(ps):
    def copies(ins, outs, ssem, rsem):
        x, y, c, chip, others = _mesh_pos()
        return [_remote(ins[i].at[2 * ox + oy], outs[i].at[chip], ssem.at[3 * i + j], rsem.at[3 * i + j], (ox, oy, c))
                for i in range(len(ps)) for j, (ox, oy) in enumerate(others)]

    def start(ins, outs, ssem, rsem):
        for cp in copies(ins, outs, ssem, rsem):
            cp.start()

    def finish(ins, outs, ssem, rsem):
        x, y, c, chip, others = _mesh_pos()
        for i in range(len(ps)):
            for j, (ox, oy) in enumerate(others):
                slot = outs[i].at[2 * ox + oy]
                _remote(slot, slot, ssem.at[3 * i + j], rsem.at[3 * i + j], (ox, oy, c)).wait_recv()
        for cp in copies(ins, outs, ssem, rsem):
            cp.wait_send()

    return _Rider(list(ps), [jax.ShapeDtypeStruct(p.shape, p.dtype) for p in ps], 3 * len(ps), start, finish)


def _pair_exchange_rider(gs):
    def copies(ins, outs, ssem, rsem):
        x, y, c, _, _ = _mesh_pos()
        return [_remote(ins[i].at[:, _half_rows(1 - c, ins[i].shape[1] // 2), :], outs[i],
                        ssem.at[i], rsem.at[i], (x, y, 1 - c)) for i in range(len(gs))]

    def start(ins, outs, ssem, rsem):
        for cp in copies(ins, outs, ssem, rsem):
            cp.start()

    def finish(ins, outs, ssem, rsem):
        for cp in copies(ins, outs, ssem, rsem):
            cp.wait()

    shapes = [jax.ShapeDtypeStruct((g.shape[0], g.shape[1] // 2, g.shape[2]), g.dtype) for g in gs]
    return _Rider(list(gs), shapes, len(gs), start, finish)


def _run_rider(rider, name):
    def body(*refs):
        n = len(rider.ins)
        ins, outs = refs[:n], refs[n:2 * n]
        ssem, rsem = refs[2 * n:]
        rider.start(ins, outs, ssem, rsem)
        rider.finish(ins, outs, ssem, rsem)

    rider.results = pl.pallas_call(
        body,
        name=name,
        in_specs=[ANY] * len(rider.ins),
        out_specs=[ANY] * len(rider.ins),
        out_shape=rider.out_shapes,
        scratch_shapes=[pltpu.SemaphoreType.DMA((rider.n_sem,)), pltpu.SemaphoreType.DMA((rider.n_sem,))],
    )(*rider.ins)
    return rider.results


def _pair_gather(fs):
    n = len(fs)

    def body(*refs):
        bufs = refs[n:2 * n]
        ssem, rsem = refs[2 * n:]
        x, y, c, _, _ = _mesh_pos()
        sends = []
        for i in range(n):
            rh = bufs[i].shape[0] // 2
            mine = bufs[i].at[_half_rows(c, rh), :]
            cp = _remote(mine, mine, ssem.at[i], rsem.at[i], (x, y, 1 - c))
            cp.start()
            sends.append(cp)
        for i in range(n):
            rh = bufs[i].shape[0] // 2
            theirs = bufs[i].at[_half_rows(1 - c, rh), :]
            _remote(theirs, theirs, ssem.at[i], rsem.at[i], (x, y, 1 - c)).wait_recv()
        for cp in sends:
            cp.wait_send()

    return pl.pallas_call(
        body,
        name="grads_pair_gather",
        in_specs=[ANY] * n,
        out_specs=[ANY] * n,
        out_shape=[jax.ShapeDtypeStruct(f.shape, f.dtype) for f in fs],
        input_output_aliases={i: i for i in range(n)},
        scratch_shapes=[pltpu.SemaphoreType.DMA((n,)), pltpu.SemaphoreType.DMA((n,))],
    )(*fs)


def _call(body, *, name, grid, in_specs, out_specs, out_shape, operands, scratch=(), sem=None, rider=None):
    if rider is None:
        return pl.pallas_call(
            body, name=name, grid=grid, in_specs=in_specs, out_specs=out_specs, out_shape=out_shape,
            scratch_shapes=list(scratch), compiler_params=_params(sem))(*operands)
    n_in, n_out, n_sc, r = len(in_specs), len(out_specs), len(scratch), len(rider.ins)

    def carrying(*refs):
        a, b = n_in, n_in + r
        c, d = b + n_out, b + n_out + r
        e = d + n_sc
        ids = [pl.program_id(k) for k in range(len(grid))]
        first = functools.reduce(jnp.logical_and, [i == 0 for i in ids])
        last = functools.reduce(jnp.logical_and, [i == g - 1 for i, g in zip(ids, grid)])

        @pl.when(first)
        def _():
            rider.start(refs[a:b], refs[c:d], refs[e], refs[e + 1])

        body(*refs[:a], *refs[b:c], *refs[d:e])

        @pl.when(last)
        def _():
            rider.finish(refs[a:b], refs[c:d], refs[e], refs[e + 1])

    outs = pl.pallas_call(
        carrying, name=name, grid=grid,
        in_specs=list(in_specs) + [ANY] * r,
        out_specs=list(out_specs) + [ANY] * r,
        out_shape=list(out_shape) + list(rider.out_shapes),
        scratch_shapes=list(scratch) + [pltpu.SemaphoreType.DMA((rider.n_sem,)), pltpu.SemaphoreType.DMA((rider.n_sem,))],
        compiler_params=_params(("arbitrary",) * len(grid)),
    )(*operands, *rider.ins)
    rider.results = outs[n_out:]
    return outs[:n_out]


def _ffn_fwd(h, gain, wg, wu, wd, layer, tm, name, rider=None):
    T, D = h.shape
    Fs = wg.shape[-1]
    F = N_CHIPS * Fs

    def body(h_ref, g_ref, wg_ref, wu_ref, wd_ref, ho_ref, a_ref, go_ref, uo_ref, act_ref, acc_ref):
        s = pl.program_id(1)

        @pl.when(s == 0)
        def _():
            a_ref[...] = _rms_fwd(h_ref[...], g_ref[...]).astype(BF16)
            acc_ref[...] = jnp.zeros_like(acc_ref)

        a = a_ref[...]
        g = _dot(a, wg_ref[...])
        u = _dot(a, wu_ref[...])
        act = (g * _sigmoid(g) * u).astype(BF16)
        go_ref[...] = g.astype(BF16)
        uo_ref[...] = u.astype(BF16)
        act_ref[...] = act
        acc_ref[...] += _dot(act, wd_ref[...])

        @pl.when(s == N_CHIPS - 1)
        def _():
            ho_ref[...] = h_ref[...] + 0.5 * acc_ref[...]

    row = pl.BlockSpec((tm, D), lambda t, s: (t, 0))
    col = pl.BlockSpec((tm, Fs), lambda t, s: (t, s))
    wcol = pl.BlockSpec((None, D, Fs), lambda t, s: (s, 0, 0))
    return _call(
        body, name=name, grid=(T // tm, N_CHIPS),
        in_specs=[row, pl.BlockSpec((None, 1, D), lambda t, s: (layer, 0, 0)), wcol, wcol,
                  pl.BlockSpec((None, Fs, D), lambda t, s: (s, 0, 0))],
        out_specs=[row, row, col, col, col],
        out_shape=[jax.ShapeDtypeStruct((T, D), F32), jax.ShapeDtypeStruct((T, D), BF16)]
        + [jax.ShapeDtypeStruct((T, F), BF16)] * 3,
        scratch=[pltpu.VMEM((tm, D), F32)],
        sem=("parallel", "arbitrary"), operands=(h, gain, wg, wu, wd), rider=rider)


def _inproj_fwd(h, gain, win, layer, tm, name, rider=None):
    T, D = h.shape
    Ns = win.shape[-1]

    def body(h_ref, g_ref, w_ref, z_ref, b_ref):
        @pl.when(pl.program_id(1) == 0)
        def _():
            b_ref[...] = _rms_fwd(h_ref[...], g_ref[...]).astype(BF16)

        z_ref[...] = _dot(b_ref[...], w_ref[...])

    return _call(
        body, name=name, grid=(T // tm, N_CHIPS),
        in_specs=[pl.BlockSpec((tm, D), lambda t, s: (t, 0)),
                  pl.BlockSpec((None, 1, D), lambda t, s: (layer, 0, 0)),
                  pl.BlockSpec((None, D, Ns), lambda t, s: (s, 0, 0))],
        out_specs=[pl.BlockSpec((tm, Ns), lambda t, s: (t, s)), pl.BlockSpec((tm, D), lambda t, s: (t, 0))],
        out_shape=[jax.ShapeDtypeStruct((T, N_CHIPS * Ns), F32), jax.ShapeDtypeStruct((T, D), BF16)],
        sem=("parallel", "arbitrary"), operands=(h, gain, win), rider=rider)


def _ret_consts(T, pad):
    half = HEAD_DIM // 2
    inv_freq = ROPE_BASE ** (-jnp.arange(half, dtype=F32) / half)
    pos = jnp.arange(T, dtype=F32) - pad
    ang = pos[:, None] * inv_freq[None, :]
    cos = jnp.cos(ang)
    sin = jnp.sin(ang)
    cosf = jnp.concatenate([cos, cos], axis=1)
    sinf = jnp.concatenate([-sin, sin], axis=1)
    log_gamma = jnp.log1p(-(2.0 ** (-5.0 - jnp.arange(RET_HEADS, dtype=F32))))
    idx = jnp.arange(CHUNK, dtype=F32)
    diff = idx[:, None] - idx[None, :]
    intra = jnp.where(diff[None] >= 0, jnp.exp(diff[None] * log_gamma[:, None, None]), 0.0)
    k_decay = jnp.exp((CHUNK - 1.0 - idx)[None, :] * log_gamma[:, None])
    q_decay = jnp.exp((idx + 1.0)[None, :] * log_gamma[:, None])
    chunk_decay = jnp.exp(CHUNK * log_gamma)
    kdec = jnp.broadcast_to(k_decay[:, :, None], (RET_HEADS, CHUNK, HEAD_DIM))
    qdec = jnp.broadcast_to(q_decay[:, :, None], (RET_HEADS, CHUNK, HEAD_DIM))
    cdb = jnp.broadcast_to(chunk_decay[:, None, None], (RET_HEADS, 8, HEAD_DIM))
    return cosf, sinf, intra, kdec, qdec, cdb


def _rot(t, cosv, sinv):
    return t * cosv + pltpu.roll(t, HEAD_DIM // 2, 1) * sinv


def _rot_t(g, cosv, sinv):
    return g * cosv + pltpu.roll(g * sinv, HEAD_DIM // 2, 1)


def _head_specs(tg, section, order):
    return pl.BlockSpec((tg, HEAD_DIM), lambda h, g: (order(g), section * RET_HEADS + h))


def _ret_fwd(z, consts, cg, name):
    T = z.shape[0]
    N = T // CHUNK
    ng = N // cg
    tg = cg * CHUNK
    cosf, sinf, intra, kdec, qdec, cdb = consts
    fwd = lambda g: g

    def body(zq, zk, zv, zg, cos_ref, sin_ref, m_ref, kd_ref, qd_ref, cd_ref, r_ref, o_ref, s_ref, st_ref):
        @pl.when(pl.program_id(1) == 0)
        def _():
            st_ref[...] = jnp.zeros_like(st_ref)

        cosv = cos_ref[...]
        sinv = sin_ref[...]
        q3 = (_rot(zq[...], cosv, sinv) * (HEAD_DIM ** -0.5)).reshape(cg, CHUNK, HEAD_DIM)
        k3 = _rot(zk[...], cosv, sinv).reshape(cg, CHUNK, HEAD_DIM)
        vb = zv[...].reshape(cg, CHUNK, HEAD_DIM).astype(BF16)
        scores = _ein("ncd,nmd->ncm", q3.astype(BF16), k3.astype(BF16)) * m_ref[...][None]
        inner = _ein("ncm,nmd->ncd", scores.astype(BF16), vb)
        kv = _ein("ncd,nce->nde", (k3 * kd_ref[...][None]).astype(BF16), vb)
        cd = cd_ref[0:1, :]
        state = st_ref[...]
        for n in range(cg):
            s_ref[n] = state
            state = state * cd + kv[n]
        st_ref[...] = state
        qdb = (q3 * qd_ref[...][None]).astype(BF16)
        cross = _ein("ncd,nde->nce", qdb, s_ref[...].astype(BF16))
        out = (inner + cross).reshape(tg, HEAD_DIM)
        o_ref[...] = out
        xc = out - jnp.mean(out, axis=-1, keepdims=True)
        rn = xc * lax.rsqrt(jnp.mean(xc * xc, axis=-1, keepdims=True) + EPS)
        g = zg[...]
        r_ref[...] = (rn * (g * _sigmoid(g))).astype(BF16)

    tab = pl.BlockSpec((tg, HEAD_DIM), lambda h, g: (g, 0))
    per_head = lambda rows: pl.BlockSpec((None, rows, HEAD_DIM), lambda h, g: (h, 0, 0))
    head_out = pl.BlockSpec((tg, HEAD_DIM), lambda h, g: (g, h))
    return _call(
        body, name=name, grid=(RET_HEADS, ng),
        in_specs=[_head_specs(tg, i, fwd) for i in range(4)]
        + [tab, tab, per_head(CHUNK), per_head(CHUNK), per_head(CHUNK), per_head(8)],
        out_specs=[head_out, head_out, pl.BlockSpec((None, cg, HEAD_DIM, HEAD_DIM), lambda h, g: (h, g, 0, 0))],
        out_shape=[jax.ShapeDtypeStruct((T, RET_WIDTH), BF16), jax.ShapeDtypeStruct((T, RET_WIDTH), F32),
                   jax.ShapeDtypeStruct((RET_HEADS, N, HEAD_DIM, HEAD_DIM), F32)],
        scratch=[pltpu.VMEM((HEAD_DIM, HEAD_DIM), F32)],
        sem=("parallel", "arbitrary"), operands=(z, z, z, z, cosf, sinf, intra, kdec, qdec, cdb))


def _window_sums(u, shift_of):
    sums = []
    s = u
    k = 1
    while k < POOL_WINDOWS[-1]:
        s = s + pltpu.roll(s, shift_of(k), 0)
        sums.append(s)
        k *= 2
    return sums


def _select_group(vals, g):
    out = vals[-1]
    for i in range(len(vals) - 2, -1, -1):
        out = jnp.where(g == i, vals[i], out)
    return out


def _pool_parts(u, g, T, pad):
    rows = lax.broadcasted_iota(jnp.int32, (T, HEAD_DIM), 0)
    valid = rows >= pad
    win = _select_group([float(w) for w in POOL_WINDOWS], g)
    div = jnp.clip((rows - pad + 1).astype(F32), 1.0, win)
    s = _select_group(_window_sums(u, lambda k: k), g)
    pooled = jnp.where(valid, s / div - u, 0.0)
    return pooled, div, valid


def _pool_specs(T, layer):
    first = 4 * RET_WIDTH // HEAD_DIM
    return [
        pl.BlockSpec((T, HEAD_DIM), lambda g: (0, first + g)),
        pl.BlockSpec((None, None, HEAD_DIM, HEAD_DIM), lambda g: (layer, g, 0, 0)),
        pl.BlockSpec((None, 1, HEAD_DIM), lambda g: (layer, 0, g)),
    ]


def _pool_fwd(z, maps, scale, layer, pad, name):
    T = z.shape[0]
    assert pad >= POOL_WINDOWS[-1], "window rolls wrap into the zero rows in front"

    def body(zu, maps_ref, sc_ref, pm_ref):
        g = pl.program_id(0)
        pooled, _, _ = _pool_parts(zu[...], g, T, pad)
        y = _dot(pooled.astype(BF16), maps_ref[...].astype(BF16))
        pm_ref[...] = (y * sc_ref[...]).astype(BF16)

    return _call(
        body, name=name, grid=(POOL_GROUPS,),
        in_specs=_pool_specs(T, layer),
        out_specs=[pl.BlockSpec((T, HEAD_DIM), lambda g: (0, g))],
        out_shape=[jax.ShapeDtypeStruct((T, POOL_WIDTH), BF16)],
        sem=("parallel",), operands=(z, maps, scale))[0]


def _gate_specs(tm, D):
    nb = D // RET_WIDTH
    first = (4 * RET_WIDTH + POOL_WIDTH) // RET_WIDTH
    return [pl.BlockSpec((tm, RET_WIDTH), functools.partial(lambda t, j: (t, j), j=first + j)) for j in range(2 * nb)]


def _load_gates(refs, nb):
    ga = jnp.concatenate([r[...] for r in refs[:nb]], axis=1) if nb > 1 else refs[0][...]
    gb = jnp.concatenate([r[...] for r in refs[nb:]], axis=1) if nb > 1 else refs[nb][...]
    return ga, gb


def _mix_fwd(h, r, pm, z, wru, wpu, wout, tm, name, rider=None):
    T, D = h.shape
    Dq = D // N_CHIPS
    nb = D // RET_WIDTH

    def body(*refs):
        h_ref, r_ref, pm_ref = refs[:3]
        gate_refs = refs[3:3 + 2 * nb]
        wru_ref, wpu_ref, wout_ref, ho_ref, mx_ref, ret_ref, pool_ref = refs[3 + 2 * nb:]
        rv = r_ref[...]
        pv = pm_ref[...]
        ret = jnp.concatenate([_dot(rv, wru_ref[s]) for s in range(N_CHIPS)], axis=1)
        pool = jnp.concatenate([_dot(pv, wpu_ref[s]) for s in range(N_CHIPS)], axis=1)
        ga, gb = _load_gates(gate_refs, nb)
        mixed = (_sigmoid(ga) * ret + _sigmoid(gb) * pool).astype(BF16)
        mx_ref[...] = mixed
        ret_ref[...] = ret.astype(BF16)
        pool_ref[...] = pool.astype(BF16)
        ho_ref[...] = h_ref[...] + _dot(mixed, wout_ref[...].reshape(D, D))

    row = pl.BlockSpec((tm, D), lambda t: (t, 0))
    half = pl.BlockSpec((tm, RET_WIDTH), lambda t: (t, 0))
    up = pl.BlockSpec((N_CHIPS, RET_WIDTH, Dq), lambda t: (0, 0, 0))
    return _call(
        body, name=name, grid=(T // tm,),
        in_specs=[row, half, half] + _gate_specs(tm, D) + [up, up, pl.BlockSpec((N_CHIPS, Dq, D), lambda t: (0, 0, 0))],
        out_specs=[row, row, row, row],
        out_shape=[jax.ShapeDtypeStruct((T, D), F32)] + [jax.ShapeDtypeStruct((T, D), BF16)] * 3,
        sem=("parallel",), operands=(h, r, pm, *([z] * (2 * nb)), wru, wpu, wout), rider=rider)


def _final_loss(h, gain, tgt, name):
    T, D = h.shape
    first = (T - tgt.shape[0]) // CHUNK

    def body(h_ref, g_ref, t_ref, dh_ref, loss_ref, dg_ref):
        i = pl.program_id(0)

        @pl.when(i == 0)
        def _():
            loss_ref[...] = jnp.zeros_like(loss_ref)
            dg_ref[...] = jnp.zeros_like(dg_ref)

        x = h_ref[...]
        gain_v = g_ref[...]
        err = jnp.where(i >= first, _rms_fwd(x, gain_v) - t_ref[...], 0.0)
        loss_ref[...] += 0.5 * jnp.sum(jnp.mean(err * err, axis=-1))
        dx, dgain = _rms_bwd(x, gain_v, err * (1.0 / D))
        dg_ref[...] += dgain
        dh_ref[...] = dx

    return _call(
        body, name=name, grid=(T // CHUNK,),
        in_specs=[pl.BlockSpec((CHUNK, D), lambda i: (i, 0)),
                  pl.BlockSpec((1, D), lambda i: (0, 0)),
                  pl.BlockSpec((CHUNK, D), lambda i: (jnp.maximum(i - first, 0), 0))],
        out_specs=[pl.BlockSpec((CHUNK, D), lambda i: (i, 0)),
                   pl.BlockSpec((1, LANES), lambda i: (0, 0)),
                   pl.BlockSpec((1, D), lambda i: (0, 0))],
        out_shape=[jax.ShapeDtypeStruct((T, D), F32), jax.ShapeDtypeStruct((1, LANES), F32),
                   jax.ShapeDtypeStruct((1, D), F32)],
        sem=("arbitrary",), operands=(h, gain, tgt))


def _ffn_bwd_dx(dy, h, gain, g, u, wg, wu, wd, layer, tm, pad, name, rider=None):
    T, D = h.shape
    Fs = wg.shape[-1]
    F = N_CHIPS * Fs

    def body(dy_ref, h_ref, g_ref, go_ref, uo_ref, wg_ref, wu_ref, wd_ref,
             dh_ref, dg_ref, du_ref, dgain_ref, dyh_ref, da_ref):
        t = pl.program_id(0)
        s = pl.program_id(1)

        @pl.when((t == 0) & (s == 0))
        def _():
            dgain_ref[...] = jnp.zeros_like(dgain_ref)

        @pl.when(s == 0)
        def _():
            dyh_ref[...] = (0.5 * dy_ref[...]).astype(BF16)
            da_ref[...] = jnp.zeros_like(da_ref)

        dact = _dot_nt(dyh_ref[...], wd_ref[...])
        gf = go_ref[...].astype(F32)
        uf = uo_ref[...].astype(F32)
        sg = _sigmoid(gf)
        du = (dact * (gf * sg)).astype(BF16)
        dg = (dact * uf * (sg * (1.0 + gf * (1.0 - sg)))).astype(BF16)
        dg_ref[...] = dg
        du_ref[...] = du
        da_ref[...] += _dot_nt(dg, wg_ref[...]) + _dot_nt(du, wu_ref[...])

        @pl.when(s == N_CHIPS - 1)
        def _():
            dx, dgain = _rms_bwd(h_ref[...], g_ref[...], da_ref[...])
            dgain_ref[...] += dgain
            dh_ref[...] = jnp.where(_row_mask(t, tm, pad, (tm, D)), dy_ref[...] + dx, 0.0)

    row = pl.BlockSpec((tm, D), lambda t, s: (t, 0))
    col = pl.BlockSpec((tm, Fs), lambda t, s: (t, s))
    wcol = pl.BlockSpec((None, D, Fs), lambda t, s: (s, 0, 0))
    return _call(
        body, name=name, grid=(T // tm, N_CHIPS),
        in_specs=[row, row, pl.BlockSpec((None, 1, D), lambda t, s: (layer, 0, 0)), col, col, wcol, wcol,
                  pl.BlockSpec((None, Fs, D), lambda t, s: (s, 0, 0))],
        out_specs=[row, col, col, pl.BlockSpec((1, D), lambda t, s: (0, 0)), row],
        out_shape=[jax.ShapeDtypeStruct((T, D), F32), jax.ShapeDtypeStruct((T, F), BF16),
                   jax.ShapeDtypeStruct((T, F), BF16), jax.ShapeDtypeStruct((1, D), F32),
                   jax.ShapeDtypeStruct((T, D), BF16)],
        scratch=[pltpu.VMEM((tm, D), F32)],
        sem=("arbitrary", "arbitrary"), operands=(dy, h, gain, g, u, wg, wu, wd), rider=rider)


def _grad_tn(a, b, mode, scale, tm, name, rider=None):
    T = a.shape[0]
    if mode == "col":
        R, C = a.shape[1], b.shape[1] // N_CHIPS
        a_spec = pl.BlockSpec((tm, R), lambda s, t: (t, 0))
        b_spec = pl.BlockSpec((tm, C), lambda s, t: (t, s))
    else:
        R, C = a.shape[1] // N_CHIPS, b.shape[1]
        a_spec = pl.BlockSpec((tm, R), lambda s, t: (t, s))
        b_spec = pl.BlockSpec((tm, C), lambda s, t: (t, 0))
    nt = T // tm

    def body(a_ref, b_ref, o_ref, acc_ref):
        t = pl.program_id(1)

        @pl.when(t == 0)
        def _():
            acc_ref[...] = jnp.zeros_like(acc_ref)

        acc_ref[...] += _dot_tn(a_ref[...].astype(BF16), b_ref[...].astype(BF16))

        @pl.when(t == nt - 1)
        def _():
            o_ref[...] = (scale * acc_ref[...]).astype(BF16)

    return _call(
        body, name=name, grid=(N_CHIPS, nt),
        in_specs=[a_spec, b_spec],
        out_specs=[pl.BlockSpec((None, R, C), lambda s, t: (s, 0, 0))],
        out_shape=[jax.ShapeDtypeStruct((N_CHIPS, R, C), BF16)],
        scratch=[pltpu.VMEM((R, C), F32)],
        sem=("parallel", "arbitrary"), operands=(a, b), rider=rider)[0]


def _mix_bwd_dx(dh, z, ret, pool, wout, wru, wpu, tm, name, rider=None):
    T, D = dh.shape
    Dq = D // N_CHIPS
    nb = D // RET_WIDTH

    def body(*refs):
        dh_ref = refs[0]
        gate_refs = refs[1:1 + 2 * nb]
        ret_ref, pool_ref, wout_ref, wru_ref, wpu_ref, dgab_ref, dret_ref, dpool_ref, dr_ref, dpm_ref = refs[1 + 2 * nb:]
        dmixed = _dot_nt(dh_ref[...].astype(BF16), wout_ref[...].reshape(D, D))
        ga, gb = _load_gates(gate_refs, nb)
        sa = _sigmoid(ga)
        sb = _sigmoid(gb)
        dgab_ref[:, :D] = (dmixed * ret_ref[...].astype(F32) * (sa * (1.0 - sa))).astype(BF16)
        dgab_ref[:, D:] = (dmixed * pool_ref[...].astype(F32) * (sb * (1.0 - sb))).astype(BF16)
        dret = (dmixed * sa).astype(BF16)
        dpool = (dmixed * sb).astype(BF16)
        dret_ref[...] = dret
        dpool_ref[...] = dpool
        dr = _dot_nt(dret[:, :Dq], wru_ref[0])
        dpm = _dot_nt(dpool[:, :Dq], wpu_ref[0])
        for s in range(1, N_CHIPS):
            dr += _dot_nt(dret[:, s * Dq:(s + 1) * Dq], wru_ref[s])
            dpm += _dot_nt(dpool[:, s * Dq:(s + 1) * Dq], wpu_ref[s])
        dr_ref[...] = dr
        dpm_ref[...] = dpm

    row = pl.BlockSpec((tm, D), lambda t: (t, 0))
    half = pl.BlockSpec((tm, RET_WIDTH), lambda t: (t, 0))
    up = pl.BlockSpec((N_CHIPS, RET_WIDTH, Dq), lambda t: (0, 0, 0))
    return _call(
        body, name=name, grid=(T // tm,),
        in_specs=[row] + _gate_specs(tm, D) + [row, row, pl.BlockSpec((N_CHIPS, Dq, D), lambda t: (0, 0, 0)), up, up],
        out_specs=[pl.BlockSpec((tm, 2 * D), lambda t: (t, 0)), row, row, half, half],
        out_shape=[jax.ShapeDtypeStruct((T, 2 * D), BF16), jax.ShapeDtypeStruct((T, D), BF16),
                   jax.ShapeDtypeStruct((T, D), BF16), jax.ShapeDtypeStruct((T, RET_WIDTH), F32),
                   jax.ShapeDtypeStruct((T, POOL_WIDTH), F32)],
        sem=("parallel",), operands=(dh, *([z] * (2 * nb)), ret, pool, wout, wru, wpu), rider=rider)


def _pool_bwd(z, dpm, maps, scale, layer, pad, name):
    T = z.shape[0]

    def body(zu, maps_ref, sc_ref, dpm_ref, du_ref, dmaps_ref, dsc_ref):
        g = pl.program_id(0)
        u = zu[...]
        pooled, div, valid = _pool_parts(u, g, T, pad)
        pb = pooled.astype(BF16)
        mb = maps_ref[...].astype(BF16)
        dp = dpm_ref[...]
        dsc_ref[...] = jnp.sum(dp * _dot(pb, mb), axis=0, keepdims=True)
        dyb = (dp * sc_ref[...]).astype(BF16)
        dmaps_ref[...] = _dot_tn(pb, dyb)
        dpooled = jnp.where(valid, _dot_nt(dyb, mb), 0.0)
        ahead = _select_group(_window_sums(dpooled / div, lambda k: T - k), g)
        du_ref[...] = jnp.where(valid, ahead - dpooled, 0.0).astype(BF16)

    blk = pl.BlockSpec((T, HEAD_DIM), lambda g: (0, g))
    return _call(
        body, name=name, grid=(POOL_GROUPS,),
        in_specs=_pool_specs(T, layer) + [blk],
        out_specs=[blk, pl.BlockSpec((None, HEAD_DIM, HEAD_DIM), lambda g: (g, 0, 0)),
                   pl.BlockSpec((1, HEAD_DIM), lambda g: (0, g))],
        out_shape=[jax.ShapeDtypeStruct((T, POOL_WIDTH), BF16),
                   jax.ShapeDtypeStruct((POOL_GROUPS, HEAD_DIM, HEAD_DIM), F32),
                   jax.ShapeDtypeStruct((1, POOL_WIDTH), F32)],
        sem=("parallel",), operands=(z, maps, scale, dpm))


def _ret_bwd_local(z, o_pre, s_all, dr, consts, cg, name):
    T = z.shape[0]
    N = T // CHUNK
    ng = N // cg
    tg = cg * CHUNK
    cosf, sinf, intra, _, qdec, _ = consts
    fwd = lambda g: g

    def body(zq, zk, zv, zg, o_ref, s_ref, dr_ref, cos_ref, sin_ref, m_ref, qd_ref,
             dq_ref, dg_ref, dk_ref, dv_ref, ds_ref):
        cosv = cos_ref[...]
        sinv = sin_ref[...]
        scale = HEAD_DIM ** -0.5
        q3 = (_rot(zq[...], cosv, sinv) * scale).reshape(cg, CHUNK, HEAD_DIM)
        k3 = _rot(zk[...], cosv, sinv).reshape(cg, CHUNK, HEAD_DIM)
        qb = q3.astype(BF16)
        kb = k3.astype(BF16)
        vb = zv[...].reshape(cg, CHUNK, HEAD_DIM).astype(BF16)
        mask = m_ref[...][None]
        sb = (_ein("ncd,nmd->ncm", qb, kb) * mask).astype(BF16)
        qdv = qd_ref[...][None]
        qdb = (q3 * qdv).astype(BF16)

        out = o_ref[...]
        xc = out - jnp.mean(out, axis=-1, keepdims=True)
        rstd = lax.rsqrt(jnp.mean(xc * xc, axis=-1, keepdims=True) + EPS)
        rn = xc * rstd
        g = zg[...]
        sg = _sigmoid(g)
        drv = dr_ref[...]
        dg_ref[...] = (drv * rn * (sg * (1.0 + g * (1.0 - sg)))).astype(BF16)
        drn = drv * (g * sg)
        dout = rstd * (drn - jnp.mean(drn, axis=-1, keepdims=True)
                       - rn * jnp.mean(drn * rn, axis=-1, keepdims=True))
        dob = dout.reshape(cg, CHUNK, HEAD_DIM).astype(BF16)

        dsb = (_ein("ncd,nmd->ncm", dob, vb) * mask).astype(BF16)
        dv_ref[...] = _ein("ncm,ncd->nmd", sb, dob).reshape(tg, HEAD_DIM)
        dk_ref[...] = _ein("ncm,ncd->nmd", dsb, qb).reshape(tg, HEAD_DIM)
        dq3 = _ein("ncm,nmd->ncd", dsb, kb) + _ein("nce,nde->ncd", dob, s_ref[...].astype(BF16)) * qdv
        dq_ref[...] = _rot_t(dq3.reshape(tg, HEAD_DIM) * scale, cosv, sinv).astype(BF16)
        ds_ref[...] = _ein("ncd,nce->nde", qdb, dob)

    tab = pl.BlockSpec((tg, HEAD_DIM), lambda h, g: (g, 0))
    per_head = pl.BlockSpec((None, CHUNK, HEAD_DIM), lambda h, g: (h, 0, 0))
    head_blk = pl.BlockSpec((tg, HEAD_DIM), lambda h, g: (g, h))
    state_blk = pl.BlockSpec((None, cg, HEAD_DIM, HEAD_DIM), lambda h, g: (h, g, 0, 0))
    return _call(
        body, name=name, grid=(RET_HEADS, ng),
        in_specs=[_head_specs(tg, i, fwd) for i in range(4)]
        + [head_blk, state_blk, head_blk, tab, tab, per_head, per_head],
        out_specs=[head_blk, head_blk, head_blk, head_blk, state_blk],
        out_shape=[jax.ShapeDtypeStruct((T, RET_WIDTH), BF16), jax.ShapeDtypeStruct((T, RET_WIDTH), BF16),
                   jax.ShapeDtypeStruct((T, RET_WIDTH), F32), jax.ShapeDtypeStruct((T, RET_WIDTH), F32),
                   jax.ShapeDtypeStruct((RET_HEADS, N, HEAD_DIM, HEAD_DIM), F32)],
        sem=("parallel", "parallel"), operands=(z, z, z, z, o_pre, s_all, dr, cosf, sinf, intra, qdec))


def _ret_bwd_state(z, dkp, dvp, ds, consts, cg, name):
    T = z.shape[0]
    N = T // CHUNK
    ng = N // cg
    tg = cg * CHUNK
    cosf, sinf, _, kdec, _, cdb = consts
    rev = lambda g: ng - 1 - g

    def body(zk, zv, dkp_ref, dvp_ref, ds_ref, cos_ref, sin_ref, kd_ref, cd_ref, dk_ref, dv_ref, gs_ref, dkv_ref):
        @pl.when(pl.program_id(1) == 0)
        def _():
            gs_ref[...] = jnp.zeros_like(gs_ref)

        cosv = cos_ref[...]
        sinv = sin_ref[...]
        cd = cd_ref[0:1, :]
        grad = gs_ref[...]
        for n in reversed(range(cg)):
            dkv_ref[n] = grad
            grad = ds_ref[n] + cd * grad
        gs_ref[...] = grad
        dkvb = dkv_ref[...].astype(BF16)
        kdv = kd_ref[...][None]
        k3 = _rot(zk[...], cosv, sinv).reshape(cg, CHUNK, HEAD_DIM)
        vb = zv[...].reshape(cg, CHUNK, HEAD_DIM).astype(BF16)
        dk3 = _ein("nce,nde->ncd", vb, dkvb) * kdv
        dv3 = _ein("ncd,nde->nce", (k3 * kdv).astype(BF16), dkvb)
        dk_ref[...] = _rot_t(dkp_ref[...] + dk3.reshape(tg, HEAD_DIM), cosv, sinv).astype(BF16)
        dv_ref[...] = (dvp_ref[...] + dv3.reshape(tg, HEAD_DIM)).astype(BF16)

    tab = pl.BlockSpec((tg, HEAD_DIM), lambda h, g: (rev(g), 0))
    head_blk = pl.BlockSpec((tg, HEAD_DIM), lambda h, g: (rev(g), h))
    return _call(
        body, name=name, grid=(RET_HEADS, ng),
        in_specs=[_head_specs(tg, 1, rev), _head_specs(tg, 2, rev), head_blk, head_blk,
                  pl.BlockSpec((None, cg, HEAD_DIM, HEAD_DIM), lambda h, g: (h, rev(g), 0, 0)),
                  tab, tab,
                  pl.BlockSpec((None, CHUNK, HEAD_DIM), lambda h, g: (h, 0, 0)),
                  pl.BlockSpec((None, 8, HEAD_DIM), lambda h, g: (h, 0, 0))],
        out_specs=[head_blk, head_blk],
        out_shape=[jax.ShapeDtypeStruct((T, RET_WIDTH), BF16)] * 2,
        scratch=[pltpu.VMEM((HEAD_DIM, HEAD_DIM), F32), pltpu.VMEM((cg, HEAD_DIM, HEAD_DIM), F32)],
        sem=("parallel", "arbitrary"), operands=(z, z, dkp, dvp, ds, cosf, sinf, kdec, cdb))


def _inproj_bwd_dx(dz, win, h, gain, dh_in, layer, tm, pad, name, rider=None):
    T, D = h.shape
    Ns = win.shape[-1]

    def body(dz_ref, w_ref, h_ref, g_ref, dhi_ref, dh_ref, dgain_ref, db_ref):
        t = pl.program_id(0)
        s = pl.program_id(1)

        @pl.when((t == 0) & (s == 0))
        def _():
            dgain_ref[...] = jnp.zeros_like(dgain_ref)

        @pl.when(s == 0)
        def _():
            db_ref[...] = jnp.zeros_like(db_ref)

        db_ref[...] += _dot_nt(dz_ref[...], w_ref[...])

        @pl.when(s == N_CHIPS - 1)
        def _():
            dx, dgain = _rms_bwd(h_ref[...], g_ref[...], db_ref[...])
            dgain_ref[...] += dgain
            dh_ref[...] = jnp.where(_row_mask(t, tm, pad, (tm, D)), dhi_ref[...] + dx, 0.0)

    row = pl.BlockSpec((tm, D), lambda t, s: (t, 0))
    return _call(
        body, name=name, grid=(T // tm, N_CHIPS),
        in_specs=[pl.BlockSpec((tm, Ns), lambda t, s: (t, s)),
                  pl.BlockSpec((None, D, Ns), lambda t, s: (s, 0, 0)),
                  row, pl.BlockSpec((None, 1, D), lambda t, s: (layer, 0, 0)), row],
        out_specs=[row, pl.BlockSpec((1, D), lambda t, s: (0, 0))],
        out_shape=[jax.ShapeDtypeStruct((T, D), F32), jax.ShapeDtypeStruct((1, D), F32)],
        scratch=[pltpu.VMEM((tm, D), F32)],
        sem=("arbitrary", "arbitrary"), operands=(dz, win, h, gain, dh_in), rider=rider)


def _sum_pair(g, r, c_idx, name):
    _, R, C = g.shape
    rh = R // 2

    def body(c_ref, g_ref, r_ref, o_ref):
        o_ref[...] = (g_ref[...].astype(F32) + r_ref[...].astype(F32)).astype(BF16)

    blk = pl.BlockSpec((None, rh, C), lambda s, c_ref: (s, 0, 0))
    return pl.pallas_call(
        body,
        name=name,
        grid_spec=pltpu.PrefetchScalarGridSpec(
            num_scalar_prefetch=1,
            grid=(N_CHIPS,),
            in_specs=[pl.BlockSpec((None, rh, C), lambda s, c_ref: (s, c_ref[0], 0)), blk],
            out_specs=blk,
        ),
        out_shape=jax.ShapeDtypeStruct((N_CHIPS, rh, C), BF16),
        compiler_params=_params(("parallel",)),
    )(c_idx, g, r)


def _sum_chips(p, r, pos, name):
    _, rh, C = r.shape

    def body(pos_ref, p_ref, r_ref, o_ref):
        chip = pos_ref[0]
        own = p_ref[...].astype(F32)
        terms = [jnp.where(chip == k, own, r_ref[k].astype(F32)) for k in range(N_CHIPS)]
        o_ref[...] = ((terms[0] + terms[1]) + terms[2]) + terms[3]

    return pl.pallas_call(
        body,
        name=name,
        grid_spec=pltpu.PrefetchScalarGridSpec(
            num_scalar_prefetch=1,
            grid=(1,),
            in_specs=[pl.BlockSpec((None, rh, C), lambda i, pos_ref: (pos_ref[0], 0, 0)),
                      pl.BlockSpec((N_CHIPS, rh, C), lambda i, pos_ref: (0, 0, 0))],
            out_specs=pl.BlockSpec((rh, C), lambda i, pos_ref: (pos_ref[1], 0)),
        ),
        out_shape=jax.ShapeDtypeStruct((2 * rh, C), F32),
        compiler_params=_params(("arbitrary",)),
    )(pos, p, r)


def _small_all_reduce(p):
    rows, width = p.shape

    def body(p_ref, o_ref, sib_ref, slot_ref, ssem, rsem):
        x, y, c, chip, others = _mesh_pos()
        pair = _remote(p_ref, sib_ref, ssem.at[0], rsem.at[0], (x, y, 1 - c))
        pair.start()
        pair.wait()
        slot_ref[chip] = p_ref[...] + sib_ref[...]
        sends = []
        for j, (ox, oy) in enumerate(others):
            cp = _remote(slot_ref.at[chip], slot_ref.at[chip], ssem.at[1 + j], rsem.at[1 + j], (ox, oy, c))
            cp.start()
            sends.append(cp)
        for j, (ox, oy) in enumerate(others):
            slot = slot_ref.at[2 * ox + oy]
            _remote(slot, slot, ssem.at[1 + j], rsem.at[1 + j], (ox, oy, c)).wait_recv()
        for cp in sends:
            cp.wait_send()
        o_ref[...] = ((slot_ref[0] + slot_ref[1]) + slot_ref[2]) + slot_ref[3]

    vmem = pl.BlockSpec(memory_space=pltpu.VMEM)
    return pl.pallas_call(
        body,
        name="small_grads_all_reduce",
        in_specs=[vmem],
        out_specs=vmem,
        out_shape=jax.ShapeDtypeStruct(p.shape, F32),
        scratch_shapes=[pltpu.VMEM((rows, width), F32), pltpu.VMEM((N_CHIPS, rows, width), F32),
                        pltpu.SemaphoreType.DMA((4,)), pltpu.SemaphoreType.DMA((4,))],
    )(p)


def _adamw(gs, w, m, v, name):
    L, R, C = w.shape
    Ct = gs[0].shape[1]
    tr = _pick_tile(R, 256, 8)

    def body(*refs):
        g_refs = refs[:L]
        w_ref, m_ref, v_ref, go_ref, d_ref, mo_ref, vo_ref = refs[L:]
        layer = pl.program_id(0)
        grad = g_refs[L - 1][...]
        for i in range(L - 2, -1, -1):
            grad = jnp.where(layer == i, g_refs[i][...], grad)
        if Ct != C:
            grad = grad[:, :C]
        m_new = ADAM_B1 * m_ref[...] + (1.0 - ADAM_B1) * grad
        v_new = ADAM_B2 * v_ref[...] + (1.0 - ADAM_B2) * jnp.square(grad)
        m_hat = m_new / (1.0 - ADAM_B1 ** ADAM_STEP)
        v_hat = v_new / (1.0 - ADAM_B2 ** ADAM_STEP)
        go_ref[...] = grad
        d_ref[...] = -ADAM_LR * (m_hat / (jnp.sqrt(v_hat) + ADAM_EPS) + ADAM_WD * w_ref[...])
        mo_ref[...] = m_new
        vo_ref[...] = v_new

    g_specs = [pl.BlockSpec((tr, Ct), functools.partial(lambda l, r, i: (jnp.where(l == i, r, 0), 0), i=i))
               for i in range(L)]
    blk = pl.BlockSpec((None, tr, C), lambda l, r: (l, r, 0))
    return pl.pallas_call(
        body,
        name=name,
        grid=(L, R // tr),
        in_specs=g_specs + [blk, blk, blk],
        out_specs=[blk] * 4,
        out_shape=[jax.ShapeDtypeStruct((L, R, C), F32)] * 4,
        compiler_params=_params(("arbitrary", "arbitrary")),
    )(*gs, w, m, v)


_FFN1 = ("ffn1_gate", "ffn1_up", "ffn1_down")
_FFN2 = ("ffn2_gate", "ffn2_up", "ffn2_down")
_MIXW = ("w_ret_up", "w_pool_up", "w_out")
_BIG = _FFN1 + ("w_in",) + _MIXW + _FFN2
_TRANSPOSED = ("ffn1_gate", "ffn1_up", "ffn2_gate", "ffn2_up")
_SMALL = ("ffn1_norm", "mix_norm", "ffn2_norm", "final_norm", "pool_scale", "pool_maps")
_ORDER = ("meta", "ffn1_norm", "ffn1_gate", "ffn1_up", "ffn1_down", "mix_norm", "w_in", "pool_maps",
          "pool_scale", "w_ret_up", "w_pool_up", "w_out", "ffn2_norm", "ffn2_gate", "ffn2_up", "ffn2_down",
          "final_norm")


def _transport(a):
    r, c = a.shape[1], a.shape[2]
    return jnp.pad(a.astype(BF16), ((0, 0), (0, _round_up(r, LANES) - r), (0, _round_up(c, LANES) - c)))


def _pack_rows(parts, width):
    rows = [p.reshape(-1, width) for p in parts]
    total = sum(r.shape[0] for r in rows)
    fill = _round_up(total, 8) - total
    if fill:
        rows.append(jnp.zeros((fill, width), F32))
    return jnp.concatenate(rows, axis=0)


def _unpack_rows(packed, shapes, width):
    out, at = [], 0
    for shp in shapes:
        n = math.prod(shp) // width
        out.append(packed[at:at + n].reshape(shp))
        at += n
    return out


class _Weights:
    def __init__(self, shards):
        self.shards = shards
        self.full = {}

    def rider(self, keys):
        r = _gather_rider([(self.shards[n], i) for n, i in keys])
        r.keys = keys
        return r

    def take(self, rider):
        for key, arr in zip(rider.keys, rider.results):
            self.full[key] = arr

    def __call__(self, name, layer):
        return self.full[(name, layer)]


def _local_step(x, meta_full, tgt, w, wts, pad, tm, cg, reducer):
    D = x.shape[1]
    T = pad + N_META + x.shape[0]
    L = w["ffn1_norm"].shape[0]
    pool_maps = w["pool_maps"]
    gains = {n: w[n].reshape(L, 1, D) for n in ("ffn1_norm", "mix_norm", "ffn2_norm")}
    scale3 = w["pool_scale"].reshape(L, 1, POOL_WIDTH)
    consts = _ret_consts(T, pad)

    def gather(keys):
        return wts.rider(keys) if keys and keys[0] not in wts.full else None

    def done(rider):
        if rider is not None:
            wts.take(rider)

    h = jnp.concatenate([jnp.zeros((pad, D), F32), meta_full, x], axis=0)
    saved = []
    for i in range(L):
        s = {"h0": h}
        rd = gather([("w_in", i)] + [(n, i) for n in _MIXW])
        h, s["a1"], s["g1"], s["u1"], s["act1"] = _ffn_fwd(
            h, gains["ffn1_norm"], wts("ffn1_gate", i), wts("ffn1_up", i), wts("ffn1_down", i), i, tm,
            f"ffn1_fwd_{i}", rd)
        done(rd)
        s["h1"] = h
        rd = gather([("ffn2_gate", i), ("ffn2_up", i)])
        s["z"], s["b"] = _inproj_fwd(h, gains["mix_norm"], wts("w_in", i), i, tm, f"inproj_fwd_{i}", rd)
        done(rd)
        s["r"], s["o_pre"], s["s_all"] = _ret_fwd(s["z"], consts, cg, f"retention_fwd_{i}")
        s["pm"] = _pool_fwd(s["z"], pool_maps, scale3, i, pad, f"pool_fwd_{i}")
        rd = gather([("ffn2_down", i)])
        h, s["mixed"], s["ret"], s["pool"] = _mix_fwd(
            h, s["r"], s["pm"], s["z"], wts("w_ret_up", i), wts("w_pool_up", i), wts("w_out", i), tm,
            f"mix_fwd_{i}", rd)
        done(rd)
        s["h2"] = h
        rd = gather([(n, i + 1) for n in _FFN1]) if i + 1 < L else None
        h, s["a2"], s["g2"], s["u2"], s["act2"] = _ffn_fwd(
            h, gains["ffn2_norm"], wts("ffn2_gate", i), wts("ffn2_up", i), wts("ffn2_down", i), i, tm,
            f"ffn2_fwd_{i}", rd)
        done(rd)
        saved.append(s)

    dh, loss_acc, d_final = _final_loss(h, w["final_norm"].reshape(1, D), tgt, "final_norm_loss")

    small = {n: [None] * L for n in ("ffn1_norm", "mix_norm", "ffn2_norm", "pool_scale", "pool_maps")}

    carry = {"ffn_bwd": 4.0, "mix_bwd": 1.0, "inproj_bwd": 2.0, "w_in": 2.0, "w_out": 0.5, "w_ret_up": 0.5,
             "w_pool_up": 0.5}

    tk = _pick_tile(T, 1408, LANES)

    def grad(n, a, b, i, mode):
        rd = reducer.rider(carry.get(n, 1.5))
        reducer.add(n, i, _grad_tn(a, b, mode, 1.0, tk, f"grad_{n}_{i}", rd))
        reducer.done(rd)

    for i in reversed(range(L)):
        s = saved[i]
        rd = reducer.rider(carry["ffn_bwd"])
        dh, dg, du, small["ffn2_norm"][i], dyh = _ffn_bwd_dx(
            dh, s["h2"], gains["ffn2_norm"], s["g2"], s["u2"], wts("ffn2_gate", i), wts("ffn2_up", i),
            wts("ffn2_down", i), i, tm, pad, f"ffn2_bwd_{i}", rd)
        reducer.done(rd)
        grad("ffn2_gate", dg, s["a2"], i, "row")
        grad("ffn2_up", du, s["a2"], i, "row")
        grad("ffn2_down", s["act2"], dyh, i, "row")
        rd = reducer.rider(carry["mix_bwd"])
        dgab, dret, dpool, dr, dpm = _mix_bwd_dx(
            dh, s["z"], s["ret"], s["pool"], wts("w_out", i), wts("w_ret_up", i), wts("w_pool_up", i), tm,
            f"mix_bwd_{i}", rd)
        reducer.done(rd)
        grad("w_out", s["mixed"], dh, i, "row")
        grad("w_ret_up", s["r"], dret, i, "col")
        grad("w_pool_up", s["pm"], dpool, i, "col")
        du_pool, small["pool_maps"][i], small["pool_scale"][i] = _pool_bwd(
            s["z"], dpm, pool_maps, scale3, i, pad, f"pool_bwd_{i}")
        dq, dgr, dkp, dvp, ds = _ret_bwd_local(s["z"], s["o_pre"], s["s_all"], dr, consts, cg, f"retention_bwd_{i}")
        dk, dv = _ret_bwd_state(s["z"], dkp, dvp, ds, consts, cg, f"retention_bwd_state_{i}")
        dz = jnp.concatenate([dq, dk, dv, dgr, du_pool, dgab], axis=1)
        dh2 = dh
        rd = reducer.rider(carry["inproj_bwd"])
        dh, small["mix_norm"][i] = _inproj_bwd_dx(
            dz, wts("w_in", i), s["h1"], gains["mix_norm"], dh2, i, tm, pad, f"inproj_bwd_{i}", rd)
        reducer.done(rd)
        grad("w_in", s["b"], dz, i, "col")
        reducer.stage(f"mid{i}")
        rd = reducer.rider(carry["ffn_bwd"])
        dh, dg, du, small["ffn1_norm"][i], dyh = _ffn_bwd_dx(
            dh, s["h0"], gains["ffn1_norm"], s["g1"], s["u1"], wts("ffn1_gate", i), wts("ffn1_up", i),
            wts("ffn1_down", i), i, tm, pad, f"ffn1_bwd_{i}", rd)
        reducer.done(rd)
        grad("ffn1_gate", dg, s["a1"], i, "row")
        grad("ffn1_up", du, s["a1"], i, "row")
        grad("ffn1_down", s["act1"], dyh, i, "row")
        reducer.stage(f"end{i}")

    return loss_acc, dh, small, d_final


class _Reducer:
    def __init__(self, unit):
        self.c_idx = lax.axis_index("c").astype(jnp.int32).reshape(1)
        chip = 2 * lax.axis_index("x") + lax.axis_index("y")
        self.pos = jnp.stack([chip, lax.axis_index("c")]).astype(jnp.int32)
        self.pending, self.queue, self.halves = [], [], {}
        self.unit = unit

    def add(self, name, layer, g):
        self.pending.append(((name, layer), g))

    def stage(self, tag):
        if not self.pending:
            return
        keys = [k for k, _ in self.pending]
        gs = [g for _, g in self.pending]
        self.pending = []
        from_sibling = _run_rider(_pair_exchange_rider(gs), f"grads_pair_exchange_{tag}")
        for key, g, r in zip(keys, gs, from_sibling):
            self.queue.append((key, _sum_pair(g, r, self.c_idx, f"sum_pair_{key[0]}_{key[1]}")))

    def rider(self, units):
        take, size = [], 0
        while self.queue and size + self.queue[0][1].size <= units * self.unit:
            take.append(self.queue.pop(0))
            size += take[-1][1].size
        if not take:
            return None
        rd = _chip_exchange_rider([p for _, p in take])
        rd.keys = [k for k, _ in take]
        return rd

    def done(self, rd):
        if rd is None:
            return
        for key, p, r in zip(rd.keys, rd.ins, rd.results):
            self.halves[key] = _sum_chips(p, r, self.pos, f"sum_chips_{key[0]}_{key[1]}")

    def finish(self):
        assert not self.pending
        if self.queue:
            rd = _chip_exchange_rider([p for _, p in self.queue])
            rd.keys = [k for k, _ in self.queue]
            self.queue = []
            _run_rider(rd, "grads_chip_exchange_tail")
            self.done(rd)
        keys = list(self.halves)
        return dict(zip(keys, _pair_gather([self.halves[k] for k in keys])))


def _update(loss, grad_x, d_meta_rows, shard_grads, small, d_final, w, mom, var):
    meta = w["meta"]
    D = w["final_norm"].shape[0]
    L = w["ffn1_norm"].shape[0]
    Dq = D // N_CHIPS

    small_parts = [jnp.concatenate(small[n], axis=0) for n in ("ffn1_norm", "mix_norm", "ffn2_norm")]
    small_parts += [d_final, jnp.concatenate(small["pool_scale"], axis=0), jnp.concatenate(small["pool_maps"], axis=0)]
    reduced = _small_all_reduce(_pack_rows(small_parts + [d_meta_rows], D))
    small_shapes = [w[n].shape for n in _SMALL]
    small_rows = sum(math.prod(shp) for shp in small_shapes) // D
    chip = 2 * lax.axis_index("x") + lax.axis_index("y")
    d_meta = lax.dynamic_slice_in_dim(reduced[small_rows:small_rows + N_META], chip * Dq, Dq, axis=1)

    out = {}
    for n in _BIG:
        gs = [shard_grads[(n, i)] for i in range(L)]
        if n in _TRANSPOSED:
            res = _adamw(gs, *(jnp.swapaxes(t[n], 1, 2) for t in (w, mom, var)), f"adamw_{n}")
            out[n] = [jnp.swapaxes(r, 1, 2) for r in res]
        else:
            out[n] = _adamw(gs, w[n], mom[n], var[n], f"adamw_{n}")
    names = _SMALL + ("meta",)
    packed_g = _pack_rows([reduced[:small_rows], d_meta], D)
    packed = [_pack_rows([t[n] for n in names], D) for t in (w, mom, var)]
    res = _adamw([packed_g], packed[0][None], packed[1][None], packed[2][None], "adamw_small")
    shapes = small_shapes + [meta.shape]
    unpacked = [_unpack_rows(r[0], shapes, D) for r in res]
    for k, n in enumerate(names):
        out[n] = tuple(u[k] for u in unpacked)

    return (loss, grad_x) + tuple(out[n][j] for j in range(4) for n in _ORDER)


def kernel(x, meta, ffn1_norm, ffn1_gate, ffn1_up, ffn1_down, mix_norm, w_in, pool_maps, pool_scale, w_ret_up, w_pool_up, w_out, ffn2_norm, ffn2_gate, ffn2_up, ffn2_down, final_norm, loss_target, m_meta, m_ffn1_norm, m_ffn1_gate, m_ffn1_up, m_ffn1_down, m_mix_norm, m_w_in, m_pool_maps, m_pool_scale, m_w_ret_up, m_w_pool_up, m_w_out, m_ffn2_norm, m_ffn2_gate, m_ffn2_up, m_ffn2_down, m_final_norm, v_meta, v_ffn1_norm, v_ffn1_gate, v_ffn1_up, v_ffn1_down, v_mix_norm, v_w_in, v_pool_maps, v_pool_scale, v_w_ret_up, v_w_pool_up, v_w_out, v_ffn2_norm, v_ffn2_gate, v_ffn2_up, v_ffn2_down, v_final_norm):
    args = dict(locals())
    w = {n: args[n] for n in _ORDER}
    mom = {n: args["m_" + n] for n in _ORDER}
    var = {n: args["v_" + n] for n in _ORDER}

    assert x.shape[0] == 1, "one batch element per device"
    seq, D = x.shape[1], x.shape[2]
    assert seq % CHUNK == 0 and D % RET_WIDTH == 0 and (2 * POOL_WIDTH) % D == 0
    pad = (-(seq + N_META)) % CHUNK
    T = seq + N_META + pad
    tm = _pick_tile(T, 528, BF16_ROWS)
    cg = _pick_tile(T // CHUNK, 11, 1)

    shards = {n: _transport(w[n]) for n in _BIG}
    shards["meta"] = meta[None]
    wts = _Weights(shards)
    head = wts.rider([(n, 0) for n in _FFN1] + [("meta", 0)])
    _run_rider(head, "weights_gather_head")
    wts.take(head)
    meta_full = jnp.transpose(wts("meta", 0), (1, 0, 2)).reshape(N_META, D)

    reducer = _Reducer(unit=2 * shards["ffn1_gate"][0].size)
    loss_acc, dh, small, d_final = _local_step(x[0], meta_full, loss_target[0], w, wts, pad, tm, cg, reducer)
    loss = lax.psum(loss_acc[0, 0], ("x", "y", "c"))
    grad_x = dh[pad + N_META:][None]
    return _update(loss, grad_x, dh[pad:pad + N_META], reducer.finish(), small, d_final, w, mom, var)
```

```python
import functools
import math

import jax
import jax.numpy as jnp
from jax import lax
from jax.experimental import pallas as pl
from jax.experimental.pallas import tpu as pltpu

F32 = jnp.float32
BF16 = jnp.bfloat16

N_META = 16
RET_HEADS = 4
HEAD_DIM = 128
RET_WIDTH = RET_HEADS * HEAD_DIM
POOL_WINDOWS = (2, 4, 8, 16)
POOL_GROUPS = len(POOL_WINDOWS)
POOL_WIDTH = POOL_GROUPS * HEAD_DIM
CHUNK = 128
ROPE_BASE = 10000.0
EPS = 1e-6
ADAM_LR = 0.001
ADAM_B1 = 0.9
ADAM_B2 = 0.999
ADAM_EPS = 1e-08
ADAM_WD = 0.01
ADAM_STEP = 10

N_CHIPS = 4
LANES = 128
BF16_ROWS = 16
V7X_VMEM_LIMIT = 52 * 1024 * 1024
MESH = pl.DeviceIdType.MESH
ANY = pl.BlockSpec(memory_space=pl.ANY)


def _round_up(n, m):
    return -(-n // m) * m


def _pick_tile(n, target, mult):
    best = None
    for d in range(mult, min(n, target) + 1, mult):
        if n % d == 0:
            best = d
    assert best is not None, (n, target, mult)
    return best


def _params(sem=None):
    return pltpu.CompilerParams(dimension_semantics=sem, vmem_limit_bytes=V7X_VMEM_LIMIT)


def _dot(a, b):
    return jnp.dot(a, b, preferred_element_type=F32)


def _dot_nt(a, b):
    return lax.dot_general(a, b, (((1,), (1,)), ((), ())), preferred_element_type=F32)


def _dot_tn(a, b):
    return lax.dot_general(a, b, (((0,), (0,)), ((), ())), preferred_element_type=F32)


def _ein(spec, a, b):
    return jnp.einsum(spec, a, b, preferred_element_type=F32)


def _sigmoid(x):
    return jax.nn.sigmoid(x)


def _rms_fwd(x, gain):
    r = lax.rsqrt(jnp.mean(x * x, axis=-1, keepdims=True) + EPS)
    return x * r * gain


def _rms_bwd(x, gain, da):
    r = lax.rsqrt(jnp.mean(x * x, axis=-1, keepdims=True) + EPS)
    xh = x * r
    dgain = jnp.sum(da * xh, axis=0, keepdims=True)
    dxh = da * gain
    dx = r * (dxh - xh * jnp.mean(dxh * xh, axis=-1, keepdims=True))
    return dx, dgain


def _row_mask(t, tm, pad, shape):
    rows = t * tm + lax.broadcasted_iota(jnp.int32, shape, 0)
    return rows >= pad


def _mesh_pos():
    x, y, c = lax.axis_index("x"), lax.axis_index("y"), lax.axis_index("c")
    others = [(1 - x, y), (x, 1 - y), (1 - x, 1 - y)]
    return x, y, c, 2 * x + y, others


def _half_rows(c, rh):
    return pl.ds(pl.multiple_of(c * rh, rh), rh)


def _remote(src, dst, ssem, rsem, dev):
    return pltpu.make_async_remote_copy(src_ref=src, dst_ref=dst, send_sem=ssem, recv_sem=rsem,
                                        device_id=dev, device_id_type=MESH)


class _Rider:
    def __init__(self, ins, out_shapes, n_sem, start, finish):
        self.ins, self.out_shapes, self.n_sem, self.start, self.finish = ins, out_shapes, n_sem, start, finish
        self.results = None


class _SemWindow:
    def __init__(self, ref, base):
        self.ref, self.base = ref, base

    @property
    def at(self):
        return self

    def __getitem__(self, k):
        return self.ref.at[self.base + k]


def _join(riders):
    riders = [r for r in riders if r is not None]
    if len(riders) <= 1:
        return riders[0] if riders else None

    def run(which):
        def go(ins, outs, ssem, rsem):
            at, sem = 0, 0
            for r in riders:
                n = len(r.ins)
                getattr(r, which)(ins[at:at + n], outs[at:at + n], _SemWindow(ssem, sem), _SemWindow(rsem, sem))
                at, sem = at + n, sem + r.n_sem
        return go

    joined = _Rider(sum([list(r.ins) for r in riders], []), sum([list(r.out_shapes) for r in riders], []),
                    sum(r.n_sem for r in riders), run("start"), run("finish"))
    joined.parts = riders
    return joined


def _split_results(rider):
    at = 0
    for r in getattr(rider, "parts", []):
        r.results = rider.results[at:at + len(r.ins)]
        at += len(r.ins)


def _gather_rider(pieces):
    per = 7
    layers = [layer for _, layer in pieces]

    def first_copies(ins, outs, ssem, rsem):
        x, y, c, chip, others = _mesh_pos()
        copies = []
        for i, layer in enumerate(layers):
            mine = _half_rows(c, ins[i].shape[1] // 2)
            for j, (ox, oy) in enumerate(others):
                copies.append(_remote(ins[i].at[layer, mine, :], outs[i].at[chip, mine, :],
                                      ssem.at[per * i + j], rsem.at[per * i + j], (ox, oy, c)))
            copies.append(_remote(ins[i].at[layer], outs[i].at[chip],
                                  ssem.at[per * i + 6], rsem.at[per * i + 6], (x, y, 1 - c)))
        return copies

    def start(ins, outs, ssem, rsem):
        for cp in first_copies(ins, outs, ssem, rsem):
            cp.start()

    def finish(ins, outs, ssem, rsem):
        x, y, c, chip, others = _mesh_pos()
        sibling = (x, y, 1 - c)
        forwards = []
        for i in range(len(layers)):
            mine = _half_rows(c, ins[i].shape[1] // 2)
            for j, (ox, oy) in enumerate(others):
                rows = outs[i].at[2 * ox + oy, mine, :]
                _remote(rows, rows, ssem.at[per * i + j], rsem.at[per * i + j], (ox, oy, c)).wait_recv()
                fwd = _remote(rows, rows, ssem.at[per * i + 3 + j], rsem.at[per * i + 3 + j], sibling)
                fwd.start()
                forwards.append(fwd)
        for i in range(len(layers)):
            theirs = _half_rows(1 - c, ins[i].shape[1] // 2)
            for j, (ox, oy) in enumerate(others):
                rows = outs[i].at[2 * ox + oy, theirs, :]
                _remote(rows, rows, ssem.at[per * i + 3 + j], rsem.at[per * i + 3 + j], sibling).wait_recv()
            own = outs[i].at[chip]
            _remote(own, own, ssem.at[per * i + 6], rsem.at[per * i + 6], sibling).wait_recv()
        for cp in first_copies(ins, outs, ssem, rsem) + forwards:
            cp.wait_send()

    shapes = [jax.ShapeDtypeStruct((N_CHIPS,) + s.shape[1:], s.dtype) for s, _ in pieces]
    return _Rider([s for s, _ in pieces], shapes, per * len(pieces), start, finish)


def _chip_exchange_rider(ps):
    def copies(ins, outs, ssem, rsem):
        x, y, c, chip, others = _mesh_pos()
        return [_remote(ins[i].at[2 * ox + oy], outs[i].at[chip], ssem.at[3 * i + j], rsem.at[3 * i + j], (ox, oy, c))
                for i in range(len(ps)) for j, (ox, oy) in enumerate(others)]

    def start(ins, outs, ssem, rsem):
        for cp in copies(ins, outs, ssem, rsem):
            cp.start()

    def finish(ins, outs, ssem, rsem):
        x, y, c, chip, others = _mesh_pos()
        for i in range(len(ps)):
            for j, (ox, oy) in enumerate(others):
                slot = outs[i].at[2 * ox + oy]
                _remote(slot, slot, ssem.at[3 * i + j], rsem.at[3 * i + j], (ox, oy, c)).wait_recv()
        for cp in copies(ins, outs, ssem, rsem):
            cp.wait_send()

    return _Rider(list(ps), [jax.ShapeDtypeStruct(p.shape, p.dtype) for p in ps], 3 * len(ps), start, finish)


def _pair_exchange_rider(gs):
    def copies(ins, outs, ssem, rsem):
        x, y, c, _, _ = _mesh_pos()
        return [_remote(ins[i].at[:, _half_rows(1 - c, ins[i].shape[1] // 2), :], outs[i],
                        ssem.at[i], rsem.at[i], (x, y, 1 - c)) for i in range(len(gs))]

    def start(ins, outs, ssem, rsem):
        for cp in copies(ins, outs, ssem, rsem):
            cp.start()

    def finish(ins, outs, ssem, rsem):
        for cp in copies(ins, outs, ssem, rsem):
            cp.wait()

    shapes = [jax.ShapeDtypeStruct((g.shape[0], g.shape[1] // 2, g.shape[2]), g.dtype) for g in gs]
    return _Rider(list(gs), shapes, len(gs), start, finish)


def _run_rider(rider, name):
    def body(*refs):
        n = len(rider.ins)
        ins, outs = refs[:n], refs[n:2 * n]
        ssem, rsem = refs[2 * n:]
        rider.start(ins, outs, ssem, rsem)
        rider.finish(ins, outs, ssem, rsem)

    rider.results = pl.pallas_call(
        body,
        name=name,
        in_specs=[ANY] * len(rider.ins),
        out_specs=[ANY] * len(rider.ins),
        out_shape=rider.out_shapes,
        scratch_shapes=[pltpu.SemaphoreType.DMA((rider.n_sem,)), pltpu.SemaphoreType.DMA((rider.n_sem,))],
    )(*rider.ins)
    return rider.results


def _pair_gather(fs):
    n = len(fs)

    def body(*refs):
        bufs = refs[n:2 * n]
        ssem, rsem = refs[2 * n:]
        x, y, c, _, _ = _mesh_pos()
        sends = []
        for i in range(n):
            rh = bufs[i].shape[0] // 2
            mine = bufs[i].at[_half_rows(c, rh), :]
            cp = _remote(mine, mine, ssem.at[i], rsem.at[i], (x, y, 1 - c))
            cp.start()
            sends.append(cp)
        for i in range(n):
            rh = bufs[i].shape[0] // 2
            theirs = bufs[i].at[_half_rows(1 - c, rh), :]
            _remote(theirs, theirs, ssem.at[i], rsem.at[i], (x, y, 1 - c)).wait_recv()
        for cp in sends:
            cp.wait_send()

    return pl.pallas_call(
        body,
        name="grads_pair_gather",
        in_specs=[ANY] * n,
        out_specs=[ANY] * n,
        out_shape=[jax.ShapeDtypeStruct(f.shape, f.dtype) for f in fs],
        input_output_aliases={i: i for i in range(n)},
        scratch_shapes=[pltpu.SemaphoreType.DMA((n,)), pltpu.SemaphoreType.DMA((n,))],
    )(*fs)


def _call(body, *, name, grid, in_specs, out_specs, out_shape, operands, scratch=(), sem=None, rider=None):
    if rider is None:
        return pl.pallas_call(
            body, name=name, grid=grid, in_specs=in_specs, out_specs=out_specs, out_shape=out_shape,
            scratch_shapes=list(scratch), compiler_params=_params(sem))(*operands)
    n_in, n_out, n_sc, r = len(in_specs), len(out_specs), len(scratch), len(rider.ins)

    def carrying(*refs):
        a, b = n_in, n_in + r
        c, d = b + n_out, b + n_out + r
        e = d + n_sc
        ids = [pl.program_id(k) for k in range(len(grid))]
        first = functools.reduce(jnp.logical_and, [i == 0 for i in ids])
        last = functools.reduce(jnp.logical_and, [i == g - 1 for i, g in zip(ids, grid)])

        @pl.when(first)
        def _():
            rider.start(refs[a:b], refs[c:d], refs[e], refs[e + 1])

        body(*refs[:a], *refs[b:c], *refs[d:e])

        @pl.when(last)
        def _():
            rider.finish(refs[a:b], refs[c:d], refs[e], refs[e + 1])

    outs = pl.pallas_call(
        carrying, name=name, grid=grid,
        in_specs=list(in_specs) + [ANY] * r,
        out_specs=list(out_specs) + [ANY] * r,
        out_shape=list(out_shape) + list(rider.out_shapes),
        scratch_shapes=list(scratch) + [pltpu.SemaphoreType.DMA((rider.n_sem,)), pltpu.SemaphoreType.DMA((rider.n_sem,))],
        compiler_params=_params(("arbitrary",) * len(grid)),
    )(*operands, *rider.ins)
    rider.results = outs[n_out:]
    return outs[:n_out]


def _ffn_fwd(h, gain, wg, wu, wd, layer, tm, name, rider=None):
    T, D = h.shape
    Fs = wg.shape[-1]
    F = N_CHIPS * Fs

    def body(h_ref, g_ref, wg_ref, wu_ref, wd_ref, ho_ref, a_ref, go_ref, uo_ref, act_ref, acc_ref):
        s = pl.program_id(1)

        @pl.when(s == 0)
        def _():
            a_ref[...] = _rms_fwd(h_ref[...], g_ref[...]).astype(BF16)
            acc_ref[...] = jnp.zeros_like(acc_ref)

        a = a_ref[...]
        g = _dot(a, wg_ref[...])
        u = _dot(a, wu_ref[...])
        act = (g * _sigmoid(g) * u).astype(BF16)
        go_ref[...] = g.astype(BF16)
        uo_ref[...] = u.astype(BF16)
        act_ref[...] = act
        acc_ref[...] += _dot(act, wd_ref[...])

        @pl.when(s == N_CHIPS - 1)
        def _():
            ho_ref[...] = h_ref[...] + 0.5 * acc_ref[...]

    row = pl.BlockSpec((tm, D), lambda t, s: (t, 0))
    col = pl.BlockSpec((tm, Fs), lambda t, s: (t, s))
    wcol = pl.BlockSpec((None, D, Fs), lambda t, s: (s, 0, 0))
    return _call(
        body, name=name, grid=(T // tm, N_CHIPS),
        in_specs=[row, pl.BlockSpec((None, 1, D), lambda t, s: (layer, 0, 0)), wcol, wcol,
                  pl.BlockSpec((None, Fs, D), lambda t, s: (s, 0, 0))],
        out_specs=[row, row, col, col, col],
        out_shape=[jax.ShapeDtypeStruct((T, D), F32), jax.ShapeDtypeStruct((T, D), BF16)]
        + [jax.ShapeDtypeStruct((T, F), BF16)] * 3,
        scratch=[pltpu.VMEM((tm, D), F32)],
        sem=("parallel", "arbitrary"), operands=(h, gain, wg, wu, wd), rider=rider)


def _inproj_fwd(h, gain, win, layer, tm, name, rider=None):
    T, D = h.shape
    Ns = win.shape[-1]

    def body(h_ref, g_ref, w_ref, z_ref, b_ref):
        @pl.when(pl.program_id(1) == 0)
        def _():
            b_ref[...] = _rms_fwd(h_ref[...], g_ref[...]).astype(BF16)

        z_ref[...] = _dot(b_ref[...], w_ref[...])

    return _call(
        body, name=name, grid=(T // tm, N_CHIPS),
        in_specs=[pl.BlockSpec((tm, D), lambda t, s: (t, 0)),
                  pl.BlockSpec((None, 1, D), lambda t, s: (layer, 0, 0)),
                  pl.BlockSpec((None, D, Ns), lambda t, s: (s, 0, 0))],
        out_specs=[pl.BlockSpec((tm, Ns), lambda t, s: (t, s)), pl.BlockSpec((tm, D), lambda t, s: (t, 0))],
        out_shape=[jax.ShapeDtypeStruct((T, N_CHIPS * Ns), F32), jax.ShapeDtypeStruct((T, D), BF16)],
        sem=("parallel", "arbitrary"), operands=(h, gain, win), rider=rider)


def _ret_consts(T, pad):
    half = HEAD_DIM // 2
    inv_freq = ROPE_BASE ** (-jnp.arange(half, dtype=F32) / half)
    pos = jnp.arange(T, dtype=F32) - pad
    ang = pos[:, None] * inv_freq[None, :]
    cos = jnp.cos(ang)
    sin = jnp.sin(ang)
    cosf = jnp.concatenate([cos, cos], axis=1)
    sinf = jnp.concatenate([-sin, sin], axis=1)
    log_gamma = jnp.log1p(-(2.0 ** (-5.0 - jnp.arange(RET_HEADS, dtype=F32))))
    idx = jnp.arange(CHUNK, dtype=F32)
    diff = idx[:, None] - idx[None, :]
    intra = jnp.where(diff[None] >= 0, jnp.exp(diff[None] * log_gamma[:, None, None]), 0.0)
    k_decay = jnp.exp((CHUNK - 1.0 - idx)[None, :] * log_gamma[:, None])
    q_decay = jnp.exp((idx + 1.0)[None, :] * log_gamma[:, None])
    chunk_decay = jnp.exp(CHUNK * log_gamma)
    kdec = jnp.broadcast_to(k_decay[:, :, None], (RET_HEADS, CHUNK, HEAD_DIM))
    qdec = jnp.broadcast_to(q_decay[:, :, None], (RET_HEADS, CHUNK, HEAD_DIM))
    cdb = jnp.broadcast_to(chunk_decay[:, None, None], (RET_HEADS, 8, HEAD_DIM))
    return cosf, sinf, intra, kdec, qdec, cdb


def _rot(t, cosv, sinv):
    return t * cosv + pltpu.roll(t, HEAD_DIM // 2, 1) * sinv


def _rot_t(g, cosv, sinv):
    return g * cosv + pltpu.roll(g * sinv, HEAD_DIM // 2, 1)


def _head_specs(tg, section, order):
    return pl.BlockSpec((tg, HEAD_DIM), lambda h, g: (order(g), section * RET_HEADS + h))


def _ret_fwd(z, consts, cg, name):
    T = z.shape[0]
    N = T // CHUNK
    ng = N // cg
    tg = cg * CHUNK
    cosf, sinf, intra, kdec, qdec, cdb = consts
    fwd = lambda g: g

    def body(zq, zk, zv, zg, cos_ref, sin_ref, m_ref, kd_ref, qd_ref, cd_ref, r_ref, o_ref, s_ref, st_ref):
        @pl.when(pl.program_id(1) == 0)
        def _():
            st_ref[...] = jnp.zeros_like(st_ref)

        cosv = cos_ref[...]
        sinv = sin_ref[...]
        q3 = (_rot(zq[...], cosv, sinv) * (HEAD_DIM ** -0.5)).reshape(cg, CHUNK, HEAD_DIM)
        k3 = _rot(zk[...], cosv, sinv).reshape(cg, CHUNK, HEAD_DIM)
        vb = zv[...].reshape(cg, CHUNK, HEAD_DIM).astype(BF16)
        scores = _ein("ncd,nmd->ncm", q3.astype(BF16), k3.astype(BF16)) * m_ref[...][None]
        inner = _ein("ncm,nmd->ncd", scores.astype(BF16), vb)
        kv = _ein("ncd,nce->nde", (k3 * kd_ref[...][None]).astype(BF16), vb)
        cd = cd_ref[0:1, :]
        state = st_ref[...]
        for n in range(cg):
            s_ref[n] = state
            state = state * cd + kv[n]
        st_ref[...] = state
        qdb = (q3 * qd_ref[...][None]).astype(BF16)
        cross = _ein("ncd,nde->nce", qdb, s_ref[...].astype(BF16))
        out = (inner + cross).reshape(tg, HEAD_DIM)
        o_ref[...] = out
        xc = out - jnp.mean(out, axis=-1, keepdims=True)
        rn = xc * lax.rsqrt(jnp.mean(xc * xc, axis=-1, keepdims=True) + EPS)
        g = zg[...]
        r_ref[...] = (rn * (g * _sigmoid(g))).astype(BF16)

    tab = pl.BlockSpec((tg, HEAD_DIM), lambda h, g: (g, 0))
    per_head = lambda rows: pl.BlockSpec((None, rows, HEAD_DIM), lambda h, g: (h, 0, 0))
    head_out = pl.BlockSpec((tg, HEAD_DIM), lambda h, g: (g, h))
    return _call(
        body, name=name, grid=(RET_HEADS, ng),
        in_specs=[_head_specs(tg, i, fwd) for i in range(4)]
        + [tab, tab, per_head(CHUNK), per_head(CHUNK), per_head(CHUNK), per_head(8)],
        out_specs=[head_out, head_out, pl.BlockSpec((None, cg, HEAD_DIM, HEAD_DIM), lambda h, g: (h, g, 0, 0))],
        out_shape=[jax.ShapeDtypeStruct((T, RET_WIDTH), BF16), jax.ShapeDtypeStruct((T, RET_WIDTH), F32),
                   jax.ShapeDtypeStruct((RET_HEADS, N, HEAD_DIM, HEAD_DIM), F32)],
        scratch=[pltpu.VMEM((HEAD_DIM, HEAD_DIM), F32)],
        sem=("parallel", "arbitrary"), operands=(z, z, z, z, cosf, sinf, intra, kdec, qdec, cdb))


def _window_sums(u, shift_of):
    sums = []
    s = u
    k = 1
    while k < POOL_WINDOWS[-1]:
        s = s + pltpu.roll(s, shift_of(k), 0)
        sums.append(s)
        k *= 2
    return sums


def _select_group(vals, g):
    out = vals[-1]
    for i in range(len(vals) - 2, -1, -1):
        out = jnp.where(g == i, vals[i], out)
    return out


def _pool_parts(u, g, T, pad):
    rows = lax.broadcasted_iota(jnp.int32, (T, HEAD_DIM), 0)
    valid = rows >= pad
    win = _select_group([float(w) for w in POOL_WINDOWS], g)
    div = jnp.clip((rows - pad + 1).astype(F32), 1.0, win)
    s = _select_group(_window_sums(u, lambda k: k), g)
    pooled = jnp.where(valid, s / div - u, 0.0)
    return pooled, div, valid


def _pool_specs(T, layer):
    first = 4 * RET_WIDTH // HEAD_DIM
    return [
        pl.BlockSpec((T, HEAD_DIM), lambda g: (0, first + g)),
        pl.BlockSpec((None, None, HEAD_DIM, HEAD_DIM), lambda g: (layer, g, 0, 0)),
        pl.BlockSpec((None, 1, HEAD_DIM), lambda g: (layer, 0, g)),
    ]


def _pool_fwd(z, maps, scale, layer, pad, name):
    T = z.shape[0]
    assert pad >= POOL_WINDOWS[-1], "window rolls wrap into the zero rows in front"

    def body(zu, maps_ref, sc_ref, pm_ref):
        g = pl.program_id(0)
        pooled, _, _ = _pool_parts(zu[...], g, T, pad)
        y = _dot(pooled.astype(BF16), maps_ref[...].astype(BF16))
        pm_ref[...] = (y * sc_ref[...]).astype(BF16)

    return _call(
        body, name=name, grid=(POOL_GROUPS,),
        in_specs=_pool_specs(T, layer),
        out_specs=[pl.BlockSpec((T, HEAD_DIM), lambda g: (0, g))],
        out_shape=[jax.ShapeDtypeStruct((T, POOL_WIDTH), BF16)],
        sem=("parallel",), operands=(z, maps, scale))[0]


def _gate_specs(tm, D):
    nb = D // RET_WIDTH
    first = (4 * RET_WIDTH + POOL_WIDTH) // RET_WIDTH
    return [pl.BlockSpec((tm, RET_WIDTH), functools.partial(lambda t, j: (t, j), j=first + j)) for j in range(2 * nb)]


def _load_gates(refs, nb):
    ga = jnp.concatenate([r[...] for r in refs[:nb]], axis=1) if nb > 1 else refs[0][...]
    gb = jnp.concatenate([r[...] for r in refs[nb:]], axis=1) if nb > 1 else refs[nb][...]
    return ga, gb


def _mix_fwd(h, r, pm, z, wru, wpu, wout, tm, name, rider=None):
    T, D = h.shape
    Dq = D // N_CHIPS
    nb = D // RET_WIDTH

    def body(*refs):
        h_ref, r_ref, pm_ref = refs[:3]
        gate_refs = refs[3:3 + 2 * nb]
        wru_ref, wpu_ref, wout_ref, ho_ref, mx_ref, ret_ref, pool_ref = refs[3 + 2 * nb:]
        rv = r_ref[...]
        pv = pm_ref[...]
        ret = jnp.concatenate([_dot(rv, wru_ref[s]) for s in range(N_CHIPS)], axis=1)
        pool = jnp.concatenate([_dot(pv, wpu_ref[s]) for s in range(N_CHIPS)], axis=1)
        ga, gb = _load_gates(gate_refs, nb)
        mixed = (_sigmoid(ga) * ret + _sigmoid(gb) * pool).astype(BF16)
        mx_ref[...] = mixed
        ret_ref[...] = ret.astype(BF16)
        pool_ref[...] = pool.astype(BF16)
        ho_ref[...] = h_ref[...] + _dot(mixed, wout_ref[...].reshape(D, D))

    row = pl.BlockSpec((tm, D), lambda t: (t, 0))
    half = pl.BlockSpec((tm, RET_WIDTH), lambda t: (t, 0))
    up = pl.BlockSpec((N_CHIPS, RET_WIDTH, Dq), lambda t: (0, 0, 0))
    return _call(
        body, name=name, grid=(T // tm,),
        in_specs=[row, half, half] + _gate_specs(tm, D) + [up, up, pl.BlockSpec((N_CHIPS, Dq, D), lambda t: (0, 0, 0))],
        out_specs=[row, row, row, row],
        out_shape=[jax.ShapeDtypeStruct((T, D), F32)] + [jax.ShapeDtypeStruct((T, D), BF16)] * 3,
        sem=("parallel",), operands=(h, r, pm, *([z] * (2 * nb)), wru, wpu, wout), rider=rider)


def _final_loss(h, gain, tgt, name):
    T, D = h.shape
    first = (T - tgt.shape[0]) // CHUNK

    def body(h_ref, g_ref, t_ref, dh_ref, loss_ref, dg_ref):
        i = pl.program_id(0)

        @pl.when(i == 0)
        def _():
            loss_ref[...] = jnp.zeros_like(loss_ref)
            dg_ref[...] = jnp.zeros_like(dg_ref)

        x = h_ref[...]
        gain_v = g_ref[...]
        err = jnp.where(i >= first, _rms_fwd(x, gain_v) - t_ref[...], 0.0)
        loss_ref[...] += 0.5 * jnp.sum(jnp.mean(err * err, axis=-1))
        dx, dgain = _rms_bwd(x, gain_v, err * (1.0 / D))
        dg_ref[...] += dgain
        dh_ref[...] = dx

    return _call(
        body, name=name, grid=(T // CHUNK,),
        in_specs=[pl.BlockSpec((CHUNK, D), lambda i: (i, 0)),
                  pl.BlockSpec((1, D), lambda i: (0, 0)),
                  pl.BlockSpec((CHUNK, D), lambda i: (jnp.maximum(i - first, 0), 0))],
        out_specs=[pl.BlockSpec((CHUNK, D), lambda i: (i, 0)),
                   pl.BlockSpec((1, LANES), lambda i: (0, 0)),
                   pl.BlockSpec((1, D), lambda i: (0, 0))],
        out_shape=[jax.ShapeDtypeStruct((T, D), F32), jax.ShapeDtypeStruct((1, LANES), F32),
                   jax.ShapeDtypeStruct((1, D), F32)],
        sem=("arbitrary",), operands=(h, gain, tgt))


def _ffn_bwd_dx(dy, h, gain, g, u, wg, wu, wd, layer, tm, pad, name, rider=None):
    T, D = h.shape
    Fs = wg.shape[-1]
    F = N_CHIPS * Fs

    def body(dy_ref, h_ref, g_ref, go_ref, uo_ref, wg_ref, wu_ref, wd_ref,
             dh_ref, dg_ref, du_ref, dgain_ref, dyh_ref, da_ref):
        t = pl.program_id(0)
        s = pl.program_id(1)

        @pl.when((t == 0) & (s == 0))
        def _():
            dgain_ref[...] = jnp.zeros_like(dgain_ref)

        @pl.when(s == 0)
        def _():
            dyh_ref[...] = (0.5 * dy_ref[...]).astype(BF16)
            da_ref[...] = jnp.zeros_like(da_ref)

        dact = _dot_nt(dyh_ref[...], wd_ref[...])
        gf = go_ref[...].astype(F32)
        uf = uo_ref[...].astype(F32)
        sg = _sigmoid(gf)
        du = (dact * (gf * sg)).astype(BF16)
        dg = (dact * uf * (sg * (1.0 + gf * (1.0 - sg)))).astype(BF16)
        dg_ref[...] = dg
        du_ref[...] = du
        da_ref[...] += _dot_nt(dg, wg_ref[...]) + _dot_nt(du, wu_ref[...])

        @pl.when(s == N_CHIPS - 1)
        def _():
            dx, dgain = _rms_bwd(h_ref[...], g_ref[...], da_ref[...])
            dgain_ref[...] += dgain
            dh_ref[...] = jnp.where(_row_mask(t, tm, pad, (tm, D)), dy_ref[...] + dx, 0.0)

    row = pl.BlockSpec((tm, D), lambda t, s: (t, 0))
    col = pl.BlockSpec((tm, Fs), lambda t, s: (t, s))
    wcol = pl.BlockSpec((None, D, Fs), lambda t, s: (s, 0, 0))
    return _call(
        body, name=name, grid=(T // tm, N_CHIPS),
        in_specs=[row, row, pl.BlockSpec((None, 1, D), lambda t, s: (layer, 0, 0)), col, col, wcol, wcol,
                  pl.BlockSpec((None, Fs, D), lambda t, s: (s, 0, 0))],
        out_specs=[row, col, col, pl.BlockSpec((1, D), lambda t, s: (0, 0)), row],
        out_shape=[jax.ShapeDtypeStruct((T, D), F32), jax.ShapeDtypeStruct((T, F), BF16),
                   jax.ShapeDtypeStruct((T, F), BF16), jax.ShapeDtypeStruct((1, D), F32),
                   jax.ShapeDtypeStruct((T, D), BF16)],
        scratch=[pltpu.VMEM((tm, D), F32)],
        sem=("arbitrary", "arbitrary"), operands=(dy, h, gain, g, u, wg, wu, wd), rider=rider)


def _grad_tn(a, b, mode, scale, tm, name, rider=None):
    T = a.shape[0]
    if mode == "col":
        R, C = a.shape[1], b.shape[1] // N_CHIPS
        a_spec = pl.BlockSpec((tm, R), lambda s, t: (t, 0))
        b_spec = pl.BlockSpec((tm, C), lambda s, t: (t, s))
    else:
        R, C = a.shape[1] // N_CHIPS, b.shape[1]
        a_spec = pl.BlockSpec((tm, R), lambda s, t: (t, s))
        b_spec = pl.BlockSpec((tm, C), lambda s, t: (t, 0))
    nt = T // tm

    def body(a_ref, b_ref, o_ref, acc_ref):
        t = pl.program_id(1)

        @pl.when(t == 0)
        def _():
            acc_ref[...] = jnp.zeros_like(acc_ref)

        acc_ref[...] += _dot_tn(a_ref[...].astype(BF16), b_ref[...].astype(BF16))

        @pl.when(t == nt - 1)
        def _():
            o_ref[...] = (scale * acc_ref[...]).astype(BF16)

    return _call(
        body, name=name, grid=(N_CHIPS, nt),
        in_specs=[a_spec, b_spec],
        out_specs=[pl.BlockSpec((None, R, C), lambda s, t: (s, 0, 0))],
        out_shape=[jax.ShapeDtypeStruct((N_CHIPS, R, C), BF16)],
        scratch=[pltpu.VMEM((R, C), F32)],
        sem=("parallel", "arbitrary"), operands=(a, b), rider=rider)[0]


def _mix_bwd_dx(dh, z, ret, pool, wout, wru, wpu, tm, name, rider=None):
    T, D = dh.shape
    Dq = D // N_CHIPS
    nb = D // RET_WIDTH

    def body(*refs):
        dh_ref = refs[0]
        gate_refs = refs[1:1 + 2 * nb]
        ret_ref, pool_ref, wout_ref, wru_ref, wpu_ref, dgab_ref, dret_ref, dpool_ref, dr_ref, dpm_ref = refs[1 + 2 * nb:]
        dmixed = _dot_nt(dh_ref[...].astype(BF16), wout_ref[...].reshape(D, D))
        ga, gb = _load_gates(gate_refs, nb)
        sa = _sigmoid(ga)
        sb = _sigmoid(gb)
        dgab_ref[:, :D] = (dmixed * ret_ref[...].astype(F32) * (sa * (1.0 - sa))).astype(BF16)
        dgab_ref[:, D:] = (dmixed * pool_ref[...].astype(F32) * (sb * (1.0 - sb))).astype(BF16)
        dret = (dmixed * sa).astype(BF16)
        dpool = (dmixed * sb).astype(BF16)
        dret_ref[...] = dret
        dpool_ref[...] = dpool
        dr = _dot_nt(dret[:, :Dq], wru_ref[0])
        dpm = _dot_nt(dpool[:, :Dq], wpu_ref[0])
        for s in range(1, N_CHIPS):
            dr += _dot_nt(dret[:, s * Dq:(s + 1) * Dq], wru_ref[s])
            dpm += _dot_nt(dpool[:, s * Dq:(s + 1) * Dq], wpu_ref[s])
        dr_ref[...] = dr
        dpm_ref[...] = dpm

    row = pl.BlockSpec((tm, D), lambda t: (t, 0))
    half = pl.BlockSpec((tm, RET_WIDTH), lambda t: (t, 0))
    up = pl.BlockSpec((N_CHIPS, RET_WIDTH, Dq), lambda t: (0, 0, 0))
    return _call(
        body, name=name, grid=(T // tm,),
        in_specs=[row] + _gate_specs(tm, D) + [row, row, pl.BlockSpec((N_CHIPS, Dq, D), lambda t: (0, 0, 0)), up, up],
        out_specs=[pl.BlockSpec((tm, 2 * D), lambda t: (t, 0)), row, row, half, half],
        out_shape=[jax.ShapeDtypeStruct((T, 2 * D), BF16), jax.ShapeDtypeStruct((T, D), BF16),
                   jax.ShapeDtypeStruct((T, D), BF16), jax.ShapeDtypeStruct((T, RET_WIDTH), F32),
                   jax.ShapeDtypeStruct((T, POOL_WIDTH), F32)],
        sem=("parallel",), operands=(dh, *([z] * (2 * nb)), ret, pool, wout, wru, wpu), rider=rider)


def _pool_bwd(z, dpm, maps, scale, layer, pad, name):
    T = z.shape[0]

    def body(zu, maps_ref, sc_ref, dpm_ref, du_ref, dmaps_ref, dsc_ref):
        g = pl.program_id(0)
        u = zu[...]
        pooled, div, valid = _pool_parts(u, g, T, pad)
        pb = pooled.astype(BF16)
        mb = maps_ref[...].astype(BF16)
        dp = dpm_ref[...]
        dsc_ref[...] = jnp.sum(dp * _dot(pb, mb), axis=0, keepdims=True)
        dyb = (dp * sc_ref[...]).astype(BF16)
        dmaps_ref[...] = _dot_tn(pb, dyb)
        dpooled = jnp.where(valid, _dot_nt(dyb, mb), 0.0)
        ahead = _select_group(_window_sums(dpooled / div, lambda k: T - k), g)
        du_ref[...] = jnp.where(valid, ahead - dpooled, 0.0).astype(BF16)

    blk = pl.BlockSpec((T, HEAD_DIM), lambda g: (0, g))
    return _call(
        body, name=name, grid=(POOL_GROUPS,),
        in_specs=_pool_specs(T, layer) + [blk],
        out_specs=[blk, pl.BlockSpec((None, HEAD_DIM, HEAD_DIM), lambda g: (g, 0, 0)),
                   pl.BlockSpec((1, HEAD_DIM), lambda g: (0, g))],
        out_shape=[jax.ShapeDtypeStruct((T, POOL_WIDTH), BF16),
                   jax.ShapeDtypeStruct((POOL_GROUPS, HEAD_DIM, HEAD_DIM), F32),
                   jax.ShapeDtypeStruct((1, POOL_WIDTH), F32)],
        sem=("parallel",), operands=(z, maps, scale, dpm))


def _ret_bwd_local(z, o_pre, s_all, dr, consts, cg, name):
    T = z.shape[0]
    N = T // CHUNK
    ng = N // cg
    tg = cg * CHUNK
    cosf, sinf, intra, _, qdec, _ = consts
    fwd = lambda g: g

    def body(zq, zk, zv, zg, o_ref, s_ref, dr_ref, cos_ref, sin_ref, m_ref, qd_ref,
             dq_ref, dg_ref, dk_ref, dv_ref, ds_ref):
        cosv = cos_ref[...]
        sinv = sin_ref[...]
        scale = HEAD_DIM ** -0.5
        q3 = (_rot(zq[...], cosv, sinv) * scale).reshape(cg, CHUNK, HEAD_DIM)
        k3 = _rot(zk[...], cosv, sinv).reshape(cg, CHUNK, HEAD_DIM)
        qb = q3.astype(BF16)
        kb = k3.astype(BF16)
        vb = zv[...].reshape(cg, CHUNK, HEAD_DIM).astype(BF16)
        mask = m_ref[...][None]
        sb = (_ein("ncd,nmd->ncm", qb, kb) * mask).astype(BF16)
        qdv = qd_ref[...][None]
        qdb = (q3 * qdv).astype(BF16)

        out = o_ref[...]
        xc = out - jnp.mean(out, axis=-1, keepdims=True)
        rstd = lax.rsqrt(jnp.mean(xc * xc, axis=-1, keepdims=True) + EPS)
        rn = xc * rstd
        g = zg[...]
        sg = _sigmoid(g)
        drv = dr_ref[...]
        dg_ref[...] = (drv * rn * (sg * (1.0 + g * (1.0 - sg)))).astype(BF16)
        drn = drv * (g * sg)
        dout = rstd * (drn - jnp.mean(drn, axis=-1, keepdims=True)
                       - rn * jnp.mean(drn * rn, axis=-1, keepdims=True))
        dob = dout.reshape(cg, CHUNK, HEAD_DIM).astype(BF16)

        dsb = (_ein("ncd,nmd->ncm", dob, vb) * mask).astype(BF16)
        dv_ref[...] = _ein("ncm,ncd->nmd", sb, dob).reshape(tg, HEAD_DIM)
        dk_ref[...] = _ein("ncm,ncd->nmd", dsb, qb).reshape(tg, HEAD_DIM)
        dq3 = _ein("ncm,nmd->ncd", dsb, kb) + _ein("nce,nde->ncd", dob, s_ref[...].astype(BF16)) * qdv
        dq_ref[...] = _rot_t(dq3.reshape(tg, HEAD_DIM) * scale, cosv, sinv).astype(BF16)
        ds_ref[...] = _ein("ncd,nce->nde", qdb, dob)

    tab = pl.BlockSpec((tg, HEAD_DIM), lambda h, g: (g, 0))
    per_head = pl.BlockSpec((None, CHUNK, HEAD_DIM), lambda h, g: (h, 0, 0))
    head_blk = pl.BlockSpec((tg, HEAD_DIM), lambda h, g: (g, h))
    state_blk = pl.BlockSpec((None, cg, HEAD_DIM, HEAD_DIM), lambda h, g: (h, g, 0, 0))
    return _call(
        body, name=name, grid=(RET_HEADS, ng),
        in_specs=[_head_specs(tg, i, fwd) for i in range(4)]
        + [head_blk, state_blk, head_blk, tab, tab, per_head, per_head],
        out_specs=[head_blk, head_blk, head_blk, head_blk, state_blk],
        out_shape=[jax.ShapeDtypeStruct((T, RET_WIDTH), BF16), jax.ShapeDtypeStruct((T, RET_WIDTH), BF16),
                   jax.ShapeDtypeStruct((T, RET_WIDTH), F32), jax.ShapeDtypeStruct((T, RET_WIDTH), F32),
                   jax.ShapeDtypeStruct((RET_HEADS, N, HEAD_DIM, HEAD_DIM), F32)],
        sem=("parallel", "parallel"), operands=(z, z, z, z, o_pre, s_all, dr, cosf, sinf, intra, qdec))


def _ret_bwd_state(z, dkp, dvp, ds, consts, cg, name):
    T = z.shape[0]
    N = T // CHUNK
    ng = N // cg
    tg = cg * CHUNK
    cosf, sinf, _, kdec, _, cdb = consts
    rev = lambda g: ng - 1 - g

    def body(zk, zv, dkp_ref, dvp_ref, ds_ref, cos_ref, sin_ref, kd_ref, cd_ref, dk_ref, dv_ref, gs_ref, dkv_ref):
        @pl.when(pl.program_id(1) == 0)
        def _():
            gs_ref[...] = jnp.zeros_like(gs_ref)

        cosv = cos_ref[...]
        sinv = sin_ref[...]
        cd = cd_ref[0:1, :]
        grad = gs_ref[...]
        for n in reversed(range(cg)):
            dkv_ref[n] = grad
            grad = ds_ref[n] + cd * grad
        gs_ref[...] = grad
        dkvb = dkv_ref[...].astype(BF16)
        kdv = kd_ref[...][None]
        k3 = _rot(zk[...], cosv, sinv).reshape(cg, CHUNK, HEAD_DIM)
        vb = zv[...].reshape(cg, CHUNK, HEAD_DIM).astype(BF16)
        dk3 = _ein("nce,nde->ncd", vb, dkvb) * kdv
        dv3 = _ein("ncd,nde->nce", (k3 * kdv).astype(BF16), dkvb)
        dk_ref[...] = _rot_t(dkp_ref[...] + dk3.reshape(tg, HEAD_DIM), cosv, sinv).astype(BF16)
        dv_ref[...] = (dvp_ref[...] + dv3.reshape(tg, HEAD_DIM)).astype(BF16)

    tab = pl.BlockSpec((tg, HEAD_DIM), lambda h, g: (rev(g), 0))
    head_blk = pl.BlockSpec((tg, HEAD_DIM), lambda h, g: (rev(g), h))
    return _call(
        body, name=name, grid=(RET_HEADS, ng),
        in_specs=[_head_specs(tg, 1, rev), _head_specs(tg, 2, rev), head_blk, head_blk,
                  pl.BlockSpec((None, cg, HEAD_DIM, HEAD_DIM), lambda h, g: (h, rev(g), 0, 0)),
                  tab, tab,
                  pl.BlockSpec((None, CHUNK, HEAD_DIM), lambda h, g: (h, 0, 0)),
                  pl.BlockSpec((None, 8, HEAD_DIM), lambda h, g: (h, 0, 0))],
        out_specs=[head_blk, head_blk],
        out_shape=[jax.ShapeDtypeStruct((T, RET_WIDTH), BF16)] * 2,
        scratch=[pltpu.VMEM((HEAD_DIM, HEAD_DIM), F32), pltpu.VMEM((cg, HEAD_DIM, HEAD_DIM), F32)],
        sem=("parallel", "arbitrary"), operands=(z, z, dkp, dvp, ds, cosf, sinf, kdec, cdb))


def _inproj_bwd_dx(dz, win, h, gain, dh_in, layer, tm, pad, name, rider=None):
    T, D = h.shape
    Ns = win.shape[-1]

    def body(dz_ref, w_ref, h_ref, g_ref, dhi_ref, dh_ref, dgain_ref, db_ref):
        t = pl.program_id(0)
        s = pl.program_id(1)

        @pl.when((t == 0) & (s == 0))
        def _():
            dgain_ref[...] = jnp.zeros_like(dgain_ref)

        @pl.when(s == 0)
        def _():
            db_ref[...] = jnp.zeros_like(db_ref)

        db_ref[...] += _dot_nt(dz_ref[...], w_ref[...])

        @pl.when(s == N_CHIPS - 1)
        def _():
            dx, dgain = _rms_bwd(h_ref[...], g_ref[...], db_ref[...])
            dgain_ref[...] += dgain
            dh_ref[...] = jnp.where(_row_mask(t, tm, pad, (tm, D)), dhi_ref[...] + dx, 0.0)

    row = pl.BlockSpec((tm, D), lambda t, s: (t, 0))
    return _call(
        body, name=name, grid=(T // tm, N_CHIPS),
        in_specs=[pl.BlockSpec((tm, Ns), lambda t, s: (t, s)),
                  pl.BlockSpec((None, D, Ns), lambda t, s: (s, 0, 0)),
                  row, pl.BlockSpec((None, 1, D), lambda t, s: (layer, 0, 0)), row],
        out_specs=[row, pl.BlockSpec((1, D), lambda t, s: (0, 0))],
        out_shape=[jax.ShapeDtypeStruct((T, D), F32), jax.ShapeDtypeStruct((1, D), F32)],
        scratch=[pltpu.VMEM((tm, D), F32)],
        sem=("arbitrary", "arbitrary"), operands=(dz, win, h, gain, dh_in), rider=rider)


def _sum_pair(gs, rs, c_idx, name):
    n = len(gs)

    def body(c_ref, *refs):
        for g_ref, r_ref, o_ref in zip(refs[:n], refs[n:2 * n], refs[2 * n:]):
            o_ref[...] = (g_ref[...].astype(F32) + r_ref[...].astype(F32)).astype(BF16)

    halves = [pl.BlockSpec((None,) + r.shape[1:], lambda s, c_ref: (s, 0, 0)) for r in rs]
    return pl.pallas_call(
        body,
        name=name,
        grid_spec=pltpu.PrefetchScalarGridSpec(
            num_scalar_prefetch=1,
            grid=(N_CHIPS,),
            in_specs=[pl.BlockSpec((None,) + r.shape[1:], lambda s, c_ref: (s, c_ref[0], 0)) for r in rs] + halves,
            out_specs=halves,
        ),
        out_shape=[jax.ShapeDtypeStruct(r.shape, BF16) for r in rs],
        compiler_params=_params(("parallel",)),
    )(c_idx, *gs, *rs)


def _sum_chips(ps, rs, pos, name):
    n = len(ps)
    quarters = 4

    def body(pos_ref, *refs):
        chip = pos_ref[0]
        for p_ref, r_ref, o_ref in zip(refs[:n], refs[n:2 * n], refs[2 * n:]):
            own = p_ref[...].astype(F32)
            terms = [jnp.where(chip == k, own, r_ref[k].astype(F32)) for k in range(N_CHIPS)]
            o_ref[...] = ((terms[0] + terms[1]) + terms[2]) + terms[3]

    def rows(r):
        assert r.shape[1] % (quarters * BF16_ROWS) == 0, r.shape
        return r.shape[1] // quarters

    return pl.pallas_call(
        body,
        name=name,
        grid_spec=pltpu.PrefetchScalarGridSpec(
            num_scalar_prefetch=1,
            grid=(quarters,),
            in_specs=[pl.BlockSpec((None, rows(r), r.shape[2]), lambda q, pos_ref: (pos_ref[0], q, 0)) for r in rs]
            + [pl.BlockSpec((N_CHIPS, rows(r), r.shape[2]), lambda q, pos_ref: (0, q, 0)) for r in rs],
            out_specs=[pl.BlockSpec((rows(r), r.shape[2]), lambda q, pos_ref: (pos_ref[1] * quarters + q, 0))
                       for r in rs],
        ),
        out_shape=[jax.ShapeDtypeStruct((2 * r.shape[1], r.shape[2]), F32) for r in rs],
        compiler_params=_params(("arbitrary",)),
    )(pos, *ps, *rs)


def _small_all_reduce(p):
    rows, width = p.shape

    def body(p_ref, o_ref, sib_ref, slot_ref, ssem, rsem):
        x, y, c, chip, others = _mesh_pos()
        pair = _remote(p_ref, sib_ref, ssem.at[0], rsem.at[0], (x, y, 1 - c))
        pair.start()
        pair.wait()
        slot_ref[chip] = p_ref[...] + sib_ref[...]
        sends = []
        for j, (ox, oy) in enumerate(others):
            cp = _remote(slot_ref.at[chip], slot_ref.at[chip], ssem.at[1 + j], rsem.at[1 + j], (ox, oy, c))
            cp.start()
            sends.append(cp)
        for j, (ox, oy) in enumerate(others):
            slot = slot_ref.at[2 * ox + oy]
            _remote(slot, slot, ssem.at[1 + j], rsem.at[1 + j], (ox, oy, c)).wait_recv()
        for cp in sends:
            cp.wait_send()
        o_ref[...] = ((slot_ref[0] + slot_ref[1]) + slot_ref[2]) + slot_ref[3]

    vmem = pl.BlockSpec(memory_space=pltpu.VMEM)
    return pl.pallas_call(
        body,
        name="small_grads_all_reduce",
        in_specs=[vmem],
        out_specs=vmem,
        out_shape=jax.ShapeDtypeStruct(p.shape, F32),
        scratch_shapes=[pltpu.VMEM((rows, width), F32), pltpu.VMEM((N_CHIPS, rows, width), F32),
                        pltpu.SemaphoreType.DMA((4,)), pltpu.SemaphoreType.DMA((4,))],
    )(p)


def _adamw(gs, w, m, v, name):
    L, R, C = w.shape
    Ct = gs[0].shape[1]
    tr = _pick_tile(R, 256, 8)

    def body(*refs):
        g_refs = refs[:L]
        w_ref, m_ref, v_ref, go_ref, d_ref, mo_ref, vo_ref = refs[L:]
        layer = pl.program_id(0)
        grad = g_refs[L - 1][...]
        for i in range(L - 2, -1, -1):
            grad = jnp.where(layer == i, g_refs[i][...], grad)
        if Ct != C:
            grad = grad[:, :C]
        m_new = ADAM_B1 * m_ref[...] + (1.0 - ADAM_B1) * grad
        v_new = ADAM_B2 * v_ref[...] + (1.0 - ADAM_B2) * jnp.square(grad)
        m_hat = m_new / (1.0 - ADAM_B1 ** ADAM_STEP)
        v_hat = v_new / (1.0 - ADAM_B2 ** ADAM_STEP)
        go_ref[...] = grad
        d_ref[...] = -ADAM_LR * (m_hat / (jnp.sqrt(v_hat) + ADAM_EPS) + ADAM_WD * w_ref[...])
        mo_ref[...] = m_new
        vo_ref[...] = v_new

    g_specs = [pl.BlockSpec((tr, Ct), functools.partial(lambda l, r, i: (jnp.where(l == i, r, 0), 0), i=i))
               for i in range(L)]
    blk = pl.BlockSpec((None, tr, C), lambda l, r: (l, r, 0))
    return pl.pallas_call(
        body,
        name=name,
        grid=(L, R // tr),
        in_specs=g_specs + [blk, blk, blk],
        out_specs=[blk] * 4,
        out_shape=[jax.ShapeDtypeStruct((L, R, C), F32)] * 4,
        compiler_params=_params(("arbitrary", "arbitrary")),
    )(*gs, w, m, v)


_FFN1 = ("ffn1_gate", "ffn1_up", "ffn1_down")
_FFN2 = ("ffn2_gate", "ffn2_up", "ffn2_down")
_MIXW = ("w_ret_up", "w_pool_up", "w_out")
_BIG = _FFN1 + ("w_in",) + _MIXW + _FFN2
_TRANSPOSED = ("ffn1_gate", "ffn1_up", "ffn2_gate", "ffn2_up")
_SMALL = ("ffn1_norm", "mix_norm", "ffn2_norm", "final_norm", "pool_scale", "pool_maps")
_ORDER = ("meta", "ffn1_norm", "ffn1_gate", "ffn1_up", "ffn1_down", "mix_norm", "w_in", "pool_maps",
          "pool_scale", "w_ret_up", "w_pool_up", "w_out", "ffn2_norm", "ffn2_gate", "ffn2_up", "ffn2_down",
          "final_norm")


def _transport(a):
    r, c = a.shape[1], a.shape[2]
    return jnp.pad(a.astype(BF16), ((0, 0), (0, _round_up(r, LANES) - r), (0, _round_up(c, LANES) - c)))


def _pack_rows(parts, width):
    rows = [p.reshape(-1, width) for p in parts]
    total = sum(r.shape[0] for r in rows)
    fill = _round_up(total, 8) - total
    if fill:
        rows.append(jnp.zeros((fill, width), F32))
    return jnp.concatenate(rows, axis=0)


def _unpack_rows(packed, shapes, width):
    out, at = [], 0
    for shp in shapes:
        n = math.prod(shp) // width
        out.append(packed[at:at + n].reshape(shp))
        at += n
    return out


class _Weights:
    def __init__(self, shards):
        self.shards = shards
        self.full = {}

    def rider(self, keys):
        r = _gather_rider([(self.shards[n], i) for n, i in keys])
        r.keys = keys
        return r

    def take(self, rider):
        for key, arr in zip(rider.keys, rider.results):
            self.full[key] = arr

    def __call__(self, name, layer):
        return self.full[(name, layer)]


def _local_step(x, meta_full, tgt, w, wts, pad, tm, cg, reducer):
    D = x.shape[1]
    T = pad + N_META + x.shape[0]
    L = w["ffn1_norm"].shape[0]
    pool_maps = w["pool_maps"]
    gains = {n: w[n].reshape(L, 1, D) for n in ("ffn1_norm", "mix_norm", "ffn2_norm")}
    scale3 = w["pool_scale"].reshape(L, 1, POOL_WIDTH)
    consts = _ret_consts(T, pad)
    tl = _pick_tile(T, 2 * tm, BF16_ROWS)

    def gather(keys):
        return wts.rider(keys) if keys and keys[0] not in wts.full else None

    def done(rider):
        if rider is not None:
            wts.take(rider)

    h = jnp.concatenate([jnp.zeros((pad, D), F32), meta_full, x], axis=0)
    saved = []
    for i in range(L):
        s = {"h0": h}
        rd = gather([("w_in", i)] + [(n, i) for n in _MIXW])
        h, s["a1"], s["g1"], s["u1"], s["act1"] = _ffn_fwd(
            h, gains["ffn1_norm"], wts("ffn1_gate", i), wts("ffn1_up", i), wts("ffn1_down", i), i, tl,
            f"ffn1_fwd_{i}", rd)
        done(rd)
        s["h1"] = h
        rd = gather([("ffn2_gate", i), ("ffn2_up", i)])
        s["z"], s["b"] = _inproj_fwd(h, gains["mix_norm"], wts("w_in", i), i, tl, f"inproj_fwd_{i}", rd)
        done(rd)
        s["r"], s["o_pre"], s["s_all"] = _ret_fwd(s["z"], consts, cg, f"retention_fwd_{i}")
        s["pm"] = _pool_fwd(s["z"], pool_maps, scale3, i, pad, f"pool_fwd_{i}")
        rd = gather([("ffn2_down", i)])
        h, s["mixed"], s["ret"], s["pool"] = _mix_fwd(
            h, s["r"], s["pm"], s["z"], wts("w_ret_up", i), wts("w_pool_up", i), wts("w_out", i), tm,
            f"mix_fwd_{i}", rd)
        done(rd)
        s["h2"] = h
        rd = gather([(n, i + 1) for n in _FFN1]) if i + 1 < L else None
        h, s["a2"], s["g2"], s["u2"], s["act2"] = _ffn_fwd(
            h, gains["ffn2_norm"], wts("ffn2_gate", i), wts("ffn2_up", i), wts("ffn2_down", i), i, tl,
            f"ffn2_fwd_{i}", rd)
        done(rd)
        saved.append(s)

    dh, loss_acc, d_final = _final_loss(h, w["final_norm"].reshape(1, D), tgt, "final_norm_loss")

    small = {n: [None] * L for n in ("ffn1_norm", "mix_norm", "ffn2_norm", "pool_scale", "pool_maps")}

    carry = {"ffn_bwd": 4.0, "mix_bwd": 1.0, "inproj_bwd": 2.0, "w_in": 2.0, "w_out": 0.5, "w_ret_up": 0.5,
             "w_pool_up": 0.5}

    tk = _pick_tile(T, 1408, LANES)

    def grad(n, a, b, i, mode):
        rd = reducer.rider(carry.get(n, 1.5))
        reducer.add(n, i, _grad_tn(a, b, mode, 1.0, tk, f"grad_{n}_{i}", rd))
        reducer.done(rd)

    for i in reversed(range(L)):
        s = saved[i]
        rd = reducer.rider(carry["ffn_bwd"])
        dh, dg, du, small["ffn2_norm"][i], dyh = _ffn_bwd_dx(
            dh, s["h2"], gains["ffn2_norm"], s["g2"], s["u2"], wts("ffn2_gate", i), wts("ffn2_up", i),
            wts("ffn2_down", i), i, tm, pad, f"ffn2_bwd_{i}", rd)
        reducer.done(rd)
        grad("ffn2_gate", dg, s["a2"], i, "row")
        grad("ffn2_up", du, s["a2"], i, "row")
        grad("ffn2_down", s["act2"], dyh, i, "row")
        rd = reducer.rider(carry["mix_bwd"])
        dgab, dret, dpool, dr, dpm = _mix_bwd_dx(
            dh, s["z"], s["ret"], s["pool"], wts("w_out", i), wts("w_ret_up", i), wts("w_pool_up", i), tm,
            f"mix_bwd_{i}", rd)
        reducer.done(rd)
        grad("w_out", s["mixed"], dh, i, "row")
        grad("w_ret_up", s["r"], dret, i, "col")
        grad("w_pool_up", s["pm"], dpool, i, "col")
        du_pool, small["pool_maps"][i], small["pool_scale"][i] = _pool_bwd(
            s["z"], dpm, pool_maps, scale3, i, pad, f"pool_bwd_{i}")
        dq, dgr, dkp, dvp, ds = _ret_bwd_local(s["z"], s["o_pre"], s["s_all"], dr, consts, cg, f"retention_bwd_{i}")
        dk, dv = _ret_bwd_state(s["z"], dkp, dvp, ds, consts, cg, f"retention_bwd_state_{i}")
        dz = jnp.concatenate([dq, dk, dv, dgr, du_pool, dgab], axis=1)
        dh2 = dh
        rd = reducer.rider(carry["inproj_bwd"])
        dh, small["mix_norm"][i] = _inproj_bwd_dx(
            dz, wts("w_in", i), s["h1"], gains["mix_norm"], dh2, i, tl, pad, f"inproj_bwd_{i}", rd)
        reducer.done(rd)
        grad("w_in", s["b"], dz, i, "col")
        reducer.stage(f"mid{i}")
        rd = reducer.rider(carry["ffn_bwd"])
        dh, dg, du, small["ffn1_norm"][i], dyh = _ffn_bwd_dx(
            dh, s["h0"], gains["ffn1_norm"], s["g1"], s["u1"], wts("ffn1_gate", i), wts("ffn1_up", i),
            wts("ffn1_down", i), i, tm, pad, f"ffn1_bwd_{i}", rd)
        reducer.done(rd)
        grad("ffn1_gate", dg, s["a1"], i, "row")
        grad("ffn1_up", du, s["a1"], i, "row")
        grad("ffn1_down", s["act1"], dyh, i, "row")
        reducer.stage(f"end{i}")

    return loss_acc, dh, small, d_final


class _Reducer:
    def __init__(self, unit):
        self.c_idx = lax.axis_index("c").astype(jnp.int32).reshape(1)
        chip = 2 * lax.axis_index("x") + lax.axis_index("y")
        self.pos = jnp.stack([chip, lax.axis_index("c")]).astype(jnp.int32)
        self.pending, self.stages, self.queue, self.halves = [], [], [], {}
        self.unit = unit
        self.calls = 0

    def add(self, name, layer, g):
        self.pending.append(((name, layer), g))

    def stage(self, tag):
        if self.pending:
            self.stages.append((tag, self.pending))
            self.pending = []

    def _pair_rider(self):
        if not self.stages:
            return None
        tag, items = self.stages.pop(0)
        rd = _pair_exchange_rider([g for _, g in items])
        rd.tag, rd.keys = tag, [k for k, _ in items]
        return rd

    def _chip_rider(self, units):
        take, size = [], 0
        while self.queue and (units is None or size + self.queue[0][1].size <= units * self.unit):
            take.append(self.queue.pop(0))
            size += take[-1][1].size
        if not take:
            return None
        rd = _chip_exchange_rider([p for _, p in take])
        rd.keys = [k for k, _ in take]
        return rd

    def rider(self, units):
        self.riding = (self._pair_rider(), self._chip_rider(units))
        return _join(self.riding)

    def done(self, rd):
        if rd is None:
            return
        _split_results(rd)
        pair, chips = self.riding
        if len([r for r in self.riding if r is not None]) == 1:
            (pair or chips).results = rd.results
        self.calls += 1
        if pair is not None:
            sums = _sum_pair(pair.ins, pair.results, self.c_idx, f"sum_pair_{pair.tag}")
            self.queue += list(zip(pair.keys, sums))
        if chips is not None:
            sums = _sum_chips(chips.ins, chips.results, self.pos, f"sum_chips_{self.calls}")
            self.halves.update(zip(chips.keys, sums))

    def finish(self):
        assert not self.pending
        while self.stages or self.queue:
            self.riding = (self._pair_rider(), self._chip_rider(None))
            rd = _join(self.riding)
            _run_rider(rd, f"grads_exchange_tail_{self.calls}")
            self.done(rd)
        keys = list(self.halves)
        return dict(zip(keys, _pair_gather([self.halves[k] for k in keys])))


def _update(loss, grad_x, d_meta_rows, shard_grads, small, d_final, w, mom, var):
    meta = w["meta"]
    D = w["final_norm"].shape[0]
    L = w["ffn1_norm"].shape[0]
    Dq = D // N_CHIPS

    small_parts = [jnp.concatenate(small[n], axis=0) for n in ("ffn1_norm", "mix_norm", "ffn2_norm")]
    small_parts += [d_final, jnp.concatenate(small["pool_scale"], axis=0), jnp.concatenate(small["pool_maps"], axis=0)]
    reduced = _small_all_reduce(_pack_rows(small_parts + [d_meta_rows], D))
    small_shapes = [w[n].shape for n in _SMALL]
    small_rows = sum(math.prod(shp) for shp in small_shapes) // D
    chip = 2 * lax.axis_index("x") + lax.axis_index("y")
    d_meta = lax.dynamic_slice_in_dim(reduced[small_rows:small_rows + N_META], chip * Dq, Dq, axis=1)

    out = {}
    for n in _BIG:
        gs = [shard_grads[(n, i)] for i in range(L)]
        if n in _TRANSPOSED:
            res = _adamw(gs, *(jnp.swapaxes(t[n], 1, 2) for t in (w, mom, var)), f"adamw_{n}")
            out[n] = [jnp.swapaxes(r, 1, 2) for r in res]
        else:
            out[n] = _adamw(gs, w[n], mom[n], var[n], f"adamw_{n}")
    names = _SMALL + ("meta",)
    packed_g = _pack_rows([reduced[:small_rows], d_meta], D)
    packed = [_pack_rows([t[n] for n in names], D) for t in (w, mom, var)]
    res = _adamw([packed_g], packed[0][None], packed[1][None], packed[2][None], "adamw_small")
    shapes = small_shapes + [meta.shape]
    unpacked = [_unpack_rows(r[0], shapes, D) for r in res]
    for k, n in enumerate(names):
        out[n] = tuple(u[k] for u in unpacked)

    return (loss, grad_x) + tuple(out[n][j] for j in range(4) for n in _ORDER)


def kernel(x, meta, ffn1_norm, ffn1_gate, ffn1_up, ffn1_down, mix_norm, w_in, pool_maps, pool_scale, w_ret_up, w_pool_up, w_out, ffn2_norm, ffn2_gate, ffn2_up, ffn2_down, final_norm, loss_target, m_meta, m_ffn1_norm, m_ffn1_gate, m_ffn1_up, m_ffn1_down, m_mix_norm, m_w_in, m_pool_maps, m_pool_scale, m_w_ret_up, m_w_pool_up, m_w_out, m_ffn2_norm, m_ffn2_gate, m_ffn2_up, m_ffn2_down, m_final_norm, v_meta, v_ffn1_norm, v_ffn1_gate, v_ffn1_up, v_ffn1_down, v_mix_norm, v_w_in, v_pool_maps, v_pool_scale, v_w_ret_up, v_w_pool_up, v_w_out, v_ffn2_norm, v_ffn2_gate, v_ffn2_up, v_ffn2_down, v_final_norm):
    args = dict(locals())
    w = {n: args[n] for n in _ORDER}
    mom = {n: args["m_" + n] for n in _ORDER}
    var = {n: args["v_" + n] for n in _ORDER}

    assert x.shape[0] == 1, "one batch element per device"
    seq, D = x.shape[1], x.shape[2]
    assert seq % CHUNK == 0 and D % RET_WIDTH == 0 and (2 * POOL_WIDTH) % D == 0
    pad = (-(seq + N_META)) % CHUNK
    T = seq + N_META + pad
    tm = _pick_tile(T, 528, BF16_ROWS)
    cg = _pick_tile(T // CHUNK, 11, 1)

    shards = {n: _transport(w[n]) for n in _BIG}
    shards["meta"] = meta[None]
    wts = _Weights(shards)
    head = wts.rider([(n, 0) for n in _FFN1] + [("meta", 0)])
    _run_rider(head, "weights_gather_head")
    wts.take(head)
    meta_full = jnp.transpose(wts("meta", 0), (1, 0, 2)).reshape(N_META, D)

    reducer = _Reducer(unit=2 * shards["ffn1_gate"][0].size)
    loss_acc, dh, small, d_final = _local_step(x[0], meta_full, loss_target[0], w, wts, pad, tm, cg, reducer)
    loss = lax.psum(loss_acc[0, 0], ("x", "y", "c"))
    grad_x = dh[pad + N_META:][None]
    return _update(loss, grad_x, dh[pad:pad + N_META], reducer.finish(), small, d_final, w, mom, var)
```

```python
import functools
import math

import jax
import jax.numpy as jnp
from jax import lax
from jax.experimental import pallas as pl
from jax.experimental.pallas import tpu as pltpu

F32 = jnp.float32
BF16 = jnp.bfloat16

N_META = 16
RET_HEADS = 4
HEAD_DIM = 128
RET_WIDTH = RET_HEADS * HEAD_DIM
POOL_WINDOWS = (2, 4, 8, 16)
POOL_GROUPS = len(POOL_WINDOWS)
POOL_WIDTH = POOL_GROUPS * HEAD_DIM
CHUNK = 128
ROPE_BASE = 10000.0
EPS = 1e-6
ADAM_LR = 0.001
ADAM_B1 = 0.9
ADAM_B2 = 0.999
ADAM_EPS = 1e-08
ADAM_WD = 0.01
ADAM_STEP = 10

N_CHIPS = 4
LANES = 128
BF16_ROWS = 16
V7X_VMEM_LIMIT = 52 * 1024 * 1024
MESH = pl.DeviceIdType.MESH
ANY = pl.BlockSpec(memory_space=pl.ANY)


def _round_up(n, m):
    return -(-n // m) * m


def _pick_tile(n, target, mult):
    best = None
    for d in range(mult, min(n, target) + 1, mult):
        if n % d == 0:
            best = d
    assert best is not None, (n, target, mult)
    return best


def _params(sem=None):
    return pltpu.CompilerParams(dimension_semantics=sem, vmem_limit_bytes=V7X_VMEM_LIMIT)


def _dot(a, b):
    return jnp.dot(a, b, preferred_element_type=F32)


def _dot_nt(a, b):
    return lax.dot_general(a, b, (((1,), (1,)), ((), ())), preferred_element_type=F32)


def _dot_tn(a, b):
    return lax.dot_general(a, b, (((0,), (0,)), ((), ())), preferred_element_type=F32)


def _ein(spec, a, b):
    return jnp.einsum(spec, a, b, preferred_element_type=F32)


def _sigmoid(x):
    return jax.nn.sigmoid(x)


def _rms_fwd(x, gain):
    r = lax.rsqrt(jnp.mean(x * x, axis=-1, keepdims=True) + EPS)
    return x * r * gain


def _rms_bwd(x, gain, da):
    r = lax.rsqrt(jnp.mean(x * x, axis=-1, keepdims=True) + EPS)
    xh = x * r
    dgain = jnp.sum(da * xh, axis=0, keepdims=True)
    dxh = da * gain
    dx = r * (dxh - xh * jnp.mean(dxh * xh, axis=-1, keepdims=True))
    return dx, dgain


def _row_mask(t, tm, pad, shape):
    rows = t * tm + lax.broadcasted_iota(jnp.int32, shape, 0)
    return rows >= pad


def _mesh_pos():
    x, y, c = lax.axis_index("x"), lax.axis_index("y"), lax.axis_index("c")
    others = [(1 - x, y), (x, 1 - y), (1 - x, 1 - y)]
    return x, y, c, 2 * x + y, others


def _half_rows(c, rh):
    return pl.ds(pl.multiple_of(c * rh, rh), rh)


def _remote(src, dst, ssem, rsem, dev):
    return pltpu.make_async_remote_copy(src_ref=src, dst_ref=dst, send_sem=ssem, recv_sem=rsem,
                                        device_id=dev, device_id_type=MESH)


class _Rider:
    def __init__(self, ins, out_shapes, n_sem, start, finish):
        self.ins, self.out_shapes, self.n_sem, self.start, self.finish = ins, out_shapes, n_sem, start, finish
        self.results = None


class _SemWindow:
    def __init__(self, ref, base):
        self.ref, self.base = ref, base

    @property
    def at(self):
        return self

    def __getitem__(self, k):
        return self.ref.at[self.base + k]


def _join(riders):
    riders = [r for r in riders if r is not None]
    if len(riders) <= 1:
        return riders[0] if riders else None

    def run(which):
        def go(ins, outs, ssem, rsem):
            at, sem = 0, 0
            for r in riders:
                n = len(r.ins)
                getattr(r, which)(ins[at:at + n], outs[at:at + n], _SemWindow(ssem, sem), _SemWindow(rsem, sem))
                at, sem = at + n, sem + r.n_sem
        return go

    joined = _Rider(sum([list(r.ins) for r in riders], []), sum([list(r.out_shapes) for r in riders], []),
                    sum(r.n_sem for r in riders), run("start"), run("finish"))
    joined.parts = riders
    return joined


def _split_results(rider):
    at = 0
    for r in getattr(rider, "parts", []):
        r.results = rider.results[at:at + len(r.ins)]
        at += len(r.ins)


def _gather_rider(pieces):
    per = 7
    layers = [layer for _, layer in pieces]

    def first_copies(ins, outs, ssem, rsem):
        x, y, c, chip, others = _mesh_pos()
        copies = []
        for i, layer in enumerate(layers):
            mine = _half_rows(c, ins[i].shape[1] // 2)
            for j, (ox, oy) in enumerate(others):
                copies.append(_remote(ins[i].at[layer, mine, :], outs[i].at[chip, mine, :],
                                      ssem.at[per * i + j], rsem.at[per * i + j], (ox, oy, c)))
            copies.append(_remote(ins[i].at[layer], outs[i].at[chip],
                                  ssem.at[per * i + 6], rsem.at[per * i + 6], (x, y, 1 - c)))
        return copies

    def start(ins, outs, ssem, rsem):
        for cp in first_copies(ins, outs, ssem, rsem):
            cp.start()

    def finish(ins, outs, ssem, rsem):
        x, y, c, chip, others = _mesh_pos()
        sibling = (x, y, 1 - c)
        forwards = []
        for i in range(len(layers)):
            mine = _half_rows(c, ins[i].shape[1] // 2)
            for j, (ox, oy) in enumerate(others):
                rows = outs[i].at[2 * ox + oy, mine, :]
                _remote(rows, rows, ssem.at[per * i + j], rsem.at[per * i + j], (ox, oy, c)).wait_recv()
                fwd = _remote(rows, rows, ssem.at[per * i + 3 + j], rsem.at[per * i + 3 + j], sibling)
                fwd.start()
                forwards.append(fwd)
        for i in range(len(layers)):
            theirs = _half_rows(1 - c, ins[i].shape[1] // 2)
            for j, (ox, oy) in enumerate(others):
                rows = outs[i].at[2 * ox + oy, theirs, :]
                _remote(rows, rows, ssem.at[per * i + 3 + j], rsem.at[per * i + 3 + j], sibling).wait_recv()
            own = outs[i].at[chip]
            _remote(own, own, ssem.at[per * i + 6], rsem.at[per * i + 6], sibling).wait_recv()
        for cp in first_copies(ins, outs, ssem, rsem) + forwards:
            cp.wait_send()

    shapes = [jax.ShapeDtypeStruct((N_CHIPS,) + s.shape[1:], s.dtype) for s, _ in pieces]
    return _Rider([s for s, _ in pieces], shapes, per * len(pieces), start, finish)


def _chip_exchange_rider(ps):
    def copies(ins, outs, ssem, rsem):
        x, y, c, chip, others = _mesh_pos()
        return [_remote(ins[i].at[2 * ox + oy], outs[i].at[chip], ssem.at[3 * i + j], rsem.at[3 * i + j], (ox, oy, c))
                for i in range(len(ps)) for j, (ox, oy) in enumerate(others)]

    def start(ins, outs, ssem, rsem):
        for cp in copies(ins, outs, ssem, rsem):
            cp.start()

    def finish(ins, outs, ssem, rsem):
        x, y, c, chip, others = _mesh_pos()
        for i in range(len(ps)):
            for j, (ox, oy) in enumerate(others):
                slot = outs[i].at[2 * ox + oy]
                _remote(slot, slot, ssem.at[3 * i + j], rsem.at[3 * i + j], (ox, oy, c)).wait_recv()
        for cp in copies(ins, outs, ssem, rsem):
            cp.wait_send()

    return _Rider(list(ps), [jax.ShapeDtypeStruct(p.shape, p.dtype) for p in ps], 3 * len(ps), start, finish)


def _pair_exchange_rider(gs):
    def copies(ins, outs, ssem, rsem):
        x, y, c, _, _ = _mesh_pos()
        return [_remote(ins[i].at[:, _half_rows(1 - c, ins[i].shape[1] // 2), :], outs[i],
                        ssem.at[i], rsem.at[i], (x, y, 1 - c)) for i in range(len(gs))]

    def start(ins, outs, ssem, rsem):
        for cp in copies(ins, outs, ssem, rsem):
            cp.start()

    def finish(ins, outs, ssem, rsem):
        for cp in copies(ins, outs, ssem, rsem):
            cp.wait()

    shapes = [jax.ShapeDtypeStruct((g.shape[0], g.shape[1] // 2, g.shape[2]), g.dtype) for g in gs]
    return _Rider(list(gs), shapes, len(gs), start, finish)


def _run_rider(rider, name):
    def body(*refs):
        n = len(rider.ins)
        ins, outs = refs[:n], refs[n:2 * n]
        ssem, rsem = refs[2 * n:]
        rider.start(ins, outs, ssem, rsem)
        rider.finish(ins, outs, ssem, rsem)

    rider.results = pl.pallas_call(
        body,
        name=name,
        in_specs=[ANY] * len(rider.ins),
        out_specs=[ANY] * len(rider.ins),
        out_shape=rider.out_shapes,
        scratch_shapes=[pltpu.SemaphoreType.DMA((rider.n_sem,)), pltpu.SemaphoreType.DMA((rider.n_sem,))],
    )(*rider.ins)
    return rider.results


def _pair_gather(fs):
    n = len(fs)

    def body(*refs):
        bufs = refs[n:2 * n]
        ssem, rsem = refs[2 * n:]
        x, y, c, _, _ = _mesh_pos()
        sends = []
        for i in range(n):
            rh = bufs[i].shape[0] // 2
            mine = bufs[i].at[_half_rows(c, rh), :]
            cp = _remote(mine, mine, ssem.at[i], rsem.at[i], (x, y, 1 - c))
            cp.start()
            sends.append(cp)
        for i in range(n):
            rh = bufs[i].shape[0] // 2
            theirs = bufs[i].at[_half_rows(1 - c, rh), :]
            _remote(theirs, theirs, ssem.at[i], rsem.at[i], (x, y, 1 - c)).wait_recv()
        for cp in sends:
            cp.wait_send()

    return pl.pallas_call(
        body,
        name="grads_pair_gather",
        in_specs=[ANY] * n,
        out_specs=[ANY] * n,
        out_shape=[jax.ShapeDtypeStruct(f.shape, f.dtype) for f in fs],
        input_output_aliases={i: i for i in range(n)},
        scratch_shapes=[pltpu.SemaphoreType.DMA((n,)), pltpu.SemaphoreType.DMA((n,))],
    )(*fs)


def _call(body, *, name, grid, in_specs, out_specs, out_shape, operands, scratch=(), sem=None, rider=None):
    if rider is None:
        return pl.pallas_call(
            body, name=name, grid=grid, in_specs=in_specs, out_specs=out_specs, out_shape=out_shape,
            scratch_shapes=list(scratch), compiler_params=_params(sem))(*operands)
    n_in, n_out, n_sc, r = len(in_specs), len(out_specs), len(scratch), len(rider.ins)

    def carrying(*refs):
        a, b = n_in, n_in + r
        c, d = b + n_out, b + n_out + r
        e = d + n_sc
        ids = [pl.program_id(k) for k in range(len(grid))]
        first = functools.reduce(jnp.logical_and, [i == 0 for i in ids])
        last = functools.reduce(jnp.logical_and, [i == g - 1 for i, g in zip(ids, grid)])

        @pl.when(first)
        def _():
            rider.start(refs[a:b], refs[c:d], refs[e], refs[e + 1])

        body(*refs[:a], *refs[b:c], *refs[d:e])

        @pl.when(last)
        def _():
            rider.finish(refs[a:b], refs[c:d], refs[e], refs[e + 1])

    outs = pl.pallas_call(
        carrying, name=name, grid=grid,
        in_specs=list(in_specs) + [ANY] * r,
        out_specs=list(out_specs) + [ANY] * r,
        out_shape=list(out_shape) + list(rider.out_shapes),
        scratch_shapes=list(scratch) + [pltpu.SemaphoreType.DMA((rider.n_sem,)), pltpu.SemaphoreType.DMA((rider.n_sem,))],
        compiler_params=_params(("arbitrary",) * len(grid)),
    )(*operands, *rider.ins)
    rider.results = outs[n_out:]
    return outs[:n_out]


def _ffn_fwd(h, gain, wg, wu, wd, layer, tm, name, rider=None):
    T, D = h.shape
    Fs = wg.shape[-1]
    F = N_CHIPS * Fs

    def body(h_ref, g_ref, wg_ref, wu_ref, wd_ref, ho_ref, a_ref, go_ref, uo_ref, act_ref, acc_ref):
        s = pl.program_id(1)

        @pl.when(s == 0)
        def _():
            a_ref[...] = _rms_fwd(h_ref[...], g_ref[...]).astype(BF16)
            acc_ref[...] = jnp.zeros_like(acc_ref)

        a = a_ref[...]
        g = _dot(a, wg_ref[...])
        u = _dot(a, wu_ref[...])
        act = (g * _sigmoid(g) * u).astype(BF16)
        go_ref[...] = g.astype(BF16)
        uo_ref[...] = u.astype(BF16)
        act_ref[...] = act
        acc_ref[...] += _dot(act, wd_ref[...])

        @pl.when(s == N_CHIPS - 1)
        def _():
            ho_ref[...] = h_ref[...] + 0.5 * acc_ref[...]

    row = pl.BlockSpec((tm, D), lambda t, s: (t, 0))
    col = pl.BlockSpec((tm, Fs), lambda t, s: (t, s))
    wcol = pl.BlockSpec((None, D, Fs), lambda t, s: (s, 0, 0))
    return _call(
        body, name=name, grid=(T // tm, N_CHIPS),
        in_specs=[row, pl.BlockSpec((None, 1, D), lambda t, s: (layer, 0, 0)), wcol, wcol,
                  pl.BlockSpec((None, Fs, D), lambda t, s: (s, 0, 0))],
        out_specs=[row, row, col, col, col],
        out_shape=[jax.ShapeDtypeStruct((T, D), F32), jax.ShapeDtypeStruct((T, D), BF16)]
        + [jax.ShapeDtypeStruct((T, F), BF16)] * 3,
        scratch=[pltpu.VMEM((tm, D), F32)],
        sem=("parallel", "arbitrary"), operands=(h, gain, wg, wu, wd), rider=rider)


def _inproj_fwd(h, gain, win, layer, tm, name, rider=None):
    T, D = h.shape
    Ns = win.shape[-1]

    def body(h_ref, g_ref, w_ref, z_ref, b_ref):
        @pl.when(pl.program_id(1) == 0)
        def _():
            b_ref[...] = _rms_fwd(h_ref[...], g_ref[...]).astype(BF16)

        z_ref[...] = _dot(b_ref[...], w_ref[...])

    return _call(
        body, name=name, grid=(T // tm, N_CHIPS),
        in_specs=[pl.BlockSpec((tm, D), lambda t, s: (t, 0)),
                  pl.BlockSpec((None, 1, D), lambda t, s: (layer, 0, 0)),
                  pl.BlockSpec((None, D, Ns), lambda t, s: (s, 0, 0))],
        out_specs=[pl.BlockSpec((tm, Ns), lambda t, s: (t, s)), pl.BlockSpec((tm, D), lambda t, s: (t, 0))],
        out_shape=[jax.ShapeDtypeStruct((T, N_CHIPS * Ns), F32), jax.ShapeDtypeStruct((T, D), BF16)],
        sem=("parallel", "arbitrary"), operands=(h, gain, win), rider=rider)


def _ret_consts(T, pad):
    half = HEAD_DIM // 2
    inv_freq = ROPE_BASE ** (-jnp.arange(half, dtype=F32) / half)
    pos = jnp.arange(T, dtype=F32) - pad
    ang = pos[:, None] * inv_freq[None, :]
    cos = jnp.cos(ang)
    sin = jnp.sin(ang)
    cosf = jnp.concatenate([cos, cos], axis=1)
    sinf = jnp.concatenate([-sin, sin], axis=1)
    log_gamma = jnp.log1p(-(2.0 ** (-5.0 - jnp.arange(RET_HEADS, dtype=F32))))
    idx = jnp.arange(CHUNK, dtype=F32)
    diff = idx[:, None] - idx[None, :]
    intra = jnp.where(diff[None] >= 0, jnp.exp(diff[None] * log_gamma[:, None, None]), 0.0)
    k_decay = jnp.exp((CHUNK - 1.0 - idx)[None, :] * log_gamma[:, None])
    q_decay = jnp.exp((idx + 1.0)[None, :] * log_gamma[:, None])
    chunk_decay = jnp.exp(CHUNK * log_gamma)
    kdec = jnp.broadcast_to(k_decay[:, :, None], (RET_HEADS, CHUNK, HEAD_DIM))
    qdec = jnp.broadcast_to(q_decay[:, :, None], (RET_HEADS, CHUNK, HEAD_DIM))
    cdb = jnp.broadcast_to(chunk_decay[:, None, None], (RET_HEADS, 8, HEAD_DIM))
    return cosf, sinf, intra, kdec, qdec, cdb


def _rot(t, cosv, sinv):
    return t * cosv + pltpu.roll(t, HEAD_DIM // 2, 1) * sinv


def _rot_t(g, cosv, sinv):
    return g * cosv + pltpu.roll(g * sinv, HEAD_DIM // 2, 1)


def _head_specs(tg, section, order):
    return pl.BlockSpec((tg, HEAD_DIM), lambda h, g: (order(g), section * RET_HEADS + h))


def _ret_fwd(z, consts, cg, name):
    T = z.shape[0]
    N = T // CHUNK
    ng = N // cg
    tg = cg * CHUNK
    cosf, sinf, intra, kdec, qdec, cdb = consts
    fwd = lambda g: g

    def body(zq, zk, zv, zg, cos_ref, sin_ref, m_ref, kd_ref, qd_ref, cd_ref, r_ref, o_ref, s_ref, st_ref):
        @pl.when(pl.program_id(1) == 0)
        def _():
            st_ref[...] = jnp.zeros_like(st_ref)

        cosv = cos_ref[...]
        sinv = sin_ref[...]
        q3 = (_rot(zq[...], cosv, sinv) * (HEAD_DIM ** -0.5)).reshape(cg, CHUNK, HEAD_DIM)
        k3 = _rot(zk[...], cosv, sinv).reshape(cg, CHUNK, HEAD_DIM)
        vb = zv[...].reshape(cg, CHUNK, HEAD_DIM).astype(BF16)
        scores = _ein("ncd,nmd->ncm", q3.astype(BF16), k3.astype(BF16)) * m_ref[...][None]
        inner = _ein("ncm,nmd->ncd", scores.astype(BF16), vb)
        kv = _ein("ncd,nce->nde", (k3 * kd_ref[...][None]).astype(BF16), vb)
        cd = cd_ref[0:1, :]
        state = st_ref[...]
        for n in range(cg):
            s_ref[n] = state
            state = state * cd + kv[n]
        st_ref[...] = state
        qdb = (q3 * qd_ref[...][None]).astype(BF16)
        cross = _ein("ncd,nde->nce", qdb, s_ref[...].astype(BF16))
        out = (inner + cross).reshape(tg, HEAD_DIM)
        o_ref[...] = out
        xc = out - jnp.mean(out, axis=-1, keepdims=True)
        rn = xc * lax.rsqrt(jnp.mean(xc * xc, axis=-1, keepdims=True) + EPS)
        g = zg[...]
        r_ref[...] = (rn * (g * _sigmoid(g))).astype(BF16)

    tab = pl.BlockSpec((tg, HEAD_DIM), lambda h, g: (g, 0))
    per_head = lambda rows: pl.BlockSpec((None, rows, HEAD_DIM), lambda h, g: (h, 0, 0))
    head_out = pl.BlockSpec((tg, HEAD_DIM), lambda h, g: (g, h))
    return _call(
        body, name=name, grid=(RET_HEADS, ng),
        in_specs=[_head_specs(tg, i, fwd) for i in range(4)]
        + [tab, tab, per_head(CHUNK), per_head(CHUNK), per_head(CHUNK), per_head(8)],
        out_specs=[head_out, head_out, pl.BlockSpec((None, cg, HEAD_DIM, HEAD_DIM), lambda h, g: (h, g, 0, 0))],
        out_shape=[jax.ShapeDtypeStruct((T, RET_WIDTH), BF16), jax.ShapeDtypeStruct((T, RET_WIDTH), F32),
                   jax.ShapeDtypeStruct((RET_HEADS, N, HEAD_DIM, HEAD_DIM), F32)],
        scratch=[pltpu.VMEM((HEAD_DIM, HEAD_DIM), F32)],
        sem=("parallel", "arbitrary"), operands=(z, z, z, z, cosf, sinf, intra, kdec, qdec, cdb))


def _window_sums(u, shift_of):
    sums = []
    s = u
    k = 1
    while k < POOL_WINDOWS[-1]:
        s = s + pltpu.roll(s, shift_of(k), 0)
        sums.append(s)
        k *= 2
    return sums


def _select_group(vals, g):
    out = vals[-1]
    for i in range(len(vals) - 2, -1, -1):
        out = jnp.where(g == i, vals[i], out)
    return out


def _pool_parts(u, g, T, pad):
    rows = lax.broadcasted_iota(jnp.int32, (T, HEAD_DIM), 0)
    valid = rows >= pad
    win = _select_group([float(w) for w in POOL_WINDOWS], g)
    div = jnp.clip((rows - pad + 1).astype(F32), 1.0, win)
    s = _select_group(_window_sums(u, lambda k: k), g)
    pooled = jnp.where(valid, s / div - u, 0.0)
    return pooled, div, valid


def _pool_specs(T, layer):
    first = 4 * RET_WIDTH // HEAD_DIM
    return [
        pl.BlockSpec((T, HEAD_DIM), lambda g: (0, first + g)),
        pl.BlockSpec((None, None, HEAD_DIM, HEAD_DIM), lambda g: (layer, g, 0, 0)),
        pl.BlockSpec((None, 1, HEAD_DIM), lambda g: (layer, 0, g)),
    ]


def _pool_fwd(z, maps, scale, layer, pad, name):
    T = z.shape[0]
    assert pad >= POOL_WINDOWS[-1], "window rolls wrap into the zero rows in front"

    def body(zu, maps_ref, sc_ref, pm_ref):
        g = pl.program_id(0)
        pooled, _, _ = _pool_parts(zu[...], g, T, pad)
        y = _dot(pooled.astype(BF16), maps_ref[...].astype(BF16))
        pm_ref[...] = (y * sc_ref[...]).astype(BF16)

    return _call(
        body, name=name, grid=(POOL_GROUPS,),
        in_specs=_pool_specs(T, layer),
        out_specs=[pl.BlockSpec((T, HEAD_DIM), lambda g: (0, g))],
        out_shape=[jax.ShapeDtypeStruct((T, POOL_WIDTH), BF16)],
        sem=("parallel",), operands=(z, maps, scale))[0]


def _gate_specs(tm, D):
    nb = D // RET_WIDTH
    first = (4 * RET_WIDTH + POOL_WIDTH) // RET_WIDTH
    return [pl.BlockSpec((tm, RET_WIDTH), functools.partial(lambda t, j: (t, j), j=first + j)) for j in range(2 * nb)]


def _load_gates(refs, nb):
    ga = jnp.concatenate([r[...] for r in refs[:nb]], axis=1) if nb > 1 else refs[0][...]
    gb = jnp.concatenate([r[...] for r in refs[nb:]], axis=1) if nb > 1 else refs[nb][...]
    return ga, gb


def _mix_fwd(h, r, pm, z, wru, wpu, wout, tm, name, rider=None):
    T, D = h.shape
    Dq = D // N_CHIPS
    nb = D // RET_WIDTH

    def body(*refs):
        h_ref, r_ref, pm_ref = refs[:3]
        gate_refs = refs[3:3 + 2 * nb]
        wru_ref, wpu_ref, wout_ref, ho_ref, mx_ref, ret_ref, pool_ref = refs[3 + 2 * nb:]
        rv = r_ref[...]
        pv = pm_ref[...]
        ret = jnp.concatenate([_dot(rv, wru_ref[s]) for s in range(N_CHIPS)], axis=1)
        pool = jnp.concatenate([_dot(pv, wpu_ref[s]) for s in range(N_CHIPS)], axis=1)
        ga, gb = _load_gates(gate_refs, nb)
        mixed = (_sigmoid(ga) * ret + _sigmoid(gb) * pool).astype(BF16)
        mx_ref[...] = mixed
        ret_ref[...] = ret.astype(BF16)
        pool_ref[...] = pool.astype(BF16)
        ho_ref[...] = h_ref[...] + _dot(mixed, wout_ref[...].reshape(D, D))

    row = pl.BlockSpec((tm, D), lambda t: (t, 0))
    half = pl.BlockSpec((tm, RET_WIDTH), lambda t: (t, 0))
    up = pl.BlockSpec((N_CHIPS, RET_WIDTH, Dq), lambda t: (0, 0, 0))
    return _call(
        body, name=name, grid=(T // tm,),
        in_specs=[row, half, half] + _gate_specs(tm, D) + [up, up, pl.BlockSpec((N_CHIPS, Dq, D), lambda t: (0, 0, 0))],
        out_specs=[row, row, row, row],
        out_shape=[jax.ShapeDtypeStruct((T, D), F32)] + [jax.ShapeDtypeStruct((T, D), BF16)] * 3,
        sem=("parallel",), operands=(h, r, pm, *([z] * (2 * nb)), wru, wpu, wout), rider=rider)


def _final_loss(h, gain, tgt, name):
    T, D = h.shape
    first = (T - tgt.shape[0]) // CHUNK

    def body(h_ref, g_ref, t_ref, dh_ref, loss_ref, dg_ref):
        i = pl.program_id(0)

        @pl.when(i == 0)
        def _():
            loss_ref[...] = jnp.zeros_like(loss_ref)
            dg_ref[...] = jnp.zeros_like(dg_ref)

        x = h_ref[...]
        gain_v = g_ref[...]
        err = jnp.where(i >= first, _rms_fwd(x, gain_v) - t_ref[...], 0.0)
        loss_ref[...] += 0.5 * jnp.sum(jnp.mean(err * err, axis=-1))
        dx, dgain = _rms_bwd(x, gain_v, err * (1.0 / D))
        dg_ref[...] += dgain
        dh_ref[...] = dx

    return _call(
        body, name=name, grid=(T // CHUNK,),
        in_specs=[pl.BlockSpec((CHUNK, D), lambda i: (i, 0)),
                  pl.BlockSpec((1, D), lambda i: (0, 0)),
                  pl.BlockSpec((CHUNK, D), lambda i: (jnp.maximum(i - first, 0), 0))],
        out_specs=[pl.BlockSpec((CHUNK, D), lambda i: (i, 0)),
                   pl.BlockSpec((1, LANES), lambda i: (0, 0)),
                   pl.BlockSpec((1, D), lambda i: (0, 0))],
        out_shape=[jax.ShapeDtypeStruct((T, D), F32), jax.ShapeDtypeStruct((1, LANES), F32),
                   jax.ShapeDtypeStruct((1, D), F32)],
        sem=("arbitrary",), operands=(h, gain, tgt))


def _ffn_bwd_dx(dy, h, gain, g, u, wg, wu, wd, layer, tm, pad, name, rider=None):
    T, D = h.shape
    Fs = wg.shape[-1]
    F = N_CHIPS * Fs

    def body(dy_ref, h_ref, g_ref, go_ref, uo_ref, wg_ref, wu_ref, wd_ref,
             dh_ref, dg_ref, du_ref, dgain_ref, dyh_ref, da_ref):
        t = pl.program_id(0)
        s = pl.program_id(1)

        @pl.when((t == 0) & (s == 0))
        def _():
            dgain_ref[...] = jnp.zeros_like(dgain_ref)

        @pl.when(s == 0)
        def _():
            dyh_ref[...] = (0.5 * dy_ref[...]).astype(BF16)
            da_ref[...] = jnp.zeros_like(da_ref)

        dact = _dot_nt(dyh_ref[...], wd_ref[...])
        gf = go_ref[...].astype(F32)
        uf = uo_ref[...].astype(F32)
        sg = _sigmoid(gf)
        du = (dact * (gf * sg)).astype(BF16)
        dg = (dact * uf * (sg * (1.0 + gf * (1.0 - sg)))).astype(BF16)
        dg_ref[...] = dg
        du_ref[...] = du
        da_ref[...] += _dot_nt(dg, wg_ref[...]) + _dot_nt(du, wu_ref[...])

        @pl.when(s == N_CHIPS - 1)
        def _():
            dx, dgain = _rms_bwd(h_ref[...], g_ref[...], da_ref[...])
            dgain_ref[...] += dgain
            dh_ref[...] = jnp.where(_row_mask(t, tm, pad, (tm, D)), dy_ref[...] + dx, 0.0)

    row = pl.BlockSpec((tm, D), lambda t, s: (t, 0))
    once = pl.BlockSpec((tm, D), lambda t, s: (t, 0), pipeline_mode=pl.Buffered(1))
    col = pl.BlockSpec((tm, Fs), lambda t, s: (t, s))
    wcol = pl.BlockSpec((None, D, Fs), lambda t, s: (s, 0, 0))
    return _call(
        body, name=name, grid=(T // tm, N_CHIPS),
        in_specs=[once, once, pl.BlockSpec((None, 1, D), lambda t, s: (layer, 0, 0)), col, col, wcol, wcol,
                  pl.BlockSpec((None, Fs, D), lambda t, s: (s, 0, 0))],
        out_specs=[row, col, col, pl.BlockSpec((1, D), lambda t, s: (0, 0)), row],
        out_shape=[jax.ShapeDtypeStruct((T, D), F32), jax.ShapeDtypeStruct((T, F), BF16),
                   jax.ShapeDtypeStruct((T, F), BF16), jax.ShapeDtypeStruct((1, D), F32),
                   jax.ShapeDtypeStruct((T, D), BF16)],
        scratch=[pltpu.VMEM((tm, D), F32)],
        sem=("arbitrary", "arbitrary"), operands=(dy, h, gain, g, u, wg, wu, wd), rider=rider)


def _grad_tn(a, b, mode, scale, tm, name, rider=None):
    T = a.shape[0]
    if mode == "col":
        R, C = a.shape[1], b.shape[1] // N_CHIPS
        a_spec = pl.BlockSpec((tm, R), lambda s, t: (t, 0))
        b_spec = pl.BlockSpec((tm, C), lambda s, t: (t, s))
    else:
        R, C = a.shape[1] // N_CHIPS, b.shape[1]
        a_spec = pl.BlockSpec((tm, R), lambda s, t: (t, s))
        b_spec = pl.BlockSpec((tm, C), lambda s, t: (t, 0))
    nt = T // tm

    def body(a_ref, b_ref, o_ref, acc_ref):
        t = pl.program_id(1)

        @pl.when(t == 0)
        def _():
            acc_ref[...] = jnp.zeros_like(acc_ref)

        acc_ref[...] += _dot_tn(a_ref[...].astype(BF16), b_ref[...].astype(BF16))

        @pl.when(t == nt - 1)
        def _():
            o_ref[...] = (scale * acc_ref[...]).astype(BF16)

    return _call(
        body, name=name, grid=(N_CHIPS, nt),
        in_specs=[a_spec, b_spec],
        out_specs=[pl.BlockSpec((None, R, C), lambda s, t: (s, 0, 0))],
        out_shape=[jax.ShapeDtypeStruct((N_CHIPS, R, C), BF16)],
        scratch=[pltpu.VMEM((R, C), F32)],
        sem=("parallel", "arbitrary"), operands=(a, b), rider=rider)[0]


def _mix_bwd_dx(dh, z, ret, pool, wout, wru, wpu, tm, name, rider=None):
    T, D = dh.shape
    Dq = D // N_CHIPS
    nb = D // RET_WIDTH

    def body(*refs):
        dh_ref = refs[0]
        gate_refs = refs[1:1 + 2 * nb]
        ret_ref, pool_ref, wout_ref, wru_ref, wpu_ref, dgab_ref, dret_ref, dpool_ref, dr_ref, dpm_ref = refs[1 + 2 * nb:]
        dmixed = _dot_nt(dh_ref[...].astype(BF16), wout_ref[...].reshape(D, D))
        ga, gb = _load_gates(gate_refs, nb)
        sa = _sigmoid(ga)
        sb = _sigmoid(gb)
        dgab_ref[:, :D] = (dmixed * ret_ref[...].astype(F32) * (sa * (1.0 - sa))).astype(BF16)
        dgab_ref[:, D:] = (dmixed * pool_ref[...].astype(F32) * (sb * (1.0 - sb))).astype(BF16)
        dret = (dmixed * sa).astype(BF16)
        dpool = (dmixed * sb).astype(BF16)
        dret_ref[...] = dret
        dpool_ref[...] = dpool
        dr = _dot_nt(dret[:, :Dq], wru_ref[0])
        dpm = _dot_nt(dpool[:, :Dq], wpu_ref[0])
        for s in range(1, N_CHIPS):
            dr += _dot_nt(dret[:, s * Dq:(s + 1) * Dq], wru_ref[s])
            dpm += _dot_nt(dpool[:, s * Dq:(s + 1) * Dq], wpu_ref[s])
        dr_ref[...] = dr
        dpm_ref[...] = dpm

    row = pl.BlockSpec((tm, D), lambda t: (t, 0))
    half = pl.BlockSpec((tm, RET_WIDTH), lambda t: (t, 0))
    up = pl.BlockSpec((N_CHIPS, RET_WIDTH, Dq), lambda t: (0, 0, 0))
    return _call(
        body, name=name, grid=(T // tm,),
        in_specs=[row] + _gate_specs(tm, D) + [row, row, pl.BlockSpec((N_CHIPS, Dq, D), lambda t: (0, 0, 0)), up, up],
        out_specs=[pl.BlockSpec((tm, 2 * D), lambda t: (t, 0)), row, row, half, half],
        out_shape=[jax.ShapeDtypeStruct((T, 2 * D), BF16), jax.ShapeDtypeStruct((T, D), BF16),
                   jax.ShapeDtypeStruct((T, D), BF16), jax.ShapeDtypeStruct((T, RET_WIDTH), F32),
                   jax.ShapeDtypeStruct((T, POOL_WIDTH), F32)],
        sem=("parallel",), operands=(dh, *([z] * (2 * nb)), ret, pool, wout, wru, wpu), rider=rider)


def _pool_bwd(z, dpm, maps, scale, layer, pad, name):
    T = z.shape[0]

    def body(zu, maps_ref, sc_ref, dpm_ref, du_ref, dmaps_ref, dsc_ref):
        g = pl.program_id(0)
        u = zu[...]
        pooled, div, valid = _pool_parts(u, g, T, pad)
        pb = pooled.astype(BF16)
        mb = maps_ref[...].astype(BF16)
        dp = dpm_ref[...]
        dsc_ref[...] = jnp.sum(dp * _dot(pb, mb), axis=0, keepdims=True)
        dyb = (dp * sc_ref[...]).astype(BF16)
        dmaps_ref[...] = _dot_tn(pb, dyb)
        dpooled = jnp.where(valid, _dot_nt(dyb, mb), 0.0)
        ahead = _select_group(_window_sums(dpooled / div, lambda k: T - k), g)
        du_ref[...] = jnp.where(valid, ahead - dpooled, 0.0).astype(BF16)

    blk = pl.BlockSpec((T, HEAD_DIM), lambda g: (0, g))
    return _call(
        body, name=name, grid=(POOL_GROUPS,),
        in_specs=_pool_specs(T, layer) + [blk],
        out_specs=[blk, pl.BlockSpec((None, HEAD_DIM, HEAD_DIM), lambda g: (g, 0, 0)),
                   pl.BlockSpec((1, HEAD_DIM), lambda g: (0, g))],
        out_shape=[jax.ShapeDtypeStruct((T, POOL_WIDTH), BF16),
                   jax.ShapeDtypeStruct((POOL_GROUPS, HEAD_DIM, HEAD_DIM), F32),
                   jax.ShapeDtypeStruct((1, POOL_WIDTH), F32)],
        sem=("parallel",), operands=(z, maps, scale, dpm))


def _ret_bwd_local(z, o_pre, s_all, dr, consts, cg, name):
    T = z.shape[0]
    N = T // CHUNK
    ng = N // cg
    tg = cg * CHUNK
    cosf, sinf, intra, _, qdec, _ = consts
    fwd = lambda g: g

    def body(zq, zk, zv, zg, o_ref, s_ref, dr_ref, cos_ref, sin_ref, m_ref, qd_ref,
             dq_ref, dg_ref, dk_ref, dv_ref, ds_ref):
        cosv = cos_ref[...]
        sinv = sin_ref[...]
        scale = HEAD_DIM ** -0.5
        q3 = (_rot(zq[...], cosv, sinv) * scale).reshape(cg, CHUNK, HEAD_DIM)
        k3 = _rot(zk[...], cosv, sinv).reshape(cg, CHUNK, HEAD_DIM)
        qb = q3.astype(BF16)
        kb = k3.astype(BF16)
        vb = zv[...].reshape(cg, CHUNK, HEAD_DIM).astype(BF16)
        mask = m_ref[...][None]
        sb = (_ein("ncd,nmd->ncm", qb, kb) * mask).astype(BF16)
        qdv = qd_ref[...][None]
        qdb = (q3 * qdv).astype(BF16)

        out = o_ref[...]
        xc = out - jnp.mean(out, axis=-1, keepdims=True)
        rstd = lax.rsqrt(jnp.mean(xc * xc, axis=-1, keepdims=True) + EPS)
        rn = xc * rstd
        g = zg[...]
        sg = _sigmoid(g)
        drv = dr_ref[...]
        dg_ref[...] = (drv * rn * (sg * (1.0 + g * (1.0 - sg)))).astype(BF16)
        drn = drv * (g * sg)
        dout = rstd * (drn - jnp.mean(drn, axis=-1, keepdims=True)
                       - rn * jnp.mean(drn * rn, axis=-1, keepdims=True))
        dob = dout.reshape(cg, CHUNK, HEAD_DIM).astype(BF16)

        dsb = (_ein("ncd,nmd->ncm", dob, vb) * mask).astype(BF16)
        dv_ref[...] = _ein("ncm,ncd->nmd", sb, dob).reshape(tg, HEAD_DIM)
        dk_ref[...] = _ein("ncm,ncd->nmd", dsb, qb).reshape(tg, HEAD_DIM)
        dq3 = _ein("ncm,nmd->ncd", dsb, kb) + _ein("nce,nde->ncd", dob, s_ref[...].astype(BF16)) * qdv
        dq_ref[...] = _rot_t(dq3.reshape(tg, HEAD_DIM) * scale, cosv, sinv).astype(BF16)
        ds_ref[...] = _ein("ncd,nce->nde", qdb, dob)

    tab = pl.BlockSpec((tg, HEAD_DIM), lambda h, g: (g, 0))
    per_head = pl.BlockSpec((None, CHUNK, HEAD_DIM), lambda h, g: (h, 0, 0))
    head_blk = pl.BlockSpec((tg, HEAD_DIM), lambda h, g: (g, h))
    state_blk = pl.BlockSpec((None, cg, HEAD_DIM, HEAD_DIM), lambda h, g: (h, g, 0, 0))
    return _call(
        body, name=name, grid=(RET_HEADS, ng),
        in_specs=[_head_specs(tg, i, fwd) for i in range(4)]
        + [head_blk, state_blk, head_blk, tab, tab, per_head, per_head],
        out_specs=[head_blk, head_blk, head_blk, head_blk, state_blk],
        out_shape=[jax.ShapeDtypeStruct((T, RET_WIDTH), BF16), jax.ShapeDtypeStruct((T, RET_WIDTH), BF16),
                   jax.ShapeDtypeStruct((T, RET_WIDTH), F32), jax.ShapeDtypeStruct((T, RET_WIDTH), F32),
                   jax.ShapeDtypeStruct((RET_HEADS, N, HEAD_DIM, HEAD_DIM), F32)],
        sem=("parallel", "parallel"), operands=(z, z, z, z, o_pre, s_all, dr, cosf, sinf, intra, qdec))


def _ret_bwd_state(z, dkp, dvp, ds, consts, cg, name):
    T = z.shape[0]
    N = T // CHUNK
    ng = N // cg
    tg = cg * CHUNK
    cosf, sinf, _, kdec, _, cdb = consts
    rev = lambda g: ng - 1 - g

    def body(zk, zv, dkp_ref, dvp_ref, ds_ref, cos_ref, sin_ref, kd_ref, cd_ref, dk_ref, dv_ref, gs_ref, dkv_ref):
        @pl.when(pl.program_id(1) == 0)
        def _():
            gs_ref[...] = jnp.zeros_like(gs_ref)

        cosv = cos_ref[...]
        sinv = sin_ref[...]
        cd = cd_ref[0:1, :]
        grad = gs_ref[...]
        for n in reversed(range(cg)):
            dkv_ref[n] = grad
            grad = ds_ref[n] + cd * grad
        gs_ref[...] = grad
        dkvb = dkv_ref[...].astype(BF16)
        kdv = kd_ref[...][None]
        k3 = _rot(zk[...], cosv, sinv).reshape(cg, CHUNK, HEAD_DIM)
        vb = zv[...].reshape(cg, CHUNK, HEAD_DIM).astype(BF16)
        dk3 = _ein("nce,nde->ncd", vb, dkvb) * kdv
        dv3 = _ein("ncd,nde->nce", (k3 * kdv).astype(BF16), dkvb)
        dk_ref[...] = _rot_t(dkp_ref[...] + dk3.reshape(tg, HEAD_DIM), cosv, sinv).astype(BF16)
        dv_ref[...] = (dvp_ref[...] + dv3.reshape(tg, HEAD_DIM)).astype(BF16)

    tab = pl.BlockSpec((tg, HEAD_DIM), lambda h, g: (rev(g), 0))
    head_blk = pl.BlockSpec((tg, HEAD_DIM), lambda h, g: (rev(g), h))
    return _call(
        body, name=name, grid=(RET_HEADS, ng),
        in_specs=[_head_specs(tg, 1, rev), _head_specs(tg, 2, rev), head_blk, head_blk,
                  pl.BlockSpec((None, cg, HEAD_DIM, HEAD_DIM), lambda h, g: (h, rev(g), 0, 0)),
                  tab, tab,
                  pl.BlockSpec((None, CHUNK, HEAD_DIM), lambda h, g: (h, 0, 0)),
                  pl.BlockSpec((None, 8, HEAD_DIM), lambda h, g: (h, 0, 0))],
        out_specs=[head_blk, head_blk],
        out_shape=[jax.ShapeDtypeStruct((T, RET_WIDTH), BF16)] * 2,
        scratch=[pltpu.VMEM((HEAD_DIM, HEAD_DIM), F32), pltpu.VMEM((cg, HEAD_DIM, HEAD_DIM), F32)],
        sem=("parallel", "arbitrary"), operands=(z, z, dkp, dvp, ds, cosf, sinf, kdec, cdb))


def _inproj_bwd_dx(dz, win, h, gain, dh_in, layer, tm, pad, name, rider=None):
    T, D = h.shape
    Ns = win.shape[-1]

    def body(dz_ref, w_ref, h_ref, g_ref, dhi_ref, dh_ref, dgain_ref, db_ref):
        t = pl.program_id(0)
        s = pl.program_id(1)

        @pl.when((t == 0) & (s == 0))
        def _():
            dgain_ref[...] = jnp.zeros_like(dgain_ref)

        @pl.when(s == 0)
        def _():
            db_ref[...] = jnp.zeros_like(db_ref)

        db_ref[...] += _dot_nt(dz_ref[...], w_ref[...])

        @pl.when(s == N_CHIPS - 1)
        def _():
            dx, dgain = _rms_bwd(h_ref[...], g_ref[...], db_ref[...])
            dgain_ref[...] += dgain
            dh_ref[...] = jnp.where(_row_mask(t, tm, pad, (tm, D)), dhi_ref[...] + dx, 0.0)

    row = pl.BlockSpec((tm, D), lambda t, s: (t, 0))
    return _call(
        body, name=name, grid=(T // tm, N_CHIPS),
        in_specs=[pl.BlockSpec((tm, Ns), lambda t, s: (t, s)),
                  pl.BlockSpec((None, D, Ns), lambda t, s: (s, 0, 0)),
                  row, pl.BlockSpec((None, 1, D), lambda t, s: (layer, 0, 0)), row],
        out_specs=[row, pl.BlockSpec((1, D), lambda t, s: (0, 0))],
        out_shape=[jax.ShapeDtypeStruct((T, D), F32), jax.ShapeDtypeStruct((1, D), F32)],
        scratch=[pltpu.VMEM((tm, D), F32)],
        sem=("arbitrary", "arbitrary"), operands=(dz, win, h, gain, dh_in), rider=rider)


def _sum_pair(gs, rs, c_idx, name):
    n = len(gs)

    def body(c_ref, *refs):
        for g_ref, r_ref, o_ref in zip(refs[:n], refs[n:2 * n], refs[2 * n:]):
            o_ref[...] = (g_ref[...].astype(F32) + r_ref[...].astype(F32)).astype(BF16)

    halves = [pl.BlockSpec((None,) + r.shape[1:], lambda s, c_ref: (s, 0, 0)) for r in rs]
    return pl.pallas_call(
        body,
        name=name,
        grid_spec=pltpu.PrefetchScalarGridSpec(
            num_scalar_prefetch=1,
            grid=(N_CHIPS,),
            in_specs=[pl.BlockSpec((None,) + r.shape[1:], lambda s, c_ref: (s, c_ref[0], 0)) for r in rs] + halves,
            out_specs=halves,
        ),
        out_shape=[jax.ShapeDtypeStruct(r.shape, BF16) for r in rs],
        compiler_params=_params(("parallel",)),
    )(c_idx, *gs, *rs)


def _sum_chips(ps, rs, pos, name):
    n = len(ps)
    quarters = 4

    def body(pos_ref, *refs):
        chip = pos_ref[0]
        for p_ref, r_ref, o_ref in zip(refs[:n], refs[n:2 * n], refs[2 * n:]):
            own = p_ref[...].astype(F32)
            terms = [jnp.where(chip == k, own, r_ref[k].astype(F32)) for k in range(N_CHIPS)]
            o_ref[...] = ((terms[0] + terms[1]) + terms[2]) + terms[3]

    def rows(r):
        assert r.shape[1] % (quarters * BF16_ROWS) == 0, r.shape
        return r.shape[1] // quarters

    return pl.pallas_call(
        body,
        name=name,
        grid_spec=pltpu.PrefetchScalarGridSpec(
            num_scalar_prefetch=1,
            grid=(quarters,),
            in_specs=[pl.BlockSpec((None, rows(r), r.shape[2]), lambda q, pos_ref: (pos_ref[0], q, 0)) for r in rs]
            + [pl.BlockSpec((N_CHIPS, rows(r), r.shape[2]), lambda q, pos_ref: (0, q, 0)) for r in rs],
            out_specs=[pl.BlockSpec((rows(r), r.shape[2]), lambda q, pos_ref: (pos_ref[1] * quarters + q, 0))
                       for r in rs],
        ),
        out_shape=[jax.ShapeDtypeStruct((2 * r.shape[1], r.shape[2]), F32) for r in rs],
        compiler_params=_params(("arbitrary",)),
    )(pos, *ps, *rs)


def _small_all_reduce(p):
    rows, width = p.shape

    def body(p_ref, o_ref, sib_ref, slot_ref, ssem, rsem):
        x, y, c, chip, others = _mesh_pos()
        pair = _remote(p_ref, sib_ref, ssem.at[0], rsem.at[0], (x, y, 1 - c))
        pair.start()
        pair.wait()
        slot_ref[chip] = p_ref[...] + sib_ref[...]
        sends = []
        for j, (ox, oy) in enumerate(others):
            cp = _remote(slot_ref.at[chip], slot_ref.at[chip], ssem.at[1 + j], rsem.at[1 + j], (ox, oy, c))
            cp.start()
            sends.append(cp)
        for j, (ox, oy) in enumerate(others):
            slot = slot_ref.at[2 * ox + oy]
            _remote(slot, slot, ssem.at[1 + j], rsem.at[1 + j], (ox, oy, c)).wait_recv()
        for cp in sends:
            cp.wait_send()
        o_ref[...] = ((slot_ref[0] + slot_ref[1]) + slot_ref[2]) + slot_ref[3]

    vmem = pl.BlockSpec(memory_space=pltpu.VMEM)
    return pl.pallas_call(
        body,
        name="small_grads_all_reduce",
        in_specs=[vmem],
        out_specs=vmem,
        out_shape=jax.ShapeDtypeStruct(p.shape, F32),
        scratch_shapes=[pltpu.VMEM((rows, width), F32), pltpu.VMEM((N_CHIPS, rows, width), F32),
                        pltpu.SemaphoreType.DMA((4,)), pltpu.SemaphoreType.DMA((4,))],
    )(p)


def _adamw(gs, w, m, v, name):
    L, R, C = w.shape
    Ct = gs[0].shape[1]
    tr = _pick_tile(R, 256, 8)

    def body(*refs):
        g_refs = refs[:L]
        w_ref, m_ref, v_ref, go_ref, d_ref, mo_ref, vo_ref = refs[L:]
        layer = pl.program_id(0)
        grad = g_refs[L - 1][...]
        for i in range(L - 2, -1, -1):
            grad = jnp.where(layer == i, g_refs[i][...], grad)
        if Ct != C:
            grad = grad[:, :C]
        m_new = ADAM_B1 * m_ref[...] + (1.0 - ADAM_B1) * grad
        v_new = ADAM_B2 * v_ref[...] + (1.0 - ADAM_B2) * jnp.square(grad)
        m_hat = m_new / (1.0 - ADAM_B1 ** ADAM_STEP)
        v_hat = v_new / (1.0 - ADAM_B2 ** ADAM_STEP)
        go_ref[...] = grad
        d_ref[...] = -ADAM_LR * (m_hat / (jnp.sqrt(v_hat) + ADAM_EPS) + ADAM_WD * w_ref[...])
        mo_ref[...] = m_new
        vo_ref[...] = v_new

    g_specs = [pl.BlockSpec((tr, Ct), functools.partial(lambda l, r, i: (jnp.where(l == i, r, 0), 0), i=i))
               for i in range(L)]
    blk = pl.BlockSpec((None, tr, C), lambda l, r: (l, r, 0))
    return pl.pallas_call(
        body,
        name=name,
        grid=(L, R // tr),
        in_specs=g_specs + [blk, blk, blk],
        out_specs=[blk] * 4,
        out_shape=[jax.ShapeDtypeStruct((L, R, C), F32)] * 4,
        compiler_params=_params(("arbitrary", "arbitrary")),
    )(*gs, w, m, v)


_FFN1 = ("ffn1_gate", "ffn1_up", "ffn1_down")
_FFN2 = ("ffn2_gate", "ffn2_up", "ffn2_down")
_MIXW = ("w_ret_up", "w_pool_up", "w_out")
_BIG = _FFN1 + ("w_in",) + _MIXW + _FFN2
_TRANSPOSED = ("ffn1_gate", "ffn1_up", "ffn2_gate", "ffn2_up")
_SMALL = ("ffn1_norm", "mix_norm", "ffn2_norm", "final_norm", "pool_scale", "pool_maps")
_ORDER = ("meta", "ffn1_norm", "ffn1_gate", "ffn1_up", "ffn1_down", "mix_norm", "w_in", "pool_maps",
          "pool_scale", "w_ret_up", "w_pool_up", "w_out", "ffn2_norm", "ffn2_gate", "ffn2_up", "ffn2_down",
          "final_norm")


def _transport(a):
    n, r, c = a.shape
    out = a.astype(BF16)
    if c % LANES:
        out = jnp.concatenate([out, jnp.zeros((n, r, _round_up(c, LANES) - c), BF16)], axis=2)
    if r % LANES:
        out = jnp.concatenate([out, jnp.zeros((n, _round_up(r, LANES) - r, out.shape[2]), BF16)], axis=1)
    return out


def _pack_rows(parts, width):
    rows = [p.reshape(-1, width) for p in parts]
    total = sum(r.shape[0] for r in rows)
    fill = _round_up(total, 8) - total
    if fill:
        rows.append(jnp.zeros((fill, width), F32))
    return jnp.concatenate(rows, axis=0)


def _unpack_rows(packed, shapes, width):
    out, at = [], 0
    for shp in shapes:
        n = math.prod(shp) // width
        out.append(packed[at:at + n].reshape(shp))
        at += n
    return out


class _Weights:
    def __init__(self, shards):
        self.shards = shards
        self.full = {}

    def rider(self, keys):
        r = _gather_rider([(self.shards[n], i) for n, i in keys])
        r.keys = keys
        return r

    def take(self, rider):
        for key, arr in zip(rider.keys, rider.results):
            self.full[key] = arr

    def __call__(self, name, layer):
        return self.full[(name, layer)]


def _local_step(x, meta_full, tgt, w, wts, pad, tm, cg, reducer):
    D = x.shape[1]
    T = pad + N_META + x.shape[0]
    L = w["ffn1_norm"].shape[0]
    pool_maps = w["pool_maps"]
    gains = {n: w[n].reshape(L, 1, D) for n in ("ffn1_norm", "mix_norm", "ffn2_norm")}
    scale3 = w["pool_scale"].reshape(L, 1, POOL_WIDTH)
    consts = _ret_consts(T, pad)
    tl = _pick_tile(T, 2 * tm, BF16_ROWS)
    tb = _pick_tile(T, 4 * tm // 3, BF16_ROWS)

    def gather(keys):
        return wts.rider(keys) if keys and keys[0] not in wts.full else None

    def done(rider):
        if rider is not None:
            wts.take(rider)

    h = jnp.concatenate([jnp.zeros((pad, D), F32), meta_full, x], axis=0)
    saved = []
    for i in range(L):
        s = {"h0": h}
        rd = gather([("w_in", i)] + [(n, i) for n in _MIXW])
        h, s["a1"], s["g1"], s["u1"], s["act1"] = _ffn_fwd(
            h, gains["ffn1_norm"], wts("ffn1_gate", i), wts("ffn1_up", i), wts("ffn1_down", i), i, tl,
            f"ffn1_fwd_{i}", rd)
        done(rd)
        s["h1"] = h
        rd = gather([("ffn2_gate", i), ("ffn2_up", i)])
        s["z"], s["b"] = _inproj_fwd(h, gains["mix_norm"], wts("w_in", i), i, tl, f"inproj_fwd_{i}", rd)
        done(rd)
        s["r"], s["o_pre"], s["s_all"] = _ret_fwd(s["z"], consts, cg, f"retention_fwd_{i}")
        s["pm"] = _pool_fwd(s["z"], pool_maps, scale3, i, pad, f"pool_fwd_{i}")
        rd = gather([("ffn2_down", i)])
        h, s["mixed"], s["ret"], s["pool"] = _mix_fwd(
            h, s["r"], s["pm"], s["z"], wts("w_ret_up", i), wts("w_pool_up", i), wts("w_out", i), tm,
            f"mix_fwd_{i}", rd)
        done(rd)
        s["h2"] = h
        rd = gather([(n, i + 1) for n in _FFN1]) if i + 1 < L else None
        h, s["a2"], s["g2"], s["u2"], s["act2"] = _ffn_fwd(
            h, gains["ffn2_norm"], wts("ffn2_gate", i), wts("ffn2_up", i), wts("ffn2_down", i), i, tl,
            f"ffn2_fwd_{i}", rd)
        done(rd)
        saved.append(s)

    dh, loss_acc, d_final = _final_loss(h, w["final_norm"].reshape(1, D), tgt, "final_norm_loss")

    small = {n: [None] * L for n in ("ffn1_norm", "mix_norm", "ffn2_norm", "pool_scale", "pool_maps")}

    carry = {"ffn_bwd": 4.0, "mix_bwd": 1.0, "inproj_bwd": 2.0, "w_in": 2.0, "w_out": 0.5, "w_ret_up": 0.5,
             "w_pool_up": 0.5}

    tk = _pick_tile(T, 1408, LANES)

    def grad(n, a, b, i, mode):
        rd = reducer.rider(carry.get(n, 1.5))
        reducer.add(n, i, _grad_tn(a, b, mode, 1.0, tk, f"grad_{n}_{i}", rd))
        reducer.done(rd)

    for i in reversed(range(L)):
        s = saved[i]
        rd = reducer.rider(carry["ffn_bwd"])
        dh, dg, du, small["ffn2_norm"][i], dyh = _ffn_bwd_dx(
            dh, s["h2"], gains["ffn2_norm"], s["g2"], s["u2"], wts("ffn2_gate", i), wts("ffn2_up", i),
            wts("ffn2_down", i), i, tb, pad, f"ffn2_bwd_{i}", rd)
        reducer.done(rd)
        grad("ffn2_gate", dg, s["a2"], i, "row")
        grad("ffn2_up", du, s["a2"], i, "row")
        grad("ffn2_down", s["act2"], dyh, i, "row")
        reducer.stage(f"ffn2_{i}")
        rd = reducer.rider(carry["mix_bwd"])
        dgab, dret, dpool, dr, dpm = _mix_bwd_dx(
            dh, s["z"], s["ret"], s["pool"], wts("w_out", i), wts("w_ret_up", i), wts("w_pool_up", i), tm,
            f"mix_bwd_{i}", rd)
        reducer.done(rd)
        grad("w_out", s["mixed"], dh, i, "row")
        grad("w_ret_up", s["r"], dret, i, "col")
        grad("w_pool_up", s["pm"], dpool, i, "col")
        du_pool, small["pool_maps"][i], small["pool_scale"][i] = _pool_bwd(
            s["z"], dpm, pool_maps, scale3, i, pad, f"pool_bwd_{i}")
        dq, dgr, dkp, dvp, ds = _ret_bwd_local(s["z"], s["o_pre"], s["s_all"], dr, consts, cg, f"retention_bwd_{i}")
        dk, dv = _ret_bwd_state(s["z"], dkp, dvp, ds, consts, cg, f"retention_bwd_state_{i}")
        dz = jnp.concatenate([dq, dk, dv, dgr, du_pool, dgab], axis=1)
        dh2 = dh
        rd = reducer.rider(carry["inproj_bwd"])
        dh, small["mix_norm"][i] = _inproj_bwd_dx(
            dz, wts("w_in", i), s["h1"], gains["mix_norm"], dh2, i, tl, pad, f"inproj_bwd_{i}", rd)
        reducer.done(rd)
        grad("w_in", s["b"], dz, i, "col")
        reducer.stage(f"mid{i}")
        rd = reducer.rider(carry["ffn_bwd"])
        dh, dg, du, small["ffn1_norm"][i], dyh = _ffn_bwd_dx(
            dh, s["h0"], gains["ffn1_norm"], s["g1"], s["u1"], wts("ffn1_gate", i), wts("ffn1_up", i),
            wts("ffn1_down", i), i, tb, pad, f"ffn1_bwd_{i}", rd)
        reducer.done(rd)
        grad("ffn1_gate", dg, s["a1"], i, "row")
        if i == 0:
            reducer.stage("gate0")
        grad("ffn1_up", du, s["a1"], i, "row")
        if i == 0:
            reducer.stage("up0")
        grad("ffn1_down", s["act1"], dyh, i, "row")
        reducer.stage(f"end{i}")

    return loss_acc, dh, small, d_final


class _Reducer:
    def __init__(self, unit):
        self.c_idx = lax.axis_index("c").astype(jnp.int32).reshape(1)
        chip = 2 * lax.axis_index("x") + lax.axis_index("y")
        self.pos = jnp.stack([chip, lax.axis_index("c")]).astype(jnp.int32)
        self.pending, self.stages, self.queue, self.halves = [], [], [], {}
        self.unit = unit
        self.calls = 0

    def add(self, name, layer, g):
        self.pending.append(((name, layer), g))

    def stage(self, tag):
        if self.pending:
            self.stages.append((tag, self.pending))
            self.pending = []

    def _pair_rider(self):
        if not self.stages:
            return None
        tag, items = self.stages.pop(0)
        rd = _pair_exchange_rider([g for _, g in items])
        rd.tag, rd.keys = tag, [k for k, _ in items]
        return rd

    def _chip_rider(self, units):
        take, size = [], 0
        while self.queue and (units is None or size + self.queue[0][1].size <= units * self.unit):
            take.append(self.queue.pop(0))
            size += take[-1][1].size
        if not take:
            return None
        rd = _chip_exchange_rider([p for _, p in take])
        rd.keys = [k for k, _ in take]
        return rd

    def rider(self, units):
        self.riding = (self._pair_rider(), self._chip_rider(units))
        return _join(self.riding)

    def done(self, rd):
        if rd is None:
            return
        _split_results(rd)
        pair, chips = self.riding
        if len([r for r in self.riding if r is not None]) == 1:
            (pair or chips).results = rd.results
        self.calls += 1
        if pair is not None:
            sums = _sum_pair(pair.ins, pair.results, self.c_idx, f"sum_pair_{pair.tag}")
            self.queue += list(zip(pair.keys, sums))
        if chips is not None:
            sums = _sum_chips(chips.ins, chips.results, self.pos, f"sum_chips_{self.calls}")
            self.halves.update(zip(chips.keys, sums))

    def finish(self):
        assert not self.pending
        while self.stages or self.queue:
            self.riding = (self._pair_rider(), self._chip_rider(None))
            rd = _join(self.riding)
            _run_rider(rd, f"grads_exchange_tail_{self.calls}")
            self.done(rd)
        keys = list(self.halves)
        return dict(zip(keys, _pair_gather([self.halves[k] for k in keys])))


def _update(loss, grad_x, d_meta_rows, shard_grads, small, d_final, w, mom, var):
    meta = w["meta"]
    D = w["final_norm"].shape[0]
    L = w["ffn1_norm"].shape[0]
    Dq = D // N_CHIPS

    small_parts = [jnp.concatenate(small[n], axis=0) for n in ("ffn1_norm", "mix_norm", "ffn2_norm")]
    small_parts += [d_final, jnp.concatenate(small["pool_scale"], axis=0), jnp.concatenate(small["pool_maps"], axis=0)]
    reduced = _small_all_reduce(_pack_rows(small_parts + [d_meta_rows], D))
    small_shapes = [w[n].shape for n in _SMALL]
    small_rows = sum(math.prod(shp) for shp in small_shapes) // D
    chip = 2 * lax.axis_index("x") + lax.axis_index("y")
    d_meta = lax.dynamic_slice_in_dim(reduced[small_rows:small_rows + N_META], chip * Dq, Dq, axis=1)

    out = {}
    for n in _BIG:
        gs = [shard_grads[(n, i)] for i in range(L)]
        if n in _TRANSPOSED:
            res = _adamw(gs, *(jnp.swapaxes(t[n], 1, 2) for t in (w, mom, var)), f"adamw_{n}")
            out[n] = [jnp.swapaxes(r, 1, 2) for r in res]
        else:
            out[n] = _adamw(gs, w[n], mom[n], var[n], f"adamw_{n}")
    names = _SMALL + ("meta",)
    packed_g = _pack_rows([reduced[:small_rows], d_meta], D)
    packed = [_pack_rows([t[n] for n in names], D) for t in (w, mom, var)]
    res = _adamw([packed_g], packed[0][None], packed[1][None], packed[2][None], "adamw_small")
    shapes = small_shapes + [meta.shape]
    unpacked = [_unpack_rows(r[0], shapes, D) for r in res]
    for k, n in enumerate(names):
        out[n] = tuple(u[k] for u in unpacked)

    return (loss, grad_x) + tuple(out[n][j] for j in range(4) for n in _ORDER)


def kernel(x, meta, ffn1_norm, ffn1_gate, ffn1_up, ffn1_down, mix_norm, w_in, pool_maps, pool_scale, w_ret_up, w_pool_up, w_out, ffn2_norm, ffn2_gate, ffn2_up, ffn2_down, final_norm, loss_target, m_meta, m_ffn1_norm, m_ffn1_gate, m_ffn1_up, m_ffn1_down, m_mix_norm, m_w_in, m_pool_maps, m_pool_scale, m_w_ret_up, m_w_pool_up, m_w_out, m_ffn2_norm, m_ffn2_gate, m_ffn2_up, m_ffn2_down, m_final_norm, v_meta, v_ffn1_norm, v_ffn1_gate, v_ffn1_up, v_ffn1_down, v_mix_norm, v_w_in, v_pool_maps, v_pool_scale, v_w_ret_up, v_w_pool_up, v_w_out, v_ffn2_norm, v_ffn2_gate, v_ffn2_up, v_ffn2_down, v_final_norm):
    args = dict(locals())
    w = {n: args[n] for n in _ORDER}
    mom = {n: args["m_" + n] for n in _ORDER}
    var = {n: args["v_" + n] for n in _ORDER}

    assert x.shape[0] == 1, "one batch element per device"
    seq, D = x.shape[1], x.shape[2]
    assert seq % CHUNK == 0 and D % RET_WIDTH == 0 and (2 * POOL_WIDTH) % D == 0
    pad = (-(seq + N_META)) % CHUNK
    T = seq + N_META + pad
    tm = _pick_tile(T, 528, BF16_ROWS)
    cg = _pick_tile(T // CHUNK, 11, 1)

    shards = {n: _transport(w[n]) for n in _BIG}
    shards["meta"] = meta[None]
    wts = _Weights(shards)
    head = wts.rider([(n, 0) for n in _FFN1] + [("meta", 0)])
    _run_rider(head, "weights_gather_head")
    wts.take(head)
    meta_full = jnp.transpose(wts("meta", 0), (1, 0, 2)).reshape(N_META, D)

    reducer = _Reducer(unit=2 * shards["ffn1_gate"][0].size)
    loss_acc, dh, small, d_final = _local_step(x[0], meta_full, loss_target[0], w, wts, pad, tm, cg, reducer)
    loss = lax.psum(loss_acc[0, 0], ("x", "y", "c"))
    grad_x = dh[pad + N_META:][None]
    return _update(loss, grad_x, dh[pad:pad + N_META], reducer.finish(), small, d_final, w, mom, var)
```

```python
import functools
import math

import jax
import jax.numpy as jnp
from jax import lax
from jax.experimental import pallas as pl
from jax.experimental.pallas import tpu as pltpu

F32 = jnp.float32
BF16 = jnp.bfloat16

N_META = 16
RET_HEADS = 4
HEAD_DIM = 128
RET_WIDTH = RET_HEADS * HEAD_DIM
POOL_WINDOWS = (2, 4, 8, 16)
POOL_GROUPS = len(POOL_WINDOWS)
POOL_WIDTH = POOL_GROUPS * HEAD_DIM
CHUNK = 128
ROPE_BASE = 10000.0
EPS = 1e-6
ADAM_LR = 0.001
ADAM_B1 = 0.9
ADAM_B2 = 0.999
ADAM_EPS = 1e-08
ADAM_WD = 0.01
ADAM_STEP = 10

N_CHIPS = 4
LANES = 128
BF16_ROWS = 16
V7X_VMEM_LIMIT = 52 * 1024 * 1024
MESH = pl.DeviceIdType.MESH
ANY = pl.BlockSpec(memory_space=pl.ANY)


def _round_up(n, m):
    return -(-n // m) * m


def _pick_tile(n, target, mult):
    best = None
    for d in range(mult, min(n, target) + 1, mult):
        if n % d == 0:
            best = d
    assert best is not None, (n, target, mult)
    return best


def _params(sem=None):
    return pltpu.CompilerParams(dimension_semantics=sem, vmem_limit_bytes=V7X_VMEM_LIMIT)


def _dot(a, b):
    return jnp.dot(a, b, preferred_element_type=F32)


def _dot_nt(a, b):
    return lax.dot_general(a, b, (((1,), (1,)), ((), ())), preferred_element_type=F32)


def _dot_tn(a, b):
    return lax.dot_general(a, b, (((0,), (0,)), ((), ())), preferred_element_type=F32)


def _ein(spec, a, b):
    return jnp.einsum(spec, a, b, preferred_element_type=F32)


def _sigmoid(x):
    return jax.nn.sigmoid(x)


def _rms_fwd(x, gain):
    r = lax.rsqrt(jnp.mean(x * x, axis=-1, keepdims=True) + EPS)
    return x * r * gain


def _rms_bwd(x, gain, da):
    r = lax.rsqrt(jnp.mean(x * x, axis=-1, keepdims=True) + EPS)
    xh = x * r
    dgain = jnp.sum(da * xh, axis=0, keepdims=True)
    dxh = da * gain
    dx = r * (dxh - xh * jnp.mean(dxh * xh, axis=-1, keepdims=True))
    return dx, dgain


def _row_mask(t, tm, pad, shape):
    rows = t * tm + lax.broadcasted_iota(jnp.int32, shape, 0)
    return rows >= pad


def _mesh_pos():
    x, y, c = lax.axis_index("x"), lax.axis_index("y"), lax.axis_index("c")
    others = [(1 - x, y), (x, 1 - y), (1 - x, 1 - y)]
    return x, y, c, 2 * x + y, others


def _half_rows(c, rh):
    return pl.ds(pl.multiple_of(c * rh, rh), rh)


def _remote(src, dst, ssem, rsem, dev):
    return pltpu.make_async_remote_copy(src_ref=src, dst_ref=dst, send_sem=ssem, recv_sem=rsem,
                                        device_id=dev, device_id_type=MESH)


class _Rider:
    def __init__(self, ins, out_shapes, n_sem, start, finish):
        self.ins, self.out_shapes, self.n_sem, self.start, self.finish = ins, out_shapes, n_sem, start, finish
        self.results = None


class _SemWindow:
    def __init__(self, ref, base):
        self.ref, self.base = ref, base

    @property
    def at(self):
        return self

    def __getitem__(self, k):
        return self.ref.at[self.base + k]


def _join(riders):
    riders = [r for r in riders if r is not None]
    if len(riders) <= 1:
        return riders[0] if riders else None

    def run(which):
        def go(ins, outs, ssem, rsem):
            at, sem = 0, 0
            for r in riders:
                n = len(r.ins)
                getattr(r, which)(ins[at:at + n], outs[at:at + n], _SemWindow(ssem, sem), _SemWindow(rsem, sem))
                at, sem = at + n, sem + r.n_sem
        return go

    joined = _Rider(sum([list(r.ins) for r in riders], []), sum([list(r.out_shapes) for r in riders], []),
                    sum(r.n_sem for r in riders), run("start"), run("finish"))
    joined.parts = riders
    return joined


def _split_results(rider):
    at = 0
    for r in getattr(rider, "parts", []):
        r.results = rider.results[at:at + len(r.ins)]
        at += len(r.ins)


def _gather_rider(pieces):
    per = 7
    layers = [layer for _, layer in pieces]

    def first_copies(ins, outs, ssem, rsem):
        x, y, c, chip, others = _mesh_pos()
        copies = []
        for i, layer in enumerate(layers):
            mine = _half_rows(c, ins[i].shape[1] // 2)
            for j, (ox, oy) in enumerate(others):
                copies.append(_remote(ins[i].at[layer, mine, :], outs[i].at[chip, mine, :],
                                      ssem.at[per * i + j], rsem.at[per * i + j], (ox, oy, c)))
            copies.append(_remote(ins[i].at[layer], outs[i].at[chip],
                                  ssem.at[per * i + 6], rsem.at[per * i + 6], (x, y, 1 - c)))
        return copies

    def start(ins, outs, ssem, rsem):
        for cp in first_copies(ins, outs, ssem, rsem):
            cp.start()

    def finish(ins, outs, ssem, rsem):
        x, y, c, chip, others = _mesh_pos()
        sibling = (x, y, 1 - c)
        forwards = []
        for i in range(len(layers)):
            mine = _half_rows(c, ins[i].shape[1] // 2)
            for j, (ox, oy) in enumerate(others):
                rows = outs[i].at[2 * ox + oy, mine, :]
                _remote(rows, rows, ssem.at[per * i + j], rsem.at[per * i + j], (ox, oy, c)).wait_recv()
                fwd = _remote(rows, rows, ssem.at[per * i + 3 + j], rsem.at[per * i + 3 + j], sibling)
                fwd.start()
                forwards.append(fwd)
        for i in range(len(layers)):
            theirs = _half_rows(1 - c, ins[i].shape[1] // 2)
            for j, (ox, oy) in enumerate(others):
                rows = outs[i].at[2 * ox + oy, theirs, :]
                _remote(rows, rows, ssem.at[per * i + 3 + j], rsem.at[per * i + 3 + j], sibling).wait_recv()
            own = outs[i].at[chip]
            _remote(own, own, ssem.at[per * i + 6], rsem.at[per * i + 6], sibling).wait_recv()
        for cp in first_copies(ins, outs, ssem, rsem) + forwards:
            cp.wait_send()

    shapes = [jax.ShapeDtypeStruct((N_CHIPS,) + s.shape[1:], s.dtype) for s, _ in pieces]
    return _Rider([s for s, _ in pieces], shapes, per * len(pieces), start, finish)


def _chip_exchange_rider(ps):
    def copies(ins, outs, ssem, rsem):
        x, y, c, chip, others = _mesh_pos()
        return [_remote(ins[i].at[2 * ox + oy], outs[i].at[chip], ssem.at[3 * i + j], rsem.at[3 * i + j], (ox, oy, c))
                for i in range(len(ps)) for j, (ox, oy) in enumerate(others)]

    def start(ins, outs, ssem, rsem):
        for cp in copies(ins, outs, ssem, rsem):
            cp.start()

    def finish(ins, outs, ssem, rsem):
        x, y, c, chip, others = _mesh_pos()
        for i in range(len(ps)):
            for j, (ox, oy) in enumerate(others):
                slot = outs[i].at[2 * ox + oy]
                _remote(slot, slot, ssem.at[3 * i + j], rsem.at[3 * i + j], (ox, oy, c)).wait_recv()
        for cp in copies(ins, outs, ssem, rsem):
            cp.wait_send()

    return _Rider(list(ps), [jax.ShapeDtypeStruct(p.shape, p.dtype) for p in ps], 3 * len(ps), start, finish)


def _pair_exchange_rider(gs):
    def copies(ins, outs, ssem, rsem):
        x, y, c, _, _ = _mesh_pos()
        return [_remote(ins[i].at[:, _half_rows(1 - c, ins[i].shape[1] // 2), :], outs[i],
                        ssem.at[i], rsem.at[i], (x, y, 1 - c)) for i in range(len(gs))]

    def start(ins, outs, ssem, rsem):
        for cp in copies(ins, outs, ssem, rsem):
            cp.start()

    def finish(ins, outs, ssem, rsem):
        for cp in copies(ins, outs, ssem, rsem):
            cp.wait()

    shapes = [jax.ShapeDtypeStruct((g.shape[0], g.shape[1] // 2, g.shape[2]), g.dtype) for g in gs]
    return _Rider(list(gs), shapes, len(gs), start, finish)


def _run_rider(rider, name):
    def body(*refs):
        n = len(rider.ins)
        ins, outs = refs[:n], refs[n:2 * n]
        ssem, rsem = refs[2 * n:]
        rider.start(ins, outs, ssem, rsem)
        rider.finish(ins, outs, ssem, rsem)

    rider.results = pl.pallas_call(
        body,
        name=name,
        in_specs=[ANY] * len(rider.ins),
        out_specs=[ANY] * len(rider.ins),
        out_shape=rider.out_shapes,
        scratch_shapes=[pltpu.SemaphoreType.DMA((rider.n_sem,)), pltpu.SemaphoreType.DMA((rider.n_sem,))],
    )(*rider.ins)
    return rider.results


def _pair_gather(fs):
    n = len(fs)

    def body(*refs):
        bufs = refs[n:2 * n]
        ssem, rsem = refs[2 * n:]
        x, y, c, _, _ = _mesh_pos()
        sends = []
        for i in range(n):
            rh = bufs[i].shape[0] // 2
            mine = bufs[i].at[_half_rows(c, rh), :]
            cp = _remote(mine, mine, ssem.at[i], rsem.at[i], (x, y, 1 - c))
            cp.start()
            sends.append(cp)
        for i in range(n):
            rh = bufs[i].shape[0] // 2
            theirs = bufs[i].at[_half_rows(1 - c, rh), :]
            _remote(theirs, theirs, ssem.at[i], rsem.at[i], (x, y, 1 - c)).wait_recv()
        for cp in sends:
            cp.wait_send()

    return pl.pallas_call(
        body,
        name="grads_pair_gather",
        in_specs=[ANY] * n,
        out_specs=[ANY] * n,
        out_shape=[jax.ShapeDtypeStruct(f.shape, f.dtype) for f in fs],
        input_output_aliases={i: i for i in range(n)},
        scratch_shapes=[pltpu.SemaphoreType.DMA((n,)), pltpu.SemaphoreType.DMA((n,))],
    )(*fs)


def _call(body, *, name, grid, in_specs, out_specs, out_shape, operands, scratch=(), sem=None, rider=None):
    if rider is None:
        return pl.pallas_call(
            body, name=name, grid=grid, in_specs=in_specs, out_specs=out_specs, out_shape=out_shape,
            scratch_shapes=list(scratch), compiler_params=_params(sem))(*operands)
    n_in, n_out, n_sc, r = len(in_specs), len(out_specs), len(scratch), len(rider.ins)

    def carrying(*refs):
        a, b = n_in, n_in + r
        c, d = b + n_out, b + n_out + r
        e = d + n_sc
        ids = [pl.program_id(k) for k in range(len(grid))]
        first = functools.reduce(jnp.logical_and, [i == 0 for i in ids])
        last = functools.reduce(jnp.logical_and, [i == g - 1 for i, g in zip(ids, grid)])

        @pl.when(first)
        def _():
            rider.start(refs[a:b], refs[c:d], refs[e], refs[e + 1])

        body(*refs[:a], *refs[b:c], *refs[d:e])

        @pl.when(last)
        def _():
            rider.finish(refs[a:b], refs[c:d], refs[e], refs[e + 1])

    outs = pl.pallas_call(
        carrying, name=name, grid=grid,
        in_specs=list(in_specs) + [ANY] * r,
        out_specs=list(out_specs) + [ANY] * r,
        out_shape=list(out_shape) + list(rider.out_shapes),
        scratch_shapes=list(scratch) + [pltpu.SemaphoreType.DMA((rider.n_sem,)), pltpu.SemaphoreType.DMA((rider.n_sem,))],
        compiler_params=_params(("arbitrary",) * len(grid)),
    )(*operands, *rider.ins)
    rider.results = outs[n_out:]
    return outs[:n_out]


def _ffn_fwd(h, gain, wg, wu, wd, layer, tm, name, rider=None):
    T, D = h.shape
    Fs = wg.shape[-1]
    F = N_CHIPS * Fs

    def body(h_ref, g_ref, wg_ref, wu_ref, wd_ref, ho_ref, a_ref, go_ref, uo_ref, act_ref, acc_ref):
        s = pl.program_id(1)

        @pl.when(s == 0)
        def _():
            a_ref[...] = _rms_fwd(h_ref[...], g_ref[...]).astype(BF16)
            acc_ref[...] = jnp.zeros_like(acc_ref)

        a = a_ref[...]
        g = _dot(a, wg_ref[...])
        u = _dot(a, wu_ref[...])
        act = (g * _sigmoid(g) * u).astype(BF16)
        go_ref[...] = g.astype(BF16)
        uo_ref[...] = u.astype(BF16)
        act_ref[...] = act
        acc_ref[...] += _dot(act, wd_ref[...])

        @pl.when(s == N_CHIPS - 1)
        def _():
            ho_ref[...] = h_ref[...] + 0.5 * acc_ref[...]

    row = pl.BlockSpec((tm, D), lambda t, s: (t, 0))
    col = pl.BlockSpec((tm, Fs), lambda t, s: (t, s))
    wcol = pl.BlockSpec((None, D, Fs), lambda t, s: (s, 0, 0))
    return _call(
        body, name=name, grid=(T // tm, N_CHIPS),
        in_specs=[row, pl.BlockSpec((None, 1, D), lambda t, s: (layer, 0, 0)), wcol, wcol,
                  pl.BlockSpec((None, Fs, D), lambda t, s: (s, 0, 0))],
        out_specs=[row, row, col, col, col],
        out_shape=[jax.ShapeDtypeStruct((T, D), F32), jax.ShapeDtypeStruct((T, D), BF16)]
        + [jax.ShapeDtypeStruct((T, F), BF16)] * 3,
        scratch=[pltpu.VMEM((tm, D), F32)],
        sem=("parallel", "arbitrary"), operands=(h, gain, wg, wu, wd), rider=rider)


def _inproj_fwd(h, gain, win, layer, tm, name, rider=None):
    T, D = h.shape
    Ns = win.shape[-1]

    def body(h_ref, g_ref, w_ref, z_ref, b_ref):
        @pl.when(pl.program_id(1) == 0)
        def _():
            b_ref[...] = _rms_fwd(h_ref[...], g_ref[...]).astype(BF16)

        z_ref[...] = _dot(b_ref[...], w_ref[...])

    return _call(
        body, name=name, grid=(T // tm, N_CHIPS),
        in_specs=[pl.BlockSpec((tm, D), lambda t, s: (t, 0)),
                  pl.BlockSpec((None, 1, D), lambda t, s: (layer, 0, 0)),
                  pl.BlockSpec((None, D, Ns), lambda t, s: (s, 0, 0))],
        out_specs=[pl.BlockSpec((tm, Ns), lambda t, s: (t, s)), pl.BlockSpec((tm, D), lambda t, s: (t, 0))],
        out_shape=[jax.ShapeDtypeStruct((T, N_CHIPS * Ns), F32), jax.ShapeDtypeStruct((T, D), BF16)],
        sem=("parallel", "arbitrary"), operands=(h, gain, win), rider=rider)


def _ret_consts(T, pad):
    half = HEAD_DIM // 2
    inv_freq = ROPE_BASE ** (-jnp.arange(half, dtype=F32) / half)
    pos = jnp.arange(T, dtype=F32) - pad
    ang = pos[:, None] * inv_freq[None, :]
    cos = jnp.cos(ang)
    sin = jnp.sin(ang)
    cosf = jnp.concatenate([cos, cos], axis=1)
    sinf = jnp.concatenate([-sin, sin], axis=1)
    log_gamma = jnp.log1p(-(2.0 ** (-5.0 - jnp.arange(RET_HEADS, dtype=F32))))
    idx = jnp.arange(CHUNK, dtype=F32)
    diff = idx[:, None] - idx[None, :]
    intra = jnp.where(diff[None] >= 0, jnp.exp(diff[None] * log_gamma[:, None, None]), 0.0)
    k_decay = jnp.exp((CHUNK - 1.0 - idx)[None, :] * log_gamma[:, None])
    q_decay = jnp.exp((idx + 1.0)[None, :] * log_gamma[:, None])
    chunk_decay = jnp.exp(CHUNK * log_gamma)
    kdec = jnp.broadcast_to(k_decay[:, :, None], (RET_HEADS, CHUNK, HEAD_DIM))
    qdec = jnp.broadcast_to(q_decay[:, :, None], (RET_HEADS, CHUNK, HEAD_DIM))
    cdb = jnp.broadcast_to(chunk_decay[:, None, None], (RET_HEADS, 8, HEAD_DIM))
    return cosf, sinf, intra, kdec, qdec, cdb


def _rot(t, cosv, sinv):
    return t * cosv + pltpu.roll(t, HEAD_DIM // 2, 1) * sinv


def _rot_t(g, cosv, sinv):
    return g * cosv + pltpu.roll(g * sinv, HEAD_DIM // 2, 1)


def _head_specs(tg, section, order):
    return pl.BlockSpec((tg, HEAD_DIM), lambda h, g: (order(g), section * RET_HEADS + h))


def _ret_fwd(z, consts, cg, name):
    T = z.shape[0]
    N = T // CHUNK
    ng = N // cg
    tg = cg * CHUNK
    cosf, sinf, intra, kdec, qdec, cdb = consts
    fwd = lambda g: g

    def body(zq, zk, zv, zg, cos_ref, sin_ref, m_ref, kd_ref, qd_ref, cd_ref, r_ref, o_ref, s_ref, st_ref):
        @pl.when(pl.program_id(1) == 0)
        def _():
            st_ref[...] = jnp.zeros_like(st_ref)

        cosv = cos_ref[...]
        sinv = sin_ref[...]
        q3 = (_rot(zq[...], cosv, sinv) * (HEAD_DIM ** -0.5)).reshape(cg, CHUNK, HEAD_DIM)
        k3 = _rot(zk[...], cosv, sinv).reshape(cg, CHUNK, HEAD_DIM)
        vb = zv[...].reshape(cg, CHUNK, HEAD_DIM).astype(BF16)
        scores = _ein("ncd,nmd->ncm", q3.astype(BF16), k3.astype(BF16)) * m_ref[...][None]
        inner = _ein("ncm,nmd->ncd", scores.astype(BF16), vb)
        kv = _ein("ncd,nce->nde", (k3 * kd_ref[...][None]).astype(BF16), vb)
        cd = cd_ref[0:1, :]
        state = st_ref[...]
        for n in range(cg):
            s_ref[n] = state
            state = state * cd + kv[n]
        st_ref[...] = state
        qdb = (q3 * qd_ref[...][None]).astype(BF16)
        cross = _ein("ncd,nde->nce", qdb, s_ref[...].astype(BF16))
        out = (inner + cross).reshape(tg, HEAD_DIM)
        o_ref[...] = out
        xc = out - jnp.mean(out, axis=-1, keepdims=True)
        rn = xc * lax.rsqrt(jnp.mean(xc * xc, axis=-1, keepdims=True) + EPS)
        g = zg[...]
        r_ref[...] = (rn * (g * _sigmoid(g))).astype(BF16)

    tab = pl.BlockSpec((tg, HEAD_DIM), lambda h, g: (g, 0))
    per_head = lambda rows: pl.BlockSpec((None, rows, HEAD_DIM), lambda h, g: (h, 0, 0))
    head_out = pl.BlockSpec((tg, HEAD_DIM), lambda h, g: (g, h))
    return _call(
        body, name=name, grid=(RET_HEADS, ng),
        in_specs=[_head_specs(tg, i, fwd) for i in range(4)]
        + [tab, tab, per_head(CHUNK), per_head(CHUNK), per_head(CHUNK), per_head(8)],
        out_specs=[head_out, head_out, pl.BlockSpec((None, cg, HEAD_DIM, HEAD_DIM), lambda h, g: (h, g, 0, 0))],
        out_shape=[jax.ShapeDtypeStruct((T, RET_WIDTH), BF16), jax.ShapeDtypeStruct((T, RET_WIDTH), F32),
                   jax.ShapeDtypeStruct((RET_HEADS, N, HEAD_DIM, HEAD_DIM), F32)],
        scratch=[pltpu.VMEM((HEAD_DIM, HEAD_DIM), F32)],
        sem=("parallel", "arbitrary"), operands=(z, z, z, z, cosf, sinf, intra, kdec, qdec, cdb))


def _window_sums(u, shift_of):
    sums = []
    s = u
    k = 1
    while k < POOL_WINDOWS[-1]:
        s = s + pltpu.roll(s, shift_of(k), 0)
        sums.append(s)
        k *= 2
    return sums


def _select_group(vals, g):
    out = vals[-1]
    for i in range(len(vals) - 2, -1, -1):
        out = jnp.where(g == i, vals[i], out)
    return out


def _pool_parts(u, g, T, pad):
    rows = lax.broadcasted_iota(jnp.int32, (T, HEAD_DIM), 0)
    valid = rows >= pad
    win = _select_group([float(w) for w in POOL_WINDOWS], g)
    div = jnp.clip((rows - pad + 1).astype(F32), 1.0, win)
    s = _select_group(_window_sums(u, lambda k: k), g)
    pooled = jnp.where(valid, s / div - u, 0.0)
    return pooled, div, valid


def _pool_specs(T, layer):
    first = 4 * RET_WIDTH // HEAD_DIM
    return [
        pl.BlockSpec((T, HEAD_DIM), lambda g: (0, first + g)),
        pl.BlockSpec((None, None, HEAD_DIM, HEAD_DIM), lambda g: (layer, g, 0, 0)),
        pl.BlockSpec((None, 1, HEAD_DIM), lambda g: (layer, 0, g)),
    ]


def _pool_fwd(z, maps, scale, layer, pad, name):
    T = z.shape[0]
    assert pad >= POOL_WINDOWS[-1], "window rolls wrap into the zero rows in front"

    def body(zu, maps_ref, sc_ref, pm_ref):
        g = pl.program_id(0)
        pooled, _, _ = _pool_parts(zu[...], g, T, pad)
        y = _dot(pooled.astype(BF16), maps_ref[...].astype(BF16))
        pm_ref[...] = (y * sc_ref[...]).astype(BF16)

    return _call(
        body, name=name, grid=(POOL_GROUPS,),
        in_specs=_pool_specs(T, layer),
        out_specs=[pl.BlockSpec((T, HEAD_DIM), lambda g: (0, g))],
        out_shape=[jax.ShapeDtypeStruct((T, POOL_WIDTH), BF16)],
        sem=("parallel",), operands=(z, maps, scale))[0]


def _gate_specs(tm, D):
    nb = D // RET_WIDTH
    first = (4 * RET_WIDTH + POOL_WIDTH) // RET_WIDTH
    return [pl.BlockSpec((tm, RET_WIDTH), functools.partial(lambda t, j: (t, j), j=first + j)) for j in range(2 * nb)]


def _load_gates(refs, nb):
    ga = jnp.concatenate([r[...] for r in refs[:nb]], axis=1) if nb > 1 else refs[0][...]
    gb = jnp.concatenate([r[...] for r in refs[nb:]], axis=1) if nb > 1 else refs[nb][...]
    return ga, gb


def _mix_fwd(h, r, pm, z, wru, wpu, wout, tm, name, rider=None):
    T, D = h.shape
    Dq = D // N_CHIPS
    nb = D // RET_WIDTH

    def body(*refs):
        h_ref, r_ref, pm_ref = refs[:3]
        gate_refs = refs[3:3 + 2 * nb]
        wru_ref, wpu_ref, wout_ref, ho_ref, mx_ref, ret_ref, pool_ref = refs[3 + 2 * nb:]
        rv = r_ref[...]
        pv = pm_ref[...]
        ret = jnp.concatenate([_dot(rv, wru_ref[s]) for s in range(N_CHIPS)], axis=1)
        pool = jnp.concatenate([_dot(pv, wpu_ref[s]) for s in range(N_CHIPS)], axis=1)
        ga, gb = _load_gates(gate_refs, nb)
        mixed = (_sigmoid(ga) * ret + _sigmoid(gb) * pool).astype(BF16)
        mx_ref[...] = mixed
        ret_ref[...] = ret.astype(BF16)
        pool_ref[...] = pool.astype(BF16)
        ho_ref[...] = h_ref[...] + _dot(mixed, wout_ref[...].reshape(D, D))

    row = pl.BlockSpec((tm, D), lambda t: (t, 0))
    half = pl.BlockSpec((tm, RET_WIDTH), lambda t: (t, 0))
    up = pl.BlockSpec((N_CHIPS, RET_WIDTH, Dq), lambda t: (0, 0, 0))
    return _call(
        body, name=name, grid=(T // tm,),
        in_specs=[row, half, half] + _gate_specs(tm, D) + [up, up, pl.BlockSpec((N_CHIPS, Dq, D), lambda t: (0, 0, 0))],
        out_specs=[row, row, row, row],
        out_shape=[jax.ShapeDtypeStruct((T, D), F32)] + [jax.ShapeDtypeStruct((T, D), BF16)] * 3,
        sem=("parallel",), operands=(h, r, pm, *([z] * (2 * nb)), wru, wpu, wout), rider=rider)


def _final_loss(h, gain, tgt, name):
    T, D = h.shape
    first = (T - tgt.shape[0]) // CHUNK

    def body(h_ref, g_ref, t_ref, dh_ref, loss_ref, dg_ref):
        i = pl.program_id(0)

        @pl.when(i == 0)
        def _():
            loss_ref[...] = jnp.zeros_like(loss_ref)
            dg_ref[...] = jnp.zeros_like(dg_ref)

        x = h_ref[...]
        gain_v = g_ref[...]
        err = jnp.where(i >= first, _rms_fwd(x, gain_v) - t_ref[...], 0.0)
        loss_ref[...] += 0.5 * jnp.sum(jnp.mean(err * err, axis=-1))
        dx, dgain = _rms_bwd(x, gain_v, err * (1.0 / D))
        dg_ref[...] += dgain
        dh_ref[...] = dx

    return _call(
        body, name=name, grid=(T // CHUNK,),
        in_specs=[pl.BlockSpec((CHUNK, D), lambda i: (i, 0)),
                  pl.BlockSpec((1, D), lambda i: (0, 0)),
                  pl.BlockSpec((CHUNK, D), lambda i: (jnp.maximum(i - first, 0), 0))],
        out_specs=[pl.BlockSpec((CHUNK, D), lambda i: (i, 0)),
                   pl.BlockSpec((1, LANES), lambda i: (0, 0)),
                   pl.BlockSpec((1, D), lambda i: (0, 0))],
        out_shape=[jax.ShapeDtypeStruct((T, D), F32), jax.ShapeDtypeStruct((1, LANES), F32),
                   jax.ShapeDtypeStruct((1, D), F32)],
        sem=("arbitrary",), operands=(h, gain, tgt))


def _ffn_bwd_act(dy, g, u, wd, tm, name, rider=None):
    T, D = dy.shape
    Fs = wd.shape[1]
    F = N_CHIPS * Fs

    def body(dy_ref, go_ref, uo_ref, wd_ref, dg_ref, du_ref, dyh_ref):
        @pl.when(pl.program_id(1) == 0)
        def _():
            dyh_ref[...] = (0.5 * dy_ref[...]).astype(BF16)

        dact = _dot_nt(dyh_ref[...], wd_ref[...])
        gf = go_ref[...].astype(F32)
        uf = uo_ref[...].astype(F32)
        sg = _sigmoid(gf)
        du_ref[...] = (dact * (gf * sg)).astype(BF16)
        dg_ref[...] = (dact * uf * (sg * (1.0 + gf * (1.0 - sg)))).astype(BF16)

    row = pl.BlockSpec((tm, D), lambda t, s: (t, 0))
    col = pl.BlockSpec((tm, Fs), lambda t, s: (t, s))
    return _call(
        body, name=name, grid=(T // tm, N_CHIPS),
        in_specs=[row, col, col, pl.BlockSpec((None, Fs, D), lambda t, s: (s, 0, 0))],
        out_specs=[col, col, row],
        out_shape=[jax.ShapeDtypeStruct((T, F), BF16), jax.ShapeDtypeStruct((T, F), BF16),
                   jax.ShapeDtypeStruct((T, D), BF16)],
        sem=("parallel", "arbitrary"), operands=(dy, g, u, wd), rider=rider)


def _ffn_bwd_in(dy, h, gain, dg, du, wg, wu, layer, tm, pad, name, rider=None):
    T, D = h.shape
    Fs = wg.shape[-1]

    def body(dy_ref, h_ref, g_ref, dg_ref, du_ref, wg_ref, wu_ref, dh_ref, dgain_ref, da_ref):
        t = pl.program_id(0)
        s = pl.program_id(1)

        @pl.when((t == 0) & (s == 0))
        def _():
            dgain_ref[...] = jnp.zeros_like(dgain_ref)

        @pl.when(s == 0)
        def _():
            da_ref[...] = jnp.zeros_like(da_ref)

        da_ref[...] += _dot_nt(dg_ref[...], wg_ref[...]) + _dot_nt(du_ref[...], wu_ref[...])

        @pl.when(s == N_CHIPS - 1)
        def _():
            dx, dgain = _rms_bwd(h_ref[...], g_ref[...], da_ref[...])
            dgain_ref[...] += dgain
            dh_ref[...] = jnp.where(_row_mask(t, tm, pad, (tm, D)), dy_ref[...] + dx, 0.0)

    row = pl.BlockSpec((tm, D), lambda t, s: (t, 0))
    col = pl.BlockSpec((tm, Fs), lambda t, s: (t, s))
    wcol = pl.BlockSpec((None, D, Fs), lambda t, s: (s, 0, 0))
    return _call(
        body, name=name, grid=(T // tm, N_CHIPS),
        in_specs=[row, row, pl.BlockSpec((None, 1, D), lambda t, s: (layer, 0, 0)), col, col, wcol, wcol],
        out_specs=[row, pl.BlockSpec((1, D), lambda t, s: (0, 0))],
        out_shape=[jax.ShapeDtypeStruct((T, D), F32), jax.ShapeDtypeStruct((1, D), F32)],
        scratch=[pltpu.VMEM((tm, D), F32)],
        sem=("arbitrary", "arbitrary"), operands=(dy, h, gain, dg, du, wg, wu), rider=rider)


def _grad_tn(a, b, mode, scale, tm, name, rider=None):
    T = a.shape[0]
    if mode == "col":
        per, R, C = 1, a.shape[1], b.shape[1] // N_CHIPS
        a_spec = pl.BlockSpec((tm, R), lambda s, t: (t, 0))
        b_spec = pl.BlockSpec((tm, C), lambda s, t: (t, s))
    else:
        per, R, C = 2, a.shape[1] // N_CHIPS, b.shape[1]
        a_spec = pl.BlockSpec((tm, per * R), lambda s, t: (t, s))
        b_spec = pl.BlockSpec((tm, C), lambda s, t: (t, 0))
    nt = T // tm

    def body(a_ref, b_ref, o_ref, acc_ref):
        t = pl.program_id(1)

        @pl.when(t == 0)
        def _():
            acc_ref[...] = jnp.zeros_like(acc_ref)

        acc_ref[...] += _dot_tn(a_ref[...].astype(BF16), b_ref[...].astype(BF16))

        @pl.when(t == nt - 1)
        def _():
            o_ref[...] = (scale * acc_ref[...]).astype(BF16).reshape(per, R, C)

    return _call(
        body, name=name, grid=(N_CHIPS // per, nt),
        in_specs=[a_spec, b_spec],
        out_specs=[pl.BlockSpec((per, R, C), lambda s, t: (s, 0, 0))],
        out_shape=[jax.ShapeDtypeStruct((N_CHIPS, R, C), BF16)],
        scratch=[pltpu.VMEM((per * R, C), F32)],
        sem=("parallel", "arbitrary"), operands=(a, b), rider=rider)[0]


def _mix_bwd_dx(dh, z, ret, pool, wout, wru, wpu, tm, name, rider=None):
    T, D = dh.shape
    Dq = D // N_CHIPS
    nb = D // RET_WIDTH

    def body(*refs):
        dh_ref = refs[0]
        gate_refs = refs[1:1 + 2 * nb]
        ret_ref, pool_ref, wout_ref, wru_ref, wpu_ref, dgab_ref, dret_ref, dpool_ref, dr_ref, dpm_ref = refs[1 + 2 * nb:]
        dmixed = _dot_nt(dh_ref[...].astype(BF16), wout_ref[...].reshape(D, D))
        ga, gb = _load_gates(gate_refs, nb)
        sa = _sigmoid(ga)
        sb = _sigmoid(gb)
        dgab_ref[:, :D] = (dmixed * ret_ref[...].astype(F32) * (sa * (1.0 - sa))).astype(BF16)
        dgab_ref[:, D:] = (dmixed * pool_ref[...].astype(F32) * (sb * (1.0 - sb))).astype(BF16)
        dret = (dmixed * sa).astype(BF16)
        dpool = (dmixed * sb).astype(BF16)
        dret_ref[...] = dret
        dpool_ref[...] = dpool
        dr = _dot_nt(dret[:, :Dq], wru_ref[0])
        dpm = _dot_nt(dpool[:, :Dq], wpu_ref[0])
        for s in range(1, N_CHIPS):
            dr += _dot_nt(dret[:, s * Dq:(s + 1) * Dq], wru_ref[s])
            dpm += _dot_nt(dpool[:, s * Dq:(s + 1) * Dq], wpu_ref[s])
        dr_ref[...] = dr
        dpm_ref[...] = dpm

    row = pl.BlockSpec((tm, D), lambda t: (t, 0))
    half = pl.BlockSpec((tm, RET_WIDTH), lambda t: (t, 0))
    up = pl.BlockSpec((N_CHIPS, RET_WIDTH, Dq), lambda t: (0, 0, 0))
    return _call(
        body, name=name, grid=(T // tm,),
        in_specs=[row] + _gate_specs(tm, D) + [row, row, pl.BlockSpec((N_CHIPS, Dq, D), lambda t: (0, 0, 0)), up, up],
        out_specs=[pl.BlockSpec((tm, 2 * D), lambda t: (t, 0)), row, row, half, half],
        out_shape=[jax.ShapeDtypeStruct((T, 2 * D), BF16), jax.ShapeDtypeStruct((T, D), BF16),
                   jax.ShapeDtypeStruct((T, D), BF16), jax.ShapeDtypeStruct((T, RET_WIDTH), F32),
                   jax.ShapeDtypeStruct((T, POOL_WIDTH), F32)],
        sem=("parallel",), operands=(dh, *([z] * (2 * nb)), ret, pool, wout, wru, wpu), rider=rider)


def _pool_bwd(z, dpm, maps, scale, layer, pad, name):
    T = z.shape[0]

    def body(zu, maps_ref, sc_ref, dpm_ref, du_ref, dmaps_ref, dsc_ref):
        g = pl.program_id(0)
        u = zu[...]
        pooled, div, valid = _pool_parts(u, g, T, pad)
        pb = pooled.astype(BF16)
        mb = maps_ref[...].astype(BF16)
        dp = dpm_ref[...]
        dsc_ref[...] = jnp.sum(dp * _dot(pb, mb), axis=0, keepdims=True)
        dyb = (dp * sc_ref[...]).astype(BF16)
        dmaps_ref[...] = _dot_tn(pb, dyb)
        dpooled = jnp.where(valid, _dot_nt(dyb, mb), 0.0)
        ahead = _select_group(_window_sums(dpooled / div, lambda k: T - k), g)
        du_ref[...] = jnp.where(valid, ahead - dpooled, 0.0).astype(BF16)

    blk = pl.BlockSpec((T, HEAD_DIM), lambda g: (0, g))
    return _call(
        body, name=name, grid=(POOL_GROUPS,),
        in_specs=_pool_specs(T, layer) + [blk],
        out_specs=[blk, pl.BlockSpec((None, HEAD_DIM, HEAD_DIM), lambda g: (g, 0, 0)),
                   pl.BlockSpec((1, HEAD_DIM), lambda g: (0, g))],
        out_shape=[jax.ShapeDtypeStruct((T, POOL_WIDTH), BF16),
                   jax.ShapeDtypeStruct((POOL_GROUPS, HEAD_DIM, HEAD_DIM), F32),
                   jax.ShapeDtypeStruct((1, POOL_WIDTH), F32)],
        sem=("parallel",), operands=(z, maps, scale, dpm))


def _ret_bwd_local(z, o_pre, s_all, dr, consts, cg, name):
    T = z.shape[0]
    N = T // CHUNK
    ng = N // cg
    tg = cg * CHUNK
    cosf, sinf, intra, _, qdec, _ = consts
    fwd = lambda g: g

    def body(zq, zk, zv, zg, o_ref, s_ref, dr_ref, cos_ref, sin_ref, m_ref, qd_ref,
             dq_ref, dg_ref, dk_ref, dv_ref, ds_ref):
        cosv = cos_ref[...]
        sinv = sin_ref[...]
        scale = HEAD_DIM ** -0.5
        q3 = (_rot(zq[...], cosv, sinv) * scale).reshape(cg, CHUNK, HEAD_DIM)
        k3 = _rot(zk[...], cosv, sinv).reshape(cg, CHUNK, HEAD_DIM)
        qb = q3.astype(BF16)
        kb = k3.astype(BF16)
        vb = zv[...].reshape(cg, CHUNK, HEAD_DIM).astype(BF16)
        mask = m_ref[...][None]
        sb = (_ein("ncd,nmd->ncm", qb, kb) * mask).astype(BF16)
        qdv = qd_ref[...][None]
        qdb = (q3 * qdv).astype(BF16)

        out = o_ref[...]
        xc = out - jnp.mean(out, axis=-1, keepdims=True)
        rstd = lax.rsqrt(jnp.mean(xc * xc, axis=-1, keepdims=True) + EPS)
        rn = xc * rstd
        g = zg[...]
        sg = _sigmoid(g)
        drv = dr_ref[...]
        dg_ref[...] = (drv * rn * (sg * (1.0 + g * (1.0 - sg)))).astype(BF16)
        drn = drv * (g * sg)
        dout = rstd * (drn - jnp.mean(drn, axis=-1, keepdims=True)
                       - rn * jnp.mean(drn * rn, axis=-1, keepdims=True))
        dob = dout.reshape(cg, CHUNK, HEAD_DIM).astype(BF16)

        dsb = (_ein("ncd,nmd->ncm", dob, vb) * mask).astype(BF16)
        dv_ref[...] = _ein("ncm,ncd->nmd", sb, dob).reshape(tg, HEAD_DIM)
        dk_ref[...] = _ein("ncm,ncd->nmd", dsb, qb).reshape(tg, HEAD_DIM)
        dq3 = _ein("ncm,nmd->ncd", dsb, kb) + _ein("nce,nde->ncd", dob, s_ref[...].astype(BF16)) * qdv
        dq_ref[...] = _rot_t(dq3.reshape(tg, HEAD_DIM) * scale, cosv, sinv).astype(BF16)
        ds_ref[...] = _ein("ncd,nce->nde", qdb, dob)

    tab = pl.BlockSpec((tg, HEAD_DIM), lambda h, g: (g, 0))
    per_head = pl.BlockSpec((None, CHUNK, HEAD_DIM), lambda h, g: (h, 0, 0))
    head_blk = pl.BlockSpec((tg, HEAD_DIM), lambda h, g: (g, h))
    state_blk = pl.BlockSpec((None, cg, HEAD_DIM, HEAD_DIM), lambda h, g: (h, g, 0, 0))
    return _call(
        body, name=name, grid=(RET_HEADS, ng),
        in_specs=[_head_specs(tg, i, fwd) for i in range(4)]
        + [head_blk, state_blk, head_blk, tab, tab, per_head, per_head],
        out_specs=[head_blk, head_blk, head_blk, head_blk, state_blk],
        out_shape=[jax.ShapeDtypeStruct((T, RET_WIDTH), BF16), jax.ShapeDtypeStruct((T, RET_WIDTH), BF16),
                   jax.ShapeDtypeStruct((T, RET_WIDTH), F32), jax.ShapeDtypeStruct((T, RET_WIDTH), F32),
                   jax.ShapeDtypeStruct((RET_HEADS, N, HEAD_DIM, HEAD_DIM), F32)],
        sem=("parallel", "parallel"), operands=(z, z, z, z, o_pre, s_all, dr, cosf, sinf, intra, qdec))


def _ret_bwd_state(z, dkp, dvp, ds, consts, cg, name):
    T = z.shape[0]
    N = T // CHUNK
    ng = N // cg
    tg = cg * CHUNK
    cosf, sinf, _, kdec, _, cdb = consts
    rev = lambda g: ng - 1 - g

    def body(zk, zv, dkp_ref, dvp_ref, ds_ref, cos_ref, sin_ref, kd_ref, cd_ref, dk_ref, dv_ref, gs_ref, dkv_ref):
        @pl.when(pl.program_id(1) == 0)
        def _():
            gs_ref[...] = jnp.zeros_like(gs_ref)

        cosv = cos_ref[...]
        sinv = sin_ref[...]
        cd = cd_ref[0:1, :]
        grad = gs_ref[...]
        for n in reversed(range(cg)):
            dkv_ref[n] = grad
            grad = ds_ref[n] + cd * grad
        gs_ref[...] = grad
        dkvb = dkv_ref[...].astype(BF16)
        kdv = kd_ref[...][None]
        k3 = _rot(zk[...], cosv, sinv).reshape(cg, CHUNK, HEAD_DIM)
        vb = zv[...].reshape(cg, CHUNK, HEAD_DIM).astype(BF16)
        dk3 = _ein("nce,nde->ncd", vb, dkvb) * kdv
        dv3 = _ein("ncd,nde->nce", (k3 * kdv).astype(BF16), dkvb)
        dk_ref[...] = _rot_t(dkp_ref[...] + dk3.reshape(tg, HEAD_DIM), cosv, sinv).astype(BF16)
        dv_ref[...] = (dvp_ref[...] + dv3.reshape(tg, HEAD_DIM)).astype(BF16)

    tab = pl.BlockSpec((tg, HEAD_DIM), lambda h, g: (rev(g), 0))
    head_blk = pl.BlockSpec((tg, HEAD_DIM), lambda h, g: (rev(g), h))
    return _call(
        body, name=name, grid=(RET_HEADS, ng),
        in_specs=[_head_specs(tg, 1, rev), _head_specs(tg, 2, rev), head_blk, head_blk,
                  pl.BlockSpec((None, cg, HEAD_DIM, HEAD_DIM), lambda h, g: (h, rev(g), 0, 0)),
                  tab, tab,
                  pl.BlockSpec((None, CHUNK, HEAD_DIM), lambda h, g: (h, 0, 0)),
                  pl.BlockSpec((None, 8, HEAD_DIM), lambda h, g: (h, 0, 0))],
        out_specs=[head_blk, head_blk],
        out_shape=[jax.ShapeDtypeStruct((T, RET_WIDTH), BF16)] * 2,
        scratch=[pltpu.VMEM((HEAD_DIM, HEAD_DIM), F32), pltpu.VMEM((cg, HEAD_DIM, HEAD_DIM), F32)],
        sem=("parallel", "arbitrary"), operands=(z, z, dkp, dvp, ds, cosf, sinf, kdec, cdb))


def _inproj_bwd_dx(dz, win, h, gain, dh_in, layer, tm, pad, name, rider=None):
    T, D = h.shape
    Ns = win.shape[-1]

    def body(dz_ref, w_ref, h_ref, g_ref, dhi_ref, dh_ref, dgain_ref, db_ref):
        t = pl.program_id(0)
        s = pl.program_id(1)

        @pl.when((t == 0) & (s == 0))
        def _():
            dgain_ref[...] = jnp.zeros_like(dgain_ref)

        @pl.when(s == 0)
        def _():
            db_ref[...] = jnp.zeros_like(db_ref)

        db_ref[...] += _dot_nt(dz_ref[...], w_ref[...])

        @pl.when(s == N_CHIPS - 1)
        def _():
            dx, dgain = _rms_bwd(h_ref[...], g_ref[...], db_ref[...])
            dgain_ref[...] += dgain
            dh_ref[...] = jnp.where(_row_mask(t, tm, pad, (tm, D)), dhi_ref[...] + dx, 0.0)

    row = pl.BlockSpec((tm, D), lambda t, s: (t, 0))
    return _call(
        body, name=name, grid=(T // tm, N_CHIPS),
        in_specs=[pl.BlockSpec((tm, Ns), lambda t, s: (t, s)),
                  pl.BlockSpec((None, D, Ns), lambda t, s: (s, 0, 0)),
                  row, pl.BlockSpec((None, 1, D), lambda t, s: (layer, 0, 0)), row],
        out_specs=[row, pl.BlockSpec((1, D), lambda t, s: (0, 0))],
        out_shape=[jax.ShapeDtypeStruct((T, D), F32), jax.ShapeDtypeStruct((1, D), F32)],
        scratch=[pltpu.VMEM((tm, D), F32)],
        sem=("arbitrary", "arbitrary"), operands=(dz, win, h, gain, dh_in), rider=rider)


def _sum_pair(gs, rs, c_idx, name):
    n = len(gs)

    def body(c_ref, *refs):
        for g_ref, r_ref, o_ref in zip(refs[:n], refs[n:2 * n], refs[2 * n:]):
            o_ref[...] = (g_ref[...].astype(F32) + r_ref[...].astype(F32)).astype(BF16)

    halves = [pl.BlockSpec((None,) + r.shape[1:], lambda s, c_ref: (s, 0, 0)) for r in rs]
    return pl.pallas_call(
        body,
        name=name,
        grid_spec=pltpu.PrefetchScalarGridSpec(
            num_scalar_prefetch=1,
            grid=(N_CHIPS,),
            in_specs=[pl.BlockSpec((None,) + r.shape[1:], lambda s, c_ref: (s, c_ref[0], 0)) for r in rs] + halves,
            out_specs=halves,
        ),
        out_shape=[jax.ShapeDtypeStruct(r.shape, BF16) for r in rs],
        compiler_params=_params(("parallel",)),
    )(c_idx, *gs, *rs)


def _sum_chips(ps, rs, pos, name):
    n = len(ps)
    quarters = 4

    def body(pos_ref, *refs):
        chip = pos_ref[0]
        for p_ref, r_ref, o_ref in zip(refs[:n], refs[n:2 * n], refs[2 * n:]):
            own = p_ref[...].astype(F32)
            terms = [jnp.where(chip == k, own, r_ref[k].astype(F32)) for k in range(N_CHIPS)]
            o_ref[...] = ((terms[0] + terms[1]) + terms[2]) + terms[3]

    def rows(r):
        assert r.shape[1] % (quarters * BF16_ROWS) == 0, r.shape
        return r.shape[1] // quarters

    return pl.pallas_call(
        body,
        name=name,
        grid_spec=pltpu.PrefetchScalarGridSpec(
            num_scalar_prefetch=1,
            grid=(quarters,),
            in_specs=[pl.BlockSpec((None, rows(r), r.shape[2]), lambda q, pos_ref: (pos_ref[0], q, 0)) for r in rs]
            + [pl.BlockSpec((N_CHIPS, rows(r), r.shape[2]), lambda q, pos_ref: (0, q, 0)) for r in rs],
            out_specs=[pl.BlockSpec((rows(r), r.shape[2]), lambda q, pos_ref: (pos_ref[1] * quarters + q, 0))
                       for r in rs],
        ),
        out_shape=[jax.ShapeDtypeStruct((2 * r.shape[1], r.shape[2]), F32) for r in rs],
        compiler_params=_params(("arbitrary",)),
    )(pos, *ps, *rs)


def _small_all_reduce(p):
    rows, width = p.shape

    def body(p_ref, o_ref, sib_ref, slot_ref, ssem, rsem):
        x, y, c, chip, others = _mesh_pos()
        pair = _remote(p_ref, sib_ref, ssem.at[0], rsem.at[0], (x, y, 1 - c))
        pair.start()
        pair.wait()
        slot_ref[chip] = p_ref[...] + sib_ref[...]
        sends = []
        for j, (ox, oy) in enumerate(others):
            cp = _remote(slot_ref.at[chip], slot_ref.at[chip], ssem.at[1 + j], rsem.at[1 + j], (ox, oy, c))
            cp.start()
            sends.append(cp)
        for j, (ox, oy) in enumerate(others):
            slot = slot_ref.at[2 * ox + oy]
            _remote(slot, slot, ssem.at[1 + j], rsem.at[1 + j], (ox, oy, c)).wait_recv()
        for cp in sends:
            cp.wait_send()
        o_ref[...] = ((slot_ref[0] + slot_ref[1]) + slot_ref[2]) + slot_ref[3]

    vmem = pl.BlockSpec(memory_space=pltpu.VMEM)
    return pl.pallas_call(
        body,
        name="small_grads_all_reduce",
        in_specs=[vmem],
        out_specs=vmem,
        out_shape=jax.ShapeDtypeStruct(p.shape, F32),
        scratch_shapes=[pltpu.VMEM((rows, width), F32), pltpu.VMEM((N_CHIPS, rows, width), F32),
                        pltpu.SemaphoreType.DMA((4,)), pltpu.SemaphoreType.DMA((4,))],
    )(p)


def _adamw(gs, w, m, v, name):
    L, R, C = w.shape
    Ct = gs[0].shape[1]
    tr = _pick_tile(R, 256, 8)

    def body(*refs):
        g_refs = refs[:L]
        w_ref, m_ref, v_ref, go_ref, d_ref, mo_ref, vo_ref = refs[L:]
        layer = pl.program_id(0)
        grad = g_refs[L - 1][...]
        for i in range(L - 2, -1, -1):
            grad = jnp.where(layer == i, g_refs[i][...], grad)
        if Ct != C:
            grad = grad[:, :C]
        m_new = ADAM_B1 * m_ref[...] + (1.0 - ADAM_B1) * grad
        v_new = ADAM_B2 * v_ref[...] + (1.0 - ADAM_B2) * jnp.square(grad)
        m_hat = m_new / (1.0 - ADAM_B1 ** ADAM_STEP)
        v_hat = v_new / (1.0 - ADAM_B2 ** ADAM_STEP)
        go_ref[...] = grad
        d_ref[...] = -ADAM_LR * (m_hat / (jnp.sqrt(v_hat) + ADAM_EPS) + ADAM_WD * w_ref[...])
        mo_ref[...] = m_new
        vo_ref[...] = v_new

    g_specs = [pl.BlockSpec((tr, Ct), functools.partial(lambda l, r, i: (jnp.where(l == i, r, 0), 0), i=i))
               for i in range(L)]
    blk = pl.BlockSpec((None, tr, C), lambda l, r: (l, r, 0))
    return pl.pallas_call(
        body,
        name=name,
        grid=(L, R // tr),
        in_specs=g_specs + [blk, blk, blk],
        out_specs=[blk] * 4,
        out_shape=[jax.ShapeDtypeStruct((L, R, C), F32)] * 4,
        compiler_params=_params(("arbitrary", "arbitrary")),
    )(*gs, w, m, v)


_FFN1 = ("ffn1_gate", "ffn1_up", "ffn1_down")
_FFN2 = ("ffn2_gate", "ffn2_up", "ffn2_down")
_MIXW = ("w_ret_up", "w_pool_up", "w_out")
_BIG = _FFN1 + ("w_in",) + _MIXW + _FFN2
_TRANSPOSED = ("ffn1_gate", "ffn1_up", "ffn2_gate", "ffn2_up")
_SMALL = ("ffn1_norm", "mix_norm", "ffn2_norm", "final_norm", "pool_scale", "pool_maps")
_ORDER = ("meta", "ffn1_norm", "ffn1_gate", "ffn1_up", "ffn1_down", "mix_norm", "w_in", "pool_maps",
          "pool_scale", "w_ret_up", "w_pool_up", "w_out", "ffn2_norm", "ffn2_gate", "ffn2_up", "ffn2_down",
          "final_norm")


def _transport(a):
    n, r, c = a.shape
    out = a.astype(BF16)
    if c % LANES:
        out = jnp.concatenate([out, jnp.zeros((n, r, _round_up(c, LANES) - c), BF16)], axis=2)
    if r % LANES:
        out = jnp.concatenate([out, jnp.zeros((n, _round_up(r, LANES) - r, out.shape[2]), BF16)], axis=1)
    return out


def _pack_rows(parts, width):
    rows = [p.reshape(-1, width) for p in parts]
    total = sum(r.shape[0] for r in rows)
    fill = _round_up(total, 8) - total
    if fill:
        rows.append(jnp.zeros((fill, width), F32))
    return jnp.concatenate(rows, axis=0)


def _unpack_rows(packed, shapes, width):
    out, at = [], 0
    for shp in shapes:
        n = math.prod(shp) // width
        out.append(packed[at:at + n].reshape(shp))
        at += n
    return out


class _Weights:
    def __init__(self, shards):
        self.shards = shards
        self.full = {}

    def rider(self, keys):
        r = _gather_rider([(self.shards[n], i) for n, i in keys])
        r.keys = keys
        return r

    def take(self, rider):
        for key, arr in zip(rider.keys, rider.results):
            self.full[key] = arr

    def __call__(self, name, layer):
        return self.full[(name, layer)]


def _local_step(x, meta_full, tgt, w, wts, pad, tm, cg, reducer):
    D = x.shape[1]
    T = pad + N_META + x.shape[0]
    L = w["ffn1_norm"].shape[0]
    pool_maps = w["pool_maps"]
    gains = {n: w[n].reshape(L, 1, D) for n in ("ffn1_norm", "mix_norm", "ffn2_norm")}
    scale3 = w["pool_scale"].reshape(L, 1, POOL_WIDTH)
    consts = _ret_consts(T, pad)
    tl = _pick_tile(T, 2 * tm, BF16_ROWS)
    def gather(keys):
        return wts.rider(keys) if keys and keys[0] not in wts.full else None

    def done(rider):
        if rider is not None:
            wts.take(rider)

    h = jnp.concatenate([jnp.zeros((pad, D), F32), meta_full, x], axis=0)
    saved = []
    for i in range(L):
        s = {"h0": h}
        rd = gather([("w_in", i)] + [(n, i) for n in _MIXW])
        h, s["a1"], s["g1"], s["u1"], s["act1"] = _ffn_fwd(
            h, gains["ffn1_norm"], wts("ffn1_gate", i), wts("ffn1_up", i), wts("ffn1_down", i), i, tl,
            f"ffn1_fwd_{i}", rd)
        done(rd)
        s["h1"] = h
        rd = gather([("ffn2_gate", i), ("ffn2_up", i)])
        s["z"], s["b"] = _inproj_fwd(h, gains["mix_norm"], wts("w_in", i), i, tl, f"inproj_fwd_{i}", rd)
        done(rd)
        s["r"], s["o_pre"], s["s_all"] = _ret_fwd(s["z"], consts, cg, f"retention_fwd_{i}")
        s["pm"] = _pool_fwd(s["z"], pool_maps, scale3, i, pad, f"pool_fwd_{i}")
        rd = gather([("ffn2_down", i)])
        h, s["mixed"], s["ret"], s["pool"] = _mix_fwd(
            h, s["r"], s["pm"], s["z"], wts("w_ret_up", i), wts("w_pool_up", i), wts("w_out", i), tm,
            f"mix_fwd_{i}", rd)
        done(rd)
        s["h2"] = h
        rd = gather([(n, i + 1) for n in _FFN1]) if i + 1 < L else None
        h, s["a2"], s["g2"], s["u2"], s["act2"] = _ffn_fwd(
            h, gains["ffn2_norm"], wts("ffn2_gate", i), wts("ffn2_up", i), wts("ffn2_down", i), i, tl,
            f"ffn2_fwd_{i}", rd)
        done(rd)
        saved.append(s)

    dh, loss_acc, d_final = _final_loss(h, w["final_norm"].reshape(1, D), tgt, "final_norm_loss")

    small = {n: [None] * L for n in ("ffn1_norm", "mix_norm", "ffn2_norm", "pool_scale", "pool_maps")}

    carry = {"ffn_act": 1.0, "ffn_in": 2.0, "mix_bwd": 1.0, "inproj_bwd": 2.0, "w_in": 2.0, "w_out": 0.5,
             "w_ret_up": 0.5, "w_pool_up": 0.5}

    tk = _pick_tile(T, 1408, LANES)

    def grad(n, a, b, i, mode):
        rd = reducer.rider(carry.get(n, 1.5))
        reducer.add(n, i, _grad_tn(a, b, mode, 1.0, tk, f"grad_{n}_{i}", rd))
        reducer.done(rd)

    def ffn_bwd(which, dy, h_in, g, u, i):
        rd = reducer.rider(carry["ffn_act"])
        dg, du, dyh = _ffn_bwd_act(dy, g, u, wts(f"{which}_down", i), tl, f"{which}_bwd_act_{i}", rd)
        reducer.done(rd)
        rd = reducer.rider(carry["ffn_in"])
        dh_in, dgain = _ffn_bwd_in(dy, h_in, gains[f"{which}_norm"], dg, du, wts(f"{which}_gate", i),
                                   wts(f"{which}_up", i), i, tl, pad, f"{which}_bwd_in_{i}", rd)
        reducer.done(rd)
        return dh_in, dg, du, dgain, dyh

    for i in reversed(range(L)):
        s = saved[i]
        dh, dg, du, small["ffn2_norm"][i], dyh = ffn_bwd("ffn2", dh, s["h2"], s["g2"], s["u2"], i)
        grad("ffn2_gate", dg, s["a2"], i, "row")
        grad("ffn2_up", du, s["a2"], i, "row")
        grad("ffn2_down", s["act2"], dyh, i, "row")
        reducer.stage(f"ffn2_{i}")
        rd = reducer.rider(carry["mix_bwd"])
        dgab, dret, dpool, dr, dpm = _mix_bwd_dx(
            dh, s["z"], s["ret"], s["pool"], wts("w_out", i), wts("w_ret_up", i), wts("w_pool_up", i), tm,
            f"mix_bwd_{i}", rd)
        reducer.done(rd)
        grad("w_out", s["mixed"], dh, i, "row")
        grad("w_ret_up", s["r"], dret, i, "col")
        grad("w_pool_up", s["pm"], dpool, i, "col")
        du_pool, small["pool_maps"][i], small["pool_scale"][i] = _pool_bwd(
            s["z"], dpm, pool_maps, scale3, i, pad, f"pool_bwd_{i}")
        dq, dgr, dkp, dvp, ds = _ret_bwd_local(s["z"], s["o_pre"], s["s_all"], dr, consts, cg, f"retention_bwd_{i}")
        dk, dv = _ret_bwd_state(s["z"], dkp, dvp, ds, consts, cg, f"retention_bwd_state_{i}")
        dz = jnp.concatenate([dq, dk, dv, dgr, du_pool, dgab], axis=1)
        dh2 = dh
        rd = reducer.rider(carry["inproj_bwd"])
        dh, small["mix_norm"][i] = _inproj_bwd_dx(
            dz, wts("w_in", i), s["h1"], gains["mix_norm"], dh2, i, tl, pad, f"inproj_bwd_{i}", rd)
        reducer.done(rd)
        grad("w_in", s["b"], dz, i, "col")
        reducer.stage(f"mid{i}")
        dh, dg, du, small["ffn1_norm"][i], dyh = ffn_bwd("ffn1", dh, s["h0"], s["g1"], s["u1"], i)
        grad("ffn1_gate", dg, s["a1"], i, "row")
        if i == 0:
            reducer.stage("gate0")
        grad("ffn1_up", du, s["a1"], i, "row")
        if i == 0:
            reducer.stage("up0")
        grad("ffn1_down", s["act1"], dyh, i, "row")
        reducer.stage(f"end{i}")

    return loss_acc, dh, small, d_final


class _Reducer:
    def __init__(self, unit):
        self.c_idx = lax.axis_index("c").astype(jnp.int32).reshape(1)
        chip = 2 * lax.axis_index("x") + lax.axis_index("y")
        self.pos = jnp.stack([chip, lax.axis_index("c")]).astype(jnp.int32)
        self.pending, self.stages, self.queue, self.halves = [], [], [], {}
        self.unit = unit
        self.calls = 0

    def add(self, name, layer, g):
        self.pending.append(((name, layer), g))

    def stage(self, tag):
        if self.pending:
            self.stages.append((tag, self.pending))
            self.pending = []

    def _pair_rider(self):
        if not self.stages:
            return None
        tag, items = self.stages.pop(0)
        rd = _pair_exchange_rider([g for _, g in items])
        rd.tag, rd.keys = tag, [k for k, _ in items]
        return rd

    def _chip_rider(self, units):
        take, size = [], 0
        while self.queue and (units is None or size + self.queue[0][1].size <= units * self.unit):
            take.append(self.queue.pop(0))
            size += take[-1][1].size
        if not take:
            return None
        rd = _chip_exchange_rider([p for _, p in take])
        rd.keys = [k for k, _ in take]
        return rd

    def rider(self, units):
        self.riding = (self._pair_rider(), self._chip_rider(units))
        return _join(self.riding)

    def done(self, rd):
        if rd is None:
            return
        _split_results(rd)
        pair, chips = self.riding
        if len([r for r in self.riding if r is not None]) == 1:
            (pair or chips).results = rd.results
        self.calls += 1
        if pair is not None:
            sums = _sum_pair(pair.ins, pair.results, self.c_idx, f"sum_pair_{pair.tag}")
            self.queue += list(zip(pair.keys, sums))
        if chips is not None:
            sums = _sum_chips(chips.ins, chips.results, self.pos, f"sum_chips_{self.calls}")
            self.halves.update(zip(chips.keys, sums))

    def finish(self):
        assert not self.pending
        while self.stages or self.queue:
            self.riding = (self._pair_rider(), self._chip_rider(None))
            rd = _join(self.riding)
            _run_rider(rd, f"grads_exchange_tail_{self.calls}")
            self.done(rd)
        keys = list(self.halves)
        return dict(zip(keys, _pair_gather([self.halves[k] for k in keys])))


def _update(loss, grad_x, d_meta_rows, shard_grads, small, d_final, w, mom, var):
    meta = w["meta"]
    D = w["final_norm"].shape[0]
    L = w["ffn1_norm"].shape[0]
    Dq = D // N_CHIPS

    small_parts = [jnp.concatenate(small[n], axis=0) for n in ("ffn1_norm", "mix_norm", "ffn2_norm")]
    small_parts += [d_final, jnp.concatenate(small["pool_scale"], axis=0), jnp.concatenate(small["pool_maps"], axis=0)]
    reduced = _small_all_reduce(_pack_rows(small_parts + [d_meta_rows], D))
    small_shapes = [w[n].shape for n in _SMALL]
    small_rows = sum(math.prod(shp) for shp in small_shapes) // D
    chip = 2 * lax.axis_index("x") + lax.axis_index("y")
    d_meta = lax.dynamic_slice_in_dim(reduced[small_rows:small_rows + N_META], chip * Dq, Dq, axis=1)

    out = {}
    for n in _BIG:
        gs = [shard_grads[(n, i)] for i in range(L)]
        if n in _TRANSPOSED:
            res = _adamw(gs, *(jnp.swapaxes(t[n], 1, 2) for t in (w, mom, var)), f"adamw_{n}")
            out[n] = [jnp.swapaxes(r, 1, 2) for r in res]
        else:
            out[n] = _adamw(gs, w[n], mom[n], var[n], f"adamw_{n}")
    names = _SMALL + ("meta",)
    packed_g = _pack_rows([reduced[:small_rows], d_meta], D)
    packed = [_pack_rows([t[n] for n in names], D) for t in (w, mom, var)]
    res = _adamw([packed_g], packed[0][None], packed[1][None], packed[2][None], "adamw_small")
    shapes = small_shapes + [meta.shape]
    unpacked = [_unpack_rows(r[0], shapes, D) for r in res]
    for k, n in enumerate(names):
        out[n] = tuple(u[k] for u in unpacked)

    return (loss, grad_x) + tuple(out[n][j] for j in range(4) for n in _ORDER)


def kernel(x, meta, ffn1_norm, ffn1_gate, ffn1_up, ffn1_down, mix_norm, w_in, pool_maps, pool_scale, w_ret_up, w_pool_up, w_out, ffn2_norm, ffn2_gate, ffn2_up, ffn2_down, final_norm, loss_target, m_meta, m_ffn1_norm, m_ffn1_gate, m_ffn1_up, m_ffn1_down, m_mix_norm, m_w_in, m_pool_maps, m_pool_scale, m_w_ret_up, m_w_pool_up, m_w_out, m_ffn2_norm, m_ffn2_gate, m_ffn2_up, m_ffn2_down, m_final_norm, v_meta, v_ffn1_norm, v_ffn1_gate, v_ffn1_up, v_ffn1_down, v_mix_norm, v_w_in, v_pool_maps, v_pool_scale, v_w_ret_up, v_w_pool_up, v_w_out, v_ffn2_norm, v_ffn2_gate, v_ffn2_up, v_ffn2_down, v_final_norm):
    args = dict(locals())
    w = {n: args[n] for n in _ORDER}
    mom = {n: args["m_" + n] for n in _ORDER}
    var = {n: args["v_" + n] for n in _ORDER}

    assert x.shape[0] == 1, "one batch element per device"
    seq, D = x.shape[1], x.shape[2]
    assert seq % CHUNK == 0 and D % RET_WIDTH == 0 and (2 * POOL_WIDTH) % D == 0
    pad = (-(seq + N_META)) % CHUNK
    T = seq + N_META + pad
    tm = _pick_tile(T, 528, BF16_ROWS)
    cg = _pick_tile(T // CHUNK, 11, 1)

    shards = {n: _transport(w[n]) for n in _BIG}
    shards["meta"] = meta[None]
    wts = _Weights(shards)
    head = wts.rider([(n, 0) for n in _FFN1] + [("meta", 0)])
    _run_rider(head, "weights_gather_head")
    wts.take(head)
    meta_full = jnp.transpose(wts("meta", 0), (1, 0, 2)).reshape(N_META, D)

    reducer = _Reducer(unit=2 * shards["ffn1_gate"][0].size)
    loss_acc, dh, small, d_final = _local_step(x[0], meta_full, loss_target[0], w, wts, pad, tm, cg, reducer)
    loss = lax.psum(loss_acc[0, 0], ("x", "y", "c"))
    grad_x = dh[pad + N_META:][None]
    return _update(loss, grad_x, dh[pad:pad + N_META], reducer.finish(), small, d_final, w, mom, var)
```

```python
import functools
import math

import jax
import jax.numpy as jnp
from jax import lax
from jax.experimental import pallas as pl
from jax.experimental.pallas import tpu as pltpu

F32 = jnp.float32
BF16 = jnp.bfloat16

N_META = 16
RET_HEADS = 4
HEAD_DIM = 128
RET_WIDTH = RET_HEADS * HEAD_DIM
POOL_WINDOWS = (2, 4, 8, 16)
POOL_GROUPS = len(POOL_WINDOWS)
POOL_WIDTH = POOL_GROUPS * HEAD_DIM
CHUNK = 128
ROPE_BASE = 10000.0
EPS = 1e-6
ADAM_LR = 0.001
ADAM_B1 = 0.9
ADAM_B2 = 0.999
ADAM_EPS = 1e-08
ADAM_WD = 0.01
ADAM_STEP = 10

N_CHIPS = 4
LANES = 128
BF16_ROWS = 16
V7X_VMEM_LIMIT = 52 * 1024 * 1024
MESH = pl.DeviceIdType.MESH
ANY = pl.BlockSpec(memory_space=pl.ANY)


def _round_up(n, m):
    return -(-n // m) * m


def _pick_tile(n, target, mult):
    best = None
    for d in range(mult, min(n, target) + 1, mult):
        if n % d == 0:
            best = d
    assert best is not None, (n, target, mult)
    return best


def _params(sem=None):
    return pltpu.CompilerParams(dimension_semantics=sem, vmem_limit_bytes=V7X_VMEM_LIMIT)


def _dot(a, b):
    return jnp.dot(a, b, preferred_element_type=F32)


def _dot_nt(a, b):
    return lax.dot_general(a, b, (((1,), (1,)), ((), ())), preferred_element_type=F32)


def _dot_tn(a, b):
    return lax.dot_general(a, b, (((0,), (0,)), ((), ())), preferred_element_type=F32)


def _ein(spec, a, b):
    return jnp.einsum(spec, a, b, preferred_element_type=F32)


def _sigmoid(x):
    return jax.nn.sigmoid(x)


def _rms_fwd(x, gain):
    r = lax.rsqrt(jnp.mean(x * x, axis=-1, keepdims=True) + EPS)
    return x * r * gain


def _rms_bwd(x, gain, da):
    r = lax.rsqrt(jnp.mean(x * x, axis=-1, keepdims=True) + EPS)
    xh = x * r
    dgain = jnp.sum(da * xh, axis=0, keepdims=True)
    dxh = da * gain
    dx = r * (dxh - xh * jnp.mean(dxh * xh, axis=-1, keepdims=True))
    return dx, dgain


def _row_mask(t, tm, pad, shape):
    rows = t * tm + lax.broadcasted_iota(jnp.int32, shape, 0)
    return rows >= pad


def _mesh_pos():
    x, y, c = lax.axis_index("x"), lax.axis_index("y"), lax.axis_index("c")
    others = [(1 - x, y), (x, 1 - y), (1 - x, 1 - y)]
    return x, y, c, 2 * x + y, others


def _half_rows(c, rh):
    return pl.ds(pl.multiple_of(c * rh, rh), rh)


def _remote(src, dst, ssem, rsem, dev):
    return pltpu.make_async_remote_copy(src_ref=src, dst_ref=dst, send_sem=ssem, recv_sem=rsem,
                                        device_id=dev, device_id_type=MESH)


class _Rider:
    def __init__(self, ins, out_shapes, n_sem, start, finish):
        self.ins, self.out_shapes, self.n_sem, self.start, self.finish = ins, out_shapes, n_sem, start, finish
        self.results = None


class _SemWindow:
    def __init__(self, ref, base):
        self.ref, self.base = ref, base

    @property
    def at(self):
        return self

    def __getitem__(self, k):
        return self.ref.at[self.base + k]


def _join(riders):
    riders = [r for r in riders if r is not None]
    if len(riders) <= 1:
        return riders[0] if riders else None

    def run(which):
        def go(ins, outs, ssem, rsem):
            at, sem = 0, 0
            for r in riders:
                n = len(r.ins)
                getattr(r, which)(ins[at:at + n], outs[at:at + n], _SemWindow(ssem, sem), _SemWindow(rsem, sem))
                at, sem = at + n, sem + r.n_sem
        return go

    joined = _Rider(sum([list(r.ins) for r in riders], []), sum([list(r.out_shapes) for r in riders], []),
                    sum(r.n_sem for r in riders), run("start"), run("finish"))
    joined.parts = riders
    return joined


def _split_results(rider):
    at = 0
    for r in getattr(rider, "parts", []):
        r.results = rider.results[at:at + len(r.ins)]
        at += len(r.ins)


def _gather_rider(pieces):
    per = 7
    layers = [layer for _, layer in pieces]

    def first_copies(ins, outs, ssem, rsem):
        x, y, c, chip, others = _mesh_pos()
        copies = []
        for i, layer in enumerate(layers):
            mine = _half_rows(c, ins[i].shape[1] // 2)
            for j, (ox, oy) in enumerate(others):
                copies.append(_remote(ins[i].at[layer, mine, :], outs[i].at[chip, mine, :],
                                      ssem.at[per * i + j], rsem.at[per * i + j], (ox, oy, c)))
            copies.append(_remote(ins[i].at[layer], outs[i].at[chip],
                                  ssem.at[per * i + 6], rsem.at[per * i + 6], (x, y, 1 - c)))
        return copies

    def start(ins, outs, ssem, rsem):
        for cp in first_copies(ins, outs, ssem, rsem):
            cp.start()

    def finish(ins, outs, ssem, rsem):
        x, y, c, chip, others = _mesh_pos()
        sibling = (x, y, 1 - c)
        forwards = []
        for i in range(len(layers)):
            mine = _half_rows(c, ins[i].shape[1] // 2)
            for j, (ox, oy) in enumerate(others):
                rows = outs[i].at[2 * ox + oy, mine, :]
                _remote(rows, rows, ssem.at[per * i + j], rsem.at[per * i + j], (ox, oy, c)).wait_recv()
                fwd = _remote(rows, rows, ssem.at[per * i + 3 + j], rsem.at[per * i + 3 + j], sibling)
                fwd.start()
                forwards.append(fwd)
        for i in range(len(layers)):
            theirs = _half_rows(1 - c, ins[i].shape[1] // 2)
            for j, (ox, oy) in enumerate(others):
                rows = outs[i].at[2 * ox + oy, theirs, :]
                _remote(rows, rows, ssem.at[per * i + 3 + j], rsem.at[per * i + 3 + j], sibling).wait_recv()
            own = outs[i].at[chip]
            _remote(own, own, ssem.at[per * i + 6], rsem.at[per * i + 6], sibling).wait_recv()
        for cp in first_copies(ins, outs, ssem, rsem) + forwards:
            cp.wait_send()

    shapes = [jax.ShapeDtypeStruct((N_CHIPS,) + s.shape[1:], s.dtype) for s, _ in pieces]
    return _Rider([s for s, _ in pieces], shapes, per * len(pieces), start, finish)


def _chip_exchange_rider(ps):
    def copies(ins, outs, ssem, rsem):
        x, y, c, chip, others = _mesh_pos()
        return [_remote(ins[i].at[2 * ox + oy], outs[i].at[chip], ssem.at[3 * i + j], rsem.at[3 * i + j], (ox, oy, c))
                for i in range(len(ps)) for j, (ox, oy) in enumerate(others)]

    def start(ins, outs, ssem, rsem):
        for cp in copies(ins, outs, ssem, rsem):
            cp.start()

    def finish(ins, outs, ssem, rsem):
        x, y, c, chip, others = _mesh_pos()
        for i in range(len(ps)):
            for j, (ox, oy) in enumerate(others):
                slot = outs[i].at[2 * ox + oy]
                _remote(slot, slot, ssem.at[3 * i + j], rsem.at[3 * i + j], (ox, oy, c)).wait_recv()
        for cp in copies(ins, outs, ssem, rsem):
            cp.wait_send()

    return _Rider(list(ps), [jax.ShapeDtypeStruct(p.shape, p.dtype) for p in ps], 3 * len(ps), start, finish)


def _pair_exchange_rider(gs):
    def copies(ins, outs, ssem, rsem):
        x, y, c, _, _ = _mesh_pos()
        return [_remote(ins[i].at[:, _half_rows(1 - c, ins[i].shape[1] // 2), :], outs[i],
                        ssem.at[i], rsem.at[i], (x, y, 1 - c)) for i in range(len(gs))]

    def start(ins, outs, ssem, rsem):
        for cp in copies(ins, outs, ssem, rsem):
            cp.start()

    def finish(ins, outs, ssem, rsem):
        for cp in copies(ins, outs, ssem, rsem):
            cp.wait()

    shapes = [jax.ShapeDtypeStruct((g.shape[0], g.shape[1] // 2, g.shape[2]), g.dtype) for g in gs]
    return _Rider(list(gs), shapes, len(gs), start, finish)


def _run_rider(rider, name):
    def body(*refs):
        n = len(rider.ins)
        ins, outs = refs[:n], refs[n:2 * n]
        ssem, rsem = refs[2 * n:]
        rider.start(ins, outs, ssem, rsem)
        rider.finish(ins, outs, ssem, rsem)

    rider.results = pl.pallas_call(
        body,
        name=name,
        in_specs=[ANY] * len(rider.ins),
        out_specs=[ANY] * len(rider.ins),
        out_shape=rider.out_shapes,
        scratch_shapes=[pltpu.SemaphoreType.DMA((rider.n_sem,)), pltpu.SemaphoreType.DMA((rider.n_sem,))],
    )(*rider.ins)
    return rider.results


def _pair_gather(fs):
    n = len(fs)

    def body(*refs):
        bufs = refs[n:2 * n]
        ssem, rsem = refs[2 * n:]
        x, y, c, _, _ = _mesh_pos()
        sends = []
        for i in range(n):
            rh = bufs[i].shape[0] // 2
            mine = bufs[i].at[_half_rows(c, rh), :]
            cp = _remote(mine, mine, ssem.at[i], rsem.at[i], (x, y, 1 - c))
            cp.start()
            sends.append(cp)
        for i in range(n):
            rh = bufs[i].shape[0] // 2
            theirs = bufs[i].at[_half_rows(1 - c, rh), :]
            _remote(theirs, theirs, ssem.at[i], rsem.at[i], (x, y, 1 - c)).wait_recv()
        for cp in sends:
            cp.wait_send()

    return pl.pallas_call(
        body,
        name="grads_pair_gather",
        in_specs=[ANY] * n,
        out_specs=[ANY] * n,
        out_shape=[jax.ShapeDtypeStruct(f.shape, f.dtype) for f in fs],
        input_output_aliases={i: i for i in range(n)},
        scratch_shapes=[pltpu.SemaphoreType.DMA((n,)), pltpu.SemaphoreType.DMA((n,))],
    )(*fs)


def _call(body, *, name, grid, in_specs, out_specs, out_shape, operands, scratch=(), sem=None, rider=None):
    if rider is None:
        return pl.pallas_call(
            body, name=name, grid=grid, in_specs=in_specs, out_specs=out_specs, out_shape=out_shape,
            scratch_shapes=list(scratch), compiler_params=_params(sem))(*operands)
    n_in, n_out, n_sc, r = len(in_specs), len(out_specs), len(scratch), len(rider.ins)

    def carrying(*refs):
        a, b = n_in, n_in + r
        c, d = b + n_out, b + n_out + r
        e = d + n_sc
        ids = [pl.program_id(k) for k in range(len(grid))]
        first = functools.reduce(jnp.logical_and, [i == 0 for i in ids])
        last = functools.reduce(jnp.logical_and, [i == g - 1 for i, g in zip(ids, grid)])

        @pl.when(first)
        def _():
            rider.start(refs[a:b], refs[c:d], refs[e], refs[e + 1])

        body(*refs[:a], *refs[b:c], *refs[d:e])

        @pl.when(last)
        def _():
            rider.finish(refs[a:b], refs[c:d], refs[e], refs[e + 1])

    outs = pl.pallas_call(
        carrying, name=name, grid=grid,
        in_specs=list(in_specs) + [ANY] * r,
        out_specs=list(out_specs) + [ANY] * r,
        out_shape=list(out_shape) + list(rider.out_shapes),
        scratch_shapes=list(scratch) + [pltpu.SemaphoreType.DMA((rider.n_sem,)), pltpu.SemaphoreType.DMA((rider.n_sem,))],
        compiler_params=_params(("arbitrary",) * len(grid)),
    )(*operands, *rider.ins)
    rider.results = outs[n_out:]
    return outs[:n_out]


def _ffn_fwd(h, gain, wg, wu, wd, layer, tm, name, rider=None):
    T, D = h.shape
    Fs = wg.shape[-1]
    F = N_CHIPS * Fs

    def body(h_ref, g_ref, wg_ref, wu_ref, wd_ref, ho_ref, a_ref, go_ref, uo_ref, act_ref, acc_ref):
        s = pl.program_id(1)

        @pl.when(s == 0)
        def _():
            a_ref[...] = _rms_fwd(h_ref[...], g_ref[...]).astype(BF16)
            acc_ref[...] = jnp.zeros_like(acc_ref)

        a = a_ref[...]
        g = _dot(a, wg_ref[...])
        u = _dot(a, wu_ref[...])
        act = (g * _sigmoid(g) * u).astype(BF16)
        go_ref[...] = g.astype(BF16)
        uo_ref[...] = u.astype(BF16)
        act_ref[...] = act
        acc_ref[...] += _dot(act, wd_ref[...])

        @pl.when(s == N_CHIPS - 1)
        def _():
            ho_ref[...] = h_ref[...] + 0.5 * acc_ref[...]

    row = pl.BlockSpec((tm, D), lambda t, s: (t, 0))
    col = pl.BlockSpec((tm, Fs), lambda t, s: (t, s))
    wcol = pl.BlockSpec((None, D, Fs), lambda t, s: (s, 0, 0))
    return _call(
        body, name=name, grid=(T // tm, N_CHIPS),
        in_specs=[row, pl.BlockSpec((None, 1, D), lambda t, s: (layer, 0, 0)), wcol, wcol,
                  pl.BlockSpec((None, Fs, D), lambda t, s: (s, 0, 0))],
        out_specs=[row, row, col, col, col],
        out_shape=[jax.ShapeDtypeStruct((T, D), F32), jax.ShapeDtypeStruct((T, D), BF16)]
        + [jax.ShapeDtypeStruct((T, F), BF16)] * 3,
        scratch=[pltpu.VMEM((tm, D), F32)],
        sem=("parallel", "arbitrary"), operands=(h, gain, wg, wu, wd), rider=rider)


def _inproj_fwd(h, gain, win, layer, tm, name, rider=None):
    T, D = h.shape
    Ns = win.shape[-1]

    def body(h_ref, g_ref, w_ref, z_ref, b_ref):
        @pl.when(pl.program_id(1) == 0)
        def _():
            b_ref[...] = _rms_fwd(h_ref[...], g_ref[...]).astype(BF16)

        z_ref[...] = _dot(b_ref[...], w_ref[...])

    return _call(
        body, name=name, grid=(T // tm, N_CHIPS),
        in_specs=[pl.BlockSpec((tm, D), lambda t, s: (t, 0)),
                  pl.BlockSpec((None, 1, D), lambda t, s: (layer, 0, 0)),
                  pl.BlockSpec((None, D, Ns), lambda t, s: (s, 0, 0))],
        out_specs=[pl.BlockSpec((tm, Ns), lambda t, s: (t, s)), pl.BlockSpec((tm, D), lambda t, s: (t, 0))],
        out_shape=[jax.ShapeDtypeStruct((T, N_CHIPS * Ns), F32), jax.ShapeDtypeStruct((T, D), BF16)],
        sem=("parallel", "arbitrary"), operands=(h, gain, win), rider=rider)


def _ret_consts(T, pad):
    half = HEAD_DIM // 2
    inv_freq = ROPE_BASE ** (-jnp.arange(half, dtype=F32) / half)
    pos = jnp.arange(T, dtype=F32) - pad
    ang = pos[:, None] * inv_freq[None, :]
    cos = jnp.cos(ang)
    sin = jnp.sin(ang)
    cosf = jnp.concatenate([cos, cos], axis=1)
    sinf = jnp.concatenate([-sin, sin], axis=1)
    log_gamma = jnp.log1p(-(2.0 ** (-5.0 - jnp.arange(RET_HEADS, dtype=F32))))
    idx = jnp.arange(CHUNK, dtype=F32)
    diff = idx[:, None] - idx[None, :]
    intra = jnp.where(diff[None] >= 0, jnp.exp(diff[None] * log_gamma[:, None, None]), 0.0)
    k_decay = jnp.exp((CHUNK - 1.0 - idx)[None, :] * log_gamma[:, None])
    q_decay = jnp.exp((idx + 1.0)[None, :] * log_gamma[:, None])
    chunk_decay = jnp.exp(CHUNK * log_gamma)
    kdec = jnp.broadcast_to(k_decay[:, :, None], (RET_HEADS, CHUNK, HEAD_DIM))
    qdec = jnp.broadcast_to(q_decay[:, :, None], (RET_HEADS, CHUNK, HEAD_DIM))
    cdb = jnp.broadcast_to(chunk_decay[:, None, None], (RET_HEADS, 8, HEAD_DIM))
    return cosf, sinf, intra, kdec, qdec, cdb


def _rot(t, cosv, sinv):
    return t * cosv + pltpu.roll(t, HEAD_DIM // 2, 1) * sinv


def _rot_t(g, cosv, sinv):
    return g * cosv + pltpu.roll(g * sinv, HEAD_DIM // 2, 1)


def _head_specs(tg, section, order):
    return pl.BlockSpec((tg, HEAD_DIM), lambda h, g: (order(g), section * RET_HEADS + h))


def _ret_fwd(z, consts, cg, name, rider=None):
    T = z.shape[0]
    N = T // CHUNK
    ng = N // cg
    tg = cg * CHUNK
    cosf, sinf, intra, kdec, qdec, cdb = consts
    fwd = lambda g: g

    def body(zq, zk, zv, zg, cos_ref, sin_ref, m_ref, kd_ref, qd_ref, cd_ref, r_ref, o_ref, s_ref, st_ref):
        @pl.when(pl.program_id(1) == 0)
        def _():
            st_ref[...] = jnp.zeros_like(st_ref)

        cosv = cos_ref[...]
        sinv = sin_ref[...]
        q3 = (_rot(zq[...], cosv, sinv) * (HEAD_DIM ** -0.5)).reshape(cg, CHUNK, HEAD_DIM)
        k3 = _rot(zk[...], cosv, sinv).reshape(cg, CHUNK, HEAD_DIM)
        vb = zv[...].reshape(cg, CHUNK, HEAD_DIM).astype(BF16)
        scores = _ein("ncd,nmd->ncm", q3.astype(BF16), k3.astype(BF16)) * m_ref[...][None]
        inner = _ein("ncm,nmd->ncd", scores.astype(BF16), vb)
        kv = _ein("ncd,nce->nde", (k3 * kd_ref[...][None]).astype(BF16), vb)
        cd = cd_ref[0:1, :]
        state = st_ref[...]
        for n in range(cg):
            s_ref[n] = state
            state = state * cd + kv[n]
        st_ref[...] = state
        qdb = (q3 * qd_ref[...][None]).astype(BF16)
        cross = _ein("ncd,nde->nce", qdb, s_ref[...].astype(BF16))
        out = (inner + cross).reshape(tg, HEAD_DIM)
        o_ref[...] = out
        xc = out - jnp.mean(out, axis=-1, keepdims=True)
        rn = xc * lax.rsqrt(jnp.mean(xc * xc, axis=-1, keepdims=True) + EPS)
        g = zg[...]
        r_ref[...] = (rn * (g * _sigmoid(g))).astype(BF16)

    tab = pl.BlockSpec((tg, HEAD_DIM), lambda h, g: (g, 0))
    per_head = lambda rows: pl.BlockSpec((None, rows, HEAD_DIM), lambda h, g: (h, 0, 0))
    head_out = pl.BlockSpec((tg, HEAD_DIM), lambda h, g: (g, h))
    return _call(
        body, name=name, grid=(RET_HEADS, ng),
        in_specs=[_head_specs(tg, i, fwd) for i in range(4)]
        + [tab, tab, per_head(CHUNK), per_head(CHUNK), per_head(CHUNK), per_head(8)],
        out_specs=[head_out, head_out, pl.BlockSpec((None, cg, HEAD_DIM, HEAD_DIM), lambda h, g: (h, g, 0, 0))],
        out_shape=[jax.ShapeDtypeStruct((T, RET_WIDTH), BF16), jax.ShapeDtypeStruct((T, RET_WIDTH), F32),
                   jax.ShapeDtypeStruct((RET_HEADS, N, HEAD_DIM, HEAD_DIM), F32)],
        scratch=[pltpu.VMEM((HEAD_DIM, HEAD_DIM), F32)],
        sem=("parallel", "arbitrary"), operands=(z, z, z, z, cosf, sinf, intra, kdec, qdec, cdb), rider=rider)


def _window_sums(u, shift_of):
    sums = []
    s = u
    k = 1
    while k < POOL_WINDOWS[-1]:
        s = s + pltpu.roll(s, shift_of(k), 0)
        sums.append(s)
        k *= 2
    return sums


def _select_group(vals, g):
    out = vals[-1]
    for i in range(len(vals) - 2, -1, -1):
        out = jnp.where(g == i, vals[i], out)
    return out


def _pool_parts(u, g, T, pad):
    rows = lax.broadcasted_iota(jnp.int32, (T, HEAD_DIM), 0)
    valid = rows >= pad
    win = _select_group([float(w) for w in POOL_WINDOWS], g)
    div = jnp.clip((rows - pad + 1).astype(F32), 1.0, win)
    s = _select_group(_window_sums(u, lambda k: k), g)
    pooled = jnp.where(valid, s / div - u, 0.0)
    return pooled, div, valid


def _pool_specs(T, layer):
    first = 4 * RET_WIDTH // HEAD_DIM
    return [
        pl.BlockSpec((T, HEAD_DIM), lambda g: (0, first + g)),
        pl.BlockSpec((None, None, HEAD_DIM, HEAD_DIM), lambda g: (layer, g, 0, 0)),
        pl.BlockSpec((None, 1, HEAD_DIM), lambda g: (layer, 0, g)),
    ]


def _pool_fwd(z, maps, scale, layer, pad, name):
    T = z.shape[0]
    assert pad >= POOL_WINDOWS[-1], "window rolls wrap into the zero rows in front"

    def body(zu, maps_ref, sc_ref, pm_ref):
        g = pl.program_id(0)
        pooled, _, _ = _pool_parts(zu[...], g, T, pad)
        y = _dot(pooled.astype(BF16), maps_ref[...].astype(BF16))
        pm_ref[...] = (y * sc_ref[...]).astype(BF16)

    return _call(
        body, name=name, grid=(POOL_GROUPS,),
        in_specs=_pool_specs(T, layer),
        out_specs=[pl.BlockSpec((T, HEAD_DIM), lambda g: (0, g))],
        out_shape=[jax.ShapeDtypeStruct((T, POOL_WIDTH), BF16)],
        sem=("parallel",), operands=(z, maps, scale))[0]


def _gate_specs(tm, D):
    nb = D // RET_WIDTH
    first = (4 * RET_WIDTH + POOL_WIDTH) // RET_WIDTH
    return [pl.BlockSpec((tm, RET_WIDTH), functools.partial(lambda t, j: (t, j), j=first + j)) for j in range(2 * nb)]


def _load_gates(refs, nb):
    ga = jnp.concatenate([r[...] for r in refs[:nb]], axis=1) if nb > 1 else refs[0][...]
    gb = jnp.concatenate([r[...] for r in refs[nb:]], axis=1) if nb > 1 else refs[nb][...]
    return ga, gb


def _mix_fwd(h, r, pm, z, wru, wpu, wout, tm, name, rider=None):
    T, D = h.shape
    Dq = D // N_CHIPS
    nb = D // RET_WIDTH

    def body(*refs):
        h_ref, r_ref, pm_ref = refs[:3]
        gate_refs = refs[3:3 + 2 * nb]
        wru_ref, wpu_ref, wout_ref, ho_ref, mx_ref, ret_ref, pool_ref = refs[3 + 2 * nb:]
        rv = r_ref[...]
        pv = pm_ref[...]
        ret = jnp.concatenate([_dot(rv, wru_ref[s]) for s in range(N_CHIPS)], axis=1)
        pool = jnp.concatenate([_dot(pv, wpu_ref[s]) for s in range(N_CHIPS)], axis=1)
        ga, gb = _load_gates(gate_refs, nb)
        mixed = (_sigmoid(ga) * ret + _sigmoid(gb) * pool).astype(BF16)
        mx_ref[...] = mixed
        ret_ref[...] = ret.astype(BF16)
        pool_ref[...] = pool.astype(BF16)
        ho_ref[...] = h_ref[...] + _dot(mixed, wout_ref[...].reshape(D, D))

    row = pl.BlockSpec((tm, D), lambda t: (t, 0))
    half = pl.BlockSpec((tm, RET_WIDTH), lambda t: (t, 0))
    up = pl.BlockSpec((N_CHIPS, RET_WIDTH, Dq), lambda t: (0, 0, 0))
    return _call(
        body, name=name, grid=(T // tm,),
        in_specs=[row, half, half] + _gate_specs(tm, D) + [up, up, pl.BlockSpec((N_CHIPS, Dq, D), lambda t: (0, 0, 0))],
        out_specs=[row, row, row, row],
        out_shape=[jax.ShapeDtypeStruct((T, D), F32)] + [jax.ShapeDtypeStruct((T, D), BF16)] * 3,
        sem=("parallel",), operands=(h, r, pm, *([z] * (2 * nb)), wru, wpu, wout), rider=rider)


def _final_loss(h, gain, tgt, name):
    T, D = h.shape
    first = (T - tgt.shape[0]) // CHUNK

    def body(h_ref, g_ref, t_ref, dh_ref, loss_ref, dg_ref):
        i = pl.program_id(0)

        @pl.when(i == 0)
        def _():
            loss_ref[...] = jnp.zeros_like(loss_ref)
            dg_ref[...] = jnp.zeros_like(dg_ref)

        x = h_ref[...]
        gain_v = g_ref[...]
        err = jnp.where(i >= first, _rms_fwd(x, gain_v) - t_ref[...], 0.0)
        loss_ref[...] += 0.5 * jnp.sum(jnp.mean(err * err, axis=-1))
        dx, dgain = _rms_bwd(x, gain_v, err * (1.0 / D))
        dg_ref[...] += dgain
        dh_ref[...] = dx

    return _call(
        body, name=name, grid=(T // CHUNK,),
        in_specs=[pl.BlockSpec((CHUNK, D), lambda i: (i, 0)),
                  pl.BlockSpec((1, D), lambda i: (0, 0)),
                  pl.BlockSpec((CHUNK, D), lambda i: (jnp.maximum(i - first, 0), 0))],
        out_specs=[pl.BlockSpec((CHUNK, D), lambda i: (i, 0)),
                   pl.BlockSpec((1, LANES), lambda i: (0, 0)),
                   pl.BlockSpec((1, D), lambda i: (0, 0))],
        out_shape=[jax.ShapeDtypeStruct((T, D), F32), jax.ShapeDtypeStruct((1, LANES), F32),
                   jax.ShapeDtypeStruct((1, D), F32)],
        sem=("arbitrary",), operands=(h, gain, tgt))


def _ffn_bwd_act(dy, g, u, wd, tm, name, rider=None):
    T, D = dy.shape
    Fs = wd.shape[1]
    F = N_CHIPS * Fs

    def body(dy_ref, go_ref, uo_ref, wd_ref, dg_ref, du_ref, dyh_ref):
        @pl.when(pl.program_id(1) == 0)
        def _():
            dyh_ref[...] = (0.5 * dy_ref[...]).astype(BF16)

        dact = _dot_nt(dyh_ref[...], wd_ref[...])
        gf = go_ref[...].astype(F32)
        uf = uo_ref[...].astype(F32)
        sg = _sigmoid(gf)
        du_ref[...] = (dact * (gf * sg)).astype(BF16)
        dg_ref[...] = (dact * uf * (sg * (1.0 + gf * (1.0 - sg)))).astype(BF16)

    row = pl.BlockSpec((tm, D), lambda t, s: (t, 0))
    col = pl.BlockSpec((tm, Fs), lambda t, s: (t, s))
    return _call(
        body, name=name, grid=(T // tm, N_CHIPS),
        in_specs=[row, col, col, pl.BlockSpec((None, Fs, D), lambda t, s: (s, 0, 0))],
        out_specs=[col, col, row],
        out_shape=[jax.ShapeDtypeStruct((T, F), BF16), jax.ShapeDtypeStruct((T, F), BF16),
                   jax.ShapeDtypeStruct((T, D), BF16)],
        sem=("parallel", "arbitrary"), operands=(dy, g, u, wd), rider=rider)


def _ffn_bwd_in(dy, h, gain, dg, du, wg, wu, layer, tm, pad, name, rider=None):
    T, D = h.shape
    Fs = wg.shape[-1]

    def body(dy_ref, h_ref, g_ref, dg_ref, du_ref, wg_ref, wu_ref, dh_ref, dgain_ref, da_ref):
        t = pl.program_id(0)
        s = pl.program_id(1)

        @pl.when((t == 0) & (s == 0))
        def _():
            dgain_ref[...] = jnp.zeros_like(dgain_ref)

        @pl.when(s == 0)
        def _():
            da_ref[...] = jnp.zeros_like(da_ref)

        da_ref[...] += _dot_nt(dg_ref[...], wg_ref[...]) + _dot_nt(du_ref[...], wu_ref[...])

        @pl.when(s == N_CHIPS - 1)
        def _():
            dx, dgain = _rms_bwd(h_ref[...], g_ref[...], da_ref[...])
            dgain_ref[...] += dgain
            dh_ref[...] = jnp.where(_row_mask(t, tm, pad, (tm, D)), dy_ref[...] + dx, 0.0)

    row = pl.BlockSpec((tm, D), lambda t, s: (t, 0))
    col = pl.BlockSpec((tm, Fs), lambda t, s: (t, s))
    wcol = pl.BlockSpec((None, D, Fs), lambda t, s: (s, 0, 0))
    return _call(
        body, name=name, grid=(T // tm, N_CHIPS),
        in_specs=[row, row, pl.BlockSpec((None, 1, D), lambda t, s: (layer, 0, 0)), col, col, wcol, wcol],
        out_specs=[row, pl.BlockSpec((1, D), lambda t, s: (0, 0))],
        out_shape=[jax.ShapeDtypeStruct((T, D), F32), jax.ShapeDtypeStruct((1, D), F32)],
        scratch=[pltpu.VMEM((tm, D), F32)],
        sem=("arbitrary", "arbitrary"), operands=(dy, h, gain, dg, du, wg, wu), rider=rider)


def _grad_tn(a, b, mode, scale, tm, name, rider=None):
    T = a.shape[0]
    if mode == "col":
        per, R, C = 1, a.shape[1], b.shape[1] // N_CHIPS
        a_spec = pl.BlockSpec((tm, R), lambda s, t: (t, 0))
        b_spec = pl.BlockSpec((tm, C), lambda s, t: (t, s))
    else:
        per, R, C = 2, a.shape[1] // N_CHIPS, b.shape[1]
        a_spec = pl.BlockSpec((tm, per * R), lambda s, t: (t, s))
        b_spec = pl.BlockSpec((tm, C), lambda s, t: (t, 0))
    nt = T // tm

    def body(a_ref, b_ref, o_ref, acc_ref):
        t = pl.program_id(1)

        @pl.when(t == 0)
        def _():
            acc_ref[...] = jnp.zeros_like(acc_ref)

        acc_ref[...] += _dot_tn(a_ref[...].astype(BF16), b_ref[...].astype(BF16))

        @pl.when(t == nt - 1)
        def _():
            o_ref[...] = (scale * acc_ref[...]).astype(BF16).reshape(per, R, C)

    return _call(
        body, name=name, grid=(N_CHIPS // per, nt),
        in_specs=[a_spec, b_spec],
        out_specs=[pl.BlockSpec((per, R, C), lambda s, t: (s, 0, 0))],
        out_shape=[jax.ShapeDtypeStruct((N_CHIPS, R, C), BF16)],
        scratch=[pltpu.VMEM((per * R, C), F32)],
        sem=("parallel", "arbitrary"), operands=(a, b), rider=rider)[0]


def _mix_bwd_dx(dh, z, ret, pool, wout, wru, wpu, tm, name, rider=None):
    T, D = dh.shape
    Dq = D // N_CHIPS
    nb = D // RET_WIDTH

    def body(*refs):
        dh_ref = refs[0]
        gate_refs = refs[1:1 + 2 * nb]
        ret_ref, pool_ref, wout_ref, wru_ref, wpu_ref, dgab_ref, dret_ref, dpool_ref, dr_ref, dpm_ref = refs[1 + 2 * nb:]
        dmixed = _dot_nt(dh_ref[...].astype(BF16), wout_ref[...].reshape(D, D))
        ga, gb = _load_gates(gate_refs, nb)
        sa = _sigmoid(ga)
        sb = _sigmoid(gb)
        dgab_ref[:, :D] = (dmixed * ret_ref[...].astype(F32) * (sa * (1.0 - sa))).astype(BF16)
        dgab_ref[:, D:] = (dmixed * pool_ref[...].astype(F32) * (sb * (1.0 - sb))).astype(BF16)
        dret = (dmixed * sa).astype(BF16)
        dpool = (dmixed * sb).astype(BF16)
        dret_ref[...] = dret
        dpool_ref[...] = dpool
        dr = _dot_nt(dret[:, :Dq], wru_ref[0])
        dpm = _dot_nt(dpool[:, :Dq], wpu_ref[0])
        for s in range(1, N_CHIPS):
            dr += _dot_nt(dret[:, s * Dq:(s + 1) * Dq], wru_ref[s])
            dpm += _dot_nt(dpool[:, s * Dq:(s + 1) * Dq], wpu_ref[s])
        dr_ref[...] = dr
        dpm_ref[...] = dpm

    row = pl.BlockSpec((tm, D), lambda t: (t, 0))
    half = pl.BlockSpec((tm, RET_WIDTH), lambda t: (t, 0))
    up = pl.BlockSpec((N_CHIPS, RET_WIDTH, Dq), lambda t: (0, 0, 0))
    return _call(
        body, name=name, grid=(T // tm,),
        in_specs=[row] + _gate_specs(tm, D) + [row, row, pl.BlockSpec((N_CHIPS, Dq, D), lambda t: (0, 0, 0)), up, up],
        out_specs=[pl.BlockSpec((tm, 2 * D), lambda t: (t, 0)), row, row, half, half],
        out_shape=[jax.ShapeDtypeStruct((T, 2 * D), BF16), jax.ShapeDtypeStruct((T, D), BF16),
                   jax.ShapeDtypeStruct((T, D), BF16), jax.ShapeDtypeStruct((T, RET_WIDTH), F32),
                   jax.ShapeDtypeStruct((T, POOL_WIDTH), F32)],
        sem=("parallel",), operands=(dh, *([z] * (2 * nb)), ret, pool, wout, wru, wpu), rider=rider)


def _pool_bwd(z, dpm, maps, scale, layer, pad, name):
    T = z.shape[0]

    def body(zu, maps_ref, sc_ref, dpm_ref, du_ref, dmaps_ref, dsc_ref):
        g = pl.program_id(0)
        u = zu[...]
        pooled, div, valid = _pool_parts(u, g, T, pad)
        pb = pooled.astype(BF16)
        mb = maps_ref[...].astype(BF16)
        dp = dpm_ref[...]
        dsc_ref[...] = jnp.sum(dp * _dot(pb, mb), axis=0, keepdims=True)
        dyb = (dp * sc_ref[...]).astype(BF16)
        dmaps_ref[...] = _dot_tn(pb, dyb)
        dpooled = jnp.where(valid, _dot_nt(dyb, mb), 0.0)
        ahead = _select_group(_window_sums(dpooled / div, lambda k: T - k), g)
        du_ref[...] = jnp.where(valid, ahead - dpooled, 0.0).astype(BF16)

    blk = pl.BlockSpec((T, HEAD_DIM), lambda g: (0, g))
    return _call(
        body, name=name, grid=(POOL_GROUPS,),
        in_specs=_pool_specs(T, layer) + [blk],
        out_specs=[blk, pl.BlockSpec((None, HEAD_DIM, HEAD_DIM), lambda g: (g, 0, 0)),
                   pl.BlockSpec((1, HEAD_DIM), lambda g: (0, g))],
        out_shape=[jax.ShapeDtypeStruct((T, POOL_WIDTH), BF16),
                   jax.ShapeDtypeStruct((POOL_GROUPS, HEAD_DIM, HEAD_DIM), F32),
                   jax.ShapeDtypeStruct((1, POOL_WIDTH), F32)],
        sem=("parallel",), operands=(z, maps, scale, dpm))


def _ret_bwd_local(z, o_pre, s_all, dr, consts, cg, name):
    T = z.shape[0]
    N = T // CHUNK
    ng = N // cg
    tg = cg * CHUNK
    cosf, sinf, intra, _, qdec, _ = consts
    fwd = lambda g: g

    def body(zq, zk, zv, zg, o_ref, s_ref, dr_ref, cos_ref, sin_ref, m_ref, qd_ref,
             dq_ref, dg_ref, dk_ref, dv_ref, ds_ref):
        cosv = cos_ref[...]
        sinv = sin_ref[...]
        scale = HEAD_DIM ** -0.5
        q3 = (_rot(zq[...], cosv, sinv) * scale).reshape(cg, CHUNK, HEAD_DIM)
        k3 = _rot(zk[...], cosv, sinv).reshape(cg, CHUNK, HEAD_DIM)
        qb = q3.astype(BF16)
        kb = k3.astype(BF16)
        vb = zv[...].reshape(cg, CHUNK, HEAD_DIM).astype(BF16)
        mask = m_ref[...][None]
        sb = (_ein("ncd,nmd->ncm", qb, kb) * mask).astype(BF16)
        qdv = qd_ref[...][None]
        qdb = (q3 * qdv).astype(BF16)

        out = o_ref[...]
        xc = out - jnp.mean(out, axis=-1, keepdims=True)
        rstd = lax.rsqrt(jnp.mean(xc * xc, axis=-1, keepdims=True) + EPS)
        rn = xc * rstd
        g = zg[...]
        sg = _sigmoid(g)
        drv = dr_ref[...]
        dg_ref[...] = (drv * rn * (sg * (1.0 + g * (1.0 - sg)))).astype(BF16)
        drn = drv * (g * sg)
        dout = rstd * (drn - jnp.mean(drn, axis=-1, keepdims=True)
                       - rn * jnp.mean(drn * rn, axis=-1, keepdims=True))
        dob = dout.reshape(cg, CHUNK, HEAD_DIM).astype(BF16)

        dsb = (_ein("ncd,nmd->ncm", dob, vb) * mask).astype(BF16)
        dv_ref[...] = _ein("ncm,ncd->nmd", sb, dob).reshape(tg, HEAD_DIM)
        dk_ref[...] = _ein("ncm,ncd->nmd", dsb, qb).reshape(tg, HEAD_DIM)
        dq3 = _ein("ncm,nmd->ncd", dsb, kb) + _ein("nce,nde->ncd", dob, s_ref[...].astype(BF16)) * qdv
        dq_ref[...] = _rot_t(dq3.reshape(tg, HEAD_DIM) * scale, cosv, sinv).astype(BF16)
        ds_ref[...] = _ein("ncd,nce->nde", qdb, dob)

    tab = pl.BlockSpec((tg, HEAD_DIM), lambda h, g: (g, 0))
    per_head = pl.BlockSpec((None, CHUNK, HEAD_DIM), lambda h, g: (h, 0, 0))
    head_blk = pl.BlockSpec((tg, HEAD_DIM), lambda h, g: (g, h))
    state_blk = pl.BlockSpec((None, cg, HEAD_DIM, HEAD_DIM), lambda h, g: (h, g, 0, 0))
    return _call(
        body, name=name, grid=(RET_HEADS, ng),
        in_specs=[_head_specs(tg, i, fwd) for i in range(4)]
        + [head_blk, state_blk, head_blk, tab, tab, per_head, per_head],
        out_specs=[head_blk, head_blk, head_blk, head_blk, state_blk],
        out_shape=[jax.ShapeDtypeStruct((T, RET_WIDTH), BF16), jax.ShapeDtypeStruct((T, RET_WIDTH), BF16),
                   jax.ShapeDtypeStruct((T, RET_WIDTH), F32), jax.ShapeDtypeStruct((T, RET_WIDTH), F32),
                   jax.ShapeDtypeStruct((RET_HEADS, N, HEAD_DIM, HEAD_DIM), F32)],
        sem=("parallel", "parallel"), operands=(z, z, z, z, o_pre, s_all, dr, cosf, sinf, intra, qdec))


def _ret_bwd_state(z, dkp, dvp, ds, consts, cg, name):
    T = z.shape[0]
    N = T // CHUNK
    ng = N // cg
    tg = cg * CHUNK
    cosf, sinf, _, kdec, _, cdb = consts
    rev = lambda g: ng - 1 - g

    def body(zk, zv, dkp_ref, dvp_ref, ds_ref, cos_ref, sin_ref, kd_ref, cd_ref, dk_ref, dv_ref, gs_ref, dkv_ref):
        @pl.when(pl.program_id(1) == 0)
        def _():
            gs_ref[...] = jnp.zeros_like(gs_ref)

        cosv = cos_ref[...]
        sinv = sin_ref[...]
        cd = cd_ref[0:1, :]
        grad = gs_ref[...]
        for n in reversed(range(cg)):
            dkv_ref[n] = grad
            grad = ds_ref[n] + cd * grad
        gs_ref[...] = grad
        dkvb = dkv_ref[...].astype(BF16)
        kdv = kd_ref[...][None]
        k3 = _rot(zk[...], cosv, sinv).reshape(cg, CHUNK, HEAD_DIM)
        vb = zv[...].reshape(cg, CHUNK, HEAD_DIM).astype(BF16)
        dk3 = _ein("nce,nde->ncd", vb, dkvb) * kdv
        dv3 = _ein("ncd,nde->nce", (k3 * kdv).astype(BF16), dkvb)
        dk_ref[...] = _rot_t(dkp_ref[...] + dk3.reshape(tg, HEAD_DIM), cosv, sinv).astype(BF16)
        dv_ref[...] = (dvp_ref[...] + dv3.reshape(tg, HEAD_DIM)).astype(BF16)

    tab = pl.BlockSpec((tg, HEAD_DIM), lambda h, g: (rev(g), 0))
    head_blk = pl.BlockSpec((tg, HEAD_DIM), lambda h, g: (rev(g), h))
    return _call(
        body, name=name, grid=(RET_HEADS, ng),
        in_specs=[_head_specs(tg, 1, rev), _head_specs(tg, 2, rev), head_blk, head_blk,
                  pl.BlockSpec((None, cg, HEAD_DIM, HEAD_DIM), lambda h, g: (h, rev(g), 0, 0)),
                  tab, tab,
                  pl.BlockSpec((None, CHUNK, HEAD_DIM), lambda h, g: (h, 0, 0)),
                  pl.BlockSpec((None, 8, HEAD_DIM), lambda h, g: (h, 0, 0))],
        out_specs=[head_blk, head_blk],
        out_shape=[jax.ShapeDtypeStruct((T, RET_WIDTH), BF16)] * 2,
        scratch=[pltpu.VMEM((HEAD_DIM, HEAD_DIM), F32), pltpu.VMEM((cg, HEAD_DIM, HEAD_DIM), F32)],
        sem=("parallel", "arbitrary"), operands=(z, z, dkp, dvp, ds, cosf, sinf, kdec, cdb))


def _inproj_bwd_dx(dz, win, h, gain, dh_in, layer, tm, pad, name, rider=None):
    T, D = h.shape
    Ns = win.shape[-1]

    def body(dz_ref, w_ref, h_ref, g_ref, dhi_ref, dh_ref, dgain_ref, db_ref):
        t = pl.program_id(0)
        s = pl.program_id(1)

        @pl.when((t == 0) & (s == 0))
        def _():
            dgain_ref[...] = jnp.zeros_like(dgain_ref)

        @pl.when(s == 0)
        def _():
            db_ref[...] = jnp.zeros_like(db_ref)

        db_ref[...] += _dot_nt(dz_ref[...], w_ref[...])

        @pl.when(s == N_CHIPS - 1)
        def _():
            dx, dgain = _rms_bwd(h_ref[...], g_ref[...], db_ref[...])
            dgain_ref[...] += dgain
            dh_ref[...] = jnp.where(_row_mask(t, tm, pad, (tm, D)), dhi_ref[...] + dx, 0.0)

    row = pl.BlockSpec((tm, D), lambda t, s: (t, 0))
    return _call(
        body, name=name, grid=(T // tm, N_CHIPS),
        in_specs=[pl.BlockSpec((tm, Ns), lambda t, s: (t, s)),
                  pl.BlockSpec((None, D, Ns), lambda t, s: (s, 0, 0)),
                  row, pl.BlockSpec((None, 1, D), lambda t, s: (layer, 0, 0)), row],
        out_specs=[row, pl.BlockSpec((1, D), lambda t, s: (0, 0))],
        out_shape=[jax.ShapeDtypeStruct((T, D), F32), jax.ShapeDtypeStruct((1, D), F32)],
        scratch=[pltpu.VMEM((tm, D), F32)],
        sem=("arbitrary", "arbitrary"), operands=(dz, win, h, gain, dh_in), rider=rider)


def _sum_pair(gs, rs, c_idx, name):
    n = len(gs)

    def body(c_ref, *refs):
        for g_ref, r_ref, o_ref in zip(refs[:n], refs[n:2 * n], refs[2 * n:]):
            o_ref[...] = (g_ref[...].astype(F32) + r_ref[...].astype(F32)).astype(BF16)

    halves = [pl.BlockSpec((None,) + r.shape[1:], lambda s, c_ref: (s, 0, 0)) for r in rs]
    return pl.pallas_call(
        body,
        name=name,
        grid_spec=pltpu.PrefetchScalarGridSpec(
            num_scalar_prefetch=1,
            grid=(N_CHIPS,),
            in_specs=[pl.BlockSpec((None,) + r.shape[1:], lambda s, c_ref: (s, c_ref[0], 0)) for r in rs] + halves,
            out_specs=halves,
        ),
        out_shape=[jax.ShapeDtypeStruct(r.shape, BF16) for r in rs],
        compiler_params=_params(("parallel",)),
    )(c_idx, *gs, *rs)


def _sum_chips(ps, rs, pos, name):
    n = len(ps)
    quarters = 4

    def body(pos_ref, *refs):
        chip = pos_ref[0]
        for p_ref, r_ref, o_ref in zip(refs[:n], refs[n:2 * n], refs[2 * n:]):
            own = p_ref[...].astype(F32)
            terms = [jnp.where(chip == k, own, r_ref[k].astype(F32)) for k in range(N_CHIPS)]
            o_ref[...] = ((terms[0] + terms[1]) + terms[2]) + terms[3]

    def rows(r):
        assert r.shape[1] % (quarters * BF16_ROWS) == 0, r.shape
        return r.shape[1] // quarters

    return pl.pallas_call(
        body,
        name=name,
        grid_spec=pltpu.PrefetchScalarGridSpec(
            num_scalar_prefetch=1,
            grid=(quarters,),
            in_specs=[pl.BlockSpec((None, rows(r), r.shape[2]), lambda q, pos_ref: (pos_ref[0], q, 0)) for r in rs]
            + [pl.BlockSpec((N_CHIPS, rows(r), r.shape[2]), lambda q, pos_ref: (0, q, 0)) for r in rs],
            out_specs=[pl.BlockSpec((rows(r), r.shape[2]), lambda q, pos_ref: (pos_ref[1] * quarters + q, 0))
                       for r in rs],
        ),
        out_shape=[jax.ShapeDtypeStruct((2 * r.shape[1], r.shape[2]), F32) for r in rs],
        compiler_params=_params(("arbitrary",)),
    )(pos, *ps, *rs)


def _small_all_reduce(p):
    rows, width = p.shape

    def body(p_ref, o_ref, sib_ref, slot_ref, ssem, rsem):
        x, y, c, chip, others = _mesh_pos()
        pair = _remote(p_ref, sib_ref, ssem.at[0], rsem.at[0], (x, y, 1 - c))
        pair.start()
        pair.wait()
        slot_ref[chip] = p_ref[...] + sib_ref[...]
        sends = []
        for j, (ox, oy) in enumerate(others):
            cp = _remote(slot_ref.at[chip], slot_ref.at[chip], ssem.at[1 + j], rsem.at[1 + j], (ox, oy, c))
            cp.start()
            sends.append(cp)
        for j, (ox, oy) in enumerate(others):
            slot = slot_ref.at[2 * ox + oy]
            _remote(slot, slot, ssem.at[1 + j], rsem.at[1 + j], (ox, oy, c)).wait_recv()
        for cp in sends:
            cp.wait_send()
        o_ref[...] = ((slot_ref[0] + slot_ref[1]) + slot_ref[2]) + slot_ref[3]

    vmem = pl.BlockSpec(memory_space=pltpu.VMEM)
    return pl.pallas_call(
        body,
        name="small_grads_all_reduce",
        in_specs=[vmem],
        out_specs=vmem,
        out_shape=jax.ShapeDtypeStruct(p.shape, F32),
        scratch_shapes=[pltpu.VMEM((rows, width), F32), pltpu.VMEM((N_CHIPS, rows, width), F32),
                        pltpu.SemaphoreType.DMA((4,)), pltpu.SemaphoreType.DMA((4,))],
    )(p)


def _adamw(gs, w, m, v, name):
    L, R, C = w.shape
    Ct = gs[0].shape[1]
    tr = _pick_tile(R, 256, 8)

    def body(*refs):
        g_refs = refs[:L]
        w_ref, m_ref, v_ref, go_ref, d_ref, mo_ref, vo_ref = refs[L:]
        layer = pl.program_id(0)
        grad = g_refs[L - 1][...]
        for i in range(L - 2, -1, -1):
            grad = jnp.where(layer == i, g_refs[i][...], grad)
        if Ct != C:
            grad = grad[:, :C]
        m_new = ADAM_B1 * m_ref[...] + (1.0 - ADAM_B1) * grad
        v_new = ADAM_B2 * v_ref[...] + (1.0 - ADAM_B2) * jnp.square(grad)
        m_hat = m_new / (1.0 - ADAM_B1 ** ADAM_STEP)
        v_hat = v_new / (1.0 - ADAM_B2 ** ADAM_STEP)
        go_ref[...] = grad
        d_ref[...] = -ADAM_LR * (m_hat / (jnp.sqrt(v_hat) + ADAM_EPS) + ADAM_WD * w_ref[...])
        mo_ref[...] = m_new
        vo_ref[...] = v_new

    g_specs = [pl.BlockSpec((tr, Ct), functools.partial(lambda l, r, i: (jnp.where(l == i, r, 0), 0), i=i))
               for i in range(L)]
    blk = pl.BlockSpec((None, tr, C), lambda l, r: (l, r, 0))
    return pl.pallas_call(
        body,
        name=name,
        grid=(L, R // tr),
        in_specs=g_specs + [blk, blk, blk],
        out_specs=[blk] * 4,
        out_shape=[jax.ShapeDtypeStruct((L, R, C), F32)] * 4,
        compiler_params=_params(("arbitrary", "arbitrary")),
    )(*gs, w, m, v)


_FFN1 = ("ffn1_gate", "ffn1_up", "ffn1_down")
_FFN2 = ("ffn2_gate", "ffn2_up", "ffn2_down")
_MIXW = ("w_ret_up", "w_pool_up", "w_out")
_BIG = _FFN1 + ("w_in",) + _MIXW + _FFN2
_TRANSPOSED = ("ffn1_gate", "ffn1_up", "ffn2_gate", "ffn2_up")
_SMALL = ("ffn1_norm", "mix_norm", "ffn2_norm", "final_norm", "pool_scale", "pool_maps")
_ORDER = ("meta", "ffn1_norm", "ffn1_gate", "ffn1_up", "ffn1_down", "mix_norm", "w_in", "pool_maps",
          "pool_scale", "w_ret_up", "w_pool_up", "w_out", "ffn2_norm", "ffn2_gate", "ffn2_up", "ffn2_down",
          "final_norm")


def _transport(a):
    n, r, c = a.shape
    out = a.astype(BF16)
    if c % LANES:
        out = jnp.concatenate([out, jnp.zeros((n, r, _round_up(c, LANES) - c), BF16)], axis=2)
    if r % LANES:
        out = jnp.concatenate([out, jnp.zeros((n, _round_up(r, LANES) - r, out.shape[2]), BF16)], axis=1)
    return out


def _pack_rows(parts, width):
    rows = [p.reshape(-1, width) for p in parts]
    total = sum(r.shape[0] for r in rows)
    fill = _round_up(total, 8) - total
    if fill:
        rows.append(jnp.zeros((fill, width), F32))
    return jnp.concatenate(rows, axis=0)


def _unpack_rows(packed, shapes, width):
    out, at = [], 0
    for shp in shapes:
        n = math.prod(shp) // width
        out.append(packed[at:at + n].reshape(shp))
        at += n
    return out


class _Weights:
    def __init__(self, shards):
        self.shards = shards
        self.full = {}

    def rider(self, keys):
        r = _gather_rider([(self.shards[n], i) for n, i in keys])
        r.keys = keys
        return r

    def take(self, rider):
        for key, arr in zip(rider.keys, rider.results):
            self.full[key] = arr

    def __call__(self, name, layer):
        return self.full[(name, layer)]


def _local_step(x, meta_full, tgt, w, wts, pad, tm, cg, reducer):
    D = x.shape[1]
    T = pad + N_META + x.shape[0]
    L = w["ffn1_norm"].shape[0]
    pool_maps = w["pool_maps"]
    gains = {n: w[n].reshape(L, 1, D) for n in ("ffn1_norm", "mix_norm", "ffn2_norm")}
    scale3 = w["pool_scale"].reshape(L, 1, POOL_WIDTH)
    consts = _ret_consts(T, pad)
    tl = _pick_tile(T, 2 * tm, BF16_ROWS)
    def gather(keys):
        return wts.rider(keys) if keys and keys[0] not in wts.full else None

    def done(rider):
        if rider is not None:
            wts.take(rider)

    h = jnp.concatenate([jnp.zeros((pad, D), F32), meta_full, x], axis=0)
    saved = []
    for i in range(L):
        s = {"h0": h}
        rd = gather([("w_in", i)] + [(n, i) for n in _MIXW])
        h, s["a1"], s["g1"], s["u1"], s["act1"] = _ffn_fwd(
            h, gains["ffn1_norm"], wts("ffn1_gate", i), wts("ffn1_up", i), wts("ffn1_down", i), i, tl,
            f"ffn1_fwd_{i}", rd)
        done(rd)
        s["h1"] = h
        rd = gather([("ffn2_gate", i)])
        s["z"], s["b"] = _inproj_fwd(h, gains["mix_norm"], wts("w_in", i), i, tl, f"inproj_fwd_{i}", rd)
        done(rd)
        rd = gather([("ffn2_up", i)])
        s["r"], s["o_pre"], s["s_all"] = _ret_fwd(s["z"], consts, cg, f"retention_fwd_{i}", rd)
        done(rd)
        s["pm"] = _pool_fwd(s["z"], pool_maps, scale3, i, pad, f"pool_fwd_{i}")
        rd = gather([("ffn2_down", i)])
        h, s["mixed"], s["ret"], s["pool"] = _mix_fwd(
            h, s["r"], s["pm"], s["z"], wts("w_ret_up", i), wts("w_pool_up", i), wts("w_out", i), tm,
            f"mix_fwd_{i}", rd)
        done(rd)
        s["h2"] = h
        rd = gather([(n, i + 1) for n in _FFN1]) if i + 1 < L else None
        h, s["a2"], s["g2"], s["u2"], s["act2"] = _ffn_fwd(
            h, gains["ffn2_norm"], wts("ffn2_gate", i), wts("ffn2_up", i), wts("ffn2_down", i), i, tl,
            f"ffn2_fwd_{i}", rd)
        done(rd)
        saved.append(s)

    dh, loss_acc, d_final = _final_loss(h, w["final_norm"].reshape(1, D), tgt, "final_norm_loss")

    small = {n: [None] * L for n in ("ffn1_norm", "mix_norm", "ffn2_norm", "pool_scale", "pool_maps")}

    carry = {"ffn_act": 1.0, "ffn_in": 2.2, "mix_bwd": 1.0, "inproj_bwd": 1.5, "w_in": 1.0}

    tk = _pick_tile(T, 1408, LANES)

    def grad(n, a, b, i, mode):
        rd = reducer.rider(carry.get(n, 1.0 if i == 0 else 0.5))
        reducer.add(n, i, _grad_tn(a, b, mode, 1.0, tk, f"grad_{n}_{i}", rd))
        reducer.done(rd)

    def ffn_bwd(which, dy, h_in, g, u, i):
        rd = reducer.rider(carry["ffn_act"])
        dg, du, dyh = _ffn_bwd_act(dy, g, u, wts(f"{which}_down", i), tl, f"{which}_bwd_act_{i}", rd)
        reducer.done(rd)
        rd = reducer.rider(carry["ffn_in"])
        dh_in, dgain = _ffn_bwd_in(dy, h_in, gains[f"{which}_norm"], dg, du, wts(f"{which}_gate", i),
                                   wts(f"{which}_up", i), i, tl, pad, f"{which}_bwd_in_{i}", rd)
        reducer.done(rd)
        return dh_in, dg, du, dgain, dyh

    for i in reversed(range(L)):
        s = saved[i]
        dh, dg, du, small["ffn2_norm"][i], dyh = ffn_bwd("ffn2", dh, s["h2"], s["g2"], s["u2"], i)
        grad("ffn2_gate", dg, s["a2"], i, "row")
        grad("ffn2_up", du, s["a2"], i, "row")
        grad("ffn2_down", s["act2"], dyh, i, "row")
        reducer.stage(f"ffn2_{i}")
        rd = reducer.rider(carry["mix_bwd"])
        dgab, dret, dpool, dr, dpm = _mix_bwd_dx(
            dh, s["z"], s["ret"], s["pool"], wts("w_out", i), wts("w_ret_up", i), wts("w_pool_up", i), tm,
            f"mix_bwd_{i}", rd)
        reducer.done(rd)
        grad("w_out", s["mixed"], dh, i, "row")
        grad("w_ret_up", s["r"], dret, i, "col")
        grad("w_pool_up", s["pm"], dpool, i, "col")
        du_pool, small["pool_maps"][i], small["pool_scale"][i] = _pool_bwd(
            s["z"], dpm, pool_maps, scale3, i, pad, f"pool_bwd_{i}")
        dq, dgr, dkp, dvp, ds = _ret_bwd_local(s["z"], s["o_pre"], s["s_all"], dr, consts, cg, f"retention_bwd_{i}")
        dk, dv = _ret_bwd_state(s["z"], dkp, dvp, ds, consts, cg, f"retention_bwd_state_{i}")
        dz = jnp.concatenate([dq, dk, dv, dgr, du_pool, dgab], axis=1)
        dh2 = dh
        rd = reducer.rider(carry["inproj_bwd"])
        dh, small["mix_norm"][i] = _inproj_bwd_dx(
            dz, wts("w_in", i), s["h1"], gains["mix_norm"], dh2, i, tl, pad, f"inproj_bwd_{i}", rd)
        reducer.done(rd)
        grad("w_in", s["b"], dz, i, "col")
        reducer.stage(f"mid{i}")
        dh, dg, du, small["ffn1_norm"][i], dyh = ffn_bwd("ffn1", dh, s["h0"], s["g1"], s["u1"], i)
        grad("ffn1_gate", dg, s["a1"], i, "row")
        if i == 0:
            reducer.stage("gate0")
        grad("ffn1_up", du, s["a1"], i, "row")
        if i == 0:
            reducer.stage("up0")
        grad("ffn1_down", s["act1"], dyh, i, "row")
        reducer.stage(f"end{i}")

    return loss_acc, dh, small, d_final


class _Reducer:
    def __init__(self, unit):
        self.c_idx = lax.axis_index("c").astype(jnp.int32).reshape(1)
        chip = 2 * lax.axis_index("x") + lax.axis_index("y")
        self.pos = jnp.stack([chip, lax.axis_index("c")]).astype(jnp.int32)
        self.pending, self.stages, self.queue, self.halves = [], [], [], {}
        self.unit = unit
        self.calls = 0

    def add(self, name, layer, g):
        self.pending.append(((name, layer), g))

    def stage(self, tag):
        if self.pending:
            self.stages.append((tag, self.pending))
            self.pending = []

    def _pair_rider(self):
        if not self.stages:
            return None
        tag, items = self.stages.pop(0)
        rd = _pair_exchange_rider([g for _, g in items])
        rd.tag, rd.keys = tag, [k for k, _ in items]
        return rd

    def _chip_rider(self, units):
        take, size = [], 0
        while self.queue and (units is None or size + self.queue[0][1].size <= units * self.unit):
            take.append(self.queue.pop(0))
            size += take[-1][1].size
        if not take:
            return None
        rd = _chip_exchange_rider([p for _, p in take])
        rd.keys = [k for k, _ in take]
        return rd

    def rider(self, units):
        self.riding = (self._pair_rider(), self._chip_rider(units))
        return _join(self.riding)

    def done(self, rd):
        if rd is None:
            return
        _split_results(rd)
        pair, chips = self.riding
        if len([r for r in self.riding if r is not None]) == 1:
            (pair or chips).results = rd.results
        self.calls += 1
        if pair is not None:
            sums = _sum_pair(pair.ins, pair.results, self.c_idx, f"sum_pair_{pair.tag}")
            self.queue += list(zip(pair.keys, sums))
        if chips is not None:
            sums = _sum_chips(chips.ins, chips.results, self.pos, f"sum_chips_{self.calls}")
            self.halves.update(zip(chips.keys, sums))

    def finish(self):
        assert not self.pending
        while self.stages or self.queue:
            self.riding = (self._pair_rider(), self._chip_rider(None))
            rd = _join(self.riding)
            _run_rider(rd, f"grads_exchange_tail_{self.calls}")
            self.done(rd)
        keys = list(self.halves)
        return dict(zip(keys, _pair_gather([self.halves[k] for k in keys])))


def _update(loss, grad_x, d_meta_rows, shard_grads, small, d_final, w, mom, var):
    meta = w["meta"]
    D = w["final_norm"].shape[0]
    L = w["ffn1_norm"].shape[0]
    Dq = D // N_CHIPS

    small_parts = [jnp.concatenate(small[n], axis=0) for n in ("ffn1_norm", "mix_norm", "ffn2_norm")]
    small_parts += [d_final, jnp.concatenate(small["pool_scale"], axis=0), jnp.concatenate(small["pool_maps"], axis=0)]
    reduced = _small_all_reduce(_pack_rows(small_parts + [d_meta_rows], D))
    small_shapes = [w[n].shape for n in _SMALL]
    small_rows = sum(math.prod(shp) for shp in small_shapes) // D
    chip = 2 * lax.axis_index("x") + lax.axis_index("y")
    d_meta = lax.dynamic_slice_in_dim(reduced[small_rows:small_rows + N_META], chip * Dq, Dq, axis=1)

    out = {}
    for n in _BIG:
        gs = [shard_grads[(n, i)] for i in range(L)]
        if n in _TRANSPOSED:
            res = _adamw(gs, *(jnp.swapaxes(t[n], 1, 2) for t in (w, mom, var)), f"adamw_{n}")
            out[n] = [jnp.swapaxes(r, 1, 2) for r in res]
        else:
            out[n] = _adamw(gs, w[n], mom[n], var[n], f"adamw_{n}")
    names = _SMALL + ("meta",)
    packed_g = _pack_rows([reduced[:small_rows], d_meta], D)
    packed = [_pack_rows([t[n] for n in names], D) for t in (w, mom, var)]
    res = _adamw([packed_g], packed[0][None], packed[1][None], packed[2][None], "adamw_small")
    shapes = small_shapes + [meta.shape]
    unpacked = [_unpack_rows(r[0], shapes, D) for r in res]
    for k, n in enumerate(names):
        out[n] = tuple(u[k] for u in unpacked)

    return (loss, grad_x) + tuple(out[n][j] for j in range(4) for n in _ORDER)


def kernel(x, meta, ffn1_norm, ffn1_gate, ffn1_up, ffn1_down, mix_norm, w_in, pool_maps, pool_scale, w_ret_up, w_pool_up, w_out, ffn2_norm, ffn2_gate, ffn2_up, ffn2_down, final_norm, loss_target, m_meta, m_ffn1_norm, m_ffn1_gate, m_ffn1_up, m_ffn1_down, m_mix_norm, m_w_in, m_pool_maps, m_pool_scale, m_w_ret_up, m_w_pool_up, m_w_out, m_ffn2_norm, m_ffn2_gate, m_ffn2_up, m_ffn2_down, m_final_norm, v_meta, v_ffn1_norm, v_ffn1_gate, v_ffn1_up, v_ffn1_down, v_mix_norm, v_w_in, v_pool_maps, v_pool_scale, v_w_ret_up, v_w_pool_up, v_w_out, v_ffn2_norm, v_ffn2_gate, v_ffn2_up, v_ffn2_down, v_final_norm):
    args = dict(locals())
    w = {n: args[n] for n in _ORDER}
    mom = {n: args["m_" + n] for n in _ORDER}
    var = {n: args["v_" + n] for n in _ORDER}

    assert x.shape[0] == 1, "one batch element per device"
    seq, D = x.shape[1], x.shape[2]
    assert seq % CHUNK == 0 and D % RET_WIDTH == 0 and (2 * POOL_WIDTH) % D == 0
    pad = (-(seq + N_META)) % CHUNK
    T = seq + N_META + pad
    tm = _pick_tile(T, 528, BF16_ROWS)
    cg = _pick_tile(T // CHUNK, 11, 1)

    shards = {n: _transport(w[n]) for n in _BIG}
    shards["meta"] = meta[None]
    wts = _Weights(shards)
    head = wts.rider([(n, 0) for n in _FFN1] + [("meta", 0)])
    _run_rider(head, "weights_gather_head")
    wts.take(head)
    meta_full = jnp.transpose(wts("meta", 0), (1, 0, 2)).reshape(N_META, D)

    reducer = _Reducer(unit=2 * shards["ffn1_gate"][0].size)
    loss_acc, dh, small, d_final = _local_step(x[0], meta_full, loss_target[0], w, wts, pad, tm, cg, reducer)
    loss = lax.psum(loss_acc[0, 0], ("x", "y", "c"))
    grad_x = dh[pad + N_META:][None]
    return _update(loss, grad_x, dh[pad:pad + N_META], reducer.finish(), small, d_final, w, mom, var)
```

```python
import functools
import math

import jax
import jax.numpy as jnp
from jax import lax
from jax.experimental import pallas as pl
from jax.experimental.pallas import tpu as pltpu

F32 = jnp.float32
BF16 = jnp.bfloat16

N_META = 16
RET_HEADS = 4
HEAD_DIM = 128
RET_WIDTH = RET_HEADS * HEAD_DIM
POOL_WINDOWS = (2, 4, 8, 16)
POOL_GROUPS = len(POOL_WINDOWS)
POOL_WIDTH = POOL_GROUPS * HEAD_DIM
CHUNK = 128
ROPE_BASE = 10000.0
EPS = 1e-6
ADAM_LR = 0.001
ADAM_B1 = 0.9
ADAM_B2 = 0.999
ADAM_EPS = 1e-08
ADAM_WD = 0.01
ADAM_STEP = 10

N_CHIPS = 4
LANES = 128
BF16_ROWS = 16
V7X_VMEM_LIMIT = 52 * 1024 * 1024
MESH = pl.DeviceIdType.MESH
ANY = pl.BlockSpec(memory_space=pl.ANY)


def _round_up(n, m):
    return -(-n // m) * m


def _pick_tile(n, target, mult):
    best = None
    for d in range(mult, min(n, target) + 1, mult):
        if n % d == 0:
            best = d
    assert best is not None, (n, target, mult)
    return best


def _params(sem=None):
    return pltpu.CompilerParams(dimension_semantics=sem, vmem_limit_bytes=V7X_VMEM_LIMIT)


def _dot(a, b):
    return jnp.dot(a, b, preferred_element_type=F32)


def _dot_nt(a, b):
    return lax.dot_general(a, b, (((1,), (1,)), ((), ())), preferred_element_type=F32)


def _dot_tn(a, b):
    return lax.dot_general(a, b, (((0,), (0,)), ((), ())), preferred_element_type=F32)


def _ein(spec, a, b):
    return jnp.einsum(spec, a, b, preferred_element_type=F32)


def _sigmoid(x):
    return jax.nn.sigmoid(x)


def _rms_fwd(x, gain):
    r = lax.rsqrt(jnp.mean(x * x, axis=-1, keepdims=True) + EPS)
    return x * r * gain


def _rms_bwd(x, gain, da):
    r = lax.rsqrt(jnp.mean(x * x, axis=-1, keepdims=True) + EPS)
    xh = x * r
    dgain = jnp.sum(da * xh, axis=0, keepdims=True)
    dxh = da * gain
    dx = r * (dxh - xh * jnp.mean(dxh * xh, axis=-1, keepdims=True))
    return dx, dgain


def _row_mask(t, tm, pad, shape):
    rows = t * tm + lax.broadcasted_iota(jnp.int32, shape, 0)
    return rows >= pad


def _mesh_pos():
    x, y, c = lax.axis_index("x"), lax.axis_index("y"), lax.axis_index("c")
    others = [(1 - x, y), (x, 1 - y), (1 - x, 1 - y)]
    return x, y, c, 2 * x + y, others


def _half_rows(c, rh):
    return pl.ds(pl.multiple_of(c * rh, rh), rh)


def _remote(src, dst, ssem, rsem, dev):
    return pltpu.make_async_remote_copy(src_ref=src, dst_ref=dst, send_sem=ssem, recv_sem=rsem,
                                        device_id=dev, device_id_type=MESH)


class _Rider:
    def __init__(self, ins, out_shapes, n_sem, start, finish):
        self.ins, self.out_shapes, self.n_sem, self.start, self.finish = ins, out_shapes, n_sem, start, finish
        self.results = None


class _SemWindow:
    def __init__(self, ref, base):
        self.ref, self.base = ref, base

    @property
    def at(self):
        return self

    def __getitem__(self, k):
        return self.ref.at[self.base + k]


def _join(riders):
    riders = [r for r in riders if r is not None]
    if len(riders) <= 1:
        return riders[0] if riders else None

    def run(which):
        def go(ins, outs, ssem, rsem):
            at, sem = 0, 0
            for r in riders:
                n = len(r.ins)
                getattr(r, which)(ins[at:at + n], outs[at:at + n], _SemWindow(ssem, sem), _SemWindow(rsem, sem))
                at, sem = at + n, sem + r.n_sem
        return go

    joined = _Rider(sum([list(r.ins) for r in riders], []), sum([list(r.out_shapes) for r in riders], []),
                    sum(r.n_sem for r in riders), run("start"), run("finish"))
    joined.parts = riders
    return joined


def _split_results(rider):
    at = 0
    for r in getattr(rider, "parts", []):
        r.results = rider.results[at:at + len(r.ins)]
        at += len(r.ins)


def _gather_rider(pieces):
    per = 7
    layers = [layer for _, layer in pieces]

    def first_copies(ins, outs, ssem, rsem):
        x, y, c, chip, others = _mesh_pos()
        copies = []
        for i, layer in enumerate(layers):
            mine = _half_rows(c, ins[i].shape[1] // 2)
            for j, (ox, oy) in enumerate(others):
                copies.append(_remote(ins[i].at[layer, mine, :], outs[i].at[chip, mine, :],
                                      ssem.at[per * i + j], rsem.at[per * i + j], (ox, oy, c)))
            copies.append(_remote(ins[i].at[layer], outs[i].at[chip],
                                  ssem.at[per * i + 6], rsem.at[per * i + 6], (x, y, 1 - c)))
        return copies

    def start(ins, outs, ssem, rsem):
        for cp in first_copies(ins, outs, ssem, rsem):
            cp.start()

    def finish(ins, outs, ssem, rsem):
        x, y, c, chip, others = _mesh_pos()
        sibling = (x, y, 1 - c)
        forwards = []
        for i in range(len(layers)):
            mine = _half_rows(c, ins[i].shape[1] // 2)
            for j, (ox, oy) in enumerate(others):
                rows = outs[i].at[2 * ox + oy, mine, :]
                _remote(rows, rows, ssem.at[per * i + j], rsem.at[per * i + j], (ox, oy, c)).wait_recv()
                fwd = _remote(rows, rows, ssem.at[per * i + 3 + j], rsem.at[per * i + 3 + j], sibling)
                fwd.start()
                forwards.append(fwd)
        for i in range(len(layers)):
            theirs = _half_rows(1 - c, ins[i].shape[1] // 2)
            for j, (ox, oy) in enumerate(others):
                rows = outs[i].at[2 * ox + oy, theirs, :]
                _remote(rows, rows, ssem.at[per * i + 3 + j], rsem.at[per * i + 3 + j], sibling).wait_recv()
            own = outs[i].at[chip]
            _remote(own, own, ssem.at[per * i + 6], rsem.at[per * i + 6], sibling).wait_recv()
        for cp in first_copies(ins, outs, ssem, rsem) + forwards:
            cp.wait_send()

    shapes = [jax.ShapeDtypeStruct((N_CHIPS,) + s.shape[1:], s.dtype) for s, _ in pieces]
    return _Rider([s for s, _ in pieces], shapes, per * len(pieces), start, finish)


def _chip_exchange_rider(ps):
    def copies(ins, outs, ssem, rsem):
        x, y, c, chip, others = _mesh_pos()
        return [_remote(ins[i].at[2 * ox + oy], outs[i].at[chip], ssem.at[3 * i + j], rsem.at[3 * i + j], (ox, oy, c))
                for i in range(len(ps)) for j, (ox, oy) in enumerate(others)]

    def start(ins, outs, ssem, rsem):
        for cp in copies(ins, outs, ssem, rsem):
            cp.start()

    def finish(ins, outs, ssem, rsem):
        x, y, c, chip, others = _mesh_pos()
        for i in range(len(ps)):
            for j, (ox, oy) in enumerate(others):
                slot = outs[i].at[2 * ox + oy]
                _remote(slot, slot, ssem.at[3 * i + j], rsem.at[3 * i + j], (ox, oy, c)).wait_recv()
        for cp in copies(ins, outs, ssem, rsem):
            cp.wait_send()

    return _Rider(list(ps), [jax.ShapeDtypeStruct(p.shape, p.dtype) for p in ps], 3 * len(ps), start, finish)


def _pair_exchange_rider(gs):
    def copies(ins, outs, ssem, rsem):
        x, y, c, _, _ = _mesh_pos()
        return [_remote(ins[i].at[:, _half_rows(1 - c, ins[i].shape[1] // 2), :], outs[i],
                        ssem.at[i], rsem.at[i], (x, y, 1 - c)) for i in range(len(gs))]

    def start(ins, outs, ssem, rsem):
        for cp in copies(ins, outs, ssem, rsem):
            cp.start()

    def finish(ins, outs, ssem, rsem):
        for cp in copies(ins, outs, ssem, rsem):
            cp.wait()

    shapes = [jax.ShapeDtypeStruct((g.shape[0], g.shape[1] // 2, g.shape[2]), g.dtype) for g in gs]
    return _Rider(list(gs), shapes, len(gs), start, finish)


def _run_rider(rider, name):
    def body(*refs):
        n = len(rider.ins)
        ins, outs = refs[:n], refs[n:2 * n]
        ssem, rsem = refs[2 * n:]
        rider.start(ins, outs, ssem, rsem)
        rider.finish(ins, outs, ssem, rsem)

    rider.results = pl.pallas_call(
        body,
        name=name,
        in_specs=[ANY] * len(rider.ins),
        out_specs=[ANY] * len(rider.ins),
        out_shape=rider.out_shapes,
        scratch_shapes=[pltpu.SemaphoreType.DMA((rider.n_sem,)), pltpu.SemaphoreType.DMA((rider.n_sem,))],
    )(*rider.ins)
    return rider.results


def _pair_gather(fs):
    n = len(fs)

    def body(*refs):
        bufs = refs[n:2 * n]
        ssem, rsem = refs[2 * n:]
        x, y, c, _, _ = _mesh_pos()
        sends = []
        for i in range(n):
            rh = bufs[i].shape[0] // 2
            mine = bufs[i].at[_half_rows(c, rh), :]
            cp = _remote(mine, mine, ssem.at[i], rsem.at[i], (x, y, 1 - c))
            cp.start()
            sends.append(cp)
        for i in range(n):
            rh = bufs[i].shape[0] // 2
            theirs = bufs[i].at[_half_rows(1 - c, rh), :]
            _remote(theirs, theirs, ssem.at[i], rsem.at[i], (x, y, 1 - c)).wait_recv()
        for cp in sends:
            cp.wait_send()

    return pl.pallas_call(
        body,
        name="grads_pair_gather",
        in_specs=[ANY] * n,
        out_specs=[ANY] * n,
        out_shape=[jax.ShapeDtypeStruct(f.shape, f.dtype) for f in fs],
        input_output_aliases={i: i for i in range(n)},
        scratch_shapes=[pltpu.SemaphoreType.DMA((n,)), pltpu.SemaphoreType.DMA((n,))],
    )(*fs)


def _call(body, *, name, grid, in_specs, out_specs, out_shape, operands, scratch=(), sem=None, rider=None):
    if rider is None:
        return pl.pallas_call(
            body, name=name, grid=grid, in_specs=in_specs, out_specs=out_specs, out_shape=out_shape,
            scratch_shapes=list(scratch), compiler_params=_params(sem))(*operands)
    n_in, n_out, n_sc, r = len(in_specs), len(out_specs), len(scratch), len(rider.ins)

    def carrying(*refs):
        a, b = n_in, n_in + r
        c, d = b + n_out, b + n_out + r
        e = d + n_sc
        ids = [pl.program_id(k) for k in range(len(grid))]
        first = functools.reduce(jnp.logical_and, [i == 0 for i in ids])
        last = functools.reduce(jnp.logical_and, [i == g - 1 for i, g in zip(ids, grid)])

        @pl.when(first)
        def _():
            rider.start(refs[a:b], refs[c:d], refs[e], refs[e + 1])

        body(*refs[:a], *refs[b:c], *refs[d:e])

        @pl.when(last)
        def _():
            rider.finish(refs[a:b], refs[c:d], refs[e], refs[e + 1])

    outs = pl.pallas_call(
        carrying, name=name, grid=grid,
        in_specs=list(in_specs) + [ANY] * r,
        out_specs=list(out_specs) + [ANY] * r,
        out_shape=list(out_shape) + list(rider.out_shapes),
        scratch_shapes=list(scratch) + [pltpu.SemaphoreType.DMA((rider.n_sem,)), pltpu.SemaphoreType.DMA((rider.n_sem,))],
        compiler_params=_params(("arbitrary",) * len(grid)),
    )(*operands, *rider.ins)
    rider.results = outs[n_out:]
    return outs[:n_out]


def _ffn_fwd(h, gain, wg, wu, wd, layer, tm, name, rider=None):
    T, D = h.shape
    Fs = wg.shape[-1]
    F = N_CHIPS * Fs

    def body(h_ref, g_ref, wg_ref, wu_ref, wd_ref, ho_ref, a_ref, go_ref, uo_ref, act_ref, acc_ref):
        s = pl.program_id(1)

        @pl.when(s == 0)
        def _():
            a_ref[...] = _rms_fwd(h_ref[...], g_ref[...]).astype(BF16)
            acc_ref[...] = jnp.zeros_like(acc_ref)

        a = a_ref[...]
        g = _dot(a, wg_ref[...])
        u = _dot(a, wu_ref[...])
        act = (g * _sigmoid(g) * u).astype(BF16)
        go_ref[...] = g.astype(BF16)
        uo_ref[...] = u.astype(BF16)
        act_ref[...] = act
        acc_ref[...] += _dot(act, wd_ref[...])

        @pl.when(s == N_CHIPS - 1)
        def _():
            ho_ref[...] = h_ref[...] + 0.5 * acc_ref[...]

    row = pl.BlockSpec((tm, D), lambda t, s: (t, 0))
    col = pl.BlockSpec((tm, Fs), lambda t, s: (t, s))
    wcol = pl.BlockSpec((None, D, Fs), lambda t, s: (s, 0, 0))
    return _call(
        body, name=name, grid=(T // tm, N_CHIPS),
        in_specs=[row, pl.BlockSpec((None, 1, D), lambda t, s: (layer, 0, 0)), wcol, wcol,
                  pl.BlockSpec((None, Fs, D), lambda t, s: (s, 0, 0))],
        out_specs=[row, row, col, col, col],
        out_shape=[jax.ShapeDtypeStruct((T, D), F32), jax.ShapeDtypeStruct((T, D), BF16)]
        + [jax.ShapeDtypeStruct((T, F), BF16)] * 3,
        scratch=[pltpu.VMEM((tm, D), F32)],
        sem=("parallel", "arbitrary"), operands=(h, gain, wg, wu, wd), rider=rider)


def _inproj_fwd(h, gain, win, layer, tm, name, rider=None):
    T, D = h.shape
    Ns = win.shape[-1]

    def body(h_ref, g_ref, w_ref, z_ref, b_ref):
        @pl.when(pl.program_id(1) == 0)
        def _():
            b_ref[...] = _rms_fwd(h_ref[...], g_ref[...]).astype(BF16)

        z_ref[...] = _dot(b_ref[...], w_ref[...])

    return _call(
        body, name=name, grid=(T // tm, N_CHIPS),
        in_specs=[pl.BlockSpec((tm, D), lambda t, s: (t, 0)),
                  pl.BlockSpec((None, 1, D), lambda t, s: (layer, 0, 0)),
                  pl.BlockSpec((None, D, Ns), lambda t, s: (s, 0, 0))],
        out_specs=[pl.BlockSpec((tm, Ns), lambda t, s: (t, s)), pl.BlockSpec((tm, D), lambda t, s: (t, 0))],
        out_shape=[jax.ShapeDtypeStruct((T, N_CHIPS * Ns), F32), jax.ShapeDtypeStruct((T, D), BF16)],
        sem=("parallel", "arbitrary"), operands=(h, gain, win), rider=rider)


def _ret_consts(T, pad):
    half = HEAD_DIM // 2
    inv_freq = ROPE_BASE ** (-jnp.arange(half, dtype=F32) / half)
    pos = jnp.arange(T, dtype=F32) - pad
    ang = pos[:, None] * inv_freq[None, :]
    cos = jnp.cos(ang)
    sin = jnp.sin(ang)
    cosf = jnp.concatenate([cos, cos], axis=1)
    sinf = jnp.concatenate([-sin, sin], axis=1)
    log_gamma = jnp.log1p(-(2.0 ** (-5.0 - jnp.arange(RET_HEADS, dtype=F32))))
    idx = jnp.arange(CHUNK, dtype=F32)
    diff = idx[:, None] - idx[None, :]
    intra = jnp.where(diff[None] >= 0, jnp.exp(diff[None] * log_gamma[:, None, None]), 0.0)
    k_decay = jnp.exp((CHUNK - 1.0 - idx)[None, :] * log_gamma[:, None])
    q_decay = jnp.exp((idx + 1.0)[None, :] * log_gamma[:, None])
    chunk_decay = jnp.exp(CHUNK * log_gamma)
    kdec = jnp.broadcast_to(k_decay[:, :, None], (RET_HEADS, CHUNK, HEAD_DIM))
    qdec = jnp.broadcast_to(q_decay[:, :, None], (RET_HEADS, CHUNK, HEAD_DIM))
    cdb = jnp.broadcast_to(chunk_decay[:, None, None], (RET_HEADS, 8, HEAD_DIM))
    return cosf, sinf, intra, kdec, qdec, cdb


def _rot(t, cosv, sinv):
    return t * cosv + pltpu.roll(t, HEAD_DIM // 2, 1) * sinv


def _rot_t(g, cosv, sinv):
    return g * cosv + pltpu.roll(g * sinv, HEAD_DIM // 2, 1)


def _head_specs(tg, section, order):
    return pl.BlockSpec((tg, HEAD_DIM), lambda h, g: (order(g), section * RET_HEADS + h))


def _ret_fwd(z, consts, cg, name, rider=None):
    T = z.shape[0]
    N = T // CHUNK
    ng = N // cg
    tg = cg * CHUNK
    cosf, sinf, intra, kdec, qdec, cdb = consts
    fwd = lambda g: g

    def body(zq, zk, zv, zg, cos_ref, sin_ref, m_ref, kd_ref, qd_ref, cd_ref, r_ref, o_ref, s_ref, st_ref):
        @pl.when(pl.program_id(1) == 0)
        def _():
            st_ref[...] = jnp.zeros_like(st_ref)

        cosv = cos_ref[...]
        sinv = sin_ref[...]
        q3 = (_rot(zq[...], cosv, sinv) * (HEAD_DIM ** -0.5)).reshape(cg, CHUNK, HEAD_DIM)
        k3 = _rot(zk[...], cosv, sinv).reshape(cg, CHUNK, HEAD_DIM)
        vb = zv[...].reshape(cg, CHUNK, HEAD_DIM).astype(BF16)
        scores = _ein("ncd,nmd->ncm", q3.astype(BF16), k3.astype(BF16)) * m_ref[...][None]
        inner = _ein("ncm,nmd->ncd", scores.astype(BF16), vb)
        kv = _ein("ncd,nce->nde", (k3 * kd_ref[...][None]).astype(BF16), vb)
        cd = cd_ref[0:1, :]
        state = st_ref[...]
        for n in range(cg):
            s_ref[n] = state
            state = state * cd + kv[n]
        st_ref[...] = state
        qdb = (q3 * qd_ref[...][None]).astype(BF16)
        cross = _ein("ncd,nde->nce", qdb, s_ref[...].astype(BF16))
        out = (inner + cross).reshape(tg, HEAD_DIM)
        o_ref[...] = out
        xc = out - jnp.mean(out, axis=-1, keepdims=True)
        rn = xc * lax.rsqrt(jnp.mean(xc * xc, axis=-1, keepdims=True) + EPS)
        g = zg[...]
        r_ref[...] = (rn * (g * _sigmoid(g))).astype(BF16)

    tab = pl.BlockSpec((tg, HEAD_DIM), lambda h, g: (g, 0))
    per_head = lambda rows: pl.BlockSpec((None, rows, HEAD_DIM), lambda h, g: (h, 0, 0))
    head_out = pl.BlockSpec((tg, HEAD_DIM), lambda h, g: (g, h))
    return _call(
        body, name=name, grid=(RET_HEADS, ng),
        in_specs=[_head_specs(tg, i, fwd) for i in range(4)]
        + [tab, tab, per_head(CHUNK), per_head(CHUNK), per_head(CHUNK), per_head(8)],
        out_specs=[head_out, head_out, pl.BlockSpec((None, cg, HEAD_DIM, HEAD_DIM), lambda h, g: (h, g, 0, 0))],
        out_shape=[jax.ShapeDtypeStruct((T, RET_WIDTH), BF16), jax.ShapeDtypeStruct((T, RET_WIDTH), F32),
                   jax.ShapeDtypeStruct((RET_HEADS, N, HEAD_DIM, HEAD_DIM), F32)],
        scratch=[pltpu.VMEM((HEAD_DIM, HEAD_DIM), F32)],
        sem=("parallel", "arbitrary"), operands=(z, z, z, z, cosf, sinf, intra, kdec, qdec, cdb), rider=rider)


def _window_sums(u, shift_of):
    sums = []
    s = u
    k = 1
    while k < POOL_WINDOWS[-1]:
        s = s + pltpu.roll(s, shift_of(k), 0)
        sums.append(s)
        k *= 2
    return sums


def _select_group(vals, g):
    out = vals[-1]
    for i in range(len(vals) - 2, -1, -1):
        out = jnp.where(g == i, vals[i], out)
    return out


def _pool_parts(u, g, T, pad):
    rows = lax.broadcasted_iota(jnp.int32, (T, HEAD_DIM), 0)
    valid = rows >= pad
    win = _select_group([float(w) for w in POOL_WINDOWS], g)
    div = jnp.clip((rows - pad + 1).astype(F32), 1.0, win)
    s = _select_group(_window_sums(u, lambda k: k), g)
    pooled = jnp.where(valid, s / div - u, 0.0)
    return pooled, div, valid


def _pool_specs(T, layer):
    first = 4 * RET_WIDTH // HEAD_DIM
    return [
        pl.BlockSpec((T, HEAD_DIM), lambda g: (0, first + g)),
        pl.BlockSpec((None, None, HEAD_DIM, HEAD_DIM), lambda g: (layer, g, 0, 0)),
        pl.BlockSpec((None, 1, HEAD_DIM), lambda g: (layer, 0, g)),
    ]


def _pool_fwd(z, maps, scale, layer, pad, name):
    T = z.shape[0]
    assert pad >= POOL_WINDOWS[-1], "window rolls wrap into the zero rows in front"

    def body(zu, maps_ref, sc_ref, pm_ref):
        g = pl.program_id(0)
        pooled, _, _ = _pool_parts(zu[...], g, T, pad)
        y = _dot(pooled.astype(BF16), maps_ref[...].astype(BF16))
        pm_ref[...] = (y * sc_ref[...]).astype(BF16)

    return _call(
        body, name=name, grid=(POOL_GROUPS,),
        in_specs=_pool_specs(T, layer),
        out_specs=[pl.BlockSpec((T, HEAD_DIM), lambda g: (0, g))],
        out_shape=[jax.ShapeDtypeStruct((T, POOL_WIDTH), BF16)],
        sem=("parallel",), operands=(z, maps, scale))[0]


def _gate_specs(tm, D):
    nb = D // RET_WIDTH
    first = (4 * RET_WIDTH + POOL_WIDTH) // RET_WIDTH
    return [pl.BlockSpec((tm, RET_WIDTH), functools.partial(lambda t, j: (t, j), j=first + j)) for j in range(2 * nb)]


def _load_gates(refs, nb):
    ga = jnp.concatenate([r[...] for r in refs[:nb]], axis=1) if nb > 1 else refs[0][...]
    gb = jnp.concatenate([r[...] for r in refs[nb:]], axis=1) if nb > 1 else refs[nb][...]
    return ga, gb


def _mix_fwd(h, r, pm, z, wru, wpu, wout, tm, name, rider=None):
    T, D = h.shape
    Dq = D // N_CHIPS
    nb = D // RET_WIDTH

    def body(*refs):
        h_ref, r_ref, pm_ref = refs[:3]
        gate_refs = refs[3:3 + 2 * nb]
        wru_ref, wpu_ref, wout_ref, ho_ref, mx_ref, ret_ref, pool_ref = refs[3 + 2 * nb:]
        rv = r_ref[...]
        pv = pm_ref[...]
        ret = jnp.concatenate([_dot(rv, wru_ref[s]) for s in range(N_CHIPS)], axis=1)
        pool = jnp.concatenate([_dot(pv, wpu_ref[s]) for s in range(N_CHIPS)], axis=1)
        ga, gb = _load_gates(gate_refs, nb)
        mixed = (_sigmoid(ga) * ret + _sigmoid(gb) * pool).astype(BF16)
        mx_ref[...] = mixed
        ret_ref[...] = ret.astype(BF16)
        pool_ref[...] = pool.astype(BF16)
        ho_ref[...] = h_ref[...] + _dot(mixed, wout_ref[...].reshape(D, D))

    row = pl.BlockSpec((tm, D), lambda t: (t, 0))
    half = pl.BlockSpec((tm, RET_WIDTH), lambda t: (t, 0))
    up = pl.BlockSpec((N_CHIPS, RET_WIDTH, Dq), lambda t: (0, 0, 0))
    return _call(
        body, name=name, grid=(T // tm,),
        in_specs=[row, half, half] + _gate_specs(tm, D) + [up, up, pl.BlockSpec((N_CHIPS, Dq, D), lambda t: (0, 0, 0))],
        out_specs=[row, row, row, row],
        out_shape=[jax.ShapeDtypeStruct((T, D), F32)] + [jax.ShapeDtypeStruct((T, D), BF16)] * 3,
        sem=("parallel",), operands=(h, r, pm, *([z] * (2 * nb)), wru, wpu, wout), rider=rider)


def _final_loss(h, gain, tgt, name):
    T, D = h.shape
    first = (T - tgt.shape[0]) // CHUNK

    def body(h_ref, g_ref, t_ref, dh_ref, loss_ref, dg_ref):
        i = pl.program_id(0)

        @pl.when(i == 0)
        def _():
            loss_ref[...] = jnp.zeros_like(loss_ref)
            dg_ref[...] = jnp.zeros_like(dg_ref)

        x = h_ref[...]
        gain_v = g_ref[...]
        err = jnp.where(i >= first, _rms_fwd(x, gain_v) - t_ref[...], 0.0)
        loss_ref[...] += 0.5 * jnp.sum(jnp.mean(err * err, axis=-1))
        dx, dgain = _rms_bwd(x, gain_v, err * (1.0 / D))
        dg_ref[...] += dgain
        dh_ref[...] = dx

    return _call(
        body, name=name, grid=(T // CHUNK,),
        in_specs=[pl.BlockSpec((CHUNK, D), lambda i: (i, 0)),
                  pl.BlockSpec((1, D), lambda i: (0, 0)),
                  pl.BlockSpec((CHUNK, D), lambda i: (jnp.maximum(i - first, 0), 0))],
        out_specs=[pl.BlockSpec((CHUNK, D), lambda i: (i, 0)),
                   pl.BlockSpec((1, LANES), lambda i: (0, 0)),
                   pl.BlockSpec((1, D), lambda i: (0, 0))],
        out_shape=[jax.ShapeDtypeStruct((T, D), F32), jax.ShapeDtypeStruct((1, LANES), F32),
                   jax.ShapeDtypeStruct((1, D), F32)],
        sem=("arbitrary",), operands=(h, gain, tgt))


def _ffn_bwd_act(dy, g, u, wd, tm, name, rider=None):
    T, D = dy.shape
    Fs = wd.shape[1]
    F = N_CHIPS * Fs

    def body(dy_ref, go_ref, uo_ref, wd_ref, dg_ref, du_ref, dyh_ref):
        @pl.when(pl.program_id(1) == 0)
        def _():
            dyh_ref[...] = (0.5 * dy_ref[...]).astype(BF16)

        dact = _dot_nt(dyh_ref[...], wd_ref[...])
        gf = go_ref[...].astype(F32)
        uf = uo_ref[...].astype(F32)
        sg = _sigmoid(gf)
        du_ref[...] = (dact * (gf * sg)).astype(BF16)
        dg_ref[...] = (dact * uf * (sg * (1.0 + gf * (1.0 - sg)))).astype(BF16)

    row = pl.BlockSpec((tm, D), lambda t, s: (t, 0))
    col = pl.BlockSpec((tm, Fs), lambda t, s: (t, s))
    return _call(
        body, name=name, grid=(T // tm, N_CHIPS),
        in_specs=[row, col, col, pl.BlockSpec((None, Fs, D), lambda t, s: (s, 0, 0))],
        out_specs=[col, col, row],
        out_shape=[jax.ShapeDtypeStruct((T, F), BF16), jax.ShapeDtypeStruct((T, F), BF16),
                   jax.ShapeDtypeStruct((T, D), BF16)],
        sem=("parallel", "arbitrary"), operands=(dy, g, u, wd), rider=rider)


def _ffn_bwd_in(dy, h, gain, dg, du, wg, wu, layer, tm, pad, name, rider=None):
    T, D = h.shape
    Fs = wg.shape[-1]

    def body(dy_ref, h_ref, g_ref, dg_ref, du_ref, wg_ref, wu_ref, dh_ref, dgain_ref, da_ref):
        t = pl.program_id(0)
        s = pl.program_id(1)

        @pl.when((t == 0) & (s == 0))
        def _():
            dgain_ref[...] = jnp.zeros_like(dgain_ref)

        @pl.when(s == 0)
        def _():
            da_ref[...] = jnp.zeros_like(da_ref)

        da_ref[...] += _dot_nt(dg_ref[...], wg_ref[...]) + _dot_nt(du_ref[...], wu_ref[...])

        @pl.when(s == N_CHIPS - 1)
        def _():
            dx, dgain = _rms_bwd(h_ref[...], g_ref[...], da_ref[...])
            dgain_ref[...] += dgain
            dh_ref[...] = jnp.where(_row_mask(t, tm, pad, (tm, D)), dy_ref[...] + dx, 0.0)

    row = pl.BlockSpec((tm, D), lambda t, s: (t, 0))
    col = pl.BlockSpec((tm, Fs), lambda t, s: (t, s))
    wcol = pl.BlockSpec((None, D, Fs), lambda t, s: (s, 0, 0))
    return _call(
        body, name=name, grid=(T // tm, N_CHIPS),
        in_specs=[row, row, pl.BlockSpec((None, 1, D), lambda t, s: (layer, 0, 0)), col, col, wcol, wcol],
        out_specs=[row, pl.BlockSpec((1, D), lambda t, s: (0, 0))],
        out_shape=[jax.ShapeDtypeStruct((T, D), F32), jax.ShapeDtypeStruct((1, D), F32)],
        scratch=[pltpu.VMEM((tm, D), F32)],
        sem=("arbitrary", "arbitrary"), operands=(dy, h, gain, dg, du, wg, wu), rider=rider)


def _grad_tn(a, b, mode, scale, tm, name, rider=None):
    T = a.shape[0]
    if mode == "col":
        per, R, C = 1, a.shape[1], b.shape[1] // N_CHIPS
        a_spec = pl.BlockSpec((tm, R), lambda s, t: (t, 0))
        b_spec = pl.BlockSpec((tm, C), lambda s, t: (t, s))
    else:
        per, R, C = 2, a.shape[1] // N_CHIPS, b.shape[1]
        a_spec = pl.BlockSpec((tm, per * R), lambda s, t: (t, s))
        b_spec = pl.BlockSpec((tm, C), lambda s, t: (t, 0))
    nt = T // tm

    def body(a_ref, b_ref, o_ref, acc_ref):
        t = pl.program_id(1)

        @pl.when(t == 0)
        def _():
            acc_ref[...] = jnp.zeros_like(acc_ref)

        acc_ref[...] += _dot_tn(a_ref[...].astype(BF16), b_ref[...].astype(BF16))

        @pl.when(t == nt - 1)
        def _():
            o_ref[...] = (scale * acc_ref[...]).astype(BF16).reshape(per, R, C)

    return _call(
        body, name=name, grid=(N_CHIPS // per, nt),
        in_specs=[a_spec, b_spec],
        out_specs=[pl.BlockSpec((per, R, C), lambda s, t: (s, 0, 0))],
        out_shape=[jax.ShapeDtypeStruct((N_CHIPS, R, C), BF16)],
        scratch=[pltpu.VMEM((per * R, C), F32)],
        sem=("parallel", "arbitrary"), operands=(a, b), rider=rider)[0]


def _mix_bwd_dx(dh, z, ret, pool, wout, wru, wpu, tm, name, rider=None):
    T, D = dh.shape
    Dq = D // N_CHIPS
    nb = D // RET_WIDTH

    def body(*refs):
        dh_ref = refs[0]
        gate_refs = refs[1:1 + 2 * nb]
        ret_ref, pool_ref, wout_ref, wru_ref, wpu_ref, dgab_ref, dret_ref, dpool_ref, dr_ref, dpm_ref = refs[1 + 2 * nb:]
        dmixed = _dot_nt(dh_ref[...].astype(BF16), wout_ref[...].reshape(D, D))
        ga, gb = _load_gates(gate_refs, nb)
        sa = _sigmoid(ga)
        sb = _sigmoid(gb)
        dgab_ref[:, :D] = (dmixed * ret_ref[...].astype(F32) * (sa * (1.0 - sa))).astype(BF16)
        dgab_ref[:, D:] = (dmixed * pool_ref[...].astype(F32) * (sb * (1.0 - sb))).astype(BF16)
        dret = (dmixed * sa).astype(BF16)
        dpool = (dmixed * sb).astype(BF16)
        dret_ref[...] = dret
        dpool_ref[...] = dpool
        dr = _dot_nt(dret[:, :Dq], wru_ref[0])
        dpm = _dot_nt(dpool[:, :Dq], wpu_ref[0])
        for s in range(1, N_CHIPS):
            dr += _dot_nt(dret[:, s * Dq:(s + 1) * Dq], wru_ref[s])
            dpm += _dot_nt(dpool[:, s * Dq:(s + 1) * Dq], wpu_ref[s])
        dr_ref[...] = dr
        dpm_ref[...] = dpm

    row = pl.BlockSpec((tm, D), lambda t: (t, 0))
    half = pl.BlockSpec((tm, RET_WIDTH), lambda t: (t, 0))
    up = pl.BlockSpec((N_CHIPS, RET_WIDTH, Dq), lambda t: (0, 0, 0))
    return _call(
        body, name=name, grid=(T // tm,),
        in_specs=[row] + _gate_specs(tm, D) + [row, row, pl.BlockSpec((N_CHIPS, Dq, D), lambda t: (0, 0, 0)), up, up],
        out_specs=[pl.BlockSpec((tm, 2 * D), lambda t: (t, 0)), row, row, half, half],
        out_shape=[jax.ShapeDtypeStruct((T, 2 * D), BF16), jax.ShapeDtypeStruct((T, D), BF16),
                   jax.ShapeDtypeStruct((T, D), BF16), jax.ShapeDtypeStruct((T, RET_WIDTH), F32),
                   jax.ShapeDtypeStruct((T, POOL_WIDTH), F32)],
        sem=("parallel",), operands=(dh, *([z] * (2 * nb)), ret, pool, wout, wru, wpu), rider=rider)


def _pool_bwd(z, dpm, maps, scale, layer, pad, name):
    T = z.shape[0]

    def body(zu, maps_ref, sc_ref, dpm_ref, du_ref, dmaps_ref, dsc_ref):
        g = pl.program_id(0)
        u = zu[...]
        pooled, div, valid = _pool_parts(u, g, T, pad)
        pb = pooled.astype(BF16)
        mb = maps_ref[...].astype(BF16)
        dp = dpm_ref[...]
        dsc_ref[...] = jnp.sum(dp * _dot(pb, mb), axis=0, keepdims=True)
        dyb = (dp * sc_ref[...]).astype(BF16)
        dmaps_ref[...] = _dot_tn(pb, dyb)
        dpooled = jnp.where(valid, _dot_nt(dyb, mb), 0.0)
        ahead = _select_group(_window_sums(dpooled / div, lambda k: T - k), g)
        du_ref[...] = jnp.where(valid, ahead - dpooled, 0.0).astype(BF16)

    blk = pl.BlockSpec((T, HEAD_DIM), lambda g: (0, g))
    return _call(
        body, name=name, grid=(POOL_GROUPS,),
        in_specs=_pool_specs(T, layer) + [blk],
        out_specs=[blk, pl.BlockSpec((None, HEAD_DIM, HEAD_DIM), lambda g: (g, 0, 0)),
                   pl.BlockSpec((1, HEAD_DIM), lambda g: (0, g))],
        out_shape=[jax.ShapeDtypeStruct((T, POOL_WIDTH), BF16),
                   jax.ShapeDtypeStruct((POOL_GROUPS, HEAD_DIM, HEAD_DIM), F32),
                   jax.ShapeDtypeStruct((1, POOL_WIDTH), F32)],
        sem=("parallel",), operands=(z, maps, scale, dpm))


def _ret_bwd_local(z, o_pre, s_all, dr, consts, cg, name):
    T = z.shape[0]
    N = T // CHUNK
    ng = N // cg
    tg = cg * CHUNK
    cosf, sinf, intra, _, qdec, _ = consts
    fwd = lambda g: g

    def body(zq, zk, zv, zg, o_ref, s_ref, dr_ref, cos_ref, sin_ref, m_ref, qd_ref,
             dq_ref, dg_ref, dk_ref, dv_ref, ds_ref):
        cosv = cos_ref[...]
        sinv = sin_ref[...]
        scale = HEAD_DIM ** -0.5
        q3 = (_rot(zq[...], cosv, sinv) * scale).reshape(cg, CHUNK, HEAD_DIM)
        k3 = _rot(zk[...], cosv, sinv).reshape(cg, CHUNK, HEAD_DIM)
        qb = q3.astype(BF16)
        kb = k3.astype(BF16)
        vb = zv[...].reshape(cg, CHUNK, HEAD_DIM).astype(BF16)
        mask = m_ref[...][None]
        sb = (_ein("ncd,nmd->ncm", qb, kb) * mask).astype(BF16)
        qdv = qd_ref[...][None]
        qdb = (q3 * qdv).astype(BF16)

        out = o_ref[...]
        xc = out - jnp.mean(out, axis=-1, keepdims=True)
        rstd = lax.rsqrt(jnp.mean(xc * xc, axis=-1, keepdims=True) + EPS)
        rn = xc * rstd
        g = zg[...]
        sg = _sigmoid(g)
        drv = dr_ref[...]
        dg_ref[...] = (drv * rn * (sg * (1.0 + g * (1.0 - sg)))).astype(BF16)
        drn = drv * (g * sg)
        dout = rstd * (drn - jnp.mean(drn, axis=-1, keepdims=True)
                       - rn * jnp.mean(drn * rn, axis=-1, keepdims=True))
        dob = dout.reshape(cg, CHUNK, HEAD_DIM).astype(BF16)

        dsb = (_ein("ncd,nmd->ncm", dob, vb) * mask).astype(BF16)
        dv_ref[...] = _ein("ncm,ncd->nmd", sb, dob).reshape(tg, HEAD_DIM)
        dk_ref[...] = _ein("ncm,ncd->nmd", dsb, qb).reshape(tg, HEAD_DIM)
        dq3 = _ein("ncm,nmd->ncd", dsb, kb) + _ein("nce,nde->ncd", dob, s_ref[...].astype(BF16)) * qdv
        dq_ref[...] = _rot_t(dq3.reshape(tg, HEAD_DIM) * scale, cosv, sinv).astype(BF16)
        ds_ref[...] = _ein("ncd,nce->nde", qdb, dob)

    tab = pl.BlockSpec((tg, HEAD_DIM), lambda h, g: (g, 0))
    per_head = pl.BlockSpec((None, CHUNK, HEAD_DIM), lambda h, g: (h, 0, 0))
    head_blk = pl.BlockSpec((tg, HEAD_DIM), lambda h, g: (g, h))
    state_blk = pl.BlockSpec((None, cg, HEAD_DIM, HEAD_DIM), lambda h, g: (h, g, 0, 0))
    return _call(
        body, name=name, grid=(RET_HEADS, ng),
        in_specs=[_head_specs(tg, i, fwd) for i in range(4)]
        + [head_blk, state_blk, head_blk, tab, tab, per_head, per_head],
        out_specs=[head_blk, head_blk, head_blk, head_blk, state_blk],
        out_shape=[jax.ShapeDtypeStruct((T, RET_WIDTH), BF16), jax.ShapeDtypeStruct((T, RET_WIDTH), BF16),
                   jax.ShapeDtypeStruct((T, RET_WIDTH), F32), jax.ShapeDtypeStruct((T, RET_WIDTH), F32),
                   jax.ShapeDtypeStruct((RET_HEADS, N, HEAD_DIM, HEAD_DIM), F32)],
        sem=("parallel", "parallel"), operands=(z, z, z, z, o_pre, s_all, dr, cosf, sinf, intra, qdec))


def _ret_bwd_state(z, dkp, dvp, ds, consts, cg, name):
    T = z.shape[0]
    N = T // CHUNK
    ng = N // cg
    tg = cg * CHUNK
    cosf, sinf, _, kdec, _, cdb = consts
    rev = lambda g: ng - 1 - g

    def body(zk, zv, dkp_ref, dvp_ref, ds_ref, cos_ref, sin_ref, kd_ref, cd_ref, dk_ref, dv_ref, gs_ref, dkv_ref):
        @pl.when(pl.program_id(1) == 0)
        def _():
            gs_ref[...] = jnp.zeros_like(gs_ref)

        cosv = cos_ref[...]
        sinv = sin_ref[...]
        cd = cd_ref[0:1, :]
        grad = gs_ref[...]
        for n in reversed(range(cg)):
            dkv_ref[n] = grad
            grad = ds_ref[n] + cd * grad
        gs_ref[...] = grad
        dkvb = dkv_ref[...].astype(BF16)
        kdv = kd_ref[...][None]
        k3 = _rot(zk[...], cosv, sinv).reshape(cg, CHUNK, HEAD_DIM)
        vb = zv[...].reshape(cg, CHUNK, HEAD_DIM).astype(BF16)
        dk3 = _ein("nce,nde->ncd", vb, dkvb) * kdv
        dv3 = _ein("ncd,nde->nce", (k3 * kdv).astype(BF16), dkvb)
        dk_ref[...] = _rot_t(dkp_ref[...] + dk3.reshape(tg, HEAD_DIM), cosv, sinv).astype(BF16)
        dv_ref[...] = (dvp_ref[...] + dv3.reshape(tg, HEAD_DIM)).astype(BF16)

    tab = pl.BlockSpec((tg, HEAD_DIM), lambda h, g: (rev(g), 0))
    head_blk = pl.BlockSpec((tg, HEAD_DIM), lambda h, g: (rev(g), h))
    return _call(
        body, name=name, grid=(RET_HEADS, ng),
        in_specs=[_head_specs(tg, 1, rev), _head_specs(tg, 2, rev), head_blk, head_blk,
                  pl.BlockSpec((None, cg, HEAD_DIM, HEAD_DIM), lambda h, g: (h, rev(g), 0, 0)),
                  tab, tab,
                  pl.BlockSpec((None, CHUNK, HEAD_DIM), lambda h, g: (h, 0, 0)),
                  pl.BlockSpec((None, 8, HEAD_DIM), lambda h, g: (h, 0, 0))],
        out_specs=[head_blk, head_blk],
        out_shape=[jax.ShapeDtypeStruct((T, RET_WIDTH), BF16)] * 2,
        scratch=[pltpu.VMEM((HEAD_DIM, HEAD_DIM), F32), pltpu.VMEM((cg, HEAD_DIM, HEAD_DIM), F32)],
        sem=("parallel", "arbitrary"), operands=(z, z, dkp, dvp, ds, cosf, sinf, kdec, cdb))


def _inproj_bwd_dx(dz, win, h, gain, dh_in, layer, tm, pad, name, rider=None):
    T, D = h.shape
    Ns = win.shape[-1]

    def body(dz_ref, w_ref, h_ref, g_ref, dhi_ref, dh_ref, dgain_ref, db_ref):
        t = pl.program_id(0)
        s = pl.program_id(1)

        @pl.when((t == 0) & (s == 0))
        def _():
            dgain_ref[...] = jnp.zeros_like(dgain_ref)

        @pl.when(s == 0)
        def _():
            db_ref[...] = jnp.zeros_like(db_ref)

        db_ref[...] += _dot_nt(dz_ref[...], w_ref[...])

        @pl.when(s == N_CHIPS - 1)
        def _():
            dx, dgain = _rms_bwd(h_ref[...], g_ref[...], db_ref[...])
            dgain_ref[...] += dgain
            dh_ref[...] = jnp.where(_row_mask(t, tm, pad, (tm, D)), dhi_ref[...] + dx, 0.0)

    row = pl.BlockSpec((tm, D), lambda t, s: (t, 0))
    return _call(
        body, name=name, grid=(T // tm, N_CHIPS),
        in_specs=[pl.BlockSpec((tm, Ns), lambda t, s: (t, s)),
                  pl.BlockSpec((None, D, Ns), lambda t, s: (s, 0, 0)),
                  row, pl.BlockSpec((None, 1, D), lambda t, s: (layer, 0, 0)), row],
        out_specs=[row, pl.BlockSpec((1, D), lambda t, s: (0, 0))],
        out_shape=[jax.ShapeDtypeStruct((T, D), F32), jax.ShapeDtypeStruct((1, D), F32)],
        scratch=[pltpu.VMEM((tm, D), F32)],
        sem=("arbitrary", "arbitrary"), operands=(dz, win, h, gain, dh_in), rider=rider)


def _sum_pair(gs, rs, c_idx, name):
    n = len(gs)

    def body(c_ref, *refs):
        for g_ref, r_ref, o_ref in zip(refs[:n], refs[n:2 * n], refs[2 * n:]):
            o_ref[...] = (g_ref[...].astype(F32) + r_ref[...].astype(F32)).astype(BF16)

    halves = [pl.BlockSpec((None,) + r.shape[1:], lambda s, c_ref: (s, 0, 0)) for r in rs]
    return pl.pallas_call(
        body,
        name=name,
        grid_spec=pltpu.PrefetchScalarGridSpec(
            num_scalar_prefetch=1,
            grid=(N_CHIPS,),
            in_specs=[pl.BlockSpec((None,) + r.shape[1:], lambda s, c_ref: (s, c_ref[0], 0)) for r in rs] + halves,
            out_specs=halves,
        ),
        out_shape=[jax.ShapeDtypeStruct(r.shape, BF16) for r in rs],
        compiler_params=_params(("parallel",)),
    )(c_idx, *gs, *rs)


def _sum_chips(ps, rs, pos, name):
    n = len(ps)
    quarters = 4

    def body(pos_ref, *refs):
        chip = pos_ref[0]
        for p_ref, r_ref, o_ref in zip(refs[:n], refs[n:2 * n], refs[2 * n:]):
            own = p_ref[...].astype(F32)
            terms = [jnp.where(chip == k, own, r_ref[k].astype(F32)) for k in range(N_CHIPS)]
            o_ref[...] = ((terms[0] + terms[1]) + terms[2]) + terms[3]

    def rows(r):
        assert r.shape[1] % (quarters * BF16_ROWS) == 0, r.shape
        return r.shape[1] // quarters

    return pl.pallas_call(
        body,
        name=name,
        grid_spec=pltpu.PrefetchScalarGridSpec(
            num_scalar_prefetch=1,
            grid=(quarters,),
            in_specs=[pl.BlockSpec((None, rows(r), r.shape[2]), lambda q, pos_ref: (pos_ref[0], q, 0)) for r in rs]
            + [pl.BlockSpec((N_CHIPS, rows(r), r.shape[2]), lambda q, pos_ref: (0, q, 0)) for r in rs],
            out_specs=[pl.BlockSpec((rows(r), r.shape[2]), lambda q, pos_ref: (pos_ref[1] * quarters + q, 0))
                       for r in rs],
        ),
        out_shape=[jax.ShapeDtypeStruct((2 * r.shape[1], r.shape[2]), F32) for r in rs],
        compiler_params=_params(("arbitrary",)),
    )(pos, *ps, *rs)


def _small_all_reduce(p):
    rows, width = p.shape

    def body(p_ref, o_ref, sib_ref, slot_ref, ssem, rsem):
        x, y, c, chip, others = _mesh_pos()
        pair = _remote(p_ref, sib_ref, ssem.at[0], rsem.at[0], (x, y, 1 - c))
        pair.start()
        pair.wait()
        slot_ref[chip] = p_ref[...] + sib_ref[...]
        sends = []
        for j, (ox, oy) in enumerate(others):
            cp = _remote(slot_ref.at[chip], slot_ref.at[chip], ssem.at[1 + j], rsem.at[1 + j], (ox, oy, c))
            cp.start()
            sends.append(cp)
        for j, (ox, oy) in enumerate(others):
            slot = slot_ref.at[2 * ox + oy]
            _remote(slot, slot, ssem.at[1 + j], rsem.at[1 + j], (ox, oy, c)).wait_recv()
        for cp in sends:
            cp.wait_send()
        o_ref[...] = ((slot_ref[0] + slot_ref[1]) + slot_ref[2]) + slot_ref[3]

    vmem = pl.BlockSpec(memory_space=pltpu.VMEM)
    return pl.pallas_call(
        body,
        name="small_grads_all_reduce",
        in_specs=[vmem],
        out_specs=vmem,
        out_shape=jax.ShapeDtypeStruct(p.shape, F32),
        scratch_shapes=[pltpu.VMEM((rows, width), F32), pltpu.VMEM((N_CHIPS, rows, width), F32),
                        pltpu.SemaphoreType.DMA((4,)), pltpu.SemaphoreType.DMA((4,))],
    )(p)


def _adamw(gs, w, m, v, name):
    L, R, C = w.shape
    Ct = gs[0].shape[1]
    tr = _pick_tile(R, 256, 8)

    def body(*refs):
        g_refs = refs[:L]
        w_ref, m_ref, v_ref, go_ref, d_ref, mo_ref, vo_ref = refs[L:]
        layer = pl.program_id(0)
        grad = g_refs[L - 1][...]
        for i in range(L - 2, -1, -1):
            grad = jnp.where(layer == i, g_refs[i][...], grad)
        if Ct != C:
            grad = grad[:, :C]
        m_new = ADAM_B1 * m_ref[...] + (1.0 - ADAM_B1) * grad
        v_new = ADAM_B2 * v_ref[...] + (1.0 - ADAM_B2) * jnp.square(grad)
        m_hat = m_new / (1.0 - ADAM_B1 ** ADAM_STEP)
        v_hat = v_new / (1.0 - ADAM_B2 ** ADAM_STEP)
        go_ref[...] = grad
        d_ref[...] = -ADAM_LR * (m_hat / (jnp.sqrt(v_hat) + ADAM_EPS) + ADAM_WD * w_ref[...])
        mo_ref[...] = m_new
        vo_ref[...] = v_new

    g_specs = [pl.BlockSpec((tr, Ct), functools.partial(lambda l, r, i: (jnp.where(l == i, r, 0), 0), i=i))
               for i in range(L)]
    blk = pl.BlockSpec((None, tr, C), lambda l, r: (l, r, 0))
    return pl.pallas_call(
        body,
        name=name,
        grid=(L, R // tr),
        in_specs=g_specs + [blk, blk, blk],
        out_specs=[blk] * 4,
        out_shape=[jax.ShapeDtypeStruct((L, R, C), F32)] * 4,
        compiler_params=_params(("arbitrary", "arbitrary")),
    )(*gs, w, m, v)


_FFN1 = ("ffn1_gate", "ffn1_up", "ffn1_down")
_FFN2 = ("ffn2_gate", "ffn2_up", "ffn2_down")
_MIXW = ("w_ret_up", "w_pool_up", "w_out")
_BIG = _FFN1 + ("w_in",) + _MIXW + _FFN2
_TRANSPOSED = ("ffn1_gate", "ffn1_up", "ffn2_gate", "ffn2_up")
_SMALL = ("ffn1_norm", "mix_norm", "ffn2_norm", "final_norm", "pool_scale", "pool_maps")
_ORDER = ("meta", "ffn1_norm", "ffn1_gate", "ffn1_up", "ffn1_down", "mix_norm", "w_in", "pool_maps",
          "pool_scale", "w_ret_up", "w_pool_up", "w_out", "ffn2_norm", "ffn2_gate", "ffn2_up", "ffn2_down",
          "final_norm")


def _transport(a):
    n, r, c = a.shape
    out = a.astype(BF16)
    if c % LANES:
        out = jnp.concatenate([out, jnp.zeros((n, r, _round_up(c, LANES) - c), BF16)], axis=2)
    if r % LANES:
        out = jnp.concatenate([out, jnp.zeros((n, _round_up(r, LANES) - r, out.shape[2]), BF16)], axis=1)
    return out


def _pack_rows(parts, width):
    rows = [p.reshape(-1, width) for p in parts]
    total = sum(r.shape[0] for r in rows)
    fill = _round_up(total, 8) - total
    if fill:
        rows.append(jnp.zeros((fill, width), F32))
    return jnp.concatenate(rows, axis=0)


def _unpack_rows(packed, shapes, width):
    out, at = [], 0
    for shp in shapes:
        n = math.prod(shp) // width
        out.append(packed[at:at + n].reshape(shp))
        at += n
    return out


class _Weights:
    def __init__(self, shards):
        self.shards = shards
        self.full = {}

    def rider(self, keys):
        r = _gather_rider([(self.shards[n], i) for n, i in keys])
        r.keys = keys
        return r

    def take(self, rider):
        for key, arr in zip(rider.keys, rider.results):
            self.full[key] = arr

    def __call__(self, name, layer):
        return self.full[(name, layer)]


def _local_step(x, meta_full, tgt, w, wts, pad, tm, cg, reducer):
    D = x.shape[1]
    T = pad + N_META + x.shape[0]
    L = w["ffn1_norm"].shape[0]
    pool_maps = w["pool_maps"]
    gains = {n: w[n].reshape(L, 1, D) for n in ("ffn1_norm", "mix_norm", "ffn2_norm")}
    scale3 = w["pool_scale"].reshape(L, 1, POOL_WIDTH)
    consts = _ret_consts(T, pad)
    tl = _pick_tile(T, 2 * tm, BF16_ROWS)
    def gather(keys):
        return wts.rider(keys) if keys and keys[0] not in wts.full else None

    def done(rider):
        if rider is not None:
            wts.take(rider)

    h = jnp.concatenate([jnp.zeros((pad, D), F32), meta_full, x], axis=0)
    saved = []
    for i in range(L):
        s = {"h0": h}
        rd = gather([("w_in", i)] + [(n, i) for n in _MIXW])
        h, s["a1"], s["g1"], s["u1"], s["act1"] = _ffn_fwd(
            h, gains["ffn1_norm"], wts("ffn1_gate", i), wts("ffn1_up", i), wts("ffn1_down", i), i, tl,
            f"ffn1_fwd_{i}", rd)
        done(rd)
        s["h1"] = h
        rd = gather([("ffn2_gate", i), ("ffn2_up", i)])
        s["z"], s["b"] = _inproj_fwd(h, gains["mix_norm"], wts("w_in", i), i, tl, f"inproj_fwd_{i}", rd)
        done(rd)
        s["r"], s["o_pre"], s["s_all"] = _ret_fwd(s["z"], consts, cg, f"retention_fwd_{i}")
        s["pm"] = _pool_fwd(s["z"], pool_maps, scale3, i, pad, f"pool_fwd_{i}")
        rd = gather([("ffn2_down", i)])
        h, s["mixed"], s["ret"], s["pool"] = _mix_fwd(
            h, s["r"], s["pm"], s["z"], wts("w_ret_up", i), wts("w_pool_up", i), wts("w_out", i), tm,
            f"mix_fwd_{i}", rd)
        done(rd)
        s["h2"] = h
        rd = gather([(n, i + 1) for n in _FFN1]) if i + 1 < L else None
        h, s["a2"], s["g2"], s["u2"], s["act2"] = _ffn_fwd(
            h, gains["ffn2_norm"], wts("ffn2_gate", i), wts("ffn2_up", i), wts("ffn2_down", i), i, tl,
            f"ffn2_fwd_{i}", rd)
        done(rd)
        saved.append(s)

    dh, loss_acc, d_final = _final_loss(h, w["final_norm"].reshape(1, D), tgt, "final_norm_loss")

    small = {n: [None] * L for n in ("ffn1_norm", "mix_norm", "ffn2_norm", "pool_scale", "pool_maps")}

    carry = {"ffn_act": 1.0, "ffn_in": 2.2, "mix_bwd": 1.0, "inproj_bwd": 1.5, "w_in": 1.0}

    tk = _pick_tile(T, 1408, LANES)

    def grad(n, a, b, i, mode):
        rd = reducer.rider(carry.get(n, 1.0 if i == 0 and n.startswith("ffn") else 0.5))
        reducer.add(n, i, _grad_tn(a, b, mode, 1.0, tk, f"grad_{n}_{i}", rd))
        reducer.done(rd)

    def ffn_bwd(which, dy, h_in, g, u, i):
        rd = reducer.rider(carry["ffn_act"])
        dg, du, dyh = _ffn_bwd_act(dy, g, u, wts(f"{which}_down", i), tl, f"{which}_bwd_act_{i}", rd)
        reducer.done(rd)
        rd = reducer.rider(carry["ffn_in"])
        dh_in, dgain = _ffn_bwd_in(dy, h_in, gains[f"{which}_norm"], dg, du, wts(f"{which}_gate", i),
                                   wts(f"{which}_up", i), i, tl, pad, f"{which}_bwd_in_{i}", rd)
        reducer.done(rd)
        return dh_in, dg, du, dgain, dyh

    for i in reversed(range(L)):
        s = saved[i]
        dh, dg, du, small["ffn2_norm"][i], dyh = ffn_bwd("ffn2", dh, s["h2"], s["g2"], s["u2"], i)
        grad("ffn2_gate", dg, s["a2"], i, "row")
        grad("ffn2_up", du, s["a2"], i, "row")
        grad("ffn2_down", s["act2"], dyh, i, "row")
        reducer.stage(f"ffn2_{i}")
        rd = reducer.rider(carry["mix_bwd"])
        dgab, dret, dpool, dr, dpm = _mix_bwd_dx(
            dh, s["z"], s["ret"], s["pool"], wts("w_out", i), wts("w_ret_up", i), wts("w_pool_up", i), tm,
            f"mix_bwd_{i}", rd)
        reducer.done(rd)
        grad("w_out", s["mixed"], dh, i, "row")
        grad("w_ret_up", s["r"], dret, i, "col")
        grad("w_pool_up", s["pm"], dpool, i, "col")
        du_pool, small["pool_maps"][i], small["pool_scale"][i] = _pool_bwd(
            s["z"], dpm, pool_maps, scale3, i, pad, f"pool_bwd_{i}")
        dq, dgr, dkp, dvp, ds = _ret_bwd_local(s["z"], s["o_pre"], s["s_all"], dr, consts, cg, f"retention_bwd_{i}")
        dk, dv = _ret_bwd_state(s["z"], dkp, dvp, ds, consts, cg, f"retention_bwd_state_{i}")
        dz = jnp.concatenate([dq, dk, dv, dgr, du_pool, dgab], axis=1)
        dh2 = dh
        rd = reducer.rider(carry["inproj_bwd"])
        dh, small["mix_norm"][i] = _inproj_bwd_dx(
            dz, wts("w_in", i), s["h1"], gains["mix_norm"], dh2, i, tl, pad, f"inproj_bwd_{i}", rd)
        reducer.done(rd)
        grad("w_in", s["b"], dz, i, "col")
        reducer.stage(f"mid{i}")
        dh, dg, du, small["ffn1_norm"][i], dyh = ffn_bwd("ffn1", dh, s["h0"], s["g1"], s["u1"], i)
        grad("ffn1_gate", dg, s["a1"], i, "row")
        if i == 0:
            reducer.stage("gate0")
        grad("ffn1_up", du, s["a1"], i, "row")
        if i == 0:
            reducer.stage("up0")
        grad("ffn1_down", s["act1"], dyh, i, "row")
        reducer.stage(f"end{i}")

    return loss_acc, dh, small, d_final


class _Reducer:
    def __init__(self, unit):
        self.c_idx = lax.axis_index("c").astype(jnp.int32).reshape(1)
        chip = 2 * lax.axis_index("x") + lax.axis_index("y")
        self.pos = jnp.stack([chip, lax.axis_index("c")]).astype(jnp.int32)
        self.pending, self.stages, self.queue, self.halves = [], [], [], {}
        self.unit = unit
        self.calls = 0

    def add(self, name, layer, g):
        self.pending.append(((name, layer), g))

    def stage(self, tag):
        if self.pending:
            self.stages.append((tag, self.pending))
            self.pending = []

    def _pair_rider(self):
        if not self.stages:
            return None
        tag, items = self.stages.pop(0)
        rd = _pair_exchange_rider([g for _, g in items])
        rd.tag, rd.keys = tag, [k for k, _ in items]
        return rd

    def _chip_rider(self, units):
        take, size = [], 0
        while self.queue and (units is None or size + self.queue[0][1].size <= units * self.unit):
            take.append(self.queue.pop(0))
            size += take[-1][1].size
        if not take:
            return None
        rd = _chip_exchange_rider([p for _, p in take])
        rd.keys = [k for k, _ in take]
        return rd

    def rider(self, units):
        self.riding = (self._pair_rider(), self._chip_rider(units))
        return _join(self.riding)

    def done(self, rd):
        if rd is None:
            return
        _split_results(rd)
        pair, chips = self.riding
        if len([r for r in self.riding if r is not None]) == 1:
            (pair or chips).results = rd.results
        self.calls += 1
        if pair is not None:
            sums = _sum_pair(pair.ins, pair.results, self.c_idx, f"sum_pair_{pair.tag}")
            self.queue += list(zip(pair.keys, sums))
        if chips is not None:
            sums = _sum_chips(chips.ins, chips.results, self.pos, f"sum_chips_{self.calls}")
            self.halves.update(zip(chips.keys, sums))

    def finish(self):
        assert not self.pending
        while self.stages or self.queue:
            self.riding = (self._pair_rider(), self._chip_rider(None))
            rd = _join(self.riding)
            _run_rider(rd, f"grads_exchange_tail_{self.calls}")
            self.done(rd)
        keys = list(self.halves)
        return dict(zip(keys, _pair_gather([self.halves[k] for k in keys])))


def _update(loss, grad_x, d_meta_rows, shard_grads, small, d_final, w, mom, var):
    meta = w["meta"]
    D = w["final_norm"].shape[0]
    L = w["ffn1_norm"].shape[0]
    Dq = D // N_CHIPS

    small_parts = [jnp.concatenate(small[n], axis=0) for n in ("ffn1_norm", "mix_norm", "ffn2_norm")]
    small_parts += [d_final, jnp.concatenate(small["pool_scale"], axis=0), jnp.concatenate(small["pool_maps"], axis=0)]
    reduced = _small_all_reduce(_pack_rows(small_parts + [d_meta_rows], D))
    small_shapes = [w[n].shape for n in _SMALL]
    small_rows = sum(math.prod(shp) for shp in small_shapes) // D
    chip = 2 * lax.axis_index("x") + lax.axis_index("y")
    d_meta = lax.dynamic_slice_in_dim(reduced[small_rows:small_rows + N_META], chip * Dq, Dq, axis=1)

    out = {}
    for n in _BIG:
        gs = [shard_grads[(n, i)] for i in range(L)]
        if n in _TRANSPOSED:
            res = _adamw(gs, *(jnp.swapaxes(t[n], 1, 2) for t in (w, mom, var)), f"adamw_{n}")
            out[n] = [jnp.swapaxes(r, 1, 2) for r in res]
        else:
            out[n] = _adamw(gs, w[n], mom[n], var[n], f"adamw_{n}")
    names = _SMALL + ("meta",)
    packed_g = _pack_rows([reduced[:small_rows], d_meta], D)
    packed = [_pack_rows([t[n] for n in names], D) for t in (w, mom, var)]
    res = _adamw([packed_g], packed[0][None], packed[1][None], packed[2][None], "adamw_small")
    shapes = small_shapes + [meta.shape]
    unpacked = [_unpack_rows(r[0], shapes, D) for r in res]
    for k, n in enumerate(names):
        out[n] = tuple(u[k] for u in unpacked)

    return (loss, grad_x) + tuple(out[n][j] for j in range(4) for n in _ORDER)


def kernel(x, meta, ffn1_norm, ffn1_gate, ffn1_up, ffn1_down, mix_norm, w_in, pool_maps, pool_scale, w_ret_up, w_pool_up, w_out, ffn2_norm, ffn2_gate, ffn2_up, ffn2_down, final_norm, loss_target, m_meta, m_ffn1_norm, m_ffn1_gate, m_ffn1_up, m_ffn1_down, m_mix_norm, m_w_in, m_pool_maps, m_pool_scale, m_w_ret_up, m_w_pool_up, m_w_out, m_ffn2_norm, m_ffn2_gate, m_ffn2_up, m_ffn2_down, m_final_norm, v_meta, v_ffn1_norm, v_ffn1_gate, v_ffn1_up, v_ffn1_down, v_mix_norm, v_w_in, v_pool_maps, v_pool_scale, v_w_ret_up, v_w_pool_up, v_w_out, v_ffn2_norm, v_ffn2_gate, v_ffn2_up, v_ffn2_down, v_final_norm):
    args = dict(locals())
    w = {n: args[n] for n in _ORDER}
    mom = {n: args["m_" + n] for n in _ORDER}
    var = {n: args["v_" + n] for n in _ORDER}

    assert x.shape[0] == 1, "one batch element per device"
    seq, D = x.shape[1], x.shape[2]
    assert seq % CHUNK == 0 and D % RET_WIDTH == 0 and (2 * POOL_WIDTH) % D == 0
    pad = (-(seq + N_META)) % CHUNK
    T = seq + N_META + pad
    tm = _pick_tile(T, 528, BF16_ROWS)
    cg = _pick_tile(T // CHUNK, 11, 1)

    shards = {n: _transport(w[n]) for n in _BIG}
    shards["meta"] = meta[None]
    wts = _Weights(shards)
    head = wts.rider([(n, 0) for n in _FFN1] + [("meta", 0)])
    _run_rider(head, "weights_gather_head")
    wts.take(head)
    meta_full = jnp.transpose(wts("meta", 0), (1, 0, 2)).reshape(N_META, D)

    reducer = _Reducer(unit=2 * shards["ffn1_gate"][0].size)
    loss_acc, dh, small, d_final = _local_step(x[0], meta_full, loss_target[0], w, wts, pad, tm, cg, reducer)
    loss = lax.psum(loss_acc[0, 0], ("x", "y", "c"))
    grad_x = dh[pad + N_META:][None]
    return _update(loss, grad_x, dh[pad:pad + N_META], reducer.finish(), small, d_final, w, mom, var)
```

```python
import functools
import math

import jax
import jax.numpy as jnp
from jax import lax
from jax.experimental import pallas as pl
from jax.experimental.pallas import tpu as pltpu

F32 = jnp.float32
BF16 = jnp.bfloat16

N_META = 16
RET_HEADS = 4
HEAD_DIM = 128
RET_WIDTH = RET_HEADS * HEAD_DIM
POOL_WINDOWS = (2, 4, 8, 16)
POOL_GROUPS = len(POOL_WINDOWS)
POOL_WIDTH = POOL_GROUPS * HEAD_DIM
CHUNK = 128
ROPE_BASE = 10000.0
EPS = 1e-6
ADAM_LR = 0.001
ADAM_B1 = 0.9
ADAM_B2 = 0.999
ADAM_EPS = 1e-08
ADAM_WD = 0.01
ADAM_STEP = 10

N_CHIPS = 4
LANES = 128
BF16_ROWS = 16
V7X_VMEM_LIMIT = 52 * 1024 * 1024
MESH = pl.DeviceIdType.MESH
ANY = pl.BlockSpec(memory_space=pl.ANY)


def _round_up(n, m):
    return -(-n // m) * m


def _pick_tile(n, target, mult):
    best = None
    for d in range(mult, min(n, target) + 1, mult):
        if n % d == 0:
            best = d
    assert best is not None, (n, target, mult)
    return best


def _params(sem=None):
    return pltpu.CompilerParams(dimension_semantics=sem, vmem_limit_bytes=V7X_VMEM_LIMIT)


def _dot(a, b):
    return jnp.dot(a, b, preferred_element_type=F32)


def _dot_nt(a, b):
    return lax.dot_general(a, b, (((1,), (1,)), ((), ())), preferred_element_type=F32)


def _dot_tn(a, b):
    return lax.dot_general(a, b, (((0,), (0,)), ((), ())), preferred_element_type=F32)


def _ein(spec, a, b):
    return jnp.einsum(spec, a, b, preferred_element_type=F32)


def _sigmoid(x):
    return jax.nn.sigmoid(x)


def _rms_fwd(x, gain):
    r = lax.rsqrt(jnp.mean(x * x, axis=-1, keepdims=True) + EPS)
    return x * r * gain


def _rms_bwd(x, gain, da):
    r = lax.rsqrt(jnp.mean(x * x, axis=-1, keepdims=True) + EPS)
    xh = x * r
    dgain = jnp.sum(da * xh, axis=0, keepdims=True)
    dxh = da * gain
    dx = r * (dxh - xh * jnp.mean(dxh * xh, axis=-1, keepdims=True))
    return dx, dgain


def _row_mask(t, tm, pad, shape):
    rows = t * tm + lax.broadcasted_iota(jnp.int32, shape, 0)
    return rows >= pad


def _mesh_pos():
    x, y, c = lax.axis_index("x"), lax.axis_index("y"), lax.axis_index("c")
    others = [(1 - x, y), (x, 1 - y), (1 - x, 1 - y)]
    return x, y, c, 2 * x + y, others


def _half_rows(c, rh):
    return pl.ds(pl.multiple_of(c * rh, rh), rh)


def _remote(src, dst, ssem, rsem, dev):
    return pltpu.make_async_remote_copy(src_ref=src, dst_ref=dst, send_sem=ssem, recv_sem=rsem,
                                        device_id=dev, device_id_type=MESH)


class _Rider:
    def __init__(self, ins, out_shapes, n_sem, start, finish):
        self.ins, self.out_shapes, self.n_sem, self.start, self.finish = ins, out_shapes, n_sem, start, finish
        self.results = None


class _SemWindow:
    def __init__(self, ref, base):
        self.ref, self.base = ref, base

    @property
    def at(self):
        return self

    def __getitem__(self, k):
        return self.ref.at[self.base + k]


def _join(riders):
    riders = [r for r in riders if r is not None]
    if len(riders) <= 1:
        return riders[0] if riders else None

    def run(which):
        def go(ins, outs, ssem, rsem):
            at, sem = 0, 0
            for r in riders:
                n = len(r.ins)
                getattr(r, which)(ins[at:at + n], outs[at:at + n], _SemWindow(ssem, sem), _SemWindow(rsem, sem))
                at, sem = at + n, sem + r.n_sem
        return go

    joined = _Rider(sum([list(r.ins) for r in riders], []), sum([list(r.out_shapes) for r in riders], []),
                    sum(r.n_sem for r in riders), run("start"), run("finish"))
    joined.parts = riders
    return joined


def _split_results(rider):
    at = 0
    for r in getattr(rider, "parts", []):
        r.results = rider.results[at:at + len(r.ins)]
        at += len(r.ins)


def _gather_rider(pieces):
    per = 7
    layers = [layer for _, layer in pieces]

    def first_copies(ins, outs, ssem, rsem):
        x, y, c, chip, others = _mesh_pos()
        copies = []
        for i, layer in enumerate(layers):
            mine = _half_rows(c, ins[i].shape[1] // 2)
            for j, (ox, oy) in enumerate(others):
                copies.append(_remote(ins[i].at[layer, mine, :], outs[i].at[chip, mine, :],
                                      ssem.at[per * i + j], rsem.at[per * i + j], (ox, oy, c)))
            copies.append(_remote(ins[i].at[layer], outs[i].at[chip],
                                  ssem.at[per * i + 6], rsem.at[per * i + 6], (x, y, 1 - c)))
        return copies

    def start(ins, outs, ssem, rsem):
        for cp in first_copies(ins, outs, ssem, rsem):
            cp.start()

    def finish(ins, outs, ssem, rsem):
        x, y, c, chip, others = _mesh_pos()
        sibling = (x, y, 1 - c)
        forwards = []
        for i in range(len(layers)):
            mine = _half_rows(c, ins[i].shape[1] // 2)
            for j, (ox, oy) in enumerate(others):
                rows = outs[i].at[2 * ox + oy, mine, :]
                _remote(rows, rows, ssem.at[per * i + j], rsem.at[per * i + j], (ox, oy, c)).wait_recv()
                fwd = _remote(rows, rows, ssem.at[per * i + 3 + j], rsem.at[per * i + 3 + j], sibling)
                fwd.start()
                forwards.append(fwd)
        for i in range(len(layers)):
            theirs = _half_rows(1 - c, ins[i].shape[1] // 2)
            for j, (ox, oy) in enumerate(others):
                rows = outs[i].at[2 * ox + oy, theirs, :]
                _remote(rows, rows, ssem.at[per * i + 3 + j], rsem.at[per * i + 3 + j], sibling).wait_recv()
            own = outs[i].at[chip]
            _remote(own, own, ssem.at[per * i + 6], rsem.at[per * i + 6], sibling).wait_recv()
        for cp in first_copies(ins, outs, ssem, rsem) + forwards:
            cp.wait_send()

    shapes = [jax.ShapeDtypeStruct((N_CHIPS,) + s.shape[1:], s.dtype) for s, _ in pieces]
    return _Rider([s for s, _ in pieces], shapes, per * len(pieces), start, finish)


def _chip_exchange_rider(ps):
    def copies(ins, outs, ssem, rsem):
        x, y, c, chip, others = _mesh_pos()
        return [_remote(ins[i].at[2 * ox + oy], outs[i].at[chip], ssem.at[3 * i + j], rsem.at[3 * i + j], (ox, oy, c))
                for i in range(len(ps)) for j, (ox, oy) in enumerate(others)]

    def start(ins, outs, ssem, rsem):
        for cp in copies(ins, outs, ssem, rsem):
            cp.start()

    def finish(ins, outs, ssem, rsem):
        x, y, c, chip, others = _mesh_pos()
        for i in range(len(ps)):
            for j, (ox, oy) in enumerate(others):
                slot = outs[i].at[2 * ox + oy]
                _remote(slot, slot, ssem.at[3 * i + j], rsem.at[3 * i + j], (ox, oy, c)).wait_recv()
        for cp in copies(ins, outs, ssem, rsem):
            cp.wait_send()

    return _Rider(list(ps), [jax.ShapeDtypeStruct(p.shape, p.dtype) for p in ps], 3 * len(ps), start, finish)


def _pair_exchange_rider(gs):
    def copies(ins, outs, ssem, rsem):
        x, y, c, _, _ = _mesh_pos()
        return [_remote(ins[i].at[:, _half_rows(1 - c, ins[i].shape[1] // 2), :], outs[i],
                        ssem.at[i], rsem.at[i], (x, y, 1 - c)) for i in range(len(gs))]

    def start(ins, outs, ssem, rsem):
        for cp in copies(ins, outs, ssem, rsem):
            cp.start()

    def finish(ins, outs, ssem, rsem):
        for cp in copies(ins, outs, ssem, rsem):
            cp.wait()

    shapes = [jax.ShapeDtypeStruct((g.shape[0], g.shape[1] // 2, g.shape[2]), g.dtype) for g in gs]
    return _Rider(list(gs), shapes, len(gs), start, finish)


def _run_rider(rider, name):
    def body(*refs):
        n = len(rider.ins)
        ins, outs = refs[:n], refs[n:2 * n]
        ssem, rsem = refs[2 * n:]
        rider.start(ins, outs, ssem, rsem)
        rider.finish(ins, outs, ssem, rsem)

    rider.results = pl.pallas_call(
        body,
        name=name,
        in_specs=[ANY] * len(rider.ins),
        out_specs=[ANY] * len(rider.ins),
        out_shape=rider.out_shapes,
        scratch_shapes=[pltpu.SemaphoreType.DMA((rider.n_sem,)), pltpu.SemaphoreType.DMA((rider.n_sem,))],
    )(*rider.ins)
    return rider.results


def _pair_gather(fs):
    n = len(fs)

    def body(*refs):
        bufs = refs[n:2 * n]
        ssem, rsem = refs[2 * n:]
        x, y, c, _, _ = _mesh_pos()
        sends = []
        for i in range(n):
            rh = bufs[i].shape[0] // 2
            mine = bufs[i].at[_half_rows(c, rh), :]
            cp = _remote(mine, mine, ssem.at[i], rsem.at[i], (x, y, 1 - c))
            cp.start()
            sends.append(cp)
        for i in range(n):
            rh = bufs[i].shape[0] // 2
            theirs = bufs[i].at[_half_rows(1 - c, rh), :]
            _remote(theirs, theirs, ssem.at[i], rsem.at[i], (x, y, 1 - c)).wait_recv()
        for cp in sends:
            cp.wait_send()

    return pl.pallas_call(
        body,
        name="grads_pair_gather",
        in_specs=[ANY] * n,
        out_specs=[ANY] * n,
        out_shape=[jax.ShapeDtypeStruct(f.shape, f.dtype) for f in fs],
        input_output_aliases={i: i for i in range(n)},
        scratch_shapes=[pltpu.SemaphoreType.DMA((n,)), pltpu.SemaphoreType.DMA((n,))],
    )(*fs)


def _call(body, *, name, grid, in_specs, out_specs, out_shape, operands, scratch=(), sem=None, rider=None):
    if rider is None:
        return pl.pallas_call(
            body, name=name, grid=grid, in_specs=in_specs, out_specs=out_specs, out_shape=out_shape,
            scratch_shapes=list(scratch), compiler_params=_params(sem))(*operands)
    n_in, n_out, n_sc, r = len(in_specs), len(out_specs), len(scratch), len(rider.ins)

    def carrying(*refs):
        a, b = n_in, n_in + r
        c, d = b + n_out, b + n_out + r
        e = d + n_sc
        ids = [pl.program_id(k) for k in range(len(grid))]
        first = functools.reduce(jnp.logical_and, [i == 0 for i in ids])
        last = functools.reduce(jnp.logical_and, [i == g - 1 for i, g in zip(ids, grid)])

        @pl.when(first)
        def _():
            rider.start(refs[a:b], refs[c:d], refs[e], refs[e + 1])

        body(*refs[:a], *refs[b:c], *refs[d:e])

        @pl.when(last)
        def _():
            rider.finish(refs[a:b], refs[c:d], refs[e], refs[e + 1])

    outs = pl.pallas_call(
        carrying, name=name, grid=grid,
        in_specs=list(in_specs) + [ANY] * r,
        out_specs=list(out_specs) + [ANY] * r,
        out_shape=list(out_shape) + list(rider.out_shapes),
        scratch_shapes=list(scratch) + [pltpu.SemaphoreType.DMA((rider.n_sem,)), pltpu.SemaphoreType.DMA((rider.n_sem,))],
        compiler_params=_params(("arbitrary",) * len(grid)),
    )(*operands, *rider.ins)
    rider.results = outs[n_out:]
    return outs[:n_out]


def _ffn_fwd(h, gain, wg, wu, wd, layer, tm, name, rider=None):
    T, D = h.shape
    Fs = wg.shape[-1]
    F = N_CHIPS * Fs

    def body(h_ref, g_ref, wg_ref, wu_ref, wd_ref, ho_ref, a_ref, go_ref, uo_ref, act_ref, acc_ref):
        s = pl.program_id(1)

        @pl.when(s == 0)
        def _():
            a_ref[...] = _rms_fwd(h_ref[...], g_ref[...]).astype(BF16)
            acc_ref[...] = jnp.zeros_like(acc_ref)

        a = a_ref[...]
        g = _dot(a, wg_ref[...])
        u = _dot(a, wu_ref[...])
        sg = _sigmoid(g)
        act = (g * sg * u).astype(BF16)
        go_ref[...] = (u * (sg * (1.0 + g * (1.0 - sg)))).astype(BF16)
        uo_ref[...] = (g * sg).astype(BF16)
        act_ref[...] = act
        acc_ref[...] += _dot(act, wd_ref[...])

        @pl.when(s == N_CHIPS - 1)
        def _():
            ho_ref[...] = h_ref[...] + 0.5 * acc_ref[...]

    row = pl.BlockSpec((tm, D), lambda t, s: (t, 0))
    col = pl.BlockSpec((tm, Fs), lambda t, s: (t, s))
    wcol = pl.BlockSpec((None, D, Fs), lambda t, s: (s, 0, 0))
    return _call(
        body, name=name, grid=(T // tm, N_CHIPS),
        in_specs=[row, pl.BlockSpec((None, 1, D), lambda t, s: (layer, 0, 0)), wcol, wcol,
                  pl.BlockSpec((None, Fs, D), lambda t, s: (s, 0, 0))],
        out_specs=[row, row, col, col, col],
        out_shape=[jax.ShapeDtypeStruct((T, D), F32), jax.ShapeDtypeStruct((T, D), BF16)]
        + [jax.ShapeDtypeStruct((T, F), BF16)] * 3,
        scratch=[pltpu.VMEM((tm, D), F32)],
        sem=("parallel", "arbitrary"), operands=(h, gain, wg, wu, wd), rider=rider)


def _inproj_fwd(h, gain, win, layer, tm, name, rider=None):
    T, D = h.shape
    Ns = win.shape[-1]

    def body(h_ref, g_ref, w_ref, z_ref, b_ref):
        @pl.when(pl.program_id(1) == 0)
        def _():
            b_ref[...] = _rms_fwd(h_ref[...], g_ref[...]).astype(BF16)

        z_ref[...] = _dot(b_ref[...], w_ref[...])

    return _call(
        body, name=name, grid=(T // tm, N_CHIPS),
        in_specs=[pl.BlockSpec((tm, D), lambda t, s: (t, 0)),
                  pl.BlockSpec((None, 1, D), lambda t, s: (layer, 0, 0)),
                  pl.BlockSpec((None, D, Ns), lambda t, s: (s, 0, 0))],
        out_specs=[pl.BlockSpec((tm, Ns), lambda t, s: (t, s)), pl.BlockSpec((tm, D), lambda t, s: (t, 0))],
        out_shape=[jax.ShapeDtypeStruct((T, N_CHIPS * Ns), F32), jax.ShapeDtypeStruct((T, D), BF16)],
        sem=("parallel", "arbitrary"), operands=(h, gain, win), rider=rider)


def _ret_consts(T, pad):
    half = HEAD_DIM // 2
    inv_freq = ROPE_BASE ** (-jnp.arange(half, dtype=F32) / half)
    pos = jnp.arange(T, dtype=F32) - pad
    ang = pos[:, None] * inv_freq[None, :]
    cos = jnp.cos(ang)
    sin = jnp.sin(ang)
    cosf = jnp.concatenate([cos, cos], axis=1)
    sinf = jnp.concatenate([-sin, sin], axis=1)
    log_gamma = jnp.log1p(-(2.0 ** (-5.0 - jnp.arange(RET_HEADS, dtype=F32))))
    idx = jnp.arange(CHUNK, dtype=F32)
    diff = idx[:, None] - idx[None, :]
    intra = jnp.where(diff[None] >= 0, jnp.exp(diff[None] * log_gamma[:, None, None]), 0.0)
    k_decay = jnp.exp((CHUNK - 1.0 - idx)[None, :] * log_gamma[:, None])
    q_decay = jnp.exp((idx + 1.0)[None, :] * log_gamma[:, None])
    chunk_decay = jnp.exp(CHUNK * log_gamma)
    kdec = jnp.broadcast_to(k_decay[:, :, None], (RET_HEADS, CHUNK, HEAD_DIM))
    qdec = jnp.broadcast_to(q_decay[:, :, None], (RET_HEADS, CHUNK, HEAD_DIM))
    cdb = jnp.broadcast_to(chunk_decay[:, None, None], (RET_HEADS, 8, HEAD_DIM))
    return cosf, sinf, intra, kdec, qdec, cdb


def _rot(t, cosv, sinv):
    return t * cosv + pltpu.roll(t, HEAD_DIM // 2, 1) * sinv


def _rot_t(g, cosv, sinv):
    return g * cosv + pltpu.roll(g * sinv, HEAD_DIM // 2, 1)


def _head_specs(tg, section, order):
    return pl.BlockSpec((tg, HEAD_DIM), lambda h, g: (order(g), section * RET_HEADS + h))


def _ret_fwd(z, consts, cg, name, rider=None):
    T = z.shape[0]
    N = T // CHUNK
    ng = N // cg
    tg = cg * CHUNK
    cosf, sinf, intra, kdec, qdec, cdb = consts
    fwd = lambda g: g

    def body(zq, zk, zv, zg, cos_ref, sin_ref, m_ref, kd_ref, qd_ref, cd_ref, r_ref, o_ref, s_ref, st_ref):
        @pl.when(pl.program_id(1) == 0)
        def _():
            st_ref[...] = jnp.zeros_like(st_ref)

        cosv = cos_ref[...]
        sinv = sin_ref[...]
        q3 = (_rot(zq[...], cosv, sinv) * (HEAD_DIM ** -0.5)).reshape(cg, CHUNK, HEAD_DIM)
        k3 = _rot(zk[...], cosv, sinv).reshape(cg, CHUNK, HEAD_DIM)
        vb = zv[...].reshape(cg, CHUNK, HEAD_DIM).astype(BF16)
        scores = _ein("ncd,nmd->ncm", q3.astype(BF16), k3.astype(BF16)) * m_ref[...][None]
        inner = _ein("ncm,nmd->ncd", scores.astype(BF16), vb)
        kv = _ein("ncd,nce->nde", (k3 * kd_ref[...][None]).astype(BF16), vb)
        cd = cd_ref[0:1, :]
        state = st_ref[...]
        for n in range(cg):
            s_ref[n] = state
            state = state * cd + kv[n]
        st_ref[...] = state
        qdb = (q3 * qd_ref[...][None]).astype(BF16)
        cross = _ein("ncd,nde->nce", qdb, s_ref[...].astype(BF16))
        out = (inner + cross).reshape(tg, HEAD_DIM)
        o_ref[...] = out
        xc = out - jnp.mean(out, axis=-1, keepdims=True)
        rn = xc * lax.rsqrt(jnp.mean(xc * xc, axis=-1, keepdims=True) + EPS)
        g = zg[...]
        r_ref[...] = (rn * (g * _sigmoid(g))).astype(BF16)

    tab = pl.BlockSpec((tg, HEAD_DIM), lambda h, g: (g, 0))
    per_head = lambda rows: pl.BlockSpec((None, rows, HEAD_DIM), lambda h, g: (h, 0, 0))
    head_out = pl.BlockSpec((tg, HEAD_DIM), lambda h, g: (g, h))
    return _call(
        body, name=name, grid=(RET_HEADS, ng),
        in_specs=[_head_specs(tg, i, fwd) for i in range(4)]
        + [tab, tab, per_head(CHUNK), per_head(CHUNK), per_head(CHUNK), per_head(8)],
        out_specs=[head_out, head_out, pl.BlockSpec((None, cg, HEAD_DIM, HEAD_DIM), lambda h, g: (h, g, 0, 0))],
        out_shape=[jax.ShapeDtypeStruct((T, RET_WIDTH), BF16), jax.ShapeDtypeStruct((T, RET_WIDTH), F32),
                   jax.ShapeDtypeStruct((RET_HEADS, N, HEAD_DIM, HEAD_DIM), F32)],
        scratch=[pltpu.VMEM((HEAD_DIM, HEAD_DIM), F32)],
        sem=("parallel", "arbitrary"), operands=(z, z, z, z, cosf, sinf, intra, kdec, qdec, cdb), rider=rider)


def _window_sums(u, shift_of):
    sums = []
    s = u
    k = 1
    while k < POOL_WINDOWS[-1]:
        s = s + pltpu.roll(s, shift_of(k), 0)
        sums.append(s)
        k *= 2
    return sums


def _select_group(vals, g):
    out = vals[-1]
    for i in range(len(vals) - 2, -1, -1):
        out = jnp.where(g == i, vals[i], out)
    return out


def _pool_parts(u, g, T, pad):
    rows = lax.broadcasted_iota(jnp.int32, (T, HEAD_DIM), 0)
    valid = rows >= pad
    win = _select_group([float(w) for w in POOL_WINDOWS], g)
    div = jnp.clip((rows - pad + 1).astype(F32), 1.0, win)
    s = _select_group(_window_sums(u, lambda k: k), g)
    pooled = jnp.where(valid, s / div - u, 0.0)
    return pooled, div, valid


def _pool_specs(T, layer):
    first = 4 * RET_WIDTH // HEAD_DIM
    return [
        pl.BlockSpec((T, HEAD_DIM), lambda g: (0, first + g)),
        pl.BlockSpec((None, None, HEAD_DIM, HEAD_DIM), lambda g: (layer, g, 0, 0)),
        pl.BlockSpec((None, 1, HEAD_DIM), lambda g: (layer, 0, g)),
    ]


def _pool_fwd(z, maps, scale, layer, pad, name):
    T = z.shape[0]
    assert pad >= POOL_WINDOWS[-1], "window rolls wrap into the zero rows in front"

    def body(zu, maps_ref, sc_ref, pm_ref):
        g = pl.program_id(0)
        pooled, _, _ = _pool_parts(zu[...], g, T, pad)
        y = _dot(pooled.astype(BF16), maps_ref[...].astype(BF16))
        pm_ref[...] = (y * sc_ref[...]).astype(BF16)

    return _call(
        body, name=name, grid=(POOL_GROUPS,),
        in_specs=_pool_specs(T, layer),
        out_specs=[pl.BlockSpec((T, HEAD_DIM), lambda g: (0, g))],
        out_shape=[jax.ShapeDtypeStruct((T, POOL_WIDTH), BF16)],
        sem=("parallel",), operands=(z, maps, scale))[0]


def _gate_specs(tm, D):
    nb = D // RET_WIDTH
    first = (4 * RET_WIDTH + POOL_WIDTH) // RET_WIDTH
    return [pl.BlockSpec((tm, RET_WIDTH), functools.partial(lambda t, j: (t, j), j=first + j)) for j in range(2 * nb)]


def _load_gates(refs, nb):
    ga = jnp.concatenate([r[...] for r in refs[:nb]], axis=1) if nb > 1 else refs[0][...]
    gb = jnp.concatenate([r[...] for r in refs[nb:]], axis=1) if nb > 1 else refs[nb][...]
    return ga, gb


def _mix_fwd(h, r, pm, z, wru, wpu, wout, tm, name, rider=None):
    T, D = h.shape
    Dq = D // N_CHIPS
    nb = D // RET_WIDTH

    def body(*refs):
        h_ref, r_ref, pm_ref = refs[:3]
        gate_refs = refs[3:3 + 2 * nb]
        wru_ref, wpu_ref, wout_ref, ho_ref, mx_ref, ret_ref, pool_ref = refs[3 + 2 * nb:]
        rv = r_ref[...]
        pv = pm_ref[...]
        ret = jnp.concatenate([_dot(rv, wru_ref[s]) for s in range(N_CHIPS)], axis=1)
        pool = jnp.concatenate([_dot(pv, wpu_ref[s]) for s in range(N_CHIPS)], axis=1)
        ga, gb = _load_gates(gate_refs, nb)
        mixed = (_sigmoid(ga) * ret + _sigmoid(gb) * pool).astype(BF16)
        mx_ref[...] = mixed
        ret_ref[...] = ret.astype(BF16)
        pool_ref[...] = pool.astype(BF16)
        ho_ref[...] = h_ref[...] + _dot(mixed, wout_ref[...].reshape(D, D))

    row = pl.BlockSpec((tm, D), lambda t: (t, 0))
    half = pl.BlockSpec((tm, RET_WIDTH), lambda t: (t, 0))
    up = pl.BlockSpec((N_CHIPS, RET_WIDTH, Dq), lambda t: (0, 0, 0))
    return _call(
        body, name=name, grid=(T // tm,),
        in_specs=[row, half, half] + _gate_specs(tm, D) + [up, up, pl.BlockSpec((N_CHIPS, Dq, D), lambda t: (0, 0, 0))],
        out_specs=[row, row, row, row],
        out_shape=[jax.ShapeDtypeStruct((T, D), F32)] + [jax.ShapeDtypeStruct((T, D), BF16)] * 3,
        sem=("parallel",), operands=(h, r, pm, *([z] * (2 * nb)), wru, wpu, wout), rider=rider)


def _final_loss(h, gain, tgt, name):
    T, D = h.shape
    first = (T - tgt.shape[0]) // CHUNK

    def body(h_ref, g_ref, t_ref, dh_ref, loss_ref, dg_ref):
        i = pl.program_id(0)

        @pl.when(i == 0)
        def _():
            loss_ref[...] = jnp.zeros_like(loss_ref)
            dg_ref[...] = jnp.zeros_like(dg_ref)

        x = h_ref[...]
        gain_v = g_ref[...]
        err = jnp.where(i >= first, _rms_fwd(x, gain_v) - t_ref[...], 0.0)
        loss_ref[...] += 0.5 * jnp.sum(jnp.mean(err * err, axis=-1))
        dx, dgain = _rms_bwd(x, gain_v, err * (1.0 / D))
        dg_ref[...] += dgain
        dh_ref[...] = dx

    return _call(
        body, name=name, grid=(T // CHUNK,),
        in_specs=[pl.BlockSpec((CHUNK, D), lambda i: (i, 0)),
                  pl.BlockSpec((1, D), lambda i: (0, 0)),
                  pl.BlockSpec((CHUNK, D), lambda i: (jnp.maximum(i - first, 0), 0))],
        out_specs=[pl.BlockSpec((CHUNK, D), lambda i: (i, 0)),
                   pl.BlockSpec((1, LANES), lambda i: (0, 0)),
                   pl.BlockSpec((1, D), lambda i: (0, 0))],
        out_shape=[jax.ShapeDtypeStruct((T, D), F32), jax.ShapeDtypeStruct((1, LANES), F32),
                   jax.ShapeDtypeStruct((1, D), F32)],
        sem=("arbitrary",), operands=(h, gain, tgt))


def _ffn_bwd_act(dy, g, u, wd, tm, name, rider=None):
    T, D = dy.shape
    Fs = wd.shape[1]
    F = N_CHIPS * Fs

    def body(dy_ref, go_ref, uo_ref, wd_ref, dg_ref, du_ref, dyh_ref):
        @pl.when(pl.program_id(1) == 0)
        def _():
            dyh_ref[...] = (0.5 * dy_ref[...]).astype(BF16)

        dact = _dot_nt(dyh_ref[...], wd_ref[...])
        du_ref[...] = (dact * uo_ref[...].astype(F32)).astype(BF16)
        dg_ref[...] = (dact * go_ref[...].astype(F32)).astype(BF16)

    row = pl.BlockSpec((tm, D), lambda t, s: (t, 0))
    col = pl.BlockSpec((tm, Fs), lambda t, s: (t, s))
    return _call(
        body, name=name, grid=(T // tm, N_CHIPS),
        in_specs=[row, col, col, pl.BlockSpec((None, Fs, D), lambda t, s: (s, 0, 0))],
        out_specs=[col, col, row],
        out_shape=[jax.ShapeDtypeStruct((T, F), BF16), jax.ShapeDtypeStruct((T, F), BF16),
                   jax.ShapeDtypeStruct((T, D), BF16)],
        sem=("parallel", "arbitrary"), operands=(dy, g, u, wd), rider=rider)


def _ffn_bwd_in(dy, h, gain, dg, du, wg, wu, layer, tm, pad, name, rider=None):
    T, D = h.shape
    Fs = wg.shape[-1]

    def body(dy_ref, h_ref, g_ref, dg_ref, du_ref, wg_ref, wu_ref, dh_ref, dgain_ref, da_ref):
        t = pl.program_id(0)
        s = pl.program_id(1)

        @pl.when((t == 0) & (s == 0))
        def _():
            dgain_ref[...] = jnp.zeros_like(dgain_ref)

        @pl.when(s == 0)
        def _():
            da_ref[...] = jnp.zeros_like(da_ref)

        da_ref[...] += _dot_nt(dg_ref[...], wg_ref[...]) + _dot_nt(du_ref[...], wu_ref[...])

        @pl.when(s == N_CHIPS - 1)
        def _():
            dx, dgain = _rms_bwd(h_ref[...], g_ref[...], da_ref[...])
            dgain_ref[...] += dgain
            dh_ref[...] = jnp.where(_row_mask(t, tm, pad, (tm, D)), dy_ref[...] + dx, 0.0)

    row = pl.BlockSpec((tm, D), lambda t, s: (t, 0))
    col = pl.BlockSpec((tm, Fs), lambda t, s: (t, s))
    wcol = pl.BlockSpec((None, D, Fs), lambda t, s: (s, 0, 0))
    return _call(
        body, name=name, grid=(T // tm, N_CHIPS),
        in_specs=[row, row, pl.BlockSpec((None, 1, D), lambda t, s: (layer, 0, 0)), col, col, wcol, wcol],
        out_specs=[row, pl.BlockSpec((1, D), lambda t, s: (0, 0))],
        out_shape=[jax.ShapeDtypeStruct((T, D), F32), jax.ShapeDtypeStruct((1, D), F32)],
        scratch=[pltpu.VMEM((tm, D), F32)],
        sem=("arbitrary", "arbitrary"), operands=(dy, h, gain, dg, du, wg, wu), rider=rider)


def _grad_tn(a, b, mode, scale, tm, name, rider=None):
    T = a.shape[0]
    if mode == "col":
        per, R, C = 1, a.shape[1], b.shape[1] // N_CHIPS
        a_spec = pl.BlockSpec((tm, R), lambda s, t: (t, 0))
        b_spec = pl.BlockSpec((tm, C), lambda s, t: (t, s))
    else:
        per, R, C = 2, a.shape[1] // N_CHIPS, b.shape[1]
        a_spec = pl.BlockSpec((tm, per * R), lambda s, t: (t, s))
        b_spec = pl.BlockSpec((tm, C), lambda s, t: (t, 0))
    nt = T // tm

    def body(a_ref, b_ref, o_ref, acc_ref):
        t = pl.program_id(1)

        @pl.when(t == 0)
        def _():
            acc_ref[...] = jnp.zeros_like(acc_ref)

        acc_ref[...] += _dot_tn(a_ref[...].astype(BF16), b_ref[...].astype(BF16))

        @pl.when(t == nt - 1)
        def _():
            o_ref[...] = (scale * acc_ref[...]).astype(BF16).reshape(per, R, C)

    return _call(
        body, name=name, grid=(N_CHIPS // per, nt),
        in_specs=[a_spec, b_spec],
        out_specs=[pl.BlockSpec((per, R, C), lambda s, t: (s, 0, 0))],
        out_shape=[jax.ShapeDtypeStruct((N_CHIPS, R, C), BF16)],
        scratch=[pltpu.VMEM((per * R, C), F32)],
        sem=("parallel", "arbitrary"), operands=(a, b), rider=rider)[0]


def _mix_bwd_dx(dh, z, ret, pool, wout, wru, wpu, tm, name, rider=None):
    T, D = dh.shape
    Dq = D // N_CHIPS
    nb = D // RET_WIDTH

    def body(*refs):
        dh_ref = refs[0]
        gate_refs = refs[1:1 + 2 * nb]
        ret_ref, pool_ref, wout_ref, wru_ref, wpu_ref, dgab_ref, dret_ref, dpool_ref, dr_ref, dpm_ref = refs[1 + 2 * nb:]
        dmixed = _dot_nt(dh_ref[...].astype(BF16), wout_ref[...].reshape(D, D))
        ga, gb = _load_gates(gate_refs, nb)
        sa = _sigmoid(ga)
        sb = _sigmoid(gb)
        dgab_ref[:, :D] = (dmixed * ret_ref[...].astype(F32) * (sa * (1.0 - sa))).astype(BF16)
        dgab_ref[:, D:] = (dmixed * pool_ref[...].astype(F32) * (sb * (1.0 - sb))).astype(BF16)
        dret = (dmixed * sa).astype(BF16)
        dpool = (dmixed * sb).astype(BF16)
        dret_ref[...] = dret
        dpool_ref[...] = dpool
        dr = _dot_nt(dret[:, :Dq], wru_ref[0])
        dpm = _dot_nt(dpool[:, :Dq], wpu_ref[0])
        for s in range(1, N_CHIPS):
            dr += _dot_nt(dret[:, s * Dq:(s + 1) * Dq], wru_ref[s])
            dpm += _dot_nt(dpool[:, s * Dq:(s + 1) * Dq], wpu_ref[s])
        dr_ref[...] = dr
        dpm_ref[...] = dpm

    row = pl.BlockSpec((tm, D), lambda t: (t, 0))
    half = pl.BlockSpec((tm, RET_WIDTH), lambda t: (t, 0))
    up = pl.BlockSpec((N_CHIPS, RET_WIDTH, Dq), lambda t: (0, 0, 0))
    return _call(
        body, name=name, grid=(T // tm,),
        in_specs=[row] + _gate_specs(tm, D) + [row, row, pl.BlockSpec((N_CHIPS, Dq, D), lambda t: (0, 0, 0)), up, up],
        out_specs=[pl.BlockSpec((tm, 2 * D), lambda t: (t, 0)), row, row, half, half],
        out_shape=[jax.ShapeDtypeStruct((T, 2 * D), BF16), jax.ShapeDtypeStruct((T, D), BF16),
                   jax.ShapeDtypeStruct((T, D), BF16), jax.ShapeDtypeStruct((T, RET_WIDTH), F32),
                   jax.ShapeDtypeStruct((T, POOL_WIDTH), F32)],
        sem=("parallel",), operands=(dh, *([z] * (2 * nb)), ret, pool, wout, wru, wpu), rider=rider)


def _pool_bwd(z, dpm, maps, scale, layer, pad, name):
    T = z.shape[0]

    def body(zu, maps_ref, sc_ref, dpm_ref, du_ref, dmaps_ref, dsc_ref):
        g = pl.program_id(0)
        u = zu[...]
        pooled, div, valid = _pool_parts(u, g, T, pad)
        pb = pooled.astype(BF16)
        mb = maps_ref[...].astype(BF16)
        dp = dpm_ref[...]
        dsc_ref[...] = jnp.sum(dp * _dot(pb, mb), axis=0, keepdims=True)
        dyb = (dp * sc_ref[...]).astype(BF16)
        dmaps_ref[...] = _dot_tn(pb, dyb)
        dpooled = jnp.where(valid, _dot_nt(dyb, mb), 0.0)
        ahead = _select_group(_window_sums(dpooled / div, lambda k: T - k), g)
        du_ref[...] = jnp.where(valid, ahead - dpooled, 0.0).astype(BF16)

    blk = pl.BlockSpec((T, HEAD_DIM), lambda g: (0, g))
    return _call(
        body, name=name, grid=(POOL_GROUPS,),
        in_specs=_pool_specs(T, layer) + [blk],
        out_specs=[blk, pl.BlockSpec((None, HEAD_DIM, HEAD_DIM), lambda g: (g, 0, 0)),
                   pl.BlockSpec((1, HEAD_DIM), lambda g: (0, g))],
        out_shape=[jax.ShapeDtypeStruct((T, POOL_WIDTH), BF16),
                   jax.ShapeDtypeStruct((POOL_GROUPS, HEAD_DIM, HEAD_DIM), F32),
                   jax.ShapeDtypeStruct((1, POOL_WIDTH), F32)],
        sem=("parallel",), operands=(z, maps, scale, dpm))


def _ret_bwd_local(z, o_pre, s_all, dr, consts, cg, name):
    T = z.shape[0]
    N = T // CHUNK
    ng = N // cg
    tg = cg * CHUNK
    cosf, sinf, intra, _, qdec, _ = consts
    fwd = lambda g: g

    def body(zq, zk, zv, zg, o_ref, s_ref, dr_ref, cos_ref, sin_ref, m_ref, qd_ref,
             dq_ref, dg_ref, dk_ref, dv_ref, ds_ref):
        cosv = cos_ref[...]
        sinv = sin_ref[...]
        scale = HEAD_DIM ** -0.5
        q3 = (_rot(zq[...], cosv, sinv) * scale).reshape(cg, CHUNK, HEAD_DIM)
        k3 = _rot(zk[...], cosv, sinv).reshape(cg, CHUNK, HEAD_DIM)
        qb = q3.astype(BF16)
        kb = k3.astype(BF16)
        vb = zv[...].reshape(cg, CHUNK, HEAD_DIM).astype(BF16)
        mask = m_ref[...][None]
        sb = (_ein("ncd,nmd->ncm", qb, kb) * mask).astype(BF16)
        qdv = qd_ref[...][None]
        qdb = (q3 * qdv).astype(BF16)

        out = o_ref[...]
        xc = out - jnp.mean(out, axis=-1, keepdims=True)
        rstd = lax.rsqrt(jnp.mean(xc * xc, axis=-1, keepdims=True) + EPS)
        rn = xc * rstd
        g = zg[...]
        sg = _sigmoid(g)
        drv = dr_ref[...]
        dg_ref[...] = (drv * rn * (sg * (1.0 + g * (1.0 - sg)))).astype(BF16)
        drn = drv * (g * sg)
        dout = rstd * (drn - jnp.mean(drn, axis=-1, keepdims=True)
                       - rn * jnp.mean(drn * rn, axis=-1, keepdims=True))
        dob = dout.reshape(cg, CHUNK, HEAD_DIM).astype(BF16)

        dsb = (_ein("ncd,nmd->ncm", dob, vb) * mask).astype(BF16)
        dv_ref[...] = _ein("ncm,ncd->nmd", sb, dob).reshape(tg, HEAD_DIM)
        dk_ref[...] = _ein("ncm,ncd->nmd", dsb, qb).reshape(tg, HEAD_DIM)
        dq3 = _ein("ncm,nmd->ncd", dsb, kb) + _ein("nce,nde->ncd", dob, s_ref[...].astype(BF16)) * qdv
        dq_ref[...] = _rot_t(dq3.reshape(tg, HEAD_DIM) * scale, cosv, sinv).astype(BF16)
        ds_ref[...] = _ein("ncd,nce->nde", qdb, dob)

    tab = pl.BlockSpec((tg, HEAD_DIM), lambda h, g: (g, 0))
    per_head = pl.BlockSpec((None, CHUNK, HEAD_DIM), lambda h, g: (h, 0, 0))
    head_blk = pl.BlockSpec((tg, HEAD_DIM), lambda h, g: (g, h))
    state_blk = pl.BlockSpec((None, cg, HEAD_DIM, HEAD_DIM), lambda h, g: (h, g, 0, 0))
    return _call(
        body, name=name, grid=(RET_HEADS, ng),
        in_specs=[_head_specs(tg, i, fwd) for i in range(4)]
        + [head_blk, state_blk, head_blk, tab, tab, per_head, per_head],
        out_specs=[head_blk, head_blk, head_blk, head_blk, state_blk],
        out_shape=[jax.ShapeDtypeStruct((T, RET_WIDTH), BF16), jax.ShapeDtypeStruct((T, RET_WIDTH), BF16),
                   jax.ShapeDtypeStruct((T, RET_WIDTH), F32), jax.ShapeDtypeStruct((T, RET_WIDTH), F32),
                   jax.ShapeDtypeStruct((RET_HEADS, N, HEAD_DIM, HEAD_DIM), F32)],
        sem=("parallel", "parallel"), operands=(z, z, z, z, o_pre, s_all, dr, cosf, sinf, intra, qdec))


def _ret_bwd_state(z, dkp, dvp, ds, consts, cg, name):
    T = z.shape[0]
    N = T // CHUNK
    ng = N // cg
    tg = cg * CHUNK
    cosf, sinf, _, kdec, _, cdb = consts
    rev = lambda g: ng - 1 - g

    def body(zk, zv, dkp_ref, dvp_ref, ds_ref, cos_ref, sin_ref, kd_ref, cd_ref, dk_ref, dv_ref, gs_ref, dkv_ref):
        @pl.when(pl.program_id(1) == 0)
        def _():
            gs_ref[...] = jnp.zeros_like(gs_ref)

        cosv = cos_ref[...]
        sinv = sin_ref[...]
        cd = cd_ref[0:1, :]
        grad = gs_ref[...]
        for n in reversed(range(cg)):
            dkv_ref[n] = grad
            grad = ds_ref[n] + cd * grad
        gs_ref[...] = grad
        dkvb = dkv_ref[...].astype(BF16)
        kdv = kd_ref[...][None]
        k3 = _rot(zk[...], cosv, sinv).reshape(cg, CHUNK, HEAD_DIM)
        vb = zv[...].reshape(cg, CHUNK, HEAD_DIM).astype(BF16)
        dk3 = _ein("nce,nde->ncd", vb, dkvb) * kdv
        dv3 = _ein("ncd,nde->nce", (k3 * kdv).astype(BF16), dkvb)
        dk_ref[...] = _rot_t(dkp_ref[...] + dk3.reshape(tg, HEAD_DIM), cosv, sinv).astype(BF16)
        dv_ref[...] = (dvp_ref[...] + dv3.reshape(tg, HEAD_DIM)).astype(BF16)

    tab = pl.BlockSpec((tg, HEAD_DIM), lambda h, g: (rev(g), 0))
    head_blk = pl.BlockSpec((tg, HEAD_DIM), lambda h, g: (rev(g), h))
    return _call(
        body, name=name, grid=(RET_HEADS, ng),
        in_specs=[_head_specs(tg, 1, rev), _head_specs(tg, 2, rev), head_blk, head_blk,
                  pl.BlockSpec((None, cg, HEAD_DIM, HEAD_DIM), lambda h, g: (h, rev(g), 0, 0)),
                  tab, tab,
                  pl.BlockSpec((None, CHUNK, HEAD_DIM), lambda h, g: (h, 0, 0)),
                  pl.BlockSpec((None, 8, HEAD_DIM), lambda h, g: (h, 0, 0))],
        out_specs=[head_blk, head_blk],
        out_shape=[jax.ShapeDtypeStruct((T, RET_WIDTH), BF16)] * 2,
        scratch=[pltpu.VMEM((HEAD_DIM, HEAD_DIM), F32), pltpu.VMEM((cg, HEAD_DIM, HEAD_DIM), F32)],
        sem=("parallel", "arbitrary"), operands=(z, z, dkp, dvp, ds, cosf, sinf, kdec, cdb))


def _inproj_bwd_dx(dz, win, h, gain, dh_in, layer, tm, pad, name, rider=None):
    T, D = h.shape
    Ns = win.shape[-1]

    def body(dz_ref, w_ref, h_ref, g_ref, dhi_ref, dh_ref, dgain_ref, db_ref):
        t = pl.program_id(0)
        s = pl.program_id(1)

        @pl.when((t == 0) & (s == 0))
        def _():
            dgain_ref[...] = jnp.zeros_like(dgain_ref)

        @pl.when(s == 0)
        def _():
            db_ref[...] = jnp.zeros_like(db_ref)

        db_ref[...] += _dot_nt(dz_ref[...], w_ref[...])

        @pl.when(s == N_CHIPS - 1)
        def _():
            dx, dgain = _rms_bwd(h_ref[...], g_ref[...], db_ref[...])
            dgain_ref[...] += dgain
            dh_ref[...] = jnp.where(_row_mask(t, tm, pad, (tm, D)), dhi_ref[...] + dx, 0.0)

    row = pl.BlockSpec((tm, D), lambda t, s: (t, 0))
    return _call(
        body, name=name, grid=(T // tm, N_CHIPS),
        in_specs=[pl.BlockSpec((tm, Ns), lambda t, s: (t, s)),
                  pl.BlockSpec((None, D, Ns), lambda t, s: (s, 0, 0)),
                  row, pl.BlockSpec((None, 1, D), lambda t, s: (layer, 0, 0)), row],
        out_specs=[row, pl.BlockSpec((1, D), lambda t, s: (0, 0))],
        out_shape=[jax.ShapeDtypeStruct((T, D), F32), jax.ShapeDtypeStruct((1, D), F32)],
        scratch=[pltpu.VMEM((tm, D), F32)],
        sem=("arbitrary", "arbitrary"), operands=(dz, win, h, gain, dh_in), rider=rider)


def _sum_pair(gs, rs, c_idx, name):
    n = len(gs)

    def body(c_ref, *refs):
        for g_ref, r_ref, o_ref in zip(refs[:n], refs[n:2 * n], refs[2 * n:]):
            o_ref[...] = (g_ref[...].astype(F32) + r_ref[...].astype(F32)).astype(BF16)

    halves = [pl.BlockSpec((None,) + r.shape[1:], lambda s, c_ref: (s, 0, 0)) for r in rs]
    return pl.pallas_call(
        body,
        name=name,
        grid_spec=pltpu.PrefetchScalarGridSpec(
            num_scalar_prefetch=1,
            grid=(N_CHIPS,),
            in_specs=[pl.BlockSpec((None,) + r.shape[1:], lambda s, c_ref: (s, c_ref[0], 0)) for r in rs] + halves,
            out_specs=halves,
        ),
        out_shape=[jax.ShapeDtypeStruct(r.shape, BF16) for r in rs],
        compiler_params=_params(("parallel",)),
    )(c_idx, *gs, *rs)


def _sum_chips(ps, rs, pos, name):
    n = len(ps)
    quarters = 4

    def body(pos_ref, *refs):
        chip = pos_ref[0]
        for p_ref, r_ref, o_ref in zip(refs[:n], refs[n:2 * n], refs[2 * n:]):
            own = p_ref[...].astype(F32)
            terms = [jnp.where(chip == k, own, r_ref[k].astype(F32)) for k in range(N_CHIPS)]
            o_ref[...] = ((terms[0] + terms[1]) + terms[2]) + terms[3]

    def rows(r):
        assert r.shape[1] % (quarters * BF16_ROWS) == 0, r.shape
        return r.shape[1] // quarters

    return pl.pallas_call(
        body,
        name=name,
        grid_spec=pltpu.PrefetchScalarGridSpec(
            num_scalar_prefetch=1,
            grid=(quarters,),
            in_specs=[pl.BlockSpec((None, rows(r), r.shape[2]), lambda q, pos_ref: (pos_ref[0], q, 0)) for r in rs]
            + [pl.BlockSpec((N_CHIPS, rows(r), r.shape[2]), lambda q, pos_ref: (0, q, 0)) for r in rs],
            out_specs=[pl.BlockSpec((rows(r), r.shape[2]), lambda q, pos_ref: (pos_ref[1] * quarters + q, 0))
                       for r in rs],
        ),
        out_shape=[jax.ShapeDtypeStruct((2 * r.shape[1], r.shape[2]), F32) for r in rs],
        compiler_params=_params(("arbitrary",)),
    )(pos, *ps, *rs)


def _small_all_reduce(p):
    rows, width = p.shape

    def body(p_ref, o_ref, sib_ref, slot_ref, ssem, rsem):
        x, y, c, chip, others = _mesh_pos()
        pair = _remote(p_ref, sib_ref, ssem.at[0], rsem.at[0], (x, y, 1 - c))
        pair.start()
        pair.wait()
        slot_ref[chip] = p_ref[...] + sib_ref[...]
        sends = []
        for j, (ox, oy) in enumerate(others):
            cp = _remote(slot_ref.at[chip], slot_ref.at[chip], ssem.at[1 + j], rsem.at[1 + j], (ox, oy, c))
            cp.start()
            sends.append(cp)
        for j, (ox, oy) in enumerate(others):
            slot = slot_ref.at[2 * ox + oy]
            _remote(slot, slot, ssem.at[1 + j], rsem.at[1 + j], (ox, oy, c)).wait_recv()
        for cp in sends:
            cp.wait_send()
        o_ref[...] = ((slot_ref[0] + slot_ref[1]) + slot_ref[2]) + slot_ref[3]

    vmem = pl.BlockSpec(memory_space=pltpu.VMEM)
    return pl.pallas_call(
        body,
        name="small_grads_all_reduce",
        in_specs=[vmem],
        out_specs=vmem,
        out_shape=jax.ShapeDtypeStruct(p.shape, F32),
        scratch_shapes=[pltpu.VMEM((rows, width), F32), pltpu.VMEM((N_CHIPS, rows, width), F32),
                        pltpu.SemaphoreType.DMA((4,)), pltpu.SemaphoreType.DMA((4,))],
    )(p)


def _adamw(gs, w, m, v, name):
    L, R, C = w.shape
    Ct = gs[0].shape[1]
    tr = _pick_tile(R, 256, 8)

    def body(*refs):
        g_refs = refs[:L]
        w_ref, m_ref, v_ref, go_ref, d_ref, mo_ref, vo_ref = refs[L:]
        layer = pl.program_id(0)
        grad = g_refs[L - 1][...]
        for i in range(L - 2, -1, -1):
            grad = jnp.where(layer == i, g_refs[i][...], grad)
        if Ct != C:
            grad = grad[:, :C]
        m_new = ADAM_B1 * m_ref[...] + (1.0 - ADAM_B1) * grad
        v_new = ADAM_B2 * v_ref[...] + (1.0 - ADAM_B2) * jnp.square(grad)
        m_hat = m_new / (1.0 - ADAM_B1 ** ADAM_STEP)
        v_hat = v_new / (1.0 - ADAM_B2 ** ADAM_STEP)
        go_ref[...] = grad
        d_ref[...] = -ADAM_LR * (m_hat / (jnp.sqrt(v_hat) + ADAM_EPS) + ADAM_WD * w_ref[...])
        mo_ref[...] = m_new
        vo_ref[...] = v_new

    g_specs = [pl.BlockSpec((tr, Ct), functools.partial(lambda l, r, i: (jnp.where(l == i, r, 0), 0), i=i))
               for i in range(L)]
    blk = pl.BlockSpec((None, tr, C), lambda l, r: (l, r, 0))
    return pl.pallas_call(
        body,
        name=name,
        grid=(L, R // tr),
        in_specs=g_specs + [blk, blk, blk],
        out_specs=[blk] * 4,
        out_shape=[jax.ShapeDtypeStruct((L, R, C), F32)] * 4,
        compiler_params=_params(("arbitrary", "arbitrary")),
    )(*gs, w, m, v)


_FFN1 = ("ffn1_gate", "ffn1_up", "ffn1_down")
_FFN2 = ("ffn2_gate", "ffn2_up", "ffn2_down")
_MIXW = ("w_ret_up", "w_pool_up", "w_out")
_BIG = _FFN1 + ("w_in",) + _MIXW + _FFN2
_TRANSPOSED = ("ffn1_gate", "ffn1_up", "ffn2_gate", "ffn2_up")
_SMALL = ("ffn1_norm", "mix_norm", "ffn2_norm", "final_norm", "pool_scale", "pool_maps")
_ORDER = ("meta", "ffn1_norm", "ffn1_gate", "ffn1_up", "ffn1_down", "mix_norm", "w_in", "pool_maps",
          "pool_scale", "w_ret_up", "w_pool_up", "w_out", "ffn2_norm", "ffn2_gate", "ffn2_up", "ffn2_down",
          "final_norm")


def _transport(a):
    n, r, c = a.shape
    out = a.astype(BF16)
    if c % LANES:
        out = jnp.concatenate([out, jnp.zeros((n, r, _round_up(c, LANES) - c), BF16)], axis=2)
    if r % LANES:
        out = jnp.concatenate([out, jnp.zeros((n, _round_up(r, LANES) - r, out.shape[2]), BF16)], axis=1)
    return out


def _pack_rows(parts, width):
    rows = [p.reshape(-1, width) for p in parts]
    total = sum(r.shape[0] for r in rows)
    fill = _round_up(total, 8) - total
    if fill:
        rows.append(jnp.zeros((fill, width), F32))
    return jnp.concatenate(rows, axis=0)


def _unpack_rows(packed, shapes, width):
    out, at = [], 0
    for shp in shapes:
        n = math.prod(shp) // width
        out.append(packed[at:at + n].reshape(shp))
        at += n
    return out


class _Weights:
    def __init__(self, shards):
        self.shards = shards
        self.full = {}

    def rider(self, keys):
        r = _gather_rider([(self.shards[n], i) for n, i in keys])
        r.keys = keys
        return r

    def take(self, rider):
        for key, arr in zip(rider.keys, rider.results):
            self.full[key] = arr

    def __call__(self, name, layer):
        return self.full[(name, layer)]


def _local_step(x, meta_full, tgt, w, wts, pad, tm, cg, reducer):
    D = x.shape[1]
    T = pad + N_META + x.shape[0]
    L = w["ffn1_norm"].shape[0]
    pool_maps = w["pool_maps"]
    gains = {n: w[n].reshape(L, 1, D) for n in ("ffn1_norm", "mix_norm", "ffn2_norm")}
    scale3 = w["pool_scale"].reshape(L, 1, POOL_WIDTH)
    consts = _ret_consts(T, pad)
    tl = _pick_tile(T, 2 * tm, BF16_ROWS)
    def gather(keys):
        return wts.rider(keys) if keys and keys[0] not in wts.full else None

    def done(rider):
        if rider is not None:
            wts.take(rider)

    h = jnp.concatenate([jnp.zeros((pad, D), F32), meta_full, x], axis=0)
    saved = []
    for i in range(L):
        s = {"h0": h}
        rd = gather([("w_in", i)] + [(n, i) for n in _MIXW])
        h, s["a1"], s["g1"], s["u1"], s["act1"] = _ffn_fwd(
            h, gains["ffn1_norm"], wts("ffn1_gate", i), wts("ffn1_up", i), wts("ffn1_down", i), i, tl,
            f"ffn1_fwd_{i}", rd)
        done(rd)
        s["h1"] = h
        rd = gather([("ffn2_gate", i), ("ffn2_up", i)])
        s["z"], s["b"] = _inproj_fwd(h, gains["mix_norm"], wts("w_in", i), i, tl, f"inproj_fwd_{i}", rd)
        done(rd)
        s["r"], s["o_pre"], s["s_all"] = _ret_fwd(s["z"], consts, cg, f"retention_fwd_{i}")
        s["pm"] = _pool_fwd(s["z"], pool_maps, scale3, i, pad, f"pool_fwd_{i}")
        rd = gather([("ffn2_down", i)])
        h, s["mixed"], s["ret"], s["pool"] = _mix_fwd(
            h, s["r"], s["pm"], s["z"], wts("w_ret_up", i), wts("w_pool_up", i), wts("w_out", i), tm,
            f"mix_fwd_{i}", rd)
        done(rd)
        s["h2"] = h
        rd = gather([(n, i + 1) for n in _FFN1]) if i + 1 < L else None
        h, s["a2"], s["g2"], s["u2"], s["act2"] = _ffn_fwd(
            h, gains["ffn2_norm"], wts("ffn2_gate", i), wts("ffn2_up", i), wts("ffn2_down", i), i, tl,
            f"ffn2_fwd_{i}", rd)
        done(rd)
        saved.append(s)

    dh, loss_acc, d_final = _final_loss(h, w["final_norm"].reshape(1, D), tgt, "final_norm_loss")

    small = {n: [None] * L for n in ("ffn1_norm", "mix_norm", "ffn2_norm", "pool_scale", "pool_maps")}

    carry = {"ffn_act": 1.0, "ffn_in": 2.2, "mix_bwd": 1.0, "inproj_bwd": 1.5, "w_in": 1.0}

    tk = _pick_tile(T, 1408, LANES)

    def grad(n, a, b, i, mode):
        rd = reducer.rider(carry.get(n, 1.0 if i == 0 and n.startswith("ffn") else 0.5))
        reducer.add(n, i, _grad_tn(a, b, mode, 1.0, tk, f"grad_{n}_{i}", rd))
        reducer.done(rd)

    def ffn_bwd(which, dy, h_in, g, u, i):
        rd = reducer.rider(carry["ffn_act"])
        dg, du, dyh = _ffn_bwd_act(dy, g, u, wts(f"{which}_down", i), tl, f"{which}_bwd_act_{i}", rd)
        reducer.done(rd)
        rd = reducer.rider(carry["ffn_in"])
        dh_in, dgain = _ffn_bwd_in(dy, h_in, gains[f"{which}_norm"], dg, du, wts(f"{which}_gate", i),
                                   wts(f"{which}_up", i), i, tl, pad, f"{which}_bwd_in_{i}", rd)
        reducer.done(rd)
        return dh_in, dg, du, dgain, dyh

    for i in reversed(range(L)):
        s = saved[i]
        dh, dg, du, small["ffn2_norm"][i], dyh = ffn_bwd("ffn2", dh, s["h2"], s["g2"], s["u2"], i)
        grad("ffn2_gate", dg, s["a2"], i, "row")
        grad("ffn2_up", du, s["a2"], i, "row")
        grad("ffn2_down", s["act2"], dyh, i, "row")
        reducer.stage(f"ffn2_{i}")
        rd = reducer.rider(carry["mix_bwd"])
        dgab, dret, dpool, dr, dpm = _mix_bwd_dx(
            dh, s["z"], s["ret"], s["pool"], wts("w_out", i), wts("w_ret_up", i), wts("w_pool_up", i), tm,
            f"mix_bwd_{i}", rd)
        reducer.done(rd)
        grad("w_out", s["mixed"], dh, i, "row")
        grad("w_ret_up", s["r"], dret, i, "col")
        grad("w_pool_up", s["pm"], dpool, i, "col")
        du_pool, small["pool_maps"][i], small["pool_scale"][i] = _pool_bwd(
            s["z"], dpm, pool_maps, scale3, i, pad, f"pool_bwd_{i}")
        dq, dgr, dkp, dvp, ds = _ret_bwd_local(s["z"], s["o_pre"], s["s_all"], dr, consts, cg, f"retention_bwd_{i}")
        dk, dv = _ret_bwd_state(s["z"], dkp, dvp, ds, consts, cg, f"retention_bwd_state_{i}")
        dz = jnp.concatenate([dq, dk, dv, dgr, du_pool, dgab], axis=1)
        dh2 = dh
        rd = reducer.rider(carry["inproj_bwd"])
        dh, small["mix_norm"][i] = _inproj_bwd_dx(
            dz, wts("w_in", i), s["h1"], gains["mix_norm"], dh2, i, tl, pad, f"inproj_bwd_{i}", rd)
        reducer.done(rd)
        grad("w_in", s["b"], dz, i, "col")
        reducer.stage(f"mid{i}")
        dh, dg, du, small["ffn1_norm"][i], dyh = ffn_bwd("ffn1", dh, s["h0"], s["g1"], s["u1"], i)
        grad("ffn1_gate", dg, s["a1"], i, "row")
        if i == 0:
            reducer.stage("gate0")
        grad("ffn1_up", du, s["a1"], i, "row")
        if i == 0:
            reducer.stage("up0")
        grad("ffn1_down", s["act1"], dyh, i, "row")
        reducer.stage(f"end{i}")

    return loss_acc, dh, small, d_final


class _Reducer:
    def __init__(self, unit):
        self.c_idx = lax.axis_index("c").astype(jnp.int32).reshape(1)
        chip = 2 * lax.axis_index("x") + lax.axis_index("y")
        self.pos = jnp.stack([chip, lax.axis_index("c")]).astype(jnp.int32)
        self.pending, self.stages, self.queue, self.halves = [], [], [], {}
        self.unit = unit
        self.calls = 0

    def add(self, name, layer, g):
        self.pending.append(((name, layer), g))

    def stage(self, tag):
        if self.pending:
            self.stages.append((tag, self.pending))
            self.pending = []

    def _pair_rider(self):
        if not self.stages:
            return None
        tag, items = self.stages.pop(0)
        rd = _pair_exchange_rider([g for _, g in items])
        rd.tag, rd.keys = tag, [k for k, _ in items]
        return rd

    def _chip_rider(self, units):
        take, size = [], 0
        while self.queue and (units is None or size + self.queue[0][1].size <= units * self.unit):
            take.append(self.queue.pop(0))
            size += take[-1][1].size
        if not take:
            return None
        rd = _chip_exchange_rider([p for _, p in take])
        rd.keys = [k for k, _ in take]
        return rd

    def rider(self, units):
        self.riding = (self._pair_rider(), self._chip_rider(units))
        return _join(self.riding)

    def done(self, rd):
        if rd is None:
            return
        _split_results(rd)
        pair, chips = self.riding
        if len([r for r in self.riding if r is not None]) == 1:
            (pair or chips).results = rd.results
        self.calls += 1
        if pair is not None:
            sums = _sum_pair(pair.ins, pair.results, self.c_idx, f"sum_pair_{pair.tag}")
            self.queue += list(zip(pair.keys, sums))
        if chips is not None:
            sums = _sum_chips(chips.ins, chips.results, self.pos, f"sum_chips_{self.calls}")
            self.halves.update(zip(chips.keys, sums))

    def finish(self):
        assert not self.pending
        while self.stages or self.queue:
            self.riding = (self._pair_rider(), self._chip_rider(None))
            rd = _join(self.riding)
            _run_rider(rd, f"grads_exchange_tail_{self.calls}")
            self.done(rd)
        keys = list(self.halves)
        return dict(zip(keys, _pair_gather([self.halves[k] for k in keys])))


def _update(loss, grad_x, d_meta_rows, shard_grads, small, d_final, w, mom, var):
    meta = w["meta"]
    D = w["final_norm"].shape[0]
    L = w["ffn1_norm"].shape[0]
    Dq = D // N_CHIPS

    small_parts = [jnp.concatenate(small[n], axis=0) for n in ("ffn1_norm", "mix_norm", "ffn2_norm")]
    small_parts += [d_final, jnp.concatenate(small["pool_scale"], axis=0), jnp.concatenate(small["pool_maps"], axis=0)]
    reduced = _small_all_reduce(_pack_rows(small_parts + [d_meta_rows], D))
    small_shapes = [w[n].shape for n in _SMALL]
    small_rows = sum(math.prod(shp) for shp in small_shapes) // D
    chip = 2 * lax.axis_index("x") + lax.axis_index("y")
    d_meta = lax.dynamic_slice_in_dim(reduced[small_rows:small_rows + N_META], chip * Dq, Dq, axis=1)

    out = {}
    for n in _BIG:
        gs = [shard_grads[(n, i)] for i in range(L)]
        if n in _TRANSPOSED:
            res = _adamw(gs, *(jnp.swapaxes(t[n], 1, 2) for t in (w, mom, var)), f"adamw_{n}")
            out[n] = [jnp.swapaxes(r, 1, 2) for r in res]
        else:
            out[n] = _adamw(gs, w[n], mom[n], var[n], f"adamw_{n}")
    names = _SMALL + ("meta",)
    packed_g = _pack_rows([reduced[:small_rows], d_meta], D)
    packed = [_pack_rows([t[n] for n in names], D) for t in (w, mom, var)]
    res = _adamw([packed_g], packed[0][None], packed[1][None], packed[2][None], "adamw_small")
    shapes = small_shapes + [meta.shape]
    unpacked = [_unpack_rows(r[0], shapes, D) for r in res]
    for k, n in enumerate(names):
        out[n] = tuple(u[k] for u in unpacked)

    return (loss, grad_x) + tuple(out[n][j] for j in range(4) for n in _ORDER)


def kernel(x, meta, ffn1_norm, ffn1_gate, ffn1_up, ffn1_down, mix_norm, w_in, pool_maps, pool_scale, w_ret_up, w_pool_up, w_out, ffn2_norm, ffn2_gate, ffn2_up, ffn2_down, final_norm, loss_target, m_meta, m_ffn1_norm, m_ffn1_gate, m_ffn1_up, m_ffn1_down, m_mix_norm, m_w_in, m_pool_maps, m_pool_scale, m_w_ret_up, m_w_pool_up, m_w_out, m_ffn2_norm, m_ffn2_gate, m_ffn2_up, m_ffn2_down, m_final_norm, v_meta, v_ffn1_norm, v_ffn1_gate, v_ffn1_up, v_ffn1_down, v_mix_norm, v_w_in, v_pool_maps, v_pool_scale, v_w_ret_up, v_w_pool_up, v_w_out, v_ffn2_norm, v_ffn2_gate, v_ffn2_up, v_ffn2_down, v_final_norm):
    args = dict(locals())
    w = {n: args[n] for n in _ORDER}
    mom = {n: args["m_" + n] for n in _ORDER}
    var = {n: args["v_" + n] for n in _ORDER}

    assert x.shape[0] == 1, "one batch element per device"
    seq, D = x.shape[1], x.shape[2]
    assert seq % CHUNK == 0 and D % RET_WIDTH == 0 and (2 * POOL_WIDTH) % D == 0
    pad = (-(seq + N_META)) % CHUNK
    T = seq + N_META + pad
    tm = _pick_tile(T, 528, BF16_ROWS)
    cg = _pick_tile(T // CHUNK, 11, 1)

    shards = {n: _transport(w[n]) for n in _BIG}
    shards["meta"] = meta[None]
    wts = _Weights(shards)
    head = wts.rider([(n, 0) for n in _FFN1] + [("meta", 0)])
    _run_rider(head, "weights_gather_head")
    wts.take(head)
    meta_full = jnp.transpose(wts("meta", 0), (1, 0, 2)).reshape(N_META, D)

    reducer = _Reducer(unit=2 * shards["ffn1_gate"][0].size)
    loss_acc, dh, small, d_final = _local_step(x[0], meta_full, loss_target[0], w, wts, pad, tm, cg, reducer)
    loss = lax.psum(loss_acc[0, 0], ("x", "y", "c"))
    grad_x = dh[pad + N_META:][None]
    return _update(loss, grad_x, dh[pad:pad + N_META], reducer.finish(), small, d_final, w, mom, var)
```

```python
import functools
import math

import jax
import jax.numpy as jnp
from jax import lax
from jax.experimental import pallas as pl
from jax.experimental.pallas import tpu as pltpu

F32 = jnp.float32
BF16 = jnp.bfloat16

N_META = 16
RET_HEADS = 4
HEAD_DIM = 128
RET_WIDTH = RET_HEADS * HEAD_DIM
POOL_WINDOWS = (2, 4, 8, 16)
POOL_GROUPS = len(POOL_WINDOWS)
POOL_WIDTH = POOL_GROUPS * HEAD_DIM
CHUNK = 128
ROPE_BASE = 10000.0
EPS = 1e-6
ADAM_LR = 0.001
ADAM_B1 = 0.9
ADAM_B2 = 0.999
ADAM_EPS = 1e-08
ADAM_WD = 0.01
ADAM_STEP = 10

N_CHIPS = 4
LANES = 128
BF16_ROWS = 16
V7X_VMEM_LIMIT = 52 * 1024 * 1024
MESH = pl.DeviceIdType.MESH
ANY = pl.BlockSpec(memory_space=pl.ANY)


def _round_up(n, m):
    return -(-n // m) * m


def _pick_tile(n, target, mult):
    best = None
    for d in range(mult, min(n, target) + 1, mult):
        if n % d == 0:
            best = d
    assert best is not None, (n, target, mult)
    return best


def _params(sem=None):
    return pltpu.CompilerParams(dimension_semantics=sem, vmem_limit_bytes=V7X_VMEM_LIMIT)


def _dot(a, b):
    return jnp.dot(a, b, preferred_element_type=F32)


def _dot_nt(a, b):
    return lax.dot_general(a, b, (((1,), (1,)), ((), ())), preferred_element_type=F32)


def _dot_tn(a, b):
    return lax.dot_general(a, b, (((0,), (0,)), ((), ())), preferred_element_type=F32)


def _ein(spec, a, b):
    return jnp.einsum(spec, a, b, preferred_element_type=F32)


def _sigmoid(x):
    return jax.nn.sigmoid(x)


def _rms_fwd(x, gain):
    r = lax.rsqrt(jnp.mean(x * x, axis=-1, keepdims=True) + EPS)
    return x * r * gain


def _rms_bwd(x, gain, da):
    r = lax.rsqrt(jnp.mean(x * x, axis=-1, keepdims=True) + EPS)
    xh = x * r
    dgain = jnp.sum(da * xh, axis=0, keepdims=True)
    dxh = da * gain
    dx = r * (dxh - xh * jnp.mean(dxh * xh, axis=-1, keepdims=True))
    return dx, dgain


def _row_mask(t, tm, pad, shape):
    rows = t * tm + lax.broadcasted_iota(jnp.int32, shape, 0)
    return rows >= pad


def _mesh_pos():
    x, y, c = lax.axis_index("x"), lax.axis_index("y"), lax.axis_index("c")
    others = [(1 - x, y), (x, 1 - y), (1 - x, 1 - y)]
    return x, y, c, 2 * x + y, others


def _half_rows(c, rh):
    return pl.ds(pl.multiple_of(c * rh, rh), rh)


def _remote(src, dst, ssem, rsem, dev):
    return pltpu.make_async_remote_copy(src_ref=src, dst_ref=dst, send_sem=ssem, recv_sem=rsem,
                                        device_id=dev, device_id_type=MESH)


class _Rider:
    def __init__(self, ins, out_shapes, n_sem, start, finish):
        self.ins, self.out_shapes, self.n_sem, self.start, self.finish = ins, out_shapes, n_sem, start, finish
        self.results = None


class _SemWindow:
    def __init__(self, ref, base):
        self.ref, self.base = ref, base

    @property
    def at(self):
        return self

    def __getitem__(self, k):
        return self.ref.at[self.base + k]


def _join(riders):
    riders = [r for r in riders if r is not None]
    if len(riders) <= 1:
        return riders[0] if riders else None

    def run(which):
        def go(ins, outs, ssem, rsem):
            at, sem = 0, 0
            for r in riders:
                n = len(r.ins)
                getattr(r, which)(ins[at:at + n], outs[at:at + n], _SemWindow(ssem, sem), _SemWindow(rsem, sem))
                at, sem = at + n, sem + r.n_sem
        return go

    joined = _Rider(sum([list(r.ins) for r in riders], []), sum([list(r.out_shapes) for r in riders], []),
                    sum(r.n_sem for r in riders), run("start"), run("finish"))
    joined.parts = riders
    return joined


def _split_results(rider):
    at = 0
    for r in getattr(rider, "parts", []):
        r.results = rider.results[at:at + len(r.ins)]
        at += len(r.ins)


def _gather_rider(pieces):
    per = 7
    layers = [layer for _, layer in pieces]

    def first_copies(ins, outs, ssem, rsem):
        x, y, c, chip, others = _mesh_pos()
        copies = []
        for i, layer in enumerate(layers):
            mine = _half_rows(c, ins[i].shape[1] // 2)
            for j, (ox, oy) in enumerate(others):
                copies.append(_remote(ins[i].at[layer, mine, :], outs[i].at[chip, mine, :],
                                      ssem.at[per * i + j], rsem.at[per * i + j], (ox, oy, c)))
            copies.append(_remote(ins[i].at[layer], outs[i].at[chip],
                                  ssem.at[per * i + 6], rsem.at[per * i + 6], (x, y, 1 - c)))
        return copies

    def start(ins, outs, ssem, rsem):
        for cp in first_copies(ins, outs, ssem, rsem):
            cp.start()

    def finish(ins, outs, ssem, rsem):
        x, y, c, chip, others = _mesh_pos()
        sibling = (x, y, 1 - c)
        forwards = []
        for i in range(len(layers)):
            mine = _half_rows(c, ins[i].shape[1] // 2)
            for j, (ox, oy) in enumerate(others):
                rows = outs[i].at[2 * ox + oy, mine, :]
                _remote(rows, rows, ssem.at[per * i + j], rsem.at[per * i + j], (ox, oy, c)).wait_recv()
                fwd = _remote(rows, rows, ssem.at[per * i + 3 + j], rsem.at[per * i + 3 + j], sibling)
                fwd.start()
                forwards.append(fwd)
        for i in range(len(layers)):
            theirs = _half_rows(1 - c, ins[i].shape[1] // 2)
            for j, (ox, oy) in enumerate(others):
                rows = outs[i].at[2 * ox + oy, theirs, :]
                _remote(rows, rows, ssem.at[per * i + 3 + j], rsem.at[per * i + 3 + j], sibling).wait_recv()
            own = outs[i].at[chip]
            _remote(own, own, ssem.at[per * i + 6], rsem.at[per * i + 6], sibling).wait_recv()
        for cp in first_copies(ins, outs, ssem, rsem) + forwards:
            cp.wait_send()

    shapes = [jax.ShapeDtypeStruct((N_CHIPS,) + s.shape[1:], s.dtype) for s, _ in pieces]
    return _Rider([s for s, _ in pieces], shapes, per * len(pieces), start, finish)


def _chip_exchange_rider(ps):
    def copies(ins, outs, ssem, rsem):
        x, y, c, chip, others = _mesh_pos()
        return [_remote(ins[i].at[2 * ox + oy], outs[i].at[chip], ssem.at[3 * i + j], rsem.at[3 * i + j], (ox, oy, c))
                for i in range(len(ps)) for j, (ox, oy) in enumerate(others)]

    def start(ins, outs, ssem, rsem):
        for cp in copies(ins, outs, ssem, rsem):
            cp.start()

    def finish(ins, outs, ssem, rsem):
        x, y, c, chip, others = _mesh_pos()
        for i in range(len(ps)):
            for j, (ox, oy) in enumerate(others):
                slot = outs[i].at[2 * ox + oy]
                _remote(slot, slot, ssem.at[3 * i + j], rsem.at[3 * i + j], (ox, oy, c)).wait_recv()
        for cp in copies(ins, outs, ssem, rsem):
            cp.wait_send()

    return _Rider(list(ps), [jax.ShapeDtypeStruct(p.shape, p.dtype) for p in ps], 3 * len(ps), start, finish)


def _pair_exchange_rider(gs):
    def copies(ins, outs, ssem, rsem):
        x, y, c, _, _ = _mesh_pos()
        return [_remote(ins[i].at[:, _half_rows(1 - c, ins[i].shape[1] // 2), :], outs[i],
                        ssem.at[i], rsem.at[i], (x, y, 1 - c)) for i in range(len(gs))]

    def start(ins, outs, ssem, rsem):
        for cp in copies(ins, outs, ssem, rsem):
            cp.start()

    def finish(ins, outs, ssem, rsem):
        for cp in copies(ins, outs, ssem, rsem):
            cp.wait()

    shapes = [jax.ShapeDtypeStruct((g.shape[0], g.shape[1] // 2, g.shape[2]), g.dtype) for g in gs]
    return _Rider(list(gs), shapes, len(gs), start, finish)


def _run_rider(rider, name):
    def body(*refs):
        n = len(rider.ins)
        ins, outs = refs[:n], refs[n:2 * n]
        ssem, rsem = refs[2 * n:]
        rider.start(ins, outs, ssem, rsem)
        rider.finish(ins, outs, ssem, rsem)

    rider.results = pl.pallas_call(
        body,
        name=name,
        in_specs=[ANY] * len(rider.ins),
        out_specs=[ANY] * len(rider.ins),
        out_shape=rider.out_shapes,
        scratch_shapes=[pltpu.SemaphoreType.DMA((rider.n_sem,)), pltpu.SemaphoreType.DMA((rider.n_sem,))],
    )(*rider.ins)
    return rider.results


def _pair_gather(fs):
    n = len(fs)

    def body(*refs):
        bufs = refs[n:2 * n]
        ssem, rsem = refs[2 * n:]
        x, y, c, _, _ = _mesh_pos()
        sends = []
        for i in range(n):
            rh = bufs[i].shape[0] // 2
            mine = bufs[i].at[_half_rows(c, rh), :]
            cp = _remote(mine, mine, ssem.at[i], rsem.at[i], (x, y, 1 - c))
            cp.start()
            sends.append(cp)
        for i in range(n):
            rh = bufs[i].shape[0] // 2
            theirs = bufs[i].at[_half_rows(1 - c, rh), :]
            _remote(theirs, theirs, ssem.at[i], rsem.at[i], (x, y, 1 - c)).wait_recv()
        for cp in sends:
            cp.wait_send()

    return pl.pallas_call(
        body,
        name="grads_pair_gather",
        in_specs=[ANY] * n,
        out_specs=[ANY] * n,
        out_shape=[jax.ShapeDtypeStruct(f.shape, f.dtype) for f in fs],
        input_output_aliases={i: i for i in range(n)},
        scratch_shapes=[pltpu.SemaphoreType.DMA((n,)), pltpu.SemaphoreType.DMA((n,))],
    )(*fs)


def _call(body, *, name, grid, in_specs, out_specs, out_shape, operands, scratch=(), sem=None, rider=None):
    if rider is None:
        return pl.pallas_call(
            body, name=name, grid=grid, in_specs=in_specs, out_specs=out_specs, out_shape=out_shape,
            scratch_shapes=list(scratch), compiler_params=_params(sem))(*operands)
    n_in, n_out, n_sc, r = len(in_specs), len(out_specs), len(scratch), len(rider.ins)

    def carrying(*refs):
        a, b = n_in, n_in + r
        c, d = b + n_out, b + n_out + r
        e = d + n_sc
        ids = [pl.program_id(k) for k in range(len(grid))]
        first = functools.reduce(jnp.logical_and, [i == 0 for i in ids])
        last = functools.reduce(jnp.logical_and, [i == g - 1 for i, g in zip(ids, grid)])

        @pl.when(first)
        def _():
            rider.start(refs[a:b], refs[c:d], refs[e], refs[e + 1])

        body(*refs[:a], *refs[b:c], *refs[d:e])

        @pl.when(last)
        def _():
            rider.finish(refs[a:b], refs[c:d], refs[e], refs[e + 1])

    outs = pl.pallas_call(
        carrying, name=name, grid=grid,
        in_specs=list(in_specs) + [ANY] * r,
        out_specs=list(out_specs) + [ANY] * r,
        out_shape=list(out_shape) + list(rider.out_shapes),
        scratch_shapes=list(scratch) + [pltpu.SemaphoreType.DMA((rider.n_sem,)), pltpu.SemaphoreType.DMA((rider.n_sem,))],
        compiler_params=_params(("arbitrary",) * len(grid)),
    )(*operands, *rider.ins)
    rider.results = outs[n_out:]
    return outs[:n_out]


def _ffn_fwd(h, gain, wg, wu, wd, layer, tm, name, rider=None):
    T, D = h.shape
    Fs = wg.shape[-1]
    F = N_CHIPS * Fs

    def body(h_ref, g_ref, wg_ref, wu_ref, wd_ref, ho_ref, a_ref, go_ref, uo_ref, act_ref, acc_ref):
        s = pl.program_id(1)

        @pl.when(s == 0)
        def _():
            a_ref[...] = _rms_fwd(h_ref[...], g_ref[...]).astype(BF16)
            acc_ref[...] = jnp.zeros_like(acc_ref)

        a = a_ref[...]
        g = _dot(a, wg_ref[...])
        u = _dot(a, wu_ref[...])
        sg = _sigmoid(g)
        act = (g * sg * u).astype(BF16)
        go_ref[...] = (u * (sg * (1.0 + g * (1.0 - sg)))).astype(BF16)
        uo_ref[...] = (g * sg).astype(BF16)
        act_ref[...] = act
        acc_ref[...] += _dot(act, wd_ref[...])

        @pl.when(s == N_CHIPS - 1)
        def _():
            ho_ref[...] = h_ref[...] + 0.5 * acc_ref[...]

    row = pl.BlockSpec((tm, D), lambda t, s: (t, 0))
    col = pl.BlockSpec((tm, Fs), lambda t, s: (t, s))
    wcol = pl.BlockSpec((None, D, Fs), lambda t, s: (s, 0, 0))
    return _call(
        body, name=name, grid=(T // tm, N_CHIPS),
        in_specs=[row, pl.BlockSpec((None, 1, D), lambda t, s: (layer, 0, 0)), wcol, wcol,
                  pl.BlockSpec((None, Fs, D), lambda t, s: (s, 0, 0))],
        out_specs=[row, row, col, col, col],
        out_shape=[jax.ShapeDtypeStruct((T, D), F32), jax.ShapeDtypeStruct((T, D), BF16)]
        + [jax.ShapeDtypeStruct((T, F), BF16)] * 3,
        scratch=[pltpu.VMEM((tm, D), F32)],
        sem=("parallel", "arbitrary"), operands=(h, gain, wg, wu, wd), rider=rider)


def _inproj_fwd(h, gain, win, layer, tm, name, rider=None):
    T, D = h.shape
    Ns = win.shape[-1]

    def body(h_ref, g_ref, w_ref, z_ref, b_ref):
        @pl.when(pl.program_id(1) == 0)
        def _():
            b_ref[...] = _rms_fwd(h_ref[...], g_ref[...]).astype(BF16)

        z_ref[...] = _dot(b_ref[...], w_ref[...]).astype(BF16)

    return _call(
        body, name=name, grid=(T // tm, N_CHIPS),
        in_specs=[pl.BlockSpec((tm, D), lambda t, s: (t, 0)),
                  pl.BlockSpec((None, 1, D), lambda t, s: (layer, 0, 0)),
                  pl.BlockSpec((None, D, Ns), lambda t, s: (s, 0, 0))],
        out_specs=[pl.BlockSpec((tm, Ns), lambda t, s: (t, s)), pl.BlockSpec((tm, D), lambda t, s: (t, 0))],
        out_shape=[jax.ShapeDtypeStruct((T, N_CHIPS * Ns), BF16), jax.ShapeDtypeStruct((T, D), BF16)],
        sem=("parallel", "arbitrary"), operands=(h, gain, win), rider=rider)


def _ret_consts(T, pad):
    half = HEAD_DIM // 2
    inv_freq = ROPE_BASE ** (-jnp.arange(half, dtype=F32) / half)
    pos = jnp.arange(T, dtype=F32) - pad
    ang = pos[:, None] * inv_freq[None, :]
    cos = jnp.cos(ang)
    sin = jnp.sin(ang)
    cosf = jnp.concatenate([cos, cos], axis=1)
    sinf = jnp.concatenate([-sin, sin], axis=1)
    log_gamma = jnp.log1p(-(2.0 ** (-5.0 - jnp.arange(RET_HEADS, dtype=F32))))
    idx = jnp.arange(CHUNK, dtype=F32)
    diff = idx[:, None] - idx[None, :]
    intra = jnp.where(diff[None] >= 0, jnp.exp(diff[None] * log_gamma[:, None, None]), 0.0)
    k_decay = jnp.exp((CHUNK - 1.0 - idx)[None, :] * log_gamma[:, None])
    q_decay = jnp.exp((idx + 1.0)[None, :] * log_gamma[:, None])
    chunk_decay = jnp.exp(CHUNK * log_gamma)
    kdec = jnp.broadcast_to(k_decay[:, :, None], (RET_HEADS, CHUNK, HEAD_DIM))
    qdec = jnp.broadcast_to(q_decay[:, :, None], (RET_HEADS, CHUNK, HEAD_DIM))
    cdb = jnp.broadcast_to(chunk_decay[:, None, None], (RET_HEADS, 8, HEAD_DIM))
    return cosf, sinf, intra, kdec, qdec, cdb


def _rot(t, cosv, sinv):
    return t * cosv + pltpu.roll(t, HEAD_DIM // 2, 1) * sinv


def _rot_t(g, cosv, sinv):
    return g * cosv + pltpu.roll(g * sinv, HEAD_DIM // 2, 1)


def _head_specs(tg, section, order):
    return pl.BlockSpec((tg, HEAD_DIM), lambda h, g: (order(g), section * RET_HEADS + h))


def _ret_fwd(z, consts, cg, name, rider=None):
    T = z.shape[0]
    N = T // CHUNK
    ng = N // cg
    tg = cg * CHUNK
    cosf, sinf, intra, kdec, qdec, cdb = consts
    fwd = lambda g: g

    def body(zq, zk, zv, zg, cos_ref, sin_ref, m_ref, kd_ref, qd_ref, cd_ref, r_ref, o_ref, s_ref, st_ref):
        @pl.when(pl.program_id(1) == 0)
        def _():
            st_ref[...] = jnp.zeros_like(st_ref)

        cosv = cos_ref[...]
        sinv = sin_ref[...]
        q3 = (_rot(zq[...].astype(F32), cosv, sinv) * (HEAD_DIM ** -0.5)).reshape(cg, CHUNK, HEAD_DIM)
        k3 = _rot(zk[...].astype(F32), cosv, sinv).reshape(cg, CHUNK, HEAD_DIM)
        vb = zv[...].reshape(cg, CHUNK, HEAD_DIM).astype(BF16)
        scores = _ein("ncd,nmd->ncm", q3.astype(BF16), k3.astype(BF16)) * m_ref[...][None]
        inner = _ein("ncm,nmd->ncd", scores.astype(BF16), vb)
        kv = _ein("ncd,nce->nde", (k3 * kd_ref[...][None]).astype(BF16), vb)
        cd = cd_ref[0:1, :]
        state = st_ref[...]
        for n in range(cg):
            s_ref[n] = state
            state = state * cd + kv[n]
        st_ref[...] = state
        qdb = (q3 * qd_ref[...][None]).astype(BF16)
        cross = _ein("ncd,nde->nce", qdb, s_ref[...].astype(BF16))
        out = (inner + cross).reshape(tg, HEAD_DIM)
        o_ref[...] = out
        xc = out - jnp.mean(out, axis=-1, keepdims=True)
        rn = xc * lax.rsqrt(jnp.mean(xc * xc, axis=-1, keepdims=True) + EPS)
        g = zg[...].astype(F32)
        r_ref[...] = (rn * (g * _sigmoid(g))).astype(BF16)

    tab = pl.BlockSpec((tg, HEAD_DIM), lambda h, g: (g, 0))
    per_head = lambda rows: pl.BlockSpec((None, rows, HEAD_DIM), lambda h, g: (h, 0, 0))
    head_out = pl.BlockSpec((tg, HEAD_DIM), lambda h, g: (g, h))
    return _call(
        body, name=name, grid=(RET_HEADS, ng),
        in_specs=[_head_specs(tg, i, fwd) for i in range(4)]
        + [tab, tab, per_head(CHUNK), per_head(CHUNK), per_head(CHUNK), per_head(8)],
        out_specs=[head_out, head_out, pl.BlockSpec((None, cg, HEAD_DIM, HEAD_DIM), lambda h, g: (h, g, 0, 0))],
        out_shape=[jax.ShapeDtypeStruct((T, RET_WIDTH), BF16), jax.ShapeDtypeStruct((T, RET_WIDTH), F32),
                   jax.ShapeDtypeStruct((RET_HEADS, N, HEAD_DIM, HEAD_DIM), F32)],
        scratch=[pltpu.VMEM((HEAD_DIM, HEAD_DIM), F32)],
        sem=("parallel", "arbitrary"), operands=(z, z, z, z, cosf, sinf, intra, kdec, qdec, cdb), rider=rider)


def _window_sums(u, shift_of):
    sums = []
    s = u
    k = 1
    while k < POOL_WINDOWS[-1]:
        s = s + pltpu.roll(s, shift_of(k), 0)
        sums.append(s)
        k *= 2
    return sums


def _select_group(vals, g):
    out = vals[-1]
    for i in range(len(vals) - 2, -1, -1):
        out = jnp.where(g == i, vals[i], out)
    return out


def _pool_parts(u, g, T, pad):
    rows = lax.broadcasted_iota(jnp.int32, (T, HEAD_DIM), 0)
    valid = rows >= pad
    win = _select_group([float(w) for w in POOL_WINDOWS], g)
    div = jnp.clip((rows - pad + 1).astype(F32), 1.0, win)
    s = _select_group(_window_sums(u, lambda k: k), g)
    pooled = jnp.where(valid, s / div - u, 0.0)
    return pooled, div, valid


def _pool_specs(T, layer):
    first = 4 * RET_WIDTH // HEAD_DIM
    return [
        pl.BlockSpec((T, HEAD_DIM), lambda g: (0, first + g)),
        pl.BlockSpec((None, None, HEAD_DIM, HEAD_DIM), lambda g: (layer, g, 0, 0)),
        pl.BlockSpec((None, 1, HEAD_DIM), lambda g: (layer, 0, g)),
    ]


def _pool_fwd(z, maps, scale, layer, pad, name):
    T = z.shape[0]
    assert pad >= POOL_WINDOWS[-1], "window rolls wrap into the zero rows in front"

    def body(zu, maps_ref, sc_ref, pm_ref):
        g = pl.program_id(0)
        pooled, _, _ = _pool_parts(zu[...].astype(F32), g, T, pad)
        y = _dot(pooled.astype(BF16), maps_ref[...].astype(BF16))
        pm_ref[...] = (y * sc_ref[...]).astype(BF16)

    return _call(
        body, name=name, grid=(POOL_GROUPS,),
        in_specs=_pool_specs(T, layer),
        out_specs=[pl.BlockSpec((T, HEAD_DIM), lambda g: (0, g))],
        out_shape=[jax.ShapeDtypeStruct((T, POOL_WIDTH), BF16)],
        sem=("parallel",), operands=(z, maps, scale))[0]


def _gate_specs(tm, D):
    nb = D // RET_WIDTH
    first = (4 * RET_WIDTH + POOL_WIDTH) // RET_WIDTH
    return [pl.BlockSpec((tm, RET_WIDTH), functools.partial(lambda t, j: (t, j), j=first + j)) for j in range(2 * nb)]


def _load_gates(refs, nb):
    ga = jnp.concatenate([r[...].astype(F32) for r in refs[:nb]], axis=1)
    gb = jnp.concatenate([r[...].astype(F32) for r in refs[nb:]], axis=1)
    return ga, gb


def _mix_fwd(h, r, pm, z, wru, wpu, wout, tm, name, rider=None):
    T, D = h.shape
    Dq = D // N_CHIPS
    nb = D // RET_WIDTH

    def body(*refs):
        h_ref, r_ref, pm_ref = refs[:3]
        gate_refs = refs[3:3 + 2 * nb]
        wru_ref, wpu_ref, wout_ref, ho_ref, mx_ref, ret_ref, pool_ref = refs[3 + 2 * nb:]
        rv = r_ref[...]
        pv = pm_ref[...]
        ret = jnp.concatenate([_dot(rv, wru_ref[s]) for s in range(N_CHIPS)], axis=1)
        pool = jnp.concatenate([_dot(pv, wpu_ref[s]) for s in range(N_CHIPS)], axis=1)
        ga, gb = _load_gates(gate_refs, nb)
        mixed = (_sigmoid(ga) * ret + _sigmoid(gb) * pool).astype(BF16)
        mx_ref[...] = mixed
        ret_ref[...] = ret.astype(BF16)
        pool_ref[...] = pool.astype(BF16)
        ho_ref[...] = h_ref[...] + _dot(mixed, wout_ref[...].reshape(D, D))

    row = pl.BlockSpec((tm, D), lambda t: (t, 0))
    half = pl.BlockSpec((tm, RET_WIDTH), lambda t: (t, 0))
    up = pl.BlockSpec((N_CHIPS, RET_WIDTH, Dq), lambda t: (0, 0, 0))
    return _call(
        body, name=name, grid=(T // tm,),
        in_specs=[row, half, half] + _gate_specs(tm, D) + [up, up, pl.BlockSpec((N_CHIPS, Dq, D), lambda t: (0, 0, 0))],
        out_specs=[row, row, row, row],
        out_shape=[jax.ShapeDtypeStruct((T, D), F32)] + [jax.ShapeDtypeStruct((T, D), BF16)] * 3,
        sem=("parallel",), operands=(h, r, pm, *([z] * (2 * nb)), wru, wpu, wout), rider=rider)


def _final_loss(h, gain, tgt, name):
    T, D = h.shape
    first = (T - tgt.shape[0]) // CHUNK

    def body(h_ref, g_ref, t_ref, dh_ref, loss_ref, dg_ref):
        i = pl.program_id(0)

        @pl.when(i == 0)
        def _():
            loss_ref[...] = jnp.zeros_like(loss_ref)
            dg_ref[...] = jnp.zeros_like(dg_ref)

        x = h_ref[...]
        gain_v = g_ref[...]
        err = jnp.where(i >= first, _rms_fwd(x, gain_v) - t_ref[...], 0.0)
        loss_ref[...] += 0.5 * jnp.sum(jnp.mean(err * err, axis=-1))
        dx, dgain = _rms_bwd(x, gain_v, err * (1.0 / D))
        dg_ref[...] += dgain
        dh_ref[...] = dx

    return _call(
        body, name=name, grid=(T // CHUNK,),
        in_specs=[pl.BlockSpec((CHUNK, D), lambda i: (i, 0)),
                  pl.BlockSpec((1, D), lambda i: (0, 0)),
                  pl.BlockSpec((CHUNK, D), lambda i: (jnp.maximum(i - first, 0), 0))],
        out_specs=[pl.BlockSpec((CHUNK, D), lambda i: (i, 0)),
                   pl.BlockSpec((1, LANES), lambda i: (0, 0)),
                   pl.BlockSpec((1, D), lambda i: (0, 0))],
        out_shape=[jax.ShapeDtypeStruct((T, D), F32), jax.ShapeDtypeStruct((1, LANES), F32),
                   jax.ShapeDtypeStruct((1, D), F32)],
        sem=("arbitrary",), operands=(h, gain, tgt))


def _ffn_bwd_act(dy, g, u, wd, tm, name, rider=None):
    T, D = dy.shape
    Fs = wd.shape[1]
    F = N_CHIPS * Fs

    def body(dy_ref, go_ref, uo_ref, wd_ref, dg_ref, du_ref, dyh_ref):
        @pl.when(pl.program_id(1) == 0)
        def _():
            dyh_ref[...] = (0.5 * dy_ref[...]).astype(BF16)

        dact = _dot_nt(dyh_ref[...], wd_ref[...])
        du_ref[...] = (dact * uo_ref[...].astype(F32)).astype(BF16)
        dg_ref[...] = (dact * go_ref[...].astype(F32)).astype(BF16)

    row = pl.BlockSpec((tm, D), lambda t, s: (t, 0))
    col = pl.BlockSpec((tm, Fs), lambda t, s: (t, s))
    return _call(
        body, name=name, grid=(T // tm, N_CHIPS),
        in_specs=[row, col, col, pl.BlockSpec((None, Fs, D), lambda t, s: (s, 0, 0))],
        out_specs=[col, col, row],
        out_shape=[jax.ShapeDtypeStruct((T, F), BF16), jax.ShapeDtypeStruct((T, F), BF16),
                   jax.ShapeDtypeStruct((T, D), BF16)],
        sem=("parallel", "arbitrary"), operands=(dy, g, u, wd), rider=rider)


def _ffn_bwd_in(dy, h, gain, dg, du, wg, wu, layer, tm, pad, name, rider=None):
    T, D = h.shape
    Fs = wg.shape[-1]

    def body(dy_ref, h_ref, g_ref, dg_ref, du_ref, wg_ref, wu_ref, dh_ref, dgain_ref, da_ref):
        t = pl.program_id(0)
        s = pl.program_id(1)

        @pl.when((t == 0) & (s == 0))
        def _():
            dgain_ref[...] = jnp.zeros_like(dgain_ref)

        @pl.when(s == 0)
        def _():
            da_ref[...] = jnp.zeros_like(da_ref)

        da_ref[...] += _dot_nt(dg_ref[...], wg_ref[...]) + _dot_nt(du_ref[...], wu_ref[...])

        @pl.when(s == N_CHIPS - 1)
        def _():
            dx, dgain = _rms_bwd(h_ref[...], g_ref[...], da_ref[...])
            dgain_ref[...] += dgain
            dh_ref[...] = jnp.where(_row_mask(t, tm, pad, (tm, D)), dy_ref[...] + dx, 0.0)

    row = pl.BlockSpec((tm, D), lambda t, s: (t, 0))
    col = pl.BlockSpec((tm, Fs), lambda t, s: (t, s))
    wcol = pl.BlockSpec((None, D, Fs), lambda t, s: (s, 0, 0))
    return _call(
        body, name=name, grid=(T // tm, N_CHIPS),
        in_specs=[row, row, pl.BlockSpec((None, 1, D), lambda t, s: (layer, 0, 0)), col, col, wcol, wcol],
        out_specs=[row, pl.BlockSpec((1, D), lambda t, s: (0, 0))],
        out_shape=[jax.ShapeDtypeStruct((T, D), F32), jax.ShapeDtypeStruct((1, D), F32)],
        scratch=[pltpu.VMEM((tm, D), F32)],
        sem=("arbitrary", "arbitrary"), operands=(dy, h, gain, dg, du, wg, wu), rider=rider)


def _grad_tn(a, b, mode, scale, tm, name, rider=None):
    T = a.shape[0]
    if mode == "col":
        per, R, C = 1, a.shape[1], b.shape[1] // N_CHIPS
        a_spec = pl.BlockSpec((tm, R), lambda s, t: (t, 0))
        b_spec = pl.BlockSpec((tm, C), lambda s, t: (t, s))
    else:
        per, R, C = 2, a.shape[1] // N_CHIPS, b.shape[1]
        a_spec = pl.BlockSpec((tm, per * R), lambda s, t: (t, s))
        b_spec = pl.BlockSpec((tm, C), lambda s, t: (t, 0))
    nt = T // tm

    def body(a_ref, b_ref, o_ref, acc_ref):
        t = pl.program_id(1)

        @pl.when(t == 0)
        def _():
            acc_ref[...] = jnp.zeros_like(acc_ref)

        acc_ref[...] += _dot_tn(a_ref[...].astype(BF16), b_ref[...].astype(BF16))

        @pl.when(t == nt - 1)
        def _():
            o_ref[...] = (scale * acc_ref[...]).astype(BF16).reshape(per, R, C)

    return _call(
        body, name=name, grid=(N_CHIPS // per, nt),
        in_specs=[a_spec, b_spec],
        out_specs=[pl.BlockSpec((per, R, C), lambda s, t: (s, 0, 0))],
        out_shape=[jax.ShapeDtypeStruct((N_CHIPS, R, C), BF16)],
        scratch=[pltpu.VMEM((per * R, C), F32)],
        sem=("parallel", "arbitrary"), operands=(a, b), rider=rider)[0]


def _mix_bwd_dx(dh, z, ret, pool, wout, wru, wpu, tm, name, rider=None):
    T, D = dh.shape
    Dq = D // N_CHIPS
    nb = D // RET_WIDTH

    def body(*refs):
        dh_ref = refs[0]
        gate_refs = refs[1:1 + 2 * nb]
        ret_ref, pool_ref, wout_ref, wru_ref, wpu_ref, dgab_ref, dret_ref, dpool_ref, dr_ref, dpm_ref = refs[1 + 2 * nb:]
        dmixed = _dot_nt(dh_ref[...].astype(BF16), wout_ref[...].reshape(D, D))
        ga, gb = _load_gates(gate_refs, nb)
        sa = _sigmoid(ga)
        sb = _sigmoid(gb)
        dgab_ref[:, :D] = (dmixed * ret_ref[...].astype(F32) * (sa * (1.0 - sa))).astype(BF16)
        dgab_ref[:, D:] = (dmixed * pool_ref[...].astype(F32) * (sb * (1.0 - sb))).astype(BF16)
        dret = (dmixed * sa).astype(BF16)
        dpool = (dmixed * sb).astype(BF16)
        dret_ref[...] = dret
        dpool_ref[...] = dpool
        dr = _dot_nt(dret[:, :Dq], wru_ref[0])
        dpm = _dot_nt(dpool[:, :Dq], wpu_ref[0])
        for s in range(1, N_CHIPS):
            dr += _dot_nt(dret[:, s * Dq:(s + 1) * Dq], wru_ref[s])
            dpm += _dot_nt(dpool[:, s * Dq:(s + 1) * Dq], wpu_ref[s])
        dr_ref[...] = dr
        dpm_ref[...] = dpm

    row = pl.BlockSpec((tm, D), lambda t: (t, 0))
    half = pl.BlockSpec((tm, RET_WIDTH), lambda t: (t, 0))
    up = pl.BlockSpec((N_CHIPS, RET_WIDTH, Dq), lambda t: (0, 0, 0))
    return _call(
        body, name=name, grid=(T // tm,),
        in_specs=[row] + _gate_specs(tm, D) + [row, row, pl.BlockSpec((N_CHIPS, Dq, D), lambda t: (0, 0, 0)), up, up],
        out_specs=[pl.BlockSpec((tm, 2 * D), lambda t: (t, 0)), row, row, half, half],
        out_shape=[jax.ShapeDtypeStruct((T, 2 * D), BF16), jax.ShapeDtypeStruct((T, D), BF16),
                   jax.ShapeDtypeStruct((T, D), BF16), jax.ShapeDtypeStruct((T, RET_WIDTH), F32),
                   jax.ShapeDtypeStruct((T, POOL_WIDTH), F32)],
        sem=("parallel",), operands=(dh, *([z] * (2 * nb)), ret, pool, wout, wru, wpu), rider=rider)


def _pool_bwd(z, dpm, maps, scale, layer, pad, name):
    T = z.shape[0]

    def body(zu, maps_ref, sc_ref, dpm_ref, du_ref, dmaps_ref, dsc_ref):
        g = pl.program_id(0)
        u = zu[...].astype(F32)
        pooled, div, valid = _pool_parts(u, g, T, pad)
        pb = pooled.astype(BF16)
        mb = maps_ref[...].astype(BF16)
        dp = dpm_ref[...]
        dsc_ref[...] = jnp.sum(dp * _dot(pb, mb), axis=0, keepdims=True)
        dyb = (dp * sc_ref[...]).astype(BF16)
        dmaps_ref[...] = _dot_tn(pb, dyb)
        dpooled = jnp.where(valid, _dot_nt(dyb, mb), 0.0)
        ahead = _select_group(_window_sums(dpooled / div, lambda k: T - k), g)
        du_ref[...] = jnp.where(valid, ahead - dpooled, 0.0).astype(BF16)

    blk = pl.BlockSpec((T, HEAD_DIM), lambda g: (0, g))
    return _call(
        body, name=name, grid=(POOL_GROUPS,),
        in_specs=_pool_specs(T, layer) + [blk],
        out_specs=[blk, pl.BlockSpec((None, HEAD_DIM, HEAD_DIM), lambda g: (g, 0, 0)),
                   pl.BlockSpec((1, HEAD_DIM), lambda g: (0, g))],
        out_shape=[jax.ShapeDtypeStruct((T, POOL_WIDTH), BF16),
                   jax.ShapeDtypeStruct((POOL_GROUPS, HEAD_DIM, HEAD_DIM), F32),
                   jax.ShapeDtypeStruct((1, POOL_WIDTH), F32)],
        sem=("parallel",), operands=(z, maps, scale, dpm))


def _ret_bwd_local(z, o_pre, s_all, dr, consts, cg, name):
    T = z.shape[0]
    N = T // CHUNK
    ng = N // cg
    tg = cg * CHUNK
    cosf, sinf, intra, _, qdec, _ = consts
    fwd = lambda g: g

    def body(zq, zk, zv, zg, o_ref, s_ref, dr_ref, cos_ref, sin_ref, m_ref, qd_ref,
             dq_ref, dg_ref, dk_ref, dv_ref, ds_ref):
        cosv = cos_ref[...]
        sinv = sin_ref[...]
        scale = HEAD_DIM ** -0.5
        q3 = (_rot(zq[...].astype(F32), cosv, sinv) * scale).reshape(cg, CHUNK, HEAD_DIM)
        k3 = _rot(zk[...].astype(F32), cosv, sinv).reshape(cg, CHUNK, HEAD_DIM)
        qb = q3.astype(BF16)
        kb = k3.astype(BF16)
        vb = zv[...].reshape(cg, CHUNK, HEAD_DIM).astype(BF16)
        mask = m_ref[...][None]
        sb = (_ein("ncd,nmd->ncm", qb, kb) * mask).astype(BF16)
        qdv = qd_ref[...][None]
        qdb = (q3 * qdv).astype(BF16)

        out = o_ref[...]
        xc = out - jnp.mean(out, axis=-1, keepdims=True)
        rstd = lax.rsqrt(jnp.mean(xc * xc, axis=-1, keepdims=True) + EPS)
        rn = xc * rstd
        g = zg[...].astype(F32)
        sg = _sigmoid(g)
        drv = dr_ref[...]
        dg_ref[...] = (drv * rn * (sg * (1.0 + g * (1.0 - sg)))).astype(BF16)
        drn = drv * (g * sg)
        dout = rstd * (drn - jnp.mean(drn, axis=-1, keepdims=True)
                       - rn * jnp.mean(drn * rn, axis=-1, keepdims=True))
        dob = dout.reshape(cg, CHUNK, HEAD_DIM).astype(BF16)

        dsb = (_ein("ncd,nmd->ncm", dob, vb) * mask).astype(BF16)
        dv_ref[...] = _ein("ncm,ncd->nmd", sb, dob).reshape(tg, HEAD_DIM)
        dk_ref[...] = _ein("ncm,ncd->nmd", dsb, qb).reshape(tg, HEAD_DIM)
        dq3 = _ein("ncm,nmd->ncd", dsb, kb) + _ein("nce,nde->ncd", dob, s_ref[...].astype(BF16)) * qdv
        dq_ref[...] = _rot_t(dq3.reshape(tg, HEAD_DIM) * scale, cosv, sinv).astype(BF16)
        ds_ref[...] = _ein("ncd,nce->nde", qdb, dob)

    tab = pl.BlockSpec((tg, HEAD_DIM), lambda h, g: (g, 0))
    per_head = pl.BlockSpec((None, CHUNK, HEAD_DIM), lambda h, g: (h, 0, 0))
    head_blk = pl.BlockSpec((tg, HEAD_DIM), lambda h, g: (g, h))
    state_blk = pl.BlockSpec((None, cg, HEAD_DIM, HEAD_DIM), lambda h, g: (h, g, 0, 0))
    return _call(
        body, name=name, grid=(RET_HEADS, ng),
        in_specs=[_head_specs(tg, i, fwd) for i in range(4)]
        + [head_blk, state_blk, head_blk, tab, tab, per_head, per_head],
        out_specs=[head_blk, head_blk, head_blk, head_blk, state_blk],
        out_shape=[jax.ShapeDtypeStruct((T, RET_WIDTH), BF16), jax.ShapeDtypeStruct((T, RET_WIDTH), BF16),
                   jax.ShapeDtypeStruct((T, RET_WIDTH), F32), jax.ShapeDtypeStruct((T, RET_WIDTH), F32),
                   jax.ShapeDtypeStruct((RET_HEADS, N, HEAD_DIM, HEAD_DIM), F32)],
        sem=("parallel", "parallel"), operands=(z, z, z, z, o_pre, s_all, dr, cosf, sinf, intra, qdec))


def _ret_bwd_state(z, dkp, dvp, ds, consts, cg, name):
    T = z.shape[0]
    N = T // CHUNK
    ng = N // cg
    tg = cg * CHUNK
    cosf, sinf, _, kdec, _, cdb = consts
    rev = lambda g: ng - 1 - g

    def body(zk, zv, dkp_ref, dvp_ref, ds_ref, cos_ref, sin_ref, kd_ref, cd_ref, dk_ref, dv_ref, gs_ref, dkv_ref):
        @pl.when(pl.program_id(1) == 0)
        def _():
            gs_ref[...] = jnp.zeros_like(gs_ref)

        cosv = cos_ref[...]
        sinv = sin_ref[...]
        cd = cd_ref[0:1, :]
        grad = gs_ref[...]
        for n in reversed(range(cg)):
            dkv_ref[n] = grad
            grad = ds_ref[n] + cd * grad
        gs_ref[...] = grad
        dkvb = dkv_ref[...].astype(BF16)
        kdv = kd_ref[...][None]
        k3 = _rot(zk[...].astype(F32), cosv, sinv).reshape(cg, CHUNK, HEAD_DIM)
        vb = zv[...].reshape(cg, CHUNK, HEAD_DIM).astype(BF16)
        dk3 = _ein("nce,nde->ncd", vb, dkvb) * kdv
        dv3 = _ein("ncd,nde->nce", (k3 * kdv).astype(BF16), dkvb)
        dk_ref[...] = _rot_t(dkp_ref[...] + dk3.reshape(tg, HEAD_DIM), cosv, sinv).astype(BF16)
        dv_ref[...] = (dvp_ref[...] + dv3.reshape(tg, HEAD_DIM)).astype(BF16)

    tab = pl.BlockSpec((tg, HEAD_DIM), lambda h, g: (rev(g), 0))
    head_blk = pl.BlockSpec((tg, HEAD_DIM), lambda h, g: (rev(g), h))
    return _call(
        body, name=name, grid=(RET_HEADS, ng),
        in_specs=[_head_specs(tg, 1, rev), _head_specs(tg, 2, rev), head_blk, head_blk,
                  pl.BlockSpec((None, cg, HEAD_DIM, HEAD_DIM), lambda h, g: (h, rev(g), 0, 0)),
                  tab, tab,
                  pl.BlockSpec((None, CHUNK, HEAD_DIM), lambda h, g: (h, 0, 0)),
                  pl.BlockSpec((None, 8, HEAD_DIM), lambda h, g: (h, 0, 0))],
        out_specs=[head_blk, head_blk],
        out_shape=[jax.ShapeDtypeStruct((T, RET_WIDTH), BF16)] * 2,
        scratch=[pltpu.VMEM((HEAD_DIM, HEAD_DIM), F32), pltpu.VMEM((cg, HEAD_DIM, HEAD_DIM), F32)],
        sem=("parallel", "arbitrary"), operands=(z, z, dkp, dvp, ds, cosf, sinf, kdec, cdb))


def _inproj_bwd_dx(dz, win, h, gain, dh_in, layer, tm, pad, name, rider=None):
    T, D = h.shape
    Ns = win.shape[-1]

    def body(dz_ref, w_ref, h_ref, g_ref, dhi_ref, dh_ref, dgain_ref, db_ref):
        t = pl.program_id(0)
        s = pl.program_id(1)

        @pl.when((t == 0) & (s == 0))
        def _():
            dgain_ref[...] = jnp.zeros_like(dgain_ref)

        @pl.when(s == 0)
        def _():
            db_ref[...] = jnp.zeros_like(db_ref)

        db_ref[...] += _dot_nt(dz_ref[...], w_ref[...])

        @pl.when(s == N_CHIPS - 1)
        def _():
            dx, dgain = _rms_bwd(h_ref[...], g_ref[...], db_ref[...])
            dgain_ref[...] += dgain
            dh_ref[...] = jnp.where(_row_mask(t, tm, pad, (tm, D)), dhi_ref[...] + dx, 0.0)

    row = pl.BlockSpec((tm, D), lambda t, s: (t, 0))
    return _call(
        body, name=name, grid=(T // tm, N_CHIPS),
        in_specs=[pl.BlockSpec((tm, Ns), lambda t, s: (t, s)),
                  pl.BlockSpec((None, D, Ns), lambda t, s: (s, 0, 0)),
                  row, pl.BlockSpec((None, 1, D), lambda t, s: (layer, 0, 0)), row],
        out_specs=[row, pl.BlockSpec((1, D), lambda t, s: (0, 0))],
        out_shape=[jax.ShapeDtypeStruct((T, D), F32), jax.ShapeDtypeStruct((1, D), F32)],
        scratch=[pltpu.VMEM((tm, D), F32)],
        sem=("arbitrary", "arbitrary"), operands=(dz, win, h, gain, dh_in), rider=rider)


def _sum_pair(gs, rs, c_idx, name):
    n = len(gs)

    def body(c_ref, *refs):
        for g_ref, r_ref, o_ref in zip(refs[:n], refs[n:2 * n], refs[2 * n:]):
            o_ref[...] = (g_ref[...].astype(F32) + r_ref[...].astype(F32)).astype(BF16)

    halves = [pl.BlockSpec((None,) + r.shape[1:], lambda s, c_ref: (s, 0, 0)) for r in rs]
    return pl.pallas_call(
        body,
        name=name,
        grid_spec=pltpu.PrefetchScalarGridSpec(
            num_scalar_prefetch=1,
            grid=(N_CHIPS,),
            in_specs=[pl.BlockSpec((None,) + r.shape[1:], lambda s, c_ref: (s, c_ref[0], 0)) for r in rs] + halves,
            out_specs=halves,
        ),
        out_shape=[jax.ShapeDtypeStruct(r.shape, BF16) for r in rs],
        compiler_params=_params(("parallel",)),
    )(c_idx, *gs, *rs)


def _sum_chips(ps, rs, pos, name):
    n = len(ps)
    quarters = 4

    def body(pos_ref, *refs):
        chip = pos_ref[0]
        for p_ref, r_ref, o_ref in zip(refs[:n], refs[n:2 * n], refs[2 * n:]):
            own = p_ref[...].astype(F32)
            terms = [jnp.where(chip == k, own, r_ref[k].astype(F32)) for k in range(N_CHIPS)]
            o_ref[...] = ((terms[0] + terms[1]) + terms[2]) + terms[3]

    def rows(r):
        assert r.shape[1] % (quarters * BF16_ROWS) == 0, r.shape
        return r.shape[1] // quarters

    return pl.pallas_call(
        body,
        name=name,
        grid_spec=pltpu.PrefetchScalarGridSpec(
            num_scalar_prefetch=1,
            grid=(quarters,),
            in_specs=[pl.BlockSpec((None, rows(r), r.shape[2]), lambda q, pos_ref: (pos_ref[0], q, 0)) for r in rs]
            + [pl.BlockSpec((N_CHIPS, rows(r), r.shape[2]), lambda q, pos_ref: (0, q, 0)) for r in rs],
            out_specs=[pl.BlockSpec((rows(r), r.shape[2]), lambda q, pos_ref: (pos_ref[1] * quarters + q, 0))
                       for r in rs],
        ),
        out_shape=[jax.ShapeDtypeStruct((2 * r.shape[1], r.shape[2]), F32) for r in rs],
        compiler_params=_params(("arbitrary",)),
    )(pos, *ps, *rs)


def _small_all_reduce(p):
    rows, width = p.shape

    def body(p_ref, o_ref, sib_ref, slot_ref, ssem, rsem):
        x, y, c, chip, others = _mesh_pos()
        pair = _remote(p_ref, sib_ref, ssem.at[0], rsem.at[0], (x, y, 1 - c))
        pair.start()
        pair.wait()
        slot_ref[chip] = p_ref[...] + sib_ref[...]
        sends = []
        for j, (ox, oy) in enumerate(others):
            cp = _remote(slot_ref.at[chip], slot_ref.at[chip], ssem.at[1 + j], rsem.at[1 + j], (ox, oy, c))
            cp.start()
            sends.append(cp)
        for j, (ox, oy) in enumerate(others):
            slot = slot_ref.at[2 * ox + oy]
            _remote(slot, slot, ssem.at[1 + j], rsem.at[1 + j], (ox, oy, c)).wait_recv()
        for cp in sends:
            cp.wait_send()
        o_ref[...] = ((slot_ref[0] + slot_ref[1]) + slot_ref[2]) + slot_ref[3]

    vmem = pl.BlockSpec(memory_space=pltpu.VMEM)
    return pl.pallas_call(
        body,
        name="small_grads_all_reduce",
        in_specs=[vmem],
        out_specs=vmem,
        out_shape=jax.ShapeDtypeStruct(p.shape, F32),
        scratch_shapes=[pltpu.VMEM((rows, width), F32), pltpu.VMEM((N_CHIPS, rows, width), F32),
                        pltpu.SemaphoreType.DMA((4,)), pltpu.SemaphoreType.DMA((4,))],
    )(p)


def _adamw(gs, w, m, v, name):
    L, R, C = w.shape
    Ct = gs[0].shape[1]
    tr = _pick_tile(R, 256, 8)

    def body(*refs):
        g_refs = refs[:L]
        w_ref, m_ref, v_ref, go_ref, d_ref, mo_ref, vo_ref = refs[L:]
        layer = pl.program_id(0)
        grad = g_refs[L - 1][...]
        for i in range(L - 2, -1, -1):
            grad = jnp.where(layer == i, g_refs[i][...], grad)
        if Ct != C:
            grad = grad[:, :C]
        m_new = ADAM_B1 * m_ref[...] + (1.0 - ADAM_B1) * grad
        v_new = ADAM_B2 * v_ref[...] + (1.0 - ADAM_B2) * jnp.square(grad)
        m_hat = m_new / (1.0 - ADAM_B1 ** ADAM_STEP)
        v_hat = v_new / (1.0 - ADAM_B2 ** ADAM_STEP)
        go_ref[...] = grad
        d_ref[...] = -ADAM_LR * (m_hat / (jnp.sqrt(v_hat) + ADAM_EPS) + ADAM_WD * w_ref[...])
        mo_ref[...] = m_new
        vo_ref[...] = v_new

    g_specs = [pl.BlockSpec((tr, Ct), functools.partial(lambda l, r, i: (jnp.where(l == i, r, 0), 0), i=i))
               for i in range(L)]
    blk = pl.BlockSpec((None, tr, C), lambda l, r: (l, r, 0))
    return pl.pallas_call(
        body,
        name=name,
        grid=(L, R // tr),
        in_specs=g_specs + [blk, blk, blk],
        out_specs=[blk] * 4,
        out_shape=[jax.ShapeDtypeStruct((L, R, C), F32)] * 4,
        compiler_params=_params(("arbitrary", "arbitrary")),
    )(*gs, w, m, v)


_FFN1 = ("ffn1_gate", "ffn1_up", "ffn1_down")
_FFN2 = ("ffn2_gate", "ffn2_up", "ffn2_down")
_MIXW = ("w_ret_up", "w_pool_up", "w_out")
_BIG = _FFN1 + ("w_in",) + _MIXW + _FFN2
_TRANSPOSED = ("ffn1_gate", "ffn1_up", "ffn2_gate", "ffn2_up")
_SMALL = ("ffn1_norm", "mix_norm", "ffn2_norm", "final_norm", "pool_scale", "pool_maps")
_ORDER = ("meta", "ffn1_norm", "ffn1_gate", "ffn1_up", "ffn1_down", "mix_norm", "w_in", "pool_maps",
          "pool_scale", "w_ret_up", "w_pool_up", "w_out", "ffn2_norm", "ffn2_gate", "ffn2_up", "ffn2_down",
          "final_norm")


def _transport(a):
    n, r, c = a.shape
    out = a.astype(BF16)
    if c % LANES:
        out = jnp.concatenate([out, jnp.zeros((n, r, _round_up(c, LANES) - c), BF16)], axis=2)
    if r % LANES:
        out = jnp.concatenate([out, jnp.zeros((n, _round_up(r, LANES) - r, out.shape[2]), BF16)], axis=1)
    return out


def _pack_rows(parts, width):
    rows = [p.reshape(-1, width) for p in parts]
    total = sum(r.shape[0] for r in rows)
    fill = _round_up(total, 8) - total
    if fill:
        rows.append(jnp.zeros((fill, width), F32))
    return jnp.concatenate(rows, axis=0)


def _unpack_rows(packed, shapes, width):
    out, at = [], 0
    for shp in shapes:
        n = math.prod(shp) // width
        out.append(packed[at:at + n].reshape(shp))
        at += n
    return out


class _Weights:
    def __init__(self, shards):
        self.shards = shards
        self.full = {}

    def rider(self, keys):
        r = _gather_rider([(self.shards[n], i) for n, i in keys])
        r.keys = keys
        return r

    def take(self, rider):
        for key, arr in zip(rider.keys, rider.results):
            self.full[key] = arr

    def __call__(self, name, layer):
        return self.full[(name, layer)]


def _local_step(x, meta_full, tgt, w, wts, pad, tm, cg, reducer):
    D = x.shape[1]
    T = pad + N_META + x.shape[0]
    L = w["ffn1_norm"].shape[0]
    pool_maps = w["pool_maps"]
    gains = {n: w[n].reshape(L, 1, D) for n in ("ffn1_norm", "mix_norm", "ffn2_norm")}
    scale3 = w["pool_scale"].reshape(L, 1, POOL_WIDTH)
    consts = _ret_consts(T, pad)
    tl = _pick_tile(T, 2 * tm, BF16_ROWS)
    def gather(keys):
        return wts.rider(keys) if keys and keys[0] not in wts.full else None

    def done(rider):
        if rider is not None:
            wts.take(rider)

    h = jnp.concatenate([jnp.zeros((pad, D), F32), meta_full, x], axis=0)
    saved = []
    for i in range(L):
        s = {"h0": h}
        rd = gather([("w_in", i)] + [(n, i) for n in _MIXW])
        h, s["a1"], s["g1"], s["u1"], s["act1"] = _ffn_fwd(
            h, gains["ffn1_norm"], wts("ffn1_gate", i), wts("ffn1_up", i), wts("ffn1_down", i), i, tl,
            f"ffn1_fwd_{i}", rd)
        done(rd)
        s["h1"] = h
        rd = gather([("ffn2_gate", i), ("ffn2_up", i)])
        s["z"], s["b"] = _inproj_fwd(h, gains["mix_norm"], wts("w_in", i), i, tl, f"inproj_fwd_{i}", rd)
        done(rd)
        s["r"], s["o_pre"], s["s_all"] = _ret_fwd(s["z"], consts, cg, f"retention_fwd_{i}")
        s["pm"] = _pool_fwd(s["z"], pool_maps, scale3, i, pad, f"pool_fwd_{i}")
        rd = gather([("ffn2_down", i)])
        h, s["mixed"], s["ret"], s["pool"] = _mix_fwd(
            h, s["r"], s["pm"], s["z"], wts("w_ret_up", i), wts("w_pool_up", i), wts("w_out", i), tm,
            f"mix_fwd_{i}", rd)
        done(rd)
        s["h2"] = h
        rd = gather([(n, i + 1) for n in _FFN1]) if i + 1 < L else None
        h, s["a2"], s["g2"], s["u2"], s["act2"] = _ffn_fwd(
            h, gains["ffn2_norm"], wts("ffn2_gate", i), wts("ffn2_up", i), wts("ffn2_down", i), i, tl,
            f"ffn2_fwd_{i}", rd)
        done(rd)
        saved.append(s)

    dh, loss_acc, d_final = _final_loss(h, w["final_norm"].reshape(1, D), tgt, "final_norm_loss")

    small = {n: [None] * L for n in ("ffn1_norm", "mix_norm", "ffn2_norm", "pool_scale", "pool_maps")}

    carry = {"ffn_act": 1.0, "ffn_in": 2.2, "mix_bwd": 1.0, "inproj_bwd": 1.5, "w_in": 1.0}

    tk = _pick_tile(T, 1408, LANES)

    def grad(n, a, b, i, mode):
        rd = reducer.rider(carry.get(n, 1.0 if i == 0 and n.startswith("ffn") else 0.5))
        reducer.add(n, i, _grad_tn(a, b, mode, 1.0, tk, f"grad_{n}_{i}", rd))
        reducer.done(rd)

    def ffn_bwd(which, dy, h_in, g, u, i):
        rd = reducer.rider(carry["ffn_act"])
        dg, du, dyh = _ffn_bwd_act(dy, g, u, wts(f"{which}_down", i), tl, f"{which}_bwd_act_{i}", rd)
        reducer.done(rd)
        rd = reducer.rider(carry["ffn_in"])
        dh_in, dgain = _ffn_bwd_in(dy, h_in, gains[f"{which}_norm"], dg, du, wts(f"{which}_gate", i),
                                   wts(f"{which}_up", i), i, tl, pad, f"{which}_bwd_in_{i}", rd)
        reducer.done(rd)
        return dh_in, dg, du, dgain, dyh

    for i in reversed(range(L)):
        s = saved[i]
        dh, dg, du, small["ffn2_norm"][i], dyh = ffn_bwd("ffn2", dh, s["h2"], s["g2"], s["u2"], i)
        grad("ffn2_gate", dg, s["a2"], i, "row")
        grad("ffn2_up", du, s["a2"], i, "row")
        grad("ffn2_down", s["act2"], dyh, i, "row")
        reducer.stage(f"ffn2_{i}")
        rd = reducer.rider(carry["mix_bwd"])
        dgab, dret, dpool, dr, dpm = _mix_bwd_dx(
            dh, s["z"], s["ret"], s["pool"], wts("w_out", i), wts("w_ret_up", i), wts("w_pool_up", i), tm,
            f"mix_bwd_{i}", rd)
        reducer.done(rd)
        grad("w_out", s["mixed"], dh, i, "row")
        grad("w_ret_up", s["r"], dret, i, "col")
        grad("w_pool_up", s["pm"], dpool, i, "col")
        du_pool, small["pool_maps"][i], small["pool_scale"][i] = _pool_bwd(
            s["z"], dpm, pool_maps, scale3, i, pad, f"pool_bwd_{i}")
        dq, dgr, dkp, dvp, ds = _ret_bwd_local(s["z"], s["o_pre"], s["s_all"], dr, consts, cg, f"retention_bwd_{i}")
        dk, dv = _ret_bwd_state(s["z"], dkp, dvp, ds, consts, cg, f"retention_bwd_state_{i}")
        dz = jnp.concatenate([dq, dk, dv, dgr, du_pool, dgab], axis=1)
        dh2 = dh
        rd = reducer.rider(carry["inproj_bwd"])
        dh, small["mix_norm"][i] = _inproj_bwd_dx(
            dz, wts("w_in", i), s["h1"], gains["mix_norm"], dh2, i, tl, pad, f"inproj_bwd_{i}", rd)
        reducer.done(rd)
        grad("w_in", s["b"], dz, i, "col")
        reducer.stage(f"mid{i}")
        dh, dg, du, small["ffn1_norm"][i], dyh = ffn_bwd("ffn1", dh, s["h0"], s["g1"], s["u1"], i)
        grad("ffn1_gate", dg, s["a1"], i, "row")
        if i == 0:
            reducer.stage("gate0")
        grad("ffn1_up", du, s["a1"], i, "row")
        if i == 0:
            reducer.stage("up0")
        grad("ffn1_down", s["act1"], dyh, i, "row")
        reducer.stage(f"end{i}")

    return loss_acc, dh, small, d_final


class _Reducer:
    def __init__(self, unit):
        self.c_idx = lax.axis_index("c").astype(jnp.int32).reshape(1)
        chip = 2 * lax.axis_index("x") + lax.axis_index("y")
        self.pos = jnp.stack([chip, lax.axis_index("c")]).astype(jnp.int32)
        self.pending, self.stages, self.queue, self.halves = [], [], [], {}
        self.unit = unit
        self.calls = 0

    def add(self, name, layer, g):
        self.pending.append(((name, layer), g))

    def stage(self, tag):
        if self.pending:
            self.stages.append((tag, self.pending))
            self.pending = []

    def _pair_rider(self):
        if not self.stages:
            return None
        tag, items = self.stages.pop(0)
        rd = _pair_exchange_rider([g for _, g in items])
        rd.tag, rd.keys = tag, [k for k, _ in items]
        return rd

    def _chip_rider(self, units):
        take, size = [], 0
        while self.queue and (units is None or size + self.queue[0][1].size <= units * self.unit):
            take.append(self.queue.pop(0))
            size += take[-1][1].size
        if not take:
            return None
        rd = _chip_exchange_rider([p for _, p in take])
        rd.keys = [k for k, _ in take]
        return rd

    def rider(self, units):
        self.riding = (self._pair_rider(), self._chip_rider(units))
        return _join(self.riding)

    def done(self, rd):
        if rd is None:
            return
        _split_results(rd)
        pair, chips = self.riding
        if len([r for r in self.riding if r is not None]) == 1:
            (pair or chips).results = rd.results
        self.calls += 1
        if pair is not None:
            sums = _sum_pair(pair.ins, pair.results, self.c_idx, f"sum_pair_{pair.tag}")
            self.queue += list(zip(pair.keys, sums))
        if chips is not None:
            sums = _sum_chips(chips.ins, chips.results, self.pos, f"sum_chips_{self.calls}")
            self.halves.update(zip(chips.keys, sums))

    def finish(self):
        assert not self.pending
        while self.stages or self.queue:
            self.riding = (self._pair_rider(), self._chip_rider(None))
            rd = _join(self.riding)
            _run_rider(rd, f"grads_exchange_tail_{self.calls}")
            self.done(rd)
        keys = list(self.halves)
        return dict(zip(keys, _pair_gather([self.halves[k] for k in keys])))


def _update(loss, grad_x, d_meta_rows, shard_grads, small, d_final, w, mom, var):
    meta = w["meta"]
    D = w["final_norm"].shape[0]
    L = w["ffn1_norm"].shape[0]
    Dq = D // N_CHIPS

    small_parts = [jnp.concatenate(small[n], axis=0) for n in ("ffn1_norm", "mix_norm", "ffn2_norm")]
    small_parts += [d_final, jnp.concatenate(small["pool_scale"], axis=0), jnp.concatenate(small["pool_maps"], axis=0)]
    reduced = _small_all_reduce(_pack_rows(small_parts + [d_meta_rows], D))
    small_shapes = [w[n].shape for n in _SMALL]
    small_rows = sum(math.prod(shp) for shp in small_shapes) // D
    chip = 2 * lax.axis_index("x") + lax.axis_index("y")
    d_meta = lax.dynamic_slice_in_dim(reduced[small_rows:small_rows + N_META], chip * Dq, Dq, axis=1)

    out = {}
    for n in _BIG:
        gs = [shard_grads[(n, i)] for i in range(L)]
        if n in _TRANSPOSED:
            res = _adamw(gs, *(jnp.swapaxes(t[n], 1, 2) for t in (w, mom, var)), f"adamw_{n}")
            out[n] = [jnp.swapaxes(r, 1, 2) for r in res]
        else:
            out[n] = _adamw(gs, w[n], mom[n], var[n], f"adamw_{n}")
    names = _SMALL + ("meta",)
    packed_g = _pack_rows([reduced[:small_rows], d_meta], D)
    packed = [_pack_rows([t[n] for n in names], D) for t in (w, mom, var)]
    res = _adamw([packed_g], packed[0][None], packed[1][None], packed[2][None], "adamw_small")
    shapes = small_shapes + [meta.shape]
    unpacked = [_unpack_rows(r[0], shapes, D) for r in res]
    for k, n in enumerate(names):
        out[n] = tuple(u[k] for u in unpacked)

    return (loss, grad_x) + tuple(out[n][j] for j in range(4) for n in _ORDER)


def kernel(x, meta, ffn1_norm, ffn1_gate, ffn1_up, ffn1_down, mix_norm, w_in, pool_maps, pool_scale, w_ret_up, w_pool_up, w_out, ffn2_norm, ffn2_gate, ffn2_up, ffn2_down, final_norm, loss_target, m_meta, m_ffn1_norm, m_ffn1_gate, m_ffn1_up, m_ffn1_down, m_mix_norm, m_w_in, m_pool_maps, m_pool_scale, m_w_ret_up, m_w_pool_up, m_w_out, m_ffn2_norm, m_ffn2_gate, m_ffn2_up, m_ffn2_down, m_final_norm, v_meta, v_ffn1_norm, v_ffn1_gate, v_ffn1_up, v_ffn1_down, v_mix_norm, v_w_in, v_pool_maps, v_pool_scale, v_w_ret_up, v_w_pool_up, v_w_out, v_ffn2_norm, v_ffn2_gate, v_ffn2_up, v_ffn2_down, v_final_norm):
    args = dict(locals())
    w = {n: args[n] for n in _ORDER}
    mom = {n: args["m_" + n] for n in _ORDER}
    var = {n: args["v_" + n] for n in _ORDER}

    assert x.shape[0] == 1, "one batch element per device"
    seq, D = x.shape[1], x.shape[2]
    assert seq % CHUNK == 0 and D % RET_WIDTH == 0 and (2 * POOL_WIDTH) % D == 0
    pad = (-(seq + N_META)) % CHUNK
    T = seq + N_META + pad
    tm = _pick_tile(T, 528, BF16_ROWS)
    cg = _pick_tile(T // CHUNK, 11, 1)

    shards = {n: _transport(w[n]) for n in _BIG}
    shards["meta"] = meta[None]
    wts = _Weights(shards)
    head = wts.rider([(n, 0) for n in _FFN1] + [("meta", 0)])
    _run_rider(head, "weights_gather_head")
    wts.take(head)
    meta_full = jnp.transpose(wts("meta", 0), (1, 0, 2)).reshape(N_META, D)

    reducer = _Reducer(unit=2 * shards["ffn1_gate"][0].size)
    loss_acc, dh, small, d_final = _local_step(x[0], meta_full, loss_target[0], w, wts, pad, tm, cg, reducer)
    loss = lax.psum(loss_acc[0, 0], ("x", "y", "c"))
    grad_x = dh[pad + N_META:][None]
    return _update(loss, grad_x, dh[pad:pad + N_META], reducer.finish(), small, d_final, w, mom, var)
```

```python
import functools
import math

import jax
import jax.numpy as jnp
from jax import lax
from jax.experimental import pallas as pl
from jax.experimental.pallas import tpu as pltpu

F32 = jnp.float32
BF16 = jnp.bfloat16

N_META = 16
RET_HEADS = 4
HEAD_DIM = 128
RET_WIDTH = RET_HEADS * HEAD_DIM
POOL_WINDOWS = (2, 4, 8, 16)
POOL_GROUPS = len(POOL_WINDOWS)
POOL_WIDTH = POOL_GROUPS * HEAD_DIM
CHUNK = 128
ROPE_BASE = 10000.0
EPS = 1e-6
ADAM_LR = 0.001
ADAM_B1 = 0.9
ADAM_B2 = 0.999
ADAM_EPS = 1e-08
ADAM_WD = 0.01
ADAM_STEP = 10

N_CHIPS = 4
LANES = 128
BF16_ROWS = 16
V7X_VMEM_LIMIT = 56 * 1024 * 1024
MESH = pl.DeviceIdType.MESH
ANY = pl.BlockSpec(memory_space=pl.ANY)


def _round_up(n, m):
    return -(-n // m) * m


def _pick_tile(n, target, mult):
    best = None
    for d in range(mult, min(n, target) + 1, mult):
        if n % d == 0:
            best = d
    assert best is not None, (n, target, mult)
    return best


def _params(sem=None):
    return pltpu.CompilerParams(dimension_semantics=sem, vmem_limit_bytes=V7X_VMEM_LIMIT)


def _dot(a, b):
    return jnp.dot(a, b, preferred_element_type=F32)


def _dot_nt(a, b):
    return lax.dot_general(a, b, (((1,), (1,)), ((), ())), preferred_element_type=F32)


def _dot_tn(a, b):
    return lax.dot_general(a, b, (((0,), (0,)), ((), ())), preferred_element_type=F32)


def _ein(spec, a, b):
    return jnp.einsum(spec, a, b, preferred_element_type=F32)


def _sigmoid(x):
    return jax.nn.sigmoid(x)


def _rms_fwd(x, gain):
    r = lax.rsqrt(jnp.mean(x * x, axis=-1, keepdims=True) + EPS)
    return x * r * gain


def _rms_bwd(x, gain, da):
    r = lax.rsqrt(jnp.mean(x * x, axis=-1, keepdims=True) + EPS)
    xh = x * r
    dgain = jnp.sum(da * xh, axis=0, keepdims=True)
    dxh = da * gain
    dx = r * (dxh - xh * jnp.mean(dxh * xh, axis=-1, keepdims=True))
    return dx, dgain


def _row_mask(t, tm, pad, shape):
    rows = t * tm + lax.broadcasted_iota(jnp.int32, shape, 0)
    return rows >= pad


def _mesh_pos():
    x, y, c = lax.axis_index("x"), lax.axis_index("y"), lax.axis_index("c")
    others = [(1 - x, y), (x, 1 - y), (1 - x, 1 - y)]
    return x, y, c, 2 * x + y, others


def _half_rows(c, rh):
    return pl.ds(pl.multiple_of(c * rh, rh), rh)


def _remote(src, dst, ssem, rsem, dev):
    return pltpu.make_async_remote_copy(src_ref=src, dst_ref=dst, send_sem=ssem, recv_sem=rsem,
                                        device_id=dev, device_id_type=MESH)


class _Rider:
    def __init__(self, ins, out_shapes, n_sem, start, finish):
        self.ins, self.out_shapes, self.n_sem, self.start, self.finish = ins, out_shapes, n_sem, start, finish
        self.results = None


class _SemWindow:
    def __init__(self, ref, base):
        self.ref, self.base = ref, base

    @property
    def at(self):
        return self

    def __getitem__(self, k):
        return self.ref.at[self.base + k]


def _join(riders):
    riders = [r for r in riders if r is not None]
    if len(riders) <= 1:
        return riders[0] if riders else None

    def run(which):
        def go(ins, outs, ssem, rsem):
            at, sem = 0, 0
            for r in riders:
                n = len(r.ins)
                getattr(r, which)(ins[at:at + n], outs[at:at + n], _SemWindow(ssem, sem), _SemWindow(rsem, sem))
                at, sem = at + n, sem + r.n_sem
        return go

    joined = _Rider(sum([list(r.ins) for r in riders], []), sum([list(r.out_shapes) for r in riders], []),
                    sum(r.n_sem for r in riders), run("start"), run("finish"))
    joined.parts = riders
    return joined


def _split_results(rider):
    at = 0
    for r in getattr(rider, "parts", []):
        r.results = rider.results[at:at + len(r.ins)]
        at += len(r.ins)


def _gather_rider(pieces):
    per = 7
    layers = [layer for _, layer in pieces]

    def first_copies(ins, outs, ssem, rsem):
        x, y, c, chip, others = _mesh_pos()
        copies = []
        for i, layer in enumerate(layers):
            mine = _half_rows(c, ins[i].shape[1] // 2)
            for j, (ox, oy) in enumerate(others):
                copies.append(_remote(ins[i].at[layer, mine, :], outs[i].at[chip, mine, :],
                                      ssem.at[per * i + j], rsem.at[per * i + j], (ox, oy, c)))
            copies.append(_remote(ins[i].at[layer], outs[i].at[chip],
                                  ssem.at[per * i + 6], rsem.at[per * i + 6], (x, y, 1 - c)))
        return copies

    def start(ins, outs, ssem, rsem):
        for cp in first_copies(ins, outs, ssem, rsem):
            cp.start()

    def finish(ins, outs, ssem, rsem):
        x, y, c, chip, others = _mesh_pos()
        sibling = (x, y, 1 - c)
        forwards = []
        for i in range(len(layers)):
            mine = _half_rows(c, ins[i].shape[1] // 2)
            for j, (ox, oy) in enumerate(others):
                rows = outs[i].at[2 * ox + oy, mine, :]
                _remote(rows, rows, ssem.at[per * i + j], rsem.at[per * i + j], (ox, oy, c)).wait_recv()
                fwd = _remote(rows, rows, ssem.at[per * i + 3 + j], rsem.at[per * i + 3 + j], sibling)
                fwd.start()
                forwards.append(fwd)
        for i in range(len(layers)):
            theirs = _half_rows(1 - c, ins[i].shape[1] // 2)
            for j, (ox, oy) in enumerate(others):
                rows = outs[i].at[2 * ox + oy, theirs, :]
                _remote(rows, rows, ssem.at[per * i + 3 + j], rsem.at[per * i + 3 + j], sibling).wait_recv()
            own = outs[i].at[chip]
            _remote(own, own, ssem.at[per * i + 6], rsem.at[per * i + 6], sibling).wait_recv()
        for cp in first_copies(ins, outs, ssem, rsem) + forwards:
            cp.wait_send()

    shapes = [jax.ShapeDtypeStruct((N_CHIPS,) + s.shape[1:], s.dtype) for s, _ in pieces]
    return _Rider([s for s, _ in pieces], shapes, per * len(pieces), start, finish)


def _chip_exchange_rider(ps):
    def copies(ins, outs, ssem, rsem):
        x, y, c, chip, others = _mesh_pos()
        return [_remote(ins[i].at[2 * ox + oy], outs[i].at[chip], ssem.at[3 * i + j], rsem.at[3 * i + j], (ox, oy, c))
                for i in range(len(ps)) for j, (ox, oy) in enumerate(others)]

    def start(ins, outs, ssem, rsem):
        for cp in copies(ins, outs, ssem, rsem):
            cp.start()

    def finish(ins, outs, ssem, rsem):
        x, y, c, chip, others = _mesh_pos()
        for i in range(len(ps)):
            for j, (ox, oy) in enumerate(others):
                slot = outs[i].at[2 * ox + oy]
                _remote(slot, slot, ssem.at[3 * i + j], rsem.at[3 * i + j], (ox, oy, c)).wait_recv()
        for cp in copies(ins, outs, ssem, rsem):
            cp.wait_send()

    return _Rider(list(ps), [jax.ShapeDtypeStruct(p.shape, p.dtype) for p in ps], 3 * len(ps), start, finish)


def _pair_exchange_rider(gs):
    def copies(ins, outs, ssem, rsem):
        x, y, c, _, _ = _mesh_pos()
        return [_remote(ins[i].at[:, _half_rows(1 - c, ins[i].shape[1] // 2), :], outs[i],
                        ssem.at[i], rsem.at[i], (x, y, 1 - c)) for i in range(len(gs))]

    def start(ins, outs, ssem, rsem):
        for cp in copies(ins, outs, ssem, rsem):
            cp.start()

    def finish(ins, outs, ssem, rsem):
        for cp in copies(ins, outs, ssem, rsem):
            cp.wait()

    shapes = [jax.ShapeDtypeStruct((g.shape[0], g.shape[1] // 2, g.shape[2]), g.dtype) for g in gs]
    return _Rider(list(gs), shapes, len(gs), start, finish)


def _run_rider(rider, name):
    def body(*refs):
        n = len(rider.ins)
        ins, outs = refs[:n], refs[n:2 * n]
        ssem, rsem = refs[2 * n:]
        rider.start(ins, outs, ssem, rsem)
        rider.finish(ins, outs, ssem, rsem)

    rider.results = pl.pallas_call(
        body,
        name=name,
        in_specs=[ANY] * len(rider.ins),
        out_specs=[ANY] * len(rider.ins),
        out_shape=rider.out_shapes,
        scratch_shapes=[pltpu.SemaphoreType.DMA((rider.n_sem,)), pltpu.SemaphoreType.DMA((rider.n_sem,))],
    )(*rider.ins)
    return rider.results


def _pair_gather(fs):
    n = len(fs)

    def body(*refs):
        bufs = refs[n:2 * n]
        ssem, rsem = refs[2 * n:]
        x, y, c, _, _ = _mesh_pos()
        sends = []
        for i in range(n):
            rh = bufs[i].shape[0] // 2
            mine = bufs[i].at[_half_rows(c, rh), :]
            cp = _remote(mine, mine, ssem.at[i], rsem.at[i], (x, y, 1 - c))
            cp.start()
            sends.append(cp)
        for i in range(n):
            rh = bufs[i].shape[0] // 2
            theirs = bufs[i].at[_half_rows(1 - c, rh), :]
            _remote(theirs, theirs, ssem.at[i], rsem.at[i], (x, y, 1 - c)).wait_recv()
        for cp in sends:
            cp.wait_send()

    return pl.pallas_call(
        body,
        name="grads_pair_gather",
        in_specs=[ANY] * n,
        out_specs=[ANY] * n,
        out_shape=[jax.ShapeDtypeStruct(f.shape, f.dtype) for f in fs],
        input_output_aliases={i: i for i in range(n)},
        scratch_shapes=[pltpu.SemaphoreType.DMA((n,)), pltpu.SemaphoreType.DMA((n,))],
    )(*fs)


def _call(body, *, name, grid, in_specs, out_specs, out_shape, operands, scratch=(), sem=None, rider=None):
    if rider is None:
        return pl.pallas_call(
            body, name=name, grid=grid, in_specs=in_specs, out_specs=out_specs, out_shape=out_shape,
            scratch_shapes=list(scratch), compiler_params=_params(sem))(*operands)
    n_in, n_out, n_sc, r = len(in_specs), len(out_specs), len(scratch), len(rider.ins)

    def carrying(*refs):
        a, b = n_in, n_in + r
        c, d = b + n_out, b + n_out + r
        e = d + n_sc
        ids = [pl.program_id(k) for k in range(len(grid))]
        first = functools.reduce(jnp.logical_and, [i == 0 for i in ids])
        last = functools.reduce(jnp.logical_and, [i == g - 1 for i, g in zip(ids, grid)])

        @pl.when(first)
        def _():
            rider.start(refs[a:b], refs[c:d], refs[e], refs[e + 1])

        body(*refs[:a], *refs[b:c], *refs[d:e])

        @pl.when(last)
        def _():
            rider.finish(refs[a:b], refs[c:d], refs[e], refs[e + 1])

    outs = pl.pallas_call(
        carrying, name=name, grid=grid,
        in_specs=list(in_specs) + [ANY] * r,
        out_specs=list(out_specs) + [ANY] * r,
        out_shape=list(out_shape) + list(rider.out_shapes),
        scratch_shapes=list(scratch) + [pltpu.SemaphoreType.DMA((rider.n_sem,)), pltpu.SemaphoreType.DMA((rider.n_sem,))],
        compiler_params=_params(("arbitrary",) * len(grid)),
    )(*operands, *rider.ins)
    rider.results = outs[n_out:]
    return outs[:n_out]


def _ffn_fwd(h, gain, wg, wu, wd, layer, tm, name, rider=None):
    T, D = h.shape
    Fs = wg.shape[-1]
    F = N_CHIPS * Fs

    def body(h_ref, g_ref, wg_ref, wu_ref, wd_ref, ho_ref, a_ref, go_ref, uo_ref, act_ref, acc_ref):
        s = pl.program_id(1)

        @pl.when(s == 0)
        def _():
            a_ref[...] = _rms_fwd(h_ref[...], g_ref[...]).astype(BF16)
            acc_ref[...] = jnp.zeros_like(acc_ref)

        a = a_ref[...]
        g = _dot(a, wg_ref[...])
        u = _dot(a, wu_ref[...])
        sg = _sigmoid(g)
        act = (g * sg * u).astype(BF16)
        go_ref[...] = (u * (sg * (1.0 + g * (1.0 - sg)))).astype(BF16)
        uo_ref[...] = (g * sg).astype(BF16)
        act_ref[...] = act
        acc_ref[...] += _dot(act, wd_ref[...])

        @pl.when(s == N_CHIPS - 1)
        def _():
            ho_ref[...] = h_ref[...] + 0.5 * acc_ref[...]

    row = pl.BlockSpec((tm, D), lambda t, s: (t, 0))
    col = pl.BlockSpec((tm, Fs), lambda t, s: (t, s))
    wcol = pl.BlockSpec((None, D, Fs), lambda t, s: (s, 0, 0))
    return _call(
        body, name=name, grid=(T // tm, N_CHIPS),
        in_specs=[row, pl.BlockSpec((None, 1, D), lambda t, s: (layer, 0, 0)), wcol, wcol,
                  pl.BlockSpec((None, Fs, D), lambda t, s: (s, 0, 0))],
        out_specs=[row, row, col, col, col],
        out_shape=[jax.ShapeDtypeStruct((T, D), F32), jax.ShapeDtypeStruct((T, D), BF16)]
        + [jax.ShapeDtypeStruct((T, F), BF16)] * 3,
        scratch=[pltpu.VMEM((tm, D), F32)],
        sem=("parallel", "arbitrary"), operands=(h, gain, wg, wu, wd), rider=rider)


def _inproj_fwd(h, gain, win, layer, tm, name, rider=None):
    T, D = h.shape
    Ns = win.shape[-1]

    def body(h_ref, g_ref, w_ref, z_ref, b_ref):
        @pl.when(pl.program_id(1) == 0)
        def _():
            b_ref[...] = _rms_fwd(h_ref[...], g_ref[...]).astype(BF16)

        z_ref[...] = _dot(b_ref[...], w_ref[...]).astype(BF16)

    return _call(
        body, name=name, grid=(T // tm, N_CHIPS),
        in_specs=[pl.BlockSpec((tm, D), lambda t, s: (t, 0)),
                  pl.BlockSpec((None, 1, D), lambda t, s: (layer, 0, 0)),
                  pl.BlockSpec((None, D, Ns), lambda t, s: (s, 0, 0))],
        out_specs=[pl.BlockSpec((tm, Ns), lambda t, s: (t, s)), pl.BlockSpec((tm, D), lambda t, s: (t, 0))],
        out_shape=[jax.ShapeDtypeStruct((T, N_CHIPS * Ns), BF16), jax.ShapeDtypeStruct((T, D), BF16)],
        sem=("parallel", "arbitrary"), operands=(h, gain, win), rider=rider)


def _ret_consts(T, pad):
    half = HEAD_DIM // 2
    inv_freq = ROPE_BASE ** (-jnp.arange(half, dtype=F32) / half)
    pos = jnp.arange(T, dtype=F32) - pad
    ang = pos[:, None] * inv_freq[None, :]
    cos = jnp.cos(ang)
    sin = jnp.sin(ang)
    cosf = jnp.concatenate([cos, cos], axis=1)
    sinf = jnp.concatenate([-sin, sin], axis=1)
    log_gamma = jnp.log1p(-(2.0 ** (-5.0 - jnp.arange(RET_HEADS, dtype=F32))))
    idx = jnp.arange(CHUNK, dtype=F32)
    diff = idx[:, None] - idx[None, :]
    intra = jnp.where(diff[None] >= 0, jnp.exp(diff[None] * log_gamma[:, None, None]), 0.0)
    k_decay = jnp.exp((CHUNK - 1.0 - idx)[None, :] * log_gamma[:, None])
    q_decay = jnp.exp((idx + 1.0)[None, :] * log_gamma[:, None])
    chunk_decay = jnp.exp(CHUNK * log_gamma)
    kdec = jnp.broadcast_to(k_decay[:, :, None], (RET_HEADS, CHUNK, HEAD_DIM))
    qdec = jnp.broadcast_to(q_decay[:, :, None], (RET_HEADS, CHUNK, HEAD_DIM))
    cdb = jnp.broadcast_to(chunk_decay[:, None, None], (RET_HEADS, 8, HEAD_DIM))
    return cosf, sinf, intra, kdec, qdec, cdb


def _rot(t, cosv, sinv):
    return t * cosv + pltpu.roll(t, HEAD_DIM // 2, 1) * sinv


def _rot_t(g, cosv, sinv):
    return g * cosv + pltpu.roll(g * sinv, HEAD_DIM // 2, 1)


def _head_specs(tg, section, order):
    return pl.BlockSpec((tg, HEAD_DIM), lambda h, g: (order(g), section * RET_HEADS + h))


def _ret_fwd(z, consts, cg, name, rider=None):
    T = z.shape[0]
    N = T // CHUNK
    ng = N // cg
    tg = cg * CHUNK
    cosf, sinf, intra, kdec, qdec, cdb = consts
    fwd = lambda g: g

    def body(zq, zk, zv, zg, cos_ref, sin_ref, m_ref, kd_ref, qd_ref, cd_ref, r_ref, o_ref, s_ref, st_ref):
        @pl.when(pl.program_id(1) == 0)
        def _():
            st_ref[...] = jnp.zeros_like(st_ref)

        cosv = cos_ref[...]
        sinv = sin_ref[...]
        q3 = (_rot(zq[...].astype(F32), cosv, sinv) * (HEAD_DIM ** -0.5)).reshape(cg, CHUNK, HEAD_DIM)
        k3 = _rot(zk[...].astype(F32), cosv, sinv).reshape(cg, CHUNK, HEAD_DIM)
        vb = zv[...].reshape(cg, CHUNK, HEAD_DIM).astype(BF16)
        scores = _ein("ncd,nmd->ncm", q3.astype(BF16), k3.astype(BF16)) * m_ref[...][None]
        inner = _ein("ncm,nmd->ncd", scores.astype(BF16), vb)
        kv = _ein("ncd,nce->nde", (k3 * kd_ref[...][None]).astype(BF16), vb)
        cd = cd_ref[0:1, :]
        state = st_ref[...]
        for n in range(cg):
            s_ref[n] = state
            state = state * cd + kv[n]
        st_ref[...] = state
        qdb = (q3 * qd_ref[...][None]).astype(BF16)
        cross = _ein("ncd,nde->nce", qdb, s_ref[...].astype(BF16))
        out = (inner + cross).reshape(tg, HEAD_DIM)
        o_ref[...] = out
        xc = out - jnp.mean(out, axis=-1, keepdims=True)
        rn = xc * lax.rsqrt(jnp.mean(xc * xc, axis=-1, keepdims=True) + EPS)
        g = zg[...].astype(F32)
        r_ref[...] = (rn * (g * _sigmoid(g))).astype(BF16)

    tab = pl.BlockSpec((tg, HEAD_DIM), lambda h, g: (g, 0))
    per_head = lambda rows: pl.BlockSpec((None, rows, HEAD_DIM), lambda h, g: (h, 0, 0))
    head_out = pl.BlockSpec((tg, HEAD_DIM), lambda h, g: (g, h))
    return _call(
        body, name=name, grid=(RET_HEADS, ng),
        in_specs=[_head_specs(tg, i, fwd) for i in range(4)]
        + [tab, tab, per_head(CHUNK), per_head(CHUNK), per_head(CHUNK), per_head(8)],
        out_specs=[head_out, head_out, pl.BlockSpec((None, cg, HEAD_DIM, HEAD_DIM), lambda h, g: (h, g, 0, 0))],
        out_shape=[jax.ShapeDtypeStruct((T, RET_WIDTH), BF16), jax.ShapeDtypeStruct((T, RET_WIDTH), F32),
                   jax.ShapeDtypeStruct((RET_HEADS, N, HEAD_DIM, HEAD_DIM), F32)],
        scratch=[pltpu.VMEM((HEAD_DIM, HEAD_DIM), F32)],
        sem=("parallel", "arbitrary"), operands=(z, z, z, z, cosf, sinf, intra, kdec, qdec, cdb), rider=rider)


def _window_sums(u, shift_of):
    sums = []
    s = u
    k = 1
    while k < POOL_WINDOWS[-1]:
        s = s + pltpu.roll(s, shift_of(k), 0)
        sums.append(s)
        k *= 2
    return sums


def _select_group(vals, g):
    out = vals[-1]
    for i in range(len(vals) - 2, -1, -1):
        out = jnp.where(g == i, vals[i], out)
    return out


def _pool_parts(u, g, T, pad):
    rows = lax.broadcasted_iota(jnp.int32, (T, HEAD_DIM), 0)
    valid = rows >= pad
    win = _select_group([float(w) for w in POOL_WINDOWS], g)
    div = jnp.clip((rows - pad + 1).astype(F32), 1.0, win)
    s = _select_group(_window_sums(u, lambda k: k), g)
    pooled = jnp.where(valid, s / div - u, 0.0)
    return pooled, div, valid


def _pool_specs(T, layer):
    first = 4 * RET_WIDTH // HEAD_DIM
    return [
        pl.BlockSpec((T, HEAD_DIM), lambda g: (0, first + g)),
        pl.BlockSpec((None, None, HEAD_DIM, HEAD_DIM), lambda g: (layer, g, 0, 0)),
        pl.BlockSpec((None, 1, HEAD_DIM), lambda g: (layer, 0, g)),
    ]


def _pool_fwd(z, maps, scale, layer, pad, name):
    T = z.shape[0]
    assert pad >= POOL_WINDOWS[-1], "window rolls wrap into the zero rows in front"

    def body(zu, maps_ref, sc_ref, pm_ref):
        g = pl.program_id(0)
        pooled, _, _ = _pool_parts(zu[...].astype(F32), g, T, pad)
        y = _dot(pooled.astype(BF16), maps_ref[...].astype(BF16))
        pm_ref[...] = (y * sc_ref[...]).astype(BF16)

    return _call(
        body, name=name, grid=(POOL_GROUPS,),
        in_specs=_pool_specs(T, layer),
        out_specs=[pl.BlockSpec((T, HEAD_DIM), lambda g: (0, g))],
        out_shape=[jax.ShapeDtypeStruct((T, POOL_WIDTH), BF16)],
        sem=("parallel",), operands=(z, maps, scale))[0]


def _gate_specs(tm, D):
    nb = D // RET_WIDTH
    first = (4 * RET_WIDTH + POOL_WIDTH) // RET_WIDTH
    return [pl.BlockSpec((tm, RET_WIDTH), functools.partial(lambda t, j: (t, j), j=first + j)) for j in range(2 * nb)]


def _load_gates(refs, nb):
    ga = jnp.concatenate([r[...].astype(F32) for r in refs[:nb]], axis=1)
    gb = jnp.concatenate([r[...].astype(F32) for r in refs[nb:]], axis=1)
    return ga, gb


def _mix_fwd(h, r, pm, z, wru, wpu, wout, tm, name, rider=None):
    T, D = h.shape
    Dq = D // N_CHIPS
    nb = D // RET_WIDTH

    def body(*refs):
        h_ref, r_ref, pm_ref = refs[:3]
        gate_refs = refs[3:3 + 2 * nb]
        wru_ref, wpu_ref, wout_ref, ho_ref, mx_ref, ret_ref, pool_ref = refs[3 + 2 * nb:]
        rv = r_ref[...]
        pv = pm_ref[...]
        ret = jnp.concatenate([_dot(rv, wru_ref[s]) for s in range(N_CHIPS)], axis=1)
        pool = jnp.concatenate([_dot(pv, wpu_ref[s]) for s in range(N_CHIPS)], axis=1)
        ga, gb = _load_gates(gate_refs, nb)
        mixed = (_sigmoid(ga) * ret + _sigmoid(gb) * pool).astype(BF16)
        mx_ref[...] = mixed
        ret_ref[...] = ret.astype(BF16)
        pool_ref[...] = pool.astype(BF16)
        ho_ref[...] = h_ref[...] + _dot(mixed, wout_ref[...].reshape(D, D))

    row = pl.BlockSpec((tm, D), lambda t: (t, 0))
    half = pl.BlockSpec((tm, RET_WIDTH), lambda t: (t, 0))
    up = pl.BlockSpec((N_CHIPS, RET_WIDTH, Dq), lambda t: (0, 0, 0))
    return _call(
        body, name=name, grid=(T // tm,),
        in_specs=[row, half, half] + _gate_specs(tm, D) + [up, up, pl.BlockSpec((N_CHIPS, Dq, D), lambda t: (0, 0, 0))],
        out_specs=[row, row, row, row],
        out_shape=[jax.ShapeDtypeStruct((T, D), F32)] + [jax.ShapeDtypeStruct((T, D), BF16)] * 3,
        sem=("parallel",), operands=(h, r, pm, *([z] * (2 * nb)), wru, wpu, wout), rider=rider)


def _final_loss(h, gain, tgt, name):
    T, D = h.shape
    first = (T - tgt.shape[0]) // CHUNK

    def body(h_ref, g_ref, t_ref, dh_ref, loss_ref, dg_ref, dhh_ref):
        i = pl.program_id(0)

        @pl.when(i == 0)
        def _():
            loss_ref[...] = jnp.zeros_like(loss_ref)
            dg_ref[...] = jnp.zeros_like(dg_ref)

        x = h_ref[...]
        gain_v = g_ref[...]
        err = jnp.where(i >= first, _rms_fwd(x, gain_v) - t_ref[...], 0.0)
        loss_ref[...] += 0.5 * jnp.sum(jnp.mean(err * err, axis=-1))
        dx, dgain = _rms_bwd(x, gain_v, err * (1.0 / D))
        dg_ref[...] += dgain
        dh_ref[...] = dx
        dhh_ref[...] = (0.5 * dx).astype(BF16)

    return _call(
        body, name=name, grid=(T // CHUNK,),
        in_specs=[pl.BlockSpec((CHUNK, D), lambda i: (i, 0)),
                  pl.BlockSpec((1, D), lambda i: (0, 0)),
                  pl.BlockSpec((CHUNK, D), lambda i: (jnp.maximum(i - first, 0), 0))],
        out_specs=[pl.BlockSpec((CHUNK, D), lambda i: (i, 0)),
                   pl.BlockSpec((1, LANES), lambda i: (0, 0)),
                   pl.BlockSpec((1, D), lambda i: (0, 0)),
                   pl.BlockSpec((CHUNK, D), lambda i: (i, 0))],
        out_shape=[jax.ShapeDtypeStruct((T, D), F32), jax.ShapeDtypeStruct((1, LANES), F32),
                   jax.ShapeDtypeStruct((1, D), F32), jax.ShapeDtypeStruct((T, D), BF16)],
        sem=("arbitrary",), operands=(h, gain, tgt))


def _ffn_bwd_act(dy, g, u, wd, tm, name, rider=None):
    T, D = dy.shape
    Fs = wd.shape[1]
    F = N_CHIPS * Fs

    def body(dy_ref, go_ref, uo_ref, wd_ref, dg_ref, du_ref, dyh_ref):
        @pl.when(pl.program_id(1) == 0)
        def _():
            dyh_ref[...] = (0.5 * dy_ref[...]).astype(BF16)

        dact = _dot_nt(dyh_ref[...], wd_ref[...])
        du_ref[...] = (dact * uo_ref[...].astype(F32)).astype(BF16)
        dg_ref[...] = (dact * go_ref[...].astype(F32)).astype(BF16)

    row = pl.BlockSpec((tm, D), lambda t, s: (t, 0))
    col = pl.BlockSpec((tm, Fs), lambda t, s: (t, s))
    return _call(
        body, name=name, grid=(T // tm, N_CHIPS),
        in_specs=[row, col, col, pl.BlockSpec((None, Fs, D), lambda t, s: (s, 0, 0))],
        out_specs=[col, col, row],
        out_shape=[jax.ShapeDtypeStruct((T, F), BF16), jax.ShapeDtypeStruct((T, F), BF16),
                   jax.ShapeDtypeStruct((T, D), BF16)],
        sem=("parallel", "arbitrary"), operands=(dy, g, u, wd), rider=rider)


def _ffn_bwd_wgrad(dyh, g, u, act, a, wd, tk, name, rider=None):
    T, D = dyh.shape
    Fs = wd.shape[1]
    F = N_CHIPS * Fs
    nt = T // tk

    def body(dyh_ref, go_ref, uo_ref, act_ref, a_ref, wd_ref, dg_ref, du_ref, gg_ref, gu_ref, gd_ref,
             accg_ref, accu_ref, accd_ref):
        t = pl.program_id(1)

        @pl.when(t == 0)
        def _():
            accg_ref[...] = jnp.zeros_like(accg_ref)
            accu_ref[...] = jnp.zeros_like(accu_ref)
            accd_ref[...] = jnp.zeros_like(accd_ref)

        dyhv = dyh_ref[...]
        dact = _dot_nt(dyhv, wd_ref[...])
        dg = (dact * go_ref[...].astype(F32)).astype(BF16)
        du = (dact * uo_ref[...].astype(F32)).astype(BF16)
        dg_ref[...] = dg
        du_ref[...] = du
        av = a_ref[...]
        accg_ref[...] += _dot_tn(dg, av)
        accu_ref[...] += _dot_tn(du, av)
        accd_ref[...] += _dot_tn(act_ref[...], dyhv)

        @pl.when(t == nt - 1)
        def _():
            gg_ref[...] = accg_ref[...].astype(BF16)
            gu_ref[...] = accu_ref[...].astype(BF16)
            gd_ref[...] = accd_ref[...].astype(BF16)

    row = pl.BlockSpec((tk, D), lambda s, t: (t, 0))
    col = pl.BlockSpec((tk, Fs), lambda s, t: (t, s))
    shard = pl.BlockSpec((None, Fs, D), lambda s, t: (s, 0, 0))
    return _call(
        body, name=name, grid=(N_CHIPS, nt),
        in_specs=[row, col, col, col, row, shard],
        out_specs=[col, col, shard, shard, shard],
        out_shape=[jax.ShapeDtypeStruct((T, F), BF16)] * 2 + [jax.ShapeDtypeStruct((N_CHIPS, Fs, D), BF16)] * 3,
        scratch=[pltpu.VMEM((Fs, D), F32)] * 3,
        sem=("parallel", "arbitrary"), operands=(dyh, g, u, act, a, wd), rider=rider)


def _ffn_bwd_in(dy, h, gain, dg, du, wg, wu, layer, tm, pad, name, rider=None):
    T, D = h.shape
    Fs = wg.shape[-1]

    def body(dy_ref, h_ref, g_ref, dg_ref, du_ref, wg_ref, wu_ref, dh_ref, dgain_ref, dhh_ref, da_ref):
        t = pl.program_id(0)
        s = pl.program_id(1)

        @pl.when((t == 0) & (s == 0))
        def _():
            dgain_ref[...] = jnp.zeros_like(dgain_ref)

        @pl.when(s == 0)
        def _():
            da_ref[...] = jnp.zeros_like(da_ref)

        da_ref[...] += _dot_nt(dg_ref[...], wg_ref[...]) + _dot_nt(du_ref[...], wu_ref[...])

        @pl.when(s == N_CHIPS - 1)
        def _():
            dx, dgain = _rms_bwd(h_ref[...], g_ref[...], da_ref[...])
            dgain_ref[...] += dgain
            dh = jnp.where(_row_mask(t, tm, pad, (tm, D)), dy_ref[...] + dx, 0.0)
            dh_ref[...] = dh
            dhh_ref[...] = (0.5 * dh).astype(BF16)

    row = pl.BlockSpec((tm, D), lambda t, s: (t, 0))
    col = pl.BlockSpec((tm, Fs), lambda t, s: (t, s))
    wcol = pl.BlockSpec((None, D, Fs), lambda t, s: (s, 0, 0))
    return _call(
        body, name=name, grid=(T // tm, N_CHIPS),
        in_specs=[row, row, pl.BlockSpec((None, 1, D), lambda t, s: (layer, 0, 0)), col, col, wcol, wcol],
        out_specs=[row, pl.BlockSpec((1, D), lambda t, s: (0, 0)), row],
        out_shape=[jax.ShapeDtypeStruct((T, D), F32), jax.ShapeDtypeStruct((1, D), F32),
                   jax.ShapeDtypeStruct((T, D), BF16)],
        scratch=[pltpu.VMEM((tm, D), F32)],
        sem=("arbitrary", "arbitrary"), operands=(dy, h, gain, dg, du, wg, wu), rider=rider)


def _grad_tn(a, b, mode, scale, tm, name, rider=None):
    T = a.shape[0]
    if mode == "col":
        per, R, C = 1, a.shape[1], b.shape[1] // N_CHIPS
        a_spec = pl.BlockSpec((tm, R), lambda s, t: (t, 0))
        b_spec = pl.BlockSpec((tm, C), lambda s, t: (t, s))
    else:
        per, R, C = 2, a.shape[1] // N_CHIPS, b.shape[1]
        a_spec = pl.BlockSpec((tm, per * R), lambda s, t: (t, s))
        b_spec = pl.BlockSpec((tm, C), lambda s, t: (t, 0))
    nt = T // tm

    def body(a_ref, b_ref, o_ref, acc_ref):
        t = pl.program_id(1)

        @pl.when(t == 0)
        def _():
            acc_ref[...] = jnp.zeros_like(acc_ref)

        acc_ref[...] += _dot_tn(a_ref[...].astype(BF16), b_ref[...].astype(BF16))

        @pl.when(t == nt - 1)
        def _():
            o_ref[...] = (scale * acc_ref[...]).astype(BF16).reshape(per, R, C)

    return _call(
        body, name=name, grid=(N_CHIPS // per, nt),
        in_specs=[a_spec, b_spec],
        out_specs=[pl.BlockSpec((per, R, C), lambda s, t: (s, 0, 0))],
        out_shape=[jax.ShapeDtypeStruct((N_CHIPS, R, C), BF16)],
        scratch=[pltpu.VMEM((per * R, C), F32)],
        sem=("parallel", "arbitrary"), operands=(a, b), rider=rider)[0]


def _mix_bwd_dx(dh, z, ret, pool, wout, wru, wpu, tm, name, rider=None):
    T, D = dh.shape
    Dq = D // N_CHIPS
    nb = D // RET_WIDTH

    def body(*refs):
        dh_ref = refs[0]
        gate_refs = refs[1:1 + 2 * nb]
        ret_ref, pool_ref, wout_ref, wru_ref, wpu_ref, dgab_ref, dret_ref, dpool_ref, dr_ref, dpm_ref = refs[1 + 2 * nb:]
        dmixed = _dot_nt(dh_ref[...].astype(BF16), wout_ref[...].reshape(D, D))
        ga, gb = _load_gates(gate_refs, nb)
        sa = _sigmoid(ga)
        sb = _sigmoid(gb)
        dgab_ref[:, :D] = (dmixed * ret_ref[...].astype(F32) * (sa * (1.0 - sa))).astype(BF16)
        dgab_ref[:, D:] = (dmixed * pool_ref[...].astype(F32) * (sb * (1.0 - sb))).astype(BF16)
        dret = (dmixed * sa).astype(BF16)
        dpool = (dmixed * sb).astype(BF16)
        dret_ref[...] = dret
        dpool_ref[...] = dpool
        dr = _dot_nt(dret[:, :Dq], wru_ref[0])
        dpm = _dot_nt(dpool[:, :Dq], wpu_ref[0])
        for s in range(1, N_CHIPS):
            dr += _dot_nt(dret[:, s * Dq:(s + 1) * Dq], wru_ref[s])
            dpm += _dot_nt(dpool[:, s * Dq:(s + 1) * Dq], wpu_ref[s])
        dr_ref[...] = dr
        dpm_ref[...] = dpm

    row = pl.BlockSpec((tm, D), lambda t: (t, 0))
    half = pl.BlockSpec((tm, RET_WIDTH), lambda t: (t, 0))
    up = pl.BlockSpec((N_CHIPS, RET_WIDTH, Dq), lambda t: (0, 0, 0))
    return _call(
        body, name=name, grid=(T // tm,),
        in_specs=[row] + _gate_specs(tm, D) + [row, row, pl.BlockSpec((N_CHIPS, Dq, D), lambda t: (0, 0, 0)), up, up],
        out_specs=[pl.BlockSpec((tm, 2 * D), lambda t: (t, 0)), row, row, half, half],
        out_shape=[jax.ShapeDtypeStruct((T, 2 * D), BF16), jax.ShapeDtypeStruct((T, D), BF16),
                   jax.ShapeDtypeStruct((T, D), BF16), jax.ShapeDtypeStruct((T, RET_WIDTH), F32),
                   jax.ShapeDtypeStruct((T, POOL_WIDTH), F32)],
        sem=("parallel",), operands=(dh, *([z] * (2 * nb)), ret, pool, wout, wru, wpu), rider=rider)


def _pool_bwd(z, dpm, maps, scale, layer, pad, name):
    T = z.shape[0]

    def body(zu, maps_ref, sc_ref, dpm_ref, du_ref, dmaps_ref, dsc_ref):
        g = pl.program_id(0)
        u = zu[...].astype(F32)
        pooled, div, valid = _pool_parts(u, g, T, pad)
        pb = pooled.astype(BF16)
        mb = maps_ref[...].astype(BF16)
        dp = dpm_ref[...]
        dsc_ref[...] = jnp.sum(dp * _dot(pb, mb), axis=0, keepdims=True)
        dyb = (dp * sc_ref[...]).astype(BF16)
        dmaps_ref[...] = _dot_tn(pb, dyb)
        dpooled = jnp.where(valid, _dot_nt(dyb, mb), 0.0)
        ahead = _select_group(_window_sums(dpooled / div, lambda k: T - k), g)
        du_ref[...] = jnp.where(valid, ahead - dpooled, 0.0).astype(BF16)

    blk = pl.BlockSpec((T, HEAD_DIM), lambda g: (0, g))
    return _call(
        body, name=name, grid=(POOL_GROUPS,),
        in_specs=_pool_specs(T, layer) + [blk],
        out_specs=[blk, pl.BlockSpec((None, HEAD_DIM, HEAD_DIM), lambda g: (g, 0, 0)),
                   pl.BlockSpec((1, HEAD_DIM), lambda g: (0, g))],
        out_shape=[jax.ShapeDtypeStruct((T, POOL_WIDTH), BF16),
                   jax.ShapeDtypeStruct((POOL_GROUPS, HEAD_DIM, HEAD_DIM), F32),
                   jax.ShapeDtypeStruct((1, POOL_WIDTH), F32)],
        sem=("parallel",), operands=(z, maps, scale, dpm))


def _ret_bwd_local(z, o_pre, s_all, dr, consts, cg, name):
    T = z.shape[0]
    N = T // CHUNK
    ng = N // cg
    tg = cg * CHUNK
    cosf, sinf, intra, _, qdec, _ = consts
    fwd = lambda g: g

    def body(zq, zk, zv, zg, o_ref, s_ref, dr_ref, cos_ref, sin_ref, m_ref, qd_ref,
             dq_ref, dg_ref, dk_ref, dv_ref, ds_ref):
        cosv = cos_ref[...]
        sinv = sin_ref[...]
        scale = HEAD_DIM ** -0.5
        q3 = (_rot(zq[...].astype(F32), cosv, sinv) * scale).reshape(cg, CHUNK, HEAD_DIM)
        k3 = _rot(zk[...].astype(F32), cosv, sinv).reshape(cg, CHUNK, HEAD_DIM)
        qb = q3.astype(BF16)
        kb = k3.astype(BF16)
        vb = zv[...].reshape(cg, CHUNK, HEAD_DIM).astype(BF16)
        mask = m_ref[...][None]
        sb = (_ein("ncd,nmd->ncm", qb, kb) * mask).astype(BF16)
        qdv = qd_ref[...][None]
        qdb = (q3 * qdv).astype(BF16)

        out = o_ref[...]
        xc = out - jnp.mean(out, axis=-1, keepdims=True)
        rstd = lax.rsqrt(jnp.mean(xc * xc, axis=-1, keepdims=True) + EPS)
        rn = xc * rstd
        g = zg[...].astype(F32)
        sg = _sigmoid(g)
        drv = dr_ref[...]
        dg_ref[...] = (drv * rn * (sg * (1.0 + g * (1.0 - sg)))).astype(BF16)
        drn = drv * (g * sg)
        dout = rstd * (drn - jnp.mean(drn, axis=-1, keepdims=True)
                       - rn * jnp.mean(drn * rn, axis=-1, keepdims=True))
        dob = dout.reshape(cg, CHUNK, HEAD_DIM).astype(BF16)

        dsb = (_ein("ncd,nmd->ncm", dob, vb) * mask).astype(BF16)
        dv_ref[...] = _ein("ncm,ncd->nmd", sb, dob).reshape(tg, HEAD_DIM)
        dk_ref[...] = _ein("ncm,ncd->nmd", dsb, qb).reshape(tg, HEAD_DIM)
        dq3 = _ein("ncm,nmd->ncd", dsb, kb) + _ein("nce,nde->ncd", dob, s_ref[...].astype(BF16)) * qdv
        dq_ref[...] = _rot_t(dq3.reshape(tg, HEAD_DIM) * scale, cosv, sinv).astype(BF16)
        ds_ref[...] = _ein("ncd,nce->nde", qdb, dob)

    tab = pl.BlockSpec((tg, HEAD_DIM), lambda h, g: (g, 0))
    per_head = pl.BlockSpec((None, CHUNK, HEAD_DIM), lambda h, g: (h, 0, 0))
    head_blk = pl.BlockSpec((tg, HEAD_DIM), lambda h, g: (g, h))
    state_blk = pl.BlockSpec((None, cg, HEAD_DIM, HEAD_DIM), lambda h, g: (h, g, 0, 0))
    return _call(
        body, name=name, grid=(RET_HEADS, ng),
        in_specs=[_head_specs(tg, i, fwd) for i in range(4)]
        + [head_blk, state_blk, head_blk, tab, tab, per_head, per_head],
        out_specs=[head_blk, head_blk, head_blk, head_blk, state_blk],
        out_shape=[jax.ShapeDtypeStruct((T, RET_WIDTH), BF16), jax.ShapeDtypeStruct((T, RET_WIDTH), BF16),
                   jax.ShapeDtypeStruct((T, RET_WIDTH), F32), jax.ShapeDtypeStruct((T, RET_WIDTH), F32),
                   jax.ShapeDtypeStruct((RET_HEADS, N, HEAD_DIM, HEAD_DIM), F32)],
        sem=("parallel", "parallel"), operands=(z, z, z, z, o_pre, s_all, dr, cosf, sinf, intra, qdec))


def _ret_bwd_state(z, dkp, dvp, ds, consts, cg, name):
    T = z.shape[0]
    N = T // CHUNK
    ng = N // cg
    tg = cg * CHUNK
    cosf, sinf, _, kdec, _, cdb = consts
    rev = lambda g: ng - 1 - g

    def body(zk, zv, dkp_ref, dvp_ref, ds_ref, cos_ref, sin_ref, kd_ref, cd_ref, dk_ref, dv_ref, gs_ref, dkv_ref):
        @pl.when(pl.program_id(1) == 0)
        def _():
            gs_ref[...] = jnp.zeros_like(gs_ref)

        cosv = cos_ref[...]
        sinv = sin_ref[...]
        cd = cd_ref[0:1, :]
        grad = gs_ref[...]
        for n in reversed(range(cg)):
            dkv_ref[n] = grad
            grad = ds_ref[n] + cd * grad
        gs_ref[...] = grad
        dkvb = dkv_ref[...].astype(BF16)
        kdv = kd_ref[...][None]
        k3 = _rot(zk[...].astype(F32), cosv, sinv).reshape(cg, CHUNK, HEAD_DIM)
        vb = zv[...].reshape(cg, CHUNK, HEAD_DIM).astype(BF16)
        dk3 = _ein("nce,nde->ncd", vb, dkvb) * kdv
        dv3 = _ein("ncd,nde->nce", (k3 * kdv).astype(BF16), dkvb)
        dk_ref[...] = _rot_t(dkp_ref[...] + dk3.reshape(tg, HEAD_DIM), cosv, sinv).astype(BF16)
        dv_ref[...] = (dvp_ref[...] + dv3.reshape(tg, HEAD_DIM)).astype(BF16)

    tab = pl.BlockSpec((tg, HEAD_DIM), lambda h, g: (rev(g), 0))
    head_blk = pl.BlockSpec((tg, HEAD_DIM), lambda h, g: (rev(g), h))
    return _call(
        body, name=name, grid=(RET_HEADS, ng),
        in_specs=[_head_specs(tg, 1, rev), _head_specs(tg, 2, rev), head_blk, head_blk,
                  pl.BlockSpec((None, cg, HEAD_DIM, HEAD_DIM), lambda h, g: (h, rev(g), 0, 0)),
                  tab, tab,
                  pl.BlockSpec((None, CHUNK, HEAD_DIM), lambda h, g: (h, 0, 0)),
                  pl.BlockSpec((None, 8, HEAD_DIM), lambda h, g: (h, 0, 0))],
        out_specs=[head_blk, head_blk],
        out_shape=[jax.ShapeDtypeStruct((T, RET_WIDTH), BF16)] * 2,
        scratch=[pltpu.VMEM((HEAD_DIM, HEAD_DIM), F32), pltpu.VMEM((cg, HEAD_DIM, HEAD_DIM), F32)],
        sem=("parallel", "arbitrary"), operands=(z, z, dkp, dvp, ds, cosf, sinf, kdec, cdb))


def _inproj_bwd_dx(dz, win, h, gain, dh_in, layer, tm, pad, name, rider=None):
    T, D = h.shape
    Ns = win.shape[-1]

    def body(dz_ref, w_ref, h_ref, g_ref, dhi_ref, dh_ref, dgain_ref, dhh_ref, db_ref):
        t = pl.program_id(0)
        s = pl.program_id(1)

        @pl.when((t == 0) & (s == 0))
        def _():
            dgain_ref[...] = jnp.zeros_like(dgain_ref)

        @pl.when(s == 0)
        def _():
            db_ref[...] = jnp.zeros_like(db_ref)

        db_ref[...] += _dot_nt(dz_ref[...], w_ref[...])

        @pl.when(s == N_CHIPS - 1)
        def _():
            dx, dgain = _rms_bwd(h_ref[...], g_ref[...], db_ref[...])
            dgain_ref[...] += dgain
            dh = jnp.where(_row_mask(t, tm, pad, (tm, D)), dhi_ref[...] + dx, 0.0)
            dh_ref[...] = dh
            dhh_ref[...] = (0.5 * dh).astype(BF16)

    row = pl.BlockSpec((tm, D), lambda t, s: (t, 0))
    return _call(
        body, name=name, grid=(T // tm, N_CHIPS),
        in_specs=[pl.BlockSpec((tm, Ns), lambda t, s: (t, s)),
                  pl.BlockSpec((None, D, Ns), lambda t, s: (s, 0, 0)),
                  row, pl.BlockSpec((None, 1, D), lambda t, s: (layer, 0, 0)), row],
        out_specs=[row, pl.BlockSpec((1, D), lambda t, s: (0, 0)), row],
        out_shape=[jax.ShapeDtypeStruct((T, D), F32), jax.ShapeDtypeStruct((1, D), F32),
                   jax.ShapeDtypeStruct((T, D), BF16)],
        scratch=[pltpu.VMEM((tm, D), F32)],
        sem=("arbitrary", "arbitrary"), operands=(dz, win, h, gain, dh_in), rider=rider)


def _sum_pair(gs, rs, c_idx, name):
    n = len(gs)

    def body(c_ref, *refs):
        for g_ref, r_ref, o_ref in zip(refs[:n], refs[n:2 * n], refs[2 * n:]):
            o_ref[...] = (g_ref[...].astype(F32) + r_ref[...].astype(F32)).astype(BF16)

    halves = [pl.BlockSpec((None,) + r.shape[1:], lambda s, c_ref: (s, 0, 0)) for r in rs]
    return pl.pallas_call(
        body,
        name=name,
        grid_spec=pltpu.PrefetchScalarGridSpec(
            num_scalar_prefetch=1,
            grid=(N_CHIPS,),
            in_specs=[pl.BlockSpec((None,) + r.shape[1:], lambda s, c_ref: (s, c_ref[0], 0)) for r in rs] + halves,
            out_specs=halves,
        ),
        out_shape=[jax.ShapeDtypeStruct(r.shape, BF16) for r in rs],
        compiler_params=_params(("parallel",)),
    )(c_idx, *gs, *rs)


def _sum_chips(ps, rs, pos, name):
    n = len(ps)
    quarters = 4

    def body(pos_ref, *refs):
        chip = pos_ref[0]
        for p_ref, r_ref, o_ref in zip(refs[:n], refs[n:2 * n], refs[2 * n:]):
            own = p_ref[...].astype(F32)
            terms = [jnp.where(chip == k, own, r_ref[k].astype(F32)) for k in range(N_CHIPS)]
            o_ref[...] = ((terms[0] + terms[1]) + terms[2]) + terms[3]

    def rows(r):
        assert r.shape[1] % (quarters * BF16_ROWS) == 0, r.shape
        return r.shape[1] // quarters

    return pl.pallas_call(
        body,
        name=name,
        grid_spec=pltpu.PrefetchScalarGridSpec(
            num_scalar_prefetch=1,
            grid=(quarters,),
            in_specs=[pl.BlockSpec((None, rows(r), r.shape[2]), lambda q, pos_ref: (pos_ref[0], q, 0)) for r in rs]
            + [pl.BlockSpec((N_CHIPS, rows(r), r.shape[2]), lambda q, pos_ref: (0, q, 0)) for r in rs],
            out_specs=[pl.BlockSpec((rows(r), r.shape[2]), lambda q, pos_ref: (pos_ref[1] * quarters + q, 0))
                       for r in rs],
        ),
        out_shape=[jax.ShapeDtypeStruct((2 * r.shape[1], r.shape[2]), F32) for r in rs],
        compiler_params=_params(("arbitrary",)),
    )(pos, *ps, *rs)


def _small_all_reduce(p):
    rows, width = p.shape

    def body(p_ref, o_ref, sib_ref, slot_ref, ssem, rsem):
        x, y, c, chip, others = _mesh_pos()
        pair = _remote(p_ref, sib_ref, ssem.at[0], rsem.at[0], (x, y, 1 - c))
        pair.start()
        pair.wait()
        slot_ref[chip] = p_ref[...] + sib_ref[...]
        sends = []
        for j, (ox, oy) in enumerate(others):
            cp = _remote(slot_ref.at[chip], slot_ref.at[chip], ssem.at[1 + j], rsem.at[1 + j], (ox, oy, c))
            cp.start()
            sends.append(cp)
        for j, (ox, oy) in enumerate(others):
            slot = slot_ref.at[2 * ox + oy]
            _remote(slot, slot, ssem.at[1 + j], rsem.at[1 + j], (ox, oy, c)).wait_recv()
        for cp in sends:
            cp.wait_send()
        o_ref[...] = ((slot_ref[0] + slot_ref[1]) + slot_ref[2]) + slot_ref[3]

    vmem = pl.BlockSpec(memory_space=pltpu.VMEM)
    return pl.pallas_call(
        body,
        name="small_grads_all_reduce",
        in_specs=[vmem],
        out_specs=vmem,
        out_shape=jax.ShapeDtypeStruct(p.shape, F32),
        scratch_shapes=[pltpu.VMEM((rows, width), F32), pltpu.VMEM((N_CHIPS, rows, width), F32),
                        pltpu.SemaphoreType.DMA((4,)), pltpu.SemaphoreType.DMA((4,))],
    )(p)


def _adamw(gs, w, m, v, name):
    L, R, C = w.shape
    Ct = gs[0].shape[1]
    tr = _pick_tile(R, 256, 8)

    def body(*refs):
        g_refs = refs[:L]
        w_ref, m_ref, v_ref, go_ref, d_ref, mo_ref, vo_ref = refs[L:]
        layer = pl.program_id(0)
        grad = g_refs[L - 1][...]
        for i in range(L - 2, -1, -1):
            grad = jnp.where(layer == i, g_refs[i][...], grad)
        if Ct != C:
            grad = grad[:, :C]
        m_new = ADAM_B1 * m_ref[...] + (1.0 - ADAM_B1) * grad
        v_new = ADAM_B2 * v_ref[...] + (1.0 - ADAM_B2) * jnp.square(grad)
        m_hat = m_new / (1.0 - ADAM_B1 ** ADAM_STEP)
        v_hat = v_new / (1.0 - ADAM_B2 ** ADAM_STEP)
        go_ref[...] = grad
        d_ref[...] = -ADAM_LR * (m_hat / (jnp.sqrt(v_hat) + ADAM_EPS) + ADAM_WD * w_ref[...])
        mo_ref[...] = m_new
        vo_ref[...] = v_new

    g_specs = [pl.BlockSpec((tr, Ct), functools.partial(lambda l, r, i: (jnp.where(l == i, r, 0), 0), i=i))
               for i in range(L)]
    blk = pl.BlockSpec((None, tr, C), lambda l, r: (l, r, 0))
    return pl.pallas_call(
        body,
        name=name,
        grid=(L, R // tr),
        in_specs=g_specs + [blk, blk, blk],
        out_specs=[blk] * 4,
        out_shape=[jax.ShapeDtypeStruct((L, R, C), F32)] * 4,
        compiler_params=_params(("arbitrary", "arbitrary")),
    )(*gs, w, m, v)


_FFN1 = ("ffn1_gate", "ffn1_up", "ffn1_down")
_FFN2 = ("ffn2_gate", "ffn2_up", "ffn2_down")
_MIXW = ("w_ret_up", "w_pool_up", "w_out")
_BIG = _FFN1 + ("w_in",) + _MIXW + _FFN2
_TRANSPOSED = ("ffn1_gate", "ffn1_up", "ffn2_gate", "ffn2_up")
_SMALL = ("ffn1_norm", "mix_norm", "ffn2_norm", "final_norm", "pool_scale", "pool_maps")
_ORDER = ("meta", "ffn1_norm", "ffn1_gate", "ffn1_up", "ffn1_down", "mix_norm", "w_in", "pool_maps",
          "pool_scale", "w_ret_up", "w_pool_up", "w_out", "ffn2_norm", "ffn2_gate", "ffn2_up", "ffn2_down",
          "final_norm")


def _transport(a):
    n, r, c = a.shape
    out = a.astype(BF16)
    if c % LANES:
        out = jnp.concatenate([out, jnp.zeros((n, r, _round_up(c, LANES) - c), BF16)], axis=2)
    if r % LANES:
        out = jnp.concatenate([out, jnp.zeros((n, _round_up(r, LANES) - r, out.shape[2]), BF16)], axis=1)
    return out


def _pack_rows(parts, width):
    rows = [p.reshape(-1, width) for p in parts]
    total = sum(r.shape[0] for r in rows)
    fill = _round_up(total, 8) - total
    if fill:
        rows.append(jnp.zeros((fill, width), F32))
    return jnp.concatenate(rows, axis=0)


def _unpack_rows(packed, shapes, width):
    out, at = [], 0
    for shp in shapes:
        n = math.prod(shp) // width
        out.append(packed[at:at + n].reshape(shp))
        at += n
    return out


class _Weights:
    def __init__(self, shards):
        self.shards = shards
        self.full = {}

    def rider(self, keys):
        r = _gather_rider([(self.shards[n], i) for n, i in keys])
        r.keys = keys
        return r

    def take(self, rider):
        for key, arr in zip(rider.keys, rider.results):
            self.full[key] = arr

    def __call__(self, name, layer):
        return self.full[(name, layer)]


def _local_step(x, meta_full, tgt, w, wts, pad, tm, cg, reducer):
    D = x.shape[1]
    T = pad + N_META + x.shape[0]
    L = w["ffn1_norm"].shape[0]
    pool_maps = w["pool_maps"]
    gains = {n: w[n].reshape(L, 1, D) for n in ("ffn1_norm", "mix_norm", "ffn2_norm")}
    scale3 = w["pool_scale"].reshape(L, 1, POOL_WIDTH)
    consts = _ret_consts(T, pad)
    tl = _pick_tile(T, 2 * tm, BF16_ROWS)
    def gather(keys):
        return wts.rider(keys) if keys and keys[0] not in wts.full else None

    def done(rider):
        if rider is not None:
            wts.take(rider)

    h = jnp.concatenate([jnp.zeros((pad, D), F32), meta_full, x], axis=0)
    saved = []
    for i in range(L):
        s = {"h0": h}
        rd = gather([("w_in", i)] + [(n, i) for n in _MIXW])
        h, s["a1"], s["g1"], s["u1"], s["act1"] = _ffn_fwd(
            h, gains["ffn1_norm"], wts("ffn1_gate", i), wts("ffn1_up", i), wts("ffn1_down", i), i, tl,
            f"ffn1_fwd_{i}", rd)
        done(rd)
        s["h1"] = h
        rd = gather([("ffn2_gate", i), ("ffn2_up", i)])
        s["z"], s["b"] = _inproj_fwd(h, gains["mix_norm"], wts("w_in", i), i, tl, f"inproj_fwd_{i}", rd)
        done(rd)
        s["r"], s["o_pre"], s["s_all"] = _ret_fwd(s["z"], consts, cg, f"retention_fwd_{i}")
        s["pm"] = _pool_fwd(s["z"], pool_maps, scale3, i, pad, f"pool_fwd_{i}")
        rd = gather([("ffn2_down", i)])
        h, s["mixed"], s["ret"], s["pool"] = _mix_fwd(
            h, s["r"], s["pm"], s["z"], wts("w_ret_up", i), wts("w_pool_up", i), wts("w_out", i), tm,
            f"mix_fwd_{i}", rd)
        done(rd)
        s["h2"] = h
        rd = gather([(n, i + 1) for n in _FFN1]) if i + 1 < L else None
        h, s["a2"], s["g2"], s["u2"], s["act2"] = _ffn_fwd(
            h, gains["ffn2_norm"], wts("ffn2_gate", i), wts("ffn2_up", i), wts("ffn2_down", i), i, tl,
            f"ffn2_fwd_{i}", rd)
        done(rd)
        saved.append(s)

    dh, loss_acc, d_final, dhh = _final_loss(h, w["final_norm"].reshape(1, D), tgt, "final_norm_loss")

    small = {n: [None] * L for n in ("ffn1_norm", "mix_norm", "ffn2_norm", "pool_scale", "pool_maps")}

    carry = {"ffn_act": 1.0, "ffn_wgrad": 3.0, "ffn_in": 2.2, "mix_bwd": 1.0, "inproj_bwd": 1.5, "w_in": 1.0}

    tk = _pick_tile(T, 1408, LANES)
    tw = _pick_tile(T, 704, LANES)

    def grad(n, a, b, i, mode):
        rd = reducer.rider(carry.get(n, 1.0 if i == 0 and n.startswith("ffn") else 0.5))
        reducer.add(n, i, _grad_tn(a, b, mode, 1.0, tk, f"grad_{n}_{i}", rd))
        reducer.done(rd)

    def ffn_bwd(which, dy, dyh, s, k, i, last):
        g, u, a, act = s[f"g{k}"], s[f"u{k}"], s[f"a{k}"], s[f"act{k}"]
        if last:
            rd = reducer.rider(carry["ffn_act"])
            dg, du, _ = _ffn_bwd_act(dy, g, u, wts(f"{which}_down", i), tl, f"{which}_bwd_act_{i}", rd)
            reducer.done(rd)
        else:
            rd = reducer.rider(carry["ffn_wgrad"])
            dg, du, gg, gu, gd = _ffn_bwd_wgrad(dyh, g, u, act, a, wts(f"{which}_down", i), tw,
                                                f"{which}_bwd_wgrad_{i}", rd)
            reducer.done(rd)
            for n, grad_n in ((f"{which}_gate", gg), (f"{which}_up", gu), (f"{which}_down", gd)):
                reducer.add(n, i, grad_n)
            reducer.stage(f"{which}_{i}")
        rd = reducer.rider(carry["ffn_in"])
        dh_in, dgain, dhh_in = _ffn_bwd_in(
            dy, s["h2" if k == 2 else "h0"], gains[f"{which}_norm"], dg, du, wts(f"{which}_gate", i),
            wts(f"{which}_up", i), i, tl, pad, f"{which}_bwd_in_{i}", rd)
        reducer.done(rd)
        if last:
            grad(f"{which}_gate", dg, a, i, "row")
            reducer.stage("gate_last")
            grad(f"{which}_up", du, a, i, "row")
            reducer.stage("up_last")
            grad(f"{which}_down", act, dyh, i, "row")
            reducer.stage("down_last")
        return dh_in, dgain, dhh_in

    for i in reversed(range(L)):
        s = saved[i]
        dh, small["ffn2_norm"][i], _ = ffn_bwd("ffn2", dh, dhh, s, 2, i, False)
        rd = reducer.rider(carry["mix_bwd"])
        dgab, dret, dpool, dr, dpm = _mix_bwd_dx(
            dh, s["z"], s["ret"], s["pool"], wts("w_out", i), wts("w_ret_up", i), wts("w_pool_up", i), tm,
            f"mix_bwd_{i}", rd)
        reducer.done(rd)
        grad("w_out", s["mixed"], dh, i, "row")
        grad("w_ret_up", s["r"], dret, i, "col")
        grad("w_pool_up", s["pm"], dpool, i, "col")
        du_pool, small["pool_maps"][i], small["pool_scale"][i] = _pool_bwd(
            s["z"], dpm, pool_maps, scale3, i, pad, f"pool_bwd_{i}")
        dq, dgr, dkp, dvp, ds = _ret_bwd_local(s["z"], s["o_pre"], s["s_all"], dr, consts, cg, f"retention_bwd_{i}")
        dk, dv = _ret_bwd_state(s["z"], dkp, dvp, ds, consts, cg, f"retention_bwd_state_{i}")
        dz = jnp.concatenate([dq, dk, dv, dgr, du_pool, dgab], axis=1)
        dh2 = dh
        rd = reducer.rider(carry["inproj_bwd"])
        dh, small["mix_norm"][i], dhh = _inproj_bwd_dx(
            dz, wts("w_in", i), s["h1"], gains["mix_norm"], dh2, i, tl, pad, f"inproj_bwd_{i}", rd)
        reducer.done(rd)
        grad("w_in", s["b"], dz, i, "col")
        reducer.stage(f"mid{i}")
        dh, small["ffn1_norm"][i], dhh = ffn_bwd("ffn1", dh, dhh, s, 1, i, i == 0)

    return loss_acc, dh, small, d_final


class _Reducer:
    def __init__(self, unit):
        self.c_idx = lax.axis_index("c").astype(jnp.int32).reshape(1)
        chip = 2 * lax.axis_index("x") + lax.axis_index("y")
        self.pos = jnp.stack([chip, lax.axis_index("c")]).astype(jnp.int32)
        self.pending, self.stages, self.queue, self.halves = [], [], [], {}
        self.unit = unit
        self.calls = 0

    def add(self, name, layer, g):
        self.pending.append(((name, layer), g))

    def stage(self, tag):
        if self.pending:
            self.stages.append((tag, self.pending))
            self.pending = []

    def _pair_rider(self):
        if not self.stages:
            return None
        tag, items = self.stages.pop(0)
        rd = _pair_exchange_rider([g for _, g in items])
        rd.tag, rd.keys = tag, [k for k, _ in items]
        return rd

    def _chip_rider(self, units):
        take, size = [], 0
        while self.queue and (units is None or size + self.queue[0][1].size <= units * self.unit):
            take.append(self.queue.pop(0))
            size += take[-1][1].size
        if not take:
            return None
        rd = _chip_exchange_rider([p for _, p in take])
        rd.keys = [k for k, _ in take]
        return rd

    def rider(self, units):
        self.riding = (self._pair_rider(), self._chip_rider(units))
        return _join(self.riding)

    def done(self, rd):
        if rd is None:
            return
        _split_results(rd)
        pair, chips = self.riding
        if len([r for r in self.riding if r is not None]) == 1:
            (pair or chips).results = rd.results
        self.calls += 1
        if pair is not None:
            sums = _sum_pair(pair.ins, pair.results, self.c_idx, f"sum_pair_{pair.tag}")
            self.queue += list(zip(pair.keys, sums))
        if chips is not None:
            sums = _sum_chips(chips.ins, chips.results, self.pos, f"sum_chips_{self.calls}")
            self.halves.update(zip(chips.keys, sums))

    def finish(self):
        assert not self.pending
        while self.stages or self.queue:
            self.riding = (self._pair_rider(), self._chip_rider(None))
            rd = _join(self.riding)
            _run_rider(rd, f"grads_exchange_tail_{self.calls}")
            self.done(rd)
        keys = list(self.halves)
        return dict(zip(keys, _pair_gather([self.halves[k] for k in keys])))


def _update(loss, grad_x, d_meta_rows, shard_grads, small, d_final, w, mom, var):
    meta = w["meta"]
    D = w["final_norm"].shape[0]
    L = w["ffn1_norm"].shape[0]
    Dq = D // N_CHIPS

    small_parts = [jnp.concatenate(small[n], axis=0) for n in ("ffn1_norm", "mix_norm", "ffn2_norm")]
    small_parts += [d_final, jnp.concatenate(small["pool_scale"], axis=0), jnp.concatenate(small["pool_maps"], axis=0)]
    reduced = _small_all_reduce(_pack_rows(small_parts + [d_meta_rows], D))
    small_shapes = [w[n].shape for n in _SMALL]
    small_rows = sum(math.prod(shp) for shp in small_shapes) // D
    chip = 2 * lax.axis_index("x") + lax.axis_index("y")
    d_meta = lax.dynamic_slice_in_dim(reduced[small_rows:small_rows + N_META], chip * Dq, Dq, axis=1)

    out = {}
    for n in _BIG:
        gs = [shard_grads[(n, i)] for i in range(L)]
        if n in _TRANSPOSED:
            res = _adamw(gs, *(jnp.swapaxes(t[n], 1, 2) for t in (w, mom, var)), f"adamw_{n}")
            out[n] = [jnp.swapaxes(r, 1, 2) for r in res]
        else:
            out[n] = _adamw(gs, w[n], mom[n], var[n], f"adamw_{n}")
    names = _SMALL + ("meta",)
    packed_g = _pack_rows([reduced[:small_rows], d_meta], D)
    packed = [_pack_rows([t[n] for n in names], D) for t in (w, mom, var)]
    res = _adamw([packed_g], packed[0][None], packed[1][None], packed[2][None], "adamw_small")
    shapes = small_shapes + [meta.shape]
    unpacked = [_unpack_rows(r[0], shapes, D) for r in res]
    for k, n in enumerate(names):
        out[n] = tuple(u[k] for u in unpacked)

    return (loss, grad_x) + tuple(out[n][j] for j in range(4) for n in _ORDER)


def kernel(x, meta, ffn1_norm, ffn1_gate, ffn1_up, ffn1_down, mix_norm, w_in, pool_maps, pool_scale, w_ret_up, w_pool_up, w_out, ffn2_norm, ffn2_gate, ffn2_up, ffn2_down, final_norm, loss_target, m_meta, m_ffn1_norm, m_ffn1_gate, m_ffn1_up, m_ffn1_down, m_mix_norm, m_w_in, m_pool_maps, m_pool_scale, m_w_ret_up, m_w_pool_up, m_w_out, m_ffn2_norm, m_ffn2_gate, m_ffn2_up, m_ffn2_down, m_final_norm, v_meta, v_ffn1_norm, v_ffn1_gate, v_ffn1_up, v_ffn1_down, v_mix_norm, v_w_in, v_pool_maps, v_pool_scale, v_w_ret_up, v_w_pool_up, v_w_out, v_ffn2_norm, v_ffn2_gate, v_ffn2_up, v_ffn2_down, v_final_norm):
    args = dict(locals())
    w = {n: args[n] for n in _ORDER}
    mom = {n: args["m_" + n] for n in _ORDER}
    var = {n: args["v_" + n] for n in _ORDER}

    assert x.shape[0] == 1, "one batch element per device"
    seq, D = x.shape[1], x.shape[2]
    assert seq % CHUNK == 0 and D % RET_WIDTH == 0 and (2 * POOL_WIDTH) % D == 0
    pad = (-(seq + N_META)) % CHUNK
    T = seq + N_META + pad
    tm = _pick_tile(T, 528, BF16_ROWS)
    cg = _pick_tile(T // CHUNK, 11, 1)

    shards = {n: _transport(w[n]) for n in _BIG}
    shards["meta"] = meta[None]
    wts = _Weights(shards)
    head = wts.rider([(n, 0) for n in _FFN1] + [("meta", 0)])
    _run_rider(head, "weights_gather_head")
    wts.take(head)
    meta_full = jnp.transpose(wts("meta", 0), (1, 0, 2)).reshape(N_META, D)

    reducer = _Reducer(unit=2 * shards["ffn1_gate"][0].size)
    loss_acc, dh, small, d_final = _local_step(x[0], meta_full, loss_target[0], w, wts, pad, tm, cg, reducer)
    loss = lax.psum(loss_acc[0, 0], ("x", "y", "c"))
    grad_x = dh[pad + N_META:][None]
    return _update(loss, grad_x, dh[pad:pad + N_META], reducer.finish(), small, d_final, w, mom, var)
```

```python
import functools
import math

import jax
import jax.numpy as jnp
from jax import lax
from jax.experimental import pallas as pl
from jax.experimental.pallas import tpu as pltpu

F32 = jnp.float32
BF16 = jnp.bfloat16

N_META = 16
RET_HEADS = 4
HEAD_DIM = 128
RET_WIDTH = RET_HEADS * HEAD_DIM
POOL_WINDOWS = (2, 4, 8, 16)
POOL_GROUPS = len(POOL_WINDOWS)
POOL_WIDTH = POOL_GROUPS * HEAD_DIM
CHUNK = 128
ROPE_BASE = 10000.0
EPS = 1e-6
ADAM_LR = 0.001
ADAM_B1 = 0.9
ADAM_B2 = 0.999
ADAM_EPS = 1e-08
ADAM_WD = 0.01
ADAM_STEP = 10

N_CHIPS = 4
LANES = 128
BF16_ROWS = 16
V7X_VMEM_LIMIT = 52 * 1024 * 1024
MESH = pl.DeviceIdType.MESH
ANY = pl.BlockSpec(memory_space=pl.ANY)


def _round_up(n, m):
    return -(-n // m) * m


def _pick_tile(n, target, mult):
    best = None
    for d in range(mult, min(n, target) + 1, mult):
        if n % d == 0:
            best = d
    assert best is not None, (n, target, mult)
    return best


def _params(sem=None):
    return pltpu.CompilerParams(dimension_semantics=sem, vmem_limit_bytes=V7X_VMEM_LIMIT)


def _dot(a, b):
    return jnp.dot(a, b, preferred_element_type=F32)


def _dot_nt(a, b):
    return lax.dot_general(a, b, (((1,), (1,)), ((), ())), preferred_element_type=F32)


def _dot_tn(a, b):
    return lax.dot_general(a, b, (((0,), (0,)), ((), ())), preferred_element_type=F32)


def _ein(spec, a, b):
    return jnp.einsum(spec, a, b, preferred_element_type=F32)


def _sigmoid(x):
    return jax.nn.sigmoid(x)


def _rms_fwd(x, gain):
    r = lax.rsqrt(jnp.mean(x * x, axis=-1, keepdims=True) + EPS)
    return x * r * gain


def _rms_bwd(x, gain, da):
    r = lax.rsqrt(jnp.mean(x * x, axis=-1, keepdims=True) + EPS)
    xh = x * r
    dgain = jnp.sum(da * xh, axis=0, keepdims=True)
    dxh = da * gain
    dx = r * (dxh - xh * jnp.mean(dxh * xh, axis=-1, keepdims=True))
    return dx, dgain


def _row_mask(t, tm, pad, shape):
    rows = t * tm + lax.broadcasted_iota(jnp.int32, shape, 0)
    return rows >= pad


def _mesh_pos():
    x, y, c = lax.axis_index("x"), lax.axis_index("y"), lax.axis_index("c")
    others = [(1 - x, y), (x, 1 - y), (1 - x, 1 - y)]
    return x, y, c, 2 * x + y, others


def _half_rows(c, rh):
    return pl.ds(pl.multiple_of(c * rh, rh), rh)


def _remote(src, dst, ssem, rsem, dev):
    return pltpu.make_async_remote_copy(src_ref=src, dst_ref=dst, send_sem=ssem, recv_sem=rsem,
                                        device_id=dev, device_id_type=MESH)


class _Rider:
    def __init__(self, ins, out_shapes, n_sem, start, finish, in_place=False):
        self.ins, self.out_shapes, self.n_sem, self.start, self.finish = ins, out_shapes, n_sem, start, finish
        self.in_place = [in_place] * len(ins)
        self.results = None

    def aliases(self, first_in, first_out):
        return {first_in + i: first_out + i for i, same in enumerate(self.in_place) if same}


class _SemWindow:
    def __init__(self, ref, base):
        self.ref, self.base = ref, base

    @property
    def at(self):
        return self

    def __getitem__(self, k):
        return self.ref.at[self.base + k]


def _join(riders):
    riders = [r for r in riders if r is not None]
    if len(riders) <= 1:
        return riders[0] if riders else None

    def run(which):
        def go(ins, outs, ssem, rsem):
            at, sem = 0, 0
            for r in riders:
                n = len(r.ins)
                getattr(r, which)(ins[at:at + n], outs[at:at + n], _SemWindow(ssem, sem), _SemWindow(rsem, sem))
                at, sem = at + n, sem + r.n_sem
        return go

    joined = _Rider(sum([list(r.ins) for r in riders], []), sum([list(r.out_shapes) for r in riders], []),
                    sum(r.n_sem for r in riders), run("start"), run("finish"))
    joined.in_place = sum([r.in_place for r in riders], [])
    joined.parts = riders
    return joined


def _split_results(rider):
    at = 0
    for r in getattr(rider, "parts", []):
        r.results = rider.results[at:at + len(r.ins)]
        at += len(r.ins)


def _gather_rider(pieces):
    per = 7
    layers = [layer for _, layer in pieces]

    def first_copies(ins, outs, ssem, rsem):
        x, y, c, chip, others = _mesh_pos()
        copies = []
        for i, layer in enumerate(layers):
            mine = _half_rows(c, ins[i].shape[1] // 2)
            for j, (ox, oy) in enumerate(others):
                copies.append(_remote(ins[i].at[layer, mine, :], outs[i].at[chip, mine, :],
                                      ssem.at[per * i + j], rsem.at[per * i + j], (ox, oy, c)))
            copies.append(_remote(ins[i].at[layer], outs[i].at[chip],
                                  ssem.at[per * i + 6], rsem.at[per * i + 6], (x, y, 1 - c)))
        return copies

    def start(ins, outs, ssem, rsem):
        for cp in first_copies(ins, outs, ssem, rsem):
            cp.start()

    def finish(ins, outs, ssem, rsem):
        x, y, c, chip, others = _mesh_pos()
        sibling = (x, y, 1 - c)
        forwards = []
        for i in range(len(layers)):
            mine = _half_rows(c, ins[i].shape[1] // 2)
            for j, (ox, oy) in enumerate(others):
                rows = outs[i].at[2 * ox + oy, mine, :]
                _remote(rows, rows, ssem.at[per * i + j], rsem.at[per * i + j], (ox, oy, c)).wait_recv()
                fwd = _remote(rows, rows, ssem.at[per * i + 3 + j], rsem.at[per * i + 3 + j], sibling)
                fwd.start()
                forwards.append(fwd)
        for i in range(len(layers)):
            theirs = _half_rows(1 - c, ins[i].shape[1] // 2)
            for j, (ox, oy) in enumerate(others):
                rows = outs[i].at[2 * ox + oy, theirs, :]
                _remote(rows, rows, ssem.at[per * i + 3 + j], rsem.at[per * i + 3 + j], sibling).wait_recv()
            own = outs[i].at[chip]
            _remote(own, own, ssem.at[per * i + 6], rsem.at[per * i + 6], sibling).wait_recv()
        for cp in first_copies(ins, outs, ssem, rsem) + forwards:
            cp.wait_send()

    shapes = [jax.ShapeDtypeStruct((N_CHIPS,) + s.shape[1:], s.dtype) for s, _ in pieces]
    return _Rider([s for s, _ in pieces], shapes, per * len(pieces), start, finish)


def _chip_exchange_rider(ps):
    def copies(ins, outs, ssem, rsem):
        x, y, c, chip, others = _mesh_pos()
        return [_remote(ins[i].at[2 * ox + oy], outs[i].at[chip], ssem.at[3 * i + j], rsem.at[3 * i + j], (ox, oy, c))
                for i in range(len(ps)) for j, (ox, oy) in enumerate(others)]

    def start(ins, outs, ssem, rsem):
        for cp in copies(ins, outs, ssem, rsem):
            cp.start()

    def finish(ins, outs, ssem, rsem):
        x, y, c, chip, others = _mesh_pos()
        for i in range(len(ps)):
            for j, (ox, oy) in enumerate(others):
                slot = outs[i].at[2 * ox + oy]
                _remote(slot, slot, ssem.at[3 * i + j], rsem.at[3 * i + j], (ox, oy, c)).wait_recv()
        for cp in copies(ins, outs, ssem, rsem):
            cp.wait_send()

    return _Rider(list(ps), [jax.ShapeDtypeStruct(p.shape, p.dtype) for p in ps], 3 * len(ps), start, finish)


def _pair_exchange_rider(gs):
    def copies(ins, outs, ssem, rsem):
        x, y, c, _, _ = _mesh_pos()
        return [_remote(ins[i].at[:, _half_rows(1 - c, ins[i].shape[1] // 2), :], outs[i],
                        ssem.at[i], rsem.at[i], (x, y, 1 - c)) for i in range(len(gs))]

    def start(ins, outs, ssem, rsem):
        for cp in copies(ins, outs, ssem, rsem):
            cp.start()

    def finish(ins, outs, ssem, rsem):
        for cp in copies(ins, outs, ssem, rsem):
            cp.wait()

    shapes = [jax.ShapeDtypeStruct((g.shape[0], g.shape[1] // 2, g.shape[2]), g.dtype) for g in gs]
    return _Rider(list(gs), shapes, len(gs), start, finish)


def _run_rider(rider, name):
    def body(*refs):
        n = len(rider.ins)
        ins, outs = refs[:n], refs[n:2 * n]
        ssem, rsem = refs[2 * n:]
        rider.start(ins, outs, ssem, rsem)
        rider.finish(ins, outs, ssem, rsem)

    rider.results = pl.pallas_call(
        body,
        name=name,
        in_specs=[ANY] * len(rider.ins),
        out_specs=[ANY] * len(rider.ins),
        out_shape=rider.out_shapes,
        input_output_aliases=rider.aliases(0, 0),
        scratch_shapes=[pltpu.SemaphoreType.DMA((rider.n_sem,)), pltpu.SemaphoreType.DMA((rider.n_sem,))],
    )(*rider.ins)
    return rider.results


def _pair_gather_rider(fs):
    n = len(fs)

    def copies(outs, ssem, rsem):
        x, y, c, _, _ = _mesh_pos()
        halves = [outs[i].at[_half_rows(c, outs[i].shape[0] // 2), :] for i in range(n)]
        return [_remote(h, h, ssem.at[i], rsem.at[i], (x, y, 1 - c)) for i, h in enumerate(halves)]

    def start(ins, outs, ssem, rsem):
        for cp in copies(outs, ssem, rsem):
            cp.start()

    def finish(ins, outs, ssem, rsem):
        x, y, c, _, _ = _mesh_pos()
        for i in range(n):
            theirs = outs[i].at[_half_rows(1 - c, outs[i].shape[0] // 2), :]
            _remote(theirs, theirs, ssem.at[i], rsem.at[i], (x, y, 1 - c)).wait_recv()
        for cp in copies(outs, ssem, rsem):
            cp.wait_send()

    return _Rider(list(fs), [jax.ShapeDtypeStruct(f.shape, f.dtype) for f in fs], n, start, finish, in_place=True)


def _call(body, *, name, grid, in_specs, out_specs, out_shape, operands, scratch=(), sem=None, rider=None):
    if rider is None:
        return pl.pallas_call(
            body, name=name, grid=grid, in_specs=in_specs, out_specs=out_specs, out_shape=out_shape,
            scratch_shapes=list(scratch), compiler_params=_params(sem))(*operands)
    n_in, n_out, n_sc, r = len(in_specs), len(out_specs), len(scratch), len(rider.ins)

    def carrying(*refs):
        a, b = n_in, n_in + r
        c, d = b + n_out, b + n_out + r
        e = d + n_sc
        ids = [pl.program_id(k) for k in range(len(grid))]
        first = functools.reduce(jnp.logical_and, [i == 0 for i in ids])
        last = functools.reduce(jnp.logical_and, [i == g - 1 for i, g in zip(ids, grid)])

        @pl.when(first)
        def _():
            rider.start(refs[a:b], refs[c:d], refs[e], refs[e + 1])

        body(*refs[:a], *refs[b:c], *refs[d:e])

        @pl.when(last)
        def _():
            rider.finish(refs[a:b], refs[c:d], refs[e], refs[e + 1])

    outs = pl.pallas_call(
        carrying, name=name, grid=grid,
        in_specs=list(in_specs) + [ANY] * r,
        out_specs=list(out_specs) + [ANY] * r,
        out_shape=list(out_shape) + list(rider.out_shapes),
        scratch_shapes=list(scratch) + [pltpu.SemaphoreType.DMA((rider.n_sem,)), pltpu.SemaphoreType.DMA((rider.n_sem,))],
        input_output_aliases=rider.aliases(n_in, n_out),
        compiler_params=_params(("arbitrary",) * len(grid)),
    )(*operands, *rider.ins)
    rider.results = outs[n_out:]
    return outs[:n_out]


def _ffn_fwd(h, gain, wg, wu, wd, layer, tm, name, rider=None):
    T, D = h.shape
    Fs = wg.shape[-1]
    F = N_CHIPS * Fs

    def body(h_ref, g_ref, wg_ref, wu_ref, wd_ref, ho_ref, a_ref, go_ref, uo_ref, act_ref, acc_ref):
        s = pl.program_id(1)

        @pl.when(s == 0)
        def _():
            a_ref[...] = _rms_fwd(h_ref[...], g_ref[...]).astype(BF16)
            acc_ref[...] = jnp.zeros_like(acc_ref)

        a = a_ref[...]
        g = _dot(a, wg_ref[...])
        u = _dot(a, wu_ref[...])
        sg = _sigmoid(g)
        act = (g * sg * u).astype(BF16)
        go_ref[...] = (u * (sg * (1.0 + g * (1.0 - sg)))).astype(BF16)
        uo_ref[...] = (g * sg).astype(BF16)
        act_ref[...] = act
        acc_ref[...] += _dot(act, wd_ref[...])

        @pl.when(s == N_CHIPS - 1)
        def _():
            ho_ref[...] = h_ref[...] + 0.5 * acc_ref[...]

    row = pl.BlockSpec((tm, D), lambda t, s: (t, 0))
    col = pl.BlockSpec((tm, Fs), lambda t, s: (t, s))
    wcol = pl.BlockSpec((None, D, Fs), lambda t, s: (s, 0, 0))
    return _call(
        body, name=name, grid=(T // tm, N_CHIPS),
        in_specs=[row, pl.BlockSpec((None, 1, D), lambda t, s: (layer, 0, 0)), wcol, wcol,
                  pl.BlockSpec((None, Fs, D), lambda t, s: (s, 0, 0))],
        out_specs=[row, row, col, col, col],
        out_shape=[jax.ShapeDtypeStruct((T, D), F32), jax.ShapeDtypeStruct((T, D), BF16)]
        + [jax.ShapeDtypeStruct((T, F), BF16)] * 3,
        scratch=[pltpu.VMEM((tm, D), F32)],
        sem=("parallel", "arbitrary"), operands=(h, gain, wg, wu, wd), rider=rider)


def _inproj_fwd(h, gain, win, layer, tm, name, rider=None):
    T, D = h.shape
    Ns = win.shape[-1]

    def body(h_ref, g_ref, w_ref, z_ref, b_ref):
        @pl.when(pl.program_id(1) == 0)
        def _():
            b_ref[...] = _rms_fwd(h_ref[...], g_ref[...]).astype(BF16)

        z_ref[...] = _dot(b_ref[...], w_ref[...]).astype(BF16)

    return _call(
        body, name=name, grid=(T // tm, N_CHIPS),
        in_specs=[pl.BlockSpec((tm, D), lambda t, s: (t, 0)),
                  pl.BlockSpec((None, 1, D), lambda t, s: (layer, 0, 0)),
                  pl.BlockSpec((None, D, Ns), lambda t, s: (s, 0, 0))],
        out_specs=[pl.BlockSpec((tm, Ns), lambda t, s: (t, s)), pl.BlockSpec((tm, D), lambda t, s: (t, 0))],
        out_shape=[jax.ShapeDtypeStruct((T, N_CHIPS * Ns), BF16), jax.ShapeDtypeStruct((T, D), BF16)],
        sem=("parallel", "arbitrary"), operands=(h, gain, win), rider=rider)


def _ret_consts(T, pad):
    half = HEAD_DIM // 2
    inv_freq = ROPE_BASE ** (-jnp.arange(half, dtype=F32) / half)
    pos = jnp.arange(T, dtype=F32) - pad
    ang = pos[:, None] * inv_freq[None, :]
    cos = jnp.cos(ang)
    sin = jnp.sin(ang)
    cosf = jnp.concatenate([cos, cos], axis=1)
    sinf = jnp.concatenate([-sin, sin], axis=1)
    log_gamma = jnp.log1p(-(2.0 ** (-5.0 - jnp.arange(RET_HEADS, dtype=F32))))
    idx = jnp.arange(CHUNK, dtype=F32)
    diff = idx[:, None] - idx[None, :]
    intra = jnp.where(diff[None] >= 0, jnp.exp(diff[None] * log_gamma[:, None, None]), 0.0)
    k_decay = jnp.exp((CHUNK - 1.0 - idx)[None, :] * log_gamma[:, None])
    q_decay = jnp.exp((idx + 1.0)[None, :] * log_gamma[:, None])
    chunk_decay = jnp.exp(CHUNK * log_gamma)
    kdec = jnp.broadcast_to(k_decay[:, :, None], (RET_HEADS, CHUNK, HEAD_DIM))
    qdec = jnp.broadcast_to(q_decay[:, :, None], (RET_HEADS, CHUNK, HEAD_DIM))
    cdb = jnp.broadcast_to(chunk_decay[:, None, None], (RET_HEADS, 8, HEAD_DIM))
    return cosf, sinf, intra, kdec, qdec, cdb


def _rot(t, cosv, sinv):
    return t * cosv + pltpu.roll(t, HEAD_DIM // 2, 1) * sinv


def _rot_t(g, cosv, sinv):
    return g * cosv + pltpu.roll(g * sinv, HEAD_DIM // 2, 1)


def _head_specs(tg, section, order):
    return pl.BlockSpec((tg, HEAD_DIM), lambda h, g: (order(g), section * RET_HEADS + h))


def _ret_fwd(z, consts, cg, name, rider=None):
    T = z.shape[0]
    N = T // CHUNK
    ng = N // cg
    tg = cg * CHUNK
    cosf, sinf, intra, kdec, qdec, cdb = consts
    fwd = lambda g: g

    def body(zq, zk, zv, zg, cos_ref, sin_ref, m_ref, kd_ref, qd_ref, cd_ref, r_ref, o_ref, s_ref, st_ref):
        @pl.when(pl.program_id(1) == 0)
        def _():
            st_ref[...] = jnp.zeros_like(st_ref)

        cosv = cos_ref[...]
        sinv = sin_ref[...]
        q3 = (_rot(zq[...].astype(F32), cosv, sinv) * (HEAD_DIM ** -0.5)).reshape(cg, CHUNK, HEAD_DIM)
        k3 = _rot(zk[...].astype(F32), cosv, sinv).reshape(cg, CHUNK, HEAD_DIM)
        vb = zv[...].reshape(cg, CHUNK, HEAD_DIM).astype(BF16)
        scores = _ein("ncd,nmd->ncm", q3.astype(BF16), k3.astype(BF16)) * m_ref[...][None]
        inner = _ein("ncm,nmd->ncd", scores.astype(BF16), vb)
        kv = _ein("ncd,nce->nde", (k3 * kd_ref[...][None]).astype(BF16), vb)
        cd = cd_ref[0:1, :]
        state = st_ref[...]
        for n in range(cg):
            s_ref[n] = state
            state = state * cd + kv[n]
        st_ref[...] = state
        qdb = (q3 * qd_ref[...][None]).astype(BF16)
        cross = _ein("ncd,nde->nce", qdb, s_ref[...].astype(BF16))
        out = (inner + cross).reshape(tg, HEAD_DIM)
        o_ref[...] = out
        xc = out - jnp.mean(out, axis=-1, keepdims=True)
        rn = xc * lax.rsqrt(jnp.mean(xc * xc, axis=-1, keepdims=True) + EPS)
        g = zg[...].astype(F32)
        r_ref[...] = (rn * (g * _sigmoid(g))).astype(BF16)

    tab = pl.BlockSpec((tg, HEAD_DIM), lambda h, g: (g, 0))
    per_head = lambda rows: pl.BlockSpec((None, rows, HEAD_DIM), lambda h, g: (h, 0, 0))
    head_out = pl.BlockSpec((tg, HEAD_DIM), lambda h, g: (g, h))
    return _call(
        body, name=name, grid=(RET_HEADS, ng),
        in_specs=[_head_specs(tg, i, fwd) for i in range(4)]
        + [tab, tab, per_head(CHUNK), per_head(CHUNK), per_head(CHUNK), per_head(8)],
        out_specs=[head_out, head_out, pl.BlockSpec((None, cg, HEAD_DIM, HEAD_DIM), lambda h, g: (h, g, 0, 0))],
        out_shape=[jax.ShapeDtypeStruct((T, RET_WIDTH), BF16), jax.ShapeDtypeStruct((T, RET_WIDTH), F32),
                   jax.ShapeDtypeStruct((RET_HEADS, N, HEAD_DIM, HEAD_DIM), F32)],
        scratch=[pltpu.VMEM((HEAD_DIM, HEAD_DIM), F32)],
        sem=("parallel", "arbitrary"), operands=(z, z, z, z, cosf, sinf, intra, kdec, qdec, cdb), rider=rider)


def _window_sums(u, shift_of):
    sums = []
    s = u
    k = 1
    while k < POOL_WINDOWS[-1]:
        s = s + pltpu.roll(s, shift_of(k), 0)
        sums.append(s)
        k *= 2
    return sums


def _select_group(vals, g):
    out = vals[-1]
    for i in range(len(vals) - 2, -1, -1):
        out = jnp.where(g == i, vals[i], out)
    return out


def _pool_parts(u, g, T, pad):
    rows = lax.broadcasted_iota(jnp.int32, (T, HEAD_DIM), 0)
    valid = rows >= pad
    win = _select_group([float(w) for w in POOL_WINDOWS], g)
    div = jnp.clip((rows - pad + 1).astype(F32), 1.0, win)
    s = _select_group(_window_sums(u, lambda k: k), g)
    pooled = jnp.where(valid, s / div - u, 0.0)
    return pooled, div, valid


def _pool_specs(T, layer):
    first = 4 * RET_WIDTH // HEAD_DIM
    return [
        pl.BlockSpec((T, HEAD_DIM), lambda g: (0, first + g)),
        pl.BlockSpec((None, None, HEAD_DIM, HEAD_DIM), lambda g: (layer, g, 0, 0)),
        pl.BlockSpec((None, 1, HEAD_DIM), lambda g: (layer, 0, g)),
    ]


def _pool_fwd(z, maps, scale, layer, pad, name):
    T = z.shape[0]
    assert pad >= POOL_WINDOWS[-1], "window rolls wrap into the zero rows in front"

    def body(zu, maps_ref, sc_ref, pm_ref):
        g = pl.program_id(0)
        pooled, _, _ = _pool_parts(zu[...].astype(F32), g, T, pad)
        y = _dot(pooled.astype(BF16), maps_ref[...].astype(BF16))
        pm_ref[...] = (y * sc_ref[...]).astype(BF16)

    return _call(
        body, name=name, grid=(POOL_GROUPS,),
        in_specs=_pool_specs(T, layer),
        out_specs=[pl.BlockSpec((T, HEAD_DIM), lambda g: (0, g))],
        out_shape=[jax.ShapeDtypeStruct((T, POOL_WIDTH), BF16)],
        sem=("parallel",), operands=(z, maps, scale))[0]


def _gate_specs(tm, D):
    nb = D // RET_WIDTH
    first = (4 * RET_WIDTH + POOL_WIDTH) // RET_WIDTH
    return [pl.BlockSpec((tm, RET_WIDTH), functools.partial(lambda t, j: (t, j), j=first + j)) for j in range(2 * nb)]


def _load_gates(refs, nb):
    ga = jnp.concatenate([r[...].astype(F32) for r in refs[:nb]], axis=1)
    gb = jnp.concatenate([r[...].astype(F32) for r in refs[nb:]], axis=1)
    return ga, gb


def _mix_fwd(h, r, pm, z, wru, wpu, wout, tm, name, rider=None):
    T, D = h.shape
    Dq = D // N_CHIPS
    nb = D // RET_WIDTH

    def body(*refs):
        h_ref, r_ref, pm_ref = refs[:3]
        gate_refs = refs[3:3 + 2 * nb]
        wru_ref, wpu_ref, wout_ref, ho_ref, mx_ref, ret_ref, pool_ref = refs[3 + 2 * nb:]
        rv = r_ref[...]
        pv = pm_ref[...]
        ret = jnp.concatenate([_dot(rv, wru_ref[s]) for s in range(N_CHIPS)], axis=1)
        pool = jnp.concatenate([_dot(pv, wpu_ref[s]) for s in range(N_CHIPS)], axis=1)
        ga, gb = _load_gates(gate_refs, nb)
        mixed = (_sigmoid(ga) * ret + _sigmoid(gb) * pool).astype(BF16)
        mx_ref[...] = mixed
        ret_ref[...] = ret.astype(BF16)
        pool_ref[...] = pool.astype(BF16)
        ho_ref[...] = h_ref[...] + _dot(mixed, wout_ref[...].reshape(D, D))

    row = pl.BlockSpec((tm, D), lambda t: (t, 0))
    half = pl.BlockSpec((tm, RET_WIDTH), lambda t: (t, 0))
    up = pl.BlockSpec((N_CHIPS, RET_WIDTH, Dq), lambda t: (0, 0, 0))
    return _call(
        body, name=name, grid=(T // tm,),
        in_specs=[row, half, half] + _gate_specs(tm, D) + [up, up, pl.BlockSpec((N_CHIPS, Dq, D), lambda t: (0, 0, 0))],
        out_specs=[row, row, row, row],
        out_shape=[jax.ShapeDtypeStruct((T, D), F32)] + [jax.ShapeDtypeStruct((T, D), BF16)] * 3,
        sem=("parallel",), operands=(h, r, pm, *([z] * (2 * nb)), wru, wpu, wout), rider=rider)


def _final_loss(h, gain, tgt, name):
    T, D = h.shape
    first = (T - tgt.shape[0]) // CHUNK

    def body(h_ref, g_ref, t_ref, dh_ref, loss_ref, dg_ref):
        i = pl.program_id(0)

        @pl.when(i == 0)
        def _():
            loss_ref[...] = jnp.zeros_like(loss_ref)
            dg_ref[...] = jnp.zeros_like(dg_ref)

        x = h_ref[...]
        gain_v = g_ref[...]
        err = jnp.where(i >= first, _rms_fwd(x, gain_v) - t_ref[...], 0.0)
        loss_ref[...] += 0.5 * jnp.sum(jnp.mean(err * err, axis=-1))
        dx, dgain = _rms_bwd(x, gain_v, err * (1.0 / D))
        dg_ref[...] += dgain
        dh_ref[...] = dx

    return _call(
        body, name=name, grid=(T // CHUNK,),
        in_specs=[pl.BlockSpec((CHUNK, D), lambda i: (i, 0)),
                  pl.BlockSpec((1, D), lambda i: (0, 0)),
                  pl.BlockSpec((CHUNK, D), lambda i: (jnp.maximum(i - first, 0), 0))],
        out_specs=[pl.BlockSpec((CHUNK, D), lambda i: (i, 0)),
                   pl.BlockSpec((1, LANES), lambda i: (0, 0)),
                   pl.BlockSpec((1, D), lambda i: (0, 0))],
        out_shape=[jax.ShapeDtypeStruct((T, D), F32), jax.ShapeDtypeStruct((1, LANES), F32),
                   jax.ShapeDtypeStruct((1, D), F32)],
        sem=("arbitrary",), operands=(h, gain, tgt))


def _ffn_bwd_act(dy, g, u, wd, tm, name, rider=None):
    T, D = dy.shape
    Fs = wd.shape[1]
    F = N_CHIPS * Fs

    def body(dy_ref, go_ref, uo_ref, wd_ref, dg_ref, du_ref, dyh_ref):
        @pl.when(pl.program_id(1) == 0)
        def _():
            dyh_ref[...] = (0.5 * dy_ref[...]).astype(BF16)

        dact = _dot_nt(dyh_ref[...], wd_ref[...])
        du_ref[...] = (dact * uo_ref[...].astype(F32)).astype(BF16)
        dg_ref[...] = (dact * go_ref[...].astype(F32)).astype(BF16)

    row = pl.BlockSpec((tm, D), lambda t, s: (t, 0))
    col = pl.BlockSpec((tm, Fs), lambda t, s: (t, s))
    return _call(
        body, name=name, grid=(T // tm, N_CHIPS),
        in_specs=[row, col, col, pl.BlockSpec((None, Fs, D), lambda t, s: (s, 0, 0))],
        out_specs=[col, col, row],
        out_shape=[jax.ShapeDtypeStruct((T, F), BF16), jax.ShapeDtypeStruct((T, F), BF16),
                   jax.ShapeDtypeStruct((T, D), BF16)],
        sem=("parallel", "arbitrary"), operands=(dy, g, u, wd), rider=rider)


def _ffn_bwd_in(dy, h, gain, dg, du, wg, wu, layer, tm, pad, name, rider=None):
    T, D = h.shape
    Fs = wg.shape[-1]

    def body(dy_ref, h_ref, g_ref, dg_ref, du_ref, wg_ref, wu_ref, dh_ref, dgain_ref, da_ref):
        t = pl.program_id(0)
        s = pl.program_id(1)

        @pl.when((t == 0) & (s == 0))
        def _():
            dgain_ref[...] = jnp.zeros_like(dgain_ref)

        @pl.when(s == 0)
        def _():
            da_ref[...] = jnp.zeros_like(da_ref)

        da_ref[...] += _dot_nt(dg_ref[...], wg_ref[...]) + _dot_nt(du_ref[...], wu_ref[...])

        @pl.when(s == N_CHIPS - 1)
        def _():
            dx, dgain = _rms_bwd(h_ref[...], g_ref[...], da_ref[...])
            dgain_ref[...] += dgain
            dh_ref[...] = jnp.where(_row_mask(t, tm, pad, (tm, D)), dy_ref[...] + dx, 0.0)

    row = pl.BlockSpec((tm, D), lambda t, s: (t, 0))
    col = pl.BlockSpec((tm, Fs), lambda t, s: (t, s))
    wcol = pl.BlockSpec((None, D, Fs), lambda t, s: (s, 0, 0))
    return _call(
        body, name=name, grid=(T // tm, N_CHIPS),
        in_specs=[row, row, pl.BlockSpec((None, 1, D), lambda t, s: (layer, 0, 0)), col, col, wcol, wcol],
        out_specs=[row, pl.BlockSpec((1, D), lambda t, s: (0, 0))],
        out_shape=[jax.ShapeDtypeStruct((T, D), F32), jax.ShapeDtypeStruct((1, D), F32)],
        scratch=[pltpu.VMEM((tm, D), F32)],
        sem=("arbitrary", "arbitrary"), operands=(dy, h, gain, dg, du, wg, wu), rider=rider)


def _grad_tn(a, b, mode, scale, tm, name, rider=None):
    T = a.shape[0]
    if mode == "col":
        per, R, C = 1, a.shape[1], b.shape[1] // N_CHIPS
        a_spec = pl.BlockSpec((tm, R), lambda s, t: (t, 0))
        b_spec = pl.BlockSpec((tm, C), lambda s, t: (t, s))
    else:
        per, R, C = 2, a.shape[1] // N_CHIPS, b.shape[1]
        a_spec = pl.BlockSpec((tm, per * R), lambda s, t: (t, s))
        b_spec = pl.BlockSpec((tm, C), lambda s, t: (t, 0))
    nt = T // tm

    def body(a_ref, b_ref, o_ref, acc_ref):
        t = pl.program_id(1)

        @pl.when(t == 0)
        def _():
            acc_ref[...] = jnp.zeros_like(acc_ref)

        acc_ref[...] += _dot_tn(a_ref[...].astype(BF16), b_ref[...].astype(BF16))

        @pl.when(t == nt - 1)
        def _():
            o_ref[...] = (scale * acc_ref[...]).astype(BF16).reshape(per, R, C)

    return _call(
        body, name=name, grid=(N_CHIPS // per, nt),
        in_specs=[a_spec, b_spec],
        out_specs=[pl.BlockSpec((per, R, C), lambda s, t: (s, 0, 0))],
        out_shape=[jax.ShapeDtypeStruct((N_CHIPS, R, C), BF16)],
        scratch=[pltpu.VMEM((per * R, C), F32)],
        sem=("parallel", "arbitrary"), operands=(a, b), rider=rider)[0]


def _grad_mix(mixed, dh, r, dret, pm, dpool, tk, name, rider=None):
    T, D = dh.shape
    Dq = D // N_CHIPS
    nt = T // tk

    def body(mx_ref, dh_ref, r_ref, dret_ref, pm_ref, dpool_ref, go_ref, gr_ref, gp_ref, ao_ref, ar_ref, ap_ref):
        t = pl.program_id(0)

        @pl.when(t == 0)
        def _():
            ao_ref[...] = jnp.zeros_like(ao_ref)
            ar_ref[...] = jnp.zeros_like(ar_ref)
            ap_ref[...] = jnp.zeros_like(ap_ref)

        ao_ref[...] += _dot_tn(mx_ref[...], dh_ref[...].astype(BF16))
        ar_ref[...] += _dot_tn(r_ref[...], dret_ref[...])
        ap_ref[...] += _dot_tn(pm_ref[...], dpool_ref[...])

        @pl.when(t == nt - 1)
        def _():
            go_ref[...] = ao_ref[...].astype(BF16).reshape(N_CHIPS, Dq, D)
            for s in range(N_CHIPS):
                gr_ref[s] = ar_ref[:, s * Dq:(s + 1) * Dq].astype(BF16)
                gp_ref[s] = ap_ref[:, s * Dq:(s + 1) * Dq].astype(BF16)

    row = pl.BlockSpec((tk, D), lambda t: (t, 0))
    half = pl.BlockSpec((tk, RET_WIDTH), lambda t: (t, 0))
    whole = lambda shape: pl.BlockSpec(shape, lambda t: (0, 0, 0))
    return _call(
        body, name=name, grid=(nt,),
        in_specs=[row, row, half, row, half, row],
        out_specs=[whole((N_CHIPS, Dq, D)), whole((N_CHIPS, RET_WIDTH, Dq)), whole((N_CHIPS, POOL_WIDTH, Dq))],
        out_shape=[jax.ShapeDtypeStruct((N_CHIPS, Dq, D), BF16),
                   jax.ShapeDtypeStruct((N_CHIPS, RET_WIDTH, Dq), BF16),
                   jax.ShapeDtypeStruct((N_CHIPS, POOL_WIDTH, Dq), BF16)],
        scratch=[pltpu.VMEM((D, D), F32), pltpu.VMEM((RET_WIDTH, D), F32), pltpu.VMEM((POOL_WIDTH, D), F32)],
        sem=("arbitrary",), operands=(mixed, dh, r, dret, pm, dpool), rider=rider)


def _mix_bwd_dx(dh, z, ret, pool, wout, wru, wpu, tm, name, rider=None):
    T, D = dh.shape
    Dq = D // N_CHIPS
    nb = D // RET_WIDTH

    def body(*refs):
        dh_ref = refs[0]
        gate_refs = refs[1:1 + 2 * nb]
        ret_ref, pool_ref, wout_ref, wru_ref, wpu_ref, dgab_ref, dret_ref, dpool_ref, dr_ref, dpm_ref = refs[1 + 2 * nb:]
        dmixed = _dot_nt(dh_ref[...].astype(BF16), wout_ref[...].reshape(D, D))
        ga, gb = _load_gates(gate_refs, nb)
        sa = _sigmoid(ga)
        sb = _sigmoid(gb)
        dgab_ref[:, :D] = (dmixed * ret_ref[...].astype(F32) * (sa * (1.0 - sa))).astype(BF16)
        dgab_ref[:, D:] = (dmixed * pool_ref[...].astype(F32) * (sb * (1.0 - sb))).astype(BF16)
        dret = (dmixed * sa).astype(BF16)
        dpool = (dmixed * sb).astype(BF16)
        dret_ref[...] = dret
        dpool_ref[...] = dpool
        dr = _dot_nt(dret[:, :Dq], wru_ref[0])
        dpm = _dot_nt(dpool[:, :Dq], wpu_ref[0])
        for s in range(1, N_CHIPS):
            dr += _dot_nt(dret[:, s * Dq:(s + 1) * Dq], wru_ref[s])
            dpm += _dot_nt(dpool[:, s * Dq:(s + 1) * Dq], wpu_ref[s])
        dr_ref[...] = dr
        dpm_ref[...] = dpm

    row = pl.BlockSpec((tm, D), lambda t: (t, 0))
    half = pl.BlockSpec((tm, RET_WIDTH), lambda t: (t, 0))
    up = pl.BlockSpec((N_CHIPS, RET_WIDTH, Dq), lambda t: (0, 0, 0))
    return _call(
        body, name=name, grid=(T // tm,),
        in_specs=[row] + _gate_specs(tm, D) + [row, row, pl.BlockSpec((N_CHIPS, Dq, D), lambda t: (0, 0, 0)), up, up],
        out_specs=[pl.BlockSpec((tm, 2 * D), lambda t: (t, 0)), row, row, half, half],
        out_shape=[jax.ShapeDtypeStruct((T, 2 * D), BF16), jax.ShapeDtypeStruct((T, D), BF16),
                   jax.ShapeDtypeStruct((T, D), BF16), jax.ShapeDtypeStruct((T, RET_WIDTH), F32),
                   jax.ShapeDtypeStruct((T, POOL_WIDTH), F32)],
        sem=("parallel",), operands=(dh, *([z] * (2 * nb)), ret, pool, wout, wru, wpu), rider=rider)


def _pool_bwd(z, dpm, maps, scale, layer, pad, name):
    T = z.shape[0]

    def body(zu, maps_ref, sc_ref, dpm_ref, du_ref, dmaps_ref, dsc_ref):
        g = pl.program_id(0)
        u = zu[...].astype(F32)
        pooled, div, valid = _pool_parts(u, g, T, pad)
        pb = pooled.astype(BF16)
        mb = maps_ref[...].astype(BF16)
        dp = dpm_ref[...]
        dsc_ref[...] = jnp.sum(dp * _dot(pb, mb), axis=0, keepdims=True)
        dyb = (dp * sc_ref[...]).astype(BF16)
        dmaps_ref[...] = _dot_tn(pb, dyb)
        dpooled = jnp.where(valid, _dot_nt(dyb, mb), 0.0)
        ahead = _select_group(_window_sums(dpooled / div, lambda k: T - k), g)
        du_ref[...] = jnp.where(valid, ahead - dpooled, 0.0).astype(BF16)

    blk = pl.BlockSpec((T, HEAD_DIM), lambda g: (0, g))
    return _call(
        body, name=name, grid=(POOL_GROUPS,),
        in_specs=_pool_specs(T, layer) + [blk],
        out_specs=[blk, pl.BlockSpec((None, HEAD_DIM, HEAD_DIM), lambda g: (g, 0, 0)),
                   pl.BlockSpec((1, HEAD_DIM), lambda g: (0, g))],
        out_shape=[jax.ShapeDtypeStruct((T, POOL_WIDTH), BF16),
                   jax.ShapeDtypeStruct((POOL_GROUPS, HEAD_DIM, HEAD_DIM), F32),
                   jax.ShapeDtypeStruct((1, POOL_WIDTH), F32)],
        sem=("parallel",), operands=(z, maps, scale, dpm))


def _ret_bwd_local(z, o_pre, s_all, dr, consts, cg, name):
    T = z.shape[0]
    N = T // CHUNK
    ng = N // cg
    tg = cg * CHUNK
    cosf, sinf, intra, _, qdec, _ = consts
    fwd = lambda g: g

    def body(zq, zk, zv, zg, o_ref, s_ref, dr_ref, cos_ref, sin_ref, m_ref, qd_ref,
             dq_ref, dg_ref, dk_ref, dv_ref, ds_ref):
        cosv = cos_ref[...]
        sinv = sin_ref[...]
        scale = HEAD_DIM ** -0.5
        q3 = (_rot(zq[...].astype(F32), cosv, sinv) * scale).reshape(cg, CHUNK, HEAD_DIM)
        k3 = _rot(zk[...].astype(F32), cosv, sinv).reshape(cg, CHUNK, HEAD_DIM)
        qb = q3.astype(BF16)
        kb = k3.astype(BF16)
        vb = zv[...].reshape(cg, CHUNK, HEAD_DIM).astype(BF16)
        mask = m_ref[...][None]
        sb = (_ein("ncd,nmd->ncm", qb, kb) * mask).astype(BF16)
        qdv = qd_ref[...][None]
        qdb = (q3 * qdv).astype(BF16)

        out = o_ref[...]
        xc = out - jnp.mean(out, axis=-1, keepdims=True)
        rstd = lax.rsqrt(jnp.mean(xc * xc, axis=-1, keepdims=True) + EPS)
        rn = xc * rstd
        g = zg[...].astype(F32)
        sg = _sigmoid(g)
        drv = dr_ref[...]
        dg_ref[...] = (drv * rn * (sg * (1.0 + g * (1.0 - sg)))).astype(BF16)
        drn = drv * (g * sg)
        dout = rstd * (drn - jnp.mean(drn, axis=-1, keepdims=True)
                       - rn * jnp.mean(drn * rn, axis=-1, keepdims=True))
        dob = dout.reshape(cg, CHUNK, HEAD_DIM).astype(BF16)

        dsb = (_ein("ncd,nmd->ncm", dob, vb) * mask).astype(BF16)
        dv_ref[...] = _ein("ncm,ncd->nmd", sb, dob).reshape(tg, HEAD_DIM)
        dk_ref[...] = _ein("ncm,ncd->nmd", dsb, qb).reshape(tg, HEAD_DIM)
        dq3 = _ein("ncm,nmd->ncd", dsb, kb) + _ein("nce,nde->ncd", dob, s_ref[...].astype(BF16)) * qdv
        dq_ref[...] = _rot_t(dq3.reshape(tg, HEAD_DIM) * scale, cosv, sinv).astype(BF16)
        ds_ref[...] = _ein("ncd,nce->nde", qdb, dob)

    tab = pl.BlockSpec((tg, HEAD_DIM), lambda h, g: (g, 0))
    per_head = pl.BlockSpec((None, CHUNK, HEAD_DIM), lambda h, g: (h, 0, 0))
    head_blk = pl.BlockSpec((tg, HEAD_DIM), lambda h, g: (g, h))
    state_blk = pl.BlockSpec((None, cg, HEAD_DIM, HEAD_DIM), lambda h, g: (h, g, 0, 0))
    return _call(
        body, name=name, grid=(RET_HEADS, ng),
        in_specs=[_head_specs(tg, i, fwd) for i in range(4)]
        + [head_blk, state_blk, head_blk, tab, tab, per_head, per_head],
        out_specs=[head_blk, head_blk, head_blk, head_blk, state_blk],
        out_shape=[jax.ShapeDtypeStruct((T, RET_WIDTH), BF16), jax.ShapeDtypeStruct((T, RET_WIDTH), BF16),
                   jax.ShapeDtypeStruct((T, RET_WIDTH), F32), jax.ShapeDtypeStruct((T, RET_WIDTH), F32),
                   jax.ShapeDtypeStruct((RET_HEADS, N, HEAD_DIM, HEAD_DIM), F32)],
        sem=("parallel", "parallel"), operands=(z, z, z, z, o_pre, s_all, dr, cosf, sinf, intra, qdec))


def _ret_bwd_state(z, dkp, dvp, ds, consts, cg, name):
    T = z.shape[0]
    N = T // CHUNK
    ng = N // cg
    tg = cg * CHUNK
    cosf, sinf, _, kdec, _, cdb = consts
    rev = lambda g: ng - 1 - g

    def body(zk, zv, dkp_ref, dvp_ref, ds_ref, cos_ref, sin_ref, kd_ref, cd_ref, dk_ref, dv_ref, gs_ref, dkv_ref):
        @pl.when(pl.program_id(1) == 0)
        def _():
            gs_ref[...] = jnp.zeros_like(gs_ref)

        cosv = cos_ref[...]
        sinv = sin_ref[...]
        cd = cd_ref[0:1, :]
        grad = gs_ref[...]
        for n in reversed(range(cg)):
            dkv_ref[n] = grad
            grad = ds_ref[n] + cd * grad
        gs_ref[...] = grad
        dkvb = dkv_ref[...].astype(BF16)
        kdv = kd_ref[...][None]
        k3 = _rot(zk[...].astype(F32), cosv, sinv).reshape(cg, CHUNK, HEAD_DIM)
        vb = zv[...].reshape(cg, CHUNK, HEAD_DIM).astype(BF16)
        dk3 = _ein("nce,nde->ncd", vb, dkvb) * kdv
        dv3 = _ein("ncd,nde->nce", (k3 * kdv).astype(BF16), dkvb)
        dk_ref[...] = _rot_t(dkp_ref[...] + dk3.reshape(tg, HEAD_DIM), cosv, sinv).astype(BF16)
        dv_ref[...] = (dvp_ref[...] + dv3.reshape(tg, HEAD_DIM)).astype(BF16)

    tab = pl.BlockSpec((tg, HEAD_DIM), lambda h, g: (rev(g), 0))
    head_blk = pl.BlockSpec((tg, HEAD_DIM), lambda h, g: (rev(g), h))
    return _call(
        body, name=name, grid=(RET_HEADS, ng),
        in_specs=[_head_specs(tg, 1, rev), _head_specs(tg, 2, rev), head_blk, head_blk,
                  pl.BlockSpec((None, cg, HEAD_DIM, HEAD_DIM), lambda h, g: (h, rev(g), 0, 0)),
                  tab, tab,
                  pl.BlockSpec((None, CHUNK, HEAD_DIM), lambda h, g: (h, 0, 0)),
                  pl.BlockSpec((None, 8, HEAD_DIM), lambda h, g: (h, 0, 0))],
        out_specs=[head_blk, head_blk],
        out_shape=[jax.ShapeDtypeStruct((T, RET_WIDTH), BF16)] * 2,
        scratch=[pltpu.VMEM((HEAD_DIM, HEAD_DIM), F32), pltpu.VMEM((cg, HEAD_DIM, HEAD_DIM), F32)],
        sem=("parallel", "arbitrary"), operands=(z, z, dkp, dvp, ds, cosf, sinf, kdec, cdb))


def _inproj_bwd_dx(dz, win, h, gain, dh_in, layer, tm, pad, name, rider=None):
    T, D = h.shape
    Ns = win.shape[-1]

    def body(dz_ref, w_ref, h_ref, g_ref, dhi_ref, dh_ref, dgain_ref, db_ref):
        t = pl.program_id(0)
        s = pl.program_id(1)

        @pl.when((t == 0) & (s == 0))
        def _():
            dgain_ref[...] = jnp.zeros_like(dgain_ref)

        @pl.when(s == 0)
        def _():
            db_ref[...] = jnp.zeros_like(db_ref)

        db_ref[...] += _dot_nt(dz_ref[...], w_ref[...])

        @pl.when(s == N_CHIPS - 1)
        def _():
            dx, dgain = _rms_bwd(h_ref[...], g_ref[...], db_ref[...])
            dgain_ref[...] += dgain
            dh_ref[...] = jnp.where(_row_mask(t, tm, pad, (tm, D)), dhi_ref[...] + dx, 0.0)

    row = pl.BlockSpec((tm, D), lambda t, s: (t, 0))
    return _call(
        body, name=name, grid=(T // tm, N_CHIPS),
        in_specs=[pl.BlockSpec((tm, Ns), lambda t, s: (t, s)),
                  pl.BlockSpec((None, D, Ns), lambda t, s: (s, 0, 0)),
                  row, pl.BlockSpec((None, 1, D), lambda t, s: (layer, 0, 0)), row],
        out_specs=[row, pl.BlockSpec((1, D), lambda t, s: (0, 0))],
        out_shape=[jax.ShapeDtypeStruct((T, D), F32), jax.ShapeDtypeStruct((1, D), F32)],
        scratch=[pltpu.VMEM((tm, D), F32)],
        sem=("arbitrary", "arbitrary"), operands=(dz, win, h, gain, dh_in), rider=rider)


def _sum_pair(gs, rs, c_idx, name):
    n = len(gs)

    def body(c_ref, *refs):
        for g_ref, r_ref, o_ref in zip(refs[:n], refs[n:2 * n], refs[2 * n:]):
            o_ref[...] = (g_ref[...].astype(F32) + r_ref[...].astype(F32)).astype(BF16)

    halves = [pl.BlockSpec((None,) + r.shape[1:], lambda s, c_ref: (s, 0, 0)) for r in rs]
    return pl.pallas_call(
        body,
        name=name,
        grid_spec=pltpu.PrefetchScalarGridSpec(
            num_scalar_prefetch=1,
            grid=(N_CHIPS,),
            in_specs=[pl.BlockSpec((None,) + r.shape[1:], lambda s, c_ref: (s, c_ref[0], 0)) for r in rs] + halves,
            out_specs=halves,
        ),
        out_shape=[jax.ShapeDtypeStruct(r.shape, BF16) for r in rs],
        compiler_params=_params(("parallel",)),
    )(c_idx, *gs, *rs)


def _sum_chips(ps, rs, pos, name):
    n = len(ps)
    quarters = 4

    def body(pos_ref, *refs):
        chip = pos_ref[0]
        for p_ref, r_ref, o_ref in zip(refs[:n], refs[n:2 * n], refs[2 * n:]):
            own = p_ref[...].astype(F32)
            terms = [jnp.where(chip == k, own, r_ref[k].astype(F32)) for k in range(N_CHIPS)]
            o_ref[...] = ((terms[0] + terms[1]) + terms[2]) + terms[3]

    def rows(r):
        assert r.shape[1] % (quarters * BF16_ROWS) == 0, r.shape
        return r.shape[1] // quarters

    return pl.pallas_call(
        body,
        name=name,
        grid_spec=pltpu.PrefetchScalarGridSpec(
            num_scalar_prefetch=1,
            grid=(quarters,),
            in_specs=[pl.BlockSpec((None, rows(r), r.shape[2]), lambda q, pos_ref: (pos_ref[0], q, 0)) for r in rs]
            + [pl.BlockSpec((N_CHIPS, rows(r), r.shape[2]), lambda q, pos_ref: (0, q, 0)) for r in rs],
            out_specs=[pl.BlockSpec((rows(r), r.shape[2]), lambda q, pos_ref: (pos_ref[1] * quarters + q, 0))
                       for r in rs],
        ),
        out_shape=[jax.ShapeDtypeStruct((2 * r.shape[1], r.shape[2]), F32) for r in rs],
        compiler_params=_params(("arbitrary",)),
    )(pos, *ps, *rs)


def _small_all_reduce(p):
    rows, width = p.shape

    def body(p_ref, o_ref, sib_ref, slot_ref, ssem, rsem):
        x, y, c, chip, others = _mesh_pos()
        pair = _remote(p_ref, sib_ref, ssem.at[0], rsem.at[0], (x, y, 1 - c))
        pair.start()
        pair.wait()
        slot_ref[chip] = p_ref[...] + sib_ref[...]
        sends = []
        for j, (ox, oy) in enumerate(others):
            cp = _remote(slot_ref.at[chip], slot_ref.at[chip], ssem.at[1 + j], rsem.at[1 + j], (ox, oy, c))
            cp.start()
            sends.append(cp)
        for j, (ox, oy) in enumerate(others):
            slot = slot_ref.at[2 * ox + oy]
            _remote(slot, slot, ssem.at[1 + j], rsem.at[1 + j], (ox, oy, c)).wait_recv()
        for cp in sends:
            cp.wait_send()
        o_ref[...] = ((slot_ref[0] + slot_ref[1]) + slot_ref[2]) + slot_ref[3]

    vmem = pl.BlockSpec(memory_space=pltpu.VMEM)
    return pl.pallas_call(
        body,
        name="small_grads_all_reduce",
        in_specs=[vmem],
        out_specs=vmem,
        out_shape=jax.ShapeDtypeStruct(p.shape, F32),
        scratch_shapes=[pltpu.VMEM((rows, width), F32), pltpu.VMEM((N_CHIPS, rows, width), F32),
                        pltpu.SemaphoreType.DMA((4,)), pltpu.SemaphoreType.DMA((4,))],
    )(p)


def _adamw(gs, w, m, v, name):
    L, R, C = w.shape
    Ct = gs[0].shape[1]
    tr = _pick_tile(R, 256, 8)

    def body(*refs):
        g_refs = refs[:L]
        w_ref, m_ref, v_ref, go_ref, d_ref, mo_ref, vo_ref = refs[L:]
        layer = pl.program_id(0)
        grad = g_refs[L - 1][...]
        for i in range(L - 2, -1, -1):
            grad = jnp.where(layer == i, g_refs[i][...], grad)
        if Ct != C:
            grad = grad[:, :C]
        m_new = ADAM_B1 * m_ref[...] + (1.0 - ADAM_B1) * grad
        v_new = ADAM_B2 * v_ref[...] + (1.0 - ADAM_B2) * jnp.square(grad)
        m_hat = m_new / (1.0 - ADAM_B1 ** ADAM_STEP)
        v_hat = v_new / (1.0 - ADAM_B2 ** ADAM_STEP)
        go_ref[...] = grad
        d_ref[...] = -ADAM_LR * (m_hat / (jnp.sqrt(v_hat) + ADAM_EPS) + ADAM_WD * w_ref[...])
        mo_ref[...] = m_new
        vo_ref[...] = v_new

    g_specs = [pl.BlockSpec((tr, Ct), functools.partial(lambda l, r, i: (jnp.where(l == i, r, 0), 0), i=i))
               for i in range(L)]
    blk = pl.BlockSpec((None, tr, C), lambda l, r: (l, r, 0))
    return pl.pallas_call(
        body,
        name=name,
        grid=(L, R // tr),
        in_specs=g_specs + [blk, blk, blk],
        out_specs=[blk] * 4,
        out_shape=[jax.ShapeDtypeStruct((L, R, C), F32)] * 4,
        compiler_params=_params(("arbitrary", "arbitrary")),
    )(*gs, w, m, v)


_FFN1 = ("ffn1_gate", "ffn1_up", "ffn1_down")
_FFN2 = ("ffn2_gate", "ffn2_up", "ffn2_down")
_MIXW = ("w_ret_up", "w_pool_up", "w_out")
_BIG = _FFN1 + ("w_in",) + _MIXW + _FFN2
_TRANSPOSED = ("ffn1_gate", "ffn1_up", "ffn2_gate", "ffn2_up")
_SMALL = ("ffn1_norm", "mix_norm", "ffn2_norm", "final_norm", "pool_scale", "pool_maps")
_ORDER = ("meta", "ffn1_norm", "ffn1_gate", "ffn1_up", "ffn1_down", "mix_norm", "w_in", "pool_maps",
          "pool_scale", "w_ret_up", "w_pool_up", "w_out", "ffn2_norm", "ffn2_gate", "ffn2_up", "ffn2_down",
          "final_norm")


def _transport(a):
    n, r, c = a.shape
    out = a.astype(BF16)
    if c % LANES:
        out = jnp.concatenate([out, jnp.zeros((n, r, _round_up(c, LANES) - c), BF16)], axis=2)
    if r % LANES:
        out = jnp.concatenate([out, jnp.zeros((n, _round_up(r, LANES) - r, out.shape[2]), BF16)], axis=1)
    return out


def _pack_rows(parts, width):
    rows = [p.reshape(-1, width) for p in parts]
    total = sum(r.shape[0] for r in rows)
    fill = _round_up(total, 8) - total
    if fill:
        rows.append(jnp.zeros((fill, width), F32))
    return jnp.concatenate(rows, axis=0)


def _unpack_rows(packed, shapes, width):
    out, at = [], 0
    for shp in shapes:
        n = math.prod(shp) // width
        out.append(packed[at:at + n].reshape(shp))
        at += n
    return out


class _Weights:
    def __init__(self, shards):
        self.shards = shards
        self.full = {}

    def rider(self, keys):
        r = _gather_rider([(self.shards[n], i) for n, i in keys])
        r.keys = keys
        return r

    def take(self, rider):
        for key, arr in zip(rider.keys, rider.results):
            self.full[key] = arr

    def __call__(self, name, layer):
        return self.full[(name, layer)]


def _local_step(x, meta_full, tgt, w, wts, pad, tm, cg, reducer):
    D = x.shape[1]
    T = pad + N_META + x.shape[0]
    L = w["ffn1_norm"].shape[0]
    pool_maps = w["pool_maps"]
    gains = {n: w[n].reshape(L, 1, D) for n in ("ffn1_norm", "mix_norm", "ffn2_norm")}
    scale3 = w["pool_scale"].reshape(L, 1, POOL_WIDTH)
    consts = _ret_consts(T, pad)
    tl = _pick_tile(T, 2 * tm, BF16_ROWS)
    def gather(keys):
        return wts.rider(keys) if keys and keys[0] not in wts.full else None

    def done(rider):
        if rider is not None:
            wts.take(rider)

    h = jnp.concatenate([jnp.zeros((pad, D), F32), meta_full, x], axis=0)
    saved = []
    for i in range(L):
        s = {"h0": h}
        rd = gather([("w_in", i)] + [(n, i) for n in _MIXW])
        h, s["a1"], s["g1"], s["u1"], s["act1"] = _ffn_fwd(
            h, gains["ffn1_norm"], wts("ffn1_gate", i), wts("ffn1_up", i), wts("ffn1_down", i), i, tl,
            f"ffn1_fwd_{i}", rd)
        done(rd)
        s["h1"] = h
        rd = gather([("ffn2_gate", i), ("ffn2_up", i)])
        s["z"], s["b"] = _inproj_fwd(h, gains["mix_norm"], wts("w_in", i), i, tl, f"inproj_fwd_{i}", rd)
        done(rd)
        s["r"], s["o_pre"], s["s_all"] = _ret_fwd(s["z"], consts, cg, f"retention_fwd_{i}")
        s["pm"] = _pool_fwd(s["z"], pool_maps, scale3, i, pad, f"pool_fwd_{i}")
        rd = gather([("ffn2_down", i)])
        h, s["mixed"], s["ret"], s["pool"] = _mix_fwd(
            h, s["r"], s["pm"], s["z"], wts("w_ret_up", i), wts("w_pool_up", i), wts("w_out", i), tm,
            f"mix_fwd_{i}", rd)
        done(rd)
        s["h2"] = h
        rd = gather([(n, i + 1) for n in _FFN1]) if i + 1 < L else None
        h, s["a2"], s["g2"], s["u2"], s["act2"] = _ffn_fwd(
            h, gains["ffn2_norm"], wts("ffn2_gate", i), wts("ffn2_up", i), wts("ffn2_down", i), i, tl,
            f"ffn2_fwd_{i}", rd)
        done(rd)
        saved.append(s)

    dh, loss_acc, d_final = _final_loss(h, w["final_norm"].reshape(1, D), tgt, "final_norm_loss")

    small = {n: [None] * L for n in ("ffn1_norm", "mix_norm", "ffn2_norm", "pool_scale", "pool_maps")}

    carry = {"ffn_act": 1.0, "ffn_in": 2.2, "mix_bwd": 1.0, "inproj_bwd": 1.5, "w_in": 1.0}

    tk = _pick_tile(T, 1408, LANES)

    def grad(n, a, b, i, mode):
        rd = reducer.rider(carry.get(n, 1.0 if i == 0 and n.startswith("ffn") else 0.5))
        reducer.add(n, i, _grad_tn(a, b, mode, 1.0, tk, f"grad_{n}_{i}", rd))
        reducer.done(rd)

    def ffn_bwd(which, dy, h_in, g, u, i):
        rd = reducer.rider(carry["ffn_act"])
        dg, du, dyh = _ffn_bwd_act(dy, g, u, wts(f"{which}_down", i), tl, f"{which}_bwd_act_{i}", rd)
        reducer.done(rd)
        rd = reducer.rider(carry["ffn_in"])
        dh_in, dgain = _ffn_bwd_in(dy, h_in, gains[f"{which}_norm"], dg, du, wts(f"{which}_gate", i),
                                   wts(f"{which}_up", i), i, tl, pad, f"{which}_bwd_in_{i}", rd)
        reducer.done(rd)
        return dh_in, dg, du, dgain, dyh

    for i in reversed(range(L)):
        s = saved[i]
        dh, dg, du, small["ffn2_norm"][i], dyh = ffn_bwd("ffn2", dh, s["h2"], s["g2"], s["u2"], i)
        grad("ffn2_gate", dg, s["a2"], i, "row")
        grad("ffn2_up", du, s["a2"], i, "row")
        grad("ffn2_down", s["act2"], dyh, i, "row")
        reducer.stage(f"ffn2_{i}")
        rd = reducer.rider(carry["mix_bwd"])
        dgab, dret, dpool, dr, dpm = _mix_bwd_dx(
            dh, s["z"], s["ret"], s["pool"], wts("w_out", i), wts("w_ret_up", i), wts("w_pool_up", i), tm,
            f"mix_bwd_{i}", rd)
        reducer.done(rd)
        rd = reducer.rider(0.5)
        g_out, g_ru, g_pu = _grad_mix(s["mixed"], dh, s["r"], dret, s["pm"], dpool, _pick_tile(T, 704, LANES),
                                      f"grad_mix_{i}", rd)
        reducer.done(rd)
        for n, g_n in (("w_out", g_out), ("w_ret_up", g_ru), ("w_pool_up", g_pu)):
            reducer.add(n, i, g_n)
        du_pool, small["pool_maps"][i], small["pool_scale"][i] = _pool_bwd(
            s["z"], dpm, pool_maps, scale3, i, pad, f"pool_bwd_{i}")
        dq, dgr, dkp, dvp, ds = _ret_bwd_local(s["z"], s["o_pre"], s["s_all"], dr, consts, cg, f"retention_bwd_{i}")
        dk, dv = _ret_bwd_state(s["z"], dkp, dvp, ds, consts, cg, f"retention_bwd_state_{i}")
        dz = jnp.concatenate([dq, dk, dv, dgr, du_pool, dgab], axis=1)
        dh2 = dh
        rd = reducer.rider(carry["inproj_bwd"])
        dh, small["mix_norm"][i] = _inproj_bwd_dx(
            dz, wts("w_in", i), s["h1"], gains["mix_norm"], dh2, i, tl, pad, f"inproj_bwd_{i}", rd)
        reducer.done(rd)
        grad("w_in", s["b"], dz, i, "col")
        reducer.stage(f"mid{i}")
        dh, dg, du, small["ffn1_norm"][i], dyh = ffn_bwd("ffn1", dh, s["h0"], s["g1"], s["u1"], i)
        grad("ffn1_gate", dg, s["a1"], i, "row")
        if i == 0:
            reducer.stage("gate0")
        grad("ffn1_up", du, s["a1"], i, "row")
        if i == 0:
            reducer.stage("up0")
        grad("ffn1_down", s["act1"], dyh, i, "row")
        reducer.stage(f"end{i}")

    return loss_acc, dh, small, d_final


class _Reducer:
    def __init__(self, unit):
        self.c_idx = lax.axis_index("c").astype(jnp.int32).reshape(1)
        chip = 2 * lax.axis_index("x") + lax.axis_index("y")
        self.pos = jnp.stack([chip, lax.axis_index("c")]).astype(jnp.int32)
        self.pending, self.stages, self.queue, self.halves, self.whole = [], [], [], {}, {}
        self.unit = unit
        self.calls = 0

    def add(self, name, layer, g):
        self.pending.append(((name, layer), g))

    def stage(self, tag):
        if self.pending:
            self.stages.append((tag, self.pending))
            self.pending = []

    def _pair_rider(self):
        if not self.stages:
            return None
        tag, items = self.stages.pop(0)
        rd = _pair_exchange_rider([g for _, g in items])
        rd.tag, rd.keys = tag, [k for k, _ in items]
        return rd

    def _chip_rider(self, units):
        take, size = [], 0
        while self.queue and (units is None or size + self.queue[0][1].size <= units * self.unit):
            take.append(self.queue.pop(0))
            size += take[-1][1].size
        if not take:
            return None
        rd = _chip_exchange_rider([p for _, p in take])
        rd.keys = [k for k, _ in take]
        return rd

    def _gather_rider(self):
        keys = [k for k in self.halves if k not in self.whole]
        if not keys:
            return None
        rd = _pair_gather_rider([self.halves[k] for k in keys])
        rd.keys = keys
        return rd

    def rider(self, units):
        self.riding = (self._pair_rider(), self._chip_rider(units), self._gather_rider())
        return _join(self.riding)

    def done(self, rd):
        if rd is None:
            return
        _split_results(rd)
        pair, chips, gather = self.riding
        if len([r for r in self.riding if r is not None]) == 1:
            (pair or chips or gather).results = rd.results
        self.calls += 1
        if gather is not None:
            self.whole.update(zip(gather.keys, gather.results))
        if pair is not None:
            sums = _sum_pair(pair.ins, pair.results, self.c_idx, f"sum_pair_{pair.tag}")
            self.queue += list(zip(pair.keys, sums))
        if chips is not None:
            sums = _sum_chips(chips.ins, chips.results, self.pos, f"sum_chips_{self.calls}")
            self.halves.update(zip(chips.keys, sums))

    def finish(self):
        assert not self.pending
        while self.stages or self.queue or len(self.whole) < len(self.halves):
            self.riding = (self._pair_rider(), self._chip_rider(None), self._gather_rider())
            rd = _join(self.riding)
            _run_rider(rd, f"grads_exchange_tail_{self.calls}")
            self.done(rd)
        return self.whole


def _update(loss, grad_x, d_meta_rows, shard_grads, small, d_final, w, mom, var):
    meta = w["meta"]
    D = w["final_norm"].shape[0]
    L = w["ffn1_norm"].shape[0]
    Dq = D // N_CHIPS

    small_parts = [jnp.concatenate(small[n], axis=0) for n in ("ffn1_norm", "mix_norm", "ffn2_norm")]
    small_parts += [d_final, jnp.concatenate(small["pool_scale"], axis=0), jnp.concatenate(small["pool_maps"], axis=0)]
    reduced = _small_all_reduce(_pack_rows(small_parts + [d_meta_rows], D))
    small_shapes = [w[n].shape for n in _SMALL]
    small_rows = sum(math.prod(shp) for shp in small_shapes) // D
    chip = 2 * lax.axis_index("x") + lax.axis_index("y")
    d_meta = lax.dynamic_slice_in_dim(reduced[small_rows:small_rows + N_META], chip * Dq, Dq, axis=1)

    out = {}
    for n in _BIG:
        gs = [shard_grads[(n, i)] for i in range(L)]
        if n in _TRANSPOSED:
            res = _adamw(gs, *(jnp.swapaxes(t[n], 1, 2) for t in (w, mom, var)), f"adamw_{n}")
            out[n] = [jnp.swapaxes(r, 1, 2) for r in res]
        else:
            out[n] = _adamw(gs, w[n], mom[n], var[n], f"adamw_{n}")
    names = _SMALL + ("meta",)
    packed_g = _pack_rows([reduced[:small_rows], d_meta], D)
    packed = [_pack_rows([t[n] for n in names], D) for t in (w, mom, var)]
    res = _adamw([packed_g], packed[0][None], packed[1][None], packed[2][None], "adamw_small")
    shapes = small_shapes + [meta.shape]
    unpacked = [_unpack_rows(r[0], shapes, D) for r in res]
    for k, n in enumerate(names):
        out[n] = tuple(u[k] for u in unpacked)

    return (loss, grad_x) + tuple(out[n][j] for j in range(4) for n in _ORDER)


def kernel(x, meta, ffn1_norm, ffn1_gate, ffn1_up, ffn1_down, mix_norm, w_in, pool_maps, pool_scale, w_ret_up, w_pool_up, w_out, ffn2_norm, ffn2_gate, ffn2_up, ffn2_down, final_norm, loss_target, m_meta, m_ffn1_norm, m_ffn1_gate, m_ffn1_up, m_ffn1_down, m_mix_norm, m_w_in, m_pool_maps, m_pool_scale, m_w_ret_up, m_w_pool_up, m_w_out, m_ffn2_norm, m_ffn2_gate, m_ffn2_up, m_ffn2_down, m_final_norm, v_meta, v_ffn1_norm, v_ffn1_gate, v_ffn1_up, v_ffn1_down, v_mix_norm, v_w_in, v_pool_maps, v_pool_scale, v_w_ret_up, v_w_pool_up, v_w_out, v_ffn2_norm, v_ffn2_gate, v_ffn2_up, v_ffn2_down, v_final_norm):
    args = dict(locals())
    w = {n: args[n] for n in _ORDER}
    mom = {n: args["m_" + n] for n in _ORDER}
    var = {n: args["v_" + n] for n in _ORDER}

    assert x.shape[0] == 1, "one batch element per device"
    seq, D = x.shape[1], x.shape[2]
    assert seq % CHUNK == 0 and D % RET_WIDTH == 0 and (2 * POOL_WIDTH) % D == 0
    pad = (-(seq + N_META)) % CHUNK
    T = seq + N_META + pad
    tm = _pick_tile(T, 528, BF16_ROWS)
    cg = _pick_tile(T // CHUNK, 11, 1)

    shards = {n: _transport(w[n]) for n in _BIG}
    shards["meta"] = meta[None]
    wts = _Weights(shards)
    head = wts.rider([(n, 0) for n in _FFN1] + [("meta", 0)])
    _run_rider(head, "weights_gather_head")
    wts.take(head)
    meta_full = jnp.transpose(wts("meta", 0), (1, 0, 2)).reshape(N_META, D)

    reducer = _Reducer(unit=2 * shards["ffn1_gate"][0].size)
    loss_acc, dh, small, d_final = _local_step(x[0], meta_full, loss_target[0], w, wts, pad, tm, cg, reducer)
    loss = lax.psum(loss_acc[0, 0], ("x", "y", "c"))
    grad_x = dh[pad + N_META:][None]
    return _update(loss, grad_x, dh[pad:pad + N_META], reducer.finish(), small, d_final, w, mom, var)
```

```python
import functools
import math

import jax
import jax.numpy as jnp
from jax import lax
from jax.experimental import pallas as pl
from jax.experimental.pallas import tpu as pltpu

F32 = jnp.float32
BF16 = jnp.bfloat16

N_META = 16
RET_HEADS = 4
HEAD_DIM = 128
RET_WIDTH = RET_HEADS * HEAD_DIM
POOL_WINDOWS = (2, 4, 8, 16)
POOL_GROUPS = len(POOL_WINDOWS)
POOL_WIDTH = POOL_GROUPS * HEAD_DIM
CHUNK = 128
ROPE_BASE = 10000.0
EPS = 1e-6
ADAM_LR = 0.001
ADAM_B1 = 0.9
ADAM_B2 = 0.999
ADAM_EPS = 1e-08
ADAM_WD = 0.01
ADAM_STEP = 10

N_CHIPS = 4
LANES = 128
BF16_ROWS = 16
V7X_VMEM_LIMIT = 52 * 1024 * 1024
MESH = pl.DeviceIdType.MESH
ANY = pl.BlockSpec(memory_space=pl.ANY)


def _round_up(n, m):
    return -(-n // m) * m


def _pick_tile(n, target, mult):
    best = None
    for d in range(mult, min(n, target) + 1, mult):
        if n % d == 0:
            best = d
    assert best is not None, (n, target, mult)
    return best


def _params(sem=None):
    return pltpu.CompilerParams(dimension_semantics=sem, vmem_limit_bytes=V7X_VMEM_LIMIT)


def _dot(a, b):
    return jnp.dot(a, b, preferred_element_type=F32)


def _dot_nt(a, b):
    return lax.dot_general(a, b, (((1,), (1,)), ((), ())), preferred_element_type=F32)


def _dot_tn(a, b):
    return lax.dot_general(a, b, (((0,), (0,)), ((), ())), preferred_element_type=F32)


def _ein(spec, a, b):
    return jnp.einsum(spec, a, b, preferred_element_type=F32)


def _sigmoid(x):
    return jax.nn.sigmoid(x)


def _rms_fwd(x, gain):
    r = lax.rsqrt(jnp.mean(x * x, axis=-1, keepdims=True) + EPS)
    return x * r * gain


def _rms_bwd(x, gain, da):
    r = lax.rsqrt(jnp.mean(x * x, axis=-1, keepdims=True) + EPS)
    xh = x * r
    dgain = jnp.sum(da * xh, axis=0, keepdims=True)
    dxh = da * gain
    dx = r * (dxh - xh * jnp.mean(dxh * xh, axis=-1, keepdims=True))
    return dx, dgain


def _row_mask(t, tm, pad, shape):
    rows = t * tm + lax.broadcasted_iota(jnp.int32, shape, 0)
    return rows >= pad


def _mesh_pos():
    x, y, c = lax.axis_index("x"), lax.axis_index("y"), lax.axis_index("c")
    others = [(1 - x, y), (x, 1 - y), (1 - x, 1 - y)]
    return x, y, c, 2 * x + y, others


def _half_rows(c, rh):
    return pl.ds(pl.multiple_of(c * rh, rh), rh)


def _remote(src, dst, ssem, rsem, dev):
    return pltpu.make_async_remote_copy(src_ref=src, dst_ref=dst, send_sem=ssem, recv_sem=rsem,
                                        device_id=dev, device_id_type=MESH)


class _Rider:
    def __init__(self, ins, out_shapes, n_sem, start, finish, in_place=False):
        self.ins, self.out_shapes, self.n_sem, self.start, self.finish = ins, out_shapes, n_sem, start, finish
        self.in_place = [in_place] * len(ins)
        self.results = None

    def aliases(self, first_in, first_out):
        return {first_in + i: first_out + i for i, same in enumerate(self.in_place) if same}


class _SemWindow:
    def __init__(self, ref, base):
        self.ref, self.base = ref, base

    @property
    def at(self):
        return self

    def __getitem__(self, k):
        return self.ref.at[self.base + k]


def _join(riders):
    riders = [r for r in riders if r is not None]
    if len(riders) <= 1:
        return riders[0] if riders else None

    def run(which):
        def go(ins, outs, ssem, rsem):
            at, sem = 0, 0
            for r in riders:
                n = len(r.ins)
                getattr(r, which)(ins[at:at + n], outs[at:at + n], _SemWindow(ssem, sem), _SemWindow(rsem, sem))
                at, sem = at + n, sem + r.n_sem
        return go

    joined = _Rider(sum([list(r.ins) for r in riders], []), sum([list(r.out_shapes) for r in riders], []),
                    sum(r.n_sem for r in riders), run("start"), run("finish"))
    joined.in_place = sum([r.in_place for r in riders], [])
    joined.parts = riders
    return joined


def _split_results(rider):
    at = 0
    for r in getattr(rider, "parts", []):
        r.results = rider.results[at:at + len(r.ins)]
        at += len(r.ins)


def _gather_rider(pieces):
    per = 7
    layers = [layer for _, layer in pieces]

    def first_copies(ins, outs, ssem, rsem):
        x, y, c, chip, others = _mesh_pos()
        copies = []
        for i, layer in enumerate(layers):
            mine = _half_rows(c, ins[i].shape[1] // 2)
            for j, (ox, oy) in enumerate(others):
                copies.append(_remote(ins[i].at[layer, mine, :], outs[i].at[chip, mine, :],
                                      ssem.at[per * i + j], rsem.at[per * i + j], (ox, oy, c)))
            copies.append(_remote(ins[i].at[layer], outs[i].at[chip],
                                  ssem.at[per * i + 6], rsem.at[per * i + 6], (x, y, 1 - c)))
        return copies

    def start(ins, outs, ssem, rsem):
        for cp in first_copies(ins, outs, ssem, rsem):
            cp.start()

    def finish(ins, outs, ssem, rsem):
        x, y, c, chip, others = _mesh_pos()
        sibling = (x, y, 1 - c)
        forwards = []
        for i in range(len(layers)):
            mine = _half_rows(c, ins[i].shape[1] // 2)
            for j, (ox, oy) in enumerate(others):
                rows = outs[i].at[2 * ox + oy, mine, :]
                _remote(rows, rows, ssem.at[per * i + j], rsem.at[per * i + j], (ox, oy, c)).wait_recv()
                fwd = _remote(rows, rows, ssem.at[per * i + 3 + j], rsem.at[per * i + 3 + j], sibling)
                fwd.start()
                forwards.append(fwd)
        for i in range(len(layers)):
            theirs = _half_rows(1 - c, ins[i].shape[1] // 2)
            for j, (ox, oy) in enumerate(others):
                rows = outs[i].at[2 * ox + oy, theirs, :]
                _remote(rows, rows, ssem.at[per * i + 3 + j], rsem.at[per * i + 3 + j], sibling).wait_recv()
            own = outs[i].at[chip]
            _remote(own, own, ssem.at[per * i + 6], rsem.at[per * i + 6], sibling).wait_recv()
        for cp in first_copies(ins, outs, ssem, rsem) + forwards:
            cp.wait_send()

    shapes = [jax.ShapeDtypeStruct((N_CHIPS,) + s.shape[1:], s.dtype) for s, _ in pieces]
    return _Rider([s for s, _ in pieces], shapes, per * len(pieces), start, finish)


def _chip_exchange_rider(ps):
    def copies(ins, outs, ssem, rsem):
        x, y, c, chip, others = _mesh_pos()
        return [_remote(ins[i].at[2 * ox + oy], outs[i].at[chip], ssem.at[3 * i + j], rsem.at[3 * i + j], (ox, oy, c))
                for i in range(len(ps)) for j, (ox, oy) in enumerate(others)]

    def start(ins, outs, ssem, rsem):
        for cp in copies(ins, outs, ssem, rsem):
            cp.start()

    def finish(ins, outs, ssem, rsem):
        x, y, c, chip, others = _mesh_pos()
        for i in range(len(ps)):
            for j, (ox, oy) in enumerate(others):
                slot = outs[i].at[2 * ox + oy]
                _remote(slot, slot, ssem.at[3 * i + j], rsem.at[3 * i + j], (ox, oy, c)).wait_recv()
        for cp in copies(ins, outs, ssem, rsem):
            cp.wait_send()

    return _Rider(list(ps), [jax.ShapeDtypeStruct(p.shape, p.dtype) for p in ps], 3 * len(ps), start, finish)


def _pair_exchange_rider(gs):
    def copies(ins, outs, ssem, rsem):
        x, y, c, _, _ = _mesh_pos()
        return [_remote(ins[i].at[:, _half_rows(1 - c, ins[i].shape[1] // 2), :], outs[i],
                        ssem.at[i], rsem.at[i], (x, y, 1 - c)) for i in range(len(gs))]

    def start(ins, outs, ssem, rsem):
        for cp in copies(ins, outs, ssem, rsem):
            cp.start()

    def finish(ins, outs, ssem, rsem):
        for cp in copies(ins, outs, ssem, rsem):
            cp.wait()

    shapes = [jax.ShapeDtypeStruct((g.shape[0], g.shape[1] // 2, g.shape[2]), g.dtype) for g in gs]
    return _Rider(list(gs), shapes, len(gs), start, finish)


def _run_rider(rider, name):
    def body(*refs):
        n = len(rider.ins)
        ins, outs = refs[:n], refs[n:2 * n]
        ssem, rsem = refs[2 * n:]
        rider.start(ins, outs, ssem, rsem)
        rider.finish(ins, outs, ssem, rsem)

    rider.results = pl.pallas_call(
        body,
        name=name,
        in_specs=[ANY] * len(rider.ins),
        out_specs=[ANY] * len(rider.ins),
        out_shape=rider.out_shapes,
        input_output_aliases=rider.aliases(0, 0),
        scratch_shapes=[pltpu.SemaphoreType.DMA((rider.n_sem,)), pltpu.SemaphoreType.DMA((rider.n_sem,))],
    )(*rider.ins)
    return rider.results


def _pair_gather_rider(fs):
    n = len(fs)

    def copies(outs, ssem, rsem):
        x, y, c, _, _ = _mesh_pos()
        halves = [outs[i].at[_half_rows(c, outs[i].shape[0] // 2), :] for i in range(n)]
        return [_remote(h, h, ssem.at[i], rsem.at[i], (x, y, 1 - c)) for i, h in enumerate(halves)]

    def start(ins, outs, ssem, rsem):
        for cp in copies(outs, ssem, rsem):
            cp.start()

    def finish(ins, outs, ssem, rsem):
        x, y, c, _, _ = _mesh_pos()
        for i in range(n):
            theirs = outs[i].at[_half_rows(1 - c, outs[i].shape[0] // 2), :]
            _remote(theirs, theirs, ssem.at[i], rsem.at[i], (x, y, 1 - c)).wait_recv()
        for cp in copies(outs, ssem, rsem):
            cp.wait_send()

    return _Rider(list(fs), [jax.ShapeDtypeStruct(f.shape, f.dtype) for f in fs], n, start, finish, in_place=True)


def _call(body, *, name, grid, in_specs, out_specs, out_shape, operands, scratch=(), sem=None, rider=None):
    if rider is None:
        return pl.pallas_call(
            body, name=name, grid=grid, in_specs=in_specs, out_specs=out_specs, out_shape=out_shape,
            scratch_shapes=list(scratch), compiler_params=_params(sem))(*operands)
    n_in, n_out, n_sc, r = len(in_specs), len(out_specs), len(scratch), len(rider.ins)

    def carrying(*refs):
        a, b = n_in, n_in + r
        c, d = b + n_out, b + n_out + r
        e = d + n_sc
        ids = [pl.program_id(k) for k in range(len(grid))]
        first = functools.reduce(jnp.logical_and, [i == 0 for i in ids])
        last = functools.reduce(jnp.logical_and, [i == g - 1 for i, g in zip(ids, grid)])

        @pl.when(first)
        def _():
            rider.start(refs[a:b], refs[c:d], refs[e], refs[e + 1])

        body(*refs[:a], *refs[b:c], *refs[d:e])

        @pl.when(last)
        def _():
            rider.finish(refs[a:b], refs[c:d], refs[e], refs[e + 1])

    outs = pl.pallas_call(
        carrying, name=name, grid=grid,
        in_specs=list(in_specs) + [ANY] * r,
        out_specs=list(out_specs) + [ANY] * r,
        out_shape=list(out_shape) + list(rider.out_shapes),
        scratch_shapes=list(scratch) + [pltpu.SemaphoreType.DMA((rider.n_sem,)), pltpu.SemaphoreType.DMA((rider.n_sem,))],
        input_output_aliases=rider.aliases(n_in, n_out),
        compiler_params=_params(("arbitrary",) * len(grid)),
    )(*operands, *rider.ins)
    rider.results = outs[n_out:]
    return outs[:n_out]


def _ffn_fwd(h, gain, wg, wu, wd, layer, tm, name, rider=None):
    T, D = h.shape
    Fs = wg.shape[-1]
    F = N_CHIPS * Fs

    def body(h_ref, g_ref, wg_ref, wu_ref, wd_ref, ho_ref, a_ref, go_ref, uo_ref, act_ref, acc_ref):
        s = pl.program_id(1)

        @pl.when(s == 0)
        def _():
            a_ref[...] = _rms_fwd(h_ref[...], g_ref[...]).astype(BF16)
            acc_ref[...] = jnp.zeros_like(acc_ref)

        a = a_ref[...]
        g = _dot(a, wg_ref[...])
        u = _dot(a, wu_ref[...])
        sg = _sigmoid(g)
        act = (g * sg * u).astype(BF16)
        go_ref[...] = (u * (sg * (1.0 + g * (1.0 - sg)))).astype(BF16)
        uo_ref[...] = (g * sg).astype(BF16)
        act_ref[...] = act
        acc_ref[...] += _dot(act, wd_ref[...])

        @pl.when(s == N_CHIPS - 1)
        def _():
            ho_ref[...] = h_ref[...] + 0.5 * acc_ref[...]

    row = pl.BlockSpec((tm, D), lambda t, s: (t, 0))
    col = pl.BlockSpec((tm, Fs), lambda t, s: (t, s))
    wcol = pl.BlockSpec((None, D, Fs), lambda t, s: (s, 0, 0))
    return _call(
        body, name=name, grid=(T // tm, N_CHIPS),
        in_specs=[row, pl.BlockSpec((None, 1, D), lambda t, s: (layer, 0, 0)), wcol, wcol,
                  pl.BlockSpec((None, Fs, D), lambda t, s: (s, 0, 0))],
        out_specs=[row, row, col, col, col],
        out_shape=[jax.ShapeDtypeStruct((T, D), F32), jax.ShapeDtypeStruct((T, D), BF16)]
        + [jax.ShapeDtypeStruct((T, F), BF16)] * 3,
        scratch=[pltpu.VMEM((tm, D), F32)],
        sem=("parallel", "arbitrary"), operands=(h, gain, wg, wu, wd), rider=rider)


def _inproj_fwd(h, gain, win, layer, tm, name, rider=None):
    T, D = h.shape
    Ns = win.shape[-1]

    def body(h_ref, g_ref, w_ref, z_ref, b_ref):
        @pl.when(pl.program_id(1) == 0)
        def _():
            b_ref[...] = _rms_fwd(h_ref[...], g_ref[...]).astype(BF16)

        z_ref[...] = _dot(b_ref[...], w_ref[...]).astype(BF16)

    return _call(
        body, name=name, grid=(T // tm, N_CHIPS),
        in_specs=[pl.BlockSpec((tm, D), lambda t, s: (t, 0)),
                  pl.BlockSpec((None, 1, D), lambda t, s: (layer, 0, 0)),
                  pl.BlockSpec((None, D, Ns), lambda t, s: (s, 0, 0))],
        out_specs=[pl.BlockSpec((tm, Ns), lambda t, s: (t, s)), pl.BlockSpec((tm, D), lambda t, s: (t, 0))],
        out_shape=[jax.ShapeDtypeStruct((T, N_CHIPS * Ns), BF16), jax.ShapeDtypeStruct((T, D), BF16)],
        sem=("parallel", "arbitrary"), operands=(h, gain, win), rider=rider)


def _ret_consts(T, pad):
    half = HEAD_DIM // 2
    inv_freq = ROPE_BASE ** (-jnp.arange(half, dtype=F32) / half)
    pos = jnp.arange(T, dtype=F32) - pad
    ang = pos[:, None] * inv_freq[None, :]
    cos = jnp.cos(ang)
    sin = jnp.sin(ang)
    cosf = jnp.concatenate([cos, cos], axis=1)
    sinf = jnp.concatenate([-sin, sin], axis=1)
    log_gamma = jnp.log1p(-(2.0 ** (-5.0 - jnp.arange(RET_HEADS, dtype=F32))))
    idx = jnp.arange(CHUNK, dtype=F32)
    diff = idx[:, None] - idx[None, :]
    intra = jnp.where(diff[None] >= 0, jnp.exp(diff[None] * log_gamma[:, None, None]), 0.0)
    k_decay = jnp.exp((CHUNK - 1.0 - idx)[None, :] * log_gamma[:, None])
    q_decay = jnp.exp((idx + 1.0)[None, :] * log_gamma[:, None])
    chunk_decay = jnp.exp(CHUNK * log_gamma)
    kdec = jnp.broadcast_to(k_decay[:, :, None], (RET_HEADS, CHUNK, HEAD_DIM))
    qdec = jnp.broadcast_to(q_decay[:, :, None], (RET_HEADS, CHUNK, HEAD_DIM))
    cdb = jnp.broadcast_to(chunk_decay[:, None, None], (RET_HEADS, 8, HEAD_DIM))
    return cosf, sinf, intra, kdec, qdec, cdb


def _rot(t, cosv, sinv):
    return t * cosv + pltpu.roll(t, HEAD_DIM // 2, 1) * sinv


def _rot_t(g, cosv, sinv):
    return g * cosv + pltpu.roll(g * sinv, HEAD_DIM // 2, 1)


def _head_specs(tg, section, order):
    return pl.BlockSpec((tg, HEAD_DIM), lambda h, g: (order(g), section * RET_HEADS + h))


def _ret_fwd(z, consts, cg, name, rider=None):
    T = z.shape[0]
    N = T // CHUNK
    ng = N // cg
    tg = cg * CHUNK
    cosf, sinf, intra, kdec, qdec, cdb = consts
    fwd = lambda g: g

    def body(zq, zk, zv, zg, cos_ref, sin_ref, m_ref, kd_ref, qd_ref, cd_ref, r_ref, o_ref, s_ref, st_ref):
        @pl.when(pl.program_id(1) == 0)
        def _():
            st_ref[...] = jnp.zeros_like(st_ref)

        cosv = cos_ref[...]
        sinv = sin_ref[...]
        q3 = (_rot(zq[...].astype(F32), cosv, sinv) * (HEAD_DIM ** -0.5)).reshape(cg, CHUNK, HEAD_DIM)
        k3 = _rot(zk[...].astype(F32), cosv, sinv).reshape(cg, CHUNK, HEAD_DIM)
        vb = zv[...].reshape(cg, CHUNK, HEAD_DIM).astype(BF16)
        scores = _ein("ncd,nmd->ncm", q3.astype(BF16), k3.astype(BF16)) * m_ref[...][None]
        inner = _ein("ncm,nmd->ncd", scores.astype(BF16), vb)
        kv = _ein("ncd,nce->nde", (k3 * kd_ref[...][None]).astype(BF16), vb)
        cd = cd_ref[0:1, :]
        state = st_ref[...]
        for n in range(cg):
            s_ref[n] = state
            state = state * cd + kv[n]
        st_ref[...] = state
        qdb = (q3 * qd_ref[...][None]).astype(BF16)
        cross = _ein("ncd,nde->nce", qdb, s_ref[...].astype(BF16))
        out = (inner + cross).reshape(tg, HEAD_DIM)
        o_ref[...] = out
        xc = out - jnp.mean(out, axis=-1, keepdims=True)
        rn = xc * lax.rsqrt(jnp.mean(xc * xc, axis=-1, keepdims=True) + EPS)
        g = zg[...].astype(F32)
        r_ref[...] = (rn * (g * _sigmoid(g))).astype(BF16)

    tab = pl.BlockSpec((tg, HEAD_DIM), lambda h, g: (g, 0))
    per_head = lambda rows: pl.BlockSpec((None, rows, HEAD_DIM), lambda h, g: (h, 0, 0))
    head_out = pl.BlockSpec((tg, HEAD_DIM), lambda h, g: (g, h))
    return _call(
        body, name=name, grid=(RET_HEADS, ng),
        in_specs=[_head_specs(tg, i, fwd) for i in range(4)]
        + [tab, tab, per_head(CHUNK), per_head(CHUNK), per_head(CHUNK), per_head(8)],
        out_specs=[head_out, head_out, pl.BlockSpec((None, cg, HEAD_DIM, HEAD_DIM), lambda h, g: (h, g, 0, 0))],
        out_shape=[jax.ShapeDtypeStruct((T, RET_WIDTH), BF16), jax.ShapeDtypeStruct((T, RET_WIDTH), F32),
                   jax.ShapeDtypeStruct((RET_HEADS, N, HEAD_DIM, HEAD_DIM), F32)],
        scratch=[pltpu.VMEM((HEAD_DIM, HEAD_DIM), F32)],
        sem=("parallel", "arbitrary"), operands=(z, z, z, z, cosf, sinf, intra, kdec, qdec, cdb), rider=rider)


def _window_sums(u, shift_of):
    sums = []
    s = u
    k = 1
    while k < POOL_WINDOWS[-1]:
        s = s + pltpu.roll(s, shift_of(k), 0)
        sums.append(s)
        k *= 2
    return sums


def _select_group(vals, g):
    out = vals[-1]
    for i in range(len(vals) - 2, -1, -1):
        out = jnp.where(g == i, vals[i], out)
    return out


def _pool_parts(u, g, T, pad):
    rows = lax.broadcasted_iota(jnp.int32, (T, HEAD_DIM), 0)
    valid = rows >= pad
    win = _select_group([float(w) for w in POOL_WINDOWS], g)
    div = jnp.clip((rows - pad + 1).astype(F32), 1.0, win)
    s = _select_group(_window_sums(u, lambda k: k), g)
    pooled = jnp.where(valid, s / div - u, 0.0)
    return pooled, div, valid


def _pool_specs(T, layer):
    first = 4 * RET_WIDTH // HEAD_DIM
    return [
        pl.BlockSpec((T, HEAD_DIM), lambda g: (0, first + g)),
        pl.BlockSpec((None, None, HEAD_DIM, HEAD_DIM), lambda g: (layer, g, 0, 0)),
        pl.BlockSpec((None, 1, HEAD_DIM), lambda g: (layer, 0, g)),
    ]


def _pool_fwd(z, maps, scale, layer, pad, name):
    T = z.shape[0]
    assert pad >= POOL_WINDOWS[-1], "window rolls wrap into the zero rows in front"

    def body(zu, maps_ref, sc_ref, pm_ref):
        g = pl.program_id(0)
        pooled, _, _ = _pool_parts(zu[...].astype(F32), g, T, pad)
        y = _dot(pooled.astype(BF16), maps_ref[...].astype(BF16))
        pm_ref[...] = (y * sc_ref[...]).astype(BF16)

    return _call(
        body, name=name, grid=(POOL_GROUPS,),
        in_specs=_pool_specs(T, layer),
        out_specs=[pl.BlockSpec((T, HEAD_DIM), lambda g: (0, g))],
        out_shape=[jax.ShapeDtypeStruct((T, POOL_WIDTH), BF16)],
        sem=("parallel",), operands=(z, maps, scale))[0]


def _gate_specs(tm, D):
    nb = D // RET_WIDTH
    first = (4 * RET_WIDTH + POOL_WIDTH) // RET_WIDTH
    return [pl.BlockSpec((tm, RET_WIDTH), functools.partial(lambda t, j: (t, j), j=first + j)) for j in range(2 * nb)]


def _load_gates(refs, nb):
    ga = jnp.concatenate([r[...].astype(F32) for r in refs[:nb]], axis=1)
    gb = jnp.concatenate([r[...].astype(F32) for r in refs[nb:]], axis=1)
    return ga, gb


def _mix_fwd(h, r, pm, z, wru, wpu, wout, tm, name, rider=None):
    T, D = h.shape
    Dq = D // N_CHIPS
    nb = D // RET_WIDTH

    def body(*refs):
        h_ref, r_ref, pm_ref = refs[:3]
        gate_refs = refs[3:3 + 2 * nb]
        wru_ref, wpu_ref, wout_ref, ho_ref, mx_ref, ret_ref, pool_ref = refs[3 + 2 * nb:]
        rv = r_ref[...]
        pv = pm_ref[...]
        ret = jnp.concatenate([_dot(rv, wru_ref[s]) for s in range(N_CHIPS)], axis=1)
        pool = jnp.concatenate([_dot(pv, wpu_ref[s]) for s in range(N_CHIPS)], axis=1)
        ga, gb = _load_gates(gate_refs, nb)
        mixed = (_sigmoid(ga) * ret + _sigmoid(gb) * pool).astype(BF16)
        mx_ref[...] = mixed
        ret_ref[...] = ret.astype(BF16)
        pool_ref[...] = pool.astype(BF16)
        ho_ref[...] = h_ref[...] + _dot(mixed, wout_ref[...].reshape(D, D))

    row = pl.BlockSpec((tm, D), lambda t: (t, 0))
    half = pl.BlockSpec((tm, RET_WIDTH), lambda t: (t, 0))
    up = pl.BlockSpec((N_CHIPS, RET_WIDTH, Dq), lambda t: (0, 0, 0))
    return _call(
        body, name=name, grid=(T // tm,),
        in_specs=[row, half, half] + _gate_specs(tm, D) + [up, up, pl.BlockSpec((N_CHIPS, Dq, D), lambda t: (0, 0, 0))],
        out_specs=[row, row, row, row],
        out_shape=[jax.ShapeDtypeStruct((T, D), F32)] + [jax.ShapeDtypeStruct((T, D), BF16)] * 3,
        sem=("parallel",), operands=(h, r, pm, *([z] * (2 * nb)), wru, wpu, wout), rider=rider)


def _final_loss(h, gain, tgt, name):
    T, D = h.shape
    first = (T - tgt.shape[0]) // CHUNK

    def body(h_ref, g_ref, t_ref, dh_ref, loss_ref, dg_ref):
        i = pl.program_id(0)

        @pl.when(i == 0)
        def _():
            loss_ref[...] = jnp.zeros_like(loss_ref)
            dg_ref[...] = jnp.zeros_like(dg_ref)

        x = h_ref[...]
        gain_v = g_ref[...]
        err = jnp.where(i >= first, _rms_fwd(x, gain_v) - t_ref[...], 0.0)
        loss_ref[...] += 0.5 * jnp.sum(jnp.mean(err * err, axis=-1))
        dx, dgain = _rms_bwd(x, gain_v, err * (1.0 / D))
        dg_ref[...] += dgain
        dh_ref[...] = dx

    return _call(
        body, name=name, grid=(T // CHUNK,),
        in_specs=[pl.BlockSpec((CHUNK, D), lambda i: (i, 0)),
                  pl.BlockSpec((1, D), lambda i: (0, 0)),
                  pl.BlockSpec((CHUNK, D), lambda i: (jnp.maximum(i - first, 0), 0))],
        out_specs=[pl.BlockSpec((CHUNK, D), lambda i: (i, 0)),
                   pl.BlockSpec((1, LANES), lambda i: (0, 0)),
                   pl.BlockSpec((1, D), lambda i: (0, 0))],
        out_shape=[jax.ShapeDtypeStruct((T, D), F32), jax.ShapeDtypeStruct((1, LANES), F32),
                   jax.ShapeDtypeStruct((1, D), F32)],
        sem=("arbitrary",), operands=(h, gain, tgt))


def _ffn_bwd_act(dy, g, u, wd, tm, name, rider=None):
    T, D = dy.shape
    Fs = wd.shape[1]
    F = N_CHIPS * Fs

    def body(dy_ref, go_ref, uo_ref, wd_ref, dg_ref, du_ref, dyh_ref):
        @pl.when(pl.program_id(1) == 0)
        def _():
            dyh_ref[...] = (0.5 * dy_ref[...]).astype(BF16)

        dact = _dot_nt(dyh_ref[...], wd_ref[...])
        du_ref[...] = (dact * uo_ref[...].astype(F32)).astype(BF16)
        dg_ref[...] = (dact * go_ref[...].astype(F32)).astype(BF16)

    row = pl.BlockSpec((tm, D), lambda t, s: (t, 0))
    col = pl.BlockSpec((tm, Fs), lambda t, s: (t, s))
    return _call(
        body, name=name, grid=(T // tm, N_CHIPS),
        in_specs=[row, col, col, pl.BlockSpec((None, Fs, D), lambda t, s: (s, 0, 0))],
        out_specs=[col, col, row],
        out_shape=[jax.ShapeDtypeStruct((T, F), BF16), jax.ShapeDtypeStruct((T, F), BF16),
                   jax.ShapeDtypeStruct((T, D), BF16)],
        sem=("parallel", "arbitrary"), operands=(dy, g, u, wd), rider=rider)


def _ffn_bwd_in(dy, h, gain, dg, du, wg, wu, layer, tm, pad, name, rider=None):
    T, D = h.shape
    Fs = wg.shape[-1]

    def body(dy_ref, h_ref, g_ref, dg_ref, du_ref, wg_ref, wu_ref, dh_ref, dgain_ref, da_ref):
        t = pl.program_id(0)
        s = pl.program_id(1)

        @pl.when((t == 0) & (s == 0))
        def _():
            dgain_ref[...] = jnp.zeros_like(dgain_ref)

        @pl.when(s == 0)
        def _():
            da_ref[...] = jnp.zeros_like(da_ref)

        da_ref[...] += _dot_nt(dg_ref[...], wg_ref[...]) + _dot_nt(du_ref[...], wu_ref[...])

        @pl.when(s == N_CHIPS - 1)
        def _():
            dx, dgain = _rms_bwd(h_ref[...], g_ref[...], da_ref[...])
            dgain_ref[...] += dgain
            dh_ref[...] = jnp.where(_row_mask(t, tm, pad, (tm, D)), dy_ref[...] + dx, 0.0)

    row = pl.BlockSpec((tm, D), lambda t, s: (t, 0))
    col = pl.BlockSpec((tm, Fs), lambda t, s: (t, s))
    wcol = pl.BlockSpec((None, D, Fs), lambda t, s: (s, 0, 0))
    return _call(
        body, name=name, grid=(T // tm, N_CHIPS),
        in_specs=[row, row, pl.BlockSpec((None, 1, D), lambda t, s: (layer, 0, 0)), col, col, wcol, wcol],
        out_specs=[row, pl.BlockSpec((1, D), lambda t, s: (0, 0))],
        out_shape=[jax.ShapeDtypeStruct((T, D), F32), jax.ShapeDtypeStruct((1, D), F32)],
        scratch=[pltpu.VMEM((tm, D), F32)],
        sem=("arbitrary", "arbitrary"), operands=(dy, h, gain, dg, du, wg, wu), rider=rider)


def _grad_tn(a, b, mode, scale, tm, name, rider=None):
    T = a.shape[0]
    if mode == "col":
        per, R, C = 1, a.shape[1], b.shape[1] // N_CHIPS
        a_spec = pl.BlockSpec((tm, R), lambda s, t: (t, 0))
        b_spec = pl.BlockSpec((tm, C), lambda s, t: (t, s))
    else:
        per, R, C = 2, a.shape[1] // N_CHIPS, b.shape[1]
        a_spec = pl.BlockSpec((tm, per * R), lambda s, t: (t, s))
        b_spec = pl.BlockSpec((tm, C), lambda s, t: (t, 0))
    nt = T // tm

    def body(a_ref, b_ref, o_ref, acc_ref):
        t = pl.program_id(1)

        @pl.when(t == 0)
        def _():
            acc_ref[...] = jnp.zeros_like(acc_ref)

        acc_ref[...] += _dot_tn(a_ref[...].astype(BF16), b_ref[...].astype(BF16))

        @pl.when(t == nt - 1)
        def _():
            o_ref[...] = (scale * acc_ref[...]).astype(BF16).reshape(per, R, C)

    return _call(
        body, name=name, grid=(N_CHIPS // per, nt),
        in_specs=[a_spec, b_spec],
        out_specs=[pl.BlockSpec((per, R, C), lambda s, t: (s, 0, 0))],
        out_shape=[jax.ShapeDtypeStruct((N_CHIPS, R, C), BF16)],
        scratch=[pltpu.VMEM((per * R, C), F32)],
        sem=("parallel", "arbitrary"), operands=(a, b), rider=rider)[0]


def _grad_mix(mixed, dh, r, dret, pm, dpool, tk, name, rider=None):
    T, D = dh.shape
    Dq = D // N_CHIPS
    nt = T // tk

    def body(mx_ref, dh_ref, r_ref, dret_ref, pm_ref, dpool_ref, go_ref, gr_ref, gp_ref, ao_ref, ar_ref, ap_ref):
        t = pl.program_id(0)

        @pl.when(t == 0)
        def _():
            ao_ref[...] = jnp.zeros_like(ao_ref)
            ar_ref[...] = jnp.zeros_like(ar_ref)
            ap_ref[...] = jnp.zeros_like(ap_ref)

        ao_ref[...] += _dot_tn(mx_ref[...], dh_ref[...].astype(BF16))
        ar_ref[...] += _dot_tn(r_ref[...], dret_ref[...])
        ap_ref[...] += _dot_tn(pm_ref[...], dpool_ref[...])

        @pl.when(t == nt - 1)
        def _():
            go_ref[...] = ao_ref[...].astype(BF16).reshape(N_CHIPS, Dq, D)
            for s in range(N_CHIPS):
                gr_ref[s] = ar_ref[:, s * Dq:(s + 1) * Dq].astype(BF16)
                gp_ref[s] = ap_ref[:, s * Dq:(s + 1) * Dq].astype(BF16)

    row = pl.BlockSpec((tk, D), lambda t: (t, 0))
    half = pl.BlockSpec((tk, RET_WIDTH), lambda t: (t, 0))
    whole = lambda shape: pl.BlockSpec(shape, lambda t: (0, 0, 0))
    return _call(
        body, name=name, grid=(nt,),
        in_specs=[row, row, half, row, half, row],
        out_specs=[whole((N_CHIPS, Dq, D)), whole((N_CHIPS, RET_WIDTH, Dq)), whole((N_CHIPS, POOL_WIDTH, Dq))],
        out_shape=[jax.ShapeDtypeStruct((N_CHIPS, Dq, D), BF16),
                   jax.ShapeDtypeStruct((N_CHIPS, RET_WIDTH, Dq), BF16),
                   jax.ShapeDtypeStruct((N_CHIPS, POOL_WIDTH, Dq), BF16)],
        scratch=[pltpu.VMEM((D, D), F32), pltpu.VMEM((RET_WIDTH, D), F32), pltpu.VMEM((POOL_WIDTH, D), F32)],
        sem=("arbitrary",), operands=(mixed, dh, r, dret, pm, dpool), rider=rider)


def _mix_bwd_dx(dh, z, ret, pool, wout, wru, wpu, tm, name, rider=None):
    T, D = dh.shape
    Dq = D // N_CHIPS
    nb = D // RET_WIDTH

    def body(*refs):
        dh_ref = refs[0]
        gate_refs = refs[1:1 + 2 * nb]
        ret_ref, pool_ref, wout_ref, wru_ref, wpu_ref, dgab_ref, dret_ref, dpool_ref, dr_ref, dpm_ref = refs[1 + 2 * nb:]
        dmixed = _dot_nt(dh_ref[...].astype(BF16), wout_ref[...].reshape(D, D))
        ga, gb = _load_gates(gate_refs, nb)
        sa = _sigmoid(ga)
        sb = _sigmoid(gb)
        dgab_ref[:, :D] = (dmixed * ret_ref[...].astype(F32) * (sa * (1.0 - sa))).astype(BF16)
        dgab_ref[:, D:] = (dmixed * pool_ref[...].astype(F32) * (sb * (1.0 - sb))).astype(BF16)
        dret = (dmixed * sa).astype(BF16)
        dpool = (dmixed * sb).astype(BF16)
        dret_ref[...] = dret
        dpool_ref[...] = dpool
        dr = _dot_nt(dret[:, :Dq], wru_ref[0])
        dpm = _dot_nt(dpool[:, :Dq], wpu_ref[0])
        for s in range(1, N_CHIPS):
            dr += _dot_nt(dret[:, s * Dq:(s + 1) * Dq], wru_ref[s])
            dpm += _dot_nt(dpool[:, s * Dq:(s + 1) * Dq], wpu_ref[s])
        dr_ref[...] = dr
        dpm_ref[...] = dpm

    row = pl.BlockSpec((tm, D), lambda t: (t, 0))
    half = pl.BlockSpec((tm, RET_WIDTH), lambda t: (t, 0))
    up = pl.BlockSpec((N_CHIPS, RET_WIDTH, Dq), lambda t: (0, 0, 0))
    return _call(
        body, name=name, grid=(T // tm,),
        in_specs=[row] + _gate_specs(tm, D) + [row, row, pl.BlockSpec((N_CHIPS, Dq, D), lambda t: (0, 0, 0)), up, up],
        out_specs=[pl.BlockSpec((tm, 2 * D), lambda t: (t, 0)), row, row, half, half],
        out_shape=[jax.ShapeDtypeStruct((T, 2 * D), BF16), jax.ShapeDtypeStruct((T, D), BF16),
                   jax.ShapeDtypeStruct((T, D), BF16), jax.ShapeDtypeStruct((T, RET_WIDTH), F32),
                   jax.ShapeDtypeStruct((T, POOL_WIDTH), F32)],
        sem=("parallel",), operands=(dh, *([z] * (2 * nb)), ret, pool, wout, wru, wpu), rider=rider)


def _pool_bwd(z, dpm, maps, scale, layer, pad, name):
    T = z.shape[0]

    def body(zu, maps_ref, sc_ref, dpm_ref, du_ref, dmaps_ref, dsc_ref):
        g = pl.program_id(0)
        u = zu[...].astype(F32)
        pooled, div, valid = _pool_parts(u, g, T, pad)
        pb = pooled.astype(BF16)
        mb = maps_ref[...].astype(BF16)
        dp = dpm_ref[...]
        dsc_ref[...] = jnp.sum(dp * _dot(pb, mb), axis=0, keepdims=True)
        dyb = (dp * sc_ref[...]).astype(BF16)
        dmaps_ref[...] = _dot_tn(pb, dyb)
        dpooled = jnp.where(valid, _dot_nt(dyb, mb), 0.0)
        ahead = _select_group(_window_sums(dpooled / div, lambda k: T - k), g)
        du_ref[...] = jnp.where(valid, ahead - dpooled, 0.0).astype(BF16)

    blk = pl.BlockSpec((T, HEAD_DIM), lambda g: (0, g))
    return _call(
        body, name=name, grid=(POOL_GROUPS,),
        in_specs=_pool_specs(T, layer) + [blk],
        out_specs=[blk, pl.BlockSpec((None, HEAD_DIM, HEAD_DIM), lambda g: (g, 0, 0)),
                   pl.BlockSpec((1, HEAD_DIM), lambda g: (0, g))],
        out_shape=[jax.ShapeDtypeStruct((T, POOL_WIDTH), BF16),
                   jax.ShapeDtypeStruct((POOL_GROUPS, HEAD_DIM, HEAD_DIM), F32),
                   jax.ShapeDtypeStruct((1, POOL_WIDTH), F32)],
        sem=("parallel",), operands=(z, maps, scale, dpm))


def _ret_bwd_local(z, o_pre, s_all, dr, consts, cg, name):
    T = z.shape[0]
    N = T // CHUNK
    ng = N // cg
    tg = cg * CHUNK
    cosf, sinf, intra, _, qdec, _ = consts
    fwd = lambda g: g

    def body(zq, zk, zv, zg, o_ref, s_ref, dr_ref, cos_ref, sin_ref, m_ref, qd_ref,
             dq_ref, dg_ref, dk_ref, dv_ref, ds_ref):
        cosv = cos_ref[...]
        sinv = sin_ref[...]
        scale = HEAD_DIM ** -0.5
        q3 = (_rot(zq[...].astype(F32), cosv, sinv) * scale).reshape(cg, CHUNK, HEAD_DIM)
        k3 = _rot(zk[...].astype(F32), cosv, sinv).reshape(cg, CHUNK, HEAD_DIM)
        qb = q3.astype(BF16)
        kb = k3.astype(BF16)
        vb = zv[...].reshape(cg, CHUNK, HEAD_DIM).astype(BF16)
        mask = m_ref[...][None]
        sb = (_ein("ncd,nmd->ncm", qb, kb) * mask).astype(BF16)
        qdv = qd_ref[...][None]
        qdb = (q3 * qdv).astype(BF16)

        out = o_ref[...]
        xc = out - jnp.mean(out, axis=-1, keepdims=True)
        rstd = lax.rsqrt(jnp.mean(xc * xc, axis=-1, keepdims=True) + EPS)
        rn = xc * rstd
        g = zg[...].astype(F32)
        sg = _sigmoid(g)
        drv = dr_ref[...]
        dg_ref[...] = (drv * rn * (sg * (1.0 + g * (1.0 - sg)))).astype(BF16)
        drn = drv * (g * sg)
        dout = rstd * (drn - jnp.mean(drn, axis=-1, keepdims=True)
                       - rn * jnp.mean(drn * rn, axis=-1, keepdims=True))
        dob = dout.reshape(cg, CHUNK, HEAD_DIM).astype(BF16)

        dsb = (_ein("ncd,nmd->ncm", dob, vb) * mask).astype(BF16)
        dv_ref[...] = _ein("ncm,ncd->nmd", sb, dob).reshape(tg, HEAD_DIM)
        dk_ref[...] = _ein("ncm,ncd->nmd", dsb, qb).reshape(tg, HEAD_DIM)
        dq3 = _ein("ncm,nmd->ncd", dsb, kb) + _ein("nce,nde->ncd", dob, s_ref[...].astype(BF16)) * qdv
        dq_ref[...] = _rot_t(dq3.reshape(tg, HEAD_DIM) * scale, cosv, sinv).astype(BF16)
        ds_ref[...] = _ein("ncd,nce->nde", qdb, dob)

    tab = pl.BlockSpec((tg, HEAD_DIM), lambda h, g: (g, 0))
    per_head = pl.BlockSpec((None, CHUNK, HEAD_DIM), lambda h, g: (h, 0, 0))
    head_blk = pl.BlockSpec((tg, HEAD_DIM), lambda h, g: (g, h))
    state_blk = pl.BlockSpec((None, cg, HEAD_DIM, HEAD_DIM), lambda h, g: (h, g, 0, 0))
    return _call(
        body, name=name, grid=(RET_HEADS, ng),
        in_specs=[_head_specs(tg, i, fwd) for i in range(4)]
        + [head_blk, state_blk, head_blk, tab, tab, per_head, per_head],
        out_specs=[head_blk, head_blk, head_blk, head_blk, state_blk],
        out_shape=[jax.ShapeDtypeStruct((T, RET_WIDTH), BF16), jax.ShapeDtypeStruct((T, RET_WIDTH), BF16),
                   jax.ShapeDtypeStruct((T, RET_WIDTH), F32), jax.ShapeDtypeStruct((T, RET_WIDTH), F32),
                   jax.ShapeDtypeStruct((RET_HEADS, N, HEAD_DIM, HEAD_DIM), F32)],
        sem=("parallel", "parallel"), operands=(z, z, z, z, o_pre, s_all, dr, cosf, sinf, intra, qdec))


def _ret_bwd_state(z, dkp, dvp, ds, consts, cg, name):
    T = z.shape[0]
    N = T // CHUNK
    ng = N // cg
    tg = cg * CHUNK
    cosf, sinf, _, kdec, _, cdb = consts
    rev = lambda g: ng - 1 - g

    def body(zk, zv, dkp_ref, dvp_ref, ds_ref, cos_ref, sin_ref, kd_ref, cd_ref, dk_ref, dv_ref, gs_ref, dkv_ref):
        @pl.when(pl.program_id(1) == 0)
        def _():
            gs_ref[...] = jnp.zeros_like(gs_ref)

        cosv = cos_ref[...]
        sinv = sin_ref[...]
        cd = cd_ref[0:1, :]
        grad = gs_ref[...]
        for n in reversed(range(cg)):
            dkv_ref[n] = grad
            grad = ds_ref[n] + cd * grad
        gs_ref[...] = grad
        dkvb = dkv_ref[...].astype(BF16)
        kdv = kd_ref[...][None]
        k3 = _rot(zk[...].astype(F32), cosv, sinv).reshape(cg, CHUNK, HEAD_DIM)
        vb = zv[...].reshape(cg, CHUNK, HEAD_DIM).astype(BF16)
        dk3 = _ein("nce,nde->ncd", vb, dkvb) * kdv
        dv3 = _ein("ncd,nde->nce", (k3 * kdv).astype(BF16), dkvb)
        dk_ref[...] = _rot_t(dkp_ref[...] + dk3.reshape(tg, HEAD_DIM), cosv, sinv).astype(BF16)
        dv_ref[...] = (dvp_ref[...] + dv3.reshape(tg, HEAD_DIM)).astype(BF16)

    tab = pl.BlockSpec((tg, HEAD_DIM), lambda h, g: (rev(g), 0))
    head_blk = pl.BlockSpec((tg, HEAD_DIM), lambda h, g: (rev(g), h))
    return _call(
        body, name=name, grid=(RET_HEADS, ng),
        in_specs=[_head_specs(tg, 1, rev), _head_specs(tg, 2, rev), head_blk, head_blk,
                  pl.BlockSpec((None, cg, HEAD_DIM, HEAD_DIM), lambda h, g: (h, rev(g), 0, 0)),
                  tab, tab,
                  pl.BlockSpec((None, CHUNK, HEAD_DIM), lambda h, g: (h, 0, 0)),
                  pl.BlockSpec((None, 8, HEAD_DIM), lambda h, g: (h, 0, 0))],
        out_specs=[head_blk, head_blk],
        out_shape=[jax.ShapeDtypeStruct((T, RET_WIDTH), BF16)] * 2,
        scratch=[pltpu.VMEM((HEAD_DIM, HEAD_DIM), F32), pltpu.VMEM((cg, HEAD_DIM, HEAD_DIM), F32)],
        sem=("parallel", "arbitrary"), operands=(z, z, dkp, dvp, ds, cosf, sinf, kdec, cdb))


def _inproj_bwd_dx(dz, win, h, gain, dh_in, layer, tm, pad, name, rider=None):
    T, D = h.shape
    Ns = win.shape[-1]

    def body(dz_ref, w_ref, h_ref, g_ref, dhi_ref, dh_ref, dgain_ref, db_ref):
        t = pl.program_id(0)
        s = pl.program_id(1)

        @pl.when((t == 0) & (s == 0))
        def _():
            dgain_ref[...] = jnp.zeros_like(dgain_ref)

        @pl.when(s == 0)
        def _():
            db_ref[...] = jnp.zeros_like(db_ref)

        db_ref[...] += _dot_nt(dz_ref[...], w_ref[...])

        @pl.when(s == N_CHIPS - 1)
        def _():
            dx, dgain = _rms_bwd(h_ref[...], g_ref[...], db_ref[...])
            dgain_ref[...] += dgain
            dh_ref[...] = jnp.where(_row_mask(t, tm, pad, (tm, D)), dhi_ref[...] + dx, 0.0)

    row = pl.BlockSpec((tm, D), lambda t, s: (t, 0))
    return _call(
        body, name=name, grid=(T // tm, N_CHIPS),
        in_specs=[pl.BlockSpec((tm, Ns), lambda t, s: (t, s)),
                  pl.BlockSpec((None, D, Ns), lambda t, s: (s, 0, 0)),
                  row, pl.BlockSpec((None, 1, D), lambda t, s: (layer, 0, 0)), row],
        out_specs=[row, pl.BlockSpec((1, D), lambda t, s: (0, 0))],
        out_shape=[jax.ShapeDtypeStruct((T, D), F32), jax.ShapeDtypeStruct((1, D), F32)],
        scratch=[pltpu.VMEM((tm, D), F32)],
        sem=("arbitrary", "arbitrary"), operands=(dz, win, h, gain, dh_in), rider=rider)


def _sum_pair(gs, rs, c_idx, name):
    n = len(gs)

    def body(c_ref, *refs):
        for g_ref, r_ref, o_ref in zip(refs[:n], refs[n:2 * n], refs[2 * n:]):
            o_ref[...] = (g_ref[...].astype(F32) + r_ref[...].astype(F32)).astype(BF16)

    halves = [pl.BlockSpec((None,) + r.shape[1:], lambda s, c_ref: (s, 0, 0)) for r in rs]
    return pl.pallas_call(
        body,
        name=name,
        grid_spec=pltpu.PrefetchScalarGridSpec(
            num_scalar_prefetch=1,
            grid=(N_CHIPS,),
            in_specs=[pl.BlockSpec((None,) + r.shape[1:], lambda s, c_ref: (s, c_ref[0], 0)) for r in rs] + halves,
            out_specs=halves,
        ),
        out_shape=[jax.ShapeDtypeStruct(r.shape, BF16) for r in rs],
        compiler_params=_params(("parallel",)),
    )(c_idx, *gs, *rs)


def _sum_chips(ps, rs, pos, name):
    n = len(ps)
    quarters = 4

    def body(pos_ref, *refs):
        chip = pos_ref[0]
        for p_ref, r_ref, o_ref in zip(refs[:n], refs[n:2 * n], refs[2 * n:]):
            own = p_ref[...].astype(F32)
            terms = [jnp.where(chip == k, own, r_ref[k].astype(F32)) for k in range(N_CHIPS)]
            o_ref[...] = ((terms[0] + terms[1]) + terms[2]) + terms[3]

    def rows(r):
        assert r.shape[1] % (quarters * BF16_ROWS) == 0, r.shape
        return r.shape[1] // quarters

    return pl.pallas_call(
        body,
        name=name,
        grid_spec=pltpu.PrefetchScalarGridSpec(
            num_scalar_prefetch=1,
            grid=(quarters,),
            in_specs=[pl.BlockSpec((None, rows(r), r.shape[2]), lambda q, pos_ref: (pos_ref[0], q, 0)) for r in rs]
            + [pl.BlockSpec((N_CHIPS, rows(r), r.shape[2]), lambda q, pos_ref: (0, q, 0)) for r in rs],
            out_specs=[pl.BlockSpec((rows(r), r.shape[2]), lambda q, pos_ref: (pos_ref[1] * quarters + q, 0))
                       for r in rs],
        ),
        out_shape=[jax.ShapeDtypeStruct((2 * r.shape[1], r.shape[2]), F32) for r in rs],
        compiler_params=_params(("arbitrary",)),
    )(pos, *ps, *rs)


def _small_all_reduce(p):
    rows, width = p.shape

    def body(p_ref, o_ref, sib_ref, slot_ref, ssem, rsem):
        x, y, c, chip, others = _mesh_pos()
        pair = _remote(p_ref, sib_ref, ssem.at[0], rsem.at[0], (x, y, 1 - c))
        pair.start()
        pair.wait()
        slot_ref[chip] = p_ref[...] + sib_ref[...]
        sends = []
        for j, (ox, oy) in enumerate(others):
            cp = _remote(slot_ref.at[chip], slot_ref.at[chip], ssem.at[1 + j], rsem.at[1 + j], (ox, oy, c))
            cp.start()
            sends.append(cp)
        for j, (ox, oy) in enumerate(others):
            slot = slot_ref.at[2 * ox + oy]
            _remote(slot, slot, ssem.at[1 + j], rsem.at[1 + j], (ox, oy, c)).wait_recv()
        for cp in sends:
            cp.wait_send()
        o_ref[...] = ((slot_ref[0] + slot_ref[1]) + slot_ref[2]) + slot_ref[3]

    vmem = pl.BlockSpec(memory_space=pltpu.VMEM)
    return pl.pallas_call(
        body,
        name="small_grads_all_reduce",
        in_specs=[vmem],
        out_specs=vmem,
        out_shape=jax.ShapeDtypeStruct(p.shape, F32),
        scratch_shapes=[pltpu.VMEM((rows, width), F32), pltpu.VMEM((N_CHIPS, rows, width), F32),
                        pltpu.SemaphoreType.DMA((4,)), pltpu.SemaphoreType.DMA((4,))],
    )(p)


def _adamw(gs, w, m, v, name, rider=None):
    L, R, C = w.shape
    Ct = gs[0].shape[1]
    tr = _pick_tile(R, 256, 8)

    def body(*refs):
        g_refs = refs[:L]
        w_ref, m_ref, v_ref, go_ref, d_ref, mo_ref, vo_ref = refs[L:]
        layer = pl.program_id(0)
        grad = g_refs[L - 1][...]
        for i in range(L - 2, -1, -1):
            grad = jnp.where(layer == i, g_refs[i][...], grad)
        if Ct != C:
            grad = grad[:, :C]
        m_new = ADAM_B1 * m_ref[...] + (1.0 - ADAM_B1) * grad
        v_new = ADAM_B2 * v_ref[...] + (1.0 - ADAM_B2) * jnp.square(grad)
        m_hat = m_new / (1.0 - ADAM_B1 ** ADAM_STEP)
        v_hat = v_new / (1.0 - ADAM_B2 ** ADAM_STEP)
        go_ref[...] = grad
        d_ref[...] = -ADAM_LR * (m_hat / (jnp.sqrt(v_hat) + ADAM_EPS) + ADAM_WD * w_ref[...])
        mo_ref[...] = m_new
        vo_ref[...] = v_new

    g_specs = [pl.BlockSpec((tr, Ct), functools.partial(lambda l, r, i: (jnp.where(l == i, r, 0), 0), i=i))
               for i in range(L)]
    blk = pl.BlockSpec((None, tr, C), lambda l, r: (l, r, 0))
    return _call(
        body, name=name, grid=(L, R // tr),
        in_specs=g_specs + [blk, blk, blk],
        out_specs=[blk] * 4,
        out_shape=[jax.ShapeDtypeStruct((L, R, C), F32)] * 4,
        sem=("arbitrary", "arbitrary"), operands=(*gs, w, m, v), rider=rider)


_FFN1 = ("ffn1_gate", "ffn1_up", "ffn1_down")
_FFN2 = ("ffn2_gate", "ffn2_up", "ffn2_down")
_MIXW = ("w_ret_up", "w_pool_up", "w_out")
_BIG = _FFN1 + ("w_in",) + _MIXW + _FFN2
_TRANSPOSED = ("ffn1_gate", "ffn1_up", "ffn2_gate", "ffn2_up")
_SMALL = ("ffn1_norm", "mix_norm", "ffn2_norm", "final_norm", "pool_scale", "pool_maps")
_ORDER = ("meta", "ffn1_norm", "ffn1_gate", "ffn1_up", "ffn1_down", "mix_norm", "w_in", "pool_maps",
          "pool_scale", "w_ret_up", "w_pool_up", "w_out", "ffn2_norm", "ffn2_gate", "ffn2_up", "ffn2_down",
          "final_norm")


def _transport(a):
    n, r, c = a.shape
    out = a.astype(BF16)
    if c % LANES:
        out = jnp.concatenate([out, jnp.zeros((n, r, _round_up(c, LANES) - c), BF16)], axis=2)
    if r % LANES:
        out = jnp.concatenate([out, jnp.zeros((n, _round_up(r, LANES) - r, out.shape[2]), BF16)], axis=1)
    return out


def _pack_rows(parts, width):
    rows = [p.reshape(-1, width) for p in parts]
    total = sum(r.shape[0] for r in rows)
    fill = _round_up(total, 8) - total
    if fill:
        rows.append(jnp.zeros((fill, width), F32))
    return jnp.concatenate(rows, axis=0)


def _unpack_rows(packed, shapes, width):
    out, at = [], 0
    for shp in shapes:
        n = math.prod(shp) // width
        out.append(packed[at:at + n].reshape(shp))
        at += n
    return out


class _Weights:
    def __init__(self, shards):
        self.shards = shards
        self.full = {}

    def rider(self, keys):
        r = _gather_rider([(self.shards[n], i) for n, i in keys])
        r.keys = keys
        return r

    def take(self, rider):
        for key, arr in zip(rider.keys, rider.results):
            self.full[key] = arr

    def __call__(self, name, layer):
        return self.full[(name, layer)]


def _local_step(x, meta_full, tgt, w, wts, pad, tm, cg, reducer):
    D = x.shape[1]
    T = pad + N_META + x.shape[0]
    L = w["ffn1_norm"].shape[0]
    pool_maps = w["pool_maps"]
    gains = {n: w[n].reshape(L, 1, D) for n in ("ffn1_norm", "mix_norm", "ffn2_norm")}
    scale3 = w["pool_scale"].reshape(L, 1, POOL_WIDTH)
    consts = _ret_consts(T, pad)
    tl = _pick_tile(T, 2 * tm, BF16_ROWS)
    def gather(keys):
        return wts.rider(keys) if keys and keys[0] not in wts.full else None

    def done(rider):
        if rider is not None:
            wts.take(rider)

    h = jnp.concatenate([jnp.zeros((pad, D), F32), meta_full, x], axis=0)
    saved = []
    for i in range(L):
        s = {"h0": h}
        rd = gather([("w_in", i)] + [(n, i) for n in _MIXW])
        h, s["a1"], s["g1"], s["u1"], s["act1"] = _ffn_fwd(
            h, gains["ffn1_norm"], wts("ffn1_gate", i), wts("ffn1_up", i), wts("ffn1_down", i), i, tl,
            f"ffn1_fwd_{i}", rd)
        done(rd)
        s["h1"] = h
        rd = gather([("ffn2_gate", i), ("ffn2_up", i)])
        s["z"], s["b"] = _inproj_fwd(h, gains["mix_norm"], wts("w_in", i), i, tl, f"inproj_fwd_{i}", rd)
        done(rd)
        s["r"], s["o_pre"], s["s_all"] = _ret_fwd(s["z"], consts, cg, f"retention_fwd_{i}")
        s["pm"] = _pool_fwd(s["z"], pool_maps, scale3, i, pad, f"pool_fwd_{i}")
        rd = gather([("ffn2_down", i)])
        h, s["mixed"], s["ret"], s["pool"] = _mix_fwd(
            h, s["r"], s["pm"], s["z"], wts("w_ret_up", i), wts("w_pool_up", i), wts("w_out", i), tm,
            f"mix_fwd_{i}", rd)
        done(rd)
        s["h2"] = h
        rd = gather([(n, i + 1) for n in _FFN1]) if i + 1 < L else None
        h, s["a2"], s["g2"], s["u2"], s["act2"] = _ffn_fwd(
            h, gains["ffn2_norm"], wts("ffn2_gate", i), wts("ffn2_up", i), wts("ffn2_down", i), i, tl,
            f"ffn2_fwd_{i}", rd)
        done(rd)
        saved.append(s)

    dh, loss_acc, d_final = _final_loss(h, w["final_norm"].reshape(1, D), tgt, "final_norm_loss")

    small = {n: [None] * L for n in ("ffn1_norm", "mix_norm", "ffn2_norm", "pool_scale", "pool_maps")}

    carry = {"ffn_act": 1.0, "ffn_in": 2.2, "mix_bwd": 1.0, "inproj_bwd": 1.5, "w_in": 1.0}

    tk = _pick_tile(T, 1408, LANES)

    def grad(n, a, b, i, mode):
        rd = reducer.rider(carry.get(n, 1.0 if i == 0 and n.startswith("ffn") else 0.5))
        reducer.add(n, i, _grad_tn(a, b, mode, 1.0, tk, f"grad_{n}_{i}", rd))
        reducer.done(rd)

    def ffn_bwd(which, dy, h_in, g, u, i):
        rd = reducer.rider(carry["ffn_act"])
        dg, du, dyh = _ffn_bwd_act(dy, g, u, wts(f"{which}_down", i), tl, f"{which}_bwd_act_{i}", rd)
        reducer.done(rd)
        rd = reducer.rider(carry["ffn_in"])
        dh_in, dgain = _ffn_bwd_in(dy, h_in, gains[f"{which}_norm"], dg, du, wts(f"{which}_gate", i),
                                   wts(f"{which}_up", i), i, tl, pad, f"{which}_bwd_in_{i}", rd)
        reducer.done(rd)
        return dh_in, dg, du, dgain, dyh

    for i in reversed(range(L)):
        s = saved[i]
        dh, dg, du, small["ffn2_norm"][i], dyh = ffn_bwd("ffn2", dh, s["h2"], s["g2"], s["u2"], i)
        grad("ffn2_gate", dg, s["a2"], i, "row")
        grad("ffn2_up", du, s["a2"], i, "row")
        grad("ffn2_down", s["act2"], dyh, i, "row")
        reducer.stage(f"ffn2_{i}")
        rd = reducer.rider(carry["mix_bwd"])
        dgab, dret, dpool, dr, dpm = _mix_bwd_dx(
            dh, s["z"], s["ret"], s["pool"], wts("w_out", i), wts("w_ret_up", i), wts("w_pool_up", i), tm,
            f"mix_bwd_{i}", rd)
        reducer.done(rd)
        rd = reducer.rider(0.5)
        g_out, g_ru, g_pu = _grad_mix(s["mixed"], dh, s["r"], dret, s["pm"], dpool, _pick_tile(T, 704, LANES),
                                      f"grad_mix_{i}", rd)
        reducer.done(rd)
        for n, g_n in (("w_out", g_out), ("w_ret_up", g_ru), ("w_pool_up", g_pu)):
            reducer.add(n, i, g_n)
        du_pool, small["pool_maps"][i], small["pool_scale"][i] = _pool_bwd(
            s["z"], dpm, pool_maps, scale3, i, pad, f"pool_bwd_{i}")
        dq, dgr, dkp, dvp, ds = _ret_bwd_local(s["z"], s["o_pre"], s["s_all"], dr, consts, cg, f"retention_bwd_{i}")
        dk, dv = _ret_bwd_state(s["z"], dkp, dvp, ds, consts, cg, f"retention_bwd_state_{i}")
        dz = jnp.concatenate([dq, dk, dv, dgr, du_pool, dgab], axis=1)
        dh2 = dh
        rd = reducer.rider(carry["inproj_bwd"])
        dh, small["mix_norm"][i] = _inproj_bwd_dx(
            dz, wts("w_in", i), s["h1"], gains["mix_norm"], dh2, i, tl, pad, f"inproj_bwd_{i}", rd)
        reducer.done(rd)
        grad("w_in", s["b"], dz, i, "col")
        reducer.stage(f"mid{i}")
        dh, dg, du, small["ffn1_norm"][i], dyh = ffn_bwd("ffn1", dh, s["h0"], s["g1"], s["u1"], i)
        grad("ffn1_gate", dg, s["a1"], i, "row")
        if i == 0:
            reducer.stage("gate0")
        grad("ffn1_up", du, s["a1"], i, "row")
        if i == 0:
            reducer.stage("up0")
        grad("ffn1_down", s["act1"], dyh, i, "row")
        reducer.stage(f"end{i}")

    return loss_acc, dh, small, d_final


class _Reducer:
    def __init__(self, unit):
        self.c_idx = lax.axis_index("c").astype(jnp.int32).reshape(1)
        chip = 2 * lax.axis_index("x") + lax.axis_index("y")
        self.pos = jnp.stack([chip, lax.axis_index("c")]).astype(jnp.int32)
        self.pending, self.stages, self.queue, self.halves, self.whole = [], [], [], {}, {}
        self.unit = unit
        self.calls = 0

    def add(self, name, layer, g):
        self.pending.append(((name, layer), g))

    def stage(self, tag):
        if self.pending:
            self.stages.append((tag, self.pending))
            self.pending = []

    def _pair_rider(self):
        if not self.stages:
            return None
        tag, items = self.stages.pop(0)
        rd = _pair_exchange_rider([g for _, g in items])
        rd.tag, rd.keys = tag, [k for k, _ in items]
        return rd

    def _chip_rider(self, units):
        take, size = [], 0
        while self.queue and (units is None or size + self.queue[0][1].size <= units * self.unit):
            take.append(self.queue.pop(0))
            size += take[-1][1].size
        if not take:
            return None
        rd = _chip_exchange_rider([p for _, p in take])
        rd.keys = [k for k, _ in take]
        return rd

    def _gather_rider(self):
        keys = [k for k in self.halves if k not in self.whole]
        if not keys:
            return None
        rd = _pair_gather_rider([self.halves[k] for k in keys])
        rd.keys = keys
        return rd

    def rider(self, units):
        self.riding = (self._pair_rider(), self._chip_rider(units), self._gather_rider())
        return _join(self.riding)

    def done(self, rd):
        if rd is None:
            return
        _split_results(rd)
        pair, chips, gather = self.riding
        if len([r for r in self.riding if r is not None]) == 1:
            (pair or chips or gather).results = rd.results
        self.calls += 1
        if gather is not None:
            self.whole.update(zip(gather.keys, gather.results))
        if pair is not None:
            sums = _sum_pair(pair.ins, pair.results, self.c_idx, f"sum_pair_{pair.tag}")
            self.queue += list(zip(pair.keys, sums))
        if chips is not None:
            sums = _sum_chips(chips.ins, chips.results, self.pos, f"sum_chips_{self.calls}")
            self.halves.update(zip(chips.keys, sums))

    def busy(self):
        assert not self.pending
        return bool(self.stages or self.queue or len(self.whole) < len(self.halves))

    def flush(self):
        self.riding = (self._pair_rider(), self._chip_rider(None), self._gather_rider())
        rd = _join(self.riding)
        _run_rider(rd, f"grads_exchange_tail_{self.calls}")
        self.done(rd)


def _update(loss, grad_x, d_meta_rows, reducer, small, d_final, w, mom, var):
    meta = w["meta"]
    D = w["final_norm"].shape[0]
    L = w["ffn1_norm"].shape[0]
    Dq = D // N_CHIPS

    out = {}

    def adamw(n, rider=None):
        gs = [reducer.whole[(n, i)] for i in range(L)]
        if n in _TRANSPOSED:
            res = _adamw(gs, *(jnp.swapaxes(t[n], 1, 2) for t in (w, mom, var)), f"adamw_{n}", rider)
            out[n] = [jnp.swapaxes(r, 1, 2) for r in res]
        else:
            out[n] = _adamw(gs, w[n], mom[n], var[n], f"adamw_{n}", rider)

    todo = sorted(_BIG, key=lambda n: -math.prod(w[n].shape))
    while reducer.busy():
        ready = [n for n in todo if all((n, i) in reducer.whole for i in range(L))]
        if ready:
            rd = reducer.rider(1.0)
            adamw(ready[0], rd)
            reducer.done(rd)
            todo.remove(ready[0])
        else:
            reducer.flush()
    for n in todo:
        adamw(n)

    small_parts = [jnp.concatenate(small[n], axis=0) for n in ("ffn1_norm", "mix_norm", "ffn2_norm")]
    small_parts += [d_final, jnp.concatenate(small["pool_scale"], axis=0), jnp.concatenate(small["pool_maps"], axis=0)]
    reduced = _small_all_reduce(_pack_rows(small_parts + [d_meta_rows], D))
    small_shapes = [w[n].shape for n in _SMALL]
    small_rows = sum(math.prod(shp) for shp in small_shapes) // D
    chip = 2 * lax.axis_index("x") + lax.axis_index("y")
    d_meta = lax.dynamic_slice_in_dim(reduced[small_rows:small_rows + N_META], chip * Dq, Dq, axis=1)
    names = _SMALL + ("meta",)
    packed_g = _pack_rows([reduced[:small_rows], d_meta], D)
    packed = [_pack_rows([t[n] for n in names], D) for t in (w, mom, var)]
    res = _adamw([packed_g], packed[0][None], packed[1][None], packed[2][None], "adamw_small")
    shapes = small_shapes + [meta.shape]
    unpacked = [_unpack_rows(r[0], shapes, D) for r in res]
    for k, n in enumerate(names):
        out[n] = tuple(u[k] for u in unpacked)

    return (loss, grad_x) + tuple(out[n][j] for j in range(4) for n in _ORDER)


def kernel(x, meta, ffn1_norm, ffn1_gate, ffn1_up, ffn1_down, mix_norm, w_in, pool_maps, pool_scale, w_ret_up, w_pool_up, w_out, ffn2_norm, ffn2_gate, ffn2_up, ffn2_down, final_norm, loss_target, m_meta, m_ffn1_norm, m_ffn1_gate, m_ffn1_up, m_ffn1_down, m_mix_norm, m_w_in, m_pool_maps, m_pool_scale, m_w_ret_up, m_w_pool_up, m_w_out, m_ffn2_norm, m_ffn2_gate, m_ffn2_up, m_ffn2_down, m_final_norm, v_meta, v_ffn1_norm, v_ffn1_gate, v_ffn1_up, v_ffn1_down, v_mix_norm, v_w_in, v_pool_maps, v_pool_scale, v_w_ret_up, v_w_pool_up, v_w_out, v_ffn2_norm, v_ffn2_gate, v_ffn2_up, v_ffn2_down, v_final_norm):
    args = dict(locals())
    w = {n: args[n] for n in _ORDER}
    mom = {n: args["m_" + n] for n in _ORDER}
    var = {n: args["v_" + n] for n in _ORDER}

    assert x.shape[0] == 1, "one batch element per device"
    seq, D = x.shape[1], x.shape[2]
    assert seq % CHUNK == 0 and D % RET_WIDTH == 0 and (2 * POOL_WIDTH) % D == 0
    pad = (-(seq + N_META)) % CHUNK
    T = seq + N_META + pad
    tm = _pick_tile(T, 528, BF16_ROWS)
    cg = _pick_tile(T // CHUNK, 11, 1)

    shards = {n: _transport(w[n]) for n in _BIG}
    shards["meta"] = meta[None]
    wts = _Weights(shards)
    head = wts.rider([(n, 0) for n in _FFN1] + [("meta", 0)])
    _run_rider(head, "weights_gather_head")
    wts.take(head)
    meta_full = jnp.transpose(wts("meta", 0), (1, 0, 2)).reshape(N_META, D)

    reducer = _Reducer(unit=2 * shards["ffn1_gate"][0].size)
    loss_acc, dh, small, d_final = _local_step(x[0], meta_full, loss_target[0], w, wts, pad, tm, cg, reducer)
    loss = lax.psum(loss_acc[0, 0], ("x", "y", "c"))
    grad_x = dh[pad + N_META:][None]
    return _update(loss, grad_x, dh[pad:pad + N_META], reducer, small, d_final, w, mom, var)
```

```python
import functools
import math

import jax
import jax.numpy as jnp
from jax import lax
from jax.experimental import pallas as pl
from jax.experimental.pallas import tpu as pltpu

F32 = jnp.float32
BF16 = jnp.bfloat16

N_META = 16
RET_HEADS = 4
HEAD_DIM = 128
RET_WIDTH = RET_HEADS * HEAD_DIM
POOL_WINDOWS = (2, 4, 8, 16)
POOL_GROUPS = len(POOL_WINDOWS)
POOL_WIDTH = POOL_GROUPS * HEAD_DIM
CHUNK = 128
ROPE_BASE = 10000.0
EPS = 1e-6
ADAM_LR = 0.001
ADAM_B1 = 0.9
ADAM_B2 = 0.999
ADAM_EPS = 1e-08
ADAM_WD = 0.01
ADAM_STEP = 10

N_CHIPS = 4
LANES = 128
BF16_ROWS = 16
V7X_VMEM_LIMIT = 52 * 1024 * 1024
MESH = pl.DeviceIdType.MESH
ANY = pl.BlockSpec(memory_space=pl.ANY)


def _round_up(n, m):
    return -(-n // m) * m


def _pick_tile(n, target, mult):
    best = None
    for d in range(mult, min(n, target) + 1, mult):
        if n % d == 0:
            best = d
    assert best is not None, (n, target, mult)
    return best


def _params(sem=None):
    return pltpu.CompilerParams(dimension_semantics=sem, vmem_limit_bytes=V7X_VMEM_LIMIT)


def _dot(a, b):
    return jnp.dot(a, b, preferred_element_type=F32)


def _dot_nt(a, b):
    return lax.dot_general(a, b, (((1,), (1,)), ((), ())), preferred_element_type=F32)


def _dot_tn(a, b):
    return lax.dot_general(a, b, (((0,), (0,)), ((), ())), preferred_element_type=F32)


def _ein(spec, a, b):
    return jnp.einsum(spec, a, b, preferred_element_type=F32)


def _sigmoid(x):
    return jax.nn.sigmoid(x)


def _rms_fwd(x, gain):
    r = lax.rsqrt(jnp.mean(x * x, axis=-1, keepdims=True) + EPS)
    return x * r * gain


def _rms_bwd(x, gain, da):
    r = lax.rsqrt(jnp.mean(x * x, axis=-1, keepdims=True) + EPS)
    xh = x * r
    dgain = jnp.sum(da * xh, axis=0, keepdims=True)
    dxh = da * gain
    dx = r * (dxh - xh * jnp.mean(dxh * xh, axis=-1, keepdims=True))
    return dx, dgain


def _row_mask(t, tm, pad, shape):
    rows = t * tm + lax.broadcasted_iota(jnp.int32, shape, 0)
    return rows >= pad


def _mesh_pos():
    x, y, c = lax.axis_index("x"), lax.axis_index("y"), lax.axis_index("c")
    others = [(1 - x, y), (x, 1 - y), (1 - x, 1 - y)]
    return x, y, c, 2 * x + y, others


def _half_rows(c, rh):
    return pl.ds(pl.multiple_of(c * rh, rh), rh)


def _remote(src, dst, ssem, rsem, dev):
    return pltpu.make_async_remote_copy(src_ref=src, dst_ref=dst, send_sem=ssem, recv_sem=rsem,
                                        device_id=dev, device_id_type=MESH)


class _Rider:
    def __init__(self, ins, out_shapes, n_sem, start, finish, in_place=False):
        self.ins, self.out_shapes, self.n_sem, self.start, self.finish = ins, out_shapes, n_sem, start, finish
        self.in_place = [in_place] * len(ins)
        self.results = None

    def aliases(self, first_in, first_out):
        return {first_in + i: first_out + i for i, same in enumerate(self.in_place) if same}


class _SemWindow:
    def __init__(self, ref, base):
        self.ref, self.base = ref, base

    @property
    def at(self):
        return self

    def __getitem__(self, k):
        return self.ref.at[self.base + k]


def _join(riders):
    riders = [r for r in riders if r is not None]
    if len(riders) <= 1:
        return riders[0] if riders else None

    def run(which):
        def go(ins, outs, ssem, rsem):
            at, sem = 0, 0
            for r in riders:
                n = len(r.ins)
                getattr(r, which)(ins[at:at + n], outs[at:at + n], _SemWindow(ssem, sem), _SemWindow(rsem, sem))
                at, sem = at + n, sem + r.n_sem
        return go

    joined = _Rider(sum([list(r.ins) for r in riders], []), sum([list(r.out_shapes) for r in riders], []),
                    sum(r.n_sem for r in riders), run("start"), run("finish"))
    joined.in_place = sum([r.in_place for r in riders], [])
    joined.parts = riders
    return joined


def _split_results(rider):
    at = 0
    for r in getattr(rider, "parts", []):
        r.results = rider.results[at:at + len(r.ins)]
        at += len(r.ins)


def _gather_rider(pieces):
    per = 7
    layers = [layer for _, layer in pieces]

    def first_copies(ins, outs, ssem, rsem):
        x, y, c, chip, others = _mesh_pos()
        copies = []
        for i, layer in enumerate(layers):
            mine = _half_rows(c, ins[i].shape[1] // 2)
            for j, (ox, oy) in enumerate(others):
                copies.append(_remote(ins[i].at[layer, mine, :], outs[i].at[chip, mine, :],
                                      ssem.at[per * i + j], rsem.at[per * i + j], (ox, oy, c)))
            copies.append(_remote(ins[i].at[layer], outs[i].at[chip],
                                  ssem.at[per * i + 6], rsem.at[per * i + 6], (x, y, 1 - c)))
        return copies

    def start(ins, outs, ssem, rsem):
        for cp in first_copies(ins, outs, ssem, rsem):
            cp.start()

    def finish(ins, outs, ssem, rsem):
        x, y, c, chip, others = _mesh_pos()
        sibling = (x, y, 1 - c)
        forwards = []
        for i in range(len(layers)):
            mine = _half_rows(c, ins[i].shape[1] // 2)
            for j, (ox, oy) in enumerate(others):
                rows = outs[i].at[2 * ox + oy, mine, :]
                _remote(rows, rows, ssem.at[per * i + j], rsem.at[per * i + j], (ox, oy, c)).wait_recv()
                fwd = _remote(rows, rows, ssem.at[per * i + 3 + j], rsem.at[per * i + 3 + j], sibling)
                fwd.start()
                forwards.append(fwd)
        for i in range(len(layers)):
            theirs = _half_rows(1 - c, ins[i].shape[1] // 2)
            for j, (ox, oy) in enumerate(others):
                rows = outs[i].at[2 * ox + oy, theirs, :]
                _remote(rows, rows, ssem.at[per * i + 3 + j], rsem.at[per * i + 3 + j], sibling).wait_recv()
            own = outs[i].at[chip]
            _remote(own, own, ssem.at[per * i + 6], rsem.at[per * i + 6], sibling).wait_recv()
        for cp in first_copies(ins, outs, ssem, rsem) + forwards:
            cp.wait_send()

    shapes = [jax.ShapeDtypeStruct((N_CHIPS,) + s.shape[1:], s.dtype) for s, _ in pieces]
    return _Rider([s for s, _ in pieces], shapes, per * len(pieces), start, finish)


def _chip_exchange_rider(ps):
    def copies(ins, outs, ssem, rsem):
        x, y, c, chip, others = _mesh_pos()
        return [_remote(ins[i].at[2 * ox + oy], outs[i].at[chip], ssem.at[3 * i + j], rsem.at[3 * i + j], (ox, oy, c))
                for i in range(len(ps)) for j, (ox, oy) in enumerate(others)]

    def start(ins, outs, ssem, rsem):
        for cp in copies(ins, outs, ssem, rsem):
            cp.start()

    def finish(ins, outs, ssem, rsem):
        x, y, c, chip, others = _mesh_pos()
        for i in range(len(ps)):
            for j, (ox, oy) in enumerate(others):
                slot = outs[i].at[2 * ox + oy]
                _remote(slot, slot, ssem.at[3 * i + j], rsem.at[3 * i + j], (ox, oy, c)).wait_recv()
        for cp in copies(ins, outs, ssem, rsem):
            cp.wait_send()

    return _Rider(list(ps), [jax.ShapeDtypeStruct(p.shape, p.dtype) for p in ps], 3 * len(ps), start, finish)


def _pair_exchange_rider(gs):
    def copies(ins, outs, ssem, rsem):
        x, y, c, _, _ = _mesh_pos()
        return [_remote(ins[i].at[:, _half_rows(1 - c, ins[i].shape[1] // 2), :], outs[i],
                        ssem.at[i], rsem.at[i], (x, y, 1 - c)) for i in range(len(gs))]

    def start(ins, outs, ssem, rsem):
        for cp in copies(ins, outs, ssem, rsem):
            cp.start()

    def finish(ins, outs, ssem, rsem):
        for cp in copies(ins, outs, ssem, rsem):
            cp.wait()

    shapes = [jax.ShapeDtypeStruct((g.shape[0], g.shape[1] // 2, g.shape[2]), g.dtype) for g in gs]
    return _Rider(list(gs), shapes, len(gs), start, finish)


def _run_rider(rider, name):
    def body(*refs):
        n = len(rider.ins)
        ins, outs = refs[:n], refs[n:2 * n]
        ssem, rsem = refs[2 * n:]
        rider.start(ins, outs, ssem, rsem)
        rider.finish(ins, outs, ssem, rsem)

    rider.results = pl.pallas_call(
        body,
        name=name,
        in_specs=[ANY] * len(rider.ins),
        out_specs=[ANY] * len(rider.ins),
        out_shape=rider.out_shapes,
        input_output_aliases=rider.aliases(0, 0),
        scratch_shapes=[pltpu.SemaphoreType.DMA((rider.n_sem,)), pltpu.SemaphoreType.DMA((rider.n_sem,))],
    )(*rider.ins)
    return rider.results


def _pair_gather_rider(fs):
    n = len(fs)

    def copies(outs, ssem, rsem):
        x, y, c, _, _ = _mesh_pos()
        halves = [outs[i].at[_half_rows(c, outs[i].shape[0] // 2), :] for i in range(n)]
        return [_remote(h, h, ssem.at[i], rsem.at[i], (x, y, 1 - c)) for i, h in enumerate(halves)]

    def start(ins, outs, ssem, rsem):
        for cp in copies(outs, ssem, rsem):
            cp.start()

    def finish(ins, outs, ssem, rsem):
        x, y, c, _, _ = _mesh_pos()
        for i in range(n):
            theirs = outs[i].at[_half_rows(1 - c, outs[i].shape[0] // 2), :]
            _remote(theirs, theirs, ssem.at[i], rsem.at[i], (x, y, 1 - c)).wait_recv()
        for cp in copies(outs, ssem, rsem):
            cp.wait_send()

    return _Rider(list(fs), [jax.ShapeDtypeStruct(f.shape, f.dtype) for f in fs], n, start, finish, in_place=True)


def _call(body, *, name, grid, in_specs, out_specs, out_shape, operands, scratch=(), sem=None, rider=None):
    if rider is None:
        return pl.pallas_call(
            body, name=name, grid=grid, in_specs=in_specs, out_specs=out_specs, out_shape=out_shape,
            scratch_shapes=list(scratch), compiler_params=_params(sem))(*operands)
    n_in, n_out, n_sc, r = len(in_specs), len(out_specs), len(scratch), len(rider.ins)

    def carrying(*refs):
        a, b = n_in, n_in + r
        c, d = b + n_out, b + n_out + r
        e = d + n_sc
        ids = [pl.program_id(k) for k in range(len(grid))]
        first = functools.reduce(jnp.logical_and, [i == 0 for i in ids])
        last = functools.reduce(jnp.logical_and, [i == g - 1 for i, g in zip(ids, grid)])

        @pl.when(first)
        def _():
            rider.start(refs[a:b], refs[c:d], refs[e], refs[e + 1])

        body(*refs[:a], *refs[b:c], *refs[d:e])

        @pl.when(last)
        def _():
            rider.finish(refs[a:b], refs[c:d], refs[e], refs[e + 1])

    outs = pl.pallas_call(
        carrying, name=name, grid=grid,
        in_specs=list(in_specs) + [ANY] * r,
        out_specs=list(out_specs) + [ANY] * r,
        out_shape=list(out_shape) + list(rider.out_shapes),
        scratch_shapes=list(scratch) + [pltpu.SemaphoreType.DMA((rider.n_sem,)), pltpu.SemaphoreType.DMA((rider.n_sem,))],
        input_output_aliases=rider.aliases(n_in, n_out),
        compiler_params=_params(("arbitrary",) * len(grid)),
    )(*operands, *rider.ins)
    rider.results = outs[n_out:]
    return outs[:n_out]


def _ffn_fwd(h, gain, wg, wu, wd, layer, tm, name, rider=None):
    T, D = h.shape
    Fs = wg.shape[-1]
    F = N_CHIPS * Fs

    def body(h_ref, g_ref, wg_ref, wu_ref, wd_ref, ho_ref, a_ref, go_ref, uo_ref, act_ref, acc_ref):
        s = pl.program_id(1)

        @pl.when(s == 0)
        def _():
            a_ref[...] = _rms_fwd(h_ref[...], g_ref[...]).astype(BF16)
            acc_ref[...] = jnp.zeros_like(acc_ref)

        a = a_ref[...]
        g = _dot(a, wg_ref[...])
        u = _dot(a, wu_ref[...])
        sg = _sigmoid(g)
        act = (g * sg * u).astype(BF16)
        go_ref[...] = (u * (sg * (1.0 + g * (1.0 - sg)))).astype(BF16)
        uo_ref[...] = (g * sg).astype(BF16)
        act_ref[...] = act
        acc_ref[...] += _dot(act, wd_ref[...])

        @pl.when(s == N_CHIPS - 1)
        def _():
            ho_ref[...] = h_ref[...] + 0.5 * acc_ref[...]

    row = pl.BlockSpec((tm, D), lambda t, s: (t, 0))
    col = pl.BlockSpec((tm, Fs), lambda t, s: (t, s))
    wcol = pl.BlockSpec((None, D, Fs), lambda t, s: (s, 0, 0))
    return _call(
        body, name=name, grid=(T // tm, N_CHIPS),
        in_specs=[row, pl.BlockSpec((None, 1, D), lambda t, s: (layer, 0, 0)), wcol, wcol,
                  pl.BlockSpec((None, Fs, D), lambda t, s: (s, 0, 0))],
        out_specs=[row, row, col, col, col],
        out_shape=[jax.ShapeDtypeStruct((T, D), F32), jax.ShapeDtypeStruct((T, D), BF16)]
        + [jax.ShapeDtypeStruct((T, F), BF16)] * 3,
        scratch=[pltpu.VMEM((tm, D), F32)],
        sem=("parallel", "arbitrary"), operands=(h, gain, wg, wu, wd), rider=rider)


def _inproj_fwd(h, gain, win, layer, tm, name, rider=None):
    T, D = h.shape
    Ns = win.shape[-1]

    def body(h_ref, g_ref, w_ref, z_ref, b_ref):
        @pl.when(pl.program_id(1) == 0)
        def _():
            b_ref[...] = _rms_fwd(h_ref[...], g_ref[...]).astype(BF16)

        z_ref[...] = _dot(b_ref[...], w_ref[...]).astype(BF16)

    return _call(
        body, name=name, grid=(T // tm, N_CHIPS),
        in_specs=[pl.BlockSpec((tm, D), lambda t, s: (t, 0)),
                  pl.BlockSpec((None, 1, D), lambda t, s: (layer, 0, 0)),
                  pl.BlockSpec((None, D, Ns), lambda t, s: (s, 0, 0))],
        out_specs=[pl.BlockSpec((tm, Ns), lambda t, s: (t, s)), pl.BlockSpec((tm, D), lambda t, s: (t, 0))],
        out_shape=[jax.ShapeDtypeStruct((T, N_CHIPS * Ns), BF16), jax.ShapeDtypeStruct((T, D), BF16)],
        sem=("parallel", "arbitrary"), operands=(h, gain, win), rider=rider)


def _ret_consts(T, pad):
    half = HEAD_DIM // 2
    inv_freq = ROPE_BASE ** (-jnp.arange(half, dtype=F32) / half)
    pos = jnp.arange(T, dtype=F32) - pad
    ang = pos[:, None] * inv_freq[None, :]
    cos = jnp.cos(ang)
    sin = jnp.sin(ang)
    cosf = jnp.concatenate([cos, cos], axis=1)
    sinf = jnp.concatenate([-sin, sin], axis=1)
    log_gamma = jnp.log1p(-(2.0 ** (-5.0 - jnp.arange(RET_HEADS, dtype=F32))))
    idx = jnp.arange(CHUNK, dtype=F32)
    diff = idx[:, None] - idx[None, :]
    intra = jnp.where(diff[None] >= 0, jnp.exp(diff[None] * log_gamma[:, None, None]), 0.0)
    k_decay = jnp.exp((CHUNK - 1.0 - idx)[None, :] * log_gamma[:, None])
    q_decay = jnp.exp((idx + 1.0)[None, :] * log_gamma[:, None])
    chunk_decay = jnp.exp(CHUNK * log_gamma)
    kdec = jnp.broadcast_to(k_decay[:, :, None], (RET_HEADS, CHUNK, HEAD_DIM))
    qdec = jnp.broadcast_to(q_decay[:, :, None], (RET_HEADS, CHUNK, HEAD_DIM))
    cdb = jnp.broadcast_to(chunk_decay[:, None, None], (RET_HEADS, 8, HEAD_DIM))
    return cosf, sinf, intra, kdec, qdec, cdb


def _rot(t, cosv, sinv):
    return t * cosv + pltpu.roll(t, HEAD_DIM // 2, 1) * sinv


def _rot_t(g, cosv, sinv):
    return g * cosv + pltpu.roll(g * sinv, HEAD_DIM // 2, 1)


def _head_specs(tg, section, order):
    return pl.BlockSpec((tg, HEAD_DIM), lambda h, g: (order(g), section * RET_HEADS + h))


def _ret_fwd(z, consts, cg, name, rider=None):
    T = z.shape[0]
    N = T // CHUNK
    ng = N // cg
    tg = cg * CHUNK
    cosf, sinf, intra, kdec, qdec, cdb = consts
    fwd = lambda g: g

    def body(zq, zk, zv, zg, cos_ref, sin_ref, m_ref, kd_ref, qd_ref, cd_ref, r_ref, o_ref, s_ref, st_ref):
        @pl.when(pl.program_id(1) == 0)
        def _():
            st_ref[...] = jnp.zeros_like(st_ref)

        cosv = cos_ref[...]
        sinv = sin_ref[...]
        q3 = (_rot(zq[...].astype(F32), cosv, sinv) * (HEAD_DIM ** -0.5)).reshape(cg, CHUNK, HEAD_DIM)
        k3 = _rot(zk[...].astype(F32), cosv, sinv).reshape(cg, CHUNK, HEAD_DIM)
        vb = zv[...].reshape(cg, CHUNK, HEAD_DIM).astype(BF16)
        scores = _ein("ncd,nmd->ncm", q3.astype(BF16), k3.astype(BF16)) * m_ref[...][None]
        inner = _ein("ncm,nmd->ncd", scores.astype(BF16), vb)
        kv = _ein("ncd,nce->nde", (k3 * kd_ref[...][None]).astype(BF16), vb)
        cd = cd_ref[0:1, :]
        state = st_ref[...]
        for n in range(cg):
            s_ref[n] = state
            state = state * cd + kv[n]
        st_ref[...] = state
        qdb = (q3 * qd_ref[...][None]).astype(BF16)
        cross = _ein("ncd,nde->nce", qdb, s_ref[...].astype(BF16))
        out = (inner + cross).reshape(tg, HEAD_DIM)
        o_ref[...] = out
        xc = out - jnp.mean(out, axis=-1, keepdims=True)
        rn = xc * lax.rsqrt(jnp.mean(xc * xc, axis=-1, keepdims=True) + EPS)
        g = zg[...].astype(F32)
        r_ref[...] = (rn * (g * _sigmoid(g))).astype(BF16)

    tab = pl.BlockSpec((tg, HEAD_DIM), lambda h, g: (g, 0))
    per_head = lambda rows: pl.BlockSpec((None, rows, HEAD_DIM), lambda h, g: (h, 0, 0))
    head_out = pl.BlockSpec((tg, HEAD_DIM), lambda h, g: (g, h))
    return _call(
        body, name=name, grid=(RET_HEADS, ng),
        in_specs=[_head_specs(tg, i, fwd) for i in range(4)]
        + [tab, tab, per_head(CHUNK), per_head(CHUNK), per_head(CHUNK), per_head(8)],
        out_specs=[head_out, head_out, pl.BlockSpec((None, cg, HEAD_DIM, HEAD_DIM), lambda h, g: (h, g, 0, 0))],
        out_shape=[jax.ShapeDtypeStruct((T, RET_WIDTH), BF16), jax.ShapeDtypeStruct((T, RET_WIDTH), F32),
                   jax.ShapeDtypeStruct((RET_HEADS, N, HEAD_DIM, HEAD_DIM), F32)],
        scratch=[pltpu.VMEM((HEAD_DIM, HEAD_DIM), F32)],
        sem=("parallel", "arbitrary"), operands=(z, z, z, z, cosf, sinf, intra, kdec, qdec, cdb), rider=rider)


def _window_sums(u, shift_of):
    sums = []
    s = u
    k = 1
    while k < POOL_WINDOWS[-1]:
        s = s + pltpu.roll(s, shift_of(k), 0)
        sums.append(s)
        k *= 2
    return sums


def _select_group(vals, g):
    out = vals[-1]
    for i in range(len(vals) - 2, -1, -1):
        out = jnp.where(g == i, vals[i], out)
    return out


def _pool_parts(u, g, T, pad):
    rows = lax.broadcasted_iota(jnp.int32, (T, HEAD_DIM), 0)
    valid = rows >= pad
    win = _select_group([float(w) for w in POOL_WINDOWS], g)
    div = jnp.clip((rows - pad + 1).astype(F32), 1.0, win)
    s = _select_group(_window_sums(u, lambda k: k), g)
    pooled = jnp.where(valid, s / div - u, 0.0)
    return pooled, div, valid


def _pool_specs(T, layer):
    first = 4 * RET_WIDTH // HEAD_DIM
    return [
        pl.BlockSpec((T, HEAD_DIM), lambda g: (0, first + g)),
        pl.BlockSpec((None, None, HEAD_DIM, HEAD_DIM), lambda g: (layer, g, 0, 0)),
        pl.BlockSpec((None, 1, HEAD_DIM), lambda g: (layer, 0, g)),
    ]


def _pool_fwd(z, maps, scale, layer, pad, name):
    T = z.shape[0]
    assert pad >= POOL_WINDOWS[-1], "window rolls wrap into the zero rows in front"

    def body(zu, maps_ref, sc_ref, pm_ref):
        g = pl.program_id(0)
        pooled, _, _ = _pool_parts(zu[...].astype(F32), g, T, pad)
        y = _dot(pooled.astype(BF16), maps_ref[...].astype(BF16))
        pm_ref[...] = (y * sc_ref[...]).astype(BF16)

    return _call(
        body, name=name, grid=(POOL_GROUPS,),
        in_specs=_pool_specs(T, layer),
        out_specs=[pl.BlockSpec((T, HEAD_DIM), lambda g: (0, g))],
        out_shape=[jax.ShapeDtypeStruct((T, POOL_WIDTH), BF16)],
        sem=("parallel",), operands=(z, maps, scale))[0]


def _gate_specs(tm, D):
    nb = D // RET_WIDTH
    first = (4 * RET_WIDTH + POOL_WIDTH) // RET_WIDTH
    return [pl.BlockSpec((tm, RET_WIDTH), functools.partial(lambda t, j: (t, j), j=first + j)) for j in range(2 * nb)]


def _load_gates(refs, nb):
    ga = jnp.concatenate([r[...].astype(F32) for r in refs[:nb]], axis=1)
    gb = jnp.concatenate([r[...].astype(F32) for r in refs[nb:]], axis=1)
    return ga, gb


def _mix_fwd(h, r, pm, z, wru, wpu, wout, tm, name, rider=None):
    T, D = h.shape
    Dq = D // N_CHIPS
    nb = D // RET_WIDTH

    def body(*refs):
        h_ref, r_ref, pm_ref = refs[:3]
        gate_refs = refs[3:3 + 2 * nb]
        wru_ref, wpu_ref, wout_ref, ho_ref, mx_ref, ret_ref, pool_ref = refs[3 + 2 * nb:]
        rv = r_ref[...]
        pv = pm_ref[...]
        ret = jnp.concatenate([_dot(rv, wru_ref[s]) for s in range(N_CHIPS)], axis=1)
        pool = jnp.concatenate([_dot(pv, wpu_ref[s]) for s in range(N_CHIPS)], axis=1)
        ga, gb = _load_gates(gate_refs, nb)
        mixed = (_sigmoid(ga) * ret + _sigmoid(gb) * pool).astype(BF16)
        mx_ref[...] = mixed
        ret_ref[...] = ret.astype(BF16)
        pool_ref[...] = pool.astype(BF16)
        ho_ref[...] = h_ref[...] + _dot(mixed, wout_ref[...].reshape(D, D))

    row = pl.BlockSpec((tm, D), lambda t: (t, 0))
    half = pl.BlockSpec((tm, RET_WIDTH), lambda t: (t, 0))
    up = pl.BlockSpec((N_CHIPS, RET_WIDTH, Dq), lambda t: (0, 0, 0))
    return _call(
        body, name=name, grid=(T // tm,),
        in_specs=[row, half, half] + _gate_specs(tm, D) + [up, up, pl.BlockSpec((N_CHIPS, Dq, D), lambda t: (0, 0, 0))],
        out_specs=[row, row, row, row],
        out_shape=[jax.ShapeDtypeStruct((T, D), F32)] + [jax.ShapeDtypeStruct((T, D), BF16)] * 3,
        sem=("parallel",), operands=(h, r, pm, *([z] * (2 * nb)), wru, wpu, wout), rider=rider)


def _final_loss(h, gain, tgt, name):
    T, D = h.shape
    first = (T - tgt.shape[0]) // CHUNK

    def body(h_ref, g_ref, t_ref, dh_ref, loss_ref, dg_ref):
        i = pl.program_id(0)

        @pl.when(i == 0)
        def _():
            loss_ref[...] = jnp.zeros_like(loss_ref)
            dg_ref[...] = jnp.zeros_like(dg_ref)

        x = h_ref[...]
        gain_v = g_ref[...]
        err = jnp.where(i >= first, _rms_fwd(x, gain_v) - t_ref[...], 0.0)
        loss_ref[...] += 0.5 * jnp.sum(jnp.mean(err * err, axis=-1))
        dx, dgain = _rms_bwd(x, gain_v, err * (1.0 / D))
        dg_ref[...] += dgain
        dh_ref[...] = dx

    return _call(
        body, name=name, grid=(T // CHUNK,),
        in_specs=[pl.BlockSpec((CHUNK, D), lambda i: (i, 0)),
                  pl.BlockSpec((1, D), lambda i: (0, 0)),
                  pl.BlockSpec((CHUNK, D), lambda i: (jnp.maximum(i - first, 0), 0))],
        out_specs=[pl.BlockSpec((CHUNK, D), lambda i: (i, 0)),
                   pl.BlockSpec((1, LANES), lambda i: (0, 0)),
                   pl.BlockSpec((1, D), lambda i: (0, 0))],
        out_shape=[jax.ShapeDtypeStruct((T, D), F32), jax.ShapeDtypeStruct((1, LANES), F32),
                   jax.ShapeDtypeStruct((1, D), F32)],
        sem=("arbitrary",), operands=(h, gain, tgt))


def _ffn_bwd_act(dy, g, u, wd, tm, name, rider=None):
    T, D = dy.shape
    Fs = wd.shape[1]
    F = N_CHIPS * Fs

    def body(dy_ref, go_ref, uo_ref, wd_ref, dg_ref, du_ref, dyh_ref):
        @pl.when(pl.program_id(1) == 0)
        def _():
            dyh_ref[...] = (0.5 * dy_ref[...]).astype(BF16)

        dact = _dot_nt(dyh_ref[...], wd_ref[...])
        du_ref[...] = (dact * uo_ref[...].astype(F32)).astype(BF16)
        dg_ref[...] = (dact * go_ref[...].astype(F32)).astype(BF16)

    row = pl.BlockSpec((tm, D), lambda t, s: (t, 0))
    col = pl.BlockSpec((tm, Fs), lambda t, s: (t, s))
    return _call(
        body, name=name, grid=(T // tm, N_CHIPS),
        in_specs=[row, col, col, pl.BlockSpec((None, Fs, D), lambda t, s: (s, 0, 0))],
        out_specs=[col, col, row],
        out_shape=[jax.ShapeDtypeStruct((T, F), BF16), jax.ShapeDtypeStruct((T, F), BF16),
                   jax.ShapeDtypeStruct((T, D), BF16)],
        sem=("parallel", "arbitrary"), operands=(dy, g, u, wd), rider=rider)


def _ffn_bwd_in(dy, h, gain, dg, du, wg, wu, layer, tm, pad, name, rider=None):
    T, D = h.shape
    Fs = wg.shape[-1]

    def body(dy_ref, h_ref, g_ref, dg_ref, du_ref, wg_ref, wu_ref, dh_ref, dgain_ref, da_ref):
        t = pl.program_id(0)
        s = pl.program_id(1)

        @pl.when((t == 0) & (s == 0))
        def _():
            dgain_ref[...] = jnp.zeros_like(dgain_ref)

        @pl.when(s == 0)
        def _():
            da_ref[...] = jnp.zeros_like(da_ref)

        da_ref[...] += _dot_nt(dg_ref[...], wg_ref[...]) + _dot_nt(du_ref[...], wu_ref[...])

        @pl.when(s == N_CHIPS - 1)
        def _():
            dx, dgain = _rms_bwd(h_ref[...], g_ref[...], da_ref[...])
            dgain_ref[...] += dgain
            dh_ref[...] = jnp.where(_row_mask(t, tm, pad, (tm, D)), dy_ref[...] + dx, 0.0)

    row = pl.BlockSpec((tm, D), lambda t, s: (t, 0))
    col = pl.BlockSpec((tm, Fs), lambda t, s: (t, s))
    wcol = pl.BlockSpec((None, D, Fs), lambda t, s: (s, 0, 0))
    return _call(
        body, name=name, grid=(T // tm, N_CHIPS),
        in_specs=[row, row, pl.BlockSpec((None, 1, D), lambda t, s: (layer, 0, 0)), col, col, wcol, wcol],
        out_specs=[row, pl.BlockSpec((1, D), lambda t, s: (0, 0))],
        out_shape=[jax.ShapeDtypeStruct((T, D), F32), jax.ShapeDtypeStruct((1, D), F32)],
        scratch=[pltpu.VMEM((tm, D), F32)],
        sem=("arbitrary", "arbitrary"), operands=(dy, h, gain, dg, du, wg, wu), rider=rider)


def _grad_tn(a, b, mode, scale, tm, name, rider=None):
    T = a.shape[0]
    if mode == "col":
        per, R, C = 1, a.shape[1], b.shape[1] // N_CHIPS
        a_spec = pl.BlockSpec((tm, R), lambda s, t: (t, 0))
        b_spec = pl.BlockSpec((tm, C), lambda s, t: (t, s))
    else:
        per, R, C = 2, a.shape[1] // N_CHIPS, b.shape[1]
        a_spec = pl.BlockSpec((tm, per * R), lambda s, t: (t, s))
        b_spec = pl.BlockSpec((tm, C), lambda s, t: (t, 0))
    nt = T // tm

    def body(a_ref, b_ref, o_ref, acc_ref):
        t = pl.program_id(1)

        @pl.when(t == 0)
        def _():
            acc_ref[...] = jnp.zeros_like(acc_ref)

        acc_ref[...] += _dot_tn(a_ref[...].astype(BF16), b_ref[...].astype(BF16))

        @pl.when(t == nt - 1)
        def _():
            o_ref[...] = (scale * acc_ref[...]).astype(BF16).reshape(per, R, C)

    return _call(
        body, name=name, grid=(N_CHIPS // per, nt),
        in_specs=[a_spec, b_spec],
        out_specs=[pl.BlockSpec((per, R, C), lambda s, t: (s, 0, 0))],
        out_shape=[jax.ShapeDtypeStruct((N_CHIPS, R, C), BF16)],
        scratch=[pltpu.VMEM((per * R, C), F32)],
        sem=("parallel", "arbitrary"), operands=(a, b), rider=rider)[0]


def _grad_mix(mixed, dh, r, dret, pm, dpool, tk, name, rider=None):
    T, D = dh.shape
    Dq = D // N_CHIPS
    nt = T // tk

    def body(mx_ref, dh_ref, r_ref, dret_ref, pm_ref, dpool_ref, go_ref, gr_ref, gp_ref, ao_ref, ar_ref, ap_ref):
        t = pl.program_id(0)

        @pl.when(t == 0)
        def _():
            ao_ref[...] = jnp.zeros_like(ao_ref)
            ar_ref[...] = jnp.zeros_like(ar_ref)
            ap_ref[...] = jnp.zeros_like(ap_ref)

        ao_ref[...] += _dot_tn(mx_ref[...], dh_ref[...].astype(BF16))
        ar_ref[...] += _dot_tn(r_ref[...], dret_ref[...])
        ap_ref[...] += _dot_tn(pm_ref[...], dpool_ref[...])

        @pl.when(t == nt - 1)
        def _():
            go_ref[...] = ao_ref[...].astype(BF16).reshape(N_CHIPS, Dq, D)
            for s in range(N_CHIPS):
                gr_ref[s] = ar_ref[:, s * Dq:(s + 1) * Dq].astype(BF16)
                gp_ref[s] = ap_ref[:, s * Dq:(s + 1) * Dq].astype(BF16)

    row = pl.BlockSpec((tk, D), lambda t: (t, 0))
    half = pl.BlockSpec((tk, RET_WIDTH), lambda t: (t, 0))
    whole = lambda shape: pl.BlockSpec(shape, lambda t: (0, 0, 0))
    return _call(
        body, name=name, grid=(nt,),
        in_specs=[row, row, half, row, half, row],
        out_specs=[whole((N_CHIPS, Dq, D)), whole((N_CHIPS, RET_WIDTH, Dq)), whole((N_CHIPS, POOL_WIDTH, Dq))],
        out_shape=[jax.ShapeDtypeStruct((N_CHIPS, Dq, D), BF16),
                   jax.ShapeDtypeStruct((N_CHIPS, RET_WIDTH, Dq), BF16),
                   jax.ShapeDtypeStruct((N_CHIPS, POOL_WIDTH, Dq), BF16)],
        scratch=[pltpu.VMEM((D, D), F32), pltpu.VMEM((RET_WIDTH, D), F32), pltpu.VMEM((POOL_WIDTH, D), F32)],
        sem=("arbitrary",), operands=(mixed, dh, r, dret, pm, dpool), rider=rider)


def _mix_bwd_dx(dh, z, ret, pool, wout, wru, wpu, tm, name, rider=None):
    T, D = dh.shape
    Dq = D // N_CHIPS
    nb = D // RET_WIDTH

    def body(*refs):
        dh_ref = refs[0]
        gate_refs = refs[1:1 + 2 * nb]
        ret_ref, pool_ref, wout_ref, wru_ref, wpu_ref, dgab_ref, dret_ref, dpool_ref, dr_ref, dpm_ref = refs[1 + 2 * nb:]
        dmixed = _dot_nt(dh_ref[...].astype(BF16), wout_ref[...].reshape(D, D))
        ga, gb = _load_gates(gate_refs, nb)
        sa = _sigmoid(ga)
        sb = _sigmoid(gb)
        dgab_ref[:, :D] = (dmixed * ret_ref[...].astype(F32) * (sa * (1.0 - sa))).astype(BF16)
        dgab_ref[:, D:] = (dmixed * pool_ref[...].astype(F32) * (sb * (1.0 - sb))).astype(BF16)
        dret = (dmixed * sa).astype(BF16)
        dpool = (dmixed * sb).astype(BF16)
        dret_ref[...] = dret
        dpool_ref[...] = dpool
        dr = _dot_nt(dret[:, :Dq], wru_ref[0])
        dpm = _dot_nt(dpool[:, :Dq], wpu_ref[0])
        for s in range(1, N_CHIPS):
            dr += _dot_nt(dret[:, s * Dq:(s + 1) * Dq], wru_ref[s])
            dpm += _dot_nt(dpool[:, s * Dq:(s + 1) * Dq], wpu_ref[s])
        dr_ref[...] = dr
        dpm_ref[...] = dpm

    row = pl.BlockSpec((tm, D), lambda t: (t, 0))
    half = pl.BlockSpec((tm, RET_WIDTH), lambda t: (t, 0))
    up = pl.BlockSpec((N_CHIPS, RET_WIDTH, Dq), lambda t: (0, 0, 0))
    return _call(
        body, name=name, grid=(T // tm,),
        in_specs=[row] + _gate_specs(tm, D) + [row, row, pl.BlockSpec((N_CHIPS, Dq, D), lambda t: (0, 0, 0)), up, up],
        out_specs=[pl.BlockSpec((tm, 2 * D), lambda t: (t, 0)), row, row, half, half],
        out_shape=[jax.ShapeDtypeStruct((T, 2 * D), BF16), jax.ShapeDtypeStruct((T, D), BF16),
                   jax.ShapeDtypeStruct((T, D), BF16), jax.ShapeDtypeStruct((T, RET_WIDTH), F32),
                   jax.ShapeDtypeStruct((T, POOL_WIDTH), F32)],
        sem=("parallel",), operands=(dh, *([z] * (2 * nb)), ret, pool, wout, wru, wpu), rider=rider)


def _pool_bwd(z, dpm, maps, scale, layer, pad, name):
    T = z.shape[0]

    def body(zu, maps_ref, sc_ref, dpm_ref, du_ref, dmaps_ref, dsc_ref):
        g = pl.program_id(0)
        u = zu[...].astype(F32)
        pooled, div, valid = _pool_parts(u, g, T, pad)
        pb = pooled.astype(BF16)
        mb = maps_ref[...].astype(BF16)
        dp = dpm_ref[...]
        dsc_ref[...] = jnp.sum(dp * _dot(pb, mb), axis=0, keepdims=True)
        dyb = (dp * sc_ref[...]).astype(BF16)
        dmaps_ref[...] = _dot_tn(pb, dyb)
        dpooled = jnp.where(valid, _dot_nt(dyb, mb), 0.0)
        ahead = _select_group(_window_sums(dpooled / div, lambda k: T - k), g)
        du_ref[...] = jnp.where(valid, ahead - dpooled, 0.0).astype(BF16)

    blk = pl.BlockSpec((T, HEAD_DIM), lambda g: (0, g))
    return _call(
        body, name=name, grid=(POOL_GROUPS,),
        in_specs=_pool_specs(T, layer) + [blk],
        out_specs=[blk, pl.BlockSpec((None, HEAD_DIM, HEAD_DIM), lambda g: (g, 0, 0)),
                   pl.BlockSpec((1, HEAD_DIM), lambda g: (0, g))],
        out_shape=[jax.ShapeDtypeStruct((T, POOL_WIDTH), BF16),
                   jax.ShapeDtypeStruct((POOL_GROUPS, HEAD_DIM, HEAD_DIM), F32),
                   jax.ShapeDtypeStruct((1, POOL_WIDTH), F32)],
        sem=("parallel",), operands=(z, maps, scale, dpm))


def _ret_bwd_local(z, o_pre, s_all, dr, consts, cg, name):
    T = z.shape[0]
    N = T // CHUNK
    ng = N // cg
    tg = cg * CHUNK
    cosf, sinf, intra, _, qdec, _ = consts
    fwd = lambda g: g

    def body(zq, zk, zv, zg, o_ref, s_ref, dr_ref, cos_ref, sin_ref, m_ref, qd_ref,
             dq_ref, dg_ref, dk_ref, dv_ref, ds_ref):
        cosv = cos_ref[...]
        sinv = sin_ref[...]
        scale = HEAD_DIM ** -0.5
        q3 = (_rot(zq[...].astype(F32), cosv, sinv) * scale).reshape(cg, CHUNK, HEAD_DIM)
        k3 = _rot(zk[...].astype(F32), cosv, sinv).reshape(cg, CHUNK, HEAD_DIM)
        qb = q3.astype(BF16)
        kb = k3.astype(BF16)
        vb = zv[...].reshape(cg, CHUNK, HEAD_DIM).astype(BF16)
        mask = m_ref[...][None]
        sb = (_ein("ncd,nmd->ncm", qb, kb) * mask).astype(BF16)
        qdv = qd_ref[...][None]
        qdb = (q3 * qdv).astype(BF16)

        out = o_ref[...]
        xc = out - jnp.mean(out, axis=-1, keepdims=True)
        rstd = lax.rsqrt(jnp.mean(xc * xc, axis=-1, keepdims=True) + EPS)
        rn = xc * rstd
        g = zg[...].astype(F32)
        sg = _sigmoid(g)
        drv = dr_ref[...]
        dg_ref[...] = (drv * rn * (sg * (1.0 + g * (1.0 - sg)))).astype(BF16)
        drn = drv * (g * sg)
        dout = rstd * (drn - jnp.mean(drn, axis=-1, keepdims=True)
                       - rn * jnp.mean(drn * rn, axis=-1, keepdims=True))
        dob = dout.reshape(cg, CHUNK, HEAD_DIM).astype(BF16)

        dsb = (_ein("ncd,nmd->ncm", dob, vb) * mask).astype(BF16)
        dv_ref[...] = _ein("ncm,ncd->nmd", sb, dob).reshape(tg, HEAD_DIM)
        dk_ref[...] = _ein("ncm,ncd->nmd", dsb, qb).reshape(tg, HEAD_DIM)
        dq3 = _ein("ncm,nmd->ncd", dsb, kb) + _ein("nce,nde->ncd", dob, s_ref[...].astype(BF16)) * qdv
        dq_ref[...] = _rot_t(dq3.reshape(tg, HEAD_DIM) * scale, cosv, sinv).astype(BF16)
        ds_ref[...] = _ein("ncd,nce->nde", qdb, dob)

    tab = pl.BlockSpec((tg, HEAD_DIM), lambda h, g: (g, 0))
    per_head = pl.BlockSpec((None, CHUNK, HEAD_DIM), lambda h, g: (h, 0, 0))
    head_blk = pl.BlockSpec((tg, HEAD_DIM), lambda h, g: (g, h))
    state_blk = pl.BlockSpec((None, cg, HEAD_DIM, HEAD_DIM), lambda h, g: (h, g, 0, 0))
    return _call(
        body, name=name, grid=(RET_HEADS, ng),
        in_specs=[_head_specs(tg, i, fwd) for i in range(4)]
        + [head_blk, state_blk, head_blk, tab, tab, per_head, per_head],
        out_specs=[head_blk, head_blk, head_blk, head_blk, state_blk],
        out_shape=[jax.ShapeDtypeStruct((T, RET_WIDTH), BF16), jax.ShapeDtypeStruct((T, RET_WIDTH), BF16),
                   jax.ShapeDtypeStruct((T, RET_WIDTH), F32), jax.ShapeDtypeStruct((T, RET_WIDTH), F32),
                   jax.ShapeDtypeStruct((RET_HEADS, N, HEAD_DIM, HEAD_DIM), F32)],
        sem=("parallel", "parallel"), operands=(z, z, z, z, o_pre, s_all, dr, cosf, sinf, intra, qdec))


def _ret_bwd_state(z, dkp, dvp, ds, consts, cg, name):
    T = z.shape[0]
    N = T // CHUNK
    ng = N // cg
    tg = cg * CHUNK
    cosf, sinf, _, kdec, _, cdb = consts
    rev = lambda g: ng - 1 - g

    def body(zk, zv, dkp_ref, dvp_ref, ds_ref, cos_ref, sin_ref, kd_ref, cd_ref, dk_ref, dv_ref, gs_ref, dkv_ref):
        @pl.when(pl.program_id(1) == 0)
        def _():
            gs_ref[...] = jnp.zeros_like(gs_ref)

        cosv = cos_ref[...]
        sinv = sin_ref[...]
        cd = cd_ref[0:1, :]
        grad = gs_ref[...]
        for n in reversed(range(cg)):
            dkv_ref[n] = grad
            grad = ds_ref[n] + cd * grad
        gs_ref[...] = grad
        dkvb = dkv_ref[...].astype(BF16)
        kdv = kd_ref[...][None]
        k3 = _rot(zk[...].astype(F32), cosv, sinv).reshape(cg, CHUNK, HEAD_DIM)
        vb = zv[...].reshape(cg, CHUNK, HEAD_DIM).astype(BF16)
        dk3 = _ein("nce,nde->ncd", vb, dkvb) * kdv
        dv3 = _ein("ncd,nde->nce", (k3 * kdv).astype(BF16), dkvb)
        dk_ref[...] = _rot_t(dkp_ref[...] + dk3.reshape(tg, HEAD_DIM), cosv, sinv).astype(BF16)
        dv_ref[...] = (dvp_ref[...] + dv3.reshape(tg, HEAD_DIM)).astype(BF16)

    tab = pl.BlockSpec((tg, HEAD_DIM), lambda h, g: (rev(g), 0))
    head_blk = pl.BlockSpec((tg, HEAD_DIM), lambda h, g: (rev(g), h))
    return _call(
        body, name=name, grid=(RET_HEADS, ng),
        in_specs=[_head_specs(tg, 1, rev), _head_specs(tg, 2, rev), head_blk, head_blk,
                  pl.BlockSpec((None, cg, HEAD_DIM, HEAD_DIM), lambda h, g: (h, rev(g), 0, 0)),
                  tab, tab,
                  pl.BlockSpec((None, CHUNK, HEAD_DIM), lambda h, g: (h, 0, 0)),
                  pl.BlockSpec((None, 8, HEAD_DIM), lambda h, g: (h, 0, 0))],
        out_specs=[head_blk, head_blk],
        out_shape=[jax.ShapeDtypeStruct((T, RET_WIDTH), BF16)] * 2,
        scratch=[pltpu.VMEM((HEAD_DIM, HEAD_DIM), F32), pltpu.VMEM((cg, HEAD_DIM, HEAD_DIM), F32)],
        sem=("parallel", "arbitrary"), operands=(z, z, dkp, dvp, ds, cosf, sinf, kdec, cdb))


def _inproj_bwd_dx(dz, win, h, gain, dh_in, layer, tm, pad, name, rider=None):
    T, D = h.shape
    Ns = win.shape[-1]

    def body(dz_ref, w_ref, h_ref, g_ref, dhi_ref, dh_ref, dgain_ref, db_ref):
        t = pl.program_id(0)
        s = pl.program_id(1)

        @pl.when((t == 0) & (s == 0))
        def _():
            dgain_ref[...] = jnp.zeros_like(dgain_ref)

        @pl.when(s == 0)
        def _():
            db_ref[...] = jnp.zeros_like(db_ref)

        db_ref[...] += _dot_nt(dz_ref[...], w_ref[...])

        @pl.when(s == N_CHIPS - 1)
        def _():
            dx, dgain = _rms_bwd(h_ref[...], g_ref[...], db_ref[...])
            dgain_ref[...] += dgain
            dh_ref[...] = jnp.where(_row_mask(t, tm, pad, (tm, D)), dhi_ref[...] + dx, 0.0)

    row = pl.BlockSpec((tm, D), lambda t, s: (t, 0))
    return _call(
        body, name=name, grid=(T // tm, N_CHIPS),
        in_specs=[pl.BlockSpec((tm, Ns), lambda t, s: (t, s)),
                  pl.BlockSpec((None, D, Ns), lambda t, s: (s, 0, 0)),
                  row, pl.BlockSpec((None, 1, D), lambda t, s: (layer, 0, 0)), row],
        out_specs=[row, pl.BlockSpec((1, D), lambda t, s: (0, 0))],
        out_shape=[jax.ShapeDtypeStruct((T, D), F32), jax.ShapeDtypeStruct((1, D), F32)],
        scratch=[pltpu.VMEM((tm, D), F32)],
        sem=("arbitrary", "arbitrary"), operands=(dz, win, h, gain, dh_in), rider=rider)


def _sum_pair(gs, rs, c_idx, name):
    n = len(gs)

    def body(c_ref, *refs):
        for g_ref, r_ref, o_ref in zip(refs[:n], refs[n:2 * n], refs[2 * n:]):
            o_ref[...] = (g_ref[...].astype(F32) + r_ref[...].astype(F32)).astype(BF16)

    halves = [pl.BlockSpec((None,) + r.shape[1:], lambda s, c_ref: (s, 0, 0)) for r in rs]
    return pl.pallas_call(
        body,
        name=name,
        grid_spec=pltpu.PrefetchScalarGridSpec(
            num_scalar_prefetch=1,
            grid=(N_CHIPS,),
            in_specs=[pl.BlockSpec((None,) + r.shape[1:], lambda s, c_ref: (s, c_ref[0], 0)) for r in rs] + halves,
            out_specs=halves,
        ),
        out_shape=[jax.ShapeDtypeStruct(r.shape, BF16) for r in rs],
        compiler_params=_params(("parallel",)),
    )(c_idx, *gs, *rs)


def _sum_chips(ps, rs, pos, name):
    n = len(ps)
    quarters = 4

    def body(pos_ref, *refs):
        chip = pos_ref[0]
        for p_ref, r_ref, o_ref in zip(refs[:n], refs[n:2 * n], refs[2 * n:]):
            own = p_ref[...].astype(F32)
            terms = [jnp.where(chip == k, own, r_ref[k].astype(F32)) for k in range(N_CHIPS)]
            o_ref[...] = ((terms[0] + terms[1]) + terms[2]) + terms[3]

    def rows(r):
        assert r.shape[1] % (quarters * BF16_ROWS) == 0, r.shape
        return r.shape[1] // quarters

    return pl.pallas_call(
        body,
        name=name,
        grid_spec=pltpu.PrefetchScalarGridSpec(
            num_scalar_prefetch=1,
            grid=(quarters,),
            in_specs=[pl.BlockSpec((None, rows(r), r.shape[2]), lambda q, pos_ref: (pos_ref[0], q, 0)) for r in rs]
            + [pl.BlockSpec((N_CHIPS, rows(r), r.shape[2]), lambda q, pos_ref: (0, q, 0)) for r in rs],
            out_specs=[pl.BlockSpec((rows(r), r.shape[2]), lambda q, pos_ref: (pos_ref[1] * quarters + q, 0))
                       for r in rs],
        ),
        out_shape=[jax.ShapeDtypeStruct((2 * r.shape[1], r.shape[2]), F32) for r in rs],
        compiler_params=_params(("arbitrary",)),
    )(pos, *ps, *rs)


def _small_all_reduce(p):
    rows, width = p.shape

    def body(p_ref, o_ref, sib_ref, slot_ref, ssem, rsem):
        x, y, c, chip, others = _mesh_pos()
        pair = _remote(p_ref, sib_ref, ssem.at[0], rsem.at[0], (x, y, 1 - c))
        pair.start()
        pair.wait()
        slot_ref[chip] = p_ref[...] + sib_ref[...]
        sends = []
        for j, (ox, oy) in enumerate(others):
            cp = _remote(slot_ref.at[chip], slot_ref.at[chip], ssem.at[1 + j], rsem.at[1 + j], (ox, oy, c))
            cp.start()
            sends.append(cp)
        for j, (ox, oy) in enumerate(others):
            slot = slot_ref.at[2 * ox + oy]
            _remote(slot, slot, ssem.at[1 + j], rsem.at[1 + j], (ox, oy, c)).wait_recv()
        for cp in sends:
            cp.wait_send()
        o_ref[...] = ((slot_ref[0] + slot_ref[1]) + slot_ref[2]) + slot_ref[3]

    vmem = pl.BlockSpec(memory_space=pltpu.VMEM)
    return pl.pallas_call(
        body,
        name="small_grads_all_reduce",
        in_specs=[vmem],
        out_specs=vmem,
        out_shape=jax.ShapeDtypeStruct(p.shape, F32),
        scratch_shapes=[pltpu.VMEM((rows, width), F32), pltpu.VMEM((N_CHIPS, rows, width), F32),
                        pltpu.SemaphoreType.DMA((4,)), pltpu.SemaphoreType.DMA((4,))],
    )(p)


def _adamw(gs, w, m, v, name):
    L, R, C = w.shape
    Ct = gs[0].shape[1]
    tr = _pick_tile(R, 256, 8)

    def body(*refs):
        g_refs = refs[:L]
        w_ref, m_ref, v_ref, go_ref, d_ref, mo_ref, vo_ref = refs[L:]
        layer = pl.program_id(0)
        grad = g_refs[L - 1][...]
        for i in range(L - 2, -1, -1):
            grad = jnp.where(layer == i, g_refs[i][...], grad)
        if Ct != C:
            grad = grad[:, :C]
        m_new = ADAM_B1 * m_ref[...] + (1.0 - ADAM_B1) * grad
        v_new = ADAM_B2 * v_ref[...] + (1.0 - ADAM_B2) * jnp.square(grad)
        m_hat = m_new / (1.0 - ADAM_B1 ** ADAM_STEP)
        v_hat = v_new / (1.0 - ADAM_B2 ** ADAM_STEP)
        go_ref[...] = grad
        d_ref[...] = -ADAM_LR * (m_hat / (jnp.sqrt(v_hat) + ADAM_EPS) + ADAM_WD * w_ref[...])
        mo_ref[...] = m_new
        vo_ref[...] = v_new

    g_specs = [pl.BlockSpec((tr, Ct), functools.partial(lambda l, r, i: (jnp.where(l == i, r, 0), 0), i=i))
               for i in range(L)]
    blk = pl.BlockSpec((None, tr, C), lambda l, r: (l, r, 0))
    return _call(
        body, name=name, grid=(L, R // tr),
        in_specs=g_specs + [blk, blk, blk],
        out_specs=[blk] * 4,
        out_shape=[jax.ShapeDtypeStruct((L, R, C), F32)] * 4,
        sem=("arbitrary", "arbitrary"), operands=(*gs, w, m, v))


_FFN1 = ("ffn1_gate", "ffn1_up", "ffn1_down")
_FFN2 = ("ffn2_gate", "ffn2_up", "ffn2_down")
_MIXW = ("w_ret_up", "w_pool_up", "w_out")
_BIG = _FFN1 + ("w_in",) + _MIXW + _FFN2
_TRANSPOSED = ("ffn1_gate", "ffn1_up", "ffn2_gate", "ffn2_up")
_SMALL = ("ffn1_norm", "mix_norm", "ffn2_norm", "final_norm", "pool_scale", "pool_maps")
_ORDER = ("meta", "ffn1_norm", "ffn1_gate", "ffn1_up", "ffn1_down", "mix_norm", "w_in", "pool_maps",
          "pool_scale", "w_ret_up", "w_pool_up", "w_out", "ffn2_norm", "ffn2_gate", "ffn2_up", "ffn2_down",
          "final_norm")


def _transport(a):
    n, r, c = a.shape
    out = a.astype(BF16)
    if c % LANES:
        out = jnp.concatenate([out, jnp.zeros((n, r, _round_up(c, LANES) - c), BF16)], axis=2)
    if r % LANES:
        out = jnp.concatenate([out, jnp.zeros((n, _round_up(r, LANES) - r, out.shape[2]), BF16)], axis=1)
    return out


def _pack_rows(parts, width):
    rows = [p.reshape(-1, width) for p in parts]
    total = sum(r.shape[0] for r in rows)
    fill = _round_up(total, 8) - total
    if fill:
        rows.append(jnp.zeros((fill, width), F32))
    return jnp.concatenate(rows, axis=0)


def _unpack_rows(packed, shapes, width):
    out, at = [], 0
    for shp in shapes:
        n = math.prod(shp) // width
        out.append(packed[at:at + n].reshape(shp))
        at += n
    return out


class _Weights:
    def __init__(self, shards):
        self.shards = shards
        self.full = {}

    def rider(self, keys):
        r = _gather_rider([(self.shards[n], i) for n, i in keys])
        r.keys = keys
        return r

    def take(self, rider):
        for key, arr in zip(rider.keys, rider.results):
            self.full[key] = arr

    def __call__(self, name, layer):
        return self.full[(name, layer)]


def _local_step(x, meta_full, tgt, w, wts, pad, tm, cg, reducer):
    D = x.shape[1]
    T = pad + N_META + x.shape[0]
    L = w["ffn1_norm"].shape[0]
    pool_maps = w["pool_maps"]
    gains = {n: w[n].reshape(L, 1, D) for n in ("ffn1_norm", "mix_norm", "ffn2_norm")}
    scale3 = w["pool_scale"].reshape(L, 1, POOL_WIDTH)
    consts = _ret_consts(T, pad)
    tl = _pick_tile(T, 2 * tm, BF16_ROWS)
    def gather(keys):
        return wts.rider(keys) if keys and keys[0] not in wts.full else None

    def done(rider):
        if rider is not None:
            wts.take(rider)

    h = jnp.concatenate([jnp.zeros((pad, D), F32), meta_full, x], axis=0)
    saved = []
    for i in range(L):
        s = {"h0": h}
        rd = gather([("w_in", i)] + [(n, i) for n in _MIXW])
        h, s["a1"], s["g1"], s["u1"], s["act1"] = _ffn_fwd(
            h, gains["ffn1_norm"], wts("ffn1_gate", i), wts("ffn1_up", i), wts("ffn1_down", i), i, tl,
            f"ffn1_fwd_{i}", rd)
        done(rd)
        s["h1"] = h
        rd = gather([("ffn2_gate", i), ("ffn2_up", i)])
        s["z"], s["b"] = _inproj_fwd(h, gains["mix_norm"], wts("w_in", i), i, tl, f"inproj_fwd_{i}", rd)
        done(rd)
        s["r"], s["o_pre"], s["s_all"] = _ret_fwd(s["z"], consts, cg, f"retention_fwd_{i}")
        s["pm"] = _pool_fwd(s["z"], pool_maps, scale3, i, pad, f"pool_fwd_{i}")
        rd = gather([("ffn2_down", i)])
        h, s["mixed"], s["ret"], s["pool"] = _mix_fwd(
            h, s["r"], s["pm"], s["z"], wts("w_ret_up", i), wts("w_pool_up", i), wts("w_out", i), tm,
            f"mix_fwd_{i}", rd)
        done(rd)
        s["h2"] = h
        rd = gather([(n, i + 1) for n in _FFN1]) if i + 1 < L else None
        h, s["a2"], s["g2"], s["u2"], s["act2"] = _ffn_fwd(
            h, gains["ffn2_norm"], wts("ffn2_gate", i), wts("ffn2_up", i), wts("ffn2_down", i), i, tl,
            f"ffn2_fwd_{i}", rd)
        done(rd)
        saved.append(s)

    dh, loss_acc, d_final = _final_loss(h, w["final_norm"].reshape(1, D), tgt, "final_norm_loss")

    small = {n: [None] * L for n in ("ffn1_norm", "mix_norm", "ffn2_norm", "pool_scale", "pool_maps")}

    carry = {"ffn_act": 1.0, "ffn_in": 2.2, "mix_bwd": 1.0, "inproj_bwd": 1.5, "w_in": 1.0}

    tk = _pick_tile(T, 1408, LANES)

    def grad(n, a, b, i, mode):
        rd = reducer.rider(carry.get(n, 1.0 if i == 0 and n.startswith("ffn") else 0.5))
        reducer.add(n, i, _grad_tn(a, b, mode, 1.0, tk, f"grad_{n}_{i}", rd))
        reducer.done(rd)

    def ffn_bwd(which, dy, h_in, g, u, i):
        rd = reducer.rider(carry["ffn_act"])
        dg, du, dyh = _ffn_bwd_act(dy, g, u, wts(f"{which}_down", i), tl, f"{which}_bwd_act_{i}", rd)
        reducer.done(rd)
        rd = reducer.rider(carry["ffn_in"])
        dh_in, dgain = _ffn_bwd_in(dy, h_in, gains[f"{which}_norm"], dg, du, wts(f"{which}_gate", i),
                                   wts(f"{which}_up", i), i, tl, pad, f"{which}_bwd_in_{i}", rd)
        reducer.done(rd)
        return dh_in, dg, du, dgain, dyh

    for i in reversed(range(L)):
        s = saved[i]
        dh, dg, du, small["ffn2_norm"][i], dyh = ffn_bwd("ffn2", dh, s["h2"], s["g2"], s["u2"], i)
        grad("ffn2_gate", dg, s["a2"], i, "row")
        grad("ffn2_up", du, s["a2"], i, "row")
        grad("ffn2_down", s["act2"], dyh, i, "row")
        reducer.stage(f"ffn2_{i}")
        rd = reducer.rider(carry["mix_bwd"])
        dgab, dret, dpool, dr, dpm = _mix_bwd_dx(
            dh, s["z"], s["ret"], s["pool"], wts("w_out", i), wts("w_ret_up", i), wts("w_pool_up", i), tm,
            f"mix_bwd_{i}", rd)
        reducer.done(rd)
        rd = reducer.rider(0.5)
        g_out, g_ru, g_pu = _grad_mix(s["mixed"], dh, s["r"], dret, s["pm"], dpool, _pick_tile(T, 704, LANES),
                                      f"grad_mix_{i}", rd)
        reducer.done(rd)
        for n, g_n in (("w_out", g_out), ("w_ret_up", g_ru), ("w_pool_up", g_pu)):
            reducer.add(n, i, g_n)
        du_pool, small["pool_maps"][i], small["pool_scale"][i] = _pool_bwd(
            s["z"], dpm, pool_maps, scale3, i, pad, f"pool_bwd_{i}")
        dq, dgr, dkp, dvp, ds = _ret_bwd_local(s["z"], s["o_pre"], s["s_all"], dr, consts, cg, f"retention_bwd_{i}")
        dk, dv = _ret_bwd_state(s["z"], dkp, dvp, ds, consts, cg, f"retention_bwd_state_{i}")
        dz = jnp.concatenate([dq, dk, dv, dgr, du_pool, dgab], axis=1)
        dh2 = dh
        rd = reducer.rider(carry["inproj_bwd"])
        dh, small["mix_norm"][i] = _inproj_bwd_dx(
            dz, wts("w_in", i), s["h1"], gains["mix_norm"], dh2, i, tl, pad, f"inproj_bwd_{i}", rd)
        reducer.done(rd)
        grad("w_in", s["b"], dz, i, "col")
        reducer.stage(f"mid{i}")
        dh, dg, du, small["ffn1_norm"][i], dyh = ffn_bwd("ffn1", dh, s["h0"], s["g1"], s["u1"], i)
        grad("ffn1_gate", dg, s["a1"], i, "row")
        if i == 0:
            reducer.stage("gate0")
        grad("ffn1_up", du, s["a1"], i, "row")
        if i == 0:
            reducer.stage("up0")
        grad("ffn1_down", s["act1"], dyh, i, "row")
        reducer.stage(f"end{i}")

    return loss_acc, dh, small, d_final


class _Reducer:
    def __init__(self, unit):
        self.c_idx = lax.axis_index("c").astype(jnp.int32).reshape(1)
        chip = 2 * lax.axis_index("x") + lax.axis_index("y")
        self.pos = jnp.stack([chip, lax.axis_index("c")]).astype(jnp.int32)
        self.pending, self.stages, self.queue, self.halves, self.whole = [], [], [], {}, {}
        self.unit = unit
        self.calls = 0

    def add(self, name, layer, g):
        self.pending.append(((name, layer), g))

    def stage(self, tag):
        if self.pending:
            self.stages.append((tag, self.pending))
            self.pending = []

    def _pair_rider(self):
        if not self.stages:
            return None
        tag, items = self.stages.pop(0)
        rd = _pair_exchange_rider([g for _, g in items])
        rd.tag, rd.keys = tag, [k for k, _ in items]
        return rd

    def _chip_rider(self, units):
        take, size = [], 0
        while self.queue and (units is None or size + self.queue[0][1].size <= units * self.unit):
            take.append(self.queue.pop(0))
            size += take[-1][1].size
        if not take:
            return None
        rd = _chip_exchange_rider([p for _, p in take])
        rd.keys = [k for k, _ in take]
        return rd

    def _gather_rider(self):
        keys = [k for k in self.halves if k not in self.whole]
        if not keys:
            return None
        rd = _pair_gather_rider([self.halves[k] for k in keys])
        rd.keys = keys
        return rd

    def rider(self, units):
        self.riding = (self._pair_rider(), self._chip_rider(units), self._gather_rider())
        return _join(self.riding)

    def done(self, rd):
        if rd is None:
            return
        _split_results(rd)
        pair, chips, gather = self.riding
        if len([r for r in self.riding if r is not None]) == 1:
            (pair or chips or gather).results = rd.results
        self.calls += 1
        if gather is not None:
            self.whole.update(zip(gather.keys, gather.results))
        if pair is not None:
            sums = _sum_pair(pair.ins, pair.results, self.c_idx, f"sum_pair_{pair.tag}")
            self.queue += list(zip(pair.keys, sums))
        if chips is not None:
            sums = _sum_chips(chips.ins, chips.results, self.pos, f"sum_chips_{self.calls}")
            self.halves.update(zip(chips.keys, sums))

    def busy(self):
        assert not self.pending
        return bool(self.stages or self.queue or len(self.whole) < len(self.halves))

    def flush(self):
        self.riding = (self._pair_rider(), self._chip_rider(None), self._gather_rider())
        rd = _join(self.riding)
        _run_rider(rd, f"grads_exchange_tail_{self.calls}")
        self.done(rd)


def _update(loss_acc, grad_x, d_meta_rows, reducer, small, d_final, w, mom, var):
    meta = w["meta"]
    D = w["final_norm"].shape[0]
    L = w["ffn1_norm"].shape[0]
    Dq = D // N_CHIPS

    out = {}

    while reducer.busy():
        reducer.flush()
    for n in _BIG:
        gs = [reducer.whole[(n, i)] for i in range(L)]
        if n in _TRANSPOSED:
            res = _adamw(gs, *(jnp.swapaxes(t[n], 1, 2) for t in (w, mom, var)), f"adamw_{n}")
            out[n] = [jnp.swapaxes(r, 1, 2) for r in res]
        else:
            out[n] = _adamw(gs, w[n], mom[n], var[n], f"adamw_{n}")

    small_parts = [jnp.concatenate(small[n], axis=0) for n in ("ffn1_norm", "mix_norm", "ffn2_norm")]
    small_parts += [d_final, jnp.concatenate(small["pool_scale"], axis=0), jnp.concatenate(small["pool_maps"], axis=0)]
    loss_row = jnp.pad(loss_acc, ((0, 0), (0, D - loss_acc.shape[1])))
    reduced = _small_all_reduce(_pack_rows(small_parts + [d_meta_rows, loss_row], D))
    small_shapes = [w[n].shape for n in _SMALL]
    small_rows = sum(math.prod(shp) for shp in small_shapes) // D
    chip = 2 * lax.axis_index("x") + lax.axis_index("y")
    d_meta = lax.dynamic_slice_in_dim(reduced[small_rows:small_rows + N_META], chip * Dq, Dq, axis=1)
    names = _SMALL + ("meta",)
    packed_g = _pack_rows([reduced[:small_rows], d_meta], D)
    packed = [_pack_rows([t[n] for n in names], D) for t in (w, mom, var)]
    res = _adamw([packed_g], packed[0][None], packed[1][None], packed[2][None], "adamw_small")
    shapes = small_shapes + [meta.shape]
    unpacked = [_unpack_rows(r[0], shapes, D) for r in res]
    for k, n in enumerate(names):
        out[n] = tuple(u[k] for u in unpacked)

    loss = reduced[small_rows + N_META, 0]
    return (loss, grad_x) + tuple(out[n][j] for j in range(4) for n in _ORDER)


def kernel(x, meta, ffn1_norm, ffn1_gate, ffn1_up, ffn1_down, mix_norm, w_in, pool_maps, pool_scale, w_ret_up, w_pool_up, w_out, ffn2_norm, ffn2_gate, ffn2_up, ffn2_down, final_norm, loss_target, m_meta, m_ffn1_norm, m_ffn1_gate, m_ffn1_up, m_ffn1_down, m_mix_norm, m_w_in, m_pool_maps, m_pool_scale, m_w_ret_up, m_w_pool_up, m_w_out, m_ffn2_norm, m_ffn2_gate, m_ffn2_up, m_ffn2_down, m_final_norm, v_meta, v_ffn1_norm, v_ffn1_gate, v_ffn1_up, v_ffn1_down, v_mix_norm, v_w_in, v_pool_maps, v_pool_scale, v_w_ret_up, v_w_pool_up, v_w_out, v_ffn2_norm, v_ffn2_gate, v_ffn2_up, v_ffn2_down, v_final_norm):
    args = dict(locals())
    w = {n: args[n] for n in _ORDER}
    mom = {n: args["m_" + n] for n in _ORDER}
    var = {n: args["v_" + n] for n in _ORDER}

    assert x.shape[0] == 1, "one batch element per device"
    seq, D = x.shape[1], x.shape[2]
    assert seq % CHUNK == 0 and D % RET_WIDTH == 0 and (2 * POOL_WIDTH) % D == 0
    pad = (-(seq + N_META)) % CHUNK
    T = seq + N_META + pad
    tm = _pick_tile(T, 528, BF16_ROWS)
    cg = _pick_tile(T // CHUNK, 11, 1)

    shards = {n: _transport(w[n]) for n in _BIG}
    shards["meta"] = meta[None]
    wts = _Weights(shards)
    head = wts.rider([(n, 0) for n in _FFN1] + [("meta", 0)])
    _run_rider(head, "weights_gather_head")
    wts.take(head)
    meta_full = jnp.transpose(wts("meta", 0), (1, 0, 2)).reshape(N_META, D)

    reducer = _Reducer(unit=2 * shards["ffn1_gate"][0].size)
    loss_acc, dh, small, d_final = _local_step(x[0], meta_full, loss_target[0], w, wts, pad, tm, cg, reducer)
    grad_x = dh[pad + N_META:][None]
    return _update(loss_acc, grad_x, dh[pad:pad + N_META], reducer, small, d_final, w, mom, var)
```

```python
import functools
import math

import jax
import jax.numpy as jnp
from jax import lax
from jax.experimental import pallas as pl
from jax.experimental.pallas import tpu as pltpu

F32 = jnp.float32
BF16 = jnp.bfloat16

N_META = 16
RET_HEADS = 4
HEAD_DIM = 128
RET_WIDTH = RET_HEADS * HEAD_DIM
POOL_WINDOWS = (2, 4, 8, 16)
POOL_GROUPS = len(POOL_WINDOWS)
POOL_WIDTH = POOL_GROUPS * HEAD_DIM
CHUNK = 128
ROPE_BASE = 10000.0
EPS = 1e-6
ADAM_LR = 0.001
ADAM_B1 = 0.9
ADAM_B2 = 0.999
ADAM_EPS = 1e-08
ADAM_WD = 0.01
ADAM_STEP = 10

N_CHIPS = 4
LANES = 128
BF16_ROWS = 16
V7X_VMEM_LIMIT = 52 * 1024 * 1024
MESH = pl.DeviceIdType.MESH
ANY = pl.BlockSpec(memory_space=pl.ANY)


def _round_up(n, m):
    return -(-n // m) * m


def _pick_tile(n, target, mult):
    best = None
    for d in range(mult, min(n, target) + 1, mult):
        if n % d == 0:
            best = d
    assert best is not None, (n, target, mult)
    return best


def _params(sem=None):
    return pltpu.CompilerParams(dimension_semantics=sem, vmem_limit_bytes=V7X_VMEM_LIMIT)


def _dot(a, b):
    return jnp.dot(a, b, preferred_element_type=F32)


def _dot_nt(a, b):
    return lax.dot_general(a, b, (((1,), (1,)), ((), ())), preferred_element_type=F32)


def _dot_tn(a, b):
    return lax.dot_general(a, b, (((0,), (0,)), ((), ())), preferred_element_type=F32)


def _ein(spec, a, b):
    return jnp.einsum(spec, a, b, preferred_element_type=F32)


def _sigmoid(x):
    return jax.nn.sigmoid(x)


def _rms_fwd(x, gain):
    r = lax.rsqrt(jnp.mean(x * x, axis=-1, keepdims=True) + EPS)
    return x * r * gain


def _rms_bwd(x, gain, da):
    r = lax.rsqrt(jnp.mean(x * x, axis=-1, keepdims=True) + EPS)
    xh = x * r
    dgain = jnp.sum(da * xh, axis=0, keepdims=True)
    dxh = da * gain
    dx = r * (dxh - xh * jnp.mean(dxh * xh, axis=-1, keepdims=True))
    return dx, dgain


def _row_mask(t, tm, pad, shape):
    rows = t * tm + lax.broadcasted_iota(jnp.int32, shape, 0)
    return rows >= pad


def _mesh_pos():
    x, y, c = lax.axis_index("x"), lax.axis_index("y"), lax.axis_index("c")
    others = [(1 - x, y), (x, 1 - y), (1 - x, 1 - y)]
    return x, y, c, 2 * x + y, others


def _half_rows(c, rh):
    return pl.ds(pl.multiple_of(c * rh, rh), rh)


def _remote(src, dst, ssem, rsem, dev):
    return pltpu.make_async_remote_copy(src_ref=src, dst_ref=dst, send_sem=ssem, recv_sem=rsem,
                                        device_id=dev, device_id_type=MESH)


class _Rider:
    def __init__(self, ins, out_shapes, n_sem, start, finish, in_place=False):
        self.ins, self.out_shapes, self.n_sem, self.start, self.finish = ins, out_shapes, n_sem, start, finish
        self.in_place = [in_place] * len(ins)
        self.results = None

    def aliases(self, first_in, first_out):
        return {first_in + i: first_out + i for i, same in enumerate(self.in_place) if same}


class _SemWindow:
    def __init__(self, ref, base):
        self.ref, self.base = ref, base

    @property
    def at(self):
        return self

    def __getitem__(self, k):
        return self.ref.at[self.base + k]


def _join(riders):
    riders = [r for r in riders if r is not None]
    if len(riders) <= 1:
        return riders[0] if riders else None

    def run(which):
        def go(ins, outs, ssem, rsem):
            at, sem = 0, 0
            for r in riders:
                n = len(r.ins)
                getattr(r, which)(ins[at:at + n], outs[at:at + n], _SemWindow(ssem, sem), _SemWindow(rsem, sem))
                at, sem = at + n, sem + r.n_sem
        return go

    joined = _Rider(sum([list(r.ins) for r in riders], []), sum([list(r.out_shapes) for r in riders], []),
                    sum(r.n_sem for r in riders), run("start"), run("finish"))
    joined.in_place = sum([r.in_place for r in riders], [])
    joined.parts = riders
    return joined


def _split_results(rider):
    at = 0
    for r in getattr(rider, "parts", []):
        r.results = rider.results[at:at + len(r.ins)]
        at += len(r.ins)


def _gather_rider(pieces):
    per = 7
    layers = [layer for _, layer in pieces]

    def first_copies(ins, outs, ssem, rsem):
        x, y, c, chip, others = _mesh_pos()
        copies = []
        for i, layer in enumerate(layers):
            mine = _half_rows(c, ins[i].shape[1] // 2)
            for j, (ox, oy) in enumerate(others):
                copies.append(_remote(ins[i].at[layer, mine, :], outs[i].at[chip, mine, :],
                                      ssem.at[per * i + j], rsem.at[per * i + j], (ox, oy, c)))
            copies.append(_remote(ins[i].at[layer], outs[i].at[chip],
                                  ssem.at[per * i + 6], rsem.at[per * i + 6], (x, y, 1 - c)))
        return copies

    def start(ins, outs, ssem, rsem):
        for cp in first_copies(ins, outs, ssem, rsem):
            cp.start()

    def finish(ins, outs, ssem, rsem):
        x, y, c, chip, others = _mesh_pos()
        sibling = (x, y, 1 - c)
        forwards = []
        for i in range(len(layers)):
            mine = _half_rows(c, ins[i].shape[1] // 2)
            for j, (ox, oy) in enumerate(others):
                rows = outs[i].at[2 * ox + oy, mine, :]
                _remote(rows, rows, ssem.at[per * i + j], rsem.at[per * i + j], (ox, oy, c)).wait_recv()
                fwd = _remote(rows, rows, ssem.at[per * i + 3 + j], rsem.at[per * i + 3 + j], sibling)
                fwd.start()
                forwards.append(fwd)
        for i in range(len(layers)):
            theirs = _half_rows(1 - c, ins[i].shape[1] // 2)
            for j, (ox, oy) in enumerate(others):
                rows = outs[i].at[2 * ox + oy, theirs, :]
                _remote(rows, rows, ssem.at[per * i + 3 + j], rsem.at[per * i + 3 + j], sibling).wait_recv()
            own = outs[i].at[chip]
            _remote(own, own, ssem.at[per * i + 6], rsem.at[per * i + 6], sibling).wait_recv()
        for cp in first_copies(ins, outs, ssem, rsem) + forwards:
            cp.wait_send()

    shapes = [jax.ShapeDtypeStruct((N_CHIPS,) + s.shape[1:], s.dtype) for s, _ in pieces]
    return _Rider([s for s, _ in pieces], shapes, per * len(pieces), start, finish)


def _chip_exchange_rider(ps):
    def copies(ins, outs, ssem, rsem):
        x, y, c, chip, others = _mesh_pos()
        return [_remote(ins[i].at[2 * ox + oy], outs[i].at[chip], ssem.at[3 * i + j], rsem.at[3 * i + j], (ox, oy, c))
                for i in range(len(ps)) for j, (ox, oy) in enumerate(others)]

    def start(ins, outs, ssem, rsem):
        for cp in copies(ins, outs, ssem, rsem):
            cp.start()

    def finish(ins, outs, ssem, rsem):
        x, y, c, chip, others = _mesh_pos()
        for i in range(len(ps)):
            for j, (ox, oy) in enumerate(others):
                slot = outs[i].at[2 * ox + oy]
                _remote(slot, slot, ssem.at[3 * i + j], rsem.at[3 * i + j], (ox, oy, c)).wait_recv()
        for cp in copies(ins, outs, ssem, rsem):
            cp.wait_send()

    return _Rider(list(ps), [jax.ShapeDtypeStruct(p.shape, p.dtype) for p in ps], 3 * len(ps), start, finish)


def _pair_exchange_rider(gs):
    def copies(ins, outs, ssem, rsem):
        x, y, c, _, _ = _mesh_pos()
        return [_remote(ins[i].at[:, _half_rows(1 - c, ins[i].shape[1] // 2), :], outs[i],
                        ssem.at[i], rsem.at[i], (x, y, 1 - c)) for i in range(len(gs))]

    def start(ins, outs, ssem, rsem):
        for cp in copies(ins, outs, ssem, rsem):
            cp.start()

    def finish(ins, outs, ssem, rsem):
        for cp in copies(ins, outs, ssem, rsem):
            cp.wait()

    shapes = [jax.ShapeDtypeStruct((g.shape[0], g.shape[1] // 2, g.shape[2]), g.dtype) for g in gs]
    return _Rider(list(gs), shapes, len(gs), start, finish)


def _run_rider(rider, name):
    def body(*refs):
        n = len(rider.ins)
        ins, outs = refs[:n], refs[n:2 * n]
        ssem, rsem = refs[2 * n:]
        rider.start(ins, outs, ssem, rsem)
        rider.finish(ins, outs, ssem, rsem)

    rider.results = pl.pallas_call(
        body,
        name=name,
        in_specs=[ANY] * len(rider.ins),
        out_specs=[ANY] * len(rider.ins),
        out_shape=rider.out_shapes,
        input_output_aliases=rider.aliases(0, 0),
        scratch_shapes=[pltpu.SemaphoreType.DMA((rider.n_sem,)), pltpu.SemaphoreType.DMA((rider.n_sem,))],
    )(*rider.ins)
    return rider.results


def _pair_gather_rider(fs):
    n = len(fs)

    def copies(outs, ssem, rsem):
        x, y, c, _, _ = _mesh_pos()
        halves = [outs[i].at[_half_rows(c, outs[i].shape[0] // 2), :] for i in range(n)]
        return [_remote(h, h, ssem.at[i], rsem.at[i], (x, y, 1 - c)) for i, h in enumerate(halves)]

    def start(ins, outs, ssem, rsem):
        for cp in copies(outs, ssem, rsem):
            cp.start()

    def finish(ins, outs, ssem, rsem):
        x, y, c, _, _ = _mesh_pos()
        for i in range(n):
            theirs = outs[i].at[_half_rows(1 - c, outs[i].shape[0] // 2), :]
            _remote(theirs, theirs, ssem.at[i], rsem.at[i], (x, y, 1 - c)).wait_recv()
        for cp in copies(outs, ssem, rsem):
            cp.wait_send()

    return _Rider(list(fs), [jax.ShapeDtypeStruct(f.shape, f.dtype) for f in fs], n, start, finish, in_place=True)


def _call(body, *, name, grid, in_specs, out_specs, out_shape, operands, scratch=(), sem=None, rider=None):
    if rider is None:
        return pl.pallas_call(
            body, name=name, grid=grid, in_specs=in_specs, out_specs=out_specs, out_shape=out_shape,
            scratch_shapes=list(scratch), compiler_params=_params(sem))(*operands)
    n_in, n_out, n_sc, r = len(in_specs), len(out_specs), len(scratch), len(rider.ins)

    def carrying(*refs):
        a, b = n_in, n_in + r
        c, d = b + n_out, b + n_out + r
        e = d + n_sc
        ids = [pl.program_id(k) for k in range(len(grid))]
        first = functools.reduce(jnp.logical_and, [i == 0 for i in ids])
        last = functools.reduce(jnp.logical_and, [i == g - 1 for i, g in zip(ids, grid)])

        @pl.when(first)
        def _():
            rider.start(refs[a:b], refs[c:d], refs[e], refs[e + 1])

        body(*refs[:a], *refs[b:c], *refs[d:e])

        @pl.when(last)
        def _():
            rider.finish(refs[a:b], refs[c:d], refs[e], refs[e + 1])

    outs = pl.pallas_call(
        carrying, name=name, grid=grid,
        in_specs=list(in_specs) + [ANY] * r,
        out_specs=list(out_specs) + [ANY] * r,
        out_shape=list(out_shape) + list(rider.out_shapes),
        scratch_shapes=list(scratch) + [pltpu.SemaphoreType.DMA((rider.n_sem,)), pltpu.SemaphoreType.DMA((rider.n_sem,))],
        input_output_aliases=rider.aliases(n_in, n_out),
        compiler_params=_params(("arbitrary",) * len(grid)),
    )(*operands, *rider.ins)
    rider.results = outs[n_out:]
    return outs[:n_out]


def _ffn_fwd(h, gain, wg, wu, wd, layer, tm, name, rider=None):
    T, D = h.shape
    Fs = wg.shape[-1]
    F = N_CHIPS * Fs

    def body(h_ref, g_ref, wg_ref, wu_ref, wd_ref, ho_ref, a_ref, go_ref, uo_ref, act_ref, acc_ref):
        s = pl.program_id(1)

        @pl.when(s == 0)
        def _():
            a_ref[...] = _rms_fwd(h_ref[...], g_ref[...]).astype(BF16)
            acc_ref[...] = jnp.zeros_like(acc_ref)

        a = a_ref[...]
        g = _dot(a, wg_ref[...])
        u = _dot(a, wu_ref[...])
        sg = _sigmoid(g)
        act = (g * sg * u).astype(BF16)
        go_ref[...] = (u * (sg * (1.0 + g * (1.0 - sg)))).astype(BF16)
        uo_ref[...] = (g * sg).astype(BF16)
        act_ref[...] = act
        acc_ref[...] += _dot(act, wd_ref[...])

        @pl.when(s == N_CHIPS - 1)
        def _():
            ho_ref[...] = h_ref[...] + 0.5 * acc_ref[...]

    row = pl.BlockSpec((tm, D), lambda t, s: (t, 0))
    col = pl.BlockSpec((tm, Fs), lambda t, s: (t, s))
    wcol = pl.BlockSpec((None, D, Fs), lambda t, s: (s, 0, 0))
    return _call(
        body, name=name, grid=(T // tm, N_CHIPS),
        in_specs=[row, pl.BlockSpec((None, 1, D), lambda t, s: (layer, 0, 0)), wcol, wcol,
                  pl.BlockSpec((None, Fs, D), lambda t, s: (s, 0, 0))],
        out_specs=[row, row, col, col, col],
        out_shape=[jax.ShapeDtypeStruct((T, D), F32), jax.ShapeDtypeStruct((T, D), BF16)]
        + [jax.ShapeDtypeStruct((T, F), BF16)] * 3,
        scratch=[pltpu.VMEM((tm, D), F32)],
        sem=("parallel", "arbitrary"), operands=(h, gain, wg, wu, wd), rider=rider)


def _inproj_fwd(h, gain, win, layer, tm, name, rider=None):
    T, D = h.shape
    Ns = win.shape[-1]

    def body(h_ref, g_ref, w_ref, z_ref, b_ref):
        @pl.when(pl.program_id(1) == 0)
        def _():
            b_ref[...] = _rms_fwd(h_ref[...], g_ref[...]).astype(BF16)

        z_ref[...] = _dot(b_ref[...], w_ref[...]).astype(BF16)

    return _call(
        body, name=name, grid=(T // tm, N_CHIPS),
        in_specs=[pl.BlockSpec((tm, D), lambda t, s: (t, 0)),
                  pl.BlockSpec((None, 1, D), lambda t, s: (layer, 0, 0)),
                  pl.BlockSpec((None, D, Ns), lambda t, s: (s, 0, 0))],
        out_specs=[pl.BlockSpec((tm, Ns), lambda t, s: (t, s)), pl.BlockSpec((tm, D), lambda t, s: (t, 0))],
        out_shape=[jax.ShapeDtypeStruct((T, N_CHIPS * Ns), BF16), jax.ShapeDtypeStruct((T, D), BF16)],
        sem=("parallel", "arbitrary"), operands=(h, gain, win), rider=rider)


def _ret_consts(T, pad):
    half = HEAD_DIM // 2
    inv_freq = ROPE_BASE ** (-jnp.arange(half, dtype=F32) / half)
    pos = jnp.arange(T, dtype=F32) - pad
    ang = pos[:, None] * inv_freq[None, :]
    cos = jnp.cos(ang)
    sin = jnp.sin(ang)
    cosf = jnp.concatenate([cos, cos], axis=1)
    sinf = jnp.concatenate([-sin, sin], axis=1)
    log_gamma = jnp.log1p(-(2.0 ** (-5.0 - jnp.arange(RET_HEADS, dtype=F32))))
    idx = jnp.arange(CHUNK, dtype=F32)
    diff = idx[:, None] - idx[None, :]
    intra = jnp.where(diff[None] >= 0, jnp.exp(diff[None] * log_gamma[:, None, None]), 0.0)
    k_decay = jnp.exp((CHUNK - 1.0 - idx)[None, :] * log_gamma[:, None])
    q_decay = jnp.exp((idx + 1.0)[None, :] * log_gamma[:, None])
    chunk_decay = jnp.exp(CHUNK * log_gamma)
    kdec = jnp.broadcast_to(k_decay[:, :, None], (RET_HEADS, CHUNK, HEAD_DIM))
    qdec = jnp.broadcast_to(q_decay[:, :, None], (RET_HEADS, CHUNK, HEAD_DIM))
    cdb = jnp.broadcast_to(chunk_decay[:, None, None], (RET_HEADS, 8, HEAD_DIM))
    return cosf, sinf, intra, kdec, qdec, cdb


def _rot(t, cosv, sinv):
    return t * cosv + pltpu.roll(t, HEAD_DIM // 2, 1) * sinv


def _rot_t(g, cosv, sinv):
    return g * cosv + pltpu.roll(g * sinv, HEAD_DIM // 2, 1)


def _head_specs(tg, section, order):
    return pl.BlockSpec((tg, HEAD_DIM), lambda h, g: (order(g), section * RET_HEADS + h))


def _ret_fwd(z, consts, cg, name, rider=None):
    T = z.shape[0]
    N = T // CHUNK
    ng = N // cg
    tg = cg * CHUNK
    cosf, sinf, intra, kdec, qdec, cdb = consts
    fwd = lambda g: g

    def body(zq, zk, zv, zg, cos_ref, sin_ref, m_ref, kd_ref, qd_ref, cd_ref, r_ref, o_ref, s_ref, st_ref):
        @pl.when(pl.program_id(1) == 0)
        def _():
            st_ref[...] = jnp.zeros_like(st_ref)

        cosv = cos_ref[...]
        sinv = sin_ref[...]
        q3 = (_rot(zq[...].astype(F32), cosv, sinv) * (HEAD_DIM ** -0.5)).reshape(cg, CHUNK, HEAD_DIM)
        k3 = _rot(zk[...].astype(F32), cosv, sinv).reshape(cg, CHUNK, HEAD_DIM)
        vb = zv[...].reshape(cg, CHUNK, HEAD_DIM).astype(BF16)
        scores = _ein("ncd,nmd->ncm", q3.astype(BF16), k3.astype(BF16)) * m_ref[...][None]
        inner = _ein("ncm,nmd->ncd", scores.astype(BF16), vb)
        kv = _ein("ncd,nce->nde", (k3 * kd_ref[...][None]).astype(BF16), vb)
        cd = cd_ref[0:1, :]
        state = st_ref[...]
        for n in range(cg):
            s_ref[n] = state
            state = state * cd + kv[n]
        st_ref[...] = state
        qdb = (q3 * qd_ref[...][None]).astype(BF16)
        cross = _ein("ncd,nde->nce", qdb, s_ref[...].astype(BF16))
        out = (inner + cross).reshape(tg, HEAD_DIM)
        o_ref[...] = out
        xc = out - jnp.mean(out, axis=-1, keepdims=True)
        rn = xc * lax.rsqrt(jnp.mean(xc * xc, axis=-1, keepdims=True) + EPS)
        g = zg[...].astype(F32)
        r_ref[...] = (rn * (g * _sigmoid(g))).astype(BF16)

    tab = pl.BlockSpec((tg, HEAD_DIM), lambda h, g: (g, 0))
    per_head = lambda rows: pl.BlockSpec((None, rows, HEAD_DIM), lambda h, g: (h, 0, 0))
    head_out = pl.BlockSpec((tg, HEAD_DIM), lambda h, g: (g, h))
    return _call(
        body, name=name, grid=(RET_HEADS, ng),
        in_specs=[_head_specs(tg, i, fwd) for i in range(4)]
        + [tab, tab, per_head(CHUNK), per_head(CHUNK), per_head(CHUNK), per_head(8)],
        out_specs=[head_out, head_out, pl.BlockSpec((None, cg, HEAD_DIM, HEAD_DIM), lambda h, g: (h, g, 0, 0))],
        out_shape=[jax.ShapeDtypeStruct((T, RET_WIDTH), BF16), jax.ShapeDtypeStruct((T, RET_WIDTH), F32),
                   jax.ShapeDtypeStruct((RET_HEADS, N, HEAD_DIM, HEAD_DIM), F32)],
        scratch=[pltpu.VMEM((HEAD_DIM, HEAD_DIM), F32)],
        sem=("parallel", "arbitrary"), operands=(z, z, z, z, cosf, sinf, intra, kdec, qdec, cdb), rider=rider)


def _window_sums(u, shift_of):
    sums = []
    s = u
    k = 1
    while k < POOL_WINDOWS[-1]:
        s = s + pltpu.roll(s, shift_of(k), 0)
        sums.append(s)
        k *= 2
    return sums


def _select_group(vals, g):
    out = vals[-1]
    for i in range(len(vals) - 2, -1, -1):
        out = jnp.where(g == i, vals[i], out)
    return out


def _pool_parts(u, g, T, pad):
    rows = lax.broadcasted_iota(jnp.int32, (T, HEAD_DIM), 0)
    valid = rows >= pad
    win = _select_group([float(w) for w in POOL_WINDOWS], g)
    div = jnp.clip((rows - pad + 1).astype(F32), 1.0, win)
    s = _select_group(_window_sums(u, lambda k: k), g)
    pooled = jnp.where(valid, s / div - u, 0.0)
    return pooled, div, valid


def _pool_specs(T, layer):
    first = 4 * RET_WIDTH // HEAD_DIM
    return [
        pl.BlockSpec((T, HEAD_DIM), lambda g: (0, first + g)),
        pl.BlockSpec((None, None, HEAD_DIM, HEAD_DIM), lambda g: (layer, g, 0, 0)),
        pl.BlockSpec((None, 1, HEAD_DIM), lambda g: (layer, 0, g)),
    ]


def _pool_fwd(z, maps, scale, layer, pad, name):
    T = z.shape[0]
    assert pad >= POOL_WINDOWS[-1], "window rolls wrap into the zero rows in front"

    def body(zu, maps_ref, sc_ref, pm_ref):
        g = pl.program_id(0)
        pooled, _, _ = _pool_parts(zu[...].astype(F32), g, T, pad)
        y = _dot(pooled.astype(BF16), maps_ref[...].astype(BF16))
        pm_ref[...] = (y * sc_ref[...]).astype(BF16)

    return _call(
        body, name=name, grid=(POOL_GROUPS,),
        in_specs=_pool_specs(T, layer),
        out_specs=[pl.BlockSpec((T, HEAD_DIM), lambda g: (0, g))],
        out_shape=[jax.ShapeDtypeStruct((T, POOL_WIDTH), BF16)],
        sem=("parallel",), operands=(z, maps, scale))[0]


def _gate_specs(tm, D):
    nb = D // RET_WIDTH
    first = (4 * RET_WIDTH + POOL_WIDTH) // RET_WIDTH
    return [pl.BlockSpec((tm, RET_WIDTH), functools.partial(lambda t, j: (t, j), j=first + j)) for j in range(2 * nb)]


def _load_gates(refs, nb):
    ga = jnp.concatenate([r[...].astype(F32) for r in refs[:nb]], axis=1)
    gb = jnp.concatenate([r[...].astype(F32) for r in refs[nb:]], axis=1)
    return ga, gb


def _mix_fwd(h, r, pm, z, wru, wpu, wout, tm, name, rider=None):
    T, D = h.shape
    Dq = D // N_CHIPS
    nb = D // RET_WIDTH

    def body(*refs):
        h_ref, r_ref, pm_ref = refs[:3]
        gate_refs = refs[3:3 + 2 * nb]
        wru_ref, wpu_ref, wout_ref, ho_ref, mx_ref, ret_ref, pool_ref = refs[3 + 2 * nb:]
        rv = r_ref[...]
        pv = pm_ref[...]
        ret = jnp.concatenate([_dot(rv, wru_ref[s]) for s in range(N_CHIPS)], axis=1)
        pool = jnp.concatenate([_dot(pv, wpu_ref[s]) for s in range(N_CHIPS)], axis=1)
        ga, gb = _load_gates(gate_refs, nb)
        mixed = (_sigmoid(ga) * ret + _sigmoid(gb) * pool).astype(BF16)
        mx_ref[...] = mixed
        ret_ref[...] = ret.astype(BF16)
        pool_ref[...] = pool.astype(BF16)
        ho_ref[...] = h_ref[...] + _dot(mixed, wout_ref[...].reshape(D, D))

    row = pl.BlockSpec((tm, D), lambda t: (t, 0))
    half = pl.BlockSpec((tm, RET_WIDTH), lambda t: (t, 0))
    up = pl.BlockSpec((N_CHIPS, RET_WIDTH, Dq), lambda t: (0, 0, 0))
    return _call(
        body, name=name, grid=(T // tm,),
        in_specs=[row, half, half] + _gate_specs(tm, D) + [up, up, pl.BlockSpec((N_CHIPS, Dq, D), lambda t: (0, 0, 0))],
        out_specs=[row, row, row, row],
        out_shape=[jax.ShapeDtypeStruct((T, D), F32)] + [jax.ShapeDtypeStruct((T, D), BF16)] * 3,
        sem=("parallel",), operands=(h, r, pm, *([z] * (2 * nb)), wru, wpu, wout), rider=rider)


def _final_loss(h, gain, tgt, name):
    T, D = h.shape
    first = (T - tgt.shape[0]) // CHUNK

    def body(h_ref, g_ref, t_ref, dh_ref, loss_ref, dg_ref):
        i = pl.program_id(0)

        @pl.when(i == 0)
        def _():
            loss_ref[...] = jnp.zeros_like(loss_ref)
            dg_ref[...] = jnp.zeros_like(dg_ref)

        x = h_ref[...]
        gain_v = g_ref[...]
        err = jnp.where(i >= first, _rms_fwd(x, gain_v) - t_ref[...], 0.0)
        loss_ref[...] += 0.5 * jnp.sum(jnp.mean(err * err, axis=-1))
        dx, dgain = _rms_bwd(x, gain_v, err * (1.0 / D))
        dg_ref[...] += dgain
        dh_ref[...] = dx

    return _call(
        body, name=name, grid=(T // CHUNK,),
        in_specs=[pl.BlockSpec((CHUNK, D), lambda i: (i, 0)),
                  pl.BlockSpec((1, D), lambda i: (0, 0)),
                  pl.BlockSpec((CHUNK, D), lambda i: (jnp.maximum(i - first, 0), 0))],
        out_specs=[pl.BlockSpec((CHUNK, D), lambda i: (i, 0)),
                   pl.BlockSpec((1, LANES), lambda i: (0, 0)),
                   pl.BlockSpec((1, D), lambda i: (0, 0))],
        out_shape=[jax.ShapeDtypeStruct((T, D), F32), jax.ShapeDtypeStruct((1, LANES), F32),
                   jax.ShapeDtypeStruct((1, D), F32)],
        sem=("arbitrary",), operands=(h, gain, tgt))


def _ffn_bwd_act(dy, g, u, wd, tm, name, rider=None):
    T, D = dy.shape
    Fs = wd.shape[1]
    F = N_CHIPS * Fs

    def body(dy_ref, go_ref, uo_ref, wd_ref, dg_ref, du_ref, dyh_ref):
        @pl.when(pl.program_id(1) == 0)
        def _():
            dyh_ref[...] = (0.5 * dy_ref[...]).astype(BF16)

        dact = _dot_nt(dyh_ref[...], wd_ref[...])
        du_ref[...] = (dact * uo_ref[...].astype(F32)).astype(BF16)
        dg_ref[...] = (dact * go_ref[...].astype(F32)).astype(BF16)

    row = pl.BlockSpec((tm, D), lambda t, s: (t, 0))
    col = pl.BlockSpec((tm, Fs), lambda t, s: (t, s))
    return _call(
        body, name=name, grid=(T // tm, N_CHIPS),
        in_specs=[row, col, col, pl.BlockSpec((None, Fs, D), lambda t, s: (s, 0, 0))],
        out_specs=[col, col, row],
        out_shape=[jax.ShapeDtypeStruct((T, F), BF16), jax.ShapeDtypeStruct((T, F), BF16),
                   jax.ShapeDtypeStruct((T, D), BF16)],
        sem=("parallel", "arbitrary"), operands=(dy, g, u, wd), rider=rider)


def _ffn_bwd_in(dy, h, gain, dg, du, wg, wu, layer, tm, pad, name, rider=None):
    T, D = h.shape
    Fs = wg.shape[-1]

    def body(dy_ref, h_ref, g_ref, dg_ref, du_ref, wg_ref, wu_ref, dh_ref, dgain_ref, da_ref):
        t = pl.program_id(0)
        s = pl.program_id(1)

        @pl.when((t == 0) & (s == 0))
        def _():
            dgain_ref[...] = jnp.zeros_like(dgain_ref)

        @pl.when(s == 0)
        def _():
            da_ref[...] = jnp.zeros_like(da_ref)

        da_ref[...] += _dot_nt(dg_ref[...], wg_ref[...]) + _dot_nt(du_ref[...], wu_ref[...])

        @pl.when(s == N_CHIPS - 1)
        def _():
            dx, dgain = _rms_bwd(h_ref[...], g_ref[...], da_ref[...])
            dgain_ref[...] += dgain
            dh_ref[...] = jnp.where(_row_mask(t, tm, pad, (tm, D)), dy_ref[...] + dx, 0.0)

    row = pl.BlockSpec((tm, D), lambda t, s: (t, 0))
    col = pl.BlockSpec((tm, Fs), lambda t, s: (t, s))
    wcol = pl.BlockSpec((None, D, Fs), lambda t, s: (s, 0, 0))
    return _call(
        body, name=name, grid=(T // tm, N_CHIPS),
        in_specs=[row, row, pl.BlockSpec((None, 1, D), lambda t, s: (layer, 0, 0)), col, col, wcol, wcol],
        out_specs=[row, pl.BlockSpec((1, D), lambda t, s: (0, 0))],
        out_shape=[jax.ShapeDtypeStruct((T, D), F32), jax.ShapeDtypeStruct((1, D), F32)],
        scratch=[pltpu.VMEM((tm, D), F32)],
        sem=("arbitrary", "arbitrary"), operands=(dy, h, gain, dg, du, wg, wu), rider=rider)


def _grad_tn(a, b, mode, scale, tm, name, rider=None):
    T = a.shape[0]
    if mode == "col":
        per, R, C = 1, a.shape[1], b.shape[1] // N_CHIPS
        a_spec = pl.BlockSpec((tm, R), lambda s, t: (t, 0))
        b_spec = pl.BlockSpec((tm, C), lambda s, t: (t, s))
    else:
        per, R, C = 2, a.shape[1] // N_CHIPS, b.shape[1]
        a_spec = pl.BlockSpec((tm, per * R), lambda s, t: (t, s))
        b_spec = pl.BlockSpec((tm, C), lambda s, t: (t, 0))
    nt = T // tm

    def body(a_ref, b_ref, o_ref, acc_ref):
        t = pl.program_id(1)

        @pl.when(t == 0)
        def _():
            acc_ref[...] = jnp.zeros_like(acc_ref)

        acc_ref[...] += _dot_tn(a_ref[...].astype(BF16), b_ref[...].astype(BF16))

        @pl.when(t == nt - 1)
        def _():
            o_ref[...] = (scale * acc_ref[...]).astype(BF16).reshape(per, R, C)

    return _call(
        body, name=name, grid=(N_CHIPS // per, nt),
        in_specs=[a_spec, b_spec],
        out_specs=[pl.BlockSpec((per, R, C), lambda s, t: (s, 0, 0))],
        out_shape=[jax.ShapeDtypeStruct((N_CHIPS, R, C), BF16)],
        scratch=[pltpu.VMEM((per * R, C), F32)],
        sem=("parallel", "arbitrary"), operands=(a, b), rider=rider)[0]


def _grad_mix(mixed, dh, r, dret, pm, dpool, tk, name, rider=None):
    T, D = dh.shape
    Dq = D // N_CHIPS
    nt = T // tk

    def body(mx_ref, dh_ref, r_ref, dret_ref, pm_ref, dpool_ref, go_ref, gr_ref, gp_ref, ao_ref, ar_ref, ap_ref):
        t = pl.program_id(0)

        @pl.when(t == 0)
        def _():
            ao_ref[...] = jnp.zeros_like(ao_ref)
            ar_ref[...] = jnp.zeros_like(ar_ref)
            ap_ref[...] = jnp.zeros_like(ap_ref)

        ao_ref[...] += _dot_tn(mx_ref[...], dh_ref[...].astype(BF16))
        ar_ref[...] += _dot_tn(r_ref[...], dret_ref[...])
        ap_ref[...] += _dot_tn(pm_ref[...], dpool_ref[...])

        @pl.when(t == nt - 1)
        def _():
            go_ref[...] = ao_ref[...].astype(BF16).reshape(N_CHIPS, Dq, D)
            for s in range(N_CHIPS):
                gr_ref[s] = ar_ref[:, s * Dq:(s + 1) * Dq].astype(BF16)
                gp_ref[s] = ap_ref[:, s * Dq:(s + 1) * Dq].astype(BF16)

    row = pl.BlockSpec((tk, D), lambda t: (t, 0))
    half = pl.BlockSpec((tk, RET_WIDTH), lambda t: (t, 0))
    whole = lambda shape: pl.BlockSpec(shape, lambda t: (0, 0, 0))
    return _call(
        body, name=name, grid=(nt,),
        in_specs=[row, row, half, row, half, row],
        out_specs=[whole((N_CHIPS, Dq, D)), whole((N_CHIPS, RET_WIDTH, Dq)), whole((N_CHIPS, POOL_WIDTH, Dq))],
        out_shape=[jax.ShapeDtypeStruct((N_CHIPS, Dq, D), BF16),
                   jax.ShapeDtypeStruct((N_CHIPS, RET_WIDTH, Dq), BF16),
                   jax.ShapeDtypeStruct((N_CHIPS, POOL_WIDTH, Dq), BF16)],
        scratch=[pltpu.VMEM((D, D), F32), pltpu.VMEM((RET_WIDTH, D), F32), pltpu.VMEM((POOL_WIDTH, D), F32)],
        sem=("arbitrary",), operands=(mixed, dh, r, dret, pm, dpool), rider=rider)


def _mix_bwd_dx(dh, z, ret, pool, wout, wru, wpu, tm, name, rider=None):
    T, D = dh.shape
    Dq = D // N_CHIPS
    nb = D // RET_WIDTH

    def body(*refs):
        dh_ref = refs[0]
        gate_refs = refs[1:1 + 2 * nb]
        ret_ref, pool_ref, wout_ref, wru_ref, wpu_ref, dgab_ref, dret_ref, dpool_ref, dr_ref, dpm_ref = refs[1 + 2 * nb:]
        dmixed = _dot_nt(dh_ref[...].astype(BF16), wout_ref[...].reshape(D, D))
        ga, gb = _load_gates(gate_refs, nb)
        sa = _sigmoid(ga)
        sb = _sigmoid(gb)
        dgab_ref[:, :D] = (dmixed * ret_ref[...].astype(F32) * (sa * (1.0 - sa))).astype(BF16)
        dgab_ref[:, D:] = (dmixed * pool_ref[...].astype(F32) * (sb * (1.0 - sb))).astype(BF16)
        dret = (dmixed * sa).astype(BF16)
        dpool = (dmixed * sb).astype(BF16)
        dret_ref[...] = dret
        dpool_ref[...] = dpool
        dr = _dot_nt(dret[:, :Dq], wru_ref[0])
        dpm = _dot_nt(dpool[:, :Dq], wpu_ref[0])
        for s in range(1, N_CHIPS):
            dr += _dot_nt(dret[:, s * Dq:(s + 1) * Dq], wru_ref[s])
            dpm += _dot_nt(dpool[:, s * Dq:(s + 1) * Dq], wpu_ref[s])
        dr_ref[...] = dr
        dpm_ref[...] = dpm

    row = pl.BlockSpec((tm, D), lambda t: (t, 0))
    half = pl.BlockSpec((tm, RET_WIDTH), lambda t: (t, 0))
    up = pl.BlockSpec((N_CHIPS, RET_WIDTH, Dq), lambda t: (0, 0, 0))
    return _call(
        body, name=name, grid=(T // tm,),
        in_specs=[row] + _gate_specs(tm, D) + [row, row, pl.BlockSpec((N_CHIPS, Dq, D), lambda t: (0, 0, 0)), up, up],
        out_specs=[pl.BlockSpec((tm, 2 * D), lambda t: (t, 0)), row, row, half, half],
        out_shape=[jax.ShapeDtypeStruct((T, 2 * D), BF16), jax.ShapeDtypeStruct((T, D), BF16),
                   jax.ShapeDtypeStruct((T, D), BF16), jax.ShapeDtypeStruct((T, RET_WIDTH), F32),
                   jax.ShapeDtypeStruct((T, POOL_WIDTH), F32)],
        sem=("parallel",), operands=(dh, *([z] * (2 * nb)), ret, pool, wout, wru, wpu), rider=rider)


def _pool_bwd(z, dpm, maps, scale, layer, pad, name):
    T = z.shape[0]

    def body(zu, maps_ref, sc_ref, dpm_ref, du_ref, dmaps_ref, dsc_ref):
        g = pl.program_id(0)
        u = zu[...].astype(F32)
        pooled, div, valid = _pool_parts(u, g, T, pad)
        pb = pooled.astype(BF16)
        mb = maps_ref[...].astype(BF16)
        dp = dpm_ref[...]
        dsc_ref[...] = jnp.sum(dp * _dot(pb, mb), axis=0, keepdims=True)
        dyb = (dp * sc_ref[...]).astype(BF16)
        dmaps_ref[...] = _dot_tn(pb, dyb)
        dpooled = jnp.where(valid, _dot_nt(dyb, mb), 0.0)
        ahead = _select_group(_window_sums(dpooled / div, lambda k: T - k), g)
        du_ref[...] = jnp.where(valid, ahead - dpooled, 0.0).astype(BF16)

    blk = pl.BlockSpec((T, HEAD_DIM), lambda g: (0, g))
    return _call(
        body, name=name, grid=(POOL_GROUPS,),
        in_specs=_pool_specs(T, layer) + [blk],
        out_specs=[blk, pl.BlockSpec((None, HEAD_DIM, HEAD_DIM), lambda g: (g, 0, 0)),
                   pl.BlockSpec((1, HEAD_DIM), lambda g: (0, g))],
        out_shape=[jax.ShapeDtypeStruct((T, POOL_WIDTH), BF16),
                   jax.ShapeDtypeStruct((POOL_GROUPS, HEAD_DIM, HEAD_DIM), F32),
                   jax.ShapeDtypeStruct((1, POOL_WIDTH), F32)],
        sem=("parallel",), operands=(z, maps, scale, dpm))


def _ret_bwd_local(z, o_pre, s_all, dr, consts, cg, name):
    T = z.shape[0]
    N = T // CHUNK
    ng = N // cg
    tg = cg * CHUNK
    cosf, sinf, intra, _, qdec, _ = consts
    fwd = lambda g: g

    def body(zq, zk, zv, zg, o_ref, s_ref, dr_ref, cos_ref, sin_ref, m_ref, qd_ref,
             dq_ref, dg_ref, dk_ref, dv_ref, ds_ref):
        cosv = cos_ref[...]
        sinv = sin_ref[...]
        scale = HEAD_DIM ** -0.5
        q3 = (_rot(zq[...].astype(F32), cosv, sinv) * scale).reshape(cg, CHUNK, HEAD_DIM)
        k3 = _rot(zk[...].astype(F32), cosv, sinv).reshape(cg, CHUNK, HEAD_DIM)
        qb = q3.astype(BF16)
        kb = k3.astype(BF16)
        vb = zv[...].reshape(cg, CHUNK, HEAD_DIM).astype(BF16)
        mask = m_ref[...][None]
        sb = (_ein("ncd,nmd->ncm", qb, kb) * mask).astype(BF16)
        qdv = qd_ref[...][None]
        qdb = (q3 * qdv).astype(BF16)

        out = o_ref[...]
        xc = out - jnp.mean(out, axis=-1, keepdims=True)
        rstd = lax.rsqrt(jnp.mean(xc * xc, axis=-1, keepdims=True) + EPS)
        rn = xc * rstd
        g = zg[...].astype(F32)
        sg = _sigmoid(g)
        drv = dr_ref[...]
        dg_ref[...] = (drv * rn * (sg * (1.0 + g * (1.0 - sg)))).astype(BF16)
        drn = drv * (g * sg)
        dout = rstd * (drn - jnp.mean(drn, axis=-1, keepdims=True)
                       - rn * jnp.mean(drn * rn, axis=-1, keepdims=True))
        dob = dout.reshape(cg, CHUNK, HEAD_DIM).astype(BF16)

        dsb = (_ein("ncd,nmd->ncm", dob, vb) * mask).astype(BF16)
        dv_ref[...] = _ein("ncm,ncd->nmd", sb, dob).reshape(tg, HEAD_DIM)
        dk_ref[...] = _ein("ncm,ncd->nmd", dsb, qb).reshape(tg, HEAD_DIM)
        dq3 = _ein("ncm,nmd->ncd", dsb, kb) + _ein("nce,nde->ncd", dob, s_ref[...].astype(BF16)) * qdv
        dq_ref[...] = _rot_t(dq3.reshape(tg, HEAD_DIM) * scale, cosv, sinv).astype(BF16)
        ds_ref[...] = _ein("ncd,nce->nde", qdb, dob)

    tab = pl.BlockSpec((tg, HEAD_DIM), lambda h, g: (g, 0))
    per_head = pl.BlockSpec((None, CHUNK, HEAD_DIM), lambda h, g: (h, 0, 0))
    head_blk = pl.BlockSpec((tg, HEAD_DIM), lambda h, g: (g, h))
    state_blk = pl.BlockSpec((None, cg, HEAD_DIM, HEAD_DIM), lambda h, g: (h, g, 0, 0))
    return _call(
        body, name=name, grid=(RET_HEADS, ng),
        in_specs=[_head_specs(tg, i, fwd) for i in range(4)]
        + [head_blk, state_blk, head_blk, tab, tab, per_head, per_head],
        out_specs=[head_blk, head_blk, head_blk, head_blk, state_blk],
        out_shape=[jax.ShapeDtypeStruct((T, RET_WIDTH), BF16), jax.ShapeDtypeStruct((T, RET_WIDTH), BF16),
                   jax.ShapeDtypeStruct((T, RET_WIDTH), F32), jax.ShapeDtypeStruct((T, RET_WIDTH), F32),
                   jax.ShapeDtypeStruct((RET_HEADS, N, HEAD_DIM, HEAD_DIM), F32)],
        sem=("parallel", "parallel"), operands=(z, z, z, z, o_pre, s_all, dr, cosf, sinf, intra, qdec))


def _ret_bwd_state(z, dkp, dvp, ds, consts, cg, name):
    T = z.shape[0]
    N = T // CHUNK
    ng = N // cg
    tg = cg * CHUNK
    cosf, sinf, _, kdec, _, cdb = consts
    rev = lambda g: ng - 1 - g

    def body(zk, zv, dkp_ref, dvp_ref, ds_ref, cos_ref, sin_ref, kd_ref, cd_ref, dk_ref, dv_ref, gs_ref, dkv_ref):
        @pl.when(pl.program_id(1) == 0)
        def _():
            gs_ref[...] = jnp.zeros_like(gs_ref)

        cosv = cos_ref[...]
        sinv = sin_ref[...]
        cd = cd_ref[0:1, :]
        grad = gs_ref[...]
        for n in reversed(range(cg)):
            dkv_ref[n] = grad
            grad = ds_ref[n] + cd * grad
        gs_ref[...] = grad
        dkvb = dkv_ref[...].astype(BF16)
        kdv = kd_ref[...][None]
        k3 = _rot(zk[...].astype(F32), cosv, sinv).reshape(cg, CHUNK, HEAD_DIM)
        vb = zv[...].reshape(cg, CHUNK, HEAD_DIM).astype(BF16)
        dk3 = _ein("nce,nde->ncd", vb, dkvb) * kdv
        dv3 = _ein("ncd,nde->nce", (k3 * kdv).astype(BF16), dkvb)
        dk_ref[...] = _rot_t(dkp_ref[...] + dk3.reshape(tg, HEAD_DIM), cosv, sinv).astype(BF16)
        dv_ref[...] = (dvp_ref[...] + dv3.reshape(tg, HEAD_DIM)).astype(BF16)

    tab = pl.BlockSpec((tg, HEAD_DIM), lambda h, g: (rev(g), 0))
    head_blk = pl.BlockSpec((tg, HEAD_DIM), lambda h, g: (rev(g), h))
    return _call(
        body, name=name, grid=(RET_HEADS, ng),
        in_specs=[_head_specs(tg, 1, rev), _head_specs(tg, 2, rev), head_blk, head_blk,
                  pl.BlockSpec((None, cg, HEAD_DIM, HEAD_DIM), lambda h, g: (h, rev(g), 0, 0)),
                  tab, tab,
                  pl.BlockSpec((None, CHUNK, HEAD_DIM), lambda h, g: (h, 0, 0)),
                  pl.BlockSpec((None, 8, HEAD_DIM), lambda h, g: (h, 0, 0))],
        out_specs=[head_blk, head_blk],
        out_shape=[jax.ShapeDtypeStruct((T, RET_WIDTH), BF16)] * 2,
        scratch=[pltpu.VMEM((HEAD_DIM, HEAD_DIM), F32), pltpu.VMEM((cg, HEAD_DIM, HEAD_DIM), F32)],
        sem=("parallel", "arbitrary"), operands=(z, z, dkp, dvp, ds, cosf, sinf, kdec, cdb))


def _inproj_bwd_dx(dz, win, h, gain, dh_in, layer, tm, pad, name, rider=None):
    T, D = h.shape
    Ns = win.shape[-1]

    def body(dz_ref, w_ref, h_ref, g_ref, dhi_ref, dh_ref, dgain_ref, db_ref):
        t = pl.program_id(0)
        s = pl.program_id(1)

        @pl.when((t == 0) & (s == 0))
        def _():
            dgain_ref[...] = jnp.zeros_like(dgain_ref)

        @pl.when(s == 0)
        def _():
            db_ref[...] = jnp.zeros_like(db_ref)

        db_ref[...] += _dot_nt(dz_ref[...], w_ref[...])

        @pl.when(s == N_CHIPS - 1)
        def _():
            dx, dgain = _rms_bwd(h_ref[...], g_ref[...], db_ref[...])
            dgain_ref[...] += dgain
            dh_ref[...] = jnp.where(_row_mask(t, tm, pad, (tm, D)), dhi_ref[...] + dx, 0.0)

    row = pl.BlockSpec((tm, D), lambda t, s: (t, 0))
    return _call(
        body, name=name, grid=(T // tm, N_CHIPS),
        in_specs=[pl.BlockSpec((tm, Ns), lambda t, s: (t, s)),
                  pl.BlockSpec((None, D, Ns), lambda t, s: (s, 0, 0)),
                  row, pl.BlockSpec((None, 1, D), lambda t, s: (layer, 0, 0)), row],
        out_specs=[row, pl.BlockSpec((1, D), lambda t, s: (0, 0))],
        out_shape=[jax.ShapeDtypeStruct((T, D), F32), jax.ShapeDtypeStruct((1, D), F32)],
        scratch=[pltpu.VMEM((tm, D), F32)],
        sem=("arbitrary", "arbitrary"), operands=(dz, win, h, gain, dh_in), rider=rider)


def _sum_pair(gs, rs, c_idx, name):
    n = len(gs)

    def body(c_ref, *refs):
        for g_ref, r_ref, o_ref in zip(refs[:n], refs[n:2 * n], refs[2 * n:]):
            o_ref[...] = (g_ref[...].astype(F32) + r_ref[...].astype(F32)).astype(BF16)

    halves = [pl.BlockSpec((None,) + r.shape[1:], lambda s, c_ref: (s, 0, 0)) for r in rs]
    return pl.pallas_call(
        body,
        name=name,
        grid_spec=pltpu.PrefetchScalarGridSpec(
            num_scalar_prefetch=1,
            grid=(N_CHIPS,),
            in_specs=[pl.BlockSpec((None,) + r.shape[1:], lambda s, c_ref: (s, c_ref[0], 0)) for r in rs] + halves,
            out_specs=halves,
        ),
        out_shape=[jax.ShapeDtypeStruct(r.shape, BF16) for r in rs],
        compiler_params=_params(("parallel",)),
    )(c_idx, *gs, *rs)


def _sum_chips(ps, rs, pos, name):
    n = len(ps)
    quarters = 4

    def body(pos_ref, *refs):
        chip = pos_ref[0]
        for p_ref, r_ref, o_ref in zip(refs[:n], refs[n:2 * n], refs[2 * n:]):
            own = p_ref[...].astype(F32)
            terms = [jnp.where(chip == k, own, r_ref[k].astype(F32)) for k in range(N_CHIPS)]
            o_ref[...] = ((terms[0] + terms[1]) + terms[2]) + terms[3]

    def rows(r):
        assert r.shape[1] % (quarters * BF16_ROWS) == 0, r.shape
        return r.shape[1] // quarters

    return pl.pallas_call(
        body,
        name=name,
        grid_spec=pltpu.PrefetchScalarGridSpec(
            num_scalar_prefetch=1,
            grid=(quarters,),
            in_specs=[pl.BlockSpec((None, rows(r), r.shape[2]), lambda q, pos_ref: (pos_ref[0], q, 0)) for r in rs]
            + [pl.BlockSpec((N_CHIPS, rows(r), r.shape[2]), lambda q, pos_ref: (0, q, 0)) for r in rs],
            out_specs=[pl.BlockSpec((rows(r), r.shape[2]), lambda q, pos_ref: (pos_ref[1] * quarters + q, 0))
                       for r in rs],
        ),
        out_shape=[jax.ShapeDtypeStruct((2 * r.shape[1], r.shape[2]), F32) for r in rs],
        compiler_params=_params(("arbitrary",)),
    )(pos, *ps, *rs)


def _small_all_reduce(p):
    rows, width = p.shape

    def body(p_ref, o_ref, sib_ref, slot_ref, ssem, rsem):
        x, y, c, chip, others = _mesh_pos()
        pair = _remote(p_ref, sib_ref, ssem.at[0], rsem.at[0], (x, y, 1 - c))
        pair.start()
        pair.wait()
        slot_ref[chip] = p_ref[...] + sib_ref[...]
        sends = []
        for j, (ox, oy) in enumerate(others):
            cp = _remote(slot_ref.at[chip], slot_ref.at[chip], ssem.at[1 + j], rsem.at[1 + j], (ox, oy, c))
            cp.start()
            sends.append(cp)
        for j, (ox, oy) in enumerate(others):
            slot = slot_ref.at[2 * ox + oy]
            _remote(slot, slot, ssem.at[1 + j], rsem.at[1 + j], (ox, oy, c)).wait_recv()
        for cp in sends:
            cp.wait_send()
        o_ref[...] = ((slot_ref[0] + slot_ref[1]) + slot_ref[2]) + slot_ref[3]

    vmem = pl.BlockSpec(memory_space=pltpu.VMEM)
    return pl.pallas_call(
        body,
        name="small_grads_all_reduce",
        in_specs=[vmem],
        out_specs=vmem,
        out_shape=jax.ShapeDtypeStruct(p.shape, F32),
        scratch_shapes=[pltpu.VMEM((rows, width), F32), pltpu.VMEM((N_CHIPS, rows, width), F32),
                        pltpu.SemaphoreType.DMA((4,)), pltpu.SemaphoreType.DMA((4,))],
    )(p)


def _adamw(gs, w, m, v, name):
    L, R, C = w.shape
    Ct = gs[0].shape[1]
    tr = _pick_tile(R, 256, 8)

    def body(*refs):
        g_refs = refs[:L]
        w_ref, m_ref, v_ref, go_ref, d_ref, mo_ref, vo_ref = refs[L:]
        layer = pl.program_id(0)
        grad = g_refs[L - 1][...]
        for i in range(L - 2, -1, -1):
            grad = jnp.where(layer == i, g_refs[i][...], grad)
        if Ct != C:
            grad = grad[:, :C]
        m_new = ADAM_B1 * m_ref[...] + (1.0 - ADAM_B1) * grad
        v_new = ADAM_B2 * v_ref[...] + (1.0 - ADAM_B2) * jnp.square(grad)
        m_hat = m_new / (1.0 - ADAM_B1 ** ADAM_STEP)
        v_hat = v_new / (1.0 - ADAM_B2 ** ADAM_STEP)
        go_ref[...] = grad
        d_ref[...] = -ADAM_LR * (m_hat / (jnp.sqrt(v_hat) + ADAM_EPS) + ADAM_WD * w_ref[...])
        mo_ref[...] = m_new
        vo_ref[...] = v_new

    g_specs = [pl.BlockSpec((tr, Ct), functools.partial(lambda l, r, i: (jnp.where(l == i, r, 0), 0), i=i))
               for i in range(L)]
    blk = pl.BlockSpec((None, tr, C), lambda l, r: (l, r, 0))
    return _call(
        body, name=name, grid=(L, R // tr),
        in_specs=g_specs + [blk, blk, blk],
        out_specs=[blk] * 4,
        out_shape=[jax.ShapeDtypeStruct((L, R, C), F32)] * 4,
        sem=("arbitrary", "arbitrary"), operands=(*gs, w, m, v))


_FFN1 = ("ffn1_gate", "ffn1_up", "ffn1_down")
_FFN2 = ("ffn2_gate", "ffn2_up", "ffn2_down")
_MIXW = ("w_ret_up", "w_pool_up", "w_out")
_BIG = _FFN1 + ("w_in",) + _MIXW + _FFN2
_TRANSPOSED = ("ffn1_gate", "ffn1_up", "ffn2_gate", "ffn2_up")
_SMALL = ("ffn1_norm", "mix_norm", "ffn2_norm", "final_norm", "pool_scale", "pool_maps")
_ORDER = ("meta", "ffn1_norm", "ffn1_gate", "ffn1_up", "ffn1_down", "mix_norm", "w_in", "pool_maps",
          "pool_scale", "w_ret_up", "w_pool_up", "w_out", "ffn2_norm", "ffn2_gate", "ffn2_up", "ffn2_down",
          "final_norm")


def _transport(a):
    n, r, c = a.shape
    out = a.astype(BF16)
    if c % LANES:
        out = jnp.concatenate([out, jnp.zeros((n, r, _round_up(c, LANES) - c), BF16)], axis=2)
    if r % LANES:
        out = jnp.concatenate([out, jnp.zeros((n, _round_up(r, LANES) - r, out.shape[2]), BF16)], axis=1)
    return out


def _pack_rows(parts, width):
    rows = [p.reshape(-1, width) for p in parts]
    total = sum(r.shape[0] for r in rows)
    fill = _round_up(total, 8) - total
    if fill:
        rows.append(jnp.zeros((fill, width), F32))
    return jnp.concatenate(rows, axis=0)


def _unpack_rows(packed, shapes, width):
    out, at = [], 0
    for shp in shapes:
        n = math.prod(shp) // width
        out.append(packed[at:at + n].reshape(shp))
        at += n
    return out


class _Weights:
    def __init__(self, shards):
        self.shards = shards
        self.full = {}

    def rider(self, keys):
        r = _gather_rider([(self.shards[n], i) for n, i in keys])
        r.keys = keys
        return r

    def take(self, rider):
        for key, arr in zip(rider.keys, rider.results):
            self.full[key] = arr

    def __call__(self, name, layer):
        return self.full[(name, layer)]


def _local_step(x, meta_full, tgt, w, wts, pad, tm, cg, reducer):
    D = x.shape[1]
    T = pad + N_META + x.shape[0]
    L = w["ffn1_norm"].shape[0]
    pool_maps = w["pool_maps"]
    gains = {n: w[n].reshape(L, 1, D) for n in ("ffn1_norm", "mix_norm", "ffn2_norm")}
    scale3 = w["pool_scale"].reshape(L, 1, POOL_WIDTH)
    consts = _ret_consts(T, pad)
    tl = _pick_tile(T, 2 * tm, BF16_ROWS)
    def gather(keys):
        return wts.rider(keys) if keys and keys[0] not in wts.full else None

    def done(rider):
        if rider is not None:
            wts.take(rider)

    h = jnp.concatenate([jnp.zeros((pad, D), F32), meta_full, x], axis=0)
    saved = []
    for i in range(L):
        s = {"h0": h}
        rd = gather([("w_in", i)] + [(n, i) for n in _MIXW])
        h, s["a1"], s["g1"], s["u1"], s["act1"] = _ffn_fwd(
            h, gains["ffn1_norm"], wts("ffn1_gate", i), wts("ffn1_up", i), wts("ffn1_down", i), i, tl,
            f"ffn1_fwd_{i}", rd)
        done(rd)
        s["h1"] = h
        rd = gather([("ffn2_gate", i), ("ffn2_up", i)])
        s["z"], s["b"] = _inproj_fwd(h, gains["mix_norm"], wts("w_in", i), i, tl, f"inproj_fwd_{i}", rd)
        done(rd)
        s["r"], s["o_pre"], s["s_all"] = _ret_fwd(s["z"], consts, cg, f"retention_fwd_{i}")
        s["pm"] = _pool_fwd(s["z"], pool_maps, scale3, i, pad, f"pool_fwd_{i}")
        rd = gather([("ffn2_down", i)])
        h, s["mixed"], s["ret"], s["pool"] = _mix_fwd(
            h, s["r"], s["pm"], s["z"], wts("w_ret_up", i), wts("w_pool_up", i), wts("w_out", i), tl,
            f"mix_fwd_{i}", rd)
        done(rd)
        s["h2"] = h
        rd = gather([(n, i + 1) for n in _FFN1]) if i + 1 < L else None
        h, s["a2"], s["g2"], s["u2"], s["act2"] = _ffn_fwd(
            h, gains["ffn2_norm"], wts("ffn2_gate", i), wts("ffn2_up", i), wts("ffn2_down", i), i, tl,
            f"ffn2_fwd_{i}", rd)
        done(rd)
        saved.append(s)

    dh, loss_acc, d_final = _final_loss(h, w["final_norm"].reshape(1, D), tgt, "final_norm_loss")

    small = {n: [None] * L for n in ("ffn1_norm", "mix_norm", "ffn2_norm", "pool_scale", "pool_maps")}

    carry = {"ffn_act": 1.0, "ffn_in": 2.2, "mix_bwd": 1.0, "inproj_bwd": 1.5, "w_in": 1.0}

    tk = _pick_tile(T, 1408, LANES)

    def grad(n, a, b, i, mode):
        rd = reducer.rider(carry.get(n, 1.0 if i == 0 and n.startswith("ffn") else 0.5))
        reducer.add(n, i, _grad_tn(a, b, mode, 1.0, tk, f"grad_{n}_{i}", rd))
        reducer.done(rd)

    def ffn_bwd(which, dy, h_in, g, u, i):
        rd = reducer.rider(carry["ffn_act"])
        dg, du, dyh = _ffn_bwd_act(dy, g, u, wts(f"{which}_down", i), tl, f"{which}_bwd_act_{i}", rd)
        reducer.done(rd)
        rd = reducer.rider(carry["ffn_in"])
        dh_in, dgain = _ffn_bwd_in(dy, h_in, gains[f"{which}_norm"], dg, du, wts(f"{which}_gate", i),
                                   wts(f"{which}_up", i), i, tl, pad, f"{which}_bwd_in_{i}", rd)
        reducer.done(rd)
        return dh_in, dg, du, dgain, dyh

    for i in reversed(range(L)):
        s = saved[i]
        dh, dg, du, small["ffn2_norm"][i], dyh = ffn_bwd("ffn2", dh, s["h2"], s["g2"], s["u2"], i)
        grad("ffn2_gate", dg, s["a2"], i, "row")
        grad("ffn2_up", du, s["a2"], i, "row")
        grad("ffn2_down", s["act2"], dyh, i, "row")
        reducer.stage(f"ffn2_{i}")
        rd = reducer.rider(carry["mix_bwd"])
        dgab, dret, dpool, dr, dpm = _mix_bwd_dx(
            dh, s["z"], s["ret"], s["pool"], wts("w_out", i), wts("w_ret_up", i), wts("w_pool_up", i), tm,
            f"mix_bwd_{i}", rd)
        reducer.done(rd)
        rd = reducer.rider(0.5)
        g_out, g_ru, g_pu = _grad_mix(s["mixed"], dh, s["r"], dret, s["pm"], dpool, _pick_tile(T, 704, LANES),
                                      f"grad_mix_{i}", rd)
        reducer.done(rd)
        for n, g_n in (("w_out", g_out), ("w_ret_up", g_ru), ("w_pool_up", g_pu)):
            reducer.add(n, i, g_n)
        du_pool, small["pool_maps"][i], small["pool_scale"][i] = _pool_bwd(
            s["z"], dpm, pool_maps, scale3, i, pad, f"pool_bwd_{i}")
        dq, dgr, dkp, dvp, ds = _ret_bwd_local(s["z"], s["o_pre"], s["s_all"], dr, consts,
                                               _pick_tile(T // CHUNK, 11, 1), f"retention_bwd_{i}")
        dk, dv = _ret_bwd_state(s["z"], dkp, dvp, ds, consts, cg, f"retention_bwd_state_{i}")
        dz = jnp.concatenate([dq, dk, dv, dgr, du_pool, dgab], axis=1)
        dh2 = dh
        rd = reducer.rider(carry["inproj_bwd"])
        dh, small["mix_norm"][i] = _inproj_bwd_dx(
            dz, wts("w_in", i), s["h1"], gains["mix_norm"], dh2, i, tl, pad, f"inproj_bwd_{i}", rd)
        reducer.done(rd)
        grad("w_in", s["b"], dz, i, "col")
        reducer.stage(f"mid{i}")
        dh, dg, du, small["ffn1_norm"][i], dyh = ffn_bwd("ffn1", dh, s["h0"], s["g1"], s["u1"], i)
        grad("ffn1_gate", dg, s["a1"], i, "row")
        if i == 0:
            reducer.stage("gate0")
        grad("ffn1_up", du, s["a1"], i, "row")
        if i == 0:
            reducer.stage("up0")
        grad("ffn1_down", s["act1"], dyh, i, "row")
        reducer.stage(f"end{i}")

    return loss_acc, dh, small, d_final


class _Reducer:
    def __init__(self, unit):
        self.c_idx = lax.axis_index("c").astype(jnp.int32).reshape(1)
        chip = 2 * lax.axis_index("x") + lax.axis_index("y")
        self.pos = jnp.stack([chip, lax.axis_index("c")]).astype(jnp.int32)
        self.pending, self.stages, self.queue, self.halves, self.whole = [], [], [], {}, {}
        self.unit = unit
        self.calls = 0

    def add(self, name, layer, g):
        self.pending.append(((name, layer), g))

    def stage(self, tag):
        if self.pending:
            self.stages.append((tag, self.pending))
            self.pending = []

    def _pair_rider(self):
        if not self.stages:
            return None
        tag, items = self.stages.pop(0)
        rd = _pair_exchange_rider([g for _, g in items])
        rd.tag, rd.keys = tag, [k for k, _ in items]
        return rd

    def _chip_rider(self, units):
        take, size = [], 0
        while self.queue and (units is None or size + self.queue[0][1].size <= units * self.unit):
            take.append(self.queue.pop(0))
            size += take[-1][1].size
        if not take:
            return None
        rd = _chip_exchange_rider([p for _, p in take])
        rd.keys = [k for k, _ in take]
        return rd

    def _gather_rider(self):
        keys = [k for k in self.halves if k not in self.whole]
        if not keys:
            return None
        rd = _pair_gather_rider([self.halves[k] for k in keys])
        rd.keys = keys
        return rd

    def rider(self, units):
        self.riding = (self._pair_rider(), self._chip_rider(units), self._gather_rider())
        return _join(self.riding)

    def done(self, rd):
        if rd is None:
            return
        _split_results(rd)
        pair, chips, gather = self.riding
        if len([r for r in self.riding if r is not None]) == 1:
            (pair or chips or gather).results = rd.results
        self.calls += 1
        if gather is not None:
            self.whole.update(zip(gather.keys, gather.results))
        if pair is not None:
            sums = _sum_pair(pair.ins, pair.results, self.c_idx, f"sum_pair_{pair.tag}")
            self.queue += list(zip(pair.keys, sums))
        if chips is not None:
            sums = _sum_chips(chips.ins, chips.results, self.pos, f"sum_chips_{self.calls}")
            self.halves.update(zip(chips.keys, sums))

    def busy(self):
        assert not self.pending
        return bool(self.stages or self.queue or len(self.whole) < len(self.halves))

    def flush(self):
        self.riding = (self._pair_rider(), self._chip_rider(None), self._gather_rider())
        rd = _join(self.riding)
        _run_rider(rd, f"grads_exchange_tail_{self.calls}")
        self.done(rd)


def _update(loss_acc, grad_x, d_meta_rows, reducer, small, d_final, w, mom, var):
    meta = w["meta"]
    D = w["final_norm"].shape[0]
    L = w["ffn1_norm"].shape[0]
    Dq = D // N_CHIPS

    out = {}

    while reducer.busy():
        reducer.flush()
    for n in _BIG:
        gs = [reducer.whole[(n, i)] for i in range(L)]
        if n in _TRANSPOSED:
            res = _adamw(gs, *(jnp.swapaxes(t[n], 1, 2) for t in (w, mom, var)), f"adamw_{n}")
            out[n] = [jnp.swapaxes(r, 1, 2) for r in res]
        else:
            out[n] = _adamw(gs, w[n], mom[n], var[n], f"adamw_{n}")

    small_parts = [jnp.concatenate(small[n], axis=0) for n in ("ffn1_norm", "mix_norm", "ffn2_norm")]
    small_parts += [d_final, jnp.concatenate(small["pool_scale"], axis=0), jnp.concatenate(small["pool_maps"], axis=0)]
    loss_row = jnp.pad(loss_acc, ((0, 0), (0, D - loss_acc.shape[1])))
    reduced = _small_all_reduce(_pack_rows(small_parts + [d_meta_rows, loss_row], D))
    small_shapes = [w[n].shape for n in _SMALL]
    small_rows = sum(math.prod(shp) for shp in small_shapes) // D
    chip = 2 * lax.axis_index("x") + lax.axis_index("y")
    d_meta = lax.dynamic_slice_in_dim(reduced[small_rows:small_rows + N_META], chip * Dq, Dq, axis=1)
    names = _SMALL + ("meta",)
    packed_g = _pack_rows([reduced[:small_rows], d_meta], D)
    packed = [_pack_rows([t[n] for n in names], D) for t in (w, mom, var)]
    res = _adamw([packed_g], packed[0][None], packed[1][None], packed[2][None], "adamw_small")
    shapes = small_shapes + [meta.shape]
    unpacked = [_unpack_rows(r[0], shapes, D) for r in res]
    for k, n in enumerate(names):
        out[n] = tuple(u[k] for u in unpacked)

    loss = reduced[small_rows + N_META, 0]
    return (loss, grad_x) + tuple(out[n][j] for j in range(4) for n in _ORDER)


def kernel(x, meta, ffn1_norm, ffn1_gate, ffn1_up, ffn1_down, mix_norm, w_in, pool_maps, pool_scale, w_ret_up, w_pool_up, w_out, ffn2_norm, ffn2_gate, ffn2_up, ffn2_down, final_norm, loss_target, m_meta, m_ffn1_norm, m_ffn1_gate, m_ffn1_up, m_ffn1_down, m_mix_norm, m_w_in, m_pool_maps, m_pool_scale, m_w_ret_up, m_w_pool_up, m_w_out, m_ffn2_norm, m_ffn2_gate, m_ffn2_up, m_ffn2_down, m_final_norm, v_meta, v_ffn1_norm, v_ffn1_gate, v_ffn1_up, v_ffn1_down, v_mix_norm, v_w_in, v_pool_maps, v_pool_scale, v_w_ret_up, v_w_pool_up, v_w_out, v_ffn2_norm, v_ffn2_gate, v_ffn2_up, v_ffn2_down, v_final_norm):
    args = dict(locals())
    w = {n: args[n] for n in _ORDER}
    mom = {n: args["m_" + n] for n in _ORDER}
    var = {n: args["v_" + n] for n in _ORDER}

    assert x.shape[0] == 1, "one batch element per device"
    seq, D = x.shape[1], x.shape[2]
    assert seq % CHUNK == 0 and D % RET_WIDTH == 0 and (2 * POOL_WIDTH) % D == 0
    pad = (-(seq + N_META)) % CHUNK
    T = seq + N_META + pad
    tm = _pick_tile(T, 528, BF16_ROWS)
    cg = _pick_tile(T // CHUNK, 33, 1)

    shards = {n: _transport(w[n]) for n in _BIG}
    shards["meta"] = meta[None]
    wts = _Weights(shards)
    head = wts.rider([(n, 0) for n in _FFN1] + [("meta", 0)])
    _run_rider(head, "weights_gather_head")
    wts.take(head)
    meta_full = jnp.transpose(wts("meta", 0), (1, 0, 2)).reshape(N_META, D)

    reducer = _Reducer(unit=2 * shards["ffn1_gate"][0].size)
    loss_acc, dh, small, d_final = _local_step(x[0], meta_full, loss_target[0], w, wts, pad, tm, cg, reducer)
    grad_x = dh[pad + N_META:][None]
    return _update(loss_acc, grad_x, dh[pad:pad + N_META], reducer, small, d_final, w, mom, var)
```

```python
import functools
import math

import jax
import jax.numpy as jnp
from jax import lax
from jax.experimental import pallas as pl
from jax.experimental.pallas import tpu as pltpu

F32 = jnp.float32
BF16 = jnp.bfloat16

N_META = 16
RET_HEADS = 4
HEAD_DIM = 128
RET_WIDTH = RET_HEADS * HEAD_DIM
POOL_WINDOWS = (2, 4, 8, 16)
POOL_GROUPS = len(POOL_WINDOWS)
POOL_WIDTH = POOL_GROUPS * HEAD_DIM
CHUNK = 128
ROPE_BASE = 10000.0
EPS = 1e-6
ADAM_LR = 0.001
ADAM_B1 = 0.9
ADAM_B2 = 0.999
ADAM_EPS = 1e-08
ADAM_WD = 0.01
ADAM_STEP = 10

N_CHIPS = 4
LANES = 128
BF16_ROWS = 16
V7X_VMEM_LIMIT = 52 * 1024 * 1024
MESH = pl.DeviceIdType.MESH
ANY = pl.BlockSpec(memory_space=pl.ANY)


def _round_up(n, m):
    return -(-n // m) * m


def _pick_tile(n, target, mult):
    best = None
    for d in range(mult, min(n, target) + 1, mult):
        if n % d == 0:
            best = d
    assert best is not None, (n, target, mult)
    return best


def _params(sem=None):
    return pltpu.CompilerParams(dimension_semantics=sem, vmem_limit_bytes=V7X_VMEM_LIMIT)


def _dot(a, b):
    return jnp.dot(a, b, preferred_element_type=F32)


def _dot_nt(a, b):
    return lax.dot_general(a, b, (((1,), (1,)), ((), ())), preferred_element_type=F32)


def _dot_tn(a, b):
    return lax.dot_general(a, b, (((0,), (0,)), ((), ())), preferred_element_type=F32)


def _ein(spec, a, b):
    return jnp.einsum(spec, a, b, preferred_element_type=F32)


def _sigmoid(x):
    return jax.nn.sigmoid(x)


def _rms_fwd(x, gain):
    r = lax.rsqrt(jnp.mean(x * x, axis=-1, keepdims=True) + EPS)
    return x * r * gain


def _rms_bwd(x, gain, da):
    r = lax.rsqrt(jnp.mean(x * x, axis=-1, keepdims=True) + EPS)
    xh = x * r
    dgain = jnp.sum(da * xh, axis=0, keepdims=True)
    dxh = da * gain
    dx = r * (dxh - xh * jnp.mean(dxh * xh, axis=-1, keepdims=True))
    return dx, dgain


def _row_mask(t, tm, pad, shape):
    rows = t * tm + lax.broadcasted_iota(jnp.int32, shape, 0)
    return rows >= pad


def _mesh_pos():
    x, y, c = lax.axis_index("x"), lax.axis_index("y"), lax.axis_index("c")
    others = [(1 - x, y), (x, 1 - y), (1 - x, 1 - y)]
    return x, y, c, 2 * x + y, others


def _half_rows(c, rh):
    return pl.ds(pl.multiple_of(c * rh, rh), rh)


def _remote(src, dst, ssem, rsem, dev):
    return pltpu.make_async_remote_copy(src_ref=src, dst_ref=dst, send_sem=ssem, recv_sem=rsem,
                                        device_id=dev, device_id_type=MESH)


class _Rider:
    def __init__(self, ins, out_shapes, n_sem, start, finish, in_place=False):
        self.ins, self.out_shapes, self.n_sem, self.start, self.finish = ins, out_shapes, n_sem, start, finish
        self.in_place = [in_place] * len(ins)
        self.results = None

    def aliases(self, first_in, first_out):
        return {first_in + i: first_out + i for i, same in enumerate(self.in_place) if same}


class _SemWindow:
    def __init__(self, ref, base):
        self.ref, self.base = ref, base

    @property
    def at(self):
        return self

    def __getitem__(self, k):
        return self.ref.at[self.base + k]


def _join(riders):
    riders = [r for r in riders if r is not None]
    if len(riders) <= 1:
        return riders[0] if riders else None

    def run(which):
        def go(ins, outs, ssem, rsem):
            at, sem = 0, 0
            for r in riders:
                n = len(r.ins)
                getattr(r, which)(ins[at:at + n], outs[at:at + n], _SemWindow(ssem, sem), _SemWindow(rsem, sem))
                at, sem = at + n, sem + r.n_sem
        return go

    joined = _Rider(sum([list(r.ins) for r in riders], []), sum([list(r.out_shapes) for r in riders], []),
                    sum(r.n_sem for r in riders), run("start"), run("finish"))
    joined.in_place = sum([r.in_place for r in riders], [])
    joined.parts = riders
    return joined


def _split_results(rider):
    at = 0
    for r in getattr(rider, "parts", []):
        r.results = rider.results[at:at + len(r.ins)]
        at += len(r.ins)


def _gather_rider(pieces):
    per = 7
    layers = [layer for _, layer in pieces]

    def first_copies(ins, outs, ssem, rsem):
        x, y, c, chip, others = _mesh_pos()
        copies = []
        for i, layer in enumerate(layers):
            mine = _half_rows(c, ins[i].shape[1] // 2)
            for j, (ox, oy) in enumerate(others):
                copies.append(_remote(ins[i].at[layer, mine, :], outs[i].at[chip, mine, :],
                                      ssem.at[per * i + j], rsem.at[per * i + j], (ox, oy, c)))
            copies.append(_remote(ins[i].at[layer], outs[i].at[chip],
                                  ssem.at[per * i + 6], rsem.at[per * i + 6], (x, y, 1 - c)))
        return copies

    def start(ins, outs, ssem, rsem):
        for cp in first_copies(ins, outs, ssem, rsem):
            cp.start()

    def finish(ins, outs, ssem, rsem):
        x, y, c, chip, others = _mesh_pos()
        sibling = (x, y, 1 - c)
        forwards = []
        for i in range(len(layers)):
            mine = _half_rows(c, ins[i].shape[1] // 2)
            for j, (ox, oy) in enumerate(others):
                rows = outs[i].at[2 * ox + oy, mine, :]
                _remote(rows, rows, ssem.at[per * i + j], rsem.at[per * i + j], (ox, oy, c)).wait_recv()
                fwd = _remote(rows, rows, ssem.at[per * i + 3 + j], rsem.at[per * i + 3 + j], sibling)
                fwd.start()
                forwards.append(fwd)
        for i in range(len(layers)):
            theirs = _half_rows(1 - c, ins[i].shape[1] // 2)
            for j, (ox, oy) in enumerate(others):
                rows = outs[i].at[2 * ox + oy, theirs, :]
                _remote(rows, rows, ssem.at[per * i + 3 + j], rsem.at[per * i + 3 + j], sibling).wait_recv()
            own = outs[i].at[chip]
            _remote(own, own, ssem.at[per * i + 6], rsem.at[per * i + 6], sibling).wait_recv()
        for cp in first_copies(ins, outs, ssem, rsem) + forwards:
            cp.wait_send()

    shapes = [jax.ShapeDtypeStruct((N_CHIPS,) + s.shape[1:], s.dtype) for s, _ in pieces]
    return _Rider([s for s, _ in pieces], shapes, per * len(pieces), start, finish)


def _chip_exchange_rider(ps):
    def copies(ins, outs, ssem, rsem):
        x, y, c, chip, others = _mesh_pos()
        return [_remote(ins[i].at[2 * ox + oy], outs[i].at[chip], ssem.at[3 * i + j], rsem.at[3 * i + j], (ox, oy, c))
                for i in range(len(ps)) for j, (ox, oy) in enumerate(others)]

    def start(ins, outs, ssem, rsem):
        for cp in copies(ins, outs, ssem, rsem):
            cp.start()

    def finish(ins, outs, ssem, rsem):
        x, y, c, chip, others = _mesh_pos()
        for i in range(len(ps)):
            for j, (ox, oy) in enumerate(others):
                slot = outs[i].at[2 * ox + oy]
                _remote(slot, slot, ssem.at[3 * i + j], rsem.at[3 * i + j], (ox, oy, c)).wait_recv()
        for cp in copies(ins, outs, ssem, rsem):
            cp.wait_send()

    return _Rider(list(ps), [jax.ShapeDtypeStruct(p.shape, p.dtype) for p in ps], 3 * len(ps), start, finish)


def _pair_exchange_rider(gs):
    def copies(ins, outs, ssem, rsem):
        x, y, c, _, _ = _mesh_pos()
        return [_remote(ins[i].at[:, _half_rows(1 - c, ins[i].shape[1] // 2), :], outs[i],
                        ssem.at[i], rsem.at[i], (x, y, 1 - c)) for i in range(len(gs))]

    def start(ins, outs, ssem, rsem):
        for cp in copies(ins, outs, ssem, rsem):
            cp.start()

    def finish(ins, outs, ssem, rsem):
        for cp in copies(ins, outs, ssem, rsem):
            cp.wait()

    shapes = [jax.ShapeDtypeStruct((g.shape[0], g.shape[1] // 2, g.shape[2]), g.dtype) for g in gs]
    return _Rider(list(gs), shapes, len(gs), start, finish)


def _run_rider(rider, name):
    def body(*refs):
        n = len(rider.ins)
        ins, outs = refs[:n], refs[n:2 * n]
        ssem, rsem = refs[2 * n:]
        rider.start(ins, outs, ssem, rsem)
        rider.finish(ins, outs, ssem, rsem)

    rider.results = pl.pallas_call(
        body,
        name=name,
        in_specs=[ANY] * len(rider.ins),
        out_specs=[ANY] * len(rider.ins),
        out_shape=rider.out_shapes,
        input_output_aliases=rider.aliases(0, 0),
        scratch_shapes=[pltpu.SemaphoreType.DMA((rider.n_sem,)), pltpu.SemaphoreType.DMA((rider.n_sem,))],
    )(*rider.ins)
    return rider.results


def _pair_gather_rider(fs):
    n = len(fs)

    def copies(outs, ssem, rsem):
        x, y, c, _, _ = _mesh_pos()
        halves = [outs[i].at[_half_rows(c, outs[i].shape[0] // 2), :] for i in range(n)]
        return [_remote(h, h, ssem.at[i], rsem.at[i], (x, y, 1 - c)) for i, h in enumerate(halves)]

    def start(ins, outs, ssem, rsem):
        for cp in copies(outs, ssem, rsem):
            cp.start()

    def finish(ins, outs, ssem, rsem):
        x, y, c, _, _ = _mesh_pos()
        for i in range(n):
            theirs = outs[i].at[_half_rows(1 - c, outs[i].shape[0] // 2), :]
            _remote(theirs, theirs, ssem.at[i], rsem.at[i], (x, y, 1 - c)).wait_recv()
        for cp in copies(outs, ssem, rsem):
            cp.wait_send()

    return _Rider(list(fs), [jax.ShapeDtypeStruct(f.shape, f.dtype) for f in fs], n, start, finish, in_place=True)


def _call(body, *, name, grid, in_specs, out_specs, out_shape, operands, scratch=(), sem=None, rider=None):
    if rider is None:
        return pl.pallas_call(
            body, name=name, grid=grid, in_specs=in_specs, out_specs=out_specs, out_shape=out_shape,
            scratch_shapes=list(scratch), compiler_params=_params(sem))(*operands)
    n_in, n_out, n_sc, r = len(in_specs), len(out_specs), len(scratch), len(rider.ins)

    def carrying(*refs):
        a, b = n_in, n_in + r
        c, d = b + n_out, b + n_out + r
        e = d + n_sc
        ids = [pl.program_id(k) for k in range(len(grid))]
        first = functools.reduce(jnp.logical_and, [i == 0 for i in ids])
        last = functools.reduce(jnp.logical_and, [i == g - 1 for i, g in zip(ids, grid)])

        @pl.when(first)
        def _():
            rider.start(refs[a:b], refs[c:d], refs[e], refs[e + 1])

        body(*refs[:a], *refs[b:c], *refs[d:e])

        @pl.when(last)
        def _():
            rider.finish(refs[a:b], refs[c:d], refs[e], refs[e + 1])

    outs = pl.pallas_call(
        carrying, name=name, grid=grid,
        in_specs=list(in_specs) + [ANY] * r,
        out_specs=list(out_specs) + [ANY] * r,
        out_shape=list(out_shape) + list(rider.out_shapes),
        scratch_shapes=list(scratch) + [pltpu.SemaphoreType.DMA((rider.n_sem,)), pltpu.SemaphoreType.DMA((rider.n_sem,))],
        input_output_aliases=rider.aliases(n_in, n_out),
        compiler_params=_params(("arbitrary",) * len(grid)),
    )(*operands, *rider.ins)
    rider.results = outs[n_out:]
    return outs[:n_out]


def _ffn_fwd(h, gain, wg, wu, wd, layer, tm, name, rider=None):
    T, D = h.shape
    Fs = wg.shape[-1]
    F = N_CHIPS * Fs

    def body(h_ref, g_ref, wg_ref, wu_ref, wd_ref, ho_ref, a_ref, go_ref, uo_ref, act_ref, acc_ref):
        s = pl.program_id(1)

        @pl.when(s == 0)
        def _():
            a_ref[...] = _rms_fwd(h_ref[...], g_ref[...]).astype(BF16)
            acc_ref[...] = jnp.zeros_like(acc_ref)

        a = a_ref[...]
        g = _dot(a, wg_ref[...])
        u = _dot(a, wu_ref[...])
        sg = _sigmoid(g)
        act = (g * sg * u).astype(BF16)
        go_ref[...] = (u * (sg * (1.0 + g * (1.0 - sg)))).astype(BF16)
        uo_ref[...] = (g * sg).astype(BF16)
        act_ref[...] = act
        acc_ref[...] += _dot(act, wd_ref[...])

        @pl.when(s == N_CHIPS - 1)
        def _():
            ho_ref[...] = h_ref[...] + 0.5 * acc_ref[...]

    row = pl.BlockSpec((tm, D), lambda t, s: (t, 0))
    col = pl.BlockSpec((tm, Fs), lambda t, s: (t, s))
    wcol = pl.BlockSpec((None, D, Fs), lambda t, s: (s, 0, 0))
    return _call(
        body, name=name, grid=(T // tm, N_CHIPS),
        in_specs=[row, pl.BlockSpec((None, 1, D), lambda t, s: (layer, 0, 0)), wcol, wcol,
                  pl.BlockSpec((None, Fs, D), lambda t, s: (s, 0, 0))],
        out_specs=[row, row, col, col, col],
        out_shape=[jax.ShapeDtypeStruct((T, D), F32), jax.ShapeDtypeStruct((T, D), BF16)]
        + [jax.ShapeDtypeStruct((T, F), BF16)] * 3,
        scratch=[pltpu.VMEM((tm, D), F32)],
        sem=("parallel", "arbitrary"), operands=(h, gain, wg, wu, wd), rider=rider)


def _inproj_fwd(h, gain, win, layer, tm, name, rider=None):
    T, D = h.shape
    Ns = win.shape[-1]

    def body(h_ref, g_ref, w_ref, z_ref, b_ref):
        @pl.when(pl.program_id(1) == 0)
        def _():
            b_ref[...] = _rms_fwd(h_ref[...], g_ref[...]).astype(BF16)

        z_ref[...] = _dot(b_ref[...], w_ref[...]).astype(BF16)

    return _call(
        body, name=name, grid=(T // tm, N_CHIPS),
        in_specs=[pl.BlockSpec((tm, D), lambda t, s: (t, 0)),
                  pl.BlockSpec((None, 1, D), lambda t, s: (layer, 0, 0)),
                  pl.BlockSpec((None, D, Ns), lambda t, s: (s, 0, 0))],
        out_specs=[pl.BlockSpec((tm, Ns), lambda t, s: (t, s)), pl.BlockSpec((tm, D), lambda t, s: (t, 0))],
        out_shape=[jax.ShapeDtypeStruct((T, N_CHIPS * Ns), BF16), jax.ShapeDtypeStruct((T, D), BF16)],
        sem=("parallel", "arbitrary"), operands=(h, gain, win), rider=rider)


def _ret_consts(T, pad):
    half = HEAD_DIM // 2
    inv_freq = ROPE_BASE ** (-jnp.arange(half, dtype=F32) / half)
    pos = jnp.arange(T, dtype=F32) - pad
    ang = pos[:, None] * inv_freq[None, :]
    cos = jnp.cos(ang)
    sin = jnp.sin(ang)
    cosf = jnp.concatenate([cos, cos], axis=1)
    sinf = jnp.concatenate([-sin, sin], axis=1)
    log_gamma = jnp.log1p(-(2.0 ** (-5.0 - jnp.arange(RET_HEADS, dtype=F32))))
    idx = jnp.arange(CHUNK, dtype=F32)
    diff = idx[:, None] - idx[None, :]
    intra = jnp.where(diff[None] >= 0, jnp.exp(diff[None] * log_gamma[:, None, None]), 0.0)
    k_decay = jnp.exp((CHUNK - 1.0 - idx)[None, :] * log_gamma[:, None])
    q_decay = jnp.exp((idx + 1.0)[None, :] * log_gamma[:, None])
    chunk_decay = jnp.exp(CHUNK * log_gamma)
    kdec = jnp.broadcast_to(k_decay[:, :, None], (RET_HEADS, CHUNK, HEAD_DIM))
    qdec = jnp.broadcast_to(q_decay[:, :, None], (RET_HEADS, CHUNK, HEAD_DIM))
    cdb = jnp.broadcast_to(chunk_decay[:, None, None], (RET_HEADS, 8, HEAD_DIM))
    return cosf, sinf, intra, kdec, qdec, cdb


def _rot(t, cosv, sinv):
    return t * cosv + pltpu.roll(t, HEAD_DIM // 2, 1) * sinv


def _rot_t(g, cosv, sinv):
    return g * cosv + pltpu.roll(g * sinv, HEAD_DIM // 2, 1)


def _head_specs(tg, section, order):
    return pl.BlockSpec((tg, HEAD_DIM), lambda h, g: (order(g), section * RET_HEADS + h))


def _ret_fwd(z, consts, cg, name, rider=None):
    T = z.shape[0]
    N = T // CHUNK
    ng = N // cg
    tg = cg * CHUNK
    cosf, sinf, intra, kdec, qdec, cdb = consts
    fwd = lambda g: g

    def body(zq, zk, zv, zg, cos_ref, sin_ref, m_ref, kd_ref, qd_ref, cd_ref, r_ref, o_ref, s_ref, st_ref):
        @pl.when(pl.program_id(1) == 0)
        def _():
            st_ref[...] = jnp.zeros_like(st_ref)

        cosv = cos_ref[...]
        sinv = sin_ref[...]
        q3 = (_rot(zq[...].astype(F32), cosv, sinv) * (HEAD_DIM ** -0.5)).reshape(cg, CHUNK, HEAD_DIM)
        k3 = _rot(zk[...].astype(F32), cosv, sinv).reshape(cg, CHUNK, HEAD_DIM)
        vb = zv[...].reshape(cg, CHUNK, HEAD_DIM).astype(BF16)
        scores = _ein("ncd,nmd->ncm", q3.astype(BF16), k3.astype(BF16)) * m_ref[...][None]
        inner = _ein("ncm,nmd->ncd", scores.astype(BF16), vb)
        kv = _ein("ncd,nce->nde", (k3 * kd_ref[...][None]).astype(BF16), vb)
        cd = cd_ref[0:1, :]
        state = st_ref[...]
        for n in range(cg):
            s_ref[n] = state
            state = state * cd + kv[n]
        st_ref[...] = state
        qdb = (q3 * qd_ref[...][None]).astype(BF16)
        cross = _ein("ncd,nde->nce", qdb, s_ref[...].astype(BF16))
        out = (inner + cross).reshape(tg, HEAD_DIM)
        o_ref[...] = out
        xc = out - jnp.mean(out, axis=-1, keepdims=True)
        rn = xc * lax.rsqrt(jnp.mean(xc * xc, axis=-1, keepdims=True) + EPS)
        g = zg[...].astype(F32)
        r_ref[...] = (rn * (g * _sigmoid(g))).astype(BF16)

    tab = pl.BlockSpec((tg, HEAD_DIM), lambda h, g: (g, 0))
    per_head = lambda rows: pl.BlockSpec((None, rows, HEAD_DIM), lambda h, g: (h, 0, 0))
    head_out = pl.BlockSpec((tg, HEAD_DIM), lambda h, g: (g, h))
    return _call(
        body, name=name, grid=(RET_HEADS, ng),
        in_specs=[_head_specs(tg, i, fwd) for i in range(4)]
        + [tab, tab, per_head(CHUNK), per_head(CHUNK), per_head(CHUNK), per_head(8)],
        out_specs=[head_out, head_out, pl.BlockSpec((None, cg, HEAD_DIM, HEAD_DIM), lambda h, g: (h, g, 0, 0))],
        out_shape=[jax.ShapeDtypeStruct((T, RET_WIDTH), BF16), jax.ShapeDtypeStruct((T, RET_WIDTH), F32),
                   jax.ShapeDtypeStruct((RET_HEADS, N, HEAD_DIM, HEAD_DIM), F32)],
        scratch=[pltpu.VMEM((HEAD_DIM, HEAD_DIM), F32)],
        sem=("parallel", "arbitrary"), operands=(z, z, z, z, cosf, sinf, intra, kdec, qdec, cdb), rider=rider)


def _window_sums(u, shift_of):
    sums = []
    s = u
    k = 1
    while k < POOL_WINDOWS[-1]:
        s = s + pltpu.roll(s, shift_of(k), 0)
        sums.append(s)
        k *= 2
    return sums


def _select_group(vals, g):
    out = vals[-1]
    for i in range(len(vals) - 2, -1, -1):
        out = jnp.where(g == i, vals[i], out)
    return out


def _pool_parts(u, g, T, pad):
    rows = lax.broadcasted_iota(jnp.int32, (T, HEAD_DIM), 0)
    valid = rows >= pad
    win = _select_group([float(w) for w in POOL_WINDOWS], g)
    div = jnp.clip((rows - pad + 1).astype(F32), 1.0, win)
    s = _select_group(_window_sums(u, lambda k: k), g)
    pooled = jnp.where(valid, s / div - u, 0.0)
    return pooled, div, valid


def _pool_specs(T, layer):
    first = 4 * RET_WIDTH // HEAD_DIM
    return [
        pl.BlockSpec((T, HEAD_DIM), lambda g: (0, first + g)),
        pl.BlockSpec((None, None, HEAD_DIM, HEAD_DIM), lambda g: (layer, g, 0, 0)),
        pl.BlockSpec((None, 1, HEAD_DIM), lambda g: (layer, 0, g)),
    ]


def _pool_fwd(z, maps, scale, layer, pad, name):
    T = z.shape[0]
    assert pad >= POOL_WINDOWS[-1], "window rolls wrap into the zero rows in front"

    def body(zu, maps_ref, sc_ref, pm_ref):
        g = pl.program_id(0)
        pooled, _, _ = _pool_parts(zu[...].astype(F32), g, T, pad)
        y = _dot(pooled.astype(BF16), maps_ref[...].astype(BF16))
        pm_ref[...] = (y * sc_ref[...]).astype(BF16)

    return _call(
        body, name=name, grid=(POOL_GROUPS,),
        in_specs=_pool_specs(T, layer),
        out_specs=[pl.BlockSpec((T, HEAD_DIM), lambda g: (0, g))],
        out_shape=[jax.ShapeDtypeStruct((T, POOL_WIDTH), BF16)],
        sem=("parallel",), operands=(z, maps, scale))[0]


def _gate_specs(tm, D):
    nb = D // RET_WIDTH
    first = (4 * RET_WIDTH + POOL_WIDTH) // RET_WIDTH
    return [pl.BlockSpec((tm, RET_WIDTH), functools.partial(lambda t, j: (t, j), j=first + j)) for j in range(2 * nb)]


def _load_gates(refs, nb):
    ga = jnp.concatenate([r[...].astype(F32) for r in refs[:nb]], axis=1)
    gb = jnp.concatenate([r[...].astype(F32) for r in refs[nb:]], axis=1)
    return ga, gb


def _mix_fwd(h, r, pm, z, wru, wpu, wout, tm, name, rider=None):
    T, D = h.shape
    Dq = D // N_CHIPS
    nb = D // RET_WIDTH

    def body(*refs):
        h_ref, r_ref, pm_ref = refs[:3]
        gate_refs = refs[3:3 + 2 * nb]
        wru_ref, wpu_ref, wout_ref, ho_ref, mx_ref, ret_ref, pool_ref = refs[3 + 2 * nb:]
        rv = r_ref[...]
        pv = pm_ref[...]
        ret = jnp.concatenate([_dot(rv, wru_ref[s]) for s in range(N_CHIPS)], axis=1)
        pool = jnp.concatenate([_dot(pv, wpu_ref[s]) for s in range(N_CHIPS)], axis=1)
        ga, gb = _load_gates(gate_refs, nb)
        mixed = (_sigmoid(ga) * ret + _sigmoid(gb) * pool).astype(BF16)
        mx_ref[...] = mixed
        ret_ref[...] = ret.astype(BF16)
        pool_ref[...] = pool.astype(BF16)
        ho_ref[...] = h_ref[...] + _dot(mixed, wout_ref[...].reshape(D, D))

    row = pl.BlockSpec((tm, D), lambda t: (t, 0))
    half = pl.BlockSpec((tm, RET_WIDTH), lambda t: (t, 0))
    up = pl.BlockSpec((N_CHIPS, RET_WIDTH, Dq), lambda t: (0, 0, 0))
    return _call(
        body, name=name, grid=(T // tm,),
        in_specs=[row, half, half] + _gate_specs(tm, D) + [up, up, pl.BlockSpec((N_CHIPS, Dq, D), lambda t: (0, 0, 0))],
        out_specs=[row, row, row, row],
        out_shape=[jax.ShapeDtypeStruct((T, D), F32)] + [jax.ShapeDtypeStruct((T, D), BF16)] * 3,
        sem=("parallel",), operands=(h, r, pm, *([z] * (2 * nb)), wru, wpu, wout), rider=rider)


def _final_loss(h, gain, tgt, name):
    T, D = h.shape
    first = (T - tgt.shape[0]) // CHUNK

    def body(h_ref, g_ref, t_ref, dh_ref, loss_ref, dg_ref):
        i = pl.program_id(0)

        @pl.when(i == 0)
        def _():
            loss_ref[...] = jnp.zeros_like(loss_ref)
            dg_ref[...] = jnp.zeros_like(dg_ref)

        x = h_ref[...]
        gain_v = g_ref[...]
        err = jnp.where(i >= first, _rms_fwd(x, gain_v) - t_ref[...], 0.0)
        loss_ref[...] += 0.5 * jnp.sum(jnp.mean(err * err, axis=-1))
        dx, dgain = _rms_bwd(x, gain_v, err * (1.0 / D))
        dg_ref[...] += dgain
        dh_ref[...] = dx

    return _call(
        body, name=name, grid=(T // CHUNK,),
        in_specs=[pl.BlockSpec((CHUNK, D), lambda i: (i, 0)),
                  pl.BlockSpec((1, D), lambda i: (0, 0)),
                  pl.BlockSpec((CHUNK, D), lambda i: (jnp.maximum(i - first, 0), 0))],
        out_specs=[pl.BlockSpec((CHUNK, D), lambda i: (i, 0)),
                   pl.BlockSpec((1, LANES), lambda i: (0, 0)),
                   pl.BlockSpec((1, D), lambda i: (0, 0))],
        out_shape=[jax.ShapeDtypeStruct((T, D), F32), jax.ShapeDtypeStruct((1, LANES), F32),
                   jax.ShapeDtypeStruct((1, D), F32)],
        sem=("arbitrary",), operands=(h, gain, tgt))


def _ffn_bwd_act(dy, g, u, wd, tm, name, rider=None):
    T, D = dy.shape
    Fs = wd.shape[1]
    F = N_CHIPS * Fs

    def body(dy_ref, go_ref, uo_ref, wd_ref, dg_ref, du_ref, dyh_ref):
        @pl.when(pl.program_id(1) == 0)
        def _():
            dyh_ref[...] = (0.5 * dy_ref[...]).astype(BF16)

        dact = _dot_nt(dyh_ref[...], wd_ref[...])
        du_ref[...] = (dact * uo_ref[...].astype(F32)).astype(BF16)
        dg_ref[...] = (dact * go_ref[...].astype(F32)).astype(BF16)

    row = pl.BlockSpec((tm, D), lambda t, s: (t, 0))
    col = pl.BlockSpec((tm, Fs), lambda t, s: (t, s))
    return _call(
        body, name=name, grid=(T // tm, N_CHIPS),
        in_specs=[row, col, col, pl.BlockSpec((None, Fs, D), lambda t, s: (s, 0, 0))],
        out_specs=[col, col, row],
        out_shape=[jax.ShapeDtypeStruct((T, F), BF16), jax.ShapeDtypeStruct((T, F), BF16),
                   jax.ShapeDtypeStruct((T, D), BF16)],
        sem=("parallel", "arbitrary"), operands=(dy, g, u, wd), rider=rider)


def _ffn_bwd_in(dy, h, gain, dg, du, wg, wu, layer, tm, pad, name, rider=None):
    T, D = h.shape
    Fs = wg.shape[-1]

    def body(dy_ref, h_ref, g_ref, dg_ref, du_ref, wg_ref, wu_ref, dh_ref, dgain_ref, da_ref):
        t = pl.program_id(0)
        s = pl.program_id(1)

        @pl.when((t == 0) & (s == 0))
        def _():
            dgain_ref[...] = jnp.zeros_like(dgain_ref)

        @pl.when(s == 0)
        def _():
            da_ref[...] = jnp.zeros_like(da_ref)

        da_ref[...] += _dot_nt(dg_ref[...], wg_ref[...]) + _dot_nt(du_ref[...], wu_ref[...])

        @pl.when(s == N_CHIPS - 1)
        def _():
            dx, dgain = _rms_bwd(h_ref[...], g_ref[...], da_ref[...])
            dgain_ref[...] += dgain
            dh_ref[...] = jnp.where(_row_mask(t, tm, pad, (tm, D)), dy_ref[...] + dx, 0.0)

    row = pl.BlockSpec((tm, D), lambda t, s: (t, 0))
    col = pl.BlockSpec((tm, Fs), lambda t, s: (t, s))
    wcol = pl.BlockSpec((None, D, Fs), lambda t, s: (s, 0, 0))
    return _call(
        body, name=name, grid=(T // tm, N_CHIPS),
        in_specs=[row, row, pl.BlockSpec((None, 1, D), lambda t, s: (layer, 0, 0)), col, col, wcol, wcol],
        out_specs=[row, pl.BlockSpec((1, D), lambda t, s: (0, 0))],
        out_shape=[jax.ShapeDtypeStruct((T, D), F32), jax.ShapeDtypeStruct((1, D), F32)],
        scratch=[pltpu.VMEM((tm, D), F32)],
        sem=("arbitrary", "arbitrary"), operands=(dy, h, gain, dg, du, wg, wu), rider=rider)


def _grad_tn(a, b, mode, scale, tm, name, rider=None):
    T = a.shape[0]
    if mode == "col":
        per, R, C = 1, a.shape[1], b.shape[1] // N_CHIPS
        a_spec = pl.BlockSpec((tm, R), lambda s, t: (t, 0))
        b_spec = pl.BlockSpec((tm, C), lambda s, t: (t, s))
    else:
        per, R, C = 2, a.shape[1] // N_CHIPS, b.shape[1]
        a_spec = pl.BlockSpec((tm, per * R), lambda s, t: (t, s))
        b_spec = pl.BlockSpec((tm, C), lambda s, t: (t, 0))
    nt = T // tm

    def body(a_ref, b_ref, o_ref, acc_ref):
        t = pl.program_id(1)

        @pl.when(t == 0)
        def _():
            acc_ref[...] = jnp.zeros_like(acc_ref)

        acc_ref[...] += _dot_tn(a_ref[...].astype(BF16), b_ref[...].astype(BF16))

        @pl.when(t == nt - 1)
        def _():
            o_ref[...] = (scale * acc_ref[...]).astype(BF16).reshape(per, R, C)

    return _call(
        body, name=name, grid=(N_CHIPS // per, nt),
        in_specs=[a_spec, b_spec],
        out_specs=[pl.BlockSpec((per, R, C), lambda s, t: (s, 0, 0))],
        out_shape=[jax.ShapeDtypeStruct((N_CHIPS, R, C), BF16)],
        scratch=[pltpu.VMEM((per * R, C), F32)],
        sem=("parallel", "arbitrary"), operands=(a, b), rider=rider)[0]


def _grad_mix(mixed, dh, r, dret, pm, dpool, tk, name, rider=None):
    T, D = dh.shape
    Dq = D // N_CHIPS
    nt = T // tk

    def body(mx_ref, dh_ref, r_ref, dret_ref, pm_ref, dpool_ref, go_ref, gr_ref, gp_ref, ao_ref, ar_ref, ap_ref):
        t = pl.program_id(0)

        @pl.when(t == 0)
        def _():
            ao_ref[...] = jnp.zeros_like(ao_ref)
            ar_ref[...] = jnp.zeros_like(ar_ref)
            ap_ref[...] = jnp.zeros_like(ap_ref)

        ao_ref[...] += _dot_tn(mx_ref[...], dh_ref[...].astype(BF16))
        ar_ref[...] += _dot_tn(r_ref[...], dret_ref[...])
        ap_ref[...] += _dot_tn(pm_ref[...], dpool_ref[...])

        @pl.when(t == nt - 1)
        def _():
            go_ref[...] = ao_ref[...].astype(BF16).reshape(N_CHIPS, Dq, D)
            for s in range(N_CHIPS):
                gr_ref[s] = ar_ref[:, s * Dq:(s + 1) * Dq].astype(BF16)
                gp_ref[s] = ap_ref[:, s * Dq:(s + 1) * Dq].astype(BF16)

    row = pl.BlockSpec((tk, D), lambda t: (t, 0))
    half = pl.BlockSpec((tk, RET_WIDTH), lambda t: (t, 0))
    whole = lambda shape: pl.BlockSpec(shape, lambda t: (0, 0, 0))
    return _call(
        body, name=name, grid=(nt,),
        in_specs=[row, row, half, row, half, row],
        out_specs=[whole((N_CHIPS, Dq, D)), whole((N_CHIPS, RET_WIDTH, Dq)), whole((N_CHIPS, POOL_WIDTH, Dq))],
        out_shape=[jax.ShapeDtypeStruct((N_CHIPS, Dq, D), BF16),
                   jax.ShapeDtypeStruct((N_CHIPS, RET_WIDTH, Dq), BF16),
                   jax.ShapeDtypeStruct((N_CHIPS, POOL_WIDTH, Dq), BF16)],
        scratch=[pltpu.VMEM((D, D), F32), pltpu.VMEM((RET_WIDTH, D), F32), pltpu.VMEM((POOL_WIDTH, D), F32)],
        sem=("arbitrary",), operands=(mixed, dh, r, dret, pm, dpool), rider=rider)


def _mix_bwd_dx(dh, z, ret, pool, wout, wru, wpu, tm, name, rider=None):
    T, D = dh.shape
    Dq = D // N_CHIPS
    nb = D // RET_WIDTH

    def body(*refs):
        dh_ref = refs[0]
        gate_refs = refs[1:1 + 2 * nb]
        ret_ref, pool_ref, wout_ref, wru_ref, wpu_ref, dgab_ref, dret_ref, dpool_ref, dr_ref, dpm_ref = refs[1 + 2 * nb:]
        dmixed = _dot_nt(dh_ref[...].astype(BF16), wout_ref[...].reshape(D, D))
        ga, gb = _load_gates(gate_refs, nb)
        sa = _sigmoid(ga)
        sb = _sigmoid(gb)
        dgab_ref[:, :D] = (dmixed * ret_ref[...].astype(F32) * (sa * (1.0 - sa))).astype(BF16)
        dgab_ref[:, D:] = (dmixed * pool_ref[...].astype(F32) * (sb * (1.0 - sb))).astype(BF16)
        dret = (dmixed * sa).astype(BF16)
        dpool = (dmixed * sb).astype(BF16)
        dret_ref[...] = dret
        dpool_ref[...] = dpool
        dr = _dot_nt(dret[:, :Dq], wru_ref[0])
        dpm = _dot_nt(dpool[:, :Dq], wpu_ref[0])
        for s in range(1, N_CHIPS):
            dr += _dot_nt(dret[:, s * Dq:(s + 1) * Dq], wru_ref[s])
            dpm += _dot_nt(dpool[:, s * Dq:(s + 1) * Dq], wpu_ref[s])
        dr_ref[...] = dr
        dpm_ref[...] = dpm

    row = pl.BlockSpec((tm, D), lambda t: (t, 0))
    half = pl.BlockSpec((tm, RET_WIDTH), lambda t: (t, 0))
    up = pl.BlockSpec((N_CHIPS, RET_WIDTH, Dq), lambda t: (0, 0, 0))
    return _call(
        body, name=name, grid=(T // tm,),
        in_specs=[row] + _gate_specs(tm, D) + [row, row, pl.BlockSpec((N_CHIPS, Dq, D), lambda t: (0, 0, 0)), up, up],
        out_specs=[pl.BlockSpec((tm, 2 * D), lambda t: (t, 0)), row, row, half, half],
        out_shape=[jax.ShapeDtypeStruct((T, 2 * D), BF16), jax.ShapeDtypeStruct((T, D), BF16),
                   jax.ShapeDtypeStruct((T, D), BF16), jax.ShapeDtypeStruct((T, RET_WIDTH), F32),
                   jax.ShapeDtypeStruct((T, POOL_WIDTH), F32)],
        sem=("parallel",), operands=(dh, *([z] * (2 * nb)), ret, pool, wout, wru, wpu), rider=rider)


def _pool_bwd(z, dpm, maps, scale, layer, pad, name):
    T = z.shape[0]

    def body(zu, maps_ref, sc_ref, dpm_ref, du_ref, dmaps_ref, dsc_ref):
        g = pl.program_id(0)
        u = zu[...].astype(F32)
        pooled, div, valid = _pool_parts(u, g, T, pad)
        pb = pooled.astype(BF16)
        mb = maps_ref[...].astype(BF16)
        dp = dpm_ref[...]
        dsc_ref[...] = jnp.sum(dp * _dot(pb, mb), axis=0, keepdims=True)
        dyb = (dp * sc_ref[...]).astype(BF16)
        dmaps_ref[...] = _dot_tn(pb, dyb)
        dpooled = jnp.where(valid, _dot_nt(dyb, mb), 0.0)
        ahead = _select_group(_window_sums(dpooled / div, lambda k: T - k), g)
        du_ref[...] = jnp.where(valid, ahead - dpooled, 0.0).astype(BF16)

    blk = pl.BlockSpec((T, HEAD_DIM), lambda g: (0, g))
    return _call(
        body, name=name, grid=(POOL_GROUPS,),
        in_specs=_pool_specs(T, layer) + [blk],
        out_specs=[blk, pl.BlockSpec((None, HEAD_DIM, HEAD_DIM), lambda g: (g, 0, 0)),
                   pl.BlockSpec((1, HEAD_DIM), lambda g: (0, g))],
        out_shape=[jax.ShapeDtypeStruct((T, POOL_WIDTH), BF16),
                   jax.ShapeDtypeStruct((POOL_GROUPS, HEAD_DIM, HEAD_DIM), F32),
                   jax.ShapeDtypeStruct((1, POOL_WIDTH), F32)],
        sem=("parallel",), operands=(z, maps, scale, dpm))


def _ret_bwd_local(z, o_pre, s_all, dr, consts, cg, name):
    T = z.shape[0]
    N = T // CHUNK
    ng = N // cg
    tg = cg * CHUNK
    cosf, sinf, intra, _, qdec, _ = consts
    fwd = lambda g: g

    def body(zq, zk, zv, zg, o_ref, s_ref, dr_ref, cos_ref, sin_ref, m_ref, qd_ref,
             dq_ref, dg_ref, dk_ref, dv_ref, ds_ref):
        cosv = cos_ref[...]
        sinv = sin_ref[...]
        scale = HEAD_DIM ** -0.5
        q3 = (_rot(zq[...].astype(F32), cosv, sinv) * scale).reshape(cg, CHUNK, HEAD_DIM)
        k3 = _rot(zk[...].astype(F32), cosv, sinv).reshape(cg, CHUNK, HEAD_DIM)
        qb = q3.astype(BF16)
        kb = k3.astype(BF16)
        vb = zv[...].reshape(cg, CHUNK, HEAD_DIM).astype(BF16)
        mask = m_ref[...][None]
        sb = (_ein("ncd,nmd->ncm", qb, kb) * mask).astype(BF16)
        qdv = qd_ref[...][None]
        qdb = (q3 * qdv).astype(BF16)

        out = o_ref[...]
        xc = out - jnp.mean(out, axis=-1, keepdims=True)
        rstd = lax.rsqrt(jnp.mean(xc * xc, axis=-1, keepdims=True) + EPS)
        rn = xc * rstd
        g = zg[...].astype(F32)
        sg = _sigmoid(g)
        drv = dr_ref[...]
        dg_ref[...] = (drv * rn * (sg * (1.0 + g * (1.0 - sg)))).astype(BF16)
        drn = drv * (g * sg)
        dout = rstd * (drn - jnp.mean(drn, axis=-1, keepdims=True)
                       - rn * jnp.mean(drn * rn, axis=-1, keepdims=True))
        dob = dout.reshape(cg, CHUNK, HEAD_DIM).astype(BF16)

        dsb = (_ein("ncd,nmd->ncm", dob, vb) * mask).astype(BF16)
        dv_ref[...] = _ein("ncm,ncd->nmd", sb, dob).reshape(tg, HEAD_DIM)
        dk_ref[...] = _ein("ncm,ncd->nmd", dsb, qb).reshape(tg, HEAD_DIM)
        dq3 = _ein("ncm,nmd->ncd", dsb, kb) + _ein("nce,nde->ncd", dob, s_ref[...].astype(BF16)) * qdv
        dq_ref[...] = _rot_t(dq3.reshape(tg, HEAD_DIM) * scale, cosv, sinv).astype(BF16)
        ds_ref[...] = _ein("ncd,nce->nde", qdb, dob)

    tab = pl.BlockSpec((tg, HEAD_DIM), lambda h, g: (g, 0))
    per_head = pl.BlockSpec((None, CHUNK, HEAD_DIM), lambda h, g: (h, 0, 0))
    head_blk = pl.BlockSpec((tg, HEAD_DIM), lambda h, g: (g, h))
    state_blk = pl.BlockSpec((None, cg, HEAD_DIM, HEAD_DIM), lambda h, g: (h, g, 0, 0))
    return _call(
        body, name=name, grid=(RET_HEADS, ng),
        in_specs=[_head_specs(tg, i, fwd) for i in range(4)]
        + [head_blk, state_blk, head_blk, tab, tab, per_head, per_head],
        out_specs=[head_blk, head_blk, head_blk, head_blk, state_blk],
        out_shape=[jax.ShapeDtypeStruct((T, RET_WIDTH), BF16), jax.ShapeDtypeStruct((T, RET_WIDTH), BF16),
                   jax.ShapeDtypeStruct((T, RET_WIDTH), F32), jax.ShapeDtypeStruct((T, RET_WIDTH), F32),
                   jax.ShapeDtypeStruct((RET_HEADS, N, HEAD_DIM, HEAD_DIM), F32)],
        sem=("parallel", "parallel"), operands=(z, z, z, z, o_pre, s_all, dr, cosf, sinf, intra, qdec))


def _ret_bwd_state(z, dkp, dvp, ds, consts, cg, name):
    T = z.shape[0]
    N = T // CHUNK
    ng = N // cg
    tg = cg * CHUNK
    cosf, sinf, _, kdec, _, cdb = consts
    rev = lambda g: ng - 1 - g

    def body(zk, zv, dkp_ref, dvp_ref, ds_ref, cos_ref, sin_ref, kd_ref, cd_ref, dk_ref, dv_ref, gs_ref, dkv_ref):
        @pl.when(pl.program_id(1) == 0)
        def _():
            gs_ref[...] = jnp.zeros_like(gs_ref)

        cosv = cos_ref[...]
        sinv = sin_ref[...]
        cd = cd_ref[0:1, :]
        grad = gs_ref[...]
        for n in reversed(range(cg)):
            dkv_ref[n] = grad
            grad = ds_ref[n] + cd * grad
        gs_ref[...] = grad
        dkvb = dkv_ref[...].astype(BF16)
        kdv = kd_ref[...][None]
        k3 = _rot(zk[...].astype(F32), cosv, sinv).reshape(cg, CHUNK, HEAD_DIM)
        vb = zv[...].reshape(cg, CHUNK, HEAD_DIM).astype(BF16)
        dk3 = _ein("nce,nde->ncd", vb, dkvb) * kdv
        dv3 = _ein("ncd,nde->nce", (k3 * kdv).astype(BF16), dkvb)
        dk_ref[...] = _rot_t(dkp_ref[...] + dk3.reshape(tg, HEAD_DIM), cosv, sinv).astype(BF16)
        dv_ref[...] = (dvp_ref[...] + dv3.reshape(tg, HEAD_DIM)).astype(BF16)

    tab = pl.BlockSpec((tg, HEAD_DIM), lambda h, g: (rev(g), 0))
    head_blk = pl.BlockSpec((tg, HEAD_DIM), lambda h, g: (rev(g), h))
    return _call(
        body, name=name, grid=(RET_HEADS, ng),
        in_specs=[_head_specs(tg, 1, rev), _head_specs(tg, 2, rev), head_blk, head_blk,
                  pl.BlockSpec((None, cg, HEAD_DIM, HEAD_DIM), lambda h, g: (h, rev(g), 0, 0)),
                  tab, tab,
                  pl.BlockSpec((None, CHUNK, HEAD_DIM), lambda h, g: (h, 0, 0)),
                  pl.BlockSpec((None, 8, HEAD_DIM), lambda h, g: (h, 0, 0))],
        out_specs=[head_blk, head_blk],
        out_shape=[jax.ShapeDtypeStruct((T, RET_WIDTH), BF16)] * 2,
        scratch=[pltpu.VMEM((HEAD_DIM, HEAD_DIM), F32), pltpu.VMEM((cg, HEAD_DIM, HEAD_DIM), F32)],
        sem=("parallel", "arbitrary"), operands=(z, z, dkp, dvp, ds, cosf, sinf, kdec, cdb))


def _z_segments(pieces, ns):
    segs, at = [], 0
    for k, p in enumerate(pieces):
        width = p.shape[1]
        lo = at
        while lo < at + width:
            s = lo // ns
            hi = min(at + width, (s + 1) * ns)
            segs.append((k, lo - at, hi - at, s, lo - s * ns, hi - s * ns))
            lo = hi
        at += width
    assert at == N_CHIPS * ns and all(v % LANES == 0 for seg in segs for v in (seg[1], seg[2], seg[4], seg[5]))
    return segs


def _inproj_bwd_dx(pieces, win, h, gain, dh_in, layer, tm, pad, name, rider=None):
    T, D = h.shape
    Ns = win.shape[-1]
    n = len(pieces)
    segs = _z_segments(pieces, Ns)

    def body(*refs):
        piece_refs = refs[:n]
        w_ref, h_ref, g_ref, dhi_ref, dh_ref, dgain_ref = refs[n:]
        t = pl.program_id(0)

        @pl.when(t == 0)
        def _():
            dgain_ref[...] = jnp.zeros_like(dgain_ref)

        db = None
        for k, a, b, s, c, d in segs:
            term = _dot_nt(piece_refs[k][:, a:b], w_ref[s, :, c:d])
            db = term if db is None else db + term
        dx, dgain = _rms_bwd(h_ref[...], g_ref[...], db)
        dgain_ref[...] += dgain
        dh_ref[...] = jnp.where(_row_mask(t, tm, pad, (tm, D)), dhi_ref[...] + dx, 0.0)

    row = pl.BlockSpec((tm, D), lambda t: (t, 0))
    return _call(
        body, name=name, grid=(T // tm,),
        in_specs=[pl.BlockSpec((tm, p.shape[1]), lambda t: (t, 0)) for p in pieces]
        + [pl.BlockSpec((N_CHIPS, D, Ns), lambda t: (0, 0, 0)), row,
           pl.BlockSpec((None, 1, D), lambda t: (layer, 0, 0)), row],
        out_specs=[row, pl.BlockSpec((1, D), lambda t: (0, 0))],
        out_shape=[jax.ShapeDtypeStruct((T, D), F32), jax.ShapeDtypeStruct((1, D), F32)],
        sem=("arbitrary",), operands=(*pieces, win, h, gain, dh_in), rider=rider)


def _grad_w_in(b, pieces, ns, tk, name, rider=None):
    T, D = b.shape
    n = len(pieces)
    nt = T // tk
    segs = _z_segments(pieces, ns)
    shards_of = [sorted({s for k, _, _, s, _, _ in segs if k == i}) for i in range(n)]

    def body(*refs):
        b_ref = refs[0]
        piece_refs = refs[1:1 + n]
        o_ref, acc_ref = refs[1 + n:]
        s = pl.program_id(0)
        t = pl.program_id(1)

        @pl.when(t == 0)
        def _():
            acc_ref[...] = jnp.zeros_like(acc_ref)

        for shard in range(N_CHIPS):
            @pl.when(s == shard)
            def _(shard=shard):
                cols = [piece_refs[k][:, a:e] for k, a, e, ss, _, _ in segs if ss == shard]
                dz = cols[0] if len(cols) == 1 else jnp.concatenate(cols, axis=1)
                acc_ref[...] += _dot_tn(b_ref[...], dz)

        @pl.when(t == nt - 1)
        def _():
            o_ref[...] = acc_ref[...].astype(BF16).reshape(1, D, ns)

    def piece_spec(i):
        def index(s, t):
            used = functools.reduce(jnp.logical_or, [s == ss for ss in shards_of[i]])
            return (jnp.where(used, t, 0), 0)
        return pl.BlockSpec((tk, pieces[i].shape[1]), index)

    return _call(
        body, name=name, grid=(N_CHIPS, nt),
        in_specs=[pl.BlockSpec((tk, D), lambda s, t: (t, 0))] + [piece_spec(i) for i in range(n)],
        out_specs=[pl.BlockSpec((1, D, ns), lambda s, t: (s, 0, 0))],
        out_shape=[jax.ShapeDtypeStruct((N_CHIPS, D, ns), BF16)],
        scratch=[pltpu.VMEM((D, ns), F32)],
        sem=("parallel", "arbitrary"), operands=(b, *pieces), rider=rider)[0]


def _sum_pair(gs, rs, c_idx, name):
    n = len(gs)

    def body(c_ref, *refs):
        for g_ref, r_ref, o_ref in zip(refs[:n], refs[n:2 * n], refs[2 * n:]):
            o_ref[...] = (g_ref[...].astype(F32) + r_ref[...].astype(F32)).astype(BF16)

    halves = [pl.BlockSpec((None,) + r.shape[1:], lambda s, c_ref: (s, 0, 0)) for r in rs]
    return pl.pallas_call(
        body,
        name=name,
        grid_spec=pltpu.PrefetchScalarGridSpec(
            num_scalar_prefetch=1,
            grid=(N_CHIPS,),
            in_specs=[pl.BlockSpec((None,) + r.shape[1:], lambda s, c_ref: (s, c_ref[0], 0)) for r in rs] + halves,
            out_specs=halves,
        ),
        out_shape=[jax.ShapeDtypeStruct(r.shape, BF16) for r in rs],
        compiler_params=_params(("parallel",)),
    )(c_idx, *gs, *rs)


def _sum_chips(ps, rs, pos, name):
    n = len(ps)
    quarters = 4

    def body(pos_ref, *refs):
        chip = pos_ref[0]
        for p_ref, r_ref, o_ref in zip(refs[:n], refs[n:2 * n], refs[2 * n:]):
            own = p_ref[...].astype(F32)
            terms = [jnp.where(chip == k, own, r_ref[k].astype(F32)) for k in range(N_CHIPS)]
            o_ref[...] = ((terms[0] + terms[1]) + terms[2]) + terms[3]

    def rows(r):
        assert r.shape[1] % (quarters * BF16_ROWS) == 0, r.shape
        return r.shape[1] // quarters

    return pl.pallas_call(
        body,
        name=name,
        grid_spec=pltpu.PrefetchScalarGridSpec(
            num_scalar_prefetch=1,
            grid=(quarters,),
            in_specs=[pl.BlockSpec((None, rows(r), r.shape[2]), lambda q, pos_ref: (pos_ref[0], q, 0)) for r in rs]
            + [pl.BlockSpec((N_CHIPS, rows(r), r.shape[2]), lambda q, pos_ref: (0, q, 0)) for r in rs],
            out_specs=[pl.BlockSpec((rows(r), r.shape[2]), lambda q, pos_ref: (pos_ref[1] * quarters + q, 0))
                       for r in rs],
        ),
        out_shape=[jax.ShapeDtypeStruct((2 * r.shape[1], r.shape[2]), F32) for r in rs],
        compiler_params=_params(("arbitrary",)),
    )(pos, *ps, *rs)


def _small_all_reduce(p):
    rows, width = p.shape

    def body(p_ref, o_ref, sib_ref, slot_ref, ssem, rsem):
        x, y, c, chip, others = _mesh_pos()
        pair = _remote(p_ref, sib_ref, ssem.at[0], rsem.at[0], (x, y, 1 - c))
        pair.start()
        pair.wait()
        slot_ref[chip] = p_ref[...] + sib_ref[...]
        sends = []
        for j, (ox, oy) in enumerate(others):
            cp = _remote(slot_ref.at[chip], slot_ref.at[chip], ssem.at[1 + j], rsem.at[1 + j], (ox, oy, c))
            cp.start()
            sends.append(cp)
        for j, (ox, oy) in enumerate(others):
            slot = slot_ref.at[2 * ox + oy]
            _remote(slot, slot, ssem.at[1 + j], rsem.at[1 + j], (ox, oy, c)).wait_recv()
        for cp in sends:
            cp.wait_send()
        o_ref[...] = ((slot_ref[0] + slot_ref[1]) + slot_ref[2]) + slot_ref[3]

    vmem = pl.BlockSpec(memory_space=pltpu.VMEM)
    return pl.pallas_call(
        body,
        name="small_grads_all_reduce",
        in_specs=[vmem],
        out_specs=vmem,
        out_shape=jax.ShapeDtypeStruct(p.shape, F32),
        scratch_shapes=[pltpu.VMEM((rows, width), F32), pltpu.VMEM((N_CHIPS, rows, width), F32),
                        pltpu.SemaphoreType.DMA((4,)), pltpu.SemaphoreType.DMA((4,))],
    )(p)


def _adamw(gs, w, m, v, name):
    L, R, C = w.shape
    Ct = gs[0].shape[1]
    tr = _pick_tile(R, 256, 8)

    def body(*refs):
        g_refs = refs[:L]
        w_ref, m_ref, v_ref, go_ref, d_ref, mo_ref, vo_ref = refs[L:]
        layer = pl.program_id(0)
        grad = g_refs[L - 1][...]
        for i in range(L - 2, -1, -1):
            grad = jnp.where(layer == i, g_refs[i][...], grad)
        if Ct != C:
            grad = grad[:, :C]
        m_new = ADAM_B1 * m_ref[...] + (1.0 - ADAM_B1) * grad
        v_new = ADAM_B2 * v_ref[...] + (1.0 - ADAM_B2) * jnp.square(grad)
        m_hat = m_new / (1.0 - ADAM_B1 ** ADAM_STEP)
        v_hat = v_new / (1.0 - ADAM_B2 ** ADAM_STEP)
        go_ref[...] = grad
        d_ref[...] = -ADAM_LR * (m_hat / (jnp.sqrt(v_hat) + ADAM_EPS) + ADAM_WD * w_ref[...])
        mo_ref[...] = m_new
        vo_ref[...] = v_new

    g_specs = [pl.BlockSpec((tr, Ct), functools.partial(lambda l, r, i: (jnp.where(l == i, r, 0), 0), i=i))
               for i in range(L)]
    blk = pl.BlockSpec((None, tr, C), lambda l, r: (l, r, 0))
    return _call(
        body, name=name, grid=(L, R // tr),
        in_specs=g_specs + [blk, blk, blk],
        out_specs=[blk] * 4,
        out_shape=[jax.ShapeDtypeStruct((L, R, C), F32)] * 4,
        sem=("arbitrary", "arbitrary"), operands=(*gs, w, m, v))


_FFN1 = ("ffn1_gate", "ffn1_up", "ffn1_down")
_FFN2 = ("ffn2_gate", "ffn2_up", "ffn2_down")
_MIXW = ("w_ret_up", "w_pool_up", "w_out")
_BIG = _FFN1 + ("w_in",) + _MIXW + _FFN2
_TRANSPOSED = ("ffn1_gate", "ffn1_up", "ffn2_gate", "ffn2_up")
_SMALL = ("ffn1_norm", "mix_norm", "ffn2_norm", "final_norm", "pool_scale", "pool_maps")
_ORDER = ("meta", "ffn1_norm", "ffn1_gate", "ffn1_up", "ffn1_down", "mix_norm", "w_in", "pool_maps",
          "pool_scale", "w_ret_up", "w_pool_up", "w_out", "ffn2_norm", "ffn2_gate", "ffn2_up", "ffn2_down",
          "final_norm")


def _transport(a):
    n, r, c = a.shape
    out = a.astype(BF16)
    if c % LANES:
        out = jnp.concatenate([out, jnp.zeros((n, r, _round_up(c, LANES) - c), BF16)], axis=2)
    if r % LANES:
        out = jnp.concatenate([out, jnp.zeros((n, _round_up(r, LANES) - r, out.shape[2]), BF16)], axis=1)
    return out


def _pack_rows(parts, width):
    rows = [p.reshape(-1, width) for p in parts]
    total = sum(r.shape[0] for r in rows)
    fill = _round_up(total, 8) - total
    if fill:
        rows.append(jnp.zeros((fill, width), F32))
    return jnp.concatenate(rows, axis=0)


def _unpack_rows(packed, shapes, width):
    out, at = [], 0
    for shp in shapes:
        n = math.prod(shp) // width
        out.append(packed[at:at + n].reshape(shp))
        at += n
    return out


class _Weights:
    def __init__(self, shards):
        self.shards = shards
        self.full = {}

    def rider(self, keys):
        r = _gather_rider([(self.shards[n], i) for n, i in keys])
        r.keys = keys
        return r

    def take(self, rider):
        for key, arr in zip(rider.keys, rider.results):
            self.full[key] = arr

    def __call__(self, name, layer):
        return self.full[(name, layer)]


def _local_step(x, meta_full, tgt, w, wts, pad, tm, cg, reducer):
    D = x.shape[1]
    T = pad + N_META + x.shape[0]
    L = w["ffn1_norm"].shape[0]
    pool_maps = w["pool_maps"]
    gains = {n: w[n].reshape(L, 1, D) for n in ("ffn1_norm", "mix_norm", "ffn2_norm")}
    scale3 = w["pool_scale"].reshape(L, 1, POOL_WIDTH)
    consts = _ret_consts(T, pad)
    tl = _pick_tile(T, 2 * tm, BF16_ROWS)
    def gather(keys):
        return wts.rider(keys) if keys and keys[0] not in wts.full else None

    def done(rider):
        if rider is not None:
            wts.take(rider)

    h = jnp.concatenate([jnp.zeros((pad, D), F32), meta_full, x], axis=0)
    saved = []
    for i in range(L):
        s = {"h0": h}
        rd = gather([("w_in", i)] + [(n, i) for n in _MIXW])
        h, s["a1"], s["g1"], s["u1"], s["act1"] = _ffn_fwd(
            h, gains["ffn1_norm"], wts("ffn1_gate", i), wts("ffn1_up", i), wts("ffn1_down", i), i, tl,
            f"ffn1_fwd_{i}", rd)
        done(rd)
        s["h1"] = h
        rd = gather([("ffn2_gate", i), ("ffn2_up", i)])
        s["z"], s["b"] = _inproj_fwd(h, gains["mix_norm"], wts("w_in", i), i, tl, f"inproj_fwd_{i}", rd)
        done(rd)
        s["r"], s["o_pre"], s["s_all"] = _ret_fwd(s["z"], consts, cg, f"retention_fwd_{i}")
        s["pm"] = _pool_fwd(s["z"], pool_maps, scale3, i, pad, f"pool_fwd_{i}")
        rd = gather([("ffn2_down", i)])
        h, s["mixed"], s["ret"], s["pool"] = _mix_fwd(
            h, s["r"], s["pm"], s["z"], wts("w_ret_up", i), wts("w_pool_up", i), wts("w_out", i), tl,
            f"mix_fwd_{i}", rd)
        done(rd)
        s["h2"] = h
        rd = gather([(n, i + 1) for n in _FFN1]) if i + 1 < L else None
        h, s["a2"], s["g2"], s["u2"], s["act2"] = _ffn_fwd(
            h, gains["ffn2_norm"], wts("ffn2_gate", i), wts("ffn2_up", i), wts("ffn2_down", i), i, tl,
            f"ffn2_fwd_{i}", rd)
        done(rd)
        saved.append(s)

    dh, loss_acc, d_final = _final_loss(h, w["final_norm"].reshape(1, D), tgt, "final_norm_loss")

    small = {n: [None] * L for n in ("ffn1_norm", "mix_norm", "ffn2_norm", "pool_scale", "pool_maps")}

    carry = {"ffn_act": 1.0, "ffn_in": 2.2, "mix_bwd": 1.0, "inproj_bwd": 1.5, "w_in": 1.0}

    tk = _pick_tile(T, 1408, LANES)

    def grad(n, a, b, i, mode):
        rd = reducer.rider(carry.get(n, 1.0 if i == 0 and n.startswith("ffn") else 0.5))
        reducer.add(n, i, _grad_tn(a, b, mode, 1.0, tk, f"grad_{n}_{i}", rd))
        reducer.done(rd)

    def ffn_bwd(which, dy, h_in, g, u, i):
        rd = reducer.rider(carry["ffn_act"])
        dg, du, dyh = _ffn_bwd_act(dy, g, u, wts(f"{which}_down", i), tl, f"{which}_bwd_act_{i}", rd)
        reducer.done(rd)
        rd = reducer.rider(carry["ffn_in"])
        dh_in, dgain = _ffn_bwd_in(dy, h_in, gains[f"{which}_norm"], dg, du, wts(f"{which}_gate", i),
                                   wts(f"{which}_up", i), i, tl, pad, f"{which}_bwd_in_{i}", rd)
        reducer.done(rd)
        return dh_in, dg, du, dgain, dyh

    for i in reversed(range(L)):
        s = saved[i]
        dh, dg, du, small["ffn2_norm"][i], dyh = ffn_bwd("ffn2", dh, s["h2"], s["g2"], s["u2"], i)
        grad("ffn2_gate", dg, s["a2"], i, "row")
        grad("ffn2_up", du, s["a2"], i, "row")
        grad("ffn2_down", s["act2"], dyh, i, "row")
        reducer.stage(f"ffn2_{i}")
        rd = reducer.rider(carry["mix_bwd"])
        dgab, dret, dpool, dr, dpm = _mix_bwd_dx(
            dh, s["z"], s["ret"], s["pool"], wts("w_out", i), wts("w_ret_up", i), wts("w_pool_up", i), tm,
            f"mix_bwd_{i}", rd)
        reducer.done(rd)
        rd = reducer.rider(0.5)
        g_out, g_ru, g_pu = _grad_mix(s["mixed"], dh, s["r"], dret, s["pm"], dpool, _pick_tile(T, 704, LANES),
                                      f"grad_mix_{i}", rd)
        reducer.done(rd)
        for n, g_n in (("w_out", g_out), ("w_ret_up", g_ru), ("w_pool_up", g_pu)):
            reducer.add(n, i, g_n)
        du_pool, small["pool_maps"][i], small["pool_scale"][i] = _pool_bwd(
            s["z"], dpm, pool_maps, scale3, i, pad, f"pool_bwd_{i}")
        dq, dgr, dkp, dvp, ds = _ret_bwd_local(s["z"], s["o_pre"], s["s_all"], dr, consts,
                                               _pick_tile(T // CHUNK, 11, 1), f"retention_bwd_{i}")
        dk, dv = _ret_bwd_state(s["z"], dkp, dvp, ds, consts, cg, f"retention_bwd_state_{i}")
        dz = [dq, dk, dv, dgr, du_pool, dgab]
        dh2 = dh
        rd = reducer.rider(carry["inproj_bwd"])
        dh, small["mix_norm"][i] = _inproj_bwd_dx(
            dz, wts("w_in", i), s["h1"], gains["mix_norm"], dh2, i, tm, pad, f"inproj_bwd_{i}", rd)
        reducer.done(rd)
        rd = reducer.rider(carry["w_in"])
        reducer.add("w_in", i, _grad_w_in(s["b"], dz, wts("w_in", i).shape[-1], _pick_tile(T, 704, LANES),
                                          f"grad_w_in_{i}", rd))
        reducer.done(rd)
        reducer.stage(f"mid{i}")
        dh, dg, du, small["ffn1_norm"][i], dyh = ffn_bwd("ffn1", dh, s["h0"], s["g1"], s["u1"], i)
        grad("ffn1_gate", dg, s["a1"], i, "row")
        if i == 0:
            reducer.stage("gate0")
        grad("ffn1_up", du, s["a1"], i, "row")
        if i == 0:
            reducer.stage("up0")
        grad("ffn1_down", s["act1"], dyh, i, "row")
        reducer.stage(f"end{i}")

    return loss_acc, dh, small, d_final


class _Reducer:
    def __init__(self, unit):
        self.c_idx = lax.axis_index("c").astype(jnp.int32).reshape(1)
        chip = 2 * lax.axis_index("x") + lax.axis_index("y")
        self.pos = jnp.stack([chip, lax.axis_index("c")]).astype(jnp.int32)
        self.pending, self.stages, self.queue, self.halves, self.whole = [], [], [], {}, {}
        self.unit = unit
        self.calls = 0

    def add(self, name, layer, g):
        self.pending.append(((name, layer), g))

    def stage(self, tag):
        if self.pending:
            self.stages.append((tag, self.pending))
            self.pending = []

    def _pair_rider(self):
        if not self.stages:
            return None
        tag, items = self.stages.pop(0)
        rd = _pair_exchange_rider([g for _, g in items])
        rd.tag, rd.keys = tag, [k for k, _ in items]
        return rd

    def _chip_rider(self, units):
        take, size = [], 0
        while self.queue and (units is None or size + self.queue[0][1].size <= units * self.unit):
            take.append(self.queue.pop(0))
            size += take[-1][1].size
        if not take:
            return None
        rd = _chip_exchange_rider([p for _, p in take])
        rd.keys = [k for k, _ in take]
        return rd

    def _gather_rider(self):
        keys = [k for k in self.halves if k not in self.whole]
        if not keys:
            return None
        rd = _pair_gather_rider([self.halves[k] for k in keys])
        rd.keys = keys
        return rd

    def rider(self, units):
        self.riding = (self._pair_rider(), self._chip_rider(units), self._gather_rider())
        return _join(self.riding)

    def done(self, rd):
        if rd is None:
            return
        _split_results(rd)
        pair, chips, gather = self.riding
        if len([r for r in self.riding if r is not None]) == 1:
            (pair or chips or gather).results = rd.results
        self.calls += 1
        if gather is not None:
            self.whole.update(zip(gather.keys, gather.results))
        if pair is not None:
            sums = _sum_pair(pair.ins, pair.results, self.c_idx, f"sum_pair_{pair.tag}")
            self.queue += list(zip(pair.keys, sums))
        if chips is not None:
            sums = _sum_chips(chips.ins, chips.results, self.pos, f"sum_chips_{self.calls}")
            self.halves.update(zip(chips.keys, sums))

    def busy(self):
        assert not self.pending
        return bool(self.stages or self.queue or len(self.whole) < len(self.halves))

    def flush(self):
        self.riding = (self._pair_rider(), self._chip_rider(None), self._gather_rider())
        rd = _join(self.riding)
        _run_rider(rd, f"grads_exchange_tail_{self.calls}")
        self.done(rd)


def _update(loss_acc, grad_x, d_meta_rows, reducer, small, d_final, w, mom, var):
    meta = w["meta"]
    D = w["final_norm"].shape[0]
    L = w["ffn1_norm"].shape[0]
    Dq = D // N_CHIPS

    out = {}

    while reducer.busy():
        reducer.flush()
    for n in _BIG:
        gs = [reducer.whole[(n, i)] for i in range(L)]
        if n in _TRANSPOSED:
            res = _adamw(gs, *(jnp.swapaxes(t[n], 1, 2) for t in (w, mom, var)), f"adamw_{n}")
            out[n] = [jnp.swapaxes(r, 1, 2) for r in res]
        else:
            out[n] = _adamw(gs, w[n], mom[n], var[n], f"adamw_{n}")

    small_parts = [jnp.concatenate(small[n], axis=0) for n in ("ffn1_norm", "mix_norm", "ffn2_norm")]
    small_parts += [d_final, jnp.concatenate(small["pool_scale"], axis=0), jnp.concatenate(small["pool_maps"], axis=0)]
    loss_row = jnp.pad(loss_acc, ((0, 0), (0, D - loss_acc.shape[1])))
    reduced = _small_all_reduce(_pack_rows(small_parts + [d_meta_rows, loss_row], D))
    small_shapes = [w[n].shape for n in _SMALL]
    small_rows = sum(math.prod(shp) for shp in small_shapes) // D
    chip = 2 * lax.axis_index("x") + lax.axis_index("y")
    d_meta = lax.dynamic_slice_in_dim(reduced[small_rows:small_rows + N_META], chip * Dq, Dq, axis=1)
    names = _SMALL + ("meta",)
    packed_g = _pack_rows([reduced[:small_rows], d_meta], D)
    packed = [_pack_rows([t[n] for n in names], D) for t in (w, mom, var)]
    res = _adamw([packed_g], packed[0][None], packed[1][None], packed[2][None], "adamw_small")
    shapes = small_shapes + [meta.shape]
    unpacked = [_unpack_rows(r[0], shapes, D) for r in res]
    for k, n in enumerate(names):
        out[n] = tuple(u[k] for u in unpacked)

    loss = reduced[small_rows + N_META, 0]
    return (loss, grad_x) + tuple(out[n][j] for j in range(4) for n in _ORDER)


def kernel(x, meta, ffn1_norm, ffn1_gate, ffn1_up, ffn1_down, mix_norm, w_in, pool_maps, pool_scale, w_ret_up, w_pool_up, w_out, ffn2_norm, ffn2_gate, ffn2_up, ffn2_down, final_norm, loss_target, m_meta, m_ffn1_norm, m_ffn1_gate, m_ffn1_up, m_ffn1_down, m_mix_norm, m_w_in, m_pool_maps, m_pool_scale, m_w_ret_up, m_w_pool_up, m_w_out, m_ffn2_norm, m_ffn2_gate, m_ffn2_up, m_ffn2_down, m_final_norm, v_meta, v_ffn1_norm, v_ffn1_gate, v_ffn1_up, v_ffn1_down, v_mix_norm, v_w_in, v_pool_maps, v_pool_scale, v_w_ret_up, v_w_pool_up, v_w_out, v_ffn2_norm, v_ffn2_gate, v_ffn2_up, v_ffn2_down, v_final_norm):
    args = dict(locals())
    w = {n: args[n] for n in _ORDER}
    mom = {n: args["m_" + n] for n in _ORDER}
    var = {n: args["v_" + n] for n in _ORDER}

    assert x.shape[0] == 1, "one batch element per device"
    seq, D = x.shape[1], x.shape[2]
    assert seq % CHUNK == 0 and D % RET_WIDTH == 0 and (2 * POOL_WIDTH) % D == 0
    pad = (-(seq + N_META)) % CHUNK
    T = seq + N_META + pad
    tm = _pick_tile(T, 528, BF16_ROWS)
    cg = _pick_tile(T // CHUNK, 33, 1)

    shards = {n: _transport(w[n]) for n in _BIG}
    shards["meta"] = meta[None]
    wts = _Weights(shards)
    head = wts.rider([(n, 0) for n in _FFN1] + [("meta", 0)])
    _run_rider(head, "weights_gather_head")
    wts.take(head)
    meta_full = jnp.transpose(wts("meta", 0), (1, 0, 2)).reshape(N_META, D)

    reducer = _Reducer(unit=2 * shards["ffn1_gate"][0].size)
    loss_acc, dh, small, d_final = _local_step(x[0], meta_full, loss_target[0], w, wts, pad, tm, cg, reducer)
    grad_x = dh[pad + N_META:][None]
    return _update(loss_acc, grad_x, dh[pad:pad + N_META], reducer, small, d_final, w, mom, var)
```

```python
import functools
import math

import jax
import jax.numpy as jnp
from jax import lax
from jax.experimental import pallas as pl
from jax.experimental.pallas import tpu as pltpu

F32 = jnp.float32
BF16 = jnp.bfloat16

N_META = 16
RET_HEADS = 4
HEAD_DIM = 128
RET_WIDTH = RET_HEADS * HEAD_DIM
POOL_WINDOWS = (2, 4, 8, 16)
POOL_GROUPS = len(POOL_WINDOWS)
POOL_WIDTH = POOL_GROUPS * HEAD_DIM
CHUNK = 128
ROPE_BASE = 10000.0
EPS = 1e-6
ADAM_LR = 0.001
ADAM_B1 = 0.9
ADAM_B2 = 0.999
ADAM_EPS = 1e-08
ADAM_WD = 0.01
ADAM_STEP = 10

N_CHIPS = 4
LANES = 128
BF16_ROWS = 16
V7X_VMEM_LIMIT = 52 * 1024 * 1024
MESH = pl.DeviceIdType.MESH
ANY = pl.BlockSpec(memory_space=pl.ANY)


def _round_up(n, m):
    return -(-n // m) * m


def _pick_tile(n, target, mult):
    best = None
    for d in range(mult, min(n, target) + 1, mult):
        if n % d == 0:
            best = d
    assert best is not None, (n, target, mult)
    return best


def _params(sem=None):
    return pltpu.CompilerParams(dimension_semantics=sem, vmem_limit_bytes=V7X_VMEM_LIMIT)


def _dot(a, b):
    return jnp.dot(a, b, preferred_element_type=F32)


def _dot_nt(a, b):
    return lax.dot_general(a, b, (((1,), (1,)), ((), ())), preferred_element_type=F32)


def _dot_tn(a, b):
    return lax.dot_general(a, b, (((0,), (0,)), ((), ())), preferred_element_type=F32)


def _ein(spec, a, b):
    return jnp.einsum(spec, a, b, preferred_element_type=F32)


def _sigmoid(x):
    return jax.nn.sigmoid(x)


def _rms_fwd(x, gain):
    r = lax.rsqrt(jnp.mean(x * x, axis=-1, keepdims=True) + EPS)
    return x * r * gain


def _rms_bwd(x, gain, da):
    r = lax.rsqrt(jnp.mean(x * x, axis=-1, keepdims=True) + EPS)
    xh = x * r
    dgain = jnp.sum(da * xh, axis=0, keepdims=True)
    dxh = da * gain
    dx = r * (dxh - xh * jnp.mean(dxh * xh, axis=-1, keepdims=True))
    return dx, dgain


def _row_mask(t, tm, pad, shape):
    rows = t * tm + lax.broadcasted_iota(jnp.int32, shape, 0)
    return rows >= pad


def _mesh_pos():
    x, y, c = lax.axis_index("x"), lax.axis_index("y"), lax.axis_index("c")
    others = [(1 - x, y), (x, 1 - y), (1 - x, 1 - y)]
    return x, y, c, 2 * x + y, others


def _half_rows(c, rh):
    return pl.ds(pl.multiple_of(c * rh, rh), rh)


def _remote(src, dst, ssem, rsem, dev):
    return pltpu.make_async_remote_copy(src_ref=src, dst_ref=dst, send_sem=ssem, recv_sem=rsem,
                                        device_id=dev, device_id_type=MESH)


class _Rider:
    def __init__(self, ins, out_shapes, n_sem, start, finish, in_place=False):
        self.ins, self.out_shapes, self.n_sem, self.start, self.finish = ins, out_shapes, n_sem, start, finish
        self.in_place = [in_place] * len(ins)
        self.results = None

    def aliases(self, first_in, first_out):
        return {first_in + i: first_out + i for i, same in enumerate(self.in_place) if same}


class _SemWindow:
    def __init__(self, ref, base):
        self.ref, self.base = ref, base

    @property
    def at(self):
        return self

    def __getitem__(self, k):
        return self.ref.at[self.base + k]


def _join(riders):
    riders = [r for r in riders if r is not None]
    if len(riders) <= 1:
        return riders[0] if riders else None

    def run(which):
        def go(ins, outs, ssem, rsem):
            at, sem = 0, 0
            for r in riders:
                n = len(r.ins)
                getattr(r, which)(ins[at:at + n], outs[at:at + n], _SemWindow(ssem, sem), _SemWindow(rsem, sem))
                at, sem = at + n, sem + r.n_sem
        return go

    joined = _Rider(sum([list(r.ins) for r in riders], []), sum([list(r.out_shapes) for r in riders], []),
                    sum(r.n_sem for r in riders), run("start"), run("finish"))
    joined.in_place = sum([r.in_place for r in riders], [])
    joined.parts = riders
    return joined


def _split_results(rider):
    at = 0
    for r in getattr(rider, "parts", []):
        r.results = rider.results[at:at + len(r.ins)]
        at += len(r.ins)


def _gather_rider(pieces):
    per = 7
    layers = [layer for _, layer in pieces]

    def first_copies(ins, outs, ssem, rsem):
        x, y, c, chip, others = _mesh_pos()
        copies = []
        for i, layer in enumerate(layers):
            mine = _half_rows(c, ins[i].shape[1] // 2)
            for j, (ox, oy) in enumerate(others):
                copies.append(_remote(ins[i].at[layer, mine, :], outs[i].at[chip, mine, :],
                                      ssem.at[per * i + j], rsem.at[per * i + j], (ox, oy, c)))
            copies.append(_remote(ins[i].at[layer], outs[i].at[chip],
                                  ssem.at[per * i + 6], rsem.at[per * i + 6], (x, y, 1 - c)))
        return copies

    def start(ins, outs, ssem, rsem):
        for cp in first_copies(ins, outs, ssem, rsem):
            cp.start()

    def finish(ins, outs, ssem, rsem):
        x, y, c, chip, others = _mesh_pos()
        sibling = (x, y, 1 - c)
        forwards = []
        for i in range(len(layers)):
            mine = _half_rows(c, ins[i].shape[1] // 2)
            for j, (ox, oy) in enumerate(others):
                rows = outs[i].at[2 * ox + oy, mine, :]
                _remote(rows, rows, ssem.at[per * i + j], rsem.at[per * i + j], (ox, oy, c)).wait_recv()
                fwd = _remote(rows, rows, ssem.at[per * i + 3 + j], rsem.at[per * i + 3 + j], sibling)
                fwd.start()
                forwards.append(fwd)
        for i in range(len(layers)):
            theirs = _half_rows(1 - c, ins[i].shape[1] // 2)
            for j, (ox, oy) in enumerate(others):
                rows = outs[i].at[2 * ox + oy, theirs, :]
                _remote(rows, rows, ssem.at[per * i + 3 + j], rsem.at[per * i + 3 + j], sibling).wait_recv()
            own = outs[i].at[chip]
            _remote(own, own, ssem.at[per * i + 6], rsem.at[per * i + 6], sibling).wait_recv()
        for cp in first_copies(ins, outs, ssem, rsem) + forwards:
            cp.wait_send()

    shapes = [jax.ShapeDtypeStruct((N_CHIPS,) + s.shape[1:], s.dtype) for s, _ in pieces]
    return _Rider([s for s, _ in pieces], shapes, per * len(pieces), start, finish)


def _chip_exchange_rider(ps):
    def copies(ins, outs, ssem, rsem):
        x, y, c, chip, others = _mesh_pos()
        return [_remote(ins[i].at[2 * ox + oy], outs[i].at[chip], ssem.at[3 * i + j], rsem.at[3 * i + j], (ox, oy, c))
                for i in range(len(ps)) for j, (ox, oy) in enumerate(others)]

    def start(ins, outs, ssem, rsem):
        for cp in copies(ins, outs, ssem, rsem):
            cp.start()

    def finish(ins, outs, ssem, rsem):
        x, y, c, chip, others = _mesh_pos()
        for i in range(len(ps)):
            for j, (ox, oy) in enumerate(others):
                slot = outs[i].at[2 * ox + oy]
                _remote(slot, slot, ssem.at[3 * i + j], rsem.at[3 * i + j], (ox, oy, c)).wait_recv()
        for cp in copies(ins, outs, ssem, rsem):
            cp.wait_send()

    return _Rider(list(ps), [jax.ShapeDtypeStruct(p.shape, p.dtype) for p in ps], 3 * len(ps), start, finish)


def _pair_exchange_rider(gs):
    def copies(ins, outs, ssem, rsem):
        x, y, c, _, _ = _mesh_pos()
        return [_remote(ins[i].at[:, _half_rows(1 - c, ins[i].shape[1] // 2), :], outs[i],
                        ssem.at[i], rsem.at[i], (x, y, 1 - c)) for i in range(len(gs))]

    def start(ins, outs, ssem, rsem):
        for cp in copies(ins, outs, ssem, rsem):
            cp.start()

    def finish(ins, outs, ssem, rsem):
        for cp in copies(ins, outs, ssem, rsem):
            cp.wait()

    shapes = [jax.ShapeDtypeStruct((g.shape[0], g.shape[1] // 2, g.shape[2]), g.dtype) for g in gs]
    return _Rider(list(gs), shapes, len(gs), start, finish)


def _run_rider(rider, name):
    def body(*refs):
        n = len(rider.ins)
        ins, outs = refs[:n], refs[n:2 * n]
        ssem, rsem = refs[2 * n:]
        rider.start(ins, outs, ssem, rsem)
        rider.finish(ins, outs, ssem, rsem)

    rider.results = pl.pallas_call(
        body,
        name=name,
        in_specs=[ANY] * len(rider.ins),
        out_specs=[ANY] * len(rider.ins),
        out_shape=rider.out_shapes,
        input_output_aliases=rider.aliases(0, 0),
        scratch_shapes=[pltpu.SemaphoreType.DMA((rider.n_sem,)), pltpu.SemaphoreType.DMA((rider.n_sem,))],
    )(*rider.ins)
    return rider.results


def _pair_gather_rider(fs):
    n = len(fs)

    def copies(outs, ssem, rsem):
        x, y, c, _, _ = _mesh_pos()
        halves = [outs[i].at[_half_rows(c, outs[i].shape[0] // 2), :] for i in range(n)]
        return [_remote(h, h, ssem.at[i], rsem.at[i], (x, y, 1 - c)) for i, h in enumerate(halves)]

    def start(ins, outs, ssem, rsem):
        for cp in copies(outs, ssem, rsem):
            cp.start()

    def finish(ins, outs, ssem, rsem):
        x, y, c, _, _ = _mesh_pos()
        for i in range(n):
            theirs = outs[i].at[_half_rows(1 - c, outs[i].shape[0] // 2), :]
            _remote(theirs, theirs, ssem.at[i], rsem.at[i], (x, y, 1 - c)).wait_recv()
        for cp in copies(outs, ssem, rsem):
            cp.wait_send()

    return _Rider(list(fs), [jax.ShapeDtypeStruct(f.shape, f.dtype) for f in fs], n, start, finish, in_place=True)


def _call(body, *, name, grid, in_specs, out_specs, out_shape, operands, scratch=(), sem=None, rider=None):
    if rider is None:
        return pl.pallas_call(
            body, name=name, grid=grid, in_specs=in_specs, out_specs=out_specs, out_shape=out_shape,
            scratch_shapes=list(scratch), compiler_params=_params(sem))(*operands)
    n_in, n_out, n_sc, r = len(in_specs), len(out_specs), len(scratch), len(rider.ins)

    def carrying(*refs):
        a, b = n_in, n_in + r
        c, d = b + n_out, b + n_out + r
        e = d + n_sc
        ids = [pl.program_id(k) for k in range(len(grid))]
        first = functools.reduce(jnp.logical_and, [i == 0 for i in ids])
        last = functools.reduce(jnp.logical_and, [i == g - 1 for i, g in zip(ids, grid)])

        @pl.when(first)
        def _():
            rider.start(refs[a:b], refs[c:d], refs[e], refs[e + 1])

        body(*refs[:a], *refs[b:c], *refs[d:e])

        @pl.when(last)
        def _():
            rider.finish(refs[a:b], refs[c:d], refs[e], refs[e + 1])

    outs = pl.pallas_call(
        carrying, name=name, grid=grid,
        in_specs=list(in_specs) + [ANY] * r,
        out_specs=list(out_specs) + [ANY] * r,
        out_shape=list(out_shape) + list(rider.out_shapes),
        scratch_shapes=list(scratch) + [pltpu.SemaphoreType.DMA((rider.n_sem,)), pltpu.SemaphoreType.DMA((rider.n_sem,))],
        input_output_aliases=rider.aliases(n_in, n_out),
        compiler_params=_params(("arbitrary",) * len(grid)),
    )(*operands, *rider.ins)
    rider.results = outs[n_out:]
    return outs[:n_out]


def _ffn_fwd(h, gain, wg, wu, wd, layer, tm, name, rider=None):
    T, D = h.shape
    Fs = wg.shape[-1]
    F = N_CHIPS * Fs

    def body(h_ref, g_ref, wg_ref, wu_ref, wd_ref, ho_ref, a_ref, go_ref, uo_ref, act_ref, acc_ref):
        s = pl.program_id(1)

        @pl.when(s == 0)
        def _():
            a_ref[...] = _rms_fwd(h_ref[...], g_ref[...]).astype(BF16)
            acc_ref[...] = jnp.zeros_like(acc_ref)

        a = a_ref[...]
        g = _dot(a, wg_ref[...])
        u = _dot(a, wu_ref[...])
        sg = _sigmoid(g)
        act = (g * sg * u).astype(BF16)
        go_ref[...] = (u * (sg * (1.0 + g * (1.0 - sg)))).astype(BF16)
        uo_ref[...] = (g * sg).astype(BF16)
        act_ref[...] = act
        acc_ref[...] += _dot(act, wd_ref[...])

        @pl.when(s == N_CHIPS - 1)
        def _():
            ho_ref[...] = h_ref[...] + 0.5 * acc_ref[...]

    row = pl.BlockSpec((tm, D), lambda t, s: (t, 0))
    col = pl.BlockSpec((tm, Fs), lambda t, s: (t, s))
    wcol = pl.BlockSpec((None, D, Fs), lambda t, s: (s, 0, 0))
    return _call(
        body, name=name, grid=(T // tm, N_CHIPS),
        in_specs=[row, pl.BlockSpec((None, 1, D), lambda t, s: (layer, 0, 0)), wcol, wcol,
                  pl.BlockSpec((None, Fs, D), lambda t, s: (s, 0, 0))],
        out_specs=[row, row, col, col, col],
        out_shape=[jax.ShapeDtypeStruct((T, D), F32), jax.ShapeDtypeStruct((T, D), BF16)]
        + [jax.ShapeDtypeStruct((T, F), BF16)] * 3,
        scratch=[pltpu.VMEM((tm, D), F32)],
        sem=("parallel", "arbitrary"), operands=(h, gain, wg, wu, wd), rider=rider)


def _ffn_fwd_up(h, gain, wg, wu, layer, tm, name, rider=None):
    T, D = h.shape
    Fs = wg.shape[-1]
    F = N_CHIPS * Fs

    def body(h_ref, g_ref, wg_ref, wu_ref, a_ref, go_ref, uo_ref, act_ref):
        @pl.when(pl.program_id(1) == 0)
        def _():
            a_ref[...] = _rms_fwd(h_ref[...], g_ref[...]).astype(BF16)

        a = a_ref[...]
        g = _dot(a, wg_ref[...])
        u = _dot(a, wu_ref[...])
        sg = _sigmoid(g)
        act_ref[...] = (g * sg * u).astype(BF16)
        go_ref[...] = (u * (sg * (1.0 + g * (1.0 - sg)))).astype(BF16)
        uo_ref[...] = (g * sg).astype(BF16)

    row = pl.BlockSpec((tm, D), lambda t, s: (t, 0))
    col = pl.BlockSpec((tm, Fs), lambda t, s: (t, s))
    wcol = pl.BlockSpec((None, D, Fs), lambda t, s: (s, 0, 0))
    return _call(
        body, name=name, grid=(T // tm, N_CHIPS),
        in_specs=[row, pl.BlockSpec((None, 1, D), lambda t, s: (layer, 0, 0)), wcol, wcol],
        out_specs=[row, col, col, col],
        out_shape=[jax.ShapeDtypeStruct((T, D), BF16)] + [jax.ShapeDtypeStruct((T, F), BF16)] * 3,
        sem=("parallel", "arbitrary"), operands=(h, gain, wg, wu), rider=rider)


def _ffn_fwd_down(h, act, wd, tm, name, rider=None):
    T, D = h.shape
    Fs = wd.shape[1]

    def body(h_ref, act_ref, wd_ref, ho_ref, acc_ref):
        s = pl.program_id(1)

        @pl.when(s == 0)
        def _():
            acc_ref[...] = jnp.zeros_like(acc_ref)

        acc_ref[...] += _dot(act_ref[...], wd_ref[...])

        @pl.when(s == N_CHIPS - 1)
        def _():
            ho_ref[...] = h_ref[...] + 0.5 * acc_ref[...]

    row = pl.BlockSpec((tm, D), lambda t, s: (t, 0))
    return _call(
        body, name=name, grid=(T // tm, N_CHIPS),
        in_specs=[row, pl.BlockSpec((tm, Fs), lambda t, s: (t, s)), pl.BlockSpec((None, Fs, D), lambda t, s: (s, 0, 0))],
        out_specs=[row],
        out_shape=[jax.ShapeDtypeStruct((T, D), F32)],
        scratch=[pltpu.VMEM((tm, D), F32)],
        sem=("parallel", "arbitrary"), operands=(h, act, wd), rider=rider)[0]


def _inproj_fwd(h, gain, win, layer, tm, name, rider=None):
    T, D = h.shape
    Ns = win.shape[-1]

    def body(h_ref, g_ref, w_ref, z_ref, b_ref):
        @pl.when(pl.program_id(1) == 0)
        def _():
            b_ref[...] = _rms_fwd(h_ref[...], g_ref[...]).astype(BF16)

        z_ref[...] = _dot(b_ref[...], w_ref[...]).astype(BF16)

    return _call(
        body, name=name, grid=(T // tm, N_CHIPS),
        in_specs=[pl.BlockSpec((tm, D), lambda t, s: (t, 0)),
                  pl.BlockSpec((None, 1, D), lambda t, s: (layer, 0, 0)),
                  pl.BlockSpec((None, D, Ns), lambda t, s: (s, 0, 0))],
        out_specs=[pl.BlockSpec((tm, Ns), lambda t, s: (t, s)), pl.BlockSpec((tm, D), lambda t, s: (t, 0))],
        out_shape=[jax.ShapeDtypeStruct((T, N_CHIPS * Ns), BF16), jax.ShapeDtypeStruct((T, D), BF16)],
        sem=("parallel", "arbitrary"), operands=(h, gain, win), rider=rider)


def _ret_consts(T, pad):
    half = HEAD_DIM // 2
    inv_freq = ROPE_BASE ** (-jnp.arange(half, dtype=F32) / half)
    pos = jnp.arange(T, dtype=F32) - pad
    ang = pos[:, None] * inv_freq[None, :]
    cos = jnp.cos(ang)
    sin = jnp.sin(ang)
    cosf = jnp.concatenate([cos, cos], axis=1)
    sinf = jnp.concatenate([-sin, sin], axis=1)
    log_gamma = jnp.log1p(-(2.0 ** (-5.0 - jnp.arange(RET_HEADS, dtype=F32))))
    idx = jnp.arange(CHUNK, dtype=F32)
    diff = idx[:, None] - idx[None, :]
    intra = jnp.where(diff[None] >= 0, jnp.exp(diff[None] * log_gamma[:, None, None]), 0.0)
    k_decay = jnp.exp((CHUNK - 1.0 - idx)[None, :] * log_gamma[:, None])
    q_decay = jnp.exp((idx + 1.0)[None, :] * log_gamma[:, None])
    chunk_decay = jnp.exp(CHUNK * log_gamma)
    kdec = jnp.broadcast_to(k_decay[:, :, None], (RET_HEADS, CHUNK, HEAD_DIM))
    qdec = jnp.broadcast_to(q_decay[:, :, None], (RET_HEADS, CHUNK, HEAD_DIM))
    cdb = jnp.broadcast_to(chunk_decay[:, None, None], (RET_HEADS, 8, HEAD_DIM))
    return cosf, sinf, intra, kdec, qdec, cdb


def _rot(t, cosv, sinv):
    return t * cosv + pltpu.roll(t, HEAD_DIM // 2, 1) * sinv


def _rot_t(g, cosv, sinv):
    return g * cosv + pltpu.roll(g * sinv, HEAD_DIM // 2, 1)


def _head_specs(tg, section, order):
    return pl.BlockSpec((tg, HEAD_DIM), lambda h, g: (order(g), section * RET_HEADS + h))


def _ret_fwd(z, consts, cg, name, rider=None):
    T = z.shape[0]
    N = T // CHUNK
    ng = N // cg
    tg = cg * CHUNK
    cosf, sinf, intra, kdec, qdec, cdb = consts
    fwd = lambda g: g

    def body(zq, zk, zv, zg, cos_ref, sin_ref, m_ref, kd_ref, qd_ref, cd_ref, r_ref, o_ref, s_ref, st_ref):
        @pl.when(pl.program_id(1) == 0)
        def _():
            st_ref[...] = jnp.zeros_like(st_ref)

        cosv = cos_ref[...]
        sinv = sin_ref[...]
        q3 = (_rot(zq[...].astype(F32), cosv, sinv) * (HEAD_DIM ** -0.5)).reshape(cg, CHUNK, HEAD_DIM)
        k3 = _rot(zk[...].astype(F32), cosv, sinv).reshape(cg, CHUNK, HEAD_DIM)
        vb = zv[...].reshape(cg, CHUNK, HEAD_DIM).astype(BF16)
        scores = _ein("ncd,nmd->ncm", q3.astype(BF16), k3.astype(BF16)) * m_ref[...][None]
        inner = _ein("ncm,nmd->ncd", scores.astype(BF16), vb)
        kv = _ein("ncd,nce->nde", (k3 * kd_ref[...][None]).astype(BF16), vb)
        cd = cd_ref[0:1, :]
        state = st_ref[...]
        for n in range(cg):
            s_ref[n] = state
            state = state * cd + kv[n]
        st_ref[...] = state
        qdb = (q3 * qd_ref[...][None]).astype(BF16)
        cross = _ein("ncd,nde->nce", qdb, s_ref[...].astype(BF16))
        out = (inner + cross).reshape(tg, HEAD_DIM)
        o_ref[...] = out
        xc = out - jnp.mean(out, axis=-1, keepdims=True)
        rn = xc * lax.rsqrt(jnp.mean(xc * xc, axis=-1, keepdims=True) + EPS)
        g = zg[...].astype(F32)
        r_ref[...] = (rn * (g * _sigmoid(g))).astype(BF16)

    tab = pl.BlockSpec((tg, HEAD_DIM), lambda h, g: (g, 0))
    per_head = lambda rows: pl.BlockSpec((None, rows, HEAD_DIM), lambda h, g: (h, 0, 0))
    head_out = pl.BlockSpec((tg, HEAD_DIM), lambda h, g: (g, h))
    return _call(
        body, name=name, grid=(RET_HEADS, ng),
        in_specs=[_head_specs(tg, i, fwd) for i in range(4)]
        + [tab, tab, per_head(CHUNK), per_head(CHUNK), per_head(CHUNK), per_head(8)],
        out_specs=[head_out, head_out, pl.BlockSpec((None, cg, HEAD_DIM, HEAD_DIM), lambda h, g: (h, g, 0, 0))],
        out_shape=[jax.ShapeDtypeStruct((T, RET_WIDTH), BF16), jax.ShapeDtypeStruct((T, RET_WIDTH), F32),
                   jax.ShapeDtypeStruct((RET_HEADS, N, HEAD_DIM, HEAD_DIM), F32)],
        scratch=[pltpu.VMEM((HEAD_DIM, HEAD_DIM), F32)],
        sem=("parallel", "arbitrary"), operands=(z, z, z, z, cosf, sinf, intra, kdec, qdec, cdb), rider=rider)


def _window_sums(u, shift_of):
    sums = []
    s = u
    k = 1
    while k < POOL_WINDOWS[-1]:
        s = s + pltpu.roll(s, shift_of(k), 0)
        sums.append(s)
        k *= 2
    return sums


def _select_group(vals, g):
    out = vals[-1]
    for i in range(len(vals) - 2, -1, -1):
        out = jnp.where(g == i, vals[i], out)
    return out


def _pool_parts(u, g, T, pad):
    rows = lax.broadcasted_iota(jnp.int32, (T, HEAD_DIM), 0)
    valid = rows >= pad
    win = _select_group([float(w) for w in POOL_WINDOWS], g)
    div = jnp.clip((rows - pad + 1).astype(F32), 1.0, win)
    s = _select_group(_window_sums(u, lambda k: k), g)
    pooled = jnp.where(valid, s / div - u, 0.0)
    return pooled, div, valid


def _pool_specs(T, layer):
    first = 4 * RET_WIDTH // HEAD_DIM
    return [
        pl.BlockSpec((T, HEAD_DIM), lambda g: (0, first + g)),
        pl.BlockSpec((None, None, HEAD_DIM, HEAD_DIM), lambda g: (layer, g, 0, 0)),
        pl.BlockSpec((None, 1, HEAD_DIM), lambda g: (layer, 0, g)),
    ]


def _pool_fwd(z, maps, scale, layer, pad, name):
    T = z.shape[0]
    assert pad >= POOL_WINDOWS[-1], "window rolls wrap into the zero rows in front"

    def body(zu, maps_ref, sc_ref, pm_ref):
        g = pl.program_id(0)
        pooled, _, _ = _pool_parts(zu[...].astype(F32), g, T, pad)
        y = _dot(pooled.astype(BF16), maps_ref[...].astype(BF16))
        pm_ref[...] = (y * sc_ref[...]).astype(BF16)

    return _call(
        body, name=name, grid=(POOL_GROUPS,),
        in_specs=_pool_specs(T, layer),
        out_specs=[pl.BlockSpec((T, HEAD_DIM), lambda g: (0, g))],
        out_shape=[jax.ShapeDtypeStruct((T, POOL_WIDTH), BF16)],
        sem=("parallel",), operands=(z, maps, scale))[0]


def _gate_specs(tm, D):
    nb = D // RET_WIDTH
    first = (4 * RET_WIDTH + POOL_WIDTH) // RET_WIDTH
    return [pl.BlockSpec((tm, RET_WIDTH), functools.partial(lambda t, j: (t, j), j=first + j)) for j in range(2 * nb)]


def _load_gates(refs, nb):
    ga = jnp.concatenate([r[...].astype(F32) for r in refs[:nb]], axis=1)
    gb = jnp.concatenate([r[...].astype(F32) for r in refs[nb:]], axis=1)
    return ga, gb


def _mix_fwd(h, r, pm, z, wru, wpu, wout, tm, name, rider=None):
    T, D = h.shape
    Dq = D // N_CHIPS
    nb = D // RET_WIDTH

    def body(*refs):
        h_ref, r_ref, pm_ref = refs[:3]
        gate_refs = refs[3:3 + 2 * nb]
        wru_ref, wpu_ref, wout_ref, ho_ref, mx_ref, ret_ref, pool_ref = refs[3 + 2 * nb:]
        rv = r_ref[...]
        pv = pm_ref[...]
        ret = jnp.concatenate([_dot(rv, wru_ref[s]) for s in range(N_CHIPS)], axis=1)
        pool = jnp.concatenate([_dot(pv, wpu_ref[s]) for s in range(N_CHIPS)], axis=1)
        ga, gb = _load_gates(gate_refs, nb)
        mixed = (_sigmoid(ga) * ret + _sigmoid(gb) * pool).astype(BF16)
        mx_ref[...] = mixed
        ret_ref[...] = ret.astype(BF16)
        pool_ref[...] = pool.astype(BF16)
        ho_ref[...] = h_ref[...] + _dot(mixed, wout_ref[...].reshape(D, D))

    row = pl.BlockSpec((tm, D), lambda t: (t, 0))
    half = pl.BlockSpec((tm, RET_WIDTH), lambda t: (t, 0))
    up = pl.BlockSpec((N_CHIPS, RET_WIDTH, Dq), lambda t: (0, 0, 0))
    return _call(
        body, name=name, grid=(T // tm,),
        in_specs=[row, half, half] + _gate_specs(tm, D) + [up, up, pl.BlockSpec((N_CHIPS, Dq, D), lambda t: (0, 0, 0))],
        out_specs=[row, row, row, row],
        out_shape=[jax.ShapeDtypeStruct((T, D), F32)] + [jax.ShapeDtypeStruct((T, D), BF16)] * 3,
        sem=("parallel",), operands=(h, r, pm, *([z] * (2 * nb)), wru, wpu, wout), rider=rider)


def _final_loss(h, gain, tgt, name):
    T, D = h.shape
    first = (T - tgt.shape[0]) // CHUNK

    def body(h_ref, g_ref, t_ref, dh_ref, loss_ref, dg_ref):
        i = pl.program_id(0)

        @pl.when(i == 0)
        def _():
            loss_ref[...] = jnp.zeros_like(loss_ref)
            dg_ref[...] = jnp.zeros_like(dg_ref)

        x = h_ref[...]
        gain_v = g_ref[...]
        err = jnp.where(i >= first, _rms_fwd(x, gain_v) - t_ref[...], 0.0)
        loss_ref[...] += 0.5 * jnp.sum(jnp.mean(err * err, axis=-1))
        dx, dgain = _rms_bwd(x, gain_v, err * (1.0 / D))
        dg_ref[...] += dgain
        dh_ref[...] = dx

    return _call(
        body, name=name, grid=(T // CHUNK,),
        in_specs=[pl.BlockSpec((CHUNK, D), lambda i: (i, 0)),
                  pl.BlockSpec((1, D), lambda i: (0, 0)),
                  pl.BlockSpec((CHUNK, D), lambda i: (jnp.maximum(i - first, 0), 0))],
        out_specs=[pl.BlockSpec((CHUNK, D), lambda i: (i, 0)),
                   pl.BlockSpec((1, LANES), lambda i: (0, 0)),
                   pl.BlockSpec((1, D), lambda i: (0, 0))],
        out_shape=[jax.ShapeDtypeStruct((T, D), F32), jax.ShapeDtypeStruct((1, LANES), F32),
                   jax.ShapeDtypeStruct((1, D), F32)],
        sem=("arbitrary",), operands=(h, gain, tgt))


def _ffn_bwd_act(dy, g, u, wd, tm, name, rider=None):
    T, D = dy.shape
    Fs = wd.shape[1]
    F = N_CHIPS * Fs

    def body(dy_ref, go_ref, uo_ref, wd_ref, dg_ref, du_ref, dyh_ref):
        @pl.when(pl.program_id(1) == 0)
        def _():
            dyh_ref[...] = (0.5 * dy_ref[...]).astype(BF16)

        dact = _dot_nt(dyh_ref[...], wd_ref[...])
        du_ref[...] = (dact * uo_ref[...].astype(F32)).astype(BF16)
        dg_ref[...] = (dact * go_ref[...].astype(F32)).astype(BF16)

    row = pl.BlockSpec((tm, D), lambda t, s: (t, 0))
    col = pl.BlockSpec((tm, Fs), lambda t, s: (t, s))
    return _call(
        body, name=name, grid=(T // tm, N_CHIPS),
        in_specs=[row, col, col, pl.BlockSpec((None, Fs, D), lambda t, s: (s, 0, 0))],
        out_specs=[col, col, row],
        out_shape=[jax.ShapeDtypeStruct((T, F), BF16), jax.ShapeDtypeStruct((T, F), BF16),
                   jax.ShapeDtypeStruct((T, D), BF16)],
        sem=("parallel", "arbitrary"), operands=(dy, g, u, wd), rider=rider)


def _ffn_bwd_in(dy, h, gain, dg, du, wg, wu, layer, tm, pad, name, rider=None):
    T, D = h.shape
    Fs = wg.shape[-1]

    def body(dy_ref, h_ref, g_ref, dg_ref, du_ref, wg_ref, wu_ref, dh_ref, dgain_ref, da_ref):
        t = pl.program_id(0)
        s = pl.program_id(1)

        @pl.when((t == 0) & (s == 0))
        def _():
            dgain_ref[...] = jnp.zeros_like(dgain_ref)

        @pl.when(s == 0)
        def _():
            da_ref[...] = jnp.zeros_like(da_ref)

        da_ref[...] += _dot_nt(dg_ref[...], wg_ref[...]) + _dot_nt(du_ref[...], wu_ref[...])

        @pl.when(s == N_CHIPS - 1)
        def _():
            dx, dgain = _rms_bwd(h_ref[...], g_ref[...], da_ref[...])
            dgain_ref[...] += dgain
            dh_ref[...] = jnp.where(_row_mask(t, tm, pad, (tm, D)), dy_ref[...] + dx, 0.0)

    row = pl.BlockSpec((tm, D), lambda t, s: (t, 0))
    col = pl.BlockSpec((tm, Fs), lambda t, s: (t, s))
    wcol = pl.BlockSpec((None, D, Fs), lambda t, s: (s, 0, 0))
    return _call(
        body, name=name, grid=(T // tm, N_CHIPS),
        in_specs=[row, row, pl.BlockSpec((None, 1, D), lambda t, s: (layer, 0, 0)), col, col, wcol, wcol],
        out_specs=[row, pl.BlockSpec((1, D), lambda t, s: (0, 0))],
        out_shape=[jax.ShapeDtypeStruct((T, D), F32), jax.ShapeDtypeStruct((1, D), F32)],
        scratch=[pltpu.VMEM((tm, D), F32)],
        sem=("arbitrary", "arbitrary"), operands=(dy, h, gain, dg, du, wg, wu), rider=rider)


def _grad_tn(a, b, mode, scale, tm, name, rider=None):
    T = a.shape[0]
    if mode == "col":
        per, R, C = 1, a.shape[1], b.shape[1] // N_CHIPS
        a_spec = pl.BlockSpec((tm, R), lambda s, t: (t, 0))
        b_spec = pl.BlockSpec((tm, C), lambda s, t: (t, s))
    else:
        per, R, C = 2, a.shape[1] // N_CHIPS, b.shape[1]
        a_spec = pl.BlockSpec((tm, per * R), lambda s, t: (t, s))
        b_spec = pl.BlockSpec((tm, C), lambda s, t: (t, 0))
    nt = T // tm

    def body(a_ref, b_ref, o_ref, acc_ref):
        t = pl.program_id(1)

        @pl.when(t == 0)
        def _():
            acc_ref[...] = jnp.zeros_like(acc_ref)

        acc_ref[...] += _dot_tn(a_ref[...].astype(BF16), b_ref[...].astype(BF16))

        @pl.when(t == nt - 1)
        def _():
            o_ref[...] = (scale * acc_ref[...]).astype(BF16).reshape(per, R, C)

    return _call(
        body, name=name, grid=(N_CHIPS // per, nt),
        in_specs=[a_spec, b_spec],
        out_specs=[pl.BlockSpec((per, R, C), lambda s, t: (s, 0, 0))],
        out_shape=[jax.ShapeDtypeStruct((N_CHIPS, R, C), BF16)],
        scratch=[pltpu.VMEM((per * R, C), F32)],
        sem=("parallel", "arbitrary"), operands=(a, b), rider=rider)[0]


def _grad_mix(mixed, dh, r, dret, pm, dpool, tk, name, rider=None):
    T, D = dh.shape
    Dq = D // N_CHIPS
    nt = T // tk

    def body(mx_ref, dh_ref, r_ref, dret_ref, pm_ref, dpool_ref, go_ref, gr_ref, gp_ref, ao_ref, ar_ref, ap_ref):
        t = pl.program_id(0)

        @pl.when(t == 0)
        def _():
            ao_ref[...] = jnp.zeros_like(ao_ref)
            ar_ref[...] = jnp.zeros_like(ar_ref)
            ap_ref[...] = jnp.zeros_like(ap_ref)

        ao_ref[...] += _dot_tn(mx_ref[...], dh_ref[...].astype(BF16))
        ar_ref[...] += _dot_tn(r_ref[...], dret_ref[...])
        ap_ref[...] += _dot_tn(pm_ref[...], dpool_ref[...])

        @pl.when(t == nt - 1)
        def _():
            go_ref[...] = ao_ref[...].astype(BF16).reshape(N_CHIPS, Dq, D)
            for s in range(N_CHIPS):
                gr_ref[s] = ar_ref[:, s * Dq:(s + 1) * Dq].astype(BF16)
                gp_ref[s] = ap_ref[:, s * Dq:(s + 1) * Dq].astype(BF16)

    row = pl.BlockSpec((tk, D), lambda t: (t, 0))
    half = pl.BlockSpec((tk, RET_WIDTH), lambda t: (t, 0))
    whole = lambda shape: pl.BlockSpec(shape, lambda t: (0, 0, 0))
    return _call(
        body, name=name, grid=(nt,),
        in_specs=[row, row, half, row, half, row],
        out_specs=[whole((N_CHIPS, Dq, D)), whole((N_CHIPS, RET_WIDTH, Dq)), whole((N_CHIPS, POOL_WIDTH, Dq))],
        out_shape=[jax.ShapeDtypeStruct((N_CHIPS, Dq, D), BF16),
                   jax.ShapeDtypeStruct((N_CHIPS, RET_WIDTH, Dq), BF16),
                   jax.ShapeDtypeStruct((N_CHIPS, POOL_WIDTH, Dq), BF16)],
        scratch=[pltpu.VMEM((D, D), F32), pltpu.VMEM((RET_WIDTH, D), F32), pltpu.VMEM((POOL_WIDTH, D), F32)],
        sem=("arbitrary",), operands=(mixed, dh, r, dret, pm, dpool), rider=rider)


def _mix_bwd_dx(dh, z, ret, pool, wout, wru, wpu, tm, name, rider=None):
    T, D = dh.shape
    Dq = D // N_CHIPS
    nb = D // RET_WIDTH

    def body(*refs):
        dh_ref = refs[0]
        gate_refs = refs[1:1 + 2 * nb]
        ret_ref, pool_ref, wout_ref, wru_ref, wpu_ref, dgab_ref, dret_ref, dpool_ref, dr_ref, dpm_ref = refs[1 + 2 * nb:]
        dmixed = _dot_nt(dh_ref[...].astype(BF16), wout_ref[...].reshape(D, D))
        ga, gb = _load_gates(gate_refs, nb)
        sa = _sigmoid(ga)
        sb = _sigmoid(gb)
        dgab_ref[:, :D] = (dmixed * ret_ref[...].astype(F32) * (sa * (1.0 - sa))).astype(BF16)
        dgab_ref[:, D:] = (dmixed * pool_ref[...].astype(F32) * (sb * (1.0 - sb))).astype(BF16)
        dret = (dmixed * sa).astype(BF16)
        dpool = (dmixed * sb).astype(BF16)
        dret_ref[...] = dret
        dpool_ref[...] = dpool
        dr = _dot_nt(dret[:, :Dq], wru_ref[0])
        dpm = _dot_nt(dpool[:, :Dq], wpu_ref[0])
        for s in range(1, N_CHIPS):
            dr += _dot_nt(dret[:, s * Dq:(s + 1) * Dq], wru_ref[s])
            dpm += _dot_nt(dpool[:, s * Dq:(s + 1) * Dq], wpu_ref[s])
        dr_ref[...] = dr
        dpm_ref[...] = dpm

    row = pl.BlockSpec((tm, D), lambda t: (t, 0))
    half = pl.BlockSpec((tm, RET_WIDTH), lambda t: (t, 0))
    up = pl.BlockSpec((N_CHIPS, RET_WIDTH, Dq), lambda t: (0, 0, 0))
    return _call(
        body, name=name, grid=(T // tm,),
        in_specs=[row] + _gate_specs(tm, D) + [row, row, pl.BlockSpec((N_CHIPS, Dq, D), lambda t: (0, 0, 0)), up, up],
        out_specs=[pl.BlockSpec((tm, 2 * D), lambda t: (t, 0)), row, row, half, half],
        out_shape=[jax.ShapeDtypeStruct((T, 2 * D), BF16), jax.ShapeDtypeStruct((T, D), BF16),
                   jax.ShapeDtypeStruct((T, D), BF16), jax.ShapeDtypeStruct((T, RET_WIDTH), F32),
                   jax.ShapeDtypeStruct((T, POOL_WIDTH), F32)],
        sem=("parallel",), operands=(dh, *([z] * (2 * nb)), ret, pool, wout, wru, wpu), rider=rider)


def _pool_bwd(z, dpm, maps, scale, layer, pad, name):
    T = z.shape[0]

    def body(zu, maps_ref, sc_ref, dpm_ref, du_ref, dmaps_ref, dsc_ref):
        g = pl.program_id(0)
        u = zu[...].astype(F32)
        pooled, div, valid = _pool_parts(u, g, T, pad)
        pb = pooled.astype(BF16)
        mb = maps_ref[...].astype(BF16)
        dp = dpm_ref[...]
        dsc_ref[...] = jnp.sum(dp * _dot(pb, mb), axis=0, keepdims=True)
        dyb = (dp * sc_ref[...]).astype(BF16)
        dmaps_ref[...] = _dot_tn(pb, dyb)
        dpooled = jnp.where(valid, _dot_nt(dyb, mb), 0.0)
        ahead = _select_group(_window_sums(dpooled / div, lambda k: T - k), g)
        du_ref[...] = jnp.where(valid, ahead - dpooled, 0.0).astype(BF16)

    blk = pl.BlockSpec((T, HEAD_DIM), lambda g: (0, g))
    return _call(
        body, name=name, grid=(POOL_GROUPS,),
        in_specs=_pool_specs(T, layer) + [blk],
        out_specs=[blk, pl.BlockSpec((None, HEAD_DIM, HEAD_DIM), lambda g: (g, 0, 0)),
                   pl.BlockSpec((1, HEAD_DIM), lambda g: (0, g))],
        out_shape=[jax.ShapeDtypeStruct((T, POOL_WIDTH), BF16),
                   jax.ShapeDtypeStruct((POOL_GROUPS, HEAD_DIM, HEAD_DIM), F32),
                   jax.ShapeDtypeStruct((1, POOL_WIDTH), F32)],
        sem=("parallel",), operands=(z, maps, scale, dpm))


def _ret_bwd_local(z, o_pre, s_all, dr, consts, cg, name):
    T = z.shape[0]
    N = T // CHUNK
    ng = N // cg
    tg = cg * CHUNK
    cosf, sinf, intra, _, qdec, _ = consts
    fwd = lambda g: g

    def body(zq, zk, zv, zg, o_ref, s_ref, dr_ref, cos_ref, sin_ref, m_ref, qd_ref,
             dq_ref, dg_ref, dk_ref, dv_ref, ds_ref):
        cosv = cos_ref[...]
        sinv = sin_ref[...]
        scale = HEAD_DIM ** -0.5
        q3 = (_rot(zq[...].astype(F32), cosv, sinv) * scale).reshape(cg, CHUNK, HEAD_DIM)
        k3 = _rot(zk[...].astype(F32), cosv, sinv).reshape(cg, CHUNK, HEAD_DIM)
        qb = q3.astype(BF16)
        kb = k3.astype(BF16)
        vb = zv[...].reshape(cg, CHUNK, HEAD_DIM).astype(BF16)
        mask = m_ref[...][None]
        sb = (_ein("ncd,nmd->ncm", qb, kb) * mask).astype(BF16)
        qdv = qd_ref[...][None]
        qdb = (q3 * qdv).astype(BF16)

        out = o_ref[...]
        xc = out - jnp.mean(out, axis=-1, keepdims=True)
        rstd = lax.rsqrt(jnp.mean(xc * xc, axis=-1, keepdims=True) + EPS)
        rn = xc * rstd
        g = zg[...].astype(F32)
        sg = _sigmoid(g)
        drv = dr_ref[...]
        dg_ref[...] = (drv * rn * (sg * (1.0 + g * (1.0 - sg)))).astype(BF16)
        drn = drv * (g * sg)
        dout = rstd * (drn - jnp.mean(drn, axis=-1, keepdims=True)
                       - rn * jnp.mean(drn * rn, axis=-1, keepdims=True))
        dob = dout.reshape(cg, CHUNK, HEAD_DIM).astype(BF16)

        dsb = (_ein("ncd,nmd->ncm", dob, vb) * mask).astype(BF16)
        dv_ref[...] = _ein("ncm,ncd->nmd", sb, dob).reshape(tg, HEAD_DIM)
        dk_ref[...] = _ein("ncm,ncd->nmd", dsb, qb).reshape(tg, HEAD_DIM)
        dq3 = _ein("ncm,nmd->ncd", dsb, kb) + _ein("nce,nde->ncd", dob, s_ref[...].astype(BF16)) * qdv
        dq_ref[...] = _rot_t(dq3.reshape(tg, HEAD_DIM) * scale, cosv, sinv).astype(BF16)
        ds_ref[...] = _ein("ncd,nce->nde", qdb, dob)

    tab = pl.BlockSpec((tg, HEAD_DIM), lambda h, g: (g, 0))
    per_head = pl.BlockSpec((None, CHUNK, HEAD_DIM), lambda h, g: (h, 0, 0))
    head_blk = pl.BlockSpec((tg, HEAD_DIM), lambda h, g: (g, h))
    state_blk = pl.BlockSpec((None, cg, HEAD_DIM, HEAD_DIM), lambda h, g: (h, g, 0, 0))
    return _call(
        body, name=name, grid=(RET_HEADS, ng),
        in_specs=[_head_specs(tg, i, fwd) for i in range(4)]
        + [head_blk, state_blk, head_blk, tab, tab, per_head, per_head],
        out_specs=[head_blk, head_blk, head_blk, head_blk, state_blk],
        out_shape=[jax.ShapeDtypeStruct((T, RET_WIDTH), BF16), jax.ShapeDtypeStruct((T, RET_WIDTH), BF16),
                   jax.ShapeDtypeStruct((T, RET_WIDTH), F32), jax.ShapeDtypeStruct((T, RET_WIDTH), F32),
                   jax.ShapeDtypeStruct((RET_HEADS, N, HEAD_DIM, HEAD_DIM), F32)],
        sem=("parallel", "parallel"), operands=(z, z, z, z, o_pre, s_all, dr, cosf, sinf, intra, qdec))


def _ret_bwd_state(z, dkp, dvp, ds, consts, cg, name):
    T = z.shape[0]
    N = T // CHUNK
    ng = N // cg
    tg = cg * CHUNK
    cosf, sinf, _, kdec, _, cdb = consts
    rev = lambda g: ng - 1 - g

    def body(zk, zv, dkp_ref, dvp_ref, ds_ref, cos_ref, sin_ref, kd_ref, cd_ref, dk_ref, dv_ref, gs_ref, dkv_ref):
        @pl.when(pl.program_id(1) == 0)
        def _():
            gs_ref[...] = jnp.zeros_like(gs_ref)

        cosv = cos_ref[...]
        sinv = sin_ref[...]
        cd = cd_ref[0:1, :]
        grad = gs_ref[...]
        for n in reversed(range(cg)):
            dkv_ref[n] = grad
            grad = ds_ref[n] + cd * grad
        gs_ref[...] = grad
        dkvb = dkv_ref[...].astype(BF16)
        kdv = kd_ref[...][None]
        k3 = _rot(zk[...].astype(F32), cosv, sinv).reshape(cg, CHUNK, HEAD_DIM)
        vb = zv[...].reshape(cg, CHUNK, HEAD_DIM).astype(BF16)
        dk3 = _ein("nce,nde->ncd", vb, dkvb) * kdv
        dv3 = _ein("ncd,nde->nce", (k3 * kdv).astype(BF16), dkvb)
        dk_ref[...] = _rot_t(dkp_ref[...] + dk3.reshape(tg, HEAD_DIM), cosv, sinv).astype(BF16)
        dv_ref[...] = (dvp_ref[...] + dv3.reshape(tg, HEAD_DIM)).astype(BF16)

    tab = pl.BlockSpec((tg, HEAD_DIM), lambda h, g: (rev(g), 0))
    head_blk = pl.BlockSpec((tg, HEAD_DIM), lambda h, g: (rev(g), h))
    return _call(
        body, name=name, grid=(RET_HEADS, ng),
        in_specs=[_head_specs(tg, 1, rev), _head_specs(tg, 2, rev), head_blk, head_blk,
                  pl.BlockSpec((None, cg, HEAD_DIM, HEAD_DIM), lambda h, g: (h, rev(g), 0, 0)),
                  tab, tab,
                  pl.BlockSpec((None, CHUNK, HEAD_DIM), lambda h, g: (h, 0, 0)),
                  pl.BlockSpec((None, 8, HEAD_DIM), lambda h, g: (h, 0, 0))],
        out_specs=[head_blk, head_blk],
        out_shape=[jax.ShapeDtypeStruct((T, RET_WIDTH), BF16)] * 2,
        scratch=[pltpu.VMEM((HEAD_DIM, HEAD_DIM), F32), pltpu.VMEM((cg, HEAD_DIM, HEAD_DIM), F32)],
        sem=("parallel", "arbitrary"), operands=(z, z, dkp, dvp, ds, cosf, sinf, kdec, cdb))


def _z_segments(pieces, ns):
    segs, at = [], 0
    for k, p in enumerate(pieces):
        width = p.shape[1]
        lo = at
        while lo < at + width:
            s = lo // ns
            hi = min(at + width, (s + 1) * ns)
            segs.append((k, lo - at, hi - at, s, lo - s * ns, hi - s * ns))
            lo = hi
        at += width
    assert at == N_CHIPS * ns and all(v % LANES == 0 for seg in segs for v in (seg[1], seg[2], seg[4], seg[5]))
    return segs


def _inproj_bwd_dx(pieces, win, h, gain, dh_in, layer, tm, pad, name, rider=None):
    T, D = h.shape
    Ns = win.shape[-1]
    n = len(pieces)
    segs = _z_segments(pieces, Ns)

    def body(*refs):
        piece_refs = refs[:n]
        w_ref, h_ref, g_ref, dhi_ref, dh_ref, dgain_ref = refs[n:]
        t = pl.program_id(0)

        @pl.when(t == 0)
        def _():
            dgain_ref[...] = jnp.zeros_like(dgain_ref)

        db = None
        for k, a, b, s, c, d in segs:
            term = _dot_nt(piece_refs[k][:, a:b], w_ref[s, :, c:d])
            db = term if db is None else db + term
        dx, dgain = _rms_bwd(h_ref[...], g_ref[...], db)
        dgain_ref[...] += dgain
        dh_ref[...] = jnp.where(_row_mask(t, tm, pad, (tm, D)), dhi_ref[...] + dx, 0.0)

    row = pl.BlockSpec((tm, D), lambda t: (t, 0))
    return _call(
        body, name=name, grid=(T // tm,),
        in_specs=[pl.BlockSpec((tm, p.shape[1]), lambda t: (t, 0)) for p in pieces]
        + [pl.BlockSpec((N_CHIPS, D, Ns), lambda t: (0, 0, 0)), row,
           pl.BlockSpec((None, 1, D), lambda t: (layer, 0, 0)), row],
        out_specs=[row, pl.BlockSpec((1, D), lambda t: (0, 0))],
        out_shape=[jax.ShapeDtypeStruct((T, D), F32), jax.ShapeDtypeStruct((1, D), F32)],
        sem=("arbitrary",), operands=(*pieces, win, h, gain, dh_in), rider=rider)


def _grad_w_in(b, pieces, ns, tk, name, rider=None):
    T, D = b.shape
    n = len(pieces)
    nt = T // tk
    segs = _z_segments(pieces, ns)
    shards_of = [sorted({s for k, _, _, s, _, _ in segs if k == i}) for i in range(n)]

    def body(*refs):
        b_ref = refs[0]
        piece_refs = refs[1:1 + n]
        o_ref, acc_ref = refs[1 + n:]
        s = pl.program_id(0)
        t = pl.program_id(1)

        @pl.when(t == 0)
        def _():
            acc_ref[...] = jnp.zeros_like(acc_ref)

        for shard in range(N_CHIPS):
            @pl.when(s == shard)
            def _(shard=shard):
                cols = [piece_refs[k][:, a:e] for k, a, e, ss, _, _ in segs if ss == shard]
                dz = cols[0] if len(cols) == 1 else jnp.concatenate(cols, axis=1)
                acc_ref[...] += _dot_tn(b_ref[...], dz)

        @pl.when(t == nt - 1)
        def _():
            o_ref[...] = acc_ref[...].astype(BF16).reshape(1, D, ns)

    def piece_spec(i):
        def index(s, t):
            used = functools.reduce(jnp.logical_or, [s == ss for ss in shards_of[i]])
            return (jnp.where(used, t, 0), 0)
        return pl.BlockSpec((tk, pieces[i].shape[1]), index)

    return _call(
        body, name=name, grid=(N_CHIPS, nt),
        in_specs=[pl.BlockSpec((tk, D), lambda s, t: (t, 0))] + [piece_spec(i) for i in range(n)],
        out_specs=[pl.BlockSpec((1, D, ns), lambda s, t: (s, 0, 0))],
        out_shape=[jax.ShapeDtypeStruct((N_CHIPS, D, ns), BF16)],
        scratch=[pltpu.VMEM((D, ns), F32)],
        sem=("parallel", "arbitrary"), operands=(b, *pieces), rider=rider)[0]


def _sum_pair(gs, rs, c_idx, name):
    n = len(gs)

    def body(c_ref, *refs):
        for g_ref, r_ref, o_ref in zip(refs[:n], refs[n:2 * n], refs[2 * n:]):
            o_ref[...] = (g_ref[...].astype(F32) + r_ref[...].astype(F32)).astype(BF16)

    halves = [pl.BlockSpec((None,) + r.shape[1:], lambda s, c_ref: (s, 0, 0)) for r in rs]
    return pl.pallas_call(
        body,
        name=name,
        grid_spec=pltpu.PrefetchScalarGridSpec(
            num_scalar_prefetch=1,
            grid=(N_CHIPS,),
            in_specs=[pl.BlockSpec((None,) + r.shape[1:], lambda s, c_ref: (s, c_ref[0], 0)) for r in rs] + halves,
            out_specs=halves,
        ),
        out_shape=[jax.ShapeDtypeStruct(r.shape, BF16) for r in rs],
        compiler_params=_params(("parallel",)),
    )(c_idx, *gs, *rs)


def _sum_chips(ps, rs, pos, name):
    n = len(ps)
    quarters = 4

    def body(pos_ref, *refs):
        chip = pos_ref[0]
        for p_ref, r_ref, o_ref in zip(refs[:n], refs[n:2 * n], refs[2 * n:]):
            own = p_ref[...].astype(F32)
            terms = [jnp.where(chip == k, own, r_ref[k].astype(F32)) for k in range(N_CHIPS)]
            o_ref[...] = ((terms[0] + terms[1]) + terms[2]) + terms[3]

    def rows(r):
        assert r.shape[1] % (quarters * BF16_ROWS) == 0, r.shape
        return r.shape[1] // quarters

    return pl.pallas_call(
        body,
        name=name,
        grid_spec=pltpu.PrefetchScalarGridSpec(
            num_scalar_prefetch=1,
            grid=(quarters,),
            in_specs=[pl.BlockSpec((None, rows(r), r.shape[2]), lambda q, pos_ref: (pos_ref[0], q, 0)) for r in rs]
            + [pl.BlockSpec((N_CHIPS, rows(r), r.shape[2]), lambda q, pos_ref: (0, q, 0)) for r in rs],
            out_specs=[pl.BlockSpec((rows(r), r.shape[2]), lambda q, pos_ref: (pos_ref[1] * quarters + q, 0))
                       for r in rs],
        ),
        out_shape=[jax.ShapeDtypeStruct((2 * r.shape[1], r.shape[2]), F32) for r in rs],
        compiler_params=_params(("arbitrary",)),
    )(pos, *ps, *rs)


def _small_all_reduce(p):
    rows, width = p.shape

    def body(p_ref, o_ref, sib_ref, slot_ref, ssem, rsem):
        x, y, c, chip, others = _mesh_pos()
        pair = _remote(p_ref, sib_ref, ssem.at[0], rsem.at[0], (x, y, 1 - c))
        pair.start()
        pair.wait()
        slot_ref[chip] = p_ref[...] + sib_ref[...]
        sends = []
        for j, (ox, oy) in enumerate(others):
            cp = _remote(slot_ref.at[chip], slot_ref.at[chip], ssem.at[1 + j], rsem.at[1 + j], (ox, oy, c))
            cp.start()
            sends.append(cp)
        for j, (ox, oy) in enumerate(others):
            slot = slot_ref.at[2 * ox + oy]
            _remote(slot, slot, ssem.at[1 + j], rsem.at[1 + j], (ox, oy, c)).wait_recv()
        for cp in sends:
            cp.wait_send()
        o_ref[...] = ((slot_ref[0] + slot_ref[1]) + slot_ref[2]) + slot_ref[3]

    vmem = pl.BlockSpec(memory_space=pltpu.VMEM)
    return pl.pallas_call(
        body,
        name="small_grads_all_reduce",
        in_specs=[vmem],
        out_specs=vmem,
        out_shape=jax.ShapeDtypeStruct(p.shape, F32),
        scratch_shapes=[pltpu.VMEM((rows, width), F32), pltpu.VMEM((N_CHIPS, rows, width), F32),
                        pltpu.SemaphoreType.DMA((4,)), pltpu.SemaphoreType.DMA((4,))],
    )(p)


def _adamw(gs, w, m, v, name):
    L, R, C = w.shape
    Ct = gs[0].shape[1]
    tr = _pick_tile(R, 256, 8)

    def body(*refs):
        g_refs = refs[:L]
        w_ref, m_ref, v_ref, go_ref, d_ref, mo_ref, vo_ref = refs[L:]
        layer = pl.program_id(0)
        grad = g_refs[L - 1][...]
        for i in range(L - 2, -1, -1):
            grad = jnp.where(layer == i, g_refs[i][...], grad)
        if Ct != C:
            grad = grad[:, :C]
        m_new = ADAM_B1 * m_ref[...] + (1.0 - ADAM_B1) * grad
        v_new = ADAM_B2 * v_ref[...] + (1.0 - ADAM_B2) * jnp.square(grad)
        m_hat = m_new / (1.0 - ADAM_B1 ** ADAM_STEP)
        v_hat = v_new / (1.0 - ADAM_B2 ** ADAM_STEP)
        go_ref[...] = grad
        d_ref[...] = -ADAM_LR * (m_hat / (jnp.sqrt(v_hat) + ADAM_EPS) + ADAM_WD * w_ref[...])
        mo_ref[...] = m_new
        vo_ref[...] = v_new

    g_specs = [pl.BlockSpec((tr, Ct), functools.partial(lambda l, r, i: (jnp.where(l == i, r, 0), 0), i=i))
               for i in range(L)]
    blk = pl.BlockSpec((None, tr, C), lambda l, r: (l, r, 0))
    return _call(
        body, name=name, grid=(L, R // tr),
        in_specs=g_specs + [blk, blk, blk],
        out_specs=[blk] * 4,
        out_shape=[jax.ShapeDtypeStruct((L, R, C), F32)] * 4,
        sem=("arbitrary", "arbitrary"), operands=(*gs, w, m, v))


_FFN1 = ("ffn1_gate", "ffn1_up", "ffn1_down")
_FFN2 = ("ffn2_gate", "ffn2_up", "ffn2_down")
_MIXW = ("w_ret_up", "w_pool_up", "w_out")
_BIG = _FFN1 + ("w_in",) + _MIXW + _FFN2
_TRANSPOSED = ("ffn1_gate", "ffn1_up", "ffn2_gate", "ffn2_up")
_SMALL = ("ffn1_norm", "mix_norm", "ffn2_norm", "final_norm", "pool_scale", "pool_maps")
_ORDER = ("meta", "ffn1_norm", "ffn1_gate", "ffn1_up", "ffn1_down", "mix_norm", "w_in", "pool_maps",
          "pool_scale", "w_ret_up", "w_pool_up", "w_out", "ffn2_norm", "ffn2_gate", "ffn2_up", "ffn2_down",
          "final_norm")


def _transport(a):
    n, r, c = a.shape
    out = a.astype(BF16)
    if c % LANES:
        out = jnp.concatenate([out, jnp.zeros((n, r, _round_up(c, LANES) - c), BF16)], axis=2)
    if r % LANES:
        out = jnp.concatenate([out, jnp.zeros((n, _round_up(r, LANES) - r, out.shape[2]), BF16)], axis=1)
    return out


def _pack_rows(parts, width):
    rows = [p.reshape(-1, width) for p in parts]
    total = sum(r.shape[0] for r in rows)
    fill = _round_up(total, 8) - total
    if fill:
        rows.append(jnp.zeros((fill, width), F32))
    return jnp.concatenate(rows, axis=0)


def _unpack_rows(packed, shapes, width):
    out, at = [], 0
    for shp in shapes:
        n = math.prod(shp) // width
        out.append(packed[at:at + n].reshape(shp))
        at += n
    return out


class _Weights:
    def __init__(self, shards):
        self.shards = shards
        self.full = {}

    def rider(self, keys):
        r = _gather_rider([(self.shards[n], i) for n, i in keys])
        r.keys = keys
        return r

    def take(self, rider):
        for key, arr in zip(rider.keys, rider.results):
            self.full[key] = arr

    def __call__(self, name, layer):
        return self.full[(name, layer)]


def _local_step(x, meta_full, tgt, w, wts, pad, tm, cg, reducer):
    D = x.shape[1]
    T = pad + N_META + x.shape[0]
    L = w["ffn1_norm"].shape[0]
    pool_maps = w["pool_maps"]
    gains = {n: w[n].reshape(L, 1, D) for n in ("ffn1_norm", "mix_norm", "ffn2_norm")}
    scale3 = w["pool_scale"].reshape(L, 1, POOL_WIDTH)
    consts = _ret_consts(T, pad)
    tl = _pick_tile(T, 2 * tm, BF16_ROWS)
    def gather(keys):
        return wts.rider(keys) if keys and keys[0] not in wts.full else None

    def done(rider):
        if rider is not None:
            wts.take(rider)

    h = jnp.concatenate([jnp.zeros((pad, D), F32), meta_full, x], axis=0)
    saved = []
    for i in range(L):
        s = {"h0": h}
        if ("ffn1_down", i) in wts.full:
            rd = gather([("w_in", i)] + [(n, i) for n in _MIXW])
            h, s["a1"], s["g1"], s["u1"], s["act1"] = _ffn_fwd(
                h, gains["ffn1_norm"], wts("ffn1_gate", i), wts("ffn1_up", i), wts("ffn1_down", i), i, tl,
                f"ffn1_fwd_{i}", rd)
            done(rd)
        else:
            rd = gather([("ffn1_down", i), ("w_in", i)])
            s["a1"], s["g1"], s["u1"], s["act1"] = _ffn_fwd_up(
                h, gains["ffn1_norm"], wts("ffn1_gate", i), wts("ffn1_up", i), i, tl, f"ffn1_fwd_up_{i}", rd)
            done(rd)
            rd = gather([(n, i) for n in _MIXW])
            h = _ffn_fwd_down(h, s["act1"], wts("ffn1_down", i), tl, f"ffn1_fwd_down_{i}", rd)
            done(rd)
        s["h1"] = h
        rd = gather([("ffn2_gate", i), ("ffn2_up", i)])
        s["z"], s["b"] = _inproj_fwd(h, gains["mix_norm"], wts("w_in", i), i, tl, f"inproj_fwd_{i}", rd)
        done(rd)
        s["r"], s["o_pre"], s["s_all"] = _ret_fwd(s["z"], consts, cg, f"retention_fwd_{i}")
        s["pm"] = _pool_fwd(s["z"], pool_maps, scale3, i, pad, f"pool_fwd_{i}")
        rd = gather([("ffn2_down", i)])
        h, s["mixed"], s["ret"], s["pool"] = _mix_fwd(
            h, s["r"], s["pm"], s["z"], wts("w_ret_up", i), wts("w_pool_up", i), wts("w_out", i), tl,
            f"mix_fwd_{i}", rd)
        done(rd)
        s["h2"] = h
        rd = gather([(n, i + 1) for n in _FFN1]) if i + 1 < L else None
        h, s["a2"], s["g2"], s["u2"], s["act2"] = _ffn_fwd(
            h, gains["ffn2_norm"], wts("ffn2_gate", i), wts("ffn2_up", i), wts("ffn2_down", i), i, tl,
            f"ffn2_fwd_{i}", rd)
        done(rd)
        saved.append(s)

    dh, loss_acc, d_final = _final_loss(h, w["final_norm"].reshape(1, D), tgt, "final_norm_loss")

    small = {n: [None] * L for n in ("ffn1_norm", "mix_norm", "ffn2_norm", "pool_scale", "pool_maps")}

    carry = {"ffn_act": 1.0, "ffn_in": 2.2, "mix_bwd": 1.0, "inproj_bwd": 1.5, "w_in": 1.0}

    tk = _pick_tile(T, 1408, LANES)

    def grad(n, a, b, i, mode):
        rd = reducer.rider(carry.get(n, 1.0 if i == 0 and n.startswith("ffn") else 0.5))
        reducer.add(n, i, _grad_tn(a, b, mode, 1.0, tk, f"grad_{n}_{i}", rd))
        reducer.done(rd)

    def ffn_bwd(which, dy, h_in, g, u, i):
        rd = reducer.rider(carry["ffn_act"])
        dg, du, dyh = _ffn_bwd_act(dy, g, u, wts(f"{which}_down", i), tl, f"{which}_bwd_act_{i}", rd)
        reducer.done(rd)
        rd = reducer.rider(carry["ffn_in"])
        dh_in, dgain = _ffn_bwd_in(dy, h_in, gains[f"{which}_norm"], dg, du, wts(f"{which}_gate", i),
                                   wts(f"{which}_up", i), i, tl, pad, f"{which}_bwd_in_{i}", rd)
        reducer.done(rd)
        return dh_in, dg, du, dgain, dyh

    for i in reversed(range(L)):
        s = saved[i]
        dh, dg, du, small["ffn2_norm"][i], dyh = ffn_bwd("ffn2", dh, s["h2"], s["g2"], s["u2"], i)
        grad("ffn2_gate", dg, s["a2"], i, "row")
        grad("ffn2_up", du, s["a2"], i, "row")
        grad("ffn2_down", s["act2"], dyh, i, "row")
        reducer.stage(f"ffn2_{i}")
        rd = reducer.rider(carry["mix_bwd"])
        dgab, dret, dpool, dr, dpm = _mix_bwd_dx(
            dh, s["z"], s["ret"], s["pool"], wts("w_out", i), wts("w_ret_up", i), wts("w_pool_up", i), tm,
            f"mix_bwd_{i}", rd)
        reducer.done(rd)
        rd = reducer.rider(0.5)
        g_out, g_ru, g_pu = _grad_mix(s["mixed"], dh, s["r"], dret, s["pm"], dpool, _pick_tile(T, 704, LANES),
                                      f"grad_mix_{i}", rd)
        reducer.done(rd)
        for n, g_n in (("w_out", g_out), ("w_ret_up", g_ru), ("w_pool_up", g_pu)):
            reducer.add(n, i, g_n)
        du_pool, small["pool_maps"][i], small["pool_scale"][i] = _pool_bwd(
            s["z"], dpm, pool_maps, scale3, i, pad, f"pool_bwd_{i}")
        dq, dgr, dkp, dvp, ds = _ret_bwd_local(s["z"], s["o_pre"], s["s_all"], dr, consts,
                                               _pick_tile(T // CHUNK, 11, 1), f"retention_bwd_{i}")
        dk, dv = _ret_bwd_state(s["z"], dkp, dvp, ds, consts, cg, f"retention_bwd_state_{i}")
        dz = [dq, dk, dv, dgr, du_pool, dgab]
        dh2 = dh
        rd = reducer.rider(carry["inproj_bwd"])
        dh, small["mix_norm"][i] = _inproj_bwd_dx(
            dz, wts("w_in", i), s["h1"], gains["mix_norm"], dh2, i, tm, pad, f"inproj_bwd_{i}", rd)
        reducer.done(rd)
        rd = reducer.rider(carry["w_in"])
        reducer.add("w_in", i, _grad_w_in(s["b"], dz, wts("w_in", i).shape[-1], tk, f"grad_w_in_{i}", rd))
        reducer.done(rd)
        reducer.stage(f"mid{i}")
        dh, dg, du, small["ffn1_norm"][i], dyh = ffn_bwd("ffn1", dh, s["h0"], s["g1"], s["u1"], i)
        grad("ffn1_gate", dg, s["a1"], i, "row")
        if i == 0:
            reducer.stage("gate0")
        grad("ffn1_up", du, s["a1"], i, "row")
        if i == 0:
            reducer.stage("up0")
        grad("ffn1_down", s["act1"], dyh, i, "row")
        reducer.stage(f"end{i}")

    return loss_acc, dh, small, d_final


class _Reducer:
    def __init__(self, unit):
        self.c_idx = lax.axis_index("c").astype(jnp.int32).reshape(1)
        chip = 2 * lax.axis_index("x") + lax.axis_index("y")
        self.pos = jnp.stack([chip, lax.axis_index("c")]).astype(jnp.int32)
        self.pending, self.stages, self.queue, self.halves, self.whole = [], [], [], {}, {}
        self.unit = unit
        self.calls = 0

    def add(self, name, layer, g):
        self.pending.append(((name, layer), g))

    def stage(self, tag):
        if self.pending:
            self.stages.append((tag, self.pending))
            self.pending = []

    def _pair_rider(self):
        if not self.stages:
            return None
        tag, items = self.stages.pop(0)
        rd = _pair_exchange_rider([g for _, g in items])
        rd.tag, rd.keys = tag, [k for k, _ in items]
        return rd

    def _chip_rider(self, units):
        take, size = [], 0
        while self.queue and (units is None or size + self.queue[0][1].size <= units * self.unit):
            take.append(self.queue.pop(0))
            size += take[-1][1].size
        if not take:
            return None
        rd = _chip_exchange_rider([p for _, p in take])
        rd.keys = [k for k, _ in take]
        return rd

    def _gather_rider(self):
        keys = [k for k in self.halves if k not in self.whole]
        if not keys:
            return None
        rd = _pair_gather_rider([self.halves[k] for k in keys])
        rd.keys = keys
        return rd

    def rider(self, units):
        self.riding = (self._pair_rider(), self._chip_rider(units), self._gather_rider())
        return _join(self.riding)

    def done(self, rd):
        if rd is None:
            return
        _split_results(rd)
        pair, chips, gather = self.riding
        if len([r for r in self.riding if r is not None]) == 1:
            (pair or chips or gather).results = rd.results
        self.calls += 1
        if gather is not None:
            self.whole.update(zip(gather.keys, gather.results))
        if pair is not None:
            sums = _sum_pair(pair.ins, pair.results, self.c_idx, f"sum_pair_{pair.tag}")
            self.queue += list(zip(pair.keys, sums))
        if chips is not None:
            sums = _sum_chips(chips.ins, chips.results, self.pos, f"sum_chips_{self.calls}")
            self.halves.update(zip(chips.keys, sums))

    def busy(self):
        assert not self.pending
        return bool(self.stages or self.queue or len(self.whole) < len(self.halves))

    def flush(self):
        self.riding = (self._pair_rider(), self._chip_rider(None), self._gather_rider())
        rd = _join(self.riding)
        _run_rider(rd, f"grads_exchange_tail_{self.calls}")
        self.done(rd)


def _update(loss_acc, grad_x, d_meta_rows, reducer, small, d_final, w, mom, var):
    meta = w["meta"]
    D = w["final_norm"].shape[0]
    L = w["ffn1_norm"].shape[0]
    Dq = D // N_CHIPS

    out = {}

    while reducer.busy():
        reducer.flush()
    for n in _BIG:
        gs = [reducer.whole[(n, i)] for i in range(L)]
        if n in _TRANSPOSED:
            res = _adamw(gs, *(jnp.swapaxes(t[n], 1, 2) for t in (w, mom, var)), f"adamw_{n}")
            out[n] = [jnp.swapaxes(r, 1, 2) for r in res]
        else:
            out[n] = _adamw(gs, w[n], mom[n], var[n], f"adamw_{n}")

    small_parts = [jnp.concatenate(small[n], axis=0) for n in ("ffn1_norm", "mix_norm", "ffn2_norm")]
    small_parts += [d_final, jnp.concatenate(small["pool_scale"], axis=0), jnp.concatenate(small["pool_maps"], axis=0)]
    loss_row = jnp.pad(loss_acc, ((0, 0), (0, D - loss_acc.shape[1])))
    reduced = _small_all_reduce(_pack_rows(small_parts + [d_meta_rows, loss_row], D))
    small_shapes = [w[n].shape for n in _SMALL]
    small_rows = sum(math.prod(shp) for shp in small_shapes) // D
    chip = 2 * lax.axis_index("x") + lax.axis_index("y")
    d_meta = lax.dynamic_slice_in_dim(reduced[small_rows:small_rows + N_META], chip * Dq, Dq, axis=1)
    names = _SMALL + ("meta",)
    packed_g = _pack_rows([reduced[:small_rows], d_meta], D)
    packed = [_pack_rows([t[n] for n in names], D) for t in (w, mom, var)]
    res = _adamw([packed_g], packed[0][None], packed[1][None], packed[2][None], "adamw_small")
    shapes = small_shapes + [meta.shape]
    unpacked = [_unpack_rows(r[0], shapes, D) for r in res]
    for k, n in enumerate(names):
        out[n] = tuple(u[k] for u in unpacked)

    loss = reduced[small_rows + N_META, 0]
    return (loss, grad_x) + tuple(out[n][j] for j in range(4) for n in _ORDER)


def kernel(x, meta, ffn1_norm, ffn1_gate, ffn1_up, ffn1_down, mix_norm, w_in, pool_maps, pool_scale, w_ret_up, w_pool_up, w_out, ffn2_norm, ffn2_gate, ffn2_up, ffn2_down, final_norm, loss_target, m_meta, m_ffn1_norm, m_ffn1_gate, m_ffn1_up, m_ffn1_down, m_mix_norm, m_w_in, m_pool_maps, m_pool_scale, m_w_ret_up, m_w_pool_up, m_w_out, m_ffn2_norm, m_ffn2_gate, m_ffn2_up, m_ffn2_down, m_final_norm, v_meta, v_ffn1_norm, v_ffn1_gate, v_ffn1_up, v_ffn1_down, v_mix_norm, v_w_in, v_pool_maps, v_pool_scale, v_w_ret_up, v_w_pool_up, v_w_out, v_ffn2_norm, v_ffn2_gate, v_ffn2_up, v_ffn2_down, v_final_norm):
    args = dict(locals())
    w = {n: args[n] for n in _ORDER}
    mom = {n: args["m_" + n] for n in _ORDER}
    var = {n: args["v_" + n] for n in _ORDER}

    assert x.shape[0] == 1, "one batch element per device"
    seq, D = x.shape[1], x.shape[2]
    assert seq % CHUNK == 0 and D % RET_WIDTH == 0 and (2 * POOL_WIDTH) % D == 0
    pad = (-(seq + N_META)) % CHUNK
    T = seq + N_META + pad
    tm = _pick_tile(T, 528, BF16_ROWS)
    cg = _pick_tile(T // CHUNK, 33, 1)

    shards = {n: _transport(w[n]) for n in _BIG}
    shards["meta"] = meta[None]
    wts = _Weights(shards)
    head = wts.rider([("ffn1_gate", 0), ("ffn1_up", 0), ("meta", 0)])
    _run_rider(head, "weights_gather_head")
    wts.take(head)
    meta_full = jnp.transpose(wts("meta", 0), (1, 0, 2)).reshape(N_META, D)

    reducer = _Reducer(unit=2 * shards["ffn1_gate"][0].size)
    loss_acc, dh, small, d_final = _local_step(x[0], meta_full, loss_target[0], w, wts, pad, tm, cg, reducer)
    grad_x = dh[pad + N_META:][None]
    return _update(loss_acc, grad_x, dh[pad:pad + N_META], reducer, small, d_final, w, mom, var)
```

```python
import functools
import math

import jax
import jax.numpy as jnp
from jax import lax
from jax.experimental import pallas as pl
from jax.experimental.pallas import tpu as pltpu

F32 = jnp.float32
BF16 = jnp.bfloat16

N_META = 16
RET_HEADS = 4
HEAD_DIM = 128
RET_WIDTH = RET_HEADS * HEAD_DIM
POOL_WINDOWS = (2, 4, 8, 16)
POOL_GROUPS = len(POOL_WINDOWS)
POOL_WIDTH = POOL_GROUPS * HEAD_DIM
CHUNK = 128
ROPE_BASE = 10000.0
EPS = 1e-6
ADAM_LR = 0.001
ADAM_B1 = 0.9
ADAM_B2 = 0.999
ADAM_EPS = 1e-08
ADAM_WD = 0.01
ADAM_STEP = 10

N_CHIPS = 4
LANES = 128
BF16_ROWS = 16
V7X_VMEM_LIMIT = 52 * 1024 * 1024
MESH = pl.DeviceIdType.MESH
ANY = pl.BlockSpec(memory_space=pl.ANY)


def _round_up(n, m):
    return -(-n // m) * m


def _pick_tile(n, target, mult):
    best = None
    for d in range(mult, min(n, target) + 1, mult):
        if n % d == 0:
            best = d
    assert best is not None, (n, target, mult)
    return best


def _params(sem=None):
    return pltpu.CompilerParams(dimension_semantics=sem, vmem_limit_bytes=V7X_VMEM_LIMIT)


def _dot(a, b):
    return jnp.dot(a, b, preferred_element_type=F32)


def _dot_nt(a, b):
    return lax.dot_general(a, b, (((1,), (1,)), ((), ())), preferred_element_type=F32)


def _dot_tn(a, b):
    return lax.dot_general(a, b, (((0,), (0,)), ((), ())), preferred_element_type=F32)


def _ein(spec, a, b):
    return jnp.einsum(spec, a, b, preferred_element_type=F32)


def _sigmoid(x):
    return jax.nn.sigmoid(x)


def _rms_fwd(x, gain):
    r = lax.rsqrt(jnp.mean(x * x, axis=-1, keepdims=True) + EPS)
    return x * r * gain


def _rms_bwd(x, gain, da):
    r = lax.rsqrt(jnp.mean(x * x, axis=-1, keepdims=True) + EPS)
    xh = x * r
    dgain = jnp.sum(da * xh, axis=0, keepdims=True)
    dxh = da * gain
    dx = r * (dxh - xh * jnp.mean(dxh * xh, axis=-1, keepdims=True))
    return dx, dgain


def _row_mask(t, tm, pad, shape):
    rows = t * tm + lax.broadcasted_iota(jnp.int32, shape, 0)
    return rows >= pad


def _mesh_pos():
    x, y, c = lax.axis_index("x"), lax.axis_index("y"), lax.axis_index("c")
    others = [(1 - x, y), (x, 1 - y), (1 - x, 1 - y)]
    return x, y, c, 2 * x + y, others


def _half_rows(c, rh):
    return pl.ds(pl.multiple_of(c * rh, rh), rh)


def _remote(src, dst, ssem, rsem, dev):
    return pltpu.make_async_remote_copy(src_ref=src, dst_ref=dst, send_sem=ssem, recv_sem=rsem,
                                        device_id=dev, device_id_type=MESH)


class _Rider:
    def __init__(self, ins, out_shapes, n_sem, start, finish, in_place=False):
        self.ins, self.out_shapes, self.n_sem, self.start, self.finish = ins, out_shapes, n_sem, start, finish
        self.in_place = [in_place] * len(ins)
        self.results = None

    def aliases(self, first_in, first_out):
        return {first_in + i: first_out + i for i, same in enumerate(self.in_place) if same}


class _SemWindow:
    def __init__(self, ref, base):
        self.ref, self.base = ref, base

    @property
    def at(self):
        return self

    def __getitem__(self, k):
        return self.ref.at[self.base + k]


def _join(riders):
    riders = [r for r in riders if r is not None]
    if len(riders) <= 1:
        return riders[0] if riders else None

    def run(which):
        def go(ins, outs, ssem, rsem):
            at, sem = 0, 0
            for r in riders:
                n = len(r.ins)
                getattr(r, which)(ins[at:at + n], outs[at:at + n], _SemWindow(ssem, sem), _SemWindow(rsem, sem))
                at, sem = at + n, sem + r.n_sem
        return go

    joined = _Rider(sum([list(r.ins) for r in riders], []), sum([list(r.out_shapes) for r in riders], []),
                    sum(r.n_sem for r in riders), run("start"), run("finish"))
    joined.in_place = sum([r.in_place for r in riders], [])
    joined.parts = riders
    return joined


def _split_results(rider):
    at = 0
    for r in getattr(rider, "parts", []):
        r.results = rider.results[at:at + len(r.ins)]
        at += len(r.ins)


def _gather_rider(pieces):
    per = 7
    layers = [layer for _, layer in pieces]

    def first_copies(ins, outs, ssem, rsem):
        x, y, c, chip, others = _mesh_pos()
        copies = []
        for i, layer in enumerate(layers):
            mine = _half_rows(c, ins[i].shape[1] // 2)
            for j, (ox, oy) in enumerate(others):
                copies.append(_remote(ins[i].at[layer, mine, :], outs[i].at[chip, mine, :],
                                      ssem.at[per * i + j], rsem.at[per * i + j], (ox, oy, c)))
            copies.append(_remote(ins[i].at[layer], outs[i].at[chip],
                                  ssem.at[per * i + 6], rsem.at[per * i + 6], (x, y, 1 - c)))
        return copies

    def start(ins, outs, ssem, rsem):
        for cp in first_copies(ins, outs, ssem, rsem):
            cp.start()

    def finish(ins, outs, ssem, rsem):
        x, y, c, chip, others = _mesh_pos()
        sibling = (x, y, 1 - c)
        forwards = []
        for i in range(len(layers)):
            mine = _half_rows(c, ins[i].shape[1] // 2)
            for j, (ox, oy) in enumerate(others):
                rows = outs[i].at[2 * ox + oy, mine, :]
                _remote(rows, rows, ssem.at[per * i + j], rsem.at[per * i + j], (ox, oy, c)).wait_recv()
                fwd = _remote(rows, rows, ssem.at[per * i + 3 + j], rsem.at[per * i + 3 + j], sibling)
                fwd.start()
                forwards.append(fwd)
        for i in range(len(layers)):
            theirs = _half_rows(1 - c, ins[i].shape[1] // 2)
            for j, (ox, oy) in enumerate(others):
                rows = outs[i].at[2 * ox + oy, theirs, :]
                _remote(rows, rows, ssem.at[per * i + 3 + j], rsem.at[per * i + 3 + j], sibling).wait_recv()
            own = outs[i].at[chip]
            _remote(own, own, ssem.at[per * i + 6], rsem.at[per * i + 6], sibling).wait_recv()
        for cp in first_copies(ins, outs, ssem, rsem) + forwards:
            cp.wait_send()

    shapes = [jax.ShapeDtypeStruct((N_CHIPS,) + s.shape[1:], s.dtype) for s, _ in pieces]
    return _Rider([s for s, _ in pieces], shapes, per * len(pieces), start, finish)


def _chip_exchange_rider(ps):
    def copies(ins, outs, ssem, rsem):
        x, y, c, chip, others = _mesh_pos()
        return [_remote(ins[i].at[2 * ox + oy], outs[i].at[chip], ssem.at[3 * i + j], rsem.at[3 * i + j], (ox, oy, c))
                for i in range(len(ps)) for j, (ox, oy) in enumerate(others)]

    def start(ins, outs, ssem, rsem):
        for cp in copies(ins, outs, ssem, rsem):
            cp.start()

    def finish(ins, outs, ssem, rsem):
        x, y, c, chip, others = _mesh_pos()
        for i in range(len(ps)):
            for j, (ox, oy) in enumerate(others):
                slot = outs[i].at[2 * ox + oy]
                _remote(slot, slot, ssem.at[3 * i + j], rsem.at[3 * i + j], (ox, oy, c)).wait_recv()
        for cp in copies(ins, outs, ssem, rsem):
            cp.wait_send()

    return _Rider(list(ps), [jax.ShapeDtypeStruct(p.shape, p.dtype) for p in ps], 3 * len(ps), start, finish)


def _pair_exchange_rider(gs):
    def copies(ins, outs, ssem, rsem):
        x, y, c, _, _ = _mesh_pos()
        return [_remote(ins[i].at[:, _half_rows(1 - c, ins[i].shape[1] // 2), :], outs[i],
                        ssem.at[i], rsem.at[i], (x, y, 1 - c)) for i in range(len(gs))]

    def start(ins, outs, ssem, rsem):
        for cp in copies(ins, outs, ssem, rsem):
            cp.start()

    def finish(ins, outs, ssem, rsem):
        for cp in copies(ins, outs, ssem, rsem):
            cp.wait()

    shapes = [jax.ShapeDtypeStruct((g.shape[0], g.shape[1] // 2, g.shape[2]), g.dtype) for g in gs]
    return _Rider(list(gs), shapes, len(gs), start, finish)


def _run_rider(rider, name):
    def body(*refs):
        n = len(rider.ins)
        ins, outs = refs[:n], refs[n:2 * n]
        ssem, rsem = refs[2 * n:]
        rider.start(ins, outs, ssem, rsem)
        rider.finish(ins, outs, ssem, rsem)

    rider.results = pl.pallas_call(
        body,
        name=name,
        in_specs=[ANY] * len(rider.ins),
        out_specs=[ANY] * len(rider.ins),
        out_shape=rider.out_shapes,
        input_output_aliases=rider.aliases(0, 0),
        scratch_shapes=[pltpu.SemaphoreType.DMA((rider.n_sem,)), pltpu.SemaphoreType.DMA((rider.n_sem,))],
    )(*rider.ins)
    return rider.results


def _pair_gather_rider(fs):
    n = len(fs)

    def copies(outs, ssem, rsem):
        x, y, c, _, _ = _mesh_pos()
        halves = [outs[i].at[_half_rows(c, outs[i].shape[0] // 2), :] for i in range(n)]
        return [_remote(h, h, ssem.at[i], rsem.at[i], (x, y, 1 - c)) for i, h in enumerate(halves)]

    def start(ins, outs, ssem, rsem):
        for cp in copies(outs, ssem, rsem):
            cp.start()

    def finish(ins, outs, ssem, rsem):
        x, y, c, _, _ = _mesh_pos()
        for i in range(n):
            theirs = outs[i].at[_half_rows(1 - c, outs[i].shape[0] // 2), :]
            _remote(theirs, theirs, ssem.at[i], rsem.at[i], (x, y, 1 - c)).wait_recv()
        for cp in copies(outs, ssem, rsem):
            cp.wait_send()

    return _Rider(list(fs), [jax.ShapeDtypeStruct(f.shape, f.dtype) for f in fs], n, start, finish, in_place=True)


def _call(body, *, name, grid, in_specs, out_specs, out_shape, operands, scratch=(), sem=None, rider=None):
    if rider is None:
        return pl.pallas_call(
            body, name=name, grid=grid, in_specs=in_specs, out_specs=out_specs, out_shape=out_shape,
            scratch_shapes=list(scratch), compiler_params=_params(sem))(*operands)
    n_in, n_out, n_sc, r = len(in_specs), len(out_specs), len(scratch), len(rider.ins)

    def carrying(*refs):
        a, b = n_in, n_in + r
        c, d = b + n_out, b + n_out + r
        e = d + n_sc
        ids = [pl.program_id(k) for k in range(len(grid))]
        first = functools.reduce(jnp.logical_and, [i == 0 for i in ids])
        last = functools.reduce(jnp.logical_and, [i == g - 1 for i, g in zip(ids, grid)])

        @pl.when(first)
        def _():
            rider.start(refs[a:b], refs[c:d], refs[e], refs[e + 1])

        body(*refs[:a], *refs[b:c], *refs[d:e])

        @pl.when(last)
        def _():
            rider.finish(refs[a:b], refs[c:d], refs[e], refs[e + 1])

    outs = pl.pallas_call(
        carrying, name=name, grid=grid,
        in_specs=list(in_specs) + [ANY] * r,
        out_specs=list(out_specs) + [ANY] * r,
        out_shape=list(out_shape) + list(rider.out_shapes),
        scratch_shapes=list(scratch) + [pltpu.SemaphoreType.DMA((rider.n_sem,)), pltpu.SemaphoreType.DMA((rider.n_sem,))],
        input_output_aliases=rider.aliases(n_in, n_out),
        compiler_params=_params(("arbitrary",) * len(grid)),
    )(*operands, *rider.ins)
    rider.results = outs[n_out:]
    return outs[:n_out]


def _ffn_fwd(h, gain, wg, wu, wd, layer, tm, name, rider=None):
    T, D = h.shape
    Fs = wg.shape[-1]
    F = N_CHIPS * Fs

    def body(h_ref, g_ref, wg_ref, wu_ref, wd_ref, ho_ref, a_ref, go_ref, uo_ref, act_ref, acc_ref):
        s = pl.program_id(1)

        @pl.when(s == 0)
        def _():
            a_ref[...] = _rms_fwd(h_ref[...], g_ref[...]).astype(BF16)
            acc_ref[...] = jnp.zeros_like(acc_ref)

        a = a_ref[...]
        g = _dot(a, wg_ref[...])
        u = _dot(a, wu_ref[...])
        sg = _sigmoid(g)
        act = (g * sg * u).astype(BF16)
        go_ref[...] = (u * (sg * (1.0 + g * (1.0 - sg)))).astype(BF16)
        uo_ref[...] = (g * sg).astype(BF16)
        act_ref[...] = act
        acc_ref[...] += _dot(act, wd_ref[...])

        @pl.when(s == N_CHIPS - 1)
        def _():
            ho_ref[...] = h_ref[...] + 0.5 * acc_ref[...]

    row = pl.BlockSpec((tm, D), lambda t, s: (t, 0))
    col = pl.BlockSpec((tm, Fs), lambda t, s: (t, s))
    wcol = pl.BlockSpec((None, D, Fs), lambda t, s: (s, 0, 0))
    return _call(
        body, name=name, grid=(T // tm, N_CHIPS),
        in_specs=[row, pl.BlockSpec((None, 1, D), lambda t, s: (layer, 0, 0)), wcol, wcol,
                  pl.BlockSpec((None, Fs, D), lambda t, s: (s, 0, 0))],
        out_specs=[row, row, col, col, col],
        out_shape=[jax.ShapeDtypeStruct((T, D), F32), jax.ShapeDtypeStruct((T, D), BF16)]
        + [jax.ShapeDtypeStruct((T, F), BF16)] * 3,
        scratch=[pltpu.VMEM((tm, D), F32)],
        sem=("parallel", "arbitrary"), operands=(h, gain, wg, wu, wd), rider=rider)


def _ffn_fwd_up(h, gain, wg, wu, layer, tm, name, rider=None):
    T, D = h.shape
    Fs = wg.shape[-1]
    F = N_CHIPS * Fs

    def body(h_ref, g_ref, wg_ref, wu_ref, a_ref, go_ref, uo_ref, act_ref):
        @pl.when(pl.program_id(1) == 0)
        def _():
            a_ref[...] = _rms_fwd(h_ref[...], g_ref[...]).astype(BF16)

        a = a_ref[...]
        g = _dot(a, wg_ref[...])
        u = _dot(a, wu_ref[...])
        sg = _sigmoid(g)
        act_ref[...] = (g * sg * u).astype(BF16)
        go_ref[...] = (u * (sg * (1.0 + g * (1.0 - sg)))).astype(BF16)
        uo_ref[...] = (g * sg).astype(BF16)

    row = pl.BlockSpec((tm, D), lambda t, s: (t, 0))
    col = pl.BlockSpec((tm, Fs), lambda t, s: (t, s))
    wcol = pl.BlockSpec((None, D, Fs), lambda t, s: (s, 0, 0))
    return _call(
        body, name=name, grid=(T // tm, N_CHIPS),
        in_specs=[row, pl.BlockSpec((None, 1, D), lambda t, s: (layer, 0, 0)), wcol, wcol],
        out_specs=[row, col, col, col],
        out_shape=[jax.ShapeDtypeStruct((T, D), BF16)] + [jax.ShapeDtypeStruct((T, F), BF16)] * 3,
        sem=("parallel", "arbitrary"), operands=(h, gain, wg, wu), rider=rider)


def _ffn_fwd_down(h, act, wd, tm, name, rider=None):
    T, D = h.shape
    Fs = wd.shape[1]

    def body(h_ref, act_ref, wd_ref, ho_ref, acc_ref):
        s = pl.program_id(1)

        @pl.when(s == 0)
        def _():
            acc_ref[...] = jnp.zeros_like(acc_ref)

        acc_ref[...] += _dot(act_ref[...], wd_ref[...])

        @pl.when(s == N_CHIPS - 1)
        def _():
            ho_ref[...] = h_ref[...] + 0.5 * acc_ref[...]

    row = pl.BlockSpec((tm, D), lambda t, s: (t, 0))
    return _call(
        body, name=name, grid=(T // tm, N_CHIPS),
        in_specs=[row, pl.BlockSpec((tm, Fs), lambda t, s: (t, s)), pl.BlockSpec((None, Fs, D), lambda t, s: (s, 0, 0))],
        out_specs=[row],
        out_shape=[jax.ShapeDtypeStruct((T, D), F32)],
        scratch=[pltpu.VMEM((tm, D), F32)],
        sem=("parallel", "arbitrary"), operands=(h, act, wd), rider=rider)[0]


def _inproj_fwd(h, gain, win, layer, tm, name, rider=None):
    T, D = h.shape
    Ns = win.shape[-1]

    def body(h_ref, g_ref, w_ref, z_ref, b_ref):
        @pl.when(pl.program_id(1) == 0)
        def _():
            b_ref[...] = _rms_fwd(h_ref[...], g_ref[...]).astype(BF16)

        z_ref[...] = _dot(b_ref[...], w_ref[...]).astype(BF16)

    return _call(
        body, name=name, grid=(T // tm, N_CHIPS),
        in_specs=[pl.BlockSpec((tm, D), lambda t, s: (t, 0)),
                  pl.BlockSpec((None, 1, D), lambda t, s: (layer, 0, 0)),
                  pl.BlockSpec((None, D, Ns), lambda t, s: (s, 0, 0))],
        out_specs=[pl.BlockSpec((tm, Ns), lambda t, s: (t, s)), pl.BlockSpec((tm, D), lambda t, s: (t, 0))],
        out_shape=[jax.ShapeDtypeStruct((T, N_CHIPS * Ns), BF16), jax.ShapeDtypeStruct((T, D), BF16)],
        sem=("parallel", "arbitrary"), operands=(h, gain, win), rider=rider)


def _ret_consts(T, pad):
    half = HEAD_DIM // 2
    inv_freq = ROPE_BASE ** (-jnp.arange(half, dtype=F32) / half)
    pos = jnp.arange(T, dtype=F32) - pad
    ang = pos[:, None] * inv_freq[None, :]
    cos = jnp.cos(ang)
    sin = jnp.sin(ang)
    cosf = jnp.concatenate([cos, cos], axis=1)
    sinf = jnp.concatenate([-sin, sin], axis=1)
    log_gamma = jnp.log1p(-(2.0 ** (-5.0 - jnp.arange(RET_HEADS, dtype=F32))))
    idx = jnp.arange(CHUNK, dtype=F32)
    diff = idx[:, None] - idx[None, :]
    intra = jnp.where(diff[None] >= 0, jnp.exp(diff[None] * log_gamma[:, None, None]), 0.0)
    k_decay = jnp.exp((CHUNK - 1.0 - idx)[None, :] * log_gamma[:, None])
    q_decay = jnp.exp((idx + 1.0)[None, :] * log_gamma[:, None])
    chunk_decay = jnp.exp(CHUNK * log_gamma)
    kdec = jnp.broadcast_to(k_decay[:, :, None], (RET_HEADS, CHUNK, HEAD_DIM))
    qdec = jnp.broadcast_to(q_decay[:, :, None], (RET_HEADS, CHUNK, HEAD_DIM))
    cdb = jnp.broadcast_to(chunk_decay[:, None, None], (RET_HEADS, 8, HEAD_DIM))
    return cosf, sinf, intra, kdec, qdec, cdb


def _rot(t, cosv, sinv):
    return t * cosv + pltpu.roll(t, HEAD_DIM // 2, 1) * sinv


def _rot_t(g, cosv, sinv):
    return g * cosv + pltpu.roll(g * sinv, HEAD_DIM // 2, 1)


def _head_specs(tg, section, order):
    return pl.BlockSpec((tg, HEAD_DIM), lambda h, g: (order(g), section * RET_HEADS + h))


def _ret_fwd(z, consts, cg, name, rider=None):
    T = z.shape[0]
    N = T // CHUNK
    ng = N // cg
    tg = cg * CHUNK
    cosf, sinf, intra, kdec, qdec, cdb = consts
    fwd = lambda g: g

    def body(zq, zk, zv, zg, cos_ref, sin_ref, m_ref, kd_ref, qd_ref, cd_ref, r_ref, o_ref, s_ref, st_ref):
        @pl.when(pl.program_id(1) == 0)
        def _():
            st_ref[...] = jnp.zeros_like(st_ref)

        cosv = cos_ref[...]
        sinv = sin_ref[...]
        q3 = (_rot(zq[...].astype(F32), cosv, sinv) * (HEAD_DIM ** -0.5)).reshape(cg, CHUNK, HEAD_DIM)
        k3 = _rot(zk[...].astype(F32), cosv, sinv).reshape(cg, CHUNK, HEAD_DIM)
        vb = zv[...].reshape(cg, CHUNK, HEAD_DIM).astype(BF16)
        scores = _ein("ncd,nmd->ncm", q3.astype(BF16), k3.astype(BF16)) * m_ref[...][None]
        inner = _ein("ncm,nmd->ncd", scores.astype(BF16), vb)
        kv = _ein("ncd,nce->nde", (k3 * kd_ref[...][None]).astype(BF16), vb)
        cd = cd_ref[0:1, :]
        state = st_ref[...]
        for n in range(cg):
            s_ref[n] = state
            state = state * cd + kv[n]
        st_ref[...] = state
        qdb = (q3 * qd_ref[...][None]).astype(BF16)
        cross = _ein("ncd,nde->nce", qdb, s_ref[...].astype(BF16))
        out = (inner + cross).reshape(tg, HEAD_DIM)
        o_ref[...] = out
        xc = out - jnp.mean(out, axis=-1, keepdims=True)
        rn = xc * lax.rsqrt(jnp.mean(xc * xc, axis=-1, keepdims=True) + EPS)
        g = zg[...].astype(F32)
        r_ref[...] = (rn * (g * _sigmoid(g))).astype(BF16)

    tab = pl.BlockSpec((tg, HEAD_DIM), lambda h, g: (g, 0))
    per_head = lambda rows: pl.BlockSpec((None, rows, HEAD_DIM), lambda h, g: (h, 0, 0))
    head_out = pl.BlockSpec((tg, HEAD_DIM), lambda h, g: (g, h))
    return _call(
        body, name=name, grid=(RET_HEADS, ng),
        in_specs=[_head_specs(tg, i, fwd) for i in range(4)]
        + [tab, tab, per_head(CHUNK), per_head(CHUNK), per_head(CHUNK), per_head(8)],
        out_specs=[head_out, head_out, pl.BlockSpec((None, cg, HEAD_DIM, HEAD_DIM), lambda h, g: (h, g, 0, 0))],
        out_shape=[jax.ShapeDtypeStruct((T, RET_WIDTH), BF16), jax.ShapeDtypeStruct((T, RET_WIDTH), F32),
                   jax.ShapeDtypeStruct((RET_HEADS, N, HEAD_DIM, HEAD_DIM), F32)],
        scratch=[pltpu.VMEM((HEAD_DIM, HEAD_DIM), F32)],
        sem=("parallel", "arbitrary"), operands=(z, z, z, z, cosf, sinf, intra, kdec, qdec, cdb), rider=rider)


def _window_sums(u, shift_of):
    sums = []
    s = u
    k = 1
    while k < POOL_WINDOWS[-1]:
        s = s + pltpu.roll(s, shift_of(k), 0)
        sums.append(s)
        k *= 2
    return sums


def _select_group(vals, g):
    out = vals[-1]
    for i in range(len(vals) - 2, -1, -1):
        out = jnp.where(g == i, vals[i], out)
    return out


def _pool_parts(u, g, T, pad):
    rows = lax.broadcasted_iota(jnp.int32, (T, HEAD_DIM), 0)
    valid = rows >= pad
    win = _select_group([float(w) for w in POOL_WINDOWS], g)
    div = jnp.clip((rows - pad + 1).astype(F32), 1.0, win)
    s = _select_group(_window_sums(u, lambda k: k), g)
    pooled = jnp.where(valid, s / div - u, 0.0)
    return pooled, div, valid


def _pool_specs(T, layer):
    first = 4 * RET_WIDTH // HEAD_DIM
    return [
        pl.BlockSpec((T, HEAD_DIM), lambda g: (0, first + g)),
        pl.BlockSpec((None, None, HEAD_DIM, HEAD_DIM), lambda g: (layer, g, 0, 0)),
        pl.BlockSpec((None, 1, HEAD_DIM), lambda g: (layer, 0, g)),
    ]


def _pool_fwd(z, maps, scale, layer, pad, name):
    T = z.shape[0]
    assert pad >= POOL_WINDOWS[-1], "window rolls wrap into the zero rows in front"

    def body(zu, maps_ref, sc_ref, pm_ref):
        g = pl.program_id(0)
        pooled, _, _ = _pool_parts(zu[...].astype(F32), g, T, pad)
        y = _dot(pooled.astype(BF16), maps_ref[...].astype(BF16))
        pm_ref[...] = (y * sc_ref[...]).astype(BF16)

    return _call(
        body, name=name, grid=(POOL_GROUPS,),
        in_specs=_pool_specs(T, layer),
        out_specs=[pl.BlockSpec((T, HEAD_DIM), lambda g: (0, g))],
        out_shape=[jax.ShapeDtypeStruct((T, POOL_WIDTH), BF16)],
        sem=("parallel",), operands=(z, maps, scale))[0]


def _gate_specs(tm, D):
    nb = D // RET_WIDTH
    first = (4 * RET_WIDTH + POOL_WIDTH) // RET_WIDTH
    return [pl.BlockSpec((tm, RET_WIDTH), functools.partial(lambda t, j: (t, j), j=first + j)) for j in range(2 * nb)]


def _load_gates(refs, nb):
    ga = jnp.concatenate([r[...].astype(F32) for r in refs[:nb]], axis=1)
    gb = jnp.concatenate([r[...].astype(F32) for r in refs[nb:]], axis=1)
    return ga, gb


def _mix_fwd(h, r, pm, z, wru, wpu, wout, tm, name, rider=None):
    T, D = h.shape
    Dq = D // N_CHIPS
    nb = D // RET_WIDTH

    def body(*refs):
        h_ref, r_ref, pm_ref = refs[:3]
        gate_refs = refs[3:3 + 2 * nb]
        wru_ref, wpu_ref, wout_ref, ho_ref, mx_ref, ret_ref, pool_ref = refs[3 + 2 * nb:]
        rv = r_ref[...]
        pv = pm_ref[...]
        ret = jnp.concatenate([_dot(rv, wru_ref[s]) for s in range(N_CHIPS)], axis=1)
        pool = jnp.concatenate([_dot(pv, wpu_ref[s]) for s in range(N_CHIPS)], axis=1)
        ga, gb = _load_gates(gate_refs, nb)
        mixed = (_sigmoid(ga) * ret + _sigmoid(gb) * pool).astype(BF16)
        mx_ref[...] = mixed
        ret_ref[...] = ret.astype(BF16)
        pool_ref[...] = pool.astype(BF16)
        ho_ref[...] = h_ref[...] + _dot(mixed, wout_ref[...].reshape(D, D))

    row = pl.BlockSpec((tm, D), lambda t: (t, 0))
    half = pl.BlockSpec((tm, RET_WIDTH), lambda t: (t, 0))
    up = pl.BlockSpec((N_CHIPS, RET_WIDTH, Dq), lambda t: (0, 0, 0))
    return _call(
        body, name=name, grid=(T // tm,),
        in_specs=[row, half, half] + _gate_specs(tm, D) + [up, up, pl.BlockSpec((N_CHIPS, Dq, D), lambda t: (0, 0, 0))],
        out_specs=[row, row, row, row],
        out_shape=[jax.ShapeDtypeStruct((T, D), F32)] + [jax.ShapeDtypeStruct((T, D), BF16)] * 3,
        sem=("parallel",), operands=(h, r, pm, *([z] * (2 * nb)), wru, wpu, wout), rider=rider)


def _final_loss(h, gain, tgt, name):
    T, D = h.shape
    first = (T - tgt.shape[0]) // CHUNK

    def body(h_ref, g_ref, t_ref, dh_ref, loss_ref, dg_ref):
        i = pl.program_id(0)

        @pl.when(i == 0)
        def _():
            loss_ref[...] = jnp.zeros_like(loss_ref)
            dg_ref[...] = jnp.zeros_like(dg_ref)

        x = h_ref[...]
        gain_v = g_ref[...]
        err = jnp.where(i >= first, _rms_fwd(x, gain_v) - t_ref[...], 0.0)
        loss_ref[...] += 0.5 * jnp.sum(jnp.mean(err * err, axis=-1))
        dx, dgain = _rms_bwd(x, gain_v, err * (1.0 / D))
        dg_ref[...] += dgain
        dh_ref[...] = dx

    return _call(
        body, name=name, grid=(T // CHUNK,),
        in_specs=[pl.BlockSpec((CHUNK, D), lambda i: (i, 0)),
                  pl.BlockSpec((1, D), lambda i: (0, 0)),
                  pl.BlockSpec((CHUNK, D), lambda i: (jnp.maximum(i - first, 0), 0))],
        out_specs=[pl.BlockSpec((CHUNK, D), lambda i: (i, 0)),
                   pl.BlockSpec((1, LANES), lambda i: (0, 0)),
                   pl.BlockSpec((1, D), lambda i: (0, 0))],
        out_shape=[jax.ShapeDtypeStruct((T, D), F32), jax.ShapeDtypeStruct((1, LANES), F32),
                   jax.ShapeDtypeStruct((1, D), F32)],
        sem=("arbitrary",), operands=(h, gain, tgt))


def _ffn_bwd_act(dy, g, u, wd, tm, name, rider=None):
    T, D = dy.shape
    Fs = wd.shape[1]
    F = N_CHIPS * Fs

    def body(dy_ref, go_ref, uo_ref, wd_ref, dg_ref, du_ref, dyh_ref):
        @pl.when(pl.program_id(1) == 0)
        def _():
            dyh_ref[...] = (0.5 * dy_ref[...]).astype(BF16)

        dact = _dot_nt(dyh_ref[...], wd_ref[...])
        du_ref[...] = (dact * uo_ref[...].astype(F32)).astype(BF16)
        dg_ref[...] = (dact * go_ref[...].astype(F32)).astype(BF16)

    row = pl.BlockSpec((tm, D), lambda t, s: (t, 0))
    col = pl.BlockSpec((tm, Fs), lambda t, s: (t, s))
    return _call(
        body, name=name, grid=(T // tm, N_CHIPS),
        in_specs=[row, col, col, pl.BlockSpec((None, Fs, D), lambda t, s: (s, 0, 0))],
        out_specs=[col, col, row],
        out_shape=[jax.ShapeDtypeStruct((T, F), BF16), jax.ShapeDtypeStruct((T, F), BF16),
                   jax.ShapeDtypeStruct((T, D), BF16)],
        sem=("parallel", "arbitrary"), operands=(dy, g, u, wd), rider=rider)


def _ffn_bwd_in(dy, h, gain, dg, du, wg, wu, layer, tm, pad, name, rider=None):
    T, D = h.shape
    Fs = wg.shape[-1]

    def body(dy_ref, h_ref, g_ref, dg_ref, du_ref, wg_ref, wu_ref, dh_ref, dgain_ref, da_ref):
        t = pl.program_id(0)
        s = pl.program_id(1)

        @pl.when((t == 0) & (s == 0))
        def _():
            dgain_ref[...] = jnp.zeros_like(dgain_ref)

        @pl.when(s == 0)
        def _():
            da_ref[...] = jnp.zeros_like(da_ref)

        da_ref[...] += _dot_nt(dg_ref[...], wg_ref[...]) + _dot_nt(du_ref[...], wu_ref[...])

        @pl.when(s == N_CHIPS - 1)
        def _():
            dx, dgain = _rms_bwd(h_ref[...], g_ref[...], da_ref[...])
            dgain_ref[...] += dgain
            dh_ref[...] = jnp.where(_row_mask(t, tm, pad, (tm, D)), dy_ref[...] + dx, 0.0)

    row = pl.BlockSpec((tm, D), lambda t, s: (t, 0))
    col = pl.BlockSpec((tm, Fs), lambda t, s: (t, s))
    wcol = pl.BlockSpec((None, D, Fs), lambda t, s: (s, 0, 0))
    return _call(
        body, name=name, grid=(T // tm, N_CHIPS),
        in_specs=[row, row, pl.BlockSpec((None, 1, D), lambda t, s: (layer, 0, 0)), col, col, wcol, wcol],
        out_specs=[row, pl.BlockSpec((1, D), lambda t, s: (0, 0))],
        out_shape=[jax.ShapeDtypeStruct((T, D), F32), jax.ShapeDtypeStruct((1, D), F32)],
        scratch=[pltpu.VMEM((tm, D), F32)],
        sem=("arbitrary", "arbitrary"), operands=(dy, h, gain, dg, du, wg, wu), rider=rider)


def _grad_tn(a, b, mode, scale, tm, name, rider=None):
    T = a.shape[0]
    if mode == "col":
        per, R, C = 1, a.shape[1], b.shape[1] // N_CHIPS
        a_spec = pl.BlockSpec((tm, R), lambda s, t: (t, 0))
        b_spec = pl.BlockSpec((tm, C), lambda s, t: (t, s))
    else:
        per, R, C = 2, a.shape[1] // N_CHIPS, b.shape[1]
        a_spec = pl.BlockSpec((tm, per * R), lambda s, t: (t, s))
        b_spec = pl.BlockSpec((tm, C), lambda s, t: (t, 0))
    nt = T // tm

    def body(a_ref, b_ref, o_ref, acc_ref):
        t = pl.program_id(1)

        @pl.when(t == 0)
        def _():
            acc_ref[...] = jnp.zeros_like(acc_ref)

        acc_ref[...] += _dot_tn(a_ref[...].astype(BF16), b_ref[...].astype(BF16))

        @pl.when(t == nt - 1)
        def _():
            o_ref[...] = (scale * acc_ref[...]).astype(BF16).reshape(per, R, C)

    return _call(
        body, name=name, grid=(N_CHIPS // per, nt),
        in_specs=[a_spec, b_spec],
        out_specs=[pl.BlockSpec((per, R, C), lambda s, t: (s, 0, 0))],
        out_shape=[jax.ShapeDtypeStruct((N_CHIPS, R, C), BF16)],
        scratch=[pltpu.VMEM((per * R, C), F32)],
        sem=("parallel", "arbitrary"), operands=(a, b), rider=rider)[0]


def _grad_mix(mixed, dh, r, dret, pm, dpool, tk, name, rider=None):
    T, D = dh.shape
    Dq = D // N_CHIPS
    nt = T // tk

    def body(mx_ref, dh_ref, r_ref, dret_ref, pm_ref, dpool_ref, go_ref, gr_ref, gp_ref, ao_ref, ar_ref, ap_ref):
        t = pl.program_id(0)

        @pl.when(t == 0)
        def _():
            ao_ref[...] = jnp.zeros_like(ao_ref)
            ar_ref[...] = jnp.zeros_like(ar_ref)
            ap_ref[...] = jnp.zeros_like(ap_ref)

        ao_ref[...] += _dot_tn(mx_ref[...], dh_ref[...].astype(BF16))
        ar_ref[...] += _dot_tn(r_ref[...], dret_ref[...])
        ap_ref[...] += _dot_tn(pm_ref[...], dpool_ref[...])

        @pl.when(t == nt - 1)
        def _():
            go_ref[...] = ao_ref[...].astype(BF16).reshape(N_CHIPS, Dq, D)
            for s in range(N_CHIPS):
                gr_ref[s] = ar_ref[:, s * Dq:(s + 1) * Dq].astype(BF16)
                gp_ref[s] = ap_ref[:, s * Dq:(s + 1) * Dq].astype(BF16)

    row = pl.BlockSpec((tk, D), lambda t: (t, 0))
    half = pl.BlockSpec((tk, RET_WIDTH), lambda t: (t, 0))
    whole = lambda shape: pl.BlockSpec(shape, lambda t: (0, 0, 0))
    return _call(
        body, name=name, grid=(nt,),
        in_specs=[row, row, half, row, half, row],
        out_specs=[whole((N_CHIPS, Dq, D)), whole((N_CHIPS, RET_WIDTH, Dq)), whole((N_CHIPS, POOL_WIDTH, Dq))],
        out_shape=[jax.ShapeDtypeStruct((N_CHIPS, Dq, D), BF16),
                   jax.ShapeDtypeStruct((N_CHIPS, RET_WIDTH, Dq), BF16),
                   jax.ShapeDtypeStruct((N_CHIPS, POOL_WIDTH, Dq), BF16)],
        scratch=[pltpu.VMEM((D, D), F32), pltpu.VMEM((RET_WIDTH, D), F32), pltpu.VMEM((POOL_WIDTH, D), F32)],
        sem=("arbitrary",), operands=(mixed, dh, r, dret, pm, dpool), rider=rider)


def _mix_bwd_dx(dh, z, ret, pool, wout, wru, wpu, tm, name, rider=None):
    T, D = dh.shape
    Dq = D // N_CHIPS
    nb = D // RET_WIDTH

    def body(*refs):
        dh_ref = refs[0]
        gate_refs = refs[1:1 + 2 * nb]
        ret_ref, pool_ref, wout_ref, wru_ref, wpu_ref, dgab_ref, dret_ref, dpool_ref, dr_ref, dpm_ref = refs[1 + 2 * nb:]
        dmixed = _dot_nt(dh_ref[...].astype(BF16), wout_ref[...].reshape(D, D))
        ga, gb = _load_gates(gate_refs, nb)
        sa = _sigmoid(ga)
        sb = _sigmoid(gb)
        dgab_ref[:, :D] = (dmixed * ret_ref[...].astype(F32) * (sa * (1.0 - sa))).astype(BF16)
        dgab_ref[:, D:] = (dmixed * pool_ref[...].astype(F32) * (sb * (1.0 - sb))).astype(BF16)
        dret = (dmixed * sa).astype(BF16)
        dpool = (dmixed * sb).astype(BF16)
        dret_ref[...] = dret
        dpool_ref[...] = dpool
        dr = _dot_nt(dret[:, :Dq], wru_ref[0])
        dpm = _dot_nt(dpool[:, :Dq], wpu_ref[0])
        for s in range(1, N_CHIPS):
            dr += _dot_nt(dret[:, s * Dq:(s + 1) * Dq], wru_ref[s])
            dpm += _dot_nt(dpool[:, s * Dq:(s + 1) * Dq], wpu_ref[s])
        dr_ref[...] = dr
        dpm_ref[...] = dpm

    row = pl.BlockSpec((tm, D), lambda t: (t, 0))
    half = pl.BlockSpec((tm, RET_WIDTH), lambda t: (t, 0))
    up = pl.BlockSpec((N_CHIPS, RET_WIDTH, Dq), lambda t: (0, 0, 0))
    return _call(
        body, name=name, grid=(T // tm,),
        in_specs=[row] + _gate_specs(tm, D) + [row, row, pl.BlockSpec((N_CHIPS, Dq, D), lambda t: (0, 0, 0)), up, up],
        out_specs=[pl.BlockSpec((tm, 2 * D), lambda t: (t, 0)), row, row, half, half],
        out_shape=[jax.ShapeDtypeStruct((T, 2 * D), BF16), jax.ShapeDtypeStruct((T, D), BF16),
                   jax.ShapeDtypeStruct((T, D), BF16), jax.ShapeDtypeStruct((T, RET_WIDTH), F32),
                   jax.ShapeDtypeStruct((T, POOL_WIDTH), F32)],
        sem=("parallel",), operands=(dh, *([z] * (2 * nb)), ret, pool, wout, wru, wpu), rider=rider)


def _pool_bwd(z, dpm, maps, scale, layer, pad, name):
    T = z.shape[0]

    def body(zu, maps_ref, sc_ref, dpm_ref, du_ref, dmaps_ref, dsc_ref):
        g = pl.program_id(0)
        u = zu[...].astype(F32)
        pooled, div, valid = _pool_parts(u, g, T, pad)
        pb = pooled.astype(BF16)
        mb = maps_ref[...].astype(BF16)
        dp = dpm_ref[...]
        dsc_ref[...] = jnp.sum(dp * _dot(pb, mb), axis=0, keepdims=True)
        dyb = (dp * sc_ref[...]).astype(BF16)
        dmaps_ref[...] = _dot_tn(pb, dyb)
        dpooled = jnp.where(valid, _dot_nt(dyb, mb), 0.0)
        ahead = _select_group(_window_sums(dpooled / div, lambda k: T - k), g)
        du_ref[...] = jnp.where(valid, ahead - dpooled, 0.0).astype(BF16)

    blk = pl.BlockSpec((T, HEAD_DIM), lambda g: (0, g))
    return _call(
        body, name=name, grid=(POOL_GROUPS,),
        in_specs=_pool_specs(T, layer) + [blk],
        out_specs=[blk, pl.BlockSpec((None, HEAD_DIM, HEAD_DIM), lambda g: (g, 0, 0)),
                   pl.BlockSpec((1, HEAD_DIM), lambda g: (0, g))],
        out_shape=[jax.ShapeDtypeStruct((T, POOL_WIDTH), BF16),
                   jax.ShapeDtypeStruct((POOL_GROUPS, HEAD_DIM, HEAD_DIM), F32),
                   jax.ShapeDtypeStruct((1, POOL_WIDTH), F32)],
        sem=("parallel",), operands=(z, maps, scale, dpm))


def _ret_bwd_local(z, o_pre, s_all, dr, consts, cg, name):
    T = z.shape[0]
    N = T // CHUNK
    ng = N // cg
    tg = cg * CHUNK
    cosf, sinf, intra, _, qdec, _ = consts
    fwd = lambda g: g

    def body(zq, zk, zv, zg, o_ref, s_ref, dr_ref, cos_ref, sin_ref, m_ref, qd_ref,
             dq_ref, dg_ref, dk_ref, dv_ref, ds_ref):
        cosv = cos_ref[...]
        sinv = sin_ref[...]
        scale = HEAD_DIM ** -0.5
        q3 = (_rot(zq[...].astype(F32), cosv, sinv) * scale).reshape(cg, CHUNK, HEAD_DIM)
        k3 = _rot(zk[...].astype(F32), cosv, sinv).reshape(cg, CHUNK, HEAD_DIM)
        qb = q3.astype(BF16)
        kb = k3.astype(BF16)
        vb = zv[...].reshape(cg, CHUNK, HEAD_DIM).astype(BF16)
        mask = m_ref[...][None]
        sb = (_ein("ncd,nmd->ncm", qb, kb) * mask).astype(BF16)
        qdv = qd_ref[...][None]
        qdb = (q3 * qdv).astype(BF16)

        out = o_ref[...]
        xc = out - jnp.mean(out, axis=-1, keepdims=True)
        rstd = lax.rsqrt(jnp.mean(xc * xc, axis=-1, keepdims=True) + EPS)
        rn = xc * rstd
        g = zg[...].astype(F32)
        sg = _sigmoid(g)
        drv = dr_ref[...]
        dg_ref[...] = (drv * rn * (sg * (1.0 + g * (1.0 - sg)))).astype(BF16)
        drn = drv * (g * sg)
        dout = rstd * (drn - jnp.mean(drn, axis=-1, keepdims=True)
                       - rn * jnp.mean(drn * rn, axis=-1, keepdims=True))
        dob = dout.reshape(cg, CHUNK, HEAD_DIM).astype(BF16)

        dsb = (_ein("ncd,nmd->ncm", dob, vb) * mask).astype(BF16)
        dv_ref[...] = _ein("ncm,ncd->nmd", sb, dob).reshape(tg, HEAD_DIM)
        dk_ref[...] = _ein("ncm,ncd->nmd", dsb, qb).reshape(tg, HEAD_DIM)
        dq3 = _ein("ncm,nmd->ncd", dsb, kb) + _ein("nce,nde->ncd", dob, s_ref[...].astype(BF16)) * qdv
        dq_ref[...] = _rot_t(dq3.reshape(tg, HEAD_DIM) * scale, cosv, sinv).astype(BF16)
        ds_ref[...] = _ein("ncd,nce->nde", qdb, dob)

    tab = pl.BlockSpec((tg, HEAD_DIM), lambda h, g: (g, 0))
    per_head = pl.BlockSpec((None, CHUNK, HEAD_DIM), lambda h, g: (h, 0, 0))
    head_blk = pl.BlockSpec((tg, HEAD_DIM), lambda h, g: (g, h))
    state_blk = pl.BlockSpec((None, cg, HEAD_DIM, HEAD_DIM), lambda h, g: (h, g, 0, 0))
    return _call(
        body, name=name, grid=(RET_HEADS, ng),
        in_specs=[_head_specs(tg, i, fwd) for i in range(4)]
        + [head_blk, state_blk, head_blk, tab, tab, per_head, per_head],
        out_specs=[head_blk, head_blk, head_blk, head_blk, state_blk],
        out_shape=[jax.ShapeDtypeStruct((T, RET_WIDTH), BF16), jax.ShapeDtypeStruct((T, RET_WIDTH), BF16),
                   jax.ShapeDtypeStruct((T, RET_WIDTH), F32), jax.ShapeDtypeStruct((T, RET_WIDTH), F32),
                   jax.ShapeDtypeStruct((RET_HEADS, N, HEAD_DIM, HEAD_DIM), F32)],
        sem=("parallel", "parallel"), operands=(z, z, z, z, o_pre, s_all, dr, cosf, sinf, intra, qdec))


def _ret_bwd_state(z, dkp, dvp, ds, consts, cg, name):
    T = z.shape[0]
    N = T // CHUNK
    ng = N // cg
    tg = cg * CHUNK
    cosf, sinf, _, kdec, _, cdb = consts
    rev = lambda g: ng - 1 - g

    def body(zk, zv, dkp_ref, dvp_ref, ds_ref, cos_ref, sin_ref, kd_ref, cd_ref, dk_ref, dv_ref, gs_ref, dkv_ref):
        @pl.when(pl.program_id(1) == 0)
        def _():
            gs_ref[...] = jnp.zeros_like(gs_ref)

        cosv = cos_ref[...]
        sinv = sin_ref[...]
        cd = cd_ref[0:1, :]
        grad = gs_ref[...]
        for n in reversed(range(cg)):
            dkv_ref[n] = grad
            grad = ds_ref[n] + cd * grad
        gs_ref[...] = grad
        dkvb = dkv_ref[...].astype(BF16)
        kdv = kd_ref[...][None]
        k3 = _rot(zk[...].astype(F32), cosv, sinv).reshape(cg, CHUNK, HEAD_DIM)
        vb = zv[...].reshape(cg, CHUNK, HEAD_DIM).astype(BF16)
        dk3 = _ein("nce,nde->ncd", vb, dkvb) * kdv
        dv3 = _ein("ncd,nde->nce", (k3 * kdv).astype(BF16), dkvb)
        dk_ref[...] = _rot_t(dkp_ref[...] + dk3.reshape(tg, HEAD_DIM), cosv, sinv).astype(BF16)
        dv_ref[...] = (dvp_ref[...] + dv3.reshape(tg, HEAD_DIM)).astype(BF16)

    tab = pl.BlockSpec((tg, HEAD_DIM), lambda h, g: (rev(g), 0))
    head_blk = pl.BlockSpec((tg, HEAD_DIM), lambda h, g: (rev(g), h))
    return _call(
        body, name=name, grid=(RET_HEADS, ng),
        in_specs=[_head_specs(tg, 1, rev), _head_specs(tg, 2, rev), head_blk, head_blk,
                  pl.BlockSpec((None, cg, HEAD_DIM, HEAD_DIM), lambda h, g: (h, rev(g), 0, 0)),
                  tab, tab,
                  pl.BlockSpec((None, CHUNK, HEAD_DIM), lambda h, g: (h, 0, 0)),
                  pl.BlockSpec((None, 8, HEAD_DIM), lambda h, g: (h, 0, 0))],
        out_specs=[head_blk, head_blk],
        out_shape=[jax.ShapeDtypeStruct((T, RET_WIDTH), BF16)] * 2,
        scratch=[pltpu.VMEM((HEAD_DIM, HEAD_DIM), F32), pltpu.VMEM((cg, HEAD_DIM, HEAD_DIM), F32)],
        sem=("parallel", "arbitrary"), operands=(z, z, dkp, dvp, ds, cosf, sinf, kdec, cdb))


def _z_segments(pieces, ns):
    segs, at = [], 0
    for k, p in enumerate(pieces):
        width = p.shape[1]
        lo = at
        while lo < at + width:
            s = lo // ns
            hi = min(at + width, (s + 1) * ns)
            segs.append((k, lo - at, hi - at, s, lo - s * ns, hi - s * ns))
            lo = hi
        at += width
    assert at == N_CHIPS * ns and all(v % LANES == 0 for seg in segs for v in (seg[1], seg[2], seg[4], seg[5]))
    return segs


def _inproj_bwd_dx(pieces, win, h, gain, dh_in, layer, tm, pad, name, rider=None):
    T, D = h.shape
    Ns = win.shape[-1]
    n = len(pieces)
    segs = _z_segments(pieces, Ns)

    def body(*refs):
        piece_refs = refs[:n]
        w_ref, h_ref, g_ref, dhi_ref, dh_ref, dgain_ref = refs[n:]
        t = pl.program_id(0)

        @pl.when(t == 0)
        def _():
            dgain_ref[...] = jnp.zeros_like(dgain_ref)

        db = None
        for k, a, b, s, c, d in segs:
            term = _dot_nt(piece_refs[k][:, a:b], w_ref[s, :, c:d])
            db = term if db is None else db + term
        dx, dgain = _rms_bwd(h_ref[...], g_ref[...], db)
        dgain_ref[...] += dgain
        dh_ref[...] = jnp.where(_row_mask(t, tm, pad, (tm, D)), dhi_ref[...] + dx, 0.0)

    row = pl.BlockSpec((tm, D), lambda t: (t, 0))
    return _call(
        body, name=name, grid=(T // tm,),
        in_specs=[pl.BlockSpec((tm, p.shape[1]), lambda t: (t, 0)) for p in pieces]
        + [pl.BlockSpec((N_CHIPS, D, Ns), lambda t: (0, 0, 0)), row,
           pl.BlockSpec((None, 1, D), lambda t: (layer, 0, 0)), row],
        out_specs=[row, pl.BlockSpec((1, D), lambda t: (0, 0))],
        out_shape=[jax.ShapeDtypeStruct((T, D), F32), jax.ShapeDtypeStruct((1, D), F32)],
        sem=("arbitrary",), operands=(*pieces, win, h, gain, dh_in), rider=rider)


def _grad_w_in(b, pieces, ns, tk, name, rider=None):
    T, D = b.shape
    n = len(pieces)
    nt = T // tk
    segs = _z_segments(pieces, ns)
    shards_of = [sorted({s for k, _, _, s, _, _ in segs if k == i}) for i in range(n)]

    def body(*refs):
        b_ref = refs[0]
        piece_refs = refs[1:1 + n]
        o_ref, acc_ref = refs[1 + n:]
        s = pl.program_id(0)
        t = pl.program_id(1)

        @pl.when(t == 0)
        def _():
            acc_ref[...] = jnp.zeros_like(acc_ref)

        for shard in range(N_CHIPS):
            @pl.when(s == shard)
            def _(shard=shard):
                cols = [piece_refs[k][:, a:e] for k, a, e, ss, _, _ in segs if ss == shard]
                dz = cols[0] if len(cols) == 1 else jnp.concatenate(cols, axis=1)
                acc_ref[...] += _dot_tn(b_ref[...], dz)

        @pl.when(t == nt - 1)
        def _():
            o_ref[...] = acc_ref[...].astype(BF16).reshape(1, D, ns)

    def piece_spec(i):
        def index(s, t):
            used = functools.reduce(jnp.logical_or, [s == ss for ss in shards_of[i]])
            return (jnp.where(used, t, 0), 0)
        return pl.BlockSpec((tk, pieces[i].shape[1]), index)

    return _call(
        body, name=name, grid=(N_CHIPS, nt),
        in_specs=[pl.BlockSpec((tk, D), lambda s, t: (t, 0))] + [piece_spec(i) for i in range(n)],
        out_specs=[pl.BlockSpec((1, D, ns), lambda s, t: (s, 0, 0))],
        out_shape=[jax.ShapeDtypeStruct((N_CHIPS, D, ns), BF16)],
        scratch=[pltpu.VMEM((D, ns), F32)],
        sem=("parallel", "arbitrary"), operands=(b, *pieces), rider=rider)[0]


def _sum_pair(gs, rs, c_idx, name):
    n = len(gs)

    def body(c_ref, *refs):
        for g_ref, r_ref, o_ref in zip(refs[:n], refs[n:2 * n], refs[2 * n:]):
            o_ref[...] = (g_ref[...].astype(F32) + r_ref[...].astype(F32)).astype(BF16)

    halves = [pl.BlockSpec((None,) + r.shape[1:], lambda s, c_ref: (s, 0, 0)) for r in rs]
    return pl.pallas_call(
        body,
        name=name,
        grid_spec=pltpu.PrefetchScalarGridSpec(
            num_scalar_prefetch=1,
            grid=(N_CHIPS,),
            in_specs=[pl.BlockSpec((None,) + r.shape[1:], lambda s, c_ref: (s, c_ref[0], 0)) for r in rs] + halves,
            out_specs=halves,
        ),
        out_shape=[jax.ShapeDtypeStruct(r.shape, BF16) for r in rs],
        compiler_params=_params(("parallel",)),
    )(c_idx, *gs, *rs)


def _sum_chips(ps, rs, pos, name):
    n = len(ps)
    quarters = 4

    def body(pos_ref, *refs):
        chip = pos_ref[0]
        for p_ref, r_ref, o_ref in zip(refs[:n], refs[n:2 * n], refs[2 * n:]):
            own = p_ref[...].astype(F32)
            terms = [jnp.where(chip == k, own, r_ref[k].astype(F32)) for k in range(N_CHIPS)]
            o_ref[...] = ((terms[0] + terms[1]) + terms[2]) + terms[3]

    def rows(r):
        assert r.shape[1] % (quarters * BF16_ROWS) == 0, r.shape
        return r.shape[1] // quarters

    return pl.pallas_call(
        body,
        name=name,
        grid_spec=pltpu.PrefetchScalarGridSpec(
            num_scalar_prefetch=1,
            grid=(quarters,),
            in_specs=[pl.BlockSpec((None, rows(r), r.shape[2]), lambda q, pos_ref: (pos_ref[0], q, 0)) for r in rs]
            + [pl.BlockSpec((N_CHIPS, rows(r), r.shape[2]), lambda q, pos_ref: (0, q, 0)) for r in rs],
            out_specs=[pl.BlockSpec((rows(r), r.shape[2]), lambda q, pos_ref: (pos_ref[1] * quarters + q, 0))
                       for r in rs],
        ),
        out_shape=[jax.ShapeDtypeStruct((2 * r.shape[1], r.shape[2]), F32) for r in rs],
        compiler_params=_params(("arbitrary",)),
    )(pos, *ps, *rs)


def _small_all_reduce(p):
    rows, width = p.shape

    def body(p_ref, o_ref, sib_ref, slot_ref, ssem, rsem):
        x, y, c, chip, others = _mesh_pos()
        pair = _remote(p_ref, sib_ref, ssem.at[0], rsem.at[0], (x, y, 1 - c))
        pair.start()
        pair.wait()
        slot_ref[chip] = p_ref[...] + sib_ref[...]
        sends = []
        for j, (ox, oy) in enumerate(others):
            cp = _remote(slot_ref.at[chip], slot_ref.at[chip], ssem.at[1 + j], rsem.at[1 + j], (ox, oy, c))
            cp.start()
            sends.append(cp)
        for j, (ox, oy) in enumerate(others):
            slot = slot_ref.at[2 * ox + oy]
            _remote(slot, slot, ssem.at[1 + j], rsem.at[1 + j], (ox, oy, c)).wait_recv()
        for cp in sends:
            cp.wait_send()
        o_ref[...] = ((slot_ref[0] + slot_ref[1]) + slot_ref[2]) + slot_ref[3]

    vmem = pl.BlockSpec(memory_space=pltpu.VMEM)
    return pl.pallas_call(
        body,
        name="small_grads_all_reduce",
        in_specs=[vmem],
        out_specs=vmem,
        out_shape=jax.ShapeDtypeStruct(p.shape, F32),
        scratch_shapes=[pltpu.VMEM((rows, width), F32), pltpu.VMEM((N_CHIPS, rows, width), F32),
                        pltpu.SemaphoreType.DMA((4,)), pltpu.SemaphoreType.DMA((4,))],
    )(p)


def _adamw(gs, w, m, v, name):
    L, R, C = w.shape
    Ct = gs[0].shape[1]
    tr = _pick_tile(R, 256, 8)

    def body(*refs):
        g_refs = refs[:L]
        w_ref, m_ref, v_ref, go_ref, d_ref, mo_ref, vo_ref = refs[L:]
        layer = pl.program_id(0)
        grad = g_refs[L - 1][...]
        for i in range(L - 2, -1, -1):
            grad = jnp.where(layer == i, g_refs[i][...], grad)
        if Ct != C:
            grad = grad[:, :C]
        m_new = ADAM_B1 * m_ref[...] + (1.0 - ADAM_B1) * grad
        v_new = ADAM_B2 * v_ref[...] + (1.0 - ADAM_B2) * jnp.square(grad)
        m_hat = m_new / (1.0 - ADAM_B1 ** ADAM_STEP)
        v_hat = v_new / (1.0 - ADAM_B2 ** ADAM_STEP)
        go_ref[...] = grad
        d_ref[...] = -ADAM_LR * (m_hat / (jnp.sqrt(v_hat) + ADAM_EPS) + ADAM_WD * w_ref[...])
        mo_ref[...] = m_new
        vo_ref[...] = v_new

    g_specs = [pl.BlockSpec((tr, Ct), functools.partial(lambda l, r, i: (jnp.where(l == i, r, 0), 0), i=i))
               for i in range(L)]
    blk = pl.BlockSpec((None, tr, C), lambda l, r: (l, r, 0))
    return _call(
        body, name=name, grid=(L, R // tr),
        in_specs=g_specs + [blk, blk, blk],
        out_specs=[blk] * 4,
        out_shape=[jax.ShapeDtypeStruct((L, R, C), F32)] * 4,
        sem=("arbitrary", "arbitrary"), operands=(*gs, w, m, v))


_FFN1 = ("ffn1_gate", "ffn1_up", "ffn1_down")
_FFN2 = ("ffn2_gate", "ffn2_up", "ffn2_down")
_MIXW = ("w_ret_up", "w_pool_up", "w_out")
_BIG = _FFN1 + ("w_in",) + _MIXW + _FFN2
_TRANSPOSED = ("ffn1_gate", "ffn1_up", "ffn2_gate", "ffn2_up")
_SMALL = ("ffn1_norm", "mix_norm", "ffn2_norm", "final_norm", "pool_scale", "pool_maps")
_ORDER = ("meta", "ffn1_norm", "ffn1_gate", "ffn1_up", "ffn1_down", "mix_norm", "w_in", "pool_maps",
          "pool_scale", "w_ret_up", "w_pool_up", "w_out", "ffn2_norm", "ffn2_gate", "ffn2_up", "ffn2_down",
          "final_norm")


def _transport(a):
    n, r, c = a.shape
    out = a.astype(BF16)
    if c % LANES:
        out = jnp.concatenate([out, jnp.zeros((n, r, _round_up(c, LANES) - c), BF16)], axis=2)
    if r % LANES:
        out = jnp.concatenate([out, jnp.zeros((n, _round_up(r, LANES) - r, out.shape[2]), BF16)], axis=1)
    return out


def _pack_rows(parts, width):
    rows = [p.reshape(-1, width) for p in parts]
    total = sum(r.shape[0] for r in rows)
    fill = _round_up(total, 8) - total
    if fill:
        rows.append(jnp.zeros((fill, width), F32))
    return jnp.concatenate(rows, axis=0)


def _unpack_rows(packed, shapes, width):
    out, at = [], 0
    for shp in shapes:
        n = math.prod(shp) // width
        out.append(packed[at:at + n].reshape(shp))
        at += n
    return out


class _Weights:
    def __init__(self, shards):
        self.shards = shards
        self.full = {}

    def rider(self, keys):
        r = _gather_rider([(self.shards[n], i) for n, i in keys])
        r.keys = keys
        return r

    def take(self, rider):
        for key, arr in zip(rider.keys, rider.results):
            self.full[key] = arr

    def __call__(self, name, layer):
        return self.full[(name, layer)]


def _local_step(x, meta_full, tgt, w, wts, pad, tm, cg, reducer):
    D = x.shape[1]
    T = pad + N_META + x.shape[0]
    L = w["ffn1_norm"].shape[0]
    pool_maps = w["pool_maps"]
    gains = {n: w[n].reshape(L, 1, D) for n in ("ffn1_norm", "mix_norm", "ffn2_norm")}
    scale3 = w["pool_scale"].reshape(L, 1, POOL_WIDTH)
    consts = _ret_consts(T, pad)
    tl = _pick_tile(T, 2 * tm, BF16_ROWS)
    def gather(keys):
        return wts.rider(keys) if keys and keys[0] not in wts.full else None

    def done(rider):
        if rider is not None:
            wts.take(rider)

    h = jnp.concatenate([jnp.zeros((pad, D), F32), meta_full, x], axis=0)
    saved = []
    for i in range(L):
        s = {"h0": h}
        if ("ffn1_down", i) in wts.full:
            rd = gather([("w_in", i)] + [(n, i) for n in _MIXW])
            h, s["a1"], s["g1"], s["u1"], s["act1"] = _ffn_fwd(
                h, gains["ffn1_norm"], wts("ffn1_gate", i), wts("ffn1_up", i), wts("ffn1_down", i), i, tl,
                f"ffn1_fwd_{i}", rd)
            done(rd)
        else:
            rd = gather([("ffn1_down", i), ("w_in", i)])
            s["a1"], s["g1"], s["u1"], s["act1"] = _ffn_fwd_up(
                h, gains["ffn1_norm"], wts("ffn1_gate", i), wts("ffn1_up", i), i, tl, f"ffn1_fwd_up_{i}", rd)
            done(rd)
            rd = gather([(n, i) for n in _MIXW])
            h = _ffn_fwd_down(h, s["act1"], wts("ffn1_down", i), tl, f"ffn1_fwd_down_{i}", rd)
            done(rd)
        s["h1"] = h
        rd = gather([("ffn2_gate", i), ("ffn2_up", i)])
        s["z"], s["b"] = _inproj_fwd(h, gains["mix_norm"], wts("w_in", i), i, tl, f"inproj_fwd_{i}", rd)
        done(rd)
        s["r"], s["o_pre"], s["s_all"] = _ret_fwd(s["z"], consts, cg, f"retention_fwd_{i}")
        s["pm"] = _pool_fwd(s["z"], pool_maps, scale3, i, pad, f"pool_fwd_{i}")
        rd = gather([("ffn2_down", i)])
        h, s["mixed"], s["ret"], s["pool"] = _mix_fwd(
            h, s["r"], s["pm"], s["z"], wts("w_ret_up", i), wts("w_pool_up", i), wts("w_out", i), tl,
            f"mix_fwd_{i}", rd)
        done(rd)
        s["h2"] = h
        rd = gather([(n, i + 1) for n in _FFN1]) if i + 1 < L else None
        h, s["a2"], s["g2"], s["u2"], s["act2"] = _ffn_fwd(
            h, gains["ffn2_norm"], wts("ffn2_gate", i), wts("ffn2_up", i), wts("ffn2_down", i), i, tl,
            f"ffn2_fwd_{i}", rd)
        done(rd)
        saved.append(s)

    dh, loss_acc, d_final = _final_loss(h, w["final_norm"].reshape(1, D), tgt, "final_norm_loss")

    small = {n: [None] * L for n in ("ffn1_norm", "mix_norm", "ffn2_norm", "pool_scale", "pool_maps")}

    carry = {"ffn_act": 1.0, "ffn_in": 2.2, "mix_bwd": 1.0, "inproj_bwd": 1.5, "w_in": 1.0}

    tk = _pick_tile(T, 1408, LANES)

    def grad(n, a, b, i, mode):
        rd = reducer.rider(carry.get(n, 1.0 if i == 0 and n.startswith("ffn") else 0.5))
        reducer.add(n, i, _grad_tn(a, b, mode, 1.0, tk, f"grad_{n}_{i}", rd))
        reducer.done(rd)

    def ffn_bwd(which, dy, h_in, g, u, i, between=None, units=carry["ffn_in"]):
        rd = reducer.rider(carry["ffn_act"])
        dg, du, dyh = _ffn_bwd_act(dy, g, u, wts(f"{which}_down", i), tl, f"{which}_bwd_act_{i}", rd)
        reducer.done(rd)
        if between is not None:
            between(dg, du, dyh)
        rd = reducer.rider(units)
        dh_in, dgain = _ffn_bwd_in(dy, h_in, gains[f"{which}_norm"], dg, du, wts(f"{which}_gate", i),
                                   wts(f"{which}_up", i), i, tl, pad, f"{which}_bwd_in_{i}", rd)
        reducer.done(rd)
        return dh_in, dg, du, dgain, dyh

    for i in reversed(range(L)):
        s = saved[i]
        dh, dg, du, small["ffn2_norm"][i], dyh = ffn_bwd("ffn2", dh, s["h2"], s["g2"], s["u2"], i)
        grad("ffn2_gate", dg, s["a2"], i, "row")
        grad("ffn2_up", du, s["a2"], i, "row")
        grad("ffn2_down", s["act2"], dyh, i, "row")
        reducer.stage(f"ffn2_{i}")
        rd = reducer.rider(carry["mix_bwd"])
        dgab, dret, dpool, dr, dpm = _mix_bwd_dx(
            dh, s["z"], s["ret"], s["pool"], wts("w_out", i), wts("w_ret_up", i), wts("w_pool_up", i), tm,
            f"mix_bwd_{i}", rd)
        reducer.done(rd)
        rd = reducer.rider(0.5)
        g_out, g_ru, g_pu = _grad_mix(s["mixed"], dh, s["r"], dret, s["pm"], dpool, _pick_tile(T, 704, LANES),
                                      f"grad_mix_{i}", rd)
        reducer.done(rd)
        for n, g_n in (("w_out", g_out), ("w_ret_up", g_ru), ("w_pool_up", g_pu)):
            reducer.add(n, i, g_n)
        du_pool, small["pool_maps"][i], small["pool_scale"][i] = _pool_bwd(
            s["z"], dpm, pool_maps, scale3, i, pad, f"pool_bwd_{i}")
        dq, dgr, dkp, dvp, ds = _ret_bwd_local(s["z"], s["o_pre"], s["s_all"], dr, consts,
                                               _pick_tile(T // CHUNK, 11, 1), f"retention_bwd_{i}")
        dk, dv = _ret_bwd_state(s["z"], dkp, dvp, ds, consts, cg, f"retention_bwd_state_{i}")
        dz = [dq, dk, dv, dgr, du_pool, dgab]
        dh2 = dh
        rd = reducer.rider(carry["inproj_bwd"])
        dh, small["mix_norm"][i] = _inproj_bwd_dx(
            dz, wts("w_in", i), s["h1"], gains["mix_norm"], dh2, i, tm, pad, f"inproj_bwd_{i}", rd)
        reducer.done(rd)
        rd = reducer.rider(carry["w_in"])
        reducer.add("w_in", i, _grad_w_in(s["b"], dz, wts("w_in", i).shape[-1], tk, f"grad_w_in_{i}", rd))
        reducer.done(rd)
        reducer.stage(f"mid{i}")
        def ffn1_grads(dg, du, dyh, i=i, s=s):
            grad("ffn1_gate", dg, s["a1"], i, "row")
            if i == 0:
                reducer.stage("gate0")
            grad("ffn1_up", du, s["a1"], i, "row")
            if i == 0:
                reducer.stage("up0")
            grad("ffn1_down", s["act1"], dyh, i, "row")
            reducer.stage(f"end{i}")

        if i == 0:
            dh, _, _, small["ffn1_norm"][i], _ = ffn_bwd("ffn1", dh, s["h0"], s["g1"], s["u1"], i, ffn1_grads, 2.5)
        else:
            dh, dg, du, small["ffn1_norm"][i], dyh = ffn_bwd("ffn1", dh, s["h0"], s["g1"], s["u1"], i)
            ffn1_grads(dg, du, dyh)

    return loss_acc, dh, small, d_final


class _Reducer:
    def __init__(self, unit):
        self.c_idx = lax.axis_index("c").astype(jnp.int32).reshape(1)
        chip = 2 * lax.axis_index("x") + lax.axis_index("y")
        self.pos = jnp.stack([chip, lax.axis_index("c")]).astype(jnp.int32)
        self.pending, self.stages, self.queue, self.halves, self.whole = [], [], [], {}, {}
        self.unit = unit
        self.calls = 0

    def add(self, name, layer, g):
        self.pending.append(((name, layer), g))

    def stage(self, tag):
        if self.pending:
            self.stages.append((tag, self.pending))
            self.pending = []

    def _pair_rider(self):
        if not self.stages:
            return None
        tag, items = self.stages.pop(0)
        rd = _pair_exchange_rider([g for _, g in items])
        rd.tag, rd.keys = tag, [k for k, _ in items]
        return rd

    def _chip_rider(self, units):
        take, keep, size = [], [], 0
        for item in self.queue:
            if units is None or size + item[1].size <= units * self.unit:
                take.append(item)
                size += item[1].size
            else:
                keep.append(item)
        self.queue = keep
        if not take:
            return None
        rd = _chip_exchange_rider([p for _, p in take])
        rd.keys = [k for k, _ in take]
        return rd

    def _gather_rider(self):
        keys = [k for k in self.halves if k not in self.whole]
        if not keys:
            return None
        rd = _pair_gather_rider([self.halves[k] for k in keys])
        rd.keys = keys
        return rd

    def rider(self, units):
        self.riding = (self._pair_rider(), self._chip_rider(units), self._gather_rider())
        return _join(self.riding)

    def done(self, rd):
        if rd is None:
            return
        _split_results(rd)
        pair, chips, gather = self.riding
        if len([r for r in self.riding if r is not None]) == 1:
            (pair or chips or gather).results = rd.results
        self.calls += 1
        if gather is not None:
            self.whole.update(zip(gather.keys, gather.results))
        if pair is not None:
            sums = _sum_pair(pair.ins, pair.results, self.c_idx, f"sum_pair_{pair.tag}")
            self.queue += list(zip(pair.keys, sums))
        if chips is not None:
            sums = _sum_chips(chips.ins, chips.results, self.pos, f"sum_chips_{self.calls}")
            self.halves.update(zip(chips.keys, sums))

    def busy(self):
        assert not self.pending
        return bool(self.stages or self.queue or len(self.whole) < len(self.halves))

    def flush(self):
        self.riding = (self._pair_rider(), self._chip_rider(None), self._gather_rider())
        rd = _join(self.riding)
        _run_rider(rd, f"grads_exchange_tail_{self.calls}")
        self.done(rd)


def _update(loss_acc, grad_x, d_meta_rows, reducer, small, d_final, w, mom, var):
    meta = w["meta"]
    D = w["final_norm"].shape[0]
    L = w["ffn1_norm"].shape[0]
    Dq = D // N_CHIPS

    out = {}

    while reducer.busy():
        reducer.flush()
    for n in _BIG:
        gs = [reducer.whole[(n, i)] for i in range(L)]
        if n in _TRANSPOSED:
            res = _adamw(gs, *(jnp.swapaxes(t[n], 1, 2) for t in (w, mom, var)), f"adamw_{n}")
            out[n] = [jnp.swapaxes(r, 1, 2) for r in res]
        else:
            out[n] = _adamw(gs, w[n], mom[n], var[n], f"adamw_{n}")

    small_parts = [jnp.concatenate(small[n], axis=0) for n in ("ffn1_norm", "mix_norm", "ffn2_norm")]
    small_parts += [d_final, jnp.concatenate(small["pool_scale"], axis=0), jnp.concatenate(small["pool_maps"], axis=0)]
    loss_row = jnp.pad(loss_acc, ((0, 0), (0, D - loss_acc.shape[1])))
    reduced = _small_all_reduce(_pack_rows(small_parts + [d_meta_rows, loss_row], D))
    small_shapes = [w[n].shape for n in _SMALL]
    small_rows = sum(math.prod(shp) for shp in small_shapes) // D
    chip = 2 * lax.axis_index("x") + lax.axis_index("y")
    d_meta = lax.dynamic_slice_in_dim(reduced[small_rows:small_rows + N_META], chip * Dq, Dq, axis=1)
    names = _SMALL + ("meta",)
    packed_g = _pack_rows([reduced[:small_rows], d_meta], D)
    packed = [_pack_rows([t[n] for n in names], D) for t in (w, mom, var)]
    res = _adamw([packed_g], packed[0][None], packed[1][None], packed[2][None], "adamw_small")
    shapes = small_shapes + [meta.shape]
    unpacked = [_unpack_rows(r[0], shapes, D) for r in res]
    for k, n in enumerate(names):
        out[n] = tuple(u[k] for u in unpacked)

    loss = reduced[small_rows + N_META, 0]
    return (loss, grad_x) + tuple(out[n][j] for j in range(4) for n in _ORDER)


def kernel(x, meta, ffn1_norm, ffn1_gate, ffn1_up, ffn1_down, mix_norm, w_in, pool_maps, pool_scale, w_ret_up, w_pool_up, w_out, ffn2_norm, ffn2_gate, ffn2_up, ffn2_down, final_norm, loss_target, m_meta, m_ffn1_norm, m_ffn1_gate, m_ffn1_up, m_ffn1_down, m_mix_norm, m_w_in, m_pool_maps, m_pool_scale, m_w_ret_up, m_w_pool_up, m_w_out, m_ffn2_norm, m_ffn2_gate, m_ffn2_up, m_ffn2_down, m_final_norm, v_meta, v_ffn1_norm, v_ffn1_gate, v_ffn1_up, v_ffn1_down, v_mix_norm, v_w_in, v_pool_maps, v_pool_scale, v_w_ret_up, v_w_pool_up, v_w_out, v_ffn2_norm, v_ffn2_gate, v_ffn2_up, v_ffn2_down, v_final_norm):
    args = dict(locals())
    w = {n: args[n] for n in _ORDER}
    mom = {n: args["m_" + n] for n in _ORDER}
    var = {n: args["v_" + n] for n in _ORDER}

    assert x.shape[0] == 1, "one batch element per device"
    seq, D = x.shape[1], x.shape[2]
    assert seq % CHUNK == 0 and D % RET_WIDTH == 0 and (2 * POOL_WIDTH) % D == 0
    pad = (-(seq + N_META)) % CHUNK
    T = seq + N_META + pad
    tm = _pick_tile(T, 528, BF16_ROWS)
    cg = _pick_tile(T // CHUNK, 33, 1)

    shards = {n: _transport(w[n]) for n in _BIG}
    shards["meta"] = meta[None]
    wts = _Weights(shards)
    head = wts.rider([("ffn1_gate", 0), ("ffn1_up", 0), ("meta", 0)])
    _run_rider(head, "weights_gather_head")
    wts.take(head)
    meta_full = jnp.transpose(wts("meta", 0), (1, 0, 2)).reshape(N_META, D)

    reducer = _Reducer(unit=2 * shards["ffn1_gate"][0].size)
    loss_acc, dh, small, d_final = _local_step(x[0], meta_full, loss_target[0], w, wts, pad, tm, cg, reducer)
    grad_x = dh[pad + N_META:][None]
    return _update(loss_acc, grad_x, dh[pad:pad + N_META], reducer, small, d_final, w, mom, var)
```

```python
import functools
import math

import jax
import jax.numpy as jnp
from jax import lax
from jax.experimental import pallas as pl
from jax.experimental.pallas import tpu as pltpu

F32 = jnp.float32
BF16 = jnp.bfloat16

N_META = 16
RET_HEADS = 4
HEAD_DIM = 128
RET_WIDTH = RET_HEADS * HEAD_DIM
POOL_WINDOWS = (2, 4, 8, 16)
POOL_GROUPS = len(POOL_WINDOWS)
POOL_WIDTH = POOL_GROUPS * HEAD_DIM
CHUNK = 128
ROPE_BASE = 10000.0
EPS = 1e-6
ADAM_LR = 0.001
ADAM_B1 = 0.9
ADAM_B2 = 0.999
ADAM_EPS = 1e-08
ADAM_WD = 0.01
ADAM_STEP = 10

N_CHIPS = 4
LANES = 128
BF16_ROWS = 16
V7X_VMEM_LIMIT = 52 * 1024 * 1024
MESH = pl.DeviceIdType.MESH
ANY = pl.BlockSpec(memory_space=pl.ANY)


def _round_up(n, m):
    return -(-n // m) * m


def _pick_tile(n, target, mult):
    best = None
    for d in range(mult, min(n, target) + 1, mult):
        if n % d == 0:
            best = d
    assert best is not None, (n, target, mult)
    return best


def _params(sem=None):
    return pltpu.CompilerParams(dimension_semantics=sem, vmem_limit_bytes=V7X_VMEM_LIMIT)


def _dot(a, b):
    return jnp.dot(a, b, preferred_element_type=F32)


def _dot_nt(a, b):
    return lax.dot_general(a, b, (((1,), (1,)), ((), ())), preferred_element_type=F32)


def _dot_tn(a, b):
    return lax.dot_general(a, b, (((0,), (0,)), ((), ())), preferred_element_type=F32)


def _ein(spec, a, b):
    return jnp.einsum(spec, a, b, preferred_element_type=F32)


def _sigmoid(x):
    return jax.nn.sigmoid(x)


def _rms_fwd(x, gain):
    r = lax.rsqrt(jnp.mean(x * x, axis=-1, keepdims=True) + EPS)
    return x * r * gain


def _rms_bwd(x, gain, da):
    r = lax.rsqrt(jnp.mean(x * x, axis=-1, keepdims=True) + EPS)
    xh = x * r
    dgain = jnp.sum(da * xh, axis=0, keepdims=True)
    dxh = da * gain
    dx = r * (dxh - xh * jnp.mean(dxh * xh, axis=-1, keepdims=True))
    return dx, dgain


def _row_mask(t, tm, pad, shape):
    rows = t * tm + lax.broadcasted_iota(jnp.int32, shape, 0)
    return rows >= pad


def _mesh_pos():
    x, y, c = lax.axis_index("x"), lax.axis_index("y"), lax.axis_index("c")
    others = [(1 - x, y), (x, 1 - y), (1 - x, 1 - y)]
    return x, y, c, 2 * x + y, others


def _half_rows(c, rh):
    return pl.ds(pl.multiple_of(c * rh, rh), rh)


def _remote(src, dst, ssem, rsem, dev):
    return pltpu.make_async_remote_copy(src_ref=src, dst_ref=dst, send_sem=ssem, recv_sem=rsem,
                                        device_id=dev, device_id_type=MESH)


class _Rider:
    def __init__(self, ins, out_shapes, n_sem, start, finish, in_place=False):
        self.ins, self.out_shapes, self.n_sem, self.start, self.finish = ins, out_shapes, n_sem, start, finish
        self.in_place = [in_place] * len(ins)
        self.results = None

    def aliases(self, first_in, first_out):
        return {first_in + i: first_out + i for i, same in enumerate(self.in_place) if same}


class _SemWindow:
    def __init__(self, ref, base):
        self.ref, self.base = ref, base

    @property
    def at(self):
        return self

    def __getitem__(self, k):
        return self.ref.at[self.base + k]


def _join(riders):
    riders = [r for r in riders if r is not None]
    if len(riders) <= 1:
        return riders[0] if riders else None

    def run(which):
        def go(ins, outs, ssem, rsem):
            at, sem = 0, 0
            for r in riders:
                n = len(r.ins)
                getattr(r, which)(ins[at:at + n], outs[at:at + n], _SemWindow(ssem, sem), _SemWindow(rsem, sem))
                at, sem = at + n, sem + r.n_sem
        return go

    joined = _Rider(sum([list(r.ins) for r in riders], []), sum([list(r.out_shapes) for r in riders], []),
                    sum(r.n_sem for r in riders), run("start"), run("finish"))
    joined.in_place = sum([r.in_place for r in riders], [])
    joined.parts = riders
    return joined


def _split_results(rider):
    at = 0
    for r in getattr(rider, "parts", []):
        r.results = rider.results[at:at + len(r.ins)]
        at += len(r.ins)


def _gather_rider(pieces):
    per = 7
    layers = [layer for _, layer in pieces]

    def first_copies(ins, outs, ssem, rsem):
        x, y, c, chip, others = _mesh_pos()
        copies = []
        for i, layer in enumerate(layers):
            mine = _half_rows(c, ins[i].shape[1] // 2)
            for j, (ox, oy) in enumerate(others):
                copies.append(_remote(ins[i].at[layer, mine, :], outs[i].at[chip, mine, :],
                                      ssem.at[per * i + j], rsem.at[per * i + j], (ox, oy, c)))
            copies.append(_remote(ins[i].at[layer], outs[i].at[chip],
                                  ssem.at[per * i + 6], rsem.at[per * i + 6], (x, y, 1 - c)))
        return copies

    def start(ins, outs, ssem, rsem):
        for cp in first_copies(ins, outs, ssem, rsem):
            cp.start()

    def finish(ins, outs, ssem, rsem):
        x, y, c, chip, others = _mesh_pos()
        sibling = (x, y, 1 - c)
        forwards = []
        for i in range(len(layers)):
            mine = _half_rows(c, ins[i].shape[1] // 2)
            for j, (ox, oy) in enumerate(others):
                rows = outs[i].at[2 * ox + oy, mine, :]
                _remote(rows, rows, ssem.at[per * i + j], rsem.at[per * i + j], (ox, oy, c)).wait_recv()
                fwd = _remote(rows, rows, ssem.at[per * i + 3 + j], rsem.at[per * i + 3 + j], sibling)
                fwd.start()
                forwards.append(fwd)
        for i in range(len(layers)):
            theirs = _half_rows(1 - c, ins[i].shape[1] // 2)
            for j, (ox, oy) in enumerate(others):
                rows = outs[i].at[2 * ox + oy, theirs, :]
                _remote(rows, rows, ssem.at[per * i + 3 + j], rsem.at[per * i + 3 + j], sibling).wait_recv()
            own = outs[i].at[chip]
            _remote(own, own, ssem.at[per * i + 6], rsem.at[per * i + 6], sibling).wait_recv()
        for cp in first_copies(ins, outs, ssem, rsem) + forwards:
            cp.wait_send()

    shapes = [jax.ShapeDtypeStruct((N_CHIPS,) + s.shape[1:], s.dtype) for s, _ in pieces]
    return _Rider([s for s, _ in pieces], shapes, per * len(pieces), start, finish)


def _chip_exchange_rider(ps):
    def copies(ins, outs, ssem, rsem):
        x, y, c, chip, others = _mesh_pos()
        return [_remote(ins[i].at[2 * ox + oy], outs[i].at[chip], ssem.at[3 * i + j], rsem.at[3 * i + j], (ox, oy, c))
                for i in range(len(ps)) for j, (ox, oy) in enumerate(others)]

    def start(ins, outs, ssem, rsem):
        for cp in copies(ins, outs, ssem, rsem):
            cp.start()

    def finish(ins, outs, ssem, rsem):
        x, y, c, chip, others = _mesh_pos()
        for i in range(len(ps)):
            for j, (ox, oy) in enumerate(others):
                slot = outs[i].at[2 * ox + oy]
                _remote(slot, slot, ssem.at[3 * i + j], rsem.at[3 * i + j], (ox, oy, c)).wait_recv()
        for cp in copies(ins, outs, ssem, rsem):
            cp.wait_send()

    return _Rider(list(ps), [jax.ShapeDtypeStruct(p.shape, p.dtype) for p in ps], 3 * len(ps), start, finish)


def _pair_exchange_rider(gs):
    def copies(ins, outs, ssem, rsem):
        x, y, c, _, _ = _mesh_pos()
        return [_remote(ins[i].at[:, _half_rows(1 - c, ins[i].shape[1] // 2), :], outs[i],
                        ssem.at[i], rsem.at[i], (x, y, 1 - c)) for i in range(len(gs))]

    def start(ins, outs, ssem, rsem):
        for cp in copies(ins, outs, ssem, rsem):
            cp.start()

    def finish(ins, outs, ssem, rsem):
        for cp in copies(ins, outs, ssem, rsem):
            cp.wait()

    shapes = [jax.ShapeDtypeStruct((g.shape[0], g.shape[1] // 2, g.shape[2]), g.dtype) for g in gs]
    return _Rider(list(gs), shapes, len(gs), start, finish)


def _run_rider(rider, name):
    def body(*refs):
        n = len(rider.ins)
        ins, outs = refs[:n], refs[n:2 * n]
        ssem, rsem = refs[2 * n:]
        rider.start(ins, outs, ssem, rsem)
        rider.finish(ins, outs, ssem, rsem)

    rider.results = pl.pallas_call(
        body,
        name=name,
        in_specs=[ANY] * len(rider.ins),
        out_specs=[ANY] * len(rider.ins),
        out_shape=rider.out_shapes,
        input_output_aliases=rider.aliases(0, 0),
        scratch_shapes=[pltpu.SemaphoreType.DMA((rider.n_sem,)), pltpu.SemaphoreType.DMA((rider.n_sem,))],
    )(*rider.ins)
    return rider.results


def _pair_gather_rider(fs):
    n = len(fs)

    def copies(outs, ssem, rsem):
        x, y, c, _, _ = _mesh_pos()
        halves = [outs[i].at[_half_rows(c, outs[i].shape[0] // 2), :] for i in range(n)]
        return [_remote(h, h, ssem.at[i], rsem.at[i], (x, y, 1 - c)) for i, h in enumerate(halves)]

    def start(ins, outs, ssem, rsem):
        for cp in copies(outs, ssem, rsem):
            cp.start()

    def finish(ins, outs, ssem, rsem):
        x, y, c, _, _ = _mesh_pos()
        for i in range(n):
            theirs = outs[i].at[_half_rows(1 - c, outs[i].shape[0] // 2), :]
            _remote(theirs, theirs, ssem.at[i], rsem.at[i], (x, y, 1 - c)).wait_recv()
        for cp in copies(outs, ssem, rsem):
            cp.wait_send()

    return _Rider(list(fs), [jax.ShapeDtypeStruct(f.shape, f.dtype) for f in fs], n, start, finish, in_place=True)


def _call(body, *, name, grid, in_specs, out_specs, out_shape, operands, scratch=(), sem=None, rider=None):
    if rider is None:
        return pl.pallas_call(
            body, name=name, grid=grid, in_specs=in_specs, out_specs=out_specs, out_shape=out_shape,
            scratch_shapes=list(scratch), compiler_params=_params(sem))(*operands)
    n_in, n_out, n_sc, r = len(in_specs), len(out_specs), len(scratch), len(rider.ins)

    def carrying(*refs):
        a, b = n_in, n_in + r
        c, d = b + n_out, b + n_out + r
        e = d + n_sc
        ids = [pl.program_id(k) for k in range(len(grid))]
        first = functools.reduce(jnp.logical_and, [i == 0 for i in ids])
        last = functools.reduce(jnp.logical_and, [i == g - 1 for i, g in zip(ids, grid)])

        @pl.when(first)
        def _():
            rider.start(refs[a:b], refs[c:d], refs[e], refs[e + 1])

        body(*refs[:a], *refs[b:c], *refs[d:e])

        @pl.when(last)
        def _():
            rider.finish(refs[a:b], refs[c:d], refs[e], refs[e + 1])

    outs = pl.pallas_call(
        carrying, name=name, grid=grid,
        in_specs=list(in_specs) + [ANY] * r,
        out_specs=list(out_specs) + [ANY] * r,
        out_shape=list(out_shape) + list(rider.out_shapes),
        scratch_shapes=list(scratch) + [pltpu.SemaphoreType.DMA((rider.n_sem,)), pltpu.SemaphoreType.DMA((rider.n_sem,))],
        input_output_aliases=rider.aliases(n_in, n_out),
        compiler_params=_params(("arbitrary",) * len(grid)),
    )(*operands, *rider.ins)
    rider.results = outs[n_out:]
    return outs[:n_out]


def _ffn_fwd(h, gain, wg, wu, wd, layer, tm, name, rider=None):
    T, D = h.shape
    Fs = wg.shape[-1]
    F = N_CHIPS * Fs

    def body(h_ref, g_ref, wg_ref, wu_ref, wd_ref, ho_ref, a_ref, go_ref, uo_ref, act_ref, acc_ref):
        s = pl.program_id(1)

        @pl.when(s == 0)
        def _():
            a_ref[...] = _rms_fwd(h_ref[...], g_ref[...]).astype(BF16)
            acc_ref[...] = jnp.zeros_like(acc_ref)

        a = a_ref[...]
        g = _dot(a, wg_ref[...])
        u = _dot(a, wu_ref[...])
        sg = _sigmoid(g)
        act = (g * sg * u).astype(BF16)
        go_ref[...] = (u * (sg * (1.0 + g * (1.0 - sg)))).astype(BF16)
        uo_ref[...] = (g * sg).astype(BF16)
        act_ref[...] = act
        acc_ref[...] += _dot(act, wd_ref[...])

        @pl.when(s == N_CHIPS - 1)
        def _():
            ho_ref[...] = h_ref[...] + 0.5 * acc_ref[...]

    row = pl.BlockSpec((tm, D), lambda t, s: (t, 0))
    col = pl.BlockSpec((tm, Fs), lambda t, s: (t, s))
    wcol = pl.BlockSpec((None, D, Fs), lambda t, s: (s, 0, 0))
    return _call(
        body, name=name, grid=(T // tm, N_CHIPS),
        in_specs=[row, pl.BlockSpec((None, 1, D), lambda t, s: (layer, 0, 0)), wcol, wcol,
                  pl.BlockSpec((None, Fs, D), lambda t, s: (s, 0, 0))],
        out_specs=[row, row, col, col, col],
        out_shape=[jax.ShapeDtypeStruct((T, D), F32), jax.ShapeDtypeStruct((T, D), BF16)]
        + [jax.ShapeDtypeStruct((T, F), BF16)] * 3,
        scratch=[pltpu.VMEM((tm, D), F32)],
        sem=("parallel", "arbitrary"), operands=(h, gain, wg, wu, wd), rider=rider)


def _ffn_fwd_up(h, gain, wg, wu, layer, tm, name, rider=None):
    T, D = h.shape
    Fs = wg.shape[-1]
    F = N_CHIPS * Fs

    def body(h_ref, g_ref, wg_ref, wu_ref, a_ref, go_ref, uo_ref, act_ref):
        @pl.when(pl.program_id(1) == 0)
        def _():
            a_ref[...] = _rms_fwd(h_ref[...], g_ref[...]).astype(BF16)

        a = a_ref[...]
        g = _dot(a, wg_ref[...])
        u = _dot(a, wu_ref[...])
        sg = _sigmoid(g)
        act_ref[...] = (g * sg * u).astype(BF16)
        go_ref[...] = (u * (sg * (1.0 + g * (1.0 - sg)))).astype(BF16)
        uo_ref[...] = (g * sg).astype(BF16)

    row = pl.BlockSpec((tm, D), lambda t, s: (t, 0))
    col = pl.BlockSpec((tm, Fs), lambda t, s: (t, s))
    wcol = pl.BlockSpec((None, D, Fs), lambda t, s: (s, 0, 0))
    return _call(
        body, name=name, grid=(T // tm, N_CHIPS),
        in_specs=[row, pl.BlockSpec((None, 1, D), lambda t, s: (layer, 0, 0)), wcol, wcol],
        out_specs=[row, col, col, col],
        out_shape=[jax.ShapeDtypeStruct((T, D), BF16)] + [jax.ShapeDtypeStruct((T, F), BF16)] * 3,
        sem=("parallel", "arbitrary"), operands=(h, gain, wg, wu), rider=rider)


def _ffn_fwd_down(h, act, wd, tm, name, rider=None):
    T, D = h.shape
    Fs = wd.shape[1]

    def body(h_ref, act_ref, wd_ref, ho_ref, acc_ref):
        s = pl.program_id(1)

        @pl.when(s == 0)
        def _():
            acc_ref[...] = jnp.zeros_like(acc_ref)

        acc_ref[...] += _dot(act_ref[...], wd_ref[...])

        @pl.when(s == N_CHIPS - 1)
        def _():
            ho_ref[...] = h_ref[...] + 0.5 * acc_ref[...]

    row = pl.BlockSpec((tm, D), lambda t, s: (t, 0))
    return _call(
        body, name=name, grid=(T // tm, N_CHIPS),
        in_specs=[row, pl.BlockSpec((tm, Fs), lambda t, s: (t, s)), pl.BlockSpec((None, Fs, D), lambda t, s: (s, 0, 0))],
        out_specs=[row],
        out_shape=[jax.ShapeDtypeStruct((T, D), F32)],
        scratch=[pltpu.VMEM((tm, D), F32)],
        sem=("parallel", "arbitrary"), operands=(h, act, wd), rider=rider)[0]


def _inproj_fwd(h, gain, win, layer, tm, name, rider=None):
    T, D = h.shape
    Ns = win.shape[-1]

    def body(h_ref, g_ref, w_ref, z_ref, b_ref):
        @pl.when(pl.program_id(1) == 0)
        def _():
            b_ref[...] = _rms_fwd(h_ref[...], g_ref[...]).astype(BF16)

        z_ref[...] = _dot(b_ref[...], w_ref[...]).astype(BF16)

    return _call(
        body, name=name, grid=(T // tm, N_CHIPS),
        in_specs=[pl.BlockSpec((tm, D), lambda t, s: (t, 0)),
                  pl.BlockSpec((None, 1, D), lambda t, s: (layer, 0, 0)),
                  pl.BlockSpec((None, D, Ns), lambda t, s: (s, 0, 0))],
        out_specs=[pl.BlockSpec((tm, Ns), lambda t, s: (t, s)), pl.BlockSpec((tm, D), lambda t, s: (t, 0))],
        out_shape=[jax.ShapeDtypeStruct((T, N_CHIPS * Ns), BF16), jax.ShapeDtypeStruct((T, D), BF16)],
        sem=("parallel", "arbitrary"), operands=(h, gain, win), rider=rider)


def _ret_consts(T, pad):
    half = HEAD_DIM // 2
    inv_freq = ROPE_BASE ** (-jnp.arange(half, dtype=F32) / half)
    pos = jnp.arange(T, dtype=F32) - pad
    ang = pos[:, None] * inv_freq[None, :]
    cos = jnp.cos(ang)
    sin = jnp.sin(ang)
    cosf = jnp.concatenate([cos, cos], axis=1)
    sinf = jnp.concatenate([-sin, sin], axis=1)
    log_gamma = jnp.log1p(-(2.0 ** (-5.0 - jnp.arange(RET_HEADS, dtype=F32))))
    idx = jnp.arange(CHUNK, dtype=F32)
    diff = idx[:, None] - idx[None, :]
    intra = jnp.where(diff[None] >= 0, jnp.exp(diff[None] * log_gamma[:, None, None]), 0.0)
    k_decay = jnp.exp((CHUNK - 1.0 - idx)[None, :] * log_gamma[:, None])
    q_decay = jnp.exp((idx + 1.0)[None, :] * log_gamma[:, None])
    chunk_decay = jnp.exp(CHUNK * log_gamma)
    kdec = jnp.broadcast_to(k_decay[:, :, None], (RET_HEADS, CHUNK, HEAD_DIM))
    qdec = jnp.broadcast_to(q_decay[:, :, None], (RET_HEADS, CHUNK, HEAD_DIM))
    cdb = jnp.broadcast_to(chunk_decay[:, None, None], (RET_HEADS, 8, HEAD_DIM))
    return cosf, sinf, intra, kdec, qdec, cdb


def _rot(t, cosv, sinv):
    return t * cosv + pltpu.roll(t, HEAD_DIM // 2, 1) * sinv


def _rot_t(g, cosv, sinv):
    return g * cosv + pltpu.roll(g * sinv, HEAD_DIM // 2, 1)


def _head_specs(tg, section, order):
    return pl.BlockSpec((tg, HEAD_DIM), lambda h, g: (order(g), section * RET_HEADS + h))


def _ret_fwd(z, consts, cg, name, rider=None):
    T = z.shape[0]
    N = T // CHUNK
    ng = N // cg
    tg = cg * CHUNK
    cosf, sinf, intra, kdec, qdec, cdb = consts
    fwd = lambda g: g

    def body(zq, zk, zv, zg, cos_ref, sin_ref, m_ref, kd_ref, qd_ref, cd_ref, r_ref, o_ref, s_ref, st_ref):
        @pl.when(pl.program_id(1) == 0)
        def _():
            st_ref[...] = jnp.zeros_like(st_ref)

        cosv = cos_ref[...]
        sinv = sin_ref[...]
        q3 = (_rot(zq[...].astype(F32), cosv, sinv) * (HEAD_DIM ** -0.5)).reshape(cg, CHUNK, HEAD_DIM)
        k3 = _rot(zk[...].astype(F32), cosv, sinv).reshape(cg, CHUNK, HEAD_DIM)
        vb = zv[...].reshape(cg, CHUNK, HEAD_DIM).astype(BF16)
        scores = _ein("ncd,nmd->ncm", q3.astype(BF16), k3.astype(BF16)) * m_ref[...][None]
        inner = _ein("ncm,nmd->ncd", scores.astype(BF16), vb)
        kv = _ein("ncd,nce->nde", (k3 * kd_ref[...][None]).astype(BF16), vb)
        cd = cd_ref[0:1, :]
        state = st_ref[...]
        for n in range(cg):
            s_ref[n] = state
            state = state * cd + kv[n]
        st_ref[...] = state
        qdb = (q3 * qd_ref[...][None]).astype(BF16)
        cross = _ein("ncd,nde->nce", qdb, s_ref[...].astype(BF16))
        out = (inner + cross).reshape(tg, HEAD_DIM)
        o_ref[...] = out
        xc = out - jnp.mean(out, axis=-1, keepdims=True)
        rn = xc * lax.rsqrt(jnp.mean(xc * xc, axis=-1, keepdims=True) + EPS)
        g = zg[...].astype(F32)
        r_ref[...] = (rn * (g * _sigmoid(g))).astype(BF16)

    tab = pl.BlockSpec((tg, HEAD_DIM), lambda h, g: (g, 0))
    per_head = lambda rows: pl.BlockSpec((None, rows, HEAD_DIM), lambda h, g: (h, 0, 0))
    head_out = pl.BlockSpec((tg, HEAD_DIM), lambda h, g: (g, h))
    return _call(
        body, name=name, grid=(RET_HEADS, ng),
        in_specs=[_head_specs(tg, i, fwd) for i in range(4)]
        + [tab, tab, per_head(CHUNK), per_head(CHUNK), per_head(CHUNK), per_head(8)],
        out_specs=[head_out, head_out, pl.BlockSpec((None, cg, HEAD_DIM, HEAD_DIM), lambda h, g: (h, g, 0, 0))],
        out_shape=[jax.ShapeDtypeStruct((T, RET_WIDTH), BF16), jax.ShapeDtypeStruct((T, RET_WIDTH), F32),
                   jax.ShapeDtypeStruct((RET_HEADS, N, HEAD_DIM, HEAD_DIM), F32)],
        scratch=[pltpu.VMEM((HEAD_DIM, HEAD_DIM), F32)],
        sem=("parallel", "arbitrary"), operands=(z, z, z, z, cosf, sinf, intra, kdec, qdec, cdb), rider=rider)


def _window_sums(u, shift_of):
    sums = []
    s = u
    k = 1
    while k < POOL_WINDOWS[-1]:
        s = s + pltpu.roll(s, shift_of(k), 0)
        sums.append(s)
        k *= 2
    return sums


def _select_group(vals, g):
    out = vals[-1]
    for i in range(len(vals) - 2, -1, -1):
        out = jnp.where(g == i, vals[i], out)
    return out


def _pool_parts(u, g, T, pad):
    rows = lax.broadcasted_iota(jnp.int32, (T, HEAD_DIM), 0)
    valid = rows >= pad
    win = _select_group([float(w) for w in POOL_WINDOWS], g)
    div = jnp.clip((rows - pad + 1).astype(F32), 1.0, win)
    s = _select_group(_window_sums(u, lambda k: k), g)
    pooled = jnp.where(valid, s / div - u, 0.0)
    return pooled, div, valid


def _pool_specs(T, layer):
    first = 4 * RET_WIDTH // HEAD_DIM
    return [
        pl.BlockSpec((T, HEAD_DIM), lambda g: (0, first + g)),
        pl.BlockSpec((None, None, HEAD_DIM, HEAD_DIM), lambda g: (layer, g, 0, 0)),
        pl.BlockSpec((None, 1, HEAD_DIM), lambda g: (layer, 0, g)),
    ]


def _pool_fwd(z, maps, scale, layer, pad, name):
    T = z.shape[0]
    assert pad >= POOL_WINDOWS[-1], "window rolls wrap into the zero rows in front"

    def body(zu, maps_ref, sc_ref, pm_ref):
        g = pl.program_id(0)
        pooled, _, _ = _pool_parts(zu[...].astype(F32), g, T, pad)
        y = _dot(pooled.astype(BF16), maps_ref[...].astype(BF16))
        pm_ref[...] = (y * sc_ref[...]).astype(BF16)

    return _call(
        body, name=name, grid=(POOL_GROUPS,),
        in_specs=_pool_specs(T, layer),
        out_specs=[pl.BlockSpec((T, HEAD_DIM), lambda g: (0, g))],
        out_shape=[jax.ShapeDtypeStruct((T, POOL_WIDTH), BF16)],
        sem=("parallel",), operands=(z, maps, scale))[0]


def _gate_specs(tm, D):
    nb = D // RET_WIDTH
    first = (4 * RET_WIDTH + POOL_WIDTH) // RET_WIDTH
    return [pl.BlockSpec((tm, RET_WIDTH), functools.partial(lambda t, j: (t, j), j=first + j)) for j in range(2 * nb)]


def _load_gates(refs, nb):
    ga = jnp.concatenate([r[...].astype(F32) for r in refs[:nb]], axis=1)
    gb = jnp.concatenate([r[...].astype(F32) for r in refs[nb:]], axis=1)
    return ga, gb


def _mix_fwd(h, r, pm, z, wru, wpu, wout, tm, name, rider=None):
    T, D = h.shape
    Dq = D // N_CHIPS
    nb = D // RET_WIDTH

    def body(*refs):
        h_ref, r_ref, pm_ref = refs[:3]
        gate_refs = refs[3:3 + 2 * nb]
        wru_ref, wpu_ref, wout_ref, ho_ref, mx_ref, ret_ref, pool_ref = refs[3 + 2 * nb:]
        rv = r_ref[...]
        pv = pm_ref[...]
        ret = jnp.concatenate([_dot(rv, wru_ref[s]) for s in range(N_CHIPS)], axis=1)
        pool = jnp.concatenate([_dot(pv, wpu_ref[s]) for s in range(N_CHIPS)], axis=1)
        ga, gb = _load_gates(gate_refs, nb)
        mixed = (_sigmoid(ga) * ret + _sigmoid(gb) * pool).astype(BF16)
        mx_ref[...] = mixed
        ret_ref[...] = ret.astype(BF16)
        pool_ref[...] = pool.astype(BF16)
        ho_ref[...] = h_ref[...] + _dot(mixed, wout_ref[...].reshape(D, D))

    row = pl.BlockSpec((tm, D), lambda t: (t, 0))
    half = pl.BlockSpec((tm, RET_WIDTH), lambda t: (t, 0))
    up = pl.BlockSpec((N_CHIPS, RET_WIDTH, Dq), lambda t: (0, 0, 0))
    return _call(
        body, name=name, grid=(T // tm,),
        in_specs=[row, half, half] + _gate_specs(tm, D) + [up, up, pl.BlockSpec((N_CHIPS, Dq, D), lambda t: (0, 0, 0))],
        out_specs=[row, row, row, row],
        out_shape=[jax.ShapeDtypeStruct((T, D), F32)] + [jax.ShapeDtypeStruct((T, D), BF16)] * 3,
        sem=("parallel",), operands=(h, r, pm, *([z] * (2 * nb)), wru, wpu, wout), rider=rider)


def _final_loss(h, gain, tgt, name):
    T, D = h.shape
    first = (T - tgt.shape[0]) // CHUNK

    def body(h_ref, g_ref, t_ref, dh_ref, loss_ref, dg_ref):
        i = pl.program_id(0)

        @pl.when(i == 0)
        def _():
            loss_ref[...] = jnp.zeros_like(loss_ref)
            dg_ref[...] = jnp.zeros_like(dg_ref)

        x = h_ref[...]
        gain_v = g_ref[...]
        err = jnp.where(i >= first, _rms_fwd(x, gain_v) - t_ref[...], 0.0)
        loss_ref[...] += 0.5 * jnp.sum(jnp.mean(err * err, axis=-1))
        dx, dgain = _rms_bwd(x, gain_v, err * (1.0 / D))
        dg_ref[...] += dgain
        dh_ref[...] = dx

    return _call(
        body, name=name, grid=(T // CHUNK,),
        in_specs=[pl.BlockSpec((CHUNK, D), lambda i: (i, 0)),
                  pl.BlockSpec((1, D), lambda i: (0, 0)),
                  pl.BlockSpec((CHUNK, D), lambda i: (jnp.maximum(i - first, 0), 0))],
        out_specs=[pl.BlockSpec((CHUNK, D), lambda i: (i, 0)),
                   pl.BlockSpec((1, LANES), lambda i: (0, 0)),
                   pl.BlockSpec((1, D), lambda i: (0, 0))],
        out_shape=[jax.ShapeDtypeStruct((T, D), F32), jax.ShapeDtypeStruct((1, LANES), F32),
                   jax.ShapeDtypeStruct((1, D), F32)],
        sem=("arbitrary",), operands=(h, gain, tgt))


def _ffn_bwd_act(dy, g, u, wd, tm, name, rider=None):
    T, D = dy.shape
    Fs = wd.shape[1]
    F = N_CHIPS * Fs

    def body(dy_ref, go_ref, uo_ref, wd_ref, dg_ref, du_ref, dyh_ref):
        @pl.when(pl.program_id(1) == 0)
        def _():
            dyh_ref[...] = (0.5 * dy_ref[...]).astype(BF16)

        dact = _dot_nt(dyh_ref[...], wd_ref[...])
        du_ref[...] = (dact * uo_ref[...].astype(F32)).astype(BF16)
        dg_ref[...] = (dact * go_ref[...].astype(F32)).astype(BF16)

    row = pl.BlockSpec((tm, D), lambda t, s: (t, 0))
    col = pl.BlockSpec((tm, Fs), lambda t, s: (t, s))
    return _call(
        body, name=name, grid=(T // tm, N_CHIPS),
        in_specs=[row, col, col, pl.BlockSpec((None, Fs, D), lambda t, s: (s, 0, 0))],
        out_specs=[col, col, row],
        out_shape=[jax.ShapeDtypeStruct((T, F), BF16), jax.ShapeDtypeStruct((T, F), BF16),
                   jax.ShapeDtypeStruct((T, D), BF16)],
        sem=("parallel", "arbitrary"), operands=(dy, g, u, wd), rider=rider)


def _ffn_bwd_in(dy, h, gain, dg, du, wg, wu, layer, tm, pad, name, rider=None):
    T, D = h.shape
    Fs = wg.shape[-1]

    def body(dy_ref, h_ref, g_ref, dg_ref, du_ref, wg_ref, wu_ref, dh_ref, dgain_ref, da_ref):
        t = pl.program_id(0)
        s = pl.program_id(1)

        @pl.when((t == 0) & (s == 0))
        def _():
            dgain_ref[...] = jnp.zeros_like(dgain_ref)

        @pl.when(s == 0)
        def _():
            da_ref[...] = jnp.zeros_like(da_ref)

        da_ref[...] += _dot_nt(dg_ref[...], wg_ref[...]) + _dot_nt(du_ref[...], wu_ref[...])

        @pl.when(s == N_CHIPS - 1)
        def _():
            dx, dgain = _rms_bwd(h_ref[...], g_ref[...], da_ref[...])
            dgain_ref[...] += dgain
            dh_ref[...] = jnp.where(_row_mask(t, tm, pad, (tm, D)), dy_ref[...] + dx, 0.0)

    row = pl.BlockSpec((tm, D), lambda t, s: (t, 0))
    col = pl.BlockSpec((tm, Fs), lambda t, s: (t, s))
    wcol = pl.BlockSpec((None, D, Fs), lambda t, s: (s, 0, 0))
    return _call(
        body, name=name, grid=(T // tm, N_CHIPS),
        in_specs=[row, row, pl.BlockSpec((None, 1, D), lambda t, s: (layer, 0, 0)), col, col, wcol, wcol],
        out_specs=[row, pl.BlockSpec((1, D), lambda t, s: (0, 0))],
        out_shape=[jax.ShapeDtypeStruct((T, D), F32), jax.ShapeDtypeStruct((1, D), F32)],
        scratch=[pltpu.VMEM((tm, D), F32)],
        sem=("arbitrary", "arbitrary"), operands=(dy, h, gain, dg, du, wg, wu), rider=rider)


def _grad_tn(a, b, mode, scale, tm, name, rider=None):
    T = a.shape[0]
    if mode == "col":
        per, R, C = 1, a.shape[1], b.shape[1] // N_CHIPS
        a_spec = pl.BlockSpec((tm, R), lambda s, t: (t, 0))
        b_spec = pl.BlockSpec((tm, C), lambda s, t: (t, s))
    else:
        per, R, C = 2, a.shape[1] // N_CHIPS, b.shape[1]
        a_spec = pl.BlockSpec((tm, per * R), lambda s, t: (t, s))
        b_spec = pl.BlockSpec((tm, C), lambda s, t: (t, 0))
    nt = T // tm

    def body(a_ref, b_ref, o_ref, acc_ref):
        t = pl.program_id(1)

        @pl.when(t == 0)
        def _():
            acc_ref[...] = jnp.zeros_like(acc_ref)

        acc_ref[...] += _dot_tn(a_ref[...].astype(BF16), b_ref[...].astype(BF16))

        @pl.when(t == nt - 1)
        def _():
            o_ref[...] = (scale * acc_ref[...]).astype(BF16).reshape(per, R, C)

    return _call(
        body, name=name, grid=(N_CHIPS // per, nt),
        in_specs=[a_spec, b_spec],
        out_specs=[pl.BlockSpec((per, R, C), lambda s, t: (s, 0, 0))],
        out_shape=[jax.ShapeDtypeStruct((N_CHIPS, R, C), BF16)],
        scratch=[pltpu.VMEM((per * R, C), F32)],
        sem=("parallel", "arbitrary"), operands=(a, b), rider=rider)[0]


def _grad_mix(mixed, dh, r, dret, pm, dpool, tk, name, rider=None):
    T, D = dh.shape
    Dq = D // N_CHIPS
    nt = T // tk

    def body(mx_ref, dh_ref, r_ref, dret_ref, pm_ref, dpool_ref, go_ref, gr_ref, gp_ref, ao_ref, ar_ref, ap_ref):
        t = pl.program_id(0)

        @pl.when(t == 0)
        def _():
            ao_ref[...] = jnp.zeros_like(ao_ref)
            ar_ref[...] = jnp.zeros_like(ar_ref)
            ap_ref[...] = jnp.zeros_like(ap_ref)

        ao_ref[...] += _dot_tn(mx_ref[...], dh_ref[...].astype(BF16))
        ar_ref[...] += _dot_tn(r_ref[...], dret_ref[...])
        ap_ref[...] += _dot_tn(pm_ref[...], dpool_ref[...])

        @pl.when(t == nt - 1)
        def _():
            go_ref[...] = ao_ref[...].astype(BF16).reshape(N_CHIPS, Dq, D)
            for s in range(N_CHIPS):
                gr_ref[s] = ar_ref[:, s * Dq:(s + 1) * Dq].astype(BF16)
                gp_ref[s] = ap_ref[:, s * Dq:(s + 1) * Dq].astype(BF16)

    row = pl.BlockSpec((tk, D), lambda t: (t, 0))
    half = pl.BlockSpec((tk, RET_WIDTH), lambda t: (t, 0))
    whole = lambda shape: pl.BlockSpec(shape, lambda t: (0, 0, 0))
    return _call(
        body, name=name, grid=(nt,),
        in_specs=[row, row, half, row, half, row],
        out_specs=[whole((N_CHIPS, Dq, D)), whole((N_CHIPS, RET_WIDTH, Dq)), whole((N_CHIPS, POOL_WIDTH, Dq))],
        out_shape=[jax.ShapeDtypeStruct((N_CHIPS, Dq, D), BF16),
                   jax.ShapeDtypeStruct((N_CHIPS, RET_WIDTH, Dq), BF16),
                   jax.ShapeDtypeStruct((N_CHIPS, POOL_WIDTH, Dq), BF16)],
        scratch=[pltpu.VMEM((D, D), F32), pltpu.VMEM((RET_WIDTH, D), F32), pltpu.VMEM((POOL_WIDTH, D), F32)],
        sem=("arbitrary",), operands=(mixed, dh, r, dret, pm, dpool), rider=rider)


def _mix_bwd_dx(dh, z, ret, pool, wout, wru, wpu, tm, name, rider=None):
    T, D = dh.shape
    Dq = D // N_CHIPS
    nb = D // RET_WIDTH

    def body(*refs):
        dh_ref = refs[0]
        gate_refs = refs[1:1 + 2 * nb]
        ret_ref, pool_ref, wout_ref, wru_ref, wpu_ref, dgab_ref, dret_ref, dpool_ref, dr_ref, dpm_ref = refs[1 + 2 * nb:]
        dmixed = _dot_nt(dh_ref[...].astype(BF16), wout_ref[...].reshape(D, D))
        ga, gb = _load_gates(gate_refs, nb)
        sa = _sigmoid(ga)
        sb = _sigmoid(gb)
        dgab_ref[:, :D] = (dmixed * ret_ref[...].astype(F32) * (sa * (1.0 - sa))).astype(BF16)
        dgab_ref[:, D:] = (dmixed * pool_ref[...].astype(F32) * (sb * (1.0 - sb))).astype(BF16)
        dret = (dmixed * sa).astype(BF16)
        dpool = (dmixed * sb).astype(BF16)
        dret_ref[...] = dret
        dpool_ref[...] = dpool
        dr = _dot_nt(dret[:, :Dq], wru_ref[0])
        dpm = _dot_nt(dpool[:, :Dq], wpu_ref[0])
        for s in range(1, N_CHIPS):
            dr += _dot_nt(dret[:, s * Dq:(s + 1) * Dq], wru_ref[s])
            dpm += _dot_nt(dpool[:, s * Dq:(s + 1) * Dq], wpu_ref[s])
        dr_ref[...] = dr
        dpm_ref[...] = dpm

    row = pl.BlockSpec((tm, D), lambda t: (t, 0))
    half = pl.BlockSpec((tm, RET_WIDTH), lambda t: (t, 0))
    up = pl.BlockSpec((N_CHIPS, RET_WIDTH, Dq), lambda t: (0, 0, 0))
    return _call(
        body, name=name, grid=(T // tm,),
        in_specs=[row] + _gate_specs(tm, D) + [row, row, pl.BlockSpec((N_CHIPS, Dq, D), lambda t: (0, 0, 0)), up, up],
        out_specs=[pl.BlockSpec((tm, 2 * D), lambda t: (t, 0)), row, row, half, half],
        out_shape=[jax.ShapeDtypeStruct((T, 2 * D), BF16), jax.ShapeDtypeStruct((T, D), BF16),
                   jax.ShapeDtypeStruct((T, D), BF16), jax.ShapeDtypeStruct((T, RET_WIDTH), F32),
                   jax.ShapeDtypeStruct((T, POOL_WIDTH), F32)],
        sem=("parallel",), operands=(dh, *([z] * (2 * nb)), ret, pool, wout, wru, wpu), rider=rider)


def _pool_bwd(z, dpm, maps, scale, layer, pad, name):
    T = z.shape[0]

    def body(zu, maps_ref, sc_ref, dpm_ref, du_ref, dmaps_ref, dsc_ref):
        g = pl.program_id(0)
        u = zu[...].astype(F32)
        pooled, div, valid = _pool_parts(u, g, T, pad)
        pb = pooled.astype(BF16)
        mb = maps_ref[...].astype(BF16)
        dp = dpm_ref[...]
        dsc_ref[...] = jnp.sum(dp * _dot(pb, mb), axis=0, keepdims=True)
        dyb = (dp * sc_ref[...]).astype(BF16)
        dmaps_ref[...] = _dot_tn(pb, dyb)
        dpooled = jnp.where(valid, _dot_nt(dyb, mb), 0.0)
        ahead = _select_group(_window_sums(dpooled / div, lambda k: T - k), g)
        du_ref[...] = jnp.where(valid, ahead - dpooled, 0.0).astype(BF16)

    blk = pl.BlockSpec((T, HEAD_DIM), lambda g: (0, g))
    return _call(
        body, name=name, grid=(POOL_GROUPS,),
        in_specs=_pool_specs(T, layer) + [blk],
        out_specs=[blk, pl.BlockSpec((None, HEAD_DIM, HEAD_DIM), lambda g: (g, 0, 0)),
                   pl.BlockSpec((1, HEAD_DIM), lambda g: (0, g))],
        out_shape=[jax.ShapeDtypeStruct((T, POOL_WIDTH), BF16),
                   jax.ShapeDtypeStruct((POOL_GROUPS, HEAD_DIM, HEAD_DIM), F32),
                   jax.ShapeDtypeStruct((1, POOL_WIDTH), F32)],
        sem=("parallel",), operands=(z, maps, scale, dpm))


def _ret_bwd_local(z, o_pre, s_all, dr, consts, cg, name):
    T = z.shape[0]
    N = T // CHUNK
    ng = N // cg
    tg = cg * CHUNK
    cosf, sinf, intra, _, qdec, _ = consts
    fwd = lambda g: g

    def body(zq, zk, zv, zg, o_ref, s_ref, dr_ref, cos_ref, sin_ref, m_ref, qd_ref,
             dq_ref, dg_ref, dk_ref, dv_ref, ds_ref):
        cosv = cos_ref[...]
        sinv = sin_ref[...]
        scale = HEAD_DIM ** -0.5
        q3 = (_rot(zq[...].astype(F32), cosv, sinv) * scale).reshape(cg, CHUNK, HEAD_DIM)
        k3 = _rot(zk[...].astype(F32), cosv, sinv).reshape(cg, CHUNK, HEAD_DIM)
        qb = q3.astype(BF16)
        kb = k3.astype(BF16)
        vb = zv[...].reshape(cg, CHUNK, HEAD_DIM).astype(BF16)
        mask = m_ref[...][None]
        sb = (_ein("ncd,nmd->ncm", qb, kb) * mask).astype(BF16)
        qdv = qd_ref[...][None]
        qdb = (q3 * qdv).astype(BF16)

        out = o_ref[...]
        xc = out - jnp.mean(out, axis=-1, keepdims=True)
        rstd = lax.rsqrt(jnp.mean(xc * xc, axis=-1, keepdims=True) + EPS)
        rn = xc * rstd
        g = zg[...].astype(F32)
        sg = _sigmoid(g)
        drv = dr_ref[...]
        dg_ref[...] = (drv * rn * (sg * (1.0 + g * (1.0 - sg)))).astype(BF16)
        drn = drv * (g * sg)
        dout = rstd * (drn - jnp.mean(drn, axis=-1, keepdims=True)
                       - rn * jnp.mean(drn * rn, axis=-1, keepdims=True))
        dob = dout.reshape(cg, CHUNK, HEAD_DIM).astype(BF16)

        dsb = (_ein("ncd,nmd->ncm", dob, vb) * mask).astype(BF16)
        dv_ref[...] = _ein("ncm,ncd->nmd", sb, dob).reshape(tg, HEAD_DIM)
        dk_ref[...] = _ein("ncm,ncd->nmd", dsb, qb).reshape(tg, HEAD_DIM)
        dq3 = _ein("ncm,nmd->ncd", dsb, kb) + _ein("nce,nde->ncd", dob, s_ref[...].astype(BF16)) * qdv
        dq_ref[...] = _rot_t(dq3.reshape(tg, HEAD_DIM) * scale, cosv, sinv).astype(BF16)
        ds_ref[...] = _ein("ncd,nce->nde", qdb, dob)

    tab = pl.BlockSpec((tg, HEAD_DIM), lambda h, g: (g, 0))
    per_head = pl.BlockSpec((None, CHUNK, HEAD_DIM), lambda h, g: (h, 0, 0))
    head_blk = pl.BlockSpec((tg, HEAD_DIM), lambda h, g: (g, h))
    state_blk = pl.BlockSpec((None, cg, HEAD_DIM, HEAD_DIM), lambda h, g: (h, g, 0, 0))
    return _call(
        body, name=name, grid=(RET_HEADS, ng),
        in_specs=[_head_specs(tg, i, fwd) for i in range(4)]
        + [head_blk, state_blk, head_blk, tab, tab, per_head, per_head],
        out_specs=[head_blk, head_blk, head_blk, head_blk, state_blk],
        out_shape=[jax.ShapeDtypeStruct((T, RET_WIDTH), BF16), jax.ShapeDtypeStruct((T, RET_WIDTH), BF16),
                   jax.ShapeDtypeStruct((T, RET_WIDTH), F32), jax.ShapeDtypeStruct((T, RET_WIDTH), F32),
                   jax.ShapeDtypeStruct((RET_HEADS, N, HEAD_DIM, HEAD_DIM), F32)],
        sem=("parallel", "parallel"), operands=(z, z, z, z, o_pre, s_all, dr, cosf, sinf, intra, qdec))


def _ret_bwd_state(z, dkp, dvp, ds, consts, cg, name):
    T = z.shape[0]
    N = T // CHUNK
    ng = N // cg
    tg = cg * CHUNK
    cosf, sinf, _, kdec, _, cdb = consts
    rev = lambda g: ng - 1 - g

    def body(zk, zv, dkp_ref, dvp_ref, ds_ref, cos_ref, sin_ref, kd_ref, cd_ref, dk_ref, dv_ref, gs_ref, dkv_ref):
        @pl.when(pl.program_id(1) == 0)
        def _():
            gs_ref[...] = jnp.zeros_like(gs_ref)

        cosv = cos_ref[...]
        sinv = sin_ref[...]
        cd = cd_ref[0:1, :]
        grad = gs_ref[...]
        for n in reversed(range(cg)):
            dkv_ref[n] = grad
            grad = ds_ref[n] + cd * grad
        gs_ref[...] = grad
        dkvb = dkv_ref[...].astype(BF16)
        kdv = kd_ref[...][None]
        k3 = _rot(zk[...].astype(F32), cosv, sinv).reshape(cg, CHUNK, HEAD_DIM)
        vb = zv[...].reshape(cg, CHUNK, HEAD_DIM).astype(BF16)
        dk3 = _ein("nce,nde->ncd", vb, dkvb) * kdv
        dv3 = _ein("ncd,nde->nce", (k3 * kdv).astype(BF16), dkvb)
        dk_ref[...] = _rot_t(dkp_ref[...] + dk3.reshape(tg, HEAD_DIM), cosv, sinv).astype(BF16)
        dv_ref[...] = (dvp_ref[...] + dv3.reshape(tg, HEAD_DIM)).astype(BF16)

    tab = pl.BlockSpec((tg, HEAD_DIM), lambda h, g: (rev(g), 0))
    head_blk = pl.BlockSpec((tg, HEAD_DIM), lambda h, g: (rev(g), h))
    return _call(
        body, name=name, grid=(RET_HEADS, ng),
        in_specs=[_head_specs(tg, 1, rev), _head_specs(tg, 2, rev), head_blk, head_blk,
                  pl.BlockSpec((None, cg, HEAD_DIM, HEAD_DIM), lambda h, g: (h, rev(g), 0, 0)),
                  tab, tab,
                  pl.BlockSpec((None, CHUNK, HEAD_DIM), lambda h, g: (h, 0, 0)),
                  pl.BlockSpec((None, 8, HEAD_DIM), lambda h, g: (h, 0, 0))],
        out_specs=[head_blk, head_blk],
        out_shape=[jax.ShapeDtypeStruct((T, RET_WIDTH), BF16)] * 2,
        scratch=[pltpu.VMEM((HEAD_DIM, HEAD_DIM), F32), pltpu.VMEM((cg, HEAD_DIM, HEAD_DIM), F32)],
        sem=("parallel", "arbitrary"), operands=(z, z, dkp, dvp, ds, cosf, sinf, kdec, cdb))


def _z_segments(pieces, ns):
    segs, at = [], 0
    for k, p in enumerate(pieces):
        width = p.shape[1]
        lo = at
        while lo < at + width:
            s = lo // ns
            hi = min(at + width, (s + 1) * ns)
            segs.append((k, lo - at, hi - at, s, lo - s * ns, hi - s * ns))
            lo = hi
        at += width
    assert at == N_CHIPS * ns and all(v % LANES == 0 for seg in segs for v in (seg[1], seg[2], seg[4], seg[5]))
    return segs


def _inproj_bwd_dx(pieces, win, h, gain, dh_in, layer, tm, pad, name, rider=None):
    T, D = h.shape
    Ns = win.shape[-1]
    n = len(pieces)
    segs = _z_segments(pieces, Ns)

    def body(*refs):
        piece_refs = refs[:n]
        w_ref, h_ref, g_ref, dhi_ref, dh_ref, dgain_ref = refs[n:]
        t = pl.program_id(0)

        @pl.when(t == 0)
        def _():
            dgain_ref[...] = jnp.zeros_like(dgain_ref)

        db = None
        for k, a, b, s, c, d in segs:
            term = _dot_nt(piece_refs[k][:, a:b], w_ref[s, :, c:d])
            db = term if db is None else db + term
        dx, dgain = _rms_bwd(h_ref[...], g_ref[...], db)
        dgain_ref[...] += dgain
        dh_ref[...] = jnp.where(_row_mask(t, tm, pad, (tm, D)), dhi_ref[...] + dx, 0.0)

    row = pl.BlockSpec((tm, D), lambda t: (t, 0))
    return _call(
        body, name=name, grid=(T // tm,),
        in_specs=[pl.BlockSpec((tm, p.shape[1]), lambda t: (t, 0)) for p in pieces]
        + [pl.BlockSpec((N_CHIPS, D, Ns), lambda t: (0, 0, 0)), row,
           pl.BlockSpec((None, 1, D), lambda t: (layer, 0, 0)), row],
        out_specs=[row, pl.BlockSpec((1, D), lambda t: (0, 0))],
        out_shape=[jax.ShapeDtypeStruct((T, D), F32), jax.ShapeDtypeStruct((1, D), F32)],
        sem=("arbitrary",), operands=(*pieces, win, h, gain, dh_in), rider=rider)


def _grad_w_in(b, pieces, ns, tk, name, rider=None):
    T, D = b.shape
    n = len(pieces)
    nt = T // tk
    segs = _z_segments(pieces, ns)
    shards_of = [sorted({s for k, _, _, s, _, _ in segs if k == i}) for i in range(n)]

    def body(*refs):
        b_ref = refs[0]
        piece_refs = refs[1:1 + n]
        o_ref, acc_ref = refs[1 + n:]
        s = pl.program_id(0)
        t = pl.program_id(1)

        @pl.when(t == 0)
        def _():
            acc_ref[...] = jnp.zeros_like(acc_ref)

        for shard in range(N_CHIPS):
            @pl.when(s == shard)
            def _(shard=shard):
                cols = [piece_refs[k][:, a:e] for k, a, e, ss, _, _ in segs if ss == shard]
                dz = cols[0] if len(cols) == 1 else jnp.concatenate(cols, axis=1)
                acc_ref[...] += _dot_tn(b_ref[...], dz)

        @pl.when(t == nt - 1)
        def _():
            o_ref[...] = acc_ref[...].astype(BF16).reshape(1, D, ns)

    def piece_spec(i):
        def index(s, t):
            used = functools.reduce(jnp.logical_or, [s == ss for ss in shards_of[i]])
            return (jnp.where(used, t, 0), 0)
        return pl.BlockSpec((tk, pieces[i].shape[1]), index)

    return _call(
        body, name=name, grid=(N_CHIPS, nt),
        in_specs=[pl.BlockSpec((tk, D), lambda s, t: (t, 0))] + [piece_spec(i) for i in range(n)],
        out_specs=[pl.BlockSpec((1, D, ns), lambda s, t: (s, 0, 0))],
        out_shape=[jax.ShapeDtypeStruct((N_CHIPS, D, ns), BF16)],
        scratch=[pltpu.VMEM((D, ns), F32)],
        sem=("parallel", "arbitrary"), operands=(b, *pieces), rider=rider)[0]


def _sum_pair(gs, rs, c_idx, name):
    n = len(gs)

    def body(c_ref, *refs):
        for g_ref, r_ref, o_ref in zip(refs[:n], refs[n:2 * n], refs[2 * n:]):
            o_ref[...] = (g_ref[...].astype(F32) + r_ref[...].astype(F32)).astype(BF16)

    halves = [pl.BlockSpec((None,) + r.shape[1:], lambda s, c_ref: (s, 0, 0)) for r in rs]
    return pl.pallas_call(
        body,
        name=name,
        grid_spec=pltpu.PrefetchScalarGridSpec(
            num_scalar_prefetch=1,
            grid=(N_CHIPS,),
            in_specs=[pl.BlockSpec((None,) + r.shape[1:], lambda s, c_ref: (s, c_ref[0], 0)) for r in rs] + halves,
            out_specs=halves,
        ),
        out_shape=[jax.ShapeDtypeStruct(r.shape, BF16) for r in rs],
        compiler_params=_params(("parallel",)),
    )(c_idx, *gs, *rs)


def _sum_chips(ps, rs, pos, name):
    n = len(ps)
    quarters = 4

    def body(pos_ref, *refs):
        chip = pos_ref[0]
        for p_ref, r_ref, o_ref in zip(refs[:n], refs[n:2 * n], refs[2 * n:]):
            own = p_ref[...].astype(F32)
            terms = [jnp.where(chip == k, own, r_ref[k].astype(F32)) for k in range(N_CHIPS)]
            o_ref[...] = ((terms[0] + terms[1]) + terms[2]) + terms[3]

    def rows(r):
        assert r.shape[1] % (quarters * BF16_ROWS) == 0, r.shape
        return r.shape[1] // quarters

    return pl.pallas_call(
        body,
        name=name,
        grid_spec=pltpu.PrefetchScalarGridSpec(
            num_scalar_prefetch=1,
            grid=(quarters,),
            in_specs=[pl.BlockSpec((None, rows(r), r.shape[2]), lambda q, pos_ref: (pos_ref[0], q, 0)) for r in rs]
            + [pl.BlockSpec((N_CHIPS, rows(r), r.shape[2]), lambda q, pos_ref: (0, q, 0)) for r in rs],
            out_specs=[pl.BlockSpec((rows(r), r.shape[2]), lambda q, pos_ref: (pos_ref[1] * quarters + q, 0))
                       for r in rs],
        ),
        out_shape=[jax.ShapeDtypeStruct((2 * r.shape[1], r.shape[2]), F32) for r in rs],
        compiler_params=_params(("arbitrary",)),
    )(pos, *ps, *rs)


def _small_all_reduce(p, rider=None):
    rows, width = p.shape
    r = 0 if rider is None else len(rider.ins)

    def body(*refs):
        p_ref, o_ref = refs[0], refs[1 + r]
        sib_ref, slot_ref, ssem, rsem = refs[2 + 2 * r:6 + 2 * r]
        if rider is not None:
            rider.start(refs[1:1 + r], refs[2 + r:2 + 2 * r], refs[6 + 2 * r], refs[7 + 2 * r])
        reduce(p_ref, o_ref, sib_ref, slot_ref, ssem, rsem)
        if rider is not None:
            rider.finish(refs[1:1 + r], refs[2 + r:2 + 2 * r], refs[6 + 2 * r], refs[7 + 2 * r])

    def reduce(p_ref, o_ref, sib_ref, slot_ref, ssem, rsem):
        x, y, c, chip, others = _mesh_pos()
        pair = _remote(p_ref, sib_ref, ssem.at[0], rsem.at[0], (x, y, 1 - c))
        pair.start()
        pair.wait()
        slot_ref[chip] = p_ref[...] + sib_ref[...]
        sends = []
        for j, (ox, oy) in enumerate(others):
            cp = _remote(slot_ref.at[chip], slot_ref.at[chip], ssem.at[1 + j], rsem.at[1 + j], (ox, oy, c))
            cp.start()
            sends.append(cp)
        for j, (ox, oy) in enumerate(others):
            slot = slot_ref.at[2 * ox + oy]
            _remote(slot, slot, ssem.at[1 + j], rsem.at[1 + j], (ox, oy, c)).wait_recv()
        for cp in sends:
            cp.wait_send()
        o_ref[...] = ((slot_ref[0] + slot_ref[1]) + slot_ref[2]) + slot_ref[3]

    vmem = pl.BlockSpec(memory_space=pltpu.VMEM)
    scratch = [pltpu.VMEM((rows, width), F32), pltpu.VMEM((N_CHIPS, rows, width), F32),
               pltpu.SemaphoreType.DMA((4,)), pltpu.SemaphoreType.DMA((4,))]
    if rider is not None:
        scratch += [pltpu.SemaphoreType.DMA((rider.n_sem,)), pltpu.SemaphoreType.DMA((rider.n_sem,))]
    outs = pl.pallas_call(
        body,
        name="small_grads_all_reduce",
        in_specs=[vmem] + [ANY] * r,
        out_specs=[vmem] + [ANY] * r,
        out_shape=[jax.ShapeDtypeStruct(p.shape, F32)] + ([] if rider is None else list(rider.out_shapes)),
        input_output_aliases={} if rider is None else rider.aliases(1, 1),
        scratch_shapes=scratch,
    )(p, *([] if rider is None else rider.ins))
    if rider is not None:
        rider.results = outs[1:]
    return outs[0]


def _adamw(gs, w, m, v, name):
    L, R, C = w.shape
    Ct = gs[0].shape[1]
    tr = _pick_tile(R, 256, 8)

    def body(*refs):
        g_refs = refs[:L]
        w_ref, m_ref, v_ref, go_ref, d_ref, mo_ref, vo_ref = refs[L:]
        layer = pl.program_id(0)
        grad = g_refs[L - 1][...]
        for i in range(L - 2, -1, -1):
            grad = jnp.where(layer == i, g_refs[i][...], grad)
        if Ct != C:
            grad = grad[:, :C]
        m_new = ADAM_B1 * m_ref[...] + (1.0 - ADAM_B1) * grad
        v_new = ADAM_B2 * v_ref[...] + (1.0 - ADAM_B2) * jnp.square(grad)
        m_hat = m_new / (1.0 - ADAM_B1 ** ADAM_STEP)
        v_hat = v_new / (1.0 - ADAM_B2 ** ADAM_STEP)
        go_ref[...] = grad
        d_ref[...] = -ADAM_LR * (m_hat / (jnp.sqrt(v_hat) + ADAM_EPS) + ADAM_WD * w_ref[...])
        mo_ref[...] = m_new
        vo_ref[...] = v_new

    g_specs = [pl.BlockSpec((tr, Ct), functools.partial(lambda l, r, i: (jnp.where(l == i, r, 0), 0), i=i))
               for i in range(L)]
    blk = pl.BlockSpec((None, tr, C), lambda l, r: (l, r, 0))
    return _call(
        body, name=name, grid=(L, R // tr),
        in_specs=g_specs + [blk, blk, blk],
        out_specs=[blk] * 4,
        out_shape=[jax.ShapeDtypeStruct((L, R, C), F32)] * 4,
        sem=("arbitrary", "arbitrary"), operands=(*gs, w, m, v))


_FFN1 = ("ffn1_gate", "ffn1_up", "ffn1_down")
_FFN2 = ("ffn2_gate", "ffn2_up", "ffn2_down")
_MIXW = ("w_ret_up", "w_pool_up", "w_out")
_BIG = _FFN1 + ("w_in",) + _MIXW + _FFN2
_TRANSPOSED = ("ffn1_gate", "ffn1_up", "ffn2_gate", "ffn2_up")
_SMALL = ("ffn1_norm", "mix_norm", "ffn2_norm", "final_norm", "pool_scale", "pool_maps")
_ORDER = ("meta", "ffn1_norm", "ffn1_gate", "ffn1_up", "ffn1_down", "mix_norm", "w_in", "pool_maps",
          "pool_scale", "w_ret_up", "w_pool_up", "w_out", "ffn2_norm", "ffn2_gate", "ffn2_up", "ffn2_down",
          "final_norm")


def _transport(a):
    n, r, c = a.shape
    out = a.astype(BF16)
    if c % LANES:
        out = jnp.concatenate([out, jnp.zeros((n, r, _round_up(c, LANES) - c), BF16)], axis=2)
    if r % LANES:
        out = jnp.concatenate([out, jnp.zeros((n, _round_up(r, LANES) - r, out.shape[2]), BF16)], axis=1)
    return out


def _pack_rows(parts, width):
    rows = [p.reshape(-1, width) for p in parts]
    total = sum(r.shape[0] for r in rows)
    fill = _round_up(total, 8) - total
    if fill:
        rows.append(jnp.zeros((fill, width), F32))
    return jnp.concatenate(rows, axis=0)


def _unpack_rows(packed, shapes, width):
    out, at = [], 0
    for shp in shapes:
        n = math.prod(shp) // width
        out.append(packed[at:at + n].reshape(shp))
        at += n
    return out


class _Weights:
    def __init__(self, shards):
        self.shards = shards
        self.full = {}

    def rider(self, keys):
        r = _gather_rider([(self.shards[n], i) for n, i in keys])
        r.keys = keys
        return r

    def take(self, rider):
        for key, arr in zip(rider.keys, rider.results):
            self.full[key] = arr

    def __call__(self, name, layer):
        return self.full[(name, layer)]


def _local_step(x, meta_full, tgt, w, wts, pad, tm, cg, reducer):
    D = x.shape[1]
    T = pad + N_META + x.shape[0]
    L = w["ffn1_norm"].shape[0]
    pool_maps = w["pool_maps"]
    gains = {n: w[n].reshape(L, 1, D) for n in ("ffn1_norm", "mix_norm", "ffn2_norm")}
    scale3 = w["pool_scale"].reshape(L, 1, POOL_WIDTH)
    consts = _ret_consts(T, pad)
    tl = _pick_tile(T, 2 * tm, BF16_ROWS)
    def gather(keys):
        return wts.rider(keys) if keys and keys[0] not in wts.full else None

    def done(rider):
        if rider is not None:
            wts.take(rider)

    h = jnp.concatenate([jnp.zeros((pad, D), F32), meta_full, x], axis=0)
    saved = []
    for i in range(L):
        s = {"h0": h}
        if ("ffn1_down", i) in wts.full:
            rd = gather([("w_in", i)] + [(n, i) for n in _MIXW])
            h, s["a1"], s["g1"], s["u1"], s["act1"] = _ffn_fwd(
                h, gains["ffn1_norm"], wts("ffn1_gate", i), wts("ffn1_up", i), wts("ffn1_down", i), i, tl,
                f"ffn1_fwd_{i}", rd)
            done(rd)
        else:
            rd = gather([("ffn1_down", i), ("w_in", i)])
            s["a1"], s["g1"], s["u1"], s["act1"] = _ffn_fwd_up(
                h, gains["ffn1_norm"], wts("ffn1_gate", i), wts("ffn1_up", i), i, tl, f"ffn1_fwd_up_{i}", rd)
            done(rd)
            rd = gather([(n, i) for n in _MIXW])
            h = _ffn_fwd_down(h, s["act1"], wts("ffn1_down", i), tl, f"ffn1_fwd_down_{i}", rd)
            done(rd)
        s["h1"] = h
        rd = gather([("ffn2_gate", i), ("ffn2_up", i)])
        s["z"], s["b"] = _inproj_fwd(h, gains["mix_norm"], wts("w_in", i), i, tl, f"inproj_fwd_{i}", rd)
        done(rd)
        s["r"], s["o_pre"], s["s_all"] = _ret_fwd(s["z"], consts, cg, f"retention_fwd_{i}")
        s["pm"] = _pool_fwd(s["z"], pool_maps, scale3, i, pad, f"pool_fwd_{i}")
        rd = gather([("ffn2_down", i)])
        h, s["mixed"], s["ret"], s["pool"] = _mix_fwd(
            h, s["r"], s["pm"], s["z"], wts("w_ret_up", i), wts("w_pool_up", i), wts("w_out", i), tl,
            f"mix_fwd_{i}", rd)
        done(rd)
        s["h2"] = h
        rd = gather([(n, i + 1) for n in _FFN1]) if i + 1 < L else None
        h, s["a2"], s["g2"], s["u2"], s["act2"] = _ffn_fwd(
            h, gains["ffn2_norm"], wts("ffn2_gate", i), wts("ffn2_up", i), wts("ffn2_down", i), i, tl,
            f"ffn2_fwd_{i}", rd)
        done(rd)
        saved.append(s)

    dh, loss_acc, d_final = _final_loss(h, w["final_norm"].reshape(1, D), tgt, "final_norm_loss")

    small = {n: [None] * L for n in ("ffn1_norm", "mix_norm", "ffn2_norm", "pool_scale", "pool_maps")}

    carry = {"ffn_act": 1.0, "ffn_in": 2.2, "mix_bwd": 1.0, "inproj_bwd": 1.5, "w_in": 1.0}

    tk = _pick_tile(T, 1408, LANES)

    def grad(n, a, b, i, mode):
        rd = reducer.rider(carry.get(n, 1.0 if i == 0 and n.startswith("ffn") else 0.5))
        reducer.add(n, i, _grad_tn(a, b, mode, 1.0, tk, f"grad_{n}_{i}", rd))
        reducer.done(rd)

    def ffn_bwd(which, dy, h_in, g, u, i, between=None, units=carry["ffn_in"]):
        rd = reducer.rider(carry["ffn_act"])
        dg, du, dyh = _ffn_bwd_act(dy, g, u, wts(f"{which}_down", i), tl, f"{which}_bwd_act_{i}", rd)
        reducer.done(rd)
        if between is not None:
            between(dg, du, dyh)
        rd = reducer.rider(units)
        dh_in, dgain = _ffn_bwd_in(dy, h_in, gains[f"{which}_norm"], dg, du, wts(f"{which}_gate", i),
                                   wts(f"{which}_up", i), i, tl, pad, f"{which}_bwd_in_{i}", rd)
        reducer.done(rd)
        return dh_in, dg, du, dgain, dyh

    for i in reversed(range(L)):
        s = saved[i]
        dh, dg, du, small["ffn2_norm"][i], dyh = ffn_bwd("ffn2", dh, s["h2"], s["g2"], s["u2"], i)
        grad("ffn2_gate", dg, s["a2"], i, "row")
        grad("ffn2_up", du, s["a2"], i, "row")
        grad("ffn2_down", s["act2"], dyh, i, "row")
        reducer.stage(f"ffn2_{i}")
        rd = reducer.rider(carry["mix_bwd"])
        dgab, dret, dpool, dr, dpm = _mix_bwd_dx(
            dh, s["z"], s["ret"], s["pool"], wts("w_out", i), wts("w_ret_up", i), wts("w_pool_up", i), tm,
            f"mix_bwd_{i}", rd)
        reducer.done(rd)
        rd = reducer.rider(0.5)
        g_out, g_ru, g_pu = _grad_mix(s["mixed"], dh, s["r"], dret, s["pm"], dpool, _pick_tile(T, 704, LANES),
                                      f"grad_mix_{i}", rd)
        reducer.done(rd)
        for n, g_n in (("w_out", g_out), ("w_ret_up", g_ru), ("w_pool_up", g_pu)):
            reducer.add(n, i, g_n)
        du_pool, small["pool_maps"][i], small["pool_scale"][i] = _pool_bwd(
            s["z"], dpm, pool_maps, scale3, i, pad, f"pool_bwd_{i}")
        dq, dgr, dkp, dvp, ds = _ret_bwd_local(s["z"], s["o_pre"], s["s_all"], dr, consts,
                                               _pick_tile(T // CHUNK, 11, 1), f"retention_bwd_{i}")
        dk, dv = _ret_bwd_state(s["z"], dkp, dvp, ds, consts, cg, f"retention_bwd_state_{i}")
        dz = [dq, dk, dv, dgr, du_pool, dgab]
        dh2 = dh
        rd = reducer.rider(carry["inproj_bwd"])
        dh, small["mix_norm"][i] = _inproj_bwd_dx(
            dz, wts("w_in", i), s["h1"], gains["mix_norm"], dh2, i, tm, pad, f"inproj_bwd_{i}", rd)
        reducer.done(rd)
        rd = reducer.rider(carry["w_in"])
        reducer.add("w_in", i, _grad_w_in(s["b"], dz, wts("w_in", i).shape[-1], tk, f"grad_w_in_{i}", rd))
        reducer.done(rd)
        reducer.stage(f"mid{i}")
        def ffn1_grads(dg, du, dyh, i=i, s=s):
            grad("ffn1_gate", dg, s["a1"], i, "row")
            if i == 0:
                reducer.stage("gate0")
            grad("ffn1_up", du, s["a1"], i, "row")
            if i == 0:
                reducer.stage("up0")
            grad("ffn1_down", s["act1"], dyh, i, "row")
            reducer.stage(f"end{i}")

        if i == 0:
            dh, _, _, small["ffn1_norm"][i], _ = ffn_bwd("ffn1", dh, s["h0"], s["g1"], s["u1"], i, ffn1_grads, 2.5)
        else:
            dh, dg, du, small["ffn1_norm"][i], dyh = ffn_bwd("ffn1", dh, s["h0"], s["g1"], s["u1"], i)
            ffn1_grads(dg, du, dyh)

    return loss_acc, dh, small, d_final


class _Reducer:
    def __init__(self, unit):
        self.c_idx = lax.axis_index("c").astype(jnp.int32).reshape(1)
        chip = 2 * lax.axis_index("x") + lax.axis_index("y")
        self.pos = jnp.stack([chip, lax.axis_index("c")]).astype(jnp.int32)
        self.pending, self.stages, self.queue, self.halves, self.whole = [], [], [], {}, {}
        self.unit = unit
        self.calls = 0

    def add(self, name, layer, g):
        self.pending.append(((name, layer), g))

    def stage(self, tag):
        if self.pending:
            self.stages.append((tag, self.pending))
            self.pending = []

    def _pair_rider(self):
        if not self.stages:
            return None
        tag, items = self.stages.pop(0)
        rd = _pair_exchange_rider([g for _, g in items])
        rd.tag, rd.keys = tag, [k for k, _ in items]
        return rd

    def _chip_rider(self, units):
        take, keep, size = [], [], 0
        for item in self.queue:
            if units is None or size + item[1].size <= units * self.unit:
                take.append(item)
                size += item[1].size
            else:
                keep.append(item)
        self.queue = keep
        if not take:
            return None
        rd = _chip_exchange_rider([p for _, p in take])
        rd.keys = [k for k, _ in take]
        return rd

    def _gather_rider(self):
        keys = [k for k in self.halves if k not in self.whole]
        if not keys:
            return None
        rd = _pair_gather_rider([self.halves[k] for k in keys])
        rd.keys = keys
        return rd

    def rider(self, units):
        self.riding = (self._pair_rider(), self._chip_rider(units), self._gather_rider())
        return _join(self.riding)

    def done(self, rd):
        if rd is None:
            return
        _split_results(rd)
        pair, chips, gather = self.riding
        if len([r for r in self.riding if r is not None]) == 1:
            (pair or chips or gather).results = rd.results
        self.calls += 1
        if gather is not None:
            self.whole.update(zip(gather.keys, gather.results))
        if pair is not None:
            sums = _sum_pair(pair.ins, pair.results, self.c_idx, f"sum_pair_{pair.tag}")
            self.queue += list(zip(pair.keys, sums))
        if chips is not None:
            sums = _sum_chips(chips.ins, chips.results, self.pos, f"sum_chips_{self.calls}")
            self.halves.update(zip(chips.keys, sums))

    def busy(self):
        assert not self.pending
        return bool(self.stages or self.queue or len(self.whole) < len(self.halves))

    def flush(self):
        self.riding = (self._pair_rider(), self._chip_rider(None), self._gather_rider())
        rd = _join(self.riding)
        _run_rider(rd, f"grads_exchange_tail_{self.calls}")
        self.done(rd)


def _update(loss_acc, grad_x, d_meta_rows, reducer, small, d_final, w, mom, var):
    meta = w["meta"]
    D = w["final_norm"].shape[0]
    L = w["ffn1_norm"].shape[0]
    Dq = D // N_CHIPS

    out = {}

    small_parts = [jnp.concatenate(small[n], axis=0) for n in ("ffn1_norm", "mix_norm", "ffn2_norm")]
    small_parts += [d_final, jnp.concatenate(small["pool_scale"], axis=0), jnp.concatenate(small["pool_maps"], axis=0)]
    loss_row = jnp.pad(loss_acc, ((0, 0), (0, D - loss_acc.shape[1])))
    rd = reducer.rider(None) if reducer.busy() else None
    reduced = _small_all_reduce(_pack_rows(small_parts + [d_meta_rows, loss_row], D), rd)
    reducer.done(rd)
    while reducer.busy():
        reducer.flush()
    for n in _BIG:
        gs = [reducer.whole[(n, i)] for i in range(L)]
        if n in _TRANSPOSED:
            res = _adamw(gs, *(jnp.swapaxes(t[n], 1, 2) for t in (w, mom, var)), f"adamw_{n}")
            out[n] = [jnp.swapaxes(r, 1, 2) for r in res]
        else:
            out[n] = _adamw(gs, w[n], mom[n], var[n], f"adamw_{n}")

    small_shapes = [w[n].shape for n in _SMALL]
    small_rows = sum(math.prod(shp) for shp in small_shapes) // D
    chip = 2 * lax.axis_index("x") + lax.axis_index("y")
    d_meta = lax.dynamic_slice_in_dim(reduced[small_rows:small_rows + N_META], chip * Dq, Dq, axis=1)
    names = _SMALL + ("meta",)
    packed_g = _pack_rows([reduced[:small_rows], d_meta], D)
    packed = [_pack_rows([t[n] for n in names], D) for t in (w, mom, var)]
    res = _adamw([packed_g], packed[0][None], packed[1][None], packed[2][None], "adamw_small")
    shapes = small_shapes + [meta.shape]
    unpacked = [_unpack_rows(r[0], shapes, D) for r in res]
    for k, n in enumerate(names):
        out[n] = tuple(u[k] for u in unpacked)

    loss = reduced[small_rows + N_META, 0]
    return (loss, grad_x) + tuple(out[n][j] for j in range(4) for n in _ORDER)


def kernel(x, meta, ffn1_norm, ffn1_gate, ffn1_up, ffn1_down, mix_norm, w_in, pool_maps, pool_scale, w_ret_up, w_pool_up, w_out, ffn2_norm, ffn2_gate, ffn2_up, ffn2_down, final_norm, loss_target, m_meta, m_ffn1_norm, m_ffn1_gate, m_ffn1_up, m_ffn1_down, m_mix_norm, m_w_in, m_pool_maps, m_pool_scale, m_w_ret_up, m_w_pool_up, m_w_out, m_ffn2_norm, m_ffn2_gate, m_ffn2_up, m_ffn2_down, m_final_norm, v_meta, v_ffn1_norm, v_ffn1_gate, v_ffn1_up, v_ffn1_down, v_mix_norm, v_w_in, v_pool_maps, v_pool_scale, v_w_ret_up, v_w_pool_up, v_w_out, v_ffn2_norm, v_ffn2_gate, v_ffn2_up, v_ffn2_down, v_final_norm):
    args = dict(locals())
    w = {n: args[n] for n in _ORDER}
    mom = {n: args["m_" + n] for n in _ORDER}
    var = {n: args["v_" + n] for n in _ORDER}

    assert x.shape[0] == 1, "one batch element per device"
    seq, D = x.shape[1], x.shape[2]
    assert seq % CHUNK == 0 and D % RET_WIDTH == 0 and (2 * POOL_WIDTH) % D == 0
    pad = (-(seq + N_META)) % CHUNK
    T = seq + N_META + pad
    tm = _pick_tile(T, 528, BF16_ROWS)
    cg = _pick_tile(T // CHUNK, 33, 1)

    shards = {n: _transport(w[n]) for n in _BIG}
    shards["meta"] = meta[None]
    wts = _Weights(shards)
    head = wts.rider([("ffn1_gate", 0), ("ffn1_up", 0), ("meta", 0)])
    _run_rider(head, "weights_gather_head")
    wts.take(head)
    meta_full = jnp.transpose(wts("meta", 0), (1, 0, 2)).reshape(N_META, D)

    reducer = _Reducer(unit=2 * shards["ffn1_gate"][0].size)
    loss_acc, dh, small, d_final = _local_step(x[0], meta_full, loss_target[0], w, wts, pad, tm, cg, reducer)
    grad_x = dh[pad + N_META:][None]
    return _update(loss_acc, grad_x, dh[pad:pad + N_META], reducer, small, d_final, w, mom, var)
```

```python
import functools
import math

import jax
import jax.numpy as jnp
from jax import lax
from jax.experimental import pallas as pl
from jax.experimental.pallas import tpu as pltpu

F32 = jnp.float32
BF16 = jnp.bfloat16

N_META = 16
RET_HEADS = 4
HEAD_DIM = 128
RET_WIDTH = RET_HEADS * HEAD_DIM
POOL_WINDOWS = (2, 4, 8, 16)
POOL_GROUPS = len(POOL_WINDOWS)
POOL_WIDTH = POOL_GROUPS * HEAD_DIM
CHUNK = 128
ROPE_BASE = 10000.0
EPS = 1e-6
ADAM_LR = 0.001
ADAM_B1 = 0.9
ADAM_B2 = 0.999
ADAM_EPS = 1e-08
ADAM_WD = 0.01
ADAM_STEP = 10

N_CHIPS = 4
LANES = 128
BF16_ROWS = 16
V7X_VMEM_LIMIT = 52 * 1024 * 1024
MESH = pl.DeviceIdType.MESH
ANY = pl.BlockSpec(memory_space=pl.ANY)


def _round_up(n, m):
    return -(-n // m) * m


def _pick_tile(n, target, mult):
    best = None
    for d in range(mult, min(n, target) + 1, mult):
        if n % d == 0:
            best = d
    assert best is not None, (n, target, mult)
    return best


def _params(sem=None):
    return pltpu.CompilerParams(dimension_semantics=sem, vmem_limit_bytes=V7X_VMEM_LIMIT)


def _dot(a, b):
    return jnp.dot(a, b, preferred_element_type=F32)


def _dot_nt(a, b):
    return lax.dot_general(a, b, (((1,), (1,)), ((), ())), preferred_element_type=F32)


def _dot_tn(a, b):
    return lax.dot_general(a, b, (((0,), (0,)), ((), ())), preferred_element_type=F32)


def _ein(spec, a, b):
    return jnp.einsum(spec, a, b, preferred_element_type=F32)


def _sigmoid(x):
    return jax.nn.sigmoid(x)


def _rms_fwd(x, gain):
    r = lax.rsqrt(jnp.mean(x * x, axis=-1, keepdims=True) + EPS)
    return x * r * gain


def _rms_bwd(x, gain, da):
    r = lax.rsqrt(jnp.mean(x * x, axis=-1, keepdims=True) + EPS)
    xh = x * r
    dgain = jnp.sum(da * xh, axis=0, keepdims=True)
    dxh = da * gain
    dx = r * (dxh - xh * jnp.mean(dxh * xh, axis=-1, keepdims=True))
    return dx, dgain


def _row_mask(t, tm, pad, shape):
    rows = t * tm + lax.broadcasted_iota(jnp.int32, shape, 0)
    return rows >= pad


def _mesh_pos():
    x, y, c = lax.axis_index("x"), lax.axis_index("y"), lax.axis_index("c")
    others = [(1 - x, y), (x, 1 - y), (1 - x, 1 - y)]
    return x, y, c, 2 * x + y, others


def _half_rows(c, rh):
    return pl.ds(pl.multiple_of(c * rh, rh), rh)


def _remote(src, dst, ssem, rsem, dev):
    return pltpu.make_async_remote_copy(src_ref=src, dst_ref=dst, send_sem=ssem, recv_sem=rsem,
                                        device_id=dev, device_id_type=MESH)


class _Rider:
    def __init__(self, ins, out_shapes, n_sem, start, finish, in_place=False):
        self.ins, self.out_shapes, self.n_sem, self.start, self.finish = ins, out_shapes, n_sem, start, finish
        self.in_place = [in_place] * len(ins)
        self.results = None
        self.middle, self.finish_late = None, finish

    def aliases(self, first_in, first_out):
        return {first_in + i: first_out + i for i, same in enumerate(self.in_place) if same}


class _SemWindow:
    def __init__(self, ref, base):
        self.ref, self.base = ref, base

    @property
    def at(self):
        return self

    def __getitem__(self, k):
        return self.ref.at[self.base + k]


def _join(riders):
    riders = [r for r in riders if r is not None]
    if len(riders) <= 1:
        return riders[0] if riders else None

    def run(which):
        def go(ins, outs, ssem, rsem):
            at, sem = 0, 0
            for r in riders:
                n = len(r.ins)
                if getattr(r, which) is not None:
                    getattr(r, which)(ins[at:at + n], outs[at:at + n], _SemWindow(ssem, sem), _SemWindow(rsem, sem))
                at, sem = at + n, sem + r.n_sem
        return go

    joined = _Rider(sum([list(r.ins) for r in riders], []), sum([list(r.out_shapes) for r in riders], []),
                    sum(r.n_sem for r in riders), run("start"), run("finish"))
    if any(r.middle is not None for r in riders):
        joined.middle, joined.finish_late = run("middle"), run("finish_late")
    joined.in_place = sum([r.in_place for r in riders], [])
    joined.parts = riders
    return joined


def _split_results(rider):
    at = 0
    for r in getattr(rider, "parts", []):
        r.results = rider.results[at:at + len(r.ins)]
        at += len(r.ins)


def _gather_rider(pieces):
    per = 7
    layers = [layer for _, layer in pieces]

    def first_copies(ins, outs, ssem, rsem):
        x, y, c, chip, others = _mesh_pos()
        copies = []
        for i, layer in enumerate(layers):
            mine = _half_rows(c, ins[i].shape[1] // 2)
            for j, (ox, oy) in enumerate(others):
                copies.append(_remote(ins[i].at[layer, mine, :], outs[i].at[chip, mine, :],
                                      ssem.at[per * i + j], rsem.at[per * i + j], (ox, oy, c)))
            copies.append(_remote(ins[i].at[layer], outs[i].at[chip],
                                  ssem.at[per * i + 6], rsem.at[per * i + 6], (x, y, 1 - c)))
        return copies

    def start(ins, outs, ssem, rsem):
        for cp in first_copies(ins, outs, ssem, rsem):
            cp.start()

    def forwards(outs, ssem, rsem):
        x, y, c, chip, others = _mesh_pos()
        copies = []
        for i in range(len(layers)):
            mine = _half_rows(c, outs[i].shape[1] // 2)
            for j, (ox, oy) in enumerate(others):
                rows = outs[i].at[2 * ox + oy, mine, :]
                copies.append(_remote(rows, rows, ssem.at[per * i + 3 + j], rsem.at[per * i + 3 + j], (x, y, 1 - c)))
        return copies

    def middle(ins, outs, ssem, rsem):
        x, y, c, chip, others = _mesh_pos()
        fwd = forwards(outs, ssem, rsem)
        for i in range(len(layers)):
            mine = _half_rows(c, ins[i].shape[1] // 2)
            for j, (ox, oy) in enumerate(others):
                rows = outs[i].at[2 * ox + oy, mine, :]
                _remote(rows, rows, ssem.at[per * i + j], rsem.at[per * i + j], (ox, oy, c)).wait_recv()
                fwd[3 * i + j].start()

    def finish_late(ins, outs, ssem, rsem):
        x, y, c, chip, others = _mesh_pos()
        sibling = (x, y, 1 - c)
        for i in range(len(layers)):
            theirs = _half_rows(1 - c, ins[i].shape[1] // 2)
            for j, (ox, oy) in enumerate(others):
                rows = outs[i].at[2 * ox + oy, theirs, :]
                _remote(rows, rows, ssem.at[per * i + 3 + j], rsem.at[per * i + 3 + j], sibling).wait_recv()
            own = outs[i].at[chip]
            _remote(own, own, ssem.at[per * i + 6], rsem.at[per * i + 6], sibling).wait_recv()
        for cp in first_copies(ins, outs, ssem, rsem) + forwards(outs, ssem, rsem):
            cp.wait_send()

    def finish(ins, outs, ssem, rsem):
        middle(ins, outs, ssem, rsem)
        finish_late(ins, outs, ssem, rsem)

    shapes = [jax.ShapeDtypeStruct((N_CHIPS,) + s.shape[1:], s.dtype) for s, _ in pieces]
    rider = _Rider([s for s, _ in pieces], shapes, per * len(pieces), start, finish)
    rider.middle, rider.finish_late = middle, finish_late
    return rider


def _chip_exchange_rider(ps):
    def copies(ins, outs, ssem, rsem):
        x, y, c, chip, others = _mesh_pos()
        return [_remote(ins[i].at[2 * ox + oy], outs[i].at[chip], ssem.at[3 * i + j], rsem.at[3 * i + j], (ox, oy, c))
                for i in range(len(ps)) for j, (ox, oy) in enumerate(others)]

    def start(ins, outs, ssem, rsem):
        for cp in copies(ins, outs, ssem, rsem):
            cp.start()

    def finish(ins, outs, ssem, rsem):
        x, y, c, chip, others = _mesh_pos()
        for i in range(len(ps)):
            for j, (ox, oy) in enumerate(others):
                slot = outs[i].at[2 * ox + oy]
                _remote(slot, slot, ssem.at[3 * i + j], rsem.at[3 * i + j], (ox, oy, c)).wait_recv()
        for cp in copies(ins, outs, ssem, rsem):
            cp.wait_send()

    return _Rider(list(ps), [jax.ShapeDtypeStruct(p.shape, p.dtype) for p in ps], 3 * len(ps), start, finish)


def _pair_exchange_rider(gs):
    def copies(ins, outs, ssem, rsem):
        x, y, c, _, _ = _mesh_pos()
        return [_remote(ins[i].at[:, _half_rows(1 - c, ins[i].shape[1] // 2), :], outs[i],
                        ssem.at[i], rsem.at[i], (x, y, 1 - c)) for i in range(len(gs))]

    def start(ins, outs, ssem, rsem):
        for cp in copies(ins, outs, ssem, rsem):
            cp.start()

    def finish(ins, outs, ssem, rsem):
        for cp in copies(ins, outs, ssem, rsem):
            cp.wait()

    shapes = [jax.ShapeDtypeStruct((g.shape[0], g.shape[1] // 2, g.shape[2]), g.dtype) for g in gs]
    return _Rider(list(gs), shapes, len(gs), start, finish)


def _run_rider(rider, name):
    def body(*refs):
        n = len(rider.ins)
        ins, outs = refs[:n], refs[n:2 * n]
        ssem, rsem = refs[2 * n:]
        rider.start(ins, outs, ssem, rsem)
        rider.finish(ins, outs, ssem, rsem)

    rider.results = pl.pallas_call(
        body,
        name=name,
        in_specs=[ANY] * len(rider.ins),
        out_specs=[ANY] * len(rider.ins),
        out_shape=rider.out_shapes,
        input_output_aliases=rider.aliases(0, 0),
        scratch_shapes=[pltpu.SemaphoreType.DMA((rider.n_sem,)), pltpu.SemaphoreType.DMA((rider.n_sem,))],
    )(*rider.ins)
    return rider.results


def _pair_gather_rider(fs):
    n = len(fs)

    def copies(outs, ssem, rsem):
        x, y, c, _, _ = _mesh_pos()
        halves = [outs[i].at[_half_rows(c, outs[i].shape[0] // 2), :] for i in range(n)]
        return [_remote(h, h, ssem.at[i], rsem.at[i], (x, y, 1 - c)) for i, h in enumerate(halves)]

    def start(ins, outs, ssem, rsem):
        for cp in copies(outs, ssem, rsem):
            cp.start()

    def finish(ins, outs, ssem, rsem):
        x, y, c, _, _ = _mesh_pos()
        for i in range(n):
            theirs = outs[i].at[_half_rows(1 - c, outs[i].shape[0] // 2), :]
            _remote(theirs, theirs, ssem.at[i], rsem.at[i], (x, y, 1 - c)).wait_recv()
        for cp in copies(outs, ssem, rsem):
            cp.wait_send()

    return _Rider(list(fs), [jax.ShapeDtypeStruct(f.shape, f.dtype) for f in fs], n, start, finish, in_place=True)


def _call(body, *, name, grid, in_specs, out_specs, out_shape, operands, scratch=(), sem=None, rider=None):
    if rider is None:
        return pl.pallas_call(
            body, name=name, grid=grid, in_specs=in_specs, out_specs=out_specs, out_shape=out_shape,
            scratch_shapes=list(scratch), compiler_params=_params(sem))(*operands)
    n_in, n_out, n_sc, r = len(in_specs), len(out_specs), len(scratch), len(rider.ins)

    steps = math.prod(grid)
    two_legs = rider.middle is not None and steps >= 4

    def carrying(*refs):
        a, b = n_in, n_in + r
        c, d = b + n_out, b + n_out + r
        e = d + n_sc
        ids = [pl.program_id(k) for k in range(len(grid))]
        first = functools.reduce(jnp.logical_and, [i == 0 for i in ids])
        last = functools.reduce(jnp.logical_and, [i == g - 1 for i, g in zip(ids, grid)])
        step = functools.reduce(lambda acc, ig: acc * ig[1] + ig[0], zip(ids, grid), 0)
        rider_refs = (refs[a:b], refs[c:d], refs[e], refs[e + 1])

        @pl.when(first)
        def _():
            rider.start(*rider_refs)

        body(*refs[:a], *refs[b:c], *refs[d:e])

        if two_legs:
            @pl.when(step == 3 * steps // 4)
            def _():
                rider.middle(*rider_refs)

        @pl.when(last)
        def _():
            (rider.finish_late if two_legs else rider.finish)(*rider_refs)

    outs = pl.pallas_call(
        carrying, name=name, grid=grid,
        in_specs=list(in_specs) + [ANY] * r,
        out_specs=list(out_specs) + [ANY] * r,
        out_shape=list(out_shape) + list(rider.out_shapes),
        scratch_shapes=list(scratch) + [pltpu.SemaphoreType.DMA((rider.n_sem,)), pltpu.SemaphoreType.DMA((rider.n_sem,))],
        input_output_aliases=rider.aliases(n_in, n_out),
        compiler_params=_params(("arbitrary",) * len(grid)),
    )(*operands, *rider.ins)
    rider.results = outs[n_out:]
    return outs[:n_out]


def _ffn_fwd(h, gain, wg, wu, wd, layer, tm, name, rider=None):
    T, D = h.shape
    Fs = wg.shape[-1]
    F = N_CHIPS * Fs

    def body(h_ref, g_ref, wg_ref, wu_ref, wd_ref, ho_ref, a_ref, go_ref, uo_ref, act_ref, acc_ref):
        s = pl.program_id(1)

        @pl.when(s == 0)
        def _():
            a_ref[...] = _rms_fwd(h_ref[...], g_ref[...]).astype(BF16)
            acc_ref[...] = jnp.zeros_like(acc_ref)

        a = a_ref[...]
        g = _dot(a, wg_ref[...])
        u = _dot(a, wu_ref[...])
        sg = _sigmoid(g)
        act = (g * sg * u).astype(BF16)
        go_ref[...] = (u * (sg * (1.0 + g * (1.0 - sg)))).astype(BF16)
        uo_ref[...] = (g * sg).astype(BF16)
        act_ref[...] = act
        acc_ref[...] += _dot(act, wd_ref[...])

        @pl.when(s == N_CHIPS - 1)
        def _():
            ho_ref[...] = h_ref[...] + 0.5 * acc_ref[...]

    row = pl.BlockSpec((tm, D), lambda t, s: (t, 0))
    col = pl.BlockSpec((tm, Fs), lambda t, s: (t, s))
    wcol = pl.BlockSpec((None, D, Fs), lambda t, s: (s, 0, 0))
    return _call(
        body, name=name, grid=(T // tm, N_CHIPS),
        in_specs=[row, pl.BlockSpec((None, 1, D), lambda t, s: (layer, 0, 0)), wcol, wcol,
                  pl.BlockSpec((None, Fs, D), lambda t, s: (s, 0, 0))],
        out_specs=[row, row, col, col, col],
        out_shape=[jax.ShapeDtypeStruct((T, D), F32), jax.ShapeDtypeStruct((T, D), BF16)]
        + [jax.ShapeDtypeStruct((T, F), BF16)] * 3,
        scratch=[pltpu.VMEM((tm, D), F32)],
        sem=("parallel", "arbitrary"), operands=(h, gain, wg, wu, wd), rider=rider)


def _ffn_fwd_up(h, gain, wg, wu, layer, tm, name, rider=None):
    T, D = h.shape
    Fs = wg.shape[-1]
    F = N_CHIPS * Fs

    def body(h_ref, g_ref, wg_ref, wu_ref, a_ref, go_ref, uo_ref, act_ref):
        @pl.when(pl.program_id(1) == 0)
        def _():
            a_ref[...] = _rms_fwd(h_ref[...], g_ref[...]).astype(BF16)

        a = a_ref[...]
        g = _dot(a, wg_ref[...])
        u = _dot(a, wu_ref[...])
        sg = _sigmoid(g)
        act_ref[...] = (g * sg * u).astype(BF16)
        go_ref[...] = (u * (sg * (1.0 + g * (1.0 - sg)))).astype(BF16)
        uo_ref[...] = (g * sg).astype(BF16)

    row = pl.BlockSpec((tm, D), lambda t, s: (t, 0))
    col = pl.BlockSpec((tm, Fs), lambda t, s: (t, s))
    wcol = pl.BlockSpec((None, D, Fs), lambda t, s: (s, 0, 0))
    return _call(
        body, name=name, grid=(T // tm, N_CHIPS),
        in_specs=[row, pl.BlockSpec((None, 1, D), lambda t, s: (layer, 0, 0)), wcol, wcol],
        out_specs=[row, col, col, col],
        out_shape=[jax.ShapeDtypeStruct((T, D), BF16)] + [jax.ShapeDtypeStruct((T, F), BF16)] * 3,
        sem=("parallel", "arbitrary"), operands=(h, gain, wg, wu), rider=rider)


def _ffn_fwd_down(h, act, wd, tm, name, rider=None):
    T, D = h.shape
    Fs = wd.shape[1]

    def body(h_ref, act_ref, wd_ref, ho_ref, acc_ref):
        s = pl.program_id(1)

        @pl.when(s == 0)
        def _():
            acc_ref[...] = jnp.zeros_like(acc_ref)

        acc_ref[...] += _dot(act_ref[...], wd_ref[...])

        @pl.when(s == N_CHIPS - 1)
        def _():
            ho_ref[...] = h_ref[...] + 0.5 * acc_ref[...]

    row = pl.BlockSpec((tm, D), lambda t, s: (t, 0))
    return _call(
        body, name=name, grid=(T // tm, N_CHIPS),
        in_specs=[row, pl.BlockSpec((tm, Fs), lambda t, s: (t, s)), pl.BlockSpec((None, Fs, D), lambda t, s: (s, 0, 0))],
        out_specs=[row],
        out_shape=[jax.ShapeDtypeStruct((T, D), F32)],
        scratch=[pltpu.VMEM((tm, D), F32)],
        sem=("parallel", "arbitrary"), operands=(h, act, wd), rider=rider)[0]


def _inproj_fwd(h, gain, win, layer, tm, name, rider=None):
    T, D = h.shape
    Ns = win.shape[-1]

    def body(h_ref, g_ref, w_ref, z_ref, b_ref):
        @pl.when(pl.program_id(1) == 0)
        def _():
            b_ref[...] = _rms_fwd(h_ref[...], g_ref[...]).astype(BF16)

        z_ref[...] = _dot(b_ref[...], w_ref[...]).astype(BF16)

    return _call(
        body, name=name, grid=(T // tm, N_CHIPS),
        in_specs=[pl.BlockSpec((tm, D), lambda t, s: (t, 0)),
                  pl.BlockSpec((None, 1, D), lambda t, s: (layer, 0, 0)),
                  pl.BlockSpec((None, D, Ns), lambda t, s: (s, 0, 0))],
        out_specs=[pl.BlockSpec((tm, Ns), lambda t, s: (t, s)), pl.BlockSpec((tm, D), lambda t, s: (t, 0))],
        out_shape=[jax.ShapeDtypeStruct((T, N_CHIPS * Ns), BF16), jax.ShapeDtypeStruct((T, D), BF16)],
        sem=("parallel", "arbitrary"), operands=(h, gain, win), rider=rider)


def _ret_consts(T, pad):
    half = HEAD_DIM // 2
    inv_freq = ROPE_BASE ** (-jnp.arange(half, dtype=F32) / half)
    pos = jnp.arange(T, dtype=F32) - pad
    ang = pos[:, None] * inv_freq[None, :]
    cos = jnp.cos(ang)
    sin = jnp.sin(ang)
    cosf = jnp.concatenate([cos, cos], axis=1)
    sinf = jnp.concatenate([-sin, sin], axis=1)
    log_gamma = jnp.log1p(-(2.0 ** (-5.0 - jnp.arange(RET_HEADS, dtype=F32))))
    idx = jnp.arange(CHUNK, dtype=F32)
    diff = idx[:, None] - idx[None, :]
    intra = jnp.where(diff[None] >= 0, jnp.exp(diff[None] * log_gamma[:, None, None]), 0.0)
    k_decay = jnp.exp((CHUNK - 1.0 - idx)[None, :] * log_gamma[:, None])
    q_decay = jnp.exp((idx + 1.0)[None, :] * log_gamma[:, None])
    chunk_decay = jnp.exp(CHUNK * log_gamma)
    kdec = jnp.broadcast_to(k_decay[:, :, None], (RET_HEADS, CHUNK, HEAD_DIM))
    qdec = jnp.broadcast_to(q_decay[:, :, None], (RET_HEADS, CHUNK, HEAD_DIM))
    cdb = jnp.broadcast_to(chunk_decay[:, None, None], (RET_HEADS, 8, HEAD_DIM))
    return cosf, sinf, intra, kdec, qdec, cdb


def _rot(t, cosv, sinv):
    return t * cosv + pltpu.roll(t, HEAD_DIM // 2, 1) * sinv


def _rot_t(g, cosv, sinv):
    return g * cosv + pltpu.roll(g * sinv, HEAD_DIM // 2, 1)


def _head_specs(tg, section, order):
    return pl.BlockSpec((tg, HEAD_DIM), lambda h, g: (order(g), section * RET_HEADS + h))


def _ret_fwd(z, consts, cg, name, rider=None):
    T = z.shape[0]
    N = T // CHUNK
    ng = N // cg
    tg = cg * CHUNK
    cosf, sinf, intra, kdec, qdec, cdb = consts
    fwd = lambda g: g

    def body(zq, zk, zv, zg, cos_ref, sin_ref, m_ref, kd_ref, qd_ref, cd_ref, r_ref, o_ref, s_ref, st_ref):
        @pl.when(pl.program_id(1) == 0)
        def _():
            st_ref[...] = jnp.zeros_like(st_ref)

        cosv = cos_ref[...]
        sinv = sin_ref[...]
        q3 = (_rot(zq[...].astype(F32), cosv, sinv) * (HEAD_DIM ** -0.5)).reshape(cg, CHUNK, HEAD_DIM)
        k3 = _rot(zk[...].astype(F32), cosv, sinv).reshape(cg, CHUNK, HEAD_DIM)
        vb = zv[...].reshape(cg, CHUNK, HEAD_DIM).astype(BF16)
        scores = _ein("ncd,nmd->ncm", q3.astype(BF16), k3.astype(BF16)) * m_ref[...][None]
        inner = _ein("ncm,nmd->ncd", scores.astype(BF16), vb)
        kv = _ein("ncd,nce->nde", (k3 * kd_ref[...][None]).astype(BF16), vb)
        cd = cd_ref[0:1, :]
        state = st_ref[...]
        for n in range(cg):
            s_ref[n] = state
            state = state * cd + kv[n]
        st_ref[...] = state
        qdb = (q3 * qd_ref[...][None]).astype(BF16)
        cross = _ein("ncd,nde->nce", qdb, s_ref[...].astype(BF16))
        out = (inner + cross).reshape(tg, HEAD_DIM)
        o_ref[...] = out
        xc = out - jnp.mean(out, axis=-1, keepdims=True)
        rn = xc * lax.rsqrt(jnp.mean(xc * xc, axis=-1, keepdims=True) + EPS)
        g = zg[...].astype(F32)
        r_ref[...] = (rn * (g * _sigmoid(g))).astype(BF16)

    tab = pl.BlockSpec((tg, HEAD_DIM), lambda h, g: (g, 0))
    per_head = lambda rows: pl.BlockSpec((None, rows, HEAD_DIM), lambda h, g: (h, 0, 0))
    head_out = pl.BlockSpec((tg, HEAD_DIM), lambda h, g: (g, h))
    return _call(
        body, name=name, grid=(RET_HEADS, ng),
        in_specs=[_head_specs(tg, i, fwd) for i in range(4)]
        + [tab, tab, per_head(CHUNK), per_head(CHUNK), per_head(CHUNK), per_head(8)],
        out_specs=[head_out, head_out, pl.BlockSpec((None, cg, HEAD_DIM, HEAD_DIM), lambda h, g: (h, g, 0, 0))],
        out_shape=[jax.ShapeDtypeStruct((T, RET_WIDTH), BF16), jax.ShapeDtypeStruct((T, RET_WIDTH), F32),
                   jax.ShapeDtypeStruct((RET_HEADS, N, HEAD_DIM, HEAD_DIM), F32)],
        scratch=[pltpu.VMEM((HEAD_DIM, HEAD_DIM), F32)],
        sem=("parallel", "arbitrary"), operands=(z, z, z, z, cosf, sinf, intra, kdec, qdec, cdb), rider=rider)


def _window_sums(u, shift_of):
    sums = []
    s = u
    k = 1
    while k < POOL_WINDOWS[-1]:
        s = s + pltpu.roll(s, shift_of(k), 0)
        sums.append(s)
        k *= 2
    return sums


def _select_group(vals, g):
    out = vals[-1]
    for i in range(len(vals) - 2, -1, -1):
        out = jnp.where(g == i, vals[i], out)
    return out


def _pool_parts(u, g, T, pad):
    rows = lax.broadcasted_iota(jnp.int32, (T, HEAD_DIM), 0)
    valid = rows >= pad
    win = _select_group([float(w) for w in POOL_WINDOWS], g)
    div = jnp.clip((rows - pad + 1).astype(F32), 1.0, win)
    s = _select_group(_window_sums(u, lambda k: k), g)
    pooled = jnp.where(valid, s / div - u, 0.0)
    return pooled, div, valid


def _pool_specs(T, layer):
    first = 4 * RET_WIDTH // HEAD_DIM
    return [
        pl.BlockSpec((T, HEAD_DIM), lambda g: (0, first + g)),
        pl.BlockSpec((None, None, HEAD_DIM, HEAD_DIM), lambda g: (layer, g, 0, 0)),
        pl.BlockSpec((None, 1, HEAD_DIM), lambda g: (layer, 0, g)),
    ]


def _pool_fwd(z, maps, scale, layer, pad, name):
    T = z.shape[0]
    assert pad >= POOL_WINDOWS[-1], "window rolls wrap into the zero rows in front"

    def body(zu, maps_ref, sc_ref, pm_ref):
        g = pl.program_id(0)
        pooled, _, _ = _pool_parts(zu[...].astype(F32), g, T, pad)
        y = _dot(pooled.astype(BF16), maps_ref[...].astype(BF16))
        pm_ref[...] = (y * sc_ref[...]).astype(BF16)

    return _call(
        body, name=name, grid=(POOL_GROUPS,),
        in_specs=_pool_specs(T, layer),
        out_specs=[pl.BlockSpec((T, HEAD_DIM), lambda g: (0, g))],
        out_shape=[jax.ShapeDtypeStruct((T, POOL_WIDTH), BF16)],
        sem=("parallel",), operands=(z, maps, scale))[0]


def _gate_specs(tm, D):
    nb = D // RET_WIDTH
    first = (4 * RET_WIDTH + POOL_WIDTH) // RET_WIDTH
    return [pl.BlockSpec((tm, RET_WIDTH), functools.partial(lambda t, j: (t, j), j=first + j)) for j in range(2 * nb)]


def _load_gates(refs, nb):
    ga = jnp.concatenate([r[...].astype(F32) for r in refs[:nb]], axis=1)
    gb = jnp.concatenate([r[...].astype(F32) for r in refs[nb:]], axis=1)
    return ga, gb


def _mix_fwd(h, r, pm, z, wru, wpu, wout, tm, name, rider=None):
    T, D = h.shape
    Dq = D // N_CHIPS
    nb = D // RET_WIDTH

    def body(*refs):
        h_ref, r_ref, pm_ref = refs[:3]
        gate_refs = refs[3:3 + 2 * nb]
        wru_ref, wpu_ref, wout_ref, ho_ref, mx_ref, ret_ref, pool_ref = refs[3 + 2 * nb:]
        rv = r_ref[...]
        pv = pm_ref[...]
        ret = jnp.concatenate([_dot(rv, wru_ref[s]) for s in range(N_CHIPS)], axis=1)
        pool = jnp.concatenate([_dot(pv, wpu_ref[s]) for s in range(N_CHIPS)], axis=1)
        ga, gb = _load_gates(gate_refs, nb)
        mixed = (_sigmoid(ga) * ret + _sigmoid(gb) * pool).astype(BF16)
        mx_ref[...] = mixed
        ret_ref[...] = ret.astype(BF16)
        pool_ref[...] = pool.astype(BF16)
        ho_ref[...] = h_ref[...] + _dot(mixed, wout_ref[...].reshape(D, D))

    row = pl.BlockSpec((tm, D), lambda t: (t, 0))
    half = pl.BlockSpec((tm, RET_WIDTH), lambda t: (t, 0))
    up = pl.BlockSpec((N_CHIPS, RET_WIDTH, Dq), lambda t: (0, 0, 0))
    return _call(
        body, name=name, grid=(T // tm,),
        in_specs=[row, half, half] + _gate_specs(tm, D) + [up, up, pl.BlockSpec((N_CHIPS, Dq, D), lambda t: (0, 0, 0))],
        out_specs=[row, row, row, row],
        out_shape=[jax.ShapeDtypeStruct((T, D), F32)] + [jax.ShapeDtypeStruct((T, D), BF16)] * 3,
        sem=("parallel",), operands=(h, r, pm, *([z] * (2 * nb)), wru, wpu, wout), rider=rider)


def _final_loss(h, gain, tgt, name):
    T, D = h.shape
    first = (T - tgt.shape[0]) // CHUNK

    def body(h_ref, g_ref, t_ref, dh_ref, loss_ref, dg_ref):
        i = pl.program_id(0)

        @pl.when(i == 0)
        def _():
            loss_ref[...] = jnp.zeros_like(loss_ref)
            dg_ref[...] = jnp.zeros_like(dg_ref)

        x = h_ref[...]
        gain_v = g_ref[...]
        err = jnp.where(i >= first, _rms_fwd(x, gain_v) - t_ref[...], 0.0)
        loss_ref[...] += 0.5 * jnp.sum(jnp.mean(err * err, axis=-1))
        dx, dgain = _rms_bwd(x, gain_v, err * (1.0 / D))
        dg_ref[...] += dgain
        dh_ref[...] = dx

    return _call(
        body, name=name, grid=(T // CHUNK,),
        in_specs=[pl.BlockSpec((CHUNK, D), lambda i: (i, 0)),
                  pl.BlockSpec((1, D), lambda i: (0, 0)),
                  pl.BlockSpec((CHUNK, D), lambda i: (jnp.maximum(i - first, 0), 0))],
        out_specs=[pl.BlockSpec((CHUNK, D), lambda i: (i, 0)),
                   pl.BlockSpec((1, LANES), lambda i: (0, 0)),
                   pl.BlockSpec((1, D), lambda i: (0, 0))],
        out_shape=[jax.ShapeDtypeStruct((T, D), F32), jax.ShapeDtypeStruct((1, LANES), F32),
                   jax.ShapeDtypeStruct((1, D), F32)],
        sem=("arbitrary",), operands=(h, gain, tgt))


def _ffn_bwd_act(dy, g, u, wd, tm, name, rider=None):
    T, D = dy.shape
    Fs = wd.shape[1]
    F = N_CHIPS * Fs

    def body(dy_ref, go_ref, uo_ref, wd_ref, dg_ref, du_ref, dyh_ref):
        @pl.when(pl.program_id(1) == 0)
        def _():
            dyh_ref[...] = (0.5 * dy_ref[...]).astype(BF16)

        dact = _dot_nt(dyh_ref[...], wd_ref[...])
        du_ref[...] = (dact * uo_ref[...].astype(F32)).astype(BF16)
        dg_ref[...] = (dact * go_ref[...].astype(F32)).astype(BF16)

    row = pl.BlockSpec((tm, D), lambda t, s: (t, 0))
    col = pl.BlockSpec((tm, Fs), lambda t, s: (t, s))
    return _call(
        body, name=name, grid=(T // tm, N_CHIPS),
        in_specs=[row, col, col, pl.BlockSpec((None, Fs, D), lambda t, s: (s, 0, 0))],
        out_specs=[col, col, row],
        out_shape=[jax.ShapeDtypeStruct((T, F), BF16), jax.ShapeDtypeStruct((T, F), BF16),
                   jax.ShapeDtypeStruct((T, D), BF16)],
        sem=("parallel", "arbitrary"), operands=(dy, g, u, wd), rider=rider)


def _ffn_bwd_in(dy, h, gain, dg, du, wg, wu, layer, tm, pad, name, rider=None):
    T, D = h.shape
    Fs = wg.shape[-1]

    def body(dy_ref, h_ref, g_ref, dg_ref, du_ref, wg_ref, wu_ref, dh_ref, dgain_ref, da_ref):
        t = pl.program_id(0)
        s = pl.program_id(1)

        @pl.when((t == 0) & (s == 0))
        def _():
            dgain_ref[...] = jnp.zeros_like(dgain_ref)

        @pl.when(s == 0)
        def _():
            da_ref[...] = jnp.zeros_like(da_ref)

        da_ref[...] += _dot_nt(dg_ref[...], wg_ref[...]) + _dot_nt(du_ref[...], wu_ref[...])

        @pl.when(s == N_CHIPS - 1)
        def _():
            dx, dgain = _rms_bwd(h_ref[...], g_ref[...], da_ref[...])
            dgain_ref[...] += dgain
            dh_ref[...] = jnp.where(_row_mask(t, tm, pad, (tm, D)), dy_ref[...] + dx, 0.0)

    row = pl.BlockSpec((tm, D), lambda t, s: (t, 0))
    col = pl.BlockSpec((tm, Fs), lambda t, s: (t, s))
    wcol = pl.BlockSpec((None, D, Fs), lambda t, s: (s, 0, 0))
    return _call(
        body, name=name, grid=(T // tm, N_CHIPS),
        in_specs=[row, row, pl.BlockSpec((None, 1, D), lambda t, s: (layer, 0, 0)), col, col, wcol, wcol],
        out_specs=[row, pl.BlockSpec((1, D), lambda t, s: (0, 0))],
        out_shape=[jax.ShapeDtypeStruct((T, D), F32), jax.ShapeDtypeStruct((1, D), F32)],
        scratch=[pltpu.VMEM((tm, D), F32)],
        sem=("arbitrary", "arbitrary"), operands=(dy, h, gain, dg, du, wg, wu), rider=rider)


def _grad_tn(a, b, mode, scale, tm, name, rider=None):
    T = a.shape[0]
    if mode == "col":
        per, R, C = 1, a.shape[1], b.shape[1] // N_CHIPS
        a_spec = pl.BlockSpec((tm, R), lambda s, t: (t, 0))
        b_spec = pl.BlockSpec((tm, C), lambda s, t: (t, s))
    else:
        per, R, C = 2, a.shape[1] // N_CHIPS, b.shape[1]
        a_spec = pl.BlockSpec((tm, per * R), lambda s, t: (t, s))
        b_spec = pl.BlockSpec((tm, C), lambda s, t: (t, 0))
    nt = T // tm

    def body(a_ref, b_ref, o_ref, acc_ref):
        t = pl.program_id(1)

        @pl.when(t == 0)
        def _():
            acc_ref[...] = jnp.zeros_like(acc_ref)

        acc_ref[...] += _dot_tn(a_ref[...].astype(BF16), b_ref[...].astype(BF16))

        @pl.when(t == nt - 1)
        def _():
            o_ref[...] = (scale * acc_ref[...]).astype(BF16).reshape(per, R, C)

    return _call(
        body, name=name, grid=(N_CHIPS // per, nt),
        in_specs=[a_spec, b_spec],
        out_specs=[pl.BlockSpec((per, R, C), lambda s, t: (s, 0, 0))],
        out_shape=[jax.ShapeDtypeStruct((N_CHIPS, R, C), BF16)],
        scratch=[pltpu.VMEM((per * R, C), F32)],
        sem=("parallel", "arbitrary"), operands=(a, b), rider=rider)[0]


def _grad_mix(mixed, dh, r, dret, pm, dpool, tk, name, rider=None):
    T, D = dh.shape
    Dq = D // N_CHIPS
    nt = T // tk

    def body(mx_ref, dh_ref, r_ref, dret_ref, pm_ref, dpool_ref, go_ref, gr_ref, gp_ref, ao_ref, ar_ref, ap_ref):
        t = pl.program_id(0)

        @pl.when(t == 0)
        def _():
            ao_ref[...] = jnp.zeros_like(ao_ref)
            ar_ref[...] = jnp.zeros_like(ar_ref)
            ap_ref[...] = jnp.zeros_like(ap_ref)

        ao_ref[...] += _dot_tn(mx_ref[...], dh_ref[...].astype(BF16))
        ar_ref[...] += _dot_tn(r_ref[...], dret_ref[...])
        ap_ref[...] += _dot_tn(pm_ref[...], dpool_ref[...])

        @pl.when(t == nt - 1)
        def _():
            go_ref[...] = ao_ref[...].astype(BF16).reshape(N_CHIPS, Dq, D)
            for s in range(N_CHIPS):
                gr_ref[s] = ar_ref[:, s * Dq:(s + 1) * Dq].astype(BF16)
                gp_ref[s] = ap_ref[:, s * Dq:(s + 1) * Dq].astype(BF16)

    row = pl.BlockSpec((tk, D), lambda t: (t, 0))
    half = pl.BlockSpec((tk, RET_WIDTH), lambda t: (t, 0))
    whole = lambda shape: pl.BlockSpec(shape, lambda t: (0, 0, 0))
    return _call(
        body, name=name, grid=(nt,),
        in_specs=[row, row, half, row, half, row],
        out_specs=[whole((N_CHIPS, Dq, D)), whole((N_CHIPS, RET_WIDTH, Dq)), whole((N_CHIPS, POOL_WIDTH, Dq))],
        out_shape=[jax.ShapeDtypeStruct((N_CHIPS, Dq, D), BF16),
                   jax.ShapeDtypeStruct((N_CHIPS, RET_WIDTH, Dq), BF16),
                   jax.ShapeDtypeStruct((N_CHIPS, POOL_WIDTH, Dq), BF16)],
        scratch=[pltpu.VMEM((D, D), F32), pltpu.VMEM((RET_WIDTH, D), F32), pltpu.VMEM((POOL_WIDTH, D), F32)],
        sem=("arbitrary",), operands=(mixed, dh, r, dret, pm, dpool), rider=rider)


def _mix_bwd_dx(dh, z, ret, pool, wout, wru, wpu, tm, name, rider=None):
    T, D = dh.shape
    Dq = D // N_CHIPS
    nb = D // RET_WIDTH

    def body(*refs):
        dh_ref = refs[0]
        gate_refs = refs[1:1 + 2 * nb]
        ret_ref, pool_ref, wout_ref, wru_ref, wpu_ref, dgab_ref, dret_ref, dpool_ref, dr_ref, dpm_ref = refs[1 + 2 * nb:]
        dmixed = _dot_nt(dh_ref[...].astype(BF16), wout_ref[...].reshape(D, D))
        ga, gb = _load_gates(gate_refs, nb)
        sa = _sigmoid(ga)
        sb = _sigmoid(gb)
        dgab_ref[:, :D] = (dmixed * ret_ref[...].astype(F32) * (sa * (1.0 - sa))).astype(BF16)
        dgab_ref[:, D:] = (dmixed * pool_ref[...].astype(F32) * (sb * (1.0 - sb))).astype(BF16)
        dret = (dmixed * sa).astype(BF16)
        dpool = (dmixed * sb).astype(BF16)
        dret_ref[...] = dret
        dpool_ref[...] = dpool
        dr = _dot_nt(dret[:, :Dq], wru_ref[0])
        dpm = _dot_nt(dpool[:, :Dq], wpu_ref[0])
        for s in range(1, N_CHIPS):
            dr += _dot_nt(dret[:, s * Dq:(s + 1) * Dq], wru_ref[s])
            dpm += _dot_nt(dpool[:, s * Dq:(s + 1) * Dq], wpu_ref[s])
        dr_ref[...] = dr
        dpm_ref[...] = dpm

    row = pl.BlockSpec((tm, D), lambda t: (t, 0))
    half = pl.BlockSpec((tm, RET_WIDTH), lambda t: (t, 0))
    up = pl.BlockSpec((N_CHIPS, RET_WIDTH, Dq), lambda t: (0, 0, 0))
    return _call(
        body, name=name, grid=(T // tm,),
        in_specs=[row] + _gate_specs(tm, D) + [row, row, pl.BlockSpec((N_CHIPS, Dq, D), lambda t: (0, 0, 0)), up, up],
        out_specs=[pl.BlockSpec((tm, 2 * D), lambda t: (t, 0)), row, row, half, half],
        out_shape=[jax.ShapeDtypeStruct((T, 2 * D), BF16), jax.ShapeDtypeStruct((T, D), BF16),
                   jax.ShapeDtypeStruct((T, D), BF16), jax.ShapeDtypeStruct((T, RET_WIDTH), F32),
                   jax.ShapeDtypeStruct((T, POOL_WIDTH), F32)],
        sem=("parallel",), operands=(dh, *([z] * (2 * nb)), ret, pool, wout, wru, wpu), rider=rider)


def _pool_bwd(z, dpm, maps, scale, layer, pad, name):
    T = z.shape[0]

    def body(zu, maps_ref, sc_ref, dpm_ref, du_ref, dmaps_ref, dsc_ref):
        g = pl.program_id(0)
        u = zu[...].astype(F32)
        pooled, div, valid = _pool_parts(u, g, T, pad)
        pb = pooled.astype(BF16)
        mb = maps_ref[...].astype(BF16)
        dp = dpm_ref[...]
        dsc_ref[...] = jnp.sum(dp * _dot(pb, mb), axis=0, keepdims=True)
        dyb = (dp * sc_ref[...]).astype(BF16)
        dmaps_ref[...] = _dot_tn(pb, dyb)
        dpooled = jnp.where(valid, _dot_nt(dyb, mb), 0.0)
        ahead = _select_group(_window_sums(dpooled / div, lambda k: T - k), g)
        du_ref[...] = jnp.where(valid, ahead - dpooled, 0.0).astype(BF16)

    blk = pl.BlockSpec((T, HEAD_DIM), lambda g: (0, g))
    return _call(
        body, name=name, grid=(POOL_GROUPS,),
        in_specs=_pool_specs(T, layer) + [blk],
        out_specs=[blk, pl.BlockSpec((None, HEAD_DIM, HEAD_DIM), lambda g: (g, 0, 0)),
                   pl.BlockSpec((1, HEAD_DIM), lambda g: (0, g))],
        out_shape=[jax.ShapeDtypeStruct((T, POOL_WIDTH), BF16),
                   jax.ShapeDtypeStruct((POOL_GROUPS, HEAD_DIM, HEAD_DIM), F32),
                   jax.ShapeDtypeStruct((1, POOL_WIDTH), F32)],
        sem=("parallel",), operands=(z, maps, scale, dpm))


def _ret_bwd_local(z, o_pre, s_all, dr, consts, cg, name):
    T = z.shape[0]
    N = T // CHUNK
    ng = N // cg
    tg = cg * CHUNK
    cosf, sinf, intra, _, qdec, _ = consts
    fwd = lambda g: g

    def body(zq, zk, zv, zg, o_ref, s_ref, dr_ref, cos_ref, sin_ref, m_ref, qd_ref,
             dq_ref, dg_ref, dk_ref, dv_ref, ds_ref):
        cosv = cos_ref[...]
        sinv = sin_ref[...]
        scale = HEAD_DIM ** -0.5
        q3 = (_rot(zq[...].astype(F32), cosv, sinv) * scale).reshape(cg, CHUNK, HEAD_DIM)
        k3 = _rot(zk[...].astype(F32), cosv, sinv).reshape(cg, CHUNK, HEAD_DIM)
        qb = q3.astype(BF16)
        kb = k3.astype(BF16)
        vb = zv[...].reshape(cg, CHUNK, HEAD_DIM).astype(BF16)
        mask = m_ref[...][None]
        sb = (_ein("ncd,nmd->ncm", qb, kb) * mask).astype(BF16)
        qdv = qd_ref[...][None]
        qdb = (q3 * qdv).astype(BF16)

        out = o_ref[...]
        xc = out - jnp.mean(out, axis=-1, keepdims=True)
        rstd = lax.rsqrt(jnp.mean(xc * xc, axis=-1, keepdims=True) + EPS)
        rn = xc * rstd
        g = zg[...].astype(F32)
        sg = _sigmoid(g)
        drv = dr_ref[...]
        dg_ref[...] = (drv * rn * (sg * (1.0 + g * (1.0 - sg)))).astype(BF16)
        drn = drv * (g * sg)
        dout = rstd * (drn - jnp.mean(drn, axis=-1, keepdims=True)
                       - rn * jnp.mean(drn * rn, axis=-1, keepdims=True))
        dob = dout.reshape(cg, CHUNK, HEAD_DIM).astype(BF16)

        dsb = (_ein("ncd,nmd->ncm", dob, vb) * mask).astype(BF16)
        dv_ref[...] = _ein("ncm,ncd->nmd", sb, dob).reshape(tg, HEAD_DIM)
        dk_ref[...] = _ein("ncm,ncd->nmd", dsb, qb).reshape(tg, HEAD_DIM)
        dq3 = _ein("ncm,nmd->ncd", dsb, kb) + _ein("nce,nde->ncd", dob, s_ref[...].astype(BF16)) * qdv
        dq_ref[...] = _rot_t(dq3.reshape(tg, HEAD_DIM) * scale, cosv, sinv).astype(BF16)
        ds_ref[...] = _ein("ncd,nce->nde", qdb, dob)

    tab = pl.BlockSpec((tg, HEAD_DIM), lambda h, g: (g, 0))
    per_head = pl.BlockSpec((None, CHUNK, HEAD_DIM), lambda h, g: (h, 0, 0))
    head_blk = pl.BlockSpec((tg, HEAD_DIM), lambda h, g: (g, h))
    state_blk = pl.BlockSpec((None, cg, HEAD_DIM, HEAD_DIM), lambda h, g: (h, g, 0, 0))
    return _call(
        body, name=name, grid=(RET_HEADS, ng),
        in_specs=[_head_specs(tg, i, fwd) for i in range(4)]
        + [head_blk, state_blk, head_blk, tab, tab, per_head, per_head],
        out_specs=[head_blk, head_blk, head_blk, head_blk, state_blk],
        out_shape=[jax.ShapeDtypeStruct((T, RET_WIDTH), BF16), jax.ShapeDtypeStruct((T, RET_WIDTH), BF16),
                   jax.ShapeDtypeStruct((T, RET_WIDTH), F32), jax.ShapeDtypeStruct((T, RET_WIDTH), F32),
                   jax.ShapeDtypeStruct((RET_HEADS, N, HEAD_DIM, HEAD_DIM), F32)],
        sem=("parallel", "parallel"), operands=(z, z, z, z, o_pre, s_all, dr, cosf, sinf, intra, qdec))


def _ret_bwd_state(z, dkp, dvp, ds, consts, cg, name):
    T = z.shape[0]
    N = T // CHUNK
    ng = N // cg
    tg = cg * CHUNK
    cosf, sinf, _, kdec, _, cdb = consts
    rev = lambda g: ng - 1 - g

    def body(zk, zv, dkp_ref, dvp_ref, ds_ref, cos_ref, sin_ref, kd_ref, cd_ref, dk_ref, dv_ref, gs_ref, dkv_ref):
        @pl.when(pl.program_id(1) == 0)
        def _():
            gs_ref[...] = jnp.zeros_like(gs_ref)

        cosv = cos_ref[...]
        sinv = sin_ref[...]
        cd = cd_ref[0:1, :]
        grad = gs_ref[...]
        for n in reversed(range(cg)):
            dkv_ref[n] = grad
            grad = ds_ref[n] + cd * grad
        gs_ref[...] = grad
        dkvb = dkv_ref[...].astype(BF16)
        kdv = kd_ref[...][None]
        k3 = _rot(zk[...].astype(F32), cosv, sinv).reshape(cg, CHUNK, HEAD_DIM)
        vb = zv[...].reshape(cg, CHUNK, HEAD_DIM).astype(BF16)
        dk3 = _ein("nce,nde->ncd", vb, dkvb) * kdv
        dv3 = _ein("ncd,nde->nce", (k3 * kdv).astype(BF16), dkvb)
        dk_ref[...] = _rot_t(dkp_ref[...] + dk3.reshape(tg, HEAD_DIM), cosv, sinv).astype(BF16)
        dv_ref[...] = (dvp_ref[...] + dv3.reshape(tg, HEAD_DIM)).astype(BF16)

    tab = pl.BlockSpec((tg, HEAD_DIM), lambda h, g: (rev(g), 0))
    head_blk = pl.BlockSpec((tg, HEAD_DIM), lambda h, g: (rev(g), h))
    return _call(
        body, name=name, grid=(RET_HEADS, ng),
        in_specs=[_head_specs(tg, 1, rev), _head_specs(tg, 2, rev), head_blk, head_blk,
                  pl.BlockSpec((None, cg, HEAD_DIM, HEAD_DIM), lambda h, g: (h, rev(g), 0, 0)),
                  tab, tab,
                  pl.BlockSpec((None, CHUNK, HEAD_DIM), lambda h, g: (h, 0, 0)),
                  pl.BlockSpec((None, 8, HEAD_DIM), lambda h, g: (h, 0, 0))],
        out_specs=[head_blk, head_blk],
        out_shape=[jax.ShapeDtypeStruct((T, RET_WIDTH), BF16)] * 2,
        scratch=[pltpu.VMEM((HEAD_DIM, HEAD_DIM), F32), pltpu.VMEM((cg, HEAD_DIM, HEAD_DIM), F32)],
        sem=("parallel", "arbitrary"), operands=(z, z, dkp, dvp, ds, cosf, sinf, kdec, cdb))


def _z_segments(pieces, ns):
    segs, at = [], 0
    for k, p in enumerate(pieces):
        width = p.shape[1]
        lo = at
        while lo < at + width:
            s = lo // ns
            hi = min(at + width, (s + 1) * ns)
            segs.append((k, lo - at, hi - at, s, lo - s * ns, hi - s * ns))
            lo = hi
        at += width
    assert at == N_CHIPS * ns and all(v % LANES == 0 for seg in segs for v in (seg[1], seg[2], seg[4], seg[5]))
    return segs


def _inproj_bwd_dx(pieces, win, h, gain, dh_in, layer, tm, pad, name, rider=None):
    T, D = h.shape
    Ns = win.shape[-1]
    n = len(pieces)
    segs = _z_segments(pieces, Ns)

    def body(*refs):
        piece_refs = refs[:n]
        w_ref, h_ref, g_ref, dhi_ref, dh_ref, dgain_ref = refs[n:]
        t = pl.program_id(0)

        @pl.when(t == 0)
        def _():
            dgain_ref[...] = jnp.zeros_like(dgain_ref)

        db = None
        for k, a, b, s, c, d in segs:
            term = _dot_nt(piece_refs[k][:, a:b], w_ref[s, :, c:d])
            db = term if db is None else db + term
        dx, dgain = _rms_bwd(h_ref[...], g_ref[...], db)
        dgain_ref[...] += dgain
        dh_ref[...] = jnp.where(_row_mask(t, tm, pad, (tm, D)), dhi_ref[...] + dx, 0.0)

    row = pl.BlockSpec((tm, D), lambda t: (t, 0))
    return _call(
        body, name=name, grid=(T // tm,),
        in_specs=[pl.BlockSpec((tm, p.shape[1]), lambda t: (t, 0)) for p in pieces]
        + [pl.BlockSpec((N_CHIPS, D, Ns), lambda t: (0, 0, 0)), row,
           pl.BlockSpec((None, 1, D), lambda t: (layer, 0, 0)), row],
        out_specs=[row, pl.BlockSpec((1, D), lambda t: (0, 0))],
        out_shape=[jax.ShapeDtypeStruct((T, D), F32), jax.ShapeDtypeStruct((1, D), F32)],
        sem=("arbitrary",), operands=(*pieces, win, h, gain, dh_in), rider=rider)


def _grad_w_in(b, pieces, ns, tk, name, rider=None):
    T, D = b.shape
    n = len(pieces)
    nt = T // tk
    segs = _z_segments(pieces, ns)
    shards_of = [sorted({s for k, _, _, s, _, _ in segs if k == i}) for i in range(n)]

    def body(*refs):
        b_ref = refs[0]
        piece_refs = refs[1:1 + n]
        o_ref, acc_ref = refs[1 + n:]
        s = pl.program_id(0)
        t = pl.program_id(1)

        @pl.when(t == 0)
        def _():
            acc_ref[...] = jnp.zeros_like(acc_ref)

        for shard in range(N_CHIPS):
            @pl.when(s == shard)
            def _(shard=shard):
                cols = [piece_refs[k][:, a:e] for k, a, e, ss, _, _ in segs if ss == shard]
                dz = cols[0] if len(cols) == 1 else jnp.concatenate(cols, axis=1)
                acc_ref[...] += _dot_tn(b_ref[...], dz)

        @pl.when(t == nt - 1)
        def _():
            o_ref[...] = acc_ref[...].astype(BF16).reshape(1, D, ns)

    def piece_spec(i):
        def index(s, t):
            used = functools.reduce(jnp.logical_or, [s == ss for ss in shards_of[i]])
            return (jnp.where(used, t, 0), 0)
        return pl.BlockSpec((tk, pieces[i].shape[1]), index)

    return _call(
        body, name=name, grid=(N_CHIPS, nt),
        in_specs=[pl.BlockSpec((tk, D), lambda s, t: (t, 0))] + [piece_spec(i) for i in range(n)],
        out_specs=[pl.BlockSpec((1, D, ns), lambda s, t: (s, 0, 0))],
        out_shape=[jax.ShapeDtypeStruct((N_CHIPS, D, ns), BF16)],
        scratch=[pltpu.VMEM((D, ns), F32)],
        sem=("parallel", "arbitrary"), operands=(b, *pieces), rider=rider)[0]


def _sum_pair(gs, rs, c_idx, name):
    n = len(gs)

    def body(c_ref, *refs):
        for g_ref, r_ref, o_ref in zip(refs[:n], refs[n:2 * n], refs[2 * n:]):
            o_ref[...] = (g_ref[...].astype(F32) + r_ref[...].astype(F32)).astype(BF16)

    halves = [pl.BlockSpec((None,) + r.shape[1:], lambda s, c_ref: (s, 0, 0)) for r in rs]
    return pl.pallas_call(
        body,
        name=name,
        grid_spec=pltpu.PrefetchScalarGridSpec(
            num_scalar_prefetch=1,
            grid=(N_CHIPS,),
            in_specs=[pl.BlockSpec((None,) + r.shape[1:], lambda s, c_ref: (s, c_ref[0], 0)) for r in rs] + halves,
            out_specs=halves,
        ),
        out_shape=[jax.ShapeDtypeStruct(r.shape, BF16) for r in rs],
        compiler_params=_params(("parallel",)),
    )(c_idx, *gs, *rs)


def _sum_chips(ps, rs, pos, name):
    n = len(ps)
    quarters = 4

    def body(pos_ref, *refs):
        chip = pos_ref[0]
        for p_ref, r_ref, o_ref in zip(refs[:n], refs[n:2 * n], refs[2 * n:]):
            own = p_ref[...].astype(F32)
            terms = [jnp.where(chip == k, own, r_ref[k].astype(F32)) for k in range(N_CHIPS)]
            o_ref[...] = ((terms[0] + terms[1]) + terms[2]) + terms[3]

    def rows(r):
        assert r.shape[1] % (quarters * BF16_ROWS) == 0, r.shape
        return r.shape[1] // quarters

    return pl.pallas_call(
        body,
        name=name,
        grid_spec=pltpu.PrefetchScalarGridSpec(
            num_scalar_prefetch=1,
            grid=(quarters,),
            in_specs=[pl.BlockSpec((None, rows(r), r.shape[2]), lambda q, pos_ref: (pos_ref[0], q, 0)) for r in rs]
            + [pl.BlockSpec((N_CHIPS, rows(r), r.shape[2]), lambda q, pos_ref: (0, q, 0)) for r in rs],
            out_specs=[pl.BlockSpec((rows(r), r.shape[2]), lambda q, pos_ref: (pos_ref[1] * quarters + q, 0))
                       for r in rs],
        ),
        out_shape=[jax.ShapeDtypeStruct((2 * r.shape[1], r.shape[2]), F32) for r in rs],
        compiler_params=_params(("arbitrary",)),
    )(pos, *ps, *rs)


def _small_all_reduce(p, rider=None):
    rows, width = p.shape
    r = 0 if rider is None else len(rider.ins)

    def body(*refs):
        p_ref, o_ref = refs[0], refs[1 + r]
        sib_ref, slot_ref, ssem, rsem = refs[2 + 2 * r:6 + 2 * r]
        if rider is not None:
            rider.start(refs[1:1 + r], refs[2 + r:2 + 2 * r], refs[6 + 2 * r], refs[7 + 2 * r])
        reduce(p_ref, o_ref, sib_ref, slot_ref, ssem, rsem)
        if rider is not None:
            rider.finish(refs[1:1 + r], refs[2 + r:2 + 2 * r], refs[6 + 2 * r], refs[7 + 2 * r])

    def reduce(p_ref, o_ref, sib_ref, slot_ref, ssem, rsem):
        x, y, c, chip, others = _mesh_pos()
        pair = _remote(p_ref, sib_ref, ssem.at[0], rsem.at[0], (x, y, 1 - c))
        pair.start()
        pair.wait()
        slot_ref[chip] = p_ref[...] + sib_ref[...]
        sends = []
        for j, (ox, oy) in enumerate(others):
            cp = _remote(slot_ref.at[chip], slot_ref.at[chip], ssem.at[1 + j], rsem.at[1 + j], (ox, oy, c))
            cp.start()
            sends.append(cp)
        for j, (ox, oy) in enumerate(others):
            slot = slot_ref.at[2 * ox + oy]
            _remote(slot, slot, ssem.at[1 + j], rsem.at[1 + j], (ox, oy, c)).wait_recv()
        for cp in sends:
            cp.wait_send()
        o_ref[...] = ((slot_ref[0] + slot_ref[1]) + slot_ref[2]) + slot_ref[3]

    vmem = pl.BlockSpec(memory_space=pltpu.VMEM)
    scratch = [pltpu.VMEM((rows, width), F32), pltpu.VMEM((N_CHIPS, rows, width), F32),
               pltpu.SemaphoreType.DMA((4,)), pltpu.SemaphoreType.DMA((4,))]
    if rider is not None:
        scratch += [pltpu.SemaphoreType.DMA((rider.n_sem,)), pltpu.SemaphoreType.DMA((rider.n_sem,))]
    outs = pl.pallas_call(
        body,
        name="small_grads_all_reduce",
        in_specs=[vmem] + [ANY] * r,
        out_specs=[vmem] + [ANY] * r,
        out_shape=[jax.ShapeDtypeStruct(p.shape, F32)] + ([] if rider is None else list(rider.out_shapes)),
        input_output_aliases={} if rider is None else rider.aliases(1, 1),
        scratch_shapes=scratch,
    )(p, *([] if rider is None else rider.ins))
    if rider is not None:
        rider.results = outs[1:]
    return outs[0]


def _adamw(gs, w, m, v, name):
    L, R, C = w.shape
    Ct = gs[0].shape[1]
    tr = _pick_tile(R, 256, 8)

    def body(*refs):
        g_refs = refs[:L]
        w_ref, m_ref, v_ref, go_ref, d_ref, mo_ref, vo_ref = refs[L:]
        layer = pl.program_id(0)
        grad = g_refs[L - 1][...]
        for i in range(L - 2, -1, -1):
            grad = jnp.where(layer == i, g_refs[i][...], grad)
        if Ct != C:
            grad = grad[:, :C]
        m_new = ADAM_B1 * m_ref[...] + (1.0 - ADAM_B1) * grad
        v_new = ADAM_B2 * v_ref[...] + (1.0 - ADAM_B2) * jnp.square(grad)
        m_hat = m_new / (1.0 - ADAM_B1 ** ADAM_STEP)
        v_hat = v_new / (1.0 - ADAM_B2 ** ADAM_STEP)
        go_ref[...] = grad
        d_ref[...] = -ADAM_LR * (m_hat / (jnp.sqrt(v_hat) + ADAM_EPS) + ADAM_WD * w_ref[...])
        mo_ref[...] = m_new
        vo_ref[...] = v_new

    g_specs = [pl.BlockSpec((tr, Ct), functools.partial(lambda l, r, i: (jnp.where(l == i, r, 0), 0), i=i))
               for i in range(L)]
    blk = pl.BlockSpec((None, tr, C), lambda l, r: (l, r, 0))
    return _call(
        body, name=name, grid=(L, R // tr),
        in_specs=g_specs + [blk, blk, blk],
        out_specs=[blk] * 4,
        out_shape=[jax.ShapeDtypeStruct((L, R, C), F32)] * 4,
        sem=("arbitrary", "arbitrary"), operands=(*gs, w, m, v))


_FFN1 = ("ffn1_gate", "ffn1_up", "ffn1_down")
_FFN2 = ("ffn2_gate", "ffn2_up", "ffn2_down")
_MIXW = ("w_ret_up", "w_pool_up", "w_out")
_BIG = _FFN1 + ("w_in",) + _MIXW + _FFN2
_TRANSPOSED = ("ffn1_gate", "ffn1_up", "ffn2_gate", "ffn2_up")
_SMALL = ("ffn1_norm", "mix_norm", "ffn2_norm", "final_norm", "pool_scale", "pool_maps")
_ORDER = ("meta", "ffn1_norm", "ffn1_gate", "ffn1_up", "ffn1_down", "mix_norm", "w_in", "pool_maps",
          "pool_scale", "w_ret_up", "w_pool_up", "w_out", "ffn2_norm", "ffn2_gate", "ffn2_up", "ffn2_down",
          "final_norm")


def _transport(a):
    n, r, c = a.shape
    out = a.astype(BF16)
    if c % LANES:
        out = jnp.concatenate([out, jnp.zeros((n, r, _round_up(c, LANES) - c), BF16)], axis=2)
    if r % LANES:
        out = jnp.concatenate([out, jnp.zeros((n, _round_up(r, LANES) - r, out.shape[2]), BF16)], axis=1)
    return out


def _pack_rows(parts, width):
    rows = [p.reshape(-1, width) for p in parts]
    total = sum(r.shape[0] for r in rows)
    fill = _round_up(total, 8) - total
    if fill:
        rows.append(jnp.zeros((fill, width), F32))
    return jnp.concatenate(rows, axis=0)


def _unpack_rows(packed, shapes, width):
    out, at = [], 0
    for shp in shapes:
        n = math.prod(shp) // width
        out.append(packed[at:at + n].reshape(shp))
        at += n
    return out


class _Weights:
    def __init__(self, shards):
        self.shards = shards
        self.full = {}

    def rider(self, keys):
        r = _gather_rider([(self.shards[n], i) for n, i in keys])
        r.keys = keys
        return r

    def take(self, rider):
        for key, arr in zip(rider.keys, rider.results):
            self.full[key] = arr

    def __call__(self, name, layer):
        return self.full[(name, layer)]


def _local_step(x, meta_full, tgt, w, wts, pad, tm, cg, reducer):
    D = x.shape[1]
    T = pad + N_META + x.shape[0]
    L = w["ffn1_norm"].shape[0]
    pool_maps = w["pool_maps"]
    gains = {n: w[n].reshape(L, 1, D) for n in ("ffn1_norm", "mix_norm", "ffn2_norm")}
    scale3 = w["pool_scale"].reshape(L, 1, POOL_WIDTH)
    consts = _ret_consts(T, pad)
    tl = _pick_tile(T, 2 * tm, BF16_ROWS)
    def gather(keys):
        return wts.rider(keys) if keys and keys[0] not in wts.full else None

    def done(rider):
        if rider is not None:
            wts.take(rider)

    h = jnp.concatenate([jnp.zeros((pad, D), F32), meta_full, x], axis=0)
    saved = []
    for i in range(L):
        s = {"h0": h}
        if ("ffn1_down", i) in wts.full:
            rd = gather([("w_in", i)] + [(n, i) for n in _MIXW])
            h, s["a1"], s["g1"], s["u1"], s["act1"] = _ffn_fwd(
                h, gains["ffn1_norm"], wts("ffn1_gate", i), wts("ffn1_up", i), wts("ffn1_down", i), i, tl,
                f"ffn1_fwd_{i}", rd)
            done(rd)
        else:
            rd = gather([("ffn1_down", i), ("w_in", i)])
            s["a1"], s["g1"], s["u1"], s["act1"] = _ffn_fwd_up(
                h, gains["ffn1_norm"], wts("ffn1_gate", i), wts("ffn1_up", i), i, tl, f"ffn1_fwd_up_{i}", rd)
            done(rd)
            rd = gather([(n, i) for n in _MIXW])
            h = _ffn_fwd_down(h, s["act1"], wts("ffn1_down", i), tl, f"ffn1_fwd_down_{i}", rd)
            done(rd)
        s["h1"] = h
        rd = gather([("ffn2_gate", i), ("ffn2_up", i)])
        s["z"], s["b"] = _inproj_fwd(h, gains["mix_norm"], wts("w_in", i), i, tl, f"inproj_fwd_{i}", rd)
        done(rd)
        s["r"], s["o_pre"], s["s_all"] = _ret_fwd(s["z"], consts, cg, f"retention_fwd_{i}")
        s["pm"] = _pool_fwd(s["z"], pool_maps, scale3, i, pad, f"pool_fwd_{i}")
        rd = gather([("ffn2_down", i)])
        h, s["mixed"], s["ret"], s["pool"] = _mix_fwd(
            h, s["r"], s["pm"], s["z"], wts("w_ret_up", i), wts("w_pool_up", i), wts("w_out", i), tl,
            f"mix_fwd_{i}", rd)
        done(rd)
        s["h2"] = h
        rd = gather([(n, i + 1) for n in _FFN1]) if i + 1 < L else None
        h, s["a2"], s["g2"], s["u2"], s["act2"] = _ffn_fwd(
            h, gains["ffn2_norm"], wts("ffn2_gate", i), wts("ffn2_up", i), wts("ffn2_down", i), i, tl,
            f"ffn2_fwd_{i}", rd)
        done(rd)
        saved.append(s)

    dh, loss_acc, d_final = _final_loss(h, w["final_norm"].reshape(1, D), tgt, "final_norm_loss")

    small = {n: [None] * L for n in ("ffn1_norm", "mix_norm", "ffn2_norm", "pool_scale", "pool_maps")}

    carry = {"ffn_act": 1.0, "ffn_in": 2.2, "mix_bwd": 1.0, "inproj_bwd": 1.5, "w_in": 1.0}

    tk = _pick_tile(T, 1408, LANES)

    def grad(n, a, b, i, mode):
        rd = reducer.rider(carry.get(n, 1.0 if i == 0 and n.startswith("ffn") else 0.5))
        reducer.add(n, i, _grad_tn(a, b, mode, 1.0, tk, f"grad_{n}_{i}", rd))
        reducer.done(rd)

    def ffn_bwd(which, dy, h_in, g, u, i, between=None, units=carry["ffn_in"]):
        rd = reducer.rider(carry["ffn_act"])
        dg, du, dyh = _ffn_bwd_act(dy, g, u, wts(f"{which}_down", i), tl, f"{which}_bwd_act_{i}", rd)
        reducer.done(rd)
        if between is not None:
            between(dg, du, dyh)
        rd = reducer.rider(units)
        dh_in, dgain = _ffn_bwd_in(dy, h_in, gains[f"{which}_norm"], dg, du, wts(f"{which}_gate", i),
                                   wts(f"{which}_up", i), i, tl, pad, f"{which}_bwd_in_{i}", rd)
        reducer.done(rd)
        return dh_in, dg, du, dgain, dyh

    for i in reversed(range(L)):
        s = saved[i]
        dh, dg, du, small["ffn2_norm"][i], dyh = ffn_bwd("ffn2", dh, s["h2"], s["g2"], s["u2"], i)
        grad("ffn2_gate", dg, s["a2"], i, "row")
        grad("ffn2_up", du, s["a2"], i, "row")
        grad("ffn2_down", s["act2"], dyh, i, "row")
        reducer.stage(f"ffn2_{i}")
        rd = reducer.rider(carry["mix_bwd"])
        dgab, dret, dpool, dr, dpm = _mix_bwd_dx(
            dh, s["z"], s["ret"], s["pool"], wts("w_out", i), wts("w_ret_up", i), wts("w_pool_up", i), tm,
            f"mix_bwd_{i}", rd)
        reducer.done(rd)
        rd = reducer.rider(0.5)
        g_out, g_ru, g_pu = _grad_mix(s["mixed"], dh, s["r"], dret, s["pm"], dpool, _pick_tile(T, 704, LANES),
                                      f"grad_mix_{i}", rd)
        reducer.done(rd)
        for n, g_n in (("w_out", g_out), ("w_ret_up", g_ru), ("w_pool_up", g_pu)):
            reducer.add(n, i, g_n)
        du_pool, small["pool_maps"][i], small["pool_scale"][i] = _pool_bwd(
            s["z"], dpm, pool_maps, scale3, i, pad, f"pool_bwd_{i}")
        dq, dgr, dkp, dvp, ds = _ret_bwd_local(s["z"], s["o_pre"], s["s_all"], dr, consts,
                                               _pick_tile(T // CHUNK, 11, 1), f"retention_bwd_{i}")
        dk, dv = _ret_bwd_state(s["z"], dkp, dvp, ds, consts, cg, f"retention_bwd_state_{i}")
        dz = [dq, dk, dv, dgr, du_pool, dgab]
        dh2 = dh
        rd = reducer.rider(carry["inproj_bwd"])
        dh, small["mix_norm"][i] = _inproj_bwd_dx(
            dz, wts("w_in", i), s["h1"], gains["mix_norm"], dh2, i, tm, pad, f"inproj_bwd_{i}", rd)
        reducer.done(rd)
        rd = reducer.rider(carry["w_in"])
        reducer.add("w_in", i, _grad_w_in(s["b"], dz, wts("w_in", i).shape[-1], tk, f"grad_w_in_{i}", rd))
        reducer.done(rd)
        reducer.stage(f"mid{i}")
        def ffn1_grads(dg, du, dyh, i=i, s=s):
            grad("ffn1_gate", dg, s["a1"], i, "row")
            if i == 0:
                reducer.stage("gate0")
            grad("ffn1_up", du, s["a1"], i, "row")
            if i == 0:
                reducer.stage("up0")
            grad("ffn1_down", s["act1"], dyh, i, "row")
            reducer.stage(f"end{i}")

        if i == 0:
            dh, _, _, small["ffn1_norm"][i], _ = ffn_bwd("ffn1", dh, s["h0"], s["g1"], s["u1"], i, ffn1_grads, 2.5)
        else:
            dh, dg, du, small["ffn1_norm"][i], dyh = ffn_bwd("ffn1", dh, s["h0"], s["g1"], s["u1"], i)
            ffn1_grads(dg, du, dyh)

    return loss_acc, dh, small, d_final


class _Reducer:
    def __init__(self, unit):
        self.c_idx = lax.axis_index("c").astype(jnp.int32).reshape(1)
        chip = 2 * lax.axis_index("x") + lax.axis_index("y")
        self.pos = jnp.stack([chip, lax.axis_index("c")]).astype(jnp.int32)
        self.pending, self.stages, self.queue, self.halves, self.whole = [], [], [], {}, {}
        self.unit = unit
        self.calls = 0

    def add(self, name, layer, g):
        self.pending.append(((name, layer), g))

    def stage(self, tag):
        if self.pending:
            self.stages.append((tag, self.pending))
            self.pending = []

    def _pair_rider(self):
        if not self.stages:
            return None
        tag, items = self.stages.pop(0)
        rd = _pair_exchange_rider([g for _, g in items])
        rd.tag, rd.keys = tag, [k for k, _ in items]
        return rd

    def _chip_rider(self, units):
        take, keep, size = [], [], 0
        for item in self.queue:
            if units is None or size + item[1].size <= units * self.unit:
                take.append(item)
                size += item[1].size
            else:
                keep.append(item)
        self.queue = keep
        if not take:
            return None
        rd = _chip_exchange_rider([p for _, p in take])
        rd.keys = [k for k, _ in take]
        return rd

    def _gather_rider(self):
        keys = [k for k in self.halves if k not in self.whole]
        if not keys:
            return None
        rd = _pair_gather_rider([self.halves[k] for k in keys])
        rd.keys = keys
        return rd

    def rider(self, units):
        self.riding = (self._pair_rider(), self._chip_rider(units), self._gather_rider())
        return _join(self.riding)

    def done(self, rd):
        if rd is None:
            return
        _split_results(rd)
        pair, chips, gather = self.riding
        if len([r for r in self.riding if r is not None]) == 1:
            (pair or chips or gather).results = rd.results
        self.calls += 1
        if gather is not None:
            self.whole.update(zip(gather.keys, gather.results))
        if pair is not None:
            sums = _sum_pair(pair.ins, pair.results, self.c_idx, f"sum_pair_{pair.tag}")
            self.queue += list(zip(pair.keys, sums))
        if chips is not None:
            sums = _sum_chips(chips.ins, chips.results, self.pos, f"sum_chips_{self.calls}")
            self.halves.update(zip(chips.keys, sums))

    def busy(self):
        assert not self.pending
        return bool(self.stages or self.queue or len(self.whole) < len(self.halves))

    def flush(self):
        self.riding = (self._pair_rider(), self._chip_rider(None), self._gather_rider())
        rd = _join(self.riding)
        _run_rider(rd, f"grads_exchange_tail_{self.calls}")
        self.done(rd)


def _update(loss_acc, grad_x, d_meta_rows, reducer, small, d_final, w, mom, var):
    meta = w["meta"]
    D = w["final_norm"].shape[0]
    L = w["ffn1_norm"].shape[0]
    Dq = D // N_CHIPS

    out = {}

    small_parts = [jnp.concatenate(small[n], axis=0) for n in ("ffn1_norm", "mix_norm", "ffn2_norm")]
    small_parts += [d_final, jnp.concatenate(small["pool_scale"], axis=0), jnp.concatenate(small["pool_maps"], axis=0)]
    loss_row = jnp.pad(loss_acc, ((0, 0), (0, D - loss_acc.shape[1])))
    rd = reducer.rider(None) if reducer.busy() else None
    reduced = _small_all_reduce(_pack_rows(small_parts + [d_meta_rows, loss_row], D), rd)
    reducer.done(rd)
    while reducer.busy():
        reducer.flush()
    for n in _BIG:
        gs = [reducer.whole[(n, i)] for i in range(L)]
        if n in _TRANSPOSED:
            res = _adamw(gs, *(jnp.swapaxes(t[n], 1, 2) for t in (w, mom, var)), f"adamw_{n}")
            out[n] = [jnp.swapaxes(r, 1, 2) for r in res]
        else:
            out[n] = _adamw(gs, w[n], mom[n], var[n], f"adamw_{n}")

    small_shapes = [w[n].shape for n in _SMALL]
    small_rows = sum(math.prod(shp) for shp in small_shapes) // D
    chip = 2 * lax.axis_index("x") + lax.axis_index("y")
    d_meta = lax.dynamic_slice_in_dim(reduced[small_rows:small_rows + N_META], chip * Dq, Dq, axis=1)
    names = _SMALL + ("meta",)
    packed_g = _pack_rows([reduced[:small_rows], d_meta], D)
    packed = [_pack_rows([t[n] for n in names], D) for t in (w, mom, var)]
    res = _adamw([packed_g], packed[0][None], packed[1][None], packed[2][None], "adamw_small")
    shapes = small_shapes + [meta.shape]
    unpacked = [_unpack_rows(r[0], shapes, D) for r in res]
    for k, n in enumerate(names):
        out[n] = tuple(u[k] for u in unpacked)

    loss = reduced[small_rows + N_META, 0]
    return (loss, grad_x) + tuple(out[n][j] for j in range(4) for n in _ORDER)


def kernel(x, meta, ffn1_norm, ffn1_gate, ffn1_up, ffn1_down, mix_norm, w_in, pool_maps, pool_scale, w_ret_up, w_pool_up, w_out, ffn2_norm, ffn2_gate, ffn2_up, ffn2_down, final_norm, loss_target, m_meta, m_ffn1_norm, m_ffn1_gate, m_ffn1_up, m_ffn1_down, m_mix_norm, m_w_in, m_pool_maps, m_pool_scale, m_w_ret_up, m_w_pool_up, m_w_out, m_ffn2_norm, m_ffn2_gate, m_ffn2_up, m_ffn2_down, m_final_norm, v_meta, v_ffn1_norm, v_ffn1_gate, v_ffn1_up, v_ffn1_down, v_mix_norm, v_w_in, v_pool_maps, v_pool_scale, v_w_ret_up, v_w_pool_up, v_w_out, v_ffn2_norm, v_ffn2_gate, v_ffn2_up, v_ffn2_down, v_final_norm):
    args = dict(locals())
    w = {n: args[n] for n in _ORDER}
    mom = {n: args["m_" + n] for n in _ORDER}
    var = {n: args["v_" + n] for n in _ORDER}

    assert x.shape[0] == 1, "one batch element per device"
    seq, D = x.shape[1], x.shape[2]
    assert seq % CHUNK == 0 and D % RET_WIDTH == 0 and (2 * POOL_WIDTH) % D == 0
    pad = (-(seq + N_META)) % CHUNK
    T = seq + N_META + pad
    tm = _pick_tile(T, 528, BF16_ROWS)
    cg = _pick_tile(T // CHUNK, 33, 1)

    shards = {n: _transport(w[n]) for n in _BIG}
    shards["meta"] = meta[None]
    wts = _Weights(shards)
    head = wts.rider([("ffn1_gate", 0), ("ffn1_up", 0), ("meta", 0)])
    _run_rider(head, "weights_gather_head")
    wts.take(head)
    meta_full = jnp.transpose(wts("meta", 0), (1, 0, 2)).reshape(N_META, D)

    reducer = _Reducer(unit=2 * shards["ffn1_gate"][0].size)
    loss_acc, dh, small, d_final = _local_step(x[0], meta_full, loss_target[0], w, wts, pad, tm, cg, reducer)
    grad_x = dh[pad + N_META:][None]
    return _update(loss_acc, grad_x, dh[pad:pad + N_META], reducer, small, d_final, w, mom, var)
```

```python
import functools
import math

import jax
import jax.numpy as jnp
from jax import lax
from jax.experimental import pallas as pl
from jax.experimental.pallas import tpu as pltpu

F32 = jnp.float32
BF16 = jnp.bfloat16

N_META = 16
RET_HEADS = 4
HEAD_DIM = 128
RET_WIDTH = RET_HEADS * HEAD_DIM
POOL_WINDOWS = (2, 4, 8, 16)
POOL_GROUPS = len(POOL_WINDOWS)
POOL_WIDTH = POOL_GROUPS * HEAD_DIM
CHUNK = 128
ROPE_BASE = 10000.0
EPS = 1e-6
ADAM_LR = 0.001
ADAM_B1 = 0.9
ADAM_B2 = 0.999
ADAM_EPS = 1e-08
ADAM_WD = 0.01
ADAM_STEP = 10

N_CHIPS = 4
LANES = 128
BF16_ROWS = 16
V7X_VMEM_LIMIT = 52 * 1024 * 1024
MESH = pl.DeviceIdType.MESH
ANY = pl.BlockSpec(memory_space=pl.ANY)


def _round_up(n, m):
    return -(-n // m) * m


def _pick_tile(n, target, mult):
    best = None
    for d in range(mult, min(n, target) + 1, mult):
        if n % d == 0:
            best = d
    assert best is not None, (n, target, mult)
    return best


def _params(sem=None):
    return pltpu.CompilerParams(dimension_semantics=sem, vmem_limit_bytes=V7X_VMEM_LIMIT)


def _dot(a, b):
    return jnp.dot(a, b, preferred_element_type=F32)


def _dot_nt(a, b):
    return lax.dot_general(a, b, (((1,), (1,)), ((), ())), preferred_element_type=F32)


def _dot_tn(a, b):
    return lax.dot_general(a, b, (((0,), (0,)), ((), ())), preferred_element_type=F32)


def _ein(spec, a, b):
    return jnp.einsum(spec, a, b, preferred_element_type=F32)


def _sigmoid(x):
    return jax.nn.sigmoid(x)


def _rms_fwd(x, gain):
    r = lax.rsqrt(jnp.mean(x * x, axis=-1, keepdims=True) + EPS)
    return x * r * gain


def _rms_bwd(x, gain, da):
    r = lax.rsqrt(jnp.mean(x * x, axis=-1, keepdims=True) + EPS)
    xh = x * r
    dgain = jnp.sum(da * xh, axis=0, keepdims=True)
    dxh = da * gain
    dx = r * (dxh - xh * jnp.mean(dxh * xh, axis=-1, keepdims=True))
    return dx, dgain


def _row_mask(t, tm, pad, shape):
    rows = t * tm + lax.broadcasted_iota(jnp.int32, shape, 0)
    return rows >= pad


def _mesh_pos():
    x, y, c = lax.axis_index("x"), lax.axis_index("y"), lax.axis_index("c")
    others = [(1 - x, y), (x, 1 - y), (1 - x, 1 - y)]
    return x, y, c, 2 * x + y, others


def _half_rows(c, rh):
    return pl.ds(pl.multiple_of(c * rh, rh), rh)


def _remote(src, dst, ssem, rsem, dev):
    return pltpu.make_async_remote_copy(src_ref=src, dst_ref=dst, send_sem=ssem, recv_sem=rsem,
                                        device_id=dev, device_id_type=MESH)


class _Rider:
    def __init__(self, ins, out_shapes, n_sem, start, finish, in_place=False):
        self.ins, self.out_shapes, self.n_sem, self.start, self.finish = ins, out_shapes, n_sem, start, finish
        self.in_place = [in_place] * len(ins)
        self.results = None

    def aliases(self, first_in, first_out):
        return {first_in + i: first_out + i for i, same in enumerate(self.in_place) if same}


class _SemWindow:
    def __init__(self, ref, base):
        self.ref, self.base = ref, base

    @property
    def at(self):
        return self

    def __getitem__(self, k):
        return self.ref.at[self.base + k]


def _join(riders):
    riders = [r for r in riders if r is not None]
    if len(riders) <= 1:
        return riders[0] if riders else None

    def run(which):
        def go(ins, outs, ssem, rsem):
            at, sem = 0, 0
            for r in riders:
                n = len(r.ins)
                getattr(r, which)(ins[at:at + n], outs[at:at + n], _SemWindow(ssem, sem), _SemWindow(rsem, sem))
                at, sem = at + n, sem + r.n_sem
        return go

    joined = _Rider(sum([list(r.ins) for r in riders], []), sum([list(r.out_shapes) for r in riders], []),
                    sum(r.n_sem for r in riders), run("start"), run("finish"))
    joined.in_place = sum([r.in_place for r in riders], [])
    joined.parts = riders
    return joined


def _split_results(rider):
    at = 0
    for r in getattr(rider, "parts", []):
        r.results = rider.results[at:at + len(r.ins)]
        at += len(r.ins)


def _gather_rider(pieces):
    per = 7
    layers = [layer for _, layer in pieces]

    def first_copies(ins, outs, ssem, rsem):
        x, y, c, chip, others = _mesh_pos()
        copies = []
        for i, layer in enumerate(layers):
            mine = _half_rows(c, ins[i].shape[1] // 2)
            for j, (ox, oy) in enumerate(others):
                copies.append(_remote(ins[i].at[layer, mine, :], outs[i].at[chip, mine, :],
                                      ssem.at[per * i + j], rsem.at[per * i + j], (ox, oy, c)))
            copies.append(_remote(ins[i].at[layer], outs[i].at[chip],
                                  ssem.at[per * i + 6], rsem.at[per * i + 6], (x, y, 1 - c)))
        return copies

    def start(ins, outs, ssem, rsem):
        for cp in first_copies(ins, outs, ssem, rsem):
            cp.start()

    def finish(ins, outs, ssem, rsem):
        x, y, c, chip, others = _mesh_pos()
        sibling = (x, y, 1 - c)
        forwards = []
        for i in range(len(layers)):
            mine = _half_rows(c, ins[i].shape[1] // 2)
            for j, (ox, oy) in enumerate(others):
                rows = outs[i].at[2 * ox + oy, mine, :]
                _remote(rows, rows, ssem.at[per * i + j], rsem.at[per * i + j], (ox, oy, c)).wait_recv()
                fwd = _remote(rows, rows, ssem.at[per * i + 3 + j], rsem.at[per * i + 3 + j], sibling)
                fwd.start()
                forwards.append(fwd)
        for i in range(len(layers)):
            theirs = _half_rows(1 - c, ins[i].shape[1] // 2)
            for j, (ox, oy) in enumerate(others):
                rows = outs[i].at[2 * ox + oy, theirs, :]
                _remote(rows, rows, ssem.at[per * i + 3 + j], rsem.at[per * i + 3 + j], sibling).wait_recv()
            own = outs[i].at[chip]
            _remote(own, own, ssem.at[per * i + 6], rsem.at[per * i + 6], sibling).wait_recv()
        for cp in first_copies(ins, outs, ssem, rsem) + forwards:
            cp.wait_send()

    shapes = [jax.ShapeDtypeStruct((N_CHIPS,) + s.shape[1:], s.dtype) for s, _ in pieces]
    return _Rider([s for s, _ in pieces], shapes, per * len(pieces), start, finish)


def _chip_exchange_rider(ps):
    def copies(ins, outs, ssem, rsem):
        x, y, c, chip, others = _mesh_pos()
        return [_remote(ins[i].at[2 * ox + oy], outs[i].at[chip], ssem.at[3 * i + j], rsem.at[3 * i + j], (ox, oy, c))
                for i in range(len(ps)) for j, (ox, oy) in enumerate(others)]

    def start(ins, outs, ssem, rsem):
        for cp in copies(ins, outs, ssem, rsem):
            cp.start()

    def finish(ins, outs, ssem, rsem):
        x, y, c, chip, others = _mesh_pos()
        for i in range(len(ps)):
            for j, (ox, oy) in enumerate(others):
                slot = outs[i].at[2 * ox + oy]
                _remote(slot, slot, ssem.at[3 * i + j], rsem.at[3 * i + j], (ox, oy, c)).wait_recv()
        for cp in copies(ins, outs, ssem, rsem):
            cp.wait_send()

    return _Rider(list(ps), [jax.ShapeDtypeStruct(p.shape, p.dtype) for p in ps], 3 * len(ps), start, finish)


def _pair_exchange_rider(gs):
    def copies(ins, outs, ssem, rsem):
        x, y, c, _, _ = _mesh_pos()
        return [_remote(ins[i].at[:, _half_rows(1 - c, ins[i].shape[1] // 2), :], outs[i],
                        ssem.at[i], rsem.at[i], (x, y, 1 - c)) for i in range(len(gs))]

    def start(ins, outs, ssem, rsem):
        for cp in copies(ins, outs, ssem, rsem):
            cp.start()

    def finish(ins, outs, ssem, rsem):
        for cp in copies(ins, outs, ssem, rsem):
            cp.wait()

    shapes = [jax.ShapeDtypeStruct((g.shape[0], g.shape[1] // 2, g.shape[2]), g.dtype) for g in gs]
    return _Rider(list(gs), shapes, len(gs), start, finish)


def _run_rider(rider, name):
    def body(*refs):
        n = len(rider.ins)
        ins, outs = refs[:n], refs[n:2 * n]
        ssem, rsem = refs[2 * n:]
        rider.start(ins, outs, ssem, rsem)
        rider.finish(ins, outs, ssem, rsem)

    rider.results = pl.pallas_call(
        body,
        name=name,
        in_specs=[ANY] * len(rider.ins),
        out_specs=[ANY] * len(rider.ins),
        out_shape=rider.out_shapes,
        input_output_aliases=rider.aliases(0, 0),
        scratch_shapes=[pltpu.SemaphoreType.DMA((rider.n_sem,)), pltpu.SemaphoreType.DMA((rider.n_sem,))],
    )(*rider.ins)
    return rider.results


def _pair_gather_rider(fs):
    n = len(fs)

    def copies(outs, ssem, rsem):
        x, y, c, _, _ = _mesh_pos()
        halves = [outs[i].at[_half_rows(c, outs[i].shape[0] // 2), :] for i in range(n)]
        return [_remote(h, h, ssem.at[i], rsem.at[i], (x, y, 1 - c)) for i, h in enumerate(halves)]

    def start(ins, outs, ssem, rsem):
        for cp in copies(outs, ssem, rsem):
            cp.start()

    def finish(ins, outs, ssem, rsem):
        x, y, c, _, _ = _mesh_pos()
        for i in range(n):
            theirs = outs[i].at[_half_rows(1 - c, outs[i].shape[0] // 2), :]
            _remote(theirs, theirs, ssem.at[i], rsem.at[i], (x, y, 1 - c)).wait_recv()
        for cp in copies(outs, ssem, rsem):
            cp.wait_send()

    return _Rider(list(fs), [jax.ShapeDtypeStruct(f.shape, f.dtype) for f in fs], n, start, finish, in_place=True)


def _call(body, *, name, grid, in_specs, out_specs, out_shape, operands, scratch=(), sem=None, rider=None):
    if rider is None:
        return pl.pallas_call(
            body, name=name, grid=grid, in_specs=in_specs, out_specs=out_specs, out_shape=out_shape,
            scratch_shapes=list(scratch), compiler_params=_params(sem))(*operands)
    n_in, n_out, n_sc, r = len(in_specs), len(out_specs), len(scratch), len(rider.ins)

    def carrying(*refs):
        a, b = n_in, n_in + r
        c, d = b + n_out, b + n_out + r
        e = d + n_sc
        ids = [pl.program_id(k) for k in range(len(grid))]
        first = functools.reduce(jnp.logical_and, [i == 0 for i in ids])
        last = functools.reduce(jnp.logical_and, [i == g - 1 for i, g in zip(ids, grid)])

        @pl.when(first)
        def _():
            rider.start(refs[a:b], refs[c:d], refs[e], refs[e + 1])

        body(*refs[:a], *refs[b:c], *refs[d:e])

        @pl.when(last)
        def _():
            rider.finish(refs[a:b], refs[c:d], refs[e], refs[e + 1])

    outs = pl.pallas_call(
        carrying, name=name, grid=grid,
        in_specs=list(in_specs) + [ANY] * r,
        out_specs=list(out_specs) + [ANY] * r,
        out_shape=list(out_shape) + list(rider.out_shapes),
        scratch_shapes=list(scratch) + [pltpu.SemaphoreType.DMA((rider.n_sem,)), pltpu.SemaphoreType.DMA((rider.n_sem,))],
        input_output_aliases=rider.aliases(n_in, n_out),
        compiler_params=_params(("arbitrary",) * len(grid)),
    )(*operands, *rider.ins)
    rider.results = outs[n_out:]
    return outs[:n_out]


def _ffn_fwd(h, gain, wg, wu, wd, layer, tm, name, rider=None):
    T, D = h.shape
    Fs = wg.shape[-1]
    F = N_CHIPS * Fs

    def body(h_ref, g_ref, wg_ref, wu_ref, wd_ref, ho_ref, a_ref, go_ref, uo_ref, act_ref, acc_ref):
        s = pl.program_id(1)

        @pl.when(s == 0)
        def _():
            a_ref[...] = _rms_fwd(h_ref[...], g_ref[...]).astype(BF16)
            acc_ref[...] = jnp.zeros_like(acc_ref)

        a = a_ref[...]
        g = _dot(a, wg_ref[...])
        u = _dot(a, wu_ref[...])
        sg = _sigmoid(g)
        act = (g * sg * u).astype(BF16)
        go_ref[...] = (u * (sg * (1.0 + g * (1.0 - sg)))).astype(BF16)
        uo_ref[...] = (g * sg).astype(BF16)
        act_ref[...] = act
        acc_ref[...] += _dot(act, wd_ref[...])

        @pl.when(s == N_CHIPS - 1)
        def _():
            ho_ref[...] = h_ref[...] + 0.5 * acc_ref[...]

    row = pl.BlockSpec((tm, D), lambda t, s: (t, 0))
    col = pl.BlockSpec((tm, Fs), lambda t, s: (t, s))
    wcol = pl.BlockSpec((None, D, Fs), lambda t, s: (s, 0, 0))
    return _call(
        body, name=name, grid=(T // tm, N_CHIPS),
        in_specs=[row, pl.BlockSpec((None, 1, D), lambda t, s: (layer, 0, 0)), wcol, wcol,
                  pl.BlockSpec((None, Fs, D), lambda t, s: (s, 0, 0))],
        out_specs=[row, row, col, col, col],
        out_shape=[jax.ShapeDtypeStruct((T, D), F32), jax.ShapeDtypeStruct((T, D), BF16)]
        + [jax.ShapeDtypeStruct((T, F), BF16)] * 3,
        scratch=[pltpu.VMEM((tm, D), F32)],
        sem=("parallel", "arbitrary"), operands=(h, gain, wg, wu, wd), rider=rider)


def _ffn_fwd_up(h, gain, wg, wu, layer, tm, name, rider=None):
    T, D = h.shape
    Fs = wg.shape[-1]
    F = N_CHIPS * Fs

    def body(h_ref, g_ref, wg_ref, wu_ref, a_ref, go_ref, uo_ref, act_ref):
        @pl.when(pl.program_id(1) == 0)
        def _():
            a_ref[...] = _rms_fwd(h_ref[...], g_ref[...]).astype(BF16)

        a = a_ref[...]
        g = _dot(a, wg_ref[...])
        u = _dot(a, wu_ref[...])
        sg = _sigmoid(g)
        act_ref[...] = (g * sg * u).astype(BF16)
        go_ref[...] = (u * (sg * (1.0 + g * (1.0 - sg)))).astype(BF16)
        uo_ref[...] = (g * sg).astype(BF16)

    row = pl.BlockSpec((tm, D), lambda t, s: (t, 0))
    col = pl.BlockSpec((tm, Fs), lambda t, s: (t, s))
    wcol = pl.BlockSpec((None, D, Fs), lambda t, s: (s, 0, 0))
    return _call(
        body, name=name, grid=(T // tm, N_CHIPS),
        in_specs=[row, pl.BlockSpec((None, 1, D), lambda t, s: (layer, 0, 0)), wcol, wcol],
        out_specs=[row, col, col, col],
        out_shape=[jax.ShapeDtypeStruct((T, D), BF16)] + [jax.ShapeDtypeStruct((T, F), BF16)] * 3,
        sem=("parallel", "arbitrary"), operands=(h, gain, wg, wu), rider=rider)


def _ffn_fwd_down(h, act, wd, tm, name, rider=None):
    T, D = h.shape
    Fs = wd.shape[1]

    def body(h_ref, act_ref, wd_ref, ho_ref, acc_ref):
        s = pl.program_id(1)

        @pl.when(s == 0)
        def _():
            acc_ref[...] = jnp.zeros_like(acc_ref)

        acc_ref[...] += _dot(act_ref[...], wd_ref[...])

        @pl.when(s == N_CHIPS - 1)
        def _():
            ho_ref[...] = h_ref[...] + 0.5 * acc_ref[...]

    row = pl.BlockSpec((tm, D), lambda t, s: (t, 0))
    return _call(
        body, name=name, grid=(T // tm, N_CHIPS),
        in_specs=[row, pl.BlockSpec((tm, Fs), lambda t, s: (t, s)), pl.BlockSpec((None, Fs, D), lambda t, s: (s, 0, 0))],
        out_specs=[row],
        out_shape=[jax.ShapeDtypeStruct((T, D), F32)],
        scratch=[pltpu.VMEM((tm, D), F32)],
        sem=("parallel", "arbitrary"), operands=(h, act, wd), rider=rider)[0]


def _inproj_fwd(h, gain, win, layer, tm, name, rider=None):
    T, D = h.shape
    Ns = win.shape[-1]

    def body(h_ref, g_ref, w_ref, z_ref, b_ref):
        @pl.when(pl.program_id(1) == 0)
        def _():
            b_ref[...] = _rms_fwd(h_ref[...], g_ref[...]).astype(BF16)

        z_ref[...] = _dot(b_ref[...], w_ref[...]).astype(BF16)

    return _call(
        body, name=name, grid=(T // tm, N_CHIPS),
        in_specs=[pl.BlockSpec((tm, D), lambda t, s: (t, 0)),
                  pl.BlockSpec((None, 1, D), lambda t, s: (layer, 0, 0)),
                  pl.BlockSpec((None, D, Ns), lambda t, s: (s, 0, 0))],
        out_specs=[pl.BlockSpec((tm, Ns), lambda t, s: (t, s)), pl.BlockSpec((tm, D), lambda t, s: (t, 0))],
        out_shape=[jax.ShapeDtypeStruct((T, N_CHIPS * Ns), BF16), jax.ShapeDtypeStruct((T, D), BF16)],
        sem=("parallel", "arbitrary"), operands=(h, gain, win), rider=rider)


def _ret_consts(T, pad):
    half = HEAD_DIM // 2
    inv_freq = ROPE_BASE ** (-jnp.arange(half, dtype=F32) / half)
    pos = jnp.arange(T, dtype=F32) - pad
    ang = pos[:, None] * inv_freq[None, :]
    cos = jnp.cos(ang)
    sin = jnp.sin(ang)
    cosf = jnp.concatenate([cos, cos], axis=1)
    sinf = jnp.concatenate([-sin, sin], axis=1)
    log_gamma = jnp.log1p(-(2.0 ** (-5.0 - jnp.arange(RET_HEADS, dtype=F32))))
    idx = jnp.arange(CHUNK, dtype=F32)
    diff = idx[:, None] - idx[None, :]
    intra = jnp.where(diff[None] >= 0, jnp.exp(diff[None] * log_gamma[:, None, None]), 0.0)
    k_decay = jnp.exp((CHUNK - 1.0 - idx)[None, :] * log_gamma[:, None])
    q_decay = jnp.exp((idx + 1.0)[None, :] * log_gamma[:, None])
    chunk_decay = jnp.exp(CHUNK * log_gamma)
    kdec = jnp.broadcast_to(k_decay[:, :, None], (RET_HEADS, CHUNK, HEAD_DIM))
    qdec = jnp.broadcast_to(q_decay[:, :, None], (RET_HEADS, CHUNK, HEAD_DIM))
    cdb = jnp.broadcast_to(chunk_decay[:, None, None], (RET_HEADS, 8, HEAD_DIM))
    return cosf, sinf, intra, kdec, qdec, cdb


def _rot(t, cosv, sinv):
    return t * cosv + pltpu.roll(t, HEAD_DIM // 2, 1) * sinv


def _rot_t(g, cosv, sinv):
    return g * cosv + pltpu.roll(g * sinv, HEAD_DIM // 2, 1)


def _head_specs(tg, section, order):
    return pl.BlockSpec((tg, HEAD_DIM), lambda h, g: (order(g), section * RET_HEADS + h))


def _ret_fwd(z, consts, cg, name, rider=None):
    T = z.shape[0]
    N = T // CHUNK
    ng = N // cg
    tg = cg * CHUNK
    cosf, sinf, intra, kdec, qdec, cdb = consts
    fwd = lambda g: g

    def body(zq, zk, zv, zg, cos_ref, sin_ref, m_ref, kd_ref, qd_ref, cd_ref, r_ref, o_ref, s_ref, st_ref):
        @pl.when(pl.program_id(1) == 0)
        def _():
            st_ref[...] = jnp.zeros_like(st_ref)

        cosv = cos_ref[...]
        sinv = sin_ref[...]
        q3 = (_rot(zq[...].astype(F32), cosv, sinv) * (HEAD_DIM ** -0.5)).reshape(cg, CHUNK, HEAD_DIM)
        k3 = _rot(zk[...].astype(F32), cosv, sinv).reshape(cg, CHUNK, HEAD_DIM)
        vb = zv[...].reshape(cg, CHUNK, HEAD_DIM).astype(BF16)
        scores = _ein("ncd,nmd->ncm", q3.astype(BF16), k3.astype(BF16)) * m_ref[...][None]
        inner = _ein("ncm,nmd->ncd", scores.astype(BF16), vb)
        kv = _ein("ncd,nce->nde", (k3 * kd_ref[...][None]).astype(BF16), vb)
        cd = cd_ref[0:1, :]
        state = st_ref[...]
        for n in range(cg):
            s_ref[n] = state
            state = state * cd + kv[n]
        st_ref[...] = state
        qdb = (q3 * qd_ref[...][None]).astype(BF16)
        cross = _ein("ncd,nde->nce", qdb, s_ref[...].astype(BF16))
        out = (inner + cross).reshape(tg, HEAD_DIM)
        o_ref[...] = out
        xc = out - jnp.mean(out, axis=-1, keepdims=True)
        rn = xc * lax.rsqrt(jnp.mean(xc * xc, axis=-1, keepdims=True) + EPS)
        g = zg[...].astype(F32)
        r_ref[...] = (rn * (g * _sigmoid(g))).astype(BF16)

    tab = pl.BlockSpec((tg, HEAD_DIM), lambda h, g: (g, 0))
    per_head = lambda rows: pl.BlockSpec((None, rows, HEAD_DIM), lambda h, g: (h, 0, 0))
    head_out = pl.BlockSpec((tg, HEAD_DIM), lambda h, g: (g, h))
    return _call(
        body, name=name, grid=(RET_HEADS, ng),
        in_specs=[_head_specs(tg, i, fwd) for i in range(4)]
        + [tab, tab, per_head(CHUNK), per_head(CHUNK), per_head(CHUNK), per_head(8)],
        out_specs=[head_out, head_out, pl.BlockSpec((None, cg, HEAD_DIM, HEAD_DIM), lambda h, g: (h, g, 0, 0))],
        out_shape=[jax.ShapeDtypeStruct((T, RET_WIDTH), BF16), jax.ShapeDtypeStruct((T, RET_WIDTH), F32),
                   jax.ShapeDtypeStruct((RET_HEADS, N, HEAD_DIM, HEAD_DIM), F32)],
        scratch=[pltpu.VMEM((HEAD_DIM, HEAD_DIM), F32)],
        sem=("parallel", "arbitrary"), operands=(z, z, z, z, cosf, sinf, intra, kdec, qdec, cdb), rider=rider)


def _window_sums(u, shift_of):
    sums = []
    s = u
    k = 1
    while k < POOL_WINDOWS[-1]:
        s = s + pltpu.roll(s, shift_of(k), 0)
        sums.append(s)
        k *= 2
    return sums


def _select_group(vals, g):
    out = vals[-1]
    for i in range(len(vals) - 2, -1, -1):
        out = jnp.where(g == i, vals[i], out)
    return out


def _pool_parts(u, g, T, pad):
    rows = lax.broadcasted_iota(jnp.int32, (T, HEAD_DIM), 0)
    valid = rows >= pad
    win = _select_group([float(w) for w in POOL_WINDOWS], g)
    div = jnp.clip((rows - pad + 1).astype(F32), 1.0, win)
    s = _select_group(_window_sums(u, lambda k: k), g)
    pooled = jnp.where(valid, s / div - u, 0.0)
    return pooled, div, valid


def _pool_specs(T, layer):
    first = 4 * RET_WIDTH // HEAD_DIM
    return [
        pl.BlockSpec((T, HEAD_DIM), lambda g: (0, first + g)),
        pl.BlockSpec((None, None, HEAD_DIM, HEAD_DIM), lambda g: (layer, g, 0, 0)),
        pl.BlockSpec((None, 1, HEAD_DIM), lambda g: (layer, 0, g)),
    ]


def _pool_fwd(z, maps, scale, layer, pad, name):
    T = z.shape[0]
    assert pad >= POOL_WINDOWS[-1], "window rolls wrap into the zero rows in front"

    def body(zu, maps_ref, sc_ref, pm_ref):
        g = pl.program_id(0)
        pooled, _, _ = _pool_parts(zu[...].astype(F32), g, T, pad)
        y = _dot(pooled.astype(BF16), maps_ref[...].astype(BF16))
        pm_ref[...] = (y * sc_ref[...]).astype(BF16)

    return _call(
        body, name=name, grid=(POOL_GROUPS,),
        in_specs=_pool_specs(T, layer),
        out_specs=[pl.BlockSpec((T, HEAD_DIM), lambda g: (0, g))],
        out_shape=[jax.ShapeDtypeStruct((T, POOL_WIDTH), BF16)],
        sem=("parallel",), operands=(z, maps, scale))[0]


def _gate_specs(tm, D):
    nb = D // RET_WIDTH
    first = (4 * RET_WIDTH + POOL_WIDTH) // RET_WIDTH
    return [pl.BlockSpec((tm, RET_WIDTH), functools.partial(lambda t, j: (t, j), j=first + j)) for j in range(2 * nb)]


def _load_gates(refs, nb):
    ga = jnp.concatenate([r[...].astype(F32) for r in refs[:nb]], axis=1)
    gb = jnp.concatenate([r[...].astype(F32) for r in refs[nb:]], axis=1)
    return ga, gb


def _mix_fwd(h, r, pm, z, wru, wpu, wout, tm, name, rider=None):
    T, D = h.shape
    Dq = D // N_CHIPS
    nb = D // RET_WIDTH

    def body(*refs):
        h_ref, r_ref, pm_ref = refs[:3]
        gate_refs = refs[3:3 + 2 * nb]
        wru_ref, wpu_ref, wout_ref, ho_ref, mx_ref, ret_ref, pool_ref = refs[3 + 2 * nb:]
        rv = r_ref[...]
        pv = pm_ref[...]
        ret = jnp.concatenate([_dot(rv, wru_ref[s]) for s in range(N_CHIPS)], axis=1)
        pool = jnp.concatenate([_dot(pv, wpu_ref[s]) for s in range(N_CHIPS)], axis=1)
        ga, gb = _load_gates(gate_refs, nb)
        mixed = (_sigmoid(ga) * ret + _sigmoid(gb) * pool).astype(BF16)
        mx_ref[...] = mixed
        ret_ref[...] = ret.astype(BF16)
        pool_ref[...] = pool.astype(BF16)
        ho_ref[...] = h_ref[...] + _dot(mixed, wout_ref[...].reshape(D, D))

    row = pl.BlockSpec((tm, D), lambda t: (t, 0))
    half = pl.BlockSpec((tm, RET_WIDTH), lambda t: (t, 0))
    up = pl.BlockSpec((N_CHIPS, RET_WIDTH, Dq), lambda t: (0, 0, 0))
    return _call(
        body, name=name, grid=(T // tm,),
        in_specs=[row, half, half] + _gate_specs(tm, D) + [up, up, pl.BlockSpec((N_CHIPS, Dq, D), lambda t: (0, 0, 0))],
        out_specs=[row, row, row, row],
        out_shape=[jax.ShapeDtypeStruct((T, D), F32)] + [jax.ShapeDtypeStruct((T, D), BF16)] * 3,
        sem=("parallel",), operands=(h, r, pm, *([z] * (2 * nb)), wru, wpu, wout), rider=rider)


def _final_loss(h, gain, tgt, name):
    T, D = h.shape
    first = (T - tgt.shape[0]) // CHUNK

    def body(h_ref, g_ref, t_ref, dh_ref, loss_ref, dg_ref):
        i = pl.program_id(0)

        @pl.when(i == 0)
        def _():
            loss_ref[...] = jnp.zeros_like(loss_ref)
            dg_ref[...] = jnp.zeros_like(dg_ref)

        x = h_ref[...]
        gain_v = g_ref[...]
        err = jnp.where(i >= first, _rms_fwd(x, gain_v) - t_ref[...], 0.0)
        loss_ref[...] += 0.5 * jnp.sum(jnp.mean(err * err, axis=-1))
        dx, dgain = _rms_bwd(x, gain_v, err * (1.0 / D))
        dg_ref[...] += dgain
        dh_ref[...] = dx

    return _call(
        body, name=name, grid=(T // CHUNK,),
        in_specs=[pl.BlockSpec((CHUNK, D), lambda i: (i, 0)),
                  pl.BlockSpec((1, D), lambda i: (0, 0)),
                  pl.BlockSpec((CHUNK, D), lambda i: (jnp.maximum(i - first, 0), 0))],
        out_specs=[pl.BlockSpec((CHUNK, D), lambda i: (i, 0)),
                   pl.BlockSpec((1, LANES), lambda i: (0, 0)),
                   pl.BlockSpec((1, D), lambda i: (0, 0))],
        out_shape=[jax.ShapeDtypeStruct((T, D), F32), jax.ShapeDtypeStruct((1, LANES), F32),
                   jax.ShapeDtypeStruct((1, D), F32)],
        sem=("arbitrary",), operands=(h, gain, tgt))


def _ffn_bwd_act(dy, g, u, wd, tm, name, rider=None):
    T, D = dy.shape
    Fs = wd.shape[1]
    F = N_CHIPS * Fs

    def body(dy_ref, go_ref, uo_ref, wd_ref, dg_ref, du_ref, dyh_ref):
        @pl.when(pl.program_id(1) == 0)
        def _():
            dyh_ref[...] = (0.5 * dy_ref[...]).astype(BF16)

        dact = _dot_nt(dyh_ref[...], wd_ref[...])
        du_ref[...] = (dact * uo_ref[...].astype(F32)).astype(BF16)
        dg_ref[...] = (dact * go_ref[...].astype(F32)).astype(BF16)

    row = pl.BlockSpec((tm, D), lambda t, s: (t, 0))
    col = pl.BlockSpec((tm, Fs), lambda t, s: (t, s))
    return _call(
        body, name=name, grid=(T // tm, N_CHIPS),
        in_specs=[row, col, col, pl.BlockSpec((None, Fs, D), lambda t, s: (s, 0, 0))],
        out_specs=[col, col, row],
        out_shape=[jax.ShapeDtypeStruct((T, F), BF16), jax.ShapeDtypeStruct((T, F), BF16),
                   jax.ShapeDtypeStruct((T, D), BF16)],
        sem=("parallel", "arbitrary"), operands=(dy, g, u, wd), rider=rider)


def _ffn_bwd_in(dy, h, gain, dg, du, wg, wu, layer, tm, pad, name, rider=None):
    T, D = h.shape
    Fs = wg.shape[-1]

    def body(dy_ref, h_ref, g_ref, dg_ref, du_ref, wg_ref, wu_ref, dh_ref, dgain_ref, da_ref):
        t = pl.program_id(0)
        s = pl.program_id(1)

        @pl.when((t == 0) & (s == 0))
        def _():
            dgain_ref[...] = jnp.zeros_like(dgain_ref)

        @pl.when(s == 0)
        def _():
            da_ref[...] = jnp.zeros_like(da_ref)

        da_ref[...] += _dot_nt(dg_ref[...], wg_ref[...]) + _dot_nt(du_ref[...], wu_ref[...])

        @pl.when(s == N_CHIPS - 1)
        def _():
            dx, dgain = _rms_bwd(h_ref[...], g_ref[...], da_ref[...])
            dgain_ref[...] += dgain
            dh_ref[...] = jnp.where(_row_mask(t, tm, pad, (tm, D)), dy_ref[...] + dx, 0.0)

    row = pl.BlockSpec((tm, D), lambda t, s: (t, 0))
    col = pl.BlockSpec((tm, Fs), lambda t, s: (t, s))
    wcol = pl.BlockSpec((None, D, Fs), lambda t, s: (s, 0, 0))
    return _call(
        body, name=name, grid=(T // tm, N_CHIPS),
        in_specs=[row, row, pl.BlockSpec((None, 1, D), lambda t, s: (layer, 0, 0)), col, col, wcol, wcol],
        out_specs=[row, pl.BlockSpec((1, D), lambda t, s: (0, 0))],
        out_shape=[jax.ShapeDtypeStruct((T, D), F32), jax.ShapeDtypeStruct((1, D), F32)],
        scratch=[pltpu.VMEM((tm, D), F32)],
        sem=("arbitrary", "arbitrary"), operands=(dy, h, gain, dg, du, wg, wu), rider=rider)


def _grad_tn(a, b, mode, scale, tm, name, rider=None):
    T = a.shape[0]
    if mode == "col":
        per, R, C = 1, a.shape[1], b.shape[1] // N_CHIPS
        a_spec = pl.BlockSpec((tm, R), lambda s, t: (t, 0))
        b_spec = pl.BlockSpec((tm, C), lambda s, t: (t, s))
    else:
        per, R, C = 2, a.shape[1] // N_CHIPS, b.shape[1]
        a_spec = pl.BlockSpec((tm, per * R), lambda s, t: (t, s))
        b_spec = pl.BlockSpec((tm, C), lambda s, t: (t, 0))
    nt = T // tm

    def body(a_ref, b_ref, o_ref, acc_ref):
        t = pl.program_id(1)

        @pl.when(t == 0)
        def _():
            acc_ref[...] = jnp.zeros_like(acc_ref)

        acc_ref[...] += _dot_tn(a_ref[...].astype(BF16), b_ref[...].astype(BF16))

        @pl.when(t == nt - 1)
        def _():
            o_ref[...] = (scale * acc_ref[...]).astype(BF16).reshape(per, R, C)

    return _call(
        body, name=name, grid=(N_CHIPS // per, nt),
        in_specs=[a_spec, b_spec],
        out_specs=[pl.BlockSpec((per, R, C), lambda s, t: (s, 0, 0))],
        out_shape=[jax.ShapeDtypeStruct((N_CHIPS, R, C), BF16)],
        scratch=[pltpu.VMEM((per * R, C), F32)],
        sem=("parallel", "arbitrary"), operands=(a, b), rider=rider)[0]


def _grad_mix(mixed, dh, r, dret, pm, dpool, tk, name, rider=None):
    T, D = dh.shape
    Dq = D // N_CHIPS
    nt = T // tk

    def body(mx_ref, dh_ref, r_ref, dret_ref, pm_ref, dpool_ref, go_ref, gr_ref, gp_ref, ao_ref, ar_ref, ap_ref):
        t = pl.program_id(0)

        @pl.when(t == 0)
        def _():
            ao_ref[...] = jnp.zeros_like(ao_ref)
            ar_ref[...] = jnp.zeros_like(ar_ref)
            ap_ref[...] = jnp.zeros_like(ap_ref)

        ao_ref[...] += _dot_tn(mx_ref[...], dh_ref[...].astype(BF16))
        ar_ref[...] += _dot_tn(r_ref[...], dret_ref[...])
        ap_ref[...] += _dot_tn(pm_ref[...], dpool_ref[...])

        @pl.when(t == nt - 1)
        def _():
            go_ref[...] = ao_ref[...].astype(BF16).reshape(N_CHIPS, Dq, D)
            for s in range(N_CHIPS):
                gr_ref[s] = ar_ref[:, s * Dq:(s + 1) * Dq].astype(BF16)
                gp_ref[s] = ap_ref[:, s * Dq:(s + 1) * Dq].astype(BF16)

    row = pl.BlockSpec((tk, D), lambda t: (t, 0))
    half = pl.BlockSpec((tk, RET_WIDTH), lambda t: (t, 0))
    whole = lambda shape: pl.BlockSpec(shape, lambda t: (0, 0, 0))
    return _call(
        body, name=name, grid=(nt,),
        in_specs=[row, row, half, row, half, row],
        out_specs=[whole((N_CHIPS, Dq, D)), whole((N_CHIPS, RET_WIDTH, Dq)), whole((N_CHIPS, POOL_WIDTH, Dq))],
        out_shape=[jax.ShapeDtypeStruct((N_CHIPS, Dq, D), BF16),
                   jax.ShapeDtypeStruct((N_CHIPS, RET_WIDTH, Dq), BF16),
                   jax.ShapeDtypeStruct((N_CHIPS, POOL_WIDTH, Dq), BF16)],
        scratch=[pltpu.VMEM((D, D), F32), pltpu.VMEM((RET_WIDTH, D), F32), pltpu.VMEM((POOL_WIDTH, D), F32)],
        sem=("arbitrary",), operands=(mixed, dh, r, dret, pm, dpool), rider=rider)


def _mix_bwd_dx(dh, z, ret, pool, wout, wru, wpu, tm, name, rider=None):
    T, D = dh.shape
    Dq = D // N_CHIPS
    nb = D // RET_WIDTH

    def body(*refs):
        dh_ref = refs[0]
        gate_refs = refs[1:1 + 2 * nb]
        ret_ref, pool_ref, wout_ref, wru_ref, wpu_ref, dgab_ref, dret_ref, dpool_ref, dr_ref, dpm_ref = refs[1 + 2 * nb:]
        dmixed = _dot_nt(dh_ref[...].astype(BF16), wout_ref[...].reshape(D, D))
        ga, gb = _load_gates(gate_refs, nb)
        sa = _sigmoid(ga)
        sb = _sigmoid(gb)
        dgab_ref[:, :D] = (dmixed * ret_ref[...].astype(F32) * (sa * (1.0 - sa))).astype(BF16)
        dgab_ref[:, D:] = (dmixed * pool_ref[...].astype(F32) * (sb * (1.0 - sb))).astype(BF16)
        dret = (dmixed * sa).astype(BF16)
        dpool = (dmixed * sb).astype(BF16)
        dret_ref[...] = dret
        dpool_ref[...] = dpool
        dr = _dot_nt(dret[:, :Dq], wru_ref[0])
        dpm = _dot_nt(dpool[:, :Dq], wpu_ref[0])
        for s in range(1, N_CHIPS):
            dr += _dot_nt(dret[:, s * Dq:(s + 1) * Dq], wru_ref[s])
            dpm += _dot_nt(dpool[:, s * Dq:(s + 1) * Dq], wpu_ref[s])
        dr_ref[...] = dr
        dpm_ref[...] = dpm

    row = pl.BlockSpec((tm, D), lambda t: (t, 0))
    half = pl.BlockSpec((tm, RET_WIDTH), lambda t: (t, 0))
    up = pl.BlockSpec((N_CHIPS, RET_WIDTH, Dq), lambda t: (0, 0, 0))
    return _call(
        body, name=name, grid=(T // tm,),
        in_specs=[row] + _gate_specs(tm, D) + [row, row, pl.BlockSpec((N_CHIPS, Dq, D), lambda t: (0, 0, 0)), up, up],
        out_specs=[pl.BlockSpec((tm, 2 * D), lambda t: (t, 0)), row, row, half, half],
        out_shape=[jax.ShapeDtypeStruct((T, 2 * D), BF16), jax.ShapeDtypeStruct((T, D), BF16),
                   jax.ShapeDtypeStruct((T, D), BF16), jax.ShapeDtypeStruct((T, RET_WIDTH), F32),
                   jax.ShapeDtypeStruct((T, POOL_WIDTH), F32)],
        sem=("parallel",), operands=(dh, *([z] * (2 * nb)), ret, pool, wout, wru, wpu), rider=rider)


def _pool_bwd(z, dpm, maps, scale, layer, pad, name):
    T = z.shape[0]

    def body(zu, maps_ref, sc_ref, dpm_ref, du_ref, dmaps_ref, dsc_ref):
        g = pl.program_id(0)
        u = zu[...].astype(F32)
        pooled, div, valid = _pool_parts(u, g, T, pad)
        pb = pooled.astype(BF16)
        mb = maps_ref[...].astype(BF16)
        dp = dpm_ref[...]
        dsc_ref[...] = jnp.sum(dp * _dot(pb, mb), axis=0, keepdims=True)
        dyb = (dp * sc_ref[...]).astype(BF16)
        dmaps_ref[...] = _dot_tn(pb, dyb)
        dpooled = jnp.where(valid, _dot_nt(dyb, mb), 0.0)
        ahead = _select_group(_window_sums(dpooled / div, lambda k: T - k), g)
        du_ref[...] = jnp.where(valid, ahead - dpooled, 0.0).astype(BF16)

    blk = pl.BlockSpec((T, HEAD_DIM), lambda g: (0, g))
    return _call(
        body, name=name, grid=(POOL_GROUPS,),
        in_specs=_pool_specs(T, layer) + [blk],
        out_specs=[blk, pl.BlockSpec((None, HEAD_DIM, HEAD_DIM), lambda g: (g, 0, 0)),
                   pl.BlockSpec((1, HEAD_DIM), lambda g: (0, g))],
        out_shape=[jax.ShapeDtypeStruct((T, POOL_WIDTH), BF16),
                   jax.ShapeDtypeStruct((POOL_GROUPS, HEAD_DIM, HEAD_DIM), F32),
                   jax.ShapeDtypeStruct((1, POOL_WIDTH), F32)],
        sem=("parallel",), operands=(z, maps, scale, dpm))


def _ret_bwd_local(z, o_pre, s_all, dr, consts, cg, name):
    T = z.shape[0]
    N = T // CHUNK
    ng = N // cg
    tg = cg * CHUNK
    cosf, sinf, intra, _, qdec, _ = consts
    fwd = lambda g: g

    def body(zq, zk, zv, zg, o_ref, s_ref, dr_ref, cos_ref, sin_ref, m_ref, qd_ref,
             dq_ref, dg_ref, dk_ref, dv_ref, ds_ref):
        cosv = cos_ref[...]
        sinv = sin_ref[...]
        scale = HEAD_DIM ** -0.5
        q3 = (_rot(zq[...].astype(F32), cosv, sinv) * scale).reshape(cg, CHUNK, HEAD_DIM)
        k3 = _rot(zk[...].astype(F32), cosv, sinv).reshape(cg, CHUNK, HEAD_DIM)
        qb = q3.astype(BF16)
        kb = k3.astype(BF16)
        vb = zv[...].reshape(cg, CHUNK, HEAD_DIM).astype(BF16)
        mask = m_ref[...][None]
        sb = (_ein("ncd,nmd->ncm", qb, kb) * mask).astype(BF16)
        qdv = qd_ref[...][None]
        qdb = (q3 * qdv).astype(BF16)

        out = o_ref[...]
        xc = out - jnp.mean(out, axis=-1, keepdims=True)
        rstd = lax.rsqrt(jnp.mean(xc * xc, axis=-1, keepdims=True) + EPS)
        rn = xc * rstd
        g = zg[...].astype(F32)
        sg = _sigmoid(g)
        drv = dr_ref[...]
        dg_ref[...] = (drv * rn * (sg * (1.0 + g * (1.0 - sg)))).astype(BF16)
        drn = drv * (g * sg)
        dout = rstd * (drn - jnp.mean(drn, axis=-1, keepdims=True)
                       - rn * jnp.mean(drn * rn, axis=-1, keepdims=True))
        dob = dout.reshape(cg, CHUNK, HEAD_DIM).astype(BF16)

        dsb = (_ein("ncd,nmd->ncm", dob, vb) * mask).astype(BF16)
        dv_ref[...] = _ein("ncm,ncd->nmd", sb, dob).reshape(tg, HEAD_DIM)
        dk_ref[...] = _ein("ncm,ncd->nmd", dsb, qb).reshape(tg, HEAD_DIM)
        dq3 = _ein("ncm,nmd->ncd", dsb, kb) + _ein("nce,nde->ncd", dob, s_ref[...].astype(BF16)) * qdv
        dq_ref[...] = _rot_t(dq3.reshape(tg, HEAD_DIM) * scale, cosv, sinv).astype(BF16)
        ds_ref[...] = _ein("ncd,nce->nde", qdb, dob)

    tab = pl.BlockSpec((tg, HEAD_DIM), lambda h, g: (g, 0))
    per_head = pl.BlockSpec((None, CHUNK, HEAD_DIM), lambda h, g: (h, 0, 0))
    head_blk = pl.BlockSpec((tg, HEAD_DIM), lambda h, g: (g, h))
    state_blk = pl.BlockSpec((None, cg, HEAD_DIM, HEAD_DIM), lambda h, g: (h, g, 0, 0))
    return _call(
        body, name=name, grid=(RET_HEADS, ng),
        in_specs=[_head_specs(tg, i, fwd) for i in range(4)]
        + [head_blk, state_blk, head_blk, tab, tab, per_head, per_head],
        out_specs=[head_blk, head_blk, head_blk, head_blk, state_blk],
        out_shape=[jax.ShapeDtypeStruct((T, RET_WIDTH), BF16), jax.ShapeDtypeStruct((T, RET_WIDTH), BF16),
                   jax.ShapeDtypeStruct((T, RET_WIDTH), F32), jax.ShapeDtypeStruct((T, RET_WIDTH), F32),
                   jax.ShapeDtypeStruct((RET_HEADS, N, HEAD_DIM, HEAD_DIM), F32)],
        sem=("parallel", "parallel"), operands=(z, z, z, z, o_pre, s_all, dr, cosf, sinf, intra, qdec))


def _ret_bwd_state(z, dkp, dvp, ds, consts, cg, name):
    T = z.shape[0]
    N = T // CHUNK
    ng = N // cg
    tg = cg * CHUNK
    cosf, sinf, _, kdec, _, cdb = consts
    rev = lambda g: ng - 1 - g

    def body(zk, zv, dkp_ref, dvp_ref, ds_ref, cos_ref, sin_ref, kd_ref, cd_ref, dk_ref, dv_ref, gs_ref, dkv_ref):
        @pl.when(pl.program_id(1) == 0)
        def _():
            gs_ref[...] = jnp.zeros_like(gs_ref)

        cosv = cos_ref[...]
        sinv = sin_ref[...]
        cd = cd_ref[0:1, :]
        grad = gs_ref[...]
        for n in reversed(range(cg)):
            dkv_ref[n] = grad
            grad = ds_ref[n] + cd * grad
        gs_ref[...] = grad
        dkvb = dkv_ref[...].astype(BF16)
        kdv = kd_ref[...][None]
        k3 = _rot(zk[...].astype(F32), cosv, sinv).reshape(cg, CHUNK, HEAD_DIM)
        vb = zv[...].reshape(cg, CHUNK, HEAD_DIM).astype(BF16)
        dk3 = _ein("nce,nde->ncd", vb, dkvb) * kdv
        dv3 = _ein("ncd,nde->nce", (k3 * kdv).astype(BF16), dkvb)
        dk_ref[...] = _rot_t(dkp_ref[...] + dk3.reshape(tg, HEAD_DIM), cosv, sinv).astype(BF16)
        dv_ref[...] = (dvp_ref[...] + dv3.reshape(tg, HEAD_DIM)).astype(BF16)

    tab = pl.BlockSpec((tg, HEAD_DIM), lambda h, g: (rev(g), 0))
    head_blk = pl.BlockSpec((tg, HEAD_DIM), lambda h, g: (rev(g), h))
    return _call(
        body, name=name, grid=(RET_HEADS, ng),
        in_specs=[_head_specs(tg, 1, rev), _head_specs(tg, 2, rev), head_blk, head_blk,
                  pl.BlockSpec((None, cg, HEAD_DIM, HEAD_DIM), lambda h, g: (h, rev(g), 0, 0)),
                  tab, tab,
                  pl.BlockSpec((None, CHUNK, HEAD_DIM), lambda h, g: (h, 0, 0)),
                  pl.BlockSpec((None, 8, HEAD_DIM), lambda h, g: (h, 0, 0))],
        out_specs=[head_blk, head_blk],
        out_shape=[jax.ShapeDtypeStruct((T, RET_WIDTH), BF16)] * 2,
        scratch=[pltpu.VMEM((HEAD_DIM, HEAD_DIM), F32), pltpu.VMEM((cg, HEAD_DIM, HEAD_DIM), F32)],
        sem=("parallel", "arbitrary"), operands=(z, z, dkp, dvp, ds, cosf, sinf, kdec, cdb))


def _z_segments(pieces, ns):
    segs, at = [], 0
    for k, p in enumerate(pieces):
        width = p.shape[1]
        lo = at
        while lo < at + width:
            s = lo // ns
            hi = min(at + width, (s + 1) * ns)
            segs.append((k, lo - at, hi - at, s, lo - s * ns, hi - s * ns))
            lo = hi
        at += width
    assert at == N_CHIPS * ns and all(v % LANES == 0 for seg in segs for v in (seg[1], seg[2], seg[4], seg[5]))
    return segs


def _inproj_bwd_dx(pieces, win, h, gain, dh_in, layer, tm, pad, name, rider=None):
    T, D = h.shape
    Ns = win.shape[-1]
    n = len(pieces)
    segs = _z_segments(pieces, Ns)

    def body(*refs):
        piece_refs = refs[:n]
        w_ref, h_ref, g_ref, dhi_ref, dh_ref, dgain_ref = refs[n:]
        t = pl.program_id(0)

        @pl.when(t == 0)
        def _():
            dgain_ref[...] = jnp.zeros_like(dgain_ref)

        db = None
        for k, a, b, s, c, d in segs:
            term = _dot_nt(piece_refs[k][:, a:b], w_ref[s, :, c:d])
            db = term if db is None else db + term
        dx, dgain = _rms_bwd(h_ref[...], g_ref[...], db)
        dgain_ref[...] += dgain
        dh_ref[...] = jnp.where(_row_mask(t, tm, pad, (tm, D)), dhi_ref[...] + dx, 0.0)

    row = pl.BlockSpec((tm, D), lambda t: (t, 0))
    return _call(
        body, name=name, grid=(T // tm,),
        in_specs=[pl.BlockSpec((tm, p.shape[1]), lambda t: (t, 0)) for p in pieces]
        + [pl.BlockSpec((N_CHIPS, D, Ns), lambda t: (0, 0, 0)), row,
           pl.BlockSpec((None, 1, D), lambda t: (layer, 0, 0)), row],
        out_specs=[row, pl.BlockSpec((1, D), lambda t: (0, 0))],
        out_shape=[jax.ShapeDtypeStruct((T, D), F32), jax.ShapeDtypeStruct((1, D), F32)],
        sem=("arbitrary",), operands=(*pieces, win, h, gain, dh_in), rider=rider)


def _grad_w_in(b, pieces, ns, tk, name, rider=None):
    T, D = b.shape
    n = len(pieces)
    nt = T // tk
    segs = _z_segments(pieces, ns)
    shards_of = [sorted({s for k, _, _, s, _, _ in segs if k == i}) for i in range(n)]

    def body(*refs):
        b_ref = refs[0]
        piece_refs = refs[1:1 + n]
        o_ref, acc_ref = refs[1 + n:]
        s = pl.program_id(0)
        t = pl.program_id(1)

        @pl.when(t == 0)
        def _():
            acc_ref[...] = jnp.zeros_like(acc_ref)

        for shard in range(N_CHIPS):
            @pl.when(s == shard)
            def _(shard=shard):
                cols = [piece_refs[k][:, a:e] for k, a, e, ss, _, _ in segs if ss == shard]
                dz = cols[0] if len(cols) == 1 else jnp.concatenate(cols, axis=1)
                acc_ref[...] += _dot_tn(b_ref[...], dz)

        @pl.when(t == nt - 1)
        def _():
            o_ref[...] = acc_ref[...].astype(BF16).reshape(1, D, ns)

    def piece_spec(i):
        def index(s, t):
            used = functools.reduce(jnp.logical_or, [s == ss for ss in shards_of[i]])
            return (jnp.where(used, t, 0), 0)
        return pl.BlockSpec((tk, pieces[i].shape[1]), index)

    return _call(
        body, name=name, grid=(N_CHIPS, nt),
        in_specs=[pl.BlockSpec((tk, D), lambda s, t: (t, 0))] + [piece_spec(i) for i in range(n)],
        out_specs=[pl.BlockSpec((1, D, ns), lambda s, t: (s, 0, 0))],
        out_shape=[jax.ShapeDtypeStruct((N_CHIPS, D, ns), BF16)],
        scratch=[pltpu.VMEM((D, ns), F32)],
        sem=("parallel", "arbitrary"), operands=(b, *pieces), rider=rider)[0]


def _sum_pair(gs, rs, c_idx, name):
    n = len(gs)

    def body(c_ref, *refs):
        for g_ref, r_ref, o_ref in zip(refs[:n], refs[n:2 * n], refs[2 * n:]):
            o_ref[...] = (g_ref[...].astype(F32) + r_ref[...].astype(F32)).astype(BF16)

    halves = [pl.BlockSpec((None,) + r.shape[1:], lambda s, c_ref: (s, 0, 0)) for r in rs]
    return pl.pallas_call(
        body,
        name=name,
        grid_spec=pltpu.PrefetchScalarGridSpec(
            num_scalar_prefetch=1,
            grid=(N_CHIPS,),
            in_specs=[pl.BlockSpec((None,) + r.shape[1:], lambda s, c_ref: (s, c_ref[0], 0)) for r in rs] + halves,
            out_specs=halves,
        ),
        out_shape=[jax.ShapeDtypeStruct(r.shape, BF16) for r in rs],
        compiler_params=_params(("parallel",)),
    )(c_idx, *gs, *rs)


def _sum_chips(ps, rs, pos, name):
    n = len(ps)
    quarters = 4

    def body(pos_ref, *refs):
        chip = pos_ref[0]
        for p_ref, r_ref, o_ref in zip(refs[:n], refs[n:2 * n], refs[2 * n:]):
            own = p_ref[...].astype(F32)
            terms = [jnp.where(chip == k, own, r_ref[k].astype(F32)) for k in range(N_CHIPS)]
            o_ref[...] = ((terms[0] + terms[1]) + terms[2]) + terms[3]

    def rows(r):
        assert r.shape[1] % (quarters * BF16_ROWS) == 0, r.shape
        return r.shape[1] // quarters

    return pl.pallas_call(
        body,
        name=name,
        grid_spec=pltpu.PrefetchScalarGridSpec(
            num_scalar_prefetch=1,
            grid=(quarters,),
            in_specs=[pl.BlockSpec((None, rows(r), r.shape[2]), lambda q, pos_ref: (pos_ref[0], q, 0)) for r in rs]
            + [pl.BlockSpec((N_CHIPS, rows(r), r.shape[2]), lambda q, pos_ref: (0, q, 0)) for r in rs],
            out_specs=[pl.BlockSpec((rows(r), r.shape[2]), lambda q, pos_ref: (pos_ref[1] * quarters + q, 0))
                       for r in rs],
        ),
        out_shape=[jax.ShapeDtypeStruct((2 * r.shape[1], r.shape[2]), F32) for r in rs],
        compiler_params=_params(("arbitrary",)),
    )(pos, *ps, *rs)


def _small_all_reduce(p, rider=None):
    rows, width = p.shape
    r = 0 if rider is None else len(rider.ins)

    def body(*refs):
        p_ref, o_ref = refs[0], refs[1 + r]
        sib_ref, slot_ref, ssem, rsem = refs[2 + 2 * r:6 + 2 * r]
        if rider is not None:
            rider.start(refs[1:1 + r], refs[2 + r:2 + 2 * r], refs[6 + 2 * r], refs[7 + 2 * r])
        reduce(p_ref, o_ref, sib_ref, slot_ref, ssem, rsem)
        if rider is not None:
            rider.finish(refs[1:1 + r], refs[2 + r:2 + 2 * r], refs[6 + 2 * r], refs[7 + 2 * r])

    def reduce(p_ref, o_ref, sib_ref, slot_ref, ssem, rsem):
        x, y, c, chip, others = _mesh_pos()
        pair = _remote(p_ref, sib_ref, ssem.at[0], rsem.at[0], (x, y, 1 - c))
        pair.start()
        pair.wait()
        slot_ref[chip] = p_ref[...] + sib_ref[...]
        sends = []
        for j, (ox, oy) in enumerate(others):
            cp = _remote(slot_ref.at[chip], slot_ref.at[chip], ssem.at[1 + j], rsem.at[1 + j], (ox, oy, c))
            cp.start()
            sends.append(cp)
        for j, (ox, oy) in enumerate(others):
            slot = slot_ref.at[2 * ox + oy]
            _remote(slot, slot, ssem.at[1 + j], rsem.at[1 + j], (ox, oy, c)).wait_recv()
        for cp in sends:
            cp.wait_send()
        o_ref[...] = ((slot_ref[0] + slot_ref[1]) + slot_ref[2]) + slot_ref[3]

    vmem = pl.BlockSpec(memory_space=pltpu.VMEM)
    scratch = [pltpu.VMEM((rows, width), F32), pltpu.VMEM((N_CHIPS, rows, width), F32),
               pltpu.SemaphoreType.DMA((4,)), pltpu.SemaphoreType.DMA((4,))]
    if rider is not None:
        scratch += [pltpu.SemaphoreType.DMA((rider.n_sem,)), pltpu.SemaphoreType.DMA((rider.n_sem,))]
    outs = pl.pallas_call(
        body,
        name="small_grads_all_reduce",
        in_specs=[vmem] + [ANY] * r,
        out_specs=[vmem] + [ANY] * r,
        out_shape=[jax.ShapeDtypeStruct(p.shape, F32)] + ([] if rider is None else list(rider.out_shapes)),
        input_output_aliases={} if rider is None else rider.aliases(1, 1),
        scratch_shapes=scratch,
    )(p, *([] if rider is None else rider.ins))
    if rider is not None:
        rider.results = outs[1:]
    return outs[0]


def _adamw(gs, w, m, v, name):
    L, R, C = w.shape
    Ct = gs[0].shape[1]
    tr = _pick_tile(R, 256, 8)

    def body(*refs):
        g_refs = refs[:L]
        w_ref, m_ref, v_ref, go_ref, d_ref, mo_ref, vo_ref = refs[L:]
        layer = pl.program_id(0)
        grad = g_refs[L - 1][...]
        for i in range(L - 2, -1, -1):
            grad = jnp.where(layer == i, g_refs[i][...], grad)
        if Ct != C:
            grad = grad[:, :C]
        m_new = ADAM_B1 * m_ref[...] + (1.0 - ADAM_B1) * grad
        v_new = ADAM_B2 * v_ref[...] + (1.0 - ADAM_B2) * jnp.square(grad)
        m_hat = m_new / (1.0 - ADAM_B1 ** ADAM_STEP)
        v_hat = v_new / (1.0 - ADAM_B2 ** ADAM_STEP)
        go_ref[...] = grad
        d_ref[...] = -ADAM_LR * (m_hat / (jnp.sqrt(v_hat) + ADAM_EPS) + ADAM_WD * w_ref[...])
        mo_ref[...] = m_new
        vo_ref[...] = v_new

    g_specs = [pl.BlockSpec((tr, Ct), functools.partial(lambda l, r, i: (jnp.where(l == i, r, 0), 0), i=i))
               for i in range(L)]
    blk = pl.BlockSpec((None, tr, C), lambda l, r: (l, r, 0))
    return _call(
        body, name=name, grid=(L, R // tr),
        in_specs=g_specs + [blk, blk, blk],
        out_specs=[blk] * 4,
        out_shape=[jax.ShapeDtypeStruct((L, R, C), F32)] * 4,
        sem=("arbitrary", "arbitrary"), operands=(*gs, w, m, v))


_FFN1 = ("ffn1_gate", "ffn1_up", "ffn1_down")
_FFN2 = ("ffn2_gate", "ffn2_up", "ffn2_down")
_MIXW = ("w_ret_up", "w_pool_up", "w_out")
_BIG = _FFN1 + ("w_in",) + _MIXW + _FFN2
_TRANSPOSED = ("ffn1_gate", "ffn1_up", "ffn2_gate", "ffn2_up")
_SMALL = ("ffn1_norm", "mix_norm", "ffn2_norm", "final_norm", "pool_scale", "pool_maps")
_ORDER = ("meta", "ffn1_norm", "ffn1_gate", "ffn1_up", "ffn1_down", "mix_norm", "w_in", "pool_maps",
          "pool_scale", "w_ret_up", "w_pool_up", "w_out", "ffn2_norm", "ffn2_gate", "ffn2_up", "ffn2_down",
          "final_norm")


def _transport(a):
    n, r, c = a.shape
    out = a.astype(BF16)
    if c % LANES:
        out = jnp.concatenate([out, jnp.zeros((n, r, _round_up(c, LANES) - c), BF16)], axis=2)
    if r % LANES:
        out = jnp.concatenate([out, jnp.zeros((n, _round_up(r, LANES) - r, out.shape[2]), BF16)], axis=1)
    return out


def _pack_rows(parts, width):
    rows = [p.reshape(-1, width) for p in parts]
    total = sum(r.shape[0] for r in rows)
    fill = _round_up(total, 8) - total
    if fill:
        rows.append(jnp.zeros((fill, width), F32))
    return jnp.concatenate(rows, axis=0)


def _unpack_rows(packed, shapes, width):
    out, at = [], 0
    for shp in shapes:
        n = math.prod(shp) // width
        out.append(packed[at:at + n].reshape(shp))
        at += n
    return out


class _Weights:
    def __init__(self, shards):
        self.shards = shards
        self.full = {}

    def rider(self, keys):
        r = _gather_rider([(self.shards[n], i) for n, i in keys])
        r.keys = keys
        return r

    def take(self, rider):
        for key, arr in zip(rider.keys, rider.results):
            self.full[key] = arr

    def __call__(self, name, layer):
        return self.full[(name, layer)]


def _local_step(x, meta_full, tgt, w, wts, pad, tm, cg, reducer):
    D = x.shape[1]
    T = pad + N_META + x.shape[0]
    L = w["ffn1_norm"].shape[0]
    pool_maps = w["pool_maps"]
    gains = {n: w[n].reshape(L, 1, D) for n in ("ffn1_norm", "mix_norm", "ffn2_norm")}
    scale3 = w["pool_scale"].reshape(L, 1, POOL_WIDTH)
    consts = _ret_consts(T, pad)
    tl = _pick_tile(T, 2 * tm, BF16_ROWS)
    def gather(keys):
        return wts.rider(keys) if keys and keys[0] not in wts.full else None

    def done(rider):
        if rider is not None:
            wts.take(rider)

    h = jnp.concatenate([jnp.zeros((pad, D), F32), meta_full, x], axis=0)
    saved = []
    for i in range(L):
        s = {"h0": h}
        if ("ffn1_down", i) in wts.full:
            rd = gather([("w_in", i)] + [(n, i) for n in _MIXW] + [("ffn2_gate", i)])
            h, s["a1"], s["g1"], s["u1"], s["act1"] = _ffn_fwd(
                h, gains["ffn1_norm"], wts("ffn1_gate", i), wts("ffn1_up", i), wts("ffn1_down", i), i, tl,
                f"ffn1_fwd_{i}", rd)
            done(rd)
        else:
            rd = gather([("ffn1_down", i), ("w_in", i)])
            s["a1"], s["g1"], s["u1"], s["act1"] = _ffn_fwd_up(
                h, gains["ffn1_norm"], wts("ffn1_gate", i), wts("ffn1_up", i), i, tl, f"ffn1_fwd_up_{i}", rd)
            done(rd)
            rd = gather([(n, i) for n in _MIXW])
            h = _ffn_fwd_down(h, s["act1"], wts("ffn1_down", i), tl, f"ffn1_fwd_down_{i}", rd)
            done(rd)
        s["h1"] = h
        rd = gather([k for k in (("ffn2_gate", i), ("ffn2_up", i)) if k not in wts.full])
        s["z"], s["b"] = _inproj_fwd(h, gains["mix_norm"], wts("w_in", i), i, tl, f"inproj_fwd_{i}", rd)
        done(rd)
        s["r"], s["o_pre"], s["s_all"] = _ret_fwd(s["z"], consts, cg, f"retention_fwd_{i}")
        s["pm"] = _pool_fwd(s["z"], pool_maps, scale3, i, pad, f"pool_fwd_{i}")
        rd = gather([("ffn2_down", i)])
        h, s["mixed"], s["ret"], s["pool"] = _mix_fwd(
            h, s["r"], s["pm"], s["z"], wts("w_ret_up", i), wts("w_pool_up", i), wts("w_out", i), tl,
            f"mix_fwd_{i}", rd)
        done(rd)
        s["h2"] = h
        rd = gather([(n, i + 1) for n in _FFN1]) if i + 1 < L else None
        h, s["a2"], s["g2"], s["u2"], s["act2"] = _ffn_fwd(
            h, gains["ffn2_norm"], wts("ffn2_gate", i), wts("ffn2_up", i), wts("ffn2_down", i), i, tl,
            f"ffn2_fwd_{i}", rd)
        done(rd)
        saved.append(s)

    dh, loss_acc, d_final = _final_loss(h, w["final_norm"].reshape(1, D), tgt, "final_norm_loss")

    small = {n: [None] * L for n in ("ffn1_norm", "mix_norm", "ffn2_norm", "pool_scale", "pool_maps")}

    carry = {"ffn_act": 1.0, "ffn_in": 2.2, "mix_bwd": 1.0, "inproj_bwd": 1.5, "w_in": 1.0}

    tk = _pick_tile(T, 1408, LANES)

    def grad(n, a, b, i, mode):
        rd = reducer.rider(carry.get(n, 1.0 if i == 0 and n.startswith("ffn") else 0.5))
        reducer.add(n, i, _grad_tn(a, b, mode, 1.0, tk, f"grad_{n}_{i}", rd))
        reducer.done(rd)

    def ffn_bwd(which, dy, h_in, g, u, i, between=None, units=carry["ffn_in"]):
        rd = reducer.rider(carry["ffn_act"])
        dg, du, dyh = _ffn_bwd_act(dy, g, u, wts(f"{which}_down", i), tl, f"{which}_bwd_act_{i}", rd)
        reducer.done(rd)
        if between is not None:
            between(dg, du, dyh)
        rd = reducer.rider(units)
        dh_in, dgain = _ffn_bwd_in(dy, h_in, gains[f"{which}_norm"], dg, du, wts(f"{which}_gate", i),
                                   wts(f"{which}_up", i), i, tl, pad, f"{which}_bwd_in_{i}", rd)
        reducer.done(rd)
        return dh_in, dg, du, dgain, dyh

    for i in reversed(range(L)):
        s = saved[i]
        dh, dg, du, small["ffn2_norm"][i], dyh = ffn_bwd("ffn2", dh, s["h2"], s["g2"], s["u2"], i)
        grad("ffn2_gate", dg, s["a2"], i, "row")
        grad("ffn2_up", du, s["a2"], i, "row")
        grad("ffn2_down", s["act2"], dyh, i, "row")
        reducer.stage(f"ffn2_{i}")
        rd = reducer.rider(carry["mix_bwd"])
        dgab, dret, dpool, dr, dpm = _mix_bwd_dx(
            dh, s["z"], s["ret"], s["pool"], wts("w_out", i), wts("w_ret_up", i), wts("w_pool_up", i), tm,
            f"mix_bwd_{i}", rd)
        reducer.done(rd)
        rd = reducer.rider(0.5)
        g_out, g_ru, g_pu = _grad_mix(s["mixed"], dh, s["r"], dret, s["pm"], dpool, _pick_tile(T, 704, LANES),
                                      f"grad_mix_{i}", rd)
        reducer.done(rd)
        for n, g_n in (("w_out", g_out), ("w_ret_up", g_ru), ("w_pool_up", g_pu)):
            reducer.add(n, i, g_n)
        du_pool, small["pool_maps"][i], small["pool_scale"][i] = _pool_bwd(
            s["z"], dpm, pool_maps, scale3, i, pad, f"pool_bwd_{i}")
        dq, dgr, dkp, dvp, ds = _ret_bwd_local(s["z"], s["o_pre"], s["s_all"], dr, consts,
                                               _pick_tile(T // CHUNK, 11, 1), f"retention_bwd_{i}")
        dk, dv = _ret_bwd_state(s["z"], dkp, dvp, ds, consts, cg, f"retention_bwd_state_{i}")
        dz = [dq, dk, dv, dgr, du_pool, dgab]
        dh2 = dh
        rd = reducer.rider(carry["inproj_bwd"])
        dh, small["mix_norm"][i] = _inproj_bwd_dx(
            dz, wts("w_in", i), s["h1"], gains["mix_norm"], dh2, i, tm, pad, f"inproj_bwd_{i}", rd)
        reducer.done(rd)
        rd = reducer.rider(carry["w_in"])
        reducer.add("w_in", i, _grad_w_in(s["b"], dz, wts("w_in", i).shape[-1], tk, f"grad_w_in_{i}", rd))
        reducer.done(rd)
        reducer.stage(f"mid{i}")
        def ffn1_grads(dg, du, dyh, i=i, s=s):
            grad("ffn1_gate", dg, s["a1"], i, "row")
            if i == 0:
                reducer.stage("gate0")
            grad("ffn1_up", du, s["a1"], i, "row")
            if i == 0:
                reducer.stage("up0")
            grad("ffn1_down", s["act1"], dyh, i, "row")
            reducer.stage(f"end{i}")

        if i == 0:
            dh, _, _, small["ffn1_norm"][i], _ = ffn_bwd("ffn1", dh, s["h0"], s["g1"], s["u1"], i, ffn1_grads, 2.5)
        else:
            dh, dg, du, small["ffn1_norm"][i], dyh = ffn_bwd("ffn1", dh, s["h0"], s["g1"], s["u1"], i)
            ffn1_grads(dg, du, dyh)

    return loss_acc, dh, small, d_final


class _Reducer:
    def __init__(self, unit):
        self.c_idx = lax.axis_index("c").astype(jnp.int32).reshape(1)
        chip = 2 * lax.axis_index("x") + lax.axis_index("y")
        self.pos = jnp.stack([chip, lax.axis_index("c")]).astype(jnp.int32)
        self.pending, self.stages, self.queue, self.halves, self.whole = [], [], [], {}, {}
        self.unit = unit
        self.calls = 0

    def add(self, name, layer, g):
        self.pending.append(((name, layer), g))

    def stage(self, tag):
        if self.pending:
            self.stages.append((tag, self.pending))
            self.pending = []

    def _pair_rider(self):
        if not self.stages:
            return None
        tag, items = self.stages.pop(0)
        rd = _pair_exchange_rider([g for _, g in items])
        rd.tag, rd.keys = tag, [k for k, _ in items]
        return rd

    def _chip_rider(self, units):
        take, keep, size = [], [], 0
        for item in self.queue:
            if units is None or size + item[1].size <= units * self.unit:
                take.append(item)
                size += item[1].size
            else:
                keep.append(item)
        self.queue = keep
        if not take:
            return None
        rd = _chip_exchange_rider([p for _, p in take])
        rd.keys = [k for k, _ in take]
        return rd

    def _gather_rider(self):
        keys = [k for k in self.halves if k not in self.whole]
        if not keys:
            return None
        rd = _pair_gather_rider([self.halves[k] for k in keys])
        rd.keys = keys
        return rd

    def rider(self, units):
        self.riding = (self._pair_rider(), self._chip_rider(units), self._gather_rider())
        return _join(self.riding)

    def done(self, rd):
        if rd is None:
            return
        _split_results(rd)
        pair, chips, gather = self.riding
        if len([r for r in self.riding if r is not None]) == 1:
            (pair or chips or gather).results = rd.results
        self.calls += 1
        if gather is not None:
            self.whole.update(zip(gather.keys, gather.results))
        if pair is not None:
            sums = _sum_pair(pair.ins, pair.results, self.c_idx, f"sum_pair_{pair.tag}")
            self.queue += list(zip(pair.keys, sums))
        if chips is not None:
            sums = _sum_chips(chips.ins, chips.results, self.pos, f"sum_chips_{self.calls}")
            self.halves.update(zip(chips.keys, sums))

    def busy(self):
        assert not self.pending
        return bool(self.stages or self.queue or len(self.whole) < len(self.halves))

    def flush(self):
        self.riding = (self._pair_rider(), self._chip_rider(None), self._gather_rider())
        rd = _join(self.riding)
        _run_rider(rd, f"grads_exchange_tail_{self.calls}")
        self.done(rd)


def _update(loss_acc, grad_x, d_meta_rows, reducer, small, d_final, w, mom, var):
    meta = w["meta"]
    D = w["final_norm"].shape[0]
    L = w["ffn1_norm"].shape[0]
    Dq = D // N_CHIPS

    out = {}

    small_parts = [jnp.concatenate(small[n], axis=0) for n in ("ffn1_norm", "mix_norm", "ffn2_norm")]
    small_parts += [d_final, jnp.concatenate(small["pool_scale"], axis=0), jnp.concatenate(small["pool_maps"], axis=0)]
    loss_row = jnp.pad(loss_acc, ((0, 0), (0, D - loss_acc.shape[1])))
    rd = reducer.rider(None) if reducer.busy() else None
    reduced = _small_all_reduce(_pack_rows(small_parts + [d_meta_rows, loss_row], D), rd)
    reducer.done(rd)
    while reducer.busy():
        reducer.flush()
    for n in _BIG:
        gs = [reducer.whole[(n, i)] for i in range(L)]
        if n in _TRANSPOSED:
            res = _adamw(gs, *(jnp.swapaxes(t[n], 1, 2) for t in (w, mom, var)), f"adamw_{n}")
            out[n] = [jnp.swapaxes(r, 1, 2) for r in res]
        else:
            out[n] = _adamw(gs, w[n], mom[n], var[n], f"adamw_{n}")

    small_shapes = [w[n].shape for n in _SMALL]
    small_rows = sum(math.prod(shp) for shp in small_shapes) // D
    chip = 2 * lax.axis_index("x") + lax.axis_index("y")
    d_meta = lax.dynamic_slice_in_dim(reduced[small_rows:small_rows + N_META], chip * Dq, Dq, axis=1)
    names = _SMALL + ("meta",)
    packed_g = _pack_rows([reduced[:small_rows], d_meta], D)
    packed = [_pack_rows([t[n] for n in names], D) for t in (w, mom, var)]
    res = _adamw([packed_g], packed[0][None], packed[1][None], packed[2][None], "adamw_small")
    shapes = small_shapes + [meta.shape]
    unpacked = [_unpack_rows(r[0], shapes, D) for r in res]
    for k, n in enumerate(names):
        out[n] = tuple(u[k] for u in unpacked)

    loss = reduced[small_rows + N_META, 0]
    return (loss, grad_x) + tuple(out[n][j] for j in range(4) for n in _ORDER)


def kernel(x, meta, ffn1_norm, ffn1_gate, ffn1_up, ffn1_down, mix_norm, w_in, pool_maps, pool_scale, w_ret_up, w_pool_up, w_out, ffn2_norm, ffn2_gate, ffn2_up, ffn2_down, final_norm, loss_target, m_meta, m_ffn1_norm, m_ffn1_gate, m_ffn1_up, m_ffn1_down, m_mix_norm, m_w_in, m_pool_maps, m_pool_scale, m_w_ret_up, m_w_pool_up, m_w_out, m_ffn2_norm, m_ffn2_gate, m_ffn2_up, m_ffn2_down, m_final_norm, v_meta, v_ffn1_norm, v_ffn1_gate, v_ffn1_up, v_ffn1_down, v_mix_norm, v_w_in, v_pool_maps, v_pool_scale, v_w_ret_up, v_w_pool_up, v_w_out, v_ffn2_norm, v_ffn2_gate, v_ffn2_up, v_ffn2_down, v_final_norm):
    args = dict(locals())
    w = {n: args[n] for n in _ORDER}
    mom = {n: args["m_" + n] for n in _ORDER}
    var = {n: args["v_" + n] for n in _ORDER}

    assert x.shape[0] == 1, "one batch element per device"
    seq, D = x.shape[1], x.shape[2]
    assert seq % CHUNK == 0 and D % RET_WIDTH == 0 and (2 * POOL_WIDTH) % D == 0
    pad = (-(seq + N_META)) % CHUNK
    T = seq + N_META + pad
    tm = _pick_tile(T, 528, BF16_ROWS)
    cg = _pick_tile(T // CHUNK, 33, 1)

    shards = {n: _transport(w[n]) for n in _BIG}
    shards["meta"] = meta[None]
    wts = _Weights(shards)
    head = wts.rider([("ffn1_gate", 0), ("ffn1_up", 0), ("meta", 0)])
    _run_rider(head, "weights_gather_head")
    wts.take(head)
    meta_full = jnp.transpose(wts("meta", 0), (1, 0, 2)).reshape(N_META, D)

    reducer = _Reducer(unit=2 * shards["ffn1_gate"][0].size)
    loss_acc, dh, small, d_final = _local_step(x[0], meta_full, loss_target[0], w, wts, pad, tm, cg, reducer)
    grad_x = dh[pad + N_META:][None]
    return _update(loss_acc, grad_x, dh[pad:pad + N_META], reducer, small, d_final, w, mom, var)
```

```python
import functools
import math

import jax
import jax.numpy as jnp
from jax import lax
from jax.experimental import pallas as pl
from jax.experimental.pallas import tpu as pltpu

F32 = jnp.float32
BF16 = jnp.bfloat16

N_META = 16
RET_HEADS = 4
HEAD_DIM = 128
RET_WIDTH = RET_HEADS * HEAD_DIM
POOL_WINDOWS = (2, 4, 8, 16)
POOL_GROUPS = len(POOL_WINDOWS)
POOL_WIDTH = POOL_GROUPS * HEAD_DIM
CHUNK = 128
ROPE_BASE = 10000.0
EPS = 1e-6
ADAM_LR = 0.001
ADAM_B1 = 0.9
ADAM_B2 = 0.999
ADAM_EPS = 1e-08
ADAM_WD = 0.01
ADAM_STEP = 10

N_CHIPS = 4
LANES = 128
BF16_ROWS = 16
V7X_VMEM_LIMIT = 52 * 1024 * 1024
MESH = pl.DeviceIdType.MESH
ANY = pl.BlockSpec(memory_space=pl.ANY)


def _round_up(n, m):
    return -(-n // m) * m


def _pick_tile(n, target, mult):
    best = None
    for d in range(mult, min(n, target) + 1, mult):
        if n % d == 0:
            best = d
    assert best is not None, (n, target, mult)
    return best


def _params(sem=None):
    return pltpu.CompilerParams(dimension_semantics=sem, vmem_limit_bytes=V7X_VMEM_LIMIT)


def _dot(a, b):
    return jnp.dot(a, b, preferred_element_type=F32)


def _dot_nt(a, b):
    return lax.dot_general(a, b, (((1,), (1,)), ((), ())), preferred_element_type=F32)


def _dot_tn(a, b):
    return lax.dot_general(a, b, (((0,), (0,)), ((), ())), preferred_element_type=F32)


def _ein(spec, a, b):
    return jnp.einsum(spec, a, b, preferred_element_type=F32)


def _sigmoid(x):
    return jax.nn.sigmoid(x)


def _rms_fwd(x, gain):
    r = lax.rsqrt(jnp.mean(x * x, axis=-1, keepdims=True) + EPS)
    return x * r * gain


def _rms_bwd(x, gain, da):
    r = lax.rsqrt(jnp.mean(x * x, axis=-1, keepdims=True) + EPS)
    xh = x * r
    dgain = jnp.sum(da * xh, axis=0, keepdims=True)
    dxh = da * gain
    dx = r * (dxh - xh * jnp.mean(dxh * xh, axis=-1, keepdims=True))
    return dx, dgain


def _row_mask(t, tm, pad, shape):
    rows = t * tm + lax.broadcasted_iota(jnp.int32, shape, 0)
    return rows >= pad


def _mesh_pos():
    x, y, c = lax.axis_index("x"), lax.axis_index("y"), lax.axis_index("c")
    others = [(1 - x, y), (x, 1 - y), (1 - x, 1 - y)]
    return x, y, c, 2 * x + y, others


def _half_rows(c, rh):
    return pl.ds(pl.multiple_of(c * rh, rh), rh)


def _remote(src, dst, ssem, rsem, dev):
    return pltpu.make_async_remote_copy(src_ref=src, dst_ref=dst, send_sem=ssem, recv_sem=rsem,
                                        device_id=dev, device_id_type=MESH)


class _Rider:
    def __init__(self, ins, out_shapes, n_sem, start, finish, in_place=False):
        self.ins, self.out_shapes, self.n_sem, self.start, self.finish = ins, out_shapes, n_sem, start, finish
        self.in_place = [in_place] * len(ins)
        self.results = None

    def aliases(self, first_in, first_out):
        return {first_in + i: first_out + i for i, same in enumerate(self.in_place) if same}


class _SemWindow:
    def __init__(self, ref, base):
        self.ref, self.base = ref, base

    @property
    def at(self):
        return self

    def __getitem__(self, k):
        return self.ref.at[self.base + k]


def _join(riders):
    riders = [r for r in riders if r is not None]
    if len(riders) <= 1:
        return riders[0] if riders else None

    def run(which):
        def go(ins, outs, ssem, rsem):
            at, sem = 0, 0
            for r in riders:
                n = len(r.ins)
                getattr(r, which)(ins[at:at + n], outs[at:at + n], _SemWindow(ssem, sem), _SemWindow(rsem, sem))
                at, sem = at + n, sem + r.n_sem
        return go

    joined = _Rider(sum([list(r.ins) for r in riders], []), sum([list(r.out_shapes) for r in riders], []),
                    sum(r.n_sem for r in riders), run("start"), run("finish"))
    joined.in_place = sum([r.in_place for r in riders], [])
    joined.parts = riders
    return joined


def _split_results(rider):
    at = 0
    for r in getattr(rider, "parts", []):
        r.results = rider.results[at:at + len(r.ins)]
        at += len(r.ins)


def _gather_rider(pieces):
    per = 7
    layers = [layer for _, layer in pieces]

    def first_copies(ins, outs, ssem, rsem):
        x, y, c, chip, others = _mesh_pos()
        copies = []
        for i, layer in enumerate(layers):
            mine = _half_rows(c, ins[i].shape[1] // 2)
            for j, (ox, oy) in enumerate(others):
                copies.append(_remote(ins[i].at[layer, mine, :], outs[i].at[chip, mine, :],
                                      ssem.at[per * i + j], rsem.at[per * i + j], (ox, oy, c)))
            copies.append(_remote(ins[i].at[layer], outs[i].at[chip],
                                  ssem.at[per * i + 6], rsem.at[per * i + 6], (x, y, 1 - c)))
        return copies

    def start(ins, outs, ssem, rsem):
        for cp in first_copies(ins, outs, ssem, rsem):
            cp.start()

    def finish(ins, outs, ssem, rsem):
        x, y, c, chip, others = _mesh_pos()
        sibling = (x, y, 1 - c)
        forwards = []
        for i in range(len(layers)):
            mine = _half_rows(c, ins[i].shape[1] // 2)
            for j, (ox, oy) in enumerate(others):
                rows = outs[i].at[2 * ox + oy, mine, :]
                _remote(rows, rows, ssem.at[per * i + j], rsem.at[per * i + j], (ox, oy, c)).wait_recv()
                fwd = _remote(rows, rows, ssem.at[per * i + 3 + j], rsem.at[per * i + 3 + j], sibling)
                fwd.start()
                forwards.append(fwd)
        for i in range(len(layers)):
            theirs = _half_rows(1 - c, ins[i].shape[1] // 2)
            for j, (ox, oy) in enumerate(others):
                rows = outs[i].at[2 * ox + oy, theirs, :]
                _remote(rows, rows, ssem.at[per * i + 3 + j], rsem.at[per * i + 3 + j], sibling).wait_recv()
            own = outs[i].at[chip]
            _remote(own, own, ssem.at[per * i + 6], rsem.at[per * i + 6], sibling).wait_recv()
        for cp in first_copies(ins, outs, ssem, rsem) + forwards:
            cp.wait_send()

    shapes = [jax.ShapeDtypeStruct((N_CHIPS,) + s.shape[1:], s.dtype) for s, _ in pieces]
    return _Rider([s for s, _ in pieces], shapes, per * len(pieces), start, finish)


def _chip_exchange_rider(ps):
    def copies(ins, outs, ssem, rsem):
        x, y, c, chip, others = _mesh_pos()
        return [_remote(ins[i].at[2 * ox + oy], outs[i].at[chip], ssem.at[3 * i + j], rsem.at[3 * i + j], (ox, oy, c))
                for i in range(len(ps)) for j, (ox, oy) in enumerate(others)]

    def start(ins, outs, ssem, rsem):
        for cp in copies(ins, outs, ssem, rsem):
            cp.start()

    def finish(ins, outs, ssem, rsem):
        x, y, c, chip, others = _mesh_pos()
        for i in range(len(ps)):
            for j, (ox, oy) in enumerate(others):
                slot = outs[i].at[2 * ox + oy]
                _remote(slot, slot, ssem.at[3 * i + j], rsem.at[3 * i + j], (ox, oy, c)).wait_recv()
        for cp in copies(ins, outs, ssem, rsem):
            cp.wait_send()

    return _Rider(list(ps), [jax.ShapeDtypeStruct(p.shape, p.dtype) for p in ps], 3 * len(ps), start, finish)


def _pair_exchange_rider(gs):
    def copies(ins, outs, ssem, rsem):
        x, y, c, _, _ = _mesh_pos()
        return [_remote(ins[i].at[:, _half_rows(1 - c, ins[i].shape[1] // 2), :], outs[i],
                        ssem.at[i], rsem.at[i], (x, y, 1 - c)) for i in range(len(gs))]

    def start(ins, outs, ssem, rsem):
        for cp in copies(ins, outs, ssem, rsem):
            cp.start()

    def finish(ins, outs, ssem, rsem):
        for cp in copies(ins, outs, ssem, rsem):
            cp.wait()

    shapes = [jax.ShapeDtypeStruct((g.shape[0], g.shape[1] // 2, g.shape[2]), g.dtype) for g in gs]
    return _Rider(list(gs), shapes, len(gs), start, finish)


def _run_rider(rider, name):
    def body(*refs):
        n = len(rider.ins)
        ins, outs = refs[:n], refs[n:2 * n]
        ssem, rsem = refs[2 * n:]
        rider.start(ins, outs, ssem, rsem)
        rider.finish(ins, outs, ssem, rsem)

    rider.results = pl.pallas_call(
        body,
        name=name,
        in_specs=[ANY] * len(rider.ins),
        out_specs=[ANY] * len(rider.ins),
        out_shape=rider.out_shapes,
        input_output_aliases=rider.aliases(0, 0),
        scratch_shapes=[pltpu.SemaphoreType.DMA((rider.n_sem,)), pltpu.SemaphoreType.DMA((rider.n_sem,))],
    )(*rider.ins)
    return rider.results


def _pair_gather_rider(fs):
    n = len(fs)

    def copies(outs, ssem, rsem):
        x, y, c, _, _ = _mesh_pos()
        halves = [outs[i].at[_half_rows(c, outs[i].shape[0] // 2), :] for i in range(n)]
        return [_remote(h, h, ssem.at[i], rsem.at[i], (x, y, 1 - c)) for i, h in enumerate(halves)]

    def start(ins, outs, ssem, rsem):
        for cp in copies(outs, ssem, rsem):
            cp.start()

    def finish(ins, outs, ssem, rsem):
        x, y, c, _, _ = _mesh_pos()
        for i in range(n):
            theirs = outs[i].at[_half_rows(1 - c, outs[i].shape[0] // 2), :]
            _remote(theirs, theirs, ssem.at[i], rsem.at[i], (x, y, 1 - c)).wait_recv()
        for cp in copies(outs, ssem, rsem):
            cp.wait_send()

    return _Rider(list(fs), [jax.ShapeDtypeStruct(f.shape, f.dtype) for f in fs], n, start, finish, in_place=True)


def _call(body, *, name, grid, in_specs, out_specs, out_shape, operands, scratch=(), sem=None, rider=None):
    if rider is None:
        return pl.pallas_call(
            body, name=name, grid=grid, in_specs=in_specs, out_specs=out_specs, out_shape=out_shape,
            scratch_shapes=list(scratch), compiler_params=_params(sem))(*operands)
    n_in, n_out, n_sc, r = len(in_specs), len(out_specs), len(scratch), len(rider.ins)

    def carrying(*refs):
        a, b = n_in, n_in + r
        c, d = b + n_out, b + n_out + r
        e = d + n_sc
        ids = [pl.program_id(k) for k in range(len(grid))]
        first = functools.reduce(jnp.logical_and, [i == 0 for i in ids])
        last = functools.reduce(jnp.logical_and, [i == g - 1 for i, g in zip(ids, grid)])

        @pl.when(first)
        def _():
            rider.start(refs[a:b], refs[c:d], refs[e], refs[e + 1])

        body(*refs[:a], *refs[b:c], *refs[d:e])

        @pl.when(last)
        def _():
            rider.finish(refs[a:b], refs[c:d], refs[e], refs[e + 1])

    outs = pl.pallas_call(
        carrying, name=name, grid=grid,
        in_specs=list(in_specs) + [ANY] * r,
        out_specs=list(out_specs) + [ANY] * r,
        out_shape=list(out_shape) + list(rider.out_shapes),
        scratch_shapes=list(scratch) + [pltpu.SemaphoreType.DMA((rider.n_sem,)), pltpu.SemaphoreType.DMA((rider.n_sem,))],
        input_output_aliases=rider.aliases(n_in, n_out),
        compiler_params=_params(("arbitrary",) * len(grid)),
    )(*operands, *rider.ins)
    rider.results = outs[n_out:]
    return outs[:n_out]


def _ffn_fwd(h, gain, wg, wu, wd, layer, tm, name, rider=None):
    T, D = h.shape
    Fs = wg.shape[-1]
    F = N_CHIPS * Fs

    def body(h_ref, g_ref, wg_ref, wu_ref, wd_ref, ho_ref, a_ref, go_ref, uo_ref, act_ref, acc_ref):
        s = pl.program_id(1)

        @pl.when(s == 0)
        def _():
            a_ref[...] = _rms_fwd(h_ref[...], g_ref[...]).astype(BF16)
            acc_ref[...] = jnp.zeros_like(acc_ref)

        a = a_ref[...]
        g = _dot(a, wg_ref[...])
        u = _dot(a, wu_ref[...])
        sg = _sigmoid(g)
        act = (g * sg * u).astype(BF16)
        go_ref[...] = (u * (sg * (1.0 + g * (1.0 - sg)))).astype(BF16)
        uo_ref[...] = (g * sg).astype(BF16)
        act_ref[...] = act
        acc_ref[...] += _dot(act, wd_ref[...])

        @pl.when(s == N_CHIPS - 1)
        def _():
            ho_ref[...] = h_ref[...] + 0.5 * acc_ref[...]

    row = pl.BlockSpec((tm, D), lambda t, s: (t, 0))
    col = pl.BlockSpec((tm, Fs), lambda t, s: (t, s))
    wcol = pl.BlockSpec((None, D, Fs), lambda t, s: (s, 0, 0))
    return _call(
        body, name=name, grid=(T // tm, N_CHIPS),
        in_specs=[row, pl.BlockSpec((None, 1, D), lambda t, s: (layer, 0, 0)), wcol, wcol,
                  pl.BlockSpec((None, Fs, D), lambda t, s: (s, 0, 0))],
        out_specs=[row, row, col, col, col],
        out_shape=[jax.ShapeDtypeStruct((T, D), F32), jax.ShapeDtypeStruct((T, D), BF16)]
        + [jax.ShapeDtypeStruct((T, F), BF16)] * 3,
        scratch=[pltpu.VMEM((tm, D), F32)],
        sem=("parallel", "arbitrary"), operands=(h, gain, wg, wu, wd), rider=rider)


def _ffn_fwd_up(h, gain, wg, wu, layer, tm, name, rider=None):
    T, D = h.shape
    Fs = wg.shape[-1]
    F = N_CHIPS * Fs

    def body(h_ref, g_ref, wg_ref, wu_ref, a_ref, go_ref, uo_ref, act_ref):
        @pl.when(pl.program_id(1) == 0)
        def _():
            a_ref[...] = _rms_fwd(h_ref[...], g_ref[...]).astype(BF16)

        a = a_ref[...]
        g = _dot(a, wg_ref[...])
        u = _dot(a, wu_ref[...])
        sg = _sigmoid(g)
        act_ref[...] = (g * sg * u).astype(BF16)
        go_ref[...] = (u * (sg * (1.0 + g * (1.0 - sg)))).astype(BF16)
        uo_ref[...] = (g * sg).astype(BF16)

    row = pl.BlockSpec((tm, D), lambda t, s: (t, 0))
    col = pl.BlockSpec((tm, Fs), lambda t, s: (t, s))
    wcol = pl.BlockSpec((None, D, Fs), lambda t, s: (s, 0, 0))
    return _call(
        body, name=name, grid=(T // tm, N_CHIPS),
        in_specs=[row, pl.BlockSpec((None, 1, D), lambda t, s: (layer, 0, 0)), wcol, wcol],
        out_specs=[row, col, col, col],
        out_shape=[jax.ShapeDtypeStruct((T, D), BF16)] + [jax.ShapeDtypeStruct((T, F), BF16)] * 3,
        sem=("parallel", "arbitrary"), operands=(h, gain, wg, wu), rider=rider)


def _ffn_fwd_down(h, act, wd, tm, name, rider=None):
    T, D = h.shape
    Fs = wd.shape[1]

    def body(h_ref, act_ref, wd_ref, ho_ref, acc_ref):
        s = pl.program_id(1)

        @pl.when(s == 0)
        def _():
            acc_ref[...] = jnp.zeros_like(acc_ref)

        acc_ref[...] += _dot(act_ref[...], wd_ref[...])

        @pl.when(s == N_CHIPS - 1)
        def _():
            ho_ref[...] = h_ref[...] + 0.5 * acc_ref[...]

    row = pl.BlockSpec((tm, D), lambda t, s: (t, 0))
    return _call(
        body, name=name, grid=(T // tm, N_CHIPS),
        in_specs=[row, pl.BlockSpec((tm, Fs), lambda t, s: (t, s)), pl.BlockSpec((None, Fs, D), lambda t, s: (s, 0, 0))],
        out_specs=[row],
        out_shape=[jax.ShapeDtypeStruct((T, D), F32)],
        scratch=[pltpu.VMEM((tm, D), F32)],
        sem=("parallel", "arbitrary"), operands=(h, act, wd), rider=rider)[0]


def _inproj_fwd(h, gain, win, layer, tm, name, rider=None):
    T, D = h.shape
    Ns = win.shape[-1]

    def body(h_ref, g_ref, w_ref, z_ref, b_ref):
        @pl.when(pl.program_id(1) == 0)
        def _():
            b_ref[...] = _rms_fwd(h_ref[...], g_ref[...]).astype(BF16)

        z_ref[...] = _dot(b_ref[...], w_ref[...]).astype(BF16)

    return _call(
        body, name=name, grid=(T // tm, N_CHIPS),
        in_specs=[pl.BlockSpec((tm, D), lambda t, s: (t, 0)),
                  pl.BlockSpec((None, 1, D), lambda t, s: (layer, 0, 0)),
                  pl.BlockSpec((None, D, Ns), lambda t, s: (s, 0, 0))],
        out_specs=[pl.BlockSpec((tm, Ns), lambda t, s: (t, s)), pl.BlockSpec((tm, D), lambda t, s: (t, 0))],
        out_shape=[jax.ShapeDtypeStruct((T, N_CHIPS * Ns), BF16), jax.ShapeDtypeStruct((T, D), BF16)],
        sem=("parallel", "arbitrary"), operands=(h, gain, win), rider=rider)


def _ret_consts(T, pad):
    half = HEAD_DIM // 2
    inv_freq = ROPE_BASE ** (-jnp.arange(half, dtype=F32) / half)
    pos = jnp.arange(T, dtype=F32) - pad
    ang = pos[:, None] * inv_freq[None, :]
    cos = jnp.cos(ang)
    sin = jnp.sin(ang)
    cosf = jnp.concatenate([cos, cos], axis=1)
    sinf = jnp.concatenate([-sin, sin], axis=1)
    log_gamma = jnp.log1p(-(2.0 ** (-5.0 - jnp.arange(RET_HEADS, dtype=F32))))
    idx = jnp.arange(CHUNK, dtype=F32)
    diff = idx[:, None] - idx[None, :]
    intra = jnp.where(diff[None] >= 0, jnp.exp(diff[None] * log_gamma[:, None, None]), 0.0)
    k_decay = jnp.exp((CHUNK - 1.0 - idx)[None, :] * log_gamma[:, None])
    q_decay = jnp.exp((idx + 1.0)[None, :] * log_gamma[:, None])
    chunk_decay = jnp.exp(CHUNK * log_gamma)
    kdec = jnp.broadcast_to(k_decay[:, :, None], (RET_HEADS, CHUNK, HEAD_DIM))
    qdec = jnp.broadcast_to(q_decay[:, :, None], (RET_HEADS, CHUNK, HEAD_DIM))
    cdb = jnp.broadcast_to(chunk_decay[:, None, None], (RET_HEADS, 8, HEAD_DIM))
    return cosf, sinf, intra, kdec, qdec, cdb


def _rot(t, cosv, sinv):
    return t * cosv + pltpu.roll(t, HEAD_DIM // 2, 1) * sinv


def _rot_t(g, cosv, sinv):
    return g * cosv + pltpu.roll(g * sinv, HEAD_DIM // 2, 1)


def _head_specs(tg, section, order):
    return pl.BlockSpec((tg, HEAD_DIM), lambda h, g: (order(g), section * RET_HEADS + h))


def _ret_fwd(z, consts, cg, name, rider=None):
    T = z.shape[0]
    N = T // CHUNK
    ng = N // cg
    tg = cg * CHUNK
    cosf, sinf, intra, kdec, qdec, cdb = consts
    fwd = lambda g: g

    def body(zq, zk, zv, zg, cos_ref, sin_ref, m_ref, kd_ref, qd_ref, cd_ref, r_ref, o_ref, s_ref, st_ref):
        @pl.when(pl.program_id(1) == 0)
        def _():
            st_ref[...] = jnp.zeros_like(st_ref)

        cosv = cos_ref[...]
        sinv = sin_ref[...]
        q3 = (_rot(zq[...].astype(F32), cosv, sinv) * (HEAD_DIM ** -0.5)).reshape(cg, CHUNK, HEAD_DIM)
        k3 = _rot(zk[...].astype(F32), cosv, sinv).reshape(cg, CHUNK, HEAD_DIM)
        vb = zv[...].reshape(cg, CHUNK, HEAD_DIM).astype(BF16)
        scores = _ein("ncd,nmd->ncm", q3.astype(BF16), k3.astype(BF16)) * m_ref[...][None]
        inner = _ein("ncm,nmd->ncd", scores.astype(BF16), vb)
        kv = _ein("ncd,nce->nde", (k3 * kd_ref[...][None]).astype(BF16), vb)
        cd = cd_ref[0:1, :]
        state = st_ref[...]
        for n in range(cg):
            s_ref[n] = state
            state = state * cd + kv[n]
        st_ref[...] = state
        qdb = (q3 * qd_ref[...][None]).astype(BF16)
        cross = _ein("ncd,nde->nce", qdb, s_ref[...].astype(BF16))
        out = (inner + cross).reshape(tg, HEAD_DIM)
        o_ref[...] = out
        xc = out - jnp.mean(out, axis=-1, keepdims=True)
        rn = xc * lax.rsqrt(jnp.mean(xc * xc, axis=-1, keepdims=True) + EPS)
        g = zg[...].astype(F32)
        r_ref[...] = (rn * (g * _sigmoid(g))).astype(BF16)

    tab = pl.BlockSpec((tg, HEAD_DIM), lambda h, g: (g, 0))
    per_head = lambda rows: pl.BlockSpec((None, rows, HEAD_DIM), lambda h, g: (h, 0, 0))
    head_out = pl.BlockSpec((tg, HEAD_DIM), lambda h, g: (g, h))
    return _call(
        body, name=name, grid=(RET_HEADS, ng),
        in_specs=[_head_specs(tg, i, fwd) for i in range(4)]
        + [tab, tab, per_head(CHUNK), per_head(CHUNK), per_head(CHUNK), per_head(8)],
        out_specs=[head_out, head_out, pl.BlockSpec((None, cg, HEAD_DIM, HEAD_DIM), lambda h, g: (h, g, 0, 0))],
        out_shape=[jax.ShapeDtypeStruct((T, RET_WIDTH), BF16), jax.ShapeDtypeStruct((T, RET_WIDTH), F32),
                   jax.ShapeDtypeStruct((RET_HEADS, N, HEAD_DIM, HEAD_DIM), F32)],
        scratch=[pltpu.VMEM((HEAD_DIM, HEAD_DIM), F32)],
        sem=("parallel", "arbitrary"), operands=(z, z, z, z, cosf, sinf, intra, kdec, qdec, cdb), rider=rider)


def _window_sums(u, shift_of):
    sums = []
    s = u
    k = 1
    while k < POOL_WINDOWS[-1]:
        s = s + pltpu.roll(s, shift_of(k), 0)
        sums.append(s)
        k *= 2
    return sums


def _select_group(vals, g):
    out = vals[-1]
    for i in range(len(vals) - 2, -1, -1):
        out = jnp.where(g == i, vals[i], out)
    return out


def _pool_parts(u, g, T, pad):
    rows = lax.broadcasted_iota(jnp.int32, (T, HEAD_DIM), 0)
    valid = rows >= pad
    win = _select_group([float(w) for w in POOL_WINDOWS], g)
    div = jnp.clip((rows - pad + 1).astype(F32), 1.0, win)
    s = _select_group(_window_sums(u, lambda k: k), g)
    pooled = jnp.where(valid, s / div - u, 0.0)
    return pooled, div, valid


def _pool_specs(T, layer):
    first = 4 * RET_WIDTH // HEAD_DIM
    return [
        pl.BlockSpec((T, HEAD_DIM), lambda g: (0, first + g)),
        pl.BlockSpec((None, None, HEAD_DIM, HEAD_DIM), lambda g: (layer, g, 0, 0)),
        pl.BlockSpec((None, 1, HEAD_DIM), lambda g: (layer, 0, g)),
    ]


def _pool_fwd(z, maps, scale, layer, pad, name):
    T = z.shape[0]
    assert pad >= POOL_WINDOWS[-1], "window rolls wrap into the zero rows in front"

    def body(zu, maps_ref, sc_ref, pm_ref):
        g = pl.program_id(0)
        pooled, _, _ = _pool_parts(zu[...].astype(F32), g, T, pad)
        y = _dot(pooled.astype(BF16), maps_ref[...].astype(BF16))
        pm_ref[...] = (y * sc_ref[...]).astype(BF16)

    return _call(
        body, name=name, grid=(POOL_GROUPS,),
        in_specs=_pool_specs(T, layer),
        out_specs=[pl.BlockSpec((T, HEAD_DIM), lambda g: (0, g))],
        out_shape=[jax.ShapeDtypeStruct((T, POOL_WIDTH), BF16)],
        sem=("parallel",), operands=(z, maps, scale))[0]


def _gate_specs(tm, D):
    nb = D // RET_WIDTH
    first = (4 * RET_WIDTH + POOL_WIDTH) // RET_WIDTH
    return [pl.BlockSpec((tm, RET_WIDTH), functools.partial(lambda t, j: (t, j), j=first + j)) for j in range(2 * nb)]


def _load_gates(refs, nb):
    ga = jnp.concatenate([r[...].astype(F32) for r in refs[:nb]], axis=1)
    gb = jnp.concatenate([r[...].astype(F32) for r in refs[nb:]], axis=1)
    return ga, gb


def _mix_fwd(h, r, pm, z, wru, wpu, wout, tm, name, rider=None):
    T, D = h.shape
    Dq = D // N_CHIPS
    nb = D // RET_WIDTH

    def body(*refs):
        h_ref, r_ref, pm_ref = refs[:3]
        gate_refs = refs[3:3 + 2 * nb]
        wru_ref, wpu_ref, wout_ref, ho_ref, mx_ref, ret_ref, pool_ref = refs[3 + 2 * nb:]
        rv = r_ref[...]
        pv = pm_ref[...]
        ret = jnp.concatenate([_dot(rv, wru_ref[s]) for s in range(N_CHIPS)], axis=1)
        pool = jnp.concatenate([_dot(pv, wpu_ref[s]) for s in range(N_CHIPS)], axis=1)
        ga, gb = _load_gates(gate_refs, nb)
        mixed = (_sigmoid(ga) * ret + _sigmoid(gb) * pool).astype(BF16)
        mx_ref[...] = mixed
        ret_ref[...] = ret.astype(BF16)
        pool_ref[...] = pool.astype(BF16)
        ho_ref[...] = h_ref[...] + _dot(mixed, wout_ref[...].reshape(D, D))

    row = pl.BlockSpec((tm, D), lambda t: (t, 0))
    half = pl.BlockSpec((tm, RET_WIDTH), lambda t: (t, 0))
    up = pl.BlockSpec((N_CHIPS, RET_WIDTH, Dq), lambda t: (0, 0, 0))
    return _call(
        body, name=name, grid=(T // tm,),
        in_specs=[row, half, half] + _gate_specs(tm, D) + [up, up, pl.BlockSpec((N_CHIPS, Dq, D), lambda t: (0, 0, 0))],
        out_specs=[row, row, row, row],
        out_shape=[jax.ShapeDtypeStruct((T, D), F32)] + [jax.ShapeDtypeStruct((T, D), BF16)] * 3,
        sem=("parallel",), operands=(h, r, pm, *([z] * (2 * nb)), wru, wpu, wout), rider=rider)


def _final_loss(h, gain, tgt, name):
    T, D = h.shape
    first = (T - tgt.shape[0]) // CHUNK

    def body(h_ref, g_ref, t_ref, dh_ref, loss_ref, dg_ref):
        i = pl.program_id(0)

        @pl.when(i == 0)
        def _():
            loss_ref[...] = jnp.zeros_like(loss_ref)
            dg_ref[...] = jnp.zeros_like(dg_ref)

        x = h_ref[...]
        gain_v = g_ref[...]
        err = jnp.where(i >= first, _rms_fwd(x, gain_v) - t_ref[...], 0.0)
        loss_ref[...] += 0.5 * jnp.sum(jnp.mean(err * err, axis=-1))
        dx, dgain = _rms_bwd(x, gain_v, err * (1.0 / D))
        dg_ref[...] += dgain
        dh_ref[...] = dx

    return _call(
        body, name=name, grid=(T // CHUNK,),
        in_specs=[pl.BlockSpec((CHUNK, D), lambda i: (i, 0)),
                  pl.BlockSpec((1, D), lambda i: (0, 0)),
                  pl.BlockSpec((CHUNK, D), lambda i: (jnp.maximum(i - first, 0), 0))],
        out_specs=[pl.BlockSpec((CHUNK, D), lambda i: (i, 0)),
                   pl.BlockSpec((1, LANES), lambda i: (0, 0)),
                   pl.BlockSpec((1, D), lambda i: (0, 0))],
        out_shape=[jax.ShapeDtypeStruct((T, D), F32), jax.ShapeDtypeStruct((1, LANES), F32),
                   jax.ShapeDtypeStruct((1, D), F32)],
        sem=("arbitrary",), operands=(h, gain, tgt))


def _ffn_bwd_act(dy, g, u, wd, tm, name, rider=None):
    T, D = dy.shape
    Fs = wd.shape[1]
    F = N_CHIPS * Fs

    def body(dy_ref, go_ref, uo_ref, wd_ref, dg_ref, du_ref, dyh_ref):
        @pl.when(pl.program_id(1) == 0)
        def _():
            dyh_ref[...] = (0.5 * dy_ref[...]).astype(BF16)

        dact = _dot_nt(dyh_ref[...], wd_ref[...])
        du_ref[...] = (dact * uo_ref[...].astype(F32)).astype(BF16)
        dg_ref[...] = (dact * go_ref[...].astype(F32)).astype(BF16)

    row = pl.BlockSpec((tm, D), lambda t, s: (t, 0))
    col = pl.BlockSpec((tm, Fs), lambda t, s: (t, s))
    return _call(
        body, name=name, grid=(T // tm, N_CHIPS),
        in_specs=[row, col, col, pl.BlockSpec((None, Fs, D), lambda t, s: (s, 0, 0))],
        out_specs=[col, col, row],
        out_shape=[jax.ShapeDtypeStruct((T, F), BF16), jax.ShapeDtypeStruct((T, F), BF16),
                   jax.ShapeDtypeStruct((T, D), BF16)],
        sem=("parallel", "arbitrary"), operands=(dy, g, u, wd), rider=rider)


def _ffn_bwd_in(dy, h, gain, dg, du, wg, wu, layer, tm, pad, name, rider=None):
    T, D = h.shape
    Fs = wg.shape[-1]

    def body(dy_ref, h_ref, g_ref, dg_ref, du_ref, wg_ref, wu_ref, dh_ref, dgain_ref, da_ref):
        t = pl.program_id(0)
        s = pl.program_id(1)

        @pl.when((t == 0) & (s == 0))
        def _():
            dgain_ref[...] = jnp.zeros_like(dgain_ref)

        @pl.when(s == 0)
        def _():
            da_ref[...] = jnp.zeros_like(da_ref)

        da_ref[...] += _dot_nt(dg_ref[...], wg_ref[...]) + _dot_nt(du_ref[...], wu_ref[...])

        @pl.when(s == N_CHIPS - 1)
        def _():
            dx, dgain = _rms_bwd(h_ref[...], g_ref[...], da_ref[...])
            dgain_ref[...] += dgain
            dh_ref[...] = jnp.where(_row_mask(t, tm, pad, (tm, D)), dy_ref[...] + dx, 0.0)

    row = pl.BlockSpec((tm, D), lambda t, s: (t, 0))
    col = pl.BlockSpec((tm, Fs), lambda t, s: (t, s))
    wcol = pl.BlockSpec((None, D, Fs), lambda t, s: (s, 0, 0))
    return _call(
        body, name=name, grid=(T // tm, N_CHIPS),
        in_specs=[row, row, pl.BlockSpec((None, 1, D), lambda t, s: (layer, 0, 0)), col, col, wcol, wcol],
        out_specs=[row, pl.BlockSpec((1, D), lambda t, s: (0, 0))],
        out_shape=[jax.ShapeDtypeStruct((T, D), F32), jax.ShapeDtypeStruct((1, D), F32)],
        scratch=[pltpu.VMEM((tm, D), F32)],
        sem=("arbitrary", "arbitrary"), operands=(dy, h, gain, dg, du, wg, wu), rider=rider)


def _grad_tn(a, b, mode, scale, tm, name, rider=None):
    T = a.shape[0]
    if mode == "col":
        per, R, C = 1, a.shape[1], b.shape[1] // N_CHIPS
        a_spec = pl.BlockSpec((tm, R), lambda s, t: (t, 0))
        b_spec = pl.BlockSpec((tm, C), lambda s, t: (t, s))
    else:
        per, R, C = 2, a.shape[1] // N_CHIPS, b.shape[1]
        a_spec = pl.BlockSpec((tm, per * R), lambda s, t: (t, s))
        b_spec = pl.BlockSpec((tm, C), lambda s, t: (t, 0))
    nt = T // tm

    def body(a_ref, b_ref, o_ref, acc_ref):
        t = pl.program_id(1)

        @pl.when(t == 0)
        def _():
            acc_ref[...] = jnp.zeros_like(acc_ref)

        acc_ref[...] += _dot_tn(a_ref[...].astype(BF16), b_ref[...].astype(BF16))

        @pl.when(t == nt - 1)
        def _():
            o_ref[...] = (scale * acc_ref[...]).astype(BF16).reshape(per, R, C)

    return _call(
        body, name=name, grid=(N_CHIPS // per, nt),
        in_specs=[a_spec, b_spec],
        out_specs=[pl.BlockSpec((per, R, C), lambda s, t: (s, 0, 0))],
        out_shape=[jax.ShapeDtypeStruct((N_CHIPS, R, C), BF16)],
        scratch=[pltpu.VMEM((per * R, C), F32)],
        sem=("parallel", "arbitrary"), operands=(a, b), rider=rider)[0]


def _grad_mix(mixed, dh, r, dret, pm, dpool, tk, name, rider=None):
    T, D = dh.shape
    Dq = D // N_CHIPS
    nt = T // tk

    def body(mx_ref, dh_ref, r_ref, dret_ref, pm_ref, dpool_ref, go_ref, gr_ref, gp_ref, ao_ref, ar_ref, ap_ref):
        t = pl.program_id(0)

        @pl.when(t == 0)
        def _():
            ao_ref[...] = jnp.zeros_like(ao_ref)
            ar_ref[...] = jnp.zeros_like(ar_ref)
            ap_ref[...] = jnp.zeros_like(ap_ref)

        ao_ref[...] += _dot_tn(mx_ref[...], dh_ref[...].astype(BF16))
        ar_ref[...] += _dot_tn(r_ref[...], dret_ref[...])
        ap_ref[...] += _dot_tn(pm_ref[...], dpool_ref[...])

        @pl.when(t == nt - 1)
        def _():
            go_ref[...] = ao_ref[...].astype(BF16).reshape(N_CHIPS, Dq, D)
            for s in range(N_CHIPS):
                gr_ref[s] = ar_ref[:, s * Dq:(s + 1) * Dq].astype(BF16)
                gp_ref[s] = ap_ref[:, s * Dq:(s + 1) * Dq].astype(BF16)

    row = pl.BlockSpec((tk, D), lambda t: (t, 0))
    half = pl.BlockSpec((tk, RET_WIDTH), lambda t: (t, 0))
    whole = lambda shape: pl.BlockSpec(shape, lambda t: (0, 0, 0))
    return _call(
        body, name=name, grid=(nt,),
        in_specs=[row, row, half, row, half, row],
        out_specs=[whole((N_CHIPS, Dq, D)), whole((N_CHIPS, RET_WIDTH, Dq)), whole((N_CHIPS, POOL_WIDTH, Dq))],
        out_shape=[jax.ShapeDtypeStruct((N_CHIPS, Dq, D), BF16),
                   jax.ShapeDtypeStruct((N_CHIPS, RET_WIDTH, Dq), BF16),
                   jax.ShapeDtypeStruct((N_CHIPS, POOL_WIDTH, Dq), BF16)],
        scratch=[pltpu.VMEM((D, D), F32), pltpu.VMEM((RET_WIDTH, D), F32), pltpu.VMEM((POOL_WIDTH, D), F32)],
        sem=("arbitrary",), operands=(mixed, dh, r, dret, pm, dpool), rider=rider)


def _mix_bwd_dx(dh, z, ret, pool, wout, wru, wpu, tm, name, rider=None):
    T, D = dh.shape
    Dq = D // N_CHIPS
    nb = D // RET_WIDTH

    def body(*refs):
        dh_ref = refs[0]
        gate_refs = refs[1:1 + 2 * nb]
        ret_ref, pool_ref, wout_ref, wru_ref, wpu_ref, dgab_ref, dret_ref, dpool_ref, dr_ref, dpm_ref = refs[1 + 2 * nb:]
        dmixed = _dot_nt(dh_ref[...].astype(BF16), wout_ref[...].reshape(D, D))
        ga, gb = _load_gates(gate_refs, nb)
        sa = _sigmoid(ga)
        sb = _sigmoid(gb)
        dgab_ref[:, :D] = (dmixed * ret_ref[...].astype(F32) * (sa * (1.0 - sa))).astype(BF16)
        dgab_ref[:, D:] = (dmixed * pool_ref[...].astype(F32) * (sb * (1.0 - sb))).astype(BF16)
        dret = (dmixed * sa).astype(BF16)
        dpool = (dmixed * sb).astype(BF16)
        dret_ref[...] = dret
        dpool_ref[...] = dpool
        dr = _dot_nt(dret[:, :Dq], wru_ref[0])
        dpm = _dot_nt(dpool[:, :Dq], wpu_ref[0])
        for s in range(1, N_CHIPS):
            dr += _dot_nt(dret[:, s * Dq:(s + 1) * Dq], wru_ref[s])
            dpm += _dot_nt(dpool[:, s * Dq:(s + 1) * Dq], wpu_ref[s])
        dr_ref[...] = dr
        dpm_ref[...] = dpm

    row = pl.BlockSpec((tm, D), lambda t: (t, 0))
    half = pl.BlockSpec((tm, RET_WIDTH), lambda t: (t, 0))
    up = pl.BlockSpec((N_CHIPS, RET_WIDTH, Dq), lambda t: (0, 0, 0))
    return _call(
        body, name=name, grid=(T // tm,),
        in_specs=[row] + _gate_specs(tm, D) + [row, row, pl.BlockSpec((N_CHIPS, Dq, D), lambda t: (0, 0, 0)), up, up],
        out_specs=[pl.BlockSpec((tm, 2 * D), lambda t: (t, 0)), row, row, half, half],
        out_shape=[jax.ShapeDtypeStruct((T, 2 * D), BF16), jax.ShapeDtypeStruct((T, D), BF16),
                   jax.ShapeDtypeStruct((T, D), BF16), jax.ShapeDtypeStruct((T, RET_WIDTH), F32),
                   jax.ShapeDtypeStruct((T, POOL_WIDTH), F32)],
        sem=("parallel",), operands=(dh, *([z] * (2 * nb)), ret, pool, wout, wru, wpu), rider=rider)


def _pool_bwd(z, dpm, maps, scale, layer, pad, name):
    T = z.shape[0]

    def body(zu, maps_ref, sc_ref, dpm_ref, du_ref, dmaps_ref, dsc_ref):
        g = pl.program_id(0)
        u = zu[...].astype(F32)
        pooled, div, valid = _pool_parts(u, g, T, pad)
        pb = pooled.astype(BF16)
        mb = maps_ref[...].astype(BF16)
        dp = dpm_ref[...]
        dsc_ref[...] = jnp.sum(dp * _dot(pb, mb), axis=0, keepdims=True)
        dyb = (dp * sc_ref[...]).astype(BF16)
        dmaps_ref[...] = _dot_tn(pb, dyb)
        dpooled = jnp.where(valid, _dot_nt(dyb, mb), 0.0)
        ahead = _select_group(_window_sums(dpooled / div, lambda k: T - k), g)
        du_ref[...] = jnp.where(valid, ahead - dpooled, 0.0).astype(BF16)

    blk = pl.BlockSpec((T, HEAD_DIM), lambda g: (0, g))
    return _call(
        body, name=name, grid=(POOL_GROUPS,),
        in_specs=_pool_specs(T, layer) + [blk],
        out_specs=[blk, pl.BlockSpec((None, HEAD_DIM, HEAD_DIM), lambda g: (g, 0, 0)),
                   pl.BlockSpec((1, HEAD_DIM), lambda g: (0, g))],
        out_shape=[jax.ShapeDtypeStruct((T, POOL_WIDTH), BF16),
                   jax.ShapeDtypeStruct((POOL_GROUPS, HEAD_DIM, HEAD_DIM), F32),
                   jax.ShapeDtypeStruct((1, POOL_WIDTH), F32)],
        sem=("parallel",), operands=(z, maps, scale, dpm))


def _ret_bwd_local(z, o_pre, s_all, dr, consts, cg, name):
    T = z.shape[0]
    N = T // CHUNK
    ng = N // cg
    tg = cg * CHUNK
    cosf, sinf, intra, _, qdec, _ = consts
    fwd = lambda g: g

    def body(zq, zk, zv, zg, o_ref, s_ref, dr_ref, cos_ref, sin_ref, m_ref, qd_ref,
             dq_ref, dg_ref, dk_ref, dv_ref, ds_ref):
        cosv = cos_ref[...]
        sinv = sin_ref[...]
        scale = HEAD_DIM ** -0.5
        q3 = (_rot(zq[...].astype(F32), cosv, sinv) * scale).reshape(cg, CHUNK, HEAD_DIM)
        k3 = _rot(zk[...].astype(F32), cosv, sinv).reshape(cg, CHUNK, HEAD_DIM)
        qb = q3.astype(BF16)
        kb = k3.astype(BF16)
        vb = zv[...].reshape(cg, CHUNK, HEAD_DIM).astype(BF16)
        mask = m_ref[...][None]
        sb = (_ein("ncd,nmd->ncm", qb, kb) * mask).astype(BF16)
        qdv = qd_ref[...][None]
        qdb = (q3 * qdv).astype(BF16)

        out = o_ref[...]
        xc = out - jnp.mean(out, axis=-1, keepdims=True)
        rstd = lax.rsqrt(jnp.mean(xc * xc, axis=-1, keepdims=True) + EPS)
        rn = xc * rstd
        g = zg[...].astype(F32)
        sg = _sigmoid(g)
        drv = dr_ref[...]
        dg_ref[...] = (drv * rn * (sg * (1.0 + g * (1.0 - sg)))).astype(BF16)
        drn = drv * (g * sg)
        dout = rstd * (drn - jnp.mean(drn, axis=-1, keepdims=True)
                       - rn * jnp.mean(drn * rn, axis=-1, keepdims=True))
        dob = dout.reshape(cg, CHUNK, HEAD_DIM).astype(BF16)

        dsb = (_ein("ncd,nmd->ncm", dob, vb) * mask).astype(BF16)
        dv_ref[...] = _ein("ncm,ncd->nmd", sb, dob).reshape(tg, HEAD_DIM)
        dk_ref[...] = _ein("ncm,ncd->nmd", dsb, qb).reshape(tg, HEAD_DIM)
        dq3 = _ein("ncm,nmd->ncd", dsb, kb) + _ein("nce,nde->ncd", dob, s_ref[...].astype(BF16)) * qdv
        dq_ref[...] = _rot_t(dq3.reshape(tg, HEAD_DIM) * scale, cosv, sinv).astype(BF16)
        ds_ref[...] = _ein("ncd,nce->nde", qdb, dob)

    tab = pl.BlockSpec((tg, HEAD_DIM), lambda h, g: (g, 0))
    per_head = pl.BlockSpec((None, CHUNK, HEAD_DIM), lambda h, g: (h, 0, 0))
    head_blk = pl.BlockSpec((tg, HEAD_DIM), lambda h, g: (g, h))
    state_blk = pl.BlockSpec((None, cg, HEAD_DIM, HEAD_DIM), lambda h, g: (h, g, 0, 0))
    return _call(
        body, name=name, grid=(RET_HEADS, ng),
        in_specs=[_head_specs(tg, i, fwd) for i in range(4)]
        + [head_blk, state_blk, head_blk, tab, tab, per_head, per_head],
        out_specs=[head_blk, head_blk, head_blk, head_blk, state_blk],
        out_shape=[jax.ShapeDtypeStruct((T, RET_WIDTH), BF16), jax.ShapeDtypeStruct((T, RET_WIDTH), BF16),
                   jax.ShapeDtypeStruct((T, RET_WIDTH), F32), jax.ShapeDtypeStruct((T, RET_WIDTH), F32),
                   jax.ShapeDtypeStruct((RET_HEADS, N, HEAD_DIM, HEAD_DIM), F32)],
        sem=("parallel", "parallel"), operands=(z, z, z, z, o_pre, s_all, dr, cosf, sinf, intra, qdec))


def _ret_bwd_state(z, dkp, dvp, ds, consts, cg, name):
    T = z.shape[0]
    N = T // CHUNK
    ng = N // cg
    tg = cg * CHUNK
    cosf, sinf, _, kdec, _, cdb = consts
    rev = lambda g: ng - 1 - g

    def body(zk, zv, dkp_ref, dvp_ref, ds_ref, cos_ref, sin_ref, kd_ref, cd_ref, dk_ref, dv_ref, gs_ref, dkv_ref):
        @pl.when(pl.program_id(1) == 0)
        def _():
            gs_ref[...] = jnp.zeros_like(gs_ref)

        cosv = cos_ref[...]
        sinv = sin_ref[...]
        cd = cd_ref[0:1, :]
        grad = gs_ref[...]
        for n in reversed(range(cg)):
            dkv_ref[n] = grad
            grad = ds_ref[n] + cd * grad
        gs_ref[...] = grad
        dkvb = dkv_ref[...].astype(BF16)
        kdv = kd_ref[...][None]
        k3 = _rot(zk[...].astype(F32), cosv, sinv).reshape(cg, CHUNK, HEAD_DIM)
        vb = zv[...].reshape(cg, CHUNK, HEAD_DIM).astype(BF16)
        dk3 = _ein("nce,nde->ncd", vb, dkvb) * kdv
        dv3 = _ein("ncd,nde->nce", (k3 * kdv).astype(BF16), dkvb)
        dk_ref[...] = _rot_t(dkp_ref[...] + dk3.reshape(tg, HEAD_DIM), cosv, sinv).astype(BF16)
        dv_ref[...] = (dvp_ref[...] + dv3.reshape(tg, HEAD_DIM)).astype(BF16)

    tab = pl.BlockSpec((tg, HEAD_DIM), lambda h, g: (rev(g), 0))
    head_blk = pl.BlockSpec((tg, HEAD_DIM), lambda h, g: (rev(g), h))
    return _call(
        body, name=name, grid=(RET_HEADS, ng),
        in_specs=[_head_specs(tg, 1, rev), _head_specs(tg, 2, rev), head_blk, head_blk,
                  pl.BlockSpec((None, cg, HEAD_DIM, HEAD_DIM), lambda h, g: (h, rev(g), 0, 0)),
                  tab, tab,
                  pl.BlockSpec((None, CHUNK, HEAD_DIM), lambda h, g: (h, 0, 0)),
                  pl.BlockSpec((None, 8, HEAD_DIM), lambda h, g: (h, 0, 0))],
        out_specs=[head_blk, head_blk],
        out_shape=[jax.ShapeDtypeStruct((T, RET_WIDTH), BF16)] * 2,
        scratch=[pltpu.VMEM((HEAD_DIM, HEAD_DIM), F32), pltpu.VMEM((cg, HEAD_DIM, HEAD_DIM), F32)],
        sem=("parallel", "arbitrary"), operands=(z, z, dkp, dvp, ds, cosf, sinf, kdec, cdb))


def _z_segments(pieces, ns):
    segs, at = [], 0
    for k, p in enumerate(pieces):
        width = p.shape[1]
        lo = at
        while lo < at + width:
            s = lo // ns
            hi = min(at + width, (s + 1) * ns)
            segs.append((k, lo - at, hi - at, s, lo - s * ns, hi - s * ns))
            lo = hi
        at += width
    assert at == N_CHIPS * ns and all(v % LANES == 0 for seg in segs for v in (seg[1], seg[2], seg[4], seg[5]))
    return segs


def _inproj_bwd_dx(pieces, win, h, gain, dh_in, layer, tm, pad, name, rider=None):
    T, D = h.shape
    Ns = win.shape[-1]
    n = len(pieces)
    segs = _z_segments(pieces, Ns)

    def body(*refs):
        piece_refs = refs[:n]
        w_ref, h_ref, g_ref, dhi_ref, dh_ref, dgain_ref = refs[n:]
        t = pl.program_id(0)

        @pl.when(t == 0)
        def _():
            dgain_ref[...] = jnp.zeros_like(dgain_ref)

        db = None
        for k, a, b, s, c, d in segs:
            term = _dot_nt(piece_refs[k][:, a:b], w_ref[s, :, c:d])
            db = term if db is None else db + term
        dx, dgain = _rms_bwd(h_ref[...], g_ref[...], db)
        dgain_ref[...] += dgain
        dh_ref[...] = jnp.where(_row_mask(t, tm, pad, (tm, D)), dhi_ref[...] + dx, 0.0)

    row = pl.BlockSpec((tm, D), lambda t: (t, 0))
    return _call(
        body, name=name, grid=(T // tm,),
        in_specs=[pl.BlockSpec((tm, p.shape[1]), lambda t: (t, 0)) for p in pieces]
        + [pl.BlockSpec((N_CHIPS, D, Ns), lambda t: (0, 0, 0)), row,
           pl.BlockSpec((None, 1, D), lambda t: (layer, 0, 0)), row],
        out_specs=[row, pl.BlockSpec((1, D), lambda t: (0, 0))],
        out_shape=[jax.ShapeDtypeStruct((T, D), F32), jax.ShapeDtypeStruct((1, D), F32)],
        sem=("arbitrary",), operands=(*pieces, win, h, gain, dh_in), rider=rider)


def _grad_w_in(b, pieces, ns, tk, name, rider=None):
    T, D = b.shape
    n = len(pieces)
    nt = T // tk
    segs = _z_segments(pieces, ns)
    shards_of = [sorted({s for k, _, _, s, _, _ in segs if k == i}) for i in range(n)]

    def body(*refs):
        b_ref = refs[0]
        piece_refs = refs[1:1 + n]
        o_ref, acc_ref = refs[1 + n:]
        s = pl.program_id(0)
        t = pl.program_id(1)

        @pl.when(t == 0)
        def _():
            acc_ref[...] = jnp.zeros_like(acc_ref)

        for shard in range(N_CHIPS):
            @pl.when(s == shard)
            def _(shard=shard):
                cols = [piece_refs[k][:, a:e] for k, a, e, ss, _, _ in segs if ss == shard]
                dz = cols[0] if len(cols) == 1 else jnp.concatenate(cols, axis=1)
                acc_ref[...] += _dot_tn(b_ref[...], dz)

        @pl.when(t == nt - 1)
        def _():
            o_ref[...] = acc_ref[...].astype(BF16).reshape(1, D, ns)

    def piece_spec(i):
        def index(s, t):
            used = functools.reduce(jnp.logical_or, [s == ss for ss in shards_of[i]])
            return (jnp.where(used, t, 0), 0)
        return pl.BlockSpec((tk, pieces[i].shape[1]), index)

    return _call(
        body, name=name, grid=(N_CHIPS, nt),
        in_specs=[pl.BlockSpec((tk, D), lambda s, t: (t, 0))] + [piece_spec(i) for i in range(n)],
        out_specs=[pl.BlockSpec((1, D, ns), lambda s, t: (s, 0, 0))],
        out_shape=[jax.ShapeDtypeStruct((N_CHIPS, D, ns), BF16)],
        scratch=[pltpu.VMEM((D, ns), F32)],
        sem=("parallel", "arbitrary"), operands=(b, *pieces), rider=rider)[0]


def _sum_pair(gs, rs, c_idx, name):
    n = len(gs)

    def body(c_ref, *refs):
        for g_ref, r_ref, o_ref in zip(refs[:n], refs[n:2 * n], refs[2 * n:]):
            o_ref[...] = (g_ref[...].astype(F32) + r_ref[...].astype(F32)).astype(BF16)

    halves = [pl.BlockSpec((None,) + r.shape[1:], lambda s, c_ref: (s, 0, 0)) for r in rs]
    return pl.pallas_call(
        body,
        name=name,
        grid_spec=pltpu.PrefetchScalarGridSpec(
            num_scalar_prefetch=1,
            grid=(N_CHIPS,),
            in_specs=[pl.BlockSpec((None,) + r.shape[1:], lambda s, c_ref: (s, c_ref[0], 0)) for r in rs] + halves,
            out_specs=halves,
        ),
        out_shape=[jax.ShapeDtypeStruct(r.shape, BF16) for r in rs],
        compiler_params=_params(("parallel",)),
    )(c_idx, *gs, *rs)


def _sum_chips(ps, rs, pos, name):
    n = len(ps)
    quarters = 4

    def body(pos_ref, *refs):
        chip = pos_ref[0]
        for p_ref, r_ref, o_ref in zip(refs[:n], refs[n:2 * n], refs[2 * n:]):
            own = p_ref[...].astype(F32)
            terms = [jnp.where(chip == k, own, r_ref[k].astype(F32)) for k in range(N_CHIPS)]
            o_ref[...] = ((terms[0] + terms[1]) + terms[2]) + terms[3]

    def rows(r):
        assert r.shape[1] % (quarters * BF16_ROWS) == 0, r.shape
        return r.shape[1] // quarters

    return pl.pallas_call(
        body,
        name=name,
        grid_spec=pltpu.PrefetchScalarGridSpec(
            num_scalar_prefetch=1,
            grid=(quarters,),
            in_specs=[pl.BlockSpec((None, rows(r), r.shape[2]), lambda q, pos_ref: (pos_ref[0], q, 0)) for r in rs]
            + [pl.BlockSpec((N_CHIPS, rows(r), r.shape[2]), lambda q, pos_ref: (0, q, 0)) for r in rs],
            out_specs=[pl.BlockSpec((rows(r), r.shape[2]), lambda q, pos_ref: (pos_ref[1] * quarters + q, 0))
                       for r in rs],
        ),
        out_shape=[jax.ShapeDtypeStruct((2 * r.shape[1], r.shape[2]), F32) for r in rs],
        compiler_params=_params(("arbitrary",)),
    )(pos, *ps, *rs)


def _small_all_reduce(p, rider=None):
    rows, width = p.shape
    r = 0 if rider is None else len(rider.ins)

    def body(*refs):
        p_ref, o_ref = refs[0], refs[1 + r]
        sib_ref, slot_ref, ssem, rsem = refs[2 + 2 * r:6 + 2 * r]
        if rider is not None:
            rider.start(refs[1:1 + r], refs[2 + r:2 + 2 * r], refs[6 + 2 * r], refs[7 + 2 * r])
        reduce(p_ref, o_ref, sib_ref, slot_ref, ssem, rsem)
        if rider is not None:
            rider.finish(refs[1:1 + r], refs[2 + r:2 + 2 * r], refs[6 + 2 * r], refs[7 + 2 * r])

    def reduce(p_ref, o_ref, sib_ref, slot_ref, ssem, rsem):
        x, y, c, chip, others = _mesh_pos()
        pair = _remote(p_ref, sib_ref, ssem.at[0], rsem.at[0], (x, y, 1 - c))
        pair.start()
        pair.wait()
        slot_ref[chip] = p_ref[...] + sib_ref[...]
        sends = []
        for j, (ox, oy) in enumerate(others):
            cp = _remote(slot_ref.at[chip], slot_ref.at[chip], ssem.at[1 + j], rsem.at[1 + j], (ox, oy, c))
            cp.start()
            sends.append(cp)
        for j, (ox, oy) in enumerate(others):
            slot = slot_ref.at[2 * ox + oy]
            _remote(slot, slot, ssem.at[1 + j], rsem.at[1 + j], (ox, oy, c)).wait_recv()
        for cp in sends:
            cp.wait_send()
        o_ref[...] = ((slot_ref[0] + slot_ref[1]) + slot_ref[2]) + slot_ref[3]

    vmem = pl.BlockSpec(memory_space=pltpu.VMEM)
    scratch = [pltpu.VMEM((rows, width), F32), pltpu.VMEM((N_CHIPS, rows, width), F32),
               pltpu.SemaphoreType.DMA((4,)), pltpu.SemaphoreType.DMA((4,))]
    if rider is not None:
        scratch += [pltpu.SemaphoreType.DMA((rider.n_sem,)), pltpu.SemaphoreType.DMA((rider.n_sem,))]
    outs = pl.pallas_call(
        body,
        name="small_grads_all_reduce",
        in_specs=[vmem] + [ANY] * r,
        out_specs=[vmem] + [ANY] * r,
        out_shape=[jax.ShapeDtypeStruct(p.shape, F32)] + ([] if rider is None else list(rider.out_shapes)),
        input_output_aliases={} if rider is None else rider.aliases(1, 1),
        scratch_shapes=scratch,
    )(p, *([] if rider is None else rider.ins))
    if rider is not None:
        rider.results = outs[1:]
    return outs[0]


def _adamw(gs, w, m, v, name):
    L, R, C = w.shape
    Ct = gs[0].shape[1]
    tr = _pick_tile(R, 512, 8)

    def body(*refs):
        g_refs = refs[:L]
        w_ref, m_ref, v_ref, go_ref, d_ref, mo_ref, vo_ref = refs[L:]
        layer = pl.program_id(0)
        grad = g_refs[L - 1][...]
        for i in range(L - 2, -1, -1):
            grad = jnp.where(layer == i, g_refs[i][...], grad)
        if Ct != C:
            grad = grad[:, :C]
        m_new = ADAM_B1 * m_ref[...] + (1.0 - ADAM_B1) * grad
        v_new = ADAM_B2 * v_ref[...] + (1.0 - ADAM_B2) * jnp.square(grad)
        m_hat = m_new / (1.0 - ADAM_B1 ** ADAM_STEP)
        v_hat = v_new / (1.0 - ADAM_B2 ** ADAM_STEP)
        go_ref[...] = grad
        d_ref[...] = -ADAM_LR * (m_hat / (jnp.sqrt(v_hat) + ADAM_EPS) + ADAM_WD * w_ref[...])
        mo_ref[...] = m_new
        vo_ref[...] = v_new

    g_specs = [pl.BlockSpec((tr, Ct), functools.partial(lambda l, r, i: (jnp.where(l == i, r, 0), 0), i=i))
               for i in range(L)]
    blk = pl.BlockSpec((None, tr, C), lambda l, r: (l, r, 0))
    return _call(
        body, name=name, grid=(L, R // tr),
        in_specs=g_specs + [blk, blk, blk],
        out_specs=[blk] * 4,
        out_shape=[jax.ShapeDtypeStruct((L, R, C), F32)] * 4,
        sem=("arbitrary", "arbitrary"), operands=(*gs, w, m, v))


_FFN1 = ("ffn1_gate", "ffn1_up", "ffn1_down")
_FFN2 = ("ffn2_gate", "ffn2_up", "ffn2_down")
_MIXW = ("w_ret_up", "w_pool_up", "w_out")
_BIG = _FFN1 + ("w_in",) + _MIXW + _FFN2
_TRANSPOSED = ("ffn1_gate", "ffn1_up", "ffn2_gate", "ffn2_up")
_SMALL = ("ffn1_norm", "mix_norm", "ffn2_norm", "final_norm", "pool_scale", "pool_maps")
_ORDER = ("meta", "ffn1_norm", "ffn1_gate", "ffn1_up", "ffn1_down", "mix_norm", "w_in", "pool_maps",
          "pool_scale", "w_ret_up", "w_pool_up", "w_out", "ffn2_norm", "ffn2_gate", "ffn2_up", "ffn2_down",
          "final_norm")


def _transport(a):
    n, r, c = a.shape
    out = a.astype(BF16)
    if c % LANES:
        out = jnp.concatenate([out, jnp.zeros((n, r, _round_up(c, LANES) - c), BF16)], axis=2)
    if r % LANES:
        out = jnp.concatenate([out, jnp.zeros((n, _round_up(r, LANES) - r, out.shape[2]), BF16)], axis=1)
    return out


def _pack_rows(parts, width):
    rows = [p.reshape(-1, width) for p in parts]
    total = sum(r.shape[0] for r in rows)
    fill = _round_up(total, 8) - total
    if fill:
        rows.append(jnp.zeros((fill, width), F32))
    return jnp.concatenate(rows, axis=0)


def _unpack_rows(packed, shapes, width):
    out, at = [], 0
    for shp in shapes:
        n = math.prod(shp) // width
        out.append(packed[at:at + n].reshape(shp))
        at += n
    return out


class _Weights:
    def __init__(self, shards):
        self.shards = shards
        self.full = {}

    def rider(self, keys):
        r = _gather_rider([(self.shards[n], i) for n, i in keys])
        r.keys = keys
        return r

    def take(self, rider):
        for key, arr in zip(rider.keys, rider.results):
            self.full[key] = arr

    def __call__(self, name, layer):
        return self.full[(name, layer)]


def _local_step(x, meta_full, tgt, w, wts, pad, tm, cg, reducer):
    D = x.shape[1]
    T = pad + N_META + x.shape[0]
    L = w["ffn1_norm"].shape[0]
    pool_maps = w["pool_maps"]
    gains = {n: w[n].reshape(L, 1, D) for n in ("ffn1_norm", "mix_norm", "ffn2_norm")}
    scale3 = w["pool_scale"].reshape(L, 1, POOL_WIDTH)
    consts = _ret_consts(T, pad)
    tl = _pick_tile(T, 2 * tm, BF16_ROWS)
    def gather(keys):
        return wts.rider(keys) if keys and keys[0] not in wts.full else None

    def done(rider):
        if rider is not None:
            wts.take(rider)

    h = jnp.concatenate([jnp.zeros((pad, D), F32), meta_full, x], axis=0)
    saved = []
    for i in range(L):
        s = {"h0": h}
        if ("ffn1_down", i) in wts.full:
            rd = gather([("w_in", i)] + [(n, i) for n in _MIXW] + [("ffn2_gate", i)])
            h, s["a1"], s["g1"], s["u1"], s["act1"] = _ffn_fwd(
                h, gains["ffn1_norm"], wts("ffn1_gate", i), wts("ffn1_up", i), wts("ffn1_down", i), i, tl,
                f"ffn1_fwd_{i}", rd)
            done(rd)
        else:
            rd = gather([("ffn1_down", i), ("w_in", i)])
            s["a1"], s["g1"], s["u1"], s["act1"] = _ffn_fwd_up(
                h, gains["ffn1_norm"], wts("ffn1_gate", i), wts("ffn1_up", i), i, tl, f"ffn1_fwd_up_{i}", rd)
            done(rd)
            rd = gather([(n, i) for n in _MIXW])
            h = _ffn_fwd_down(h, s["act1"], wts("ffn1_down", i), tl, f"ffn1_fwd_down_{i}", rd)
            done(rd)
        s["h1"] = h
        rd = gather([k for k in (("ffn2_gate", i), ("ffn2_up", i)) if k not in wts.full])
        s["z"], s["b"] = _inproj_fwd(h, gains["mix_norm"], wts("w_in", i), i, tl, f"inproj_fwd_{i}", rd)
        done(rd)
        s["r"], s["o_pre"], s["s_all"] = _ret_fwd(s["z"], consts, cg, f"retention_fwd_{i}")
        s["pm"] = _pool_fwd(s["z"], pool_maps, scale3, i, pad, f"pool_fwd_{i}")
        rd = gather([("ffn2_down", i)])
        h, s["mixed"], s["ret"], s["pool"] = _mix_fwd(
            h, s["r"], s["pm"], s["z"], wts("w_ret_up", i), wts("w_pool_up", i), wts("w_out", i), tl,
            f"mix_fwd_{i}", rd)
        done(rd)
        s["h2"] = h
        rd = gather([(n, i + 1) for n in _FFN1]) if i + 1 < L else None
        h, s["a2"], s["g2"], s["u2"], s["act2"] = _ffn_fwd(
            h, gains["ffn2_norm"], wts("ffn2_gate", i), wts("ffn2_up", i), wts("ffn2_down", i), i, tl,
            f"ffn2_fwd_{i}", rd)
        done(rd)
        saved.append(s)

    dh, loss_acc, d_final = _final_loss(h, w["final_norm"].reshape(1, D), tgt, "final_norm_loss")

    small = {n: [None] * L for n in ("ffn1_norm", "mix_norm", "ffn2_norm", "pool_scale", "pool_maps")}

    carry = {"ffn_act": 1.0, "ffn_in": 2.2, "mix_bwd": 1.0, "inproj_bwd": 1.5, "w_in": 1.0}

    tk = _pick_tile(T, 1408, LANES)

    def grad(n, a, b, i, mode):
        rd = reducer.rider(carry.get(n, 1.0 if i == 0 and n.startswith("ffn") else 0.5))
        reducer.add(n, i, _grad_tn(a, b, mode, 1.0, tk, f"grad_{n}_{i}", rd))
        reducer.done(rd)

    def ffn_bwd(which, dy, h_in, g, u, i, between=None, units=carry["ffn_in"]):
        rd = reducer.rider(carry["ffn_act"])
        dg, du, dyh = _ffn_bwd_act(dy, g, u, wts(f"{which}_down", i), tl, f"{which}_bwd_act_{i}", rd)
        reducer.done(rd)
        if between is not None:
            between(dg, du, dyh)
        rd = reducer.rider(units)
        dh_in, dgain = _ffn_bwd_in(dy, h_in, gains[f"{which}_norm"], dg, du, wts(f"{which}_gate", i),
                                   wts(f"{which}_up", i), i, tl, pad, f"{which}_bwd_in_{i}", rd)
        reducer.done(rd)
        return dh_in, dg, du, dgain, dyh

    for i in reversed(range(L)):
        s = saved[i]
        dh, dg, du, small["ffn2_norm"][i], dyh = ffn_bwd("ffn2", dh, s["h2"], s["g2"], s["u2"], i)
        grad("ffn2_gate", dg, s["a2"], i, "row")
        grad("ffn2_up", du, s["a2"], i, "row")
        grad("ffn2_down", s["act2"], dyh, i, "row")
        reducer.stage(f"ffn2_{i}")
        rd = reducer.rider(carry["mix_bwd"])
        dgab, dret, dpool, dr, dpm = _mix_bwd_dx(
            dh, s["z"], s["ret"], s["pool"], wts("w_out", i), wts("w_ret_up", i), wts("w_pool_up", i), tm,
            f"mix_bwd_{i}", rd)
        reducer.done(rd)
        rd = reducer.rider(0.5)
        g_out, g_ru, g_pu = _grad_mix(s["mixed"], dh, s["r"], dret, s["pm"], dpool, _pick_tile(T, 704, LANES),
                                      f"grad_mix_{i}", rd)
        reducer.done(rd)
        for n, g_n in (("w_out", g_out), ("w_ret_up", g_ru), ("w_pool_up", g_pu)):
            reducer.add(n, i, g_n)
        du_pool, small["pool_maps"][i], small["pool_scale"][i] = _pool_bwd(
            s["z"], dpm, pool_maps, scale3, i, pad, f"pool_bwd_{i}")
        dq, dgr, dkp, dvp, ds = _ret_bwd_local(s["z"], s["o_pre"], s["s_all"], dr, consts,
                                               _pick_tile(T // CHUNK, 11, 1), f"retention_bwd_{i}")
        dk, dv = _ret_bwd_state(s["z"], dkp, dvp, ds, consts, cg, f"retention_bwd_state_{i}")
        dz = [dq, dk, dv, dgr, du_pool, dgab]
        dh2 = dh
        rd = reducer.rider(carry["inproj_bwd"])
        dh, small["mix_norm"][i] = _inproj_bwd_dx(
            dz, wts("w_in", i), s["h1"], gains["mix_norm"], dh2, i, tm, pad, f"inproj_bwd_{i}", rd)
        reducer.done(rd)
        rd = reducer.rider(carry["w_in"])
        reducer.add("w_in", i, _grad_w_in(s["b"], dz, wts("w_in", i).shape[-1], tk, f"grad_w_in_{i}", rd))
        reducer.done(rd)
        reducer.stage(f"mid{i}")
        def ffn1_grads(dg, du, dyh, i=i, s=s):
            grad("ffn1_gate", dg, s["a1"], i, "row")
            if i == 0:
                reducer.stage("gate0")
            grad("ffn1_up", du, s["a1"], i, "row")
            if i == 0:
                reducer.stage("up0")
            grad("ffn1_down", s["act1"], dyh, i, "row")
            reducer.stage(f"end{i}")

        if i == 0:
            dh, _, _, small["ffn1_norm"][i], _ = ffn_bwd("ffn1", dh, s["h0"], s["g1"], s["u1"], i, ffn1_grads, 2.5)
        else:
            dh, dg, du, small["ffn1_norm"][i], dyh = ffn_bwd("ffn1", dh, s["h0"], s["g1"], s["u1"], i)
            ffn1_grads(dg, du, dyh)

    return loss_acc, dh, small, d_final


class _Reducer:
    def __init__(self, unit):
        self.c_idx = lax.axis_index("c").astype(jnp.int32).reshape(1)
        chip = 2 * lax.axis_index("x") + lax.axis_index("y")
        self.pos = jnp.stack([chip, lax.axis_index("c")]).astype(jnp.int32)
        self.pending, self.stages, self.queue, self.halves, self.whole = [], [], [], {}, {}
        self.unit = unit
        self.calls = 0

    def add(self, name, layer, g):
        self.pending.append(((name, layer), g))

    def stage(self, tag):
        if self.pending:
            self.stages.append((tag, self.pending))
            self.pending = []

    def _pair_rider(self):
        if not self.stages:
            return None
        tag, items = self.stages.pop(0)
        rd = _pair_exchange_rider([g for _, g in items])
        rd.tag, rd.keys = tag, [k for k, _ in items]
        return rd

    def _chip_rider(self, units):
        take, keep, size = [], [], 0
        for item in self.queue:
            if units is None or size + item[1].size <= units * self.unit:
                take.append(item)
                size += item[1].size
            else:
                keep.append(item)
        self.queue = keep
        if not take:
            return None
        rd = _chip_exchange_rider([p for _, p in take])
        rd.keys = [k for k, _ in take]
        return rd

    def _gather_rider(self):
        keys = [k for k in self.halves if k not in self.whole]
        if not keys:
            return None
        rd = _pair_gather_rider([self.halves[k] for k in keys])
        rd.keys = keys
        return rd

    def rider(self, units):
        self.riding = (self._pair_rider(), self._chip_rider(units), self._gather_rider())
        return _join(self.riding)

    def done(self, rd):
        if rd is None:
            return
        _split_results(rd)
        pair, chips, gather = self.riding
        if len([r for r in self.riding if r is not None]) == 1:
            (pair or chips or gather).results = rd.results
        self.calls += 1
        if gather is not None:
            self.whole.update(zip(gather.keys, gather.results))
        if pair is not None:
            sums = _sum_pair(pair.ins, pair.results, self.c_idx, f"sum_pair_{pair.tag}")
            self.queue += list(zip(pair.keys, sums))
        if chips is not None:
            sums = _sum_chips(chips.ins, chips.results, self.pos, f"sum_chips_{self.calls}")
            self.halves.update(zip(chips.keys, sums))

    def busy(self):
        assert not self.pending
        return bool(self.stages or self.queue or len(self.whole) < len(self.halves))

    def flush(self):
        self.riding = (self._pair_rider(), self._chip_rider(None), self._gather_rider())
        rd = _join(self.riding)
        _run_rider(rd, f"grads_exchange_tail_{self.calls}")
        self.done(rd)


def _update(loss_acc, grad_x, d_meta_rows, reducer, small, d_final, w, mom, var):
    meta = w["meta"]
    D = w["final_norm"].shape[0]
    L = w["ffn1_norm"].shape[0]
    Dq = D // N_CHIPS

    out = {}

    small_parts = [jnp.concatenate(small[n], axis=0) for n in ("ffn1_norm", "mix_norm", "ffn2_norm")]
    small_parts += [d_final, jnp.concatenate(small["pool_scale"], axis=0), jnp.concatenate(small["pool_maps"], axis=0)]
    loss_row = jnp.pad(loss_acc, ((0, 0), (0, D - loss_acc.shape[1])))
    rd = reducer.rider(None) if reducer.busy() else None
    reduced = _small_all_reduce(_pack_rows(small_parts + [d_meta_rows, loss_row], D), rd)
    reducer.done(rd)
    while reducer.busy():
        reducer.flush()
    for n in _BIG:
        gs = [reducer.whole[(n, i)] for i in range(L)]
        if n in _TRANSPOSED:
            res = _adamw(gs, *(jnp.swapaxes(t[n], 1, 2) for t in (w, mom, var)), f"adamw_{n}")
            out[n] = [jnp.swapaxes(r, 1, 2) for r in res]
        else:
            out[n] = _adamw(gs, w[n], mom[n], var[n], f"adamw_{n}")

    small_shapes = [w[n].shape for n in _SMALL]
    small_rows = sum(math.prod(shp) for shp in small_shapes) // D
    chip = 2 * lax.axis_index("x") + lax.axis_index("y")
    d_meta = lax.dynamic_slice_in_dim(reduced[small_rows:small_rows + N_META], chip * Dq, Dq, axis=1)
    names = _SMALL + ("meta",)
    packed_g = _pack_rows([reduced[:small_rows], d_meta], D)
    packed = [_pack_rows([t[n] for n in names], D) for t in (w, mom, var)]
    res = _adamw([packed_g], packed[0][None], packed[1][None], packed[2][None], "adamw_small")
    shapes = small_shapes + [meta.shape]
    unpacked = [_unpack_rows(r[0], shapes, D) for r in res]
    for k, n in enumerate(names):
        out[n] = tuple(u[k] for u in unpacked)

    loss = reduced[small_rows + N_META, 0]
    return (loss, grad_x) + tuple(out[n][j] for j in range(4) for n in _ORDER)


def kernel(x, meta, ffn1_norm, ffn1_gate, ffn1_up, ffn1_down, mix_norm, w_in, pool_maps, pool_scale, w_ret_up, w_pool_up, w_out, ffn2_norm, ffn2_gate, ffn2_up, ffn2_down, final_norm, loss_target, m_meta, m_ffn1_norm, m_ffn1_gate, m_ffn1_up, m_ffn1_down, m_mix_norm, m_w_in, m_pool_maps, m_pool_scale, m_w_ret_up, m_w_pool_up, m_w_out, m_ffn2_norm, m_ffn2_gate, m_ffn2_up, m_ffn2_down, m_final_norm, v_meta, v_ffn1_norm, v_ffn1_gate, v_ffn1_up, v_ffn1_down, v_mix_norm, v_w_in, v_pool_maps, v_pool_scale, v_w_ret_up, v_w_pool_up, v_w_out, v_ffn2_norm, v_ffn2_gate, v_ffn2_up, v_ffn2_down, v_final_norm):
    args = dict(locals())
    w = {n: args[n] for n in _ORDER}
    mom = {n: args["m_" + n] for n in _ORDER}
    var = {n: args["v_" + n] for n in _ORDER}

    assert x.shape[0] == 1, "one batch element per device"
    seq, D = x.shape[1], x.shape[2]
    assert seq % CHUNK == 0 and D % RET_WIDTH == 0 and (2 * POOL_WIDTH) % D == 0
    pad = (-(seq + N_META)) % CHUNK
    T = seq + N_META + pad
    tm = _pick_tile(T, 528, BF16_ROWS)
    cg = _pick_tile(T // CHUNK, 33, 1)

    shards = {n: _transport(w[n]) for n in _BIG}
    shards["meta"] = meta[None]
    wts = _Weights(shards)
    head = wts.rider([("ffn1_gate", 0), ("ffn1_up", 0), ("meta", 0)])
    _run_rider(head, "weights_gather_head")
    wts.take(head)
    meta_full = jnp.transpose(wts("meta", 0), (1, 0, 2)).reshape(N_META, D)

    reducer = _Reducer(unit=2 * shards["ffn1_gate"][0].size)
    loss_acc, dh, small, d_final = _local_step(x[0], meta_full, loss_target[0], w, wts, pad, tm, cg, reducer)
    grad_x = dh[pad + N_META:][None]
    return _update(loss_acc, grad_x, dh[pad:pad + N_META], reducer, small, d_final, w, mom, var)
```

```python
import functools
import math

import jax
import jax.numpy as jnp
from jax import lax
from jax.experimental import pallas as pl
from jax.experimental.pallas import tpu as pltpu

F32 = jnp.float32
BF16 = jnp.bfloat16

N_META = 16
RET_HEADS = 4
HEAD_DIM = 128
RET_WIDTH = RET_HEADS * HEAD_DIM
POOL_WINDOWS = (2, 4, 8, 16)
POOL_GROUPS = len(POOL_WINDOWS)
POOL_WIDTH = POOL_GROUPS * HEAD_DIM
CHUNK = 128
ROPE_BASE = 10000.0
EPS = 1e-6
ADAM_LR = 0.001
ADAM_B1 = 0.9
ADAM_B2 = 0.999
ADAM_EPS = 1e-08
ADAM_WD = 0.01
ADAM_STEP = 10

N_CHIPS = 4
LANES = 128
BF16_ROWS = 16
V7X_VMEM_LIMIT = 52 * 1024 * 1024
MESH = pl.DeviceIdType.MESH
ANY = pl.BlockSpec(memory_space=pl.ANY)


def _round_up(n, m):
    return -(-n // m) * m


def _pick_tile(n, target, mult):
    best = None
    for d in range(mult, min(n, target) + 1, mult):
        if n % d == 0:
            best = d
    assert best is not None, (n, target, mult)
    return best


def _params(sem=None):
    return pltpu.CompilerParams(dimension_semantics=sem, vmem_limit_bytes=V7X_VMEM_LIMIT)


def _dot(a, b):
    return jnp.dot(a, b, preferred_element_type=F32)


def _dot_nt(a, b):
    return lax.dot_general(a, b, (((1,), (1,)), ((), ())), preferred_element_type=F32)


def _dot_tn(a, b):
    return lax.dot_general(a, b, (((0,), (0,)), ((), ())), preferred_element_type=F32)


def _ein(spec, a, b):
    return jnp.einsum(spec, a, b, preferred_element_type=F32)


def _sigmoid(x):
    return jax.nn.sigmoid(x)


def _rms_fwd(x, gain):
    r = lax.rsqrt(jnp.mean(x * x, axis=-1, keepdims=True) + EPS)
    return x * r * gain


def _rms_bwd(x, gain, da):
    r = lax.rsqrt(jnp.mean(x * x, axis=-1, keepdims=True) + EPS)
    xh = x * r
    dgain = jnp.sum(da * xh, axis=0, keepdims=True)
    dxh = da * gain
    dx = r * (dxh - xh * jnp.mean(dxh * xh, axis=-1, keepdims=True))
    return dx, dgain


def _row_mask(t, tm, pad, shape):
    rows = t * tm + lax.broadcasted_iota(jnp.int32, shape, 0)
    return rows >= pad


def _mesh_pos():
    x, y, c = lax.axis_index("x"), lax.axis_index("y"), lax.axis_index("c")
    others = [(1 - x, y), (x, 1 - y), (1 - x, 1 - y)]
    return x, y, c, 2 * x + y, others


def _half_rows(c, rh):
    return pl.ds(pl.multiple_of(c * rh, rh), rh)


def _remote(src, dst, ssem, rsem, dev):
    return pltpu.make_async_remote_copy(src_ref=src, dst_ref=dst, send_sem=ssem, recv_sem=rsem,
                                        device_id=dev, device_id_type=MESH)


class _Rider:
    def __init__(self, ins, out_shapes, n_sem, start, finish, in_place=False):
        self.ins, self.out_shapes, self.n_sem, self.start, self.finish = ins, out_shapes, n_sem, start, finish
        self.in_place = [in_place] * len(ins)
        self.results = None

    def aliases(self, first_in, first_out):
        return {first_in + i: first_out + i for i, same in enumerate(self.in_place) if same}


class _SemWindow:
    def __init__(self, ref, base):
        self.ref, self.base = ref, base

    @property
    def at(self):
        return self

    def __getitem__(self, k):
        return self.ref.at[self.base + k]


def _join(riders):
    riders = [r for r in riders if r is not None]
    if len(riders) <= 1:
        return riders[0] if riders else None

    def run(which):
        def go(ins, outs, ssem, rsem):
            at, sem = 0, 0
            for r in riders:
                n = len(r.ins)
                getattr(r, which)(ins[at:at + n], outs[at:at + n], _SemWindow(ssem, sem), _SemWindow(rsem, sem))
                at, sem = at + n, sem + r.n_sem
        return go

    joined = _Rider(sum([list(r.ins) for r in riders], []), sum([list(r.out_shapes) for r in riders], []),
                    sum(r.n_sem for r in riders), run("start"), run("finish"))
    joined.in_place = sum([r.in_place for r in riders], [])
    joined.parts = riders
    return joined


def _split_results(rider):
    at = 0
    for r in getattr(rider, "parts", []):
        r.results = rider.results[at:at + len(r.ins)]
        at += len(r.ins)


def _gather_rider(pieces):
    per = 7
    layers = [layer for _, layer in pieces]

    def first_copies(ins, outs, ssem, rsem):
        x, y, c, chip, others = _mesh_pos()
        copies = []
        for i, layer in enumerate(layers):
            mine = _half_rows(c, ins[i].shape[1] // 2)
            for j, (ox, oy) in enumerate(others):
                copies.append(_remote(ins[i].at[layer, mine, :], outs[i].at[chip, mine, :],
                                      ssem.at[per * i + j], rsem.at[per * i + j], (ox, oy, c)))
            copies.append(_remote(ins[i].at[layer], outs[i].at[chip],
                                  ssem.at[per * i + 6], rsem.at[per * i + 6], (x, y, 1 - c)))
        return copies

    def start(ins, outs, ssem, rsem):
        for cp in first_copies(ins, outs, ssem, rsem):
            cp.start()

    def finish(ins, outs, ssem, rsem):
        x, y, c, chip, others = _mesh_pos()
        sibling = (x, y, 1 - c)
        forwards = []
        for i in range(len(layers)):
            mine = _half_rows(c, ins[i].shape[1] // 2)
            for j, (ox, oy) in enumerate(others):
                rows = outs[i].at[2 * ox + oy, mine, :]
                _remote(rows, rows, ssem.at[per * i + j], rsem.at[per * i + j], (ox, oy, c)).wait_recv()
                fwd = _remote(rows, rows, ssem.at[per * i + 3 + j], rsem.at[per * i + 3 + j], sibling)
                fwd.start()
                forwards.append(fwd)
        for i in range(len(layers)):
            theirs = _half_rows(1 - c, ins[i].shape[1] // 2)
            for j, (ox, oy) in enumerate(others):
                rows = outs[i].at[2 * ox + oy, theirs, :]
                _remote(rows, rows, ssem.at[per * i + 3 + j], rsem.at[per * i + 3 + j], sibling).wait_recv()
            own = outs[i].at[chip]
            _remote(own, own, ssem.at[per * i + 6], rsem.at[per * i + 6], sibling).wait_recv()
        for cp in first_copies(ins, outs, ssem, rsem) + forwards:
            cp.wait_send()

    shapes = [jax.ShapeDtypeStruct((N_CHIPS,) + s.shape[1:], s.dtype) for s, _ in pieces]
    return _Rider([s for s, _ in pieces], shapes, per * len(pieces), start, finish)


def _chip_exchange_rider(ps):
    def copies(ins, outs, ssem, rsem):
        x, y, c, chip, others = _mesh_pos()
        return [_remote(ins[i].at[2 * ox + oy], outs[i].at[chip], ssem.at[3 * i + j], rsem.at[3 * i + j], (ox, oy, c))
                for i in range(len(ps)) for j, (ox, oy) in enumerate(others)]

    def start(ins, outs, ssem, rsem):
        for cp in copies(ins, outs, ssem, rsem):
            cp.start()

    def finish(ins, outs, ssem, rsem):
        x, y, c, chip, others = _mesh_pos()
        for i in range(len(ps)):
            for j, (ox, oy) in enumerate(others):
                slot = outs[i].at[2 * ox + oy]
                _remote(slot, slot, ssem.at[3 * i + j], rsem.at[3 * i + j], (ox, oy, c)).wait_recv()
        for cp in copies(ins, outs, ssem, rsem):
            cp.wait_send()

    return _Rider(list(ps), [jax.ShapeDtypeStruct(p.shape, p.dtype) for p in ps], 3 * len(ps), start, finish)


def _pair_exchange_rider(gs):
    def copies(ins, outs, ssem, rsem):
        x, y, c, _, _ = _mesh_pos()
        return [_remote(ins[i].at[:, _half_rows(1 - c, ins[i].shape[1] // 2), :], outs[i],
                        ssem.at[i], rsem.at[i], (x, y, 1 - c)) for i in range(len(gs))]

    def start(ins, outs, ssem, rsem):
        for cp in copies(ins, outs, ssem, rsem):
            cp.start()

    def finish(ins, outs, ssem, rsem):
        for cp in copies(ins, outs, ssem, rsem):
            cp.wait()

    shapes = [jax.ShapeDtypeStruct((g.shape[0], g.shape[1] // 2, g.shape[2]), g.dtype) for g in gs]
    return _Rider(list(gs), shapes, len(gs), start, finish)


def _run_rider(rider, name):
    def body(*refs):
        n = len(rider.ins)
        ins, outs = refs[:n], refs[n:2 * n]
        ssem, rsem = refs[2 * n:]
        rider.start(ins, outs, ssem, rsem)
        rider.finish(ins, outs, ssem, rsem)

    rider.results = pl.pallas_call(
        body,
        name=name,
        in_specs=[ANY] * len(rider.ins),
        out_specs=[ANY] * len(rider.ins),
        out_shape=rider.out_shapes,
        input_output_aliases=rider.aliases(0, 0),
        scratch_shapes=[pltpu.SemaphoreType.DMA((rider.n_sem,)), pltpu.SemaphoreType.DMA((rider.n_sem,))],
    )(*rider.ins)
    return rider.results


def _pair_gather_rider(fs):
    n = len(fs)

    def copies(outs, ssem, rsem):
        x, y, c, _, _ = _mesh_pos()
        halves = [outs[i].at[_half_rows(c, outs[i].shape[0] // 2), :] for i in range(n)]
        return [_remote(h, h, ssem.at[i], rsem.at[i], (x, y, 1 - c)) for i, h in enumerate(halves)]

    def start(ins, outs, ssem, rsem):
        for cp in copies(outs, ssem, rsem):
            cp.start()

    def finish(ins, outs, ssem, rsem):
        x, y, c, _, _ = _mesh_pos()
        for i in range(n):
            theirs = outs[i].at[_half_rows(1 - c, outs[i].shape[0] // 2), :]
            _remote(theirs, theirs, ssem.at[i], rsem.at[i], (x, y, 1 - c)).wait_recv()
        for cp in copies(outs, ssem, rsem):
            cp.wait_send()

    return _Rider(list(fs), [jax.ShapeDtypeStruct(f.shape, f.dtype) for f in fs], n, start, finish, in_place=True)


def _call(body, *, name, grid, in_specs, out_specs, out_shape, operands, scratch=(), sem=None, rider=None):
    if rider is None:
        return pl.pallas_call(
            body, name=name, grid=grid, in_specs=in_specs, out_specs=out_specs, out_shape=out_shape,
            scratch_shapes=list(scratch), compiler_params=_params(sem))(*operands)
    n_in, n_out, n_sc, r = len(in_specs), len(out_specs), len(scratch), len(rider.ins)

    def carrying(*refs):
        a, b = n_in, n_in + r
        c, d = b + n_out, b + n_out + r
        e = d + n_sc
        ids = [pl.program_id(k) for k in range(len(grid))]
        first = functools.reduce(jnp.logical_and, [i == 0 for i in ids])
        last = functools.reduce(jnp.logical_and, [i == g - 1 for i, g in zip(ids, grid)])

        @pl.when(first)
        def _():
            rider.start(refs[a:b], refs[c:d], refs[e], refs[e + 1])

        body(*refs[:a], *refs[b:c], *refs[d:e])

        @pl.when(last)
        def _():
            rider.finish(refs[a:b], refs[c:d], refs[e], refs[e + 1])

    outs = pl.pallas_call(
        carrying, name=name, grid=grid,
        in_specs=list(in_specs) + [ANY] * r,
        out_specs=list(out_specs) + [ANY] * r,
        out_shape=list(out_shape) + list(rider.out_shapes),
        scratch_shapes=list(scratch) + [pltpu.SemaphoreType.DMA((rider.n_sem,)), pltpu.SemaphoreType.DMA((rider.n_sem,))],
        input_output_aliases=rider.aliases(n_in, n_out),
        compiler_params=_params(("arbitrary",) * len(grid)),
    )(*operands, *rider.ins)
    rider.results = outs[n_out:]
    return outs[:n_out]


def _ffn_fwd(h, gain, wg, wu, wd, layer, tm, name, rider=None):
    T, D = h.shape
    Fs = wg.shape[-1]
    F = N_CHIPS * Fs

    def body(h_ref, g_ref, wg_ref, wu_ref, wd_ref, ho_ref, a_ref, go_ref, uo_ref, act_ref, acc_ref):
        s = pl.program_id(1)

        @pl.when(s == 0)
        def _():
            a_ref[...] = _rms_fwd(h_ref[...], g_ref[...]).astype(BF16)
            acc_ref[...] = jnp.zeros_like(acc_ref)

        a = a_ref[...]
        g = _dot(a, wg_ref[...])
        u = _dot(a, wu_ref[...])
        sg = _sigmoid(g)
        act = (g * sg * u).astype(BF16)
        go_ref[...] = (u * (sg * (1.0 + g * (1.0 - sg)))).astype(BF16)
        uo_ref[...] = (g * sg).astype(BF16)
        act_ref[...] = act
        acc_ref[...] += _dot(act, wd_ref[...])

        @pl.when(s == N_CHIPS - 1)
        def _():
            ho_ref[...] = h_ref[...] + 0.5 * acc_ref[...]

    row = pl.BlockSpec((tm, D), lambda t, s: (t, 0))
    col = pl.BlockSpec((tm, Fs), lambda t, s: (t, s))
    wcol = pl.BlockSpec((None, D, Fs), lambda t, s: (s, 0, 0))
    return _call(
        body, name=name, grid=(T // tm, N_CHIPS),
        in_specs=[row, pl.BlockSpec((None, 1, D), lambda t, s: (layer, 0, 0)), wcol, wcol,
                  pl.BlockSpec((None, Fs, D), lambda t, s: (s, 0, 0))],
        out_specs=[row, row, col, col, col],
        out_shape=[jax.ShapeDtypeStruct((T, D), F32), jax.ShapeDtypeStruct((T, D), BF16)]
        + [jax.ShapeDtypeStruct((T, F), BF16)] * 3,
        scratch=[pltpu.VMEM((tm, D), F32)],
        sem=("parallel", "arbitrary"), operands=(h, gain, wg, wu, wd), rider=rider)


def _ffn_fwd_up(h, gain, wg, wu, layer, tm, name, rider=None):
    T, D = h.shape
    Fs = wg.shape[-1]
    F = N_CHIPS * Fs

    def body(h_ref, g_ref, wg_ref, wu_ref, a_ref, go_ref, uo_ref, act_ref):
        @pl.when(pl.program_id(1) == 0)
        def _():
            a_ref[...] = _rms_fwd(h_ref[...], g_ref[...]).astype(BF16)

        a = a_ref[...]
        g = _dot(a, wg_ref[...])
        u = _dot(a, wu_ref[...])
        sg = _sigmoid(g)
        act_ref[...] = (g * sg * u).astype(BF16)
        go_ref[...] = (u * (sg * (1.0 + g * (1.0 - sg)))).astype(BF16)
        uo_ref[...] = (g * sg).astype(BF16)

    row = pl.BlockSpec((tm, D), lambda t, s: (t, 0))
    col = pl.BlockSpec((tm, Fs), lambda t, s: (t, s))
    wcol = pl.BlockSpec((None, D, Fs), lambda t, s: (s, 0, 0))
    return _call(
        body, name=name, grid=(T // tm, N_CHIPS),
        in_specs=[row, pl.BlockSpec((None, 1, D), lambda t, s: (layer, 0, 0)), wcol, wcol],
        out_specs=[row, col, col, col],
        out_shape=[jax.ShapeDtypeStruct((T, D), BF16)] + [jax.ShapeDtypeStruct((T, F), BF16)] * 3,
        sem=("parallel", "arbitrary"), operands=(h, gain, wg, wu), rider=rider)


def _ffn_fwd_down(h, act, wd, tm, name, rider=None):
    T, D = h.shape
    Fs = wd.shape[1]

    def body(h_ref, act_ref, wd_ref, ho_ref, acc_ref):
        s = pl.program_id(1)

        @pl.when(s == 0)
        def _():
            acc_ref[...] = jnp.zeros_like(acc_ref)

        acc_ref[...] += _dot(act_ref[...], wd_ref[...])

        @pl.when(s == N_CHIPS - 1)
        def _():
            ho_ref[...] = h_ref[...] + 0.5 * acc_ref[...]

    row = pl.BlockSpec((tm, D), lambda t, s: (t, 0))
    return _call(
        body, name=name, grid=(T // tm, N_CHIPS),
        in_specs=[row, pl.BlockSpec((tm, Fs), lambda t, s: (t, s)), pl.BlockSpec((None, Fs, D), lambda t, s: (s, 0, 0))],
        out_specs=[row],
        out_shape=[jax.ShapeDtypeStruct((T, D), F32)],
        scratch=[pltpu.VMEM((tm, D), F32)],
        sem=("parallel", "arbitrary"), operands=(h, act, wd), rider=rider)[0]


def _inproj_fwd(h, gain, win, layer, tm, name, rider=None):
    T, D = h.shape
    Ns = win.shape[-1]

    def body(h_ref, g_ref, w_ref, z_ref, b_ref):
        @pl.when(pl.program_id(1) == 0)
        def _():
            b_ref[...] = _rms_fwd(h_ref[...], g_ref[...]).astype(BF16)

        z_ref[...] = _dot(b_ref[...], w_ref[...]).astype(BF16)

    return _call(
        body, name=name, grid=(T // tm, N_CHIPS),
        in_specs=[pl.BlockSpec((tm, D), lambda t, s: (t, 0)),
                  pl.BlockSpec((None, 1, D), lambda t, s: (layer, 0, 0)),
                  pl.BlockSpec((None, D, Ns), lambda t, s: (s, 0, 0))],
        out_specs=[pl.BlockSpec((tm, Ns), lambda t, s: (t, s)), pl.BlockSpec((tm, D), lambda t, s: (t, 0))],
        out_shape=[jax.ShapeDtypeStruct((T, N_CHIPS * Ns), BF16), jax.ShapeDtypeStruct((T, D), BF16)],
        sem=("parallel", "arbitrary"), operands=(h, gain, win), rider=rider)


def _ret_consts(T, pad):
    half = HEAD_DIM // 2
    inv_freq = ROPE_BASE ** (-jnp.arange(half, dtype=F32) / half)
    pos = jnp.arange(T, dtype=F32) - pad
    ang = pos[:, None] * inv_freq[None, :]
    cos = jnp.cos(ang)
    sin = jnp.sin(ang)
    cosf = jnp.concatenate([cos, cos], axis=1)
    sinf = jnp.concatenate([-sin, sin], axis=1)
    log_gamma = jnp.log1p(-(2.0 ** (-5.0 - jnp.arange(RET_HEADS, dtype=F32))))
    idx = jnp.arange(CHUNK, dtype=F32)
    diff = idx[:, None] - idx[None, :]
    intra = jnp.where(diff[None] >= 0, jnp.exp(diff[None] * log_gamma[:, None, None]), 0.0)
    k_decay = jnp.exp((CHUNK - 1.0 - idx)[None, :] * log_gamma[:, None])
    q_decay = jnp.exp((idx + 1.0)[None, :] * log_gamma[:, None])
    chunk_decay = jnp.exp(CHUNK * log_gamma)
    kdec = jnp.broadcast_to(k_decay[:, :, None], (RET_HEADS, CHUNK, HEAD_DIM))
    qdec = jnp.broadcast_to(q_decay[:, :, None], (RET_HEADS, CHUNK, HEAD_DIM))
    cdb = jnp.broadcast_to(chunk_decay[:, None, None], (RET_HEADS, 8, HEAD_DIM))
    return cosf, sinf, intra, kdec, qdec, cdb


def _rot(t, cosv, sinv):
    return t * cosv + pltpu.roll(t, HEAD_DIM // 2, 1) * sinv


def _rot_t(g, cosv, sinv):
    return g * cosv + pltpu.roll(g * sinv, HEAD_DIM // 2, 1)


def _head_specs(tg, section, order):
    return pl.BlockSpec((tg, HEAD_DIM), lambda h, g: (order(g), section * RET_HEADS + h))


def _ret_fwd(z, consts, cg, name, rider=None):
    T = z.shape[0]
    N = T // CHUNK
    ng = N // cg
    tg = cg * CHUNK
    cosf, sinf, intra, kdec, qdec, cdb = consts
    fwd = lambda g: g

    def body(zq, zk, zv, zg, cos_ref, sin_ref, m_ref, kd_ref, qd_ref, cd_ref, r_ref, o_ref, s_ref, st_ref):
        @pl.when(pl.program_id(1) == 0)
        def _():
            st_ref[...] = jnp.zeros_like(st_ref)

        cosv = cos_ref[...]
        sinv = sin_ref[...]
        q3 = (_rot(zq[...].astype(F32), cosv, sinv) * (HEAD_DIM ** -0.5)).reshape(cg, CHUNK, HEAD_DIM)
        k3 = _rot(zk[...].astype(F32), cosv, sinv).reshape(cg, CHUNK, HEAD_DIM)
        vb = zv[...].reshape(cg, CHUNK, HEAD_DIM).astype(BF16)
        scores = _ein("ncd,nmd->ncm", q3.astype(BF16), k3.astype(BF16)) * m_ref[...][None]
        inner = _ein("ncm,nmd->ncd", scores.astype(BF16), vb)
        kv = _ein("ncd,nce->nde", (k3 * kd_ref[...][None]).astype(BF16), vb)
        cd = cd_ref[0:1, :]
        state = st_ref[...]
        for n in range(cg):
            s_ref[n] = state
            state = state * cd + kv[n]
        st_ref[...] = state
        qdb = (q3 * qd_ref[...][None]).astype(BF16)
        cross = _ein("ncd,nde->nce", qdb, s_ref[...].astype(BF16))
        out = (inner + cross).reshape(tg, HEAD_DIM)
        o_ref[...] = out
        xc = out - jnp.mean(out, axis=-1, keepdims=True)
        rn = xc * lax.rsqrt(jnp.mean(xc * xc, axis=-1, keepdims=True) + EPS)
        g = zg[...].astype(F32)
        r_ref[...] = (rn * (g * _sigmoid(g))).astype(BF16)

    tab = pl.BlockSpec((tg, HEAD_DIM), lambda h, g: (g, 0))
    per_head = lambda rows: pl.BlockSpec((None, rows, HEAD_DIM), lambda h, g: (h, 0, 0))
    head_out = pl.BlockSpec((tg, HEAD_DIM), lambda h, g: (g, h))
    return _call(
        body, name=name, grid=(RET_HEADS, ng),
        in_specs=[_head_specs(tg, i, fwd) for i in range(4)]
        + [tab, tab, per_head(CHUNK), per_head(CHUNK), per_head(CHUNK), per_head(8)],
        out_specs=[head_out, head_out, pl.BlockSpec((None, cg, HEAD_DIM, HEAD_DIM), lambda h, g: (h, g, 0, 0))],
        out_shape=[jax.ShapeDtypeStruct((T, RET_WIDTH), BF16), jax.ShapeDtypeStruct((T, RET_WIDTH), F32),
                   jax.ShapeDtypeStruct((RET_HEADS, N, HEAD_DIM, HEAD_DIM), F32)],
        scratch=[pltpu.VMEM((HEAD_DIM, HEAD_DIM), F32)],
        sem=("parallel", "arbitrary"), operands=(z, z, z, z, cosf, sinf, intra, kdec, qdec, cdb), rider=rider)


def _window_sums(u, shift_of):
    sums = []
    s = u
    k = 1
    while k < POOL_WINDOWS[-1]:
        s = s + pltpu.roll(s, shift_of(k), 0)
        sums.append(s)
        k *= 2
    return sums


def _select_group(vals, g):
    out = vals[-1]
    for i in range(len(vals) - 2, -1, -1):
        out = jnp.where(g == i, vals[i], out)
    return out


def _pool_parts(u, g, T, pad):
    rows = lax.broadcasted_iota(jnp.int32, (T, HEAD_DIM), 0)
    valid = rows >= pad
    win = _select_group([float(w) for w in POOL_WINDOWS], g)
    div = jnp.clip((rows - pad + 1).astype(F32), 1.0, win)
    s = _select_group(_window_sums(u, lambda k: k), g)
    pooled = jnp.where(valid, s / div - u, 0.0)
    return pooled, div, valid


def _pool_specs(T, layer):
    first = 4 * RET_WIDTH // HEAD_DIM
    return [
        pl.BlockSpec((T, HEAD_DIM), lambda g: (0, first + g)),
        pl.BlockSpec((None, None, HEAD_DIM, HEAD_DIM), lambda g: (layer, g, 0, 0)),
        pl.BlockSpec((None, 1, HEAD_DIM), lambda g: (layer, 0, g)),
    ]


def _pool_fwd(z, maps, scale, layer, pad, name):
    T = z.shape[0]
    assert pad >= POOL_WINDOWS[-1], "window rolls wrap into the zero rows in front"

    def body(zu, maps_ref, sc_ref, pm_ref):
        g = pl.program_id(0)
        pooled, _, _ = _pool_parts(zu[...].astype(F32), g, T, pad)
        y = _dot(pooled.astype(BF16), maps_ref[...].astype(BF16))
        pm_ref[...] = (y * sc_ref[...]).astype(BF16)

    return _call(
        body, name=name, grid=(POOL_GROUPS,),
        in_specs=_pool_specs(T, layer),
        out_specs=[pl.BlockSpec((T, HEAD_DIM), lambda g: (0, g))],
        out_shape=[jax.ShapeDtypeStruct((T, POOL_WIDTH), BF16)],
        sem=("parallel",), operands=(z, maps, scale))[0]


def _gate_specs(tm, D):
    nb = D // RET_WIDTH
    first = (4 * RET_WIDTH + POOL_WIDTH) // RET_WIDTH
    return [pl.BlockSpec((tm, RET_WIDTH), functools.partial(lambda t, j: (t, j), j=first + j)) for j in range(2 * nb)]


def _load_gates(refs, nb):
    ga = jnp.concatenate([r[...].astype(F32) for r in refs[:nb]], axis=1)
    gb = jnp.concatenate([r[...].astype(F32) for r in refs[nb:]], axis=1)
    return ga, gb


def _mix_fwd(h, r, pm, z, wru, wpu, wout, tm, name, rider=None):
    T, D = h.shape
    Dq = D // N_CHIPS
    nb = D // RET_WIDTH

    def body(*refs):
        h_ref, r_ref, pm_ref = refs[:3]
        gate_refs = refs[3:3 + 2 * nb]
        wru_ref, wpu_ref, wout_ref, ho_ref, mx_ref, ret_ref, pool_ref = refs[3 + 2 * nb:]
        rv = r_ref[...]
        pv = pm_ref[...]
        ret = jnp.concatenate([_dot(rv, wru_ref[s]) for s in range(N_CHIPS)], axis=1)
        pool = jnp.concatenate([_dot(pv, wpu_ref[s]) for s in range(N_CHIPS)], axis=1)
        ga, gb = _load_gates(gate_refs, nb)
        mixed = (_sigmoid(ga) * ret + _sigmoid(gb) * pool).astype(BF16)
        mx_ref[...] = mixed
        ret_ref[...] = ret.astype(BF16)
        pool_ref[...] = pool.astype(BF16)
        ho_ref[...] = h_ref[...] + _dot(mixed, wout_ref[...].reshape(D, D))

    row = pl.BlockSpec((tm, D), lambda t: (t, 0))
    half = pl.BlockSpec((tm, RET_WIDTH), lambda t: (t, 0))
    up = pl.BlockSpec((N_CHIPS, RET_WIDTH, Dq), lambda t: (0, 0, 0))
    return _call(
        body, name=name, grid=(T // tm,),
        in_specs=[row, half, half] + _gate_specs(tm, D) + [up, up, pl.BlockSpec((N_CHIPS, Dq, D), lambda t: (0, 0, 0))],
        out_specs=[row, row, row, row],
        out_shape=[jax.ShapeDtypeStruct((T, D), F32)] + [jax.ShapeDtypeStruct((T, D), BF16)] * 3,
        sem=("parallel",), operands=(h, r, pm, *([z] * (2 * nb)), wru, wpu, wout), rider=rider)


def _final_loss(h, gain, tgt, name):
    T, D = h.shape
    first = (T - tgt.shape[0]) // CHUNK

    def body(h_ref, g_ref, t_ref, dh_ref, loss_ref, dg_ref):
        i = pl.program_id(0)

        @pl.when(i == 0)
        def _():
            loss_ref[...] = jnp.zeros_like(loss_ref)
            dg_ref[...] = jnp.zeros_like(dg_ref)

        x = h_ref[...]
        gain_v = g_ref[...]
        err = jnp.where(i >= first, _rms_fwd(x, gain_v) - t_ref[...], 0.0)
        loss_ref[...] += 0.5 * jnp.sum(jnp.mean(err * err, axis=-1))
        dx, dgain = _rms_bwd(x, gain_v, err * (1.0 / D))
        dg_ref[...] += dgain
        dh_ref[...] = dx

    return _call(
        body, name=name, grid=(T // CHUNK,),
        in_specs=[pl.BlockSpec((CHUNK, D), lambda i: (i, 0)),
                  pl.BlockSpec((1, D), lambda i: (0, 0)),
                  pl.BlockSpec((CHUNK, D), lambda i: (jnp.maximum(i - first, 0), 0))],
        out_specs=[pl.BlockSpec((CHUNK, D), lambda i: (i, 0)),
                   pl.BlockSpec((1, LANES), lambda i: (0, 0)),
                   pl.BlockSpec((1, D), lambda i: (0, 0))],
        out_shape=[jax.ShapeDtypeStruct((T, D), F32), jax.ShapeDtypeStruct((1, LANES), F32),
                   jax.ShapeDtypeStruct((1, D), F32)],
        sem=("arbitrary",), operands=(h, gain, tgt))


def _ffn_bwd_act(dy, g, u, wd, tm, name, rider=None):
    T, D = dy.shape
    Fs = wd.shape[1]
    F = N_CHIPS * Fs

    def body(dy_ref, go_ref, uo_ref, wd_ref, dg_ref, du_ref, dyh_ref):
        @pl.when(pl.program_id(1) == 0)
        def _():
            dyh_ref[...] = (0.5 * dy_ref[...]).astype(BF16)

        dact = _dot_nt(dyh_ref[...], wd_ref[...])
        du_ref[...] = (dact * uo_ref[...].astype(F32)).astype(BF16)
        dg_ref[...] = (dact * go_ref[...].astype(F32)).astype(BF16)

    row = pl.BlockSpec((tm, D), lambda t, s: (t, 0))
    col = pl.BlockSpec((tm, Fs), lambda t, s: (t, s))
    return _call(
        body, name=name, grid=(T // tm, N_CHIPS),
        in_specs=[row, col, col, pl.BlockSpec((None, Fs, D), lambda t, s: (s, 0, 0))],
        out_specs=[col, col, row],
        out_shape=[jax.ShapeDtypeStruct((T, F), BF16), jax.ShapeDtypeStruct((T, F), BF16),
                   jax.ShapeDtypeStruct((T, D), BF16)],
        sem=("parallel", "arbitrary"), operands=(dy, g, u, wd), rider=rider)


def _ffn_bwd_in(dy, h, gain, dg, du, wg, wu, layer, tm, pad, name, rider=None):
    T, D = h.shape
    Fs = wg.shape[-1]

    def body(dy_ref, h_ref, g_ref, dg_ref, du_ref, wg_ref, wu_ref, dh_ref, dgain_ref, da_ref):
        t = pl.program_id(0)
        s = pl.program_id(1)

        @pl.when((t == 0) & (s == 0))
        def _():
            dgain_ref[...] = jnp.zeros_like(dgain_ref)

        @pl.when(s == 0)
        def _():
            da_ref[...] = jnp.zeros_like(da_ref)

        da_ref[...] += _dot_nt(dg_ref[...], wg_ref[...]) + _dot_nt(du_ref[...], wu_ref[...])

        @pl.when(s == N_CHIPS - 1)
        def _():
            dx, dgain = _rms_bwd(h_ref[...], g_ref[...], da_ref[...])
            dgain_ref[...] += dgain
            dh_ref[...] = jnp.where(_row_mask(t, tm, pad, (tm, D)), dy_ref[...] + dx, 0.0)

    row = pl.BlockSpec((tm, D), lambda t, s: (t, 0))
    col = pl.BlockSpec((tm, Fs), lambda t, s: (t, s))
    wcol = pl.BlockSpec((None, D, Fs), lambda t, s: (s, 0, 0))
    return _call(
        body, name=name, grid=(T // tm, N_CHIPS),
        in_specs=[row, row, pl.BlockSpec((None, 1, D), lambda t, s: (layer, 0, 0)), col, col, wcol, wcol],
        out_specs=[row, pl.BlockSpec((1, D), lambda t, s: (0, 0))],
        out_shape=[jax.ShapeDtypeStruct((T, D), F32), jax.ShapeDtypeStruct((1, D), F32)],
        scratch=[pltpu.VMEM((tm, D), F32)],
        sem=("arbitrary", "arbitrary"), operands=(dy, h, gain, dg, du, wg, wu), rider=rider)


def _grad_tn(a, b, mode, scale, tm, name, rider=None):
    T = a.shape[0]
    if mode == "col":
        per, R, C = 1, a.shape[1], b.shape[1] // N_CHIPS
        a_spec = pl.BlockSpec((tm, R), lambda s, t: (t, 0))
        b_spec = pl.BlockSpec((tm, C), lambda s, t: (t, s))
    else:
        per, R, C = 2, a.shape[1] // N_CHIPS, b.shape[1]
        a_spec = pl.BlockSpec((tm, per * R), lambda s, t: (t, s))
        b_spec = pl.BlockSpec((tm, C), lambda s, t: (t, 0))
    nt = T // tm

    def body(a_ref, b_ref, o_ref, acc_ref):
        t = pl.program_id(1)

        @pl.when(t == 0)
        def _():
            acc_ref[...] = jnp.zeros_like(acc_ref)

        acc_ref[...] += _dot_tn(a_ref[...].astype(BF16), b_ref[...].astype(BF16))

        @pl.when(t == nt - 1)
        def _():
            o_ref[...] = (scale * acc_ref[...]).astype(BF16).reshape(per, R, C)

    return _call(
        body, name=name, grid=(N_CHIPS // per, nt),
        in_specs=[a_spec, b_spec],
        out_specs=[pl.BlockSpec((per, R, C), lambda s, t: (s, 0, 0))],
        out_shape=[jax.ShapeDtypeStruct((N_CHIPS, R, C), BF16)],
        scratch=[pltpu.VMEM((per * R, C), F32)],
        sem=("parallel", "arbitrary"), operands=(a, b), rider=rider)[0]


def _grad_mix(mixed, dh, r, dret, pm, dpool, tk, name, rider=None):
    T, D = dh.shape
    Dq = D // N_CHIPS
    nt = T // tk

    def body(mx_ref, dh_ref, r_ref, dret_ref, pm_ref, dpool_ref, go_ref, gr_ref, gp_ref, ao_ref, ar_ref, ap_ref):
        t = pl.program_id(0)

        @pl.when(t == 0)
        def _():
            ao_ref[...] = jnp.zeros_like(ao_ref)
            ar_ref[...] = jnp.zeros_like(ar_ref)
            ap_ref[...] = jnp.zeros_like(ap_ref)

        ao_ref[...] += _dot_tn(mx_ref[...], dh_ref[...].astype(BF16))
        ar_ref[...] += _dot_tn(r_ref[...], dret_ref[...])
        ap_ref[...] += _dot_tn(pm_ref[...], dpool_ref[...])

        @pl.when(t == nt - 1)
        def _():
            go_ref[...] = ao_ref[...].astype(BF16).reshape(N_CHIPS, Dq, D)
            for s in range(N_CHIPS):
                gr_ref[s] = ar_ref[:, s * Dq:(s + 1) * Dq].astype(BF16)
                gp_ref[s] = ap_ref[:, s * Dq:(s + 1) * Dq].astype(BF16)

    row = pl.BlockSpec((tk, D), lambda t: (t, 0))
    half = pl.BlockSpec((tk, RET_WIDTH), lambda t: (t, 0))
    whole = lambda shape: pl.BlockSpec(shape, lambda t: (0, 0, 0))
    return _call(
        body, name=name, grid=(nt,),
        in_specs=[row, row, half, row, half, row],
        out_specs=[whole((N_CHIPS, Dq, D)), whole((N_CHIPS, RET_WIDTH, Dq)), whole((N_CHIPS, POOL_WIDTH, Dq))],
        out_shape=[jax.ShapeDtypeStruct((N_CHIPS, Dq, D), BF16),
                   jax.ShapeDtypeStruct((N_CHIPS, RET_WIDTH, Dq), BF16),
                   jax.ShapeDtypeStruct((N_CHIPS, POOL_WIDTH, Dq), BF16)],
        scratch=[pltpu.VMEM((D, D), F32), pltpu.VMEM((RET_WIDTH, D), F32), pltpu.VMEM((POOL_WIDTH, D), F32)],
        sem=("arbitrary",), operands=(mixed, dh, r, dret, pm, dpool), rider=rider)


def _mix_bwd_dx(dh, z, ret, pool, wout, wru, wpu, tm, name, rider=None):
    T, D = dh.shape
    Dq = D // N_CHIPS
    nb = D // RET_WIDTH

    def body(*refs):
        dh_ref = refs[0]
        gate_refs = refs[1:1 + 2 * nb]
        ret_ref, pool_ref, wout_ref, wru_ref, wpu_ref, dgab_ref, dret_ref, dpool_ref, dr_ref, dpm_ref = refs[1 + 2 * nb:]
        dmixed = _dot_nt(dh_ref[...].astype(BF16), wout_ref[...].reshape(D, D))
        ga, gb = _load_gates(gate_refs, nb)
        sa = _sigmoid(ga)
        sb = _sigmoid(gb)
        dgab_ref[:, :D] = (dmixed * ret_ref[...].astype(F32) * (sa * (1.0 - sa))).astype(BF16)
        dgab_ref[:, D:] = (dmixed * pool_ref[...].astype(F32) * (sb * (1.0 - sb))).astype(BF16)
        dret = (dmixed * sa).astype(BF16)
        dpool = (dmixed * sb).astype(BF16)
        dret_ref[...] = dret
        dpool_ref[...] = dpool
        dr = _dot_nt(dret[:, :Dq], wru_ref[0])
        dpm = _dot_nt(dpool[:, :Dq], wpu_ref[0])
        for s in range(1, N_CHIPS):
            dr += _dot_nt(dret[:, s * Dq:(s + 1) * Dq], wru_ref[s])
            dpm += _dot_nt(dpool[:, s * Dq:(s + 1) * Dq], wpu_ref[s])
        dr_ref[...] = dr
        dpm_ref[...] = dpm

    row = pl.BlockSpec((tm, D), lambda t: (t, 0))
    half = pl.BlockSpec((tm, RET_WIDTH), lambda t: (t, 0))
    up = pl.BlockSpec((N_CHIPS, RET_WIDTH, Dq), lambda t: (0, 0, 0))
    return _call(
        body, name=name, grid=(T // tm,),
        in_specs=[row] + _gate_specs(tm, D) + [row, row, pl.BlockSpec((N_CHIPS, Dq, D), lambda t: (0, 0, 0)), up, up],
        out_specs=[pl.BlockSpec((tm, 2 * D), lambda t: (t, 0)), row, row, half, half],
        out_shape=[jax.ShapeDtypeStruct((T, 2 * D), BF16), jax.ShapeDtypeStruct((T, D), BF16),
                   jax.ShapeDtypeStruct((T, D), BF16), jax.ShapeDtypeStruct((T, RET_WIDTH), F32),
                   jax.ShapeDtypeStruct((T, POOL_WIDTH), F32)],
        sem=("parallel",), operands=(dh, *([z] * (2 * nb)), ret, pool, wout, wru, wpu), rider=rider)


def _pool_bwd(z, dpm, maps, scale, layer, pad, name):
    T = z.shape[0]

    def body(zu, maps_ref, sc_ref, dpm_ref, du_ref, dmaps_ref, dsc_ref):
        g = pl.program_id(0)
        u = zu[...].astype(F32)
        pooled, div, valid = _pool_parts(u, g, T, pad)
        pb = pooled.astype(BF16)
        mb = maps_ref[...].astype(BF16)
        dp = dpm_ref[...]
        dsc_ref[...] = jnp.sum(dp * _dot(pb, mb), axis=0, keepdims=True)
        dyb = (dp * sc_ref[...]).astype(BF16)
        dmaps_ref[...] = _dot_tn(pb, dyb)
        dpooled = jnp.where(valid, _dot_nt(dyb, mb), 0.0)
        ahead = _select_group(_window_sums(dpooled / div, lambda k: T - k), g)
        du_ref[...] = jnp.where(valid, ahead - dpooled, 0.0).astype(BF16)

    blk = pl.BlockSpec((T, HEAD_DIM), lambda g: (0, g))
    return _call(
        body, name=name, grid=(POOL_GROUPS,),
        in_specs=_pool_specs(T, layer) + [blk],
        out_specs=[blk, pl.BlockSpec((None, HEAD_DIM, HEAD_DIM), lambda g: (g, 0, 0)),
                   pl.BlockSpec((1, HEAD_DIM), lambda g: (0, g))],
        out_shape=[jax.ShapeDtypeStruct((T, POOL_WIDTH), BF16),
                   jax.ShapeDtypeStruct((POOL_GROUPS, HEAD_DIM, HEAD_DIM), F32),
                   jax.ShapeDtypeStruct((1, POOL_WIDTH), F32)],
        sem=("parallel",), operands=(z, maps, scale, dpm))


def _ret_bwd_local(z, o_pre, s_all, dr, consts, cg, name):
    T = z.shape[0]
    N = T // CHUNK
    ng = N // cg
    tg = cg * CHUNK
    cosf, sinf, intra, _, qdec, _ = consts
    fwd = lambda g: g

    def body(zq, zk, zv, zg, o_ref, s_ref, dr_ref, cos_ref, sin_ref, m_ref, qd_ref,
             dq_ref, dg_ref, dk_ref, dv_ref, ds_ref):
        cosv = cos_ref[...]
        sinv = sin_ref[...]
        scale = HEAD_DIM ** -0.5
        q3 = (_rot(zq[...].astype(F32), cosv, sinv) * scale).reshape(cg, CHUNK, HEAD_DIM)
        k3 = _rot(zk[...].astype(F32), cosv, sinv).reshape(cg, CHUNK, HEAD_DIM)
        qb = q3.astype(BF16)
        kb = k3.astype(BF16)
        vb = zv[...].reshape(cg, CHUNK, HEAD_DIM).astype(BF16)
        mask = m_ref[...][None]
        sb = (_ein("ncd,nmd->ncm", qb, kb) * mask).astype(BF16)
        qdv = qd_ref[...][None]
        qdb = (q3 * qdv).astype(BF16)

        out = o_ref[...]
        xc = out - jnp.mean(out, axis=-1, keepdims=True)
        rstd = lax.rsqrt(jnp.mean(xc * xc, axis=-1, keepdims=True) + EPS)
        rn = xc * rstd
        g = zg[...].astype(F32)
        sg = _sigmoid(g)
        drv = dr_ref[...]
        dg_ref[...] = (drv * rn * (sg * (1.0 + g * (1.0 - sg)))).astype(BF16)
        drn = drv * (g * sg)
        dout = rstd * (drn - jnp.mean(drn, axis=-1, keepdims=True)
                       - rn * jnp.mean(drn * rn, axis=-1, keepdims=True))
        dob = dout.reshape(cg, CHUNK, HEAD_DIM).astype(BF16)

        dsb = (_ein("ncd,nmd->ncm", dob, vb) * mask).astype(BF16)
        dv_ref[...] = _ein("ncm,ncd->nmd", sb, dob).reshape(tg, HEAD_DIM)
        dk_ref[...] = _ein("ncm,ncd->nmd", dsb, qb).reshape(tg, HEAD_DIM)
        dq3 = _ein("ncm,nmd->ncd", dsb, kb) + _ein("nce,nde->ncd", dob, s_ref[...].astype(BF16)) * qdv
        dq_ref[...] = _rot_t(dq3.reshape(tg, HEAD_DIM) * scale, cosv, sinv).astype(BF16)
        ds_ref[...] = _ein("ncd,nce->nde", qdb, dob)

    tab = pl.BlockSpec((tg, HEAD_DIM), lambda h, g: (g, 0))
    per_head = pl.BlockSpec((None, CHUNK, HEAD_DIM), lambda h, g: (h, 0, 0))
    head_blk = pl.BlockSpec((tg, HEAD_DIM), lambda h, g: (g, h))
    state_blk = pl.BlockSpec((None, cg, HEAD_DIM, HEAD_DIM), lambda h, g: (h, g, 0, 0))
    return _call(
        body, name=name, grid=(RET_HEADS, ng),
        in_specs=[_head_specs(tg, i, fwd) for i in range(4)]
        + [head_blk, state_blk, head_blk, tab, tab, per_head, per_head],
        out_specs=[head_blk, head_blk, head_blk, head_blk, state_blk],
        out_shape=[jax.ShapeDtypeStruct((T, RET_WIDTH), BF16), jax.ShapeDtypeStruct((T, RET_WIDTH), BF16),
                   jax.ShapeDtypeStruct((T, RET_WIDTH), F32), jax.ShapeDtypeStruct((T, RET_WIDTH), F32),
                   jax.ShapeDtypeStruct((RET_HEADS, N, HEAD_DIM, HEAD_DIM), F32)],
        sem=("parallel", "parallel"), operands=(z, z, z, z, o_pre, s_all, dr, cosf, sinf, intra, qdec))


def _ret_bwd_state(z, dkp, dvp, ds, consts, cg, name):
    T = z.shape[0]
    N = T // CHUNK
    ng = N // cg
    tg = cg * CHUNK
    cosf, sinf, _, kdec, _, cdb = consts
    rev = lambda g: ng - 1 - g

    def body(zk, zv, dkp_ref, dvp_ref, ds_ref, cos_ref, sin_ref, kd_ref, cd_ref, dk_ref, dv_ref, gs_ref, dkv_ref):
        @pl.when(pl.program_id(1) == 0)
        def _():
            gs_ref[...] = jnp.zeros_like(gs_ref)

        cosv = cos_ref[...]
        sinv = sin_ref[...]
        cd = cd_ref[0:1, :]
        grad = gs_ref[...]
        for n in reversed(range(cg)):
            dkv_ref[n] = grad
            grad = ds_ref[n] + cd * grad
        gs_ref[...] = grad
        dkvb = dkv_ref[...].astype(BF16)
        kdv = kd_ref[...][None]
        k3 = _rot(zk[...].astype(F32), cosv, sinv).reshape(cg, CHUNK, HEAD_DIM)
        vb = zv[...].reshape(cg, CHUNK, HEAD_DIM).astype(BF16)
        dk3 = _ein("nce,nde->ncd", vb, dkvb) * kdv
        dv3 = _ein("ncd,nde->nce", (k3 * kdv).astype(BF16), dkvb)
        dk_ref[...] = _rot_t(dkp_ref[...] + dk3.reshape(tg, HEAD_DIM), cosv, sinv).astype(BF16)
        dv_ref[...] = (dvp_ref[...] + dv3.reshape(tg, HEAD_DIM)).astype(BF16)

    tab = pl.BlockSpec((tg, HEAD_DIM), lambda h, g: (rev(g), 0))
    head_blk = pl.BlockSpec((tg, HEAD_DIM), lambda h, g: (rev(g), h))
    return _call(
        body, name=name, grid=(RET_HEADS, ng),
        in_specs=[_head_specs(tg, 1, rev), _head_specs(tg, 2, rev), head_blk, head_blk,
                  pl.BlockSpec((None, cg, HEAD_DIM, HEAD_DIM), lambda h, g: (h, rev(g), 0, 0)),
                  tab, tab,
                  pl.BlockSpec((None, CHUNK, HEAD_DIM), lambda h, g: (h, 0, 0)),
                  pl.BlockSpec((None, 8, HEAD_DIM), lambda h, g: (h, 0, 0))],
        out_specs=[head_blk, head_blk],
        out_shape=[jax.ShapeDtypeStruct((T, RET_WIDTH), BF16)] * 2,
        scratch=[pltpu.VMEM((HEAD_DIM, HEAD_DIM), F32), pltpu.VMEM((cg, HEAD_DIM, HEAD_DIM), F32)],
        sem=("parallel", "arbitrary"), operands=(z, z, dkp, dvp, ds, cosf, sinf, kdec, cdb))


def _z_segments(pieces, ns):
    segs, at = [], 0
    for k, p in enumerate(pieces):
        width = p.shape[1]
        lo = at
        while lo < at + width:
            s = lo // ns
            hi = min(at + width, (s + 1) * ns)
            segs.append((k, lo - at, hi - at, s, lo - s * ns, hi - s * ns))
            lo = hi
        at += width
    assert at == N_CHIPS * ns and all(v % LANES == 0 for seg in segs for v in (seg[1], seg[2], seg[4], seg[5]))
    return segs


def _inproj_bwd_dx(pieces, win, h, gain, dh_in, layer, tm, pad, name, rider=None):
    T, D = h.shape
    Ns = win.shape[-1]
    n = len(pieces)
    segs = _z_segments(pieces, Ns)

    def body(*refs):
        piece_refs = refs[:n]
        w_ref, h_ref, g_ref, dhi_ref, dh_ref, dgain_ref = refs[n:]
        t = pl.program_id(0)

        @pl.when(t == 0)
        def _():
            dgain_ref[...] = jnp.zeros_like(dgain_ref)

        db = None
        for k, a, b, s, c, d in segs:
            term = _dot_nt(piece_refs[k][:, a:b], w_ref[s, :, c:d])
            db = term if db is None else db + term
        dx, dgain = _rms_bwd(h_ref[...], g_ref[...], db)
        dgain_ref[...] += dgain
        dh_ref[...] = jnp.where(_row_mask(t, tm, pad, (tm, D)), dhi_ref[...] + dx, 0.0)

    row = pl.BlockSpec((tm, D), lambda t: (t, 0))
    return _call(
        body, name=name, grid=(T // tm,),
        in_specs=[pl.BlockSpec((tm, p.shape[1]), lambda t: (t, 0)) for p in pieces]
        + [pl.BlockSpec((N_CHIPS, D, Ns), lambda t: (0, 0, 0)), row,
           pl.BlockSpec((None, 1, D), lambda t: (layer, 0, 0)), row],
        out_specs=[row, pl.BlockSpec((1, D), lambda t: (0, 0))],
        out_shape=[jax.ShapeDtypeStruct((T, D), F32), jax.ShapeDtypeStruct((1, D), F32)],
        sem=("arbitrary",), operands=(*pieces, win, h, gain, dh_in), rider=rider)


def _grad_w_in(b, pieces, ns, tk, name, rider=None):
    T, D = b.shape
    n = len(pieces)
    nt = T // tk
    segs = _z_segments(pieces, ns)
    shards_of = [sorted({s for k, _, _, s, _, _ in segs if k == i}) for i in range(n)]

    def body(*refs):
        b_ref = refs[0]
        piece_refs = refs[1:1 + n]
        o_ref, acc_ref = refs[1 + n:]
        s = pl.program_id(0)
        t = pl.program_id(1)

        @pl.when(t == 0)
        def _():
            acc_ref[...] = jnp.zeros_like(acc_ref)

        for shard in range(N_CHIPS):
            @pl.when(s == shard)
            def _(shard=shard):
                cols = [piece_refs[k][:, a:e] for k, a, e, ss, _, _ in segs if ss == shard]
                dz = cols[0] if len(cols) == 1 else jnp.concatenate(cols, axis=1)
                acc_ref[...] += _dot_tn(b_ref[...], dz)

        @pl.when(t == nt - 1)
        def _():
            o_ref[...] = acc_ref[...].astype(BF16).reshape(1, D, ns)

    def piece_spec(i):
        def index(s, t):
            used = functools.reduce(jnp.logical_or, [s == ss for ss in shards_of[i]])
            return (jnp.where(used, t, 0), 0)
        return pl.BlockSpec((tk, pieces[i].shape[1]), index)

    return _call(
        body, name=name, grid=(N_CHIPS, nt),
        in_specs=[pl.BlockSpec((tk, D), lambda s, t: (t, 0))] + [piece_spec(i) for i in range(n)],
        out_specs=[pl.BlockSpec((1, D, ns), lambda s, t: (s, 0, 0))],
        out_shape=[jax.ShapeDtypeStruct((N_CHIPS, D, ns), BF16)],
        scratch=[pltpu.VMEM((D, ns), F32)],
        sem=("parallel", "arbitrary"), operands=(b, *pieces), rider=rider)[0]


def _sum_pair(gs, rs, c_idx, name):
    n = len(gs)

    def body(c_ref, *refs):
        for g_ref, r_ref, o_ref in zip(refs[:n], refs[n:2 * n], refs[2 * n:]):
            o_ref[...] = (g_ref[...].astype(F32) + r_ref[...].astype(F32)).astype(BF16)

    halves = [pl.BlockSpec((None,) + r.shape[1:], lambda s, c_ref: (s, 0, 0)) for r in rs]
    return pl.pallas_call(
        body,
        name=name,
        grid_spec=pltpu.PrefetchScalarGridSpec(
            num_scalar_prefetch=1,
            grid=(N_CHIPS,),
            in_specs=[pl.BlockSpec((None,) + r.shape[1:], lambda s, c_ref: (s, c_ref[0], 0)) for r in rs] + halves,
            out_specs=halves,
        ),
        out_shape=[jax.ShapeDtypeStruct(r.shape, BF16) for r in rs],
        compiler_params=_params(("parallel",)),
    )(c_idx, *gs, *rs)


def _sum_chips(ps, rs, pos, name):
    n = len(ps)
    quarters = 4

    def body(pos_ref, *refs):
        chip = pos_ref[0]
        for p_ref, r_ref, o_ref in zip(refs[:n], refs[n:2 * n], refs[2 * n:]):
            own = p_ref[...].astype(F32)
            terms = [jnp.where(chip == k, own, r_ref[k].astype(F32)) for k in range(N_CHIPS)]
            o_ref[...] = ((terms[0] + terms[1]) + terms[2]) + terms[3]

    def rows(r):
        assert r.shape[1] % (quarters * BF16_ROWS) == 0, r.shape
        return r.shape[1] // quarters

    return pl.pallas_call(
        body,
        name=name,
        grid_spec=pltpu.PrefetchScalarGridSpec(
            num_scalar_prefetch=1,
            grid=(quarters,),
            in_specs=[pl.BlockSpec((None, rows(r), r.shape[2]), lambda q, pos_ref: (pos_ref[0], q, 0)) for r in rs]
            + [pl.BlockSpec((N_CHIPS, rows(r), r.shape[2]), lambda q, pos_ref: (0, q, 0)) for r in rs],
            out_specs=[pl.BlockSpec((rows(r), r.shape[2]), lambda q, pos_ref: (pos_ref[1] * quarters + q, 0))
                       for r in rs],
        ),
        out_shape=[jax.ShapeDtypeStruct((2 * r.shape[1], r.shape[2]), F32) for r in rs],
        compiler_params=_params(("arbitrary",)),
    )(pos, *ps, *rs)


def _small_all_reduce(p, rider=None):
    rows, width = p.shape
    r = 0 if rider is None else len(rider.ins)

    def body(*refs):
        p_ref, o_ref = refs[0], refs[1 + r]
        sib_ref, slot_ref, ssem, rsem = refs[2 + 2 * r:6 + 2 * r]
        if rider is not None:
            rider.start(refs[1:1 + r], refs[2 + r:2 + 2 * r], refs[6 + 2 * r], refs[7 + 2 * r])
        reduce(p_ref, o_ref, sib_ref, slot_ref, ssem, rsem)
        if rider is not None:
            rider.finish(refs[1:1 + r], refs[2 + r:2 + 2 * r], refs[6 + 2 * r], refs[7 + 2 * r])

    def reduce(p_ref, o_ref, sib_ref, slot_ref, ssem, rsem):
        x, y, c, chip, others = _mesh_pos()
        pair = _remote(p_ref, sib_ref, ssem.at[0], rsem.at[0], (x, y, 1 - c))
        pair.start()
        pair.wait()
        slot_ref[chip] = p_ref[...] + sib_ref[...]
        sends = []
        for j, (ox, oy) in enumerate(others):
            cp = _remote(slot_ref.at[chip], slot_ref.at[chip], ssem.at[1 + j], rsem.at[1 + j], (ox, oy, c))
            cp.start()
            sends.append(cp)
        for j, (ox, oy) in enumerate(others):
            slot = slot_ref.at[2 * ox + oy]
            _remote(slot, slot, ssem.at[1 + j], rsem.at[1 + j], (ox, oy, c)).wait_recv()
        for cp in sends:
            cp.wait_send()
        o_ref[...] = ((slot_ref[0] + slot_ref[1]) + slot_ref[2]) + slot_ref[3]

    vmem = pl.BlockSpec(memory_space=pltpu.VMEM)
    scratch = [pltpu.VMEM((rows, width), F32), pltpu.VMEM((N_CHIPS, rows, width), F32),
               pltpu.SemaphoreType.DMA((4,)), pltpu.SemaphoreType.DMA((4,))]
    if rider is not None:
        scratch += [pltpu.SemaphoreType.DMA((rider.n_sem,)), pltpu.SemaphoreType.DMA((rider.n_sem,))]
    outs = pl.pallas_call(
        body,
        name="small_grads_all_reduce",
        in_specs=[vmem] + [ANY] * r,
        out_specs=[vmem] + [ANY] * r,
        out_shape=[jax.ShapeDtypeStruct(p.shape, F32)] + ([] if rider is None else list(rider.out_shapes)),
        input_output_aliases={} if rider is None else rider.aliases(1, 1),
        scratch_shapes=scratch,
    )(p, *([] if rider is None else rider.ins))
    if rider is not None:
        rider.results = outs[1:]
    return outs[0]


def _adamw(gs, w, m, v, name):
    L, R, C = w.shape
    Ct = gs[0].shape[1]
    tr = _pick_tile(R, 256, 8)

    def body(*refs):
        g_refs = refs[:L]
        w_ref, m_ref, v_ref, go_ref, d_ref, mo_ref, vo_ref = refs[L:]
        layer = pl.program_id(0)
        grad = g_refs[L - 1][...]
        for i in range(L - 2, -1, -1):
            grad = jnp.where(layer == i, g_refs[i][...], grad)
        if Ct != C:
            grad = grad[:, :C]
        m_new = ADAM_B1 * m_ref[...] + (1.0 - ADAM_B1) * grad
        v_new = ADAM_B2 * v_ref[...] + (1.0 - ADAM_B2) * jnp.square(grad)
        m_hat = m_new / (1.0 - ADAM_B1 ** ADAM_STEP)
        v_hat = v_new / (1.0 - ADAM_B2 ** ADAM_STEP)
        go_ref[...] = grad
        d_ref[...] = -ADAM_LR * (m_hat / (jnp.sqrt(v_hat) + ADAM_EPS) + ADAM_WD * w_ref[...])
        mo_ref[...] = m_new
        vo_ref[...] = v_new

    g_specs = [pl.BlockSpec((tr, Ct), functools.partial(lambda l, r, i: (jnp.where(l == i, r, 0), 0), i=i))
               for i in range(L)]
    blk = pl.BlockSpec((None, tr, C), lambda l, r: (l, r, 0))
    return _call(
        body, name=name, grid=(L, R // tr),
        in_specs=g_specs + [blk, blk, blk],
        out_specs=[blk] * 4,
        out_shape=[jax.ShapeDtypeStruct((L, R, C), F32)] * 4,
        sem=("arbitrary", "arbitrary"), operands=(*gs, w, m, v))


_FFN1 = ("ffn1_gate", "ffn1_up", "ffn1_down")
_FFN2 = ("ffn2_gate", "ffn2_up", "ffn2_down")
_MIXW = ("w_ret_up", "w_pool_up", "w_out")
_BIG = _FFN1 + ("w_in",) + _MIXW + _FFN2
_TRANSPOSED = ("ffn1_gate", "ffn1_up", "ffn2_gate", "ffn2_up")
_SMALL = ("ffn1_norm", "mix_norm", "ffn2_norm", "final_norm", "pool_scale", "pool_maps")
_ORDER = ("meta", "ffn1_norm", "ffn1_gate", "ffn1_up", "ffn1_down", "mix_norm", "w_in", "pool_maps",
          "pool_scale", "w_ret_up", "w_pool_up", "w_out", "ffn2_norm", "ffn2_gate", "ffn2_up", "ffn2_down",
          "final_norm")


def _transport(a):
    n, r, c = a.shape
    out = a.astype(BF16)
    if c % LANES:
        out = jnp.concatenate([out, jnp.zeros((n, r, _round_up(c, LANES) - c), BF16)], axis=2)
    if r % LANES:
        out = jnp.concatenate([out, jnp.zeros((n, _round_up(r, LANES) - r, out.shape[2]), BF16)], axis=1)
    return out


def _pack_rows(parts, width):
    rows = [p.reshape(-1, width) for p in parts]
    total = sum(r.shape[0] for r in rows)
    fill = _round_up(total, 8) - total
    if fill:
        rows.append(jnp.zeros((fill, width), F32))
    return jnp.concatenate(rows, axis=0)


def _unpack_rows(packed, shapes, width):
    out, at = [], 0
    for shp in shapes:
        n = math.prod(shp) // width
        out.append(packed[at:at + n].reshape(shp))
        at += n
    return out


class _Weights:
    def __init__(self, shards):
        self.shards = shards
        self.full = {}

    def rider(self, keys):
        r = _gather_rider([(self.shards[n], i) for n, i in keys])
        r.keys = keys
        return r

    def take(self, rider):
        for key, arr in zip(rider.keys, rider.results):
            self.full[key] = arr

    def __call__(self, name, layer):
        return self.full[(name, layer)]


def _local_step(x, meta_full, tgt, w, wts, pad, tm, cg, reducer):
    D = x.shape[1]
    T = pad + N_META + x.shape[0]
    L = w["ffn1_norm"].shape[0]
    pool_maps = w["pool_maps"]
    gains = {n: w[n].reshape(L, 1, D) for n in ("ffn1_norm", "mix_norm", "ffn2_norm")}
    scale3 = w["pool_scale"].reshape(L, 1, POOL_WIDTH)
    consts = _ret_consts(T, pad)
    tl = _pick_tile(T, 2 * tm, BF16_ROWS)
    def gather(keys):
        return wts.rider(keys) if keys and keys[0] not in wts.full else None

    def done(rider):
        if rider is not None:
            wts.take(rider)

    h = jnp.concatenate([jnp.zeros((pad, D), F32), meta_full, x], axis=0)
    saved = []
    for i in range(L):
        s = {"h0": h}
        if ("ffn1_down", i) in wts.full:
            rd = gather([("w_in", i), ("ffn2_gate", i)])
            h, s["a1"], s["g1"], s["u1"], s["act1"] = _ffn_fwd(
                h, gains["ffn1_norm"], wts("ffn1_gate", i), wts("ffn1_up", i), wts("ffn1_down", i), i, tl,
                f"ffn1_fwd_{i}", rd)
            done(rd)
        else:
            rd = gather([("ffn1_down", i), ("w_in", i)])
            s["a1"], s["g1"], s["u1"], s["act1"] = _ffn_fwd_up(
                h, gains["ffn1_norm"], wts("ffn1_gate", i), wts("ffn1_up", i), i, tl, f"ffn1_fwd_up_{i}", rd)
            done(rd)
            rd = gather([(n, i) for n in _MIXW])
            h = _ffn_fwd_down(h, s["act1"], wts("ffn1_down", i), tl, f"ffn1_fwd_down_{i}", rd)
            done(rd)
        s["h1"] = h
        rd = gather([k for k in [(n, i) for n in _MIXW] + [("ffn2_gate", i), ("ffn2_up", i)] if k not in wts.full])
        s["z"], s["b"] = _inproj_fwd(h, gains["mix_norm"], wts("w_in", i), i, tl, f"inproj_fwd_{i}", rd)
        done(rd)
        s["r"], s["o_pre"], s["s_all"] = _ret_fwd(s["z"], consts, cg, f"retention_fwd_{i}")
        s["pm"] = _pool_fwd(s["z"], pool_maps, scale3, i, pad, f"pool_fwd_{i}")
        rd = gather([("ffn2_down", i)])
        h, s["mixed"], s["ret"], s["pool"] = _mix_fwd(
            h, s["r"], s["pm"], s["z"], wts("w_ret_up", i), wts("w_pool_up", i), wts("w_out", i), tl,
            f"mix_fwd_{i}", rd)
        done(rd)
        s["h2"] = h
        rd = gather([(n, i + 1) for n in _FFN1]) if i + 1 < L else None
        h, s["a2"], s["g2"], s["u2"], s["act2"] = _ffn_fwd(
            h, gains["ffn2_norm"], wts("ffn2_gate", i), wts("ffn2_up", i), wts("ffn2_down", i), i, tl,
            f"ffn2_fwd_{i}", rd)
        done(rd)
        saved.append(s)

    dh, loss_acc, d_final = _final_loss(h, w["final_norm"].reshape(1, D), tgt, "final_norm_loss")

    small = {n: [None] * L for n in ("ffn1_norm", "mix_norm", "ffn2_norm", "pool_scale", "pool_maps")}

    carry = {"ffn_act": 1.0, "ffn_in": 2.2, "mix_bwd": 1.0, "inproj_bwd": 1.5, "w_in": 1.0}

    tk = _pick_tile(T, 1408, LANES)

    def grad(n, a, b, i, mode):
        rd = reducer.rider(carry.get(n, 1.0 if i == 0 and n.startswith("ffn") else 0.5))
        reducer.add(n, i, _grad_tn(a, b, mode, 1.0, tk, f"grad_{n}_{i}", rd))
        reducer.done(rd)

    def ffn_bwd(which, dy, h_in, g, u, i, between=None, units=carry["ffn_in"]):
        rd = reducer.rider(carry["ffn_act"])
        dg, du, dyh = _ffn_bwd_act(dy, g, u, wts(f"{which}_down", i), tl, f"{which}_bwd_act_{i}", rd)
        reducer.done(rd)
        if between is not None:
            between(dg, du, dyh)
        rd = reducer.rider(units)
        dh_in, dgain = _ffn_bwd_in(dy, h_in, gains[f"{which}_norm"], dg, du, wts(f"{which}_gate", i),
                                   wts(f"{which}_up", i), i, tl, pad, f"{which}_bwd_in_{i}", rd)
        reducer.done(rd)
        return dh_in, dg, du, dgain, dyh

    for i in reversed(range(L)):
        s = saved[i]
        dh, dg, du, small["ffn2_norm"][i], dyh = ffn_bwd("ffn2", dh, s["h2"], s["g2"], s["u2"], i)
        grad("ffn2_gate", dg, s["a2"], i, "row")
        grad("ffn2_up", du, s["a2"], i, "row")
        grad("ffn2_down", s["act2"], dyh, i, "row")
        reducer.stage(f"ffn2_{i}")
        rd = reducer.rider(carry["mix_bwd"])
        dgab, dret, dpool, dr, dpm = _mix_bwd_dx(
            dh, s["z"], s["ret"], s["pool"], wts("w_out", i), wts("w_ret_up", i), wts("w_pool_up", i), tm,
            f"mix_bwd_{i}", rd)
        reducer.done(rd)
        rd = reducer.rider(0.5)
        g_out, g_ru, g_pu = _grad_mix(s["mixed"], dh, s["r"], dret, s["pm"], dpool, _pick_tile(T, 704, LANES),
                                      f"grad_mix_{i}", rd)
        reducer.done(rd)
        for n, g_n in (("w_out", g_out), ("w_ret_up", g_ru), ("w_pool_up", g_pu)):
            reducer.add(n, i, g_n)
        du_pool, small["pool_maps"][i], small["pool_scale"][i] = _pool_bwd(
            s["z"], dpm, pool_maps, scale3, i, pad, f"pool_bwd_{i}")
        dq, dgr, dkp, dvp, ds = _ret_bwd_local(s["z"], s["o_pre"], s["s_all"], dr, consts,
                                               _pick_tile(T // CHUNK, 11, 1), f"retention_bwd_{i}")
        dk, dv = _ret_bwd_state(s["z"], dkp, dvp, ds, consts, cg, f"retention_bwd_state_{i}")
        dz = [dq, dk, dv, dgr, du_pool, dgab]
        dh2 = dh
        rd = reducer.rider(carry["inproj_bwd"])
        dh, small["mix_norm"][i] = _inproj_bwd_dx(
            dz, wts("w_in", i), s["h1"], gains["mix_norm"], dh2, i, tm, pad, f"inproj_bwd_{i}", rd)
        reducer.done(rd)
        rd = reducer.rider(carry["w_in"])
        reducer.add("w_in", i, _grad_w_in(s["b"], dz, wts("w_in", i).shape[-1], tk, f"grad_w_in_{i}", rd))
        reducer.done(rd)
        reducer.stage(f"mid{i}")
        def ffn1_grads(dg, du, dyh, i=i, s=s):
            grad("ffn1_gate", dg, s["a1"], i, "row")
            if i == 0:
                reducer.stage("gate0")
            grad("ffn1_up", du, s["a1"], i, "row")
            if i == 0:
                reducer.stage("up0")
            grad("ffn1_down", s["act1"], dyh, i, "row")
            reducer.stage(f"end{i}")

        if i == 0:
            dh, _, _, small["ffn1_norm"][i], _ = ffn_bwd("ffn1", dh, s["h0"], s["g1"], s["u1"], i, ffn1_grads, 2.5)
        else:
            dh, dg, du, small["ffn1_norm"][i], dyh = ffn_bwd("ffn1", dh, s["h0"], s["g1"], s["u1"], i)
            ffn1_grads(dg, du, dyh)

    return loss_acc, dh, small, d_final


class _Reducer:
    def __init__(self, unit):
        self.c_idx = lax.axis_index("c").astype(jnp.int32).reshape(1)
        chip = 2 * lax.axis_index("x") + lax.axis_index("y")
        self.pos = jnp.stack([chip, lax.axis_index("c")]).astype(jnp.int32)
        self.pending, self.stages, self.queue, self.halves, self.whole = [], [], [], {}, {}
        self.unit = unit
        self.calls = 0

    def add(self, name, layer, g):
        self.pending.append(((name, layer), g))

    def stage(self, tag):
        if self.pending:
            self.stages.append((tag, self.pending))
            self.pending = []

    def _pair_rider(self):
        if not self.stages:
            return None
        tag, items = self.stages.pop(0)
        rd = _pair_exchange_rider([g for _, g in items])
        rd.tag, rd.keys = tag, [k for k, _ in items]
        return rd

    def _chip_rider(self, units):
        take, keep, size = [], [], 0
        for item in self.queue:
            if units is None or size + item[1].size <= units * self.unit:
                take.append(item)
                size += item[1].size
            else:
                keep.append(item)
        self.queue = keep
        if not take:
            return None
        rd = _chip_exchange_rider([p for _, p in take])
        rd.keys = [k for k, _ in take]
        return rd

    def _gather_rider(self):
        keys = [k for k in self.halves if k not in self.whole]
        if not keys:
            return None
        rd = _pair_gather_rider([self.halves[k] for k in keys])
        rd.keys = keys
        return rd

    def rider(self, units):
        self.riding = (self._pair_rider(), self._chip_rider(units), self._gather_rider())
        return _join(self.riding)

    def done(self, rd):
        if rd is None:
            return
        _split_results(rd)
        pair, chips, gather = self.riding
        if len([r for r in self.riding if r is not None]) == 1:
            (pair or chips or gather).results = rd.results
        self.calls += 1
        if gather is not None:
            self.whole.update(zip(gather.keys, gather.results))
        if pair is not None:
            sums = _sum_pair(pair.ins, pair.results, self.c_idx, f"sum_pair_{pair.tag}")
            self.queue += list(zip(pair.keys, sums))
        if chips is not None:
            sums = _sum_chips(chips.ins, chips.results, self.pos, f"sum_chips_{self.calls}")
            self.halves.update(zip(chips.keys, sums))

    def busy(self):
        assert not self.pending
        return bool(self.stages or self.queue or len(self.whole) < len(self.halves))

    def flush(self):
        self.riding = (self._pair_rider(), self._chip_rider(None), self._gather_rider())
        rd = _join(self.riding)
        _run_rider(rd, f"grads_exchange_tail_{self.calls}")
        self.done(rd)


def _update(loss_acc, grad_x, d_meta_rows, reducer, small, d_final, w, mom, var):
    meta = w["meta"]
    D = w["final_norm"].shape[0]
    L = w["ffn1_norm"].shape[0]
    Dq = D // N_CHIPS

    out = {}

    small_parts = [jnp.concatenate(small[n], axis=0) for n in ("ffn1_norm", "mix_norm", "ffn2_norm")]
    small_parts += [d_final, jnp.concatenate(small["pool_scale"], axis=0), jnp.concatenate(small["pool_maps"], axis=0)]
    loss_row = jnp.pad(loss_acc, ((0, 0), (0, D - loss_acc.shape[1])))
    rd = reducer.rider(None) if reducer.busy() else None
    reduced = _small_all_reduce(_pack_rows(small_parts + [d_meta_rows, loss_row], D), rd)
    reducer.done(rd)
    while reducer.busy():
        reducer.flush()
    for n in _BIG:
        gs = [reducer.whole[(n, i)] for i in range(L)]
        if n in _TRANSPOSED:
            res = _adamw(gs, *(jnp.swapaxes(t[n], 1, 2) for t in (w, mom, var)), f"adamw_{n}")
            out[n] = [jnp.swapaxes(r, 1, 2) for r in res]
        else:
            out[n] = _adamw(gs, w[n], mom[n], var[n], f"adamw_{n}")

    small_shapes = [w[n].shape for n in _SMALL]
    small_rows = sum(math.prod(shp) for shp in small_shapes) // D
    chip = 2 * lax.axis_index("x") + lax.axis_index("y")
    d_meta = lax.dynamic_slice_in_dim(reduced[small_rows:small_rows + N_META], chip * Dq, Dq, axis=1)
    names = _SMALL + ("meta",)
    packed_g = _pack_rows([reduced[:small_rows], d_meta], D)
    packed = [_pack_rows([t[n] for n in names], D) for t in (w, mom, var)]
    res = _adamw([packed_g], packed[0][None], packed[1][None], packed[2][None], "adamw_small")
    shapes = small_shapes + [meta.shape]
    unpacked = [_unpack_rows(r[0], shapes, D) for r in res]
    for k, n in enumerate(names):
        out[n] = tuple(u[k] for u in unpacked)

    loss = reduced[small_rows + N_META, 0]
    return (loss, grad_x) + tuple(out[n][j] for j in range(4) for n in _ORDER)


def kernel(x, meta, ffn1_norm, ffn1_gate, ffn1_up, ffn1_down, mix_norm, w_in, pool_maps, pool_scale, w_ret_up, w_pool_up, w_out, ffn2_norm, ffn2_gate, ffn2_up, ffn2_down, final_norm, loss_target, m_meta, m_ffn1_norm, m_ffn1_gate, m_ffn1_up, m_ffn1_down, m_mix_norm, m_w_in, m_pool_maps, m_pool_scale, m_w_ret_up, m_w_pool_up, m_w_out, m_ffn2_norm, m_ffn2_gate, m_ffn2_up, m_ffn2_down, m_final_norm, v_meta, v_ffn1_norm, v_ffn1_gate, v_ffn1_up, v_ffn1_down, v_mix_norm, v_w_in, v_pool_maps, v_pool_scale, v_w_ret_up, v_w_pool_up, v_w_out, v_ffn2_norm, v_ffn2_gate, v_ffn2_up, v_ffn2_down, v_final_norm):
    args = dict(locals())
    w = {n: args[n] for n in _ORDER}
    mom = {n: args["m_" + n] for n in _ORDER}
    var = {n: args["v_" + n] for n in _ORDER}

    assert x.shape[0] == 1, "one batch element per device"
    seq, D = x.shape[1], x.shape[2]
    assert seq % CHUNK == 0 and D % RET_WIDTH == 0 and (2 * POOL_WIDTH) % D == 0
    pad = (-(seq + N_META)) % CHUNK
    T = seq + N_META + pad
    tm = _pick_tile(T, 528, BF16_ROWS)
    cg = _pick_tile(T // CHUNK, 33, 1)

    shards = {n: _transport(w[n]) for n in _BIG}
    shards["meta"] = meta[None]
    wts = _Weights(shards)
    head = wts.rider([("ffn1_gate", 0), ("ffn1_up", 0), ("meta", 0)])
    _run_rider(head, "weights_gather_head")
    wts.take(head)
    meta_full = jnp.transpose(wts("meta", 0), (1, 0, 2)).reshape(N_META, D)

    reducer = _Reducer(unit=2 * shards["ffn1_gate"][0].size)
    loss_acc, dh, small, d_final = _local_step(x[0], meta_full, loss_target[0], w, wts, pad, tm, cg, reducer)
    grad_x = dh[pad + N_META:][None]
    return _update(loss_acc, grad_x, dh[pad:pad + N_META], reducer, small, d_final, w, mom, var)
```

```python
import functools
import math

import jax
import jax.numpy as jnp
from jax import lax
from jax.experimental import pallas as pl
from jax.experimental.pallas import tpu as pltpu

F32 = jnp.float32
BF16 = jnp.bfloat16

N_META = 16
RET_HEADS = 4
HEAD_DIM = 128
RET_WIDTH = RET_HEADS * HEAD_DIM
POOL_WINDOWS = (2, 4, 8, 16)
POOL_GROUPS = len(POOL_WINDOWS)
POOL_WIDTH = POOL_GROUPS * HEAD_DIM
CHUNK = 128
ROPE_BASE = 10000.0
EPS = 1e-6
ADAM_LR = 0.001
ADAM_B1 = 0.9
ADAM_B2 = 0.999
ADAM_EPS = 1e-08
ADAM_WD = 0.01
ADAM_STEP = 10

N_CHIPS = 4
LANES = 128
BF16_ROWS = 16
V7X_VMEM_LIMIT = 52 * 1024 * 1024
MESH = pl.DeviceIdType.MESH
ANY = pl.BlockSpec(memory_space=pl.ANY)


def _round_up(n, m):
    return -(-n // m) * m


def _pick_tile(n, target, mult):
    best = None
    for d in range(mult, min(n, target) + 1, mult):
        if n % d == 0:
            best = d
    assert best is not None, (n, target, mult)
    return best


def _params(sem=None):
    return pltpu.CompilerParams(dimension_semantics=sem, vmem_limit_bytes=V7X_VMEM_LIMIT)


def _dot(a, b):
    return jnp.dot(a, b, preferred_element_type=F32)


def _dot_nt(a, b):
    return lax.dot_general(a, b, (((1,), (1,)), ((), ())), preferred_element_type=F32)


def _dot_tn(a, b):
    return lax.dot_general(a, b, (((0,), (0,)), ((), ())), preferred_element_type=F32)


def _ein(spec, a, b):
    return jnp.einsum(spec, a, b, preferred_element_type=F32)


def _sigmoid(x):
    return jax.nn.sigmoid(x)


def _rms_fwd(x, gain):
    r = lax.rsqrt(jnp.mean(x * x, axis=-1, keepdims=True) + EPS)
    return x * r * gain


def _rms_bwd(x, gain, da):
    r = lax.rsqrt(jnp.mean(x * x, axis=-1, keepdims=True) + EPS)
    xh = x * r
    dgain = jnp.sum(da * xh, axis=0, keepdims=True)
    dxh = da * gain
    dx = r * (dxh - xh * jnp.mean(dxh * xh, axis=-1, keepdims=True))
    return dx, dgain


def _row_mask(t, tm, pad, shape):
    rows = t * tm + lax.broadcasted_iota(jnp.int32, shape, 0)
    return rows >= pad


def _mesh_pos():
    x, y, c = lax.axis_index("x"), lax.axis_index("y"), lax.axis_index("c")
    others = [(1 - x, y), (x, 1 - y), (1 - x, 1 - y)]
    return x, y, c, 2 * x + y, others


def _half_rows(c, rh):
    return pl.ds(pl.multiple_of(c * rh, rh), rh)


def _remote(src, dst, ssem, rsem, dev):
    return pltpu.make_async_remote_copy(src_ref=src, dst_ref=dst, send_sem=ssem, recv_sem=rsem,
                                        device_id=dev, device_id_type=MESH)


class _Rider:
    def __init__(self, ins, out_shapes, n_sem, start, finish, in_place=False):
        self.ins, self.out_shapes, self.n_sem, self.start, self.finish = ins, out_shapes, n_sem, start, finish
        self.in_place = [in_place] * len(ins)
        self.results = None

    def aliases(self, first_in, first_out):
        return {first_in + i: first_out + i for i, same in enumerate(self.in_place) if same}


class _SemWindow:
    def __init__(self, ref, base):
        self.ref, self.base = ref, base

    @property
    def at(self):
        return self

    def __getitem__(self, k):
        return self.ref.at[self.base + k]


def _join(riders):
    riders = [r for r in riders if r is not None]
    if len(riders) <= 1:
        return riders[0] if riders else None

    def run(which):
        def go(ins, outs, ssem, rsem):
            at, sem = 0, 0
            for r in riders:
                n = len(r.ins)
                getattr(r, which)(ins[at:at + n], outs[at:at + n], _SemWindow(ssem, sem), _SemWindow(rsem, sem))
                at, sem = at + n, sem + r.n_sem
        return go

    joined = _Rider(sum([list(r.ins) for r in riders], []), sum([list(r.out_shapes) for r in riders], []),
                    sum(r.n_sem for r in riders), run("start"), run("finish"))
    joined.in_place = sum([r.in_place for r in riders], [])
    joined.parts = riders
    return joined


def _split_results(rider):
    at = 0
    for r in getattr(rider, "parts", []):
        r.results = rider.results[at:at + len(r.ins)]
        at += len(r.ins)


def _gather_rider(pieces):
    per = 7
    layers = [layer for _, layer in pieces]

    def first_copies(ins, outs, ssem, rsem):
        x, y, c, chip, others = _mesh_pos()
        copies = []
        for i, layer in enumerate(layers):
            mine = _half_rows(c, ins[i].shape[1] // 2)
            for j, (ox, oy) in enumerate(others):
                copies.append(_remote(ins[i].at[layer, mine, :], outs[i].at[chip, mine, :],
                                      ssem.at[per * i + j], rsem.at[per * i + j], (ox, oy, c)))
            copies.append(_remote(ins[i].at[layer], outs[i].at[chip],
                                  ssem.at[per * i + 6], rsem.at[per * i + 6], (x, y, 1 - c)))
        return copies

    def start(ins, outs, ssem, rsem):
        for cp in first_copies(ins, outs, ssem, rsem):
            cp.start()

    def finish(ins, outs, ssem, rsem):
        x, y, c, chip, others = _mesh_pos()
        sibling = (x, y, 1 - c)
        forwards = []
        for i in range(len(layers)):
            mine = _half_rows(c, ins[i].shape[1] // 2)
            for j, (ox, oy) in enumerate(others):
                rows = outs[i].at[2 * ox + oy, mine, :]
                _remote(rows, rows, ssem.at[per * i + j], rsem.at[per * i + j], (ox, oy, c)).wait_recv()
                fwd = _remote(rows, rows, ssem.at[per * i + 3 + j], rsem.at[per * i + 3 + j], sibling)
                fwd.start()
                forwards.append(fwd)
        for i in range(len(layers)):
            theirs = _half_rows(1 - c, ins[i].shape[1] // 2)
            for j, (ox, oy) in enumerate(others):
                rows = outs[i].at[2 * ox + oy, theirs, :]
                _remote(rows, rows, ssem.at[per * i + 3 + j], rsem.at[per * i + 3 + j], sibling).wait_recv()
            own = outs[i].at[chip]
            _remote(own, own, ssem.at[per * i + 6], rsem.at[per * i + 6], sibling).wait_recv()
        for cp in first_copies(ins, outs, ssem, rsem) + forwards:
            cp.wait_send()

    shapes = [jax.ShapeDtypeStruct((N_CHIPS,) + s.shape[1:], s.dtype) for s, _ in pieces]
    return _Rider([s for s, _ in pieces], shapes, per * len(pieces), start, finish)


def _chip_exchange_rider(ps):
    def copies(ins, outs, ssem, rsem):
        x, y, c, chip, others = _mesh_pos()
        return [_remote(ins[i].at[2 * ox + oy], outs[i].at[chip], ssem.at[3 * i + j], rsem.at[3 * i + j], (ox, oy, c))
                for i in range(len(ps)) for j, (ox, oy) in enumerate(others)]

    def start(ins, outs, ssem, rsem):
        for cp in copies(ins, outs, ssem, rsem):
            cp.start()

    def finish(ins, outs, ssem, rsem):
        x, y, c, chip, others = _mesh_pos()
        for i in range(len(ps)):
            for j, (ox, oy) in enumerate(others):
                slot = outs[i].at[2 * ox + oy]
                _remote(slot, slot, ssem.at[3 * i + j], rsem.at[3 * i + j], (ox, oy, c)).wait_recv()
        for cp in copies(ins, outs, ssem, rsem):
            cp.wait_send()

    return _Rider(list(ps), [jax.ShapeDtypeStruct(p.shape, p.dtype) for p in ps], 3 * len(ps), start, finish)


def _pair_exchange_rider(gs):
    def copies(ins, outs, ssem, rsem):
        x, y, c, _, _ = _mesh_pos()
        return [_remote(ins[i].at[:, _half_rows(1 - c, ins[i].shape[1] // 2), :], outs[i],
                        ssem.at[i], rsem.at[i], (x, y, 1 - c)) for i in range(len(gs))]

    def start(ins, outs, ssem, rsem):
        for cp in copies(ins, outs, ssem, rsem):
            cp.start()

    def finish(ins, outs, ssem, rsem):
        for cp in copies(ins, outs, ssem, rsem):
            cp.wait()

    shapes = [jax.ShapeDtypeStruct((g.shape[0], g.shape[1] // 2, g.shape[2]), g.dtype) for g in gs]
    return _Rider(list(gs), shapes, len(gs), start, finish)


def _run_rider(rider, name):
    def body(*refs):
        n = len(rider.ins)
        ins, outs = refs[:n], refs[n:2 * n]
        ssem, rsem = refs[2 * n:]
        rider.start(ins, outs, ssem, rsem)
        rider.finish(ins, outs, ssem, rsem)

    rider.results = pl.pallas_call(
        body,
        name=name,
        in_specs=[ANY] * len(rider.ins),
        out_specs=[ANY] * len(rider.ins),
        out_shape=rider.out_shapes,
        input_output_aliases=rider.aliases(0, 0),
        scratch_shapes=[pltpu.SemaphoreType.DMA((rider.n_sem,)), pltpu.SemaphoreType.DMA((rider.n_sem,))],
    )(*rider.ins)
    return rider.results


def _pair_gather_rider(fs):
    n = len(fs)

    def copies(outs, ssem, rsem):
        x, y, c, _, _ = _mesh_pos()
        halves = [outs[i].at[_half_rows(c, outs[i].shape[0] // 2), :] for i in range(n)]
        return [_remote(h, h, ssem.at[i], rsem.at[i], (x, y, 1 - c)) for i, h in enumerate(halves)]

    def start(ins, outs, ssem, rsem):
        for cp in copies(outs, ssem, rsem):
            cp.start()

    def finish(ins, outs, ssem, rsem):
        x, y, c, _, _ = _mesh_pos()
        for i in range(n):
            theirs = outs[i].at[_half_rows(1 - c, outs[i].shape[0] // 2), :]
            _remote(theirs, theirs, ssem.at[i], rsem.at[i], (x, y, 1 - c)).wait_recv()
        for cp in copies(outs, ssem, rsem):
            cp.wait_send()

    return _Rider(list(fs), [jax.ShapeDtypeStruct(f.shape, f.dtype) for f in fs], n, start, finish, in_place=True)


def _call(body, *, name, grid, in_specs, out_specs, out_shape, operands, scratch=(), sem=None, rider=None):
    if rider is None:
        return pl.pallas_call(
            body, name=name, grid=grid, in_specs=in_specs, out_specs=out_specs, out_shape=out_shape,
            scratch_shapes=list(scratch), compiler_params=_params(sem))(*operands)
    n_in, n_out, n_sc, r = len(in_specs), len(out_specs), len(scratch), len(rider.ins)

    def carrying(*refs):
        a, b = n_in, n_in + r
        c, d = b + n_out, b + n_out + r
        e = d + n_sc
        ids = [pl.program_id(k) for k in range(len(grid))]
        first = functools.reduce(jnp.logical_and, [i == 0 for i in ids])
        last = functools.reduce(jnp.logical_and, [i == g - 1 for i, g in zip(ids, grid)])

        @pl.when(first)
        def _():
            rider.start(refs[a:b], refs[c:d], refs[e], refs[e + 1])

        body(*refs[:a], *refs[b:c], *refs[d:e])

        @pl.when(last)
        def _():
            rider.finish(refs[a:b], refs[c:d], refs[e], refs[e + 1])

    outs = pl.pallas_call(
        carrying, name=name, grid=grid,
        in_specs=list(in_specs) + [ANY] * r,
        out_specs=list(out_specs) + [ANY] * r,
        out_shape=list(out_shape) + list(rider.out_shapes),
        scratch_shapes=list(scratch) + [pltpu.SemaphoreType.DMA((rider.n_sem,)), pltpu.SemaphoreType.DMA((rider.n_sem,))],
        input_output_aliases=rider.aliases(n_in, n_out),
        compiler_params=_params(("arbitrary",) * len(grid)),
    )(*operands, *rider.ins)
    rider.results = outs[n_out:]
    return outs[:n_out]


def _ffn_fwd(h, gain, wg, wu, wd, layer, tm, name, rider=None):
    T, D = h.shape
    Fs = wg.shape[-1]
    F = N_CHIPS * Fs

    def body(h_ref, g_ref, wg_ref, wu_ref, wd_ref, ho_ref, a_ref, go_ref, uo_ref, act_ref, acc_ref):
        s = pl.program_id(1)

        @pl.when(s == 0)
        def _():
            a_ref[...] = _rms_fwd(h_ref[...], g_ref[...]).astype(BF16)
            acc_ref[...] = jnp.zeros_like(acc_ref)

        a = a_ref[...]
        g = _dot(a, wg_ref[...])
        u = _dot(a, wu_ref[...])
        sg = _sigmoid(g)
        act = (g * sg * u).astype(BF16)
        go_ref[...] = (u * (sg * (1.0 + g * (1.0 - sg)))).astype(BF16)
        uo_ref[...] = (g * sg).astype(BF16)
        act_ref[...] = act
        acc_ref[...] += _dot(act, wd_ref[...])

        @pl.when(s == N_CHIPS - 1)
        def _():
            ho_ref[...] = h_ref[...] + 0.5 * acc_ref[...]

    row = pl.BlockSpec((tm, D), lambda t, s: (t, 0))
    col = pl.BlockSpec((tm, Fs), lambda t, s: (t, s))
    wcol = pl.BlockSpec((None, D, Fs), lambda t, s: (s, 0, 0))
    return _call(
        body, name=name, grid=(T // tm, N_CHIPS),
        in_specs=[row, pl.BlockSpec((None, 1, D), lambda t, s: (layer, 0, 0)), wcol, wcol,
                  pl.BlockSpec((None, Fs, D), lambda t, s: (s, 0, 0))],
        out_specs=[row, row, col, col, col],
        out_shape=[jax.ShapeDtypeStruct((T, D), F32), jax.ShapeDtypeStruct((T, D), BF16)]
        + [jax.ShapeDtypeStruct((T, F), BF16)] * 3,
        scratch=[pltpu.VMEM((tm, D), F32)],
        sem=("parallel", "arbitrary"), operands=(h, gain, wg, wu, wd), rider=rider)


def _ffn_fwd_up(h, gain, wg, wu, layer, tm, name, rider=None):
    T, D = h.shape
    Fs = wg.shape[-1]
    F = N_CHIPS * Fs

    def body(h_ref, g_ref, wg_ref, wu_ref, a_ref, go_ref, uo_ref, act_ref):
        @pl.when(pl.program_id(1) == 0)
        def _():
            a_ref[...] = _rms_fwd(h_ref[...], g_ref[...]).astype(BF16)

        a = a_ref[...]
        g = _dot(a, wg_ref[...])
        u = _dot(a, wu_ref[...])
        sg = _sigmoid(g)
        act_ref[...] = (g * sg * u).astype(BF16)
        go_ref[...] = (u * (sg * (1.0 + g * (1.0 - sg)))).astype(BF16)
        uo_ref[...] = (g * sg).astype(BF16)

    row = pl.BlockSpec((tm, D), lambda t, s: (t, 0))
    col = pl.BlockSpec((tm, Fs), lambda t, s: (t, s))
    wcol = pl.BlockSpec((None, D, Fs), lambda t, s: (s, 0, 0))
    return _call(
        body, name=name, grid=(T // tm, N_CHIPS),
        in_specs=[row, pl.BlockSpec((None, 1, D), lambda t, s: (layer, 0, 0)), wcol, wcol],
        out_specs=[row, col, col, col],
        out_shape=[jax.ShapeDtypeStruct((T, D), BF16)] + [jax.ShapeDtypeStruct((T, F), BF16)] * 3,
        sem=("parallel", "arbitrary"), operands=(h, gain, wg, wu), rider=rider)


def _ffn_fwd_down(h, act, wd, tm, name, rider=None):
    T, D = h.shape
    Fs = wd.shape[1]

    def body(h_ref, act_ref, wd_ref, ho_ref, acc_ref):
        s = pl.program_id(1)

        @pl.when(s == 0)
        def _():
            acc_ref[...] = jnp.zeros_like(acc_ref)

        acc_ref[...] += _dot(act_ref[...], wd_ref[...])

        @pl.when(s == N_CHIPS - 1)
        def _():
            ho_ref[...] = h_ref[...] + 0.5 * acc_ref[...]

    row = pl.BlockSpec((tm, D), lambda t, s: (t, 0))
    return _call(
        body, name=name, grid=(T // tm, N_CHIPS),
        in_specs=[row, pl.BlockSpec((tm, Fs), lambda t, s: (t, s)), pl.BlockSpec((None, Fs, D), lambda t, s: (s, 0, 0))],
        out_specs=[row],
        out_shape=[jax.ShapeDtypeStruct((T, D), F32)],
        scratch=[pltpu.VMEM((tm, D), F32)],
        sem=("parallel", "arbitrary"), operands=(h, act, wd), rider=rider)[0]


def _inproj_fwd(h, gain, win, layer, tm, name, rider=None):
    T, D = h.shape
    Ns = win.shape[-1]

    def body(h_ref, g_ref, w_ref, z_ref, b_ref):
        @pl.when(pl.program_id(1) == 0)
        def _():
            b_ref[...] = _rms_fwd(h_ref[...], g_ref[...]).astype(BF16)

        z_ref[...] = _dot(b_ref[...], w_ref[...]).astype(BF16)

    return _call(
        body, name=name, grid=(T // tm, N_CHIPS),
        in_specs=[pl.BlockSpec((tm, D), lambda t, s: (t, 0)),
                  pl.BlockSpec((None, 1, D), lambda t, s: (layer, 0, 0)),
                  pl.BlockSpec((None, D, Ns), lambda t, s: (s, 0, 0))],
        out_specs=[pl.BlockSpec((tm, Ns), lambda t, s: (t, s)), pl.BlockSpec((tm, D), lambda t, s: (t, 0))],
        out_shape=[jax.ShapeDtypeStruct((T, N_CHIPS * Ns), BF16), jax.ShapeDtypeStruct((T, D), BF16)],
        sem=("parallel", "arbitrary"), operands=(h, gain, win), rider=rider)


def _ret_consts(T, pad):
    half = HEAD_DIM // 2
    inv_freq = ROPE_BASE ** (-jnp.arange(half, dtype=F32) / half)
    pos = jnp.arange(T, dtype=F32) - pad
    ang = pos[:, None] * inv_freq[None, :]
    cos = jnp.cos(ang)
    sin = jnp.sin(ang)
    cosf = jnp.concatenate([cos, cos], axis=1)
    sinf = jnp.concatenate([-sin, sin], axis=1)
    log_gamma = jnp.log1p(-(2.0 ** (-5.0 - jnp.arange(RET_HEADS, dtype=F32))))
    idx = jnp.arange(CHUNK, dtype=F32)
    diff = idx[:, None] - idx[None, :]
    intra = jnp.where(diff[None] >= 0, jnp.exp(diff[None] * log_gamma[:, None, None]), 0.0)
    k_decay = jnp.exp((CHUNK - 1.0 - idx)[None, :] * log_gamma[:, None])
    q_decay = jnp.exp((idx + 1.0)[None, :] * log_gamma[:, None])
    chunk_decay = jnp.exp(CHUNK * log_gamma)
    kdec = jnp.broadcast_to(k_decay[:, :, None], (RET_HEADS, CHUNK, HEAD_DIM))
    qdec = jnp.broadcast_to(q_decay[:, :, None], (RET_HEADS, CHUNK, HEAD_DIM))
    cdb = jnp.broadcast_to(chunk_decay[:, None, None], (RET_HEADS, 8, HEAD_DIM))
    return cosf, sinf, intra, kdec, qdec, cdb


def _rot(t, cosv, sinv):
    return t * cosv + pltpu.roll(t, HEAD_DIM // 2, 1) * sinv


def _rot_t(g, cosv, sinv):
    return g * cosv + pltpu.roll(g * sinv, HEAD_DIM // 2, 1)


def _head_specs(tg, section, order):
    return pl.BlockSpec((tg, HEAD_DIM), lambda h, g: (order(g), section * RET_HEADS + h))


def _ret_fwd(z, consts, cg, name, rider=None):
    T = z.shape[0]
    N = T // CHUNK
    ng = N // cg
    tg = cg * CHUNK
    cosf, sinf, intra, kdec, qdec, cdb = consts
    fwd = lambda g: g

    def body(zq, zk, zv, zg, cos_ref, sin_ref, m_ref, kd_ref, qd_ref, cd_ref, r_ref, o_ref, s_ref, st_ref):
        @pl.when(pl.program_id(1) == 0)
        def _():
            st_ref[...] = jnp.zeros_like(st_ref)

        cosv = cos_ref[...]
        sinv = sin_ref[...]
        q3 = (_rot(zq[...].astype(F32), cosv, sinv) * (HEAD_DIM ** -0.5)).reshape(cg, CHUNK, HEAD_DIM)
        k3 = _rot(zk[...].astype(F32), cosv, sinv).reshape(cg, CHUNK, HEAD_DIM)
        vb = zv[...].reshape(cg, CHUNK, HEAD_DIM).astype(BF16)
        scores = _ein("ncd,nmd->ncm", q3.astype(BF16), k3.astype(BF16)) * m_ref[...][None]
        inner = _ein("ncm,nmd->ncd", scores.astype(BF16), vb)
        kv = _ein("ncd,nce->nde", (k3 * kd_ref[...][None]).astype(BF16), vb)
        cd = cd_ref[0:1, :]
        state = st_ref[...]
        for n in range(cg):
            s_ref[n] = state
            state = state * cd + kv[n]
        st_ref[...] = state
        qdb = (q3 * qd_ref[...][None]).astype(BF16)
        cross = _ein("ncd,nde->nce", qdb, s_ref[...].astype(BF16))
        out = (inner + cross).reshape(tg, HEAD_DIM)
        o_ref[...] = out
        xc = out - jnp.mean(out, axis=-1, keepdims=True)
        rn = xc * lax.rsqrt(jnp.mean(xc * xc, axis=-1, keepdims=True) + EPS)
        g = zg[...].astype(F32)
        r_ref[...] = (rn * (g * _sigmoid(g))).astype(BF16)

    tab = pl.BlockSpec((tg, HEAD_DIM), lambda h, g: (g, 0))
    per_head = lambda rows: pl.BlockSpec((None, rows, HEAD_DIM), lambda h, g: (h, 0, 0))
    head_out = pl.BlockSpec((tg, HEAD_DIM), lambda h, g: (g, h))
    return _call(
        body, name=name, grid=(RET_HEADS, ng),
        in_specs=[_head_specs(tg, i, fwd) for i in range(4)]
        + [tab, tab, per_head(CHUNK), per_head(CHUNK), per_head(CHUNK), per_head(8)],
        out_specs=[head_out, head_out, pl.BlockSpec((None, cg, HEAD_DIM, HEAD_DIM), lambda h, g: (h, g, 0, 0))],
        out_shape=[jax.ShapeDtypeStruct((T, RET_WIDTH), BF16), jax.ShapeDtypeStruct((T, RET_WIDTH), F32),
                   jax.ShapeDtypeStruct((RET_HEADS, N, HEAD_DIM, HEAD_DIM), F32)],
        scratch=[pltpu.VMEM((HEAD_DIM, HEAD_DIM), F32)],
        sem=("parallel", "arbitrary"), operands=(z, z, z, z, cosf, sinf, intra, kdec, qdec, cdb), rider=rider)


def _window_sums(u, shift_of):
    sums = []
    s = u
    k = 1
    while k < POOL_WINDOWS[-1]:
        s = s + pltpu.roll(s, shift_of(k), 0)
        sums.append(s)
        k *= 2
    return sums


def _select_group(vals, g):
    out = vals[-1]
    for i in range(len(vals) - 2, -1, -1):
        out = jnp.where(g == i, vals[i], out)
    return out


def _pool_parts(u, g, T, pad):
    rows = lax.broadcasted_iota(jnp.int32, (T, HEAD_DIM), 0)
    valid = rows >= pad
    win = _select_group([float(w) for w in POOL_WINDOWS], g)
    div = jnp.clip((rows - pad + 1).astype(F32), 1.0, win)
    s = _select_group(_window_sums(u, lambda k: k), g)
    pooled = jnp.where(valid, s / div - u, 0.0)
    return pooled, div, valid


def _pool_specs(T, layer):
    first = 4 * RET_WIDTH // HEAD_DIM
    return [
        pl.BlockSpec((T, HEAD_DIM), lambda g: (0, first + g)),
        pl.BlockSpec((None, None, HEAD_DIM, HEAD_DIM), lambda g: (layer, g, 0, 0)),
        pl.BlockSpec((None, 1, HEAD_DIM), lambda g: (layer, 0, g)),
    ]


def _pool_fwd(z, maps, scale, layer, pad, name):
    T = z.shape[0]
    assert pad >= POOL_WINDOWS[-1], "window rolls wrap into the zero rows in front"

    def body(zu, maps_ref, sc_ref, pm_ref):
        g = pl.program_id(0)
        pooled, _, _ = _pool_parts(zu[...].astype(F32), g, T, pad)
        y = _dot(pooled.astype(BF16), maps_ref[...].astype(BF16))
        pm_ref[...] = (y * sc_ref[...]).astype(BF16)

    return _call(
        body, name=name, grid=(POOL_GROUPS,),
        in_specs=_pool_specs(T, layer),
        out_specs=[pl.BlockSpec((T, HEAD_DIM), lambda g: (0, g))],
        out_shape=[jax.ShapeDtypeStruct((T, POOL_WIDTH), BF16)],
        sem=("parallel",), operands=(z, maps, scale))[0]


def _gate_specs(tm, D):
    nb = D // RET_WIDTH
    first = (4 * RET_WIDTH + POOL_WIDTH) // RET_WIDTH
    return [pl.BlockSpec((tm, RET_WIDTH), functools.partial(lambda t, j: (t, j), j=first + j)) for j in range(2 * nb)]


def _load_gates(refs, nb):
    ga = jnp.concatenate([r[...].astype(F32) for r in refs[:nb]], axis=1)
    gb = jnp.concatenate([r[...].astype(F32) for r in refs[nb:]], axis=1)
    return ga, gb


def _mix_fwd(h, r, pm, z, wru, wpu, wout, tm, name, rider=None):
    T, D = h.shape
    Dq = D // N_CHIPS
    nb = D // RET_WIDTH

    def body(*refs):
        h_ref, r_ref, pm_ref = refs[:3]
        gate_refs = refs[3:3 + 2 * nb]
        wru_ref, wpu_ref, wout_ref, ho_ref, mx_ref, ret_ref, pool_ref = refs[3 + 2 * nb:]
        rv = r_ref[...]
        pv = pm_ref[...]
        ret = jnp.concatenate([_dot(rv, wru_ref[s]) for s in range(N_CHIPS)], axis=1)
        pool = jnp.concatenate([_dot(pv, wpu_ref[s]) for s in range(N_CHIPS)], axis=1)
        ga, gb = _load_gates(gate_refs, nb)
        mixed = (_sigmoid(ga) * ret + _sigmoid(gb) * pool).astype(BF16)
        mx_ref[...] = mixed
        ret_ref[...] = ret.astype(BF16)
        pool_ref[...] = pool.astype(BF16)
        ho_ref[...] = h_ref[...] + _dot(mixed, wout_ref[...].reshape(D, D))

    row = pl.BlockSpec((tm, D), lambda t: (t, 0))
    half = pl.BlockSpec((tm, RET_WIDTH), lambda t: (t, 0))
    up = pl.BlockSpec((N_CHIPS, RET_WIDTH, Dq), lambda t: (0, 0, 0))
    return _call(
        body, name=name, grid=(T // tm,),
        in_specs=[row, half, half] + _gate_specs(tm, D) + [up, up, pl.BlockSpec((N_CHIPS, Dq, D), lambda t: (0, 0, 0))],
        out_specs=[row, row, row, row],
        out_shape=[jax.ShapeDtypeStruct((T, D), F32)] + [jax.ShapeDtypeStruct((T, D), BF16)] * 3,
        sem=("parallel",), operands=(h, r, pm, *([z] * (2 * nb)), wru, wpu, wout), rider=rider)


def _final_loss(h, gain, tgt, name):
    T, D = h.shape
    first = (T - tgt.shape[0]) // CHUNK

    def body(h_ref, g_ref, t_ref, dh_ref, loss_ref, dg_ref):
        i = pl.program_id(0)

        @pl.when(i == 0)
        def _():
            loss_ref[...] = jnp.zeros_like(loss_ref)
            dg_ref[...] = jnp.zeros_like(dg_ref)

        x = h_ref[...]
        gain_v = g_ref[...]
        err = jnp.where(i >= first, _rms_fwd(x, gain_v) - t_ref[...], 0.0)
        loss_ref[...] += 0.5 * jnp.sum(jnp.mean(err * err, axis=-1))
        dx, dgain = _rms_bwd(x, gain_v, err * (1.0 / D))
        dg_ref[...] += dgain
        dh_ref[...] = dx

    return _call(
        body, name=name, grid=(T // CHUNK,),
        in_specs=[pl.BlockSpec((CHUNK, D), lambda i: (i, 0)),
                  pl.BlockSpec((1, D), lambda i: (0, 0)),
                  pl.BlockSpec((CHUNK, D), lambda i: (jnp.maximum(i - first, 0), 0))],
        out_specs=[pl.BlockSpec((CHUNK, D), lambda i: (i, 0)),
                   pl.BlockSpec((1, LANES), lambda i: (0, 0)),
                   pl.BlockSpec((1, D), lambda i: (0, 0))],
        out_shape=[jax.ShapeDtypeStruct((T, D), F32), jax.ShapeDtypeStruct((1, LANES), F32),
                   jax.ShapeDtypeStruct((1, D), F32)],
        sem=("arbitrary",), operands=(h, gain, tgt))


def _ffn_bwd_act(dy, g, u, wd, tm, name, rider=None):
    T, D = dy.shape
    Fs = wd.shape[1]
    F = N_CHIPS * Fs

    def body(dy_ref, go_ref, uo_ref, wd_ref, dg_ref, du_ref, dyh_ref):
        @pl.when(pl.program_id(1) == 0)
        def _():
            dyh_ref[...] = (0.5 * dy_ref[...]).astype(BF16)

        dact = _dot_nt(dyh_ref[...], wd_ref[...])
        du_ref[...] = (dact * uo_ref[...].astype(F32)).astype(BF16)
        dg_ref[...] = (dact * go_ref[...].astype(F32)).astype(BF16)

    row = pl.BlockSpec((tm, D), lambda t, s: (t, 0))
    col = pl.BlockSpec((tm, Fs), lambda t, s: (t, s))
    return _call(
        body, name=name, grid=(T // tm, N_CHIPS),
        in_specs=[row, col, col, pl.BlockSpec((None, Fs, D), lambda t, s: (s, 0, 0))],
        out_specs=[col, col, row],
        out_shape=[jax.ShapeDtypeStruct((T, F), BF16), jax.ShapeDtypeStruct((T, F), BF16),
                   jax.ShapeDtypeStruct((T, D), BF16)],
        sem=("parallel", "arbitrary"), operands=(dy, g, u, wd), rider=rider)


def _ffn_bwd_in(dy, h, gain, dg, du, wg, wu, layer, tm, pad, name, rider=None):
    T, D = h.shape
    Fs = wg.shape[-1]

    def body(dy_ref, h_ref, g_ref, dg_ref, du_ref, wg_ref, wu_ref, dh_ref, dgain_ref, da_ref):
        t = pl.program_id(0)
        s = pl.program_id(1)

        @pl.when((t == 0) & (s == 0))
        def _():
            dgain_ref[...] = jnp.zeros_like(dgain_ref)

        @pl.when(s == 0)
        def _():
            da_ref[...] = jnp.zeros_like(da_ref)

        da_ref[...] += _dot_nt(dg_ref[...], wg_ref[...]) + _dot_nt(du_ref[...], wu_ref[...])

        @pl.when(s == N_CHIPS - 1)
        def _():
            dx, dgain = _rms_bwd(h_ref[...], g_ref[...], da_ref[...])
            dgain_ref[...] += dgain
            dh_ref[...] = jnp.where(_row_mask(t, tm, pad, (tm, D)), dy_ref[...] + dx, 0.0)

    row = pl.BlockSpec((tm, D), lambda t, s: (t, 0))
    col = pl.BlockSpec((tm, Fs), lambda t, s: (t, s))
    wcol = pl.BlockSpec((None, D, Fs), lambda t, s: (s, 0, 0))
    return _call(
        body, name=name, grid=(T // tm, N_CHIPS),
        in_specs=[row, row, pl.BlockSpec((None, 1, D), lambda t, s: (layer, 0, 0)), col, col, wcol, wcol],
        out_specs=[row, pl.BlockSpec((1, D), lambda t, s: (0, 0))],
        out_shape=[jax.ShapeDtypeStruct((T, D), F32), jax.ShapeDtypeStruct((1, D), F32)],
        scratch=[pltpu.VMEM((tm, D), F32)],
        sem=("arbitrary", "arbitrary"), operands=(dy, h, gain, dg, du, wg, wu), rider=rider)


def _grad_tn(a, b, mode, scale, tm, name, rider=None):
    T = a.shape[0]
    if mode == "col":
        per, R, C = 1, a.shape[1], b.shape[1] // N_CHIPS
        a_spec = pl.BlockSpec((tm, R), lambda s, t: (t, 0))
        b_spec = pl.BlockSpec((tm, C), lambda s, t: (t, s))
    else:
        per, R, C = 2, a.shape[1] // N_CHIPS, b.shape[1]
        a_spec = pl.BlockSpec((tm, per * R), lambda s, t: (t, s))
        b_spec = pl.BlockSpec((tm, C), lambda s, t: (t, 0))
    nt = T // tm

    def body(a_ref, b_ref, o_ref, acc_ref):
        t = pl.program_id(1)

        @pl.when(t == 0)
        def _():
            acc_ref[...] = jnp.zeros_like(acc_ref)

        acc_ref[...] += _dot_tn(a_ref[...].astype(BF16), b_ref[...].astype(BF16))

        @pl.when(t == nt - 1)
        def _():
            o_ref[...] = (scale * acc_ref[...]).astype(BF16).reshape(per, R, C)

    return _call(
        body, name=name, grid=(N_CHIPS // per, nt),
        in_specs=[a_spec, b_spec],
        out_specs=[pl.BlockSpec((per, R, C), lambda s, t: (s, 0, 0))],
        out_shape=[jax.ShapeDtypeStruct((N_CHIPS, R, C), BF16)],
        scratch=[pltpu.VMEM((per * R, C), F32)],
        sem=("parallel", "arbitrary"), operands=(a, b), rider=rider)[0]


def _grad_mix(mixed, dh, r, dret, pm, dpool, tk, name, rider=None):
    T, D = dh.shape
    Dq = D // N_CHIPS
    nt = T // tk

    def body(mx_ref, dh_ref, r_ref, dret_ref, pm_ref, dpool_ref, go_ref, gr_ref, gp_ref, ao_ref, ar_ref, ap_ref):
        t = pl.program_id(0)

        @pl.when(t == 0)
        def _():
            ao_ref[...] = jnp.zeros_like(ao_ref)
            ar_ref[...] = jnp.zeros_like(ar_ref)
            ap_ref[...] = jnp.zeros_like(ap_ref)

        ao_ref[...] += _dot_tn(mx_ref[...], dh_ref[...].astype(BF16))
        ar_ref[...] += _dot_tn(r_ref[...], dret_ref[...])
        ap_ref[...] += _dot_tn(pm_ref[...], dpool_ref[...])

        @pl.when(t == nt - 1)
        def _():
            go_ref[...] = ao_ref[...].astype(BF16).reshape(N_CHIPS, Dq, D)
            for s in range(N_CHIPS):
                gr_ref[s] = ar_ref[:, s * Dq:(s + 1) * Dq].astype(BF16)
                gp_ref[s] = ap_ref[:, s * Dq:(s + 1) * Dq].astype(BF16)

    row = pl.BlockSpec((tk, D), lambda t: (t, 0))
    half = pl.BlockSpec((tk, RET_WIDTH), lambda t: (t, 0))
    whole = lambda shape: pl.BlockSpec(shape, lambda t: (0, 0, 0))
    return _call(
        body, name=name, grid=(nt,),
        in_specs=[row, row, half, row, half, row],
        out_specs=[whole((N_CHIPS, Dq, D)), whole((N_CHIPS, RET_WIDTH, Dq)), whole((N_CHIPS, POOL_WIDTH, Dq))],
        out_shape=[jax.ShapeDtypeStruct((N_CHIPS, Dq, D), BF16),
                   jax.ShapeDtypeStruct((N_CHIPS, RET_WIDTH, Dq), BF16),
                   jax.ShapeDtypeStruct((N_CHIPS, POOL_WIDTH, Dq), BF16)],
        scratch=[pltpu.VMEM((D, D), F32), pltpu.VMEM((RET_WIDTH, D), F32), pltpu.VMEM((POOL_WIDTH, D), F32)],
        sem=("arbitrary",), operands=(mixed, dh, r, dret, pm, dpool), rider=rider)


def _mix_bwd_dx(dh, z, ret, pool, wout, wru, wpu, tm, name, rider=None):
    T, D = dh.shape
    Dq = D // N_CHIPS
    nb = D // RET_WIDTH

    def body(*refs):
        dh_ref = refs[0]
        gate_refs = refs[1:1 + 2 * nb]
        ret_ref, pool_ref, wout_ref, wru_ref, wpu_ref, dgab_ref, dret_ref, dpool_ref, dr_ref, dpm_ref = refs[1 + 2 * nb:]
        dmixed = _dot_nt(dh_ref[...].astype(BF16), wout_ref[...].reshape(D, D))
        ga, gb = _load_gates(gate_refs, nb)
        sa = _sigmoid(ga)
        sb = _sigmoid(gb)
        dgab_ref[:, :D] = (dmixed * ret_ref[...].astype(F32) * (sa * (1.0 - sa))).astype(BF16)
        dgab_ref[:, D:] = (dmixed * pool_ref[...].astype(F32) * (sb * (1.0 - sb))).astype(BF16)
        dret = (dmixed * sa).astype(BF16)
        dpool = (dmixed * sb).astype(BF16)
        dret_ref[...] = dret
        dpool_ref[...] = dpool
        dr = _dot_nt(dret[:, :Dq], wru_ref[0])
        dpm = _dot_nt(dpool[:, :Dq], wpu_ref[0])
        for s in range(1, N_CHIPS):
            dr += _dot_nt(dret[:, s * Dq:(s + 1) * Dq], wru_ref[s])
            dpm += _dot_nt(dpool[:, s * Dq:(s + 1) * Dq], wpu_ref[s])
        dr_ref[...] = dr
        dpm_ref[...] = dpm

    row = pl.BlockSpec((tm, D), lambda t: (t, 0))
    half = pl.BlockSpec((tm, RET_WIDTH), lambda t: (t, 0))
    up = pl.BlockSpec((N_CHIPS, RET_WIDTH, Dq), lambda t: (0, 0, 0))
    return _call(
        body, name=name, grid=(T // tm,),
        in_specs=[row] + _gate_specs(tm, D) + [row, row, pl.BlockSpec((N_CHIPS, Dq, D), lambda t: (0, 0, 0)), up, up],
        out_specs=[pl.BlockSpec((tm, 2 * D), lambda t: (t, 0)), row, row, half, half],
        out_shape=[jax.ShapeDtypeStruct((T, 2 * D), BF16), jax.ShapeDtypeStruct((T, D), BF16),
                   jax.ShapeDtypeStruct((T, D), BF16), jax.ShapeDtypeStruct((T, RET_WIDTH), F32),
                   jax.ShapeDtypeStruct((T, POOL_WIDTH), F32)],
        sem=("parallel",), operands=(dh, *([z] * (2 * nb)), ret, pool, wout, wru, wpu), rider=rider)


def _pool_bwd(z, dpm, maps, scale, layer, pad, name):
    T = z.shape[0]

    def body(zu, maps_ref, sc_ref, dpm_ref, du_ref, dmaps_ref, dsc_ref):
        g = pl.program_id(0)
        u = zu[...].astype(F32)
        pooled, div, valid = _pool_parts(u, g, T, pad)
        pb = pooled.astype(BF16)
        mb = maps_ref[...].astype(BF16)
        dp = dpm_ref[...]
        dsc_ref[...] = jnp.sum(dp * _dot(pb, mb), axis=0, keepdims=True)
        dyb = (dp * sc_ref[...]).astype(BF16)
        dmaps_ref[...] = _dot_tn(pb, dyb)
        dpooled = jnp.where(valid, _dot_nt(dyb, mb), 0.0)
        ahead = _select_group(_window_sums(dpooled / div, lambda k: T - k), g)
        du_ref[...] = jnp.where(valid, ahead - dpooled, 0.0).astype(BF16)

    blk = pl.BlockSpec((T, HEAD_DIM), lambda g: (0, g))
    return _call(
        body, name=name, grid=(POOL_GROUPS,),
        in_specs=_pool_specs(T, layer) + [blk],
        out_specs=[blk, pl.BlockSpec((None, HEAD_DIM, HEAD_DIM), lambda g: (g, 0, 0)),
                   pl.BlockSpec((1, HEAD_DIM), lambda g: (0, g))],
        out_shape=[jax.ShapeDtypeStruct((T, POOL_WIDTH), BF16),
                   jax.ShapeDtypeStruct((POOL_GROUPS, HEAD_DIM, HEAD_DIM), F32),
                   jax.ShapeDtypeStruct((1, POOL_WIDTH), F32)],
        sem=("parallel",), operands=(z, maps, scale, dpm))


def _ret_bwd_local(z, o_pre, s_all, dr, consts, cg, name):
    T = z.shape[0]
    N = T // CHUNK
    ng = N // cg
    tg = cg * CHUNK
    cosf, sinf, intra, _, qdec, _ = consts
    fwd = lambda g: g

    def body(zq, zk, zv, zg, o_ref, s_ref, dr_ref, cos_ref, sin_ref, m_ref, qd_ref,
             dq_ref, dg_ref, dk_ref, dv_ref, ds_ref):
        cosv = cos_ref[...]
        sinv = sin_ref[...]
        scale = HEAD_DIM ** -0.5
        q3 = (_rot(zq[...].astype(F32), cosv, sinv) * scale).reshape(cg, CHUNK, HEAD_DIM)
        k3 = _rot(zk[...].astype(F32), cosv, sinv).reshape(cg, CHUNK, HEAD_DIM)
        qb = q3.astype(BF16)
        kb = k3.astype(BF16)
        vb = zv[...].reshape(cg, CHUNK, HEAD_DIM).astype(BF16)
        mask = m_ref[...][None]
        sb = (_ein("ncd,nmd->ncm", qb, kb) * mask).astype(BF16)
        qdv = qd_ref[...][None]
        qdb = (q3 * qdv).astype(BF16)

        out = o_ref[...]
        xc = out - jnp.mean(out, axis=-1, keepdims=True)
        rstd = lax.rsqrt(jnp.mean(xc * xc, axis=-1, keepdims=True) + EPS)
        rn = xc * rstd
        g = zg[...].astype(F32)
        sg = _sigmoid(g)
        drv = dr_ref[...]
        dg_ref[...] = (drv * rn * (sg * (1.0 + g * (1.0 - sg)))).astype(BF16)
        drn = drv * (g * sg)
        dout = rstd * (drn - jnp.mean(drn, axis=-1, keepdims=True)
                       - rn * jnp.mean(drn * rn, axis=-1, keepdims=True))
        dob = dout.reshape(cg, CHUNK, HEAD_DIM).astype(BF16)

        dsb = (_ein("ncd,nmd->ncm", dob, vb) * mask).astype(BF16)
        dv_ref[...] = _ein("ncm,ncd->nmd", sb, dob).reshape(tg, HEAD_DIM)
        dk_ref[...] = _ein("ncm,ncd->nmd", dsb, qb).reshape(tg, HEAD_DIM)
        dq3 = _ein("ncm,nmd->ncd", dsb, kb) + _ein("nce,nde->ncd", dob, s_ref[...].astype(BF16)) * qdv
        dq_ref[...] = _rot_t(dq3.reshape(tg, HEAD_DIM) * scale, cosv, sinv).astype(BF16)
        ds_ref[...] = _ein("ncd,nce->nde", qdb, dob)

    tab = pl.BlockSpec((tg, HEAD_DIM), lambda h, g: (g, 0))
    per_head = pl.BlockSpec((None, CHUNK, HEAD_DIM), lambda h, g: (h, 0, 0))
    head_blk = pl.BlockSpec((tg, HEAD_DIM), lambda h, g: (g, h))
    state_blk = pl.BlockSpec((None, cg, HEAD_DIM, HEAD_DIM), lambda h, g: (h, g, 0, 0))
    return _call(
        body, name=name, grid=(RET_HEADS, ng),
        in_specs=[_head_specs(tg, i, fwd) for i in range(4)]
        + [head_blk, state_blk, head_blk, tab, tab, per_head, per_head],
        out_specs=[head_blk, head_blk, head_blk, head_blk, state_blk],
        out_shape=[jax.ShapeDtypeStruct((T, RET_WIDTH), BF16), jax.ShapeDtypeStruct((T, RET_WIDTH), BF16),
                   jax.ShapeDtypeStruct((T, RET_WIDTH), F32), jax.ShapeDtypeStruct((T, RET_WIDTH), F32),
                   jax.ShapeDtypeStruct((RET_HEADS, N, HEAD_DIM, HEAD_DIM), F32)],
        sem=("parallel", "parallel"), operands=(z, z, z, z, o_pre, s_all, dr, cosf, sinf, intra, qdec))


def _ret_bwd_state(z, dkp, dvp, ds, consts, cg, name):
    T = z.shape[0]
    N = T // CHUNK
    ng = N // cg
    tg = cg * CHUNK
    cosf, sinf, _, kdec, _, cdb = consts
    rev = lambda g: ng - 1 - g

    def body(zk, zv, dkp_ref, dvp_ref, ds_ref, cos_ref, sin_ref, kd_ref, cd_ref, dk_ref, dv_ref, gs_ref, dkv_ref):
        @pl.when(pl.program_id(1) == 0)
        def _():
            gs_ref[...] = jnp.zeros_like(gs_ref)

        cosv = cos_ref[...]
        sinv = sin_ref[...]
        cd = cd_ref[0:1, :]
        grad = gs_ref[...]
        for n in reversed(range(cg)):
            dkv_ref[n] = grad
            grad = ds_ref[n] + cd * grad
        gs_ref[...] = grad
        dkvb = dkv_ref[...].astype(BF16)
        kdv = kd_ref[...][None]
        k3 = _rot(zk[...].astype(F32), cosv, sinv).reshape(cg, CHUNK, HEAD_DIM)
        vb = zv[...].reshape(cg, CHUNK, HEAD_DIM).astype(BF16)
        dk3 = _ein("nce,nde->ncd", vb, dkvb) * kdv
        dv3 = _ein("ncd,nde->nce", (k3 * kdv).astype(BF16), dkvb)
        dk_ref[...] = _rot_t(dkp_ref[...] + dk3.reshape(tg, HEAD_DIM), cosv, sinv).astype(BF16)
        dv_ref[...] = (dvp_ref[...] + dv3.reshape(tg, HEAD_DIM)).astype(BF16)

    tab = pl.BlockSpec((tg, HEAD_DIM), lambda h, g: (rev(g), 0))
    head_blk = pl.BlockSpec((tg, HEAD_DIM), lambda h, g: (rev(g), h))
    return _call(
        body, name=name, grid=(RET_HEADS, ng),
        in_specs=[_head_specs(tg, 1, rev), _head_specs(tg, 2, rev), head_blk, head_blk,
                  pl.BlockSpec((None, cg, HEAD_DIM, HEAD_DIM), lambda h, g: (h, rev(g), 0, 0)),
                  tab, tab,
                  pl.BlockSpec((None, CHUNK, HEAD_DIM), lambda h, g: (h, 0, 0)),
                  pl.BlockSpec((None, 8, HEAD_DIM), lambda h, g: (h, 0, 0))],
        out_specs=[head_blk, head_blk],
        out_shape=[jax.ShapeDtypeStruct((T, RET_WIDTH), BF16)] * 2,
        scratch=[pltpu.VMEM((HEAD_DIM, HEAD_DIM), F32), pltpu.VMEM((cg, HEAD_DIM, HEAD_DIM), F32)],
        sem=("parallel", "arbitrary"), operands=(z, z, dkp, dvp, ds, cosf, sinf, kdec, cdb))


def _z_segments(pieces, ns):
    segs, at = [], 0
    for k, p in enumerate(pieces):
        width = p.shape[1]
        lo = at
        while lo < at + width:
            s = lo // ns
            hi = min(at + width, (s + 1) * ns)
            segs.append((k, lo - at, hi - at, s, lo - s * ns, hi - s * ns))
            lo = hi
        at += width
    assert at == N_CHIPS * ns and all(v % LANES == 0 for seg in segs for v in (seg[1], seg[2], seg[4], seg[5]))
    return segs


def _inproj_bwd_dx(pieces, win, h, gain, dh_in, layer, tm, pad, name, rider=None):
    T, D = h.shape
    Ns = win.shape[-1]
    n = len(pieces)
    segs = _z_segments(pieces, Ns)

    def body(*refs):
        piece_refs = refs[:n]
        w_ref, h_ref, g_ref, dhi_ref, dh_ref, dgain_ref = refs[n:]
        t = pl.program_id(0)

        @pl.when(t == 0)
        def _():
            dgain_ref[...] = jnp.zeros_like(dgain_ref)

        db = None
        for k, a, b, s, c, d in segs:
            term = _dot_nt(piece_refs[k][:, a:b], w_ref[s, :, c:d])
            db = term if db is None else db + term
        dx, dgain = _rms_bwd(h_ref[...], g_ref[...], db)
        dgain_ref[...] += dgain
        dh_ref[...] = jnp.where(_row_mask(t, tm, pad, (tm, D)), dhi_ref[...] + dx, 0.0)

    row = pl.BlockSpec((tm, D), lambda t: (t, 0))
    return _call(
        body, name=name, grid=(T // tm,),
        in_specs=[pl.BlockSpec((tm, p.shape[1]), lambda t: (t, 0)) for p in pieces]
        + [pl.BlockSpec((N_CHIPS, D, Ns), lambda t: (0, 0, 0)), row,
           pl.BlockSpec((None, 1, D), lambda t: (layer, 0, 0)), row],
        out_specs=[row, pl.BlockSpec((1, D), lambda t: (0, 0))],
        out_shape=[jax.ShapeDtypeStruct((T, D), F32), jax.ShapeDtypeStruct((1, D), F32)],
        sem=("arbitrary",), operands=(*pieces, win, h, gain, dh_in), rider=rider)


def _grad_w_in(b, pieces, ns, tk, name, rider=None):
    T, D = b.shape
    n = len(pieces)
    nt = T // tk
    segs = _z_segments(pieces, ns)
    shards_of = [sorted({s for k, _, _, s, _, _ in segs if k == i}) for i in range(n)]

    def body(*refs):
        b_ref = refs[0]
        piece_refs = refs[1:1 + n]
        o_ref, acc_ref = refs[1 + n:]
        s = pl.program_id(0)
        t = pl.program_id(1)

        @pl.when(t == 0)
        def _():
            acc_ref[...] = jnp.zeros_like(acc_ref)

        for shard in range(N_CHIPS):
            @pl.when(s == shard)
            def _(shard=shard):
                cols = [piece_refs[k][:, a:e] for k, a, e, ss, _, _ in segs if ss == shard]
                dz = cols[0] if len(cols) == 1 else jnp.concatenate(cols, axis=1)
                acc_ref[...] += _dot_tn(b_ref[...], dz)

        @pl.when(t == nt - 1)
        def _():
            o_ref[...] = acc_ref[...].astype(BF16).reshape(1, D, ns)

    def piece_spec(i):
        def index(s, t):
            used = functools.reduce(jnp.logical_or, [s == ss for ss in shards_of[i]])
            return (jnp.where(used, t, 0), 0)
        return pl.BlockSpec((tk, pieces[i].shape[1]), index)

    return _call(
        body, name=name, grid=(N_CHIPS, nt),
        in_specs=[pl.BlockSpec((tk, D), lambda s, t: (t, 0))] + [piece_spec(i) for i in range(n)],
        out_specs=[pl.BlockSpec((1, D, ns), lambda s, t: (s, 0, 0))],
        out_shape=[jax.ShapeDtypeStruct((N_CHIPS, D, ns), BF16)],
        scratch=[pltpu.VMEM((D, ns), F32)],
        sem=("parallel", "arbitrary"), operands=(b, *pieces), rider=rider)[0]


def _sum_pair(gs, rs, c_idx, name):
    n = len(gs)

    def body(c_ref, *refs):
        for g_ref, r_ref, o_ref in zip(refs[:n], refs[n:2 * n], refs[2 * n:]):
            o_ref[...] = (g_ref[...].astype(F32) + r_ref[...].astype(F32)).astype(BF16)

    halves = [pl.BlockSpec((None,) + r.shape[1:], lambda s, c_ref: (s, 0, 0)) for r in rs]
    return pl.pallas_call(
        body,
        name=name,
        grid_spec=pltpu.PrefetchScalarGridSpec(
            num_scalar_prefetch=1,
            grid=(N_CHIPS,),
            in_specs=[pl.BlockSpec((None,) + r.shape[1:], lambda s, c_ref: (s, c_ref[0], 0)) for r in rs] + halves,
            out_specs=halves,
        ),
        out_shape=[jax.ShapeDtypeStruct(r.shape, BF16) for r in rs],
        compiler_params=_params(("parallel",)),
    )(c_idx, *gs, *rs)


def _sum_chips(ps, rs, pos, name):
    n = len(ps)
    quarters = 4

    def body(pos_ref, *refs):
        chip = pos_ref[0]
        for p_ref, r_ref, o_ref in zip(refs[:n], refs[n:2 * n], refs[2 * n:]):
            own = p_ref[...].astype(F32)
            terms = [jnp.where(chip == k, own, r_ref[k].astype(F32)) for k in range(N_CHIPS)]
            o_ref[...] = ((terms[0] + terms[1]) + terms[2]) + terms[3]

    def rows(r):
        assert r.shape[1] % (quarters * BF16_ROWS) == 0, r.shape
        return r.shape[1] // quarters

    return pl.pallas_call(
        body,
        name=name,
        grid_spec=pltpu.PrefetchScalarGridSpec(
            num_scalar_prefetch=1,
            grid=(quarters,),
            in_specs=[pl.BlockSpec((None, rows(r), r.shape[2]), lambda q, pos_ref: (pos_ref[0], q, 0)) for r in rs]
            + [pl.BlockSpec((N_CHIPS, rows(r), r.shape[2]), lambda q, pos_ref: (0, q, 0)) for r in rs],
            out_specs=[pl.BlockSpec((rows(r), r.shape[2]), lambda q, pos_ref: (pos_ref[1] * quarters + q, 0))
                       for r in rs],
        ),
        out_shape=[jax.ShapeDtypeStruct((2 * r.shape[1], r.shape[2]), F32) for r in rs],
        compiler_params=_params(("arbitrary",)),
    )(pos, *ps, *rs)


def _small_all_reduce(p, rider=None):
    rows, width = p.shape
    r = 0 if rider is None else len(rider.ins)

    def body(*refs):
        p_ref, o_ref = refs[0], refs[1 + r]
        sib_ref, slot_ref, ssem, rsem = refs[2 + 2 * r:6 + 2 * r]
        if rider is not None:
            rider.start(refs[1:1 + r], refs[2 + r:2 + 2 * r], refs[6 + 2 * r], refs[7 + 2 * r])
        reduce(p_ref, o_ref, sib_ref, slot_ref, ssem, rsem)
        if rider is not None:
            rider.finish(refs[1:1 + r], refs[2 + r:2 + 2 * r], refs[6 + 2 * r], refs[7 + 2 * r])

    def reduce(p_ref, o_ref, sib_ref, slot_ref, ssem, rsem):
        x, y, c, chip, others = _mesh_pos()
        pair = _remote(p_ref, sib_ref, ssem.at[0], rsem.at[0], (x, y, 1 - c))
        pair.start()
        pair.wait()
        slot_ref[chip] = p_ref[...] + sib_ref[...]
        half = rows // 2
        mine = pl.ds(pl.multiple_of(c * half, 8), half)
        theirs = pl.ds(pl.multiple_of((1 - c) * half, 8), half)
        sends = []
        for j, (ox, oy) in enumerate(others):
            cp = _remote(slot_ref.at[chip, mine, :], slot_ref.at[chip, mine, :], ssem.at[1 + j], rsem.at[1 + j],
                         (ox, oy, c))
            cp.start()
            sends.append(cp)
        for j, (ox, oy) in enumerate(others):
            slot = slot_ref.at[2 * ox + oy, mine, :]
            _remote(slot, slot, ssem.at[1 + j], rsem.at[1 + j], (ox, oy, c)).wait_recv()
        for cp in sends:
            cp.wait_send()
        o_ref[mine, :] = ((slot_ref[0, mine, :] + slot_ref[1, mine, :]) + slot_ref[2, mine, :]) + slot_ref[3, mine, :]
        back = _remote(o_ref.at[mine, :], o_ref.at[mine, :], ssem.at[4], rsem.at[4], (x, y, 1 - c))
        back.start()
        _remote(o_ref.at[theirs, :], o_ref.at[theirs, :], ssem.at[4], rsem.at[4], (x, y, 1 - c)).wait_recv()
        back.wait_send()

    assert rows % 16 == 0, rows
    vmem = pl.BlockSpec(memory_space=pltpu.VMEM)
    scratch = [pltpu.VMEM((rows, width), F32), pltpu.VMEM((N_CHIPS, rows, width), F32),
               pltpu.SemaphoreType.DMA((5,)), pltpu.SemaphoreType.DMA((5,))]
    if rider is not None:
        scratch += [pltpu.SemaphoreType.DMA((rider.n_sem,)), pltpu.SemaphoreType.DMA((rider.n_sem,))]
    outs = pl.pallas_call(
        body,
        name="small_grads_all_reduce",
        in_specs=[vmem] + [ANY] * r,
        out_specs=[vmem] + [ANY] * r,
        out_shape=[jax.ShapeDtypeStruct(p.shape, F32)] + ([] if rider is None else list(rider.out_shapes)),
        input_output_aliases={} if rider is None else rider.aliases(1, 1),
        scratch_shapes=scratch,
    )(p, *([] if rider is None else rider.ins))
    if rider is not None:
        rider.results = outs[1:]
    return outs[0]


def _adamw(gs, w, m, v, name):
    L, R, C = w.shape
    Ct = gs[0].shape[1]
    tr = _pick_tile(R, 256, 8)

    def body(*refs):
        g_refs = refs[:L]
        w_ref, m_ref, v_ref, go_ref, d_ref, mo_ref, vo_ref = refs[L:]
        layer = pl.program_id(0)
        grad = g_refs[L - 1][...]
        for i in range(L - 2, -1, -1):
            grad = jnp.where(layer == i, g_refs[i][...], grad)
        if Ct != C:
            grad = grad[:, :C]
        m_new = ADAM_B1 * m_ref[...] + (1.0 - ADAM_B1) * grad
        v_new = ADAM_B2 * v_ref[...] + (1.0 - ADAM_B2) * jnp.square(grad)
        m_hat = m_new / (1.0 - ADAM_B1 ** ADAM_STEP)
        v_hat = v_new / (1.0 - ADAM_B2 ** ADAM_STEP)
        go_ref[...] = grad
        d_ref[...] = -ADAM_LR * (m_hat / (jnp.sqrt(v_hat) + ADAM_EPS) + ADAM_WD * w_ref[...])
        mo_ref[...] = m_new
        vo_ref[...] = v_new

    g_specs = [pl.BlockSpec((tr, Ct), functools.partial(lambda l, r, i: (jnp.where(l == i, r, 0), 0), i=i))
               for i in range(L)]
    blk = pl.BlockSpec((None, tr, C), lambda l, r: (l, r, 0))
    return _call(
        body, name=name, grid=(L, R // tr),
        in_specs=g_specs + [blk, blk, blk],
        out_specs=[blk] * 4,
        out_shape=[jax.ShapeDtypeStruct((L, R, C), F32)] * 4,
        sem=("arbitrary", "arbitrary"), operands=(*gs, w, m, v))


_FFN1 = ("ffn1_gate", "ffn1_up", "ffn1_down")
_FFN2 = ("ffn2_gate", "ffn2_up", "ffn2_down")
_MIXW = ("w_ret_up", "w_pool_up", "w_out")
_BIG = _FFN1 + ("w_in",) + _MIXW + _FFN2
_TRANSPOSED = ("ffn1_gate", "ffn1_up", "ffn2_gate", "ffn2_up")
_SMALL = ("ffn1_norm", "mix_norm", "ffn2_norm", "final_norm", "pool_scale", "pool_maps")
_ORDER = ("meta", "ffn1_norm", "ffn1_gate", "ffn1_up", "ffn1_down", "mix_norm", "w_in", "pool_maps",
          "pool_scale", "w_ret_up", "w_pool_up", "w_out", "ffn2_norm", "ffn2_gate", "ffn2_up", "ffn2_down",
          "final_norm")


def _transport(a):
    n, r, c = a.shape
    out = a.astype(BF16)
    if c % LANES:
        out = jnp.concatenate([out, jnp.zeros((n, r, _round_up(c, LANES) - c), BF16)], axis=2)
    if r % LANES:
        out = jnp.concatenate([out, jnp.zeros((n, _round_up(r, LANES) - r, out.shape[2]), BF16)], axis=1)
    return out


def _pack_rows(parts, width):
    rows = [p.reshape(-1, width) for p in parts]
    total = sum(r.shape[0] for r in rows)
    fill = _round_up(total, 8) - total
    if fill:
        rows.append(jnp.zeros((fill, width), F32))
    return jnp.concatenate(rows, axis=0)


def _unpack_rows(packed, shapes, width):
    out, at = [], 0
    for shp in shapes:
        n = math.prod(shp) // width
        out.append(packed[at:at + n].reshape(shp))
        at += n
    return out


class _Weights:
    def __init__(self, shards):
        self.shards = shards
        self.full = {}

    def rider(self, keys):
        r = _gather_rider([(self.shards[n], i) for n, i in keys])
        r.keys = keys
        return r

    def take(self, rider):
        for key, arr in zip(rider.keys, rider.results):
            self.full[key] = arr

    def __call__(self, name, layer):
        return self.full[(name, layer)]


def _local_step(x, meta_full, tgt, w, wts, pad, tm, cg, reducer):
    D = x.shape[1]
    T = pad + N_META + x.shape[0]
    L = w["ffn1_norm"].shape[0]
    pool_maps = w["pool_maps"]
    gains = {n: w[n].reshape(L, 1, D) for n in ("ffn1_norm", "mix_norm", "ffn2_norm")}
    scale3 = w["pool_scale"].reshape(L, 1, POOL_WIDTH)
    consts = _ret_consts(T, pad)
    tl = _pick_tile(T, 2 * tm, BF16_ROWS)
    def gather(keys):
        return wts.rider(keys) if keys and keys[0] not in wts.full else None

    def done(rider):
        if rider is not None:
            wts.take(rider)

    h = jnp.concatenate([jnp.zeros((pad, D), F32), meta_full, x], axis=0)
    saved = []
    for i in range(L):
        s = {"h0": h}
        if ("ffn1_down", i) in wts.full:
            rd = gather([("w_in", i)] + [(n, i) for n in _MIXW] + [("ffn2_gate", i)])
            h, s["a1"], s["g1"], s["u1"], s["act1"] = _ffn_fwd(
                h, gains["ffn1_norm"], wts("ffn1_gate", i), wts("ffn1_up", i), wts("ffn1_down", i), i, tl,
                f"ffn1_fwd_{i}", rd)
            done(rd)
        else:
            rd = gather([("ffn1_down", i), ("w_in", i)])
            s["a1"], s["g1"], s["u1"], s["act1"] = _ffn_fwd_up(
                h, gains["ffn1_norm"], wts("ffn1_gate", i), wts("ffn1_up", i), i, tl, f"ffn1_fwd_up_{i}", rd)
            done(rd)
            rd = gather([(n, i) for n in _MIXW])
            h = _ffn_fwd_down(h, s["act1"], wts("ffn1_down", i), tl, f"ffn1_fwd_down_{i}", rd)
            done(rd)
        s["h1"] = h
        rd = gather([k for k in (("ffn2_gate", i), ("ffn2_up", i)) if k not in wts.full])
        s["z"], s["b"] = _inproj_fwd(h, gains["mix_norm"], wts("w_in", i), i, tl, f"inproj_fwd_{i}", rd)
        done(rd)
        s["r"], s["o_pre"], s["s_all"] = _ret_fwd(s["z"], consts, cg, f"retention_fwd_{i}")
        s["pm"] = _pool_fwd(s["z"], pool_maps, scale3, i, pad, f"pool_fwd_{i}")
        rd = gather([("ffn2_down", i)])
        h, s["mixed"], s["ret"], s["pool"] = _mix_fwd(
            h, s["r"], s["pm"], s["z"], wts("w_ret_up", i), wts("w_pool_up", i), wts("w_out", i), tl,
            f"mix_fwd_{i}", rd)
        done(rd)
        s["h2"] = h
        rd = gather([(n, i + 1) for n in _FFN1]) if i + 1 < L else None
        h, s["a2"], s["g2"], s["u2"], s["act2"] = _ffn_fwd(
            h, gains["ffn2_norm"], wts("ffn2_gate", i), wts("ffn2_up", i), wts("ffn2_down", i), i, tl,
            f"ffn2_fwd_{i}", rd)
        done(rd)
        saved.append(s)

    dh, loss_acc, d_final = _final_loss(h, w["final_norm"].reshape(1, D), tgt, "final_norm_loss")

    small = {n: [None] * L for n in ("ffn1_norm", "mix_norm", "ffn2_norm", "pool_scale", "pool_maps")}

    carry = {"ffn_act": 1.0, "ffn_in": 2.2, "mix_bwd": 1.0, "inproj_bwd": 1.5, "w_in": 1.0}

    tk = _pick_tile(T, 1408, LANES)

    def grad(n, a, b, i, mode):
        rd = reducer.rider(carry.get(n, 1.0 if i == 0 and n.startswith("ffn") else 0.5))
        reducer.add(n, i, _grad_tn(a, b, mode, 1.0, tk, f"grad_{n}_{i}", rd))
        reducer.done(rd)

    def ffn_bwd(which, dy, h_in, g, u, i, between=None, units=carry["ffn_in"]):
        rd = reducer.rider(carry["ffn_act"])
        dg, du, dyh = _ffn_bwd_act(dy, g, u, wts(f"{which}_down", i), tl, f"{which}_bwd_act_{i}", rd)
        reducer.done(rd)
        if between is not None:
            between(dg, du, dyh)
        rd = reducer.rider(units)
        dh_in, dgain = _ffn_bwd_in(dy, h_in, gains[f"{which}_norm"], dg, du, wts(f"{which}_gate", i),
                                   wts(f"{which}_up", i), i, tl, pad, f"{which}_bwd_in_{i}", rd)
        reducer.done(rd)
        return dh_in, dg, du, dgain, dyh

    for i in reversed(range(L)):
        s = saved[i]
        dh, dg, du, small["ffn2_norm"][i], dyh = ffn_bwd("ffn2", dh, s["h2"], s["g2"], s["u2"], i)
        grad("ffn2_gate", dg, s["a2"], i, "row")
        grad("ffn2_up", du, s["a2"], i, "row")
        grad("ffn2_down", s["act2"], dyh, i, "row")
        reducer.stage(f"ffn2_{i}")
        rd = reducer.rider(carry["mix_bwd"])
        dgab, dret, dpool, dr, dpm = _mix_bwd_dx(
            dh, s["z"], s["ret"], s["pool"], wts("w_out", i), wts("w_ret_up", i), wts("w_pool_up", i), tm,
            f"mix_bwd_{i}", rd)
        reducer.done(rd)
        rd = reducer.rider(0.5)
        g_out, g_ru, g_pu = _grad_mix(s["mixed"], dh, s["r"], dret, s["pm"], dpool, _pick_tile(T, 704, LANES),
                                      f"grad_mix_{i}", rd)
        reducer.done(rd)
        for n, g_n in (("w_out", g_out), ("w_ret_up", g_ru), ("w_pool_up", g_pu)):
            reducer.add(n, i, g_n)
        du_pool, small["pool_maps"][i], small["pool_scale"][i] = _pool_bwd(
            s["z"], dpm, pool_maps, scale3, i, pad, f"pool_bwd_{i}")
        dq, dgr, dkp, dvp, ds = _ret_bwd_local(s["z"], s["o_pre"], s["s_all"], dr, consts,
                                               _pick_tile(T // CHUNK, 11, 1), f"retention_bwd_{i}")
        dk, dv = _ret_bwd_state(s["z"], dkp, dvp, ds, consts, cg, f"retention_bwd_state_{i}")
        dz = [dq, dk, dv, dgr, du_pool, dgab]
        dh2 = dh
        rd = reducer.rider(carry["inproj_bwd"])
        dh, small["mix_norm"][i] = _inproj_bwd_dx(
            dz, wts("w_in", i), s["h1"], gains["mix_norm"], dh2, i, tm, pad, f"inproj_bwd_{i}", rd)
        reducer.done(rd)
        rd = reducer.rider(carry["w_in"])
        reducer.add("w_in", i, _grad_w_in(s["b"], dz, wts("w_in", i).shape[-1], tk, f"grad_w_in_{i}", rd))
        reducer.done(rd)
        reducer.stage(f"mid{i}")
        def ffn1_grads(dg, du, dyh, i=i, s=s):
            grad("ffn1_gate", dg, s["a1"], i, "row")
            if i == 0:
                reducer.stage("gate0")
            grad("ffn1_up", du, s["a1"], i, "row")
            if i == 0:
                reducer.stage("up0")
            grad("ffn1_down", s["act1"], dyh, i, "row")
            reducer.stage(f"end{i}")

        if i == 0:
            dh, _, _, small["ffn1_norm"][i], _ = ffn_bwd("ffn1", dh, s["h0"], s["g1"], s["u1"], i, ffn1_grads, 2.5)
        else:
            dh, dg, du, small["ffn1_norm"][i], dyh = ffn_bwd("ffn1", dh, s["h0"], s["g1"], s["u1"], i)
            ffn1_grads(dg, du, dyh)

    return loss_acc, dh, small, d_final


class _Reducer:
    def __init__(self, unit):
        self.c_idx = lax.axis_index("c").astype(jnp.int32).reshape(1)
        chip = 2 * lax.axis_index("x") + lax.axis_index("y")
        self.pos = jnp.stack([chip, lax.axis_index("c")]).astype(jnp.int32)
        self.pending, self.stages, self.queue, self.halves, self.whole = [], [], [], {}, {}
        self.unit = unit
        self.calls = 0

    def add(self, name, layer, g):
        self.pending.append(((name, layer), g))

    def stage(self, tag):
        if self.pending:
            self.stages.append((tag, self.pending))
            self.pending = []

    def _pair_rider(self):
        if not self.stages:
            return None
        tag, items = self.stages.pop(0)
        rd = _pair_exchange_rider([g for _, g in items])
        rd.tag, rd.keys = tag, [k for k, _ in items]
        return rd

    def _chip_rider(self, units):
        take, keep, size = [], [], 0
        for item in self.queue:
            if units is None or size + item[1].size <= units * self.unit:
                take.append(item)
                size += item[1].size
            else:
                keep.append(item)
        self.queue = keep
        if not take:
            return None
        rd = _chip_exchange_rider([p for _, p in take])
        rd.keys = [k for k, _ in take]
        return rd

    def _gather_rider(self):
        keys = [k for k in self.halves if k not in self.whole]
        if not keys:
            return None
        rd = _pair_gather_rider([self.halves[k] for k in keys])
        rd.keys = keys
        return rd

    def rider(self, units):
        self.riding = (self._pair_rider(), self._chip_rider(units), self._gather_rider())
        return _join(self.riding)

    def done(self, rd):
        if rd is None:
            return
        _split_results(rd)
        pair, chips, gather = self.riding
        if len([r for r in self.riding if r is not None]) == 1:
            (pair or chips or gather).results = rd.results
        self.calls += 1
        if gather is not None:
            self.whole.update(zip(gather.keys, gather.results))
        if pair is not None:
            sums = _sum_pair(pair.ins, pair.results, self.c_idx, f"sum_pair_{pair.tag}")
            self.queue += list(zip(pair.keys, sums))
        if chips is not None:
            sums = _sum_chips(chips.ins, chips.results, self.pos, f"sum_chips_{self.calls}")
            self.halves.update(zip(chips.keys, sums))

    def busy(self):
        assert not self.pending
        return bool(self.stages or self.queue or len(self.whole) < len(self.halves))

    def flush(self):
        self.riding = (self._pair_rider(), self._chip_rider(None), self._gather_rider())
        rd = _join(self.riding)
        _run_rider(rd, f"grads_exchange_tail_{self.calls}")
        self.done(rd)


def _update(loss_acc, grad_x, d_meta_rows, reducer, small, d_final, w, mom, var):
    meta = w["meta"]
    D = w["final_norm"].shape[0]
    L = w["ffn1_norm"].shape[0]
    Dq = D // N_CHIPS

    out = {}

    small_parts = [jnp.concatenate(small[n], axis=0) for n in ("ffn1_norm", "mix_norm", "ffn2_norm")]
    small_parts += [d_final, jnp.concatenate(small["pool_scale"], axis=0), jnp.concatenate(small["pool_maps"], axis=0)]
    loss_row = jnp.pad(loss_acc, ((0, 0), (0, D - loss_acc.shape[1])))
    rd = reducer.rider(None) if reducer.busy() else None
    reduced = _small_all_reduce(_pack_rows(small_parts + [d_meta_rows, loss_row], D), rd)
    reducer.done(rd)
    while reducer.busy():
        reducer.flush()
    for n in _BIG:
        gs = [reducer.whole[(n, i)] for i in range(L)]
        if n in _TRANSPOSED:
            res = _adamw(gs, *(jnp.swapaxes(t[n], 1, 2) for t in (w, mom, var)), f"adamw_{n}")
            out[n] = [jnp.swapaxes(r, 1, 2) for r in res]
        else:
            out[n] = _adamw(gs, w[n], mom[n], var[n], f"adamw_{n}")

    small_shapes = [w[n].shape for n in _SMALL]
    small_rows = sum(math.prod(shp) for shp in small_shapes) // D
    chip = 2 * lax.axis_index("x") + lax.axis_index("y")
    d_meta = lax.dynamic_slice_in_dim(reduced[small_rows:small_rows + N_META], chip * Dq, Dq, axis=1)
    names = _SMALL + ("meta",)
    packed_g = _pack_rows([reduced[:small_rows], d_meta], D)
    packed = [_pack_rows([t[n] for n in names], D) for t in (w, mom, var)]
    res = _adamw([packed_g], packed[0][None], packed[1][None], packed[2][None], "adamw_small")
    shapes = small_shapes + [meta.shape]
    unpacked = [_unpack_rows(r[0], shapes, D) for r in res]
    for k, n in enumerate(names):
        out[n] = tuple(u[k] for u in unpacked)

    loss = reduced[small_rows + N_META, 0]
    return (loss, grad_x) + tuple(out[n][j] for j in range(4) for n in _ORDER)


def kernel(x, meta, ffn1_norm, ffn1_gate, ffn1_up, ffn1_down, mix_norm, w_in, pool_maps, pool_scale, w_ret_up, w_pool_up, w_out, ffn2_norm, ffn2_gate, ffn2_up, ffn2_down, final_norm, loss_target, m_meta, m_ffn1_norm, m_ffn1_gate, m_ffn1_up, m_ffn1_down, m_mix_norm, m_w_in, m_pool_maps, m_pool_scale, m_w_ret_up, m_w_pool_up, m_w_out, m_ffn2_norm, m_ffn2_gate, m_ffn2_up, m_ffn2_down, m_final_norm, v_meta, v_ffn1_norm, v_ffn1_gate, v_ffn1_up, v_ffn1_down, v_mix_norm, v_w_in, v_pool_maps, v_pool_scale, v_w_ret_up, v_w_pool_up, v_w_out, v_ffn2_norm, v_ffn2_gate, v_ffn2_up, v_ffn2_down, v_final_norm):
    args = dict(locals())
    w = {n: args[n] for n in _ORDER}
    mom = {n: args["m_" + n] for n in _ORDER}
    var = {n: args["v_" + n] for n in _ORDER}

    assert x.shape[0] == 1, "one batch element per device"
    seq, D = x.shape[1], x.shape[2]
    assert seq % CHUNK == 0 and D % RET_WIDTH == 0 and (2 * POOL_WIDTH) % D == 0
    pad = (-(seq + N_META)) % CHUNK
    T = seq + N_META + pad
    tm = _pick_tile(T, 528, BF16_ROWS)
    cg = _pick_tile(T // CHUNK, 33, 1)

    shards = {n: _transport(w[n]) for n in _BIG}
    shards["meta"] = meta[None]
    wts = _Weights(shards)
    head = wts.rider([("ffn1_gate", 0), ("ffn1_up", 0), ("meta", 0)])
    _run_rider(head, "weights_gather_head")
    wts.take(head)
    meta_full = jnp.transpose(wts("meta", 0), (1, 0, 2)).reshape(N_META, D)

    reducer = _Reducer(unit=2 * shards["ffn1_gate"][0].size)
    loss_acc, dh, small, d_final = _local_step(x[0], meta_full, loss_target[0], w, wts, pad, tm, cg, reducer)
    grad_x = dh[pad + N_META:][None]
    return _update(loss_acc, grad_x, dh[pad:pad + N_META], reducer, small, d_final, w, mom, var)
```
